```python
import math
import jax, jax.numpy as jnp
from jax import lax
import numpy as np


D_MODEL = 2048
BATCH = 8
SEQ = 8192
DEPTH = 1

MLA_HEADS = 8
MLA_NOPE_DIM = 128
MLA_ROPE_DIM = 64
MLA_QK_DIM = MLA_NOPE_DIM + MLA_ROPE_DIM
MLA_V_DIM = 128
MLA_Q_LORA = 512
MLA_KV_LORA = 256
MLA_WIDTH = MLA_HEADS * MLA_V_DIM
GDN_HEADS = 8
GDN_K_DIM = 128
GDN_V_DIM = 128
GDN_WIDTH = GDN_HEADS * GDN_V_DIM
GDN_QKV = 2 * GDN_HEADS * GDN_K_DIM + GDN_WIDTH
CONV_WIDTH = 4
CHUNK = 64
D_MIX = MLA_WIDTH + GDN_WIDTH
ROPE_THETA = 10000.0
NORM_EPS = 1e-6
Q_BLOCK = 128
SPLIT_SIZES = (MLA_Q_LORA, MLA_KV_LORA, MLA_ROPE_DIM, MLA_WIDTH,
               GDN_HEADS * GDN_K_DIM, GDN_HEADS * GDN_K_DIM, GDN_WIDTH,
               GDN_HEADS, GDN_HEADS, GDN_WIDTH)
IN_COLS = sum(SPLIT_SIZES)

kernel_name = 'hymba_mla_gdn_gated_hybrid'


def rms_norm(x, gain):
    xf = x.astype(jnp.float32)
    y = xf * lax.rsqrt(jnp.mean(xf * xf, axis=-1, keepdims=True) + NORM_EPS)
    return (y * gain.astype(jnp.float32)).astype(x.dtype)


def l2_norm(x):
    xf = x.astype(jnp.float32)
    return xf * lax.rsqrt(jnp.sum(xf * xf, axis=-1, keepdims=True) + NORM_EPS)


def rope(x, positions):
    half = x.shape[-1] // 2
    inv_freq = jnp.power(ROPE_THETA, -jnp.arange(half, dtype=jnp.float32) / half)
    ang = positions.astype(jnp.float32)[..., None] * inv_freq
    cos = jnp.cos(ang)[:, :, None, :]
    sin = jnp.sin(ang)[:, :, None, :]
    xf = x.astype(jnp.float32)
    x1, x2 = xf[..., :half], xf[..., half:]
    return jnp.concatenate([x1 * cos - x2 * sin, x2 * cos + x1 * sin], axis=-1).astype(x.dtype)


def mla_branch(cq, ckv, k_rope, positions, q_a_gain, kv_a_gain, w_uq, w_ukv, q_gain, k_gain):
    B, S, _ = cq.shape
    q = (rms_norm(cq, q_a_gain) @ w_uq).reshape(B, S, MLA_HEADS, MLA_QK_DIM)
    kv = (rms_norm(ckv, kv_a_gain) @ w_ukv).reshape(B, S, MLA_HEADS, MLA_NOPE_DIM + MLA_V_DIM)
    k_nope, v = kv[..., :MLA_NOPE_DIM], kv[..., MLA_NOPE_DIM:]
    k_shared = jnp.broadcast_to(k_rope[:, :, None, :], (B, S, MLA_HEADS, MLA_ROPE_DIM))
    k = jnp.concatenate([k_nope, k_shared], axis=-1)
    q = rms_norm(q, q_gain)
    k = rms_norm(k, k_gain)
    q = jnp.concatenate([q[..., :MLA_NOPE_DIM], rope(q[..., MLA_NOPE_DIM:], positions)], axis=-1)
    k = jnp.concatenate([k[..., :MLA_NOPE_DIM], rope(k[..., MLA_NOPE_DIM:], positions)], axis=-1)
    q = q.transpose(0, 2, 1, 3)
    k = k.transpose(0, 2, 1, 3)
    v = v.transpose(0, 2, 1, 3)
    scale = MLA_QK_DIM ** -0.5
    key_idx = jnp.arange(S)

    def block(i):
        start = i * Q_BLOCK
        qb = lax.dynamic_slice_in_dim(q, start, Q_BLOCK, axis=2)
        s = jnp.einsum('bhqd,bhkd->bhqk', qb, k, preferred_element_type=jnp.float32) * scale
        q_idx = start + jnp.arange(Q_BLOCK)
        s = jnp.where(key_idx[None, :] <= q_idx[:, None], s, -jnp.inf)
        p = jax.nn.softmax(s, axis=-1).astype(v.dtype)
        return jnp.einsum('bhqk,bhkd->bqhd', p, v)

    o = lax.map(block, jnp.arange(S // Q_BLOCK))
    return o.transpose(1, 0, 2, 3, 4).reshape(B, S, MLA_WIDTH)


def causal_conv_silu(x, w):
    S = x.shape[1]
    xp = jnp.pad(x, ((0, 0), (CONV_WIDTH - 1, 0), (0, 0)))
    y = xp[:, 0:S, :] * w[0]
    for j in range(1, CONV_WIDTH):
        y = y + xp[:, j:j + S, :] * w[j]
    return jax.nn.silu(y)


def chunked_gated_delta(q, k, v, g, beta):
    B, S, H, Dk = q.shape
    Dv = v.shape[-1]
    N = S // CHUNK

    def to_chunks(t):
        return t.reshape(B, N, CHUNK, H, t.shape[-1]).transpose(0, 3, 1, 2, 4)

    q, k, v = to_chunks(q), to_chunks(k), to_chunks(v)
    g = g.reshape(B, N, CHUNK, H).transpose(0, 3, 1, 2)
    beta = beta.reshape(B, N, CHUNK, H).transpose(0, 3, 1, 2)
    gc = jnp.cumsum(g, axis=-1)
    idx = jnp.arange(CHUNK)
    lower_incl = idx[:, None] >= idx[None, :]
    strict = idx[:, None] > idx[None, :]
    diff = gc[..., :, None] - gc[..., None, :]
    decay = jnp.where(lower_incl, jnp.exp(jnp.where(lower_incl, diff, 0.0)), 0.0)
    k_beta = k * beta[..., None]
    v_beta = v * beta[..., None]
    L = jnp.where(strict, jnp.einsum('bhncd,bhnjd->bhncj', k_beta, k) * decay, 0.0)
    eye = jnp.eye(CHUNK, dtype=jnp.float32)
    rhs = jnp.concatenate([v_beta, k_beta * jnp.exp(gc)[..., None]], axis=-1)
    sol = lax.linalg.triangular_solve(eye + L, rhs, left_side=True, lower=True, unit_diagonal=True)
    u, w = sol[..., :Dv], sol[..., Dv:]
    attn_intra = jnp.einsum('bhncd,bhnjd->bhncj', q, k) * decay
    q_dec = q * jnp.exp(gc)[..., None]
    k_dec = k * jnp.exp(gc[..., -1:] - gc)[..., None]
    g_last = jnp.exp(gc[..., -1])

    def mv(t):
        return jnp.moveaxis(t, 2, 0)

    xs = (mv(u), mv(w), mv(q_dec), mv(k_dec), mv(attn_intra), jnp.moveaxis(g_last, 2, 0))

    def step(state, inp):
        u_c, w_c, qd, kd, a_c, gl = inp
        v_new = u_c - jnp.einsum('bhck,bhkv->bhcv', w_c, state)
        o = jnp.einsum('bhck,bhkv->bhcv', qd, state) + jnp.einsum('bhcj,bhjv->bhcv', a_c, v_new)
        state = state * gl[..., None, None] + jnp.einsum('bhck,bhcv->bhkv', kd, v_new)
        return state, o

    state0 = jnp.zeros((B, H, Dk, Dv), jnp.float32)
    _, o = lax.scan(step, state0, xs)
    return o.transpose(1, 0, 3, 2, 4).reshape(B, S, H, Dv)


def gdn_branch(gq, gk, gv, ga, gb, conv_w, a_log, dt_bias, out_gain):
    B, S, _ = gq.shape
    qkv = causal_conv_silu(jnp.concatenate([gq, gk, gv], axis=-1), conv_w)
    hk = GDN_HEADS * GDN_K_DIM
    q = qkv[..., :hk].reshape(B, S, GDN_HEADS, GDN_K_DIM)
    k = qkv[..., hk:2 * hk].reshape(B, S, GDN_HEADS, GDN_K_DIM)
    v = qkv[..., 2 * hk:].reshape(B, S, GDN_HEADS, GDN_V_DIM).astype(jnp.float32)
    q = l2_norm(q) * (GDN_K_DIM ** -0.5)
    k = l2_norm(k)
    g = -jnp.exp(a_log.astype(jnp.float32)) * jax.nn.softplus(ga.astype(jnp.float32) + dt_bias.astype(jnp.float32))
    beta = jax.nn.sigmoid(gb.astype(jnp.float32))
    o = chunked_gated_delta(q, k, v, g, beta)
    o = rms_norm(o, out_gain).astype(gq.dtype)
    return o.reshape(B, S, GDN_WIDTH)


def _fwd_setup_inputs(seed: int = 0) -> dict:
    key = jax.random.key(seed)
    ks = jax.random.split(key, 16)
    f32 = jnp.float32

    def dense(k, shape, fan_in):
        return jax.random.normal(k, shape, f32) * fan_in ** -0.5

    def gain(k, shape):
        return 1.0 + 0.02 * jax.random.normal(k, shape, f32)

    x = jax.random.normal(ks[0], (BATCH, SEQ, D_MODEL), f32)
    positions = (jax.random.randint(ks[1], (BATCH, 1), 0, 1024, jnp.int32)
                 + jnp.arange(SEQ, dtype=jnp.int32)[None, :])
    dt = jnp.exp(jax.random.uniform(ks[11], (DEPTH, GDN_HEADS), f32,
                                    minval=math.log(1e-3), maxval=math.log(1e-1)))
    dt_bias = dt + jnp.log(-jnp.expm1(-dt))
    a_log = jnp.log(jax.random.uniform(ks[12], (DEPTH, GDN_HEADS), f32, minval=1.0, maxval=16.0))
    return {
        'x': x,
        'positions': positions,
        'norm_gain': gain(ks[2], (DEPTH, D_MODEL)),
        'w_in': dense(ks[3], (DEPTH, D_MODEL, IN_COLS), D_MODEL),
        'mla_q_a_gain': gain(ks[4], (DEPTH, MLA_Q_LORA)),
        'mla_kv_a_gain': gain(ks[5], (DEPTH, MLA_KV_LORA)),
        'w_uq': dense(ks[6], (DEPTH, MLA_Q_LORA, MLA_HEADS * MLA_QK_DIM), MLA_Q_LORA),
        'w_ukv': dense(ks[7], (DEPTH, MLA_KV_LORA, MLA_HEADS * (MLA_NOPE_DIM + MLA_V_DIM)), MLA_KV_LORA),
        'mla_q_norm_gain': gain(ks[8], (DEPTH, MLA_QK_DIM)),
        'mla_k_norm_gain': gain(ks[9], (DEPTH, MLA_QK_DIM)),
        'gdn_conv_w': dense(ks[10], (DEPTH, CONV_WIDTH, GDN_QKV), CONV_WIDTH),
        'gdn_a_log': a_log,
        'gdn_dt_bias': dt_bias,
        'gdn_out_norm_gain': gain(ks[13], (DEPTH, GDN_V_DIM)),
        'w_out': dense(ks[14], (DEPTH, D_MIX, D_MODEL), D_MIX),
    }


def _fwd_reference(x, positions, norm_gain, w_in, mla_q_a_gain, mla_kv_a_gain, w_uq, w_ukv,
              mla_q_norm_gain, mla_k_norm_gain, gdn_conv_w, gdn_a_log, gdn_dt_bias,
              gdn_out_norm_gain, w_out):
    B, S, _ = x.shape
    split_idx = np.cumsum(SPLIT_SIZES)[:-1].tolist()
    h = x
    for layer in range(DEPTH):
        xn = rms_norm(h, norm_gain[layer])
        proj = xn @ w_in[layer]
        cq, ckv, k_rope, mla_gate, gq, gk, gv, ga, gb, gdn_gate = jnp.split(proj, split_idx, axis=-1)
        o_mla = mla_branch(cq, ckv, k_rope, positions, mla_q_a_gain[layer], mla_kv_a_gain[layer],
                           w_uq[layer], w_ukv[layer], mla_q_norm_gain[layer], mla_k_norm_gain[layer])
        o_mla = o_mla * jax.nn.silu(mla_gate)
        o_gdn = gdn_branch(gq, gk, gv, ga, gb, gdn_conv_w[layer], gdn_a_log[layer],
                           gdn_dt_bias[layer], gdn_out_norm_gain[layer])
        o_gdn = o_gdn * jax.nn.silu(gdn_gate)
        mixed = jnp.concatenate([o_mla, o_gdn], axis=-1)
        h = h + mixed @ w_out[layer]
    return h


import jax as _jax
import jax.numpy as _jnp

TWIN_FORMAT = 'train_step'
FWD_PARAMS = ['x', 'positions', 'norm_gain', 'w_in', 'mla_q_a_gain', 'mla_kv_a_gain', 'w_uq', 'w_ukv', 'mla_q_norm_gain', 'mla_k_norm_gain', 'gdn_conv_w', 'gdn_a_log', 'gdn_dt_bias', 'gdn_out_norm_gain', 'w_out']
TWIN_WEIGHTS = ['norm_gain', 'w_in', 'mla_q_a_gain', 'mla_kv_a_gain', 'w_uq', 'w_ukv', 'mla_q_norm_gain', 'mla_k_norm_gain', 'gdn_conv_w', 'gdn_a_log', 'gdn_dt_bias', 'gdn_out_norm_gain', 'w_out']
TWIN_DIFF_INPUT = 'x'
TWIN_INPUTS = ['x', 'positions', 'norm_gain', 'w_in', 'mla_q_a_gain', 'mla_kv_a_gain', 'w_uq', 'w_ukv', 'mla_q_norm_gain', 'mla_k_norm_gain', 'gdn_conv_w', 'gdn_a_log', 'gdn_dt_bias', 'gdn_out_norm_gain', 'w_out', 'loss_target', 'm_norm_gain', 'm_w_in', 'm_mla_q_a_gain', 'm_mla_kv_a_gain', 'm_w_uq', 'm_w_ukv', 'm_mla_q_norm_gain', 'm_mla_k_norm_gain', 'm_gdn_conv_w', 'm_gdn_a_log', 'm_gdn_dt_bias', 'm_gdn_out_norm_gain', 'm_w_out', 'v_norm_gain', 'v_w_in', 'v_mla_q_a_gain', 'v_mla_kv_a_gain', 'v_w_uq', 'v_w_ukv', 'v_mla_q_norm_gain', 'v_mla_k_norm_gain', 'v_gdn_conv_w', 'v_gdn_a_log', 'v_gdn_dt_bias', 'v_gdn_out_norm_gain', 'v_w_out']
TWIN_OUTPUTS = ['loss', 'grad_x', 'grad_norm_gain', 'grad_w_in', 'grad_mla_q_a_gain', 'grad_mla_kv_a_gain', 'grad_w_uq', 'grad_w_ukv', 'grad_mla_q_norm_gain', 'grad_mla_k_norm_gain', 'grad_gdn_conv_w', 'grad_gdn_a_log', 'grad_gdn_dt_bias', 'grad_gdn_out_norm_gain', 'grad_w_out', 'delta_norm_gain', 'delta_w_in', 'delta_mla_q_a_gain', 'delta_mla_kv_a_gain', 'delta_w_uq', 'delta_w_ukv', 'delta_mla_q_norm_gain', 'delta_mla_k_norm_gain', 'delta_gdn_conv_w', 'delta_gdn_a_log', 'delta_gdn_dt_bias', 'delta_gdn_out_norm_gain', 'delta_w_out', 'new_m_norm_gain', 'new_m_w_in', 'new_m_mla_q_a_gain', 'new_m_mla_kv_a_gain', 'new_m_w_uq', 'new_m_w_ukv', 'new_m_mla_q_norm_gain', 'new_m_mla_k_norm_gain', 'new_m_gdn_conv_w', 'new_m_gdn_a_log', 'new_m_gdn_dt_bias', 'new_m_gdn_out_norm_gain', 'new_m_w_out', 'new_v_norm_gain', 'new_v_w_in', 'new_v_mla_q_a_gain', 'new_v_mla_kv_a_gain', 'new_v_w_uq', 'new_v_w_ukv', 'new_v_mla_q_norm_gain', 'new_v_mla_k_norm_gain', 'new_v_gdn_conv_w', 'new_v_gdn_a_log', 'new_v_gdn_dt_bias', 'new_v_gdn_out_norm_gain', 'new_v_w_out']
TWIN_LEAF_KINDS = {'loss': 'loss', 'grad_x': 'grad_x', 'grad_norm_gain': 'grad_w', 'grad_w_in': 'grad_w', 'grad_mla_q_a_gain': 'grad_w', 'grad_mla_kv_a_gain': 'grad_w', 'grad_w_uq': 'grad_w', 'grad_w_ukv': 'grad_w', 'grad_mla_q_norm_gain': 'grad_w', 'grad_mla_k_norm_gain': 'grad_w', 'grad_gdn_conv_w': 'grad_w', 'grad_gdn_a_log': 'grad_w', 'grad_gdn_dt_bias': 'grad_w', 'grad_gdn_out_norm_gain': 'grad_w', 'grad_w_out': 'grad_w', 'delta_norm_gain': 'delta_w', 'delta_w_in': 'delta_w', 'delta_mla_q_a_gain': 'delta_w', 'delta_mla_kv_a_gain': 'delta_w', 'delta_w_uq': 'delta_w', 'delta_w_ukv': 'delta_w', 'delta_mla_q_norm_gain': 'delta_w', 'delta_mla_k_norm_gain': 'delta_w', 'delta_gdn_conv_w': 'delta_w', 'delta_gdn_a_log': 'delta_w', 'delta_gdn_dt_bias': 'delta_w', 'delta_gdn_out_norm_gain': 'delta_w', 'delta_w_out': 'delta_w', 'new_m_norm_gain': 'new_m', 'new_m_w_in': 'new_m', 'new_m_mla_q_a_gain': 'new_m', 'new_m_mla_kv_a_gain': 'new_m', 'new_m_w_uq': 'new_m', 'new_m_w_ukv': 'new_m', 'new_m_mla_q_norm_gain': 'new_m', 'new_m_mla_k_norm_gain': 'new_m', 'new_m_gdn_conv_w': 'new_m', 'new_m_gdn_a_log': 'new_m', 'new_m_gdn_dt_bias': 'new_m', 'new_m_gdn_out_norm_gain': 'new_m', 'new_m_w_out': 'new_m', 'new_v_norm_gain': 'new_v', 'new_v_w_in': 'new_v', 'new_v_mla_q_a_gain': 'new_v', 'new_v_mla_kv_a_gain': 'new_v', 'new_v_w_uq': 'new_v', 'new_v_w_ukv': 'new_v', 'new_v_mla_q_norm_gain': 'new_v', 'new_v_mla_k_norm_gain': 'new_v', 'new_v_gdn_conv_w': 'new_v', 'new_v_gdn_a_log': 'new_v', 'new_v_gdn_dt_bias': 'new_v', 'new_v_gdn_out_norm_gain': 'new_v', 'new_v_w_out': 'new_v'}


def _forward(args):
    return _fwd_reference(*[args[k] for k in FWD_PARAMS])


def _output_shape():
    def fwd():
        inp = _fwd_setup_inputs(0)
        return _fwd_reference(*[inp[k] for k in FWD_PARAMS])
    out = _jax.eval_shape(fwd)
    return out.shape, out.dtype

N_MICROBATCH = 1
ADAM_LR = 0.001
ADAM_B1 = 0.9
ADAM_B2 = 0.999
ADAM_EPS = 1e-08
ADAM_WD = 0.01
ADAM_STEP = 10
PER_EXAMPLE_BATCH_AXIS = {'x': 0, 'positions': 0, 'loss_target': 0}
SHARED_INPUTS = []
_WEIGHT_DTYPES = {'norm_gain': _jnp.float32, 'w_in': _jnp.float32, 'mla_q_a_gain': _jnp.float32, 'mla_kv_a_gain': _jnp.float32, 'w_uq': _jnp.float32, 'w_ukv': _jnp.float32, 'mla_q_norm_gain': _jnp.float32, 'mla_k_norm_gain': _jnp.float32, 'gdn_conv_w': _jnp.float32, 'gdn_a_log': _jnp.float32, 'gdn_dt_bias': _jnp.float32, 'gdn_out_norm_gain': _jnp.float32, 'w_out': _jnp.float32}
MOMENT_SCALE = {'norm_gain': 6.416373e+00, 'w_in': 1.764540e-01, 'mla_q_a_gain': 3.445135e-02, 'mla_kv_a_gain': 4.006617e-01, 'w_uq': 1.957648e-02, 'w_ukv': 2.543882e-02, 'mla_q_norm_gain': 3.079774e-01, 'mla_k_norm_gain': 3.083103e-01, 'gdn_conv_w': 3.177394e-01, 'gdn_a_log': 3.609697e+01, 'gdn_dt_bias': 3.414497e+01, 'gdn_out_norm_gain': 9.226849e+01, 'w_out': 2.869339e-01}


def _to_microbatches(a, axis):
    t = _jnp.moveaxis(a, axis, 0)
    t = t.reshape((N_MICROBATCH, t.shape[0] // N_MICROBATCH) + t.shape[1:])
    return _jnp.moveaxis(t, 1, axis + 1)


def setup_inputs(seed: int = 0) -> dict:
    inp = _fwd_setup_inputs(seed)
    key = _jax.random.fold_in(_jax.random.key(seed), 7919)
    shape, _ = _output_shape()
    out = dict(inp)
    out["loss_target"] = _jax.random.normal(_jax.random.fold_in(key, 0), shape, _jnp.float32)
    for i, name in enumerate(TWIN_WEIGHTS):
        w = inp[name].astype(_jnp.float32)
        if MOMENT_SCALE is None:
            s = _jnp.sqrt(_jnp.mean(_jnp.square(w)) + 1e-30)
        else:
            s = MOMENT_SCALE[name]
        km, kv = _jax.random.split(_jax.random.fold_in(key, i + 1))
        out[name] = w
        out["m_" + name] = s * _jax.random.normal(km, w.shape, _jnp.float32)
        out["v_" + name] = (s * s) * _jax.random.uniform(kv, w.shape, _jnp.float32, 0.5, 1.5)
    if N_MICROBATCH > 1:
        for name, axis in PER_EXAMPLE_BATCH_AXIS.items():
            out[name] = _to_microbatches(out[name], axis)
    return {'x': out['x'], 'positions': out['positions'], 'norm_gain': out['norm_gain'], 'w_in': out['w_in'], 'mla_q_a_gain': out['mla_q_a_gain'], 'mla_kv_a_gain': out['mla_kv_a_gain'], 'w_uq': out['w_uq'], 'w_ukv': out['w_ukv'], 'mla_q_norm_gain': out['mla_q_norm_gain'], 'mla_k_norm_gain': out['mla_k_norm_gain'], 'gdn_conv_w': out['gdn_conv_w'], 'gdn_a_log': out['gdn_a_log'], 'gdn_dt_bias': out['gdn_dt_bias'], 'gdn_out_norm_gain': out['gdn_out_norm_gain'], 'w_out': out['w_out'], 'loss_target': out['loss_target'], 'm_norm_gain': out['m_norm_gain'], 'm_w_in': out['m_w_in'], 'm_mla_q_a_gain': out['m_mla_q_a_gain'], 'm_mla_kv_a_gain': out['m_mla_kv_a_gain'], 'm_w_uq': out['m_w_uq'], 'm_w_ukv': out['m_w_ukv'], 'm_mla_q_norm_gain': out['m_mla_q_norm_gain'], 'm_mla_k_norm_gain': out['m_mla_k_norm_gain'], 'm_gdn_conv_w': out['m_gdn_conv_w'], 'm_gdn_a_log': out['m_gdn_a_log'], 'm_gdn_dt_bias': out['m_gdn_dt_bias'], 'm_gdn_out_norm_gain': out['m_gdn_out_norm_gain'], 'm_w_out': out['m_w_out'], 'v_norm_gain': out['v_norm_gain'], 'v_w_in': out['v_w_in'], 'v_mla_q_a_gain': out['v_mla_q_a_gain'], 'v_mla_kv_a_gain': out['v_mla_kv_a_gain'], 'v_w_uq': out['v_w_uq'], 'v_w_ukv': out['v_w_ukv'], 'v_mla_q_norm_gain': out['v_mla_q_norm_gain'], 'v_mla_k_norm_gain': out['v_mla_k_norm_gain'], 'v_gdn_conv_w': out['v_gdn_conv_w'], 'v_gdn_a_log': out['v_gdn_a_log'], 'v_gdn_dt_bias': out['v_gdn_dt_bias'], 'v_gdn_out_norm_gain': out['v_gdn_out_norm_gain'], 'v_w_out': out['v_w_out']}


def _loss(weights, diff, rest, loss_target):
    with _jax.named_scope("forward"):
        args = {**rest, TWIN_DIFF_INPUT: diff, **{k: w.astype(_WEIGHT_DTYPES[k]) for k, w in weights.items()}}
        y = _forward(args)
    with _jax.named_scope("loss_head"):
        err = _jnp.square(y.astype(_jnp.float32) - loss_target)
        return 0.5 * _jnp.sum(_jnp.mean(err, axis=-1)) if err.ndim else 0.5 * err


def _adamw(w, g, m, v):
    m = ADAM_B1 * m + (1.0 - ADAM_B1) * g
    v = ADAM_B2 * v + (1.0 - ADAM_B2) * _jnp.square(g)
    m_hat = m / (1.0 - ADAM_B1 ** ADAM_STEP)
    v_hat = v / (1.0 - ADAM_B2 ** ADAM_STEP)
    delta = -ADAM_LR * (m_hat / (_jnp.sqrt(v_hat) + ADAM_EPS) + ADAM_WD * w)
    return delta, m, v


def reference(x, positions, norm_gain, w_in, mla_q_a_gain, mla_kv_a_gain, w_uq, w_ukv, mla_q_norm_gain, mla_k_norm_gain, gdn_conv_w, gdn_a_log, gdn_dt_bias, gdn_out_norm_gain, w_out, loss_target, m_norm_gain, m_w_in, m_mla_q_a_gain, m_mla_kv_a_gain, m_w_uq, m_w_ukv, m_mla_q_norm_gain, m_mla_k_norm_gain, m_gdn_conv_w, m_gdn_a_log, m_gdn_dt_bias, m_gdn_out_norm_gain, m_w_out, v_norm_gain, v_w_in, v_mla_q_a_gain, v_mla_kv_a_gain, v_w_uq, v_w_ukv, v_mla_q_norm_gain, v_mla_k_norm_gain, v_gdn_conv_w, v_gdn_a_log, v_gdn_dt_bias, v_gdn_out_norm_gain, v_w_out):
    given = dict(x=x, positions=positions, norm_gain=norm_gain, w_in=w_in, mla_q_a_gain=mla_q_a_gain, mla_kv_a_gain=mla_kv_a_gain, w_uq=w_uq, w_ukv=w_ukv, mla_q_norm_gain=mla_q_norm_gain, mla_k_norm_gain=mla_k_norm_gain, gdn_conv_w=gdn_conv_w, gdn_a_log=gdn_a_log, gdn_dt_bias=gdn_dt_bias, gdn_out_norm_gain=gdn_out_norm_gain, w_out=w_out, loss_target=loss_target, m_norm_gain=m_norm_gain, m_w_in=m_w_in, m_mla_q_a_gain=m_mla_q_a_gain, m_mla_kv_a_gain=m_mla_kv_a_gain, m_w_uq=m_w_uq, m_w_ukv=m_w_ukv, m_mla_q_norm_gain=m_mla_q_norm_gain, m_mla_k_norm_gain=m_mla_k_norm_gain, m_gdn_conv_w=m_gdn_conv_w, m_gdn_a_log=m_gdn_a_log, m_gdn_dt_bias=m_gdn_dt_bias, m_gdn_out_norm_gain=m_gdn_out_norm_gain, m_w_out=m_w_out, v_norm_gain=v_norm_gain, v_w_in=v_w_in, v_mla_q_a_gain=v_mla_q_a_gain, v_mla_kv_a_gain=v_mla_kv_a_gain, v_w_uq=v_w_uq, v_w_ukv=v_w_ukv, v_mla_q_norm_gain=v_mla_q_norm_gain, v_mla_k_norm_gain=v_mla_k_norm_gain, v_gdn_conv_w=v_gdn_conv_w, v_gdn_a_log=v_gdn_a_log, v_gdn_dt_bias=v_gdn_dt_bias, v_gdn_out_norm_gain=v_gdn_out_norm_gain, v_w_out=v_w_out)
    weights = {n: given[n] for n in TWIN_WEIGHTS}
    shared = {n: given[n] for n in SHARED_INPUTS}
    per_example = {n: given[n] for n in ['x', 'positions']}
    grad_fn = _jax.value_and_grad(_loss, argnums=(0, 1))

    def one_microbatch(ex, loss_target):
        ex = dict(ex)
        diff = ex.pop(TWIN_DIFF_INPUT)
        return grad_fn(weights, diff, {**shared, **ex}, loss_target)

    if N_MICROBATCH == 1:
        loss, (grad_w, grad_x) = one_microbatch(per_example, given["loss_target"])
    else:
        def body(carry, xs):
            loss_sum, grad_sum = carry
            l_k, (gw_k, gx_k) = one_microbatch(xs[0], xs[1])
            with _jax.named_scope("update"):
                return (loss_sum + l_k, _jax.tree.map(_jnp.add, grad_sum, gw_k)), gx_k

        init = (_jnp.zeros((), _jnp.float32), _jax.tree.map(_jnp.zeros_like, weights))
        (loss, grad_w), grad_x = _jax.lax.scan(body, init, (per_example, given["loss_target"]))
    with _jax.named_scope("update"):
        delta_w, new_m, new_v = {}, {}, {}
        for n in TWIN_WEIGHTS:
            delta_w[n], new_m[n], new_v[n] = _adamw(weights[n], grad_w[n], given["m_" + n], given["v_" + n])
    return (loss, grad_x, *[grad_w[n] for n in TWIN_WEIGHTS], *[delta_w[n] for n in TWIN_WEIGHTS],
            *[new_m[n] for n in TWIN_WEIGHTS], *[new_v[n] for n in TWIN_WEIGHTS])
```

```python
import functools
import math

import jax
import jax.numpy as jnp
from jax import lax
from jax.experimental import pallas as pl
from jax.experimental.pallas import tpu as pltpu

F32 = jnp.float32
BF16 = jnp.bfloat16
MESH = pl.DeviceIdType.MESH

D_MODEL = 2048
HEADS = 8
HEAD_DIM = 128
QK_DIM = 192
QK_PAD = 256
HALF_ROPE = 32
CHUNK = 64
NORM_EPS = 1e-6
W_IN_COLS = 5968
W_IN_PAD = 6144
GA_LANE = 64
GB_LANE = 72
ADAM_LR, ADAM_B1, ADAM_B2, ADAM_EPS, ADAM_WD, ADAM_STEP = 0.001, 0.9, 0.999, 1e-08, 0.01, 10
VMEM_LIMIT_V7X = 52 * 1024 * 1024
HI = lax.Precision.HIGHEST
NN = (((1,), (0,)), ((), ()))
NT = (((1,), (1,)), ((), ()))
TN = (((0,), (0,)), ((), ()))

TILES = {"row": 512, "attn": 1024, "mm": 1024}


def _call(body, *, name, grid, in_specs, out_specs, out_shape, args, scratch=(), sem=None):
    def kfn(*refs):
        body(*refs)
    if sem is None:
        sem = ("arbitrary",) * len(grid)
    return pl.pallas_call(
        kfn, name=name, grid=grid, in_specs=in_specs, out_specs=out_specs, out_shape=out_shape,
        scratch_shapes=list(scratch),
        compiler_params=pltpu.CompilerParams(dimension_semantics=sem, vmem_limit_bytes=VMEM_LIMIT_V7X),
    )(*args)


def _rows(tm, w, cb=0):
    return pl.BlockSpec((tm, w), lambda i: (i, cb))


def _full(shape):
    n = len(shape)
    return pl.BlockSpec(shape, lambda *_: (0,) * n)


def _sds(shape, dtype):
    return jax.ShapeDtypeStruct(shape, dtype)


def _acc8(x):
    tm, c = x.shape
    return jnp.sum(x.reshape(tm // 8, 8, c), axis=0)


def _sigmoid(x):
    return 1.0 / (1.0 + jnp.exp(-x))


def _silu(x):
    return x * _sigmoid(x)


def _dsilu(x):
    s = _sigmoid(x)
    return s * (1.0 + x * (1.0 - s))


def _dot(a, b, dims=NN):
    return lax.dot_general(a.astype(BF16), b.astype(BF16), dims, preferred_element_type=F32)


def _dot_hi(a, b, dims=NN):
    return lax.dot_general(a, b, dims, precision=HI, preferred_element_type=F32)


def _matmul(a, b, *, mode, out_dtype, name, tm=None, tn=None, tk=None):
    if mode == "tn":
        kdim, m = a.shape
    else:
        m, kdim = a.shape
    n = b.shape[0] if mode == "nt" else b.shape[1]
    tm = min(tm or TILES["mm"], m)
    tn = min(tn or TILES["mm"], n)
    tk = min(tk or kdim, kdim)
    nk = kdim // tk
    dims = {"nn": NN, "nt": NT, "tn": TN}[mode]
    if mode == "tn":
        a_spec = pl.BlockSpec((tk, tm), lambda i, j, k: (k, i))
    else:
        a_spec = pl.BlockSpec((tm, tk), lambda i, j, k: (i, k))
    if mode == "nt":
        b_spec = pl.BlockSpec((tn, tk), lambda i, j, k: (j, k))
    else:
        b_spec = pl.BlockSpec((tk, tn), lambda i, j, k: (k, j))

    def body(a_ref, b_ref, o_ref):
        r = _dot(a_ref[...], b_ref[...], dims)
        if nk == 1:
            o_ref[...] = r.astype(o_ref.dtype)
        else:
            k = pl.program_id(2)

            @pl.when(k == 0)
            def _():
                o_ref[...] = r

            @pl.when(k > 0)
            def _():
                o_ref[...] += r

    if nk > 1:
        assert out_dtype == F32
    return _call(body, name=name, grid=(m // tm, n // tn, nk), in_specs=[a_spec, b_spec],
                 out_specs=pl.BlockSpec((tm, tn), lambda i, j, k: (i, j)), out_shape=_sds((m, n), out_dtype),
                 args=(a, b))


def _norm1_fwd(x, gain):
    s = x.shape[0]
    tm = min(TILES["row"], s)

    def body(x_ref, g_ref, o_ref):
        xv = x_ref[...]
        r = lax.rsqrt(jnp.mean(xv * xv, axis=-1, keepdims=True) + NORM_EPS)
        o_ref[...] = (xv * r * g_ref[...]).astype(BF16)

    return _call(body, name="norm1_fwd", grid=(s // tm,), in_specs=[_rows(tm, D_MODEL), _full((1, D_MODEL))],
                 out_specs=_rows(tm, D_MODEL), out_shape=_sds((s, D_MODEL), BF16), args=(x, gain))


def _norm1_bwd(x, gain, dxn, dy):
    s = x.shape[0]
    tm = min(TILES["row"], s)

    def body(x_ref, g_ref, dxn_ref, dy_ref, gx_ref, dg_ref):
        xv = x_ref[...]
        r = lax.rsqrt(jnp.mean(xv * xv, axis=-1, keepdims=True) + NORM_EPS)
        nrm = xv * r
        d = dxn_ref[...]
        dn = d * g_ref[...]
        gx_ref[...] = dy_ref[...] + r * (dn - nrm * jnp.mean(dn * nrm, axis=-1, keepdims=True))

        @pl.when(pl.program_id(0) == 0)
        def _():
            dg_ref[...] = jnp.zeros_like(dg_ref)

        dg_ref[...] += _acc8(d * nrm)

    return _call(body, name="norm1_bwd", grid=(s // tm,),
                 in_specs=[_rows(tm, D_MODEL), _full((1, D_MODEL)), _rows(tm, D_MODEL), _rows(tm, D_MODEL)],
                 out_specs=[_rows(tm, D_MODEL), _full((8, D_MODEL))],
                 out_shape=[_sds((s, D_MODEL), F32), _sds((8, D_MODEL), F32)], args=(x, gain, dxn, dy))


def _rms(xv, width):
    return lax.rsqrt(jnp.sum(xv * xv, axis=-1, keepdims=True) * (1.0 / width) + NORM_EPS)


def _mla_a_norm(proj, gq, gkv):
    s = proj.shape[0]
    tm = min(TILES["row"], s)

    def body(cq_ref, ckv_ref, gq_ref, gkv_ref, oq_ref, okv_ref):
        a = cq_ref[...]
        oq_ref[...] = (a * _rms(a, 512) * gq_ref[...]).astype(BF16)
        b = ckv_ref[...]
        okv_ref[...] = (b * _rms(b, 256) * gkv_ref[...]).astype(BF16)

    return _call(body, name="mla_a_norm", grid=(s // tm,),
                 in_specs=[_rows(tm, 512, 0), _rows(tm, 256, 2), _full((1, 512)), _full((1, 256))],
                 out_specs=[_rows(tm, 512), _rows(tm, 256)],
                 out_shape=[_sds((s, 512), BF16), _sds((s, 256), BF16)], args=(proj, proj, gq, gkv))


def _rms_bwd(xv, gain, d, width):
    r = _rms(xv, width)
    nrm = xv * r
    dn = d * gain
    dx = r * (dn - nrm * (jnp.sum(dn * nrm, axis=-1, keepdims=True) * (1.0 / width)))
    return dx, d * nrm


def _mla_a_norm_bwd(proj, gq, gkv, dcqn, dckvn):
    s = proj.shape[0]
    tm = min(TILES["row"], s)

    def body(cq_ref, ckv_ref, gq_ref, gkv_ref, dq_ref, dkv_ref, oq_ref, okv_ref, aq_ref, akv_ref):
        dxq, gq_part = _rms_bwd(cq_ref[...], gq_ref[...], dq_ref[...], 512)
        dxk, gk_part = _rms_bwd(ckv_ref[...], gkv_ref[...], dkv_ref[...], 256)
        oq_ref[...] = dxq.astype(BF16)
        okv_ref[...] = dxk.astype(BF16)

        @pl.when(pl.program_id(0) == 0)
        def _():
            aq_ref[...] = jnp.zeros_like(aq_ref)
            akv_ref[...] = jnp.zeros_like(akv_ref)

        aq_ref[...] += _acc8(gq_part)
        akv_ref[...] += _acc8(gk_part)

    return _call(body, name="mla_a_norm_bwd", grid=(s // tm,),
                 in_specs=[_rows(tm, 512, 0), _rows(tm, 256, 2), _full((1, 512)), _full((1, 256)),
                           _rows(tm, 512), _rows(tm, 256)],
                 out_specs=[_rows(tm, 512), _rows(tm, 256), _full((8, 512)), _full((8, 256))],
                 out_shape=[_sds((s, 512), BF16), _sds((s, 256), BF16), _sds((8, 512), F32), _sds((8, 256), F32)],
                 args=(proj, proj, gq, gkv, dcqn, dckvn))


def _swap32(r):
    lane = lax.broadcasted_iota(jnp.int32, r.shape, 1)
    return jnp.where(lane < HALF_ROPE, pltpu.roll(r, 128 - HALF_ROPE, 1), pltpu.roll(r, HALF_ROPE, 1))


def _mla_post_fwd(q_pre, kv_pre, proj, cs, sn, gq, gk):
    s = q_pre.shape[0]
    tm = min(TILES["row"], s)

    def body(qp_ref, kvp_ref, misc_ref, cs_ref, sn_ref, gq_ref, gk_ref, q_ref, k_ref, v_ref):
        csv, snv = cs_ref[...], sn_ref[...]
        lane = lax.broadcasted_iota(jnp.int32, (tm, 128), 1)
        kr = jnp.where(lane < 64, misc_ref[...], 0.0)
        for h in range(HEADS):
            for src, g_ref, o_ref in ((None, gq_ref, q_ref), (kr, gk_ref, k_ref)):
                if src is None:
                    xv = qp_ref[:, QK_PAD * h:QK_PAD * (h + 1)]
                else:
                    xv = jnp.concatenate([kvp_ref[:, 256 * h:256 * h + 128], src], axis=-1)
                y = xv * _rms(xv, QK_DIM) * g_ref[...]
                hi = y[:, 128:]
                hi = hi * csv + _swap32(hi) * snv
                o_ref[:, QK_PAD * h:QK_PAD * h + 128] = y[:, :128].astype(BF16)
                o_ref[:, QK_PAD * h + 128:QK_PAD * (h + 1)] = hi.astype(BF16)
            v_ref[:, 128 * h:128 * (h + 1)] = kvp_ref[:, 256 * h + 128:256 * (h + 1)].astype(BF16)

    return _call(body, name="mla_post_fwd", grid=(s // tm,),
                 in_specs=[_rows(tm, 2048), _rows(tm, 2048), _rows(tm, 128, 6), _rows(tm, 128), _rows(tm, 128),
                           _full((1, QK_PAD)), _full((1, QK_PAD))],
                 out_specs=[_rows(tm, 2048), _rows(tm, 2048), _rows(tm, 1024)],
                 out_shape=[_sds((s, 2048), BF16), _sds((s, 2048), BF16), _sds((s, 1024), BF16)],
                 args=(q_pre, kv_pre, proj, cs, sn, gq, gk))


def _mla_post_bwd(q_pre, kv_pre, proj, cs, sn, gq, gk, dq, dk, dv):
    s = q_pre.shape[0]
    tm = min(TILES["row"], s)

    def body(qp_ref, kvp_ref, misc_ref, cs_ref, sn_ref, gq_ref, gk_ref, dq_ref, dk_ref, dv_ref,
             oq_ref, okv_ref, okr_ref, agq_ref, agk_ref):
        csv, snv = cs_ref[...], sn_ref[...]
        lane = lax.broadcasted_iota(jnp.int32, (tm, 128), 1)
        kr = jnp.where(lane < 64, misc_ref[...], 0.0)

        @pl.when(pl.program_id(0) == 0)
        def _():
            agq_ref[...] = jnp.zeros_like(agq_ref)
            agk_ref[...] = jnp.zeros_like(agk_ref)

        dkr = jnp.zeros((tm, 128), F32)
        for h in range(HEADS):
            for which in (0, 1):
                if which == 0:
                    xv = qp_ref[:, QK_PAD * h:QK_PAD * (h + 1)]
                    d_ref, g_ref, a_ref = dq_ref, gq_ref, agq_ref
                else:
                    xv = jnp.concatenate([kvp_ref[:, 256 * h:256 * h + 128], kr], axis=-1)
                    d_ref, g_ref, a_ref = dk_ref, gk_ref, agk_ref
                dhi = d_ref[:, QK_PAD * h + 128:QK_PAD * (h + 1)]
                dhi = dhi * csv - _swap32(dhi) * snv
                dyv = jnp.concatenate([d_ref[:, QK_PAD * h:QK_PAD * h + 128], dhi], axis=-1)
                dx, gpart = _rms_bwd(xv, g_ref[...], dyv, QK_DIM)
                a_ref[...] += _acc8(gpart)
                if which == 0:
                    oq_ref[:, QK_PAD * h:QK_PAD * (h + 1)] = dx.astype(BF16)
                else:
                    okv_ref[:, 256 * h:256 * h + 128] = dx[:, :128].astype(BF16)
                    dkr = dkr + dx[:, 128:]
            okv_ref[:, 256 * h + 128:256 * (h + 1)] = dv_ref[:, 128 * h:128 * (h + 1)].astype(BF16)
        okr_ref[...] = dkr

    return _call(body, name="mla_post_bwd", grid=(s // tm,),
                 in_specs=[_rows(tm, 2048), _rows(tm, 2048), _rows(tm, 128, 6), _rows(tm, 128), _rows(tm, 128),
                           _full((1, QK_PAD)), _full((1, QK_PAD)), _rows(tm, 2048), _rows(tm, 2048), _rows(tm, 1024)],
                 out_specs=[_rows(tm, 2048), _rows(tm, 2048), _rows(tm, 128), _full((8, QK_PAD)), _full((8, QK_PAD))],
                 out_shape=[_sds((s, 2048), BF16), _sds((s, 2048), BF16), _sds((s, 128), F32),
                            _sds((8, QK_PAD), F32), _sds((8, QK_PAD), F32)],
                 args=(q_pre, kv_pre, proj, cs, sn, gq, gk, dq, dk, dv))


ATTN_SCALE = QK_DIM ** -0.5
NEG = -1e30


def _causal_mask(sc, i, j, t):
    row = lax.broadcasted_iota(jnp.int32, (t, t), 0) + i * t
    col = lax.broadcasted_iota(jnp.int32, (t, t), 1) + j * t
    return jnp.where(col <= row, sc, NEG)


def _attn_fwd(q, k, v):
    s = q.shape[0]
    t = min(TILES["attn"], s)
    nq = s // t

    def body(q_ref, k_ref, v_ref, o_ref, lse_ref, m_s, l_s, acc_s):
        i, j = pl.program_id(1), pl.program_id(2)

        @pl.when(j == 0)
        def _():
            m_s[...] = jnp.full_like(m_s, NEG)
            l_s[...] = jnp.zeros_like(l_s)
            acc_s[...] = jnp.zeros_like(acc_s)

        @pl.when(j <= i)
        def _():
            sc = lax.dot_general(q_ref[...], k_ref[...], NT, preferred_element_type=F32) * ATTN_SCALE
            sc = _causal_mask(sc, i, j, t)
            m_prev = m_s[...]
            m_new = jnp.maximum(m_prev, jnp.max(sc, axis=-1, keepdims=True))
            p = jnp.exp(sc - m_new)
            alpha = jnp.exp(m_prev - m_new)
            l_s[...] = alpha * l_s[...] + jnp.sum(p, axis=-1, keepdims=True)
            acc_s[...] = acc_s[...] * alpha + lax.dot_general(p.astype(BF16), v_ref[...], NN,
                                                              preferred_element_type=F32)
            m_s[...] = m_new

        @pl.when(j == i)
        def _():
            o_ref[...] = acc_s[...] / l_s[...]
            lse_ref[...] = m_s[...] + jnp.log(l_s[...])

    return _call(
        body, name="attn_fwd", grid=(HEADS, nq, nq),
        in_specs=[pl.BlockSpec((t, QK_PAD), lambda h, i, j: (i, h)),
                  pl.BlockSpec((t, QK_PAD), lambda h, i, j: (jnp.minimum(i, j), h)),
                  pl.BlockSpec((t, HEAD_DIM), lambda h, i, j: (jnp.minimum(i, j), h))],
        out_specs=[pl.BlockSpec((t, HEAD_DIM), lambda h, i, j: (i, h)),
                   pl.BlockSpec((None, t, 1), lambda h, i, j: (h, i, 0))],
        out_shape=[_sds((s, HEADS * HEAD_DIM), F32), _sds((HEADS, s, 1), F32)],
        scratch=[pltpu.VMEM((t, 1), F32), pltpu.VMEM((t, 1), F32), pltpu.VMEM((t, HEAD_DIM), F32)],
        sem=("parallel", "parallel", "arbitrary"), args=(q, k, v))


def _attn_bwd(q, k, v, o, lse, do):
    s = q.shape[0]
    t = min(TILES["attn"], s)
    nq = s // t

    def body(q_ref, k_ref, v_ref, o_ref, lse_ref, do_ref, dq_ref, dk_ref, dv_ref):
        j, i = pl.program_id(1), pl.program_id(2)

        @pl.when((j == 0) & (i == 0))
        def _():
            dq_ref[...] = jnp.zeros_like(dq_ref)

        @pl.when(i == 0)
        def _():
            dk_ref[...] = jnp.zeros_like(dk_ref)
            dv_ref[...] = jnp.zeros_like(dv_ref)

        @pl.when(i >= j)
        def _():
            qv, kv = q_ref[...], k_ref[...]
            sc = lax.dot_general(qv, kv, NT, preferred_element_type=F32) * ATTN_SCALE
            sc = _causal_mask(sc, i, j, t)
            p = jnp.exp(sc - lse_ref[...])
            dof = do_ref[...]
            delta = jnp.sum(dof * o_ref[...], axis=-1, keepdims=True)
            dob = dof.astype(BF16)
            dv_ref[...] += lax.dot_general(p.astype(BF16), dob, TN, preferred_element_type=F32)
            dp = lax.dot_general(dob, v_ref[...], NT, preferred_element_type=F32)
            ds = (p * (dp - delta) * ATTN_SCALE).astype(BF16)
            dk_ref[...] += lax.dot_general(ds, qv, TN, preferred_element_type=F32)
            rows = pl.ds(pl.multiple_of(i * t, t), t)
            dq_ref[rows, :] += lax.dot_general(ds, kv, NN, preferred_element_type=F32)

    qi = lambda h, j, i: (jnp.maximum(i, j), h)
    return _call(
        body, name="attn_bwd", grid=(HEADS, nq, nq),
        in_specs=[pl.BlockSpec((t, QK_PAD), qi),
                  pl.BlockSpec((t, QK_PAD), lambda h, j, i: (j, h)),
                  pl.BlockSpec((t, HEAD_DIM), lambda h, j, i: (j, h)),
                  pl.BlockSpec((t, HEAD_DIM), qi),
                  pl.BlockSpec((None, t, 1), lambda h, j, i: (h, jnp.maximum(i, j), 0)),
                  pl.BlockSpec((t, HEAD_DIM), qi)],
        out_specs=[pl.BlockSpec((s, QK_PAD), lambda h, j, i: (0, h)),
                   pl.BlockSpec((t, QK_PAD), lambda h, j, i: (j, h)),
                   pl.BlockSpec((t, HEAD_DIM), lambda h, j, i: (j, h))],
        out_shape=[_sds((s, HEADS * QK_PAD), F32), _sds((s, HEADS * QK_PAD), F32), _sds((s, HEADS * HEAD_DIM), F32)],
        sem=("parallel", "arbitrary", "arbitrary"), args=(q, k, v, o, lse, do))


GDN_Q_SCALE = HEAD_DIM ** -0.5


def _shift_down(xv, prev8, sft):
    rolled = pltpu.roll(xv, sft, 0)
    top = pltpu.roll(jnp.concatenate([prev8, xv[:8]], axis=0), sft, 0)[8:]
    return jnp.concatenate([top, rolled[8:]], axis=0)


def _shift_up(xv, next8, sft):
    tm = xv.shape[0]
    rolled = pltpu.roll(xv, tm - sft, 0)
    bot = pltpu.roll(jnp.concatenate([xv[tm - 8:], next8], axis=0), 16 - sft, 0)[:8]
    return jnp.concatenate([rolled[:tm - 8], bot], axis=0)


def _conv_z(xv, prev8, w_ref):
    z = xv * w_ref[3:4, :]
    for sft in (1, 2, 3):
        z = z + _shift_down(xv, prev8, sft) * w_ref[3 - sft:4 - sft, :]
    return z


def _conv_specs(s, tm):
    nb8 = tm // 8
    cur = pl.BlockSpec((tm, 1024), lambda j, i: (i, 2 + j))
    prev = pl.BlockSpec((8, 1024), lambda j, i: (jnp.maximum(i * nb8 - 1, 0), 2 + j))
    return cur, prev


def _gdn_conv_fwd(proj, conv_w):
    s = proj.shape[0]
    tm = min(TILES["row"], s)
    cur, prev = _conv_specs(s, tm)

    def body(x_ref, xp_ref, w_ref, o_ref):
        j, i = pl.program_id(0), pl.program_id(1)
        prev8 = jnp.where(i > 0, xp_ref[...], 0.0)
        a = _silu(_conv_z(x_ref[...], prev8, w_ref))
        qk_scale = jnp.where(j == 0, GDN_Q_SCALE, 1.0)
        for h in range(HEADS):
            seg = a[:, 128 * h:128 * (h + 1)]
            r = lax.rsqrt(jnp.sum(seg * seg, axis=-1, keepdims=True) + NORM_EPS)
            o_ref[:, 128 * h:128 * (h + 1)] = jnp.where(j < 2, seg * r * qk_scale, seg)

    return _call(body, name="gdn_conv_fwd", grid=(3, s // tm),
                 in_specs=[cur, prev, pl.BlockSpec((4, 1024), lambda j, i: (0, j))],
                 out_specs=pl.BlockSpec((tm, 1024), lambda j, i: (i, j)), out_shape=_sds((s, 3072), F32),
                 args=(proj, proj, conv_w))


def _gdn_conv_bwd_a(proj, conv_w, dqkv):
    s = proj.shape[0]
    tm = min(TILES["row"], s)
    cur, prev = _conv_specs(s, tm)

    def body(x_ref, xp_ref, w_ref, d_ref, o_ref):
        j, i = pl.program_id(0), pl.program_id(1)
        prev8 = jnp.where(i > 0, xp_ref[...], 0.0)
        z = _conv_z(x_ref[...], prev8, w_ref)
        a = _silu(z)
        dsl = _dsilu(z)
        qk_scale = jnp.where(j == 0, GDN_Q_SCALE, 1.0)
        for h in range(HEADS):
            sl = slice(128 * h, 128 * (h + 1))
            seg = a[:, sl]
            dyv = d_ref[:, sl]
            r = lax.rsqrt(jnp.sum(seg * seg, axis=-1, keepdims=True) + NORM_EPS)
            yh = seg * r
            da_n = qk_scale * r * (dyv - yh * jnp.sum(yh * dyv, axis=-1, keepdims=True))
            o_ref[:, sl] = jnp.where(j < 2, da_n, dyv) * dsl[:, sl]

    return _call(body, name="gdn_conv_bwd_a", grid=(3, s // tm),
                 in_specs=[cur, prev, pl.BlockSpec((4, 1024), lambda j, i: (0, j)),
                           pl.BlockSpec((tm, 1024), lambda j, i: (i, j))],
                 out_specs=pl.BlockSpec((tm, 1024), lambda j, i: (i, j)), out_shape=_sds((s, 3072), F32),
                 args=(proj, proj, conv_w, dqkv))


def _gdn_conv_bwd_b(proj, conv_w, dz):
    s = proj.shape[0]
    tm = min(TILES["row"], s)
    nb8 = tm // 8
    last8 = s // 8 - 1
    cur, prev = _conv_specs(s, tm)

    def body(x_ref, xp_ref, w_ref, dz_ref, dzn_ref, dx_ref, dw_ref):
        i = pl.program_id(1)
        prev8 = jnp.where(i > 0, xp_ref[...], 0.0)
        next8 = jnp.where(i < pl.num_programs(1) - 1, dzn_ref[...], 0.0)
        xv, dzv = x_ref[...], dz_ref[...]

        @pl.when(i == 0)
        def _():
            dw_ref[...] = jnp.zeros_like(dw_ref)

        dx = dzv * w_ref[3:4, :]
        dw_ref[3:4, :] += jnp.sum(dzv * xv, axis=0, keepdims=True)
        for sft in (1, 2, 3):
            dx = dx + _shift_up(dzv, next8, sft) * w_ref[3 - sft:4 - sft, :]
            dw_ref[3 - sft:4 - sft, :] += jnp.sum(dzv * _shift_down(xv, prev8, sft), axis=0, keepdims=True)
        dx_ref[...] = dx.astype(BF16)

    return _call(body, name="gdn_conv_bwd_b", grid=(3, s // tm),
                 in_specs=[cur, prev, pl.BlockSpec((4, 1024), lambda j, i: (0, j)),
                           pl.BlockSpec((tm, 1024), lambda j, i: (i, j)),
                           pl.BlockSpec((8, 1024), lambda j, i: (jnp.minimum((i + 1) * nb8, last8), j))],
                 out_specs=[pl.BlockSpec((tm, 1024), lambda j, i: (i, j)), pl.BlockSpec((4, 1024), lambda j, i: (0, j))],
                 out_shape=[_sds((s, 3072), BF16), _sds((4, 3072), F32)], args=(proj, proj, conv_w, dz, dz))


def _softplus(xv):
    return jnp.maximum(xv, 0.0) + jnp.log(1.0 + jnp.exp(-jnp.abs(xv)))


def _gdn_gates_fwd(proj, alog128, dtb128):
    s = proj.shape[0]
    tm = min(TILES["row"], s)

    def body(m_ref, a_ref, b_ref, o_ref):
        mv = m_ref[...]
        lane = lax.broadcasted_iota(jnp.int32, mv.shape, 1)
        g = -jnp.exp(a_ref[...]) * _softplus(mv + b_ref[...])
        is_g = (lane >= GA_LANE) & (lane < GA_LANE + HEADS)
        is_b = (lane >= GB_LANE) & (lane < GB_LANE + HEADS)
        o_ref[...] = jnp.where(is_g, g, jnp.where(is_b, _sigmoid(mv), 0.0))

    return _call(body, name="gdn_gates_fwd", grid=(s // tm,),
                 in_specs=[_rows(tm, 128, 6), _full((1, 128)), _full((1, 128))],
                 out_specs=_rows(tm, 128), out_shape=_sds((s, 128), F32), args=(proj, alog128, dtb128))


def _gdn_gates_bwd(proj, alog128, dtb128, gbeta, dgbeta, dkr):
    s = proj.shape[0]
    tm = min(TILES["row"], s)

    def body(m_ref, a_ref, b_ref, gb_ref, d_ref, kr_ref, o_ref, da_ref, db_ref):
        mv, dv = m_ref[...], d_ref[...]
        lane = lax.broadcasted_iota(jnp.int32, mv.shape, 1)
        is_g = (lane >= GA_LANE) & (lane < GA_LANE + HEADS)
        is_b = (lane >= GB_LANE) & (lane < GB_LANE + HEADS)
        dga = jnp.where(is_g, dv * (-jnp.exp(a_ref[...])) * _sigmoid(mv + b_ref[...]), 0.0)
        beta = gb_ref[...]
        dgb = jnp.where(is_b, dv * beta * (1.0 - beta), 0.0)
        o_ref[...] = jnp.where(lane < 64, kr_ref[...], dga + dgb).astype(BF16)

        @pl.when(pl.program_id(0) == 0)
        def _():
            da_ref[...] = jnp.zeros_like(da_ref)
            db_ref[...] = jnp.zeros_like(db_ref)

        da_ref[...] += _acc8(jnp.where(is_g, dv * gb_ref[...], 0.0))
        db_ref[...] += _acc8(dga)

    return _call(body, name="gdn_gates_bwd", grid=(s // tm,),
                 in_specs=[_rows(tm, 128, 6), _full((1, 128)), _full((1, 128)), _rows(tm, 128), _rows(tm, 128),
                           _rows(tm, 128)],
                 out_specs=[_rows(tm, 128), _full((8, 128)), _full((8, 128))],
                 out_shape=[_sds((s, 128), BF16), _sds((8, 128), F32), _sds((8, 128), F32)],
                 args=(proj, alog128, dtb128, gbeta, dgbeta, dkr))


def _col(mat, lane_idx, lane):
    return jnp.sum(jnp.where(lane == lane_idx, mat, 0.0), axis=-1, keepdims=True)


def _chunk_local(qh, kh, vh, gcol, bcol, ii, jj):
    lower, strict, eye = ii >= jj, ii > jj, ii == jj
    grow = jnp.sum(jnp.where(eye, gcol, 0.0), axis=0, keepdims=True)
    decay = jnp.where(lower, jnp.exp(jnp.where(lower, gcol - grow, 0.0)), 0.0)
    kb = kh * bcol
    vb = vh * bcol
    mm = _dot(kb, kh, NT)
    lmat = jnp.where(strict, mm * decay, 0.0)
    pw = -lmat
    tinv = jnp.where(eye, 1.0, 0.0) + pw
    for _ in range(5):
        pw = _dot_hi(pw, pw)
        tinv = tinv + _dot_hi(tinv, pw)
    egc = jnp.exp(gcol)
    kbg = kb * egc
    rhs = jnp.concatenate([vb, kbg], axis=-1)
    sol = _dot_hi(tinv, rhs)
    qk = _dot(qh, kh, NT)
    glast = jnp.sum(jnp.where(ii[:, :1] == CHUNK - 1, gcol, 0.0), axis=0, keepdims=True)
    ekd = jnp.exp(glast - gcol)
    return dict(decay=decay, kb=kb, vb=vb, mm=mm, lmat=lmat, tinv=tinv, egc=egc, kbg=kbg, rhs=rhs,
                u=sol[:, :HEAD_DIM], w=sol[:, HEAD_DIM:], qk=qk, amat=qk * decay, qd=qh * egc, ekd=ekd,
                kd=kh * ekd, gl=jnp.exp(glast), strict=strict, lower=lower, eye=eye)


def _tri(ii, jj):
    return jnp.where(ii >= jj, 1.0, 0.0)


def _gdn_fwd(qkv, gbeta):
    s = qkv.shape[0]
    n = s // CHUNK

    def body(qkv_ref, gb_ref, o_ref, st_ref, state):
        @pl.when(pl.program_id(0) == 0)
        def _():
            state[...] = jnp.zeros_like(state)

        ii = lax.broadcasted_iota(jnp.int32, (CHUNK, CHUNK), 0)
        jj = lax.broadcasted_iota(jnp.int32, (CHUNK, CHUNK), 1)
        lane = lax.broadcasted_iota(jnp.int32, (CHUNK, 128), 1)
        gbv = gb_ref[...]
        gc = _dot_hi(_tri(ii, jj), gbv)
        for h in range(HEADS):
            sl = slice(128 * h, 128 * (h + 1))
            qh = qkv_ref[:, 128 * h:128 * (h + 1)]
            kh = qkv_ref[:, 1024 + 128 * h:1024 + 128 * (h + 1)]
            vh = qkv_ref[:, 2048 + 128 * h:2048 + 128 * (h + 1)]
            c = _chunk_local(qh, kh, vh, _col(gc, GA_LANE + h, lane), _col(gbv, GB_LANE + h, lane), ii, jj)
            st = state[sl, :]
            st_ref[sl, :] = st
            vn = c["u"] - _dot(c["w"], st)
            o_ref[:, sl] = _dot(c["qd"], st) + _dot(c["amat"], vn)
            state[sl, :] = st * c["gl"] + _dot(c["kd"], vn, TN)

    return _call(body, name="gdn_fwd", grid=(n,),
                 in_specs=[_rows(CHUNK, 3072), _rows(CHUNK, 128)],
                 out_specs=[_rows(CHUNK, 1024), _rows(HEADS * 128, 128)],
                 out_shape=[_sds((s, 1024), F32), _sds((n * HEADS * 128, 128), F32)],
                 scratch=[pltpu.VMEM((HEADS * 128, 128), F32)], args=(qkv, gbeta))


def _gdn_bwd(qkv, gbeta, states, do):
    s = qkv.shape[0]
    n = s // CHUNK

    def body(qkv_ref, gb_ref, st_ref, do_ref, dqkv_ref, dgb_ref, dstate):
        @pl.when(pl.program_id(0) == 0)
        def _():
            dstate[...] = jnp.zeros_like(dstate)

        ii = lax.broadcasted_iota(jnp.int32, (CHUNK, CHUNK), 0)
        jj = lax.broadcasted_iota(jnp.int32, (CHUNK, CHUNK), 1)
        lane = lax.broadcasted_iota(jnp.int32, (CHUNK, 128), 1)
        row1 = ii[:, :1]
        gbv = gb_ref[...]
        gc = _dot_hi(_tri(ii, jj), gbv)
        dgc_all = jnp.zeros((CHUNK, 128), F32)
        db_all = jnp.zeros((CHUNK, 128), F32)
        for h in range(HEADS):
            sl = slice(128 * h, 128 * (h + 1))
            qh = qkv_ref[:, 128 * h:128 * (h + 1)]
            kh = qkv_ref[:, 1024 + 128 * h:1024 + 128 * (h + 1)]
            vh = qkv_ref[:, 2048 + 128 * h:2048 + 128 * (h + 1)]
            bcol = _col(gbv, GB_LANE + h, lane)
            c = _chunk_local(qh, kh, vh, _col(gc, GA_LANE + h, lane), bcol, ii, jj)
            st = st_ref[sl, :]
            dst = dstate[sl, :]
            dov = do_ref[:, sl]
            vn = c["u"] - _dot(c["w"], st)
            dvn = _dot(c["amat"], dov, TN) + _dot(c["kd"], dst)
            damat = jnp.where(c["lower"], _dot(dov, vn, NT), 0.0)
            dqd = _dot(dov, st, NT)
            dkd = _dot(vn, dst, NT)
            dw = -_dot(dvn, st, NT)
            dgl = jnp.sum(jnp.sum(st * dst, axis=-1, keepdims=True), axis=0, keepdims=True)
            dstate[sl, :] = _dot(c["qd"], dov, TN) + c["gl"] * dst - _dot(c["w"], dvn, TN)
            dsol = jnp.concatenate([dvn, dw], axis=-1)
            drhs = _dot_hi(c["tinv"], dsol, TN)
            dtinv = _dot_hi(dsol, c["rhs"], NT)
            dl = -_dot_hi(_dot_hi(c["tinv"], dtinv, TN), c["tinv"], NT)
            dl = jnp.where(c["strict"], dl, 0.0)
            dmm = dl * c["decay"]
            dqk = damat * c["decay"]
            wmat = dl * c["lmat"] + damat * c["amat"]
            dgc = jnp.sum(wmat, axis=-1, keepdims=True)
            wcol = jnp.sum(wmat, axis=0, keepdims=True)
            dgc = dgc - jnp.sum(jnp.where(c["eye"], wcol, 0.0), axis=-1, keepdims=True)
            dkb = _dot(dmm, kh) + drhs[:, HEAD_DIM:] * c["egc"]
            dk = _dot(dmm, c["kb"], TN) + _dot(dqk, qh, TN) + dkd * c["ekd"]
            dq = _dot(dqk, kh) + dqd * c["egc"]
            dgc = dgc + jnp.sum(drhs[:, HEAD_DIM:] * c["kbg"], axis=-1, keepdims=True)
            dgc = dgc + jnp.sum(dqd * c["qd"], axis=-1, keepdims=True)
            tmp = jnp.sum(dkd * c["kd"], axis=-1, keepdims=True)
            dgc = dgc - tmp
            dglast = jnp.sum(tmp, axis=0, keepdims=True) + dgl * c["gl"]
            dgc = dgc + jnp.where(row1 == CHUNK - 1, dglast, 0.0)
            dk = dk + dkb * bcol
            db = jnp.sum(dkb * kh, axis=-1, keepdims=True) + jnp.sum(drhs[:, :HEAD_DIM] * vh, axis=-1, keepdims=True)
            dqkv_ref[:, 128 * h:128 * (h + 1)] = dq
            dqkv_ref[:, 1024 + 128 * h:1024 + 128 * (h + 1)] = dk
            dqkv_ref[:, 2048 + 128 * h:2048 + 128 * (h + 1)] = drhs[:, :HEAD_DIM] * bcol
            dgc_all = dgc_all + jnp.where(lane == GA_LANE + h, dgc, 0.0)
            db_all = db_all + jnp.where(lane == GB_LANE + h, db, 0.0)
        dgb_ref[...] = _dot_hi(_tri(jj, ii), dgc_all) + db_all

    rev = lambda w: pl.BlockSpec((CHUNK, w), lambda i: (n - 1 - i, 0))
    return _call(body, name="gdn_bwd", grid=(n,),
                 in_specs=[rev(3072), rev(128), pl.BlockSpec((HEADS * 128, 128), lambda i: (n - 1 - i, 0)), rev(1024)],
                 out_specs=[rev(3072), rev(128)],
                 out_shape=[_sds((s, 3072), F32), _sds((s, 128), F32)],
                 scratch=[pltpu.VMEM((HEADS * 128, 128), F32)], args=(qkv, gbeta, states, do))


def _mix_fwd(o_mla, o_gdn, proj, out_gain):
    s = proj.shape[0]
    tm = min(TILES["row"], s)

    def body(om_ref, og_ref, mg_ref, gg_ref, g_ref, o_ref):
        o_ref[:, :1024] = (om_ref[...] * _silu(mg_ref[...])).astype(BF16)
        for h in range(HEADS):
            sl = slice(128 * h, 128 * (h + 1))
            og = og_ref[:, sl]
            on = og * _rms(og, HEAD_DIM) * g_ref[...]
            o_ref[:, 1024 + 128 * h:1024 + 128 * (h + 1)] = (on * _silu(gg_ref[:, sl])).astype(BF16)

    return _call(body, name="mix_fwd", grid=(s // tm,),
                 in_specs=[_rows(tm, 1024), _rows(tm, 1024), _rows(tm, 1024, 1), _rows(tm, 1024, 5), _full((1, 128))],
                 out_specs=_rows(tm, 2048), out_shape=_sds((s, 2048), BF16), args=(o_mla, o_gdn, proj, proj, out_gain))


def _mix_bwd(o_mla, o_gdn, proj, out_gain, dmixed):
    s = proj.shape[0]
    tm = min(TILES["row"], s)

    def body(om_ref, og_ref, mg_ref, gg_ref, g_ref, dm_ref, dg_ref, dom_ref, dog_ref, dmg_ref, dgg_ref, ag_ref):
        @pl.when(pl.program_id(0) == 0)
        def _():
            ag_ref[...] = jnp.zeros_like(ag_ref)

        mg = mg_ref[...]
        dom_ref[...] = dm_ref[...] * _silu(mg)
        dmg_ref[...] = (dm_ref[...] * om_ref[...] * _dsilu(mg)).astype(BF16)
        for h in range(HEADS):
            sl = slice(128 * h, 128 * (h + 1))
            og, gg, d = og_ref[:, sl], gg_ref[:, sl], dg_ref[:, sl]
            on = og * _rms(og, HEAD_DIM) * g_ref[...]
            dgg_ref[:, sl] = (d * on * _dsilu(gg)).astype(BF16)
            dx, gpart = _rms_bwd(og, g_ref[...], d * _silu(gg), HEAD_DIM)
            dog_ref[:, sl] = dx
            ag_ref[...] += _acc8(gpart)

    return _call(body, name="mix_bwd", grid=(s // tm,),
                 in_specs=[_rows(tm, 1024), _rows(tm, 1024), _rows(tm, 1024, 1), _rows(tm, 1024, 5), _full((1, 128)),
                           _rows(tm, 1024, 0), _rows(tm, 1024, 1)],
                 out_specs=[_rows(tm, 1024), _rows(tm, 1024), _rows(tm, 1024), _rows(tm, 1024), _full((8, 128))],
                 out_shape=[_sds((s, 1024), F32), _sds((s, 1024), F32), _sds((s, 1024), BF16), _sds((s, 1024), BF16),
                            _sds((8, 128), F32)],
                 args=(o_mla, o_gdn, proj, proj, out_gain, dmixed, dmixed))


def _out_fwd(mixed, w_out, x, target):
    s = x.shape[0]
    tm = min(TILES["mm"], s)
    tn = min(TILES["mm"], D_MODEL)

    def body(m_ref, w_ref, x_ref, t_ref, dy_ref, acc_ref):
        err = x_ref[...] + _dot(m_ref[...], w_ref[...]) - t_ref[...]
        dy_ref[...] = err * (1.0 / D_MODEL)

        @pl.when(pl.program_id(1) == 0)
        def _():
            acc_ref[...] = jnp.zeros_like(acc_ref)

        acc_ref[...] += _acc8(err * err)

    return _call(body, name="out_fwd", grid=(D_MODEL // tn, s // tm),
                 in_specs=[pl.BlockSpec((tm, D_MODEL), lambda j, i: (i, 0)), pl.BlockSpec((D_MODEL, tn), lambda j, i: (0, j)),
                           pl.BlockSpec((tm, tn), lambda j, i: (i, j)), pl.BlockSpec((tm, tn), lambda j, i: (i, j))],
                 out_specs=[pl.BlockSpec((tm, tn), lambda j, i: (i, j)), pl.BlockSpec((8, tn), lambda j, i: (0, j))],
                 out_shape=[_sds((s, D_MODEL), F32), _sds((8, D_MODEL), F32)], args=(mixed, w_out, x, target))


def _row_tile(r, c):
    if r % 8 != 0:
        return r
    t = 8
    while r % (2 * t) == 0 and 2 * t * c * 4 <= (1 << 20):
        t *= 2
    return t


def _sum_arrays(parts, name):
    r, c = parts[0].shape
    tr = _row_tile(r, c)

    def body(*refs):
        acc = refs[0][...]
        for p_ref in refs[1:-1]:
            acc = acc + p_ref[...]
        refs[-1][...] = acc

    return _call(body, name=name, grid=(r // tr,), in_specs=[_rows(tr, c)] * len(parts), out_specs=_rows(tr, c),
                 out_shape=_sds((r, c), F32), args=tuple(parts))


def _adamw(w, g, m, v, name):
    r, c = w.shape
    tr = _row_tile(r, c)
    c1 = 1.0 - ADAM_B1 ** ADAM_STEP
    c2 = 1.0 - ADAM_B2 ** ADAM_STEP

    def body(w_ref, g_ref, m_ref, v_ref, d_ref, nm_ref, nv_ref):
        gv = g_ref[...]
        nm = ADAM_B1 * m_ref[...] + (1.0 - ADAM_B1) * gv
        nv = ADAM_B2 * v_ref[...] + (1.0 - ADAM_B2) * (gv * gv)
        nm_ref[...] = nm
        nv_ref[...] = nv
        d_ref[...] = -ADAM_LR * ((nm / c1) / (jnp.sqrt(nv / c2) + ADAM_EPS) + ADAM_WD * w_ref[...])

    return _call(body, name=name, grid=(r // tr,), in_specs=[_rows(tr, c)] * 4, out_specs=[_rows(tr, c)] * 3,
                 out_shape=[_sds((r, c), F32)] * 3, args=(w, g, m, v))


ANY = pl.BlockSpec(memory_space=pl.ANY)
CHIP_FLIPS = ((1, 0), (0, 1), (1, 1))


def _comm_call(body, *, name, n_in, out_shape, scratch):
    def kfn(*refs):
        body(*refs)
    return pl.pallas_call(kfn, name=name, in_specs=[ANY] * n_in, out_specs=[ANY] * len(out_shape), out_shape=out_shape,
                          scratch_shapes=list(scratch),
                          compiler_params=pltpu.CompilerParams(has_side_effects=True))


def _all_gather_chips(shards):
    na = len(shards)

    def body(*refs):
        ins, outs = refs[:na], refs[na:2 * na]
        send_sems, recv_sems, local_sems = refs[2 * na:]
        x, y, c = lax.axis_index("x"), lax.axis_index("y"), lax.axis_index("c")
        my_k = 2 * x + y
        copies, local = [], []
        for a in range(na):
            cp = pltpu.make_async_copy(ins[a], outs[a].at[my_k], local_sems.at[a])
            cp.start()
            local.append(cp)
            for r, (fx, fy) in enumerate(CHIP_FLIPS):
                rc = pltpu.make_async_remote_copy(
                    src_ref=ins[a], dst_ref=outs[a].at[my_k], send_sem=send_sems.at[a, r], recv_sem=recv_sems.at[a, r],
                    device_id=(x ^ fx, y ^ fy, c), device_id_type=MESH)
                rc.start()
                copies.append(rc)
        for rc in copies:
            rc.wait()
        for cp in local:
            cp.wait()

    out_shape = [_sds((4,) + a.shape, a.dtype) for a in shards]
    return _comm_call(body, name="all_gather_weights", n_in=na, out_shape=out_shape,
                      scratch=[pltpu.SemaphoreType.DMA((na, 3)), pltpu.SemaphoreType.DMA((na, 3)),
                               pltpu.SemaphoreType.DMA((na,))])(*shards)


def _all_reduce_small(vec):
    r = vec.shape[0]

    def body(v_ref, o_ref, gath, send_sems, recv_sems):
        x, y, c = lax.axis_index("x"), lax.axis_index("y"), lax.axis_index("c")
        me = 4 * x + 2 * y + c
        gath[me] = v_ref[...]
        copies = []
        for rel in range(1, 8):
            fx, fy, fc = (rel >> 2) & 1, (rel >> 1) & 1, rel & 1
            rc = pltpu.make_async_remote_copy(
                src_ref=v_ref, dst_ref=gath.at[me], send_sem=send_sems.at[rel - 1], recv_sem=recv_sems.at[rel - 1],
                device_id=(x ^ fx, y ^ fy, c ^ fc), device_id_type=MESH)
            rc.start()
            copies.append(rc)
        for rc in copies:
            rc.wait()
        acc = gath[0]
        for d in range(1, 8):
            acc = acc + gath[d]
        o_ref[...] = acc

    def kfn(*refs):
        body(*refs)
    vm = pl.BlockSpec(memory_space=pltpu.VMEM)
    return pl.pallas_call(kfn, name="all_reduce_small", in_specs=[vm], out_specs=vm, out_shape=_sds((r, 128), F32),
                          scratch_shapes=[pltpu.VMEM((8, r, 128), F32), pltpu.SemaphoreType.DMA((7,)),
                                          pltpu.SemaphoreType.DMA((7,))],
                          compiler_params=pltpu.CompilerParams(has_side_effects=True))(vec)


def _exchange_halves(arrs):
    na = len(arrs)

    def body(*refs):
        ins, outs = refs[:na], refs[na:2 * na]
        send_sems, recv_sems = refs[2 * na:]
        x, y, c = lax.axis_index("x"), lax.axis_index("y"), lax.axis_index("c")
        copies = []
        for a in range(na):
            half = ins[a].shape[1] // 2
            src = ins[a].at[:, pl.ds(pl.multiple_of((1 - c) * half, 8), half), :]
            rc = pltpu.make_async_remote_copy(src_ref=src, dst_ref=outs[a], send_sem=send_sems.at[a],
                                              recv_sem=recv_sems.at[a], device_id=(x, y, 1 - c), device_id_type=MESH)
            rc.start()
            copies.append(rc)
        for rc in copies:
            rc.wait()

    out_shape = [_sds((4, a.shape[1] // 2, a.shape[2]), F32) for a in arrs]
    return _comm_call(body, name="rs_pair_exchange", n_in=na, out_shape=out_shape,
                      scratch=[pltpu.SemaphoreType.DMA((na,)), pltpu.SemaphoreType.DMA((na,))])(*arrs)


def _scatter_to_chips(arrs):
    na = len(arrs)

    def body(*refs):
        ins, outs = refs[:na], refs[na:2 * na]
        send_sems, recv_sems = refs[2 * na:]
        x, y, c = lax.axis_index("x"), lax.axis_index("y"), lax.axis_index("c")
        copies = []
        for a in range(na):
            for r, (fx, fy) in enumerate(CHIP_FLIPS):
                px, py = x ^ fx, y ^ fy
                rc = pltpu.make_async_remote_copy(
                    src_ref=ins[a].at[2 * px + py], dst_ref=outs[a].at[r], send_sem=send_sems.at[a, r],
                    recv_sem=recv_sems.at[a, r], device_id=(px, py, c), device_id_type=MESH)
                rc.start()
                copies.append(rc)
        for rc in copies:
            rc.wait()

    out_shape = [_sds((3,) + a.shape[1:], F32) for a in arrs]
    return _comm_call(body, name="rs_chip_scatter", n_in=na, out_shape=out_shape,
                      scratch=[pltpu.SemaphoreType.DMA((na, 3)), pltpu.SemaphoreType.DMA((na, 3))])(*arrs)


def _join_halves(arrs):
    na = len(arrs)

    def body(*refs):
        ins, outs = refs[:na], refs[na:2 * na]
        send_sems, recv_sems, local_sems = refs[2 * na:]
        x, y, c = lax.axis_index("x"), lax.axis_index("y"), lax.axis_index("c")
        copies = []
        for a in range(na):
            half = ins[a].shape[0]
            dst = outs[a].at[pl.ds(pl.multiple_of(c * half, 8), half), :]
            cp = pltpu.make_async_copy(ins[a], dst, local_sems.at[a])
            cp.start()
            rc = pltpu.make_async_remote_copy(src_ref=ins[a], dst_ref=dst, send_sem=send_sems.at[a],
                                              recv_sem=recv_sems.at[a], device_id=(x, y, 1 - c), device_id_type=MESH)
            rc.start()
            copies += [cp, rc]
        for cp in copies:
            cp.wait()

    out_shape = [_sds((2 * a.shape[0], a.shape[1]), F32) for a in arrs]
    return _comm_call(body, name="rs_pair_join", n_in=na, out_shape=out_shape,
                      scratch=[pltpu.SemaphoreType.DMA((na,)), pltpu.SemaphoreType.DMA((na,)),
                               pltpu.SemaphoreType.DMA((na,))])(*arrs)


def _reduce_scatter(grads):
    c = lax.axis_index("c")
    k_me = 2 * lax.axis_index("x") + lax.axis_index("y")
    got = _exchange_halves(grads)
    pair = []
    for a, (g, o) in enumerate(zip(grads, got)):
        half = g.shape[1] // 2
        mine = lax.dynamic_slice_in_dim(g, c * half, half, axis=1)
        pair.append(_sum_arrays([mine.reshape(4 * half, -1), o.reshape(4 * half, -1)], f"rs_pair_sum_{a}")
                    .reshape(4, half, -1))
    recv = _scatter_to_chips(pair)
    halves = []
    for a, (p, rv) in enumerate(zip(pair, recv)):
        own = lax.dynamic_index_in_dim(p, k_me, 0, keepdims=False)
        halves.append(_sum_arrays([own, rv[0], rv[1], rv[2]], f"rs_chip_sum_{a}"))
    return _join_halves(halves)


def _pad_w_in(w):
    z = jnp.zeros((w.shape[0], 1024 - 848), w.dtype)
    return jnp.concatenate([w[:, 0:832], w[:, 4928:4944], z, w[:, 832:4928], w[:, 4944:5968]], axis=1)


def _unpad_w_in(g):
    return jnp.concatenate([g[:, 0:832], g[:, 1024:5120], g[:, 832:848], g[:, 5120:6144]], axis=1)


def _pad_heads(w):
    r = w.shape[0]
    return jnp.pad(w.reshape(r, HEADS, QK_DIM), ((0, 0), (0, 0), (0, QK_PAD - QK_DIM))).reshape(r, HEADS * QK_PAD)


def _unpad_heads(w):
    r = w.shape[0]
    return w.reshape(r, HEADS, QK_PAD)[:, :, :QK_DIM].reshape(r, HEADS * QK_DIM)


def _cols_to_blocks(w):
    r = w.shape[0]
    return w.reshape(r, 4, -1).transpose(1, 0, 2)


def _blocks_to_cols(w):
    return w.transpose(1, 0, 2).reshape(w.shape[1], -1)


SMALL_ROWS = {"norm_gain": (0, 2048), "mla_q_a_gain": (16, 512), "mla_kv_a_gain": (20, 256),
              "mla_q_norm_gain": (22, 192), "mla_k_norm_gain": (24, 192), "gdn_a_log": (26, 8),
              "gdn_dt_bias": (27, 8), "gdn_out_norm_gain": (28, 128)}
LOSS_ROW = 29
SMALL_PACK_ROWS = 32
CONV_ROW = 32


def _pack_small(vals, loss=None):
    rows = []
    at = 0
    for name, (row, size) in SMALL_ROWS.items():
        assert row == at
        nr = -(-size // 128)
        rows.append(jnp.pad(vals[name].reshape(-1).astype(F32), (0, nr * 128 - size)).reshape(nr, 128))
        at += nr
    assert at == LOSS_ROW
    if loss is not None:
        rows.append(jnp.pad(loss.reshape(1, 1), ((0, 0), (0, 127))))
        at += 1
    rows.append(jnp.zeros((SMALL_PACK_ROWS - at, 128), F32))
    return jnp.concatenate(rows, axis=0)


def _unpack_small(pack, name):
    row, size = SMALL_ROWS[name]
    nr = -(-size // 128)
    return pack[row:row + nr].reshape(-1)[:size].reshape(1, size)


def _local_step(x, positions, target, norm_gain, w_in_p, q_a_gain, kv_a_gain, w_uq_p, w_ukv, q_norm_gain,
                k_norm_gain, conv_w, a_log, dt_bias, out_gain, w_out):
    half = HALF_ROPE
    inv_freq = jnp.power(10000.0, -jnp.arange(half, dtype=F32) / half)
    ang = positions.astype(F32)[:, None] * inv_freq
    cos, sin = jnp.cos(ang), jnp.sin(ang)
    zpad = jnp.zeros((x.shape[0], 64), F32)
    cs = jnp.concatenate([cos, cos, zpad], axis=1)
    sn = jnp.concatenate([-sin, sin, zpad], axis=1)
    gq = jnp.pad(q_norm_gain.reshape(1, QK_DIM), ((0, 0), (0, QK_PAD - QK_DIM)))
    gk = jnp.pad(k_norm_gain.reshape(1, QK_DIM), ((0, 0), (0, QK_PAD - QK_DIM)))
    lane_pad = ((0, 0), (GA_LANE, 128 - GA_LANE - HEADS))
    alog128 = jnp.pad(a_log.reshape(1, HEADS), lane_pad)
    dtb128 = jnp.pad(dt_bias.reshape(1, HEADS), lane_pad)
    ng, qag, kvag, og = (norm_gain.reshape(1, -1), q_a_gain.reshape(1, -1), kv_a_gain.reshape(1, -1),
                         out_gain.reshape(1, -1))

    xn = _norm1_fwd(x, ng)
    proj = _matmul(xn, w_in_p, mode="nn", out_dtype=F32, name="in_proj")
    cqn, ckvn = _mla_a_norm(proj, qag, kvag)
    q_pre = _matmul(cqn, w_uq_p, mode="nn", out_dtype=F32, name="q_up")
    kv_pre = _matmul(ckvn, w_ukv, mode="nn", out_dtype=F32, name="kv_up")
    q, k, v = _mla_post_fwd(q_pre, kv_pre, proj, cs, sn, gq, gk)
    o_mla, lse = _attn_fwd(q, k, v)
    qkv = _gdn_conv_fwd(proj, conv_w)
    gbeta = _gdn_gates_fwd(proj, alog128, dtb128)
    o_gdn, states = _gdn_fwd(qkv, gbeta)
    mixed = _mix_fwd(o_mla, o_gdn, proj, og)
    dy, sq = _out_fwd(mixed, w_out, x, target)

    dmixed = _matmul(dy, w_out, mode="nt", out_dtype=F32, name="d_mixed")
    d_w_out = _matmul(mixed, dy, mode="tn", out_dtype=F32, name="d_w_out", tk=1024)
    do_mla, do_gdn, dmg, dgg, d_out_gain = _mix_bwd(o_mla, o_gdn, proj, og, dmixed)
    dq, dk, dv = _attn_bwd(q, k, v, o_mla, lse, do_mla)
    dq_pre, dkv_pre, dkr, d_gq, d_gk = _mla_post_bwd(q_pre, kv_pre, proj, cs, sn, gq, gk, dq, dk, dv)
    d_w_uq_p = _matmul(cqn, dq_pre, mode="tn", out_dtype=F32, name="d_w_uq", tk=1024)
    d_w_ukv = _matmul(ckvn, dkv_pre, mode="tn", out_dtype=F32, name="d_w_ukv", tk=1024)
    dcqn = _matmul(dq_pre, w_uq_p, mode="nt", out_dtype=F32, name="d_cqn")
    dckvn = _matmul(dkv_pre, w_ukv, mode="nt", out_dtype=F32, name="d_ckvn")
    dcq, dckv, d_qag, d_kvag = _mla_a_norm_bwd(proj, qag, kvag, dcqn, dckvn)
    dqkv, dgbeta = _gdn_bwd(qkv, gbeta, states, do_gdn)
    dz = _gdn_conv_bwd_a(proj, conv_w, dqkv)
    dgx, d_conv = _gdn_conv_bwd_b(proj, conv_w, dz)
    dmisc, d_alog, d_dtb = _gdn_gates_bwd(proj, alog128, dtb128, gbeta, dgbeta, dkr)
    dproj = jnp.concatenate([dcq, dckv, dmisc, jnp.zeros((x.shape[0], 128), BF16), dmg, dgx, dgg], axis=1)
    dxn = _matmul(dproj, w_in_p, mode="nt", out_dtype=F32, name="d_xn", tm=512, tn=512)
    d_w_in_p = _matmul(xn, dproj, mode="tn", out_dtype=F32, name="d_w_in", tk=1024)
    grad_x, d_ng = _norm1_bwd(x, ng, dxn, dy)

    small = {"norm_gain": d_ng.sum(0), "mla_q_a_gain": d_qag.sum(0), "mla_kv_a_gain": d_kvag.sum(0),
             "mla_q_norm_gain": d_gq.sum(0)[:QK_DIM], "mla_k_norm_gain": d_gk.sum(0)[:QK_DIM],
             "gdn_a_log": d_alog.sum(0)[GA_LANE:GA_LANE + HEADS], "gdn_dt_bias": d_dtb.sum(0)[GA_LANE:GA_LANE + HEADS],
             "gdn_out_norm_gain": d_out_gain.sum(0)}
    big = {"w_in": d_w_in_p, "w_uq": d_w_uq_p, "w_ukv": d_w_ukv, "w_out": d_w_out, "gdn_conv_w": d_conv}
    return sq, grad_x, small, big


WEIGHTS = ["norm_gain", "w_in", "mla_q_a_gain", "mla_kv_a_gain", "w_uq", "w_ukv", "mla_q_norm_gain", "mla_k_norm_gain",
           "gdn_conv_w", "gdn_a_log", "gdn_dt_bias", "gdn_out_norm_gain", "w_out"]
BIG = ["w_in", "w_uq", "w_ukv", "w_out"]


def kernel(x, positions, norm_gain, w_in, mla_q_a_gain, mla_kv_a_gain, w_uq, w_ukv, mla_q_norm_gain, mla_k_norm_gain, gdn_conv_w, gdn_a_log, gdn_dt_bias, gdn_out_norm_gain, w_out, loss_target, m_norm_gain, m_w_in, m_mla_q_a_gain, m_mla_kv_a_gain, m_w_uq, m_w_ukv, m_mla_q_norm_gain, m_mla_k_norm_gain, m_gdn_conv_w, m_gdn_a_log, m_gdn_dt_bias, m_gdn_out_norm_gain, m_w_out, v_norm_gain, v_w_in, v_mla_q_a_gain, v_mla_kv_a_gain, v_w_uq, v_w_ukv, v_mla_q_norm_gain, v_mla_k_norm_gain, v_gdn_conv_w, v_gdn_a_log, v_gdn_dt_bias, v_gdn_out_norm_gain, v_w_out):
    w = dict(norm_gain=norm_gain, w_in=w_in, mla_q_a_gain=mla_q_a_gain, mla_kv_a_gain=mla_kv_a_gain, w_uq=w_uq,
             w_ukv=w_ukv, mla_q_norm_gain=mla_q_norm_gain, mla_k_norm_gain=mla_k_norm_gain, gdn_conv_w=gdn_conv_w,
             gdn_a_log=gdn_a_log, gdn_dt_bias=gdn_dt_bias, gdn_out_norm_gain=gdn_out_norm_gain, w_out=w_out)
    m = dict(norm_gain=m_norm_gain, w_in=m_w_in, mla_q_a_gain=m_mla_q_a_gain, mla_kv_a_gain=m_mla_kv_a_gain,
             w_uq=m_w_uq, w_ukv=m_w_ukv, mla_q_norm_gain=m_mla_q_norm_gain, mla_k_norm_gain=m_mla_k_norm_gain,
             gdn_conv_w=m_gdn_conv_w, gdn_a_log=m_gdn_a_log, gdn_dt_bias=m_gdn_dt_bias,
             gdn_out_norm_gain=m_gdn_out_norm_gain, w_out=m_w_out)
    v = dict(norm_gain=v_norm_gain, w_in=v_w_in, mla_q_a_gain=v_mla_q_a_gain, mla_kv_a_gain=v_mla_kv_a_gain,
             w_uq=v_w_uq, w_ukv=v_w_ukv, mla_q_norm_gain=v_mla_q_norm_gain, mla_k_norm_gain=v_mla_k_norm_gain,
             gdn_conv_w=v_gdn_conv_w, gdn_a_log=v_gdn_a_log, gdn_dt_bias=v_gdn_dt_bias,
             gdn_out_norm_gain=v_gdn_out_norm_gain, w_out=v_w_out)
    k_me = 2 * lax.axis_index("x") + lax.axis_index("y")

    g_in, g_uq, g_ukv, g_out, g_conv = _all_gather_chips(
        [w_in[0].astype(BF16), w_uq[0].astype(BF16), w_ukv[0].astype(BF16), w_out[0].astype(BF16), gdn_conv_w[0]])
    w_in_p = _pad_w_in(_blocks_to_cols(g_in))
    w_uq_p = _pad_heads(_blocks_to_cols(g_uq))
    w_ukv_f = _blocks_to_cols(g_ukv)
    w_out_f = g_out.reshape(D_MODEL, D_MODEL)
    conv_f = _blocks_to_cols(g_conv)

    sq, grad_x, small, big = _local_step(
        x[0], positions[0], loss_target[0], norm_gain, w_in_p, mla_q_a_gain, mla_kv_a_gain, w_uq_p, w_ukv_f,
        mla_q_norm_gain, mla_k_norm_gain, conv_f, gdn_a_log, gdn_dt_bias, gdn_out_norm_gain, w_out_f)

    loss_local = (0.5 / D_MODEL) * jnp.sum(sq)
    pack = jnp.concatenate([_pack_small(small, loss_local), big["gdn_conv_w"].reshape(96, 128)], axis=0)
    tot = _all_reduce_small(pack)
    loss = tot[LOSS_ROW, 0]
    conv_grad = lax.dynamic_slice_in_dim(tot[CONV_ROW:].reshape(4, 3072), k_me * 768, 768, axis=1)

    shard_grads = _reduce_scatter([
        _cols_to_blocks(_unpad_w_in(big["w_in"])), _cols_to_blocks(_unpad_heads(big["w_uq"])),
        _cols_to_blocks(big["w_ukv"]), big["w_out"].reshape(4, 512, D_MODEL)])

    grads = {n: _unpack_small(tot, n) for n in SMALL_ROWS}
    grads["gdn_conv_w"] = conv_grad[None]
    for n, g in zip(BIG, shard_grads):
        grads[n] = g[None]

    delta, new_m, new_v = {}, {}, {}
    sw = _pack_small({n: w[n] for n in SMALL_ROWS})
    sm = _pack_small({n: m[n] for n in SMALL_ROWS})
    sv = _pack_small({n: v[n] for n in SMALL_ROWS})
    sd, snm, snv = _adamw(sw, tot[:SMALL_PACK_ROWS], sm, sv, "adamw_small")
    for n in SMALL_ROWS:
        delta[n], new_m[n], new_v[n] = _unpack_small(sd, n), _unpack_small(snm, n), _unpack_small(snv, n)
    for n in BIG + ["gdn_conv_w"]:
        d, nm, nv = _adamw(w[n][0], grads[n][0], m[n][0], v[n][0], f"adamw_{n}")
        delta[n], new_m[n], new_v[n] = d[None], nm[None], nv[None]

    return (loss, grad_x[None], *[grads[n] for n in WEIGHTS], *[delta[n] for n in WEIGHTS],
            *[new_m[n] for n in WEIGHTS], *[new_v[n] for n in WEIGHTS])
```

```python
import functools
import math

import jax
import jax.numpy as jnp
from jax import lax
from jax.experimental import pallas as pl
from jax.experimental.pallas import tpu as pltpu

F32 = jnp.float32
BF16 = jnp.bfloat16
MESH = pl.DeviceIdType.MESH

D_MODEL = 2048
HEADS = 8
HEAD_DIM = 128
QK_DIM = 192
QK_PAD = 256
HALF_ROPE = 32
CHUNK = 64
NORM_EPS = 1e-6
W_IN_COLS = 5968
W_IN_PAD = 6144
GA_LANE = 64
GB_LANE = 72
ADAM_LR, ADAM_B1, ADAM_B2, ADAM_EPS, ADAM_WD, ADAM_STEP = 0.001, 0.9, 0.999, 1e-08, 0.01, 10
VMEM_LIMIT_V7X = 52 * 1024 * 1024
HI = lax.Precision.HIGHEST
NN = (((1,), (0,)), ((), ()))
NT = (((1,), (1,)), ((), ()))
TN = (((0,), (0,)), ((), ()))

TILES = {"row": 512, "attn": 1024, "mm": 1024}


def _call(body, *, name, grid, in_specs, out_specs, out_shape, args, scratch=(), sem=None):
    def kfn(*refs):
        body(*refs)
    if sem is None:
        sem = ("arbitrary",) * len(grid)
    return pl.pallas_call(
        kfn, name=name, grid=grid, in_specs=in_specs, out_specs=out_specs, out_shape=out_shape,
        scratch_shapes=list(scratch),
        compiler_params=pltpu.CompilerParams(dimension_semantics=sem, vmem_limit_bytes=VMEM_LIMIT_V7X),
    )(*args)


def _rows(tm, w, cb=0):
    return pl.BlockSpec((tm, w), lambda i: (i, cb))


def _full(shape):
    n = len(shape)
    return pl.BlockSpec(shape, lambda *_: (0,) * n)


def _sds(shape, dtype):
    return jax.ShapeDtypeStruct(shape, dtype)


def _acc8(x):
    tm, c = x.shape
    return jnp.sum(x.reshape(tm // 8, 8, c), axis=0)


def _sigmoid(x):
    return 1.0 / (1.0 + jnp.exp(-x))


def _silu(x):
    return x * _sigmoid(x)


def _dsilu(x):
    s = _sigmoid(x)
    return s * (1.0 + x * (1.0 - s))


def _dot(a, b, dims=NN):
    return lax.dot_general(a.astype(BF16), b.astype(BF16), dims, preferred_element_type=F32)


def _dot_hi(a, b, dims=NN):
    return lax.dot_general(a, b, dims, precision=HI, preferred_element_type=F32)


def _matmul(a, b, *, mode, out_dtype, name, tm=None, tn=None, tk=None):
    if mode == "tn":
        kdim, m = a.shape
    else:
        m, kdim = a.shape
    n = b.shape[0] if mode == "nt" else b.shape[1]
    tm = min(tm or TILES["mm"], m)
    tn = min(tn or TILES["mm"], n)
    tk = min(tk or kdim, kdim)
    nk = kdim // tk
    dims = {"nn": NN, "nt": NT, "tn": TN}[mode]
    if mode == "tn":
        a_spec = pl.BlockSpec((tk, tm), lambda i, j, k: (k, i))
    else:
        a_spec = pl.BlockSpec((tm, tk), lambda i, j, k: (i, k))
    if mode == "nt":
        b_spec = pl.BlockSpec((tn, tk), lambda i, j, k: (j, k))
    else:
        b_spec = pl.BlockSpec((tk, tn), lambda i, j, k: (k, j))

    def body(a_ref, b_ref, o_ref):
        r = _dot(a_ref[...], b_ref[...], dims)
        if nk == 1:
            o_ref[...] = r.astype(o_ref.dtype)
        else:
            k = pl.program_id(2)

            @pl.when(k == 0)
            def _():
                o_ref[...] = r

            @pl.when(k > 0)
            def _():
                o_ref[...] += r

    if nk > 1:
        assert out_dtype == F32
    return _call(body, name=name, grid=(m // tm, n // tn, nk), in_specs=[a_spec, b_spec],
                 out_specs=pl.BlockSpec((tm, tn), lambda i, j, k: (i, j)), out_shape=_sds((m, n), out_dtype),
                 args=(a, b))


def _norm1_fwd(x, gain):
    s = x.shape[0]
    tm = min(TILES["row"], s)

    def body(x_ref, g_ref, o_ref):
        xv = x_ref[...]
        r = lax.rsqrt(jnp.mean(xv * xv, axis=-1, keepdims=True) + NORM_EPS)
        o_ref[...] = (xv * r * g_ref[...]).astype(BF16)

    return _call(body, name="norm1_fwd", grid=(s // tm,), in_specs=[_rows(tm, D_MODEL), _full((1, D_MODEL))],
                 out_specs=_rows(tm, D_MODEL), out_shape=_sds((s, D_MODEL), BF16), args=(x, gain))


def _norm1_bwd(x, gain, dxn, dy):
    s = x.shape[0]
    tm = min(TILES["row"], s)

    def body(x_ref, g_ref, dxn_ref, dy_ref, gx_ref, dg_ref):
        xv = x_ref[...]
        r = lax.rsqrt(jnp.mean(xv * xv, axis=-1, keepdims=True) + NORM_EPS)
        nrm = xv * r
        d = dxn_ref[...]
        dn = d * g_ref[...]
        gx_ref[...] = dy_ref[...] + r * (dn - nrm * jnp.mean(dn * nrm, axis=-1, keepdims=True))

        @pl.when(pl.program_id(0) == 0)
        def _():
            dg_ref[...] = jnp.zeros_like(dg_ref)

        dg_ref[...] += _acc8(d * nrm)

    return _call(body, name="norm1_bwd", grid=(s // tm,),
                 in_specs=[_rows(tm, D_MODEL), _full((1, D_MODEL)), _rows(tm, D_MODEL), _rows(tm, D_MODEL)],
                 out_specs=[_rows(tm, D_MODEL), _full((8, D_MODEL))],
                 out_shape=[_sds((s, D_MODEL), F32), _sds((8, D_MODEL), F32)], args=(x, gain, dxn, dy))


def _rms(xv, width):
    return lax.rsqrt(jnp.sum(xv * xv, axis=-1, keepdims=True) * (1.0 / width) + NORM_EPS)


def _mla_a_norm(proj, gq, gkv):
    s = proj.shape[0]
    tm = min(TILES["row"], s)

    def body(cq_ref, ckv_ref, gq_ref, gkv_ref, oq_ref, okv_ref):
        a = cq_ref[...]
        oq_ref[...] = (a * _rms(a, 512) * gq_ref[...]).astype(BF16)
        b = ckv_ref[...]
        okv_ref[...] = (b * _rms(b, 256) * gkv_ref[...]).astype(BF16)

    return _call(body, name="mla_a_norm", grid=(s // tm,),
                 in_specs=[_rows(tm, 512, 0), _rows(tm, 256, 2), _full((1, 512)), _full((1, 256))],
                 out_specs=[_rows(tm, 512), _rows(tm, 256)],
                 out_shape=[_sds((s, 512), BF16), _sds((s, 256), BF16)], args=(proj, proj, gq, gkv))


def _rms_bwd(xv, gain, d, width):
    r = _rms(xv, width)
    nrm = xv * r
    dn = d * gain
    dx = r * (dn - nrm * (jnp.sum(dn * nrm, axis=-1, keepdims=True) * (1.0 / width)))
    return dx, d * nrm


def _mla_a_norm_bwd(proj, gq, gkv, dcqn, dckvn):
    s = proj.shape[0]
    tm = min(TILES["row"], s)

    def body(cq_ref, ckv_ref, gq_ref, gkv_ref, dq_ref, dkv_ref, oq_ref, okv_ref, aq_ref, akv_ref):
        dxq, gq_part = _rms_bwd(cq_ref[...], gq_ref[...], dq_ref[...], 512)
        dxk, gk_part = _rms_bwd(ckv_ref[...], gkv_ref[...], dkv_ref[...], 256)
        oq_ref[...] = dxq.astype(BF16)
        okv_ref[...] = dxk.astype(BF16)

        @pl.when(pl.program_id(0) == 0)
        def _():
            aq_ref[...] = jnp.zeros_like(aq_ref)
            akv_ref[...] = jnp.zeros_like(akv_ref)

        aq_ref[...] += _acc8(gq_part)
        akv_ref[...] += _acc8(gk_part)

    return _call(body, name="mla_a_norm_bwd", grid=(s // tm,),
                 in_specs=[_rows(tm, 512, 0), _rows(tm, 256, 2), _full((1, 512)), _full((1, 256)),
                           _rows(tm, 512), _rows(tm, 256)],
                 out_specs=[_rows(tm, 512), _rows(tm, 256), _full((8, 512)), _full((8, 256))],
                 out_shape=[_sds((s, 512), BF16), _sds((s, 256), BF16), _sds((8, 512), F32), _sds((8, 256), F32)],
                 args=(proj, proj, gq, gkv, dcqn, dckvn))


def _swap32(r):
    lane = lax.broadcasted_iota(jnp.int32, r.shape, 1)
    return jnp.where(lane < HALF_ROPE, pltpu.roll(r, 128 - HALF_ROPE, 1), pltpu.roll(r, HALF_ROPE, 1))


def _mla_post_fwd(q_pre, kv_pre, proj, cs, sn, gq, gk):
    s = q_pre.shape[0]
    tm = min(TILES["row"], s)

    def body(qp_ref, kvp_ref, misc_ref, cs_ref, sn_ref, gq_ref, gk_ref, q_ref, k_ref, v_ref):
        csv, snv = cs_ref[...], sn_ref[...]
        lane = lax.broadcasted_iota(jnp.int32, (tm, 128), 1)
        kr = jnp.where(lane < 64, misc_ref[...], 0.0)
        for h in range(HEADS):
            for src, g_ref, o_ref in ((None, gq_ref, q_ref), (kr, gk_ref, k_ref)):
                if src is None:
                    xv = qp_ref[:, QK_PAD * h:QK_PAD * (h + 1)]
                else:
                    xv = jnp.concatenate([kvp_ref[:, 256 * h:256 * h + 128], src], axis=-1)
                y = xv * _rms(xv, QK_DIM) * g_ref[...]
                hi = y[:, 128:]
                hi = hi * csv + _swap32(hi) * snv
                o_ref[:, QK_PAD * h:QK_PAD * h + 128] = y[:, :128].astype(BF16)
                o_ref[:, QK_PAD * h + 128:QK_PAD * (h + 1)] = hi.astype(BF16)
            v_ref[:, 128 * h:128 * (h + 1)] = kvp_ref[:, 256 * h + 128:256 * (h + 1)].astype(BF16)

    return _call(body, name="mla_post_fwd", grid=(s // tm,),
                 in_specs=[_rows(tm, 2048), _rows(tm, 2048), _rows(tm, 128, 6), _rows(tm, 128), _rows(tm, 128),
                           _full((1, QK_PAD)), _full((1, QK_PAD))],
                 out_specs=[_rows(tm, 2048), _rows(tm, 2048), _rows(tm, 1024)],
                 out_shape=[_sds((s, 2048), BF16), _sds((s, 2048), BF16), _sds((s, 1024), BF16)],
                 args=(q_pre, kv_pre, proj, cs, sn, gq, gk))


def _mla_post_bwd(q_pre, kv_pre, proj, cs, sn, gq, gk, dq, dk, dv):
    s = q_pre.shape[0]
    tm = min(TILES["row"], s)

    def body(qp_ref, kvp_ref, misc_ref, cs_ref, sn_ref, gq_ref, gk_ref, dq_ref, dk_ref, dv_ref,
             oq_ref, okv_ref, okr_ref, agq_ref, agk_ref):
        csv, snv = cs_ref[...], sn_ref[...]
        lane = lax.broadcasted_iota(jnp.int32, (tm, 128), 1)
        kr = jnp.where(lane < 64, misc_ref[...], 0.0)

        @pl.when(pl.program_id(0) == 0)
        def _():
            agq_ref[...] = jnp.zeros_like(agq_ref)
            agk_ref[...] = jnp.zeros_like(agk_ref)

        dkr = jnp.zeros((tm, 128), F32)
        for h in range(HEADS):
            for which in (0, 1):
                if which == 0:
                    xv = qp_ref[:, QK_PAD * h:QK_PAD * (h + 1)]
                    d_ref, g_ref, a_ref = dq_ref, gq_ref, agq_ref
                else:
                    xv = jnp.concatenate([kvp_ref[:, 256 * h:256 * h + 128], kr], axis=-1)
                    d_ref, g_ref, a_ref = dk_ref, gk_ref, agk_ref
                dhi = d_ref[:, QK_PAD * h + 128:QK_PAD * (h + 1)]
                dhi = dhi * csv - _swap32(dhi) * snv
                dyv = jnp.concatenate([d_ref[:, QK_PAD * h:QK_PAD * h + 128], dhi], axis=-1)
                dx, gpart = _rms_bwd(xv, g_ref[...], dyv, QK_DIM)
                a_ref[...] += _acc8(gpart)
                if which == 0:
                    oq_ref[:, QK_PAD * h:QK_PAD * (h + 1)] = dx.astype(BF16)
                else:
                    okv_ref[:, 256 * h:256 * h + 128] = dx[:, :128].astype(BF16)
                    dkr = dkr + dx[:, 128:]
            okv_ref[:, 256 * h + 128:256 * (h + 1)] = dv_ref[:, 128 * h:128 * (h + 1)].astype(BF16)
        okr_ref[...] = dkr

    return _call(body, name="mla_post_bwd", grid=(s // tm,),
                 in_specs=[_rows(tm, 2048), _rows(tm, 2048), _rows(tm, 128, 6), _rows(tm, 128), _rows(tm, 128),
                           _full((1, QK_PAD)), _full((1, QK_PAD)), _rows(tm, 2048), _rows(tm, 2048), _rows(tm, 1024)],
                 out_specs=[_rows(tm, 2048), _rows(tm, 2048), _rows(tm, 128), _full((8, QK_PAD)), _full((8, QK_PAD))],
                 out_shape=[_sds((s, 2048), BF16), _sds((s, 2048), BF16), _sds((s, 128), F32),
                            _sds((8, QK_PAD), F32), _sds((8, QK_PAD), F32)],
                 args=(q_pre, kv_pre, proj, cs, sn, gq, gk, dq, dk, dv))


ATTN_SCALE = QK_DIM ** -0.5
NEG = -1e30


def _causal_mask(sc, i, j, t):
    row = lax.broadcasted_iota(jnp.int32, (t, t), 0) + i * t
    col = lax.broadcasted_iota(jnp.int32, (t, t), 1) + j * t
    return jnp.where(col <= row, sc, NEG)


def _attn_fwd(q, k, v):
    s = q.shape[0]
    t = min(TILES["attn"], s)
    nq = s // t

    def body(q_ref, k_ref, v_ref, o_ref, lse_ref, m_s, l_s, acc_s):
        i, j = pl.program_id(1), pl.program_id(2)

        @pl.when(j == 0)
        def _():
            m_s[...] = jnp.full_like(m_s, NEG)
            l_s[...] = jnp.zeros_like(l_s)
            acc_s[...] = jnp.zeros_like(acc_s)

        @pl.when(j <= i)
        def _():
            sc = lax.dot_general(q_ref[...], k_ref[...], NT, preferred_element_type=F32) * ATTN_SCALE
            sc = _causal_mask(sc, i, j, t)
            m_prev = m_s[...]
            m_new = jnp.maximum(m_prev, jnp.max(sc, axis=-1, keepdims=True))
            p = jnp.exp(sc - m_new)
            alpha = jnp.exp(m_prev - m_new)
            l_s[...] = alpha * l_s[...] + jnp.sum(p, axis=-1, keepdims=True)
            acc_s[...] = acc_s[...] * alpha + lax.dot_general(p.astype(BF16), v_ref[...], NN,
                                                              preferred_element_type=F32)
            m_s[...] = m_new

        @pl.when(j == i)
        def _():
            o_ref[...] = acc_s[...] / l_s[...]
            lse_ref[...] = m_s[...] + jnp.log(l_s[...])

    return _call(
        body, name="attn_fwd", grid=(HEADS, nq, nq),
        in_specs=[pl.BlockSpec((t, QK_PAD), lambda h, i, j: (i, h)),
                  pl.BlockSpec((t, QK_PAD), lambda h, i, j: (jnp.minimum(i, j), h)),
                  pl.BlockSpec((t, HEAD_DIM), lambda h, i, j: (jnp.minimum(i, j), h))],
        out_specs=[pl.BlockSpec((t, HEAD_DIM), lambda h, i, j: (i, h)),
                   pl.BlockSpec((None, t, 1), lambda h, i, j: (h, i, 0))],
        out_shape=[_sds((s, HEADS * HEAD_DIM), F32), _sds((HEADS, s, 1), F32)],
        scratch=[pltpu.VMEM((t, 1), F32), pltpu.VMEM((t, 1), F32), pltpu.VMEM((t, HEAD_DIM), F32)],
        sem=("parallel", "parallel", "arbitrary"), args=(q, k, v))


def _attn_bwd(q, k, v, o, lse, do):
    s = q.shape[0]
    t = min(TILES["attn"], s)
    nq = s // t

    def body(q_ref, k_ref, v_ref, o_ref, lse_ref, do_ref, dq_ref, dk_ref, dv_ref):
        j, i = pl.program_id(1), pl.program_id(2)

        @pl.when((j == 0) & (i == 0))
        def _():
            dq_ref[...] = jnp.zeros_like(dq_ref)

        @pl.when(i == 0)
        def _():
            dk_ref[...] = jnp.zeros_like(dk_ref)
            dv_ref[...] = jnp.zeros_like(dv_ref)

        @pl.when(i >= j)
        def _():
            qv, kv = q_ref[...], k_ref[...]
            sc = lax.dot_general(qv, kv, NT, preferred_element_type=F32) * ATTN_SCALE
            sc = _causal_mask(sc, i, j, t)
            p = jnp.exp(sc - lse_ref[...])
            dof = do_ref[...]
            delta = jnp.sum(dof * o_ref[...], axis=-1, keepdims=True)
            dob = dof.astype(BF16)
            dv_ref[...] += lax.dot_general(p.astype(BF16), dob, TN, preferred_element_type=F32)
            dp = lax.dot_general(dob, v_ref[...], NT, preferred_element_type=F32)
            ds = (p * (dp - delta) * ATTN_SCALE).astype(BF16)
            dk_ref[...] += lax.dot_general(ds, qv, TN, preferred_element_type=F32)
            rows = pl.ds(pl.multiple_of(i * t, t), t)
            dq_ref[rows, :] += lax.dot_general(ds, kv, NN, preferred_element_type=F32)

    qi = lambda h, j, i: (jnp.maximum(i, j), h)
    return _call(
        body, name="attn_bwd", grid=(HEADS, nq, nq),
        in_specs=[pl.BlockSpec((t, QK_PAD), qi),
                  pl.BlockSpec((t, QK_PAD), lambda h, j, i: (j, h)),
                  pl.BlockSpec((t, HEAD_DIM), lambda h, j, i: (j, h)),
                  pl.BlockSpec((t, HEAD_DIM), qi),
                  pl.BlockSpec((None, t, 1), lambda h, j, i: (h, jnp.maximum(i, j), 0)),
                  pl.BlockSpec((t, HEAD_DIM), qi)],
        out_specs=[pl.BlockSpec((s, QK_PAD), lambda h, j, i: (0, h)),
                   pl.BlockSpec((t, QK_PAD), lambda h, j, i: (j, h)),
                   pl.BlockSpec((t, HEAD_DIM), lambda h, j, i: (j, h))],
        out_shape=[_sds((s, HEADS * QK_PAD), F32), _sds((s, HEADS * QK_PAD), F32), _sds((s, HEADS * HEAD_DIM), F32)],
        sem=("parallel", "arbitrary", "arbitrary"), args=(q, k, v, o, lse, do))


GDN_Q_SCALE = HEAD_DIM ** -0.5


def _shift_down(xv, prev8, sft):
    rolled = pltpu.roll(xv, sft, 0)
    top = pltpu.roll(jnp.concatenate([prev8, xv[:8]], axis=0), sft, 0)[8:]
    return jnp.concatenate([top, rolled[8:]], axis=0)


def _shift_up(xv, next8, sft):
    tm = xv.shape[0]
    rolled = pltpu.roll(xv, tm - sft, 0)
    bot = pltpu.roll(jnp.concatenate([xv[tm - 8:], next8], axis=0), 16 - sft, 0)[:8]
    return jnp.concatenate([rolled[:tm - 8], bot], axis=0)


def _conv_z(xv, prev8, w_ref):
    z = xv * w_ref[3:4, :]
    for sft in (1, 2, 3):
        z = z + _shift_down(xv, prev8, sft) * w_ref[3 - sft:4 - sft, :]
    return z


def _conv_specs(s, tm):
    nb8 = tm // 8
    cur = pl.BlockSpec((tm, 1024), lambda j, i: (i, 2 + j))
    prev = pl.BlockSpec((8, 1024), lambda j, i: (jnp.maximum(i * nb8 - 1, 0), 2 + j))
    return cur, prev


def _gdn_conv_fwd(proj, conv_w):
    s = proj.shape[0]
    tm = min(TILES["row"], s)
    cur, prev = _conv_specs(s, tm)

    def body(x_ref, xp_ref, w_ref, o_ref):
        j, i = pl.program_id(0), pl.program_id(1)
        prev8 = jnp.where(i > 0, xp_ref[...], 0.0)
        a = _silu(_conv_z(x_ref[...], prev8, w_ref))
        qk_scale = jnp.where(j == 0, GDN_Q_SCALE, 1.0)
        for h in range(HEADS):
            seg = a[:, 128 * h:128 * (h + 1)]
            r = lax.rsqrt(jnp.sum(seg * seg, axis=-1, keepdims=True) + NORM_EPS)
            o_ref[:, 128 * h:128 * (h + 1)] = jnp.where(j < 2, seg * r * qk_scale, seg)

    return _call(body, name="gdn_conv_fwd", grid=(3, s // tm),
                 in_specs=[cur, prev, pl.BlockSpec((4, 1024), lambda j, i: (0, j))],
                 out_specs=pl.BlockSpec((tm, 1024), lambda j, i: (i, j)), out_shape=_sds((s, 3072), F32),
                 args=(proj, proj, conv_w))


def _gdn_conv_bwd_a(proj, conv_w, dqkv):
    s = proj.shape[0]
    tm = min(TILES["row"], s)
    cur, prev = _conv_specs(s, tm)

    def body(x_ref, xp_ref, w_ref, d_ref, o_ref):
        j, i = pl.program_id(0), pl.program_id(1)
        prev8 = jnp.where(i > 0, xp_ref[...], 0.0)
        z = _conv_z(x_ref[...], prev8, w_ref)
        a = _silu(z)
        dsl = _dsilu(z)
        qk_scale = jnp.where(j == 0, GDN_Q_SCALE, 1.0)
        for h in range(HEADS):
            sl = slice(128 * h, 128 * (h + 1))
            seg = a[:, sl]
            dyv = d_ref[:, sl]
            r = lax.rsqrt(jnp.sum(seg * seg, axis=-1, keepdims=True) + NORM_EPS)
            yh = seg * r
            da_n = qk_scale * r * (dyv - yh * jnp.sum(yh * dyv, axis=-1, keepdims=True))
            o_ref[:, sl] = jnp.where(j < 2, da_n, dyv) * dsl[:, sl]

    return _call(body, name="gdn_conv_bwd_a", grid=(3, s // tm),
                 in_specs=[cur, prev, pl.BlockSpec((4, 1024), lambda j, i: (0, j)),
                           pl.BlockSpec((tm, 1024), lambda j, i: (i, j))],
                 out_specs=pl.BlockSpec((tm, 1024), lambda j, i: (i, j)), out_shape=_sds((s, 3072), F32),
                 args=(proj, proj, conv_w, dqkv))


def _gdn_conv_bwd_b(proj, conv_w, dz):
    s = proj.shape[0]
    tm = min(TILES["row"], s)
    nb8 = tm // 8
    last8 = s // 8 - 1
    cur, prev = _conv_specs(s, tm)

    def body(x_ref, xp_ref, w_ref, dz_ref, dzn_ref, dx_ref, dw_ref):
        i = pl.program_id(1)
        prev8 = jnp.where(i > 0, xp_ref[...], 0.0)
        next8 = jnp.where(i < pl.num_programs(1) - 1, dzn_ref[...], 0.0)
        xv, dzv = x_ref[...], dz_ref[...]

        @pl.when(i == 0)
        def _():
            dw_ref[...] = jnp.zeros_like(dw_ref)

        dx = dzv * w_ref[3:4, :]
        dw_ref[3:4, :] += jnp.sum(dzv * xv, axis=0, keepdims=True)
        for sft in (1, 2, 3):
            dx = dx + _shift_up(dzv, next8, sft) * w_ref[3 - sft:4 - sft, :]
            dw_ref[3 - sft:4 - sft, :] += jnp.sum(dzv * _shift_down(xv, prev8, sft), axis=0, keepdims=True)
        dx_ref[...] = dx.astype(BF16)

    return _call(body, name="gdn_conv_bwd_b", grid=(3, s // tm),
                 in_specs=[cur, prev, pl.BlockSpec((4, 1024), lambda j, i: (0, j)),
                           pl.BlockSpec((tm, 1024), lambda j, i: (i, j)),
                           pl.BlockSpec((8, 1024), lambda j, i: (jnp.minimum((i + 1) * nb8, last8), j))],
                 out_specs=[pl.BlockSpec((tm, 1024), lambda j, i: (i, j)), pl.BlockSpec((4, 1024), lambda j, i: (0, j))],
                 out_shape=[_sds((s, 3072), BF16), _sds((4, 3072), F32)], args=(proj, proj, conv_w, dz, dz))


def _softplus(xv):
    return jnp.maximum(xv, 0.0) + jnp.log(1.0 + jnp.exp(-jnp.abs(xv)))


def _gdn_gates_fwd(proj, alog128, dtb128):
    s = proj.shape[0]
    tm = min(TILES["row"], s)

    def body(m_ref, a_ref, b_ref, o_ref):
        mv = m_ref[...]
        lane = lax.broadcasted_iota(jnp.int32, mv.shape, 1)
        g = -jnp.exp(a_ref[...]) * _softplus(mv + b_ref[...])
        is_g = (lane >= GA_LANE) & (lane < GA_LANE + HEADS)
        is_b = (lane >= GB_LANE) & (lane < GB_LANE + HEADS)
        o_ref[...] = jnp.where(is_g, g, jnp.where(is_b, _sigmoid(mv), 0.0))

    return _call(body, name="gdn_gates_fwd", grid=(s // tm,),
                 in_specs=[_rows(tm, 128, 6), _full((1, 128)), _full((1, 128))],
                 out_specs=_rows(tm, 128), out_shape=_sds((s, 128), F32), args=(proj, alog128, dtb128))


def _gdn_gates_bwd(proj, alog128, dtb128, gbeta, dgbeta, dkr):
    s = proj.shape[0]
    tm = min(TILES["row"], s)

    def body(m_ref, a_ref, b_ref, gb_ref, d_ref, kr_ref, o_ref, da_ref, db_ref):
        mv, dv = m_ref[...], d_ref[...]
        lane = lax.broadcasted_iota(jnp.int32, mv.shape, 1)
        is_g = (lane >= GA_LANE) & (lane < GA_LANE + HEADS)
        is_b = (lane >= GB_LANE) & (lane < GB_LANE + HEADS)
        dga = jnp.where(is_g, dv * (-jnp.exp(a_ref[...])) * _sigmoid(mv + b_ref[...]), 0.0)
        beta = gb_ref[...]
        dgb = jnp.where(is_b, dv * beta * (1.0 - beta), 0.0)
        o_ref[...] = jnp.where(lane < 64, kr_ref[...], dga + dgb).astype(BF16)

        @pl.when(pl.program_id(0) == 0)
        def _():
            da_ref[...] = jnp.zeros_like(da_ref)
            db_ref[...] = jnp.zeros_like(db_ref)

        da_ref[...] += _acc8(jnp.where(is_g, dv * gb_ref[...], 0.0))
        db_ref[...] += _acc8(dga)

    return _call(body, name="gdn_gates_bwd", grid=(s // tm,),
                 in_specs=[_rows(tm, 128, 6), _full((1, 128)), _full((1, 128)), _rows(tm, 128), _rows(tm, 128),
                           _rows(tm, 128)],
                 out_specs=[_rows(tm, 128), _full((8, 128)), _full((8, 128))],
                 out_shape=[_sds((s, 128), BF16), _sds((8, 128), F32), _sds((8, 128), F32)],
                 args=(proj, alog128, dtb128, gbeta, dgbeta, dkr))


def _col(mat, lane_idx, lane):
    return jnp.sum(jnp.where(lane == lane_idx, mat, 0.0), axis=-1, keepdims=True)


def _chunk_local(qh, kh, vh, gcol, bcol, ii, jj):
    lower, strict, eye = ii >= jj, ii > jj, ii == jj
    grow = jnp.sum(jnp.where(eye, gcol, 0.0), axis=0, keepdims=True)
    decay = jnp.where(lower, jnp.exp(jnp.where(lower, gcol - grow, 0.0)), 0.0)
    kb = kh * bcol
    vb = vh * bcol
    mm = _dot(kb, kh, NT)
    lmat = jnp.where(strict, mm * decay, 0.0)
    pw = -lmat
    tinv = jnp.where(eye, 1.0, 0.0) + pw
    for _ in range(5):
        pw = _dot_hi(pw, pw)
        tinv = tinv + _dot_hi(tinv, pw)
    egc = jnp.exp(gcol)
    kbg = kb * egc
    rhs = jnp.concatenate([vb, kbg], axis=-1)
    sol = _dot_hi(tinv, rhs)
    qk = _dot(qh, kh, NT)
    glast = jnp.sum(jnp.where(ii[:, :1] == CHUNK - 1, gcol, 0.0), axis=0, keepdims=True)
    ekd = jnp.exp(glast - gcol)
    return dict(decay=decay, kb=kb, vb=vb, mm=mm, lmat=lmat, tinv=tinv, egc=egc, kbg=kbg, rhs=rhs,
                u=sol[:, :HEAD_DIM], w=sol[:, HEAD_DIM:], qk=qk, amat=qk * decay, qd=qh * egc, ekd=ekd,
                kd=kh * ekd, gl=jnp.exp(glast), strict=strict, lower=lower, eye=eye)


def _tri(ii, jj):
    return jnp.where(ii >= jj, 1.0, 0.0)


def _gdn_fwd(qkv, gbeta):
    s = qkv.shape[0]
    n = s // CHUNK

    def body(qkv_ref, gb_ref, o_ref, st_ref, state):
        @pl.when(pl.program_id(0) == 0)
        def _():
            state[...] = jnp.zeros_like(state)

        ii = lax.broadcasted_iota(jnp.int32, (CHUNK, CHUNK), 0)
        jj = lax.broadcasted_iota(jnp.int32, (CHUNK, CHUNK), 1)
        lane = lax.broadcasted_iota(jnp.int32, (CHUNK, 128), 1)
        gbv = gb_ref[...]
        gc = _dot_hi(_tri(ii, jj), gbv)
        for h in range(HEADS):
            sl = slice(128 * h, 128 * (h + 1))
            qh = qkv_ref[:, 128 * h:128 * (h + 1)]
            kh = qkv_ref[:, 1024 + 128 * h:1024 + 128 * (h + 1)]
            vh = qkv_ref[:, 2048 + 128 * h:2048 + 128 * (h + 1)]
            c = _chunk_local(qh, kh, vh, _col(gc, GA_LANE + h, lane), _col(gbv, GB_LANE + h, lane), ii, jj)
            st = state[sl, :]
            st_ref[sl, :] = st
            vn = c["u"] - _dot(c["w"], st)
            o_ref[:, sl] = _dot(c["qd"], st) + _dot(c["amat"], vn)
            state[sl, :] = st * c["gl"] + _dot(c["kd"], vn, TN)

    return _call(body, name="gdn_fwd", grid=(n,),
                 in_specs=[_rows(CHUNK, 3072), _rows(CHUNK, 128)],
                 out_specs=[_rows(CHUNK, 1024), _rows(HEADS * 128, 128)],
                 out_shape=[_sds((s, 1024), F32), _sds((n * HEADS * 128, 128), F32)],
                 scratch=[pltpu.VMEM((HEADS * 128, 128), F32)], args=(qkv, gbeta))


def _gdn_bwd(qkv, gbeta, states, do):
    s = qkv.shape[0]
    n = s // CHUNK

    def body(qkv_ref, gb_ref, st_ref, do_ref, dqkv_ref, dgb_ref, dstate):
        @pl.when(pl.program_id(0) == 0)
        def _():
            dstate[...] = jnp.zeros_like(dstate)

        ii = lax.broadcasted_iota(jnp.int32, (CHUNK, CHUNK), 0)
        jj = lax.broadcasted_iota(jnp.int32, (CHUNK, CHUNK), 1)
        lane = lax.broadcasted_iota(jnp.int32, (CHUNK, 128), 1)
        row1 = ii[:, :1]
        gbv = gb_ref[...]
        gc = _dot_hi(_tri(ii, jj), gbv)
        dgc_all = jnp.zeros((CHUNK, 128), F32)
        db_all = jnp.zeros((CHUNK, 128), F32)
        for h in range(HEADS):
            sl = slice(128 * h, 128 * (h + 1))
            qh = qkv_ref[:, 128 * h:128 * (h + 1)]
            kh = qkv_ref[:, 1024 + 128 * h:1024 + 128 * (h + 1)]
            vh = qkv_ref[:, 2048 + 128 * h:2048 + 128 * (h + 1)]
            bcol = _col(gbv, GB_LANE + h, lane)
            c = _chunk_local(qh, kh, vh, _col(gc, GA_LANE + h, lane), bcol, ii, jj)
            st = st_ref[sl, :]
            dst = dstate[sl, :]
            dov = do_ref[:, sl]
            vn = c["u"] - _dot(c["w"], st)
            dvn = _dot(c["amat"], dov, TN) + _dot(c["kd"], dst)
            damat = jnp.where(c["lower"], _dot(dov, vn, NT), 0.0)
            dqd = _dot(dov, st, NT)
            dkd = _dot(vn, dst, NT)
            dw = -_dot(dvn, st, NT)
            dgl = jnp.sum(jnp.sum(st * dst, axis=-1, keepdims=True), axis=0, keepdims=True)
            dstate[sl, :] = _dot(c["qd"], dov, TN) + c["gl"] * dst - _dot(c["w"], dvn, TN)
            dsol = jnp.concatenate([dvn, dw], axis=-1)
            drhs = _dot_hi(c["tinv"], dsol, TN)
            dtinv = _dot_hi(dsol, c["rhs"], NT)
            dl = -_dot_hi(_dot_hi(c["tinv"], dtinv, TN), c["tinv"], NT)
            dl = jnp.where(c["strict"], dl, 0.0)
            dmm = dl * c["decay"]
            dqk = damat * c["decay"]
            wmat = dl * c["lmat"] + damat * c["amat"]
            dgc = jnp.sum(wmat, axis=-1, keepdims=True)
            wcol = jnp.sum(wmat, axis=0, keepdims=True)
            dgc = dgc - jnp.sum(jnp.where(c["eye"], wcol, 0.0), axis=-1, keepdims=True)
            dkb = _dot(dmm, kh) + drhs[:, HEAD_DIM:] * c["egc"]
            dk = _dot(dmm, c["kb"], TN) + _dot(dqk, qh, TN) + dkd * c["ekd"]
            dq = _dot(dqk, kh) + dqd * c["egc"]
            dgc = dgc + jnp.sum(drhs[:, HEAD_DIM:] * c["kbg"], axis=-1, keepdims=True)
            dgc = dgc + jnp.sum(dqd * c["qd"], axis=-1, keepdims=True)
            tmp = jnp.sum(dkd * c["kd"], axis=-1, keepdims=True)
            dgc = dgc - tmp
            dglast = jnp.sum(tmp, axis=0, keepdims=True) + dgl * c["gl"]
            dgc = dgc + jnp.where(row1 == CHUNK - 1, dglast, 0.0)
            dk = dk + dkb * bcol
            db = jnp.sum(dkb * kh, axis=-1, keepdims=True) + jnp.sum(drhs[:, :HEAD_DIM] * vh, axis=-1, keepdims=True)
            dqkv_ref[:, 128 * h:128 * (h + 1)] = dq
            dqkv_ref[:, 1024 + 128 * h:1024 + 128 * (h + 1)] = dk
            dqkv_ref[:, 2048 + 128 * h:2048 + 128 * (h + 1)] = drhs[:, :HEAD_DIM] * bcol
            dgc_all = dgc_all + jnp.where(lane == GA_LANE + h, dgc, 0.0)
            db_all = db_all + jnp.where(lane == GB_LANE + h, db, 0.0)
        dgb_ref[...] = _dot_hi(_tri(jj, ii), dgc_all) + db_all

    rev = lambda w: pl.BlockSpec((CHUNK, w), lambda i: (n - 1 - i, 0))
    return _call(body, name="gdn_bwd", grid=(n,),
                 in_specs=[rev(3072), rev(128), pl.BlockSpec((HEADS * 128, 128), lambda i: (n - 1 - i, 0)), rev(1024)],
                 out_specs=[rev(3072), rev(128)],
                 out_shape=[_sds((s, 3072), F32), _sds((s, 128), F32)],
                 scratch=[pltpu.VMEM((HEADS * 128, 128), F32)], args=(qkv, gbeta, states, do))


NN_B = (((2,), (1,)), ((0,), (0,)))
NT_B = (((2,), (2,)), ((0,), (0,)))
TN_B = (((1,), (1,)), ((0,), (0,)))
GDN_PAR_CHUNKS = 2
GDN_SEQ_CHUNKS = 4


def _gather_heads(qkv_ref, gc, gbv, qs, ks, vs, gs, bs, nchunks):
    lane = lax.broadcasted_iota(jnp.int32, (CHUNK, 128), 1)
    for c in range(nchunks):
        rows = slice(CHUNK * c, CHUNK * (c + 1))
        for h in range(HEADS):
            b = HEADS * c + h
            qs[b] = qkv_ref[rows, 128 * h:128 * (h + 1)]
            ks[b] = qkv_ref[rows, 1024 + 128 * h:1024 + 128 * (h + 1)]
            vs[b] = qkv_ref[rows, 2048 + 128 * h:2048 + 128 * (h + 1)]
            gs[b] = jnp.broadcast_to(_col(gc[rows], GA_LANE + h, lane), (CHUNK, 128))
            bs[b] = jnp.broadcast_to(_col(gbv[rows], GB_LANE + h, lane), (CHUNK, 128))


def _block_tri(rows, transpose=False):
    ri = lax.broadcasted_iota(jnp.int32, (rows, rows), 0)
    ci = lax.broadcasted_iota(jnp.int32, (rows, rows), 1)
    same = (ri >> 6) == (ci >> 6)
    return jnp.where(same & ((ci >= ri) if transpose else (ri >= ci)), 1.0, 0.0)


def _local_b(q, k, v, g128, b128):
    ii = lax.broadcasted_iota(jnp.int32, (1, CHUNK, CHUNK), 1)
    jj = lax.broadcasted_iota(jnp.int32, (1, CHUNK, CHUNK), 2)
    lower, strict, eye = ii >= jj, ii > jj, ii == jj
    g64 = g128[:, :, :CHUNK]
    grow = jnp.sum(jnp.where(eye, g64, 0.0), axis=1, keepdims=True)
    decay = jnp.where(lower, jnp.exp(jnp.where(lower, g64 - grow, 0.0)), 0.0)
    kb = k * b128
    vb = v * b128
    mm = lax.dot_general(kb.astype(BF16), k.astype(BF16), NT_B, preferred_element_type=F32)
    lmat = jnp.where(strict, mm * decay, 0.0)
    egc = jnp.exp(g128)
    kbg = kb * egc
    qk = lax.dot_general(q.astype(BF16), k.astype(BF16), NT_B, preferred_element_type=F32)
    row = lax.broadcasted_iota(jnp.int32, (1, CHUNK, 128), 1)
    glast = jnp.sum(jnp.where(row == CHUNK - 1, g128, 0.0), axis=1, keepdims=True)
    ekd = jnp.exp(glast - g128)
    return dict(decay=decay, kb=kb, vb=vb, lmat=lmat, egc=egc, kbg=kbg, amat=qk * decay, qd=q * egc, ekd=ekd,
                kd=k * ekd, gl=jnp.exp(glast), lower=lower, strict=strict, eye=eye)


def _bdot(a, b, dims):
    return lax.dot_general(a.astype(BF16), b.astype(BF16), dims, preferred_element_type=F32)


def _bdot_hi(a, b, dims):
    return lax.dot_general(a, b, dims, precision=HI, preferred_element_type=F32)


def _gdn_pre(qkv, gbeta):
    s = qkv.shape[0]
    n = s // CHUNK
    cb = min(GDN_PAR_CHUNKS, n)
    nb = cb * HEADS
    rows = cb * CHUNK

    def body(qkv_ref, gb_ref, u_ref, w_ref, qd_ref, kd_ref, a_ref, t_ref, gl_ref, qs, ks, vs, gs, bs):
        gbv = gb_ref[...]
        gc = _dot_hi(_block_tri(rows), gbv)
        _gather_heads(qkv_ref, gc, gbv, qs, ks, vs, gs, bs, cb)
        c = _local_b(qs[...], ks[...], vs[...], gs[...], bs[...])
        pw = -c["lmat"]
        tinv = jnp.where(c["eye"], 1.0, 0.0) + pw
        for _ in range(5):
            pw = _bdot_hi(pw, pw, NN_B)
            tinv = tinv + _bdot_hi(tinv, pw, NN_B)
        u_ref[...] = _bdot_hi(tinv, c["vb"], NN_B)
        w_ref[...] = _bdot_hi(tinv, c["kbg"], NN_B).astype(BF16)
        qd_ref[...] = c["qd"].astype(BF16)
        kd_ref[...] = c["kd"].astype(BF16)
        a_ref[...] = c["amat"].astype(BF16)
        t_ref[...] = tinv
        gl_ref[...] = c["gl"]

    b3 = lambda d: pl.BlockSpec((nb, CHUNK, d), lambda i: (i, 0, 0))
    nt = n * HEADS
    return _call(body, name="gdn_pre", grid=(n // cb,),
                 in_specs=[_rows(rows, 3072), _rows(rows, 128)],
                 out_specs=[b3(128), b3(128), b3(128), b3(128), b3(CHUNK), b3(CHUNK),
                            pl.BlockSpec((nb, 1, 128), lambda i: (i, 0, 0))],
                 out_shape=[_sds((nt, CHUNK, 128), F32), _sds((nt, CHUNK, 128), BF16), _sds((nt, CHUNK, 128), BF16),
                            _sds((nt, CHUNK, 128), BF16), _sds((nt, CHUNK, CHUNK), BF16), _sds((nt, CHUNK, CHUNK), F32),
                            _sds((nt, 1, 128), F32)],
                 scratch=[pltpu.VMEM((nb, CHUNK, 128), F32)] * 5, sem=("parallel",), args=(qkv, gbeta))


def _gdn_scan_fwd(u, w, qd, kd, amat, gl):
    nt = u.shape[0]
    n = nt // HEADS
    cs = min(GDN_SEQ_CHUNKS, n)

    def body(u_ref, w_ref, qd_ref, kd_ref, a_ref, gl_ref, o_ref, st_ref, state):
        @pl.when(pl.program_id(0) == 0)
        def _():
            state[...] = jnp.zeros_like(state)

        for c in range(cs):
            sl = slice(HEADS * c, HEADS * (c + 1))
            st = state[...]
            st_ref[sl] = st
            stb = st.astype(BF16)
            vn = u_ref[sl] - lax.dot_general(w_ref[sl], stb, NN_B, preferred_element_type=F32)
            vnb = vn.astype(BF16)
            o = (lax.dot_general(qd_ref[sl], stb, NN_B, preferred_element_type=F32)
                 + lax.dot_general(a_ref[sl], vnb, NN_B, preferred_element_type=F32))
            state[...] = st * gl_ref[sl] + lax.dot_general(kd_ref[sl], vnb, TN_B, preferred_element_type=F32)
            for h in range(HEADS):
                o_ref[CHUNK * c:CHUNK * (c + 1), 128 * h:128 * (h + 1)] = o[h]

    b3 = lambda d: pl.BlockSpec((cs * HEADS, CHUNK, d), lambda i: (i, 0, 0))
    return _call(body, name="gdn_scan_fwd", grid=(n // cs,),
                 in_specs=[b3(128), b3(128), b3(128), b3(128), b3(CHUNK), pl.BlockSpec((cs * HEADS, 1, 128), lambda i: (i, 0, 0))],
                 out_specs=[_rows(cs * CHUNK, 1024), pl.BlockSpec((cs * HEADS, 128, 128), lambda i: (i, 0, 0))],
                 out_shape=[_sds((n * CHUNK, 1024), F32), _sds((nt, 128, 128), F32)],
                 scratch=[pltpu.VMEM((HEADS, 128, 128), F32)], args=(u, w, qd, kd, amat, gl))


def _gdn_scan_bwd(w, qd, kd, amat, gl, do):
    nt = w.shape[0]
    n = nt // HEADS
    cs = min(GDN_SEQ_CHUNKS, n)
    ng = n // cs

    def body(w_ref, qd_ref, kd_ref, a_ref, gl_ref, do_ref, ds_ref, dstate, dos):
        @pl.when(pl.program_id(0) == 0)
        def _():
            dstate[...] = jnp.zeros_like(dstate)

        for c in reversed(range(cs)):
            sl = slice(HEADS * c, HEADS * (c + 1))
            for h in range(HEADS):
                dos[h] = do_ref[CHUNK * c:CHUNK * (c + 1), 128 * h:128 * (h + 1)].astype(BF16)
            dob = dos[...]
            dst = dstate[...]
            ds_ref[sl] = dst
            dvn = (lax.dot_general(a_ref[sl], dob, TN_B, preferred_element_type=F32)
                   + lax.dot_general(kd_ref[sl], dst.astype(BF16), NN_B, preferred_element_type=F32))
            dstate[...] = (lax.dot_general(qd_ref[sl], dob, TN_B, preferred_element_type=F32) + gl_ref[sl] * dst
                           - lax.dot_general(w_ref[sl], dvn.astype(BF16), TN_B, preferred_element_type=F32))

    b3 = lambda d: pl.BlockSpec((cs * HEADS, CHUNK, d), lambda i: (ng - 1 - i, 0, 0))
    return _call(body, name="gdn_scan_bwd", grid=(ng,),
                 in_specs=[b3(128), b3(128), b3(128), b3(CHUNK), pl.BlockSpec((cs * HEADS, 1, 128), lambda i: (ng - 1 - i, 0, 0)),
                           pl.BlockSpec((cs * CHUNK, 1024), lambda i: (ng - 1 - i, 0))],
                 out_specs=pl.BlockSpec((cs * HEADS, 128, 128), lambda i: (ng - 1 - i, 0, 0)),
                 out_shape=_sds((nt, 128, 128), F32),
                 scratch=[pltpu.VMEM((HEADS, 128, 128), F32), pltpu.VMEM((HEADS, CHUNK, 128), BF16)],
                 args=(w, qd, kd, amat, gl, do))


def _gdn_post_bwd(qkv, gbeta, u, w, tinv, states, dstates, do):
    s = qkv.shape[0]
    n = s // CHUNK
    cb = min(GDN_PAR_CHUNKS, n)
    nb = cb * HEADS
    rows = cb * CHUNK

    def body(qkv_ref, gb_ref, u_ref, w_ref, t_ref, st_ref, ds_ref, do_ref, dqkv_ref, dgb_ref, qs, ks, vs, gs, bs, dos):
        gbv = gb_ref[...]
        gc = _dot_hi(_block_tri(rows), gbv)
        _gather_heads(qkv_ref, gc, gbv, qs, ks, vs, gs, bs, cb)
        for c in range(cb):
            for h in range(HEADS):
                dos[HEADS * c + h] = do_ref[CHUNK * c:CHUNK * (c + 1), 128 * h:128 * (h + 1)]
        q, k, v, b128 = qs[...], ks[...], vs[...], bs[...]
        c = _local_b(q, k, v, gs[...], b128)
        tinv, st, dst, dov = t_ref[...], st_ref[...], ds_ref[...], dos[...]
        wv = w_ref[...]
        vn = u_ref[...] - _bdot(wv, st, NN_B)
        dvn = _bdot(c["amat"], dov, TN_B) + _bdot(c["kd"], dst, NN_B)
        damat = jnp.where(c["lower"], _bdot(dov, vn, NT_B), 0.0)
        dqd = _bdot(dov, st, NT_B)
        dkd = _bdot(vn, dst, NT_B)
        dw = -_bdot(dvn, st, NT_B)
        dgl = jnp.sum(jnp.sum(st * dst, axis=1, keepdims=True), axis=-1, keepdims=True)
        dvb = _bdot_hi(tinv, dvn, TN_B)
        dkbg = _bdot_hi(tinv, dw, TN_B)
        dtinv = _bdot_hi(dvn, c["vb"], NT_B) + _bdot_hi(dw, c["kbg"], NT_B)
        dl = -_bdot_hi(_bdot_hi(tinv, dtinv, TN_B), tinv, NT_B)
        dl = jnp.where(c["strict"], dl, 0.0)
        dmm = dl * c["decay"]
        dqk = damat * c["decay"]
        wmat = dl * c["lmat"] + damat * c["amat"]
        wcol = jnp.sum(wmat, axis=1, keepdims=True)
        dgc = jnp.sum(wmat, axis=-1, keepdims=True) - jnp.sum(jnp.where(c["eye"], wcol, 0.0), axis=-1, keepdims=True)
        dkb = _bdot(dmm, k, NN_B) + dkbg * c["egc"]
        dk = _bdot(dmm, c["kb"], TN_B) + _bdot(dqk, q, TN_B) + dkd * c["ekd"] + dkb * b128
        dq = _bdot(dqk, k, NN_B) + dqd * c["egc"]
        tmp = jnp.sum(dkd * c["kd"], axis=-1, keepdims=True)
        dgc = (dgc + jnp.sum(dkbg * c["kbg"], axis=-1, keepdims=True) + jnp.sum(dqd * c["qd"], axis=-1, keepdims=True)
               - tmp)
        dglast = jnp.sum(tmp, axis=1, keepdims=True) + dgl * c["gl"][:, :, :1]
        row1 = lax.broadcasted_iota(jnp.int32, (1, CHUNK, 1), 1)
        dgc = dgc + jnp.where(row1 == CHUNK - 1, dglast, 0.0)
        db = jnp.sum(dkb * k, axis=-1, keepdims=True) + jnp.sum(dvb * v, axis=-1, keepdims=True)
        dv = dvb * b128
        lane = lax.broadcasted_iota(jnp.int32, (CHUNK, 128), 1)
        parts = []
        for cc in range(cb):
            acc = jnp.zeros((CHUNK, 128), F32)
            for h in range(HEADS):
                bi = HEADS * cc + h
                rs = slice(CHUNK * cc, CHUNK * (cc + 1))
                dqkv_ref[rs, 128 * h:128 * (h + 1)] = dq[bi]
                dqkv_ref[rs, 1024 + 128 * h:1024 + 128 * (h + 1)] = dk[bi]
                dqkv_ref[rs, 2048 + 128 * h:2048 + 128 * (h + 1)] = dv[bi]
                acc = acc + jnp.where(lane == GA_LANE + h, dgc[bi], 0.0)
            parts.append(acc)
        dgc_all = jnp.concatenate(parts, axis=0)
        dg_all = _dot_hi(_block_tri(rows, transpose=True), dgc_all)
        for cc in range(cb):
            acc = dg_all[CHUNK * cc:CHUNK * (cc + 1)]
            for h in range(HEADS):
                acc = acc + jnp.where(lane == GB_LANE + h, db[HEADS * cc + h], 0.0)
            dgb_ref[CHUNK * cc:CHUNK * (cc + 1), :] = acc

    b3 = lambda d1, d2: pl.BlockSpec((nb, d1, d2), lambda i: (i, 0, 0))
    return _call(body, name="gdn_post_bwd", grid=(n // cb,),
                 in_specs=[_rows(rows, 3072), _rows(rows, 128), b3(CHUNK, 128), b3(CHUNK, 128), b3(CHUNK, CHUNK),
                           b3(128, 128), b3(128, 128), _rows(rows, 1024)],
                 out_specs=[_rows(rows, 3072), _rows(rows, 128)],
                 out_shape=[_sds((s, 3072), F32), _sds((s, 128), F32)],
                 scratch=[pltpu.VMEM((nb, CHUNK, 128), F32)] * 6, sem=("parallel",),
                 args=(qkv, gbeta, u, w, tinv, states, dstates, do))


def _mix_fwd(o_mla, o_gdn, proj, out_gain):
    s = proj.shape[0]
    tm = min(TILES["row"], s)

    def body(om_ref, og_ref, mg_ref, gg_ref, g_ref, o_ref):
        o_ref[:, :1024] = (om_ref[...] * _silu(mg_ref[...])).astype(BF16)
        for h in range(HEADS):
            sl = slice(128 * h, 128 * (h + 1))
            og = og_ref[:, sl]
            on = og * _rms(og, HEAD_DIM) * g_ref[...]
            o_ref[:, 1024 + 128 * h:1024 + 128 * (h + 1)] = (on * _silu(gg_ref[:, sl])).astype(BF16)

    return _call(body, name="mix_fwd", grid=(s // tm,),
                 in_specs=[_rows(tm, 1024), _rows(tm, 1024), _rows(tm, 1024, 1), _rows(tm, 1024, 5), _full((1, 128))],
                 out_specs=_rows(tm, 2048), out_shape=_sds((s, 2048), BF16), args=(o_mla, o_gdn, proj, proj, out_gain))


def _mix_bwd(o_mla, o_gdn, proj, out_gain, dmixed):
    s = proj.shape[0]
    tm = min(TILES["row"], s)

    def body(om_ref, og_ref, mg_ref, gg_ref, g_ref, dm_ref, dg_ref, dom_ref, dog_ref, dmg_ref, dgg_ref, ag_ref):
        @pl.when(pl.program_id(0) == 0)
        def _():
            ag_ref[...] = jnp.zeros_like(ag_ref)

        mg = mg_ref[...]
        dom_ref[...] = dm_ref[...] * _silu(mg)
        dmg_ref[...] = (dm_ref[...] * om_ref[...] * _dsilu(mg)).astype(BF16)
        for h in range(HEADS):
            sl = slice(128 * h, 128 * (h + 1))
            og, gg, d = og_ref[:, sl], gg_ref[:, sl], dg_ref[:, sl]
            on = og * _rms(og, HEAD_DIM) * g_ref[...]
            dgg_ref[:, sl] = (d * on * _dsilu(gg)).astype(BF16)
            dx, gpart = _rms_bwd(og, g_ref[...], d * _silu(gg), HEAD_DIM)
            dog_ref[:, sl] = dx
            ag_ref[...] += _acc8(gpart)

    return _call(body, name="mix_bwd", grid=(s // tm,),
                 in_specs=[_rows(tm, 1024), _rows(tm, 1024), _rows(tm, 1024, 1), _rows(tm, 1024, 5), _full((1, 128)),
                           _rows(tm, 1024, 0), _rows(tm, 1024, 1)],
                 out_specs=[_rows(tm, 1024), _rows(tm, 1024), _rows(tm, 1024), _rows(tm, 1024), _full((8, 128))],
                 out_shape=[_sds((s, 1024), F32), _sds((s, 1024), F32), _sds((s, 1024), BF16), _sds((s, 1024), BF16),
                            _sds((8, 128), F32)],
                 args=(o_mla, o_gdn, proj, proj, out_gain, dmixed, dmixed))


def _out_fwd(mixed, w_out, x, target):
    s = x.shape[0]
    tm = min(TILES["mm"], s)
    tn = min(TILES["mm"], D_MODEL)

    def body(m_ref, w_ref, x_ref, t_ref, dy_ref, acc_ref):
        err = x_ref[...] + _dot(m_ref[...], w_ref[...]) - t_ref[...]
        dy_ref[...] = err * (1.0 / D_MODEL)

        @pl.when(pl.program_id(1) == 0)
        def _():
            acc_ref[...] = jnp.zeros_like(acc_ref)

        acc_ref[...] += _acc8(err * err)

    return _call(body, name="out_fwd", grid=(D_MODEL // tn, s // tm),
                 in_specs=[pl.BlockSpec((tm, D_MODEL), lambda j, i: (i, 0)), pl.BlockSpec((D_MODEL, tn), lambda j, i: (0, j)),
                           pl.BlockSpec((tm, tn), lambda j, i: (i, j)), pl.BlockSpec((tm, tn), lambda j, i: (i, j))],
                 out_specs=[pl.BlockSpec((tm, tn), lambda j, i: (i, j)), pl.BlockSpec((8, tn), lambda j, i: (0, j))],
                 out_shape=[_sds((s, D_MODEL), F32), _sds((8, D_MODEL), F32)], args=(mixed, w_out, x, target))


def _row_tile(r, c):
    if r % 8 != 0:
        return r
    t = 8
    while r % (2 * t) == 0 and 2 * t * c * 4 <= (1 << 20):
        t *= 2
    return t


def _sum_arrays(parts, name):
    r, c = parts[0].shape
    tr = _row_tile(r, c)

    def body(*refs):
        acc = refs[0][...]
        for p_ref in refs[1:-1]:
            acc = acc + p_ref[...]
        refs[-1][...] = acc

    return _call(body, name=name, grid=(r // tr,), in_specs=[_rows(tr, c)] * len(parts), out_specs=_rows(tr, c),
                 out_shape=_sds((r, c), F32), args=tuple(parts))


def _adamw(w, g, m, v, name):
    r, c = w.shape
    tr = _row_tile(r, c)
    c1 = 1.0 - ADAM_B1 ** ADAM_STEP
    c2 = 1.0 - ADAM_B2 ** ADAM_STEP

    def body(w_ref, g_ref, m_ref, v_ref, d_ref, nm_ref, nv_ref):
        gv = g_ref[...]
        nm = ADAM_B1 * m_ref[...] + (1.0 - ADAM_B1) * gv
        nv = ADAM_B2 * v_ref[...] + (1.0 - ADAM_B2) * (gv * gv)
        nm_ref[...] = nm
        nv_ref[...] = nv
        d_ref[...] = -ADAM_LR * ((nm / c1) / (jnp.sqrt(nv / c2) + ADAM_EPS) + ADAM_WD * w_ref[...])

    return _call(body, name=name, grid=(r // tr,), in_specs=[_rows(tr, c)] * 4, out_specs=[_rows(tr, c)] * 3,
                 out_shape=[_sds((r, c), F32)] * 3, args=(w, g, m, v))


ANY = pl.BlockSpec(memory_space=pl.ANY)
CHIP_FLIPS = ((1, 0), (0, 1), (1, 1))


def _comm_call(body, *, name, n_in, out_shape, scratch):
    def kfn(*refs):
        body(*refs)
    return pl.pallas_call(kfn, name=name, in_specs=[ANY] * n_in, out_specs=[ANY] * len(out_shape), out_shape=out_shape,
                          scratch_shapes=list(scratch),
                          compiler_params=pltpu.CompilerParams(has_side_effects=True))


def _all_gather_chips(shards):
    na = len(shards)

    def body(*refs):
        ins, outs = refs[:na], refs[na:2 * na]
        send_sems, recv_sems, local_sems = refs[2 * na:]
        x, y, c = lax.axis_index("x"), lax.axis_index("y"), lax.axis_index("c")
        my_k = 2 * x + y
        copies, local = [], []
        for a in range(na):
            cp = pltpu.make_async_copy(ins[a], outs[a].at[my_k], local_sems.at[a])
            cp.start()
            local.append(cp)
            for r, (fx, fy) in enumerate(CHIP_FLIPS):
                rc = pltpu.make_async_remote_copy(
                    src_ref=ins[a], dst_ref=outs[a].at[my_k], send_sem=send_sems.at[a, r], recv_sem=recv_sems.at[a, r],
                    device_id=(x ^ fx, y ^ fy, c), device_id_type=MESH)
                rc.start()
                copies.append(rc)
        for rc in copies:
            rc.wait()
        for cp in local:
            cp.wait()

    out_shape = [_sds((4,) + a.shape, a.dtype) for a in shards]
    return _comm_call(body, name="all_gather_weights", n_in=na, out_shape=out_shape,
                      scratch=[pltpu.SemaphoreType.DMA((na, 3)), pltpu.SemaphoreType.DMA((na, 3)),
                               pltpu.SemaphoreType.DMA((na,))])(*shards)


def _all_reduce_small(vec):
    r = vec.shape[0]

    def body(v_ref, o_ref, gath, send_sems, recv_sems):
        x, y, c = lax.axis_index("x"), lax.axis_index("y"), lax.axis_index("c")
        me = 4 * x + 2 * y + c
        gath[me] = v_ref[...]
        copies = []
        for rel in range(1, 8):
            fx, fy, fc = (rel >> 2) & 1, (rel >> 1) & 1, rel & 1
            rc = pltpu.make_async_remote_copy(
                src_ref=v_ref, dst_ref=gath.at[me], send_sem=send_sems.at[rel - 1], recv_sem=recv_sems.at[rel - 1],
                device_id=(x ^ fx, y ^ fy, c ^ fc), device_id_type=MESH)
            rc.start()
            copies.append(rc)
        for rc in copies:
            rc.wait()
        acc = gath[0]
        for d in range(1, 8):
            acc = acc + gath[d]
        o_ref[...] = acc

    def kfn(*refs):
        body(*refs)
    vm = pl.BlockSpec(memory_space=pltpu.VMEM)
    return pl.pallas_call(kfn, name="all_reduce_small", in_specs=[vm], out_specs=vm, out_shape=_sds((r, 128), F32),
                          scratch_shapes=[pltpu.VMEM((8, r, 128), F32), pltpu.SemaphoreType.DMA((7,)),
                                          pltpu.SemaphoreType.DMA((7,))],
                          compiler_params=pltpu.CompilerParams(has_side_effects=True))(vec)


def _exchange_halves(arrs):
    na = len(arrs)

    def body(*refs):
        ins, outs = refs[:na], refs[na:2 * na]
        send_sems, recv_sems = refs[2 * na:]
        x, y, c = lax.axis_index("x"), lax.axis_index("y"), lax.axis_index("c")
        copies = []
        for a in range(na):
            half = ins[a].shape[1] // 2
            src = ins[a].at[:, pl.ds(pl.multiple_of((1 - c) * half, 8), half), :]
            rc = pltpu.make_async_remote_copy(src_ref=src, dst_ref=outs[a], send_sem=send_sems.at[a],
                                              recv_sem=recv_sems.at[a], device_id=(x, y, 1 - c), device_id_type=MESH)
            rc.start()
            copies.append(rc)
        for rc in copies:
            rc.wait()

    out_shape = [_sds((4, a.shape[1] // 2, a.shape[2]), F32) for a in arrs]
    return _comm_call(body, name="rs_pair_exchange", n_in=na, out_shape=out_shape,
                      scratch=[pltpu.SemaphoreType.DMA((na,)), pltpu.SemaphoreType.DMA((na,))])(*arrs)


def _scatter_to_chips(arrs):
    na = len(arrs)

    def body(*refs):
        ins, outs = refs[:na], refs[na:2 * na]
        send_sems, recv_sems = refs[2 * na:]
        x, y, c = lax.axis_index("x"), lax.axis_index("y"), lax.axis_index("c")
        copies = []
        for a in range(na):
            for r, (fx, fy) in enumerate(CHIP_FLIPS):
                px, py = x ^ fx, y ^ fy
                rc = pltpu.make_async_remote_copy(
                    src_ref=ins[a].at[2 * px + py], dst_ref=outs[a].at[r], send_sem=send_sems.at[a, r],
                    recv_sem=recv_sems.at[a, r], device_id=(px, py, c), device_id_type=MESH)
                rc.start()
                copies.append(rc)
        for rc in copies:
            rc.wait()

    out_shape = [_sds((3,) + a.shape[1:], F32) for a in arrs]
    return _comm_call(body, name="rs_chip_scatter", n_in=na, out_shape=out_shape,
                      scratch=[pltpu.SemaphoreType.DMA((na, 3)), pltpu.SemaphoreType.DMA((na, 3))])(*arrs)


def _join_halves(arrs):
    na = len(arrs)

    def body(*refs):
        ins, outs = refs[:na], refs[na:2 * na]
        send_sems, recv_sems, local_sems = refs[2 * na:]
        x, y, c = lax.axis_index("x"), lax.axis_index("y"), lax.axis_index("c")
        copies = []
        for a in range(na):
            half = ins[a].shape[0]
            dst = outs[a].at[pl.ds(pl.multiple_of(c * half, 8), half), :]
            cp = pltpu.make_async_copy(ins[a], dst, local_sems.at[a])
            cp.start()
            rc = pltpu.make_async_remote_copy(src_ref=ins[a], dst_ref=dst, send_sem=send_sems.at[a],
                                              recv_sem=recv_sems.at[a], device_id=(x, y, 1 - c), device_id_type=MESH)
            rc.start()
            copies += [cp, rc]
        for cp in copies:
            cp.wait()

    out_shape = [_sds((2 * a.shape[0], a.shape[1]), F32) for a in arrs]
    return _comm_call(body, name="rs_pair_join", n_in=na, out_shape=out_shape,
                      scratch=[pltpu.SemaphoreType.DMA((na,)), pltpu.SemaphoreType.DMA((na,)),
                               pltpu.SemaphoreType.DMA((na,))])(*arrs)


def _reduce_scatter(grads):
    c = lax.axis_index("c")
    k_me = 2 * lax.axis_index("x") + lax.axis_index("y")
    got = _exchange_halves(grads)
    pair = []
    for a, (g, o) in enumerate(zip(grads, got)):
        half = g.shape[1] // 2
        mine = lax.dynamic_slice_in_dim(g, c * half, half, axis=1)
        pair.append(_sum_arrays([mine.reshape(4 * half, -1), o.reshape(4 * half, -1)], f"rs_pair_sum_{a}")
                    .reshape(4, half, -1))
    recv = _scatter_to_chips(pair)
    halves = []
    for a, (p, rv) in enumerate(zip(pair, recv)):
        own = lax.dynamic_index_in_dim(p, k_me, 0, keepdims=False)
        halves.append(_sum_arrays([own, rv[0], rv[1], rv[2]], f"rs_chip_sum_{a}"))
    return _join_halves(halves)


def _pad_w_in(w):
    z = jnp.zeros((w.shape[0], 1024 - 848), w.dtype)
    return jnp.concatenate([w[:, 0:832], w[:, 4928:4944], z, w[:, 832:4928], w[:, 4944:5968]], axis=1)


def _unpad_w_in(g):
    return jnp.concatenate([g[:, 0:832], g[:, 1024:5120], g[:, 832:848], g[:, 5120:6144]], axis=1)


def _pad_heads(w):
    r = w.shape[0]
    return jnp.pad(w.reshape(r, HEADS, QK_DIM), ((0, 0), (0, 0), (0, QK_PAD - QK_DIM))).reshape(r, HEADS * QK_PAD)


def _unpad_heads(w):
    r = w.shape[0]
    return w.reshape(r, HEADS, QK_PAD)[:, :, :QK_DIM].reshape(r, HEADS * QK_DIM)


def _cols_to_blocks(w):
    r = w.shape[0]
    return w.reshape(r, 4, -1).transpose(1, 0, 2)


def _blocks_to_cols(w):
    return w.transpose(1, 0, 2).reshape(w.shape[1], -1)


SMALL_ROWS = {"norm_gain": (0, 2048), "mla_q_a_gain": (16, 512), "mla_kv_a_gain": (20, 256),
              "mla_q_norm_gain": (22, 192), "mla_k_norm_gain": (24, 192), "gdn_a_log": (26, 8),
              "gdn_dt_bias": (27, 8), "gdn_out_norm_gain": (28, 128)}
LOSS_ROW = 29
SMALL_PACK_ROWS = 32
CONV_ROW = 32


def _pack_small(vals, loss=None):
    rows = []
    at = 0
    for name, (row, size) in SMALL_ROWS.items():
        assert row == at
        nr = -(-size // 128)
        rows.append(jnp.pad(vals[name].reshape(-1).astype(F32), (0, nr * 128 - size)).reshape(nr, 128))
        at += nr
    assert at == LOSS_ROW
    if loss is not None:
        rows.append(jnp.pad(loss.reshape(1, 1), ((0, 0), (0, 127))))
        at += 1
    rows.append(jnp.zeros((SMALL_PACK_ROWS - at, 128), F32))
    return jnp.concatenate(rows, axis=0)


def _unpack_small(pack, name):
    row, size = SMALL_ROWS[name]
    nr = -(-size // 128)
    return pack[row:row + nr].reshape(-1)[:size].reshape(1, size)


def _local_step(x, positions, target, norm_gain, w_in_p, q_a_gain, kv_a_gain, w_uq_p, w_ukv, q_norm_gain,
                k_norm_gain, conv_w, a_log, dt_bias, out_gain, w_out):
    half = HALF_ROPE
    inv_freq = jnp.power(10000.0, -jnp.arange(half, dtype=F32) / half)
    ang = positions.astype(F32)[:, None] * inv_freq
    cos, sin = jnp.cos(ang), jnp.sin(ang)
    zpad = jnp.zeros((x.shape[0], 64), F32)
    cs = jnp.concatenate([cos, cos, zpad], axis=1)
    sn = jnp.concatenate([-sin, sin, zpad], axis=1)
    gq = jnp.pad(q_norm_gain.reshape(1, QK_DIM), ((0, 0), (0, QK_PAD - QK_DIM)))
    gk = jnp.pad(k_norm_gain.reshape(1, QK_DIM), ((0, 0), (0, QK_PAD - QK_DIM)))
    lane_pad = ((0, 0), (GA_LANE, 128 - GA_LANE - HEADS))
    alog128 = jnp.pad(a_log.reshape(1, HEADS), lane_pad)
    dtb128 = jnp.pad(dt_bias.reshape(1, HEADS), lane_pad)
    ng, qag, kvag, og = (norm_gain.reshape(1, -1), q_a_gain.reshape(1, -1), kv_a_gain.reshape(1, -1),
                         out_gain.reshape(1, -1))

    xn = _norm1_fwd(x, ng)
    proj = _matmul(xn, w_in_p, mode="nn", out_dtype=F32, name="in_proj")
    cqn, ckvn = _mla_a_norm(proj, qag, kvag)
    q_pre = _matmul(cqn, w_uq_p, mode="nn", out_dtype=F32, name="q_up")
    kv_pre = _matmul(ckvn, w_ukv, mode="nn", out_dtype=F32, name="kv_up")
    q, k, v = _mla_post_fwd(q_pre, kv_pre, proj, cs, sn, gq, gk)
    o_mla, lse = _attn_fwd(q, k, v)
    qkv = _gdn_conv_fwd(proj, conv_w)
    gbeta = _gdn_gates_fwd(proj, alog128, dtb128)
    g_u, g_w, g_qd, g_kd, g_a, g_t, g_gl = _gdn_pre(qkv, gbeta)
    o_gdn, states = _gdn_scan_fwd(g_u, g_w, g_qd, g_kd, g_a, g_gl)
    mixed = _mix_fwd(o_mla, o_gdn, proj, og)
    dy, sq = _out_fwd(mixed, w_out, x, target)

    dmixed = _matmul(dy, w_out, mode="nt", out_dtype=F32, name="d_mixed")
    d_w_out = _matmul(mixed, dy, mode="tn", out_dtype=F32, name="d_w_out", tk=1024)
    do_mla, do_gdn, dmg, dgg, d_out_gain = _mix_bwd(o_mla, o_gdn, proj, og, dmixed)
    dq, dk, dv = _attn_bwd(q, k, v, o_mla, lse, do_mla)
    dq_pre, dkv_pre, dkr, d_gq, d_gk = _mla_post_bwd(q_pre, kv_pre, proj, cs, sn, gq, gk, dq, dk, dv)
    d_w_uq_p = _matmul(cqn, dq_pre, mode="tn", out_dtype=F32, name="d_w_uq", tk=1024)
    d_w_ukv = _matmul(ckvn, dkv_pre, mode="tn", out_dtype=F32, name="d_w_ukv", tk=1024)
    dcqn = _matmul(dq_pre, w_uq_p, mode="nt", out_dtype=F32, name="d_cqn")
    dckvn = _matmul(dkv_pre, w_ukv, mode="nt", out_dtype=F32, name="d_ckvn")
    dcq, dckv, d_qag, d_kvag = _mla_a_norm_bwd(proj, qag, kvag, dcqn, dckvn)
    dstates = _gdn_scan_bwd(g_w, g_qd, g_kd, g_a, g_gl, do_gdn)
    dqkv, dgbeta = _gdn_post_bwd(qkv, gbeta, g_u, g_w, g_t, states, dstates, do_gdn)
    dz = _gdn_conv_bwd_a(proj, conv_w, dqkv)
    dgx, d_conv = _gdn_conv_bwd_b(proj, conv_w, dz)
    dmisc, d_alog, d_dtb = _gdn_gates_bwd(proj, alog128, dtb128, gbeta, dgbeta, dkr)
    dproj = jnp.concatenate([dcq, dckv, dmisc, jnp.zeros((x.shape[0], 128), BF16), dmg, dgx, dgg], axis=1)
    dxn = _matmul(dproj, w_in_p, mode="nt", out_dtype=F32, name="d_xn", tm=512, tn=512)
    d_w_in_p = _matmul(xn, dproj, mode="tn", out_dtype=F32, name="d_w_in", tk=1024)
    grad_x, d_ng = _norm1_bwd(x, ng, dxn, dy)

    small = {"norm_gain": d_ng.sum(0), "mla_q_a_gain": d_qag.sum(0), "mla_kv_a_gain": d_kvag.sum(0),
             "mla_q_norm_gain": d_gq.sum(0)[:QK_DIM], "mla_k_norm_gain": d_gk.sum(0)[:QK_DIM],
             "gdn_a_log": d_alog.sum(0)[GA_LANE:GA_LANE + HEADS], "gdn_dt_bias": d_dtb.sum(0)[GA_LANE:GA_LANE + HEADS],
             "gdn_out_norm_gain": d_out_gain.sum(0)}
    big = {"w_in": d_w_in_p, "w_uq": d_w_uq_p, "w_ukv": d_w_ukv, "w_out": d_w_out, "gdn_conv_w": d_conv}
    return sq, grad_x, small, big


WEIGHTS = ["norm_gain", "w_in", "mla_q_a_gain", "mla_kv_a_gain", "w_uq", "w_ukv", "mla_q_norm_gain", "mla_k_norm_gain",
           "gdn_conv_w", "gdn_a_log", "gdn_dt_bias", "gdn_out_norm_gain", "w_out"]
BIG = ["w_in", "w_uq", "w_ukv", "w_out"]


def kernel(x, positions, norm_gain, w_in, mla_q_a_gain, mla_kv_a_gain, w_uq, w_ukv, mla_q_norm_gain, mla_k_norm_gain, gdn_conv_w, gdn_a_log, gdn_dt_bias, gdn_out_norm_gain, w_out, loss_target, m_norm_gain, m_w_in, m_mla_q_a_gain, m_mla_kv_a_gain, m_w_uq, m_w_ukv, m_mla_q_norm_gain, m_mla_k_norm_gain, m_gdn_conv_w, m_gdn_a_log, m_gdn_dt_bias, m_gdn_out_norm_gain, m_w_out, v_norm_gain, v_w_in, v_mla_q_a_gain, v_mla_kv_a_gain, v_w_uq, v_w_ukv, v_mla_q_norm_gain, v_mla_k_norm_gain, v_gdn_conv_w, v_gdn_a_log, v_gdn_dt_bias, v_gdn_out_norm_gain, v_w_out):
    w = dict(norm_gain=norm_gain, w_in=w_in, mla_q_a_gain=mla_q_a_gain, mla_kv_a_gain=mla_kv_a_gain, w_uq=w_uq,
             w_ukv=w_ukv, mla_q_norm_gain=mla_q_norm_gain, mla_k_norm_gain=mla_k_norm_gain, gdn_conv_w=gdn_conv_w,
             gdn_a_log=gdn_a_log, gdn_dt_bias=gdn_dt_bias, gdn_out_norm_gain=gdn_out_norm_gain, w_out=w_out)
    m = dict(norm_gain=m_norm_gain, w_in=m_w_in, mla_q_a_gain=m_mla_q_a_gain, mla_kv_a_gain=m_mla_kv_a_gain,
             w_uq=m_w_uq, w_ukv=m_w_ukv, mla_q_norm_gain=m_mla_q_norm_gain, mla_k_norm_gain=m_mla_k_norm_gain,
             gdn_conv_w=m_gdn_conv_w, gdn_a_log=m_gdn_a_log, gdn_dt_bias=m_gdn_dt_bias,
             gdn_out_norm_gain=m_gdn_out_norm_gain, w_out=m_w_out)
    v = dict(norm_gain=v_norm_gain, w_in=v_w_in, mla_q_a_gain=v_mla_q_a_gain, mla_kv_a_gain=v_mla_kv_a_gain,
             w_uq=v_w_uq, w_ukv=v_w_ukv, mla_q_norm_gain=v_mla_q_norm_gain, mla_k_norm_gain=v_mla_k_norm_gain,
             gdn_conv_w=v_gdn_conv_w, gdn_a_log=v_gdn_a_log, gdn_dt_bias=v_gdn_dt_bias,
             gdn_out_norm_gain=v_gdn_out_norm_gain, w_out=v_w_out)
    k_me = 2 * lax.axis_index("x") + lax.axis_index("y")

    g_in, g_uq, g_ukv, g_out, g_conv = _all_gather_chips(
        [w_in[0].astype(BF16), w_uq[0].astype(BF16), w_ukv[0].astype(BF16), w_out[0].astype(BF16), gdn_conv_w[0]])
    w_in_p = _pad_w_in(_blocks_to_cols(g_in))
    w_uq_p = _pad_heads(_blocks_to_cols(g_uq))
    w_ukv_f = _blocks_to_cols(g_ukv)
    w_out_f = g_out.reshape(D_MODEL, D_MODEL)
    conv_f = _blocks_to_cols(g_conv)

    sq, grad_x, small, big = _local_step(
        x[0], positions[0], loss_target[0], norm_gain, w_in_p, mla_q_a_gain, mla_kv_a_gain, w_uq_p, w_ukv_f,
        mla_q_norm_gain, mla_k_norm_gain, conv_f, gdn_a_log, gdn_dt_bias, gdn_out_norm_gain, w_out_f)

    loss_local = (0.5 / D_MODEL) * jnp.sum(sq)
    pack = jnp.concatenate([_pack_small(small, loss_local), big["gdn_conv_w"].reshape(96, 128)], axis=0)
    tot = _all_reduce_small(pack)
    loss = tot[LOSS_ROW, 0]
    conv_grad = lax.dynamic_slice_in_dim(tot[CONV_ROW:].reshape(4, 3072), k_me * 768, 768, axis=1)

    shard_grads = _reduce_scatter([
        _cols_to_blocks(_unpad_w_in(big["w_in"])), _cols_to_blocks(_unpad_heads(big["w_uq"])),
        _cols_to_blocks(big["w_ukv"]), big["w_out"].reshape(4, 512, D_MODEL)])

    grads = {n: _unpack_small(tot, n) for n in SMALL_ROWS}
    grads["gdn_conv_w"] = conv_grad[None]
    for n, g in zip(BIG, shard_grads):
        grads[n] = g[None]

    delta, new_m, new_v = {}, {}, {}
    sw = _pack_small({n: w[n] for n in SMALL_ROWS})
    sm = _pack_small({n: m[n] for n in SMALL_ROWS})
    sv = _pack_small({n: v[n] for n in SMALL_ROWS})
    sd, snm, snv = _adamw(sw, tot[:SMALL_PACK_ROWS], sm, sv, "adamw_small")
    for n in SMALL_ROWS:
        delta[n], new_m[n], new_v[n] = _unpack_small(sd, n), _unpack_small(snm, n), _unpack_small(snv, n)
    for n in BIG + ["gdn_conv_w"]:
        d, nm, nv = _adamw(w[n][0], grads[n][0], m[n][0], v[n][0], f"adamw_{n}")
        delta[n], new_m[n], new_v[n] = d[None], nm[None], nv[None]

    return (loss, grad_x[None], *[grads[n] for n in WEIGHTS], *[delta[n] for n in WEIGHTS],
            *[new_m[n] for n in WEIGHTS], *[new_v[n] for n in WEIGHTS])
```

```python
import functools
import math

import jax
import jax.numpy as jnp
from jax import lax
from jax.experimental import pallas as pl
from jax.experimental.pallas import tpu as pltpu

F32 = jnp.float32
BF16 = jnp.bfloat16
MESH = pl.DeviceIdType.MESH

D_MODEL = 2048
HEADS = 8
HEAD_DIM = 128
QK_DIM = 192
QK_PAD = 256
HALF_ROPE = 32
CHUNK = 64
NORM_EPS = 1e-6
W_IN_COLS = 5968
W_IN_PAD = 6144
GA_LANE = 64
GB_LANE = 72
ADAM_LR, ADAM_B1, ADAM_B2, ADAM_EPS, ADAM_WD, ADAM_STEP = 0.001, 0.9, 0.999, 1e-08, 0.01, 10
VMEM_LIMIT_V7X = 52 * 1024 * 1024
HI = lax.Precision.HIGHEST
NN = (((1,), (0,)), ((), ()))
NT = (((1,), (1,)), ((), ()))
TN = (((0,), (0,)), ((), ()))

TILES = {"row": 512, "attn": 1024, "mm": 1024}


def _call(body, *, name, grid, in_specs, out_specs, out_shape, args, scratch=(), sem=None):
    def kfn(*refs):
        body(*refs)
    if sem is None:
        sem = ("arbitrary",) * len(grid)
    return pl.pallas_call(
        kfn, name=name, grid=grid, in_specs=in_specs, out_specs=out_specs, out_shape=out_shape,
        scratch_shapes=list(scratch),
        compiler_params=pltpu.CompilerParams(dimension_semantics=sem, vmem_limit_bytes=VMEM_LIMIT_V7X),
    )(*args)


def _rows(tm, w, cb=0):
    return pl.BlockSpec((tm, w), lambda i: (i, cb))


def _full(shape):
    n = len(shape)
    return pl.BlockSpec(shape, lambda *_: (0,) * n)


def _sds(shape, dtype):
    return jax.ShapeDtypeStruct(shape, dtype)


def _acc8(x):
    tm, c = x.shape
    return jnp.sum(x.reshape(tm // 8, 8, c), axis=0)


def _sigmoid(x):
    return 1.0 / (1.0 + jnp.exp(-x))


def _silu(x):
    return x * _sigmoid(x)


def _dsilu(x):
    s = _sigmoid(x)
    return s * (1.0 + x * (1.0 - s))


def _dot(a, b, dims=NN):
    return lax.dot_general(a.astype(BF16), b.astype(BF16), dims, preferred_element_type=F32)


def _dot_hi(a, b, dims=NN):
    return lax.dot_general(a, b, dims, precision=HI, preferred_element_type=F32)


def _matmul(a, b, *, mode, out_dtype, name, tm=None, tn=None, tk=None):
    if mode == "tn":
        kdim, m = a.shape
    else:
        m, kdim = a.shape
    n = b.shape[0] if mode == "nt" else b.shape[1]
    tm = min(tm or TILES["mm"], m)
    tn = min(tn or TILES["mm"], n)
    tk = min(tk or kdim, kdim)
    nk = kdim // tk
    dims = {"nn": NN, "nt": NT, "tn": TN}[mode]
    if mode == "tn":
        a_spec = pl.BlockSpec((tk, tm), lambda i, j, k: (k, i))
    else:
        a_spec = pl.BlockSpec((tm, tk), lambda i, j, k: (i, k))
    if mode == "nt":
        b_spec = pl.BlockSpec((tn, tk), lambda i, j, k: (j, k))
    else:
        b_spec = pl.BlockSpec((tk, tn), lambda i, j, k: (k, j))

    def body(a_ref, b_ref, o_ref):
        r = _dot(a_ref[...], b_ref[...], dims)
        if nk == 1:
            o_ref[...] = r.astype(o_ref.dtype)
        else:
            k = pl.program_id(2)

            @pl.when(k == 0)
            def _():
                o_ref[...] = r

            @pl.when(k > 0)
            def _():
                o_ref[...] += r

    if nk > 1:
        assert out_dtype == F32
    return _call(body, name=name, grid=(m // tm, n // tn, nk), in_specs=[a_spec, b_spec],
                 out_specs=pl.BlockSpec((tm, tn), lambda i, j, k: (i, j)), out_shape=_sds((m, n), out_dtype),
                 args=(a, b))


def _norm1_fwd(x, gain):
    s = x.shape[0]
    tm = min(TILES["row"], s)

    def body(x_ref, g_ref, o_ref):
        xv = x_ref[...]
        r = lax.rsqrt(jnp.mean(xv * xv, axis=-1, keepdims=True) + NORM_EPS)
        o_ref[...] = (xv * r * g_ref[...]).astype(BF16)

    return _call(body, name="norm1_fwd", grid=(s // tm,), in_specs=[_rows(tm, D_MODEL), _full((1, D_MODEL))],
                 out_specs=_rows(tm, D_MODEL), out_shape=_sds((s, D_MODEL), BF16), args=(x, gain))


def _norm1_bwd(x, gain, dxn, dy):
    s = x.shape[0]
    tm = min(TILES["row"], s)

    def body(x_ref, g_ref, dxn_ref, dy_ref, gx_ref, dg_ref):
        xv = x_ref[...]
        r = lax.rsqrt(jnp.mean(xv * xv, axis=-1, keepdims=True) + NORM_EPS)
        nrm = xv * r
        d = dxn_ref[...]
        dn = d * g_ref[...]
        gx_ref[...] = dy_ref[...] + r * (dn - nrm * jnp.mean(dn * nrm, axis=-1, keepdims=True))

        @pl.when(pl.program_id(0) == 0)
        def _():
            dg_ref[...] = jnp.zeros_like(dg_ref)

        dg_ref[...] += _acc8(d * nrm)

    return _call(body, name="norm1_bwd", grid=(s // tm,),
                 in_specs=[_rows(tm, D_MODEL), _full((1, D_MODEL)), _rows(tm, D_MODEL), _rows(tm, D_MODEL)],
                 out_specs=[_rows(tm, D_MODEL), _full((8, D_MODEL))],
                 out_shape=[_sds((s, D_MODEL), F32), _sds((8, D_MODEL), F32)], args=(x, gain, dxn, dy))


def _rms(xv, width):
    return lax.rsqrt(jnp.sum(xv * xv, axis=-1, keepdims=True) * (1.0 / width) + NORM_EPS)


def _mla_a_norm(proj, gq, gkv):
    s = proj.shape[0]
    tm = min(TILES["row"], s)

    def body(cq_ref, ckv_ref, gq_ref, gkv_ref, oq_ref, okv_ref):
        a = cq_ref[...]
        oq_ref[...] = (a * _rms(a, 512) * gq_ref[...]).astype(BF16)
        b = ckv_ref[...]
        okv_ref[...] = (b * _rms(b, 256) * gkv_ref[...]).astype(BF16)

    return _call(body, name="mla_a_norm", grid=(s // tm,),
                 in_specs=[_rows(tm, 512, 0), _rows(tm, 256, 2), _full((1, 512)), _full((1, 256))],
                 out_specs=[_rows(tm, 512), _rows(tm, 256)],
                 out_shape=[_sds((s, 512), BF16), _sds((s, 256), BF16)], args=(proj, proj, gq, gkv))


def _rms_bwd(xv, gain, d, width):
    r = _rms(xv, width)
    nrm = xv * r
    dn = d * gain
    dx = r * (dn - nrm * (jnp.sum(dn * nrm, axis=-1, keepdims=True) * (1.0 / width)))
    return dx, d * nrm


def _mla_a_norm_bwd(proj, gq, gkv, dcqn, dckvn):
    s = proj.shape[0]
    tm = min(TILES["row"], s)

    def body(cq_ref, ckv_ref, gq_ref, gkv_ref, dq_ref, dkv_ref, oq_ref, okv_ref, aq_ref, akv_ref):
        dxq, gq_part = _rms_bwd(cq_ref[...], gq_ref[...], dq_ref[...], 512)
        dxk, gk_part = _rms_bwd(ckv_ref[...], gkv_ref[...], dkv_ref[...], 256)
        oq_ref[...] = dxq.astype(BF16)
        okv_ref[...] = dxk.astype(BF16)

        @pl.when(pl.program_id(0) == 0)
        def _():
            aq_ref[...] = jnp.zeros_like(aq_ref)
            akv_ref[...] = jnp.zeros_like(akv_ref)

        aq_ref[...] += _acc8(gq_part)
        akv_ref[...] += _acc8(gk_part)

    return _call(body, name="mla_a_norm_bwd", grid=(s // tm,),
                 in_specs=[_rows(tm, 512, 0), _rows(tm, 256, 2), _full((1, 512)), _full((1, 256)),
                           _rows(tm, 512), _rows(tm, 256)],
                 out_specs=[_rows(tm, 512), _rows(tm, 256), _full((8, 512)), _full((8, 256))],
                 out_shape=[_sds((s, 512), BF16), _sds((s, 256), BF16), _sds((8, 512), F32), _sds((8, 256), F32)],
                 args=(proj, proj, gq, gkv, dcqn, dckvn))


def _swap32(r):
    lane = lax.broadcasted_iota(jnp.int32, r.shape, 1)
    return jnp.where(lane < HALF_ROPE, pltpu.roll(r, 128 - HALF_ROPE, 1), pltpu.roll(r, HALF_ROPE, 1))


def _mla_post_fwd(q_pre, kv_pre, proj, cs, sn, gq, gk):
    s = q_pre.shape[0]
    tm = min(TILES["row"], s)

    def body(qp_ref, kvp_ref, misc_ref, cs_ref, sn_ref, gq_ref, gk_ref, q_ref, k_ref, v_ref):
        csv, snv = cs_ref[...], sn_ref[...]
        lane = lax.broadcasted_iota(jnp.int32, (tm, 128), 1)
        kr = jnp.where(lane < 64, misc_ref[...], 0.0)
        for h in range(HEADS):
            for src, g_ref, o_ref in ((None, gq_ref, q_ref), (kr, gk_ref, k_ref)):
                if src is None:
                    xv = qp_ref[:, QK_PAD * h:QK_PAD * (h + 1)]
                else:
                    xv = jnp.concatenate([kvp_ref[:, 256 * h:256 * h + 128], src], axis=-1)
                y = xv * _rms(xv, QK_DIM) * g_ref[...]
                if src is None:
                    y = y * Q_PRESCALE
                hi = y[:, 128:]
                hi = hi * csv + _swap32(hi) * snv
                o_ref[:, QK_PAD * h:QK_PAD * h + 128] = y[:, :128].astype(BF16)
                o_ref[:, QK_PAD * h + 128:QK_PAD * (h + 1)] = hi.astype(BF16)
            v_ref[:, 128 * h:128 * (h + 1)] = kvp_ref[:, 256 * h + 128:256 * (h + 1)].astype(BF16)

    return _call(body, name="mla_post_fwd", grid=(s // tm,),
                 in_specs=[_rows(tm, 2048), _rows(tm, 2048), _rows(tm, 128, 6), _rows(tm, 128), _rows(tm, 128),
                           _full((1, QK_PAD)), _full((1, QK_PAD))],
                 out_specs=[_rows(tm, 2048), _rows(tm, 2048), _rows(tm, 1024)],
                 out_shape=[_sds((s, 2048), BF16), _sds((s, 2048), BF16), _sds((s, 1024), BF16)],
                 args=(q_pre, kv_pre, proj, cs, sn, gq, gk))


def _mla_post_bwd(q_pre, kv_pre, proj, cs, sn, gq, gk, dq, dk, dv):
    s = q_pre.shape[0]
    tm = min(TILES["row"], s)

    def body(qp_ref, kvp_ref, misc_ref, cs_ref, sn_ref, gq_ref, gk_ref, dq_ref, dk_ref, dv_ref,
             oq_ref, okv_ref, okr_ref, agq_ref, agk_ref):
        csv, snv = cs_ref[...], sn_ref[...]
        lane = lax.broadcasted_iota(jnp.int32, (tm, 128), 1)
        kr = jnp.where(lane < 64, misc_ref[...], 0.0)

        @pl.when(pl.program_id(0) == 0)
        def _():
            agq_ref[...] = jnp.zeros_like(agq_ref)
            agk_ref[...] = jnp.zeros_like(agk_ref)

        dkr = jnp.zeros((tm, 128), F32)
        for h in range(HEADS):
            for which in (0, 1):
                if which == 0:
                    xv = qp_ref[:, QK_PAD * h:QK_PAD * (h + 1)]
                    d_ref, g_ref, a_ref = dq_ref, gq_ref, agq_ref
                else:
                    xv = jnp.concatenate([kvp_ref[:, 256 * h:256 * h + 128], kr], axis=-1)
                    d_ref, g_ref, a_ref = dk_ref, gk_ref, agk_ref
                dhi = d_ref[:, QK_PAD * h + 128:QK_PAD * (h + 1)]
                dhi = dhi * csv - _swap32(dhi) * snv
                dyv = jnp.concatenate([d_ref[:, QK_PAD * h:QK_PAD * h + 128], dhi], axis=-1)
                if which == 0:
                    dyv = dyv * ATTN_SCALE
                dx, gpart = _rms_bwd(xv, g_ref[...], dyv, QK_DIM)
                a_ref[...] += _acc8(gpart)
                if which == 0:
                    oq_ref[:, QK_PAD * h:QK_PAD * (h + 1)] = dx.astype(BF16)
                else:
                    okv_ref[:, 256 * h:256 * h + 128] = dx[:, :128].astype(BF16)
                    dkr = dkr + dx[:, 128:]
            okv_ref[:, 256 * h + 128:256 * (h + 1)] = dv_ref[:, 128 * h:128 * (h + 1)].astype(BF16)
        okr_ref[...] = dkr

    return _call(body, name="mla_post_bwd", grid=(s // tm,),
                 in_specs=[_rows(tm, 2048), _rows(tm, 2048), _rows(tm, 128, 6), _rows(tm, 128), _rows(tm, 128),
                           _full((1, QK_PAD)), _full((1, QK_PAD)), _rows(tm, 2048), _rows(tm, 2048), _rows(tm, 1024)],
                 out_specs=[_rows(tm, 2048), _rows(tm, 2048), _rows(tm, 128), _full((8, QK_PAD)), _full((8, QK_PAD))],
                 out_shape=[_sds((s, 2048), BF16), _sds((s, 2048), BF16), _sds((s, 128), F32),
                            _sds((8, QK_PAD), F32), _sds((8, QK_PAD), F32)],
                 args=(q_pre, kv_pre, proj, cs, sn, gq, gk, dq, dk, dv))


ATTN_SCALE = QK_DIM ** -0.5
NEG = -1e30


LOG2E = 1.4426950408889634
LN2 = 0.6931471805599453
Q_PRESCALE = ATTN_SCALE * LOG2E
ATTN_SUB_FWD = 256
ATTN_SUB_BWD = 512


def _diag_mask(sc, ts, qs):
    row = lax.broadcasted_iota(jnp.int32, sc.shape, 0) + qs * ts
    col = lax.broadcasted_iota(jnp.int32, sc.shape, 1)
    return jnp.where(col <= row, sc, NEG)


def _attn_fwd(q, k, v):
    s = q.shape[0]
    t = min(TILES["attn"], s)
    ts = min(ATTN_SUB_FWD, t)
    nq = s // t

    def slabs(q_ref, k_ref, v_ref, m_s, l_s, acc_s, diag):
        for qs in range(t // ts):
            rq = slice(qs * ts, (qs + 1) * ts)
            kw = (qs + 1) * ts if diag else t
            sc = lax.dot_general(q_ref[rq, :], k_ref[0:kw, :], NT, preferred_element_type=F32)
            if diag:
                sc = _diag_mask(sc, ts, qs)
            m_prev = m_s[rq, :]
            m_new = jnp.maximum(m_prev, jnp.max(sc, axis=-1, keepdims=True))
            p = jnp.exp2(sc - m_new)
            alpha = jnp.exp2(m_prev - m_new)
            l_s[rq, :] = alpha * l_s[rq, :] + jnp.sum(p, axis=-1, keepdims=True)
            acc_s[rq, :] = acc_s[rq, :] * alpha + lax.dot_general(p.astype(BF16), v_ref[0:kw, :], NN,
                                                                  preferred_element_type=F32)
            m_s[rq, :] = m_new

    def body(q_ref, k_ref, v_ref, o_ref, lse_ref, m_s, l_s, acc_s):
        i, j = pl.program_id(1), pl.program_id(2)

        @pl.when(j == 0)
        def _():
            m_s[...] = jnp.full_like(m_s, NEG)
            l_s[...] = jnp.zeros_like(l_s)
            acc_s[...] = jnp.zeros_like(acc_s)

        @pl.when(j < i)
        def _():
            slabs(q_ref, k_ref, v_ref, m_s, l_s, acc_s, False)

        @pl.when(j == i)
        def _():
            slabs(q_ref, k_ref, v_ref, m_s, l_s, acc_s, True)
            o_ref[...] = acc_s[...] / l_s[...]
            lse_ref[...] = m_s[...] + jnp.log2(l_s[...])

    return _call(
        body, name="attn_fwd", grid=(HEADS, nq, nq),
        in_specs=[pl.BlockSpec((t, QK_PAD), lambda h, i, j: (i, h)),
                  pl.BlockSpec((t, QK_PAD), lambda h, i, j: (jnp.minimum(i, j), h)),
                  pl.BlockSpec((t, HEAD_DIM), lambda h, i, j: (jnp.minimum(i, j), h))],
        out_specs=[pl.BlockSpec((t, HEAD_DIM), lambda h, i, j: (i, h)),
                   pl.BlockSpec((None, t, 1), lambda h, i, j: (h, i, 0))],
        out_shape=[_sds((s, HEADS * HEAD_DIM), F32), _sds((HEADS, s, 1), F32)],
        scratch=[pltpu.VMEM((t, 1), F32), pltpu.VMEM((t, 1), F32), pltpu.VMEM((t, HEAD_DIM), F32)],
        sem=("parallel", "parallel", "arbitrary"), args=(q, k, v))


def _attn_bwd(q, k, v, o, lse, do):
    s = q.shape[0]
    t = min(TILES["attn"], s)
    ts = min(ATTN_SUB_BWD, t)
    nq = s // t

    def slabs(q_ref, k_ref, v_ref, o_ref, lse_ref, do_ref, dq_ref, dk_ref, dv_ref, i, diag):
        for qs in range(t // ts):
            rq = slice(qs * ts, (qs + 1) * ts)
            kw = (qs + 1) * ts if diag else t
            qv, kv = q_ref[rq, :], k_ref[0:kw, :]
            sc = lax.dot_general(qv, kv, NT, preferred_element_type=F32)
            if diag:
                sc = _diag_mask(sc, ts, qs)
            p = jnp.exp2(sc - lse_ref[rq, :])
            dof = do_ref[rq, :]
            delta = jnp.sum(dof * o_ref[rq, :], axis=-1, keepdims=True)
            dob = dof.astype(BF16)
            dv_ref[0:kw, :] += lax.dot_general(p.astype(BF16), dob, TN, preferred_element_type=F32)
            dp = lax.dot_general(dob, v_ref[0:kw, :], NT, preferred_element_type=F32)
            ds = (p * (dp - delta)).astype(BF16)
            dk_ref[0:kw, :] += lax.dot_general(ds, qv, TN, preferred_element_type=F32)
            rows = pl.ds(pl.multiple_of(i * t + qs * ts, ts), ts)
            dq_ref[rows, :] += lax.dot_general(ds, kv, NN, preferred_element_type=F32)

    def body(q_ref, k_ref, v_ref, o_ref, lse_ref, do_ref, dq_ref, dk_ref, dv_ref):
        j, i = pl.program_id(1), pl.program_id(2)
        refs = (q_ref, k_ref, v_ref, o_ref, lse_ref, do_ref, dq_ref, dk_ref, dv_ref)

        @pl.when((j == 0) & (i == 0))
        def _():
            dq_ref[...] = jnp.zeros_like(dq_ref)

        @pl.when(i == 0)
        def _():
            dk_ref[...] = jnp.zeros_like(dk_ref)
            dv_ref[...] = jnp.zeros_like(dv_ref)

        @pl.when(i > j)
        def _():
            slabs(*refs, i, False)

        @pl.when(i == j)
        def _():
            slabs(*refs, i, True)

        @pl.when(i == nq - 1)
        def _():
            dk_ref[...] = dk_ref[...] * LN2

    qi = lambda h, j, i: (jnp.maximum(i, j), h)
    return _call(
        body, name="attn_bwd", grid=(HEADS, nq, nq),
        in_specs=[pl.BlockSpec((t, QK_PAD), qi),
                  pl.BlockSpec((t, QK_PAD), lambda h, j, i: (j, h)),
                  pl.BlockSpec((t, HEAD_DIM), lambda h, j, i: (j, h)),
                  pl.BlockSpec((t, HEAD_DIM), qi),
                  pl.BlockSpec((None, t, 1), lambda h, j, i: (h, jnp.maximum(i, j), 0)),
                  pl.BlockSpec((t, HEAD_DIM), qi)],
        out_specs=[pl.BlockSpec((s, QK_PAD), lambda h, j, i: (0, h)),
                   pl.BlockSpec((t, QK_PAD), lambda h, j, i: (j, h)),
                   pl.BlockSpec((t, HEAD_DIM), lambda h, j, i: (j, h))],
        out_shape=[_sds((s, HEADS * QK_PAD), F32), _sds((s, HEADS * QK_PAD), F32), _sds((s, HEADS * HEAD_DIM), F32)],
        sem=("parallel", "arbitrary", "arbitrary"), args=(q, k, v, o, lse, do))


GDN_Q_SCALE = HEAD_DIM ** -0.5


def _shift_down(xv, prev8, sft):
    rolled = pltpu.roll(xv, sft, 0)
    top = pltpu.roll(jnp.concatenate([prev8, xv[:8]], axis=0), sft, 0)[8:]
    return jnp.concatenate([top, rolled[8:]], axis=0)


def _shift_up(xv, next8, sft):
    tm = xv.shape[0]
    rolled = pltpu.roll(xv, tm - sft, 0)
    bot = pltpu.roll(jnp.concatenate([xv[tm - 8:], next8], axis=0), 16 - sft, 0)[:8]
    return jnp.concatenate([rolled[:tm - 8], bot], axis=0)


def _conv_z(xv, prev8, w_ref):
    z = xv * w_ref[3:4, :]
    for sft in (1, 2, 3):
        z = z + _shift_down(xv, prev8, sft) * w_ref[3 - sft:4 - sft, :]
    return z


def _conv_specs(s, tm):
    nb8 = tm // 8
    cur = pl.BlockSpec((tm, 1024), lambda j, i: (i, 2 + j))
    prev = pl.BlockSpec((8, 1024), lambda j, i: (jnp.maximum(i * nb8 - 1, 0), 2 + j))
    return cur, prev


def _gdn_conv_fwd(proj, conv_w):
    s = proj.shape[0]
    tm = min(TILES["row"], s)
    cur, prev = _conv_specs(s, tm)

    def body(x_ref, xp_ref, w_ref, o_ref):
        j, i = pl.program_id(0), pl.program_id(1)
        prev8 = jnp.where(i > 0, xp_ref[...], 0.0)
        a = _silu(_conv_z(x_ref[...], prev8, w_ref))
        qk_scale = jnp.where(j == 0, GDN_Q_SCALE, 1.0)
        for h in range(HEADS):
            seg = a[:, 128 * h:128 * (h + 1)]
            r = lax.rsqrt(jnp.sum(seg * seg, axis=-1, keepdims=True) + NORM_EPS)
            o_ref[:, 128 * h:128 * (h + 1)] = jnp.where(j < 2, seg * r * qk_scale, seg)

    return _call(body, name="gdn_conv_fwd", grid=(3, s // tm),
                 in_specs=[cur, prev, pl.BlockSpec((4, 1024), lambda j, i: (0, j))],
                 out_specs=pl.BlockSpec((tm, 1024), lambda j, i: (i, j)), out_shape=_sds((s, 3072), F32),
                 args=(proj, proj, conv_w))


def _gdn_conv_bwd_a(proj, conv_w, dqkv):
    s = proj.shape[0]
    tm = min(TILES["row"], s)
    cur, prev = _conv_specs(s, tm)

    def body(x_ref, xp_ref, w_ref, d_ref, o_ref):
        j, i = pl.program_id(0), pl.program_id(1)
        prev8 = jnp.where(i > 0, xp_ref[...], 0.0)
        z = _conv_z(x_ref[...], prev8, w_ref)
        a = _silu(z)
        dsl = _dsilu(z)
        qk_scale = jnp.where(j == 0, GDN_Q_SCALE, 1.0)
        for h in range(HEADS):
            sl = slice(128 * h, 128 * (h + 1))
            seg = a[:, sl]
            dyv = d_ref[:, sl]
            r = lax.rsqrt(jnp.sum(seg * seg, axis=-1, keepdims=True) + NORM_EPS)
            yh = seg * r
            da_n = qk_scale * r * (dyv - yh * jnp.sum(yh * dyv, axis=-1, keepdims=True))
            o_ref[:, sl] = jnp.where(j < 2, da_n, dyv) * dsl[:, sl]

    return _call(body, name="gdn_conv_bwd_a", grid=(3, s // tm),
                 in_specs=[cur, prev, pl.BlockSpec((4, 1024), lambda j, i: (0, j)),
                           pl.BlockSpec((tm, 1024), lambda j, i: (i, j))],
                 out_specs=pl.BlockSpec((tm, 1024), lambda j, i: (i, j)), out_shape=_sds((s, 3072), F32),
                 args=(proj, proj, conv_w, dqkv))


def _gdn_conv_bwd_b(proj, conv_w, dz):
    s = proj.shape[0]
    tm = min(TILES["row"], s)
    nb8 = tm // 8
    last8 = s // 8 - 1
    cur, prev = _conv_specs(s, tm)

    def body(x_ref, xp_ref, w_ref, dz_ref, dzn_ref, dx_ref, dw_ref):
        i = pl.program_id(1)
        prev8 = jnp.where(i > 0, xp_ref[...], 0.0)
        next8 = jnp.where(i < pl.num_programs(1) - 1, dzn_ref[...], 0.0)
        xv, dzv = x_ref[...], dz_ref[...]

        @pl.when(i == 0)
        def _():
            dw_ref[...] = jnp.zeros_like(dw_ref)

        dx = dzv * w_ref[3:4, :]
        dw_ref[3:4, :] += jnp.sum(dzv * xv, axis=0, keepdims=True)
        for sft in (1, 2, 3):
            dx = dx + _shift_up(dzv, next8, sft) * w_ref[3 - sft:4 - sft, :]
            dw_ref[3 - sft:4 - sft, :] += jnp.sum(dzv * _shift_down(xv, prev8, sft), axis=0, keepdims=True)
        dx_ref[...] = dx.astype(BF16)

    return _call(body, name="gdn_conv_bwd_b", grid=(3, s // tm),
                 in_specs=[cur, prev, pl.BlockSpec((4, 1024), lambda j, i: (0, j)),
                           pl.BlockSpec((tm, 1024), lambda j, i: (i, j)),
                           pl.BlockSpec((8, 1024), lambda j, i: (jnp.minimum((i + 1) * nb8, last8), j))],
                 out_specs=[pl.BlockSpec((tm, 1024), lambda j, i: (i, j)), pl.BlockSpec((4, 1024), lambda j, i: (0, j))],
                 out_shape=[_sds((s, 3072), BF16), _sds((4, 3072), F32)], args=(proj, proj, conv_w, dz, dz))


def _softplus(xv):
    return jnp.maximum(xv, 0.0) + jnp.log(1.0 + jnp.exp(-jnp.abs(xv)))


def _gdn_gates_fwd(proj, alog128, dtb128):
    s = proj.shape[0]
    tm = min(TILES["row"], s)

    def body(m_ref, a_ref, b_ref, o_ref):
        mv = m_ref[...]
        lane = lax.broadcasted_iota(jnp.int32, mv.shape, 1)
        g = -jnp.exp(a_ref[...]) * _softplus(mv + b_ref[...])
        is_g = (lane >= GA_LANE) & (lane < GA_LANE + HEADS)
        is_b = (lane >= GB_LANE) & (lane < GB_LANE + HEADS)
        o_ref[...] = jnp.where(is_g, g, jnp.where(is_b, _sigmoid(mv), 0.0))

    return _call(body, name="gdn_gates_fwd", grid=(s // tm,),
                 in_specs=[_rows(tm, 128, 6), _full((1, 128)), _full((1, 128))],
                 out_specs=_rows(tm, 128), out_shape=_sds((s, 128), F32), args=(proj, alog128, dtb128))


def _gdn_gates_bwd(proj, alog128, dtb128, gbeta, dgbeta, dkr):
    s = proj.shape[0]
    tm = min(TILES["row"], s)

    def body(m_ref, a_ref, b_ref, gb_ref, d_ref, kr_ref, o_ref, da_ref, db_ref):
        mv, dv = m_ref[...], d_ref[...]
        lane = lax.broadcasted_iota(jnp.int32, mv.shape, 1)
        is_g = (lane >= GA_LANE) & (lane < GA_LANE + HEADS)
        is_b = (lane >= GB_LANE) & (lane < GB_LANE + HEADS)
        dga = jnp.where(is_g, dv * (-jnp.exp(a_ref[...])) * _sigmoid(mv + b_ref[...]), 0.0)
        beta = gb_ref[...]
        dgb = jnp.where(is_b, dv * beta * (1.0 - beta), 0.0)
        o_ref[...] = jnp.where(lane < 64, kr_ref[...], dga + dgb).astype(BF16)

        @pl.when(pl.program_id(0) == 0)
        def _():
            da_ref[...] = jnp.zeros_like(da_ref)
            db_ref[...] = jnp.zeros_like(db_ref)

        da_ref[...] += _acc8(jnp.where(is_g, dv * gb_ref[...], 0.0))
        db_ref[...] += _acc8(dga)

    return _call(body, name="gdn_gates_bwd", grid=(s // tm,),
                 in_specs=[_rows(tm, 128, 6), _full((1, 128)), _full((1, 128)), _rows(tm, 128), _rows(tm, 128),
                           _rows(tm, 128)],
                 out_specs=[_rows(tm, 128), _full((8, 128)), _full((8, 128))],
                 out_shape=[_sds((s, 128), BF16), _sds((8, 128), F32), _sds((8, 128), F32)],
                 args=(proj, alog128, dtb128, gbeta, dgbeta, dkr))


def _col(mat, lane_idx, lane):
    return jnp.sum(jnp.where(lane == lane_idx, mat, 0.0), axis=-1, keepdims=True)


def _chunk_local(qh, kh, vh, gcol, bcol, ii, jj):
    lower, strict, eye = ii >= jj, ii > jj, ii == jj
    grow = jnp.sum(jnp.where(eye, gcol, 0.0), axis=0, keepdims=True)
    decay = jnp.where(lower, jnp.exp(jnp.where(lower, gcol - grow, 0.0)), 0.0)
    kb = kh * bcol
    vb = vh * bcol
    mm = _dot(kb, kh, NT)
    lmat = jnp.where(strict, mm * decay, 0.0)
    pw = -lmat
    tinv = jnp.where(eye, 1.0, 0.0) + pw
    for _ in range(5):
        pw = _dot_hi(pw, pw)
        tinv = tinv + _dot_hi(tinv, pw)
    egc = jnp.exp(gcol)
    kbg = kb * egc
    rhs = jnp.concatenate([vb, kbg], axis=-1)
    sol = _dot_hi(tinv, rhs)
    qk = _dot(qh, kh, NT)
    glast = jnp.sum(jnp.where(ii[:, :1] == CHUNK - 1, gcol, 0.0), axis=0, keepdims=True)
    ekd = jnp.exp(glast - gcol)
    return dict(decay=decay, kb=kb, vb=vb, mm=mm, lmat=lmat, tinv=tinv, egc=egc, kbg=kbg, rhs=rhs,
                u=sol[:, :HEAD_DIM], w=sol[:, HEAD_DIM:], qk=qk, amat=qk * decay, qd=qh * egc, ekd=ekd,
                kd=kh * ekd, gl=jnp.exp(glast), strict=strict, lower=lower, eye=eye)


def _tri(ii, jj):
    return jnp.where(ii >= jj, 1.0, 0.0)


def _gdn_fwd(qkv, gbeta):
    s = qkv.shape[0]
    n = s // CHUNK

    def body(qkv_ref, gb_ref, o_ref, st_ref, state):
        @pl.when(pl.program_id(0) == 0)
        def _():
            state[...] = jnp.zeros_like(state)

        ii = lax.broadcasted_iota(jnp.int32, (CHUNK, CHUNK), 0)
        jj = lax.broadcasted_iota(jnp.int32, (CHUNK, CHUNK), 1)
        lane = lax.broadcasted_iota(jnp.int32, (CHUNK, 128), 1)
        gbv = gb_ref[...]
        gc = _dot_hi(_tri(ii, jj), gbv)
        for h in range(HEADS):
            sl = slice(128 * h, 128 * (h + 1))
            qh = qkv_ref[:, 128 * h:128 * (h + 1)]
            kh = qkv_ref[:, 1024 + 128 * h:1024 + 128 * (h + 1)]
            vh = qkv_ref[:, 2048 + 128 * h:2048 + 128 * (h + 1)]
            c = _chunk_local(qh, kh, vh, _col(gc, GA_LANE + h, lane), _col(gbv, GB_LANE + h, lane), ii, jj)
            st = state[sl, :]
            st_ref[sl, :] = st
            vn = c["u"] - _dot(c["w"], st)
            o_ref[:, sl] = _dot(c["qd"], st) + _dot(c["amat"], vn)
            state[sl, :] = st * c["gl"] + _dot(c["kd"], vn, TN)

    return _call(body, name="gdn_fwd", grid=(n,),
                 in_specs=[_rows(CHUNK, 3072), _rows(CHUNK, 128)],
                 out_specs=[_rows(CHUNK, 1024), _rows(HEADS * 128, 128)],
                 out_shape=[_sds((s, 1024), F32), _sds((n * HEADS * 128, 128), F32)],
                 scratch=[pltpu.VMEM((HEADS * 128, 128), F32)], args=(qkv, gbeta))


def _gdn_bwd(qkv, gbeta, states, do):
    s = qkv.shape[0]
    n = s // CHUNK

    def body(qkv_ref, gb_ref, st_ref, do_ref, dqkv_ref, dgb_ref, dstate):
        @pl.when(pl.program_id(0) == 0)
        def _():
            dstate[...] = jnp.zeros_like(dstate)

        ii = lax.broadcasted_iota(jnp.int32, (CHUNK, CHUNK), 0)
        jj = lax.broadcasted_iota(jnp.int32, (CHUNK, CHUNK), 1)
        lane = lax.broadcasted_iota(jnp.int32, (CHUNK, 128), 1)
        row1 = ii[:, :1]
        gbv = gb_ref[...]
        gc = _dot_hi(_tri(ii, jj), gbv)
        dgc_all = jnp.zeros((CHUNK, 128), F32)
        db_all = jnp.zeros((CHUNK, 128), F32)
        for h in range(HEADS):
            sl = slice(128 * h, 128 * (h + 1))
            qh = qkv_ref[:, 128 * h:128 * (h + 1)]
            kh = qkv_ref[:, 1024 + 128 * h:1024 + 128 * (h + 1)]
            vh = qkv_ref[:, 2048 + 128 * h:2048 + 128 * (h + 1)]
            bcol = _col(gbv, GB_LANE + h, lane)
            c = _chunk_local(qh, kh, vh, _col(gc, GA_LANE + h, lane), bcol, ii, jj)
            st = st_ref[sl, :]
            dst = dstate[sl, :]
            dov = do_ref[:, sl]
            vn = c["u"] - _dot(c["w"], st)
            dvn = _dot(c["amat"], dov, TN) + _dot(c["kd"], dst)
            damat = jnp.where(c["lower"], _dot(dov, vn, NT), 0.0)
            dqd = _dot(dov, st, NT)
            dkd = _dot(vn, dst, NT)
            dw = -_dot(dvn, st, NT)
            dgl = jnp.sum(jnp.sum(st * dst, axis=-1, keepdims=True), axis=0, keepdims=True)
            dstate[sl, :] = _dot(c["qd"], dov, TN) + c["gl"] * dst - _dot(c["w"], dvn, TN)
            dsol = jnp.concatenate([dvn, dw], axis=-1)
            drhs = _dot_hi(c["tinv"], dsol, TN)
            dtinv = _dot_hi(dsol, c["rhs"], NT)
            dl = -_dot_hi(_dot_hi(c["tinv"], dtinv, TN), c["tinv"], NT)
            dl = jnp.where(c["strict"], dl, 0.0)
            dmm = dl * c["decay"]
            dqk = damat * c["decay"]
            wmat = dl * c["lmat"] + damat * c["amat"]
            dgc = jnp.sum(wmat, axis=-1, keepdims=True)
            wcol = jnp.sum(wmat, axis=0, keepdims=True)
            dgc = dgc - jnp.sum(jnp.where(c["eye"], wcol, 0.0), axis=-1, keepdims=True)
            dkb = _dot(dmm, kh) + drhs[:, HEAD_DIM:] * c["egc"]
            dk = _dot(dmm, c["kb"], TN) + _dot(dqk, qh, TN) + dkd * c["ekd"]
            dq = _dot(dqk, kh) + dqd * c["egc"]
            dgc = dgc + jnp.sum(drhs[:, HEAD_DIM:] * c["kbg"], axis=-1, keepdims=True)
            dgc = dgc + jnp.sum(dqd * c["qd"], axis=-1, keepdims=True)
            tmp = jnp.sum(dkd * c["kd"], axis=-1, keepdims=True)
            dgc = dgc - tmp
            dglast = jnp.sum(tmp, axis=0, keepdims=True) + dgl * c["gl"]
            dgc = dgc + jnp.where(row1 == CHUNK - 1, dglast, 0.0)
            dk = dk + dkb * bcol
            db = jnp.sum(dkb * kh, axis=-1, keepdims=True) + jnp.sum(drhs[:, :HEAD_DIM] * vh, axis=-1, keepdims=True)
            dqkv_ref[:, 128 * h:128 * (h + 1)] = dq
            dqkv_ref[:, 1024 + 128 * h:1024 + 128 * (h + 1)] = dk
            dqkv_ref[:, 2048 + 128 * h:2048 + 128 * (h + 1)] = drhs[:, :HEAD_DIM] * bcol
            dgc_all = dgc_all + jnp.where(lane == GA_LANE + h, dgc, 0.0)
            db_all = db_all + jnp.where(lane == GB_LANE + h, db, 0.0)
        dgb_ref[...] = _dot_hi(_tri(jj, ii), dgc_all) + db_all

    rev = lambda w: pl.BlockSpec((CHUNK, w), lambda i: (n - 1 - i, 0))
    return _call(body, name="gdn_bwd", grid=(n,),
                 in_specs=[rev(3072), rev(128), pl.BlockSpec((HEADS * 128, 128), lambda i: (n - 1 - i, 0)), rev(1024)],
                 out_specs=[rev(3072), rev(128)],
                 out_shape=[_sds((s, 3072), F32), _sds((s, 128), F32)],
                 scratch=[pltpu.VMEM((HEADS * 128, 128), F32)], args=(qkv, gbeta, states, do))


NN_B = (((2,), (1,)), ((0,), (0,)))
NT_B = (((2,), (2,)), ((0,), (0,)))
TN_B = (((1,), (1,)), ((0,), (0,)))
GDN_PAR_CHUNKS = 2
GDN_SEQ_CHUNKS = 4


def _gather_heads(qkv_ref, gc, gbv, qs, ks, vs, gs, bs, nchunks):
    lane = lax.broadcasted_iota(jnp.int32, (CHUNK, 128), 1)
    for c in range(nchunks):
        rows = slice(CHUNK * c, CHUNK * (c + 1))
        for h in range(HEADS):
            b = HEADS * c + h
            qs[b] = qkv_ref[rows, 128 * h:128 * (h + 1)]
            ks[b] = qkv_ref[rows, 1024 + 128 * h:1024 + 128 * (h + 1)]
            vs[b] = qkv_ref[rows, 2048 + 128 * h:2048 + 128 * (h + 1)]
            gs[b] = jnp.broadcast_to(_col(gc[rows], GA_LANE + h, lane), (CHUNK, 128))
            bs[b] = jnp.broadcast_to(_col(gbv[rows], GB_LANE + h, lane), (CHUNK, 128))


def _block_tri(rows, transpose=False):
    ri = lax.broadcasted_iota(jnp.int32, (rows, rows), 0)
    ci = lax.broadcasted_iota(jnp.int32, (rows, rows), 1)
    same = (ri >> 6) == (ci >> 6)
    return jnp.where(same & ((ci >= ri) if transpose else (ri >= ci)), 1.0, 0.0)


def _local_b(q, k, v, g128, b128):
    ii = lax.broadcasted_iota(jnp.int32, (1, CHUNK, CHUNK), 1)
    jj = lax.broadcasted_iota(jnp.int32, (1, CHUNK, CHUNK), 2)
    lower, strict, eye = ii >= jj, ii > jj, ii == jj
    g64 = g128[:, :, :CHUNK]
    grow = jnp.sum(jnp.where(eye, g64, 0.0), axis=1, keepdims=True)
    decay = jnp.where(lower, jnp.exp(jnp.where(lower, g64 - grow, 0.0)), 0.0)
    kb = k * b128
    vb = v * b128
    mm = lax.dot_general(kb.astype(BF16), k.astype(BF16), NT_B, preferred_element_type=F32)
    lmat = jnp.where(strict, mm * decay, 0.0)
    egc = jnp.exp(g128)
    kbg = kb * egc
    qk = lax.dot_general(q.astype(BF16), k.astype(BF16), NT_B, preferred_element_type=F32)
    row = lax.broadcasted_iota(jnp.int32, (1, CHUNK, 128), 1)
    glast = jnp.sum(jnp.where(row == CHUNK - 1, g128, 0.0), axis=1, keepdims=True)
    ekd = jnp.exp(glast - g128)
    return dict(decay=decay, kb=kb, vb=vb, lmat=lmat, egc=egc, kbg=kbg, amat=qk * decay, qd=q * egc, ekd=ekd,
                kd=k * ekd, gl=jnp.exp(glast), lower=lower, strict=strict, eye=eye)


def _bdot(a, b, dims):
    return lax.dot_general(a.astype(BF16), b.astype(BF16), dims, preferred_element_type=F32)


def _split(a):
    hi = a.astype(BF16)
    return hi, (a - hi.astype(F32)).astype(BF16)


def _bdot_hi(a, b, dims):
    ah, al = _split(a)
    bh, bl = _split(b)
    d = lambda x, y: lax.dot_general(x, y, dims, preferred_element_type=F32)
    return d(ah, bh) + d(ah, bl) + d(al, bh)


def _gdn_pre(qkv, gbeta):
    s = qkv.shape[0]
    n = s // CHUNK
    cb = min(GDN_PAR_CHUNKS, n)
    nb = cb * HEADS
    rows = cb * CHUNK

    def body(qkv_ref, gb_ref, u_ref, w_ref, qd_ref, kd_ref, a_ref, t_ref, gl_ref, qs, ks, vs, gs, bs):
        gbv = gb_ref[...]
        gc = _dot_hi(_block_tri(rows), gbv)
        _gather_heads(qkv_ref, gc, gbv, qs, ks, vs, gs, bs, cb)
        c = _local_b(qs[...], ks[...], vs[...], gs[...], bs[...])
        pw = -c["lmat"]
        tinv = jnp.where(c["eye"], 1.0, 0.0) + pw
        for _ in range(5):
            pw = _bdot_hi(pw, pw, NN_B)
            tinv = tinv + _bdot_hi(tinv, pw, NN_B)
        u_ref[...] = _bdot_hi(tinv, c["vb"], NN_B)
        w_ref[...] = _bdot_hi(tinv, c["kbg"], NN_B).astype(BF16)
        qd_ref[...] = c["qd"].astype(BF16)
        kd_ref[...] = c["kd"].astype(BF16)
        a_ref[...] = c["amat"].astype(BF16)
        t_ref[...] = tinv
        gl_ref[...] = c["gl"]

    b3 = lambda d: pl.BlockSpec((nb, CHUNK, d), lambda i: (i, 0, 0))
    nt = n * HEADS
    return _call(body, name="gdn_pre", grid=(n // cb,),
                 in_specs=[_rows(rows, 3072), _rows(rows, 128)],
                 out_specs=[b3(128), b3(128), b3(128), b3(128), b3(CHUNK), b3(CHUNK),
                            pl.BlockSpec((nb, 1, 128), lambda i: (i, 0, 0))],
                 out_shape=[_sds((nt, CHUNK, 128), F32), _sds((nt, CHUNK, 128), BF16), _sds((nt, CHUNK, 128), BF16),
                            _sds((nt, CHUNK, 128), BF16), _sds((nt, CHUNK, CHUNK), BF16), _sds((nt, CHUNK, CHUNK), F32),
                            _sds((nt, 1, 128), F32)],
                 scratch=[pltpu.VMEM((nb, CHUNK, 128), F32)] * 5, sem=("parallel",), args=(qkv, gbeta))


def _gdn_scan_fwd(u, w, qd, kd, amat, gl):
    nt = u.shape[0]
    n = nt // HEADS
    cs = min(GDN_SEQ_CHUNKS, n)

    def body(u_ref, w_ref, qd_ref, kd_ref, a_ref, gl_ref, o_ref, st_ref, state):
        @pl.when(pl.program_id(0) == 0)
        def _():
            state[...] = jnp.zeros_like(state)

        for c in range(cs):
            sl = slice(HEADS * c, HEADS * (c + 1))
            st = state[...]
            st_ref[sl] = st
            stb = st.astype(BF16)
            vn = u_ref[sl] - lax.dot_general(w_ref[sl], stb, NN_B, preferred_element_type=F32)
            vnb = vn.astype(BF16)
            o = (lax.dot_general(qd_ref[sl], stb, NN_B, preferred_element_type=F32)
                 + lax.dot_general(a_ref[sl], vnb, NN_B, preferred_element_type=F32))
            state[...] = st * gl_ref[sl] + lax.dot_general(kd_ref[sl], vnb, TN_B, preferred_element_type=F32)
            for h in range(HEADS):
                o_ref[CHUNK * c:CHUNK * (c + 1), 128 * h:128 * (h + 1)] = o[h]

    b3 = lambda d: pl.BlockSpec((cs * HEADS, CHUNK, d), lambda i: (i, 0, 0))
    return _call(body, name="gdn_scan_fwd", grid=(n // cs,),
                 in_specs=[b3(128), b3(128), b3(128), b3(128), b3(CHUNK), pl.BlockSpec((cs * HEADS, 1, 128), lambda i: (i, 0, 0))],
                 out_specs=[_rows(cs * CHUNK, 1024), pl.BlockSpec((cs * HEADS, 128, 128), lambda i: (i, 0, 0))],
                 out_shape=[_sds((n * CHUNK, 1024), F32), _sds((nt, 128, 128), F32)],
                 scratch=[pltpu.VMEM((HEADS, 128, 128), F32)], args=(u, w, qd, kd, amat, gl))


def _gdn_scan_bwd(w, qd, kd, amat, gl, do):
    nt = w.shape[0]
    n = nt // HEADS
    cs = min(GDN_SEQ_CHUNKS, n)
    ng = n // cs

    def body(w_ref, qd_ref, kd_ref, a_ref, gl_ref, do_ref, ds_ref, dstate, dos):
        @pl.when(pl.program_id(0) == 0)
        def _():
            dstate[...] = jnp.zeros_like(dstate)

        for c in reversed(range(cs)):
            sl = slice(HEADS * c, HEADS * (c + 1))
            for h in range(HEADS):
                dos[h] = do_ref[CHUNK * c:CHUNK * (c + 1), 128 * h:128 * (h + 1)].astype(BF16)
            dob = dos[...]
            dst = dstate[...]
            ds_ref[sl] = dst
            dvn = (lax.dot_general(a_ref[sl], dob, TN_B, preferred_element_type=F32)
                   + lax.dot_general(kd_ref[sl], dst.astype(BF16), NN_B, preferred_element_type=F32))
            dstate[...] = (lax.dot_general(qd_ref[sl], dob, TN_B, preferred_element_type=F32) + gl_ref[sl] * dst
                           - lax.dot_general(w_ref[sl], dvn.astype(BF16), TN_B, preferred_element_type=F32))

    b3 = lambda d: pl.BlockSpec((cs * HEADS, CHUNK, d), lambda i: (ng - 1 - i, 0, 0))
    return _call(body, name="gdn_scan_bwd", grid=(ng,),
                 in_specs=[b3(128), b3(128), b3(128), b3(CHUNK), pl.BlockSpec((cs * HEADS, 1, 128), lambda i: (ng - 1 - i, 0, 0)),
                           pl.BlockSpec((cs * CHUNK, 1024), lambda i: (ng - 1 - i, 0))],
                 out_specs=pl.BlockSpec((cs * HEADS, 128, 128), lambda i: (ng - 1 - i, 0, 0)),
                 out_shape=_sds((nt, 128, 128), F32),
                 scratch=[pltpu.VMEM((HEADS, 128, 128), F32), pltpu.VMEM((HEADS, CHUNK, 128), BF16)],
                 args=(w, qd, kd, amat, gl, do))


def _gdn_post_bwd(qkv, gbeta, u, w, tinv, states, dstates, do):
    s = qkv.shape[0]
    n = s // CHUNK
    cb = min(GDN_PAR_CHUNKS, n)
    nb = cb * HEADS
    rows = cb * CHUNK

    def body(qkv_ref, gb_ref, u_ref, w_ref, t_ref, st_ref, ds_ref, do_ref, dqkv_ref, dgb_ref, qs, ks, vs, gs, bs, dos):
        gbv = gb_ref[...]
        gc = _dot_hi(_block_tri(rows), gbv)
        _gather_heads(qkv_ref, gc, gbv, qs, ks, vs, gs, bs, cb)
        for c in range(cb):
            for h in range(HEADS):
                dos[HEADS * c + h] = do_ref[CHUNK * c:CHUNK * (c + 1), 128 * h:128 * (h + 1)]
        q, k, v, b128 = qs[...], ks[...], vs[...], bs[...]
        c = _local_b(q, k, v, gs[...], b128)
        tinv, st, dst, dov = t_ref[...], st_ref[...], ds_ref[...], dos[...]
        wv = w_ref[...]
        vn = u_ref[...] - _bdot(wv, st, NN_B)
        dvn = _bdot(c["amat"], dov, TN_B) + _bdot(c["kd"], dst, NN_B)
        damat = jnp.where(c["lower"], _bdot(dov, vn, NT_B), 0.0)
        dqd = _bdot(dov, st, NT_B)
        dkd = _bdot(vn, dst, NT_B)
        dw = -_bdot(dvn, st, NT_B)
        dgl = jnp.sum(jnp.sum(st * dst, axis=1, keepdims=True), axis=-1, keepdims=True)
        dvb = _bdot_hi(tinv, dvn, TN_B)
        dkbg = _bdot_hi(tinv, dw, TN_B)
        dtinv = _bdot_hi(dvn, c["vb"], NT_B) + _bdot_hi(dw, c["kbg"], NT_B)
        dl = -_bdot_hi(_bdot_hi(tinv, dtinv, TN_B), tinv, NT_B)
        dl = jnp.where(c["strict"], dl, 0.0)
        dmm = dl * c["decay"]
        dqk = damat * c["decay"]
        wmat = dl * c["lmat"] + damat * c["amat"]
        wcol = jnp.sum(wmat, axis=1, keepdims=True)
        dgc = jnp.sum(wmat, axis=-1, keepdims=True) - jnp.sum(jnp.where(c["eye"], wcol, 0.0), axis=-1, keepdims=True)
        dkb = _bdot(dmm, k, NN_B) + dkbg * c["egc"]
        dk = _bdot(dmm, c["kb"], TN_B) + _bdot(dqk, q, TN_B) + dkd * c["ekd"] + dkb * b128
        dq = _bdot(dqk, k, NN_B) + dqd * c["egc"]
        tmp = jnp.sum(dkd * c["kd"], axis=-1, keepdims=True)
        dgc = (dgc + jnp.sum(dkbg * c["kbg"], axis=-1, keepdims=True) + jnp.sum(dqd * c["qd"], axis=-1, keepdims=True)
               - tmp)
        dglast = jnp.sum(tmp, axis=1, keepdims=True) + dgl * c["gl"][:, :, :1]
        row1 = lax.broadcasted_iota(jnp.int32, (1, CHUNK, 1), 1)
        dgc = dgc + jnp.where(row1 == CHUNK - 1, dglast, 0.0)
        db = jnp.sum(dkb * k, axis=-1, keepdims=True) + jnp.sum(dvb * v, axis=-1, keepdims=True)
        dv = dvb * b128
        lane = lax.broadcasted_iota(jnp.int32, (CHUNK, 128), 1)
        parts = []
        for cc in range(cb):
            acc = jnp.zeros((CHUNK, 128), F32)
            for h in range(HEADS):
                bi = HEADS * cc + h
                rs = slice(CHUNK * cc, CHUNK * (cc + 1))
                dqkv_ref[rs, 128 * h:128 * (h + 1)] = dq[bi]
                dqkv_ref[rs, 1024 + 128 * h:1024 + 128 * (h + 1)] = dk[bi]
                dqkv_ref[rs, 2048 + 128 * h:2048 + 128 * (h + 1)] = dv[bi]
                acc = acc + jnp.where(lane == GA_LANE + h, dgc[bi], 0.0)
            parts.append(acc)
        dgc_all = jnp.concatenate(parts, axis=0)
        dg_all = _dot_hi(_block_tri(rows, transpose=True), dgc_all)
        for cc in range(cb):
            acc = dg_all[CHUNK * cc:CHUNK * (cc + 1)]
            for h in range(HEADS):
                acc = acc + jnp.where(lane == GB_LANE + h, db[HEADS * cc + h], 0.0)
            dgb_ref[CHUNK * cc:CHUNK * (cc + 1), :] = acc

    b3 = lambda d1, d2: pl.BlockSpec((nb, d1, d2), lambda i: (i, 0, 0))
    return _call(body, name="gdn_post_bwd", grid=(n // cb,),
                 in_specs=[_rows(rows, 3072), _rows(rows, 128), b3(CHUNK, 128), b3(CHUNK, 128), b3(CHUNK, CHUNK),
                           b3(128, 128), b3(128, 128), _rows(rows, 1024)],
                 out_specs=[_rows(rows, 3072), _rows(rows, 128)],
                 out_shape=[_sds((s, 3072), F32), _sds((s, 128), F32)],
                 scratch=[pltpu.VMEM((nb, CHUNK, 128), F32)] * 6, sem=("parallel",),
                 args=(qkv, gbeta, u, w, tinv, states, dstates, do))


def _mix_fwd(o_mla, o_gdn, proj, out_gain):
    s = proj.shape[0]
    tm = min(TILES["row"], s)

    def body(om_ref, og_ref, mg_ref, gg_ref, g_ref, o_ref):
        o_ref[:, :1024] = (om_ref[...] * _silu(mg_ref[...])).astype(BF16)
        for h in range(HEADS):
            sl = slice(128 * h, 128 * (h + 1))
            og = og_ref[:, sl]
            on = og * _rms(og, HEAD_DIM) * g_ref[...]
            o_ref[:, 1024 + 128 * h:1024 + 128 * (h + 1)] = (on * _silu(gg_ref[:, sl])).astype(BF16)

    return _call(body, name="mix_fwd", grid=(s // tm,),
                 in_specs=[_rows(tm, 1024), _rows(tm, 1024), _rows(tm, 1024, 1), _rows(tm, 1024, 5), _full((1, 128))],
                 out_specs=_rows(tm, 2048), out_shape=_sds((s, 2048), BF16), args=(o_mla, o_gdn, proj, proj, out_gain))


def _mix_bwd(o_mla, o_gdn, proj, out_gain, dmixed):
    s = proj.shape[0]
    tm = min(TILES["row"], s)

    def body(om_ref, og_ref, mg_ref, gg_ref, g_ref, dm_ref, dg_ref, dom_ref, dog_ref, dmg_ref, dgg_ref, ag_ref):
        @pl.when(pl.program_id(0) == 0)
        def _():
            ag_ref[...] = jnp.zeros_like(ag_ref)

        mg = mg_ref[...]
        dom_ref[...] = dm_ref[...] * _silu(mg)
        dmg_ref[...] = (dm_ref[...] * om_ref[...] * _dsilu(mg)).astype(BF16)
        for h in range(HEADS):
            sl = slice(128 * h, 128 * (h + 1))
            og, gg, d = og_ref[:, sl], gg_ref[:, sl], dg_ref[:, sl]
            on = og * _rms(og, HEAD_DIM) * g_ref[...]
            dgg_ref[:, sl] = (d * on * _dsilu(gg)).astype(BF16)
            dx, gpart = _rms_bwd(og, g_ref[...], d * _silu(gg), HEAD_DIM)
            dog_ref[:, sl] = dx
            ag_ref[...] += _acc8(gpart)

    return _call(body, name="mix_bwd", grid=(s // tm,),
                 in_specs=[_rows(tm, 1024), _rows(tm, 1024), _rows(tm, 1024, 1), _rows(tm, 1024, 5), _full((1, 128)),
                           _rows(tm, 1024, 0), _rows(tm, 1024, 1)],
                 out_specs=[_rows(tm, 1024), _rows(tm, 1024), _rows(tm, 1024), _rows(tm, 1024), _full((8, 128))],
                 out_shape=[_sds((s, 1024), F32), _sds((s, 1024), F32), _sds((s, 1024), BF16), _sds((s, 1024), BF16),
                            _sds((8, 128), F32)],
                 args=(o_mla, o_gdn, proj, proj, out_gain, dmixed, dmixed))


def _out_fwd(mixed, w_out, x, target):
    s = x.shape[0]
    tm = min(TILES["mm"], s)
    tn = min(TILES["mm"], D_MODEL)

    def body(m_ref, w_ref, x_ref, t_ref, dy_ref, acc_ref):
        err = x_ref[...] + _dot(m_ref[...], w_ref[...]) - t_ref[...]
        dy_ref[...] = err * (1.0 / D_MODEL)

        @pl.when(pl.program_id(1) == 0)
        def _():
            acc_ref[...] = jnp.zeros_like(acc_ref)

        acc_ref[...] += _acc8(err * err)

    return _call(body, name="out_fwd", grid=(D_MODEL // tn, s // tm),
                 in_specs=[pl.BlockSpec((tm, D_MODEL), lambda j, i: (i, 0)), pl.BlockSpec((D_MODEL, tn), lambda j, i: (0, j)),
                           pl.BlockSpec((tm, tn), lambda j, i: (i, j)), pl.BlockSpec((tm, tn), lambda j, i: (i, j))],
                 out_specs=[pl.BlockSpec((tm, tn), lambda j, i: (i, j)), pl.BlockSpec((8, tn), lambda j, i: (0, j))],
                 out_shape=[_sds((s, D_MODEL), F32), _sds((8, D_MODEL), F32)], args=(mixed, w_out, x, target))


def _row_tile(r, c):
    if r % 8 != 0:
        return r
    t = 8
    while r % (2 * t) == 0 and 2 * t * c * 4 <= (1 << 20):
        t *= 2
    return t


def _sum_arrays(parts, name, also_bf16=False):
    r, c = parts[0].shape
    tr = _row_tile(r, c)
    n = len(parts)

    def body(*refs):
        acc = refs[0][...].astype(F32)
        for p_ref in refs[1:n]:
            acc = acc + p_ref[...].astype(F32)
        refs[n][...] = acc
        if also_bf16:
            refs[n + 1][...] = acc.astype(BF16)

    nout = 2 if also_bf16 else 1
    out = _call(body, name=name, grid=(r // tr,), in_specs=[_rows(tr, c)] * n, out_specs=[_rows(tr, c)] * nout,
                out_shape=[_sds((r, c), F32), _sds((r, c), BF16)][:nout], args=tuple(parts))
    return out if also_bf16 else out[0]


def _adamw(w, g, m, v, name):
    r, c = w.shape
    tr = _row_tile(r, c)
    c1 = 1.0 - ADAM_B1 ** ADAM_STEP
    c2 = 1.0 - ADAM_B2 ** ADAM_STEP

    def body(w_ref, g_ref, m_ref, v_ref, d_ref, nm_ref, nv_ref):
        gv = g_ref[...]
        nm = ADAM_B1 * m_ref[...] + (1.0 - ADAM_B1) * gv
        nv = ADAM_B2 * v_ref[...] + (1.0 - ADAM_B2) * (gv * gv)
        nm_ref[...] = nm
        nv_ref[...] = nv
        d_ref[...] = -ADAM_LR * ((nm / c1) / (jnp.sqrt(nv / c2) + ADAM_EPS) + ADAM_WD * w_ref[...])

    return _call(body, name=name, grid=(r // tr,), in_specs=[_rows(tr, c)] * 4, out_specs=[_rows(tr, c)] * 3,
                 out_shape=[_sds((r, c), F32)] * 3, args=(w, g, m, v))


ANY = pl.BlockSpec(memory_space=pl.ANY)
CHIP_FLIPS = ((1, 0), (0, 1), (1, 1))


def _comm_call(body, *, name, n_in, out_shape, scratch):
    def kfn(*refs):
        body(*refs)
    return pl.pallas_call(kfn, name=name, in_specs=[ANY] * n_in, out_specs=[ANY] * len(out_shape), out_shape=out_shape,
                          scratch_shapes=list(scratch),
                          compiler_params=pltpu.CompilerParams(has_side_effects=True))


def _all_gather_chips(shards):
    na = len(shards)

    def body(*refs):
        ins, outs = refs[:na], refs[na:2 * na]
        send_sems, recv_sems, fwd_send, fwd_recv, local_sems = refs[2 * na:]
        x, y, c = lax.axis_index("x"), lax.axis_index("y"), lax.axis_index("c")
        my_k = 2 * x + y
        pending, forwards = [], []
        for a in range(na):
            cp = pltpu.make_async_copy(ins[a], outs[a].at[my_k], local_sems.at[a])
            cp.start()
            pending.append(cp)
            rows = ins[a].shape[0]
            split = rows % 32 == 0
            for r, (fx, fy) in enumerate(CHIP_FLIPS):
                px, py = x ^ fx, y ^ fy
                if split:
                    mine = pl.ds(pl.multiple_of(c * (rows // 2), 16), rows // 2)
                    other = pl.ds(pl.multiple_of((1 - c) * (rows // 2), 16), rows // 2)
                    rc = pltpu.make_async_remote_copy(
                        src_ref=ins[a].at[mine], dst_ref=outs[a].at[my_k, mine], send_sem=send_sems.at[a, r],
                        recv_sem=recv_sems.at[a, r], device_id=(px, py, c), device_id_type=MESH)
                    landed = outs[a].at[2 * px + py, mine]
                    fw = pltpu.make_async_remote_copy(
                        src_ref=landed, dst_ref=landed, send_sem=fwd_send.at[a, r], recv_sem=fwd_recv.at[a, r],
                        device_id=(x, y, 1 - c), device_id_type=MESH)
                    from_sib = outs[a].at[2 * px + py, other]
                    fw_in = pltpu.make_async_remote_copy(
                        src_ref=from_sib, dst_ref=from_sib, send_sem=fwd_send.at[a, r], recv_sem=fwd_recv.at[a, r],
                        device_id=(x, y, 1 - c), device_id_type=MESH)
                    forwards.append((rc, fw, fw_in))
                else:
                    rc = pltpu.make_async_remote_copy(
                        src_ref=ins[a], dst_ref=outs[a].at[my_k], send_sem=send_sems.at[a, r],
                        recv_sem=recv_sems.at[a, r], device_id=(px, py, c), device_id_type=MESH)
                    pending.append(rc)
                rc.start()
        for rc, fw, _ in forwards:
            rc.wait_recv()
            fw.start()
        for rc, fw, fw_in in forwards:
            rc.wait_send()
            fw.wait_send()
            fw_in.wait_recv()
        for cp in pending:
            cp.wait()

    out_shape = [_sds((4,) + a.shape, a.dtype) for a in shards]
    sem = pltpu.SemaphoreType.DMA((na, 3))
    return _comm_call(body, name="all_gather_weights", n_in=na, out_shape=out_shape,
                      scratch=[sem, sem, sem, sem, pltpu.SemaphoreType.DMA((na,))])(*shards)


def _all_reduce_small(vec):
    r = vec.shape[0]

    def body(v_ref, o_ref, gath, send_sems, recv_sems):
        x, y, c = lax.axis_index("x"), lax.axis_index("y"), lax.axis_index("c")
        me = 4 * x + 2 * y + c
        gath[me] = v_ref[...]
        copies = []
        for rel in range(1, 8):
            fx, fy, fc = (rel >> 2) & 1, (rel >> 1) & 1, rel & 1
            rc = pltpu.make_async_remote_copy(
                src_ref=v_ref, dst_ref=gath.at[me], send_sem=send_sems.at[rel - 1], recv_sem=recv_sems.at[rel - 1],
                device_id=(x ^ fx, y ^ fy, c ^ fc), device_id_type=MESH)
            rc.start()
            copies.append(rc)
        for rc in copies:
            rc.wait()
        acc = gath[0]
        for d in range(1, 8):
            acc = acc + gath[d]
        o_ref[...] = acc

    def kfn(*refs):
        body(*refs)
    vm = pl.BlockSpec(memory_space=pltpu.VMEM)
    return pl.pallas_call(kfn, name="all_reduce_small", in_specs=[vm], out_specs=vm, out_shape=_sds((r, 128), F32),
                          scratch_shapes=[pltpu.VMEM((8, r, 128), F32), pltpu.SemaphoreType.DMA((7,)),
                                          pltpu.SemaphoreType.DMA((7,))],
                          compiler_params=pltpu.CompilerParams(has_side_effects=True))(vec)


def _exchange_halves(arrs):
    na = len(arrs)

    def body(*refs):
        ins, outs = refs[:na], refs[na:2 * na]
        send_sems, recv_sems = refs[2 * na:]
        x, y, c = lax.axis_index("x"), lax.axis_index("y"), lax.axis_index("c")
        copies = []
        for a in range(na):
            half = ins[a].shape[1] // 2
            src = ins[a].at[:, pl.ds(pl.multiple_of((1 - c) * half, 8), half), :]
            rc = pltpu.make_async_remote_copy(src_ref=src, dst_ref=outs[a], send_sem=send_sems.at[a],
                                              recv_sem=recv_sems.at[a], device_id=(x, y, 1 - c), device_id_type=MESH)
            rc.start()
            copies.append(rc)
        for rc in copies:
            rc.wait()

    out_shape = [_sds((4, a.shape[1] // 2, a.shape[2]), F32) for a in arrs]
    return _comm_call(body, name="rs_pair_exchange", n_in=na, out_shape=out_shape,
                      scratch=[pltpu.SemaphoreType.DMA((na,)), pltpu.SemaphoreType.DMA((na,))])(*arrs)


def _scatter_to_chips(arrs):
    na = len(arrs)

    def body(*refs):
        ins, outs = refs[:na], refs[na:2 * na]
        send_sems, recv_sems = refs[2 * na:]
        x, y, c = lax.axis_index("x"), lax.axis_index("y"), lax.axis_index("c")
        copies = []
        for a in range(na):
            for r, (fx, fy) in enumerate(CHIP_FLIPS):
                px, py = x ^ fx, y ^ fy
                rc = pltpu.make_async_remote_copy(
                    src_ref=ins[a].at[2 * px + py], dst_ref=outs[a].at[r], send_sem=send_sems.at[a, r],
                    recv_sem=recv_sems.at[a, r], device_id=(px, py, c), device_id_type=MESH)
                rc.start()
                copies.append(rc)
        for rc in copies:
            rc.wait()

    out_shape = [_sds((3,) + a.shape[1:], a.dtype) for a in arrs]
    return _comm_call(body, name="rs_chip_scatter", n_in=na, out_shape=out_shape,
                      scratch=[pltpu.SemaphoreType.DMA((na, 3)), pltpu.SemaphoreType.DMA((na, 3))])(*arrs)


def _join_halves(arrs):
    na = len(arrs)

    def body(*refs):
        ins, outs = refs[:na], refs[na:2 * na]
        send_sems, recv_sems, local_sems = refs[2 * na:]
        x, y, c = lax.axis_index("x"), lax.axis_index("y"), lax.axis_index("c")
        copies = []
        for a in range(na):
            half = ins[a].shape[0]
            dst = outs[a].at[pl.ds(pl.multiple_of(c * half, 8), half), :]
            cp = pltpu.make_async_copy(ins[a], dst, local_sems.at[a])
            cp.start()
            rc = pltpu.make_async_remote_copy(src_ref=ins[a], dst_ref=dst, send_sem=send_sems.at[a],
                                              recv_sem=recv_sems.at[a], device_id=(x, y, 1 - c), device_id_type=MESH)
            rc.start()
            copies += [cp, rc]
        for cp in copies:
            cp.wait()

    out_shape = [_sds((2 * a.shape[0], a.shape[1]), F32) for a in arrs]
    return _comm_call(body, name="rs_pair_join", n_in=na, out_shape=out_shape,
                      scratch=[pltpu.SemaphoreType.DMA((na,)), pltpu.SemaphoreType.DMA((na,)),
                               pltpu.SemaphoreType.DMA((na,))])(*arrs)


def _reduce_scatter(grads):
    c = lax.axis_index("c")
    k_me = 2 * lax.axis_index("x") + lax.axis_index("y")
    got = _exchange_halves(grads)
    pair, pair_bf16 = [], []
    for a, (g, o) in enumerate(zip(grads, got)):
        half = g.shape[1] // 2
        mine = lax.dynamic_slice_in_dim(g, c * half, half, axis=1)
        p32, p16 = _sum_arrays([mine.reshape(4 * half, -1), o.reshape(4 * half, -1)], f"rs_pair_sum_{a}", also_bf16=True)
        pair.append(p32.reshape(4, half, -1))
        pair_bf16.append(p16.reshape(4, half, -1))
    recv = _scatter_to_chips(pair_bf16)
    halves = []
    for a, (p, rv) in enumerate(zip(pair, recv)):
        own = lax.dynamic_index_in_dim(p, k_me, 0, keepdims=False)
        halves.append(_sum_arrays([own, rv[0], rv[1], rv[2]], f"rs_chip_sum_{a}"))
    return _join_halves(halves)


def _pad_w_in(w):
    z = jnp.zeros((w.shape[0], 1024 - 848), w.dtype)
    return jnp.concatenate([w[:, 0:832], w[:, 4928:4944], z, w[:, 832:4928], w[:, 4944:5968]], axis=1)


def _unpad_w_in(g):
    return jnp.concatenate([g[:, 0:832], g[:, 1024:5120], g[:, 832:848], g[:, 5120:6144]], axis=1)


def _pad_heads(w):
    r = w.shape[0]
    return jnp.pad(w.reshape(r, HEADS, QK_DIM), ((0, 0), (0, 0), (0, QK_PAD - QK_DIM))).reshape(r, HEADS * QK_PAD)


def _unpad_heads(w):
    r = w.shape[0]
    return w.reshape(r, HEADS, QK_PAD)[:, :, :QK_DIM].reshape(r, HEADS * QK_DIM)


def _cols_to_blocks(w):
    r = w.shape[0]
    return w.reshape(r, 4, -1).transpose(1, 0, 2)


def _blocks_to_cols(w):
    return w.transpose(1, 0, 2).reshape(w.shape[1], -1)


SMALL_ROWS = {"norm_gain": (0, 2048), "mla_q_a_gain": (16, 512), "mla_kv_a_gain": (20, 256),
              "mla_q_norm_gain": (22, 192), "mla_k_norm_gain": (24, 192), "gdn_a_log": (26, 8),
              "gdn_dt_bias": (27, 8), "gdn_out_norm_gain": (28, 128)}
LOSS_ROW = 29
SMALL_PACK_ROWS = 32
CONV_ROW = 32


def _pack_small(vals, loss=None):
    rows = []
    at = 0
    for name, (row, size) in SMALL_ROWS.items():
        assert row == at
        nr = -(-size // 128)
        rows.append(jnp.pad(vals[name].reshape(-1).astype(F32), (0, nr * 128 - size)).reshape(nr, 128))
        at += nr
    assert at == LOSS_ROW
    if loss is not None:
        rows.append(jnp.pad(loss.reshape(1, 1), ((0, 0), (0, 127))))
        at += 1
    rows.append(jnp.zeros((SMALL_PACK_ROWS - at, 128), F32))
    return jnp.concatenate(rows, axis=0)


def _unpack_small(pack, name):
    row, size = SMALL_ROWS[name]
    nr = -(-size // 128)
    return pack[row:row + nr].reshape(-1)[:size].reshape(1, size)


def _local_step(x, positions, target, norm_gain, w_in_p, q_a_gain, kv_a_gain, w_uq_p, w_ukv, q_norm_gain,
                k_norm_gain, conv_w, a_log, dt_bias, out_gain, w_out):
    half = HALF_ROPE
    inv_freq = jnp.power(10000.0, -jnp.arange(half, dtype=F32) / half)
    ang = positions.astype(F32)[:, None] * inv_freq
    cos, sin = jnp.cos(ang), jnp.sin(ang)
    zpad = jnp.zeros((x.shape[0], 64), F32)
    cs = jnp.concatenate([cos, cos, zpad], axis=1)
    sn = jnp.concatenate([-sin, sin, zpad], axis=1)
    gq = jnp.pad(q_norm_gain.reshape(1, QK_DIM), ((0, 0), (0, QK_PAD - QK_DIM)))
    gk = jnp.pad(k_norm_gain.reshape(1, QK_DIM), ((0, 0), (0, QK_PAD - QK_DIM)))
    lane_pad = ((0, 0), (GA_LANE, 128 - GA_LANE - HEADS))
    alog128 = jnp.pad(a_log.reshape(1, HEADS), lane_pad)
    dtb128 = jnp.pad(dt_bias.reshape(1, HEADS), lane_pad)
    ng, qag, kvag, og = (norm_gain.reshape(1, -1), q_a_gain.reshape(1, -1), kv_a_gain.reshape(1, -1),
                         out_gain.reshape(1, -1))

    xn = _norm1_fwd(x, ng)
    proj = _matmul(xn, w_in_p, mode="nn", out_dtype=F32, name="in_proj")
    cqn, ckvn = _mla_a_norm(proj, qag, kvag)
    q_pre = _matmul(cqn, w_uq_p, mode="nn", out_dtype=F32, name="q_up")
    kv_pre = _matmul(ckvn, w_ukv, mode="nn", out_dtype=F32, name="kv_up")
    q, k, v = _mla_post_fwd(q_pre, kv_pre, proj, cs, sn, gq, gk)
    o_mla, lse = _attn_fwd(q, k, v)
    qkv = _gdn_conv_fwd(proj, conv_w)
    gbeta = _gdn_gates_fwd(proj, alog128, dtb128)
    g_u, g_w, g_qd, g_kd, g_a, g_t, g_gl = _gdn_pre(qkv, gbeta)
    o_gdn, states = _gdn_scan_fwd(g_u, g_w, g_qd, g_kd, g_a, g_gl)
    mixed = _mix_fwd(o_mla, o_gdn, proj, og)
    dy, sq = _out_fwd(mixed, w_out, x, target)

    dmixed = _matmul(dy, w_out, mode="nt", out_dtype=F32, name="d_mixed")
    d_w_out = _matmul(mixed, dy, mode="tn", out_dtype=F32, name="d_w_out", tk=1024)
    do_mla, do_gdn, dmg, dgg, d_out_gain = _mix_bwd(o_mla, o_gdn, proj, og, dmixed)
    dq, dk, dv = _attn_bwd(q, k, v, o_mla, lse, do_mla)
    dq_pre, dkv_pre, dkr, d_gq, d_gk = _mla_post_bwd(q_pre, kv_pre, proj, cs, sn, gq, gk, dq, dk, dv)
    d_w_uq_p = _matmul(cqn, dq_pre, mode="tn", out_dtype=F32, name="d_w_uq", tk=1024)
    d_w_ukv = _matmul(ckvn, dkv_pre, mode="tn", out_dtype=F32, name="d_w_ukv", tk=1024)
    dcqn = _matmul(dq_pre, w_uq_p, mode="nt", out_dtype=F32, name="d_cqn")
    dckvn = _matmul(dkv_pre, w_ukv, mode="nt", out_dtype=F32, name="d_ckvn")
    dcq, dckv, d_qag, d_kvag = _mla_a_norm_bwd(proj, qag, kvag, dcqn, dckvn)
    dstates = _gdn_scan_bwd(g_w, g_qd, g_kd, g_a, g_gl, do_gdn)
    dqkv, dgbeta = _gdn_post_bwd(qkv, gbeta, g_u, g_w, g_t, states, dstates, do_gdn)
    dz = _gdn_conv_bwd_a(proj, conv_w, dqkv)
    dgx, d_conv = _gdn_conv_bwd_b(proj, conv_w, dz)
    dmisc, d_alog, d_dtb = _gdn_gates_bwd(proj, alog128, dtb128, gbeta, dgbeta, dkr)
    dproj = jnp.concatenate([dcq, dckv, dmisc, jnp.zeros((x.shape[0], 128), BF16), dmg, dgx, dgg], axis=1)
    dxn = _matmul(dproj, w_in_p, mode="nt", out_dtype=F32, name="d_xn", tm=512, tn=512)
    d_w_in_p = _matmul(xn, dproj, mode="tn", out_dtype=F32, name="d_w_in", tk=1024)
    grad_x, d_ng = _norm1_bwd(x, ng, dxn, dy)

    small = {"norm_gain": d_ng.sum(0), "mla_q_a_gain": d_qag.sum(0), "mla_kv_a_gain": d_kvag.sum(0),
             "mla_q_norm_gain": d_gq.sum(0)[:QK_DIM], "mla_k_norm_gain": d_gk.sum(0)[:QK_DIM],
             "gdn_a_log": d_alog.sum(0)[GA_LANE:GA_LANE + HEADS], "gdn_dt_bias": d_dtb.sum(0)[GA_LANE:GA_LANE + HEADS],
             "gdn_out_norm_gain": d_out_gain.sum(0)}
    big = {"w_in": d_w_in_p, "w_uq": d_w_uq_p, "w_ukv": d_w_ukv, "w_out": d_w_out, "gdn_conv_w": d_conv}
    return sq, grad_x, small, big


WEIGHTS = ["norm_gain", "w_in", "mla_q_a_gain", "mla_kv_a_gain", "w_uq", "w_ukv", "mla_q_norm_gain", "mla_k_norm_gain",
           "gdn_conv_w", "gdn_a_log", "gdn_dt_bias", "gdn_out_norm_gain", "w_out"]
BIG = ["w_in", "w_uq", "w_ukv", "w_out"]


def kernel(x, positions, norm_gain, w_in, mla_q_a_gain, mla_kv_a_gain, w_uq, w_ukv, mla_q_norm_gain, mla_k_norm_gain, gdn_conv_w, gdn_a_log, gdn_dt_bias, gdn_out_norm_gain, w_out, loss_target, m_norm_gain, m_w_in, m_mla_q_a_gain, m_mla_kv_a_gain, m_w_uq, m_w_ukv, m_mla_q_norm_gain, m_mla_k_norm_gain, m_gdn_conv_w, m_gdn_a_log, m_gdn_dt_bias, m_gdn_out_norm_gain, m_w_out, v_norm_gain, v_w_in, v_mla_q_a_gain, v_mla_kv_a_gain, v_w_uq, v_w_ukv, v_mla_q_norm_gain, v_mla_k_norm_gain, v_gdn_conv_w, v_gdn_a_log, v_gdn_dt_bias, v_gdn_out_norm_gain, v_w_out):
    w = dict(norm_gain=norm_gain, w_in=w_in, mla_q_a_gain=mla_q_a_gain, mla_kv_a_gain=mla_kv_a_gain, w_uq=w_uq,
             w_ukv=w_ukv, mla_q_norm_gain=mla_q_norm_gain, mla_k_norm_gain=mla_k_norm_gain, gdn_conv_w=gdn_conv_w,
             gdn_a_log=gdn_a_log, gdn_dt_bias=gdn_dt_bias, gdn_out_norm_gain=gdn_out_norm_gain, w_out=w_out)
    m = dict(norm_gain=m_norm_gain, w_in=m_w_in, mla_q_a_gain=m_mla_q_a_gain, mla_kv_a_gain=m_mla_kv_a_gain,
             w_uq=m_w_uq, w_ukv=m_w_ukv, mla_q_norm_gain=m_mla_q_norm_gain, mla_k_norm_gain=m_mla_k_norm_gain,
             gdn_conv_w=m_gdn_conv_w, gdn_a_log=m_gdn_a_log, gdn_dt_bias=m_gdn_dt_bias,
             gdn_out_norm_gain=m_gdn_out_norm_gain, w_out=m_w_out)
    v = dict(norm_gain=v_norm_gain, w_in=v_w_in, mla_q_a_gain=v_mla_q_a_gain, mla_kv_a_gain=v_mla_kv_a_gain,
             w_uq=v_w_uq, w_ukv=v_w_ukv, mla_q_norm_gain=v_mla_q_norm_gain, mla_k_norm_gain=v_mla_k_norm_gain,
             gdn_conv_w=v_gdn_conv_w, gdn_a_log=v_gdn_a_log, gdn_dt_bias=v_gdn_dt_bias,
             gdn_out_norm_gain=v_gdn_out_norm_gain, w_out=v_w_out)
    k_me = 2 * lax.axis_index("x") + lax.axis_index("y")

    g_in, g_uq, g_ukv, g_out, g_conv = _all_gather_chips(
        [w_in[0].astype(BF16), w_uq[0].astype(BF16), w_ukv[0].astype(BF16), w_out[0].astype(BF16), gdn_conv_w[0]])
    w_in_p = _pad_w_in(_blocks_to_cols(g_in))
    w_uq_p = _pad_heads(_blocks_to_cols(g_uq))
    w_ukv_f = _blocks_to_cols(g_ukv)
    w_out_f = g_out.reshape(D_MODEL, D_MODEL)
    conv_f = _blocks_to_cols(g_conv)

    sq, grad_x, small, big = _local_step(
        x[0], positions[0], loss_target[0], norm_gain, w_in_p, mla_q_a_gain, mla_kv_a_gain, w_uq_p, w_ukv_f,
        mla_q_norm_gain, mla_k_norm_gain, conv_f, gdn_a_log, gdn_dt_bias, gdn_out_norm_gain, w_out_f)

    loss_local = (0.5 / D_MODEL) * jnp.sum(sq)
    pack = jnp.concatenate([_pack_small(small, loss_local), big["gdn_conv_w"].reshape(96, 128)], axis=0)
    tot = _all_reduce_small(pack)
    loss = tot[LOSS_ROW, 0]
    conv_grad = lax.dynamic_slice_in_dim(tot[CONV_ROW:].reshape(4, 3072), k_me * 768, 768, axis=1)

    shard_grads = _reduce_scatter([
        _cols_to_blocks(_unpad_w_in(big["w_in"])), _cols_to_blocks(_unpad_heads(big["w_uq"])),
        _cols_to_blocks(big["w_ukv"]), big["w_out"].reshape(4, 512, D_MODEL)])

    grads = {n: _unpack_small(tot, n) for n in SMALL_ROWS}
    grads["gdn_conv_w"] = conv_grad[None]
    for n, g in zip(BIG, shard_grads):
        grads[n] = g[None]

    delta, new_m, new_v = {}, {}, {}
    sw = _pack_small({n: w[n] for n in SMALL_ROWS})
    sm = _pack_small({n: m[n] for n in SMALL_ROWS})
    sv = _pack_small({n: v[n] for n in SMALL_ROWS})
    sd, snm, snv = _adamw(sw, tot[:SMALL_PACK_ROWS], sm, sv, "adamw_small")
    for n in SMALL_ROWS:
        delta[n], new_m[n], new_v[n] = _unpack_small(sd, n), _unpack_small(snm, n), _unpack_small(snv, n)
    for n in BIG + ["gdn_conv_w"]:
        d, nm, nv = _adamw(w[n][0], grads[n][0], m[n][0], v[n][0], f"adamw_{n}")
        delta[n], new_m[n], new_v[n] = d[None], nm[None], nv[None]

    return (loss, grad_x[None], *[grads[n] for n in WEIGHTS], *[delta[n] for n in WEIGHTS],
            *[new_m[n] for n in WEIGHTS], *[new_v[n] for n in WEIGHTS])
```

```python
import functools
import math

import jax
import jax.numpy as jnp
from jax import lax
from jax.experimental import pallas as pl
from jax.experimental.pallas import tpu as pltpu

F32 = jnp.float32
BF16 = jnp.bfloat16
MESH = pl.DeviceIdType.MESH

D_MODEL = 2048
HEADS = 8
HEAD_DIM = 128
QK_DIM = 192
QK_PAD = 256
HALF_ROPE = 32
CHUNK = 64
NORM_EPS = 1e-6
W_IN_COLS = 5968
W_IN_PAD = 6144
GA_LANE = 64
GB_LANE = 72
ADAM_LR, ADAM_B1, ADAM_B2, ADAM_EPS, ADAM_WD, ADAM_STEP = 0.001, 0.9, 0.999, 1e-08, 0.01, 10
VMEM_LIMIT_V7X = 52 * 1024 * 1024
HI = lax.Precision.HIGHEST
NN = (((1,), (0,)), ((), ()))
NT = (((1,), (1,)), ((), ()))
TN = (((0,), (0,)), ((), ()))

TILES = {"row": 512, "attn": 1024, "mm": 1024}


def _call(body, *, name, grid, in_specs, out_specs, out_shape, args, scratch=(), sem=None):
    def kfn(*refs):
        body(*refs)
    if sem is None:
        sem = ("arbitrary",) * len(grid)
    return pl.pallas_call(
        kfn, name=name, grid=grid, in_specs=in_specs, out_specs=out_specs, out_shape=out_shape,
        scratch_shapes=list(scratch),
        compiler_params=pltpu.CompilerParams(dimension_semantics=sem, vmem_limit_bytes=VMEM_LIMIT_V7X),
    )(*args)


def _rows(tm, w, cb=0):
    return pl.BlockSpec((tm, w), lambda i: (i, cb))


def _full(shape):
    n = len(shape)
    return pl.BlockSpec(shape, lambda *_: (0,) * n)


def _sds(shape, dtype):
    return jax.ShapeDtypeStruct(shape, dtype)


def _acc8(x):
    tm, c = x.shape
    return jnp.sum(x.reshape(tm // 8, 8, c), axis=0)


def _sigmoid(x):
    return 1.0 / (1.0 + jnp.exp(-x))


def _silu(x):
    return x * _sigmoid(x)


def _dsilu(x):
    s = _sigmoid(x)
    return s * (1.0 + x * (1.0 - s))


def _dot(a, b, dims=NN):
    return lax.dot_general(a.astype(BF16), b.astype(BF16), dims, preferred_element_type=F32)


def _dot_hi(a, b, dims=NN):
    return lax.dot_general(a, b, dims, precision=HI, preferred_element_type=F32)


def _matmul(a, b, *, mode, out_dtype, name, tm=None, tn=None, tk=None):
    if mode == "tn":
        kdim, m = a.shape
    else:
        m, kdim = a.shape
    n = b.shape[0] if mode == "nt" else b.shape[1]
    tm = min(tm or TILES["mm"], m)
    tn = min(tn or TILES["mm"], n)
    tk = min(tk or kdim, kdim)
    nk = kdim // tk
    dims = {"nn": NN, "nt": NT, "tn": TN}[mode]
    if mode == "tn":
        a_spec = pl.BlockSpec((tk, tm), lambda i, j, k: (k, i))
    else:
        a_spec = pl.BlockSpec((tm, tk), lambda i, j, k: (i, k))
    if mode == "nt":
        b_spec = pl.BlockSpec((tn, tk), lambda i, j, k: (j, k))
    else:
        b_spec = pl.BlockSpec((tk, tn), lambda i, j, k: (k, j))

    def body(a_ref, b_ref, o_ref):
        r = _dot(a_ref[...], b_ref[...], dims)
        if nk == 1:
            o_ref[...] = r.astype(o_ref.dtype)
        else:
            k = pl.program_id(2)

            @pl.when(k == 0)
            def _():
                o_ref[...] = r

            @pl.when(k > 0)
            def _():
                o_ref[...] += r

    if nk > 1:
        assert out_dtype == F32
    return _call(body, name=name, grid=(m // tm, n // tn, nk), in_specs=[a_spec, b_spec],
                 out_specs=pl.BlockSpec((tm, tn), lambda i, j, k: (i, j)), out_shape=_sds((m, n), out_dtype),
                 args=(a, b))


def _norm1_fwd(x, gain):
    s = x.shape[0]
    tm = min(TILES["row"], s)

    def body(x_ref, g_ref, o_ref):
        xv = x_ref[...]
        r = lax.rsqrt(jnp.mean(xv * xv, axis=-1, keepdims=True) + NORM_EPS)
        o_ref[...] = (xv * r * g_ref[...]).astype(BF16)

    return _call(body, name="norm1_fwd", grid=(s // tm,), in_specs=[_rows(tm, D_MODEL), _full((1, D_MODEL))],
                 out_specs=_rows(tm, D_MODEL), out_shape=_sds((s, D_MODEL), BF16), args=(x, gain))


def _norm1_bwd(x, gain, dxn, dy):
    s = x.shape[0]
    tm = min(TILES["row"], s)

    def body(x_ref, g_ref, dxn_ref, dy_ref, gx_ref, dg_ref):
        xv = x_ref[...]
        r = lax.rsqrt(jnp.mean(xv * xv, axis=-1, keepdims=True) + NORM_EPS)
        nrm = xv * r
        d = dxn_ref[...]
        dn = d * g_ref[...]
        gx_ref[...] = dy_ref[...] + r * (dn - nrm * jnp.mean(dn * nrm, axis=-1, keepdims=True))

        @pl.when(pl.program_id(0) == 0)
        def _():
            dg_ref[...] = jnp.zeros_like(dg_ref)

        dg_ref[...] += _acc8(d * nrm)

    return _call(body, name="norm1_bwd", grid=(s // tm,),
                 in_specs=[_rows(tm, D_MODEL), _full((1, D_MODEL)), _rows(tm, D_MODEL), _rows(tm, D_MODEL)],
                 out_specs=[_rows(tm, D_MODEL), _full((8, D_MODEL))],
                 out_shape=[_sds((s, D_MODEL), F32), _sds((8, D_MODEL), F32)], args=(x, gain, dxn, dy))


def _rms(xv, width):
    return lax.rsqrt(jnp.sum(xv * xv, axis=-1, keepdims=True) * (1.0 / width) + NORM_EPS)


def _mla_a_norm(proj, gq, gkv):
    s = proj.shape[0]
    tm = min(TILES["row"], s)

    def body(cq_ref, ckv_ref, gq_ref, gkv_ref, oq_ref, okv_ref):
        a = cq_ref[...]
        oq_ref[...] = (a * _rms(a, 512) * gq_ref[...]).astype(BF16)
        b = ckv_ref[...]
        okv_ref[...] = (b * _rms(b, 256) * gkv_ref[...]).astype(BF16)

    return _call(body, name="mla_a_norm", grid=(s // tm,),
                 in_specs=[_rows(tm, 512, 0), _rows(tm, 256, 2), _full((1, 512)), _full((1, 256))],
                 out_specs=[_rows(tm, 512), _rows(tm, 256)],
                 out_shape=[_sds((s, 512), BF16), _sds((s, 256), BF16)], args=(proj, proj, gq, gkv))


def _rms_bwd(xv, gain, d, width):
    r = _rms(xv, width)
    nrm = xv * r
    dn = d * gain
    dx = r * (dn - nrm * (jnp.sum(dn * nrm, axis=-1, keepdims=True) * (1.0 / width)))
    return dx, d * nrm


def _mla_a_norm_bwd(proj, gq, gkv, dcqn, dckvn):
    s = proj.shape[0]
    tm = min(TILES["row"], s)

    def body(cq_ref, ckv_ref, gq_ref, gkv_ref, dq_ref, dkv_ref, oq_ref, okv_ref, aq_ref, akv_ref):
        dxq, gq_part = _rms_bwd(cq_ref[...], gq_ref[...], dq_ref[...], 512)
        dxk, gk_part = _rms_bwd(ckv_ref[...], gkv_ref[...], dkv_ref[...], 256)
        oq_ref[...] = dxq.astype(BF16)
        okv_ref[...] = dxk.astype(BF16)

        @pl.when(pl.program_id(0) == 0)
        def _():
            aq_ref[...] = jnp.zeros_like(aq_ref)
            akv_ref[...] = jnp.zeros_like(akv_ref)

        aq_ref[...] += _acc8(gq_part)
        akv_ref[...] += _acc8(gk_part)

    return _call(body, name="mla_a_norm_bwd", grid=(s // tm,),
                 in_specs=[_rows(tm, 512, 0), _rows(tm, 256, 2), _full((1, 512)), _full((1, 256)),
                           _rows(tm, 512), _rows(tm, 256)],
                 out_specs=[_rows(tm, 512), _rows(tm, 256), _full((8, 512)), _full((8, 256))],
                 out_shape=[_sds((s, 512), BF16), _sds((s, 256), BF16), _sds((8, 512), F32), _sds((8, 256), F32)],
                 args=(proj, proj, gq, gkv, dcqn, dckvn))


def _swap32(r):
    lane = lax.broadcasted_iota(jnp.int32, r.shape, 1)
    return jnp.where(lane < HALF_ROPE, pltpu.roll(r, 128 - HALF_ROPE, 1), pltpu.roll(r, HALF_ROPE, 1))


def _mla_post_fwd(q_pre, kv_pre, proj, cs, sn, gq, gk):
    s = q_pre.shape[0]
    tm = min(TILES["row"], s)

    def body(qp_ref, kvp_ref, misc_ref, cs_ref, sn_ref, gq_ref, gk_ref, q_ref, k_ref, v_ref):
        csv, snv = cs_ref[...], sn_ref[...]
        lane = lax.broadcasted_iota(jnp.int32, (tm, 128), 1)
        kr = jnp.where(lane < 64, misc_ref[...], 0.0)
        for h in range(HEADS):
            for src, g_ref, o_ref in ((None, gq_ref, q_ref), (kr, gk_ref, k_ref)):
                if src is None:
                    xv = qp_ref[:, QK_PAD * h:QK_PAD * (h + 1)]
                else:
                    xv = jnp.concatenate([kvp_ref[:, 256 * h:256 * h + 128], src], axis=-1)
                y = xv * _rms(xv, QK_DIM) * g_ref[...]
                if src is None:
                    y = y * Q_PRESCALE
                hi = y[:, 128:]
                hi = hi * csv + _swap32(hi) * snv
                o_ref[:, QK_PAD * h:QK_PAD * h + 128] = y[:, :128].astype(BF16)
                o_ref[:, QK_PAD * h + 128:QK_PAD * (h + 1)] = hi.astype(BF16)
            v_ref[:, 128 * h:128 * (h + 1)] = kvp_ref[:, 256 * h + 128:256 * (h + 1)].astype(BF16)

    return _call(body, name="mla_post_fwd", grid=(s // tm,),
                 in_specs=[_rows(tm, 2048), _rows(tm, 2048), _rows(tm, 128, 6), _rows(tm, 128), _rows(tm, 128),
                           _full((1, QK_PAD)), _full((1, QK_PAD))],
                 out_specs=[_rows(tm, 2048), _rows(tm, 2048), _rows(tm, 1024)],
                 out_shape=[_sds((s, 2048), BF16), _sds((s, 2048), BF16), _sds((s, 1024), BF16)],
                 args=(q_pre, kv_pre, proj, cs, sn, gq, gk))


def _mla_post_bwd(q_pre, kv_pre, proj, cs, sn, gq, gk, dq, dk, dv):
    s = q_pre.shape[0]
    tm = min(TILES["row"], s)

    def body(qp_ref, kvp_ref, misc_ref, cs_ref, sn_ref, gq_ref, gk_ref, dq_ref, dk_ref, dv_ref,
             oq_ref, okv_ref, okr_ref, agq_ref, agk_ref):
        csv, snv = cs_ref[...], sn_ref[...]
        lane = lax.broadcasted_iota(jnp.int32, (tm, 128), 1)
        kr = jnp.where(lane < 64, misc_ref[...], 0.0)

        @pl.when(pl.program_id(0) == 0)
        def _():
            agq_ref[...] = jnp.zeros_like(agq_ref)
            agk_ref[...] = jnp.zeros_like(agk_ref)

        dkr = jnp.zeros((tm, 128), F32)
        for h in range(HEADS):
            for which in (0, 1):
                if which == 0:
                    xv = qp_ref[:, QK_PAD * h:QK_PAD * (h + 1)]
                    d_ref, g_ref, a_ref = dq_ref, gq_ref, agq_ref
                else:
                    xv = jnp.concatenate([kvp_ref[:, 256 * h:256 * h + 128], kr], axis=-1)
                    d_ref, g_ref, a_ref = dk_ref, gk_ref, agk_ref
                dhi = d_ref[:, QK_PAD * h + 128:QK_PAD * (h + 1)]
                dhi = dhi * csv - _swap32(dhi) * snv
                dyv = jnp.concatenate([d_ref[:, QK_PAD * h:QK_PAD * h + 128], dhi], axis=-1)
                if which == 0:
                    dyv = dyv * ATTN_SCALE
                dx, gpart = _rms_bwd(xv, g_ref[...], dyv, QK_DIM)
                a_ref[...] += _acc8(gpart)
                if which == 0:
                    oq_ref[:, QK_PAD * h:QK_PAD * (h + 1)] = dx.astype(BF16)
                else:
                    okv_ref[:, 256 * h:256 * h + 128] = dx[:, :128].astype(BF16)
                    dkr = dkr + dx[:, 128:]
            okv_ref[:, 256 * h + 128:256 * (h + 1)] = dv_ref[:, 128 * h:128 * (h + 1)].astype(BF16)
        okr_ref[...] = dkr

    return _call(body, name="mla_post_bwd", grid=(s // tm,),
                 in_specs=[_rows(tm, 2048), _rows(tm, 2048), _rows(tm, 128, 6), _rows(tm, 128), _rows(tm, 128),
                           _full((1, QK_PAD)), _full((1, QK_PAD)), _rows(tm, 2048), _rows(tm, 2048), _rows(tm, 1024)],
                 out_specs=[_rows(tm, 2048), _rows(tm, 2048), _rows(tm, 128), _full((8, QK_PAD)), _full((8, QK_PAD))],
                 out_shape=[_sds((s, 2048), BF16), _sds((s, 2048), BF16), _sds((s, 128), F32),
                            _sds((8, QK_PAD), F32), _sds((8, QK_PAD), F32)],
                 args=(q_pre, kv_pre, proj, cs, sn, gq, gk, dq, dk, dv))


ATTN_SCALE = QK_DIM ** -0.5
NEG = -1e30


LOG2E = 1.4426950408889634
LN2 = 0.6931471805599453
Q_PRESCALE = ATTN_SCALE * LOG2E
ATTN_SUB_FWD = 256
ATTN_SUB_BWD = 512


def _causal_pairs(nq, kv_major):
    prs = [(i, j) for i in range(nq) for j in range(i + 1)]
    if kv_major:
        prs.sort(key=lambda ij: (ij[1], ij[0]))
    return (jnp.asarray([p[0] for p in prs], jnp.int32), jnp.asarray([p[1] for p in prs], jnp.int32))


def _pair_call(body, *, name, tables, in_specs, out_specs, out_shape, scratch, args):
    def kfn(*refs):
        body(*refs)
    spec = pltpu.PrefetchScalarGridSpec(num_scalar_prefetch=2, grid=(HEADS, tables[0].shape[0]), in_specs=in_specs,
                                        out_specs=out_specs, scratch_shapes=list(scratch))
    return pl.pallas_call(
        kfn, name=name, grid_spec=spec, out_shape=out_shape,
        compiler_params=pltpu.CompilerParams(dimension_semantics=("parallel", "arbitrary"),
                                             vmem_limit_bytes=VMEM_LIMIT_V7X))(*tables, *args)


def _diag_mask(sc, ts, qs):
    row = lax.broadcasted_iota(jnp.int32, sc.shape, 0) + qs * ts
    col = lax.broadcasted_iota(jnp.int32, sc.shape, 1)
    return jnp.where(col <= row, sc, NEG)


def _attn_fwd(q, k, v):
    s = q.shape[0]
    t = min(TILES["attn"], s)
    ts = min(ATTN_SUB_FWD, t)
    nq = s // t

    def slabs(q_ref, k_ref, v_ref, m_s, l_s, acc_s, diag):
        for qs in range(t // ts):
            rq = slice(qs * ts, (qs + 1) * ts)
            kw = (qs + 1) * ts if diag else t
            sc = lax.dot_general(q_ref[rq, :], k_ref[0:kw, :], NT, preferred_element_type=F32)
            if diag:
                sc = _diag_mask(sc, ts, qs)
            m_prev = m_s[rq, :]
            m_new = jnp.maximum(m_prev, jnp.max(sc, axis=-1, keepdims=True))
            p = jnp.exp2(sc - m_new)
            alpha = jnp.exp2(m_prev - m_new)
            l_s[rq, :] = alpha * l_s[rq, :] + jnp.sum(p, axis=-1, keepdims=True)
            acc_s[rq, :] = acc_s[rq, :] * alpha + lax.dot_general(p.astype(BF16), v_ref[0:kw, :], NN,
                                                                  preferred_element_type=F32)
            m_s[rq, :] = m_new

    def body(it_ref, jt_ref, q_ref, k_ref, v_ref, o_ref, lse_ref, m_s, l_s, acc_s):
        p = pl.program_id(1)
        i, j = it_ref[p], jt_ref[p]

        @pl.when(j == 0)
        def _():
            m_s[...] = jnp.full_like(m_s, NEG)
            l_s[...] = jnp.zeros_like(l_s)
            acc_s[...] = jnp.zeros_like(acc_s)

        @pl.when(j < i)
        def _():
            slabs(q_ref, k_ref, v_ref, m_s, l_s, acc_s, False)

        @pl.when(j == i)
        def _():
            slabs(q_ref, k_ref, v_ref, m_s, l_s, acc_s, True)
            o_ref[...] = acc_s[...] / l_s[...]
            lse_ref[...] = m_s[...] + jnp.log2(l_s[...])

    qb = lambda h, p, it, jt: (it[p], h)
    kb = lambda h, p, it, jt: (jt[p], h)
    return _pair_call(
        body, name="attn_fwd", tables=_causal_pairs(nq, kv_major=False),
        in_specs=[pl.BlockSpec((t, QK_PAD), qb), pl.BlockSpec((t, QK_PAD), kb), pl.BlockSpec((t, HEAD_DIM), kb)],
        out_specs=[pl.BlockSpec((t, HEAD_DIM), qb),
                   pl.BlockSpec((None, t, 1), lambda h, p, it, jt: (h, it[p], 0))],
        out_shape=[_sds((s, HEADS * HEAD_DIM), F32), _sds((HEADS, s, 1), F32)],
        scratch=[pltpu.VMEM((t, 1), F32), pltpu.VMEM((t, 1), F32), pltpu.VMEM((t, HEAD_DIM), F32)],
        args=(q, k, v))


def _attn_bwd(q, k, v, o, lse, do):
    s = q.shape[0]
    t = min(TILES["attn"], s)
    ts = min(ATTN_SUB_BWD, t)
    nq = s // t

    def slabs(q_ref, k_ref, v_ref, o_ref, lse_ref, do_ref, dq_ref, dk_ref, dv_ref, i, diag):
        for qs in range(t // ts):
            rq = slice(qs * ts, (qs + 1) * ts)
            kw = (qs + 1) * ts if diag else t
            qv, kv = q_ref[rq, :], k_ref[0:kw, :]
            sc = lax.dot_general(qv, kv, NT, preferred_element_type=F32)
            if diag:
                sc = _diag_mask(sc, ts, qs)
            p = jnp.exp2(sc - lse_ref[rq, :])
            dof = do_ref[rq, :]
            delta = jnp.sum(dof * o_ref[rq, :], axis=-1, keepdims=True)
            dob = dof.astype(BF16)
            dv_ref[0:kw, :] += lax.dot_general(p.astype(BF16), dob, TN, preferred_element_type=F32)
            dp = lax.dot_general(dob, v_ref[0:kw, :], NT, preferred_element_type=F32)
            ds = (p * (dp - delta)).astype(BF16)
            dk_ref[0:kw, :] += lax.dot_general(ds, qv, TN, preferred_element_type=F32)
            rows = pl.ds(pl.multiple_of(i * t + qs * ts, ts), ts)
            dq_ref[rows, :] += lax.dot_general(ds, kv, NN, preferred_element_type=F32)

    def body(it_ref, jt_ref, q_ref, k_ref, v_ref, o_ref, lse_ref, do_ref, dq_ref, dk_ref, dv_ref):
        p = pl.program_id(1)
        i, j = it_ref[p], jt_ref[p]
        refs = (q_ref, k_ref, v_ref, o_ref, lse_ref, do_ref, dq_ref, dk_ref, dv_ref)

        @pl.when(p == 0)
        def _():
            dq_ref[...] = jnp.zeros_like(dq_ref)

        @pl.when(i == j)
        def _():
            dk_ref[...] = jnp.zeros_like(dk_ref)
            dv_ref[...] = jnp.zeros_like(dv_ref)

        @pl.when(i > j)
        def _():
            slabs(*refs, i, False)

        @pl.when(i == j)
        def _():
            slabs(*refs, i, True)

        @pl.when(i == nq - 1)
        def _():
            dk_ref[...] = dk_ref[...] * LN2

    qb = lambda h, p, it, jt: (it[p], h)
    kb = lambda h, p, it, jt: (jt[p], h)
    return _pair_call(
        body, name="attn_bwd", tables=_causal_pairs(nq, kv_major=True),
        in_specs=[pl.BlockSpec((t, QK_PAD), qb), pl.BlockSpec((t, QK_PAD), kb), pl.BlockSpec((t, HEAD_DIM), kb),
                  pl.BlockSpec((t, HEAD_DIM), qb), pl.BlockSpec((None, t, 1), lambda h, p, it, jt: (h, it[p], 0)),
                  pl.BlockSpec((t, HEAD_DIM), qb)],
        out_specs=[pl.BlockSpec((s, QK_PAD), lambda h, p, it, jt: (0, h)), pl.BlockSpec((t, QK_PAD), kb),
                   pl.BlockSpec((t, HEAD_DIM), kb)],
        out_shape=[_sds((s, HEADS * QK_PAD), F32), _sds((s, HEADS * QK_PAD), F32), _sds((s, HEADS * HEAD_DIM), F32)],
        scratch=(), args=(q, k, v, o, lse, do))


GDN_Q_SCALE = HEAD_DIM ** -0.5


def _shift_down(xv, prev8, sft):
    rolled = pltpu.roll(xv, sft, 0)
    top = pltpu.roll(jnp.concatenate([prev8, xv[:8]], axis=0), sft, 0)[8:]
    return jnp.concatenate([top, rolled[8:]], axis=0)


def _shift_up(xv, next8, sft):
    tm = xv.shape[0]
    rolled = pltpu.roll(xv, tm - sft, 0)
    bot = pltpu.roll(jnp.concatenate([xv[tm - 8:], next8], axis=0), 16 - sft, 0)[:8]
    return jnp.concatenate([rolled[:tm - 8], bot], axis=0)


def _conv_z(xv, prev8, w_ref):
    z = xv * w_ref[3:4, :]
    for sft in (1, 2, 3):
        z = z + _shift_down(xv, prev8, sft) * w_ref[3 - sft:4 - sft, :]
    return z


def _conv_specs(s, tm):
    nb8 = tm // 8
    cur = pl.BlockSpec((tm, 1024), lambda j, i: (i, 2 + j))
    prev = pl.BlockSpec((8, 1024), lambda j, i: (jnp.maximum(i * nb8 - 1, 0), 2 + j))
    return cur, prev


def _gdn_conv_fwd(proj, conv_w):
    s = proj.shape[0]
    tm = min(TILES["row"], s)
    cur, prev = _conv_specs(s, tm)

    def body(x_ref, xp_ref, w_ref, o_ref):
        j, i = pl.program_id(0), pl.program_id(1)
        prev8 = jnp.where(i > 0, xp_ref[...], 0.0)
        a = _silu(_conv_z(x_ref[...], prev8, w_ref))
        qk_scale = jnp.where(j == 0, GDN_Q_SCALE, 1.0)
        for h in range(HEADS):
            seg = a[:, 128 * h:128 * (h + 1)]
            r = lax.rsqrt(jnp.sum(seg * seg, axis=-1, keepdims=True) + NORM_EPS)
            o_ref[:, 128 * h:128 * (h + 1)] = jnp.where(j < 2, seg * r * qk_scale, seg)

    return _call(body, name="gdn_conv_fwd", grid=(3, s // tm),
                 in_specs=[cur, prev, pl.BlockSpec((4, 1024), lambda j, i: (0, j))],
                 out_specs=pl.BlockSpec((tm, 1024), lambda j, i: (i, j)), out_shape=_sds((s, 3072), F32),
                 args=(proj, proj, conv_w))


def _gdn_conv_bwd_a(proj, conv_w, dqkv):
    s = proj.shape[0]
    tm = min(TILES["row"], s)
    cur, prev = _conv_specs(s, tm)

    def body(x_ref, xp_ref, w_ref, d_ref, o_ref):
        j, i = pl.program_id(0), pl.program_id(1)
        prev8 = jnp.where(i > 0, xp_ref[...], 0.0)
        z = _conv_z(x_ref[...], prev8, w_ref)
        a = _silu(z)
        dsl = _dsilu(z)
        qk_scale = jnp.where(j == 0, GDN_Q_SCALE, 1.0)
        for h in range(HEADS):
            sl = slice(128 * h, 128 * (h + 1))
            seg = a[:, sl]
            dyv = d_ref[:, sl]
            r = lax.rsqrt(jnp.sum(seg * seg, axis=-1, keepdims=True) + NORM_EPS)
            yh = seg * r
            da_n = qk_scale * r * (dyv - yh * jnp.sum(yh * dyv, axis=-1, keepdims=True))
            o_ref[:, sl] = jnp.where(j < 2, da_n, dyv) * dsl[:, sl]

    return _call(body, name="gdn_conv_bwd_a", grid=(3, s // tm),
                 in_specs=[cur, prev, pl.BlockSpec((4, 1024), lambda j, i: (0, j)),
                           pl.BlockSpec((tm, 1024), lambda j, i: (i, j))],
                 out_specs=pl.BlockSpec((tm, 1024), lambda j, i: (i, j)), out_shape=_sds((s, 3072), F32),
                 args=(proj, proj, conv_w, dqkv))


def _gdn_conv_bwd_b(proj, conv_w, dz):
    s = proj.shape[0]
    tm = min(TILES["row"], s)
    nb8 = tm // 8
    last8 = s // 8 - 1
    cur, prev = _conv_specs(s, tm)

    def body(x_ref, xp_ref, w_ref, dz_ref, dzn_ref, dx_ref, dw_ref):
        i = pl.program_id(1)
        prev8 = jnp.where(i > 0, xp_ref[...], 0.0)
        next8 = jnp.where(i < pl.num_programs(1) - 1, dzn_ref[...], 0.0)
        xv, dzv = x_ref[...], dz_ref[...]

        @pl.when(i == 0)
        def _():
            dw_ref[...] = jnp.zeros_like(dw_ref)

        dx = dzv * w_ref[3:4, :]
        dw_ref[3:4, :] += jnp.sum(dzv * xv, axis=0, keepdims=True)
        for sft in (1, 2, 3):
            dx = dx + _shift_up(dzv, next8, sft) * w_ref[3 - sft:4 - sft, :]
            dw_ref[3 - sft:4 - sft, :] += jnp.sum(dzv * _shift_down(xv, prev8, sft), axis=0, keepdims=True)
        dx_ref[...] = dx.astype(BF16)

    return _call(body, name="gdn_conv_bwd_b", grid=(3, s // tm),
                 in_specs=[cur, prev, pl.BlockSpec((4, 1024), lambda j, i: (0, j)),
                           pl.BlockSpec((tm, 1024), lambda j, i: (i, j)),
                           pl.BlockSpec((8, 1024), lambda j, i: (jnp.minimum((i + 1) * nb8, last8), j))],
                 out_specs=[pl.BlockSpec((tm, 1024), lambda j, i: (i, j)), pl.BlockSpec((4, 1024), lambda j, i: (0, j))],
                 out_shape=[_sds((s, 3072), BF16), _sds((4, 3072), F32)], args=(proj, proj, conv_w, dz, dz))


def _softplus(xv):
    return jnp.maximum(xv, 0.0) + jnp.log(1.0 + jnp.exp(-jnp.abs(xv)))


def _gdn_gates_fwd(proj, alog128, dtb128):
    s = proj.shape[0]
    tm = min(TILES["row"], s)

    def body(m_ref, a_ref, b_ref, o_ref):
        mv = m_ref[...]
        lane = lax.broadcasted_iota(jnp.int32, mv.shape, 1)
        g = -jnp.exp(a_ref[...]) * _softplus(mv + b_ref[...])
        is_g = (lane >= GA_LANE) & (lane < GA_LANE + HEADS)
        is_b = (lane >= GB_LANE) & (lane < GB_LANE + HEADS)
        o_ref[...] = jnp.where(is_g, g, jnp.where(is_b, _sigmoid(mv), 0.0))

    return _call(body, name="gdn_gates_fwd", grid=(s // tm,),
                 in_specs=[_rows(tm, 128, 6), _full((1, 128)), _full((1, 128))],
                 out_specs=_rows(tm, 128), out_shape=_sds((s, 128), F32), args=(proj, alog128, dtb128))


def _gdn_gates_bwd(proj, alog128, dtb128, gbeta, dgbeta, dkr):
    s = proj.shape[0]
    tm = min(TILES["row"], s)

    def body(m_ref, a_ref, b_ref, gb_ref, d_ref, kr_ref, o_ref, da_ref, db_ref):
        mv, dv = m_ref[...], d_ref[...]
        lane = lax.broadcasted_iota(jnp.int32, mv.shape, 1)
        is_g = (lane >= GA_LANE) & (lane < GA_LANE + HEADS)
        is_b = (lane >= GB_LANE) & (lane < GB_LANE + HEADS)
        dga = jnp.where(is_g, dv * (-jnp.exp(a_ref[...])) * _sigmoid(mv + b_ref[...]), 0.0)
        beta = gb_ref[...]
        dgb = jnp.where(is_b, dv * beta * (1.0 - beta), 0.0)
        o_ref[...] = jnp.where(lane < 64, kr_ref[...], dga + dgb).astype(BF16)

        @pl.when(pl.program_id(0) == 0)
        def _():
            da_ref[...] = jnp.zeros_like(da_ref)
            db_ref[...] = jnp.zeros_like(db_ref)

        da_ref[...] += _acc8(jnp.where(is_g, dv * gb_ref[...], 0.0))
        db_ref[...] += _acc8(dga)

    return _call(body, name="gdn_gates_bwd", grid=(s // tm,),
                 in_specs=[_rows(tm, 128, 6), _full((1, 128)), _full((1, 128)), _rows(tm, 128), _rows(tm, 128),
                           _rows(tm, 128)],
                 out_specs=[_rows(tm, 128), _full((8, 128)), _full((8, 128))],
                 out_shape=[_sds((s, 128), BF16), _sds((8, 128), F32), _sds((8, 128), F32)],
                 args=(proj, alog128, dtb128, gbeta, dgbeta, dkr))


def _col(mat, lane_idx, lane):
    return jnp.sum(jnp.where(lane == lane_idx, mat, 0.0), axis=-1, keepdims=True)


def _chunk_local(qh, kh, vh, gcol, bcol, ii, jj):
    lower, strict, eye = ii >= jj, ii > jj, ii == jj
    grow = jnp.sum(jnp.where(eye, gcol, 0.0), axis=0, keepdims=True)
    decay = jnp.where(lower, jnp.exp(jnp.where(lower, gcol - grow, 0.0)), 0.0)
    kb = kh * bcol
    vb = vh * bcol
    mm = _dot(kb, kh, NT)
    lmat = jnp.where(strict, mm * decay, 0.0)
    pw = -lmat
    tinv = jnp.where(eye, 1.0, 0.0) + pw
    for _ in range(5):
        pw = _dot_hi(pw, pw)
        tinv = tinv + _dot_hi(tinv, pw)
    egc = jnp.exp(gcol)
    kbg = kb * egc
    rhs = jnp.concatenate([vb, kbg], axis=-1)
    sol = _dot_hi(tinv, rhs)
    qk = _dot(qh, kh, NT)
    glast = jnp.sum(jnp.where(ii[:, :1] == CHUNK - 1, gcol, 0.0), axis=0, keepdims=True)
    ekd = jnp.exp(glast - gcol)
    return dict(decay=decay, kb=kb, vb=vb, mm=mm, lmat=lmat, tinv=tinv, egc=egc, kbg=kbg, rhs=rhs,
                u=sol[:, :HEAD_DIM], w=sol[:, HEAD_DIM:], qk=qk, amat=qk * decay, qd=qh * egc, ekd=ekd,
                kd=kh * ekd, gl=jnp.exp(glast), strict=strict, lower=lower, eye=eye)


def _tri(ii, jj):
    return jnp.where(ii >= jj, 1.0, 0.0)


def _gdn_fwd(qkv, gbeta):
    s = qkv.shape[0]
    n = s // CHUNK

    def body(qkv_ref, gb_ref, o_ref, st_ref, state):
        @pl.when(pl.program_id(0) == 0)
        def _():
            state[...] = jnp.zeros_like(state)

        ii = lax.broadcasted_iota(jnp.int32, (CHUNK, CHUNK), 0)
        jj = lax.broadcasted_iota(jnp.int32, (CHUNK, CHUNK), 1)
        lane = lax.broadcasted_iota(jnp.int32, (CHUNK, 128), 1)
        gbv = gb_ref[...]
        gc = _dot_hi(_tri(ii, jj), gbv)
        for h in range(HEADS):
            sl = slice(128 * h, 128 * (h + 1))
            qh = qkv_ref[:, 128 * h:128 * (h + 1)]
            kh = qkv_ref[:, 1024 + 128 * h:1024 + 128 * (h + 1)]
            vh = qkv_ref[:, 2048 + 128 * h:2048 + 128 * (h + 1)]
            c = _chunk_local(qh, kh, vh, _col(gc, GA_LANE + h, lane), _col(gbv, GB_LANE + h, lane), ii, jj)
            st = state[sl, :]
            st_ref[sl, :] = st
            vn = c["u"] - _dot(c["w"], st)
            o_ref[:, sl] = _dot(c["qd"], st) + _dot(c["amat"], vn)
            state[sl, :] = st * c["gl"] + _dot(c["kd"], vn, TN)

    return _call(body, name="gdn_fwd", grid=(n,),
                 in_specs=[_rows(CHUNK, 3072), _rows(CHUNK, 128)],
                 out_specs=[_rows(CHUNK, 1024), _rows(HEADS * 128, 128)],
                 out_shape=[_sds((s, 1024), F32), _sds((n * HEADS * 128, 128), F32)],
                 scratch=[pltpu.VMEM((HEADS * 128, 128), F32)], args=(qkv, gbeta))


def _gdn_bwd(qkv, gbeta, states, do):
    s = qkv.shape[0]
    n = s // CHUNK

    def body(qkv_ref, gb_ref, st_ref, do_ref, dqkv_ref, dgb_ref, dstate):
        @pl.when(pl.program_id(0) == 0)
        def _():
            dstate[...] = jnp.zeros_like(dstate)

        ii = lax.broadcasted_iota(jnp.int32, (CHUNK, CHUNK), 0)
        jj = lax.broadcasted_iota(jnp.int32, (CHUNK, CHUNK), 1)
        lane = lax.broadcasted_iota(jnp.int32, (CHUNK, 128), 1)
        row1 = ii[:, :1]
        gbv = gb_ref[...]
        gc = _dot_hi(_tri(ii, jj), gbv)
        dgc_all = jnp.zeros((CHUNK, 128), F32)
        db_all = jnp.zeros((CHUNK, 128), F32)
        for h in range(HEADS):
            sl = slice(128 * h, 128 * (h + 1))
            qh = qkv_ref[:, 128 * h:128 * (h + 1)]
            kh = qkv_ref[:, 1024 + 128 * h:1024 + 128 * (h + 1)]
            vh = qkv_ref[:, 2048 + 128 * h:2048 + 128 * (h + 1)]
            bcol = _col(gbv, GB_LANE + h, lane)
            c = _chunk_local(qh, kh, vh, _col(gc, GA_LANE + h, lane), bcol, ii, jj)
            st = st_ref[sl, :]
            dst = dstate[sl, :]
            dov = do_ref[:, sl]
            vn = c["u"] - _dot(c["w"], st)
            dvn = _dot(c["amat"], dov, TN) + _dot(c["kd"], dst)
            damat = jnp.where(c["lower"], _dot(dov, vn, NT), 0.0)
            dqd = _dot(dov, st, NT)
            dkd = _dot(vn, dst, NT)
            dw = -_dot(dvn, st, NT)
            dgl = jnp.sum(jnp.sum(st * dst, axis=-1, keepdims=True), axis=0, keepdims=True)
            dstate[sl, :] = _dot(c["qd"], dov, TN) + c["gl"] * dst - _dot(c["w"], dvn, TN)
            dsol = jnp.concatenate([dvn, dw], axis=-1)
            drhs = _dot_hi(c["tinv"], dsol, TN)
            dtinv = _dot_hi(dsol, c["rhs"], NT)
            dl = -_dot_hi(_dot_hi(c["tinv"], dtinv, TN), c["tinv"], NT)
            dl = jnp.where(c["strict"], dl, 0.0)
            dmm = dl * c["decay"]
            dqk = damat * c["decay"]
            wmat = dl * c["lmat"] + damat * c["amat"]
            dgc = jnp.sum(wmat, axis=-1, keepdims=True)
            wcol = jnp.sum(wmat, axis=0, keepdims=True)
            dgc = dgc - jnp.sum(jnp.where(c["eye"], wcol, 0.0), axis=-1, keepdims=True)
            dkb = _dot(dmm, kh) + drhs[:, HEAD_DIM:] * c["egc"]
            dk = _dot(dmm, c["kb"], TN) + _dot(dqk, qh, TN) + dkd * c["ekd"]
            dq = _dot(dqk, kh) + dqd * c["egc"]
            dgc = dgc + jnp.sum(drhs[:, HEAD_DIM:] * c["kbg"], axis=-1, keepdims=True)
            dgc = dgc + jnp.sum(dqd * c["qd"], axis=-1, keepdims=True)
            tmp = jnp.sum(dkd * c["kd"], axis=-1, keepdims=True)
            dgc = dgc - tmp
            dglast = jnp.sum(tmp, axis=0, keepdims=True) + dgl * c["gl"]
            dgc = dgc + jnp.where(row1 == CHUNK - 1, dglast, 0.0)
            dk = dk + dkb * bcol
            db = jnp.sum(dkb * kh, axis=-1, keepdims=True) + jnp.sum(drhs[:, :HEAD_DIM] * vh, axis=-1, keepdims=True)
            dqkv_ref[:, 128 * h:128 * (h + 1)] = dq
            dqkv_ref[:, 1024 + 128 * h:1024 + 128 * (h + 1)] = dk
            dqkv_ref[:, 2048 + 128 * h:2048 + 128 * (h + 1)] = drhs[:, :HEAD_DIM] * bcol
            dgc_all = dgc_all + jnp.where(lane == GA_LANE + h, dgc, 0.0)
            db_all = db_all + jnp.where(lane == GB_LANE + h, db, 0.0)
        dgb_ref[...] = _dot_hi(_tri(jj, ii), dgc_all) + db_all

    rev = lambda w: pl.BlockSpec((CHUNK, w), lambda i: (n - 1 - i, 0))
    return _call(body, name="gdn_bwd", grid=(n,),
                 in_specs=[rev(3072), rev(128), pl.BlockSpec((HEADS * 128, 128), lambda i: (n - 1 - i, 0)), rev(1024)],
                 out_specs=[rev(3072), rev(128)],
                 out_shape=[_sds((s, 3072), F32), _sds((s, 128), F32)],
                 scratch=[pltpu.VMEM((HEADS * 128, 128), F32)], args=(qkv, gbeta, states, do))


NN_B = (((2,), (1,)), ((0,), (0,)))
NT_B = (((2,), (2,)), ((0,), (0,)))
TN_B = (((1,), (1,)), ((0,), (0,)))
GDN_PAR_CHUNKS = 2
GDN_SEQ_CHUNKS = 4


def _gather_heads(qkv_ref, gc, gbv, qs, ks, vs, gs, bs, nchunks):
    lane = lax.broadcasted_iota(jnp.int32, (CHUNK, 128), 1)
    for c in range(nchunks):
        rows = slice(CHUNK * c, CHUNK * (c + 1))
        for h in range(HEADS):
            b = HEADS * c + h
            qs[b] = qkv_ref[rows, 128 * h:128 * (h + 1)]
            ks[b] = qkv_ref[rows, 1024 + 128 * h:1024 + 128 * (h + 1)]
            vs[b] = qkv_ref[rows, 2048 + 128 * h:2048 + 128 * (h + 1)]
            gs[b] = jnp.broadcast_to(_col(gc[rows], GA_LANE + h, lane), (CHUNK, 128))
            bs[b] = jnp.broadcast_to(_col(gbv[rows], GB_LANE + h, lane), (CHUNK, 128))


def _block_tri(rows, transpose=False):
    ri = lax.broadcasted_iota(jnp.int32, (rows, rows), 0)
    ci = lax.broadcasted_iota(jnp.int32, (rows, rows), 1)
    same = (ri >> 6) == (ci >> 6)
    return jnp.where(same & ((ci >= ri) if transpose else (ri >= ci)), 1.0, 0.0)


def _local_b(q, k, v, g128, b128):
    ii = lax.broadcasted_iota(jnp.int32, (1, CHUNK, CHUNK), 1)
    jj = lax.broadcasted_iota(jnp.int32, (1, CHUNK, CHUNK), 2)
    lower, strict, eye = ii >= jj, ii > jj, ii == jj
    g64 = g128[:, :, :CHUNK]
    grow = jnp.sum(jnp.where(eye, g64, 0.0), axis=1, keepdims=True)
    decay = jnp.where(lower, jnp.exp(jnp.where(lower, g64 - grow, 0.0)), 0.0)
    kb = k * b128
    vb = v * b128
    mm = lax.dot_general(kb.astype(BF16), k.astype(BF16), NT_B, preferred_element_type=F32)
    lmat = jnp.where(strict, mm * decay, 0.0)
    egc = jnp.exp(g128)
    kbg = kb * egc
    qk = lax.dot_general(q.astype(BF16), k.astype(BF16), NT_B, preferred_element_type=F32)
    row = lax.broadcasted_iota(jnp.int32, (1, CHUNK, 128), 1)
    glast = jnp.sum(jnp.where(row == CHUNK - 1, g128, 0.0), axis=1, keepdims=True)
    ekd = jnp.exp(glast - g128)
    return dict(decay=decay, kb=kb, vb=vb, lmat=lmat, egc=egc, kbg=kbg, amat=qk * decay, qd=q * egc, ekd=ekd,
                kd=k * ekd, gl=jnp.exp(glast), lower=lower, strict=strict, eye=eye)


def _bdot(a, b, dims):
    return lax.dot_general(a.astype(BF16), b.astype(BF16), dims, preferred_element_type=F32)


def _split(a):
    hi = a.astype(BF16)
    return hi, (a - hi.astype(F32)).astype(BF16)


def _bdot_hi(a, b, dims):
    ah, al = _split(a)
    bh, bl = _split(b)
    d = lambda x, y: lax.dot_general(x, y, dims, preferred_element_type=F32)
    return d(ah, bh) + d(ah, bl) + d(al, bh)


def _gdn_pre(qkv, gbeta):
    s = qkv.shape[0]
    n = s // CHUNK
    cb = min(GDN_PAR_CHUNKS, n)
    nb = cb * HEADS
    rows = cb * CHUNK

    def body(qkv_ref, gb_ref, u_ref, w_ref, qd_ref, kd_ref, a_ref, t_ref, gl_ref, qs, ks, vs, gs, bs):
        gbv = gb_ref[...]
        gc = _dot_hi(_block_tri(rows), gbv)
        _gather_heads(qkv_ref, gc, gbv, qs, ks, vs, gs, bs, cb)
        c = _local_b(qs[...], ks[...], vs[...], gs[...], bs[...])
        pw = -c["lmat"]
        tinv = jnp.where(c["eye"], 1.0, 0.0) + pw
        for _ in range(5):
            pw = _bdot_hi(pw, pw, NN_B)
            tinv = tinv + _bdot_hi(tinv, pw, NN_B)
        u_ref[...] = _bdot_hi(tinv, c["vb"], NN_B)
        w_ref[...] = _bdot_hi(tinv, c["kbg"], NN_B).astype(BF16)
        qd_ref[...] = c["qd"].astype(BF16)
        kd_ref[...] = c["kd"].astype(BF16)
        a_ref[...] = c["amat"].astype(BF16)
        t_ref[...] = tinv
        gl_ref[...] = c["gl"]

    b3 = lambda d: pl.BlockSpec((nb, CHUNK, d), lambda i: (i, 0, 0))
    nt = n * HEADS
    return _call(body, name="gdn_pre", grid=(n // cb,),
                 in_specs=[_rows(rows, 3072), _rows(rows, 128)],
                 out_specs=[b3(128), b3(128), b3(128), b3(128), b3(CHUNK), b3(CHUNK),
                            pl.BlockSpec((nb, 1, 128), lambda i: (i, 0, 0))],
                 out_shape=[_sds((nt, CHUNK, 128), F32), _sds((nt, CHUNK, 128), BF16), _sds((nt, CHUNK, 128), BF16),
                            _sds((nt, CHUNK, 128), BF16), _sds((nt, CHUNK, CHUNK), BF16), _sds((nt, CHUNK, CHUNK), F32),
                            _sds((nt, 1, 128), F32)],
                 scratch=[pltpu.VMEM((nb, CHUNK, 128), F32)] * 5, sem=("parallel",), args=(qkv, gbeta))


def _gdn_scan_fwd(u, w, qd, kd, amat, gl):
    nt = u.shape[0]
    n = nt // HEADS
    cs = min(GDN_SEQ_CHUNKS, n)

    def body(u_ref, w_ref, qd_ref, kd_ref, a_ref, gl_ref, o_ref, st_ref, state):
        @pl.when(pl.program_id(0) == 0)
        def _():
            state[...] = jnp.zeros_like(state)

        for c in range(cs):
            sl = slice(HEADS * c, HEADS * (c + 1))
            st = state[...]
            stb = st.astype(BF16)
            st_ref[sl] = stb
            vn = u_ref[sl] - lax.dot_general(w_ref[sl], stb, NN_B, preferred_element_type=F32)
            vnb = vn.astype(BF16)
            o = (lax.dot_general(qd_ref[sl], stb, NN_B, preferred_element_type=F32)
                 + lax.dot_general(a_ref[sl], vnb, NN_B, preferred_element_type=F32))
            state[...] = st * gl_ref[sl] + lax.dot_general(kd_ref[sl], vnb, TN_B, preferred_element_type=F32)
            for h in range(HEADS):
                o_ref[CHUNK * c:CHUNK * (c + 1), 128 * h:128 * (h + 1)] = o[h]

    b3 = lambda d: pl.BlockSpec((cs * HEADS, CHUNK, d), lambda i: (i, 0, 0))
    return _call(body, name="gdn_scan_fwd", grid=(n // cs,),
                 in_specs=[b3(128), b3(128), b3(128), b3(128), b3(CHUNK), pl.BlockSpec((cs * HEADS, 1, 128), lambda i: (i, 0, 0))],
                 out_specs=[_rows(cs * CHUNK, 1024), pl.BlockSpec((cs * HEADS, 128, 128), lambda i: (i, 0, 0))],
                 out_shape=[_sds((n * CHUNK, 1024), F32), _sds((nt, 128, 128), BF16)],
                 scratch=[pltpu.VMEM((HEADS, 128, 128), F32)], args=(u, w, qd, kd, amat, gl))


def _gdn_scan_bwd(w, qd, kd, amat, gl, do):
    nt = w.shape[0]
    n = nt // HEADS
    cs = min(GDN_SEQ_CHUNKS, n)
    ng = n // cs

    def body(w_ref, qd_ref, kd_ref, a_ref, gl_ref, do_ref, ds_ref, dstate, dos):
        @pl.when(pl.program_id(0) == 0)
        def _():
            dstate[...] = jnp.zeros_like(dstate)

        for c in reversed(range(cs)):
            sl = slice(HEADS * c, HEADS * (c + 1))
            for h in range(HEADS):
                dos[h] = do_ref[CHUNK * c:CHUNK * (c + 1), 128 * h:128 * (h + 1)].astype(BF16)
            dob = dos[...]
            dst = dstate[...]
            dstb = dst.astype(BF16)
            ds_ref[sl] = dstb
            dvn = (lax.dot_general(a_ref[sl], dob, TN_B, preferred_element_type=F32)
                   + lax.dot_general(kd_ref[sl], dstb, NN_B, preferred_element_type=F32))
            dstate[...] = (lax.dot_general(qd_ref[sl], dob, TN_B, preferred_element_type=F32) + gl_ref[sl] * dst
                           - lax.dot_general(w_ref[sl], dvn.astype(BF16), TN_B, preferred_element_type=F32))

    b3 = lambda d: pl.BlockSpec((cs * HEADS, CHUNK, d), lambda i: (ng - 1 - i, 0, 0))
    return _call(body, name="gdn_scan_bwd", grid=(ng,),
                 in_specs=[b3(128), b3(128), b3(128), b3(CHUNK), pl.BlockSpec((cs * HEADS, 1, 128), lambda i: (ng - 1 - i, 0, 0)),
                           pl.BlockSpec((cs * CHUNK, 1024), lambda i: (ng - 1 - i, 0))],
                 out_specs=pl.BlockSpec((cs * HEADS, 128, 128), lambda i: (ng - 1 - i, 0, 0)),
                 out_shape=_sds((nt, 128, 128), BF16),
                 scratch=[pltpu.VMEM((HEADS, 128, 128), F32), pltpu.VMEM((HEADS, CHUNK, 128), BF16)],
                 args=(w, qd, kd, amat, gl, do))


def _gdn_post_bwd(qkv, gbeta, u, w, tinv, states, dstates, do):
    s = qkv.shape[0]
    n = s // CHUNK
    cb = min(GDN_PAR_CHUNKS, n)
    nb = cb * HEADS
    rows = cb * CHUNK

    def body(qkv_ref, gb_ref, u_ref, w_ref, t_ref, st_ref, ds_ref, do_ref, dqkv_ref, dgb_ref, qs, ks, vs, gs, bs, dos):
        gbv = gb_ref[...]
        gc = _dot_hi(_block_tri(rows), gbv)
        _gather_heads(qkv_ref, gc, gbv, qs, ks, vs, gs, bs, cb)
        for c in range(cb):
            for h in range(HEADS):
                dos[HEADS * c + h] = do_ref[CHUNK * c:CHUNK * (c + 1), 128 * h:128 * (h + 1)]
        q, k, v, b128 = qs[...], ks[...], vs[...], bs[...]
        c = _local_b(q, k, v, gs[...], b128)
        tinv, st, dst, dov = t_ref[...], st_ref[...], ds_ref[...], dos[...]
        wv = w_ref[...]
        vn = u_ref[...] - _bdot(wv, st, NN_B)
        dvn = _bdot(c["amat"], dov, TN_B) + _bdot(c["kd"], dst, NN_B)
        damat = jnp.where(c["lower"], _bdot(dov, vn, NT_B), 0.0)
        dqd = _bdot(dov, st, NT_B)
        dkd = _bdot(vn, dst, NT_B)
        dw = -_bdot(dvn, st, NT_B)
        dgl = jnp.sum(jnp.sum(st.astype(F32) * dst.astype(F32), axis=1, keepdims=True), axis=-1, keepdims=True)
        dvb = _bdot_hi(tinv, dvn, TN_B)
        dkbg = _bdot_hi(tinv, dw, TN_B)
        dtinv = _bdot_hi(dvn, c["vb"], NT_B) + _bdot_hi(dw, c["kbg"], NT_B)
        dl = -_bdot_hi(_bdot_hi(tinv, dtinv, TN_B), tinv, NT_B)
        dl = jnp.where(c["strict"], dl, 0.0)
        dmm = dl * c["decay"]
        dqk = damat * c["decay"]
        wmat = dl * c["lmat"] + damat * c["amat"]
        wcol = jnp.sum(wmat, axis=1, keepdims=True)
        dgc = jnp.sum(wmat, axis=-1, keepdims=True) - jnp.sum(jnp.where(c["eye"], wcol, 0.0), axis=-1, keepdims=True)
        dkb = _bdot(dmm, k, NN_B) + dkbg * c["egc"]
        dk = _bdot(dmm, c["kb"], TN_B) + _bdot(dqk, q, TN_B) + dkd * c["ekd"] + dkb * b128
        dq = _bdot(dqk, k, NN_B) + dqd * c["egc"]
        tmp = jnp.sum(dkd * c["kd"], axis=-1, keepdims=True)
        dgc = (dgc + jnp.sum(dkbg * c["kbg"], axis=-1, keepdims=True) + jnp.sum(dqd * c["qd"], axis=-1, keepdims=True)
               - tmp)
        dglast = jnp.sum(tmp, axis=1, keepdims=True) + dgl * c["gl"][:, :, :1]
        row1 = lax.broadcasted_iota(jnp.int32, (1, CHUNK, 1), 1)
        dgc = dgc + jnp.where(row1 == CHUNK - 1, dglast, 0.0)
        db = jnp.sum(dkb * k, axis=-1, keepdims=True) + jnp.sum(dvb * v, axis=-1, keepdims=True)
        dv = dvb * b128
        lane = lax.broadcasted_iota(jnp.int32, (CHUNK, 128), 1)
        parts = []
        for cc in range(cb):
            acc = jnp.zeros((CHUNK, 128), F32)
            for h in range(HEADS):
                bi = HEADS * cc + h
                rs = slice(CHUNK * cc, CHUNK * (cc + 1))
                dqkv_ref[rs, 128 * h:128 * (h + 1)] = dq[bi]
                dqkv_ref[rs, 1024 + 128 * h:1024 + 128 * (h + 1)] = dk[bi]
                dqkv_ref[rs, 2048 + 128 * h:2048 + 128 * (h + 1)] = dv[bi]
                acc = acc + jnp.where(lane == GA_LANE + h, dgc[bi], 0.0)
            parts.append(acc)
        dgc_all = jnp.concatenate(parts, axis=0)
        dg_all = _dot_hi(_block_tri(rows, transpose=True), dgc_all)
        for cc in range(cb):
            acc = dg_all[CHUNK * cc:CHUNK * (cc + 1)]
            for h in range(HEADS):
                acc = acc + jnp.where(lane == GB_LANE + h, db[HEADS * cc + h], 0.0)
            dgb_ref[CHUNK * cc:CHUNK * (cc + 1), :] = acc

    b3 = lambda d1, d2: pl.BlockSpec((nb, d1, d2), lambda i: (i, 0, 0))
    return _call(body, name="gdn_post_bwd", grid=(n // cb,),
                 in_specs=[_rows(rows, 3072), _rows(rows, 128), b3(CHUNK, 128), b3(CHUNK, 128), b3(CHUNK, CHUNK),
                           b3(128, 128), b3(128, 128), _rows(rows, 1024)],
                 out_specs=[_rows(rows, 3072), _rows(rows, 128)],
                 out_shape=[_sds((s, 3072), F32), _sds((s, 128), F32)],
                 scratch=[pltpu.VMEM((nb, CHUNK, 128), F32)] * 6, sem=("parallel",),
                 args=(qkv, gbeta, u, w, tinv, states, dstates, do))


def _mix_fwd(o_mla, o_gdn, proj, out_gain):
    s = proj.shape[0]
    tm = min(TILES["row"], s)

    def body(om_ref, og_ref, mg_ref, gg_ref, g_ref, o_ref):
        o_ref[:, :1024] = (om_ref[...] * _silu(mg_ref[...])).astype(BF16)
        for h in range(HEADS):
            sl = slice(128 * h, 128 * (h + 1))
            og = og_ref[:, sl]
            on = og * _rms(og, HEAD_DIM) * g_ref[...]
            o_ref[:, 1024 + 128 * h:1024 + 128 * (h + 1)] = (on * _silu(gg_ref[:, sl])).astype(BF16)

    return _call(body, name="mix_fwd", grid=(s // tm,),
                 in_specs=[_rows(tm, 1024), _rows(tm, 1024), _rows(tm, 1024, 1), _rows(tm, 1024, 5), _full((1, 128))],
                 out_specs=_rows(tm, 2048), out_shape=_sds((s, 2048), BF16), args=(o_mla, o_gdn, proj, proj, out_gain))


def _mix_bwd(o_mla, o_gdn, proj, out_gain, dmixed):
    s = proj.shape[0]
    tm = min(TILES["row"], s)

    def body(om_ref, og_ref, mg_ref, gg_ref, g_ref, dm_ref, dg_ref, dom_ref, dog_ref, dmg_ref, dgg_ref, ag_ref):
        @pl.when(pl.program_id(0) == 0)
        def _():
            ag_ref[...] = jnp.zeros_like(ag_ref)

        mg = mg_ref[...]
        dom_ref[...] = dm_ref[...] * _silu(mg)
        dmg_ref[...] = (dm_ref[...] * om_ref[...] * _dsilu(mg)).astype(BF16)
        for h in range(HEADS):
            sl = slice(128 * h, 128 * (h + 1))
            og, gg, d = og_ref[:, sl], gg_ref[:, sl], dg_ref[:, sl]
            on = og * _rms(og, HEAD_DIM) * g_ref[...]
            dgg_ref[:, sl] = (d * on * _dsilu(gg)).astype(BF16)
            dx, gpart = _rms_bwd(og, g_ref[...], d * _silu(gg), HEAD_DIM)
            dog_ref[:, sl] = dx
            ag_ref[...] += _acc8(gpart)

    return _call(body, name="mix_bwd", grid=(s // tm,),
                 in_specs=[_rows(tm, 1024), _rows(tm, 1024), _rows(tm, 1024, 1), _rows(tm, 1024, 5), _full((1, 128)),
                           _rows(tm, 1024, 0), _rows(tm, 1024, 1)],
                 out_specs=[_rows(tm, 1024), _rows(tm, 1024), _rows(tm, 1024), _rows(tm, 1024), _full((8, 128))],
                 out_shape=[_sds((s, 1024), F32), _sds((s, 1024), F32), _sds((s, 1024), BF16), _sds((s, 1024), BF16),
                            _sds((8, 128), F32)],
                 args=(o_mla, o_gdn, proj, proj, out_gain, dmixed, dmixed))


def _out_fwd(mixed, w_out, x, target):
    s = x.shape[0]
    tm = min(TILES["mm"], s)
    tn = min(TILES["mm"], D_MODEL)

    def body(m_ref, w_ref, x_ref, t_ref, dy_ref, acc_ref):
        err = x_ref[...] + _dot(m_ref[...], w_ref[...]) - t_ref[...]
        dy_ref[...] = err * (1.0 / D_MODEL)

        @pl.when(pl.program_id(1) == 0)
        def _():
            acc_ref[...] = jnp.zeros_like(acc_ref)

        acc_ref[...] += _acc8(err * err)

    return _call(body, name="out_fwd", grid=(D_MODEL // tn, s // tm),
                 in_specs=[pl.BlockSpec((tm, D_MODEL), lambda j, i: (i, 0)), pl.BlockSpec((D_MODEL, tn), lambda j, i: (0, j)),
                           pl.BlockSpec((tm, tn), lambda j, i: (i, j)), pl.BlockSpec((tm, tn), lambda j, i: (i, j))],
                 out_specs=[pl.BlockSpec((tm, tn), lambda j, i: (i, j)), pl.BlockSpec((8, tn), lambda j, i: (0, j))],
                 out_shape=[_sds((s, D_MODEL), F32), _sds((8, D_MODEL), F32)], args=(mixed, w_out, x, target))


def _row_tile(r, c):
    if r % 8 != 0:
        return r
    t = 8
    while r % (2 * t) == 0 and 2 * t * c * 4 <= (1 << 20):
        t *= 2
    return t


def _sum_arrays(parts, name, also_bf16=False):
    r, c = parts[0].shape
    tr = _row_tile(r, c)
    n = len(parts)

    def body(*refs):
        acc = refs[0][...].astype(F32)
        for p_ref in refs[1:n]:
            acc = acc + p_ref[...].astype(F32)
        refs[n][...] = acc
        if also_bf16:
            refs[n + 1][...] = acc.astype(BF16)

    nout = 2 if also_bf16 else 1
    out = _call(body, name=name, grid=(r // tr,), in_specs=[_rows(tr, c)] * n, out_specs=[_rows(tr, c)] * nout,
                out_shape=[_sds((r, c), F32), _sds((r, c), BF16)][:nout], args=tuple(parts))
    return out if also_bf16 else out[0]


def _adamw(w, g, m, v, name):
    r, c = w.shape
    tr = _row_tile(r, c)
    c1 = 1.0 - ADAM_B1 ** ADAM_STEP
    c2 = 1.0 - ADAM_B2 ** ADAM_STEP

    def body(w_ref, g_ref, m_ref, v_ref, d_ref, nm_ref, nv_ref):
        gv = g_ref[...]
        nm = ADAM_B1 * m_ref[...] + (1.0 - ADAM_B1) * gv
        nv = ADAM_B2 * v_ref[...] + (1.0 - ADAM_B2) * (gv * gv)
        nm_ref[...] = nm
        nv_ref[...] = nv
        d_ref[...] = -ADAM_LR * ((nm / c1) / (jnp.sqrt(nv / c2) + ADAM_EPS) + ADAM_WD * w_ref[...])

    return _call(body, name=name, grid=(r // tr,), in_specs=[_rows(tr, c)] * 4, out_specs=[_rows(tr, c)] * 3,
                 out_shape=[_sds((r, c), F32)] * 3, args=(w, g, m, v))


ANY = pl.BlockSpec(memory_space=pl.ANY)
CHIP_FLIPS = ((1, 0), (0, 1), (1, 1))


def _comm_call(body, *, name, n_in, out_shape, scratch):
    def kfn(*refs):
        body(*refs)
    return pl.pallas_call(kfn, name=name, in_specs=[ANY] * n_in, out_specs=[ANY] * len(out_shape), out_shape=out_shape,
                          scratch_shapes=list(scratch),
                          compiler_params=pltpu.CompilerParams(has_side_effects=True))


def _all_gather_chips(shards):
    na = len(shards)

    def body(*refs):
        ins, outs = refs[:na], refs[na:2 * na]
        send_sems, recv_sems, fwd_send, fwd_recv, local_sems = refs[2 * na:]
        x, y, c = lax.axis_index("x"), lax.axis_index("y"), lax.axis_index("c")
        my_k = 2 * x + y
        pending, forwards = [], []
        for a in range(na):
            cp = pltpu.make_async_copy(ins[a], outs[a].at[my_k], local_sems.at[a])
            cp.start()
            pending.append(cp)
            rows = ins[a].shape[0]
            split = rows % 32 == 0
            for r, (fx, fy) in enumerate(CHIP_FLIPS):
                px, py = x ^ fx, y ^ fy
                if split:
                    mine = pl.ds(pl.multiple_of(c * (rows // 2), 16), rows // 2)
                    other = pl.ds(pl.multiple_of((1 - c) * (rows // 2), 16), rows // 2)
                    rc = pltpu.make_async_remote_copy(
                        src_ref=ins[a].at[mine], dst_ref=outs[a].at[my_k, mine], send_sem=send_sems.at[a, r],
                        recv_sem=recv_sems.at[a, r], device_id=(px, py, c), device_id_type=MESH)
                    landed = outs[a].at[2 * px + py, mine]
                    fw = pltpu.make_async_remote_copy(
                        src_ref=landed, dst_ref=landed, send_sem=fwd_send.at[a, r], recv_sem=fwd_recv.at[a, r],
                        device_id=(x, y, 1 - c), device_id_type=MESH)
                    from_sib = outs[a].at[2 * px + py, other]
                    fw_in = pltpu.make_async_remote_copy(
                        src_ref=from_sib, dst_ref=from_sib, send_sem=fwd_send.at[a, r], recv_sem=fwd_recv.at[a, r],
                        device_id=(x, y, 1 - c), device_id_type=MESH)
                    forwards.append((rc, fw, fw_in))
                else:
                    rc = pltpu.make_async_remote_copy(
                        src_ref=ins[a], dst_ref=outs[a].at[my_k], send_sem=send_sems.at[a, r],
                        recv_sem=recv_sems.at[a, r], device_id=(px, py, c), device_id_type=MESH)
                    pending.append(rc)
                rc.start()
        for rc, fw, _ in forwards:
            rc.wait_recv()
            fw.start()
        for rc, fw, fw_in in forwards:
            rc.wait_send()
            fw.wait_send()
            fw_in.wait_recv()
        for cp in pending:
            cp.wait()

    out_shape = [_sds((4,) + a.shape, a.dtype) for a in shards]
    sem = pltpu.SemaphoreType.DMA((na, 3))
    return _comm_call(body, name="all_gather_weights", n_in=na, out_shape=out_shape,
                      scratch=[sem, sem, sem, sem, pltpu.SemaphoreType.DMA((na,))])(*shards)


def _all_reduce_small(vec):
    r = vec.shape[0]

    def body(v_ref, o_ref, gath, send_sems, recv_sems):
        x, y, c = lax.axis_index("x"), lax.axis_index("y"), lax.axis_index("c")
        me = 4 * x + 2 * y + c
        gath[me] = v_ref[...]
        copies = []
        for rel in range(1, 8):
            fx, fy, fc = (rel >> 2) & 1, (rel >> 1) & 1, rel & 1
            rc = pltpu.make_async_remote_copy(
                src_ref=v_ref, dst_ref=gath.at[me], send_sem=send_sems.at[rel - 1], recv_sem=recv_sems.at[rel - 1],
                device_id=(x ^ fx, y ^ fy, c ^ fc), device_id_type=MESH)
            rc.start()
            copies.append(rc)
        for rc in copies:
            rc.wait()
        acc = gath[0]
        for d in range(1, 8):
            acc = acc + gath[d]
        o_ref[...] = acc

    def kfn(*refs):
        body(*refs)
    vm = pl.BlockSpec(memory_space=pltpu.VMEM)
    return pl.pallas_call(kfn, name="all_reduce_small", in_specs=[vm], out_specs=vm, out_shape=_sds((r, 128), F32),
                          scratch_shapes=[pltpu.VMEM((8, r, 128), F32), pltpu.SemaphoreType.DMA((7,)),
                                          pltpu.SemaphoreType.DMA((7,))],
                          compiler_params=pltpu.CompilerParams(has_side_effects=True))(vec)


def _exchange_halves(arrs):
    na = len(arrs)

    def body(*refs):
        ins, outs = refs[:na], refs[na:2 * na]
        send_sems, recv_sems = refs[2 * na:]
        x, y, c = lax.axis_index("x"), lax.axis_index("y"), lax.axis_index("c")
        copies = []
        for a in range(na):
            half = ins[a].shape[1] // 2
            src = ins[a].at[:, pl.ds(pl.multiple_of((1 - c) * half, 8), half), :]
            rc = pltpu.make_async_remote_copy(src_ref=src, dst_ref=outs[a], send_sem=send_sems.at[a],
                                              recv_sem=recv_sems.at[a], device_id=(x, y, 1 - c), device_id_type=MESH)
            rc.start()
            copies.append(rc)
        for rc in copies:
            rc.wait()

    out_shape = [_sds((4, a.shape[1] // 2, a.shape[2]), F32) for a in arrs]
    return _comm_call(body, name="rs_pair_exchange", n_in=na, out_shape=out_shape,
                      scratch=[pltpu.SemaphoreType.DMA((na,)), pltpu.SemaphoreType.DMA((na,))])(*arrs)


def _scatter_to_chips(arrs):
    na = len(arrs)

    def body(*refs):
        ins, outs = refs[:na], refs[na:2 * na]
        send_sems, recv_sems = refs[2 * na:]
        x, y, c = lax.axis_index("x"), lax.axis_index("y"), lax.axis_index("c")
        copies = []
        for a in range(na):
            for r, (fx, fy) in enumerate(CHIP_FLIPS):
                px, py = x ^ fx, y ^ fy
                rc = pltpu.make_async_remote_copy(
                    src_ref=ins[a].at[2 * px + py], dst_ref=outs[a].at[r], send_sem=send_sems.at[a, r],
                    recv_sem=recv_sems.at[a, r], device_id=(px, py, c), device_id_type=MESH)
                rc.start()
                copies.append(rc)
        for rc in copies:
            rc.wait()

    out_shape = [_sds((3,) + a.shape[1:], a.dtype) for a in arrs]
    return _comm_call(body, name="rs_chip_scatter", n_in=na, out_shape=out_shape,
                      scratch=[pltpu.SemaphoreType.DMA((na, 3)), pltpu.SemaphoreType.DMA((na, 3))])(*arrs)


def _join_halves(arrs):
    na = len(arrs)

    def body(*refs):
        ins, outs = refs[:na], refs[na:2 * na]
        send_sems, recv_sems, local_sems = refs[2 * na:]
        x, y, c = lax.axis_index("x"), lax.axis_index("y"), lax.axis_index("c")
        copies = []
        for a in range(na):
            half = ins[a].shape[0]
            dst = outs[a].at[pl.ds(pl.multiple_of(c * half, 8), half), :]
            cp = pltpu.make_async_copy(ins[a], dst, local_sems.at[a])
            cp.start()
            rc = pltpu.make_async_remote_copy(src_ref=ins[a], dst_ref=dst, send_sem=send_sems.at[a],
                                              recv_sem=recv_sems.at[a], device_id=(x, y, 1 - c), device_id_type=MESH)
            rc.start()
            copies += [cp, rc]
        for cp in copies:
            cp.wait()

    out_shape = [_sds((2 * a.shape[0], a.shape[1]), F32) for a in arrs]
    return _comm_call(body, name="rs_pair_join", n_in=na, out_shape=out_shape,
                      scratch=[pltpu.SemaphoreType.DMA((na,)), pltpu.SemaphoreType.DMA((na,)),
                               pltpu.SemaphoreType.DMA((na,))])(*arrs)


def _rs_pair_stage(grads):
    c = lax.axis_index("c")
    got = _exchange_halves(grads)
    pair, pair_bf16 = [], []
    for a, (g, o) in enumerate(zip(grads, got)):
        half = g.shape[1] // 2
        mine = lax.dynamic_slice_in_dim(g, c * half, half, axis=1)
        p32, p16 = _sum_arrays([mine.reshape(4 * half, -1), o.reshape(4 * half, -1)], f"rs_pair_sum_{a}", also_bf16=True)
        pair.append(p32.reshape(4, half, -1))
        pair_bf16.append(p16.reshape(4, half, -1))
    return pair, pair_bf16


def _rs_chip_stage(pair, recv):
    k_me = 2 * lax.axis_index("x") + lax.axis_index("y")
    halves = []
    for a, (p, rv) in enumerate(zip(pair, recv)):
        own = lax.dynamic_index_in_dim(p, k_me, 0, keepdims=False)
        halves.append(_sum_arrays([own, rv[0], rv[1], rv[2]], f"rs_chip_sum_{a}"))
    return _join_halves(halves)


def _reduce_scatter(grads):
    pair, pair_bf16 = _rs_pair_stage(grads)
    return _rs_chip_stage(pair, _scatter_to_chips(pair_bf16))


def _matmul_nt_scatter(a, b, send, *, name, tm, tn):
    m, kdim = a.shape
    n = b.shape[0]
    ni, nj = m // tm, n // tn
    na = len(send)

    def body(a_ref, b_ref, *rest):
        send_refs, o_ref, recv_refs = rest[:na], rest[na], rest[na + 1:2 * na + 1]
        send_sems, recv_sems = rest[2 * na + 1:]
        i, j = pl.program_id(0), pl.program_id(1)

        def copies():
            x, y, c = lax.axis_index("x"), lax.axis_index("y"), lax.axis_index("c")
            out = []
            for s_i in range(na):
                for r, (fx, fy) in enumerate(CHIP_FLIPS):
                    px, py = x ^ fx, y ^ fy
                    out.append(pltpu.make_async_remote_copy(
                        src_ref=send_refs[s_i].at[2 * px + py], dst_ref=recv_refs[s_i].at[r],
                        send_sem=send_sems.at[s_i, r], recv_sem=recv_sems.at[s_i, r], device_id=(px, py, c),
                        device_id_type=MESH))
            return out

        @pl.when((i == 0) & (j == 0))
        def _():
            for cp in copies():
                cp.start()

        o_ref[...] = _dot(a_ref[...], b_ref[...], NT)

        @pl.when((i == ni - 1) & (j == nj - 1))
        def _():
            for cp in copies():
                cp.wait()

    def kfn(*refs):
        body(*refs)
    sem = pltpu.SemaphoreType.DMA((na, 3))
    out = pl.pallas_call(
        kfn, name=name, grid=(ni, nj),
        in_specs=[pl.BlockSpec((tm, kdim), lambda i, j: (i, 0)), pl.BlockSpec((tn, kdim), lambda i, j: (j, 0))] + [ANY] * na,
        out_specs=[pl.BlockSpec((tm, tn), lambda i, j: (i, j))] + [ANY] * na,
        out_shape=[_sds((m, n), F32)] + [_sds((3,) + s_a.shape[1:], s_a.dtype) for s_a in send],
        scratch_shapes=[sem, sem],
        compiler_params=pltpu.CompilerParams(dimension_semantics=("arbitrary", "arbitrary"),
                                             vmem_limit_bytes=VMEM_LIMIT_V7X, has_side_effects=True))(a, b, *send)
    return out[0], list(out[1:])


def _pad_w_in(w):
    z = jnp.zeros((w.shape[0], 1024 - 848), w.dtype)
    return jnp.concatenate([w[:, 0:832], w[:, 4928:4944], z, w[:, 832:4928], w[:, 4944:5968]], axis=1)


def _unpad_w_in(g):
    return jnp.concatenate([g[:, 0:832], g[:, 1024:5120], g[:, 832:848], g[:, 5120:6144]], axis=1)


def _pad_heads(w):
    r = w.shape[0]
    return jnp.pad(w.reshape(r, HEADS, QK_DIM), ((0, 0), (0, 0), (0, QK_PAD - QK_DIM))).reshape(r, HEADS * QK_PAD)


def _unpad_heads(w):
    r = w.shape[0]
    return w.reshape(r, HEADS, QK_PAD)[:, :, :QK_DIM].reshape(r, HEADS * QK_DIM)


def _cols_to_blocks(w):
    r = w.shape[0]
    return w.reshape(r, 4, -1).transpose(1, 0, 2)


def _blocks_to_cols(w):
    return w.transpose(1, 0, 2).reshape(w.shape[1], -1)


SMALL_ROWS = {"norm_gain": (0, 2048), "mla_q_a_gain": (16, 512), "mla_kv_a_gain": (20, 256),
              "mla_q_norm_gain": (22, 192), "mla_k_norm_gain": (24, 192), "gdn_a_log": (26, 8),
              "gdn_dt_bias": (27, 8), "gdn_out_norm_gain": (28, 128)}
LOSS_ROW = 29
SMALL_PACK_ROWS = 32
CONV_ROW = 32


def _pack_small(vals, loss=None):
    rows = []
    at = 0
    for name, (row, size) in SMALL_ROWS.items():
        assert row == at
        nr = -(-size // 128)
        rows.append(jnp.pad(vals[name].reshape(-1).astype(F32), (0, nr * 128 - size)).reshape(nr, 128))
        at += nr
    assert at == LOSS_ROW
    if loss is not None:
        rows.append(jnp.pad(loss.reshape(1, 1), ((0, 0), (0, 127))))
        at += 1
    rows.append(jnp.zeros((SMALL_PACK_ROWS - at, 128), F32))
    return jnp.concatenate(rows, axis=0)


def _unpack_small(pack, name):
    row, size = SMALL_ROWS[name]
    nr = -(-size // 128)
    return pack[row:row + nr].reshape(-1)[:size].reshape(1, size)


def _local_step(x, positions, target, norm_gain, w_in_p, q_a_gain, kv_a_gain, w_uq_p, w_ukv, q_norm_gain,
                k_norm_gain, conv_w, a_log, dt_bias, out_gain, w_out, scatter_hook=None):
    half = HALF_ROPE
    inv_freq = jnp.power(10000.0, -jnp.arange(half, dtype=F32) / half)
    ang = positions.astype(F32)[:, None] * inv_freq
    cos, sin = jnp.cos(ang), jnp.sin(ang)
    zpad = jnp.zeros((x.shape[0], 64), F32)
    cs = jnp.concatenate([cos, cos, zpad], axis=1)
    sn = jnp.concatenate([-sin, sin, zpad], axis=1)
    gq = jnp.pad(q_norm_gain.reshape(1, QK_DIM), ((0, 0), (0, QK_PAD - QK_DIM)))
    gk = jnp.pad(k_norm_gain.reshape(1, QK_DIM), ((0, 0), (0, QK_PAD - QK_DIM)))
    lane_pad = ((0, 0), (GA_LANE, 128 - GA_LANE - HEADS))
    alog128 = jnp.pad(a_log.reshape(1, HEADS), lane_pad)
    dtb128 = jnp.pad(dt_bias.reshape(1, HEADS), lane_pad)
    ng, qag, kvag, og = (norm_gain.reshape(1, -1), q_a_gain.reshape(1, -1), kv_a_gain.reshape(1, -1),
                         out_gain.reshape(1, -1))

    xn = _norm1_fwd(x, ng)
    proj = _matmul(xn, w_in_p, mode="nn", out_dtype=F32, name="in_proj")
    cqn, ckvn = _mla_a_norm(proj, qag, kvag)
    q_pre = _matmul(cqn, w_uq_p, mode="nn", out_dtype=F32, name="q_up")
    kv_pre = _matmul(ckvn, w_ukv, mode="nn", out_dtype=F32, name="kv_up")
    q, k, v = _mla_post_fwd(q_pre, kv_pre, proj, cs, sn, gq, gk)
    o_mla, lse = _attn_fwd(q, k, v)
    qkv = _gdn_conv_fwd(proj, conv_w)
    gbeta = _gdn_gates_fwd(proj, alog128, dtb128)
    g_u, g_w, g_qd, g_kd, g_a, g_t, g_gl = _gdn_pre(qkv, gbeta)
    o_gdn, states = _gdn_scan_fwd(g_u, g_w, g_qd, g_kd, g_a, g_gl)
    mixed = _mix_fwd(o_mla, o_gdn, proj, og)
    dy, sq = _out_fwd(mixed, w_out, x, target)

    dmixed = _matmul(dy, w_out, mode="nt", out_dtype=F32, name="d_mixed")
    d_w_out = _matmul(mixed, dy, mode="tn", out_dtype=F32, name="d_w_out", tk=1024)
    do_mla, do_gdn, dmg, dgg, d_out_gain = _mix_bwd(o_mla, o_gdn, proj, og, dmixed)
    dq, dk, dv = _attn_bwd(q, k, v, o_mla, lse, do_mla)
    dq_pre, dkv_pre, dkr, d_gq, d_gk = _mla_post_bwd(q_pre, kv_pre, proj, cs, sn, gq, gk, dq, dk, dv)
    d_w_uq_p = _matmul(cqn, dq_pre, mode="tn", out_dtype=F32, name="d_w_uq", tk=1024)
    d_w_ukv = _matmul(ckvn, dkv_pre, mode="tn", out_dtype=F32, name="d_w_ukv", tk=1024)
    dcqn = _matmul(dq_pre, w_uq_p, mode="nt", out_dtype=F32, name="d_cqn")
    dckvn = _matmul(dkv_pre, w_ukv, mode="nt", out_dtype=F32, name="d_ckvn")
    dcq, dckv, d_qag, d_kvag = _mla_a_norm_bwd(proj, qag, kvag, dcqn, dckvn)
    dstates = _gdn_scan_bwd(g_w, g_qd, g_kd, g_a, g_gl, do_gdn)
    dqkv, dgbeta = _gdn_post_bwd(qkv, gbeta, g_u, g_w, g_t, states, dstates, do_gdn)
    dz = _gdn_conv_bwd_a(proj, conv_w, dqkv)
    dgx, d_conv = _gdn_conv_bwd_b(proj, conv_w, dz)
    dmisc, d_alog, d_dtb = _gdn_gates_bwd(proj, alog128, dtb128, gbeta, dgbeta, dkr)
    dproj = jnp.concatenate([dcq, dckv, dmisc, jnp.zeros((x.shape[0], 128), BF16), dmg, dgx, dgg], axis=1)
    d_w_in_p = _matmul(xn, dproj, mode="tn", out_dtype=F32, name="d_w_in", tk=1024)
    big = {"w_in": d_w_in_p, "w_uq": d_w_uq_p, "w_ukv": d_w_ukv, "w_out": d_w_out, "gdn_conv_w": d_conv}
    if scatter_hook is None:
        dxn, received = _matmul(dproj, w_in_p, mode="nt", out_dtype=F32, name="d_xn", tm=512, tn=512), None
    else:
        dxn, received = _matmul_nt_scatter(dproj, w_in_p, scatter_hook(big), name="d_xn_scatter", tm=512, tn=512)
    grad_x, d_ng = _norm1_bwd(x, ng, dxn, dy)

    small = {"norm_gain": d_ng.sum(0), "mla_q_a_gain": d_qag.sum(0), "mla_kv_a_gain": d_kvag.sum(0),
             "mla_q_norm_gain": d_gq.sum(0)[:QK_DIM], "mla_k_norm_gain": d_gk.sum(0)[:QK_DIM],
             "gdn_a_log": d_alog.sum(0)[GA_LANE:GA_LANE + HEADS], "gdn_dt_bias": d_dtb.sum(0)[GA_LANE:GA_LANE + HEADS],
             "gdn_out_norm_gain": d_out_gain.sum(0)}
    return sq, grad_x, small, big, received


WEIGHTS = ["norm_gain", "w_in", "mla_q_a_gain", "mla_kv_a_gain", "w_uq", "w_ukv", "mla_q_norm_gain", "mla_k_norm_gain",
           "gdn_conv_w", "gdn_a_log", "gdn_dt_bias", "gdn_out_norm_gain", "w_out"]
BIG = ["w_in", "w_uq", "w_ukv", "w_out"]


def kernel(x, positions, norm_gain, w_in, mla_q_a_gain, mla_kv_a_gain, w_uq, w_ukv, mla_q_norm_gain, mla_k_norm_gain, gdn_conv_w, gdn_a_log, gdn_dt_bias, gdn_out_norm_gain, w_out, loss_target, m_norm_gain, m_w_in, m_mla_q_a_gain, m_mla_kv_a_gain, m_w_uq, m_w_ukv, m_mla_q_norm_gain, m_mla_k_norm_gain, m_gdn_conv_w, m_gdn_a_log, m_gdn_dt_bias, m_gdn_out_norm_gain, m_w_out, v_norm_gain, v_w_in, v_mla_q_a_gain, v_mla_kv_a_gain, v_w_uq, v_w_ukv, v_mla_q_norm_gain, v_mla_k_norm_gain, v_gdn_conv_w, v_gdn_a_log, v_gdn_dt_bias, v_gdn_out_norm_gain, v_w_out):
    w = dict(norm_gain=norm_gain, w_in=w_in, mla_q_a_gain=mla_q_a_gain, mla_kv_a_gain=mla_kv_a_gain, w_uq=w_uq,
             w_ukv=w_ukv, mla_q_norm_gain=mla_q_norm_gain, mla_k_norm_gain=mla_k_norm_gain, gdn_conv_w=gdn_conv_w,
             gdn_a_log=gdn_a_log, gdn_dt_bias=gdn_dt_bias, gdn_out_norm_gain=gdn_out_norm_gain, w_out=w_out)
    m = dict(norm_gain=m_norm_gain, w_in=m_w_in, mla_q_a_gain=m_mla_q_a_gain, mla_kv_a_gain=m_mla_kv_a_gain,
             w_uq=m_w_uq, w_ukv=m_w_ukv, mla_q_norm_gain=m_mla_q_norm_gain, mla_k_norm_gain=m_mla_k_norm_gain,
             gdn_conv_w=m_gdn_conv_w, gdn_a_log=m_gdn_a_log, gdn_dt_bias=m_gdn_dt_bias,
             gdn_out_norm_gain=m_gdn_out_norm_gain, w_out=m_w_out)
    v = dict(norm_gain=v_norm_gain, w_in=v_w_in, mla_q_a_gain=v_mla_q_a_gain, mla_kv_a_gain=v_mla_kv_a_gain,
             w_uq=v_w_uq, w_ukv=v_w_ukv, mla_q_norm_gain=v_mla_q_norm_gain, mla_k_norm_gain=v_mla_k_norm_gain,
             gdn_conv_w=v_gdn_conv_w, gdn_a_log=v_gdn_a_log, gdn_dt_bias=v_gdn_dt_bias,
             gdn_out_norm_gain=v_gdn_out_norm_gain, w_out=v_w_out)
    k_me = 2 * lax.axis_index("x") + lax.axis_index("y")

    g_in, g_uq, g_ukv, g_out, g_conv = _all_gather_chips(
        [w_in[0].astype(BF16), w_uq[0].astype(BF16), w_ukv[0].astype(BF16), w_out[0].astype(BF16), gdn_conv_w[0]])
    w_in_p = _pad_w_in(_blocks_to_cols(g_in))
    w_uq_p = _pad_heads(_blocks_to_cols(g_uq))
    w_ukv_f = _blocks_to_cols(g_ukv)
    w_out_f = g_out.reshape(D_MODEL, D_MODEL)
    conv_f = _blocks_to_cols(g_conv)

    pair_sums = []

    def scatter_hook(big):
        pair, pair_bf16 = _rs_pair_stage([
            _cols_to_blocks(_unpad_w_in(big["w_in"])), _cols_to_blocks(_unpad_heads(big["w_uq"])),
            _cols_to_blocks(big["w_ukv"]), big["w_out"].reshape(4, 512, D_MODEL)])
        pair_sums.extend(pair)
        return pair_bf16

    sq, grad_x, small, big, received = _local_step(
        x[0], positions[0], loss_target[0], norm_gain, w_in_p, mla_q_a_gain, mla_kv_a_gain, w_uq_p, w_ukv_f,
        mla_q_norm_gain, mla_k_norm_gain, conv_f, gdn_a_log, gdn_dt_bias, gdn_out_norm_gain, w_out_f, scatter_hook)

    loss_local = (0.5 / D_MODEL) * jnp.sum(sq)
    pack = jnp.concatenate([_pack_small(small, loss_local), big["gdn_conv_w"].reshape(96, 128)], axis=0)
    tot = _all_reduce_small(pack)
    loss = tot[LOSS_ROW, 0]
    conv_grad = lax.dynamic_slice_in_dim(tot[CONV_ROW:].reshape(4, 3072), k_me * 768, 768, axis=1)

    shard_grads = _rs_chip_stage(pair_sums, received)

    grads = {n: _unpack_small(tot, n) for n in SMALL_ROWS}
    grads["gdn_conv_w"] = conv_grad[None]
    for n, g in zip(BIG, shard_grads):
        grads[n] = g[None]

    delta, new_m, new_v = {}, {}, {}
    sw = _pack_small({n: w[n] for n in SMALL_ROWS})
    sm = _pack_small({n: m[n] for n in SMALL_ROWS})
    sv = _pack_small({n: v[n] for n in SMALL_ROWS})
    sd, snm, snv = _adamw(sw, tot[:SMALL_PACK_ROWS], sm, sv, "adamw_small")
    for n in SMALL_ROWS:
        delta[n], new_m[n], new_v[n] = _unpack_small(sd, n), _unpack_small(snm, n), _unpack_small(snv, n)
    for n in BIG + ["gdn_conv_w"]:
        d, nm, nv = _adamw(w[n][0], grads[n][0], m[n][0], v[n][0], f"adamw_{n}")
        delta[n], new_m[n], new_v[n] = d[None], nm[None], nv[None]

    return (loss, grad_x[None], *[grads[n] for n in WEIGHTS], *[delta[n] for n in WEIGHTS],
            *[new_m[n] for n in WEIGHTS], *[new_v[n] for n in WEIGHTS])
```

```python
import functools
import math

import jax
import jax.numpy as jnp
from jax import lax
from jax.experimental import pallas as pl
from jax.experimental.pallas import tpu as pltpu

F32 = jnp.float32
BF16 = jnp.bfloat16
MESH = pl.DeviceIdType.MESH

D_MODEL = 2048
HEADS = 8
HEAD_DIM = 128
QK_DIM = 192
QK_PAD = 256
HALF_ROPE = 32
CHUNK = 64
NORM_EPS = 1e-6
W_IN_COLS = 5968
W_IN_PAD = 6144
GA_LANE = 64
GB_LANE = 72
ADAM_LR, ADAM_B1, ADAM_B2, ADAM_EPS, ADAM_WD, ADAM_STEP = 0.001, 0.9, 0.999, 1e-08, 0.01, 10
VMEM_LIMIT_V7X = 52 * 1024 * 1024
HI = lax.Precision.HIGHEST
NN = (((1,), (0,)), ((), ()))
NT = (((1,), (1,)), ((), ()))
TN = (((0,), (0,)), ((), ()))

TILES = {"row": 512, "attn": 1024, "mm": 1024}


def _call(body, *, name, grid, in_specs, out_specs, out_shape, args, scratch=(), sem=None):
    def kfn(*refs):
        body(*refs)
    if sem is None:
        sem = ("arbitrary",) * len(grid)
    return pl.pallas_call(
        kfn, name=name, grid=grid, in_specs=in_specs, out_specs=out_specs, out_shape=out_shape,
        scratch_shapes=list(scratch),
        compiler_params=pltpu.CompilerParams(dimension_semantics=sem, vmem_limit_bytes=VMEM_LIMIT_V7X),
    )(*args)


def _rows(tm, w, cb=0):
    return pl.BlockSpec((tm, w), lambda i: (i, cb))


def _full(shape):
    n = len(shape)
    return pl.BlockSpec(shape, lambda *_: (0,) * n)


def _sds(shape, dtype):
    return jax.ShapeDtypeStruct(shape, dtype)


def _acc8(x):
    tm, c = x.shape
    return jnp.sum(x.reshape(tm // 8, 8, c), axis=0)


def _sigmoid(x):
    return 1.0 / (1.0 + jnp.exp(-x))


def _silu(x):
    return x * _sigmoid(x)


def _dsilu(x):
    s = _sigmoid(x)
    return s * (1.0 + x * (1.0 - s))


def _dot(a, b, dims=NN):
    return lax.dot_general(a.astype(BF16), b.astype(BF16), dims, preferred_element_type=F32)


def _dot_hi(a, b, dims=NN):
    return lax.dot_general(a, b, dims, precision=HI, preferred_element_type=F32)


def _matmul(a, b, *, mode, out_dtype, name, tm=None, tn=None, tk=None):
    if mode == "tn":
        kdim, m = a.shape
    else:
        m, kdim = a.shape
    n = b.shape[0] if mode == "nt" else b.shape[1]
    tm = min(tm or TILES["mm"], m)
    tn = min(tn or TILES["mm"], n)
    tk = min(tk or kdim, kdim)
    nk = kdim // tk
    dims = {"nn": NN, "nt": NT, "tn": TN}[mode]
    if mode == "tn":
        a_spec = pl.BlockSpec((tk, tm), lambda i, j, k: (k, i))
    else:
        a_spec = pl.BlockSpec((tm, tk), lambda i, j, k: (i, k))
    if mode == "nt":
        b_spec = pl.BlockSpec((tn, tk), lambda i, j, k: (j, k))
    else:
        b_spec = pl.BlockSpec((tk, tn), lambda i, j, k: (k, j))

    def body(a_ref, b_ref, o_ref):
        r = _dot(a_ref[...], b_ref[...], dims)
        if nk == 1:
            o_ref[...] = r.astype(o_ref.dtype)
        else:
            k = pl.program_id(2)

            @pl.when(k == 0)
            def _():
                o_ref[...] = r

            @pl.when(k > 0)
            def _():
                o_ref[...] += r

    if nk > 1:
        assert out_dtype == F32
    return _call(body, name=name, grid=(m // tm, n // tn, nk), in_specs=[a_spec, b_spec],
                 out_specs=pl.BlockSpec((tm, tn), lambda i, j, k: (i, j)), out_shape=_sds((m, n), out_dtype),
                 args=(a, b))


def _norm1_fwd(x, gain):
    s = x.shape[0]
    tm = min(TILES["row"], s)

    def body(x_ref, g_ref, o_ref):
        xv = x_ref[...]
        r = lax.rsqrt(jnp.mean(xv * xv, axis=-1, keepdims=True) + NORM_EPS)
        o_ref[...] = (xv * r * g_ref[...]).astype(BF16)

    return _call(body, name="norm1_fwd", grid=(s // tm,), in_specs=[_rows(tm, D_MODEL), _full((1, D_MODEL))],
                 out_specs=_rows(tm, D_MODEL), out_shape=_sds((s, D_MODEL), BF16), args=(x, gain))


def _norm1_bwd(x, gain, dxn, dy):
    s = x.shape[0]
    tm = min(TILES["row"], s)

    def body(x_ref, g_ref, dxn_ref, dy_ref, gx_ref, dg_ref):
        xv = x_ref[...]
        r = lax.rsqrt(jnp.mean(xv * xv, axis=-1, keepdims=True) + NORM_EPS)
        nrm = xv * r
        d = dxn_ref[...]
        dn = d * g_ref[...]
        gx_ref[...] = dy_ref[...] + r * (dn - nrm * jnp.mean(dn * nrm, axis=-1, keepdims=True))

        @pl.when(pl.program_id(0) == 0)
        def _():
            dg_ref[...] = jnp.zeros_like(dg_ref)

        dg_ref[...] += _acc8(d * nrm)

    return _call(body, name="norm1_bwd", grid=(s // tm,),
                 in_specs=[_rows(tm, D_MODEL), _full((1, D_MODEL)), _rows(tm, D_MODEL), _rows(tm, D_MODEL)],
                 out_specs=[_rows(tm, D_MODEL), _full((8, D_MODEL))],
                 out_shape=[_sds((s, D_MODEL), F32), _sds((8, D_MODEL), F32)], args=(x, gain, dxn, dy))


def _rms(xv, width):
    return lax.rsqrt(jnp.sum(xv * xv, axis=-1, keepdims=True) * (1.0 / width) + NORM_EPS)


def _mla_a_norm(proj, gq, gkv):
    s = proj.shape[0]
    tm = min(TILES["row"], s)

    def body(cq_ref, ckv_ref, gq_ref, gkv_ref, oq_ref, okv_ref):
        a = cq_ref[...]
        oq_ref[...] = (a * _rms(a, 512) * gq_ref[...]).astype(BF16)
        b = ckv_ref[...]
        okv_ref[...] = (b * _rms(b, 256) * gkv_ref[...]).astype(BF16)

    return _call(body, name="mla_a_norm", grid=(s // tm,),
                 in_specs=[_rows(tm, 512, 0), _rows(tm, 256, 2), _full((1, 512)), _full((1, 256))],
                 out_specs=[_rows(tm, 512), _rows(tm, 256)],
                 out_shape=[_sds((s, 512), BF16), _sds((s, 256), BF16)], args=(proj, proj, gq, gkv))


def _rms_bwd(xv, gain, d, width):
    r = _rms(xv, width)
    nrm = xv * r
    dn = d * gain
    dx = r * (dn - nrm * (jnp.sum(dn * nrm, axis=-1, keepdims=True) * (1.0 / width)))
    return dx, d * nrm


def _mla_a_norm_bwd(proj, gq, gkv, dcqn, dckvn):
    s = proj.shape[0]
    tm = min(TILES["row"], s)

    def body(cq_ref, ckv_ref, gq_ref, gkv_ref, dq_ref, dkv_ref, oq_ref, okv_ref, aq_ref, akv_ref):
        dxq, gq_part = _rms_bwd(cq_ref[...], gq_ref[...], dq_ref[...], 512)
        dxk, gk_part = _rms_bwd(ckv_ref[...], gkv_ref[...], dkv_ref[...], 256)
        oq_ref[...] = dxq.astype(BF16)
        okv_ref[...] = dxk.astype(BF16)

        @pl.when(pl.program_id(0) == 0)
        def _():
            aq_ref[...] = jnp.zeros_like(aq_ref)
            akv_ref[...] = jnp.zeros_like(akv_ref)

        aq_ref[...] += _acc8(gq_part)
        akv_ref[...] += _acc8(gk_part)

    return _call(body, name="mla_a_norm_bwd", grid=(s // tm,),
                 in_specs=[_rows(tm, 512, 0), _rows(tm, 256, 2), _full((1, 512)), _full((1, 256)),
                           _rows(tm, 512), _rows(tm, 256)],
                 out_specs=[_rows(tm, 512), _rows(tm, 256), _full((8, 512)), _full((8, 256))],
                 out_shape=[_sds((s, 512), BF16), _sds((s, 256), BF16), _sds((8, 512), F32), _sds((8, 256), F32)],
                 args=(proj, proj, gq, gkv, dcqn, dckvn))


def _swap32(r):
    lane = lax.broadcasted_iota(jnp.int32, r.shape, 1)
    return jnp.where(lane < HALF_ROPE, pltpu.roll(r, 128 - HALF_ROPE, 1), pltpu.roll(r, HALF_ROPE, 1))


def _mla_post_fwd(q_pre, kv_pre, proj, cs, sn, gq, gk):
    s = q_pre.shape[0]
    tm = min(TILES["row"], s)

    def body(qp_ref, kvp_ref, misc_ref, cs_ref, sn_ref, gq_ref, gk_ref, q_ref, k_ref, v_ref):
        csv, snv = cs_ref[...], sn_ref[...]
        lane = lax.broadcasted_iota(jnp.int32, (tm, 128), 1)
        kr = jnp.where(lane < 64, misc_ref[...], 0.0)
        for h in range(HEADS):
            for src, g_ref, o_ref in ((None, gq_ref, q_ref), (kr, gk_ref, k_ref)):
                if src is None:
                    xv = qp_ref[:, QK_PAD * h:QK_PAD * (h + 1)]
                else:
                    xv = jnp.concatenate([kvp_ref[:, 256 * h:256 * h + 128], src], axis=-1)
                y = xv * _rms(xv, QK_DIM) * g_ref[...]
                if src is None:
                    y = y * Q_PRESCALE
                hi = y[:, 128:]
                hi = hi * csv + _swap32(hi) * snv
                o_ref[:, QK_PAD * h:QK_PAD * h + 128] = y[:, :128].astype(BF16)
                o_ref[:, QK_PAD * h + 128:QK_PAD * (h + 1)] = hi.astype(BF16)
            v_ref[:, 128 * h:128 * (h + 1)] = kvp_ref[:, 256 * h + 128:256 * (h + 1)].astype(BF16)

    return _call(body, name="mla_post_fwd", grid=(s // tm,),
                 in_specs=[_rows(tm, 2048), _rows(tm, 2048), _rows(tm, 128, 6), _rows(tm, 128), _rows(tm, 128),
                           _full((1, QK_PAD)), _full((1, QK_PAD))],
                 out_specs=[_rows(tm, 2048), _rows(tm, 2048), _rows(tm, 1024)],
                 out_shape=[_sds((s, 2048), BF16), _sds((s, 2048), BF16), _sds((s, 1024), BF16)],
                 args=(q_pre, kv_pre, proj, cs, sn, gq, gk))


def _mla_post_bwd(q_pre, kv_pre, proj, cs, sn, gq, gk, dq, dk, dv):
    s = q_pre.shape[0]
    tm = min(TILES["row"], s)

    def body(qp_ref, kvp_ref, misc_ref, cs_ref, sn_ref, gq_ref, gk_ref, dq_ref, dk_ref, dv_ref,
             oq_ref, okv_ref, okr_ref, agq_ref, agk_ref):
        csv, snv = cs_ref[...], sn_ref[...]
        lane = lax.broadcasted_iota(jnp.int32, (tm, 128), 1)
        kr = jnp.where(lane < 64, misc_ref[...], 0.0)

        @pl.when(pl.program_id(0) == 0)
        def _():
            agq_ref[...] = jnp.zeros_like(agq_ref)
            agk_ref[...] = jnp.zeros_like(agk_ref)

        dkr = jnp.zeros((tm, 128), F32)
        for h in range(HEADS):
            for which in (0, 1):
                if which == 0:
                    xv = qp_ref[:, QK_PAD * h:QK_PAD * (h + 1)]
                    d_ref, g_ref, a_ref = dq_ref, gq_ref, agq_ref
                else:
                    xv = jnp.concatenate([kvp_ref[:, 256 * h:256 * h + 128], kr], axis=-1)
                    d_ref, g_ref, a_ref = dk_ref, gk_ref, agk_ref
                dhi = d_ref[:, QK_PAD * h + 128:QK_PAD * (h + 1)]
                dhi = dhi * csv - _swap32(dhi) * snv
                dyv = jnp.concatenate([d_ref[:, QK_PAD * h:QK_PAD * h + 128], dhi], axis=-1)
                if which == 0:
                    dyv = dyv * ATTN_SCALE
                dx, gpart = _rms_bwd(xv, g_ref[...], dyv, QK_DIM)
                a_ref[...] += _acc8(gpart)
                if which == 0:
                    oq_ref[:, QK_PAD * h:QK_PAD * (h + 1)] = dx.astype(BF16)
                else:
                    okv_ref[:, 256 * h:256 * h + 128] = dx[:, :128].astype(BF16)
                    dkr = dkr + dx[:, 128:]
            okv_ref[:, 256 * h + 128:256 * (h + 1)] = dv_ref[:, 128 * h:128 * (h + 1)].astype(BF16)
        okr_ref[...] = dkr

    return _call(body, name="mla_post_bwd", grid=(s // tm,),
                 in_specs=[_rows(tm, 2048), _rows(tm, 2048), _rows(tm, 128, 6), _rows(tm, 128), _rows(tm, 128),
                           _full((1, QK_PAD)), _full((1, QK_PAD)), _rows(tm, 2048), _rows(tm, 2048), _rows(tm, 1024)],
                 out_specs=[_rows(tm, 2048), _rows(tm, 2048), _rows(tm, 128), _full((8, QK_PAD)), _full((8, QK_PAD))],
                 out_shape=[_sds((s, 2048), BF16), _sds((s, 2048), BF16), _sds((s, 128), F32),
                            _sds((8, QK_PAD), F32), _sds((8, QK_PAD), F32)],
                 args=(q_pre, kv_pre, proj, cs, sn, gq, gk, dq, dk, dv))


ATTN_SCALE = QK_DIM ** -0.5
NEG = -1e30


LOG2E = 1.4426950408889634
LN2 = 0.6931471805599453
Q_PRESCALE = ATTN_SCALE * LOG2E
ATTN_SUB_FWD = 256
ATTN_SUB_BWD = 512


def _causal_pairs(nq, kv_major):
    prs = [(i, j) for i in range(nq) for j in range(i + 1)]
    if kv_major:
        prs.sort(key=lambda ij: (ij[1], ij[0]))
    return (jnp.asarray([p[0] for p in prs], jnp.int32), jnp.asarray([p[1] for p in prs], jnp.int32))


def _pair_call(body, *, name, tables, in_specs, out_specs, out_shape, scratch, args):
    def kfn(*refs):
        body(*refs)
    spec = pltpu.PrefetchScalarGridSpec(num_scalar_prefetch=2, grid=(HEADS, tables[0].shape[0]), in_specs=in_specs,
                                        out_specs=out_specs, scratch_shapes=list(scratch))
    return pl.pallas_call(
        kfn, name=name, grid_spec=spec, out_shape=out_shape,
        compiler_params=pltpu.CompilerParams(dimension_semantics=("parallel", "arbitrary"),
                                             vmem_limit_bytes=VMEM_LIMIT_V7X))(*tables, *args)


def _diag_mask(sc, ts, qs):
    row = lax.broadcasted_iota(jnp.int32, sc.shape, 0) + qs * ts
    col = lax.broadcasted_iota(jnp.int32, sc.shape, 1)
    return jnp.where(col <= row, sc, NEG)


def _attn_fwd(q, k, v):
    s = q.shape[0]
    t = min(TILES["attn"], s)
    ts = min(ATTN_SUB_FWD, t)
    nq = s // t

    def slabs(q_ref, k_ref, v_ref, m_s, l_s, acc_s, diag):
        for qs in range(t // ts):
            rq = slice(qs * ts, (qs + 1) * ts)
            kw = (qs + 1) * ts if diag else t
            sc = lax.dot_general(q_ref[rq, :], k_ref[0:kw, :], NT, preferred_element_type=F32)
            if diag:
                sc = _diag_mask(sc, ts, qs)
            m_prev = m_s[rq, :]
            m_new = jnp.maximum(m_prev, jnp.max(sc, axis=-1, keepdims=True))
            p = jnp.exp2(sc - m_new)
            alpha = jnp.exp2(m_prev - m_new)
            l_s[rq, :] = alpha * l_s[rq, :] + jnp.sum(p, axis=-1, keepdims=True)
            acc_s[rq, :] = acc_s[rq, :] * alpha + lax.dot_general(p.astype(BF16), v_ref[0:kw, :], NN,
                                                                  preferred_element_type=F32)
            m_s[rq, :] = m_new

    def body(it_ref, jt_ref, q_ref, k_ref, v_ref, o_ref, lse_ref, m_s, l_s, acc_s):
        p = pl.program_id(1)
        i, j = it_ref[p], jt_ref[p]

        @pl.when(j == 0)
        def _():
            m_s[...] = jnp.full_like(m_s, NEG)
            l_s[...] = jnp.zeros_like(l_s)
            acc_s[...] = jnp.zeros_like(acc_s)

        @pl.when(j < i)
        def _():
            slabs(q_ref, k_ref, v_ref, m_s, l_s, acc_s, False)

        @pl.when(j == i)
        def _():
            slabs(q_ref, k_ref, v_ref, m_s, l_s, acc_s, True)
            o_ref[...] = acc_s[...] / l_s[...]
            lse_ref[...] = m_s[...] + jnp.log2(l_s[...])

    qb = lambda h, p, it, jt: (it[p], h)
    kb = lambda h, p, it, jt: (jt[p], h)
    return _pair_call(
        body, name="attn_fwd", tables=_causal_pairs(nq, kv_major=False),
        in_specs=[pl.BlockSpec((t, QK_PAD), qb), pl.BlockSpec((t, QK_PAD), kb), pl.BlockSpec((t, HEAD_DIM), kb)],
        out_specs=[pl.BlockSpec((t, HEAD_DIM), qb),
                   pl.BlockSpec((None, t, 1), lambda h, p, it, jt: (h, it[p], 0))],
        out_shape=[_sds((s, HEADS * HEAD_DIM), F32), _sds((HEADS, s, 1), F32)],
        scratch=[pltpu.VMEM((t, 1), F32), pltpu.VMEM((t, 1), F32), pltpu.VMEM((t, HEAD_DIM), F32)],
        args=(q, k, v))


def _attn_bwd(q, k, v, o, lse, do):
    s = q.shape[0]
    t = min(TILES["attn"], s)
    ts = min(ATTN_SUB_BWD, t)
    nq = s // t

    def slabs(q_ref, k_ref, v_ref, o_ref, lse_ref, do_ref, dq_ref, dk_ref, dv_ref, i, diag):
        for qs in range(t // ts):
            rq = slice(qs * ts, (qs + 1) * ts)
            kw = (qs + 1) * ts if diag else t
            qv, kv = q_ref[rq, :], k_ref[0:kw, :]
            sc = lax.dot_general(qv, kv, NT, preferred_element_type=F32)
            if diag:
                sc = _diag_mask(sc, ts, qs)
            p = jnp.exp2(sc - lse_ref[rq, :])
            dof = do_ref[rq, :]
            delta = jnp.sum(dof * o_ref[rq, :], axis=-1, keepdims=True)
            dob = dof.astype(BF16)
            dv_ref[0:kw, :] += lax.dot_general(p.astype(BF16), dob, TN, preferred_element_type=F32)
            dp = lax.dot_general(dob, v_ref[0:kw, :], NT, preferred_element_type=F32)
            ds = (p * (dp - delta)).astype(BF16)
            dk_ref[0:kw, :] += lax.dot_general(ds, qv, TN, preferred_element_type=F32)
            rows = pl.ds(pl.multiple_of(i * t + qs * ts, ts), ts)
            dq_ref[rows, :] += lax.dot_general(ds, kv, NN, preferred_element_type=F32)

    def body(it_ref, jt_ref, q_ref, k_ref, v_ref, o_ref, lse_ref, do_ref, dq_ref, dk_ref, dv_ref):
        p = pl.program_id(1)
        i, j = it_ref[p], jt_ref[p]
        refs = (q_ref, k_ref, v_ref, o_ref, lse_ref, do_ref, dq_ref, dk_ref, dv_ref)

        @pl.when(p == 0)
        def _():
            dq_ref[...] = jnp.zeros_like(dq_ref)

        @pl.when(i == j)
        def _():
            dk_ref[...] = jnp.zeros_like(dk_ref)
            dv_ref[...] = jnp.zeros_like(dv_ref)

        @pl.when(i > j)
        def _():
            slabs(*refs, i, False)

        @pl.when(i == j)
        def _():
            slabs(*refs, i, True)

        @pl.when(i == nq - 1)
        def _():
            dk_ref[...] = dk_ref[...] * LN2

    qb = lambda h, p, it, jt: (it[p], h)
    kb = lambda h, p, it, jt: (jt[p], h)
    return _pair_call(
        body, name="attn_bwd", tables=_causal_pairs(nq, kv_major=True),
        in_specs=[pl.BlockSpec((t, QK_PAD), qb), pl.BlockSpec((t, QK_PAD), kb), pl.BlockSpec((t, HEAD_DIM), kb),
                  pl.BlockSpec((t, HEAD_DIM), qb), pl.BlockSpec((None, t, 1), lambda h, p, it, jt: (h, it[p], 0)),
                  pl.BlockSpec((t, HEAD_DIM), qb)],
        out_specs=[pl.BlockSpec((s, QK_PAD), lambda h, p, it, jt: (0, h)), pl.BlockSpec((t, QK_PAD), kb),
                   pl.BlockSpec((t, HEAD_DIM), kb)],
        out_shape=[_sds((s, HEADS * QK_PAD), F32), _sds((s, HEADS * QK_PAD), F32), _sds((s, HEADS * HEAD_DIM), F32)],
        scratch=(), args=(q, k, v, o, lse, do))


GDN_Q_SCALE = HEAD_DIM ** -0.5


def _shift_down(xv, prev8, sft):
    rolled = pltpu.roll(xv, sft, 0)
    top = pltpu.roll(jnp.concatenate([prev8, xv[:8]], axis=0), sft, 0)[8:]
    return jnp.concatenate([top, rolled[8:]], axis=0)


def _shift_up(xv, next8, sft):
    tm = xv.shape[0]
    rolled = pltpu.roll(xv, tm - sft, 0)
    bot = pltpu.roll(jnp.concatenate([xv[tm - 8:], next8], axis=0), 16 - sft, 0)[:8]
    return jnp.concatenate([rolled[:tm - 8], bot], axis=0)


def _conv_z(xv, prev8, w_ref):
    z = xv * w_ref[3:4, :]
    for sft in (1, 2, 3):
        z = z + _shift_down(xv, prev8, sft) * w_ref[3 - sft:4 - sft, :]
    return z


def _conv_specs(s, tm):
    nb8 = tm // 8
    cur = pl.BlockSpec((tm, 1024), lambda j, i: (i, 2 + j))
    prev = pl.BlockSpec((8, 1024), lambda j, i: (jnp.maximum(i * nb8 - 1, 0), 2 + j))
    return cur, prev


def _gdn_conv_fwd(proj, conv_w):
    s = proj.shape[0]
    tm = min(TILES["row"], s)
    cur, prev = _conv_specs(s, tm)

    def body(x_ref, xp_ref, w_ref, o_ref):
        j, i = pl.program_id(0), pl.program_id(1)
        prev8 = jnp.where(i > 0, xp_ref[...], 0.0)
        a = _silu(_conv_z(x_ref[...], prev8, w_ref))
        qk_scale = jnp.where(j == 0, GDN_Q_SCALE, 1.0)
        for h in range(HEADS):
            seg = a[:, 128 * h:128 * (h + 1)]
            r = lax.rsqrt(jnp.sum(seg * seg, axis=-1, keepdims=True) + NORM_EPS)
            o_ref[:, 128 * h:128 * (h + 1)] = jnp.where(j < 2, seg * r * qk_scale, seg)

    return _call(body, name="gdn_conv_fwd", grid=(3, s // tm),
                 in_specs=[cur, prev, pl.BlockSpec((4, 1024), lambda j, i: (0, j))],
                 out_specs=pl.BlockSpec((tm, 1024), lambda j, i: (i, j)), out_shape=_sds((s, 3072), F32),
                 args=(proj, proj, conv_w))


def _gdn_conv_bwd_a(proj, conv_w, dqkv):
    s = proj.shape[0]
    tm = min(TILES["row"], s)
    cur, prev = _conv_specs(s, tm)

    def body(x_ref, xp_ref, w_ref, d_ref, o_ref):
        j, i = pl.program_id(0), pl.program_id(1)
        prev8 = jnp.where(i > 0, xp_ref[...], 0.0)
        z = _conv_z(x_ref[...], prev8, w_ref)
        a = _silu(z)
        dsl = _dsilu(z)
        qk_scale = jnp.where(j == 0, GDN_Q_SCALE, 1.0)
        for h in range(HEADS):
            sl = slice(128 * h, 128 * (h + 1))
            seg = a[:, sl]
            dyv = d_ref[:, sl]
            r = lax.rsqrt(jnp.sum(seg * seg, axis=-1, keepdims=True) + NORM_EPS)
            yh = seg * r
            da_n = qk_scale * r * (dyv - yh * jnp.sum(yh * dyv, axis=-1, keepdims=True))
            o_ref[:, sl] = jnp.where(j < 2, da_n, dyv) * dsl[:, sl]

    return _call(body, name="gdn_conv_bwd_a", grid=(3, s // tm),
                 in_specs=[cur, prev, pl.BlockSpec((4, 1024), lambda j, i: (0, j)),
                           pl.BlockSpec((tm, 1024), lambda j, i: (i, j))],
                 out_specs=pl.BlockSpec((tm, 1024), lambda j, i: (i, j)), out_shape=_sds((s, 3072), F32),
                 args=(proj, proj, conv_w, dqkv))


def _gdn_conv_bwd_b(proj, conv_w, dz):
    s = proj.shape[0]
    tm = min(TILES["row"], s)
    nb8 = tm // 8
    last8 = s // 8 - 1
    cur, prev = _conv_specs(s, tm)

    def body(x_ref, xp_ref, w_ref, dz_ref, dzn_ref, dx_ref, dw_ref):
        i = pl.program_id(1)
        prev8 = jnp.where(i > 0, xp_ref[...], 0.0)
        next8 = jnp.where(i < pl.num_programs(1) - 1, dzn_ref[...], 0.0)
        xv, dzv = x_ref[...], dz_ref[...]

        @pl.when(i == 0)
        def _():
            dw_ref[...] = jnp.zeros_like(dw_ref)

        dx = dzv * w_ref[3:4, :]
        dw_ref[3:4, :] += jnp.sum(dzv * xv, axis=0, keepdims=True)
        for sft in (1, 2, 3):
            dx = dx + _shift_up(dzv, next8, sft) * w_ref[3 - sft:4 - sft, :]
            dw_ref[3 - sft:4 - sft, :] += jnp.sum(dzv * _shift_down(xv, prev8, sft), axis=0, keepdims=True)
        dx_ref[...] = dx.astype(BF16)

    return _call(body, name="gdn_conv_bwd_b", grid=(3, s // tm),
                 in_specs=[cur, prev, pl.BlockSpec((4, 1024), lambda j, i: (0, j)),
                           pl.BlockSpec((tm, 1024), lambda j, i: (i, j)),
                           pl.BlockSpec((8, 1024), lambda j, i: (jnp.minimum((i + 1) * nb8, last8), j))],
                 out_specs=[pl.BlockSpec((tm, 1024), lambda j, i: (i, j)), pl.BlockSpec((4, 1024), lambda j, i: (0, j))],
                 out_shape=[_sds((s, 3072), BF16), _sds((4, 3072), F32)], args=(proj, proj, conv_w, dz, dz))


def _softplus(xv):
    return jnp.maximum(xv, 0.0) + jnp.log(1.0 + jnp.exp(-jnp.abs(xv)))


def _gdn_gates_fwd(proj, alog128, dtb128):
    s = proj.shape[0]
    tm = min(TILES["row"], s)

    def body(m_ref, a_ref, b_ref, o_ref):
        mv = m_ref[...]
        lane = lax.broadcasted_iota(jnp.int32, mv.shape, 1)
        g = -jnp.exp(a_ref[...]) * _softplus(mv + b_ref[...])
        is_g = (lane >= GA_LANE) & (lane < GA_LANE + HEADS)
        is_b = (lane >= GB_LANE) & (lane < GB_LANE + HEADS)
        o_ref[...] = jnp.where(is_g, g, jnp.where(is_b, _sigmoid(mv), 0.0))

    return _call(body, name="gdn_gates_fwd", grid=(s // tm,),
                 in_specs=[_rows(tm, 128, 6), _full((1, 128)), _full((1, 128))],
                 out_specs=_rows(tm, 128), out_shape=_sds((s, 128), F32), args=(proj, alog128, dtb128))


def _gdn_gates_bwd(proj, alog128, dtb128, gbeta, dgbeta, dkr):
    s = proj.shape[0]
    tm = min(TILES["row"], s)

    def body(m_ref, a_ref, b_ref, gb_ref, d_ref, kr_ref, o_ref, da_ref, db_ref):
        mv, dv = m_ref[...], d_ref[...]
        lane = lax.broadcasted_iota(jnp.int32, mv.shape, 1)
        is_g = (lane >= GA_LANE) & (lane < GA_LANE + HEADS)
        is_b = (lane >= GB_LANE) & (lane < GB_LANE + HEADS)
        dga = jnp.where(is_g, dv * (-jnp.exp(a_ref[...])) * _sigmoid(mv + b_ref[...]), 0.0)
        beta = gb_ref[...]
        dgb = jnp.where(is_b, dv * beta * (1.0 - beta), 0.0)
        o_ref[...] = jnp.where(lane < 64, kr_ref[...], dga + dgb).astype(BF16)

        @pl.when(pl.program_id(0) == 0)
        def _():
            da_ref[...] = jnp.zeros_like(da_ref)
            db_ref[...] = jnp.zeros_like(db_ref)

        da_ref[...] += _acc8(jnp.where(is_g, dv * gb_ref[...], 0.0))
        db_ref[...] += _acc8(dga)

    return _call(body, name="gdn_gates_bwd", grid=(s // tm,),
                 in_specs=[_rows(tm, 128, 6), _full((1, 128)), _full((1, 128)), _rows(tm, 128), _rows(tm, 128),
                           _rows(tm, 128)],
                 out_specs=[_rows(tm, 128), _full((8, 128)), _full((8, 128))],
                 out_shape=[_sds((s, 128), BF16), _sds((8, 128), F32), _sds((8, 128), F32)],
                 args=(proj, alog128, dtb128, gbeta, dgbeta, dkr))


def _col(mat, lane_idx, lane):
    return jnp.sum(jnp.where(lane == lane_idx, mat, 0.0), axis=-1, keepdims=True)


def _chunk_local(qh, kh, vh, gcol, bcol, ii, jj):
    lower, strict, eye = ii >= jj, ii > jj, ii == jj
    grow = jnp.sum(jnp.where(eye, gcol, 0.0), axis=0, keepdims=True)
    decay = jnp.where(lower, jnp.exp(jnp.where(lower, gcol - grow, 0.0)), 0.0)
    kb = kh * bcol
    vb = vh * bcol
    mm = _dot(kb, kh, NT)
    lmat = jnp.where(strict, mm * decay, 0.0)
    pw = -lmat
    tinv = jnp.where(eye, 1.0, 0.0) + pw
    for _ in range(5):
        pw = _dot_hi(pw, pw)
        tinv = tinv + _dot_hi(tinv, pw)
    egc = jnp.exp(gcol)
    kbg = kb * egc
    rhs = jnp.concatenate([vb, kbg], axis=-1)
    sol = _dot_hi(tinv, rhs)
    qk = _dot(qh, kh, NT)
    glast = jnp.sum(jnp.where(ii[:, :1] == CHUNK - 1, gcol, 0.0), axis=0, keepdims=True)
    ekd = jnp.exp(glast - gcol)
    return dict(decay=decay, kb=kb, vb=vb, mm=mm, lmat=lmat, tinv=tinv, egc=egc, kbg=kbg, rhs=rhs,
                u=sol[:, :HEAD_DIM], w=sol[:, HEAD_DIM:], qk=qk, amat=qk * decay, qd=qh * egc, ekd=ekd,
                kd=kh * ekd, gl=jnp.exp(glast), strict=strict, lower=lower, eye=eye)


def _tri(ii, jj):
    return jnp.where(ii >= jj, 1.0, 0.0)


def _gdn_fwd(qkv, gbeta):
    s = qkv.shape[0]
    n = s // CHUNK

    def body(qkv_ref, gb_ref, o_ref, st_ref, state):
        @pl.when(pl.program_id(0) == 0)
        def _():
            state[...] = jnp.zeros_like(state)

        ii = lax.broadcasted_iota(jnp.int32, (CHUNK, CHUNK), 0)
        jj = lax.broadcasted_iota(jnp.int32, (CHUNK, CHUNK), 1)
        lane = lax.broadcasted_iota(jnp.int32, (CHUNK, 128), 1)
        gbv = gb_ref[...]
        gc = _dot_hi(_tri(ii, jj), gbv)
        for h in range(HEADS):
            sl = slice(128 * h, 128 * (h + 1))
            qh = qkv_ref[:, 128 * h:128 * (h + 1)]
            kh = qkv_ref[:, 1024 + 128 * h:1024 + 128 * (h + 1)]
            vh = qkv_ref[:, 2048 + 128 * h:2048 + 128 * (h + 1)]
            c = _chunk_local(qh, kh, vh, _col(gc, GA_LANE + h, lane), _col(gbv, GB_LANE + h, lane), ii, jj)
            st = state[sl, :]
            st_ref[sl, :] = st
            vn = c["u"] - _dot(c["w"], st)
            o_ref[:, sl] = _dot(c["qd"], st) + _dot(c["amat"], vn)
            state[sl, :] = st * c["gl"] + _dot(c["kd"], vn, TN)

    return _call(body, name="gdn_fwd", grid=(n,),
                 in_specs=[_rows(CHUNK, 3072), _rows(CHUNK, 128)],
                 out_specs=[_rows(CHUNK, 1024), _rows(HEADS * 128, 128)],
                 out_shape=[_sds((s, 1024), F32), _sds((n * HEADS * 128, 128), F32)],
                 scratch=[pltpu.VMEM((HEADS * 128, 128), F32)], args=(qkv, gbeta))


def _gdn_bwd(qkv, gbeta, states, do):
    s = qkv.shape[0]
    n = s // CHUNK

    def body(qkv_ref, gb_ref, st_ref, do_ref, dqkv_ref, dgb_ref, dstate):
        @pl.when(pl.program_id(0) == 0)
        def _():
            dstate[...] = jnp.zeros_like(dstate)

        ii = lax.broadcasted_iota(jnp.int32, (CHUNK, CHUNK), 0)
        jj = lax.broadcasted_iota(jnp.int32, (CHUNK, CHUNK), 1)
        lane = lax.broadcasted_iota(jnp.int32, (CHUNK, 128), 1)
        row1 = ii[:, :1]
        gbv = gb_ref[...]
        gc = _dot_hi(_tri(ii, jj), gbv)
        dgc_all = jnp.zeros((CHUNK, 128), F32)
        db_all = jnp.zeros((CHUNK, 128), F32)
        for h in range(HEADS):
            sl = slice(128 * h, 128 * (h + 1))
            qh = qkv_ref[:, 128 * h:128 * (h + 1)]
            kh = qkv_ref[:, 1024 + 128 * h:1024 + 128 * (h + 1)]
            vh = qkv_ref[:, 2048 + 128 * h:2048 + 128 * (h + 1)]
            bcol = _col(gbv, GB_LANE + h, lane)
            c = _chunk_local(qh, kh, vh, _col(gc, GA_LANE + h, lane), bcol, ii, jj)
            st = st_ref[sl, :]
            dst = dstate[sl, :]
            dov = do_ref[:, sl]
            vn = c["u"] - _dot(c["w"], st)
            dvn = _dot(c["amat"], dov, TN) + _dot(c["kd"], dst)
            damat = jnp.where(c["lower"], _dot(dov, vn, NT), 0.0)
            dqd = _dot(dov, st, NT)
            dkd = _dot(vn, dst, NT)
            dw = -_dot(dvn, st, NT)
            dgl = jnp.sum(jnp.sum(st * dst, axis=-1, keepdims=True), axis=0, keepdims=True)
            dstate[sl, :] = _dot(c["qd"], dov, TN) + c["gl"] * dst - _dot(c["w"], dvn, TN)
            dsol = jnp.concatenate([dvn, dw], axis=-1)
            drhs = _dot_hi(c["tinv"], dsol, TN)
            dtinv = _dot_hi(dsol, c["rhs"], NT)
            dl = -_dot_hi(_dot_hi(c["tinv"], dtinv, TN), c["tinv"], NT)
            dl = jnp.where(c["strict"], dl, 0.0)
            dmm = dl * c["decay"]
            dqk = damat * c["decay"]
            wmat = dl * c["lmat"] + damat * c["amat"]
            dgc = jnp.sum(wmat, axis=-1, keepdims=True)
            wcol = jnp.sum(wmat, axis=0, keepdims=True)
            dgc = dgc - jnp.sum(jnp.where(c["eye"], wcol, 0.0), axis=-1, keepdims=True)
            dkb = _dot(dmm, kh) + drhs[:, HEAD_DIM:] * c["egc"]
            dk = _dot(dmm, c["kb"], TN) + _dot(dqk, qh, TN) + dkd * c["ekd"]
            dq = _dot(dqk, kh) + dqd * c["egc"]
            dgc = dgc + jnp.sum(drhs[:, HEAD_DIM:] * c["kbg"], axis=-1, keepdims=True)
            dgc = dgc + jnp.sum(dqd * c["qd"], axis=-1, keepdims=True)
            tmp = jnp.sum(dkd * c["kd"], axis=-1, keepdims=True)
            dgc = dgc - tmp
            dglast = jnp.sum(tmp, axis=0, keepdims=True) + dgl * c["gl"]
            dgc = dgc + jnp.where(row1 == CHUNK - 1, dglast, 0.0)
            dk = dk + dkb * bcol
            db = jnp.sum(dkb * kh, axis=-1, keepdims=True) + jnp.sum(drhs[:, :HEAD_DIM] * vh, axis=-1, keepdims=True)
            dqkv_ref[:, 128 * h:128 * (h + 1)] = dq
            dqkv_ref[:, 1024 + 128 * h:1024 + 128 * (h + 1)] = dk
            dqkv_ref[:, 2048 + 128 * h:2048 + 128 * (h + 1)] = drhs[:, :HEAD_DIM] * bcol
            dgc_all = dgc_all + jnp.where(lane == GA_LANE + h, dgc, 0.0)
            db_all = db_all + jnp.where(lane == GB_LANE + h, db, 0.0)
        dgb_ref[...] = _dot_hi(_tri(jj, ii), dgc_all) + db_all

    rev = lambda w: pl.BlockSpec((CHUNK, w), lambda i: (n - 1 - i, 0))
    return _call(body, name="gdn_bwd", grid=(n,),
                 in_specs=[rev(3072), rev(128), pl.BlockSpec((HEADS * 128, 128), lambda i: (n - 1 - i, 0)), rev(1024)],
                 out_specs=[rev(3072), rev(128)],
                 out_shape=[_sds((s, 3072), F32), _sds((s, 128), F32)],
                 scratch=[pltpu.VMEM((HEADS * 128, 128), F32)], args=(qkv, gbeta, states, do))


NN_B = (((2,), (1,)), ((0,), (0,)))
NT_B = (((2,), (2,)), ((0,), (0,)))
TN_B = (((1,), (1,)), ((0,), (0,)))
GDN_PAR_CHUNKS = 2
GDN_SEQ_CHUNKS = 4


def _gather_heads(qkv_ref, gc, gbv, qs, ks, vs, gs, bs, nchunks):
    lane = lax.broadcasted_iota(jnp.int32, (CHUNK, 128), 1)
    for c in range(nchunks):
        rows = slice(CHUNK * c, CHUNK * (c + 1))
        for h in range(HEADS):
            b = HEADS * c + h
            qs[b] = qkv_ref[rows, 128 * h:128 * (h + 1)]
            ks[b] = qkv_ref[rows, 1024 + 128 * h:1024 + 128 * (h + 1)]
            vs[b] = qkv_ref[rows, 2048 + 128 * h:2048 + 128 * (h + 1)]
            gs[b] = jnp.broadcast_to(_col(gc[rows], GA_LANE + h, lane), (CHUNK, 128))
            bs[b] = jnp.broadcast_to(_col(gbv[rows], GB_LANE + h, lane), (CHUNK, 128))


def _block_tri(rows, transpose=False):
    ri = lax.broadcasted_iota(jnp.int32, (rows, rows), 0)
    ci = lax.broadcasted_iota(jnp.int32, (rows, rows), 1)
    same = (ri >> 6) == (ci >> 6)
    return jnp.where(same & ((ci >= ri) if transpose else (ri >= ci)), 1.0, 0.0)


def _local_b(q, k, v, g128, b128):
    ii = lax.broadcasted_iota(jnp.int32, (1, CHUNK, CHUNK), 1)
    jj = lax.broadcasted_iota(jnp.int32, (1, CHUNK, CHUNK), 2)
    lower, strict, eye = ii >= jj, ii > jj, ii == jj
    g64 = g128[:, :, :CHUNK]
    grow = jnp.sum(jnp.where(eye, g64, 0.0), axis=1, keepdims=True)
    decay = jnp.where(lower, jnp.exp(jnp.where(lower, g64 - grow, 0.0)), 0.0)
    kb = k * b128
    vb = v * b128
    mm = lax.dot_general(kb.astype(BF16), k.astype(BF16), NT_B, preferred_element_type=F32)
    lmat = jnp.where(strict, mm * decay, 0.0)
    egc = jnp.exp(g128)
    kbg = kb * egc
    qk = lax.dot_general(q.astype(BF16), k.astype(BF16), NT_B, preferred_element_type=F32)
    row = lax.broadcasted_iota(jnp.int32, (1, CHUNK, 128), 1)
    glast = jnp.sum(jnp.where(row == CHUNK - 1, g128, 0.0), axis=1, keepdims=True)
    ekd = jnp.exp(glast - g128)
    return dict(decay=decay, kb=kb, vb=vb, lmat=lmat, egc=egc, kbg=kbg, amat=qk * decay, qd=q * egc, ekd=ekd,
                kd=k * ekd, gl=jnp.exp(glast), lower=lower, strict=strict, eye=eye)


def _bdot(a, b, dims):
    return lax.dot_general(a.astype(BF16), b.astype(BF16), dims, preferred_element_type=F32)


def _split(a):
    hi = a.astype(BF16)
    return hi, (a - hi.astype(F32)).astype(BF16)


def _bdot_hi(a, b, dims):
    ah, al = _split(a)
    bh, bl = _split(b)
    d = lambda x, y: lax.dot_general(x, y, dims, preferred_element_type=F32)
    return d(ah, bh) + d(ah, bl) + d(al, bh)


def _gdn_pre(qkv, gbeta):
    s = qkv.shape[0]
    n = s // CHUNK
    cb = min(GDN_PAR_CHUNKS, n)
    nb = cb * HEADS
    rows = cb * CHUNK

    def body(qkv_ref, gb_ref, u_ref, w_ref, qd_ref, kd_ref, a_ref, t_ref, gl_ref, qs, ks, vs, gs, bs):
        gbv = gb_ref[...]
        gc = _dot_hi(_block_tri(rows), gbv)
        _gather_heads(qkv_ref, gc, gbv, qs, ks, vs, gs, bs, cb)
        c = _local_b(qs[...], ks[...], vs[...], gs[...], bs[...])
        pw = -c["lmat"]
        tinv = jnp.where(c["eye"], 1.0, 0.0) + pw
        for _ in range(5):
            pw = _bdot_hi(pw, pw, NN_B)
            tinv = tinv + _bdot_hi(tinv, pw, NN_B)
        u_ref[...] = _bdot_hi(tinv, c["vb"], NN_B)
        w_ref[...] = _bdot_hi(tinv, c["kbg"], NN_B).astype(BF16)
        qd_ref[...] = c["qd"].astype(BF16)
        kd_ref[...] = c["kd"].astype(BF16)
        a_ref[...] = c["amat"].astype(BF16)
        t_ref[...] = tinv
        gl_ref[...] = c["gl"]

    b3 = lambda d: pl.BlockSpec((nb, CHUNK, d), lambda i: (i, 0, 0))
    nt = n * HEADS
    return _call(body, name="gdn_pre", grid=(n // cb,),
                 in_specs=[_rows(rows, 3072), _rows(rows, 128)],
                 out_specs=[b3(128), b3(128), b3(128), b3(128), b3(CHUNK), b3(CHUNK),
                            pl.BlockSpec((nb, 1, 128), lambda i: (i, 0, 0))],
                 out_shape=[_sds((nt, CHUNK, 128), F32), _sds((nt, CHUNK, 128), BF16), _sds((nt, CHUNK, 128), BF16),
                            _sds((nt, CHUNK, 128), BF16), _sds((nt, CHUNK, CHUNK), BF16), _sds((nt, CHUNK, CHUNK), F32),
                            _sds((nt, 1, 128), F32)],
                 scratch=[pltpu.VMEM((nb, CHUNK, 128), F32)] * 5, sem=("parallel",), args=(qkv, gbeta))


def _gdn_scan_fwd(u, w, qd, kd, amat, gl):
    nt = u.shape[0]
    n = nt // HEADS
    cs = min(GDN_SEQ_CHUNKS, n)

    def body(u_ref, w_ref, qd_ref, kd_ref, a_ref, gl_ref, o_ref, st_ref, state):
        @pl.when(pl.program_id(0) == 0)
        def _():
            state[...] = jnp.zeros_like(state)

        for c in range(cs):
            sl = slice(HEADS * c, HEADS * (c + 1))
            st = state[...]
            stb = st.astype(BF16)
            st_ref[sl] = stb
            vn = u_ref[sl] - lax.dot_general(w_ref[sl], stb, NN_B, preferred_element_type=F32)
            vnb = vn.astype(BF16)
            o = (lax.dot_general(qd_ref[sl], stb, NN_B, preferred_element_type=F32)
                 + lax.dot_general(a_ref[sl], vnb, NN_B, preferred_element_type=F32))
            state[...] = st * gl_ref[sl] + lax.dot_general(kd_ref[sl], vnb, TN_B, preferred_element_type=F32)
            for h in range(HEADS):
                o_ref[CHUNK * c:CHUNK * (c + 1), 128 * h:128 * (h + 1)] = o[h]

    b3 = lambda d: pl.BlockSpec((cs * HEADS, CHUNK, d), lambda i: (i, 0, 0))
    return _call(body, name="gdn_scan_fwd", grid=(n // cs,),
                 in_specs=[b3(128), b3(128), b3(128), b3(128), b3(CHUNK), pl.BlockSpec((cs * HEADS, 1, 128), lambda i: (i, 0, 0))],
                 out_specs=[_rows(cs * CHUNK, 1024), pl.BlockSpec((cs * HEADS, 128, 128), lambda i: (i, 0, 0))],
                 out_shape=[_sds((n * CHUNK, 1024), F32), _sds((nt, 128, 128), BF16)],
                 scratch=[pltpu.VMEM((HEADS, 128, 128), F32)], args=(u, w, qd, kd, amat, gl))


def _gdn_scan_bwd(w, qd, kd, amat, gl, do):
    nt = w.shape[0]
    n = nt // HEADS
    cs = min(GDN_SEQ_CHUNKS, n)
    ng = n // cs

    def body(w_ref, qd_ref, kd_ref, a_ref, gl_ref, do_ref, ds_ref, dstate, dos):
        @pl.when(pl.program_id(0) == 0)
        def _():
            dstate[...] = jnp.zeros_like(dstate)

        for c in reversed(range(cs)):
            sl = slice(HEADS * c, HEADS * (c + 1))
            for h in range(HEADS):
                dos[h] = do_ref[CHUNK * c:CHUNK * (c + 1), 128 * h:128 * (h + 1)].astype(BF16)
            dob = dos[...]
            dst = dstate[...]
            dstb = dst.astype(BF16)
            ds_ref[sl] = dstb
            dvn = (lax.dot_general(a_ref[sl], dob, TN_B, preferred_element_type=F32)
                   + lax.dot_general(kd_ref[sl], dstb, NN_B, preferred_element_type=F32))
            dstate[...] = (lax.dot_general(qd_ref[sl], dob, TN_B, preferred_element_type=F32) + gl_ref[sl] * dst
                           - lax.dot_general(w_ref[sl], dvn.astype(BF16), TN_B, preferred_element_type=F32))

    b3 = lambda d: pl.BlockSpec((cs * HEADS, CHUNK, d), lambda i: (ng - 1 - i, 0, 0))
    return _call(body, name="gdn_scan_bwd", grid=(ng,),
                 in_specs=[b3(128), b3(128), b3(128), b3(CHUNK), pl.BlockSpec((cs * HEADS, 1, 128), lambda i: (ng - 1 - i, 0, 0)),
                           pl.BlockSpec((cs * CHUNK, 1024), lambda i: (ng - 1 - i, 0))],
                 out_specs=pl.BlockSpec((cs * HEADS, 128, 128), lambda i: (ng - 1 - i, 0, 0)),
                 out_shape=_sds((nt, 128, 128), BF16),
                 scratch=[pltpu.VMEM((HEADS, 128, 128), F32), pltpu.VMEM((HEADS, CHUNK, 128), BF16)],
                 args=(w, qd, kd, amat, gl, do))


def _gdn_post_bwd(qkv, gbeta, u, w, tinv, states, dstates, do):
    s = qkv.shape[0]
    n = s // CHUNK
    cb = min(GDN_PAR_CHUNKS, n)
    nb = cb * HEADS
    rows = cb * CHUNK

    def body(qkv_ref, gb_ref, u_ref, w_ref, t_ref, st_ref, ds_ref, do_ref, dqkv_ref, dgb_ref, qs, ks, vs, gs, bs, dos):
        gbv = gb_ref[...]
        gc = _dot_hi(_block_tri(rows), gbv)
        _gather_heads(qkv_ref, gc, gbv, qs, ks, vs, gs, bs, cb)
        for c in range(cb):
            for h in range(HEADS):
                dos[HEADS * c + h] = do_ref[CHUNK * c:CHUNK * (c + 1), 128 * h:128 * (h + 1)]
        q, k, v, b128 = qs[...], ks[...], vs[...], bs[...]
        c = _local_b(q, k, v, gs[...], b128)
        tinv, st, dst, dov = t_ref[...], st_ref[...], ds_ref[...], dos[...]
        wv = w_ref[...]
        vn = u_ref[...] - _bdot(wv, st, NN_B)
        dvn = _bdot(c["amat"], dov, TN_B) + _bdot(c["kd"], dst, NN_B)
        damat = jnp.where(c["lower"], _bdot(dov, vn, NT_B), 0.0)
        dqd = _bdot(dov, st, NT_B)
        dkd = _bdot(vn, dst, NT_B)
        dw = -_bdot(dvn, st, NT_B)
        dgl = jnp.sum(jnp.sum(st.astype(F32) * dst.astype(F32), axis=1, keepdims=True), axis=-1, keepdims=True)
        dvb = _bdot_hi(tinv, dvn, TN_B)
        dkbg = _bdot_hi(tinv, dw, TN_B)
        dtinv = _bdot_hi(dvn, c["vb"], NT_B) + _bdot_hi(dw, c["kbg"], NT_B)
        dl = -_bdot_hi(_bdot_hi(tinv, dtinv, TN_B), tinv, NT_B)
        dl = jnp.where(c["strict"], dl, 0.0)
        dmm = dl * c["decay"]
        dqk = damat * c["decay"]
        wmat = dl * c["lmat"] + damat * c["amat"]
        wcol = jnp.sum(wmat, axis=1, keepdims=True)
        dgc = jnp.sum(wmat, axis=-1, keepdims=True) - jnp.sum(jnp.where(c["eye"], wcol, 0.0), axis=-1, keepdims=True)
        dkb = _bdot(dmm, k, NN_B) + dkbg * c["egc"]
        dk = _bdot(dmm, c["kb"], TN_B) + _bdot(dqk, q, TN_B) + dkd * c["ekd"] + dkb * b128
        dq = _bdot(dqk, k, NN_B) + dqd * c["egc"]
        tmp = jnp.sum(dkd * c["kd"], axis=-1, keepdims=True)
        dgc = (dgc + jnp.sum(dkbg * c["kbg"], axis=-1, keepdims=True) + jnp.sum(dqd * c["qd"], axis=-1, keepdims=True)
               - tmp)
        dglast = jnp.sum(tmp, axis=1, keepdims=True) + dgl * c["gl"][:, :, :1]
        row1 = lax.broadcasted_iota(jnp.int32, (1, CHUNK, 1), 1)
        dgc = dgc + jnp.where(row1 == CHUNK - 1, dglast, 0.0)
        db = jnp.sum(dkb * k, axis=-1, keepdims=True) + jnp.sum(dvb * v, axis=-1, keepdims=True)
        dv = dvb * b128
        lane = lax.broadcasted_iota(jnp.int32, (CHUNK, 128), 1)
        parts = []
        for cc in range(cb):
            acc = jnp.zeros((CHUNK, 128), F32)
            for h in range(HEADS):
                bi = HEADS * cc + h
                rs = slice(CHUNK * cc, CHUNK * (cc + 1))
                dqkv_ref[rs, 128 * h:128 * (h + 1)] = dq[bi]
                dqkv_ref[rs, 1024 + 128 * h:1024 + 128 * (h + 1)] = dk[bi]
                dqkv_ref[rs, 2048 + 128 * h:2048 + 128 * (h + 1)] = dv[bi]
                acc = acc + jnp.where(lane == GA_LANE + h, dgc[bi], 0.0)
            parts.append(acc)
        dgc_all = jnp.concatenate(parts, axis=0)
        dg_all = _dot_hi(_block_tri(rows, transpose=True), dgc_all)
        for cc in range(cb):
            acc = dg_all[CHUNK * cc:CHUNK * (cc + 1)]
            for h in range(HEADS):
                acc = acc + jnp.where(lane == GB_LANE + h, db[HEADS * cc + h], 0.0)
            dgb_ref[CHUNK * cc:CHUNK * (cc + 1), :] = acc

    b3 = lambda d1, d2: pl.BlockSpec((nb, d1, d2), lambda i: (i, 0, 0))
    return _call(body, name="gdn_post_bwd", grid=(n // cb,),
                 in_specs=[_rows(rows, 3072), _rows(rows, 128), b3(CHUNK, 128), b3(CHUNK, 128), b3(CHUNK, CHUNK),
                           b3(128, 128), b3(128, 128), _rows(rows, 1024)],
                 out_specs=[_rows(rows, 3072), _rows(rows, 128)],
                 out_shape=[_sds((s, 3072), F32), _sds((s, 128), F32)],
                 scratch=[pltpu.VMEM((nb, CHUNK, 128), F32)] * 6, sem=("parallel",),
                 args=(qkv, gbeta, u, w, tinv, states, dstates, do))


def _mix_fwd(o_mla, o_gdn, proj, out_gain):
    s = proj.shape[0]
    tm = min(TILES["row"], s)

    def body(om_ref, og_ref, mg_ref, gg_ref, g_ref, o_ref):
        o_ref[:, :1024] = (om_ref[...] * _silu(mg_ref[...])).astype(BF16)
        for h in range(HEADS):
            sl = slice(128 * h, 128 * (h + 1))
            og = og_ref[:, sl]
            on = og * _rms(og, HEAD_DIM) * g_ref[...]
            o_ref[:, 1024 + 128 * h:1024 + 128 * (h + 1)] = (on * _silu(gg_ref[:, sl])).astype(BF16)

    return _call(body, name="mix_fwd", grid=(s // tm,),
                 in_specs=[_rows(tm, 1024), _rows(tm, 1024), _rows(tm, 1024, 1), _rows(tm, 1024, 5), _full((1, 128))],
                 out_specs=_rows(tm, 2048), out_shape=_sds((s, 2048), BF16), args=(o_mla, o_gdn, proj, proj, out_gain))


def _mix_bwd(o_mla, o_gdn, proj, out_gain, dmixed):
    s = proj.shape[0]
    tm = min(TILES["row"], s)

    def body(om_ref, og_ref, mg_ref, gg_ref, g_ref, dm_ref, dg_ref, dom_ref, dog_ref, dmg_ref, dgg_ref, ag_ref):
        @pl.when(pl.program_id(0) == 0)
        def _():
            ag_ref[...] = jnp.zeros_like(ag_ref)

        mg = mg_ref[...]
        dom_ref[...] = dm_ref[...] * _silu(mg)
        dmg_ref[...] = (dm_ref[...] * om_ref[...] * _dsilu(mg)).astype(BF16)
        for h in range(HEADS):
            sl = slice(128 * h, 128 * (h + 1))
            og, gg, d = og_ref[:, sl], gg_ref[:, sl], dg_ref[:, sl]
            on = og * _rms(og, HEAD_DIM) * g_ref[...]
            dgg_ref[:, sl] = (d * on * _dsilu(gg)).astype(BF16)
            dx, gpart = _rms_bwd(og, g_ref[...], d * _silu(gg), HEAD_DIM)
            dog_ref[:, sl] = dx
            ag_ref[...] += _acc8(gpart)

    return _call(body, name="mix_bwd", grid=(s // tm,),
                 in_specs=[_rows(tm, 1024), _rows(tm, 1024), _rows(tm, 1024, 1), _rows(tm, 1024, 5), _full((1, 128)),
                           _rows(tm, 1024, 0), _rows(tm, 1024, 1)],
                 out_specs=[_rows(tm, 1024), _rows(tm, 1024), _rows(tm, 1024), _rows(tm, 1024), _full((8, 128))],
                 out_shape=[_sds((s, 1024), F32), _sds((s, 1024), F32), _sds((s, 1024), BF16), _sds((s, 1024), BF16),
                            _sds((8, 128), F32)],
                 args=(o_mla, o_gdn, proj, proj, out_gain, dmixed, dmixed))


def _out_fwd(mixed, w_out, x, target):
    s = x.shape[0]
    tm = min(TILES["mm"], s)
    tn = min(TILES["mm"], D_MODEL)

    def body(m_ref, w_ref, x_ref, t_ref, dy_ref, acc_ref):
        err = x_ref[...] + _dot(m_ref[...], w_ref[...]) - t_ref[...]
        dy_ref[...] = err * (1.0 / D_MODEL)

        @pl.when(pl.program_id(1) == 0)
        def _():
            acc_ref[...] = jnp.zeros_like(acc_ref)

        acc_ref[...] += _acc8(err * err)

    return _call(body, name="out_fwd", grid=(D_MODEL // tn, s // tm),
                 in_specs=[pl.BlockSpec((tm, D_MODEL), lambda j, i: (i, 0)), pl.BlockSpec((D_MODEL, tn), lambda j, i: (0, j)),
                           pl.BlockSpec((tm, tn), lambda j, i: (i, j)), pl.BlockSpec((tm, tn), lambda j, i: (i, j))],
                 out_specs=[pl.BlockSpec((tm, tn), lambda j, i: (i, j)), pl.BlockSpec((8, tn), lambda j, i: (0, j))],
                 out_shape=[_sds((s, D_MODEL), F32), _sds((8, D_MODEL), F32)], args=(mixed, w_out, x, target))


def _row_tile(r, c):
    if r % 8 != 0:
        return r
    t = 8
    while r % (2 * t) == 0 and 2 * t * c * 4 <= (1 << 20):
        t *= 2
    return t


def _sum_arrays(parts, name, also_bf16=False):
    r, c = parts[0].shape
    tr = _row_tile(r, c)
    n = len(parts)

    def body(*refs):
        acc = refs[0][...].astype(F32)
        for p_ref in refs[1:n]:
            acc = acc + p_ref[...].astype(F32)
        refs[n][...] = acc
        if also_bf16:
            refs[n + 1][...] = acc.astype(BF16)

    nout = 2 if also_bf16 else 1
    out = _call(body, name=name, grid=(r // tr,), in_specs=[_rows(tr, c)] * n, out_specs=[_rows(tr, c)] * nout,
                out_shape=[_sds((r, c), F32), _sds((r, c), BF16)][:nout], args=tuple(parts))
    return out if also_bf16 else out[0]


def _adamw(w, g, m, v, name):
    r, c = w.shape
    tr = _row_tile(r, c)
    c1 = 1.0 - ADAM_B1 ** ADAM_STEP
    c2 = 1.0 - ADAM_B2 ** ADAM_STEP

    def body(w_ref, g_ref, m_ref, v_ref, d_ref, nm_ref, nv_ref):
        gv = g_ref[...]
        nm = ADAM_B1 * m_ref[...] + (1.0 - ADAM_B1) * gv
        nv = ADAM_B2 * v_ref[...] + (1.0 - ADAM_B2) * (gv * gv)
        nm_ref[...] = nm
        nv_ref[...] = nv
        d_ref[...] = -ADAM_LR * ((nm / c1) / (jnp.sqrt(nv / c2) + ADAM_EPS) + ADAM_WD * w_ref[...])

    return _call(body, name=name, grid=(r // tr,), in_specs=[_rows(tr, c)] * 4, out_specs=[_rows(tr, c)] * 3,
                 out_shape=[_sds((r, c), F32)] * 3, args=(w, g, m, v))


ANY = pl.BlockSpec(memory_space=pl.ANY)
CHIP_FLIPS = ((1, 0), (0, 1), (1, 1))


def _comm_call(body, *, name, n_in, out_shape, scratch):
    def kfn(*refs):
        body(*refs)
    return pl.pallas_call(kfn, name=name, in_specs=[ANY] * n_in, out_specs=[ANY] * len(out_shape), out_shape=out_shape,
                          scratch_shapes=list(scratch),
                          compiler_params=pltpu.CompilerParams(has_side_effects=True))


def _all_gather_chips(shards):
    na = len(shards)

    def body(*refs):
        ins, outs = refs[:na], refs[na:2 * na]
        send_sems, recv_sems, fwd_send, fwd_recv = refs[2 * na:]
        x, y, c = lax.axis_index("x"), lax.axis_index("y"), lax.axis_index("c")
        my_k = 2 * x + y
        pending, forwards = [], []
        for a in range(na):
            rows = ins[a].shape[0]
            split = rows % 32 == 0
            for r, (fx, fy) in enumerate(CHIP_FLIPS):
                px, py = x ^ fx, y ^ fy
                if split:
                    mine = pl.ds(pl.multiple_of(c * (rows // 2), 16), rows // 2)
                    other = pl.ds(pl.multiple_of((1 - c) * (rows // 2), 16), rows // 2)
                    rc = pltpu.make_async_remote_copy(
                        src_ref=ins[a].at[mine], dst_ref=outs[a].at[my_k, mine], send_sem=send_sems.at[a, r],
                        recv_sem=recv_sems.at[a, r], device_id=(px, py, c), device_id_type=MESH)
                    landed = outs[a].at[2 * px + py, mine]
                    fw = pltpu.make_async_remote_copy(
                        src_ref=landed, dst_ref=landed, send_sem=fwd_send.at[a, r], recv_sem=fwd_recv.at[a, r],
                        device_id=(x, y, 1 - c), device_id_type=MESH)
                    from_sib = outs[a].at[2 * px + py, other]
                    fw_in = pltpu.make_async_remote_copy(
                        src_ref=from_sib, dst_ref=from_sib, send_sem=fwd_send.at[a, r], recv_sem=fwd_recv.at[a, r],
                        device_id=(x, y, 1 - c), device_id_type=MESH)
                    forwards.append((rc, fw, fw_in))
                else:
                    rc = pltpu.make_async_remote_copy(
                        src_ref=ins[a], dst_ref=outs[a].at[my_k], send_sem=send_sems.at[a, r],
                        recv_sem=recv_sems.at[a, r], device_id=(px, py, c), device_id_type=MESH)
                    pending.append(rc)
                rc.start()
        for rc, fw, _ in forwards:
            rc.wait_recv()
            fw.start()
        for rc, fw, fw_in in forwards:
            rc.wait_send()
            fw.wait_send()
            fw_in.wait_recv()
        for cp in pending:
            cp.wait()

    out_shape = [_sds((4,) + a.shape, a.dtype) for a in shards]
    sem = pltpu.SemaphoreType.DMA((na, 3))
    got = _comm_call(body, name="all_gather_weights", n_in=na, out_shape=out_shape, scratch=[sem, sem, sem, sem])(*shards)
    my_k = 2 * lax.axis_index("x") + lax.axis_index("y")
    return [lax.dynamic_update_index_in_dim(g, a, my_k, 0) for g, a in zip(got, shards)]


def _all_reduce_small(vec):
    r = vec.shape[0]

    def body(v_ref, o_ref, gath, send_sems, recv_sems):
        x, y, c = lax.axis_index("x"), lax.axis_index("y"), lax.axis_index("c")
        me = 4 * x + 2 * y + c
        gath[me] = v_ref[...]
        copies = []
        for rel in range(1, 8):
            fx, fy, fc = (rel >> 2) & 1, (rel >> 1) & 1, rel & 1
            rc = pltpu.make_async_remote_copy(
                src_ref=v_ref, dst_ref=gath.at[me], send_sem=send_sems.at[rel - 1], recv_sem=recv_sems.at[rel - 1],
                device_id=(x ^ fx, y ^ fy, c ^ fc), device_id_type=MESH)
            rc.start()
            copies.append(rc)
        for rc in copies:
            rc.wait()
        acc = gath[0]
        for d in range(1, 8):
            acc = acc + gath[d]
        o_ref[...] = acc

    def kfn(*refs):
        body(*refs)
    vm = pl.BlockSpec(memory_space=pltpu.VMEM)
    return pl.pallas_call(kfn, name="all_reduce_small", in_specs=[vm], out_specs=vm, out_shape=_sds((r, 128), F32),
                          scratch_shapes=[pltpu.VMEM((8, r, 128), F32), pltpu.SemaphoreType.DMA((7,)),
                                          pltpu.SemaphoreType.DMA((7,))],
                          compiler_params=pltpu.CompilerParams(has_side_effects=True))(vec)


def _exchange_halves(arrs):
    na = len(arrs)

    def body(*refs):
        ins, outs = refs[:na], refs[na:2 * na]
        send_sems, recv_sems = refs[2 * na:]
        x, y, c = lax.axis_index("x"), lax.axis_index("y"), lax.axis_index("c")
        copies = []
        for a in range(na):
            half = ins[a].shape[1] // 2
            src = ins[a].at[:, pl.ds(pl.multiple_of((1 - c) * half, 8), half), :]
            rc = pltpu.make_async_remote_copy(src_ref=src, dst_ref=outs[a], send_sem=send_sems.at[a],
                                              recv_sem=recv_sems.at[a], device_id=(x, y, 1 - c), device_id_type=MESH)
            rc.start()
            copies.append(rc)
        for rc in copies:
            rc.wait()

    out_shape = [_sds((4, a.shape[1] // 2, a.shape[2]), F32) for a in arrs]
    return _comm_call(body, name="rs_pair_exchange", n_in=na, out_shape=out_shape,
                      scratch=[pltpu.SemaphoreType.DMA((na,)), pltpu.SemaphoreType.DMA((na,))])(*arrs)


def _scatter_to_chips(arrs):
    na = len(arrs)

    def body(*refs):
        ins, outs = refs[:na], refs[na:2 * na]
        send_sems, recv_sems = refs[2 * na:]
        x, y, c = lax.axis_index("x"), lax.axis_index("y"), lax.axis_index("c")
        copies = []
        for a in range(na):
            for r, (fx, fy) in enumerate(CHIP_FLIPS):
                px, py = x ^ fx, y ^ fy
                rc = pltpu.make_async_remote_copy(
                    src_ref=ins[a].at[2 * px + py], dst_ref=outs[a].at[r], send_sem=send_sems.at[a, r],
                    recv_sem=recv_sems.at[a, r], device_id=(px, py, c), device_id_type=MESH)
                rc.start()
                copies.append(rc)
        for rc in copies:
            rc.wait()

    out_shape = [_sds((3,) + a.shape[1:], a.dtype) for a in arrs]
    return _comm_call(body, name="rs_chip_scatter", n_in=na, out_shape=out_shape,
                      scratch=[pltpu.SemaphoreType.DMA((na, 3)), pltpu.SemaphoreType.DMA((na, 3))])(*arrs)


def _sum_into_half(parts, name):
    r2, c = parts[0].shape
    tr = _row_tile(r2, c)
    nb = r2 // tr
    n = len(parts)

    def kfn(c_ref, *refs):
        acc = refs[0][...].astype(F32)
        for p_ref in refs[1:n]:
            acc = acc + p_ref[...].astype(F32)
        refs[n][...] = acc

    spec = pltpu.PrefetchScalarGridSpec(
        num_scalar_prefetch=1, grid=(nb,), in_specs=[pl.BlockSpec((tr, c), lambda i, cr: (i, 0))] * n,
        out_specs=pl.BlockSpec((tr, c), lambda i, cr: (cr[0] * nb + i, 0)))
    core = lax.axis_index("c").astype(jnp.int32).reshape(1)
    return pl.pallas_call(kfn, name=name, grid_spec=spec, out_shape=_sds((2 * r2, c), F32),
                          compiler_params=pltpu.CompilerParams(dimension_semantics=("arbitrary",),
                                                               vmem_limit_bytes=VMEM_LIMIT_V7X))(core, *parts)


def _join_in_place(arrs):
    na = len(arrs)

    def body(*refs):
        outs = refs[na:2 * na]
        send_sems, recv_sems = refs[2 * na:]
        x, y, c = lax.axis_index("x"), lax.axis_index("y"), lax.axis_index("c")
        copies = []
        for a in range(na):
            half = outs[a].shape[0] // 2
            mine = outs[a].at[pl.ds(pl.multiple_of(c * half, 8), half), :]
            rc = pltpu.make_async_remote_copy(src_ref=mine, dst_ref=mine, send_sem=send_sems.at[a],
                                              recv_sem=recv_sems.at[a], device_id=(x, y, 1 - c), device_id_type=MESH)
            rc.start()
            copies.append(rc)
        for rc in copies:
            rc.wait()

    def kfn(*refs):
        body(*refs)
    return pl.pallas_call(kfn, name="rs_pair_join", in_specs=[ANY] * na, out_specs=[ANY] * na,
                          out_shape=[_sds(a.shape, F32) for a in arrs],
                          input_output_aliases={a: a for a in range(na)},
                          scratch_shapes=[pltpu.SemaphoreType.DMA((na,)), pltpu.SemaphoreType.DMA((na,))],
                          compiler_params=pltpu.CompilerParams(has_side_effects=True))(*arrs)


def _rs_pair_stage(grads):
    c = lax.axis_index("c")
    got = _exchange_halves(grads)
    pair, pair_bf16 = [], []
    for a, (g, o) in enumerate(zip(grads, got)):
        half = g.shape[1] // 2
        mine = lax.dynamic_slice_in_dim(g, c * half, half, axis=1)
        p32, p16 = _sum_arrays([mine.reshape(4 * half, -1), o.reshape(4 * half, -1)], f"rs_pair_sum_{a}", also_bf16=True)
        pair.append(p32.reshape(4, half, -1))
        pair_bf16.append(p16.reshape(4, half, -1))
    return pair, pair_bf16


def _rs_chip_stage(pair, recv):
    k_me = 2 * lax.axis_index("x") + lax.axis_index("y")
    halves = []
    for a, (p, rv) in enumerate(zip(pair, recv)):
        own = lax.dynamic_index_in_dim(p, k_me, 0, keepdims=False)
        halves.append(_sum_into_half([own, rv[0], rv[1], rv[2]], f"rs_chip_sum_{a}"))
    return _join_in_place(halves)


def _reduce_scatter(grads):
    pair, pair_bf16 = _rs_pair_stage(grads)
    return _rs_chip_stage(pair, _scatter_to_chips(pair_bf16))


def _matmul_nt_scatter(a, b, send, *, name, tm, tn):
    m, kdim = a.shape
    n = b.shape[0]
    ni, nj = m // tm, n // tn
    na = len(send)

    def body(a_ref, b_ref, *rest):
        send_refs, o_ref, recv_refs = rest[:na], rest[na], rest[na + 1:2 * na + 1]
        send_sems, recv_sems = rest[2 * na + 1:]
        i, j = pl.program_id(0), pl.program_id(1)

        def copies():
            x, y, c = lax.axis_index("x"), lax.axis_index("y"), lax.axis_index("c")
            out = []
            for s_i in range(na):
                for r, (fx, fy) in enumerate(CHIP_FLIPS):
                    px, py = x ^ fx, y ^ fy
                    out.append(pltpu.make_async_remote_copy(
                        src_ref=send_refs[s_i].at[2 * px + py], dst_ref=recv_refs[s_i].at[r],
                        send_sem=send_sems.at[s_i, r], recv_sem=recv_sems.at[s_i, r], device_id=(px, py, c),
                        device_id_type=MESH))
            return out

        @pl.when((i == 0) & (j == 0))
        def _():
            for cp in copies():
                cp.start()

        o_ref[...] = _dot(a_ref[...], b_ref[...], NT)

        @pl.when((i == ni - 1) & (j == nj - 1))
        def _():
            for cp in copies():
                cp.wait()

    def kfn(*refs):
        body(*refs)
    sem = pltpu.SemaphoreType.DMA((na, 3))
    out = pl.pallas_call(
        kfn, name=name, grid=(ni, nj),
        in_specs=[pl.BlockSpec((tm, kdim), lambda i, j: (i, 0)), pl.BlockSpec((tn, kdim), lambda i, j: (j, 0))] + [ANY] * na,
        out_specs=[pl.BlockSpec((tm, tn), lambda i, j: (i, j))] + [ANY] * na,
        out_shape=[_sds((m, n), F32)] + [_sds((3,) + s_a.shape[1:], s_a.dtype) for s_a in send],
        scratch_shapes=[sem, sem],
        compiler_params=pltpu.CompilerParams(dimension_semantics=("arbitrary", "arbitrary"),
                                             vmem_limit_bytes=VMEM_LIMIT_V7X, has_side_effects=True))(a, b, *send)
    return out[0], list(out[1:])


def _pad_w_in(w):
    z = jnp.zeros((w.shape[0], 1024 - 848), w.dtype)
    return jnp.concatenate([w[:, 0:832], w[:, 4928:4944], z, w[:, 832:4928], w[:, 4944:5968]], axis=1)


def _unpad_w_in(g):
    return jnp.concatenate([g[:, 0:832], g[:, 1024:5120], g[:, 832:848], g[:, 5120:6144]], axis=1)


def _pad_heads(w):
    r = w.shape[0]
    return jnp.pad(w.reshape(r, HEADS, QK_DIM), ((0, 0), (0, 0), (0, QK_PAD - QK_DIM))).reshape(r, HEADS * QK_PAD)


def _unpad_heads(w):
    r = w.shape[0]
    return w.reshape(r, HEADS, QK_PAD)[:, :, :QK_DIM].reshape(r, HEADS * QK_DIM)


def _cols_to_blocks(w):
    r = w.shape[0]
    return w.reshape(r, 4, -1).transpose(1, 0, 2)


def _blocks_to_cols(w):
    return w.transpose(1, 0, 2).reshape(w.shape[1], -1)


SMALL_ROWS = {"norm_gain": (0, 2048), "mla_q_a_gain": (16, 512), "mla_kv_a_gain": (20, 256),
              "mla_q_norm_gain": (22, 192), "mla_k_norm_gain": (24, 192), "gdn_a_log": (26, 8),
              "gdn_dt_bias": (27, 8), "gdn_out_norm_gain": (28, 128)}
LOSS_ROW = 29
SMALL_PACK_ROWS = 32
CONV_ROW = 32


def _pack_small(vals, loss=None):
    rows = []
    at = 0
    for name, (row, size) in SMALL_ROWS.items():
        assert row == at
        nr = -(-size // 128)
        rows.append(jnp.pad(vals[name].reshape(-1).astype(F32), (0, nr * 128 - size)).reshape(nr, 128))
        at += nr
    assert at == LOSS_ROW
    if loss is not None:
        rows.append(jnp.pad(loss.reshape(1, 1), ((0, 0), (0, 127))))
        at += 1
    rows.append(jnp.zeros((SMALL_PACK_ROWS - at, 128), F32))
    return jnp.concatenate(rows, axis=0)


def _unpack_small(pack, name):
    row, size = SMALL_ROWS[name]
    nr = -(-size // 128)
    return pack[row:row + nr].reshape(-1)[:size].reshape(1, size)


def _local_step(x, positions, target, norm_gain, w_in_p, q_a_gain, kv_a_gain, w_uq_p, w_ukv, q_norm_gain,
                k_norm_gain, conv_w, a_log, dt_bias, out_gain, w_out, scatter_hook=None):
    half = HALF_ROPE
    inv_freq = jnp.power(10000.0, -jnp.arange(half, dtype=F32) / half)
    ang = positions.astype(F32)[:, None] * inv_freq
    cos, sin = jnp.cos(ang), jnp.sin(ang)
    zpad = jnp.zeros((x.shape[0], 64), F32)
    cs = jnp.concatenate([cos, cos, zpad], axis=1)
    sn = jnp.concatenate([-sin, sin, zpad], axis=1)
    gq = jnp.pad(q_norm_gain.reshape(1, QK_DIM), ((0, 0), (0, QK_PAD - QK_DIM)))
    gk = jnp.pad(k_norm_gain.reshape(1, QK_DIM), ((0, 0), (0, QK_PAD - QK_DIM)))
    lane_pad = ((0, 0), (GA_LANE, 128 - GA_LANE - HEADS))
    alog128 = jnp.pad(a_log.reshape(1, HEADS), lane_pad)
    dtb128 = jnp.pad(dt_bias.reshape(1, HEADS), lane_pad)
    ng, qag, kvag, og = (norm_gain.reshape(1, -1), q_a_gain.reshape(1, -1), kv_a_gain.reshape(1, -1),
                         out_gain.reshape(1, -1))

    xn = _norm1_fwd(x, ng)
    proj = _matmul(xn, w_in_p, mode="nn", out_dtype=F32, name="in_proj")
    cqn, ckvn = _mla_a_norm(proj, qag, kvag)
    q_pre = _matmul(cqn, w_uq_p, mode="nn", out_dtype=F32, name="q_up")
    kv_pre = _matmul(ckvn, w_ukv, mode="nn", out_dtype=F32, name="kv_up")
    q, k, v = _mla_post_fwd(q_pre, kv_pre, proj, cs, sn, gq, gk)
    o_mla, lse = _attn_fwd(q, k, v)
    qkv = _gdn_conv_fwd(proj, conv_w)
    gbeta = _gdn_gates_fwd(proj, alog128, dtb128)
    g_u, g_w, g_qd, g_kd, g_a, g_t, g_gl = _gdn_pre(qkv, gbeta)
    o_gdn, states = _gdn_scan_fwd(g_u, g_w, g_qd, g_kd, g_a, g_gl)
    mixed = _mix_fwd(o_mla, o_gdn, proj, og)
    dy, sq = _out_fwd(mixed, w_out, x, target)

    dmixed = _matmul(dy, w_out, mode="nt", out_dtype=F32, name="d_mixed")
    d_w_out = _matmul(mixed, dy, mode="tn", out_dtype=F32, name="d_w_out", tk=1024)
    do_mla, do_gdn, dmg, dgg, d_out_gain = _mix_bwd(o_mla, o_gdn, proj, og, dmixed)
    dq, dk, dv = _attn_bwd(q, k, v, o_mla, lse, do_mla)
    dq_pre, dkv_pre, dkr, d_gq, d_gk = _mla_post_bwd(q_pre, kv_pre, proj, cs, sn, gq, gk, dq, dk, dv)
    d_w_uq_p = _matmul(cqn, dq_pre, mode="tn", out_dtype=F32, name="d_w_uq", tk=1024)
    d_w_ukv = _matmul(ckvn, dkv_pre, mode="tn", out_dtype=F32, name="d_w_ukv", tk=1024)
    dcqn = _matmul(dq_pre, w_uq_p, mode="nt", out_dtype=F32, name="d_cqn")
    dckvn = _matmul(dkv_pre, w_ukv, mode="nt", out_dtype=F32, name="d_ckvn")
    dcq, dckv, d_qag, d_kvag = _mla_a_norm_bwd(proj, qag, kvag, dcqn, dckvn)
    dstates = _gdn_scan_bwd(g_w, g_qd, g_kd, g_a, g_gl, do_gdn)
    dqkv, dgbeta = _gdn_post_bwd(qkv, gbeta, g_u, g_w, g_t, states, dstates, do_gdn)
    dz = _gdn_conv_bwd_a(proj, conv_w, dqkv)
    dgx, d_conv = _gdn_conv_bwd_b(proj, conv_w, dz)
    dmisc, d_alog, d_dtb = _gdn_gates_bwd(proj, alog128, dtb128, gbeta, dgbeta, dkr)
    dproj = jnp.concatenate([dcq, dckv, dmisc, jnp.zeros((x.shape[0], 128), BF16), dmg, dgx, dgg], axis=1)
    d_w_in_p = _matmul(xn, dproj, mode="tn", out_dtype=F32, name="d_w_in", tk=1024)
    big = {"w_in": d_w_in_p, "w_uq": d_w_uq_p, "w_ukv": d_w_ukv, "w_out": d_w_out, "gdn_conv_w": d_conv}
    if scatter_hook is None:
        dxn, received = _matmul(dproj, w_in_p, mode="nt", out_dtype=F32, name="d_xn", tm=512, tn=512), None
    else:
        dxn, received = _matmul_nt_scatter(dproj, w_in_p, scatter_hook(big), name="d_xn_scatter", tm=512, tn=512)
    grad_x, d_ng = _norm1_bwd(x, ng, dxn, dy)

    small = {"norm_gain": d_ng.sum(0), "mla_q_a_gain": d_qag.sum(0), "mla_kv_a_gain": d_kvag.sum(0),
             "mla_q_norm_gain": d_gq.sum(0)[:QK_DIM], "mla_k_norm_gain": d_gk.sum(0)[:QK_DIM],
             "gdn_a_log": d_alog.sum(0)[GA_LANE:GA_LANE + HEADS], "gdn_dt_bias": d_dtb.sum(0)[GA_LANE:GA_LANE + HEADS],
             "gdn_out_norm_gain": d_out_gain.sum(0)}
    return sq, grad_x, small, big, received


WEIGHTS = ["norm_gain", "w_in", "mla_q_a_gain", "mla_kv_a_gain", "w_uq", "w_ukv", "mla_q_norm_gain", "mla_k_norm_gain",
           "gdn_conv_w", "gdn_a_log", "gdn_dt_bias", "gdn_out_norm_gain", "w_out"]
BIG = ["w_in", "w_uq", "w_ukv", "w_out"]


def kernel(x, positions, norm_gain, w_in, mla_q_a_gain, mla_kv_a_gain, w_uq, w_ukv, mla_q_norm_gain, mla_k_norm_gain, gdn_conv_w, gdn_a_log, gdn_dt_bias, gdn_out_norm_gain, w_out, loss_target, m_norm_gain, m_w_in, m_mla_q_a_gain, m_mla_kv_a_gain, m_w_uq, m_w_ukv, m_mla_q_norm_gain, m_mla_k_norm_gain, m_gdn_conv_w, m_gdn_a_log, m_gdn_dt_bias, m_gdn_out_norm_gain, m_w_out, v_norm_gain, v_w_in, v_mla_q_a_gain, v_mla_kv_a_gain, v_w_uq, v_w_ukv, v_mla_q_norm_gain, v_mla_k_norm_gain, v_gdn_conv_w, v_gdn_a_log, v_gdn_dt_bias, v_gdn_out_norm_gain, v_w_out):
    w = dict(norm_gain=norm_gain, w_in=w_in, mla_q_a_gain=mla_q_a_gain, mla_kv_a_gain=mla_kv_a_gain, w_uq=w_uq,
             w_ukv=w_ukv, mla_q_norm_gain=mla_q_norm_gain, mla_k_norm_gain=mla_k_norm_gain, gdn_conv_w=gdn_conv_w,
             gdn_a_log=gdn_a_log, gdn_dt_bias=gdn_dt_bias, gdn_out_norm_gain=gdn_out_norm_gain, w_out=w_out)
    m = dict(norm_gain=m_norm_gain, w_in=m_w_in, mla_q_a_gain=m_mla_q_a_gain, mla_kv_a_gain=m_mla_kv_a_gain,
             w_uq=m_w_uq, w_ukv=m_w_ukv, mla_q_norm_gain=m_mla_q_norm_gain, mla_k_norm_gain=m_mla_k_norm_gain,
             gdn_conv_w=m_gdn_conv_w, gdn_a_log=m_gdn_a_log, gdn_dt_bias=m_gdn_dt_bias,
             gdn_out_norm_gain=m_gdn_out_norm_gain, w_out=m_w_out)
    v = dict(norm_gain=v_norm_gain, w_in=v_w_in, mla_q_a_gain=v_mla_q_a_gain, mla_kv_a_gain=v_mla_kv_a_gain,
             w_uq=v_w_uq, w_ukv=v_w_ukv, mla_q_norm_gain=v_mla_q_norm_gain, mla_k_norm_gain=v_mla_k_norm_gain,
             gdn_conv_w=v_gdn_conv_w, gdn_a_log=v_gdn_a_log, gdn_dt_bias=v_gdn_dt_bias,
             gdn_out_norm_gain=v_gdn_out_norm_gain, w_out=v_w_out)
    k_me = 2 * lax.axis_index("x") + lax.axis_index("y")

    g_in, g_uq, g_ukv, g_out, g_conv = _all_gather_chips(
        [w_in[0].astype(BF16), w_uq[0].astype(BF16), w_ukv[0].astype(BF16), w_out[0].astype(BF16), gdn_conv_w[0]])
    w_in_p = _pad_w_in(_blocks_to_cols(g_in))
    w_uq_p = _pad_heads(_blocks_to_cols(g_uq))
    w_ukv_f = _blocks_to_cols(g_ukv)
    w_out_f = g_out.reshape(D_MODEL, D_MODEL)
    conv_f = _blocks_to_cols(g_conv)

    pair_sums = []

    def scatter_hook(big):
        pair, pair_bf16 = _rs_pair_stage([
            _cols_to_blocks(_unpad_w_in(big["w_in"])), _cols_to_blocks(_unpad_heads(big["w_uq"])),
            _cols_to_blocks(big["w_ukv"]), big["w_out"].reshape(4, 512, D_MODEL)])
        pair_sums.extend(pair)
        return pair_bf16

    sq, grad_x, small, big, received = _local_step(
        x[0], positions[0], loss_target[0], norm_gain, w_in_p, mla_q_a_gain, mla_kv_a_gain, w_uq_p, w_ukv_f,
        mla_q_norm_gain, mla_k_norm_gain, conv_f, gdn_a_log, gdn_dt_bias, gdn_out_norm_gain, w_out_f, scatter_hook)

    loss_local = (0.5 / D_MODEL) * jnp.sum(sq)
    pack = jnp.concatenate([_pack_small(small, loss_local), big["gdn_conv_w"].reshape(96, 128)], axis=0)
    tot = _all_reduce_small(pack)
    loss = tot[LOSS_ROW, 0]
    conv_grad = lax.dynamic_slice_in_dim(tot[CONV_ROW:].reshape(4, 3072), k_me * 768, 768, axis=1)

    shard_grads = _rs_chip_stage(pair_sums, received)

    grads = {n: _unpack_small(tot, n) for n in SMALL_ROWS}
    grads["gdn_conv_w"] = conv_grad[None]
    for n, g in zip(BIG, shard_grads):
        grads[n] = g[None]

    delta, new_m, new_v = {}, {}, {}
    sw = _pack_small({n: w[n] for n in SMALL_ROWS})
    sm = _pack_small({n: m[n] for n in SMALL_ROWS})
    sv = _pack_small({n: v[n] for n in SMALL_ROWS})
    sd, snm, snv = _adamw(sw, tot[:SMALL_PACK_ROWS], sm, sv, "adamw_small")
    for n in SMALL_ROWS:
        delta[n], new_m[n], new_v[n] = _unpack_small(sd, n), _unpack_small(snm, n), _unpack_small(snv, n)
    for n in BIG + ["gdn_conv_w"]:
        d, nm, nv = _adamw(w[n][0], grads[n][0], m[n][0], v[n][0], f"adamw_{n}")
        delta[n], new_m[n], new_v[n] = d[None], nm[None], nv[None]

    return (loss, grad_x[None], *[grads[n] for n in WEIGHTS], *[delta[n] for n in WEIGHTS],
            *[new_m[n] for n in WEIGHTS], *[new_v[n] for n in WEIGHTS])
```

```python
import functools
import math

import jax
import jax.numpy as jnp
from jax import lax
from jax.experimental import pallas as pl
from jax.experimental.pallas import tpu as pltpu

F32 = jnp.float32
BF16 = jnp.bfloat16
MESH = pl.DeviceIdType.MESH

D_MODEL = 2048
HEADS = 8
HEAD_DIM = 128
QK_DIM = 192
QK_PAD = 256
HALF_ROPE = 32
CHUNK = 64
NORM_EPS = 1e-6
W_IN_COLS = 5968
W_IN_PAD = 6144
GA_LANE = 64
GB_LANE = 72
ADAM_LR, ADAM_B1, ADAM_B2, ADAM_EPS, ADAM_WD, ADAM_STEP = 0.001, 0.9, 0.999, 1e-08, 0.01, 10
VMEM_LIMIT_V7X = 52 * 1024 * 1024
HI = lax.Precision.HIGHEST
NN = (((1,), (0,)), ((), ()))
NT = (((1,), (1,)), ((), ()))
TN = (((0,), (0,)), ((), ()))

TILES = {"row": 512, "attn": 1024, "mm": 1024}


def _call(body, *, name, grid, in_specs, out_specs, out_shape, args, scratch=(), sem=None):
    def kfn(*refs):
        body(*refs)
    if sem is None:
        sem = ("arbitrary",) * len(grid)
    return pl.pallas_call(
        kfn, name=name, grid=grid, in_specs=in_specs, out_specs=out_specs, out_shape=out_shape,
        scratch_shapes=list(scratch),
        compiler_params=pltpu.CompilerParams(dimension_semantics=sem, vmem_limit_bytes=VMEM_LIMIT_V7X),
    )(*args)


def _rows(tm, w, cb=0):
    return pl.BlockSpec((tm, w), lambda i: (i, cb))


def _full(shape):
    n = len(shape)
    return pl.BlockSpec(shape, lambda *_: (0,) * n)


def _sds(shape, dtype):
    return jax.ShapeDtypeStruct(shape, dtype)


def _acc8(x):
    tm, c = x.shape
    return jnp.sum(x.reshape(tm // 8, 8, c), axis=0)


def _sigmoid(x):
    return 1.0 / (1.0 + jnp.exp(-x))


def _silu(x):
    return x * _sigmoid(x)


def _dsilu(x):
    s = _sigmoid(x)
    return s * (1.0 + x * (1.0 - s))


def _dot(a, b, dims=NN):
    return lax.dot_general(a.astype(BF16), b.astype(BF16), dims, preferred_element_type=F32)


def _dot_hi(a, b, dims=NN):
    return lax.dot_general(a, b, dims, precision=HI, preferred_element_type=F32)


def _matmul(a, b, *, mode, out_dtype, name, tm=None, tn=None, tk=None):
    if mode == "tn":
        kdim, m = a.shape
    else:
        m, kdim = a.shape
    n = b.shape[0] if mode == "nt" else b.shape[1]
    tm = min(tm or TILES["mm"], m)
    tn = min(tn or TILES["mm"], n)
    tk = min(tk or kdim, kdim)
    nk = kdim // tk
    dims = {"nn": NN, "nt": NT, "tn": TN}[mode]
    if mode == "tn":
        a_spec = pl.BlockSpec((tk, tm), lambda i, j, k: (k, i))
    else:
        a_spec = pl.BlockSpec((tm, tk), lambda i, j, k: (i, k))
    if mode == "nt":
        b_spec = pl.BlockSpec((tn, tk), lambda i, j, k: (j, k))
    else:
        b_spec = pl.BlockSpec((tk, tn), lambda i, j, k: (k, j))

    def body(a_ref, b_ref, o_ref):
        r = _dot(a_ref[...], b_ref[...], dims)
        if nk == 1:
            o_ref[...] = r.astype(o_ref.dtype)
        else:
            k = pl.program_id(2)

            @pl.when(k == 0)
            def _():
                o_ref[...] = r

            @pl.when(k > 0)
            def _():
                o_ref[...] += r

    if nk > 1:
        assert out_dtype == F32
    return _call(body, name=name, grid=(m // tm, n // tn, nk), in_specs=[a_spec, b_spec],
                 out_specs=pl.BlockSpec((tm, tn), lambda i, j, k: (i, j)), out_shape=_sds((m, n), out_dtype),
                 args=(a, b))


def _norm1_fwd(x, gain):
    s = x.shape[0]
    tm = min(TILES["row"], s)

    def body(x_ref, g_ref, o_ref):
        xv = x_ref[...]
        r = lax.rsqrt(jnp.mean(xv * xv, axis=-1, keepdims=True) + NORM_EPS)
        o_ref[...] = (xv * r * g_ref[...]).astype(BF16)

    return _call(body, name="norm1_fwd", grid=(s // tm,), in_specs=[_rows(tm, D_MODEL), _full((1, D_MODEL))],
                 out_specs=_rows(tm, D_MODEL), out_shape=_sds((s, D_MODEL), BF16), args=(x, gain))


def _norm1_bwd(x, gain, dxn, dy):
    s = x.shape[0]
    tm = min(TILES["row"], s)

    def body(x_ref, g_ref, dxn_ref, dy_ref, gx_ref, dg_ref):
        xv = x_ref[...]
        r = lax.rsqrt(jnp.mean(xv * xv, axis=-1, keepdims=True) + NORM_EPS)
        nrm = xv * r
        d = dxn_ref[...]
        dn = d * g_ref[...]
        gx_ref[...] = dy_ref[...] + r * (dn - nrm * jnp.mean(dn * nrm, axis=-1, keepdims=True))

        @pl.when(pl.program_id(0) == 0)
        def _():
            dg_ref[...] = jnp.zeros_like(dg_ref)

        dg_ref[...] += _acc8(d * nrm)

    return _call(body, name="norm1_bwd", grid=(s // tm,),
                 in_specs=[_rows(tm, D_MODEL), _full((1, D_MODEL)), _rows(tm, D_MODEL), _rows(tm, D_MODEL)],
                 out_specs=[_rows(tm, D_MODEL), _full((8, D_MODEL))],
                 out_shape=[_sds((s, D_MODEL), F32), _sds((8, D_MODEL), F32)], args=(x, gain, dxn, dy))


def _rms(xv, width):
    return lax.rsqrt(jnp.sum(xv * xv, axis=-1, keepdims=True) * (1.0 / width) + NORM_EPS)


def _mla_a_norm(proj, gq, gkv):
    s = proj.shape[0]
    tm = min(TILES["row"], s)

    def body(cq_ref, ckv_ref, gq_ref, gkv_ref, oq_ref, okv_ref):
        a = cq_ref[...]
        oq_ref[...] = (a * _rms(a, 512) * gq_ref[...]).astype(BF16)
        b = ckv_ref[...]
        okv_ref[...] = (b * _rms(b, 256) * gkv_ref[...]).astype(BF16)

    return _call(body, name="mla_a_norm", grid=(s // tm,),
                 in_specs=[_rows(tm, 512, 0), _rows(tm, 256, 2), _full((1, 512)), _full((1, 256))],
                 out_specs=[_rows(tm, 512), _rows(tm, 256)],
                 out_shape=[_sds((s, 512), BF16), _sds((s, 256), BF16)], args=(proj, proj, gq, gkv))


def _rms_bwd(xv, gain, d, width):
    r = _rms(xv, width)
    nrm = xv * r
    dn = d * gain
    dx = r * (dn - nrm * (jnp.sum(dn * nrm, axis=-1, keepdims=True) * (1.0 / width)))
    return dx, d * nrm


def _mla_a_norm_bwd(proj, gq, gkv, dcqn, dckvn):
    s = proj.shape[0]
    tm = min(TILES["row"], s)

    def body(cq_ref, ckv_ref, gq_ref, gkv_ref, dq_ref, dkv_ref, oq_ref, okv_ref, aq_ref, akv_ref):
        dxq, gq_part = _rms_bwd(cq_ref[...], gq_ref[...], dq_ref[...], 512)
        dxk, gk_part = _rms_bwd(ckv_ref[...], gkv_ref[...], dkv_ref[...], 256)
        oq_ref[...] = dxq.astype(BF16)
        okv_ref[...] = dxk.astype(BF16)

        @pl.when(pl.program_id(0) == 0)
        def _():
            aq_ref[...] = jnp.zeros_like(aq_ref)
            akv_ref[...] = jnp.zeros_like(akv_ref)

        aq_ref[...] += _acc8(gq_part)
        akv_ref[...] += _acc8(gk_part)

    return _call(body, name="mla_a_norm_bwd", grid=(s // tm,),
                 in_specs=[_rows(tm, 512, 0), _rows(tm, 256, 2), _full((1, 512)), _full((1, 256)),
                           _rows(tm, 512), _rows(tm, 256)],
                 out_specs=[_rows(tm, 512), _rows(tm, 256), _full((8, 512)), _full((8, 256))],
                 out_shape=[_sds((s, 512), BF16), _sds((s, 256), BF16), _sds((8, 512), F32), _sds((8, 256), F32)],
                 args=(proj, proj, gq, gkv, dcqn, dckvn))


def _swap32(r):
    lane = lax.broadcasted_iota(jnp.int32, r.shape, 1)
    return jnp.where(lane < HALF_ROPE, pltpu.roll(r, 128 - HALF_ROPE, 1), pltpu.roll(r, HALF_ROPE, 1))


def _mla_post_fwd(q_pre, kv_pre, proj, cs, sn, gq, gk):
    s = q_pre.shape[0]
    tm = min(TILES["row"], s)

    def body(qp_ref, kvp_ref, misc_ref, cs_ref, sn_ref, gq_ref, gk_ref, q_ref, k_ref, v_ref):
        csv, snv = cs_ref[...], sn_ref[...]
        lane = lax.broadcasted_iota(jnp.int32, (tm, 128), 1)
        kr = jnp.where(lane < 64, misc_ref[...], 0.0)
        for h in range(HEADS):
            for src, g_ref, o_ref in ((None, gq_ref, q_ref), (kr, gk_ref, k_ref)):
                if src is None:
                    xv = qp_ref[:, QK_PAD * h:QK_PAD * (h + 1)]
                else:
                    xv = jnp.concatenate([kvp_ref[:, 256 * h:256 * h + 128], src], axis=-1)
                y = xv * _rms(xv, QK_DIM) * g_ref[...]
                if src is None:
                    y = y * Q_PRESCALE
                hi = y[:, 128:]
                hi = hi * csv + _swap32(hi) * snv
                o_ref[:, QK_PAD * h:QK_PAD * h + 128] = y[:, :128].astype(BF16)
                o_ref[:, QK_PAD * h + 128:QK_PAD * (h + 1)] = hi.astype(BF16)
            v_ref[:, 128 * h:128 * (h + 1)] = kvp_ref[:, 256 * h + 128:256 * (h + 1)].astype(BF16)

    return _call(body, name="mla_post_fwd", grid=(s // tm,),
                 in_specs=[_rows(tm, 2048), _rows(tm, 2048), _rows(tm, 128, 6), _rows(tm, 128), _rows(tm, 128),
                           _full((1, QK_PAD)), _full((1, QK_PAD))],
                 out_specs=[_rows(tm, 2048), _rows(tm, 2048), _rows(tm, 1024)],
                 out_shape=[_sds((s, 2048), BF16), _sds((s, 2048), BF16), _sds((s, 1024), BF16)],
                 args=(q_pre, kv_pre, proj, cs, sn, gq, gk))


def _mla_post_bwd(q_pre, kv_pre, proj, cs, sn, gq, gk, dq, dk, dv):
    s = q_pre.shape[0]
    tm = min(TILES["row"], s)

    def body(qp_ref, kvp_ref, misc_ref, cs_ref, sn_ref, gq_ref, gk_ref, dq_ref, dk_ref, dv_ref,
             oq_ref, okv_ref, okr_ref, agq_ref, agk_ref):
        csv, snv = cs_ref[...], sn_ref[...]
        lane = lax.broadcasted_iota(jnp.int32, (tm, 128), 1)
        kr = jnp.where(lane < 64, misc_ref[...], 0.0)

        @pl.when(pl.program_id(0) == 0)
        def _():
            agq_ref[...] = jnp.zeros_like(agq_ref)
            agk_ref[...] = jnp.zeros_like(agk_ref)

        dkr = jnp.zeros((tm, 128), F32)
        for h in range(HEADS):
            for which in (0, 1):
                if which == 0:
                    xv = qp_ref[:, QK_PAD * h:QK_PAD * (h + 1)]
                    d_ref, g_ref, a_ref = dq_ref, gq_ref, agq_ref
                else:
                    xv = jnp.concatenate([kvp_ref[:, 256 * h:256 * h + 128], kr], axis=-1)
                    d_ref, g_ref, a_ref = dk_ref, gk_ref, agk_ref
                dhi = d_ref[:, QK_PAD * h + 128:QK_PAD * (h + 1)]
                dhi = dhi * csv - _swap32(dhi) * snv
                dyv = jnp.concatenate([d_ref[:, QK_PAD * h:QK_PAD * h + 128], dhi], axis=-1)
                if which == 0:
                    dyv = dyv * ATTN_SCALE
                dx, gpart = _rms_bwd(xv, g_ref[...], dyv, QK_DIM)
                a_ref[...] += _acc8(gpart)
                if which == 0:
                    oq_ref[:, QK_PAD * h:QK_PAD * (h + 1)] = dx.astype(BF16)
                else:
                    okv_ref[:, 256 * h:256 * h + 128] = dx[:, :128].astype(BF16)
                    dkr = dkr + dx[:, 128:]
            okv_ref[:, 256 * h + 128:256 * (h + 1)] = dv_ref[:, 128 * h:128 * (h + 1)].astype(BF16)
        okr_ref[...] = dkr

    return _call(body, name="mla_post_bwd", grid=(s // tm,),
                 in_specs=[_rows(tm, 2048), _rows(tm, 2048), _rows(tm, 128, 6), _rows(tm, 128), _rows(tm, 128),
                           _full((1, QK_PAD)), _full((1, QK_PAD)), _rows(tm, 2048), _rows(tm, 2048), _rows(tm, 1024)],
                 out_specs=[_rows(tm, 2048), _rows(tm, 2048), _rows(tm, 128), _full((8, QK_PAD)), _full((8, QK_PAD))],
                 out_shape=[_sds((s, 2048), BF16), _sds((s, 2048), BF16), _sds((s, 128), F32),
                            _sds((8, QK_PAD), F32), _sds((8, QK_PAD), F32)],
                 args=(q_pre, kv_pre, proj, cs, sn, gq, gk, dq, dk, dv))


ATTN_SCALE = QK_DIM ** -0.5
NEG = -1e30


LOG2E = 1.4426950408889634
LN2 = 0.6931471805599453
Q_PRESCALE = ATTN_SCALE * LOG2E
ATTN_SUB_FWD = 256
ATTN_SUB_BWD = 512


def _causal_pairs(nq, kv_major):
    prs = [(i, j) for i in range(nq) for j in range(i + 1)]
    if kv_major:
        prs.sort(key=lambda ij: (ij[1], ij[0]))
    return (jnp.asarray([p[0] for p in prs], jnp.int32), jnp.asarray([p[1] for p in prs], jnp.int32))


def _pair_call(body, *, name, tables, in_specs, out_specs, out_shape, scratch, args):
    def kfn(*refs):
        body(*refs)
    spec = pltpu.PrefetchScalarGridSpec(num_scalar_prefetch=2, grid=(HEADS, tables[0].shape[0]), in_specs=in_specs,
                                        out_specs=out_specs, scratch_shapes=list(scratch))
    return pl.pallas_call(
        kfn, name=name, grid_spec=spec, out_shape=out_shape,
        compiler_params=pltpu.CompilerParams(dimension_semantics=("parallel", "arbitrary"),
                                             vmem_limit_bytes=VMEM_LIMIT_V7X))(*tables, *args)


def _diag_mask(sc, ts, qs):
    row = lax.broadcasted_iota(jnp.int32, sc.shape, 0) + qs * ts
    col = lax.broadcasted_iota(jnp.int32, sc.shape, 1)
    return jnp.where(col <= row, sc, NEG)


def _attn_fwd(q, k, v):
    s = q.shape[0]
    t = min(TILES["attn"], s)
    ts = min(ATTN_SUB_FWD, t)
    nq = s // t

    def slabs(q_ref, k_ref, v_ref, m_s, l_s, acc_s, diag):
        for qs in range(t // ts):
            rq = slice(qs * ts, (qs + 1) * ts)
            kw = (qs + 1) * ts if diag else t
            sc = lax.dot_general(q_ref[rq, :], k_ref[0:kw, :], NT, preferred_element_type=F32)
            if diag:
                sc = _diag_mask(sc, ts, qs)
            m_prev = m_s[rq, :]
            m_new = jnp.maximum(m_prev, jnp.max(sc, axis=-1, keepdims=True))
            p = jnp.exp2(sc - m_new)
            alpha = jnp.exp2(m_prev - m_new)
            l_s[rq, :] = alpha * l_s[rq, :] + jnp.sum(p, axis=-1, keepdims=True)
            acc_s[rq, :] = acc_s[rq, :] * alpha + lax.dot_general(p.astype(BF16), v_ref[0:kw, :], NN,
                                                                  preferred_element_type=F32)
            m_s[rq, :] = m_new

    def body(it_ref, jt_ref, q_ref, k_ref, v_ref, o_ref, lse_ref, m_s, l_s, acc_s):
        p = pl.program_id(1)
        i, j = it_ref[p], jt_ref[p]

        @pl.when(j == 0)
        def _():
            m_s[...] = jnp.full_like(m_s, NEG)
            l_s[...] = jnp.zeros_like(l_s)
            acc_s[...] = jnp.zeros_like(acc_s)

        @pl.when(j < i)
        def _():
            slabs(q_ref, k_ref, v_ref, m_s, l_s, acc_s, False)

        @pl.when(j == i)
        def _():
            slabs(q_ref, k_ref, v_ref, m_s, l_s, acc_s, True)
            o_ref[...] = acc_s[...] / l_s[...]
            lse_ref[...] = m_s[...] + jnp.log2(l_s[...])

    qb = lambda h, p, it, jt: (it[p], h)
    kb = lambda h, p, it, jt: (jt[p], h)
    return _pair_call(
        body, name="attn_fwd", tables=_causal_pairs(nq, kv_major=False),
        in_specs=[pl.BlockSpec((t, QK_PAD), qb), pl.BlockSpec((t, QK_PAD), kb), pl.BlockSpec((t, HEAD_DIM), kb)],
        out_specs=[pl.BlockSpec((t, HEAD_DIM), qb),
                   pl.BlockSpec((None, t, 1), lambda h, p, it, jt: (h, it[p], 0))],
        out_shape=[_sds((s, HEADS * HEAD_DIM), F32), _sds((HEADS, s, 1), F32)],
        scratch=[pltpu.VMEM((t, 1), F32), pltpu.VMEM((t, 1), F32), pltpu.VMEM((t, HEAD_DIM), F32)],
        args=(q, k, v))


def _attn_bwd(q, k, v, o, lse, do):
    s = q.shape[0]
    t = min(TILES["attn"], s)
    ts = min(ATTN_SUB_BWD, t)
    nq = s // t

    def slabs(q_ref, k_ref, v_ref, o_ref, lse_ref, do_ref, dq_ref, dk_ref, dv_ref, i, diag):
        for qs in range(t // ts):
            rq = slice(qs * ts, (qs + 1) * ts)
            kw = (qs + 1) * ts if diag else t
            qv, kv = q_ref[rq, :], k_ref[0:kw, :]
            sc = lax.dot_general(qv, kv, NT, preferred_element_type=F32)
            if diag:
                sc = _diag_mask(sc, ts, qs)
            p = jnp.exp2(sc - lse_ref[rq, :])
            dof = do_ref[rq, :]
            delta = jnp.sum(dof * o_ref[rq, :], axis=-1, keepdims=True)
            dob = dof.astype(BF16)
            dv_ref[0:kw, :] += lax.dot_general(p.astype(BF16), dob, TN, preferred_element_type=F32)
            dp = lax.dot_general(dob, v_ref[0:kw, :], NT, preferred_element_type=F32)
            ds = (p * (dp - delta)).astype(BF16)
            dk_ref[0:kw, :] += lax.dot_general(ds, qv, TN, preferred_element_type=F32)
            rows = pl.ds(pl.multiple_of(i * t + qs * ts, ts), ts)
            dq_ref[rows, :] += lax.dot_general(ds, kv, NN, preferred_element_type=F32)

    def body(it_ref, jt_ref, q_ref, k_ref, v_ref, o_ref, lse_ref, do_ref, dq_ref, dk_ref, dv_ref):
        p = pl.program_id(1)
        i, j = it_ref[p], jt_ref[p]
        refs = (q_ref, k_ref, v_ref, o_ref, lse_ref, do_ref, dq_ref, dk_ref, dv_ref)

        @pl.when(p == 0)
        def _():
            dq_ref[...] = jnp.zeros_like(dq_ref)

        @pl.when(i == j)
        def _():
            dk_ref[...] = jnp.zeros_like(dk_ref)
            dv_ref[...] = jnp.zeros_like(dv_ref)

        @pl.when(i > j)
        def _():
            slabs(*refs, i, False)

        @pl.when(i == j)
        def _():
            slabs(*refs, i, True)

        @pl.when(i == nq - 1)
        def _():
            dk_ref[...] = dk_ref[...] * LN2

    qb = lambda h, p, it, jt: (it[p], h)
    kb = lambda h, p, it, jt: (jt[p], h)
    return _pair_call(
        body, name="attn_bwd", tables=_causal_pairs(nq, kv_major=True),
        in_specs=[pl.BlockSpec((t, QK_PAD), qb), pl.BlockSpec((t, QK_PAD), kb), pl.BlockSpec((t, HEAD_DIM), kb),
                  pl.BlockSpec((t, HEAD_DIM), qb), pl.BlockSpec((None, t, 1), lambda h, p, it, jt: (h, it[p], 0)),
                  pl.BlockSpec((t, HEAD_DIM), qb)],
        out_specs=[pl.BlockSpec((s, QK_PAD), lambda h, p, it, jt: (0, h)), pl.BlockSpec((t, QK_PAD), kb),
                   pl.BlockSpec((t, HEAD_DIM), kb)],
        out_shape=[_sds((s, HEADS * QK_PAD), F32), _sds((s, HEADS * QK_PAD), F32), _sds((s, HEADS * HEAD_DIM), F32)],
        scratch=(), args=(q, k, v, o, lse, do))


GDN_Q_SCALE = HEAD_DIM ** -0.5


def _shift_down(xv, prev8, sft):
    rolled = pltpu.roll(xv, sft, 0)
    top = pltpu.roll(jnp.concatenate([prev8, xv[:8]], axis=0), sft, 0)[8:]
    return jnp.concatenate([top, rolled[8:]], axis=0)


def _shift_up(xv, next8, sft):
    tm = xv.shape[0]
    rolled = pltpu.roll(xv, tm - sft, 0)
    bot = pltpu.roll(jnp.concatenate([xv[tm - 8:], next8], axis=0), 16 - sft, 0)[:8]
    return jnp.concatenate([rolled[:tm - 8], bot], axis=0)


def _conv_z(xv, prev8, w_ref):
    z = xv * w_ref[3:4, :]
    for sft in (1, 2, 3):
        z = z + _shift_down(xv, prev8, sft) * w_ref[3 - sft:4 - sft, :]
    return z


def _conv_specs(s, tm):
    nb8 = tm // 8
    cur = pl.BlockSpec((tm, 1024), lambda j, i: (i, 2 + j))
    prev = pl.BlockSpec((8, 1024), lambda j, i: (jnp.maximum(i * nb8 - 1, 0), 2 + j))
    return cur, prev


def _gdn_conv_fwd(proj, conv_w):
    s = proj.shape[0]
    tm = min(TILES["row"], s)
    cur, prev = _conv_specs(s, tm)

    def body(x_ref, xp_ref, w_ref, o_ref):
        j, i = pl.program_id(0), pl.program_id(1)
        prev8 = jnp.where(i > 0, xp_ref[...], 0.0)
        a = _silu(_conv_z(x_ref[...], prev8, w_ref))
        qk_scale = jnp.where(j == 0, GDN_Q_SCALE, 1.0)
        for h in range(HEADS):
            seg = a[:, 128 * h:128 * (h + 1)]
            r = lax.rsqrt(jnp.sum(seg * seg, axis=-1, keepdims=True) + NORM_EPS)
            o_ref[:, 128 * h:128 * (h + 1)] = jnp.where(j < 2, seg * r * qk_scale, seg)

    return _call(body, name="gdn_conv_fwd", grid=(3, s // tm),
                 in_specs=[cur, prev, pl.BlockSpec((4, 1024), lambda j, i: (0, j))],
                 out_specs=pl.BlockSpec((tm, 1024), lambda j, i: (i, j)), out_shape=_sds((s, 3072), F32),
                 args=(proj, proj, conv_w))


def _gdn_conv_bwd_a(proj, conv_w, dqkv):
    s = proj.shape[0]
    tm = min(TILES["row"], s)
    cur, prev = _conv_specs(s, tm)

    def body(x_ref, xp_ref, w_ref, d_ref, o_ref):
        j, i = pl.program_id(0), pl.program_id(1)
        prev8 = jnp.where(i > 0, xp_ref[...], 0.0)
        z = _conv_z(x_ref[...], prev8, w_ref)
        a = _silu(z)
        dsl = _dsilu(z)
        qk_scale = jnp.where(j == 0, GDN_Q_SCALE, 1.0)
        for h in range(HEADS):
            sl = slice(128 * h, 128 * (h + 1))
            seg = a[:, sl]
            dyv = d_ref[:, sl]
            r = lax.rsqrt(jnp.sum(seg * seg, axis=-1, keepdims=True) + NORM_EPS)
            yh = seg * r
            da_n = qk_scale * r * (dyv - yh * jnp.sum(yh * dyv, axis=-1, keepdims=True))
            o_ref[:, sl] = jnp.where(j < 2, da_n, dyv) * dsl[:, sl]

    return _call(body, name="gdn_conv_bwd_a", grid=(3, s // tm),
                 in_specs=[cur, prev, pl.BlockSpec((4, 1024), lambda j, i: (0, j)),
                           pl.BlockSpec((tm, 1024), lambda j, i: (i, j))],
                 out_specs=pl.BlockSpec((tm, 1024), lambda j, i: (i, j)), out_shape=_sds((s, 3072), F32),
                 args=(proj, proj, conv_w, dqkv))


def _gdn_conv_bwd_b(proj, conv_w, dz):
    s = proj.shape[0]
    tm = min(TILES["row"], s)
    nb8 = tm // 8
    last8 = s // 8 - 1
    cur, prev = _conv_specs(s, tm)

    def body(x_ref, w_ref, dz_ref, dzn_ref, dx_ref, dw_ref):
        i = pl.program_id(1)
        next8 = jnp.where(i < pl.num_programs(1) - 1, dzn_ref[...], 0.0)
        xv, dzv = x_ref[...], dz_ref[...]

        @pl.when(i == 0)
        def _():
            dw_ref[...] = jnp.zeros_like(dw_ref)

        dx = dzv * w_ref[3:4, :]
        dw_ref[3:4, :] += jnp.sum(dzv * xv, axis=0, keepdims=True)
        for sft in (1, 2, 3):
            up = _shift_up(dzv, next8, sft)
            dx = dx + up * w_ref[3 - sft:4 - sft, :]
            dw_ref[3 - sft:4 - sft, :] += jnp.sum(up * xv, axis=0, keepdims=True)
        dx_ref[...] = dx.astype(BF16)

    return _call(body, name="gdn_conv_bwd_b", grid=(3, s // tm),
                 in_specs=[cur, pl.BlockSpec((4, 1024), lambda j, i: (0, j)),
                           pl.BlockSpec((tm, 1024), lambda j, i: (i, j)),
                           pl.BlockSpec((8, 1024), lambda j, i: (jnp.minimum((i + 1) * nb8, last8), j))],
                 out_specs=[pl.BlockSpec((tm, 1024), lambda j, i: (i, j)), pl.BlockSpec((4, 1024), lambda j, i: (0, j))],
                 out_shape=[_sds((s, 3072), BF16), _sds((4, 3072), F32)], args=(proj, conv_w, dz, dz))


def _softplus(xv):
    return jnp.maximum(xv, 0.0) + jnp.log(1.0 + jnp.exp(-jnp.abs(xv)))


def _gdn_gates_fwd(proj, alog128, dtb128):
    s = proj.shape[0]
    tm = min(TILES["row"], s)

    def body(m_ref, a_ref, b_ref, o_ref):
        mv = m_ref[...]
        lane = lax.broadcasted_iota(jnp.int32, mv.shape, 1)
        g = -jnp.exp(a_ref[...]) * _softplus(mv + b_ref[...])
        is_g = (lane >= GA_LANE) & (lane < GA_LANE + HEADS)
        is_b = (lane >= GB_LANE) & (lane < GB_LANE + HEADS)
        o_ref[...] = jnp.where(is_g, g, jnp.where(is_b, _sigmoid(mv), 0.0))

    return _call(body, name="gdn_gates_fwd", grid=(s // tm,),
                 in_specs=[_rows(tm, 128, 6), _full((1, 128)), _full((1, 128))],
                 out_specs=_rows(tm, 128), out_shape=_sds((s, 128), F32), args=(proj, alog128, dtb128))


def _gdn_gates_bwd(proj, alog128, dtb128, gbeta, dgbeta, dkr):
    s = proj.shape[0]
    tm = min(TILES["row"], s)

    def body(m_ref, a_ref, b_ref, gb_ref, d_ref, kr_ref, o_ref, da_ref, db_ref):
        mv, dv = m_ref[...], d_ref[...]
        lane = lax.broadcasted_iota(jnp.int32, mv.shape, 1)
        is_g = (lane >= GA_LANE) & (lane < GA_LANE + HEADS)
        is_b = (lane >= GB_LANE) & (lane < GB_LANE + HEADS)
        dga = jnp.where(is_g, dv * (-jnp.exp(a_ref[...])) * _sigmoid(mv + b_ref[...]), 0.0)
        beta = gb_ref[...]
        dgb = jnp.where(is_b, dv * beta * (1.0 - beta), 0.0)
        o_ref[...] = jnp.where(lane < 64, kr_ref[...], dga + dgb).astype(BF16)

        @pl.when(pl.program_id(0) == 0)
        def _():
            da_ref[...] = jnp.zeros_like(da_ref)
            db_ref[...] = jnp.zeros_like(db_ref)

        da_ref[...] += _acc8(jnp.where(is_g, dv * gb_ref[...], 0.0))
        db_ref[...] += _acc8(dga)

    return _call(body, name="gdn_gates_bwd", grid=(s // tm,),
                 in_specs=[_rows(tm, 128, 6), _full((1, 128)), _full((1, 128)), _rows(tm, 128), _rows(tm, 128),
                           _rows(tm, 128)],
                 out_specs=[_rows(tm, 128), _full((8, 128)), _full((8, 128))],
                 out_shape=[_sds((s, 128), BF16), _sds((8, 128), F32), _sds((8, 128), F32)],
                 args=(proj, alog128, dtb128, gbeta, dgbeta, dkr))


def _col(mat, lane_idx, lane):
    return jnp.sum(jnp.where(lane == lane_idx, mat, 0.0), axis=-1, keepdims=True)


def _chunk_local(qh, kh, vh, gcol, bcol, ii, jj):
    lower, strict, eye = ii >= jj, ii > jj, ii == jj
    grow = jnp.sum(jnp.where(eye, gcol, 0.0), axis=0, keepdims=True)
    decay = jnp.where(lower, jnp.exp(jnp.where(lower, gcol - grow, 0.0)), 0.0)
    kb = kh * bcol
    vb = vh * bcol
    mm = _dot(kb, kh, NT)
    lmat = jnp.where(strict, mm * decay, 0.0)
    pw = -lmat
    tinv = jnp.where(eye, 1.0, 0.0) + pw
    for _ in range(5):
        pw = _dot_hi(pw, pw)
        tinv = tinv + _dot_hi(tinv, pw)
    egc = jnp.exp(gcol)
    kbg = kb * egc
    rhs = jnp.concatenate([vb, kbg], axis=-1)
    sol = _dot_hi(tinv, rhs)
    qk = _dot(qh, kh, NT)
    glast = jnp.sum(jnp.where(ii[:, :1] == CHUNK - 1, gcol, 0.0), axis=0, keepdims=True)
    ekd = jnp.exp(glast - gcol)
    return dict(decay=decay, kb=kb, vb=vb, mm=mm, lmat=lmat, tinv=tinv, egc=egc, kbg=kbg, rhs=rhs,
                u=sol[:, :HEAD_DIM], w=sol[:, HEAD_DIM:], qk=qk, amat=qk * decay, qd=qh * egc, ekd=ekd,
                kd=kh * ekd, gl=jnp.exp(glast), strict=strict, lower=lower, eye=eye)


def _tri(ii, jj):
    return jnp.where(ii >= jj, 1.0, 0.0)


def _gdn_fwd(qkv, gbeta):
    s = qkv.shape[0]
    n = s // CHUNK

    def body(qkv_ref, gb_ref, o_ref, st_ref, state):
        @pl.when(pl.program_id(0) == 0)
        def _():
            state[...] = jnp.zeros_like(state)

        ii = lax.broadcasted_iota(jnp.int32, (CHUNK, CHUNK), 0)
        jj = lax.broadcasted_iota(jnp.int32, (CHUNK, CHUNK), 1)
        lane = lax.broadcasted_iota(jnp.int32, (CHUNK, 128), 1)
        gbv = gb_ref[...]
        gc = _dot_hi(_tri(ii, jj), gbv)
        for h in range(HEADS):
            sl = slice(128 * h, 128 * (h + 1))
            qh = qkv_ref[:, 128 * h:128 * (h + 1)]
            kh = qkv_ref[:, 1024 + 128 * h:1024 + 128 * (h + 1)]
            vh = qkv_ref[:, 2048 + 128 * h:2048 + 128 * (h + 1)]
            c = _chunk_local(qh, kh, vh, _col(gc, GA_LANE + h, lane), _col(gbv, GB_LANE + h, lane), ii, jj)
            st = state[sl, :]
            st_ref[sl, :] = st
            vn = c["u"] - _dot(c["w"], st)
            o_ref[:, sl] = _dot(c["qd"], st) + _dot(c["amat"], vn)
            state[sl, :] = st * c["gl"] + _dot(c["kd"], vn, TN)

    return _call(body, name="gdn_fwd", grid=(n,),
                 in_specs=[_rows(CHUNK, 3072), _rows(CHUNK, 128)],
                 out_specs=[_rows(CHUNK, 1024), _rows(HEADS * 128, 128)],
                 out_shape=[_sds((s, 1024), F32), _sds((n * HEADS * 128, 128), F32)],
                 scratch=[pltpu.VMEM((HEADS * 128, 128), F32)], args=(qkv, gbeta))


def _gdn_bwd(qkv, gbeta, states, do):
    s = qkv.shape[0]
    n = s // CHUNK

    def body(qkv_ref, gb_ref, st_ref, do_ref, dqkv_ref, dgb_ref, dstate):
        @pl.when(pl.program_id(0) == 0)
        def _():
            dstate[...] = jnp.zeros_like(dstate)

        ii = lax.broadcasted_iota(jnp.int32, (CHUNK, CHUNK), 0)
        jj = lax.broadcasted_iota(jnp.int32, (CHUNK, CHUNK), 1)
        lane = lax.broadcasted_iota(jnp.int32, (CHUNK, 128), 1)
        row1 = ii[:, :1]
        gbv = gb_ref[...]
        gc = _dot_hi(_tri(ii, jj), gbv)
        dgc_all = jnp.zeros((CHUNK, 128), F32)
        db_all = jnp.zeros((CHUNK, 128), F32)
        for h in range(HEADS):
            sl = slice(128 * h, 128 * (h + 1))
            qh = qkv_ref[:, 128 * h:128 * (h + 1)]
            kh = qkv_ref[:, 1024 + 128 * h:1024 + 128 * (h + 1)]
            vh = qkv_ref[:, 2048 + 128 * h:2048 + 128 * (h + 1)]
            bcol = _col(gbv, GB_LANE + h, lane)
            c = _chunk_local(qh, kh, vh, _col(gc, GA_LANE + h, lane), bcol, ii, jj)
            st = st_ref[sl, :]
            dst = dstate[sl, :]
            dov = do_ref[:, sl]
            vn = c["u"] - _dot(c["w"], st)
            dvn = _dot(c["amat"], dov, TN) + _dot(c["kd"], dst)
            damat = jnp.where(c["lower"], _dot(dov, vn, NT), 0.0)
            dqd = _dot(dov, st, NT)
            dkd = _dot(vn, dst, NT)
            dw = -_dot(dvn, st, NT)
            dgl = jnp.sum(jnp.sum(st * dst, axis=-1, keepdims=True), axis=0, keepdims=True)
            dstate[sl, :] = _dot(c["qd"], dov, TN) + c["gl"] * dst - _dot(c["w"], dvn, TN)
            dsol = jnp.concatenate([dvn, dw], axis=-1)
            drhs = _dot_hi(c["tinv"], dsol, TN)
            dtinv = _dot_hi(dsol, c["rhs"], NT)
            dl = -_dot_hi(_dot_hi(c["tinv"], dtinv, TN), c["tinv"], NT)
            dl = jnp.where(c["strict"], dl, 0.0)
            dmm = dl * c["decay"]
            dqk = damat * c["decay"]
            wmat = dl * c["lmat"] + damat * c["amat"]
            dgc = jnp.sum(wmat, axis=-1, keepdims=True)
            wcol = jnp.sum(wmat, axis=0, keepdims=True)
            dgc = dgc - jnp.sum(jnp.where(c["eye"], wcol, 0.0), axis=-1, keepdims=True)
            dkb = _dot(dmm, kh) + drhs[:, HEAD_DIM:] * c["egc"]
            dk = _dot(dmm, c["kb"], TN) + _dot(dqk, qh, TN) + dkd * c["ekd"]
            dq = _dot(dqk, kh) + dqd * c["egc"]
            dgc = dgc + jnp.sum(drhs[:, HEAD_DIM:] * c["kbg"], axis=-1, keepdims=True)
            dgc = dgc + jnp.sum(dqd * c["qd"], axis=-1, keepdims=True)
            tmp = jnp.sum(dkd * c["kd"], axis=-1, keepdims=True)
            dgc = dgc - tmp
            dglast = jnp.sum(tmp, axis=0, keepdims=True) + dgl * c["gl"]
            dgc = dgc + jnp.where(row1 == CHUNK - 1, dglast, 0.0)
            dk = dk + dkb * bcol
            db = jnp.sum(dkb * kh, axis=-1, keepdims=True) + jnp.sum(drhs[:, :HEAD_DIM] * vh, axis=-1, keepdims=True)
            dqkv_ref[:, 128 * h:128 * (h + 1)] = dq
            dqkv_ref[:, 1024 + 128 * h:1024 + 128 * (h + 1)] = dk
            dqkv_ref[:, 2048 + 128 * h:2048 + 128 * (h + 1)] = drhs[:, :HEAD_DIM] * bcol
            dgc_all = dgc_all + jnp.where(lane == GA_LANE + h, dgc, 0.0)
            db_all = db_all + jnp.where(lane == GB_LANE + h, db, 0.0)
        dgb_ref[...] = _dot_hi(_tri(jj, ii), dgc_all) + db_all

    rev = lambda w: pl.BlockSpec((CHUNK, w), lambda i: (n - 1 - i, 0))
    return _call(body, name="gdn_bwd", grid=(n,),
                 in_specs=[rev(3072), rev(128), pl.BlockSpec((HEADS * 128, 128), lambda i: (n - 1 - i, 0)), rev(1024)],
                 out_specs=[rev(3072), rev(128)],
                 out_shape=[_sds((s, 3072), F32), _sds((s, 128), F32)],
                 scratch=[pltpu.VMEM((HEADS * 128, 128), F32)], args=(qkv, gbeta, states, do))


NN_B = (((2,), (1,)), ((0,), (0,)))
NT_B = (((2,), (2,)), ((0,), (0,)))
TN_B = (((1,), (1,)), ((0,), (0,)))
GDN_PAR_CHUNKS = 2
GDN_SEQ_CHUNKS = 4


def _gather_heads(qkv_ref, gc, gbv, qs, ks, vs, gs, bs, nchunks):
    lane = lax.broadcasted_iota(jnp.int32, (CHUNK, 128), 1)
    for c in range(nchunks):
        rows = slice(CHUNK * c, CHUNK * (c + 1))
        for h in range(HEADS):
            b = HEADS * c + h
            qs[b] = qkv_ref[rows, 128 * h:128 * (h + 1)]
            ks[b] = qkv_ref[rows, 1024 + 128 * h:1024 + 128 * (h + 1)]
            vs[b] = qkv_ref[rows, 2048 + 128 * h:2048 + 128 * (h + 1)]
            gs[b] = jnp.broadcast_to(_col(gc[rows], GA_LANE + h, lane), (CHUNK, 128))
            bs[b] = jnp.broadcast_to(_col(gbv[rows], GB_LANE + h, lane), (CHUNK, 128))


def _block_tri(rows, transpose=False):
    ri = lax.broadcasted_iota(jnp.int32, (rows, rows), 0)
    ci = lax.broadcasted_iota(jnp.int32, (rows, rows), 1)
    same = (ri >> 6) == (ci >> 6)
    return jnp.where(same & ((ci >= ri) if transpose else (ri >= ci)), 1.0, 0.0)


def _local_b(q, k, v, g128, b128):
    ii = lax.broadcasted_iota(jnp.int32, (1, CHUNK, CHUNK), 1)
    jj = lax.broadcasted_iota(jnp.int32, (1, CHUNK, CHUNK), 2)
    lower, strict, eye = ii >= jj, ii > jj, ii == jj
    g64 = g128[:, :, :CHUNK]
    grow = jnp.sum(jnp.where(eye, g64, 0.0), axis=1, keepdims=True)
    decay = jnp.where(lower, jnp.exp(jnp.where(lower, g64 - grow, 0.0)), 0.0)
    kb = k * b128
    vb = v * b128
    mm = lax.dot_general(kb.astype(BF16), k.astype(BF16), NT_B, preferred_element_type=F32)
    lmat = jnp.where(strict, mm * decay, 0.0)
    egc = jnp.exp(g128)
    kbg = kb * egc
    qk = lax.dot_general(q.astype(BF16), k.astype(BF16), NT_B, preferred_element_type=F32)
    row = lax.broadcasted_iota(jnp.int32, (1, CHUNK, 128), 1)
    glast = jnp.sum(jnp.where(row == CHUNK - 1, g128, 0.0), axis=1, keepdims=True)
    ekd = jnp.exp(glast - g128)
    return dict(decay=decay, kb=kb, vb=vb, lmat=lmat, egc=egc, kbg=kbg, amat=qk * decay, qd=q * egc, ekd=ekd,
                kd=k * ekd, gl=jnp.exp(glast), lower=lower, strict=strict, eye=eye)


def _bdot(a, b, dims):
    return lax.dot_general(a.astype(BF16), b.astype(BF16), dims, preferred_element_type=F32)


def _split(a):
    hi = a.astype(BF16)
    return hi, (a - hi.astype(F32)).astype(BF16)


def _bdot_hi(a, b, dims):
    ah, al = _split(a)
    bh, bl = _split(b)
    d = lambda x, y: lax.dot_general(x, y, dims, preferred_element_type=F32)
    return d(ah, bh) + d(ah, bl) + d(al, bh)


def _gdn_pre(qkv, gbeta):
    s = qkv.shape[0]
    n = s // CHUNK
    cb = min(GDN_PAR_CHUNKS, n)
    nb = cb * HEADS
    rows = cb * CHUNK

    def body(qkv_ref, gb_ref, u_ref, w_ref, qd_ref, kd_ref, a_ref, t_ref, gl_ref, qs, ks, vs, gs, bs):
        gbv = gb_ref[...]
        gc = _dot_hi(_block_tri(rows), gbv)
        _gather_heads(qkv_ref, gc, gbv, qs, ks, vs, gs, bs, cb)
        c = _local_b(qs[...], ks[...], vs[...], gs[...], bs[...])
        pw = -c["lmat"]
        tinv = jnp.where(c["eye"], 1.0, 0.0) + pw
        for _ in range(5):
            pw = _bdot_hi(pw, pw, NN_B)
            tinv = tinv + _bdot_hi(tinv, pw, NN_B)
        u_ref[...] = _bdot_hi(tinv, c["vb"], NN_B)
        w_ref[...] = _bdot_hi(tinv, c["kbg"], NN_B).astype(BF16)
        qd_ref[...] = c["qd"].astype(BF16)
        kd_ref[...] = c["kd"].astype(BF16)
        a_ref[...] = c["amat"].astype(BF16)
        t_ref[...] = tinv
        gl_ref[...] = c["gl"]

    b3 = lambda d: pl.BlockSpec((nb, CHUNK, d), lambda i: (i, 0, 0))
    nt = n * HEADS
    return _call(body, name="gdn_pre", grid=(n // cb,),
                 in_specs=[_rows(rows, 3072), _rows(rows, 128)],
                 out_specs=[b3(128), b3(128), b3(128), b3(128), b3(CHUNK), b3(CHUNK),
                            pl.BlockSpec((nb, 1, 128), lambda i: (i, 0, 0))],
                 out_shape=[_sds((nt, CHUNK, 128), F32), _sds((nt, CHUNK, 128), BF16), _sds((nt, CHUNK, 128), BF16),
                            _sds((nt, CHUNK, 128), BF16), _sds((nt, CHUNK, CHUNK), BF16), _sds((nt, CHUNK, CHUNK), F32),
                            _sds((nt, 1, 128), F32)],
                 scratch=[pltpu.VMEM((nb, CHUNK, 128), F32)] * 5, sem=("parallel",), args=(qkv, gbeta))


def _gdn_scan_fwd(u, w, qd, kd, amat, gl):
    nt = u.shape[0]
    n = nt // HEADS
    cs = min(GDN_SEQ_CHUNKS, n)

    def body(u_ref, w_ref, qd_ref, kd_ref, a_ref, gl_ref, o_ref, st_ref, state):
        @pl.when(pl.program_id(0) == 0)
        def _():
            state[...] = jnp.zeros_like(state)

        for c in range(cs):
            sl = slice(HEADS * c, HEADS * (c + 1))
            st = state[...]
            stb = st.astype(BF16)
            st_ref[sl] = stb
            vn = u_ref[sl] - lax.dot_general(w_ref[sl], stb, NN_B, preferred_element_type=F32)
            vnb = vn.astype(BF16)
            o = (lax.dot_general(qd_ref[sl], stb, NN_B, preferred_element_type=F32)
                 + lax.dot_general(a_ref[sl], vnb, NN_B, preferred_element_type=F32))
            state[...] = st * gl_ref[sl] + lax.dot_general(kd_ref[sl], vnb, TN_B, preferred_element_type=F32)
            for h in range(HEADS):
                o_ref[CHUNK * c:CHUNK * (c + 1), 128 * h:128 * (h + 1)] = o[h]

    b3 = lambda d: pl.BlockSpec((cs * HEADS, CHUNK, d), lambda i: (i, 0, 0))
    return _call(body, name="gdn_scan_fwd", grid=(n // cs,),
                 in_specs=[b3(128), b3(128), b3(128), b3(128), b3(CHUNK), pl.BlockSpec((cs * HEADS, 1, 128), lambda i: (i, 0, 0))],
                 out_specs=[_rows(cs * CHUNK, 1024), pl.BlockSpec((cs * HEADS, 128, 128), lambda i: (i, 0, 0))],
                 out_shape=[_sds((n * CHUNK, 1024), F32), _sds((nt, 128, 128), BF16)],
                 scratch=[pltpu.VMEM((HEADS, 128, 128), F32)], args=(u, w, qd, kd, amat, gl))


def _gdn_scan_bwd(w, qd, kd, amat, gl, do):
    nt = w.shape[0]
    n = nt // HEADS
    cs = min(GDN_SEQ_CHUNKS, n)
    ng = n // cs

    def body(w_ref, qd_ref, kd_ref, a_ref, gl_ref, do_ref, ds_ref, dstate, dos):
        @pl.when(pl.program_id(0) == 0)
        def _():
            dstate[...] = jnp.zeros_like(dstate)

        for c in reversed(range(cs)):
            sl = slice(HEADS * c, HEADS * (c + 1))
            for h in range(HEADS):
                dos[h] = do_ref[CHUNK * c:CHUNK * (c + 1), 128 * h:128 * (h + 1)].astype(BF16)
            dob = dos[...]
            dst = dstate[...]
            dstb = dst.astype(BF16)
            ds_ref[sl] = dstb
            dvn = (lax.dot_general(a_ref[sl], dob, TN_B, preferred_element_type=F32)
                   + lax.dot_general(kd_ref[sl], dstb, NN_B, preferred_element_type=F32))
            dstate[...] = (lax.dot_general(qd_ref[sl], dob, TN_B, preferred_element_type=F32) + gl_ref[sl] * dst
                           - lax.dot_general(w_ref[sl], dvn.astype(BF16), TN_B, preferred_element_type=F32))

    b3 = lambda d: pl.BlockSpec((cs * HEADS, CHUNK, d), lambda i: (ng - 1 - i, 0, 0))
    return _call(body, name="gdn_scan_bwd", grid=(ng,),
                 in_specs=[b3(128), b3(128), b3(128), b3(CHUNK), pl.BlockSpec((cs * HEADS, 1, 128), lambda i: (ng - 1 - i, 0, 0)),
                           pl.BlockSpec((cs * CHUNK, 1024), lambda i: (ng - 1 - i, 0))],
                 out_specs=pl.BlockSpec((cs * HEADS, 128, 128), lambda i: (ng - 1 - i, 0, 0)),
                 out_shape=_sds((nt, 128, 128), BF16),
                 scratch=[pltpu.VMEM((HEADS, 128, 128), F32), pltpu.VMEM((HEADS, CHUNK, 128), BF16)],
                 args=(w, qd, kd, amat, gl, do))


def _gdn_post_bwd(qkv, gbeta, u, w, tinv, states, dstates, do):
    s = qkv.shape[0]
    n = s // CHUNK
    cb = min(GDN_PAR_CHUNKS, n)
    nb = cb * HEADS
    rows = cb * CHUNK

    def body(qkv_ref, gb_ref, u_ref, w_ref, t_ref, st_ref, ds_ref, do_ref, dqkv_ref, dgb_ref, qs, ks, vs, gs, bs, dos):
        gbv = gb_ref[...]
        gc = _dot_hi(_block_tri(rows), gbv)
        _gather_heads(qkv_ref, gc, gbv, qs, ks, vs, gs, bs, cb)
        for c in range(cb):
            for h in range(HEADS):
                dos[HEADS * c + h] = do_ref[CHUNK * c:CHUNK * (c + 1), 128 * h:128 * (h + 1)]
        q, k, v, b128 = qs[...], ks[...], vs[...], bs[...]
        c = _local_b(q, k, v, gs[...], b128)
        tinv, st, dst, dov = t_ref[...], st_ref[...], ds_ref[...], dos[...]
        wv = w_ref[...]
        vn = u_ref[...] - _bdot(wv, st, NN_B)
        dvn = _bdot(c["amat"], dov, TN_B) + _bdot(c["kd"], dst, NN_B)
        damat = jnp.where(c["lower"], _bdot(dov, vn, NT_B), 0.0)
        dqd = _bdot(dov, st, NT_B)
        dkd = _bdot(vn, dst, NT_B)
        dw = -_bdot(dvn, st, NT_B)
        dgl = jnp.sum(jnp.sum(st.astype(F32) * dst.astype(F32), axis=1, keepdims=True), axis=-1, keepdims=True)
        dvb = _bdot(tinv, dvn, TN_B)
        dkbg = _bdot(tinv, dw, TN_B)
        dtinv = _bdot(dvn, c["vb"], NT_B) + _bdot(dw, c["kbg"], NT_B)
        dl = -_bdot(_bdot(tinv, dtinv, TN_B), tinv, NT_B)
        dl = jnp.where(c["strict"], dl, 0.0)
        dmm = dl * c["decay"]
        dqk = damat * c["decay"]
        wmat = dl * c["lmat"] + damat * c["amat"]
        wcol = jnp.sum(wmat, axis=1, keepdims=True)
        dgc = jnp.sum(wmat, axis=-1, keepdims=True) - jnp.sum(jnp.where(c["eye"], wcol, 0.0), axis=-1, keepdims=True)
        dkb = _bdot(dmm, k, NN_B) + dkbg * c["egc"]
        dk = _bdot(dmm, c["kb"], TN_B) + _bdot(dqk, q, TN_B) + dkd * c["ekd"] + dkb * b128
        dq = _bdot(dqk, k, NN_B) + dqd * c["egc"]
        tmp = jnp.sum(dkd * c["kd"], axis=-1, keepdims=True)
        dgc = (dgc + jnp.sum(dkbg * c["kbg"], axis=-1, keepdims=True) + jnp.sum(dqd * c["qd"], axis=-1, keepdims=True)
               - tmp)
        dglast = jnp.sum(tmp, axis=1, keepdims=True) + dgl * c["gl"][:, :, :1]
        row1 = lax.broadcasted_iota(jnp.int32, (1, CHUNK, 1), 1)
        dgc = dgc + jnp.where(row1 == CHUNK - 1, dglast, 0.0)
        db = jnp.sum(dkb * k, axis=-1, keepdims=True) + jnp.sum(dvb * v, axis=-1, keepdims=True)
        dv = dvb * b128
        lane = lax.broadcasted_iota(jnp.int32, (CHUNK, 128), 1)
        parts = []
        for cc in range(cb):
            acc = jnp.zeros((CHUNK, 128), F32)
            for h in range(HEADS):
                bi = HEADS * cc + h
                rs = slice(CHUNK * cc, CHUNK * (cc + 1))
                dqkv_ref[rs, 128 * h:128 * (h + 1)] = dq[bi]
                dqkv_ref[rs, 1024 + 128 * h:1024 + 128 * (h + 1)] = dk[bi]
                dqkv_ref[rs, 2048 + 128 * h:2048 + 128 * (h + 1)] = dv[bi]
                acc = acc + jnp.where(lane == GA_LANE + h, dgc[bi], 0.0)
            parts.append(acc)
        dgc_all = jnp.concatenate(parts, axis=0)
        dg_all = _dot_hi(_block_tri(rows, transpose=True), dgc_all)
        for cc in range(cb):
            acc = dg_all[CHUNK * cc:CHUNK * (cc + 1)]
            for h in range(HEADS):
                acc = acc + jnp.where(lane == GB_LANE + h, db[HEADS * cc + h], 0.0)
            dgb_ref[CHUNK * cc:CHUNK * (cc + 1), :] = acc

    b3 = lambda d1, d2: pl.BlockSpec((nb, d1, d2), lambda i: (i, 0, 0))
    return _call(body, name="gdn_post_bwd", grid=(n // cb,),
                 in_specs=[_rows(rows, 3072), _rows(rows, 128), b3(CHUNK, 128), b3(CHUNK, 128), b3(CHUNK, CHUNK),
                           b3(128, 128), b3(128, 128), _rows(rows, 1024)],
                 out_specs=[_rows(rows, 3072), _rows(rows, 128)],
                 out_shape=[_sds((s, 3072), F32), _sds((s, 128), F32)],
                 scratch=[pltpu.VMEM((nb, CHUNK, 128), F32)] * 6, sem=("parallel",),
                 args=(qkv, gbeta, u, w, tinv, states, dstates, do))


def _mix_fwd(o_mla, o_gdn, proj, out_gain):
    s = proj.shape[0]
    tm = min(TILES["row"], s)

    def body(om_ref, og_ref, mg_ref, gg_ref, g_ref, o_ref):
        o_ref[:, :1024] = (om_ref[...] * _silu(mg_ref[...])).astype(BF16)
        for h in range(HEADS):
            sl = slice(128 * h, 128 * (h + 1))
            og = og_ref[:, sl]
            on = og * _rms(og, HEAD_DIM) * g_ref[...]
            o_ref[:, 1024 + 128 * h:1024 + 128 * (h + 1)] = (on * _silu(gg_ref[:, sl])).astype(BF16)

    return _call(body, name="mix_fwd", grid=(s // tm,),
                 in_specs=[_rows(tm, 1024), _rows(tm, 1024), _rows(tm, 1024, 1), _rows(tm, 1024, 5), _full((1, 128))],
                 out_specs=_rows(tm, 2048), out_shape=_sds((s, 2048), BF16), args=(o_mla, o_gdn, proj, proj, out_gain))


def _mix_bwd(o_mla, o_gdn, proj, out_gain, dmixed):
    s = proj.shape[0]
    tm = min(TILES["row"], s)

    def body(om_ref, og_ref, mg_ref, gg_ref, g_ref, dm_ref, dg_ref, dom_ref, dog_ref, dmg_ref, dgg_ref, ag_ref):
        @pl.when(pl.program_id(0) == 0)
        def _():
            ag_ref[...] = jnp.zeros_like(ag_ref)

        mg = mg_ref[...]
        dom_ref[...] = dm_ref[...] * _silu(mg)
        dmg_ref[...] = (dm_ref[...] * om_ref[...] * _dsilu(mg)).astype(BF16)
        for h in range(HEADS):
            sl = slice(128 * h, 128 * (h + 1))
            og, gg, d = og_ref[:, sl], gg_ref[:, sl], dg_ref[:, sl]
            on = og * _rms(og, HEAD_DIM) * g_ref[...]
            dgg_ref[:, sl] = (d * on * _dsilu(gg)).astype(BF16)
            dx, gpart = _rms_bwd(og, g_ref[...], d * _silu(gg), HEAD_DIM)
            dog_ref[:, sl] = dx
            ag_ref[...] += _acc8(gpart)

    return _call(body, name="mix_bwd", grid=(s // tm,),
                 in_specs=[_rows(tm, 1024), _rows(tm, 1024), _rows(tm, 1024, 1), _rows(tm, 1024, 5), _full((1, 128)),
                           _rows(tm, 1024, 0), _rows(tm, 1024, 1)],
                 out_specs=[_rows(tm, 1024), _rows(tm, 1024), _rows(tm, 1024), _rows(tm, 1024), _full((8, 128))],
                 out_shape=[_sds((s, 1024), F32), _sds((s, 1024), F32), _sds((s, 1024), BF16), _sds((s, 1024), BF16),
                            _sds((8, 128), F32)],
                 args=(o_mla, o_gdn, proj, proj, out_gain, dmixed, dmixed))


def _out_fwd(mixed, w_out, x, target):
    s = x.shape[0]
    tm = min(TILES["mm"], s)
    tn = min(TILES["mm"], D_MODEL)

    def body(m_ref, w_ref, x_ref, t_ref, dy_ref, acc_ref):
        err = x_ref[...] + _dot(m_ref[...], w_ref[...]) - t_ref[...]
        dy_ref[...] = err * (1.0 / D_MODEL)

        @pl.when(pl.program_id(1) == 0)
        def _():
            acc_ref[...] = jnp.zeros_like(acc_ref)

        acc_ref[...] += _acc8(err * err)

    return _call(body, name="out_fwd", grid=(D_MODEL // tn, s // tm),
                 in_specs=[pl.BlockSpec((tm, D_MODEL), lambda j, i: (i, 0)), pl.BlockSpec((D_MODEL, tn), lambda j, i: (0, j)),
                           pl.BlockSpec((tm, tn), lambda j, i: (i, j)), pl.BlockSpec((tm, tn), lambda j, i: (i, j))],
                 out_specs=[pl.BlockSpec((tm, tn), lambda j, i: (i, j)), pl.BlockSpec((8, tn), lambda j, i: (0, j))],
                 out_shape=[_sds((s, D_MODEL), F32), _sds((8, D_MODEL), F32)], args=(mixed, w_out, x, target))


def _row_tile(r, c):
    if r % 8 != 0:
        return r
    t = 8
    while r % (2 * t) == 0 and 2 * t * c * 4 <= (1 << 20):
        t *= 2
    return t


def _sum_arrays(parts, name, also_bf16=False):
    r, c = parts[0].shape
    tr = _row_tile(r, c)
    n = len(parts)

    def body(*refs):
        acc = refs[0][...].astype(F32)
        for p_ref in refs[1:n]:
            acc = acc + p_ref[...].astype(F32)
        refs[n][...] = acc
        if also_bf16:
            refs[n + 1][...] = acc.astype(BF16)

    nout = 2 if also_bf16 else 1
    out = _call(body, name=name, grid=(r // tr,), in_specs=[_rows(tr, c)] * n, out_specs=[_rows(tr, c)] * nout,
                out_shape=[_sds((r, c), F32), _sds((r, c), BF16)][:nout], args=tuple(parts))
    return out if also_bf16 else out[0]


def _adamw(w, g, m, v, name):
    r, c = w.shape
    tr = _row_tile(r, c)
    c1 = 1.0 - ADAM_B1 ** ADAM_STEP
    c2 = 1.0 - ADAM_B2 ** ADAM_STEP

    def body(w_ref, g_ref, m_ref, v_ref, d_ref, nm_ref, nv_ref):
        gv = g_ref[...]
        nm = ADAM_B1 * m_ref[...] + (1.0 - ADAM_B1) * gv
        nv = ADAM_B2 * v_ref[...] + (1.0 - ADAM_B2) * (gv * gv)
        nm_ref[...] = nm
        nv_ref[...] = nv
        d_ref[...] = -ADAM_LR * ((nm / c1) / (jnp.sqrt(nv / c2) + ADAM_EPS) + ADAM_WD * w_ref[...])

    return _call(body, name=name, grid=(r // tr,), in_specs=[_rows(tr, c)] * 4, out_specs=[_rows(tr, c)] * 3,
                 out_shape=[_sds((r, c), F32)] * 3, args=(w, g, m, v))


ANY = pl.BlockSpec(memory_space=pl.ANY)
CHIP_FLIPS = ((1, 0), (0, 1), (1, 1))


def _comm_call(body, *, name, n_in, out_shape, scratch):
    def kfn(*refs):
        body(*refs)
    return pl.pallas_call(kfn, name=name, in_specs=[ANY] * n_in, out_specs=[ANY] * len(out_shape), out_shape=out_shape,
                          scratch_shapes=list(scratch),
                          compiler_params=pltpu.CompilerParams(has_side_effects=True))


def _all_gather_chips(shards):
    na = len(shards)

    def body(*refs):
        copies = _gather_copies(refs[:na], refs[na:2 * na], *refs[2 * na:])
        _gather_start(copies)
        _gather_finish(copies)

    out_shape = [_sds((4,) + a.shape, a.dtype) for a in shards]
    sem = pltpu.SemaphoreType.DMA((na, 3))
    got = _comm_call(body, name="all_gather_weights", n_in=na, out_shape=out_shape, scratch=[sem, sem, sem, sem])(*shards)
    return _place_own_blocks(got, shards)


def _gather_copies(ins, outs, send_sems, recv_sems, fwd_send, fwd_recv):
    x, y, c = lax.axis_index("x"), lax.axis_index("y"), lax.axis_index("c")
    my_k = 2 * x + y
    direct, forwards = [], []
    for a in range(len(ins)):
        rows = ins[a].shape[0]
        for r, (fx, fy) in enumerate(CHIP_FLIPS):
            px, py = x ^ fx, y ^ fy
            if rows % 32 == 0:
                mine = pl.ds(pl.multiple_of(c * (rows // 2), 16), rows // 2)
                other = pl.ds(pl.multiple_of((1 - c) * (rows // 2), 16), rows // 2)
                rc = pltpu.make_async_remote_copy(
                    src_ref=ins[a].at[mine], dst_ref=outs[a].at[my_k, mine], send_sem=send_sems.at[a, r],
                    recv_sem=recv_sems.at[a, r], device_id=(px, py, c), device_id_type=MESH)
                landed = outs[a].at[2 * px + py, mine]
                fw = pltpu.make_async_remote_copy(
                    src_ref=landed, dst_ref=landed, send_sem=fwd_send.at[a, r], recv_sem=fwd_recv.at[a, r],
                    device_id=(x, y, 1 - c), device_id_type=MESH)
                from_sib = outs[a].at[2 * px + py, other]
                fw_in = pltpu.make_async_remote_copy(
                    src_ref=from_sib, dst_ref=from_sib, send_sem=fwd_send.at[a, r], recv_sem=fwd_recv.at[a, r],
                    device_id=(x, y, 1 - c), device_id_type=MESH)
                forwards.append((rc, fw, fw_in))
            else:
                direct.append(pltpu.make_async_remote_copy(
                    src_ref=ins[a], dst_ref=outs[a].at[my_k], send_sem=send_sems.at[a, r],
                    recv_sem=recv_sems.at[a, r], device_id=(px, py, c), device_id_type=MESH))
    return forwards, direct


def _gather_start(copies):
    forwards, direct = copies
    for rc, _, _ in forwards:
        rc.start()
    for rc in direct:
        rc.start()


def _gather_finish(copies):
    forwards, direct = copies
    for rc, fw, _ in forwards:
        rc.wait_recv()
        fw.start()
    for rc, fw, fw_in in forwards:
        rc.wait_send()
        fw.wait_send()
        fw_in.wait_recv()
    for rc in direct:
        rc.wait()


def _place_own_blocks(got, shards):
    my_k = 2 * lax.axis_index("x") + lax.axis_index("y")
    return [lax.dynamic_update_index_in_dim(g, a, my_k, 0) for g, a in zip(got, shards)]


def _matmul_nn_gather(a, b, shards, *, name, tm, tn):
    m, kdim = a.shape
    n = b.shape[1]
    ni, nj = m // tm, n // tn
    na = len(shards)

    def body(a_ref, b_ref, *rest):
        o_ref = rest[na]
        sems = rest[2 * na + 1:]
        i, j = pl.program_id(0), pl.program_id(1)

        @pl.when((i == 0) & (j == 0))
        def _():
            _gather_start(_gather_copies(rest[:na], rest[na + 1:2 * na + 1], *sems))

        o_ref[...] = _dot(a_ref[...], b_ref[...])

        @pl.when((i == ni - 1) & (j == nj - 1))
        def _():
            _gather_finish(_gather_copies(rest[:na], rest[na + 1:2 * na + 1], *sems))

    def kfn(*refs):
        body(*refs)
    sem = pltpu.SemaphoreType.DMA((na, 3))
    out = pl.pallas_call(
        kfn, name=name, grid=(ni, nj),
        in_specs=[pl.BlockSpec((tm, kdim), lambda i, j: (i, 0)), pl.BlockSpec((kdim, tn), lambda i, j: (0, j))] + [ANY] * na,
        out_specs=[pl.BlockSpec((tm, tn), lambda i, j: (i, j))] + [ANY] * na,
        out_shape=[_sds((m, n), F32)] + [_sds((4,) + s_a.shape, s_a.dtype) for s_a in shards],
        scratch_shapes=[sem, sem, sem, sem],
        compiler_params=pltpu.CompilerParams(dimension_semantics=("arbitrary", "arbitrary"),
                                             vmem_limit_bytes=VMEM_LIMIT_V7X, has_side_effects=True))(a, b, *shards)
    return out[0], _place_own_blocks(list(out[1:]), shards)


def _all_reduce_small(vec):
    r = vec.shape[0]

    def body(v_ref, o_ref, gath, send_sems, recv_sems):
        x, y, c = lax.axis_index("x"), lax.axis_index("y"), lax.axis_index("c")
        me = 4 * x + 2 * y + c
        gath[me] = v_ref[...]
        copies = []
        for rel in range(1, 8):
            fx, fy, fc = (rel >> 2) & 1, (rel >> 1) & 1, rel & 1
            rc = pltpu.make_async_remote_copy(
                src_ref=v_ref, dst_ref=gath.at[me], send_sem=send_sems.at[rel - 1], recv_sem=recv_sems.at[rel - 1],
                device_id=(x ^ fx, y ^ fy, c ^ fc), device_id_type=MESH)
            rc.start()
            copies.append(rc)
        for rc in copies:
            rc.wait()
        acc = gath[0]
        for d in range(1, 8):
            acc = acc + gath[d]
        o_ref[...] = acc

    def kfn(*refs):
        body(*refs)
    vm = pl.BlockSpec(memory_space=pltpu.VMEM)
    return pl.pallas_call(kfn, name="all_reduce_small", in_specs=[vm], out_specs=vm, out_shape=_sds((r, 128), F32),
                          scratch_shapes=[pltpu.VMEM((8, r, 128), F32), pltpu.SemaphoreType.DMA((7,)),
                                          pltpu.SemaphoreType.DMA((7,))],
                          compiler_params=pltpu.CompilerParams(has_side_effects=True))(vec)


def _exchange_halves(arrs):
    na = len(arrs)

    def body(*refs):
        ins, outs = refs[:na], refs[na:2 * na]
        send_sems, recv_sems = refs[2 * na:]
        x, y, c = lax.axis_index("x"), lax.axis_index("y"), lax.axis_index("c")
        copies = []
        for a in range(na):
            half = ins[a].shape[1] // 2
            src = ins[a].at[:, pl.ds(pl.multiple_of((1 - c) * half, 8), half), :]
            rc = pltpu.make_async_remote_copy(src_ref=src, dst_ref=outs[a], send_sem=send_sems.at[a],
                                              recv_sem=recv_sems.at[a], device_id=(x, y, 1 - c), device_id_type=MESH)
            rc.start()
            copies.append(rc)
        for rc in copies:
            rc.wait()

    out_shape = [_sds((4, a.shape[1] // 2, a.shape[2]), F32) for a in arrs]
    return _comm_call(body, name="rs_pair_exchange", n_in=na, out_shape=out_shape,
                      scratch=[pltpu.SemaphoreType.DMA((na,)), pltpu.SemaphoreType.DMA((na,))])(*arrs)


def _scatter_to_chips(arrs):
    na = len(arrs)

    def body(*refs):
        ins, outs = refs[:na], refs[na:2 * na]
        send_sems, recv_sems = refs[2 * na:]
        x, y, c = lax.axis_index("x"), lax.axis_index("y"), lax.axis_index("c")
        copies = []
        for a in range(na):
            for r, (fx, fy) in enumerate(CHIP_FLIPS):
                px, py = x ^ fx, y ^ fy
                rc = pltpu.make_async_remote_copy(
                    src_ref=ins[a].at[2 * px + py], dst_ref=outs[a].at[r], send_sem=send_sems.at[a, r],
                    recv_sem=recv_sems.at[a, r], device_id=(px, py, c), device_id_type=MESH)
                rc.start()
                copies.append(rc)
        for rc in copies:
            rc.wait()

    out_shape = [_sds((3,) + a.shape[1:], a.dtype) for a in arrs]
    return _comm_call(body, name="rs_chip_scatter", n_in=na, out_shape=out_shape,
                      scratch=[pltpu.SemaphoreType.DMA((na, 3)), pltpu.SemaphoreType.DMA((na, 3))])(*arrs)


def _sum_into_half(parts, name):
    r2, c = parts[0].shape
    tr = _row_tile(r2, c)
    nb = r2 // tr
    n = len(parts)

    def kfn(c_ref, *refs):
        acc = refs[0][...].astype(F32)
        for p_ref in refs[1:n]:
            acc = acc + p_ref[...].astype(F32)
        refs[n][...] = acc

    spec = pltpu.PrefetchScalarGridSpec(
        num_scalar_prefetch=1, grid=(nb,), in_specs=[pl.BlockSpec((tr, c), lambda i, cr: (i, 0))] * n,
        out_specs=pl.BlockSpec((tr, c), lambda i, cr: (cr[0] * nb + i, 0)))
    core = lax.axis_index("c").astype(jnp.int32).reshape(1)
    return pl.pallas_call(kfn, name=name, grid_spec=spec, out_shape=_sds((2 * r2, c), F32),
                          compiler_params=pltpu.CompilerParams(dimension_semantics=("arbitrary",),
                                                               vmem_limit_bytes=VMEM_LIMIT_V7X))(core, *parts)


def _join_in_place(arrs):
    na = len(arrs)

    def body(*refs):
        outs = refs[na:2 * na]
        send_sems, recv_sems = refs[2 * na:]
        x, y, c = lax.axis_index("x"), lax.axis_index("y"), lax.axis_index("c")
        copies = []
        for a in range(na):
            half = outs[a].shape[0] // 2
            mine = outs[a].at[pl.ds(pl.multiple_of(c * half, 8), half), :]
            rc = pltpu.make_async_remote_copy(src_ref=mine, dst_ref=mine, send_sem=send_sems.at[a],
                                              recv_sem=recv_sems.at[a], device_id=(x, y, 1 - c), device_id_type=MESH)
            rc.start()
            copies.append(rc)
        for rc in copies:
            rc.wait()

    def kfn(*refs):
        body(*refs)
    return pl.pallas_call(kfn, name="rs_pair_join", in_specs=[ANY] * na, out_specs=[ANY] * na,
                          out_shape=[_sds(a.shape, F32) for a in arrs],
                          input_output_aliases={a: a for a in range(na)},
                          scratch_shapes=[pltpu.SemaphoreType.DMA((na,)), pltpu.SemaphoreType.DMA((na,))],
                          compiler_params=pltpu.CompilerParams(has_side_effects=True))(*arrs)


def _pair_sum(g, o, name):
    _, r, c = g.shape
    half = r // 2
    tr = _row_tile(half, c)
    nb = half // tr

    def kfn(c_ref, g_ref, o_ref, s32_ref, s16_ref):
        acc = g_ref[...] + o_ref[...]
        s32_ref[...] = acc
        s16_ref[...] = acc.astype(BF16)

    blk = lambda imap: pl.BlockSpec((None, tr, c), imap)
    same = lambda k, i, cr: (k, i, 0)
    spec = pltpu.PrefetchScalarGridSpec(
        num_scalar_prefetch=1, grid=(4, nb), in_specs=[blk(lambda k, i, cr: (k, cr[0] * nb + i, 0)), blk(same)],
        out_specs=[blk(same), blk(same)])
    core = lax.axis_index("c").astype(jnp.int32).reshape(1)
    return pl.pallas_call(kfn, name=name, grid_spec=spec, out_shape=[_sds((4, half, c), F32), _sds((4, half, c), BF16)],
                          compiler_params=pltpu.CompilerParams(dimension_semantics=("arbitrary", "arbitrary"),
                                                               vmem_limit_bytes=VMEM_LIMIT_V7X))(core, g, o)


def _rs_pair_stage(grads):
    got = _exchange_halves(grads)
    sums = [_pair_sum(g, o, f"rs_pair_sum_{a}") for a, (g, o) in enumerate(zip(grads, got))]
    return [s32 for s32, _ in sums], [s16 for _, s16 in sums]


def _rs_chip_stage(pair, recv):
    k_me = 2 * lax.axis_index("x") + lax.axis_index("y")
    halves = []
    for a, (p, rv) in enumerate(zip(pair, recv)):
        own = lax.dynamic_index_in_dim(p, k_me, 0, keepdims=False)
        halves.append(_sum_into_half([own, rv[0], rv[1], rv[2]], f"rs_chip_sum_{a}"))
    return _join_in_place(halves)


def _reduce_scatter(grads):
    pair, pair_bf16 = _rs_pair_stage(grads)
    return _rs_chip_stage(pair, _scatter_to_chips(pair_bf16))


def _matmul_nt_scatter(a, b, send, *, name, tm, tn):
    m, kdim = a.shape
    n = b.shape[0]
    ni, nj = m // tm, n // tn
    na = len(send)

    def body(a_ref, b_ref, *rest):
        send_refs, o_ref, recv_refs = rest[:na], rest[na], rest[na + 1:2 * na + 1]
        send_sems, recv_sems = rest[2 * na + 1:]
        i, j = pl.program_id(0), pl.program_id(1)

        def copies():
            x, y, c = lax.axis_index("x"), lax.axis_index("y"), lax.axis_index("c")
            out = []
            for s_i in range(na):
                for r, (fx, fy) in enumerate(CHIP_FLIPS):
                    px, py = x ^ fx, y ^ fy
                    out.append(pltpu.make_async_remote_copy(
                        src_ref=send_refs[s_i].at[2 * px + py], dst_ref=recv_refs[s_i].at[r],
                        send_sem=send_sems.at[s_i, r], recv_sem=recv_sems.at[s_i, r], device_id=(px, py, c),
                        device_id_type=MESH))
            return out

        @pl.when((i == 0) & (j == 0))
        def _():
            for cp in copies():
                cp.start()

        o_ref[...] = _dot(a_ref[...], b_ref[...], NT)

        @pl.when((i == ni - 1) & (j == nj - 1))
        def _():
            for cp in copies():
                cp.wait()

    def kfn(*refs):
        body(*refs)
    sem = pltpu.SemaphoreType.DMA((na, 3))
    out = pl.pallas_call(
        kfn, name=name, grid=(ni, nj),
        in_specs=[pl.BlockSpec((tm, kdim), lambda i, j: (i, 0)), pl.BlockSpec((tn, kdim), lambda i, j: (j, 0))] + [ANY] * na,
        out_specs=[pl.BlockSpec((tm, tn), lambda i, j: (i, j))] + [ANY] * na,
        out_shape=[_sds((m, n), F32)] + [_sds((3,) + s_a.shape[1:], s_a.dtype) for s_a in send],
        scratch_shapes=[sem, sem],
        compiler_params=pltpu.CompilerParams(dimension_semantics=("arbitrary", "arbitrary"),
                                             vmem_limit_bytes=VMEM_LIMIT_V7X, has_side_effects=True))(a, b, *send)
    return out[0], list(out[1:])


def _pad_w_in(w):
    z = jnp.zeros((w.shape[0], 1024 - 848), w.dtype)
    return jnp.concatenate([w[:, 0:832], w[:, 4928:4944], z, w[:, 832:4928], w[:, 4944:5968]], axis=1)


def _unpad_w_in(g):
    return jnp.concatenate([g[:, 0:832], g[:, 1024:5120], g[:, 832:848], g[:, 5120:6144]], axis=1)


W_IN_SHARD = W_IN_COLS // 4
W_IN_RUNS = ((0, 832, 0), (832, 4928, 1024), (4928, 4944, 832), (4944, 5968, 5120))


def _w_in_grad_blocks(p):
    def orig_cols(lo, hi):
        parts = [p[:, pa + max(lo, a) - a:pa + min(hi, b) - a] for a, b, pa in W_IN_RUNS if max(lo, a) < min(hi, b)]
        return parts[0] if len(parts) == 1 else jnp.concatenate(parts, axis=1)
    return jnp.stack([orig_cols(W_IN_SHARD * k, W_IN_SHARD * (k + 1)) for k in range(4)])


def _pad_w_in_blocks(g):
    def orig_cols(lo, hi):
        return [g[k][:, max(lo, W_IN_SHARD * k) - W_IN_SHARD * k:min(hi, W_IN_SHARD * (k + 1)) - W_IN_SHARD * k]
                for k in range(4) if max(lo, W_IN_SHARD * k) < min(hi, W_IN_SHARD * (k + 1))]
    z = jnp.zeros((g.shape[1], 1024 - 848), g.dtype)
    return jnp.concatenate(orig_cols(0, 832) + orig_cols(4928, 4944) + [z] + orig_cols(832, 4928) + orig_cols(4944, 5968),
                           axis=1)


def _pad_heads(w):
    r = w.shape[0]
    return jnp.pad(w.reshape(r, HEADS, QK_DIM), ((0, 0), (0, 0), (0, QK_PAD - QK_DIM))).reshape(r, HEADS * QK_PAD)


def _unpad_heads(w):
    r = w.shape[0]
    return w.reshape(r, HEADS, QK_PAD)[:, :, :QK_DIM].reshape(r, HEADS * QK_DIM)


def _cols_to_blocks(w):
    r = w.shape[0]
    return w.reshape(r, 4, -1).transpose(1, 0, 2)


def _blocks_to_cols(w):
    return w.transpose(1, 0, 2).reshape(w.shape[1], -1)


SMALL_ROWS = {"norm_gain": (0, 2048), "mla_q_a_gain": (16, 512), "mla_kv_a_gain": (20, 256),
              "mla_q_norm_gain": (22, 192), "mla_k_norm_gain": (24, 192), "gdn_a_log": (26, 8),
              "gdn_dt_bias": (27, 8), "gdn_out_norm_gain": (28, 128)}
LOSS_ROW = 29
SMALL_PACK_ROWS = 32
CONV_ROW = 32


def _pack_small(vals, loss=None):
    rows = []
    at = 0
    for name, (row, size) in SMALL_ROWS.items():
        assert row == at
        nr = -(-size // 128)
        rows.append(jnp.pad(vals[name].reshape(-1).astype(F32), (0, nr * 128 - size)).reshape(nr, 128))
        at += nr
    assert at == LOSS_ROW
    if loss is not None:
        rows.append(jnp.pad(loss.reshape(1, 1), ((0, 0), (0, 127))))
        at += 1
    rows.append(jnp.zeros((SMALL_PACK_ROWS - at, 128), F32))
    return jnp.concatenate(rows, axis=0)


def _unpack_small(pack, name):
    row, size = SMALL_ROWS[name]
    nr = -(-size // 128)
    return pack[row:row + nr].reshape(-1)[:size].reshape(1, size)


def _local_step(x, positions, target, norm_gain, w_in_p, q_a_gain, kv_a_gain, w_uq_p, w_ukv, q_norm_gain,
                k_norm_gain, conv_w, a_log, dt_bias, out_gain, w_out, scatter_hook=None, late_weights=None):
    half = HALF_ROPE
    inv_freq = jnp.power(10000.0, -jnp.arange(half, dtype=F32) / half)
    ang = positions.astype(F32)[:, None] * inv_freq
    cos, sin = jnp.cos(ang), jnp.sin(ang)
    zpad = jnp.zeros((x.shape[0], 64), F32)
    cs = jnp.concatenate([cos, cos, zpad], axis=1)
    sn = jnp.concatenate([-sin, sin, zpad], axis=1)
    gq = jnp.pad(q_norm_gain.reshape(1, QK_DIM), ((0, 0), (0, QK_PAD - QK_DIM)))
    gk = jnp.pad(k_norm_gain.reshape(1, QK_DIM), ((0, 0), (0, QK_PAD - QK_DIM)))
    lane_pad = ((0, 0), (GA_LANE, 128 - GA_LANE - HEADS))
    alog128 = jnp.pad(a_log.reshape(1, HEADS), lane_pad)
    dtb128 = jnp.pad(dt_bias.reshape(1, HEADS), lane_pad)
    ng, qag, kvag, og = (norm_gain.reshape(1, -1), q_a_gain.reshape(1, -1), kv_a_gain.reshape(1, -1),
                         out_gain.reshape(1, -1))

    xn = _norm1_fwd(x, ng)
    if late_weights is None:
        proj = _matmul(xn, w_in_p, mode="nn", out_dtype=F32, name="in_proj")
    else:
        shards, assemble = late_weights
        proj, gathered = _matmul_nn_gather(xn, w_in_p, shards, name="in_proj_gather", tm=TILES["mm"], tn=TILES["mm"])
        w_uq_p, w_ukv, w_out = assemble(gathered)
    cqn, ckvn = _mla_a_norm(proj, qag, kvag)
    q_pre = _matmul(cqn, w_uq_p, mode="nn", out_dtype=F32, name="q_up")
    kv_pre = _matmul(ckvn, w_ukv, mode="nn", out_dtype=F32, name="kv_up")
    q, k, v = _mla_post_fwd(q_pre, kv_pre, proj, cs, sn, gq, gk)
    o_mla, lse = _attn_fwd(q, k, v)
    qkv = _gdn_conv_fwd(proj, conv_w)
    gbeta = _gdn_gates_fwd(proj, alog128, dtb128)
    g_u, g_w, g_qd, g_kd, g_a, g_t, g_gl = _gdn_pre(qkv, gbeta)
    o_gdn, states = _gdn_scan_fwd(g_u, g_w, g_qd, g_kd, g_a, g_gl)
    mixed = _mix_fwd(o_mla, o_gdn, proj, og)
    dy, sq = _out_fwd(mixed, w_out, x, target)

    dmixed = _matmul(dy, w_out, mode="nt", out_dtype=F32, name="d_mixed")
    d_w_out = _matmul(mixed, dy, mode="tn", out_dtype=F32, name="d_w_out", tk=1024)
    do_mla, do_gdn, dmg, dgg, d_out_gain = _mix_bwd(o_mla, o_gdn, proj, og, dmixed)
    dq, dk, dv = _attn_bwd(q, k, v, o_mla, lse, do_mla)
    dq_pre, dkv_pre, dkr, d_gq, d_gk = _mla_post_bwd(q_pre, kv_pre, proj, cs, sn, gq, gk, dq, dk, dv)
    d_w_uq_p = _matmul(cqn, dq_pre, mode="tn", out_dtype=F32, name="d_w_uq", tk=1024)
    d_w_ukv = _matmul(ckvn, dkv_pre, mode="tn", out_dtype=F32, name="d_w_ukv", tk=1024)
    dcqn = _matmul(dq_pre, w_uq_p, mode="nt", out_dtype=F32, name="d_cqn")
    dckvn = _matmul(dkv_pre, w_ukv, mode="nt", out_dtype=F32, name="d_ckvn")
    dcq, dckv, d_qag, d_kvag = _mla_a_norm_bwd(proj, qag, kvag, dcqn, dckvn)
    dstates = _gdn_scan_bwd(g_w, g_qd, g_kd, g_a, g_gl, do_gdn)
    dqkv, dgbeta = _gdn_post_bwd(qkv, gbeta, g_u, g_w, g_t, states, dstates, do_gdn)
    dz = _gdn_conv_bwd_a(proj, conv_w, dqkv)
    dgx, d_conv = _gdn_conv_bwd_b(proj, conv_w, dz)
    dmisc, d_alog, d_dtb = _gdn_gates_bwd(proj, alog128, dtb128, gbeta, dgbeta, dkr)
    dproj = jnp.concatenate([dcq, dckv, dmisc, jnp.zeros((x.shape[0], 128), BF16), dmg, dgx, dgg], axis=1)
    d_w_in_p = _matmul(xn, dproj, mode="tn", out_dtype=F32, name="d_w_in", tk=1024)
    big = {"w_in": d_w_in_p, "w_uq": d_w_uq_p, "w_ukv": d_w_ukv, "w_out": d_w_out, "gdn_conv_w": d_conv}
    if scatter_hook is None:
        dxn, received = _matmul(dproj, w_in_p, mode="nt", out_dtype=F32, name="d_xn", tm=512, tn=512), None
    else:
        dxn, received = _matmul_nt_scatter(dproj, w_in_p, scatter_hook(big), name="d_xn_scatter", tm=512, tn=512)
    grad_x, d_ng = _norm1_bwd(x, ng, dxn, dy)

    small = {"norm_gain": d_ng.sum(0), "mla_q_a_gain": d_qag.sum(0), "mla_kv_a_gain": d_kvag.sum(0),
             "mla_q_norm_gain": d_gq.sum(0)[:QK_DIM], "mla_k_norm_gain": d_gk.sum(0)[:QK_DIM],
             "gdn_a_log": d_alog.sum(0)[GA_LANE:GA_LANE + HEADS], "gdn_dt_bias": d_dtb.sum(0)[GA_LANE:GA_LANE + HEADS],
             "gdn_out_norm_gain": d_out_gain.sum(0)}
    return sq, grad_x, small, big, received


WEIGHTS = ["norm_gain", "w_in", "mla_q_a_gain", "mla_kv_a_gain", "w_uq", "w_ukv", "mla_q_norm_gain", "mla_k_norm_gain",
           "gdn_conv_w", "gdn_a_log", "gdn_dt_bias", "gdn_out_norm_gain", "w_out"]
BIG = ["w_in", "w_uq", "w_ukv", "w_out"]


def kernel(x, positions, norm_gain, w_in, mla_q_a_gain, mla_kv_a_gain, w_uq, w_ukv, mla_q_norm_gain, mla_k_norm_gain, gdn_conv_w, gdn_a_log, gdn_dt_bias, gdn_out_norm_gain, w_out, loss_target, m_norm_gain, m_w_in, m_mla_q_a_gain, m_mla_kv_a_gain, m_w_uq, m_w_ukv, m_mla_q_norm_gain, m_mla_k_norm_gain, m_gdn_conv_w, m_gdn_a_log, m_gdn_dt_bias, m_gdn_out_norm_gain, m_w_out, v_norm_gain, v_w_in, v_mla_q_a_gain, v_mla_kv_a_gain, v_w_uq, v_w_ukv, v_mla_q_norm_gain, v_mla_k_norm_gain, v_gdn_conv_w, v_gdn_a_log, v_gdn_dt_bias, v_gdn_out_norm_gain, v_w_out):
    w = dict(norm_gain=norm_gain, w_in=w_in, mla_q_a_gain=mla_q_a_gain, mla_kv_a_gain=mla_kv_a_gain, w_uq=w_uq,
             w_ukv=w_ukv, mla_q_norm_gain=mla_q_norm_gain, mla_k_norm_gain=mla_k_norm_gain, gdn_conv_w=gdn_conv_w,
             gdn_a_log=gdn_a_log, gdn_dt_bias=gdn_dt_bias, gdn_out_norm_gain=gdn_out_norm_gain, w_out=w_out)
    m = dict(norm_gain=m_norm_gain, w_in=m_w_in, mla_q_a_gain=m_mla_q_a_gain, mla_kv_a_gain=m_mla_kv_a_gain,
             w_uq=m_w_uq, w_ukv=m_w_ukv, mla_q_norm_gain=m_mla_q_norm_gain, mla_k_norm_gain=m_mla_k_norm_gain,
             gdn_conv_w=m_gdn_conv_w, gdn_a_log=m_gdn_a_log, gdn_dt_bias=m_gdn_dt_bias,
             gdn_out_norm_gain=m_gdn_out_norm_gain, w_out=m_w_out)
    v = dict(norm_gain=v_norm_gain, w_in=v_w_in, mla_q_a_gain=v_mla_q_a_gain, mla_kv_a_gain=v_mla_kv_a_gain,
             w_uq=v_w_uq, w_ukv=v_w_ukv, mla_q_norm_gain=v_mla_q_norm_gain, mla_k_norm_gain=v_mla_k_norm_gain,
             gdn_conv_w=v_gdn_conv_w, gdn_a_log=v_gdn_a_log, gdn_dt_bias=v_gdn_dt_bias,
             gdn_out_norm_gain=v_gdn_out_norm_gain, w_out=v_w_out)
    k_me = 2 * lax.axis_index("x") + lax.axis_index("y")

    g_in, g_conv = _all_gather_chips([w_in[0].astype(BF16), gdn_conv_w[0]])
    w_in_p = _pad_w_in_blocks(g_in)
    conv_f = _blocks_to_cols(g_conv)
    late_weights = ([w_uq[0].astype(BF16), w_ukv[0].astype(BF16), w_out[0].astype(BF16)],
                    lambda g: (_pad_heads(_blocks_to_cols(g[0])), _blocks_to_cols(g[1]), g[2].reshape(D_MODEL, D_MODEL)))

    pair_sums = []

    def scatter_hook(big):
        pair, pair_bf16 = _rs_pair_stage([
            _w_in_grad_blocks(big["w_in"]), _cols_to_blocks(_unpad_heads(big["w_uq"])),
            _cols_to_blocks(big["w_ukv"]), big["w_out"].reshape(4, 512, D_MODEL)])
        pair_sums.extend(pair)
        return pair_bf16

    sq, grad_x, small, big, received = _local_step(
        x[0], positions[0], loss_target[0], norm_gain, w_in_p, mla_q_a_gain, mla_kv_a_gain, None, None,
        mla_q_norm_gain, mla_k_norm_gain, conv_f, gdn_a_log, gdn_dt_bias, gdn_out_norm_gain, None, scatter_hook,
        late_weights)

    loss_local = (0.5 / D_MODEL) * jnp.sum(sq)
    pack = jnp.concatenate([_pack_small(small, loss_local), big["gdn_conv_w"].reshape(96, 128)], axis=0)
    tot = _all_reduce_small(pack)
    loss = tot[LOSS_ROW, 0]
    conv_grad = lax.dynamic_slice_in_dim(tot[CONV_ROW:].reshape(4, 3072), k_me * 768, 768, axis=1)

    shard_grads = _rs_chip_stage(pair_sums, received)

    grads = {n: _unpack_small(tot, n) for n in SMALL_ROWS}
    grads["gdn_conv_w"] = conv_grad[None]
    for n, g in zip(BIG, shard_grads):
        grads[n] = g[None]

    delta, new_m, new_v = {}, {}, {}
    sw = _pack_small({n: w[n] for n in SMALL_ROWS})
    sm = _pack_small({n: m[n] for n in SMALL_ROWS})
    sv = _pack_small({n: v[n] for n in SMALL_ROWS})
    sd, snm, snv = _adamw(sw, tot[:SMALL_PACK_ROWS], sm, sv, "adamw_small")
    for n in SMALL_ROWS:
        delta[n], new_m[n], new_v[n] = _unpack_small(sd, n), _unpack_small(snm, n), _unpack_small(snv, n)
    for n in BIG + ["gdn_conv_w"]:
        d, nm, nv = _adamw(w[n][0], grads[n][0], m[n][0], v[n][0], f"adamw_{n}")
        delta[n], new_m[n], new_v[n] = d[None], nm[None], nv[None]

    return (loss, grad_x[None], *[grads[n] for n in WEIGHTS], *[delta[n] for n in WEIGHTS],
            *[new_m[n] for n in WEIGHTS], *[new_v[n] for n in WEIGHTS])
```

```python
import functools
import math

import jax
import jax.numpy as jnp
from jax import lax
from jax.experimental import pallas as pl
from jax.experimental.pallas import tpu as pltpu

F32 = jnp.float32
BF16 = jnp.bfloat16
MESH = pl.DeviceIdType.MESH

D_MODEL = 2048
HEADS = 8
HEAD_DIM = 128
QK_DIM = 192
QK_PAD = 256
HALF_ROPE = 32
CHUNK = 64
NORM_EPS = 1e-6
W_IN_COLS = 5968
W_IN_PAD = 6144
GA_LANE = 64
GB_LANE = 72
ADAM_LR, ADAM_B1, ADAM_B2, ADAM_EPS, ADAM_WD, ADAM_STEP = 0.001, 0.9, 0.999, 1e-08, 0.01, 10
VMEM_LIMIT_V7X = 52 * 1024 * 1024
HI = lax.Precision.HIGHEST
NN = (((1,), (0,)), ((), ()))
NT = (((1,), (1,)), ((), ()))
TN = (((0,), (0,)), ((), ()))

TILES = {"row": 512, "attn": 1024, "mm": 1024}


def _call(body, *, name, grid, in_specs, out_specs, out_shape, args, scratch=(), sem=None):
    def kfn(*refs):
        body(*refs)
    if sem is None:
        sem = ("arbitrary",) * len(grid)
    return pl.pallas_call(
        kfn, name=name, grid=grid, in_specs=in_specs, out_specs=out_specs, out_shape=out_shape,
        scratch_shapes=list(scratch),
        compiler_params=pltpu.CompilerParams(dimension_semantics=sem, vmem_limit_bytes=VMEM_LIMIT_V7X),
    )(*args)


def _rows(tm, w, cb=0):
    return pl.BlockSpec((tm, w), lambda i: (i, cb))


def _full(shape):
    n = len(shape)
    return pl.BlockSpec(shape, lambda *_: (0,) * n)


def _sds(shape, dtype):
    return jax.ShapeDtypeStruct(shape, dtype)


def _acc8(x):
    tm, c = x.shape
    return jnp.sum(x.reshape(tm // 8, 8, c), axis=0)


def _sigmoid(x):
    return 1.0 / (1.0 + jnp.exp(-x))


def _silu(x):
    return x * _sigmoid(x)


def _dsilu(x):
    s = _sigmoid(x)
    return s * (1.0 + x * (1.0 - s))


def _dot(a, b, dims=NN):
    return lax.dot_general(a.astype(BF16), b.astype(BF16), dims, preferred_element_type=F32)


def _dot_hi(a, b, dims=NN):
    return lax.dot_general(a, b, dims, precision=HI, preferred_element_type=F32)


def _matmul(a, b, *, mode, out_dtype, name, tm=None, tn=None, tk=None):
    if mode == "tn":
        kdim, m = a.shape
    else:
        m, kdim = a.shape
    n = b.shape[0] if mode == "nt" else b.shape[1]
    tm = min(tm or TILES["mm"], m)
    tn = min(tn or TILES["mm"], n)
    tk = min(tk or kdim, kdim)
    nk = kdim // tk
    dims = {"nn": NN, "nt": NT, "tn": TN}[mode]
    if mode == "tn":
        a_spec = pl.BlockSpec((tk, tm), lambda i, j, k: (k, i))
    else:
        a_spec = pl.BlockSpec((tm, tk), lambda i, j, k: (i, k))
    if mode == "nt":
        b_spec = pl.BlockSpec((tn, tk), lambda i, j, k: (j, k))
    else:
        b_spec = pl.BlockSpec((tk, tn), lambda i, j, k: (k, j))

    def body(a_ref, b_ref, o_ref):
        r = _dot(a_ref[...], b_ref[...], dims)
        if nk == 1:
            o_ref[...] = r.astype(o_ref.dtype)
        else:
            k = pl.program_id(2)

            @pl.when(k == 0)
            def _():
                o_ref[...] = r

            @pl.when(k > 0)
            def _():
                o_ref[...] += r

    if nk > 1:
        assert out_dtype == F32
    return _call(body, name=name, grid=(m // tm, n // tn, nk), in_specs=[a_spec, b_spec],
                 out_specs=pl.BlockSpec((tm, tn), lambda i, j, k: (i, j)), out_shape=_sds((m, n), out_dtype),
                 args=(a, b))


def _norm1_fwd(x, gain):
    s = x.shape[0]
    tm = min(TILES["row"], s)

    def body(x_ref, g_ref, o_ref):
        xv = x_ref[...]
        r = lax.rsqrt(jnp.mean(xv * xv, axis=-1, keepdims=True) + NORM_EPS)
        o_ref[...] = (xv * r * g_ref[...]).astype(BF16)

    return _call(body, name="norm1_fwd", grid=(s // tm,), in_specs=[_rows(tm, D_MODEL), _full((1, D_MODEL))],
                 out_specs=_rows(tm, D_MODEL), out_shape=_sds((s, D_MODEL), BF16), args=(x, gain))


def _norm1_bwd(x, gain, dxn, dy):
    s = x.shape[0]
    tm = min(TILES["row"], s)

    def body(x_ref, g_ref, dxn_ref, dy_ref, gx_ref, dg_ref):
        xv = x_ref[...]
        r = lax.rsqrt(jnp.mean(xv * xv, axis=-1, keepdims=True) + NORM_EPS)
        nrm = xv * r
        d = dxn_ref[...].astype(F32)
        dn = d * g_ref[...]
        gx_ref[...] = dy_ref[...].astype(F32) + r * (dn - nrm * jnp.mean(dn * nrm, axis=-1, keepdims=True))

        @pl.when(pl.program_id(0) == 0)
        def _():
            dg_ref[...] = jnp.zeros_like(dg_ref)

        dg_ref[...] += _acc8(d * nrm)

    return _call(body, name="norm1_bwd", grid=(s // tm,),
                 in_specs=[_rows(tm, D_MODEL), _full((1, D_MODEL)), _rows(tm, D_MODEL), _rows(tm, D_MODEL)],
                 out_specs=[_rows(tm, D_MODEL), _full((8, D_MODEL))],
                 out_shape=[_sds((s, D_MODEL), F32), _sds((8, D_MODEL), F32)], args=(x, gain, dxn, dy))


def _rms(xv, width):
    return lax.rsqrt(jnp.sum(xv * xv, axis=-1, keepdims=True) * (1.0 / width) + NORM_EPS)


def _mla_a_norm(proj, gq, gkv):
    s = proj.shape[0]
    tm = min(TILES["row"], s)

    def body(cq_ref, ckv_ref, gq_ref, gkv_ref, oq_ref, okv_ref):
        a = cq_ref[...].astype(F32)
        oq_ref[...] = (a * _rms(a, 512) * gq_ref[...]).astype(BF16)
        b = ckv_ref[...].astype(F32)
        okv_ref[...] = (b * _rms(b, 256) * gkv_ref[...]).astype(BF16)

    return _call(body, name="mla_a_norm", grid=(s // tm,),
                 in_specs=[_rows(tm, 512, 0), _rows(tm, 256, 2), _full((1, 512)), _full((1, 256))],
                 out_specs=[_rows(tm, 512), _rows(tm, 256)],
                 out_shape=[_sds((s, 512), BF16), _sds((s, 256), BF16)], args=(proj, proj, gq, gkv))


def _rms_bwd(xv, gain, d, width):
    r = _rms(xv, width)
    nrm = xv * r
    dn = d * gain
    dx = r * (dn - nrm * (jnp.sum(dn * nrm, axis=-1, keepdims=True) * (1.0 / width)))
    return dx, d * nrm


def _mla_a_norm_bwd(proj, gq, gkv, dcqn, dckvn):
    s = proj.shape[0]
    tm = min(TILES["row"], s)

    def body(cq_ref, ckv_ref, gq_ref, gkv_ref, dq_ref, dkv_ref, oq_ref, okv_ref, aq_ref, akv_ref):
        dxq, gq_part = _rms_bwd(cq_ref[...].astype(F32), gq_ref[...], dq_ref[...].astype(F32), 512)
        dxk, gk_part = _rms_bwd(ckv_ref[...].astype(F32), gkv_ref[...], dkv_ref[...].astype(F32), 256)
        oq_ref[...] = dxq.astype(BF16)
        okv_ref[...] = dxk.astype(BF16)

        @pl.when(pl.program_id(0) == 0)
        def _():
            aq_ref[...] = jnp.zeros_like(aq_ref)
            akv_ref[...] = jnp.zeros_like(akv_ref)

        aq_ref[...] += _acc8(gq_part)
        akv_ref[...] += _acc8(gk_part)

    return _call(body, name="mla_a_norm_bwd", grid=(s // tm,),
                 in_specs=[_rows(tm, 512, 0), _rows(tm, 256, 2), _full((1, 512)), _full((1, 256)),
                           _rows(tm, 512), _rows(tm, 256)],
                 out_specs=[_rows(tm, 512), _rows(tm, 256), _full((8, 512)), _full((8, 256))],
                 out_shape=[_sds((s, 512), BF16), _sds((s, 256), BF16), _sds((8, 512), F32), _sds((8, 256), F32)],
                 args=(proj, proj, gq, gkv, dcqn, dckvn))


def _swap32(r):
    lane = lax.broadcasted_iota(jnp.int32, r.shape, 1)
    return jnp.where(lane < HALF_ROPE, pltpu.roll(r, 128 - HALF_ROPE, 1), pltpu.roll(r, HALF_ROPE, 1))


def _mla_post_fwd(q_pre, kv_pre, proj, cs, sn, gq, gk):
    s = q_pre.shape[0]
    tm = min(TILES["row"], s)

    def body(qp_ref, kvp_ref, misc_ref, cs_ref, sn_ref, gq_ref, gk_ref, q_ref, k_ref, v_ref):
        csv, snv = cs_ref[...], sn_ref[...]
        lane = lax.broadcasted_iota(jnp.int32, (tm, 128), 1)
        kr = jnp.where(lane < 64, misc_ref[...], 0.0)
        for h in range(HEADS):
            for src, g_ref, o_ref in ((None, gq_ref, q_ref), (kr, gk_ref, k_ref)):
                if src is None:
                    xv = qp_ref[:, QK_PAD * h:QK_PAD * (h + 1)].astype(F32)
                else:
                    xv = jnp.concatenate([kvp_ref[:, 256 * h:256 * h + 128].astype(F32), src], axis=-1)
                y = xv * _rms(xv, QK_DIM) * g_ref[...]
                if src is None:
                    y = y * Q_PRESCALE
                hi = y[:, 128:]
                hi = hi * csv + _swap32(hi) * snv
                o_ref[:, QK_PAD * h:QK_PAD * h + 128] = y[:, :128].astype(BF16)
                o_ref[:, QK_PAD * h + 128:QK_PAD * (h + 1)] = hi.astype(BF16)
            v_ref[:, 128 * h:128 * (h + 1)] = kvp_ref[:, 256 * h + 128:256 * (h + 1)].astype(BF16)

    return _call(body, name="mla_post_fwd", grid=(s // tm,),
                 in_specs=[_rows(tm, 2048), _rows(tm, 2048), _rows(tm, 128), _rows(tm, 128), _rows(tm, 128),
                           _full((1, QK_PAD)), _full((1, QK_PAD))],
                 out_specs=[_rows(tm, 2048), _rows(tm, 2048), _rows(tm, 1024)],
                 out_shape=[_sds((s, 2048), BF16), _sds((s, 2048), BF16), _sds((s, 1024), BF16)],
                 args=(q_pre, kv_pre, proj, cs, sn, gq, gk))


def _mla_post_bwd(q_pre, kv_pre, proj, cs, sn, gq, gk, dq, dk, dv):
    s = q_pre.shape[0]
    tm = min(TILES["row"], s)

    def body(qp_ref, kvp_ref, misc_ref, cs_ref, sn_ref, gq_ref, gk_ref, dq_ref, dk_ref, dv_ref,
             oq_ref, okv_ref, okr_ref, agq_ref, agk_ref):
        csv, snv = cs_ref[...], sn_ref[...]
        lane = lax.broadcasted_iota(jnp.int32, (tm, 128), 1)
        kr = jnp.where(lane < 64, misc_ref[...], 0.0)

        @pl.when(pl.program_id(0) == 0)
        def _():
            agq_ref[...] = jnp.zeros_like(agq_ref)
            agk_ref[...] = jnp.zeros_like(agk_ref)

        dkr = jnp.zeros((tm, 128), F32)
        for h in range(HEADS):
            for which in (0, 1):
                if which == 0:
                    xv = qp_ref[:, QK_PAD * h:QK_PAD * (h + 1)].astype(F32)
                    d_ref, g_ref, a_ref = dq_ref, gq_ref, agq_ref
                else:
                    xv = jnp.concatenate([kvp_ref[:, 256 * h:256 * h + 128].astype(F32), kr], axis=-1)
                    d_ref, g_ref, a_ref = dk_ref, gk_ref, agk_ref
                dhi = d_ref[:, QK_PAD * h + 128:QK_PAD * (h + 1)]
                dhi = dhi * csv - _swap32(dhi) * snv
                dyv = jnp.concatenate([d_ref[:, QK_PAD * h:QK_PAD * h + 128], dhi], axis=-1)
                if which == 0:
                    dyv = dyv * ATTN_SCALE
                dx, gpart = _rms_bwd(xv, g_ref[...], dyv, QK_DIM)
                a_ref[...] += _acc8(gpart)
                if which == 0:
                    oq_ref[:, QK_PAD * h:QK_PAD * (h + 1)] = dx.astype(BF16)
                else:
                    okv_ref[:, 256 * h:256 * h + 128] = dx[:, :128].astype(BF16)
                    dkr = dkr + dx[:, 128:]
            okv_ref[:, 256 * h + 128:256 * (h + 1)] = dv_ref[:, 128 * h:128 * (h + 1)].astype(BF16)
        okr_ref[...] = dkr

    return _call(body, name="mla_post_bwd", grid=(s // tm,),
                 in_specs=[_rows(tm, 2048), _rows(tm, 2048), _rows(tm, 128), _rows(tm, 128), _rows(tm, 128),
                           _full((1, QK_PAD)), _full((1, QK_PAD)), _rows(tm, 2048), _rows(tm, 2048), _rows(tm, 1024)],
                 out_specs=[_rows(tm, 2048), _rows(tm, 2048), _rows(tm, 128), _full((8, QK_PAD)), _full((8, QK_PAD))],
                 out_shape=[_sds((s, 2048), BF16), _sds((s, 2048), BF16), _sds((s, 128), F32),
                            _sds((8, QK_PAD), F32), _sds((8, QK_PAD), F32)],
                 args=(q_pre, kv_pre, proj, cs, sn, gq, gk, dq, dk, dv))


ATTN_SCALE = QK_DIM ** -0.5
NEG = -1e30


LOG2E = 1.4426950408889634
LN2 = 0.6931471805599453
Q_PRESCALE = ATTN_SCALE * LOG2E
ATTN_SUB_FWD = 256
ATTN_SUB_BWD = 256


def _causal_pairs(nq, kv_major):
    prs = [(i, j) for i in range(nq) for j in range(i + 1)]
    if kv_major:
        prs.sort(key=lambda ij: (ij[1], ij[0]))
    return (jnp.asarray([p[0] for p in prs], jnp.int32), jnp.asarray([p[1] for p in prs], jnp.int32))


def _pair_call(body, *, name, tables, in_specs, out_specs, out_shape, scratch, args):
    def kfn(*refs):
        body(*refs)
    spec = pltpu.PrefetchScalarGridSpec(num_scalar_prefetch=2, grid=(HEADS, tables[0].shape[0]), in_specs=in_specs,
                                        out_specs=out_specs, scratch_shapes=list(scratch))
    return pl.pallas_call(
        kfn, name=name, grid_spec=spec, out_shape=out_shape,
        compiler_params=pltpu.CompilerParams(dimension_semantics=("parallel", "arbitrary"),
                                             vmem_limit_bytes=VMEM_LIMIT_V7X))(*tables, *args)


def _diag_mask(sc, ts, qs):
    row = lax.broadcasted_iota(jnp.int32, sc.shape, 0) + qs * ts
    col = lax.broadcasted_iota(jnp.int32, sc.shape, 1)
    return jnp.where(col <= row, sc, NEG)


def _attn_fwd(q, k, v):
    s = q.shape[0]
    t = min(TILES["attn"], s)
    ts = min(ATTN_SUB_FWD, t)
    nq = s // t

    def slabs(q_ref, k_ref, v_ref, m_s, l_s, acc_s, diag):
        def scores(qs):
            kw = (qs + 1) * ts if diag else t
            sc = lax.dot_general(q_ref[qs * ts:(qs + 1) * ts, :], k_ref[0:kw, :], NT, preferred_element_type=F32)
            return _diag_mask(sc, ts, qs) if diag else sc

        nsub = t // ts
        sc_next = scores(0)
        for qs in range(nsub):
            rq = slice(qs * ts, (qs + 1) * ts)
            kw = (qs + 1) * ts if diag else t
            sc = sc_next
            if qs + 1 < nsub:
                sc_next = scores(qs + 1)
            m_prev = m_s[rq, :]
            m_new = jnp.maximum(m_prev, jnp.max(sc, axis=-1, keepdims=True))
            p = jnp.exp2(sc - m_new)
            alpha = jnp.exp2(m_prev - m_new)
            l_s[rq, :] = alpha * l_s[rq, :] + jnp.sum(p, axis=-1, keepdims=True)
            acc_s[rq, :] = acc_s[rq, :] * alpha + lax.dot_general(p.astype(BF16), v_ref[0:kw, :], NN,
                                                                  preferred_element_type=F32)
            m_s[rq, :] = m_new

    def body(it_ref, jt_ref, q_ref, k_ref, v_ref, o_ref, lse_ref, m_s, l_s, acc_s):
        p = pl.program_id(1)
        i, j = it_ref[p], jt_ref[p]

        @pl.when(j == 0)
        def _():
            m_s[...] = jnp.full_like(m_s, NEG)
            l_s[...] = jnp.zeros_like(l_s)
            acc_s[...] = jnp.zeros_like(acc_s)

        @pl.when(j < i)
        def _():
            slabs(q_ref, k_ref, v_ref, m_s, l_s, acc_s, False)

        @pl.when(j == i)
        def _():
            slabs(q_ref, k_ref, v_ref, m_s, l_s, acc_s, True)
            o_ref[...] = acc_s[...] / l_s[...]
            lse_ref[...] = m_s[...] + jnp.log2(l_s[...])

    qb = lambda h, p, it, jt: (it[p], h)
    kb = lambda h, p, it, jt: (jt[p], h)
    return _pair_call(
        body, name="attn_fwd", tables=_causal_pairs(nq, kv_major=False),
        in_specs=[pl.BlockSpec((t, QK_PAD), qb), pl.BlockSpec((t, QK_PAD), kb), pl.BlockSpec((t, HEAD_DIM), kb)],
        out_specs=[pl.BlockSpec((t, HEAD_DIM), qb),
                   pl.BlockSpec((None, t, 1), lambda h, p, it, jt: (h, it[p], 0))],
        out_shape=[_sds((s, HEADS * HEAD_DIM), F32), _sds((HEADS, s, 1), F32)],
        scratch=[pltpu.VMEM((t, 1), F32), pltpu.VMEM((t, 1), F32), pltpu.VMEM((t, HEAD_DIM), F32)],
        args=(q, k, v))


def _attn_bwd(q, k, v, o, lse, do):
    s = q.shape[0]
    t = min(TILES["attn"], s)
    ts = min(ATTN_SUB_BWD, t)
    nq = s // t

    def slabs(q_ref, k_ref, v_ref, o_ref, lse_ref, do_ref, dq_ref, dk_ref, dv_ref, i, diag):
        def products(qs):
            rq = slice(qs * ts, (qs + 1) * ts)
            kw = (qs + 1) * ts if diag else t
            qv, dof = q_ref[rq, :], do_ref[rq, :].astype(F32)
            sc = lax.dot_general(qv, k_ref[0:kw, :], NT, preferred_element_type=F32)
            dob = dof.astype(BF16)
            dp = lax.dot_general(dob, v_ref[0:kw, :], NT, preferred_element_type=F32)
            return qv, dof, dob, (_diag_mask(sc, ts, qs) if diag else sc), dp

        nsub = t // ts
        ahead = products(0)
        for qs in range(nsub):
            rq = slice(qs * ts, (qs + 1) * ts)
            kw = (qs + 1) * ts if diag else t
            qv, dof, dob, sc, dp = ahead
            if qs + 1 < nsub:
                ahead = products(qs + 1)
            kv = k_ref[0:kw, :]
            p = jnp.exp2(sc - lse_ref[rq, :])
            delta = jnp.sum(dof * o_ref[rq, :], axis=-1, keepdims=True)
            dv_ref[0:kw, :] += lax.dot_general(p.astype(BF16), dob, TN, preferred_element_type=F32)
            ds = (p * (dp - delta)).astype(BF16)
            dk_ref[0:kw, :] += lax.dot_general(ds, qv, TN, preferred_element_type=F32)
            rows = pl.ds(pl.multiple_of(i * t + qs * ts, ts), ts)
            dq_ref[rows, :] += lax.dot_general(ds, kv, NN, preferred_element_type=F32)

    def body(it_ref, jt_ref, q_ref, k_ref, v_ref, o_ref, lse_ref, do_ref, dq_ref, dk_ref, dv_ref):
        p = pl.program_id(1)
        i, j = it_ref[p], jt_ref[p]
        refs = (q_ref, k_ref, v_ref, o_ref, lse_ref, do_ref, dq_ref, dk_ref, dv_ref)

        @pl.when(p == 0)
        def _():
            dq_ref[...] = jnp.zeros_like(dq_ref)

        @pl.when(i == j)
        def _():
            dk_ref[...] = jnp.zeros_like(dk_ref)
            dv_ref[...] = jnp.zeros_like(dv_ref)

        @pl.when(i > j)
        def _():
            slabs(*refs, i, False)

        @pl.when(i == j)
        def _():
            slabs(*refs, i, True)

        @pl.when(i == nq - 1)
        def _():
            dk_ref[...] = dk_ref[...] * LN2

    qb = lambda h, p, it, jt: (it[p], h)
    kb = lambda h, p, it, jt: (jt[p], h)
    return _pair_call(
        body, name="attn_bwd", tables=_causal_pairs(nq, kv_major=True),
        in_specs=[pl.BlockSpec((t, QK_PAD), qb), pl.BlockSpec((t, QK_PAD), kb), pl.BlockSpec((t, HEAD_DIM), kb),
                  pl.BlockSpec((t, HEAD_DIM), qb), pl.BlockSpec((None, t, 1), lambda h, p, it, jt: (h, it[p], 0)),
                  pl.BlockSpec((t, HEAD_DIM), qb)],
        out_specs=[pl.BlockSpec((s, QK_PAD), lambda h, p, it, jt: (0, h)), pl.BlockSpec((t, QK_PAD), kb),
                   pl.BlockSpec((t, HEAD_DIM), kb)],
        out_shape=[_sds((s, HEADS * QK_PAD), F32), _sds((s, HEADS * QK_PAD), F32), _sds((s, HEADS * HEAD_DIM), F32)],
        scratch=(), args=(q, k, v, o, lse, do))


GDN_Q_SCALE = HEAD_DIM ** -0.5


def _shift_down(xv, prev8, sft):
    rolled = pltpu.roll(xv, sft, 0)
    top = pltpu.roll(jnp.concatenate([prev8, xv[:8]], axis=0), sft, 0)[8:]
    return jnp.concatenate([top, rolled[8:]], axis=0)


def _shift_up(xv, next8, sft):
    tm = xv.shape[0]
    rolled = pltpu.roll(xv, tm - sft, 0)
    bot = pltpu.roll(jnp.concatenate([xv[tm - 8:], next8], axis=0), 16 - sft, 0)[:8]
    return jnp.concatenate([rolled[:tm - 8], bot], axis=0)


def _conv_z(xv, prev8, w_ref):
    z = xv * w_ref[3:4, :]
    for sft in (1, 2, 3):
        z = z + _shift_down(xv, prev8, sft) * w_ref[3 - sft:4 - sft, :]
    return z


def _conv_specs(s, tm):
    nb16 = tm // 16
    cur = pl.BlockSpec((tm, 1024), lambda j, i: (i, 2 + j))
    prev = pl.BlockSpec((16, 1024), lambda j, i: (jnp.maximum(i * nb16 - 1, 0), 2 + j))
    return cur, prev


def _prev8(xp_ref, i):
    return jnp.where(i > 0, xp_ref[...].astype(F32)[8:], 0.0)


def _gdn_conv_fwd(proj, conv_w):
    s = proj.shape[0]
    tm = min(TILES["row"], s)
    cur, prev = _conv_specs(s, tm)

    def body(x_ref, xp_ref, w_ref, o_ref):
        j, i = pl.program_id(0), pl.program_id(1)
        a = _silu(_conv_z(x_ref[...].astype(F32), _prev8(xp_ref, i), w_ref))
        qk_scale = jnp.where(j == 0, GDN_Q_SCALE, 1.0)
        for h in range(HEADS):
            seg = a[:, 128 * h:128 * (h + 1)]
            r = lax.rsqrt(jnp.sum(seg * seg, axis=-1, keepdims=True) + NORM_EPS)
            o_ref[:, 128 * h:128 * (h + 1)] = jnp.where(j < 2, seg * r * qk_scale, seg)

    return _call(body, name="gdn_conv_fwd", grid=(3, s // tm),
                 in_specs=[cur, prev, pl.BlockSpec((4, 1024), lambda j, i: (0, j))],
                 out_specs=pl.BlockSpec((tm, 1024), lambda j, i: (i, j)), out_shape=_sds((s, 3072), F32),
                 args=(proj, proj, conv_w))


def _gdn_conv_bwd_a(proj, conv_w, dqkv):
    s = proj.shape[0]
    tm = min(TILES["row"], s)
    cur, prev = _conv_specs(s, tm)

    def body(x_ref, xp_ref, w_ref, d_ref, o_ref):
        j, i = pl.program_id(0), pl.program_id(1)
        z = _conv_z(x_ref[...].astype(F32), _prev8(xp_ref, i), w_ref)
        a = _silu(z)
        dsl = _dsilu(z)
        qk_scale = jnp.where(j == 0, GDN_Q_SCALE, 1.0)
        for h in range(HEADS):
            sl = slice(128 * h, 128 * (h + 1))
            seg = a[:, sl]
            dyv = d_ref[:, sl]
            r = lax.rsqrt(jnp.sum(seg * seg, axis=-1, keepdims=True) + NORM_EPS)
            yh = seg * r
            da_n = qk_scale * r * (dyv - yh * jnp.sum(yh * dyv, axis=-1, keepdims=True))
            o_ref[:, sl] = jnp.where(j < 2, da_n, dyv) * dsl[:, sl]

    return _call(body, name="gdn_conv_bwd_a", grid=(3, s // tm),
                 in_specs=[cur, prev, pl.BlockSpec((4, 1024), lambda j, i: (0, j)),
                           pl.BlockSpec((tm, 1024), lambda j, i: (i, j))],
                 out_specs=pl.BlockSpec((tm, 1024), lambda j, i: (i, j)), out_shape=_sds((s, 3072), F32),
                 args=(proj, proj, conv_w, dqkv))


def _gdn_conv_bwd_b(proj, conv_w, dz):
    s = proj.shape[0]
    tm = min(TILES["row"], s)
    nb8 = tm // 8
    last8 = s // 8 - 1
    cur, prev = _conv_specs(s, tm)

    def body(x_ref, w_ref, dz_ref, dzn_ref, dx_ref, dw_ref):
        i = pl.program_id(1)
        next8 = jnp.where(i < pl.num_programs(1) - 1, dzn_ref[...], 0.0)
        xv, dzv = x_ref[...].astype(F32), dz_ref[...]

        @pl.when(i == 0)
        def _():
            dw_ref[...] = jnp.zeros_like(dw_ref)

        dx = dzv * w_ref[3:4, :]
        dw_ref[3:4, :] += jnp.sum(dzv * xv, axis=0, keepdims=True)
        for sft in (1, 2, 3):
            up = _shift_up(dzv, next8, sft)
            dx = dx + up * w_ref[3 - sft:4 - sft, :]
            dw_ref[3 - sft:4 - sft, :] += jnp.sum(up * xv, axis=0, keepdims=True)
        dx_ref[...] = dx.astype(BF16)

    return _call(body, name="gdn_conv_bwd_b", grid=(3, s // tm),
                 in_specs=[cur, pl.BlockSpec((4, 1024), lambda j, i: (0, j)),
                           pl.BlockSpec((tm, 1024), lambda j, i: (i, j)),
                           pl.BlockSpec((8, 1024), lambda j, i: (jnp.minimum((i + 1) * nb8, last8), j))],
                 out_specs=[pl.BlockSpec((tm, 1024), lambda j, i: (i, j)), pl.BlockSpec((4, 1024), lambda j, i: (0, j))],
                 out_shape=[_sds((s, 3072), BF16), _sds((4, 3072), F32)], args=(proj, conv_w, dz, dz))


def _softplus(xv):
    return jnp.maximum(xv, 0.0) + jnp.log(1.0 + jnp.exp(-jnp.abs(xv)))


def _gdn_gates_fwd(proj, alog128, dtb128):
    s = proj.shape[0]
    tm = min(TILES["row"], s)

    def body(m_ref, a_ref, b_ref, o_ref):
        mv = m_ref[...]
        lane = lax.broadcasted_iota(jnp.int32, mv.shape, 1)
        g = -jnp.exp(a_ref[...]) * _softplus(mv + b_ref[...])
        is_g = (lane >= GA_LANE) & (lane < GA_LANE + HEADS)
        is_b = (lane >= GB_LANE) & (lane < GB_LANE + HEADS)
        o_ref[...] = jnp.where(is_g, g, jnp.where(is_b, _sigmoid(mv), 0.0))

    return _call(body, name="gdn_gates_fwd", grid=(s // tm,),
                 in_specs=[_rows(tm, 128), _full((1, 128)), _full((1, 128))],
                 out_specs=_rows(tm, 128), out_shape=_sds((s, 128), F32), args=(proj, alog128, dtb128))


def _gdn_gates_bwd(proj, alog128, dtb128, gbeta, dgbeta, dkr):
    s = proj.shape[0]
    tm = min(TILES["row"], s)

    def body(m_ref, a_ref, b_ref, gb_ref, d_ref, kr_ref, o_ref, da_ref, db_ref):
        mv, dv = m_ref[...], d_ref[...]
        lane = lax.broadcasted_iota(jnp.int32, mv.shape, 1)
        is_g = (lane >= GA_LANE) & (lane < GA_LANE + HEADS)
        is_b = (lane >= GB_LANE) & (lane < GB_LANE + HEADS)
        dga = jnp.where(is_g, dv * (-jnp.exp(a_ref[...])) * _sigmoid(mv + b_ref[...]), 0.0)
        beta = gb_ref[...]
        dgb = jnp.where(is_b, dv * beta * (1.0 - beta), 0.0)
        o_ref[...] = jnp.where(lane < 64, kr_ref[...], dga + dgb).astype(BF16)

        @pl.when(pl.program_id(0) == 0)
        def _():
            da_ref[...] = jnp.zeros_like(da_ref)
            db_ref[...] = jnp.zeros_like(db_ref)

        da_ref[...] += _acc8(jnp.where(is_g, dv * gb_ref[...], 0.0))
        db_ref[...] += _acc8(dga)

    return _call(body, name="gdn_gates_bwd", grid=(s // tm,),
                 in_specs=[_rows(tm, 128), _full((1, 128)), _full((1, 128)), _rows(tm, 128), _rows(tm, 128),
                           _rows(tm, 128)],
                 out_specs=[_rows(tm, 128), _full((8, 128)), _full((8, 128))],
                 out_shape=[_sds((s, 128), BF16), _sds((8, 128), F32), _sds((8, 128), F32)],
                 args=(proj, alog128, dtb128, gbeta, dgbeta, dkr))


def _col(mat, lane_idx, lane):
    return jnp.sum(jnp.where(lane == lane_idx, mat, 0.0), axis=-1, keepdims=True)


def _chunk_local(qh, kh, vh, gcol, bcol, ii, jj):
    lower, strict, eye = ii >= jj, ii > jj, ii == jj
    grow = jnp.sum(jnp.where(eye, gcol, 0.0), axis=0, keepdims=True)
    decay = jnp.where(lower, jnp.exp(jnp.where(lower, gcol - grow, 0.0)), 0.0)
    kb = kh * bcol
    vb = vh * bcol
    mm = _dot(kb, kh, NT)
    lmat = jnp.where(strict, mm * decay, 0.0)
    pw = -lmat
    tinv = jnp.where(eye, 1.0, 0.0) + pw
    for _ in range(5):
        pw = _dot_hi(pw, pw)
        tinv = tinv + _dot_hi(tinv, pw)
    egc = jnp.exp(gcol)
    kbg = kb * egc
    rhs = jnp.concatenate([vb, kbg], axis=-1)
    sol = _dot_hi(tinv, rhs)
    qk = _dot(qh, kh, NT)
    glast = jnp.sum(jnp.where(ii[:, :1] == CHUNK - 1, gcol, 0.0), axis=0, keepdims=True)
    ekd = jnp.exp(glast - gcol)
    return dict(decay=decay, kb=kb, vb=vb, mm=mm, lmat=lmat, tinv=tinv, egc=egc, kbg=kbg, rhs=rhs,
                u=sol[:, :HEAD_DIM], w=sol[:, HEAD_DIM:], qk=qk, amat=qk * decay, qd=qh * egc, ekd=ekd,
                kd=kh * ekd, gl=jnp.exp(glast), strict=strict, lower=lower, eye=eye)


def _tri(ii, jj):
    return jnp.where(ii >= jj, 1.0, 0.0)


def _gdn_fwd(qkv, gbeta):
    s = qkv.shape[0]
    n = s // CHUNK

    def body(qkv_ref, gb_ref, o_ref, st_ref, state):
        @pl.when(pl.program_id(0) == 0)
        def _():
            state[...] = jnp.zeros_like(state)

        ii = lax.broadcasted_iota(jnp.int32, (CHUNK, CHUNK), 0)
        jj = lax.broadcasted_iota(jnp.int32, (CHUNK, CHUNK), 1)
        lane = lax.broadcasted_iota(jnp.int32, (CHUNK, 128), 1)
        gbv = gb_ref[...]
        gc = _dot_hi(_tri(ii, jj), gbv)
        for h in range(HEADS):
            sl = slice(128 * h, 128 * (h + 1))
            qh = qkv_ref[:, 128 * h:128 * (h + 1)]
            kh = qkv_ref[:, 1024 + 128 * h:1024 + 128 * (h + 1)]
            vh = qkv_ref[:, 2048 + 128 * h:2048 + 128 * (h + 1)]
            c = _chunk_local(qh, kh, vh, _col(gc, GA_LANE + h, lane), _col(gbv, GB_LANE + h, lane), ii, jj)
            st = state[sl, :]
            st_ref[sl, :] = st
            vn = c["u"] - _dot(c["w"], st)
            o_ref[:, sl] = _dot(c["qd"], st) + _dot(c["amat"], vn)
            state[sl, :] = st * c["gl"] + _dot(c["kd"], vn, TN)

    return _call(body, name="gdn_fwd", grid=(n,),
                 in_specs=[_rows(CHUNK, 3072), _rows(CHUNK, 128)],
                 out_specs=[_rows(CHUNK, 1024), _rows(HEADS * 128, 128)],
                 out_shape=[_sds((s, 1024), F32), _sds((n * HEADS * 128, 128), F32)],
                 scratch=[pltpu.VMEM((HEADS * 128, 128), F32)], args=(qkv, gbeta))


def _gdn_bwd(qkv, gbeta, states, do):
    s = qkv.shape[0]
    n = s // CHUNK

    def body(qkv_ref, gb_ref, st_ref, do_ref, dqkv_ref, dgb_ref, dstate):
        @pl.when(pl.program_id(0) == 0)
        def _():
            dstate[...] = jnp.zeros_like(dstate)

        ii = lax.broadcasted_iota(jnp.int32, (CHUNK, CHUNK), 0)
        jj = lax.broadcasted_iota(jnp.int32, (CHUNK, CHUNK), 1)
        lane = lax.broadcasted_iota(jnp.int32, (CHUNK, 128), 1)
        row1 = ii[:, :1]
        gbv = gb_ref[...]
        gc = _dot_hi(_tri(ii, jj), gbv)
        dgc_all = jnp.zeros((CHUNK, 128), F32)
        db_all = jnp.zeros((CHUNK, 128), F32)
        for h in range(HEADS):
            sl = slice(128 * h, 128 * (h + 1))
            qh = qkv_ref[:, 128 * h:128 * (h + 1)]
            kh = qkv_ref[:, 1024 + 128 * h:1024 + 128 * (h + 1)]
            vh = qkv_ref[:, 2048 + 128 * h:2048 + 128 * (h + 1)]
            bcol = _col(gbv, GB_LANE + h, lane)
            c = _chunk_local(qh, kh, vh, _col(gc, GA_LANE + h, lane), bcol, ii, jj)
            st = st_ref[sl, :]
            dst = dstate[sl, :]
            dov = do_ref[:, sl]
            vn = c["u"] - _dot(c["w"], st)
            dvn = _dot(c["amat"], dov, TN) + _dot(c["kd"], dst)
            damat = jnp.where(c["lower"], _dot(dov, vn, NT), 0.0)
            dqd = _dot(dov, st, NT)
            dkd = _dot(vn, dst, NT)
            dw = -_dot(dvn, st, NT)
            dgl = jnp.sum(jnp.sum(st * dst, axis=-1, keepdims=True), axis=0, keepdims=True)
            dstate[sl, :] = _dot(c["qd"], dov, TN) + c["gl"] * dst - _dot(c["w"], dvn, TN)
            dsol = jnp.concatenate([dvn, dw], axis=-1)
            drhs = _dot_hi(c["tinv"], dsol, TN)
            dtinv = _dot_hi(dsol, c["rhs"], NT)
            dl = -_dot_hi(_dot_hi(c["tinv"], dtinv, TN), c["tinv"], NT)
            dl = jnp.where(c["strict"], dl, 0.0)
            dmm = dl * c["decay"]
            dqk = damat * c["decay"]
            wmat = dl * c["lmat"] + damat * c["amat"]
            dgc = jnp.sum(wmat, axis=-1, keepdims=True)
            wcol = jnp.sum(wmat, axis=0, keepdims=True)
            dgc = dgc - jnp.sum(jnp.where(c["eye"], wcol, 0.0), axis=-1, keepdims=True)
            dkb = _dot(dmm, kh) + drhs[:, HEAD_DIM:] * c["egc"]
            dk = _dot(dmm, c["kb"], TN) + _dot(dqk, qh, TN) + dkd * c["ekd"]
            dq = _dot(dqk, kh) + dqd * c["egc"]
            dgc = dgc + jnp.sum(drhs[:, HEAD_DIM:] * c["kbg"], axis=-1, keepdims=True)
            dgc = dgc + jnp.sum(dqd * c["qd"], axis=-1, keepdims=True)
            tmp = jnp.sum(dkd * c["kd"], axis=-1, keepdims=True)
            dgc = dgc - tmp
            dglast = jnp.sum(tmp, axis=0, keepdims=True) + dgl * c["gl"]
            dgc = dgc + jnp.where(row1 == CHUNK - 1, dglast, 0.0)
            dk = dk + dkb * bcol
            db = jnp.sum(dkb * kh, axis=-1, keepdims=True) + jnp.sum(drhs[:, :HEAD_DIM] * vh, axis=-1, keepdims=True)
            dqkv_ref[:, 128 * h:128 * (h + 1)] = dq
            dqkv_ref[:, 1024 + 128 * h:1024 + 128 * (h + 1)] = dk
            dqkv_ref[:, 2048 + 128 * h:2048 + 128 * (h + 1)] = drhs[:, :HEAD_DIM] * bcol
            dgc_all = dgc_all + jnp.where(lane == GA_LANE + h, dgc, 0.0)
            db_all = db_all + jnp.where(lane == GB_LANE + h, db, 0.0)
        dgb_ref[...] = _dot_hi(_tri(jj, ii), dgc_all) + db_all

    rev = lambda w: pl.BlockSpec((CHUNK, w), lambda i: (n - 1 - i, 0))
    return _call(body, name="gdn_bwd", grid=(n,),
                 in_specs=[rev(3072), rev(128), pl.BlockSpec((HEADS * 128, 128), lambda i: (n - 1 - i, 0)), rev(1024)],
                 out_specs=[rev(3072), rev(128)],
                 out_shape=[_sds((s, 3072), F32), _sds((s, 128), F32)],
                 scratch=[pltpu.VMEM((HEADS * 128, 128), F32)], args=(qkv, gbeta, states, do))


NN_B = (((2,), (1,)), ((0,), (0,)))
NT_B = (((2,), (2,)), ((0,), (0,)))
TN_B = (((1,), (1,)), ((0,), (0,)))
GDN_PAR_CHUNKS = 2
GDN_SEQ_CHUNKS = 4


def _gather_heads(qkv_ref, gc, gbv, qs, ks, vs, gs, bs, nchunks):
    lane = lax.broadcasted_iota(jnp.int32, (CHUNK, 128), 1)
    for c in range(nchunks):
        rows = slice(CHUNK * c, CHUNK * (c + 1))
        for h in range(HEADS):
            b = HEADS * c + h
            qs[b] = qkv_ref[rows, 128 * h:128 * (h + 1)]
            ks[b] = qkv_ref[rows, 1024 + 128 * h:1024 + 128 * (h + 1)]
            vs[b] = qkv_ref[rows, 2048 + 128 * h:2048 + 128 * (h + 1)]
            gs[b] = jnp.broadcast_to(_col(gc[rows], GA_LANE + h, lane), (CHUNK, 128))
            bs[b] = jnp.broadcast_to(_col(gbv[rows], GB_LANE + h, lane), (CHUNK, 128))


def _block_tri(rows, transpose=False):
    ri = lax.broadcasted_iota(jnp.int32, (rows, rows), 0)
    ci = lax.broadcasted_iota(jnp.int32, (rows, rows), 1)
    same = (ri >> 6) == (ci >> 6)
    return jnp.where(same & ((ci >= ri) if transpose else (ri >= ci)), 1.0, 0.0)


def _local_b(q, k, v, g128, b128):
    ii = lax.broadcasted_iota(jnp.int32, (1, CHUNK, CHUNK), 1)
    jj = lax.broadcasted_iota(jnp.int32, (1, CHUNK, CHUNK), 2)
    lower, strict, eye = ii >= jj, ii > jj, ii == jj
    g64 = g128[:, :, :CHUNK]
    grow = jnp.sum(jnp.where(eye, g64, 0.0), axis=1, keepdims=True)
    decay = jnp.where(lower, jnp.exp(jnp.where(lower, g64 - grow, 0.0)), 0.0)
    kb = k * b128
    vb = v * b128
    mm = lax.dot_general(kb.astype(BF16), k.astype(BF16), NT_B, preferred_element_type=F32)
    lmat = jnp.where(strict, mm * decay, 0.0)
    egc = jnp.exp(g128)
    kbg = kb * egc
    qk = lax.dot_general(q.astype(BF16), k.astype(BF16), NT_B, preferred_element_type=F32)
    row = lax.broadcasted_iota(jnp.int32, (1, CHUNK, 128), 1)
    glast = jnp.sum(jnp.where(row == CHUNK - 1, g128, 0.0), axis=1, keepdims=True)
    ekd = jnp.exp(glast - g128)
    return dict(decay=decay, kb=kb, vb=vb, lmat=lmat, egc=egc, kbg=kbg, amat=qk * decay, qd=q * egc, ekd=ekd,
                kd=k * ekd, gl=jnp.exp(glast), lower=lower, strict=strict, eye=eye)


def _bdot(a, b, dims):
    return lax.dot_general(a.astype(BF16), b.astype(BF16), dims, preferred_element_type=F32)


def _split(a):
    hi = a.astype(BF16)
    return hi, (a - hi.astype(F32)).astype(BF16)


def _bdot_hi(a, b, dims):
    ah, al = _split(a)
    bh, bl = _split(b)
    d = lambda x, y: lax.dot_general(x, y, dims, preferred_element_type=F32)
    return d(ah, bh) + d(ah, bl) + d(al, bh)


def _gdn_pre(qkv, gbeta):
    s = qkv.shape[0]
    n = s // CHUNK
    cb = min(GDN_PAR_CHUNKS, n)
    nb = cb * HEADS
    rows = cb * CHUNK

    def body(qkv_ref, gb_ref, u_ref, w_ref, qd_ref, kd_ref, a_ref, t_ref, gl_ref, qs, ks, vs, gs, bs):
        gbv = gb_ref[...]
        gc = _dot_hi(_block_tri(rows), gbv)
        _gather_heads(qkv_ref, gc, gbv, qs, ks, vs, gs, bs, cb)
        c = _local_b(qs[...], ks[...], vs[...], gs[...], bs[...])
        pw = -c["lmat"]
        tinv = jnp.where(c["eye"], 1.0, 0.0) + pw
        for _ in range(5):
            pw = _bdot_hi(pw, pw, NN_B)
            tinv = tinv + _bdot_hi(tinv, pw, NN_B)
        u_ref[...] = _bdot_hi(tinv, c["vb"], NN_B)
        w_ref[...] = _bdot_hi(tinv, c["kbg"], NN_B).astype(BF16)
        qd_ref[...] = c["qd"].astype(BF16)
        kd_ref[...] = c["kd"].astype(BF16)
        a_ref[...] = c["amat"].astype(BF16)
        t_ref[...] = tinv
        gl_ref[...] = c["gl"]

    b3 = lambda d: pl.BlockSpec((nb, CHUNK, d), lambda i: (i, 0, 0))
    nt = n * HEADS
    return _call(body, name="gdn_pre", grid=(n // cb,),
                 in_specs=[_rows(rows, 3072), _rows(rows, 128)],
                 out_specs=[b3(128), b3(128), b3(128), b3(128), b3(CHUNK), b3(CHUNK),
                            pl.BlockSpec((nb, 1, 128), lambda i: (i, 0, 0))],
                 out_shape=[_sds((nt, CHUNK, 128), F32), _sds((nt, CHUNK, 128), BF16), _sds((nt, CHUNK, 128), BF16),
                            _sds((nt, CHUNK, 128), BF16), _sds((nt, CHUNK, CHUNK), BF16), _sds((nt, CHUNK, CHUNK), F32),
                            _sds((nt, 1, 128), F32)],
                 scratch=[pltpu.VMEM((nb, CHUNK, 128), F32)] * 5, sem=("parallel",), args=(qkv, gbeta))


def _gdn_scan_fwd(u, w, qd, kd, amat, gl):
    nt = u.shape[0]
    n = nt // HEADS
    cs = min(GDN_SEQ_CHUNKS, n)

    def body(u_ref, w_ref, qd_ref, kd_ref, a_ref, gl_ref, o_ref, st_ref, state):
        @pl.when(pl.program_id(0) == 0)
        def _():
            state[...] = jnp.zeros_like(state)

        for c in range(cs):
            sl = slice(HEADS * c, HEADS * (c + 1))
            st = state[...]
            stb = st.astype(BF16)
            st_ref[sl] = stb
            vn = u_ref[sl] - lax.dot_general(w_ref[sl], stb, NN_B, preferred_element_type=F32)
            vnb = vn.astype(BF16)
            o = (lax.dot_general(qd_ref[sl], stb, NN_B, preferred_element_type=F32)
                 + lax.dot_general(a_ref[sl], vnb, NN_B, preferred_element_type=F32))
            state[...] = st * gl_ref[sl] + lax.dot_general(kd_ref[sl], vnb, TN_B, preferred_element_type=F32)
            for h in range(HEADS):
                o_ref[CHUNK * c:CHUNK * (c + 1), 128 * h:128 * (h + 1)] = o[h]

    b3 = lambda d: pl.BlockSpec((cs * HEADS, CHUNK, d), lambda i: (i, 0, 0))
    return _call(body, name="gdn_scan_fwd", grid=(n // cs,),
                 in_specs=[b3(128), b3(128), b3(128), b3(128), b3(CHUNK), pl.BlockSpec((cs * HEADS, 1, 128), lambda i: (i, 0, 0))],
                 out_specs=[_rows(cs * CHUNK, 1024), pl.BlockSpec((cs * HEADS, 128, 128), lambda i: (i, 0, 0))],
                 out_shape=[_sds((n * CHUNK, 1024), F32), _sds((nt, 128, 128), BF16)],
                 scratch=[pltpu.VMEM((HEADS, 128, 128), F32)], args=(u, w, qd, kd, amat, gl))


def _gdn_scan_bwd(w, qd, kd, amat, gl, do):
    nt = w.shape[0]
    n = nt // HEADS
    cs = min(GDN_SEQ_CHUNKS, n)
    ng = n // cs

    def body(w_ref, qd_ref, kd_ref, a_ref, gl_ref, do_ref, ds_ref, dstate, dos):
        @pl.when(pl.program_id(0) == 0)
        def _():
            dstate[...] = jnp.zeros_like(dstate)

        for c in reversed(range(cs)):
            sl = slice(HEADS * c, HEADS * (c + 1))
            for h in range(HEADS):
                dos[h] = do_ref[CHUNK * c:CHUNK * (c + 1), 128 * h:128 * (h + 1)].astype(BF16)
            dob = dos[...]
            dst = dstate[...]
            dstb = dst.astype(BF16)
            ds_ref[sl] = dstb
            dvn = (lax.dot_general(a_ref[sl], dob, TN_B, preferred_element_type=F32)
                   + lax.dot_general(kd_ref[sl], dstb, NN_B, preferred_element_type=F32))
            dstate[...] = (lax.dot_general(qd_ref[sl], dob, TN_B, preferred_element_type=F32) + gl_ref[sl] * dst
                           - lax.dot_general(w_ref[sl], dvn.astype(BF16), TN_B, preferred_element_type=F32))

    b3 = lambda d: pl.BlockSpec((cs * HEADS, CHUNK, d), lambda i: (ng - 1 - i, 0, 0))
    return _call(body, name="gdn_scan_bwd", grid=(ng,),
                 in_specs=[b3(128), b3(128), b3(128), b3(CHUNK), pl.BlockSpec((cs * HEADS, 1, 128), lambda i: (ng - 1 - i, 0, 0)),
                           pl.BlockSpec((cs * CHUNK, 1024), lambda i: (ng - 1 - i, 0))],
                 out_specs=pl.BlockSpec((cs * HEADS, 128, 128), lambda i: (ng - 1 - i, 0, 0)),
                 out_shape=_sds((nt, 128, 128), BF16),
                 scratch=[pltpu.VMEM((HEADS, 128, 128), F32), pltpu.VMEM((HEADS, CHUNK, 128), BF16)],
                 args=(w, qd, kd, amat, gl, do))


def _gdn_post_bwd(qkv, gbeta, u, w, tinv, states, dstates, do):
    s = qkv.shape[0]
    n = s // CHUNK
    cb = min(GDN_PAR_CHUNKS, n)
    nb = cb * HEADS
    rows = cb * CHUNK

    def body(qkv_ref, gb_ref, u_ref, w_ref, t_ref, st_ref, ds_ref, do_ref, dqkv_ref, dgb_ref, qs, ks, vs, gs, bs, dos):
        gbv = gb_ref[...]
        gc = _dot_hi(_block_tri(rows), gbv)
        _gather_heads(qkv_ref, gc, gbv, qs, ks, vs, gs, bs, cb)
        for c in range(cb):
            for h in range(HEADS):
                dos[HEADS * c + h] = do_ref[CHUNK * c:CHUNK * (c + 1), 128 * h:128 * (h + 1)].astype(F32)
        q, k, v, b128 = qs[...], ks[...], vs[...], bs[...]
        c = _local_b(q, k, v, gs[...], b128)
        tinv, st, dst, dov = t_ref[...], st_ref[...], ds_ref[...], dos[...]
        wv = w_ref[...]
        vn = u_ref[...] - _bdot(wv, st, NN_B)
        dvn = _bdot(c["amat"], dov, TN_B) + _bdot(c["kd"], dst, NN_B)
        damat = jnp.where(c["lower"], _bdot(dov, vn, NT_B), 0.0)
        dqd = _bdot(dov, st, NT_B)
        dkd = _bdot(vn, dst, NT_B)
        dw = -_bdot(dvn, st, NT_B)
        dgl = jnp.sum(jnp.sum(st.astype(F32) * dst.astype(F32), axis=1, keepdims=True), axis=-1, keepdims=True)
        dvb = _bdot(tinv, dvn, TN_B)
        dkbg = _bdot(tinv, dw, TN_B)
        dtinv = _bdot(dvn, c["vb"], NT_B) + _bdot(dw, c["kbg"], NT_B)
        dl = -_bdot(_bdot(tinv, dtinv, TN_B), tinv, NT_B)
        dl = jnp.where(c["strict"], dl, 0.0)
        dmm = dl * c["decay"]
        dqk = damat * c["decay"]
        wmat = dl * c["lmat"] + damat * c["amat"]
        wcol = jnp.sum(wmat, axis=1, keepdims=True)
        dgc = jnp.sum(wmat, axis=-1, keepdims=True) - jnp.sum(jnp.where(c["eye"], wcol, 0.0), axis=-1, keepdims=True)
        dkb = _bdot(dmm, k, NN_B) + dkbg * c["egc"]
        dk = _bdot(dmm, c["kb"], TN_B) + _bdot(dqk, q, TN_B) + dkd * c["ekd"] + dkb * b128
        dq = _bdot(dqk, k, NN_B) + dqd * c["egc"]
        tmp = jnp.sum(dkd * c["kd"], axis=-1, keepdims=True)
        dgc = (dgc + jnp.sum(dkbg * c["kbg"], axis=-1, keepdims=True) + jnp.sum(dqd * c["qd"], axis=-1, keepdims=True)
               - tmp)
        dglast = jnp.sum(tmp, axis=1, keepdims=True) + dgl * c["gl"][:, :, :1]
        row1 = lax.broadcasted_iota(jnp.int32, (1, CHUNK, 1), 1)
        dgc = dgc + jnp.where(row1 == CHUNK - 1, dglast, 0.0)
        db = jnp.sum(dkb * k, axis=-1, keepdims=True) + jnp.sum(dvb * v, axis=-1, keepdims=True)
        dv = dvb * b128
        lane = lax.broadcasted_iota(jnp.int32, (CHUNK, 128), 1)
        parts = []
        for cc in range(cb):
            acc = jnp.zeros((CHUNK, 128), F32)
            for h in range(HEADS):
                bi = HEADS * cc + h
                rs = slice(CHUNK * cc, CHUNK * (cc + 1))
                dqkv_ref[rs, 128 * h:128 * (h + 1)] = dq[bi]
                dqkv_ref[rs, 1024 + 128 * h:1024 + 128 * (h + 1)] = dk[bi]
                dqkv_ref[rs, 2048 + 128 * h:2048 + 128 * (h + 1)] = dv[bi]
                acc = acc + jnp.where(lane == GA_LANE + h, dgc[bi], 0.0)
            parts.append(acc)
        dgc_all = jnp.concatenate(parts, axis=0)
        dg_all = _dot_hi(_block_tri(rows, transpose=True), dgc_all)
        for cc in range(cb):
            acc = dg_all[CHUNK * cc:CHUNK * (cc + 1)]
            for h in range(HEADS):
                acc = acc + jnp.where(lane == GB_LANE + h, db[HEADS * cc + h], 0.0)
            dgb_ref[CHUNK * cc:CHUNK * (cc + 1), :] = acc

    b3 = lambda d1, d2: pl.BlockSpec((nb, d1, d2), lambda i: (i, 0, 0))
    return _call(body, name="gdn_post_bwd", grid=(n // cb,),
                 in_specs=[_rows(rows, 3072), _rows(rows, 128), b3(CHUNK, 128), b3(CHUNK, 128), b3(CHUNK, CHUNK),
                           b3(128, 128), b3(128, 128), _rows(rows, 1024)],
                 out_specs=[_rows(rows, 3072), _rows(rows, 128)],
                 out_shape=[_sds((s, 3072), F32), _sds((s, 128), F32)],
                 scratch=[pltpu.VMEM((nb, CHUNK, 128), F32)] * 6, sem=("parallel",),
                 args=(qkv, gbeta, u, w, tinv, states, dstates, do))


def _mix_fwd(o_mla, o_gdn, proj, out_gain):
    s = proj.shape[0]
    tm = min(TILES["row"], s)

    def body(om_ref, og_ref, mg_ref, gg_ref, g_ref, o_ref):
        o_ref[:, :1024] = (om_ref[...] * _silu(mg_ref[...].astype(F32))).astype(BF16)
        for h in range(HEADS):
            sl = slice(128 * h, 128 * (h + 1))
            og = og_ref[:, sl]
            on = og * _rms(og, HEAD_DIM) * g_ref[...]
            o_ref[:, 1024 + 128 * h:1024 + 128 * (h + 1)] = (on * _silu(gg_ref[:, sl].astype(F32))).astype(BF16)

    return _call(body, name="mix_fwd", grid=(s // tm,),
                 in_specs=[_rows(tm, 1024), _rows(tm, 1024), _rows(tm, 1024, 1), _rows(tm, 1024, 5), _full((1, 128))],
                 out_specs=_rows(tm, 2048), out_shape=_sds((s, 2048), BF16), args=(o_mla, o_gdn, proj, proj, out_gain))


def _mix_bwd(o_mla, o_gdn, proj, out_gain, dmixed):
    s = proj.shape[0]
    tm = min(TILES["row"], s)

    def body(om_ref, og_ref, mg_ref, gg_ref, g_ref, dm_ref, dg_ref, dom_ref, dog_ref, dmg_ref, dgg_ref, ag_ref):
        @pl.when(pl.program_id(0) == 0)
        def _():
            ag_ref[...] = jnp.zeros_like(ag_ref)

        mg = mg_ref[...].astype(F32)
        dm = dm_ref[...].astype(F32)
        dom_ref[...] = (dm * _silu(mg)).astype(BF16)
        dmg_ref[...] = (dm * om_ref[...] * _dsilu(mg)).astype(BF16)
        for h in range(HEADS):
            sl = slice(128 * h, 128 * (h + 1))
            og, gg, d = og_ref[:, sl], gg_ref[:, sl].astype(F32), dg_ref[:, sl].astype(F32)
            on = og * _rms(og, HEAD_DIM) * g_ref[...]
            dgg_ref[:, sl] = (d * on * _dsilu(gg)).astype(BF16)
            dx, gpart = _rms_bwd(og, g_ref[...], d * _silu(gg), HEAD_DIM)
            dog_ref[:, sl] = dx.astype(BF16)
            ag_ref[...] += _acc8(gpart)

    return _call(body, name="mix_bwd", grid=(s // tm,),
                 in_specs=[_rows(tm, 1024), _rows(tm, 1024), _rows(tm, 1024, 1), _rows(tm, 1024, 5), _full((1, 128)),
                           _rows(tm, 1024, 0), _rows(tm, 1024, 1)],
                 out_specs=[_rows(tm, 1024), _rows(tm, 1024), _rows(tm, 1024), _rows(tm, 1024), _full((8, 128))],
                 out_shape=[_sds((s, 1024), BF16), _sds((s, 1024), BF16), _sds((s, 1024), BF16), _sds((s, 1024), BF16),
                            _sds((8, 128), F32)],
                 args=(o_mla, o_gdn, proj, proj, out_gain, dmixed, dmixed))


def _out_fwd(mixed, w_out, x, target):
    s = x.shape[0]
    tm = min(TILES["mm"], s)
    tn = min(TILES["mm"], D_MODEL)

    def body(m_ref, w_ref, x_ref, t_ref, dy_ref, acc_ref):
        err = x_ref[...] + _dot(m_ref[...], w_ref[...]) - t_ref[...]
        dy_ref[...] = (err * (1.0 / D_MODEL)).astype(BF16)

        @pl.when(pl.program_id(1) == 0)
        def _():
            acc_ref[...] = jnp.zeros_like(acc_ref)

        acc_ref[...] += _acc8(err * err)

    return _call(body, name="out_fwd", grid=(D_MODEL // tn, s // tm),
                 in_specs=[pl.BlockSpec((tm, D_MODEL), lambda j, i: (i, 0)), pl.BlockSpec((D_MODEL, tn), lambda j, i: (0, j)),
                           pl.BlockSpec((tm, tn), lambda j, i: (i, j)), pl.BlockSpec((tm, tn), lambda j, i: (i, j))],
                 out_specs=[pl.BlockSpec((tm, tn), lambda j, i: (i, j)), pl.BlockSpec((8, tn), lambda j, i: (0, j))],
                 out_shape=[_sds((s, D_MODEL), BF16), _sds((8, D_MODEL), F32)], args=(mixed, w_out, x, target))


def _row_tile(r, c):
    if r % 8 != 0:
        return r
    t = 8
    while r % (2 * t) == 0 and 2 * t * c * 4 <= (1 << 20):
        t *= 2
    return t


def _sum_arrays(parts, name, also_bf16=False):
    r, c = parts[0].shape
    tr = _row_tile(r, c)
    n = len(parts)

    def body(*refs):
        acc = refs[0][...].astype(F32)
        for p_ref in refs[1:n]:
            acc = acc + p_ref[...].astype(F32)
        refs[n][...] = acc
        if also_bf16:
            refs[n + 1][...] = acc.astype(BF16)

    nout = 2 if also_bf16 else 1
    out = _call(body, name=name, grid=(r // tr,), in_specs=[_rows(tr, c)] * n, out_specs=[_rows(tr, c)] * nout,
                out_shape=[_sds((r, c), F32), _sds((r, c), BF16)][:nout], args=tuple(parts))
    return out if also_bf16 else out[0]


def _adamw(w, g, m, v, name):
    r, c = w.shape
    tr = _row_tile(r, c)
    c1 = 1.0 - ADAM_B1 ** ADAM_STEP
    c2 = 1.0 - ADAM_B2 ** ADAM_STEP

    def body(w_ref, g_ref, m_ref, v_ref, d_ref, nm_ref, nv_ref):
        gv = g_ref[...]
        nm = ADAM_B1 * m_ref[...] + (1.0 - ADAM_B1) * gv
        nv = ADAM_B2 * v_ref[...] + (1.0 - ADAM_B2) * (gv * gv)
        nm_ref[...] = nm
        nv_ref[...] = nv
        d_ref[...] = -ADAM_LR * ((nm / c1) / (jnp.sqrt(nv / c2) + ADAM_EPS) + ADAM_WD * w_ref[...])

    return _call(body, name=name, grid=(r // tr,), in_specs=[_rows(tr, c)] * 4, out_specs=[_rows(tr, c)] * 3,
                 out_shape=[_sds((r, c), F32)] * 3, args=(w, g, m, v))


ANY = pl.BlockSpec(memory_space=pl.ANY)
CHIP_FLIPS = ((1, 0), (0, 1), (1, 1))


def _comm_call(body, *, name, n_in, out_shape, scratch):
    def kfn(*refs):
        body(*refs)
    return pl.pallas_call(kfn, name=name, in_specs=[ANY] * n_in, out_specs=[ANY] * len(out_shape), out_shape=out_shape,
                          scratch_shapes=list(scratch),
                          compiler_params=pltpu.CompilerParams(has_side_effects=True))


def _all_gather_chips(shards):
    na = len(shards)

    def body(*refs):
        copies = _gather_copies(refs[:na], refs[na:2 * na], *refs[2 * na:])
        _gather_start(copies)
        _gather_finish(copies)

    out_shape = [_sds((4,) + a.shape, a.dtype) for a in shards]
    sem = pltpu.SemaphoreType.DMA((na, 3))
    got = _comm_call(body, name="all_gather_weights", n_in=na, out_shape=out_shape, scratch=[sem, sem, sem, sem])(*shards)
    return _place_own_blocks(got, shards)


def _gather_copies(ins, outs, send_sems, recv_sems, fwd_send, fwd_recv):
    x, y, c = lax.axis_index("x"), lax.axis_index("y"), lax.axis_index("c")
    my_k = 2 * x + y
    direct, forwards = [], []
    for a in range(len(ins)):
        rows = ins[a].shape[0]
        for r, (fx, fy) in enumerate(CHIP_FLIPS):
            px, py = x ^ fx, y ^ fy
            if rows % 32 == 0:
                mine = pl.ds(pl.multiple_of(c * (rows // 2), 16), rows // 2)
                other = pl.ds(pl.multiple_of((1 - c) * (rows // 2), 16), rows // 2)
                rc = pltpu.make_async_remote_copy(
                    src_ref=ins[a].at[mine], dst_ref=outs[a].at[my_k, mine], send_sem=send_sems.at[a, r],
                    recv_sem=recv_sems.at[a, r], device_id=(px, py, c), device_id_type=MESH)
                landed = outs[a].at[2 * px + py, mine]
                fw = pltpu.make_async_remote_copy(
                    src_ref=landed, dst_ref=landed, send_sem=fwd_send.at[a, r], recv_sem=fwd_recv.at[a, r],
                    device_id=(x, y, 1 - c), device_id_type=MESH)
                from_sib = outs[a].at[2 * px + py, other]
                fw_in = pltpu.make_async_remote_copy(
                    src_ref=from_sib, dst_ref=from_sib, send_sem=fwd_send.at[a, r], recv_sem=fwd_recv.at[a, r],
                    device_id=(x, y, 1 - c), device_id_type=MESH)
                forwards.append((rc, fw, fw_in))
            else:
                direct.append(pltpu.make_async_remote_copy(
                    src_ref=ins[a], dst_ref=outs[a].at[my_k], send_sem=send_sems.at[a, r],
                    recv_sem=recv_sems.at[a, r], device_id=(px, py, c), device_id_type=MESH))
    return forwards, direct


def _gather_start(copies):
    forwards, direct = copies
    for rc, _, _ in forwards:
        rc.start()
    for rc in direct:
        rc.start()


def _gather_finish(copies):
    forwards, direct = copies
    for rc, fw, _ in forwards:
        rc.wait_recv()
        fw.start()
    for rc, fw, fw_in in forwards:
        rc.wait_send()
        fw.wait_send()
        fw_in.wait_recv()
    for rc in direct:
        rc.wait()


def _place_own_blocks(got, shards):
    my_k = 2 * lax.axis_index("x") + lax.axis_index("y")
    return [lax.dynamic_update_index_in_dim(g, a, my_k, 0) for g, a in zip(got, shards)]


def _matmul_nn_gather(a, b, shards, *, name, tm, tn, out_dtype):
    m, kdim = a.shape
    n = b.shape[1]
    ni, nj = m // tm, n // tn
    na = len(shards)

    def body(a_ref, b_ref, *rest):
        o_ref = rest[na]
        sems = rest[2 * na + 1:]
        i, j = pl.program_id(0), pl.program_id(1)

        @pl.when((i == 0) & (j == 0))
        def _():
            _gather_start(_gather_copies(rest[:na], rest[na + 1:2 * na + 1], *sems))

        o_ref[...] = _dot(a_ref[...], b_ref[...]).astype(out_dtype)

        @pl.when((i == ni - 1) & (j == nj - 1))
        def _():
            _gather_finish(_gather_copies(rest[:na], rest[na + 1:2 * na + 1], *sems))

    def kfn(*refs):
        body(*refs)
    sem = pltpu.SemaphoreType.DMA((na, 3))
    out = pl.pallas_call(
        kfn, name=name, grid=(ni, nj),
        in_specs=[pl.BlockSpec((tm, kdim), lambda i, j: (i, 0)), pl.BlockSpec((kdim, tn), lambda i, j: (0, j))] + [ANY] * na,
        out_specs=[pl.BlockSpec((tm, tn), lambda i, j: (i, j))] + [ANY] * na,
        out_shape=[_sds((m, n), out_dtype)] + [_sds((4,) + s_a.shape, s_a.dtype) for s_a in shards],
        scratch_shapes=[sem, sem, sem, sem],
        compiler_params=pltpu.CompilerParams(dimension_semantics=("arbitrary", "arbitrary"),
                                             vmem_limit_bytes=VMEM_LIMIT_V7X, has_side_effects=True))(a, b, *shards)
    return out[0], _place_own_blocks(list(out[1:]), shards)


def _all_reduce_small(vec):
    r = vec.shape[0]

    def body(v_ref, o_ref, gath, send_sems, recv_sems):
        x, y, c = lax.axis_index("x"), lax.axis_index("y"), lax.axis_index("c")
        me = 4 * x + 2 * y + c
        gath[me] = v_ref[...]
        copies = []
        for rel in range(1, 8):
            fx, fy, fc = (rel >> 2) & 1, (rel >> 1) & 1, rel & 1
            rc = pltpu.make_async_remote_copy(
                src_ref=v_ref, dst_ref=gath.at[me], send_sem=send_sems.at[rel - 1], recv_sem=recv_sems.at[rel - 1],
                device_id=(x ^ fx, y ^ fy, c ^ fc), device_id_type=MESH)
            rc.start()
            copies.append(rc)
        for rc in copies:
            rc.wait()
        acc = gath[0]
        for d in range(1, 8):
            acc = acc + gath[d]
        o_ref[...] = acc

    def kfn(*refs):
        body(*refs)
    vm = pl.BlockSpec(memory_space=pltpu.VMEM)
    return pl.pallas_call(kfn, name="all_reduce_small", in_specs=[vm], out_specs=vm, out_shape=_sds((r, 128), F32),
                          scratch_shapes=[pltpu.VMEM((8, r, 128), F32), pltpu.SemaphoreType.DMA((7,)),
                                          pltpu.SemaphoreType.DMA((7,))],
                          compiler_params=pltpu.CompilerParams(has_side_effects=True))(vec)


def _exchange_halves(arrs):
    na = len(arrs)

    def body(*refs):
        ins, outs = refs[:na], refs[na:2 * na]
        send_sems, recv_sems = refs[2 * na:]
        x, y, c = lax.axis_index("x"), lax.axis_index("y"), lax.axis_index("c")
        copies = []
        for a in range(na):
            half = ins[a].shape[1] // 2
            src = ins[a].at[:, pl.ds(pl.multiple_of((1 - c) * half, 8), half), :]
            rc = pltpu.make_async_remote_copy(src_ref=src, dst_ref=outs[a], send_sem=send_sems.at[a],
                                              recv_sem=recv_sems.at[a], device_id=(x, y, 1 - c), device_id_type=MESH)
            rc.start()
            copies.append(rc)
        for rc in copies:
            rc.wait()

    out_shape = [_sds((4, a.shape[1] // 2, a.shape[2]), F32) for a in arrs]
    return _comm_call(body, name="rs_pair_exchange", n_in=na, out_shape=out_shape,
                      scratch=[pltpu.SemaphoreType.DMA((na,)), pltpu.SemaphoreType.DMA((na,))])(*arrs)


def _scatter_to_chips(arrs):
    na = len(arrs)

    def body(*refs):
        ins, outs = refs[:na], refs[na:2 * na]
        send_sems, recv_sems = refs[2 * na:]
        x, y, c = lax.axis_index("x"), lax.axis_index("y"), lax.axis_index("c")
        copies = []
        for a in range(na):
            for r, (fx, fy) in enumerate(CHIP_FLIPS):
                px, py = x ^ fx, y ^ fy
                rc = pltpu.make_async_remote_copy(
                    src_ref=ins[a].at[2 * px + py], dst_ref=outs[a].at[r], send_sem=send_sems.at[a, r],
                    recv_sem=recv_sems.at[a, r], device_id=(px, py, c), device_id_type=MESH)
                rc.start()
                copies.append(rc)
        for rc in copies:
            rc.wait()

    out_shape = [_sds((3,) + a.shape[1:], a.dtype) for a in arrs]
    return _comm_call(body, name="rs_chip_scatter", n_in=na, out_shape=out_shape,
                      scratch=[pltpu.SemaphoreType.DMA((na, 3)), pltpu.SemaphoreType.DMA((na, 3))])(*arrs)


def _sum_into_half(parts, name):
    r2, c = parts[0].shape
    tr = _row_tile(r2, c)
    nb = r2 // tr
    n = len(parts)

    def kfn(c_ref, *refs):
        acc = refs[0][...].astype(F32)
        for p_ref in refs[1:n]:
            acc = acc + p_ref[...].astype(F32)
        refs[n][...] = acc

    spec = pltpu.PrefetchScalarGridSpec(
        num_scalar_prefetch=1, grid=(nb,), in_specs=[pl.BlockSpec((tr, c), lambda i, cr: (i, 0))] * n,
        out_specs=pl.BlockSpec((tr, c), lambda i, cr: (cr[0] * nb + i, 0)))
    core = lax.axis_index("c").astype(jnp.int32).reshape(1)
    return pl.pallas_call(kfn, name=name, grid_spec=spec, out_shape=_sds((2 * r2, c), F32),
                          compiler_params=pltpu.CompilerParams(dimension_semantics=("arbitrary",),
                                                               vmem_limit_bytes=VMEM_LIMIT_V7X))(core, *parts)


def _join_in_place(arrs):
    na = len(arrs)

    def body(*refs):
        outs = refs[na:2 * na]
        send_sems, recv_sems = refs[2 * na:]
        x, y, c = lax.axis_index("x"), lax.axis_index("y"), lax.axis_index("c")
        copies = []
        for a in range(na):
            half = outs[a].shape[0] // 2
            mine = outs[a].at[pl.ds(pl.multiple_of(c * half, 8), half), :]
            rc = pltpu.make_async_remote_copy(src_ref=mine, dst_ref=mine, send_sem=send_sems.at[a],
                                              recv_sem=recv_sems.at[a], device_id=(x, y, 1 - c), device_id_type=MESH)
            rc.start()
            copies.append(rc)
        for rc in copies:
            rc.wait()

    def kfn(*refs):
        body(*refs)
    return pl.pallas_call(kfn, name="rs_pair_join", in_specs=[ANY] * na, out_specs=[ANY] * na,
                          out_shape=[_sds(a.shape, F32) for a in arrs],
                          input_output_aliases={a: a for a in range(na)},
                          scratch_shapes=[pltpu.SemaphoreType.DMA((na,)), pltpu.SemaphoreType.DMA((na,))],
                          compiler_params=pltpu.CompilerParams(has_side_effects=True))(*arrs)


def _pair_sum(g, o, name):
    _, r, c = g.shape
    half = r // 2
    tr = _row_tile(half, c)
    nb = half // tr

    def kfn(c_ref, g_ref, o_ref, s32_ref, s16_ref):
        acc = g_ref[...] + o_ref[...]
        s32_ref[...] = acc
        s16_ref[...] = acc.astype(BF16)

    blk = lambda imap: pl.BlockSpec((None, tr, c), imap)
    same = lambda k, i, cr: (k, i, 0)
    spec = pltpu.PrefetchScalarGridSpec(
        num_scalar_prefetch=1, grid=(4, nb), in_specs=[blk(lambda k, i, cr: (k, cr[0] * nb + i, 0)), blk(same)],
        out_specs=[blk(same), blk(same)])
    core = lax.axis_index("c").astype(jnp.int32).reshape(1)
    return pl.pallas_call(kfn, name=name, grid_spec=spec, out_shape=[_sds((4, half, c), F32), _sds((4, half, c), BF16)],
                          compiler_params=pltpu.CompilerParams(dimension_semantics=("arbitrary", "arbitrary"),
                                                               vmem_limit_bytes=VMEM_LIMIT_V7X))(core, g, o)


def _rs_pair_stage(grads):
    got = _exchange_halves(grads)
    sums = [_pair_sum(g, o, f"rs_pair_sum_{a}") for a, (g, o) in enumerate(zip(grads, got))]
    return [s32 for s32, _ in sums], [s16 for _, s16 in sums]


def _rs_chip_stage(pair, recv):
    k_me = 2 * lax.axis_index("x") + lax.axis_index("y")
    halves = []
    for a, (p, rv) in enumerate(zip(pair, recv)):
        own = lax.dynamic_index_in_dim(p, k_me, 0, keepdims=False)
        halves.append(_sum_into_half([own, rv[0], rv[1], rv[2]], f"rs_chip_sum_{a}"))
    return _join_in_place(halves)


def _reduce_scatter(grads):
    pair, pair_bf16 = _rs_pair_stage(grads)
    return _rs_chip_stage(pair, _scatter_to_chips(pair_bf16))


def _matmul_nt_scatter(a, b, send, *, name, tm, tn, out_dtype):
    m, kdim = a.shape
    n = b.shape[0]
    ni, nj = m // tm, n // tn
    na = len(send)

    def body(a_ref, b_ref, *rest):
        send_refs, o_ref, recv_refs = rest[:na], rest[na], rest[na + 1:2 * na + 1]
        send_sems, recv_sems = rest[2 * na + 1:]
        i, j = pl.program_id(0), pl.program_id(1)

        def copies():
            x, y, c = lax.axis_index("x"), lax.axis_index("y"), lax.axis_index("c")
            out = []
            for s_i in range(na):
                for r, (fx, fy) in enumerate(CHIP_FLIPS):
                    px, py = x ^ fx, y ^ fy
                    out.append(pltpu.make_async_remote_copy(
                        src_ref=send_refs[s_i].at[2 * px + py], dst_ref=recv_refs[s_i].at[r],
                        send_sem=send_sems.at[s_i, r], recv_sem=recv_sems.at[s_i, r], device_id=(px, py, c),
                        device_id_type=MESH))
            return out

        @pl.when((i == 0) & (j == 0))
        def _():
            for cp in copies():
                cp.start()

        o_ref[...] = _dot(a_ref[...], b_ref[...], NT).astype(out_dtype)

        @pl.when((i == ni - 1) & (j == nj - 1))
        def _():
            for cp in copies():
                cp.wait()

    def kfn(*refs):
        body(*refs)
    sem = pltpu.SemaphoreType.DMA((na, 3))
    out = pl.pallas_call(
        kfn, name=name, grid=(ni, nj),
        in_specs=[pl.BlockSpec((tm, kdim), lambda i, j: (i, 0)), pl.BlockSpec((tn, kdim), lambda i, j: (j, 0))] + [ANY] * na,
        out_specs=[pl.BlockSpec((tm, tn), lambda i, j: (i, j))] + [ANY] * na,
        out_shape=[_sds((m, n), out_dtype)] + [_sds((3,) + s_a.shape[1:], s_a.dtype) for s_a in send],
        scratch_shapes=[sem, sem],
        compiler_params=pltpu.CompilerParams(dimension_semantics=("arbitrary", "arbitrary"),
                                             vmem_limit_bytes=VMEM_LIMIT_V7X, has_side_effects=True))(a, b, *send)
    return out[0], list(out[1:])


def _pad_w_in(w):
    z = jnp.zeros((w.shape[0], 1024 - 848), w.dtype)
    return jnp.concatenate([w[:, 0:832], w[:, 4928:4944], z, w[:, 832:4928], w[:, 4944:5968]], axis=1)


def _unpad_w_in(g):
    return jnp.concatenate([g[:, 0:832], g[:, 1024:5120], g[:, 832:848], g[:, 5120:6144]], axis=1)


W_IN_SHARD = W_IN_COLS // 4
W_IN_RUNS = ((0, 832, 0), (832, 4928, 1024), (4928, 4944, 832), (4944, 5968, 5120))


def _w_in_grad_blocks(p):
    def orig_cols(lo, hi):
        parts = [p[:, pa + max(lo, a) - a:pa + min(hi, b) - a] for a, b, pa in W_IN_RUNS if max(lo, a) < min(hi, b)]
        return parts[0] if len(parts) == 1 else jnp.concatenate(parts, axis=1)
    return jnp.stack([orig_cols(W_IN_SHARD * k, W_IN_SHARD * (k + 1)) for k in range(4)])


def _pad_w_in_blocks(g):
    def orig_cols(lo, hi):
        return [g[k][:, max(lo, W_IN_SHARD * k) - W_IN_SHARD * k:min(hi, W_IN_SHARD * (k + 1)) - W_IN_SHARD * k]
                for k in range(4) if max(lo, W_IN_SHARD * k) < min(hi, W_IN_SHARD * (k + 1))]
    z = jnp.zeros((g.shape[1], 1024 - 848), g.dtype)
    return jnp.concatenate(orig_cols(0, 832) + orig_cols(4928, 4944) + [z] + orig_cols(832, 4928) + orig_cols(4944, 5968),
                           axis=1)


def _pad_heads(w):
    r = w.shape[0]
    return jnp.pad(w.reshape(r, HEADS, QK_DIM), ((0, 0), (0, 0), (0, QK_PAD - QK_DIM))).reshape(r, HEADS * QK_PAD)


def _unpad_heads(w):
    r = w.shape[0]
    return w.reshape(r, HEADS, QK_PAD)[:, :, :QK_DIM].reshape(r, HEADS * QK_DIM)


def _cols_to_blocks(w):
    r = w.shape[0]
    return w.reshape(r, 4, -1).transpose(1, 0, 2)


def _blocks_to_cols(w):
    return w.transpose(1, 0, 2).reshape(w.shape[1], -1)


SMALL_ROWS = {"norm_gain": (0, 2048), "mla_q_a_gain": (16, 512), "mla_kv_a_gain": (20, 256),
              "mla_q_norm_gain": (22, 192), "mla_k_norm_gain": (24, 192), "gdn_a_log": (26, 8),
              "gdn_dt_bias": (27, 8), "gdn_out_norm_gain": (28, 128)}
LOSS_ROW = 29
SMALL_PACK_ROWS = 32
CONV_ROW = 32


def _pack_small(vals, loss=None):
    rows = []
    at = 0
    for name, (row, size) in SMALL_ROWS.items():
        assert row == at
        nr = -(-size // 128)
        rows.append(jnp.pad(vals[name].reshape(-1).astype(F32), (0, nr * 128 - size)).reshape(nr, 128))
        at += nr
    assert at == LOSS_ROW
    if loss is not None:
        rows.append(jnp.pad(loss.reshape(1, 1), ((0, 0), (0, 127))))
        at += 1
    rows.append(jnp.zeros((SMALL_PACK_ROWS - at, 128), F32))
    return jnp.concatenate(rows, axis=0)


def _unpack_small(pack, name):
    row, size = SMALL_ROWS[name]
    nr = -(-size // 128)
    return pack[row:row + nr].reshape(-1)[:size].reshape(1, size)


def _local_step(x, positions, target, norm_gain, w_in_p, q_a_gain, kv_a_gain, w_uq_p, w_ukv, q_norm_gain,
                k_norm_gain, conv_w, a_log, dt_bias, out_gain, w_out, scatter_hook=None, late_weights=None):
    half = HALF_ROPE
    inv_freq = jnp.power(10000.0, -jnp.arange(half, dtype=F32) / half)
    ang = positions.astype(F32)[:, None] * inv_freq
    cos, sin = jnp.cos(ang), jnp.sin(ang)
    zpad = jnp.zeros((x.shape[0], 64), F32)
    cs = jnp.concatenate([cos, cos, zpad], axis=1)
    sn = jnp.concatenate([-sin, sin, zpad], axis=1)
    gq = jnp.pad(q_norm_gain.reshape(1, QK_DIM), ((0, 0), (0, QK_PAD - QK_DIM)))
    gk = jnp.pad(k_norm_gain.reshape(1, QK_DIM), ((0, 0), (0, QK_PAD - QK_DIM)))
    lane_pad = ((0, 0), (GA_LANE, 128 - GA_LANE - HEADS))
    alog128 = jnp.pad(a_log.reshape(1, HEADS), lane_pad)
    dtb128 = jnp.pad(dt_bias.reshape(1, HEADS), lane_pad)
    ng, qag, kvag, og = (norm_gain.reshape(1, -1), q_a_gain.reshape(1, -1), kv_a_gain.reshape(1, -1),
                         out_gain.reshape(1, -1))

    xn = _norm1_fwd(x, ng)
    misc = _matmul(xn, w_in_p[:, 768:896], mode="nn", out_dtype=F32, name="in_proj_misc")
    if late_weights is None:
        proj = _matmul(xn, w_in_p, mode="nn", out_dtype=BF16, name="in_proj")
    else:
        shards, assemble = late_weights
        proj, gathered = _matmul_nn_gather(xn, w_in_p, shards, name="in_proj_gather", tm=TILES["mm"], tn=TILES["mm"],
                                           out_dtype=BF16)
        w_uq_p, w_ukv, w_out = assemble(gathered)
    cqn, ckvn = _mla_a_norm(proj, qag, kvag)
    q_pre = _matmul(cqn, w_uq_p, mode="nn", out_dtype=BF16, name="q_up")
    kv_pre = _matmul(ckvn, w_ukv, mode="nn", out_dtype=BF16, name="kv_up")
    q, k, v = _mla_post_fwd(q_pre, kv_pre, misc, cs, sn, gq, gk)
    o_mla, lse = _attn_fwd(q, k, v)
    qkv = _gdn_conv_fwd(proj, conv_w)
    gbeta = _gdn_gates_fwd(misc, alog128, dtb128)
    g_u, g_w, g_qd, g_kd, g_a, g_t, g_gl = _gdn_pre(qkv, gbeta)
    o_gdn, states = _gdn_scan_fwd(g_u, g_w, g_qd, g_kd, g_a, g_gl)
    mixed = _mix_fwd(o_mla, o_gdn, proj, og)
    dy, sq = _out_fwd(mixed, w_out, x, target)

    dmixed = _matmul(dy, w_out, mode="nt", out_dtype=BF16, name="d_mixed")
    d_w_out = _matmul(mixed, dy, mode="tn", out_dtype=F32, name="d_w_out", tk=4096)
    do_mla, do_gdn, dmg, dgg, d_out_gain = _mix_bwd(o_mla, o_gdn, proj, og, dmixed)
    dq, dk, dv = _attn_bwd(q, k, v, o_mla, lse, do_mla)
    dq_pre, dkv_pre, dkr, d_gq, d_gk = _mla_post_bwd(q_pre, kv_pre, misc, cs, sn, gq, gk, dq, dk, dv)
    d_w_uq_p = _matmul(cqn, dq_pre, mode="tn", out_dtype=F32, name="d_w_uq", tk=1024)
    d_w_ukv = _matmul(ckvn, dkv_pre, mode="tn", out_dtype=F32, name="d_w_ukv", tk=1024)
    dcqn = _matmul(dq_pre, w_uq_p, mode="nt", out_dtype=F32, name="d_cqn")
    dckvn = _matmul(dkv_pre, w_ukv, mode="nt", out_dtype=F32, name="d_ckvn")
    dcq, dckv, d_qag, d_kvag = _mla_a_norm_bwd(proj, qag, kvag, dcqn, dckvn)
    dstates = _gdn_scan_bwd(g_w, g_qd, g_kd, g_a, g_gl, do_gdn)
    dqkv, dgbeta = _gdn_post_bwd(qkv, gbeta, g_u, g_w, g_t, states, dstates, do_gdn)
    dz = _gdn_conv_bwd_a(proj, conv_w, dqkv)
    dgx, d_conv = _gdn_conv_bwd_b(proj, conv_w, dz)
    dmisc, d_alog, d_dtb = _gdn_gates_bwd(misc, alog128, dtb128, gbeta, dgbeta, dkr)
    dproj = jnp.concatenate([dcq, dckv, dmisc, jnp.zeros((x.shape[0], 128), BF16), dmg, dgx, dgg], axis=1)
    d_w_in_p = _matmul(xn, dproj, mode="tn", out_dtype=F32, name="d_w_in", tk=4096)
    big = {"w_in": d_w_in_p, "w_uq": d_w_uq_p, "w_ukv": d_w_ukv, "w_out": d_w_out, "gdn_conv_w": d_conv}
    if scatter_hook is None:
        dxn, received = _matmul(dproj, w_in_p, mode="nt", out_dtype=BF16, name="d_xn", tm=512, tn=512), None
    else:
        dxn, received = _matmul_nt_scatter(dproj, w_in_p, scatter_hook(big), name="d_xn_scatter", tm=512, tn=512,
                                           out_dtype=BF16)
    grad_x, d_ng = _norm1_bwd(x, ng, dxn, dy)

    small = {"norm_gain": d_ng.sum(0), "mla_q_a_gain": d_qag.sum(0), "mla_kv_a_gain": d_kvag.sum(0),
             "mla_q_norm_gain": d_gq.sum(0)[:QK_DIM], "mla_k_norm_gain": d_gk.sum(0)[:QK_DIM],
             "gdn_a_log": d_alog.sum(0)[GA_LANE:GA_LANE + HEADS], "gdn_dt_bias": d_dtb.sum(0)[GA_LANE:GA_LANE + HEADS],
             "gdn_out_norm_gain": d_out_gain.sum(0)}
    return sq, grad_x, small, big, received


WEIGHTS = ["norm_gain", "w_in", "mla_q_a_gain", "mla_kv_a_gain", "w_uq", "w_ukv", "mla_q_norm_gain", "mla_k_norm_gain",
           "gdn_conv_w", "gdn_a_log", "gdn_dt_bias", "gdn_out_norm_gain", "w_out"]
BIG = ["w_in", "w_uq", "w_ukv", "w_out"]


def kernel(x, positions, norm_gain, w_in, mla_q_a_gain, mla_kv_a_gain, w_uq, w_ukv, mla_q_norm_gain, mla_k_norm_gain, gdn_conv_w, gdn_a_log, gdn_dt_bias, gdn_out_norm_gain, w_out, loss_target, m_norm_gain, m_w_in, m_mla_q_a_gain, m_mla_kv_a_gain, m_w_uq, m_w_ukv, m_mla_q_norm_gain, m_mla_k_norm_gain, m_gdn_conv_w, m_gdn_a_log, m_gdn_dt_bias, m_gdn_out_norm_gain, m_w_out, v_norm_gain, v_w_in, v_mla_q_a_gain, v_mla_kv_a_gain, v_w_uq, v_w_ukv, v_mla_q_norm_gain, v_mla_k_norm_gain, v_gdn_conv_w, v_gdn_a_log, v_gdn_dt_bias, v_gdn_out_norm_gain, v_w_out):
    w = dict(norm_gain=norm_gain, w_in=w_in, mla_q_a_gain=mla_q_a_gain, mla_kv_a_gain=mla_kv_a_gain, w_uq=w_uq,
             w_ukv=w_ukv, mla_q_norm_gain=mla_q_norm_gain, mla_k_norm_gain=mla_k_norm_gain, gdn_conv_w=gdn_conv_w,
             gdn_a_log=gdn_a_log, gdn_dt_bias=gdn_dt_bias, gdn_out_norm_gain=gdn_out_norm_gain, w_out=w_out)
    m = dict(norm_gain=m_norm_gain, w_in=m_w_in, mla_q_a_gain=m_mla_q_a_gain, mla_kv_a_gain=m_mla_kv_a_gain,
             w_uq=m_w_uq, w_ukv=m_w_ukv, mla_q_norm_gain=m_mla_q_norm_gain, mla_k_norm_gain=m_mla_k_norm_gain,
             gdn_conv_w=m_gdn_conv_w, gdn_a_log=m_gdn_a_log, gdn_dt_bias=m_gdn_dt_bias,
             gdn_out_norm_gain=m_gdn_out_norm_gain, w_out=m_w_out)
    v = dict(norm_gain=v_norm_gain, w_in=v_w_in, mla_q_a_gain=v_mla_q_a_gain, mla_kv_a_gain=v_mla_kv_a_gain,
             w_uq=v_w_uq, w_ukv=v_w_ukv, mla_q_norm_gain=v_mla_q_norm_gain, mla_k_norm_gain=v_mla_k_norm_gain,
             gdn_conv_w=v_gdn_conv_w, gdn_a_log=v_gdn_a_log, gdn_dt_bias=v_gdn_dt_bias,
             gdn_out_norm_gain=v_gdn_out_norm_gain, w_out=v_w_out)
    k_me = 2 * lax.axis_index("x") + lax.axis_index("y")

    g_in, g_conv = _all_gather_chips([w_in[0].astype(BF16), gdn_conv_w[0]])
    w_in_p = _pad_w_in_blocks(g_in)
    conv_f = _blocks_to_cols(g_conv)
    late_weights = ([w_uq[0].astype(BF16), w_ukv[0].astype(BF16), w_out[0].astype(BF16)],
                    lambda g: (_pad_heads(_blocks_to_cols(g[0])), _blocks_to_cols(g[1]), g[2].reshape(D_MODEL, D_MODEL)))

    pair_sums = []

    def scatter_hook(big):
        pair, pair_bf16 = _rs_pair_stage([
            _w_in_grad_blocks(big["w_in"]), _cols_to_blocks(_unpad_heads(big["w_uq"])),
            _cols_to_blocks(big["w_ukv"]), big["w_out"].reshape(4, 512, D_MODEL)])
        pair_sums.extend(pair)
        return pair_bf16

    sq, grad_x, small, big, received = _local_step(
        x[0], positions[0], loss_target[0], norm_gain, w_in_p, mla_q_a_gain, mla_kv_a_gain, None, None,
        mla_q_norm_gain, mla_k_norm_gain, conv_f, gdn_a_log, gdn_dt_bias, gdn_out_norm_gain, None, scatter_hook,
        late_weights)

    loss_local = (0.5 / D_MODEL) * jnp.sum(sq)
    pack = jnp.concatenate([_pack_small(small, loss_local), big["gdn_conv_w"].reshape(96, 128)], axis=0)
    tot = _all_reduce_small(pack)
    loss = tot[LOSS_ROW, 0]
    conv_grad = lax.dynamic_slice_in_dim(tot[CONV_ROW:].reshape(4, 3072), k_me * 768, 768, axis=1)

    shard_grads = _rs_chip_stage(pair_sums, received)

    grads = {n: _unpack_small(tot, n) for n in SMALL_ROWS}
    grads["gdn_conv_w"] = conv_grad[None]
    for n, g in zip(BIG, shard_grads):
        grads[n] = g[None]

    delta, new_m, new_v = {}, {}, {}
    sw = _pack_small({n: w[n] for n in SMALL_ROWS})
    sm = _pack_small({n: m[n] for n in SMALL_ROWS})
    sv = _pack_small({n: v[n] for n in SMALL_ROWS})
    sd, snm, snv = _adamw(sw, tot[:SMALL_PACK_ROWS], sm, sv, "adamw_small")
    for n in SMALL_ROWS:
        delta[n], new_m[n], new_v[n] = _unpack_small(sd, n), _unpack_small(snm, n), _unpack_small(snv, n)
    for n in BIG + ["gdn_conv_w"]:
        d, nm, nv = _adamw(w[n][0], grads[n][0], m[n][0], v[n][0], f"adamw_{n}")
        delta[n], new_m[n], new_v[n] = d[None], nm[None], nv[None]

    return (loss, grad_x[None], *[grads[n] for n in WEIGHTS], *[delta[n] for n in WEIGHTS],
            *[new_m[n] for n in WEIGHTS], *[new_v[n] for n in WEIGHTS])
```

```python
import functools
import math

import jax
import jax.numpy as jnp
from jax import lax
from jax.experimental import pallas as pl
from jax.experimental.pallas import tpu as pltpu

F32 = jnp.float32
BF16 = jnp.bfloat16
MESH = pl.DeviceIdType.MESH

D_MODEL = 2048
HEADS = 8
HEAD_DIM = 128
QK_DIM = 192
QK_PAD = 256
HALF_ROPE = 32
CHUNK = 64
NORM_EPS = 1e-6
W_IN_COLS = 5968
W_IN_PAD = 6144
GA_LANE = 64
GB_LANE = 72
ADAM_LR, ADAM_B1, ADAM_B2, ADAM_EPS, ADAM_WD, ADAM_STEP = 0.001, 0.9, 0.999, 1e-08, 0.01, 10
VMEM_LIMIT_V7X = 52 * 1024 * 1024
HI = lax.Precision.HIGHEST
NN = (((1,), (0,)), ((), ()))
NT = (((1,), (1,)), ((), ()))
TN = (((0,), (0,)), ((), ()))

TILES = {"row": 512, "attn": 1024, "mm": 1024}


def _call(body, *, name, grid, in_specs, out_specs, out_shape, args, scratch=(), sem=None):
    def kfn(*refs):
        body(*refs)
    if sem is None:
        sem = ("arbitrary",) * len(grid)
    return pl.pallas_call(
        kfn, name=name, grid=grid, in_specs=in_specs, out_specs=out_specs, out_shape=out_shape,
        scratch_shapes=list(scratch),
        compiler_params=pltpu.CompilerParams(dimension_semantics=sem, vmem_limit_bytes=VMEM_LIMIT_V7X),
    )(*args)


def _rows(tm, w, cb=0):
    return pl.BlockSpec((tm, w), lambda i: (i, cb))


def _full(shape):
    n = len(shape)
    return pl.BlockSpec(shape, lambda *_: (0,) * n)


def _sds(shape, dtype):
    return jax.ShapeDtypeStruct(shape, dtype)


def _acc8(x):
    tm, c = x.shape
    return jnp.sum(x.reshape(tm // 8, 8, c), axis=0)


def _sigmoid(x):
    return 1.0 / (1.0 + jnp.exp(-x))


def _silu(x):
    return x * _sigmoid(x)


def _dsilu(x):
    s = _sigmoid(x)
    return s * (1.0 + x * (1.0 - s))


def _dot(a, b, dims=NN):
    return lax.dot_general(a.astype(BF16), b.astype(BF16), dims, preferred_element_type=F32)


def _dot_hi(a, b, dims=NN):
    return lax.dot_general(a, b, dims, precision=HI, preferred_element_type=F32)


def _matmul(a, b, *, mode, out_dtype, name, tm=None, tn=None, tk=None):
    if mode == "tn":
        kdim, m = a.shape
    else:
        m, kdim = a.shape
    n = b.shape[0] if mode == "nt" else b.shape[1]
    tm = min(tm or TILES["mm"], m)
    tn = min(tn or TILES["mm"], n)
    tk = min(tk or kdim, kdim)
    nk = kdim // tk
    dims = {"nn": NN, "nt": NT, "tn": TN}[mode]
    if mode == "tn":
        a_spec = pl.BlockSpec((tk, tm), lambda i, j, k: (k, i))
    else:
        a_spec = pl.BlockSpec((tm, tk), lambda i, j, k: (i, k))
    if mode == "nt":
        b_spec = pl.BlockSpec((tn, tk), lambda i, j, k: (j, k))
    else:
        b_spec = pl.BlockSpec((tk, tn), lambda i, j, k: (k, j))

    def body(a_ref, b_ref, o_ref):
        r = _dot(a_ref[...], b_ref[...], dims)
        if nk == 1:
            o_ref[...] = r.astype(o_ref.dtype)
        else:
            k = pl.program_id(2)

            @pl.when(k == 0)
            def _():
                o_ref[...] = r

            @pl.when(k > 0)
            def _():
                o_ref[...] += r

    if nk > 1:
        assert out_dtype == F32
    return _call(body, name=name, grid=(m // tm, n // tn, nk), in_specs=[a_spec, b_spec],
                 out_specs=pl.BlockSpec((tm, tn), lambda i, j, k: (i, j)), out_shape=_sds((m, n), out_dtype),
                 args=(a, b))


def _norm1_fwd(x, gain):
    s = x.shape[0]
    tm = min(TILES["row"], s)

    def body(x_ref, g_ref, o_ref):
        xv = x_ref[...]
        r = lax.rsqrt(jnp.mean(xv * xv, axis=-1, keepdims=True) + NORM_EPS)
        o_ref[...] = (xv * r * g_ref[...]).astype(BF16)

    return _call(body, name="norm1_fwd", grid=(s // tm,), in_specs=[_rows(tm, D_MODEL), _full((1, D_MODEL))],
                 out_specs=_rows(tm, D_MODEL), out_shape=_sds((s, D_MODEL), BF16), args=(x, gain))


def _norm1_bwd(x, gain, dxn, dy):
    s = x.shape[0]
    tm = min(TILES["row"], s)

    def body(x_ref, g_ref, dxn_ref, dy_ref, gx_ref, dg_ref):
        xv = x_ref[...]
        r = lax.rsqrt(jnp.mean(xv * xv, axis=-1, keepdims=True) + NORM_EPS)
        nrm = xv * r
        d = dxn_ref[...].astype(F32)
        dn = d * g_ref[...]
        gx_ref[...] = dy_ref[...].astype(F32) + r * (dn - nrm * jnp.mean(dn * nrm, axis=-1, keepdims=True))

        @pl.when(pl.program_id(0) == 0)
        def _():
            dg_ref[...] = jnp.zeros_like(dg_ref)

        dg_ref[...] += _acc8(d * nrm)

    return _call(body, name="norm1_bwd", grid=(s // tm,),
                 in_specs=[_rows(tm, D_MODEL), _full((1, D_MODEL)), _rows(tm, D_MODEL), _rows(tm, D_MODEL)],
                 out_specs=[_rows(tm, D_MODEL), _full((8, D_MODEL))],
                 out_shape=[_sds((s, D_MODEL), F32), _sds((8, D_MODEL), F32)], args=(x, gain, dxn, dy))


def _rms(xv, width):
    return lax.rsqrt(jnp.sum(xv * xv, axis=-1, keepdims=True) * (1.0 / width) + NORM_EPS)


def _mla_a_norm(proj, gq, gkv):
    s = proj.shape[0]
    tm = min(TILES["row"], s)

    def body(cq_ref, ckv_ref, gq_ref, gkv_ref, oq_ref, okv_ref):
        a = cq_ref[...].astype(F32)
        oq_ref[...] = (a * _rms(a, 512) * gq_ref[...]).astype(BF16)
        b = ckv_ref[...].astype(F32)
        okv_ref[...] = (b * _rms(b, 256) * gkv_ref[...]).astype(BF16)

    return _call(body, name="mla_a_norm", grid=(s // tm,),
                 in_specs=[_rows(tm, 512, 0), _rows(tm, 256, 2), _full((1, 512)), _full((1, 256))],
                 out_specs=[_rows(tm, 512), _rows(tm, 256)],
                 out_shape=[_sds((s, 512), BF16), _sds((s, 256), BF16)], args=(proj, proj, gq, gkv))


def _rms_bwd(xv, gain, d, width):
    r = _rms(xv, width)
    nrm = xv * r
    dn = d * gain
    dx = r * (dn - nrm * (jnp.sum(dn * nrm, axis=-1, keepdims=True) * (1.0 / width)))
    return dx, d * nrm


def _mla_a_norm_bwd(proj, gq, gkv, dcqn, dckvn):
    s = proj.shape[0]
    tm = min(TILES["row"], s)

    def body(cq_ref, ckv_ref, gq_ref, gkv_ref, dq_ref, dkv_ref, oq_ref, okv_ref, aq_ref, akv_ref):
        dxq, gq_part = _rms_bwd(cq_ref[...].astype(F32), gq_ref[...], dq_ref[...].astype(F32), 512)
        dxk, gk_part = _rms_bwd(ckv_ref[...].astype(F32), gkv_ref[...], dkv_ref[...].astype(F32), 256)
        oq_ref[...] = dxq.astype(BF16)
        okv_ref[...] = dxk.astype(BF16)

        @pl.when(pl.program_id(0) == 0)
        def _():
            aq_ref[...] = jnp.zeros_like(aq_ref)
            akv_ref[...] = jnp.zeros_like(akv_ref)

        aq_ref[...] += _acc8(gq_part)
        akv_ref[...] += _acc8(gk_part)

    return _call(body, name="mla_a_norm_bwd", grid=(s // tm,),
                 in_specs=[_rows(tm, 512, 0), _rows(tm, 256, 2), _full((1, 512)), _full((1, 256)),
                           _rows(tm, 512), _rows(tm, 256)],
                 out_specs=[_rows(tm, 512), _rows(tm, 256), _full((8, 512)), _full((8, 256))],
                 out_shape=[_sds((s, 512), BF16), _sds((s, 256), BF16), _sds((8, 512), F32), _sds((8, 256), F32)],
                 args=(proj, proj, gq, gkv, dcqn, dckvn))


def _swap32(r):
    lane = lax.broadcasted_iota(jnp.int32, r.shape, 1)
    return jnp.where(lane < HALF_ROPE, pltpu.roll(r, 128 - HALF_ROPE, 1), pltpu.roll(r, HALF_ROPE, 1))


def _mla_post_fwd(q_pre, kv_pre, proj, cs, sn, gq, gk):
    s = q_pre.shape[0]
    tm = min(TILES["row"], s)

    def body(qp_ref, kvp_ref, misc_ref, cs_ref, sn_ref, gq_ref, gk_ref, q_ref, k_ref, v_ref):
        csv, snv = cs_ref[...], sn_ref[...]
        lane = lax.broadcasted_iota(jnp.int32, (tm, 128), 1)
        kr = jnp.where(lane < 64, misc_ref[...], 0.0)
        for h in range(HEADS):
            for src, g_ref, o_ref in ((None, gq_ref, q_ref), (kr, gk_ref, k_ref)):
                if src is None:
                    xv = qp_ref[:, QK_PAD * h:QK_PAD * (h + 1)].astype(F32)
                else:
                    xv = jnp.concatenate([kvp_ref[:, 256 * h:256 * h + 128].astype(F32), src], axis=-1)
                y = xv * _rms(xv, QK_DIM) * g_ref[...]
                if src is None:
                    y = y * Q_PRESCALE
                hi = y[:, 128:]
                hi = hi * csv + _swap32(hi) * snv
                o_ref[:, QK_PAD * h:QK_PAD * h + 128] = y[:, :128].astype(BF16)
                o_ref[:, QK_PAD * h + 128:QK_PAD * (h + 1)] = hi.astype(BF16)
            v_ref[:, 128 * h:128 * (h + 1)] = kvp_ref[:, 256 * h + 128:256 * (h + 1)].astype(BF16)

    return _call(body, name="mla_post_fwd", grid=(s // tm,),
                 in_specs=[_rows(tm, 2048), _rows(tm, 2048), _rows(tm, 128), _rows(tm, 128), _rows(tm, 128),
                           _full((1, QK_PAD)), _full((1, QK_PAD))],
                 out_specs=[_rows(tm, 2048), _rows(tm, 2048), _rows(tm, 1024)],
                 out_shape=[_sds((s, 2048), BF16), _sds((s, 2048), BF16), _sds((s, 1024), BF16)],
                 args=(q_pre, kv_pre, proj, cs, sn, gq, gk))


def _mla_post_bwd(q_pre, kv_pre, proj, cs, sn, gq, gk, dq, dk, dv):
    s = q_pre.shape[0]
    tm = min(TILES["row"], s)

    def body(qp_ref, kvp_ref, misc_ref, cs_ref, sn_ref, gq_ref, gk_ref, dq_ref, dk_ref, dv_ref,
             oq_ref, okv_ref, okr_ref, agq_ref, agk_ref):
        csv, snv = cs_ref[...], sn_ref[...]
        lane = lax.broadcasted_iota(jnp.int32, (tm, 128), 1)
        kr = jnp.where(lane < 64, misc_ref[...], 0.0)

        @pl.when(pl.program_id(0) == 0)
        def _():
            agq_ref[...] = jnp.zeros_like(agq_ref)
            agk_ref[...] = jnp.zeros_like(agk_ref)

        dkr = jnp.zeros((tm, 128), F32)
        for h in range(HEADS):
            for which in (0, 1):
                if which == 0:
                    xv = qp_ref[:, QK_PAD * h:QK_PAD * (h + 1)].astype(F32)
                    d_ref, g_ref, a_ref = dq_ref, gq_ref, agq_ref
                else:
                    xv = jnp.concatenate([kvp_ref[:, 256 * h:256 * h + 128].astype(F32), kr], axis=-1)
                    d_ref, g_ref, a_ref = dk_ref, gk_ref, agk_ref
                dhi = d_ref[:, QK_PAD * h + 128:QK_PAD * (h + 1)]
                dhi = dhi * csv - _swap32(dhi) * snv
                dyv = jnp.concatenate([d_ref[:, QK_PAD * h:QK_PAD * h + 128], dhi], axis=-1)
                if which == 0:
                    dyv = dyv * ATTN_SCALE
                dx, gpart = _rms_bwd(xv, g_ref[...], dyv, QK_DIM)
                a_ref[...] += _acc8(gpart)
                if which == 0:
                    oq_ref[:, QK_PAD * h:QK_PAD * (h + 1)] = dx.astype(BF16)
                else:
                    okv_ref[:, 256 * h:256 * h + 128] = dx[:, :128].astype(BF16)
                    dkr = dkr + dx[:, 128:]
            okv_ref[:, 256 * h + 128:256 * (h + 1)] = dv_ref[:, 128 * h:128 * (h + 1)].astype(BF16)
        okr_ref[...] = dkr

    return _call(body, name="mla_post_bwd", grid=(s // tm,),
                 in_specs=[_rows(tm, 2048), _rows(tm, 2048), _rows(tm, 128), _rows(tm, 128), _rows(tm, 128),
                           _full((1, QK_PAD)), _full((1, QK_PAD)), _rows(tm, 2048), _rows(tm, 2048), _rows(tm, 1024)],
                 out_specs=[_rows(tm, 2048), _rows(tm, 2048), _rows(tm, 128), _full((8, QK_PAD)), _full((8, QK_PAD))],
                 out_shape=[_sds((s, 2048), BF16), _sds((s, 2048), BF16), _sds((s, 128), F32),
                            _sds((8, QK_PAD), F32), _sds((8, QK_PAD), F32)],
                 args=(q_pre, kv_pre, proj, cs, sn, gq, gk, dq, dk, dv))


ATTN_SCALE = QK_DIM ** -0.5
NEG = -1e30


LOG2E = 1.4426950408889634
LN2 = 0.6931471805599453
Q_PRESCALE = ATTN_SCALE * LOG2E
ATTN_SUB_FWD = 256
ATTN_SUB_BWD = 256


def _causal_pairs(nq, kv_major):
    prs = [(i, j) for i in range(nq) for j in range(i + 1)]
    if kv_major:
        prs.sort(key=lambda ij: (ij[1], ij[0]))
    return (jnp.asarray([p[0] for p in prs], jnp.int32), jnp.asarray([p[1] for p in prs], jnp.int32))


def _pair_call(body, *, name, tables, in_specs, out_specs, out_shape, scratch, args):
    def kfn(*refs):
        body(*refs)
    spec = pltpu.PrefetchScalarGridSpec(num_scalar_prefetch=2, grid=(HEADS, tables[0].shape[0]), in_specs=in_specs,
                                        out_specs=out_specs, scratch_shapes=list(scratch))
    return pl.pallas_call(
        kfn, name=name, grid_spec=spec, out_shape=out_shape,
        compiler_params=pltpu.CompilerParams(dimension_semantics=("parallel", "arbitrary"),
                                             vmem_limit_bytes=VMEM_LIMIT_V7X))(*tables, *args)


def _diag_mask(sc, ts, qs):
    row = lax.broadcasted_iota(jnp.int32, sc.shape, 0) + qs * ts
    col = lax.broadcasted_iota(jnp.int32, sc.shape, 1)
    return jnp.where(col <= row, sc, NEG)


def _attn_fwd(q, k, v):
    s = q.shape[0]
    t = min(TILES["attn"], s)
    ts = min(ATTN_SUB_FWD, t)
    nq = s // t

    def slabs(q_ref, k_ref, v_ref, m_s, l_s, acc_s, diag):
        def scores(qs):
            kw = (qs + 1) * ts if diag else t
            sc = lax.dot_general(q_ref[qs * ts:(qs + 1) * ts, :], k_ref[0:kw, :], NT, preferred_element_type=F32)
            return _diag_mask(sc, ts, qs) if diag else sc

        nsub = t // ts
        sc_next = scores(0)
        for qs in range(nsub):
            rq = slice(qs * ts, (qs + 1) * ts)
            kw = (qs + 1) * ts if diag else t
            sc = sc_next
            if qs + 1 < nsub:
                sc_next = scores(qs + 1)
            m_prev = m_s[rq, :]
            m_new = jnp.maximum(m_prev, jnp.max(sc, axis=-1, keepdims=True))
            p = jnp.exp2(sc - m_new)
            alpha = jnp.exp2(m_prev - m_new)
            l_s[rq, :] = alpha * l_s[rq, :] + jnp.sum(p, axis=-1, keepdims=True)
            acc_s[rq, :] = acc_s[rq, :] * alpha + lax.dot_general(p.astype(BF16), v_ref[0:kw, :], NN,
                                                                  preferred_element_type=F32)
            m_s[rq, :] = m_new

    def body(it_ref, jt_ref, q_ref, k_ref, v_ref, o_ref, lse_ref, m_s, l_s, acc_s):
        p = pl.program_id(1)
        i, j = it_ref[p], jt_ref[p]

        @pl.when(j == 0)
        def _():
            m_s[...] = jnp.full_like(m_s, NEG)
            l_s[...] = jnp.zeros_like(l_s)
            acc_s[...] = jnp.zeros_like(acc_s)

        @pl.when(j < i)
        def _():
            slabs(q_ref, k_ref, v_ref, m_s, l_s, acc_s, False)

        @pl.when(j == i)
        def _():
            slabs(q_ref, k_ref, v_ref, m_s, l_s, acc_s, True)
            o_ref[...] = acc_s[...] / l_s[...]
            lse_ref[...] = m_s[...] + jnp.log2(l_s[...])

    qb = lambda h, p, it, jt: (it[p], h)
    kb = lambda h, p, it, jt: (jt[p], h)
    return _pair_call(
        body, name="attn_fwd", tables=_causal_pairs(nq, kv_major=False),
        in_specs=[pl.BlockSpec((t, QK_PAD), qb), pl.BlockSpec((t, QK_PAD), kb), pl.BlockSpec((t, HEAD_DIM), kb)],
        out_specs=[pl.BlockSpec((t, HEAD_DIM), qb),
                   pl.BlockSpec((None, t, 1), lambda h, p, it, jt: (h, it[p], 0))],
        out_shape=[_sds((s, HEADS * HEAD_DIM), F32), _sds((HEADS, s, 1), F32)],
        scratch=[pltpu.VMEM((t, 1), F32), pltpu.VMEM((t, 1), F32), pltpu.VMEM((t, HEAD_DIM), F32)],
        args=(q, k, v))


def _attn_bwd(q, k, v, o, lse, do):
    s = q.shape[0]
    t = min(TILES["attn"], s)
    ts = min(ATTN_SUB_BWD, t)
    nq = s // t

    def slabs(q_ref, k_ref, v_ref, o_ref, lse_ref, do_ref, dq_ref, dk_ref, dv_ref, i, diag):
        def products(qs):
            rq = slice(qs * ts, (qs + 1) * ts)
            kw = (qs + 1) * ts if diag else t
            qv, dof = q_ref[rq, :], do_ref[rq, :].astype(F32)
            sc = lax.dot_general(qv, k_ref[0:kw, :], NT, preferred_element_type=F32)
            dob = dof.astype(BF16)
            dp = lax.dot_general(dob, v_ref[0:kw, :], NT, preferred_element_type=F32)
            return qv, dof, dob, (_diag_mask(sc, ts, qs) if diag else sc), dp

        nsub = t // ts
        ahead = products(0)
        for qs in range(nsub):
            rq = slice(qs * ts, (qs + 1) * ts)
            kw = (qs + 1) * ts if diag else t
            qv, dof, dob, sc, dp = ahead
            if qs + 1 < nsub:
                ahead = products(qs + 1)
            kv = k_ref[0:kw, :]
            p = jnp.exp2(sc - lse_ref[rq, :])
            delta = jnp.sum(dof * o_ref[rq, :], axis=-1, keepdims=True)
            dv_ref[0:kw, :] += lax.dot_general(p.astype(BF16), dob, TN, preferred_element_type=F32)
            ds = (p * (dp - delta)).astype(BF16)
            dk_ref[0:kw, :] += lax.dot_general(ds, qv, TN, preferred_element_type=F32)
            rows = pl.ds(pl.multiple_of(i * t + qs * ts, ts), ts)
            dq_ref[rows, :] += lax.dot_general(ds, kv, NN, preferred_element_type=F32)

    def body(it_ref, jt_ref, q_ref, k_ref, v_ref, o_ref, lse_ref, do_ref, dq_ref, dk_ref, dv_ref):
        p = pl.program_id(1)
        i, j = it_ref[p], jt_ref[p]
        refs = (q_ref, k_ref, v_ref, o_ref, lse_ref, do_ref, dq_ref, dk_ref, dv_ref)

        @pl.when(p == 0)
        def _():
            dq_ref[...] = jnp.zeros_like(dq_ref)

        @pl.when(i == j)
        def _():
            dk_ref[...] = jnp.zeros_like(dk_ref)
            dv_ref[...] = jnp.zeros_like(dv_ref)

        @pl.when(i > j)
        def _():
            slabs(*refs, i, False)

        @pl.when(i == j)
        def _():
            slabs(*refs, i, True)

        @pl.when(i == nq - 1)
        def _():
            dk_ref[...] = dk_ref[...] * LN2

    qb = lambda h, p, it, jt: (it[p], h)
    kb = lambda h, p, it, jt: (jt[p], h)
    return _pair_call(
        body, name="attn_bwd", tables=_causal_pairs(nq, kv_major=True),
        in_specs=[pl.BlockSpec((t, QK_PAD), qb), pl.BlockSpec((t, QK_PAD), kb), pl.BlockSpec((t, HEAD_DIM), kb),
                  pl.BlockSpec((t, HEAD_DIM), qb), pl.BlockSpec((None, t, 1), lambda h, p, it, jt: (h, it[p], 0)),
                  pl.BlockSpec((t, HEAD_DIM), qb)],
        out_specs=[pl.BlockSpec((s, QK_PAD), lambda h, p, it, jt: (0, h)), pl.BlockSpec((t, QK_PAD), kb),
                   pl.BlockSpec((t, HEAD_DIM), kb)],
        out_shape=[_sds((s, HEADS * QK_PAD), F32), _sds((s, HEADS * QK_PAD), F32), _sds((s, HEADS * HEAD_DIM), F32)],
        scratch=(), args=(q, k, v, o, lse, do))


GDN_Q_SCALE = HEAD_DIM ** -0.5


def _shift_down(xv, prev8, sft):
    rolled = pltpu.roll(xv, sft, 0)
    top = pltpu.roll(jnp.concatenate([prev8, xv[:8]], axis=0), sft, 0)[8:]
    return jnp.concatenate([top, rolled[8:]], axis=0)


def _shift_up(xv, next8, sft):
    tm = xv.shape[0]
    rolled = pltpu.roll(xv, tm - sft, 0)
    bot = pltpu.roll(jnp.concatenate([xv[tm - 8:], next8], axis=0), 16 - sft, 0)[:8]
    return jnp.concatenate([rolled[:tm - 8], bot], axis=0)


def _conv_z(xv, prev8, w_ref):
    z = xv * w_ref[3:4, :]
    for sft in (1, 2, 3):
        z = z + _shift_down(xv, prev8, sft) * w_ref[3 - sft:4 - sft, :]
    return z


def _conv_specs(s, tm):
    nb16 = tm // 16
    cur = pl.BlockSpec((tm, 1024), lambda j, i: (i, 2 + j))
    prev = pl.BlockSpec((16, 1024), lambda j, i: (jnp.maximum(i * nb16 - 1, 0), 2 + j))
    return cur, prev


def _prev8(xp_ref, i):
    return jnp.where(i > 0, xp_ref[...].astype(F32)[8:], 0.0)


def _gdn_conv_fwd(proj, conv_w):
    s = proj.shape[0]
    tm = min(TILES["row"], s)
    cur, prev = _conv_specs(s, tm)

    def body(x_ref, xp_ref, w_ref, o_ref):
        j, i = pl.program_id(0), pl.program_id(1)
        a = _silu(_conv_z(x_ref[...].astype(F32), _prev8(xp_ref, i), w_ref))
        qk_scale = jnp.where(j == 0, GDN_Q_SCALE, 1.0)
        for h in range(HEADS):
            seg = a[:, 128 * h:128 * (h + 1)]
            r = lax.rsqrt(jnp.sum(seg * seg, axis=-1, keepdims=True) + NORM_EPS)
            o_ref[:, 128 * h:128 * (h + 1)] = jnp.where(j < 2, seg * r * qk_scale, seg)

    return _call(body, name="gdn_conv_fwd", grid=(3, s // tm),
                 in_specs=[cur, prev, pl.BlockSpec((4, 1024), lambda j, i: (0, j))],
                 out_specs=pl.BlockSpec((tm, 1024), lambda j, i: (i, j)), out_shape=_sds((s, 3072), F32),
                 args=(proj, proj, conv_w))


def _gdn_conv_bwd_a(proj, conv_w, dqkv):
    s = proj.shape[0]
    tm = min(TILES["row"], s)
    cur, prev = _conv_specs(s, tm)

    def body(x_ref, xp_ref, w_ref, d_ref, o_ref):
        j, i = pl.program_id(0), pl.program_id(1)
        z = _conv_z(x_ref[...].astype(F32), _prev8(xp_ref, i), w_ref)
        a = _silu(z)
        dsl = _dsilu(z)
        qk_scale = jnp.where(j == 0, GDN_Q_SCALE, 1.0)
        for h in range(HEADS):
            sl = slice(128 * h, 128 * (h + 1))
            seg = a[:, sl]
            dyv = d_ref[:, sl]
            r = lax.rsqrt(jnp.sum(seg * seg, axis=-1, keepdims=True) + NORM_EPS)
            yh = seg * r
            da_n = qk_scale * r * (dyv - yh * jnp.sum(yh * dyv, axis=-1, keepdims=True))
            o_ref[:, sl] = jnp.where(j < 2, da_n, dyv) * dsl[:, sl]

    return _call(body, name="gdn_conv_bwd_a", grid=(3, s // tm),
                 in_specs=[cur, prev, pl.BlockSpec((4, 1024), lambda j, i: (0, j)),
                           pl.BlockSpec((tm, 1024), lambda j, i: (i, j))],
                 out_specs=pl.BlockSpec((tm, 1024), lambda j, i: (i, j)), out_shape=_sds((s, 3072), F32),
                 args=(proj, proj, conv_w, dqkv))


def _gdn_conv_bwd_b(proj, conv_w, dz):
    s = proj.shape[0]
    tm = min(TILES["row"], s)
    nb8 = tm // 8
    last8 = s // 8 - 1
    cur, prev = _conv_specs(s, tm)

    def body(x_ref, w_ref, dz_ref, dzn_ref, dx_ref, dw_ref):
        i = pl.program_id(1)
        next8 = jnp.where(i < pl.num_programs(1) - 1, dzn_ref[...], 0.0)
        xv, dzv = x_ref[...].astype(F32), dz_ref[...]

        @pl.when(i == 0)
        def _():
            dw_ref[...] = jnp.zeros_like(dw_ref)

        dx = dzv * w_ref[3:4, :]
        dw_ref[3:4, :] += jnp.sum(dzv * xv, axis=0, keepdims=True)
        for sft in (1, 2, 3):
            up = _shift_up(dzv, next8, sft)
            dx = dx + up * w_ref[3 - sft:4 - sft, :]
            dw_ref[3 - sft:4 - sft, :] += jnp.sum(up * xv, axis=0, keepdims=True)
        dx_ref[...] = dx.astype(BF16)

    return _call(body, name="gdn_conv_bwd_b", grid=(3, s // tm),
                 in_specs=[cur, pl.BlockSpec((4, 1024), lambda j, i: (0, j)),
                           pl.BlockSpec((tm, 1024), lambda j, i: (i, j)),
                           pl.BlockSpec((8, 1024), lambda j, i: (jnp.minimum((i + 1) * nb8, last8), j))],
                 out_specs=[pl.BlockSpec((tm, 1024), lambda j, i: (i, j)), pl.BlockSpec((4, 1024), lambda j, i: (0, j))],
                 out_shape=[_sds((s, 3072), BF16), _sds((4, 3072), F32)], args=(proj, conv_w, dz, dz))


def _softplus(xv):
    return jnp.maximum(xv, 0.0) + jnp.log(1.0 + jnp.exp(-jnp.abs(xv)))


def _gdn_gates_fwd(proj, alog128, dtb128):
    s = proj.shape[0]
    tm = min(TILES["row"], s)

    def body(m_ref, a_ref, b_ref, o_ref):
        mv = m_ref[...]
        lane = lax.broadcasted_iota(jnp.int32, mv.shape, 1)
        g = -jnp.exp(a_ref[...]) * _softplus(mv + b_ref[...])
        is_g = (lane >= GA_LANE) & (lane < GA_LANE + HEADS)
        is_b = (lane >= GB_LANE) & (lane < GB_LANE + HEADS)
        o_ref[...] = jnp.where(is_g, g, jnp.where(is_b, _sigmoid(mv), 0.0))

    return _call(body, name="gdn_gates_fwd", grid=(s // tm,),
                 in_specs=[_rows(tm, 128), _full((1, 128)), _full((1, 128))],
                 out_specs=_rows(tm, 128), out_shape=_sds((s, 128), F32), args=(proj, alog128, dtb128))


def _gdn_gates_bwd(proj, alog128, dtb128, gbeta, dgbeta, dkr):
    s = proj.shape[0]
    tm = min(TILES["row"], s)

    def body(m_ref, a_ref, b_ref, gb_ref, d_ref, kr_ref, o_ref, da_ref, db_ref):
        mv, dv = m_ref[...], d_ref[...]
        lane = lax.broadcasted_iota(jnp.int32, mv.shape, 1)
        is_g = (lane >= GA_LANE) & (lane < GA_LANE + HEADS)
        is_b = (lane >= GB_LANE) & (lane < GB_LANE + HEADS)
        dga = jnp.where(is_g, dv * (-jnp.exp(a_ref[...])) * _sigmoid(mv + b_ref[...]), 0.0)
        beta = gb_ref[...]
        dgb = jnp.where(is_b, dv * beta * (1.0 - beta), 0.0)
        o_ref[...] = jnp.where(lane < 64, kr_ref[...], dga + dgb).astype(BF16)

        @pl.when(pl.program_id(0) == 0)
        def _():
            da_ref[...] = jnp.zeros_like(da_ref)
            db_ref[...] = jnp.zeros_like(db_ref)

        da_ref[...] += _acc8(jnp.where(is_g, dv * gb_ref[...], 0.0))
        db_ref[...] += _acc8(dga)

    return _call(body, name="gdn_gates_bwd", grid=(s // tm,),
                 in_specs=[_rows(tm, 128), _full((1, 128)), _full((1, 128)), _rows(tm, 128), _rows(tm, 128),
                           _rows(tm, 128)],
                 out_specs=[_rows(tm, 128), _full((8, 128)), _full((8, 128))],
                 out_shape=[_sds((s, 128), BF16), _sds((8, 128), F32), _sds((8, 128), F32)],
                 args=(proj, alog128, dtb128, gbeta, dgbeta, dkr))


def _col(mat, lane_idx, lane):
    return jnp.sum(jnp.where(lane == lane_idx, mat, 0.0), axis=-1, keepdims=True)


def _chunk_local(qh, kh, vh, gcol, bcol, ii, jj):
    lower, strict, eye = ii >= jj, ii > jj, ii == jj
    grow = jnp.sum(jnp.where(eye, gcol, 0.0), axis=0, keepdims=True)
    decay = jnp.where(lower, jnp.exp(jnp.where(lower, gcol - grow, 0.0)), 0.0)
    kb = kh * bcol
    vb = vh * bcol
    mm = _dot(kb, kh, NT)
    lmat = jnp.where(strict, mm * decay, 0.0)
    pw = -lmat
    tinv = jnp.where(eye, 1.0, 0.0) + pw
    for _ in range(5):
        pw = _dot_hi(pw, pw)
        tinv = tinv + _dot_hi(tinv, pw)
    egc = jnp.exp(gcol)
    kbg = kb * egc
    rhs = jnp.concatenate([vb, kbg], axis=-1)
    sol = _dot_hi(tinv, rhs)
    qk = _dot(qh, kh, NT)
    glast = jnp.sum(jnp.where(ii[:, :1] == CHUNK - 1, gcol, 0.0), axis=0, keepdims=True)
    ekd = jnp.exp(glast - gcol)
    return dict(decay=decay, kb=kb, vb=vb, mm=mm, lmat=lmat, tinv=tinv, egc=egc, kbg=kbg, rhs=rhs,
                u=sol[:, :HEAD_DIM], w=sol[:, HEAD_DIM:], qk=qk, amat=qk * decay, qd=qh * egc, ekd=ekd,
                kd=kh * ekd, gl=jnp.exp(glast), strict=strict, lower=lower, eye=eye)


def _tri(ii, jj):
    return jnp.where(ii >= jj, 1.0, 0.0)


def _gdn_fwd(qkv, gbeta):
    s = qkv.shape[0]
    n = s // CHUNK

    def body(qkv_ref, gb_ref, o_ref, st_ref, state):
        @pl.when(pl.program_id(0) == 0)
        def _():
            state[...] = jnp.zeros_like(state)

        ii = lax.broadcasted_iota(jnp.int32, (CHUNK, CHUNK), 0)
        jj = lax.broadcasted_iota(jnp.int32, (CHUNK, CHUNK), 1)
        lane = lax.broadcasted_iota(jnp.int32, (CHUNK, 128), 1)
        gbv = gb_ref[...]
        gc = _dot_hi(_tri(ii, jj), gbv)
        for h in range(HEADS):
            sl = slice(128 * h, 128 * (h + 1))
            qh = qkv_ref[:, 128 * h:128 * (h + 1)]
            kh = qkv_ref[:, 1024 + 128 * h:1024 + 128 * (h + 1)]
            vh = qkv_ref[:, 2048 + 128 * h:2048 + 128 * (h + 1)]
            c = _chunk_local(qh, kh, vh, _col(gc, GA_LANE + h, lane), _col(gbv, GB_LANE + h, lane), ii, jj)
            st = state[sl, :]
            st_ref[sl, :] = st
            vn = c["u"] - _dot(c["w"], st)
            o_ref[:, sl] = _dot(c["qd"], st) + _dot(c["amat"], vn)
            state[sl, :] = st * c["gl"] + _dot(c["kd"], vn, TN)

    return _call(body, name="gdn_fwd", grid=(n,),
                 in_specs=[_rows(CHUNK, 3072), _rows(CHUNK, 128)],
                 out_specs=[_rows(CHUNK, 1024), _rows(HEADS * 128, 128)],
                 out_shape=[_sds((s, 1024), F32), _sds((n * HEADS * 128, 128), F32)],
                 scratch=[pltpu.VMEM((HEADS * 128, 128), F32)], args=(qkv, gbeta))


def _gdn_bwd(qkv, gbeta, states, do):
    s = qkv.shape[0]
    n = s // CHUNK

    def body(qkv_ref, gb_ref, st_ref, do_ref, dqkv_ref, dgb_ref, dstate):
        @pl.when(pl.program_id(0) == 0)
        def _():
            dstate[...] = jnp.zeros_like(dstate)

        ii = lax.broadcasted_iota(jnp.int32, (CHUNK, CHUNK), 0)
        jj = lax.broadcasted_iota(jnp.int32, (CHUNK, CHUNK), 1)
        lane = lax.broadcasted_iota(jnp.int32, (CHUNK, 128), 1)
        row1 = ii[:, :1]
        gbv = gb_ref[...]
        gc = _dot_hi(_tri(ii, jj), gbv)
        dgc_all = jnp.zeros((CHUNK, 128), F32)
        db_all = jnp.zeros((CHUNK, 128), F32)
        for h in range(HEADS):
            sl = slice(128 * h, 128 * (h + 1))
            qh = qkv_ref[:, 128 * h:128 * (h + 1)]
            kh = qkv_ref[:, 1024 + 128 * h:1024 + 128 * (h + 1)]
            vh = qkv_ref[:, 2048 + 128 * h:2048 + 128 * (h + 1)]
            bcol = _col(gbv, GB_LANE + h, lane)
            c = _chunk_local(qh, kh, vh, _col(gc, GA_LANE + h, lane), bcol, ii, jj)
            st = st_ref[sl, :]
            dst = dstate[sl, :]
            dov = do_ref[:, sl]
            vn = c["u"] - _dot(c["w"], st)
            dvn = _dot(c["amat"], dov, TN) + _dot(c["kd"], dst)
            damat = jnp.where(c["lower"], _dot(dov, vn, NT), 0.0)
            dqd = _dot(dov, st, NT)
            dkd = _dot(vn, dst, NT)
            dw = -_dot(dvn, st, NT)
            dgl = jnp.sum(jnp.sum(st * dst, axis=-1, keepdims=True), axis=0, keepdims=True)
            dstate[sl, :] = _dot(c["qd"], dov, TN) + c["gl"] * dst - _dot(c["w"], dvn, TN)
            dsol = jnp.concatenate([dvn, dw], axis=-1)
            drhs = _dot_hi(c["tinv"], dsol, TN)
            dtinv = _dot_hi(dsol, c["rhs"], NT)
            dl = -_dot_hi(_dot_hi(c["tinv"], dtinv, TN), c["tinv"], NT)
            dl = jnp.where(c["strict"], dl, 0.0)
            dmm = dl * c["decay"]
            dqk = damat * c["decay"]
            wmat = dl * c["lmat"] + damat * c["amat"]
            dgc = jnp.sum(wmat, axis=-1, keepdims=True)
            wcol = jnp.sum(wmat, axis=0, keepdims=True)
            dgc = dgc - jnp.sum(jnp.where(c["eye"], wcol, 0.0), axis=-1, keepdims=True)
            dkb = _dot(dmm, kh) + drhs[:, HEAD_DIM:] * c["egc"]
            dk = _dot(dmm, c["kb"], TN) + _dot(dqk, qh, TN) + dkd * c["ekd"]
            dq = _dot(dqk, kh) + dqd * c["egc"]
            dgc = dgc + jnp.sum(drhs[:, HEAD_DIM:] * c["kbg"], axis=-1, keepdims=True)
            dgc = dgc + jnp.sum(dqd * c["qd"], axis=-1, keepdims=True)
            tmp = jnp.sum(dkd * c["kd"], axis=-1, keepdims=True)
            dgc = dgc - tmp
            dglast = jnp.sum(tmp, axis=0, keepdims=True) + dgl * c["gl"]
            dgc = dgc + jnp.where(row1 == CHUNK - 1, dglast, 0.0)
            dk = dk + dkb * bcol
            db = jnp.sum(dkb * kh, axis=-1, keepdims=True) + jnp.sum(drhs[:, :HEAD_DIM] * vh, axis=-1, keepdims=True)
            dqkv_ref[:, 128 * h:128 * (h + 1)] = dq
            dqkv_ref[:, 1024 + 128 * h:1024 + 128 * (h + 1)] = dk
            dqkv_ref[:, 2048 + 128 * h:2048 + 128 * (h + 1)] = drhs[:, :HEAD_DIM] * bcol
            dgc_all = dgc_all + jnp.where(lane == GA_LANE + h, dgc, 0.0)
            db_all = db_all + jnp.where(lane == GB_LANE + h, db, 0.0)
        dgb_ref[...] = _dot_hi(_tri(jj, ii), dgc_all) + db_all

    rev = lambda w: pl.BlockSpec((CHUNK, w), lambda i: (n - 1 - i, 0))
    return _call(body, name="gdn_bwd", grid=(n,),
                 in_specs=[rev(3072), rev(128), pl.BlockSpec((HEADS * 128, 128), lambda i: (n - 1 - i, 0)), rev(1024)],
                 out_specs=[rev(3072), rev(128)],
                 out_shape=[_sds((s, 3072), F32), _sds((s, 128), F32)],
                 scratch=[pltpu.VMEM((HEADS * 128, 128), F32)], args=(qkv, gbeta, states, do))


NN_B = (((2,), (1,)), ((0,), (0,)))
NT_B = (((2,), (2,)), ((0,), (0,)))
TN_B = (((1,), (1,)), ((0,), (0,)))
GDN_PAR_CHUNKS = 2
GDN_SEQ_CHUNKS = 4


def _gather_heads(qkv_ref, gc, gbv, qs, ks, vs, gs, bs, nchunks):
    lane = lax.broadcasted_iota(jnp.int32, (CHUNK, 128), 1)
    for c in range(nchunks):
        rows = slice(CHUNK * c, CHUNK * (c + 1))
        for h in range(HEADS):
            b = HEADS * c + h
            qs[b] = qkv_ref[rows, 128 * h:128 * (h + 1)]
            ks[b] = qkv_ref[rows, 1024 + 128 * h:1024 + 128 * (h + 1)]
            vs[b] = qkv_ref[rows, 2048 + 128 * h:2048 + 128 * (h + 1)]
            gs[b] = jnp.broadcast_to(_col(gc[rows], GA_LANE + h, lane), (CHUNK, 128))
            bs[b] = jnp.broadcast_to(_col(gbv[rows], GB_LANE + h, lane), (CHUNK, 128))


def _block_tri(rows, transpose=False):
    ri = lax.broadcasted_iota(jnp.int32, (rows, rows), 0)
    ci = lax.broadcasted_iota(jnp.int32, (rows, rows), 1)
    same = (ri >> 6) == (ci >> 6)
    return jnp.where(same & ((ci >= ri) if transpose else (ri >= ci)), 1.0, 0.0)


def _local_b(q, k, v, g128, b128):
    ii = lax.broadcasted_iota(jnp.int32, (1, CHUNK, CHUNK), 1)
    jj = lax.broadcasted_iota(jnp.int32, (1, CHUNK, CHUNK), 2)
    lower, strict, eye = ii >= jj, ii > jj, ii == jj
    g64 = g128[:, :, :CHUNK]
    grow = jnp.sum(jnp.where(eye, g64, 0.0), axis=1, keepdims=True)
    decay = jnp.where(lower, jnp.exp(jnp.where(lower, g64 - grow, 0.0)), 0.0)
    kb = k * b128
    vb = v * b128
    mm = lax.dot_general(kb.astype(BF16), k.astype(BF16), NT_B, preferred_element_type=F32)
    lmat = jnp.where(strict, mm * decay, 0.0)
    egc = jnp.exp(g128)
    kbg = kb * egc
    qk = lax.dot_general(q.astype(BF16), k.astype(BF16), NT_B, preferred_element_type=F32)
    row = lax.broadcasted_iota(jnp.int32, (1, CHUNK, 128), 1)
    glast = jnp.sum(jnp.where(row == CHUNK - 1, g128, 0.0), axis=1, keepdims=True)
    ekd = jnp.exp(glast - g128)
    return dict(decay=decay, kb=kb, vb=vb, lmat=lmat, egc=egc, kbg=kbg, amat=qk * decay, qd=q * egc, ekd=ekd,
                kd=k * ekd, gl=jnp.exp(glast), lower=lower, strict=strict, eye=eye)


def _bdot(a, b, dims):
    return lax.dot_general(a.astype(BF16), b.astype(BF16), dims, preferred_element_type=F32)


def _split(a):
    hi = a.astype(BF16)
    return hi, (a - hi.astype(F32)).astype(BF16)


def _bdot_hi(a, b, dims):
    ah, al = _split(a)
    bh, bl = _split(b)
    d = lambda x, y: lax.dot_general(x, y, dims, preferred_element_type=F32)
    return d(ah, bh) + d(ah, bl) + d(al, bh)


def _gdn_pre(qkv, gbeta):
    s = qkv.shape[0]
    n = s // CHUNK
    cb = min(GDN_PAR_CHUNKS, n)
    nb = cb * HEADS
    rows = cb * CHUNK

    def body(qkv_ref, gb_ref, u_ref, w_ref, qd_ref, kd_ref, a_ref, t_ref, gl_ref, qs, ks, vs, gs, bs):
        gbv = gb_ref[...]
        gc = _dot_hi(_block_tri(rows), gbv)
        _gather_heads(qkv_ref, gc, gbv, qs, ks, vs, gs, bs, cb)
        c = _local_b(qs[...], ks[...], vs[...], gs[...], bs[...])
        pw = -c["lmat"]
        tinv = jnp.where(c["eye"], 1.0, 0.0) + pw
        for _ in range(5):
            pw = _bdot_hi(pw, pw, NN_B)
            tinv = tinv + _bdot_hi(tinv, pw, NN_B)
        u_ref[...] = _bdot_hi(tinv, c["vb"], NN_B)
        w_ref[...] = _bdot_hi(tinv, c["kbg"], NN_B).astype(BF16)
        qd_ref[...] = c["qd"].astype(BF16)
        kd_ref[...] = c["kd"].astype(BF16)
        a_ref[...] = c["amat"].astype(BF16)
        t_ref[...] = tinv
        gl_ref[...] = c["gl"]

    b3 = lambda d: pl.BlockSpec((nb, CHUNK, d), lambda i: (i, 0, 0))
    nt = n * HEADS
    return _call(body, name="gdn_pre", grid=(n // cb,),
                 in_specs=[_rows(rows, 3072), _rows(rows, 128)],
                 out_specs=[b3(128), b3(128), b3(128), b3(128), b3(CHUNK), b3(CHUNK),
                            pl.BlockSpec((nb, 1, 128), lambda i: (i, 0, 0))],
                 out_shape=[_sds((nt, CHUNK, 128), F32), _sds((nt, CHUNK, 128), BF16), _sds((nt, CHUNK, 128), BF16),
                            _sds((nt, CHUNK, 128), BF16), _sds((nt, CHUNK, CHUNK), BF16), _sds((nt, CHUNK, CHUNK), F32),
                            _sds((nt, 1, 128), F32)],
                 scratch=[pltpu.VMEM((nb, CHUNK, 128), F32)] * 5, sem=("parallel",), args=(qkv, gbeta))


def _gdn_scan_fwd(u, w, qd, kd, amat, gl):
    nt = u.shape[0]
    n = nt // HEADS
    cs = min(GDN_SEQ_CHUNKS, n)

    def body(u_ref, w_ref, qd_ref, kd_ref, a_ref, gl_ref, o_ref, st_ref, state):
        @pl.when(pl.program_id(0) == 0)
        def _():
            state[...] = jnp.zeros_like(state)

        for c in range(cs):
            sl = slice(HEADS * c, HEADS * (c + 1))
            st = state[...]
            stb = st.astype(BF16)
            st_ref[sl] = stb
            vn = u_ref[sl] - lax.dot_general(w_ref[sl], stb, NN_B, preferred_element_type=F32)
            vnb = vn.astype(BF16)
            o = (lax.dot_general(qd_ref[sl], stb, NN_B, preferred_element_type=F32)
                 + lax.dot_general(a_ref[sl], vnb, NN_B, preferred_element_type=F32))
            state[...] = st * gl_ref[sl] + lax.dot_general(kd_ref[sl], vnb, TN_B, preferred_element_type=F32)
            for h in range(HEADS):
                o_ref[CHUNK * c:CHUNK * (c + 1), 128 * h:128 * (h + 1)] = o[h]

    b3 = lambda d: pl.BlockSpec((cs * HEADS, CHUNK, d), lambda i: (i, 0, 0))
    return _call(body, name="gdn_scan_fwd", grid=(n // cs,),
                 in_specs=[b3(128), b3(128), b3(128), b3(128), b3(CHUNK), pl.BlockSpec((cs * HEADS, 1, 128), lambda i: (i, 0, 0))],
                 out_specs=[_rows(cs * CHUNK, 1024), pl.BlockSpec((cs * HEADS, 128, 128), lambda i: (i, 0, 0))],
                 out_shape=[_sds((n * CHUNK, 1024), F32), _sds((nt, 128, 128), BF16)],
                 scratch=[pltpu.VMEM((HEADS, 128, 128), F32)], args=(u, w, qd, kd, amat, gl))


def _gdn_scan_bwd(w, qd, kd, amat, gl, do):
    nt = w.shape[0]
    n = nt // HEADS
    cs = min(GDN_SEQ_CHUNKS, n)
    ng = n // cs

    def body(w_ref, qd_ref, kd_ref, a_ref, gl_ref, do_ref, ds_ref, dstate, dos):
        @pl.when(pl.program_id(0) == 0)
        def _():
            dstate[...] = jnp.zeros_like(dstate)

        for c in reversed(range(cs)):
            sl = slice(HEADS * c, HEADS * (c + 1))
            for h in range(HEADS):
                dos[h] = do_ref[CHUNK * c:CHUNK * (c + 1), 128 * h:128 * (h + 1)].astype(BF16)
            dob = dos[...]
            dst = dstate[...]
            dstb = dst.astype(BF16)
            ds_ref[sl] = dstb
            dvn = (lax.dot_general(a_ref[sl], dob, TN_B, preferred_element_type=F32)
                   + lax.dot_general(kd_ref[sl], dstb, NN_B, preferred_element_type=F32))
            dstate[...] = (lax.dot_general(qd_ref[sl], dob, TN_B, preferred_element_type=F32) + gl_ref[sl] * dst
                           - lax.dot_general(w_ref[sl], dvn.astype(BF16), TN_B, preferred_element_type=F32))

    b3 = lambda d: pl.BlockSpec((cs * HEADS, CHUNK, d), lambda i: (ng - 1 - i, 0, 0))
    return _call(body, name="gdn_scan_bwd", grid=(ng,),
                 in_specs=[b3(128), b3(128), b3(128), b3(CHUNK), pl.BlockSpec((cs * HEADS, 1, 128), lambda i: (ng - 1 - i, 0, 0)),
                           pl.BlockSpec((cs * CHUNK, 1024), lambda i: (ng - 1 - i, 0))],
                 out_specs=pl.BlockSpec((cs * HEADS, 128, 128), lambda i: (ng - 1 - i, 0, 0)),
                 out_shape=_sds((nt, 128, 128), BF16),
                 scratch=[pltpu.VMEM((HEADS, 128, 128), F32), pltpu.VMEM((HEADS, CHUNK, 128), BF16)],
                 args=(w, qd, kd, amat, gl, do))


def _gdn_post_bwd(qkv, gbeta, u, w, tinv, states, dstates, do):
    s = qkv.shape[0]
    n = s // CHUNK
    cb = min(GDN_PAR_CHUNKS, n)
    nb = cb * HEADS
    rows = cb * CHUNK

    def body(qkv_ref, gb_ref, u_ref, w_ref, t_ref, st_ref, ds_ref, do_ref, dqkv_ref, dgb_ref, qs, ks, vs, gs, bs, dos):
        gbv = gb_ref[...]
        gc = _dot_hi(_block_tri(rows), gbv)
        _gather_heads(qkv_ref, gc, gbv, qs, ks, vs, gs, bs, cb)
        for c in range(cb):
            for h in range(HEADS):
                dos[HEADS * c + h] = do_ref[CHUNK * c:CHUNK * (c + 1), 128 * h:128 * (h + 1)].astype(F32)
        q, k, v, b128 = qs[...], ks[...], vs[...], bs[...]
        c = _local_b(q, k, v, gs[...], b128)
        tinv, st, dst, dov = t_ref[...], st_ref[...], ds_ref[...], dos[...]
        wv = w_ref[...]
        vn = u_ref[...] - _bdot(wv, st, NN_B)
        dvn = _bdot(c["amat"], dov, TN_B) + _bdot(c["kd"], dst, NN_B)
        damat = jnp.where(c["lower"], _bdot(dov, vn, NT_B), 0.0)
        dqd = _bdot(dov, st, NT_B)
        dkd = _bdot(vn, dst, NT_B)
        dw = -_bdot(dvn, st, NT_B)
        dgl = jnp.sum(jnp.sum(st.astype(F32) * dst.astype(F32), axis=1, keepdims=True), axis=-1, keepdims=True)
        dvb = _bdot(tinv, dvn, TN_B)
        dkbg = _bdot(tinv, dw, TN_B)
        dtinv = _bdot(dvn, c["vb"], NT_B) + _bdot(dw, c["kbg"], NT_B)
        dl = -_bdot(_bdot(tinv, dtinv, TN_B), tinv, NT_B)
        dl = jnp.where(c["strict"], dl, 0.0)
        dmm = dl * c["decay"]
        dqk = damat * c["decay"]
        wmat = dl * c["lmat"] + damat * c["amat"]
        wcol = jnp.sum(wmat, axis=1, keepdims=True)
        dgc = jnp.sum(wmat, axis=-1, keepdims=True) - jnp.sum(jnp.where(c["eye"], wcol, 0.0), axis=-1, keepdims=True)
        dkb = _bdot(dmm, k, NN_B) + dkbg * c["egc"]
        dk = _bdot(dmm, c["kb"], TN_B) + _bdot(dqk, q, TN_B) + dkd * c["ekd"] + dkb * b128
        dq = _bdot(dqk, k, NN_B) + dqd * c["egc"]
        tmp = jnp.sum(dkd * c["kd"], axis=-1, keepdims=True)
        dgc = (dgc + jnp.sum(dkbg * c["kbg"], axis=-1, keepdims=True) + jnp.sum(dqd * c["qd"], axis=-1, keepdims=True)
               - tmp)
        dglast = jnp.sum(tmp, axis=1, keepdims=True) + dgl * c["gl"][:, :, :1]
        row1 = lax.broadcasted_iota(jnp.int32, (1, CHUNK, 1), 1)
        dgc = dgc + jnp.where(row1 == CHUNK - 1, dglast, 0.0)
        db = jnp.sum(dkb * k, axis=-1, keepdims=True) + jnp.sum(dvb * v, axis=-1, keepdims=True)
        dv = dvb * b128
        lane = lax.broadcasted_iota(jnp.int32, (CHUNK, 128), 1)
        parts = []
        for cc in range(cb):
            acc = jnp.zeros((CHUNK, 128), F32)
            for h in range(HEADS):
                bi = HEADS * cc + h
                rs = slice(CHUNK * cc, CHUNK * (cc + 1))
                dqkv_ref[rs, 128 * h:128 * (h + 1)] = dq[bi]
                dqkv_ref[rs, 1024 + 128 * h:1024 + 128 * (h + 1)] = dk[bi]
                dqkv_ref[rs, 2048 + 128 * h:2048 + 128 * (h + 1)] = dv[bi]
                acc = acc + jnp.where(lane == GA_LANE + h, dgc[bi], 0.0)
            parts.append(acc)
        dgc_all = jnp.concatenate(parts, axis=0)
        dg_all = _dot_hi(_block_tri(rows, transpose=True), dgc_all)
        for cc in range(cb):
            acc = dg_all[CHUNK * cc:CHUNK * (cc + 1)]
            for h in range(HEADS):
                acc = acc + jnp.where(lane == GB_LANE + h, db[HEADS * cc + h], 0.0)
            dgb_ref[CHUNK * cc:CHUNK * (cc + 1), :] = acc

    b3 = lambda d1, d2: pl.BlockSpec((nb, d1, d2), lambda i: (i, 0, 0))
    return _call(body, name="gdn_post_bwd", grid=(n // cb,),
                 in_specs=[_rows(rows, 3072), _rows(rows, 128), b3(CHUNK, 128), b3(CHUNK, 128), b3(CHUNK, CHUNK),
                           b3(128, 128), b3(128, 128), _rows(rows, 1024)],
                 out_specs=[_rows(rows, 3072), _rows(rows, 128)],
                 out_shape=[_sds((s, 3072), F32), _sds((s, 128), F32)],
                 scratch=[pltpu.VMEM((nb, CHUNK, 128), F32)] * 6, sem=("parallel",),
                 args=(qkv, gbeta, u, w, tinv, states, dstates, do))


def _mix_fwd(o_mla, o_gdn, proj, out_gain):
    s = proj.shape[0]
    tm = min(TILES["row"], s)

    def body(om_ref, og_ref, mg_ref, gg_ref, g_ref, o_ref):
        o_ref[:, :1024] = (om_ref[...] * _silu(mg_ref[...].astype(F32))).astype(BF16)
        for h in range(HEADS):
            sl = slice(128 * h, 128 * (h + 1))
            og = og_ref[:, sl]
            on = og * _rms(og, HEAD_DIM) * g_ref[...]
            o_ref[:, 1024 + 128 * h:1024 + 128 * (h + 1)] = (on * _silu(gg_ref[:, sl].astype(F32))).astype(BF16)

    return _call(body, name="mix_fwd", grid=(s // tm,),
                 in_specs=[_rows(tm, 1024), _rows(tm, 1024), _rows(tm, 1024, 1), _rows(tm, 1024, 5), _full((1, 128))],
                 out_specs=_rows(tm, 2048), out_shape=_sds((s, 2048), BF16), args=(o_mla, o_gdn, proj, proj, out_gain))


def _mix_bwd(o_mla, o_gdn, proj, out_gain, dmixed):
    s = proj.shape[0]
    tm = min(TILES["row"], s)

    def body(om_ref, og_ref, mg_ref, gg_ref, g_ref, dm_ref, dg_ref, dom_ref, dog_ref, dmg_ref, dgg_ref, ag_ref):
        @pl.when(pl.program_id(0) == 0)
        def _():
            ag_ref[...] = jnp.zeros_like(ag_ref)

        mg = mg_ref[...].astype(F32)
        dm = dm_ref[...].astype(F32)
        dom_ref[...] = (dm * _silu(mg)).astype(BF16)
        dmg_ref[...] = (dm * om_ref[...] * _dsilu(mg)).astype(BF16)
        for h in range(HEADS):
            sl = slice(128 * h, 128 * (h + 1))
            og, gg, d = og_ref[:, sl], gg_ref[:, sl].astype(F32), dg_ref[:, sl].astype(F32)
            on = og * _rms(og, HEAD_DIM) * g_ref[...]
            dgg_ref[:, sl] = (d * on * _dsilu(gg)).astype(BF16)
            dx, gpart = _rms_bwd(og, g_ref[...], d * _silu(gg), HEAD_DIM)
            dog_ref[:, sl] = dx.astype(BF16)
            ag_ref[...] += _acc8(gpart)

    return _call(body, name="mix_bwd", grid=(s // tm,),
                 in_specs=[_rows(tm, 1024), _rows(tm, 1024), _rows(tm, 1024, 1), _rows(tm, 1024, 5), _full((1, 128)),
                           _rows(tm, 1024, 0), _rows(tm, 1024, 1)],
                 out_specs=[_rows(tm, 1024), _rows(tm, 1024), _rows(tm, 1024), _rows(tm, 1024), _full((8, 128))],
                 out_shape=[_sds((s, 1024), BF16), _sds((s, 1024), BF16), _sds((s, 1024), BF16), _sds((s, 1024), BF16),
                            _sds((8, 128), F32)],
                 args=(o_mla, o_gdn, proj, proj, out_gain, dmixed, dmixed))


def _out_fwd(mixed, w_out, x, target):
    s = x.shape[0]
    tm = min(TILES["mm"], s)
    tn = min(TILES["mm"], D_MODEL)

    def body(m_ref, w_ref, x_ref, t_ref, dy_ref, acc_ref):
        err = x_ref[...] + _dot(m_ref[...], w_ref[...]) - t_ref[...]
        dy_ref[...] = (err * (1.0 / D_MODEL)).astype(BF16)

        @pl.when(pl.program_id(1) == 0)
        def _():
            acc_ref[...] = jnp.zeros_like(acc_ref)

        acc_ref[...] += _acc8(err * err)

    return _call(body, name="out_fwd", grid=(D_MODEL // tn, s // tm),
                 in_specs=[pl.BlockSpec((tm, D_MODEL), lambda j, i: (i, 0)), pl.BlockSpec((D_MODEL, tn), lambda j, i: (0, j)),
                           pl.BlockSpec((tm, tn), lambda j, i: (i, j)), pl.BlockSpec((tm, tn), lambda j, i: (i, j))],
                 out_specs=[pl.BlockSpec((tm, tn), lambda j, i: (i, j)), pl.BlockSpec((8, tn), lambda j, i: (0, j))],
                 out_shape=[_sds((s, D_MODEL), BF16), _sds((8, D_MODEL), F32)], args=(mixed, w_out, x, target))


def _row_tile(r, c):
    if r % 8 != 0:
        return r
    t = 8
    while r % (2 * t) == 0 and 2 * t * c * 4 <= (1 << 20):
        t *= 2
    return t


def _sum_arrays(parts, name, also_bf16=False):
    r, c = parts[0].shape
    tr = _row_tile(r, c)
    n = len(parts)

    def body(*refs):
        acc = refs[0][...].astype(F32)
        for p_ref in refs[1:n]:
            acc = acc + p_ref[...].astype(F32)
        refs[n][...] = acc
        if also_bf16:
            refs[n + 1][...] = acc.astype(BF16)

    nout = 2 if also_bf16 else 1
    out = _call(body, name=name, grid=(r // tr,), in_specs=[_rows(tr, c)] * n, out_specs=[_rows(tr, c)] * nout,
                out_shape=[_sds((r, c), F32), _sds((r, c), BF16)][:nout], args=tuple(parts))
    return out if also_bf16 else out[0]


def _adamw(w, g, m, v, name):
    r, c = w.shape
    c1 = 1.0 - ADAM_B1 ** ADAM_STEP
    c2 = 1.0 - ADAM_B2 ** ADAM_STEP

    def body(w_ref, g_ref, m_ref, v_ref, d_ref, nm_ref, nv_ref):
        gv = g_ref[...]
        nm = ADAM_B1 * m_ref[...] + (1.0 - ADAM_B1) * gv
        nv = ADAM_B2 * v_ref[...] + (1.0 - ADAM_B2) * (gv * gv)
        nm_ref[...] = nm
        nv_ref[...] = nv
        d_ref[...] = -ADAM_LR * ((nm / c1) / (jnp.sqrt(nv / c2) + ADAM_EPS) + ADAM_WD * w_ref[...])

    if r % 8 == 0:
        tr = _row_tile(r, c)
        grid, spec = (r // tr,), _rows(tr, c)
    else:
        tc = c
        while tc % 256 == 0 and r * tc * 4 > (3 << 19):
            tc //= 2
        grid, spec = (c // tc,), pl.BlockSpec((r, tc), lambda i: (0, i))
    return _call(body, name=name, grid=grid, in_specs=[spec] * 4, out_specs=[spec] * 3,
                 out_shape=[_sds((r, c), F32)] * 3, args=(w, g, m, v))


ANY = pl.BlockSpec(memory_space=pl.ANY)
CHIP_FLIPS = ((1, 0), (0, 1), (1, 1))


def _comm_call(body, *, name, n_in, out_shape, scratch):
    def kfn(*refs):
        body(*refs)
    return pl.pallas_call(kfn, name=name, in_specs=[ANY] * n_in, out_specs=[ANY] * len(out_shape), out_shape=out_shape,
                          scratch_shapes=list(scratch),
                          compiler_params=pltpu.CompilerParams(has_side_effects=True))


def _all_gather_chips(shards):
    na = len(shards)

    def body(*refs):
        copies = _gather_copies(refs[:na], refs[na:2 * na], *refs[2 * na:])
        _gather_start(copies)
        _gather_finish(copies)

    out_shape = [_sds((4,) + a.shape, a.dtype) for a in shards]
    sem = pltpu.SemaphoreType.DMA((na, 3))
    got = _comm_call(body, name="all_gather_weights", n_in=na, out_shape=out_shape, scratch=[sem, sem, sem, sem])(*shards)
    return _place_own_blocks(got, shards)


def _gather_copies(ins, outs, send_sems, recv_sems, fwd_send, fwd_recv):
    x, y, c = lax.axis_index("x"), lax.axis_index("y"), lax.axis_index("c")
    my_k = 2 * x + y
    direct, forwards = [], []
    for a in range(len(ins)):
        rows = ins[a].shape[0]
        for r, (fx, fy) in enumerate(CHIP_FLIPS):
            px, py = x ^ fx, y ^ fy
            if rows % 32 == 0:
                mine = pl.ds(pl.multiple_of(c * (rows // 2), 16), rows // 2)
                other = pl.ds(pl.multiple_of((1 - c) * (rows // 2), 16), rows // 2)
                rc = pltpu.make_async_remote_copy(
                    src_ref=ins[a].at[mine], dst_ref=outs[a].at[my_k, mine], send_sem=send_sems.at[a, r],
                    recv_sem=recv_sems.at[a, r], device_id=(px, py, c), device_id_type=MESH)
                landed = outs[a].at[2 * px + py, mine]
                fw = pltpu.make_async_remote_copy(
                    src_ref=landed, dst_ref=landed, send_sem=fwd_send.at[a, r], recv_sem=fwd_recv.at[a, r],
                    device_id=(x, y, 1 - c), device_id_type=MESH)
                from_sib = outs[a].at[2 * px + py, other]
                fw_in = pltpu.make_async_remote_copy(
                    src_ref=from_sib, dst_ref=from_sib, send_sem=fwd_send.at[a, r], recv_sem=fwd_recv.at[a, r],
                    device_id=(x, y, 1 - c), device_id_type=MESH)
                forwards.append((rc, fw, fw_in))
            else:
                direct.append(pltpu.make_async_remote_copy(
                    src_ref=ins[a], dst_ref=outs[a].at[my_k], send_sem=send_sems.at[a, r],
                    recv_sem=recv_sems.at[a, r], device_id=(px, py, c), device_id_type=MESH))
    return forwards, direct


def _gather_start(copies):
    forwards, direct = copies
    for rc, _, _ in forwards:
        rc.start()
    for rc in direct:
        rc.start()


def _gather_finish(copies):
    forwards, direct = copies
    for rc, fw, _ in forwards:
        rc.wait_recv()
        fw.start()
    for rc, fw, fw_in in forwards:
        rc.wait_send()
        fw.wait_send()
        fw_in.wait_recv()
    for rc in direct:
        rc.wait()


def _place_own_blocks(got, shards):
    my_k = 2 * lax.axis_index("x") + lax.axis_index("y")
    return [lax.dynamic_update_index_in_dim(g, a, my_k, 0) for g, a in zip(got, shards)]


def _norm1_fwd_gather(x, gain, shards):
    s = x.shape[0]
    tm = min(TILES["row"], s)
    ni = s // tm
    na = len(shards)

    def kfn(x_ref, g_ref, *rest):
        o_ref = rest[na]
        sems = rest[2 * na + 1:]
        i = pl.program_id(0)

        @pl.when(i == 0)
        def _():
            _gather_start(_gather_copies(rest[:na], rest[na + 1:2 * na + 1], *sems))

        xv = x_ref[...]
        r = lax.rsqrt(jnp.mean(xv * xv, axis=-1, keepdims=True) + NORM_EPS)
        o_ref[...] = (xv * r * g_ref[...]).astype(BF16)

        @pl.when(i == ni - 1)
        def _():
            _gather_finish(_gather_copies(rest[:na], rest[na + 1:2 * na + 1], *sems))

    sem = pltpu.SemaphoreType.DMA((na, 3))
    out = pl.pallas_call(
        kfn, name="norm1_fwd_gather", grid=(ni,),
        in_specs=[_rows(tm, D_MODEL), _full((1, D_MODEL))] + [ANY] * na,
        out_specs=[_rows(tm, D_MODEL)] + [ANY] * na,
        out_shape=[_sds((s, D_MODEL), BF16)] + [_sds((4,) + s_a.shape, s_a.dtype) for s_a in shards],
        scratch_shapes=[sem, sem, sem, sem],
        compiler_params=pltpu.CompilerParams(dimension_semantics=("arbitrary",), vmem_limit_bytes=VMEM_LIMIT_V7X,
                                             has_side_effects=True))(x, gain, *shards)
    return out[0], _place_own_blocks(list(out[1:]), shards)


def _matmul_nn_gather(a, b, shards, *, name, tm, tn, out_dtype):
    m, kdim = a.shape
    n = b.shape[1]
    ni, nj = m // tm, n // tn
    na = len(shards)

    def body(a_ref, b_ref, *rest):
        o_ref = rest[na]
        sems = rest[2 * na + 1:]
        i, j = pl.program_id(0), pl.program_id(1)

        @pl.when((i == 0) & (j == 0))
        def _():
            _gather_start(_gather_copies(rest[:na], rest[na + 1:2 * na + 1], *sems))

        o_ref[...] = _dot(a_ref[...], b_ref[...]).astype(out_dtype)

        @pl.when((i == ni - 1) & (j == nj - 1))
        def _():
            _gather_finish(_gather_copies(rest[:na], rest[na + 1:2 * na + 1], *sems))

    def kfn(*refs):
        body(*refs)
    sem = pltpu.SemaphoreType.DMA((na, 3))
    out = pl.pallas_call(
        kfn, name=name, grid=(ni, nj),
        in_specs=[pl.BlockSpec((tm, kdim), lambda i, j: (i, 0)), pl.BlockSpec((kdim, tn), lambda i, j: (0, j))] + [ANY] * na,
        out_specs=[pl.BlockSpec((tm, tn), lambda i, j: (i, j))] + [ANY] * na,
        out_shape=[_sds((m, n), out_dtype)] + [_sds((4,) + s_a.shape, s_a.dtype) for s_a in shards],
        scratch_shapes=[sem, sem, sem, sem],
        compiler_params=pltpu.CompilerParams(dimension_semantics=("arbitrary", "arbitrary"),
                                             vmem_limit_bytes=VMEM_LIMIT_V7X, has_side_effects=True))(a, b, *shards)
    return out[0], _place_own_blocks(list(out[1:]), shards)


def _all_reduce_small(vec):
    r = vec.shape[0]

    def body(v_ref, o_ref, gath, send_sems, recv_sems):
        x, y, c = lax.axis_index("x"), lax.axis_index("y"), lax.axis_index("c")
        me = 4 * x + 2 * y + c
        gath[me] = v_ref[...]
        copies = []
        for rel in range(1, 8):
            fx, fy, fc = (rel >> 2) & 1, (rel >> 1) & 1, rel & 1
            rc = pltpu.make_async_remote_copy(
                src_ref=v_ref, dst_ref=gath.at[me], send_sem=send_sems.at[rel - 1], recv_sem=recv_sems.at[rel - 1],
                device_id=(x ^ fx, y ^ fy, c ^ fc), device_id_type=MESH)
            rc.start()
            copies.append(rc)
        for rc in copies:
            rc.wait()
        acc = gath[0]
        for d in range(1, 8):
            acc = acc + gath[d]
        o_ref[...] = acc

    def kfn(*refs):
        body(*refs)
    vm = pl.BlockSpec(memory_space=pltpu.VMEM)
    return pl.pallas_call(kfn, name="all_reduce_small", in_specs=[vm], out_specs=vm, out_shape=_sds((r, 128), F32),
                          scratch_shapes=[pltpu.VMEM((8, r, 128), F32), pltpu.SemaphoreType.DMA((7,)),
                                          pltpu.SemaphoreType.DMA((7,))],
                          compiler_params=pltpu.CompilerParams(has_side_effects=True))(vec)


def _exchange_halves(arrs):
    na = len(arrs)

    def body(*refs):
        ins, outs = refs[:na], refs[na:2 * na]
        send_sems, recv_sems = refs[2 * na:]
        x, y, c = lax.axis_index("x"), lax.axis_index("y"), lax.axis_index("c")
        copies = []
        for a in range(na):
            half = ins[a].shape[1] // 2
            src = ins[a].at[:, pl.ds(pl.multiple_of((1 - c) * half, 8), half), :]
            rc = pltpu.make_async_remote_copy(src_ref=src, dst_ref=outs[a], send_sem=send_sems.at[a],
                                              recv_sem=recv_sems.at[a], device_id=(x, y, 1 - c), device_id_type=MESH)
            rc.start()
            copies.append(rc)
        for rc in copies:
            rc.wait()

    out_shape = [_sds((4, a.shape[1] // 2, a.shape[2]), F32) for a in arrs]
    return _comm_call(body, name="rs_pair_exchange", n_in=na, out_shape=out_shape,
                      scratch=[pltpu.SemaphoreType.DMA((na,)), pltpu.SemaphoreType.DMA((na,))])(*arrs)


def _scatter_to_chips(arrs):
    na = len(arrs)

    def body(*refs):
        ins, outs = refs[:na], refs[na:2 * na]
        send_sems, recv_sems = refs[2 * na:]
        x, y, c = lax.axis_index("x"), lax.axis_index("y"), lax.axis_index("c")
        copies = []
        for a in range(na):
            for r, (fx, fy) in enumerate(CHIP_FLIPS):
                px, py = x ^ fx, y ^ fy
                rc = pltpu.make_async_remote_copy(
                    src_ref=ins[a].at[2 * px + py], dst_ref=outs[a].at[r], send_sem=send_sems.at[a, r],
                    recv_sem=recv_sems.at[a, r], device_id=(px, py, c), device_id_type=MESH)
                rc.start()
                copies.append(rc)
        for rc in copies:
            rc.wait()

    out_shape = [_sds((3,) + a.shape[1:], a.dtype) for a in arrs]
    return _comm_call(body, name="rs_chip_scatter", n_in=na, out_shape=out_shape,
                      scratch=[pltpu.SemaphoreType.DMA((na, 3)), pltpu.SemaphoreType.DMA((na, 3))])(*arrs)


def _sum_into_half(parts, name):
    r2, c = parts[0].shape
    tr = _row_tile(r2, c)
    nb = r2 // tr
    n = len(parts)

    def kfn(c_ref, *refs):
        acc = refs[0][...].astype(F32)
        for p_ref in refs[1:n]:
            acc = acc + p_ref[...].astype(F32)
        refs[n][...] = acc

    spec = pltpu.PrefetchScalarGridSpec(
        num_scalar_prefetch=1, grid=(nb,), in_specs=[pl.BlockSpec((tr, c), lambda i, cr: (i, 0))] * n,
        out_specs=pl.BlockSpec((tr, c), lambda i, cr: (cr[0] * nb + i, 0)))
    core = lax.axis_index("c").astype(jnp.int32).reshape(1)
    return pl.pallas_call(kfn, name=name, grid_spec=spec, out_shape=_sds((2 * r2, c), F32),
                          compiler_params=pltpu.CompilerParams(dimension_semantics=("arbitrary",),
                                                               vmem_limit_bytes=VMEM_LIMIT_V7X))(core, *parts)


def _join_in_place(arrs):
    na = len(arrs)

    def body(*refs):
        outs = refs[na:2 * na]
        send_sems, recv_sems = refs[2 * na:]
        x, y, c = lax.axis_index("x"), lax.axis_index("y"), lax.axis_index("c")
        copies = []
        for a in range(na):
            half = outs[a].shape[0] // 2
            mine = outs[a].at[pl.ds(pl.multiple_of(c * half, 8), half), :]
            rc = pltpu.make_async_remote_copy(src_ref=mine, dst_ref=mine, send_sem=send_sems.at[a],
                                              recv_sem=recv_sems.at[a], device_id=(x, y, 1 - c), device_id_type=MESH)
            rc.start()
            copies.append(rc)
        for rc in copies:
            rc.wait()

    def kfn(*refs):
        body(*refs)
    return pl.pallas_call(kfn, name="rs_pair_join", in_specs=[ANY] * na, out_specs=[ANY] * na,
                          out_shape=[_sds(a.shape, F32) for a in arrs],
                          input_output_aliases={a: a for a in range(na)},
                          scratch_shapes=[pltpu.SemaphoreType.DMA((na,)), pltpu.SemaphoreType.DMA((na,))],
                          compiler_params=pltpu.CompilerParams(has_side_effects=True))(*arrs)


def _pair_sum(g, o, name):
    _, r, c = g.shape
    half = r // 2
    tr = _row_tile(half, c)
    nb = half // tr

    def kfn(c_ref, g_ref, o_ref, s32_ref, s16_ref):
        acc = g_ref[...] + o_ref[...]
        s32_ref[...] = acc
        s16_ref[...] = acc.astype(BF16)

    blk = lambda imap: pl.BlockSpec((None, tr, c), imap)
    same = lambda k, i, cr: (k, i, 0)
    spec = pltpu.PrefetchScalarGridSpec(
        num_scalar_prefetch=1, grid=(4, nb), in_specs=[blk(lambda k, i, cr: (k, cr[0] * nb + i, 0)), blk(same)],
        out_specs=[blk(same), blk(same)])
    core = lax.axis_index("c").astype(jnp.int32).reshape(1)
    return pl.pallas_call(kfn, name=name, grid_spec=spec, out_shape=[_sds((4, half, c), F32), _sds((4, half, c), BF16)],
                          compiler_params=pltpu.CompilerParams(dimension_semantics=("arbitrary", "arbitrary"),
                                                               vmem_limit_bytes=VMEM_LIMIT_V7X))(core, g, o)


def _rs_pair_stage(grads):
    got = _exchange_halves(grads)
    sums = [_pair_sum(g, o, f"rs_pair_sum_{a}") for a, (g, o) in enumerate(zip(grads, got))]
    return [s32 for s32, _ in sums], [s16 for _, s16 in sums]


def _rs_chip_stage(pair, recv):
    k_me = 2 * lax.axis_index("x") + lax.axis_index("y")
    halves = []
    for a, (p, rv) in enumerate(zip(pair, recv)):
        own = lax.dynamic_index_in_dim(p, k_me, 0, keepdims=False)
        halves.append(_sum_into_half([own, rv[0], rv[1], rv[2]], f"rs_chip_sum_{a}"))
    return _join_in_place(halves)


def _reduce_scatter(grads):
    pair, pair_bf16 = _rs_pair_stage(grads)
    return _rs_chip_stage(pair, _scatter_to_chips(pair_bf16))


def _matmul_nt_scatter(a, b, send, *, name, tm, tn, out_dtype):
    m, kdim = a.shape
    n = b.shape[0]
    ni, nj = m // tm, n // tn
    na = len(send)

    def body(a_ref, b_ref, *rest):
        send_refs, o_ref, recv_refs = rest[:na], rest[na], rest[na + 1:2 * na + 1]
        send_sems, recv_sems = rest[2 * na + 1:]
        i, j = pl.program_id(0), pl.program_id(1)

        def copies():
            x, y, c = lax.axis_index("x"), lax.axis_index("y"), lax.axis_index("c")
            out = []
            for s_i in range(na):
                for r, (fx, fy) in enumerate(CHIP_FLIPS):
                    px, py = x ^ fx, y ^ fy
                    out.append(pltpu.make_async_remote_copy(
                        src_ref=send_refs[s_i].at[2 * px + py], dst_ref=recv_refs[s_i].at[r],
                        send_sem=send_sems.at[s_i, r], recv_sem=recv_sems.at[s_i, r], device_id=(px, py, c),
                        device_id_type=MESH))
            return out

        @pl.when((i == 0) & (j == 0))
        def _():
            for cp in copies():
                cp.start()

        o_ref[...] = _dot(a_ref[...], b_ref[...], NT).astype(out_dtype)

        @pl.when((i == ni - 1) & (j == nj - 1))
        def _():
            for cp in copies():
                cp.wait()

    def kfn(*refs):
        body(*refs)
    sem = pltpu.SemaphoreType.DMA((na, 3))
    out = pl.pallas_call(
        kfn, name=name, grid=(ni, nj),
        in_specs=[pl.BlockSpec((tm, kdim), lambda i, j: (i, 0)), pl.BlockSpec((tn, kdim), lambda i, j: (j, 0))] + [ANY] * na,
        out_specs=[pl.BlockSpec((tm, tn), lambda i, j: (i, j))] + [ANY] * na,
        out_shape=[_sds((m, n), out_dtype)] + [_sds((3,) + s_a.shape[1:], s_a.dtype) for s_a in send],
        scratch_shapes=[sem, sem],
        compiler_params=pltpu.CompilerParams(dimension_semantics=("arbitrary", "arbitrary"),
                                             vmem_limit_bytes=VMEM_LIMIT_V7X, has_side_effects=True))(a, b, *send)
    return out[0], list(out[1:])


def _pad_w_in(w):
    z = jnp.zeros((w.shape[0], 1024 - 848), w.dtype)
    return jnp.concatenate([w[:, 0:832], w[:, 4928:4944], z, w[:, 832:4928], w[:, 4944:5968]], axis=1)


def _unpad_w_in(g):
    return jnp.concatenate([g[:, 0:832], g[:, 1024:5120], g[:, 832:848], g[:, 5120:6144]], axis=1)


W_IN_SHARD = W_IN_COLS // 4
W_IN_RUNS = ((0, 832, 0), (832, 4928, 1024), (4928, 4944, 832), (4944, 5968, 5120))


def _w_in_grad_blocks(p):
    def orig_cols(lo, hi):
        parts = [p[:, pa + max(lo, a) - a:pa + min(hi, b) - a] for a, b, pa in W_IN_RUNS if max(lo, a) < min(hi, b)]
        return parts[0] if len(parts) == 1 else jnp.concatenate(parts, axis=1)
    return jnp.stack([orig_cols(W_IN_SHARD * k, W_IN_SHARD * (k + 1)) for k in range(4)])


def _pad_w_in_blocks(g):
    def orig_cols(lo, hi):
        return [g[k][:, max(lo, W_IN_SHARD * k) - W_IN_SHARD * k:min(hi, W_IN_SHARD * (k + 1)) - W_IN_SHARD * k]
                for k in range(4) if max(lo, W_IN_SHARD * k) < min(hi, W_IN_SHARD * (k + 1))]
    z = jnp.zeros((g.shape[1], 1024 - 848), g.dtype)
    return jnp.concatenate(orig_cols(0, 832) + orig_cols(4928, 4944) + [z] + orig_cols(832, 4928) + orig_cols(4944, 5968),
                           axis=1)


def _pad_heads(w):
    r = w.shape[0]
    return jnp.pad(w.reshape(r, HEADS, QK_DIM), ((0, 0), (0, 0), (0, QK_PAD - QK_DIM))).reshape(r, HEADS * QK_PAD)


def _unpad_heads(w):
    r = w.shape[0]
    return w.reshape(r, HEADS, QK_PAD)[:, :, :QK_DIM].reshape(r, HEADS * QK_DIM)


def _cols_to_blocks(w):
    r = w.shape[0]
    return w.reshape(r, 4, -1).transpose(1, 0, 2)


def _blocks_to_cols(w):
    return w.transpose(1, 0, 2).reshape(w.shape[1], -1)


SMALL_ROWS = {"norm_gain": (0, 2048), "mla_q_a_gain": (16, 512), "mla_kv_a_gain": (20, 256),
              "mla_q_norm_gain": (22, 192), "mla_k_norm_gain": (24, 192), "gdn_a_log": (26, 8),
              "gdn_dt_bias": (27, 8), "gdn_out_norm_gain": (28, 128)}
LOSS_ROW = 29
SMALL_PACK_ROWS = 32
CONV_ROW = 32


def _pack_small(vals, loss=None):
    rows = []
    at = 0
    for name, (row, size) in SMALL_ROWS.items():
        assert row == at
        nr = -(-size // 128)
        rows.append(jnp.pad(vals[name].reshape(-1).astype(F32), (0, nr * 128 - size)).reshape(nr, 128))
        at += nr
    assert at == LOSS_ROW
    if loss is not None:
        rows.append(jnp.pad(loss.reshape(1, 1), ((0, 0), (0, 127))))
        at += 1
    rows.append(jnp.zeros((SMALL_PACK_ROWS - at, 128), F32))
    return jnp.concatenate(rows, axis=0)


def _unpack_small(pack, name):
    row, size = SMALL_ROWS[name]
    nr = -(-size // 128)
    return pack[row:row + nr].reshape(-1)[:size].reshape(1, size)


def _local_step(x, positions, target, norm_gain, w_in_p, q_a_gain, kv_a_gain, w_uq_p, w_ukv, q_norm_gain,
                k_norm_gain, conv_w, a_log, dt_bias, out_gain, w_out, scatter_hook=None, late_weights=None,
                first_weights=None):
    half = HALF_ROPE
    inv_freq = jnp.power(10000.0, -jnp.arange(half, dtype=F32) / half)
    ang = positions.astype(F32)[:, None] * inv_freq
    cos, sin = jnp.cos(ang), jnp.sin(ang)
    zpad = jnp.zeros((x.shape[0], 64), F32)
    cs = jnp.concatenate([cos, cos, zpad], axis=1)
    sn = jnp.concatenate([-sin, sin, zpad], axis=1)
    gq = jnp.pad(q_norm_gain.reshape(1, QK_DIM), ((0, 0), (0, QK_PAD - QK_DIM)))
    gk = jnp.pad(k_norm_gain.reshape(1, QK_DIM), ((0, 0), (0, QK_PAD - QK_DIM)))
    lane_pad = ((0, 0), (GA_LANE, 128 - GA_LANE - HEADS))
    alog128 = jnp.pad(a_log.reshape(1, HEADS), lane_pad)
    dtb128 = jnp.pad(dt_bias.reshape(1, HEADS), lane_pad)
    ng, qag, kvag, og = (norm_gain.reshape(1, -1), q_a_gain.reshape(1, -1), kv_a_gain.reshape(1, -1),
                         out_gain.reshape(1, -1))

    if first_weights is None:
        xn = _norm1_fwd(x, ng)
    else:
        shards, assemble = first_weights
        xn, gathered = _norm1_fwd_gather(x, ng, shards)
        w_in_p, conv_w = assemble(gathered)
    misc = _matmul(xn, w_in_p[:, 768:896], mode="nn", out_dtype=F32, name="in_proj_misc")
    if late_weights is None:
        proj = _matmul(xn, w_in_p, mode="nn", out_dtype=BF16, name="in_proj")
    else:
        shards, assemble = late_weights
        proj, gathered = _matmul_nn_gather(xn, w_in_p, shards, name="in_proj_gather", tm=TILES["mm"], tn=TILES["mm"],
                                           out_dtype=BF16)
        w_uq_p, w_ukv, w_out = assemble(gathered)
    cqn, ckvn = _mla_a_norm(proj, qag, kvag)
    q_pre = _matmul(cqn, w_uq_p, mode="nn", out_dtype=BF16, name="q_up")
    kv_pre = _matmul(ckvn, w_ukv, mode="nn", out_dtype=BF16, name="kv_up")
    q, k, v = _mla_post_fwd(q_pre, kv_pre, misc, cs, sn, gq, gk)
    o_mla, lse = _attn_fwd(q, k, v)
    qkv = _gdn_conv_fwd(proj, conv_w)
    gbeta = _gdn_gates_fwd(misc, alog128, dtb128)
    g_u, g_w, g_qd, g_kd, g_a, g_t, g_gl = _gdn_pre(qkv, gbeta)
    o_gdn, states = _gdn_scan_fwd(g_u, g_w, g_qd, g_kd, g_a, g_gl)
    mixed = _mix_fwd(o_mla, o_gdn, proj, og)
    dy, sq = _out_fwd(mixed, w_out, x, target)

    dmixed = _matmul(dy, w_out, mode="nt", out_dtype=BF16, name="d_mixed")
    d_w_out = _matmul(mixed, dy, mode="tn", out_dtype=F32, name="d_w_out", tk=4096)
    do_mla, do_gdn, dmg, dgg, d_out_gain = _mix_bwd(o_mla, o_gdn, proj, og, dmixed)
    dq, dk, dv = _attn_bwd(q, k, v, o_mla, lse, do_mla)
    dq_pre, dkv_pre, dkr, d_gq, d_gk = _mla_post_bwd(q_pre, kv_pre, misc, cs, sn, gq, gk, dq, dk, dv)
    d_w_uq_p = _matmul(cqn, dq_pre, mode="tn", out_dtype=F32, name="d_w_uq", tk=1024)
    d_w_ukv = _matmul(ckvn, dkv_pre, mode="tn", out_dtype=F32, name="d_w_ukv", tk=1024)
    dcqn = _matmul(dq_pre, w_uq_p, mode="nt", out_dtype=F32, name="d_cqn")
    dckvn = _matmul(dkv_pre, w_ukv, mode="nt", out_dtype=F32, name="d_ckvn")
    dcq, dckv, d_qag, d_kvag = _mla_a_norm_bwd(proj, qag, kvag, dcqn, dckvn)
    dstates = _gdn_scan_bwd(g_w, g_qd, g_kd, g_a, g_gl, do_gdn)
    dqkv, dgbeta = _gdn_post_bwd(qkv, gbeta, g_u, g_w, g_t, states, dstates, do_gdn)
    dz = _gdn_conv_bwd_a(proj, conv_w, dqkv)
    dgx, d_conv = _gdn_conv_bwd_b(proj, conv_w, dz)
    dmisc, d_alog, d_dtb = _gdn_gates_bwd(misc, alog128, dtb128, gbeta, dgbeta, dkr)
    dproj = jnp.concatenate([dcq, dckv, dmisc, jnp.zeros((x.shape[0], 128), BF16), dmg, dgx, dgg], axis=1)
    d_w_in_p = _matmul(xn, dproj, mode="tn", out_dtype=F32, name="d_w_in", tk=4096)
    big = {"w_in": d_w_in_p, "w_uq": d_w_uq_p, "w_ukv": d_w_ukv, "w_out": d_w_out, "gdn_conv_w": d_conv}
    if scatter_hook is None:
        dxn, received = _matmul(dproj, w_in_p, mode="nt", out_dtype=BF16, name="d_xn", tm=512, tn=512), None
    else:
        dxn, received = _matmul_nt_scatter(dproj, w_in_p, scatter_hook(big), name="d_xn_scatter", tm=512, tn=512,
                                           out_dtype=BF16)
    grad_x, d_ng = _norm1_bwd(x, ng, dxn, dy)

    small = {"norm_gain": d_ng.sum(0), "mla_q_a_gain": d_qag.sum(0), "mla_kv_a_gain": d_kvag.sum(0),
             "mla_q_norm_gain": d_gq.sum(0)[:QK_DIM], "mla_k_norm_gain": d_gk.sum(0)[:QK_DIM],
             "gdn_a_log": d_alog.sum(0)[GA_LANE:GA_LANE + HEADS], "gdn_dt_bias": d_dtb.sum(0)[GA_LANE:GA_LANE + HEADS],
             "gdn_out_norm_gain": d_out_gain.sum(0)}
    return sq, grad_x, small, big, received


WEIGHTS = ["norm_gain", "w_in", "mla_q_a_gain", "mla_kv_a_gain", "w_uq", "w_ukv", "mla_q_norm_gain", "mla_k_norm_gain",
           "gdn_conv_w", "gdn_a_log", "gdn_dt_bias", "gdn_out_norm_gain", "w_out"]
BIG = ["w_in", "w_uq", "w_ukv", "w_out"]


def kernel(x, positions, norm_gain, w_in, mla_q_a_gain, mla_kv_a_gain, w_uq, w_ukv, mla_q_norm_gain, mla_k_norm_gain, gdn_conv_w, gdn_a_log, gdn_dt_bias, gdn_out_norm_gain, w_out, loss_target, m_norm_gain, m_w_in, m_mla_q_a_gain, m_mla_kv_a_gain, m_w_uq, m_w_ukv, m_mla_q_norm_gain, m_mla_k_norm_gain, m_gdn_conv_w, m_gdn_a_log, m_gdn_dt_bias, m_gdn_out_norm_gain, m_w_out, v_norm_gain, v_w_in, v_mla_q_a_gain, v_mla_kv_a_gain, v_w_uq, v_w_ukv, v_mla_q_norm_gain, v_mla_k_norm_gain, v_gdn_conv_w, v_gdn_a_log, v_gdn_dt_bias, v_gdn_out_norm_gain, v_w_out):
    w = dict(norm_gain=norm_gain, w_in=w_in, mla_q_a_gain=mla_q_a_gain, mla_kv_a_gain=mla_kv_a_gain, w_uq=w_uq,
             w_ukv=w_ukv, mla_q_norm_gain=mla_q_norm_gain, mla_k_norm_gain=mla_k_norm_gain, gdn_conv_w=gdn_conv_w,
             gdn_a_log=gdn_a_log, gdn_dt_bias=gdn_dt_bias, gdn_out_norm_gain=gdn_out_norm_gain, w_out=w_out)
    m = dict(norm_gain=m_norm_gain, w_in=m_w_in, mla_q_a_gain=m_mla_q_a_gain, mla_kv_a_gain=m_mla_kv_a_gain,
             w_uq=m_w_uq, w_ukv=m_w_ukv, mla_q_norm_gain=m_mla_q_norm_gain, mla_k_norm_gain=m_mla_k_norm_gain,
             gdn_conv_w=m_gdn_conv_w, gdn_a_log=m_gdn_a_log, gdn_dt_bias=m_gdn_dt_bias,
             gdn_out_norm_gain=m_gdn_out_norm_gain, w_out=m_w_out)
    v = dict(norm_gain=v_norm_gain, w_in=v_w_in, mla_q_a_gain=v_mla_q_a_gain, mla_kv_a_gain=v_mla_kv_a_gain,
             w_uq=v_w_uq, w_ukv=v_w_ukv, mla_q_norm_gain=v_mla_q_norm_gain, mla_k_norm_gain=v_mla_k_norm_gain,
             gdn_conv_w=v_gdn_conv_w, gdn_a_log=v_gdn_a_log, gdn_dt_bias=v_gdn_dt_bias,
             gdn_out_norm_gain=v_gdn_out_norm_gain, w_out=v_w_out)
    k_me = 2 * lax.axis_index("x") + lax.axis_index("y")

    first_weights = ([w_in[0].astype(BF16), gdn_conv_w[0]], lambda g: (_pad_w_in_blocks(g[0]), _blocks_to_cols(g[1])))
    late_weights = ([w_uq[0].astype(BF16), w_ukv[0].astype(BF16), w_out[0].astype(BF16)],
                    lambda g: (_pad_heads(_blocks_to_cols(g[0])), _blocks_to_cols(g[1]), g[2].reshape(D_MODEL, D_MODEL)))

    pair_sums = []

    def scatter_hook(big):
        pair, pair_bf16 = _rs_pair_stage([
            _w_in_grad_blocks(big["w_in"]), _cols_to_blocks(_unpad_heads(big["w_uq"])),
            _cols_to_blocks(big["w_ukv"]), big["w_out"].reshape(4, 512, D_MODEL)])
        pair_sums.extend(pair)
        return pair_bf16

    sq, grad_x, small, big, received = _local_step(
        x[0], positions[0], loss_target[0], norm_gain, None, mla_q_a_gain, mla_kv_a_gain, None, None,
        mla_q_norm_gain, mla_k_norm_gain, None, gdn_a_log, gdn_dt_bias, gdn_out_norm_gain, None, scatter_hook,
        late_weights, first_weights)

    loss_local = (0.5 / D_MODEL) * jnp.sum(sq)
    pack = jnp.concatenate([_pack_small(small, loss_local), big["gdn_conv_w"].reshape(96, 128)], axis=0)
    tot = _all_reduce_small(pack)
    loss = tot[LOSS_ROW, 0]
    conv_grad = lax.dynamic_slice_in_dim(tot[CONV_ROW:].reshape(4, 3072), k_me * 768, 768, axis=1)

    shard_grads = _rs_chip_stage(pair_sums, received)

    grads = {n: _unpack_small(tot, n) for n in SMALL_ROWS}
    grads["gdn_conv_w"] = conv_grad[None]
    for n, g in zip(BIG, shard_grads):
        grads[n] = g[None]

    delta, new_m, new_v = {}, {}, {}
    sw = _pack_small({n: w[n] for n in SMALL_ROWS})
    sm = _pack_small({n: m[n] for n in SMALL_ROWS})
    sv = _pack_small({n: v[n] for n in SMALL_ROWS})
    sd, snm, snv = _adamw(sw, tot[:SMALL_PACK_ROWS], sm, sv, "adamw_small")
    for n in SMALL_ROWS:
        delta[n], new_m[n], new_v[n] = _unpack_small(sd, n), _unpack_small(snm, n), _unpack_small(snv, n)
    for n in BIG + ["gdn_conv_w"]:
        if n == "w_in":
            d, nm, nv = _adamw(w[n][0].T, grads[n][0].T, m[n][0].T, v[n][0].T, f"adamw_{n}")
            delta[n], new_m[n], new_v[n] = d.T[None], nm.T[None], nv.T[None]
        else:
            d, nm, nv = _adamw(w[n][0], grads[n][0], m[n][0], v[n][0], f"adamw_{n}")
            delta[n], new_m[n], new_v[n] = d[None], nm[None], nv[None]

    return (loss, grad_x[None], *[grads[n] for n in WEIGHTS], *[delta[n] for n in WEIGHTS],
            *[new_m[n] for n in WEIGHTS], *[new_v[n] for n in WEIGHTS])
```

```python
import functools
import math

import jax
import jax.numpy as jnp
from jax import lax
from jax.experimental import pallas as pl
from jax.experimental.pallas import tpu as pltpu

F32 = jnp.float32
BF16 = jnp.bfloat16
MESH = pl.DeviceIdType.MESH

D_MODEL = 2048
HEADS = 8
HEAD_DIM = 128
QK_DIM = 192
QK_PAD = 256
HALF_ROPE = 32
CHUNK = 64
NORM_EPS = 1e-6
W_IN_COLS = 5968
W_IN_PAD = 6144
GA_LANE = 64
GB_LANE = 72
ADAM_LR, ADAM_B1, ADAM_B2, ADAM_EPS, ADAM_WD, ADAM_STEP = 0.001, 0.9, 0.999, 1e-08, 0.01, 10
VMEM_LIMIT_V7X = 52 * 1024 * 1024
HI = lax.Precision.HIGHEST
NN = (((1,), (0,)), ((), ()))
NT = (((1,), (1,)), ((), ()))
TN = (((0,), (0,)), ((), ()))

TILES = {"row": 512, "attn": 1024, "mm": 1024}


def _call(body, *, name, grid, in_specs, out_specs, out_shape, args, scratch=(), sem=None):
    def kfn(*refs):
        body(*refs)
    if sem is None:
        sem = ("arbitrary",) * len(grid)
    return pl.pallas_call(
        kfn, name=name, grid=grid, in_specs=in_specs, out_specs=out_specs, out_shape=out_shape,
        scratch_shapes=list(scratch),
        compiler_params=pltpu.CompilerParams(dimension_semantics=sem, vmem_limit_bytes=VMEM_LIMIT_V7X),
    )(*args)


def _rows(tm, w, cb=0):
    return pl.BlockSpec((tm, w), lambda i: (i, cb))


def _full(shape):
    n = len(shape)
    return pl.BlockSpec(shape, lambda *_: (0,) * n)


def _sds(shape, dtype):
    return jax.ShapeDtypeStruct(shape, dtype)


def _acc8(x):
    tm, c = x.shape
    return jnp.sum(x.reshape(tm // 8, 8, c), axis=0)


def _sigmoid(x):
    return 1.0 / (1.0 + jnp.exp(-x))


def _silu(x):
    return x * _sigmoid(x)


def _dsilu(x):
    s = _sigmoid(x)
    return s * (1.0 + x * (1.0 - s))


def _dot(a, b, dims=NN):
    return lax.dot_general(a.astype(BF16), b.astype(BF16), dims, preferred_element_type=F32)


def _dot_hi(a, b, dims=NN):
    return lax.dot_general(a, b, dims, precision=HI, preferred_element_type=F32)


def _matmul(a, b, *, mode, out_dtype, name, tm=None, tn=None, tk=None):
    if mode == "tn":
        kdim, m = a.shape
    else:
        m, kdim = a.shape
    n = b.shape[0] if mode == "nt" else b.shape[1]
    tm = min(tm or TILES["mm"], m)
    tn = min(tn or TILES["mm"], n)
    tk = min(tk or kdim, kdim)
    nk = kdim // tk
    dims = {"nn": NN, "nt": NT, "tn": TN}[mode]
    if mode == "tn":
        a_spec = pl.BlockSpec((tk, tm), lambda i, j, k: (k, i))
    else:
        a_spec = pl.BlockSpec((tm, tk), lambda i, j, k: (i, k))
    if mode == "nt":
        b_spec = pl.BlockSpec((tn, tk), lambda i, j, k: (j, k))
    else:
        b_spec = pl.BlockSpec((tk, tn), lambda i, j, k: (k, j))

    def body(a_ref, b_ref, o_ref):
        r = _dot(a_ref[...], b_ref[...], dims)
        if nk == 1:
            o_ref[...] = r.astype(o_ref.dtype)
        else:
            k = pl.program_id(2)

            @pl.when(k == 0)
            def _():
                o_ref[...] = r

            @pl.when(k > 0)
            def _():
                o_ref[...] += r

    if nk > 1:
        assert out_dtype == F32
    return _call(body, name=name, grid=(m // tm, n // tn, nk), in_specs=[a_spec, b_spec],
                 out_specs=pl.BlockSpec((tm, tn), lambda i, j, k: (i, j)), out_shape=_sds((m, n), out_dtype),
                 args=(a, b))


def _norm1_fwd(x, gain):
    s = x.shape[0]
    tm = min(TILES["row"], s)

    def body(x_ref, g_ref, o_ref):
        xv = x_ref[...]
        r = lax.rsqrt(jnp.mean(xv * xv, axis=-1, keepdims=True) + NORM_EPS)
        o_ref[...] = (xv * r * g_ref[...]).astype(BF16)

    return _call(body, name="norm1_fwd", grid=(s // tm,), in_specs=[_rows(tm, D_MODEL), _full((1, D_MODEL))],
                 out_specs=_rows(tm, D_MODEL), out_shape=_sds((s, D_MODEL), BF16), args=(x, gain))


def _norm1_bwd(x, gain, dxn, dy):
    s = x.shape[0]
    tm = min(TILES["row"], s)

    def body(x_ref, g_ref, dxn_ref, dy_ref, gx_ref, dg_ref):
        xv = x_ref[...]
        r = lax.rsqrt(jnp.mean(xv * xv, axis=-1, keepdims=True) + NORM_EPS)
        nrm = xv * r
        d = dxn_ref[...].astype(F32)
        dn = d * g_ref[...]
        gx_ref[...] = dy_ref[...].astype(F32) + r * (dn - nrm * jnp.mean(dn * nrm, axis=-1, keepdims=True))

        @pl.when(pl.program_id(0) == 0)
        def _():
            dg_ref[...] = jnp.zeros_like(dg_ref)

        dg_ref[...] += _acc8(d * nrm)

    return _call(body, name="norm1_bwd", grid=(s // tm,),
                 in_specs=[_rows(tm, D_MODEL), _full((1, D_MODEL)), _rows(tm, D_MODEL), _rows(tm, D_MODEL)],
                 out_specs=[_rows(tm, D_MODEL), _full((8, D_MODEL))],
                 out_shape=[_sds((s, D_MODEL), F32), _sds((8, D_MODEL), F32)], args=(x, gain, dxn, dy))


def _rms(xv, width):
    return lax.rsqrt(jnp.sum(xv * xv, axis=-1, keepdims=True) * (1.0 / width) + NORM_EPS)


def _mla_a_norm(proj, gq, gkv):
    s = proj.shape[0]
    tm = min(TILES["row"], s)

    def body(cq_ref, ckv_ref, gq_ref, gkv_ref, oq_ref, okv_ref):
        a = cq_ref[...].astype(F32)
        oq_ref[...] = (a * _rms(a, 512) * gq_ref[...]).astype(BF16)
        b = ckv_ref[...].astype(F32)
        okv_ref[...] = (b * _rms(b, 256) * gkv_ref[...]).astype(BF16)

    return _call(body, name="mla_a_norm", grid=(s // tm,),
                 in_specs=[_rows(tm, 512, 0), _rows(tm, 256, 2), _full((1, 512)), _full((1, 256))],
                 out_specs=[_rows(tm, 512), _rows(tm, 256)],
                 out_shape=[_sds((s, 512), BF16), _sds((s, 256), BF16)], args=(proj, proj, gq, gkv))


def _rms_bwd(xv, gain, d, width):
    r = _rms(xv, width)
    nrm = xv * r
    dn = d * gain
    dx = r * (dn - nrm * (jnp.sum(dn * nrm, axis=-1, keepdims=True) * (1.0 / width)))
    return dx, d * nrm


def _mla_a_norm_bwd(proj, gq, gkv, dcqn, dckvn):
    s = proj.shape[0]
    tm = min(TILES["row"], s)

    def body(cq_ref, ckv_ref, gq_ref, gkv_ref, dq_ref, dkv_ref, oq_ref, okv_ref, aq_ref, akv_ref):
        dxq, gq_part = _rms_bwd(cq_ref[...].astype(F32), gq_ref[...], dq_ref[...].astype(F32), 512)
        dxk, gk_part = _rms_bwd(ckv_ref[...].astype(F32), gkv_ref[...], dkv_ref[...].astype(F32), 256)
        oq_ref[...] = dxq.astype(BF16)
        okv_ref[...] = dxk.astype(BF16)

        @pl.when(pl.program_id(0) == 0)
        def _():
            aq_ref[...] = jnp.zeros_like(aq_ref)
            akv_ref[...] = jnp.zeros_like(akv_ref)

        aq_ref[...] += _acc8(gq_part)
        akv_ref[...] += _acc8(gk_part)

    return _call(body, name="mla_a_norm_bwd", grid=(s // tm,),
                 in_specs=[_rows(tm, 512, 0), _rows(tm, 256, 2), _full((1, 512)), _full((1, 256)),
                           _rows(tm, 512), _rows(tm, 256)],
                 out_specs=[_rows(tm, 512), _rows(tm, 256), _full((8, 512)), _full((8, 256))],
                 out_shape=[_sds((s, 512), BF16), _sds((s, 256), BF16), _sds((8, 512), F32), _sds((8, 256), F32)],
                 args=(proj, proj, gq, gkv, dcqn, dckvn))


def _swap32(r):
    lane = lax.broadcasted_iota(jnp.int32, r.shape, 1)
    return jnp.where(lane < HALF_ROPE, pltpu.roll(r, 128 - HALF_ROPE, 1), pltpu.roll(r, HALF_ROPE, 1))


def _mla_post_fwd(q_pre, kv_pre, proj, cs, sn, gq, gk):
    s = q_pre.shape[0]
    tm = min(TILES["row"], s)

    def body(qp_ref, kvp_ref, misc_ref, cs_ref, sn_ref, gq_ref, gk_ref, q_ref, k_ref, v_ref):
        csv, snv = cs_ref[...], sn_ref[...]
        lane = lax.broadcasted_iota(jnp.int32, (tm, 128), 1)
        kr = jnp.where(lane < 64, misc_ref[...], 0.0)
        for h in range(HEADS):
            for src, g_ref, o_ref in ((None, gq_ref, q_ref), (kr, gk_ref, k_ref)):
                if src is None:
                    xv = qp_ref[:, QK_PAD * h:QK_PAD * (h + 1)].astype(F32)
                else:
                    xv = jnp.concatenate([kvp_ref[:, 256 * h:256 * h + 128].astype(F32), src], axis=-1)
                y = xv * _rms(xv, QK_DIM) * g_ref[...]
                if src is None:
                    y = y * Q_PRESCALE
                hi = y[:, 128:]
                hi = hi * csv + _swap32(hi) * snv
                o_ref[:, QK_PAD * h:QK_PAD * h + 128] = y[:, :128].astype(BF16)
                o_ref[:, QK_PAD * h + 128:QK_PAD * (h + 1)] = hi.astype(BF16)
            v_ref[:, 128 * h:128 * (h + 1)] = kvp_ref[:, 256 * h + 128:256 * (h + 1)].astype(BF16)

    return _call(body, name="mla_post_fwd", grid=(s // tm,),
                 in_specs=[_rows(tm, 2048), _rows(tm, 2048), _rows(tm, 128), _rows(tm, 128), _rows(tm, 128),
                           _full((1, QK_PAD)), _full((1, QK_PAD))],
                 out_specs=[_rows(tm, 2048), _rows(tm, 2048), _rows(tm, 1024)],
                 out_shape=[_sds((s, 2048), BF16), _sds((s, 2048), BF16), _sds((s, 1024), BF16)],
                 args=(q_pre, kv_pre, proj, cs, sn, gq, gk))


def _mla_post_bwd(q_pre, kv_pre, proj, cs, sn, gq, gk, dq, dk, dv):
    s = q_pre.shape[0]
    tm = min(TILES["row"], s)

    def body(qp_ref, kvp_ref, misc_ref, cs_ref, sn_ref, gq_ref, gk_ref, dq_ref, dk_ref, dv_ref,
             oq_ref, okv_ref, okr_ref, agq_ref, agk_ref):
        csv, snv = cs_ref[...], sn_ref[...]
        lane = lax.broadcasted_iota(jnp.int32, (tm, 128), 1)
        kr = jnp.where(lane < 64, misc_ref[...], 0.0)

        @pl.when(pl.program_id(0) == 0)
        def _():
            agq_ref[...] = jnp.zeros_like(agq_ref)
            agk_ref[...] = jnp.zeros_like(agk_ref)

        dkr = jnp.zeros((tm, 128), F32)
        for h in range(HEADS):
            for which in (0, 1):
                if which == 0:
                    xv = qp_ref[:, QK_PAD * h:QK_PAD * (h + 1)].astype(F32)
                    d_ref, g_ref, a_ref = dq_ref, gq_ref, agq_ref
                else:
                    xv = jnp.concatenate([kvp_ref[:, 256 * h:256 * h + 128].astype(F32), kr], axis=-1)
                    d_ref, g_ref, a_ref = dk_ref, gk_ref, agk_ref
                dhi = d_ref[:, QK_PAD * h + 128:QK_PAD * (h + 1)]
                dhi = dhi * csv - _swap32(dhi) * snv
                dyv = jnp.concatenate([d_ref[:, QK_PAD * h:QK_PAD * h + 128], dhi], axis=-1)
                if which == 0:
                    dyv = dyv * ATTN_SCALE
                dx, gpart = _rms_bwd(xv, g_ref[...], dyv, QK_DIM)
                a_ref[...] += _acc8(gpart)
                if which == 0:
                    oq_ref[:, QK_PAD * h:QK_PAD * (h + 1)] = dx.astype(BF16)
                else:
                    okv_ref[:, 256 * h:256 * h + 128] = dx[:, :128].astype(BF16)
                    dkr = dkr + dx[:, 128:]
            okv_ref[:, 256 * h + 128:256 * (h + 1)] = dv_ref[:, 128 * h:128 * (h + 1)].astype(BF16)
        okr_ref[...] = dkr

    return _call(body, name="mla_post_bwd", grid=(s // tm,),
                 in_specs=[_rows(tm, 2048), _rows(tm, 2048), _rows(tm, 128), _rows(tm, 128), _rows(tm, 128),
                           _full((1, QK_PAD)), _full((1, QK_PAD)), _rows(tm, 2048), _rows(tm, 2048), _rows(tm, 1024)],
                 out_specs=[_rows(tm, 2048), _rows(tm, 2048), _rows(tm, 128), _full((8, QK_PAD)), _full((8, QK_PAD))],
                 out_shape=[_sds((s, 2048), BF16), _sds((s, 2048), BF16), _sds((s, 128), F32),
                            _sds((8, QK_PAD), F32), _sds((8, QK_PAD), F32)],
                 args=(q_pre, kv_pre, proj, cs, sn, gq, gk, dq, dk, dv))


ATTN_SCALE = QK_DIM ** -0.5
NEG = -1e30


LOG2E = 1.4426950408889634
LN2 = 0.6931471805599453
Q_PRESCALE = ATTN_SCALE * LOG2E
ATTN_SUB_FWD = 256
ATTN_SUB_BWD = 256


def _causal_pairs(nq, kv_major):
    prs = [(i, j) for i in range(nq) for j in range(i + 1)]
    if kv_major:
        prs.sort(key=lambda ij: (ij[1], ij[0]))
    return (jnp.asarray([p[0] for p in prs], jnp.int32), jnp.asarray([p[1] for p in prs], jnp.int32))


def _pair_call(body, *, name, tables, in_specs, out_specs, out_shape, scratch, args):
    def kfn(*refs):
        body(*refs)
    spec = pltpu.PrefetchScalarGridSpec(num_scalar_prefetch=2, grid=(HEADS, tables[0].shape[0]), in_specs=in_specs,
                                        out_specs=out_specs, scratch_shapes=list(scratch))
    return pl.pallas_call(
        kfn, name=name, grid_spec=spec, out_shape=out_shape,
        compiler_params=pltpu.CompilerParams(dimension_semantics=("parallel", "arbitrary"),
                                             vmem_limit_bytes=VMEM_LIMIT_V7X))(*tables, *args)


def _diag_mask(sc, ts, qs):
    row = lax.broadcasted_iota(jnp.int32, sc.shape, 0) + qs * ts
    col = lax.broadcasted_iota(jnp.int32, sc.shape, 1)
    return jnp.where(col <= row, sc, NEG)


def _attn_fwd(q, k, v):
    s = q.shape[0]
    t = min(TILES["attn"], s)
    ts = min(ATTN_SUB_FWD, t)
    nq = s // t

    def slabs(q_ref, k_ref, v_ref, m_s, l_s, acc_s, diag):
        def scores(qs):
            kw = (qs + 1) * ts if diag else t
            sc = lax.dot_general(q_ref[qs * ts:(qs + 1) * ts, :], k_ref[0:kw, :], NT, preferred_element_type=F32)
            return _diag_mask(sc, ts, qs) if diag else sc

        nsub = t // ts
        sc_next = scores(0)
        for qs in range(nsub):
            rq = slice(qs * ts, (qs + 1) * ts)
            kw = (qs + 1) * ts if diag else t
            sc = sc_next
            if qs + 1 < nsub:
                sc_next = scores(qs + 1)
            m_prev = m_s[rq, :]
            m_new = jnp.maximum(m_prev, jnp.max(sc, axis=-1, keepdims=True))
            p = jnp.exp2(sc - m_new)
            alpha = jnp.exp2(m_prev - m_new)
            l_s[rq, :] = alpha * l_s[rq, :] + jnp.sum(p, axis=-1, keepdims=True)
            acc_s[rq, :] = acc_s[rq, :] * alpha + lax.dot_general(p.astype(BF16), v_ref[0:kw, :], NN,
                                                                  preferred_element_type=F32)
            m_s[rq, :] = m_new

    def body(it_ref, jt_ref, q_ref, k_ref, v_ref, o_ref, lse_ref, m_s, l_s, acc_s):
        p = pl.program_id(1)
        i, j = it_ref[p], jt_ref[p]

        @pl.when(j == 0)
        def _():
            m_s[...] = jnp.full_like(m_s, NEG)
            l_s[...] = jnp.zeros_like(l_s)
            acc_s[...] = jnp.zeros_like(acc_s)

        @pl.when(j < i)
        def _():
            slabs(q_ref, k_ref, v_ref, m_s, l_s, acc_s, False)

        @pl.when(j == i)
        def _():
            slabs(q_ref, k_ref, v_ref, m_s, l_s, acc_s, True)
            o_ref[...] = acc_s[...] / l_s[...]
            lse_ref[...] = m_s[...] + jnp.log2(l_s[...])

    qb = lambda h, p, it, jt: (it[p], h)
    kb = lambda h, p, it, jt: (jt[p], h)
    return _pair_call(
        body, name="attn_fwd", tables=_causal_pairs(nq, kv_major=False),
        in_specs=[pl.BlockSpec((t, QK_PAD), qb), pl.BlockSpec((t, QK_PAD), kb), pl.BlockSpec((t, HEAD_DIM), kb)],
        out_specs=[pl.BlockSpec((t, HEAD_DIM), qb),
                   pl.BlockSpec((None, t, 1), lambda h, p, it, jt: (h, it[p], 0))],
        out_shape=[_sds((s, HEADS * HEAD_DIM), F32), _sds((HEADS, s, 1), F32)],
        scratch=[pltpu.VMEM((t, 1), F32), pltpu.VMEM((t, 1), F32), pltpu.VMEM((t, HEAD_DIM), F32)],
        args=(q, k, v))


def _attn_bwd(q, k, v, lse_rows, delta_rows, do):
    s = q.shape[0]
    t = min(TILES["attn"], s)
    ts = min(ATTN_SUB_BWD, t)
    nq = s // t

    def slabs(q_ref, k_ref, v_ref, lse_ref, delta_ref, do_ref, dq_ref, dk_ref, dv_ref, i, diag):
        def products(qs):
            rq = slice(qs * ts, (qs + 1) * ts)
            kw = (qs + 1) * ts if diag else t
            qv, dob = q_ref[rq, :], do_ref[rq, :]
            sct = lax.dot_general(k_ref[0:kw, :], qv, NT, preferred_element_type=F32)
            dpt = lax.dot_general(v_ref[0:kw, :], dob, NT, preferred_element_type=F32)
            if diag:
                row = lax.broadcasted_iota(jnp.int32, sct.shape, 0)
                col = lax.broadcasted_iota(jnp.int32, sct.shape, 1) + qs * ts
                sct = jnp.where(row <= col, sct, NEG)
            return qv, dob, sct, dpt

        nsub = t // ts
        ahead = products(0)
        for qs in range(nsub):
            rq = slice(qs * ts, (qs + 1) * ts)
            kw = (qs + 1) * ts if diag else t
            qv, dob, sct, dpt = ahead
            if qs + 1 < nsub:
                ahead = products(qs + 1)
            pt = jnp.exp2(sct - lse_ref[:, rq])
            dv_ref[0:kw, :] += lax.dot_general(pt.astype(BF16), dob, NN, preferred_element_type=F32)
            dst = (pt * (dpt - delta_ref[:, rq])).astype(BF16)
            dk_ref[0:kw, :] += lax.dot_general(dst, qv, NN, preferred_element_type=F32)
            rows = pl.ds(pl.multiple_of(i * t + qs * ts, ts), ts)
            dq_ref[rows, :] += lax.dot_general(dst, k_ref[0:kw, :], TN, preferred_element_type=F32)

    def body(it_ref, jt_ref, q_ref, k_ref, v_ref, lse_ref, delta_ref, do_ref, dq_ref, dk_ref, dv_ref):
        p = pl.program_id(1)
        i, j = it_ref[p], jt_ref[p]
        refs = (q_ref, k_ref, v_ref, lse_ref, delta_ref, do_ref, dq_ref, dk_ref, dv_ref)

        @pl.when(p == 0)
        def _():
            dq_ref[...] = jnp.zeros_like(dq_ref)

        @pl.when(i == j)
        def _():
            dk_ref[...] = jnp.zeros_like(dk_ref)
            dv_ref[...] = jnp.zeros_like(dv_ref)

        @pl.when(i > j)
        def _():
            slabs(*refs, i, False)

        @pl.when(i == j)
        def _():
            slabs(*refs, i, True)

        @pl.when(i == nq - 1)
        def _():
            dk_ref[...] = dk_ref[...] * LN2

    qb = lambda h, p, it, jt: (it[p], h)
    kb = lambda h, p, it, jt: (jt[p], h)
    rowb = pl.BlockSpec((None, 1, t), lambda h, p, it, jt: (h, 0, it[p]))
    return _pair_call(
        body, name="attn_bwd", tables=_causal_pairs(nq, kv_major=True),
        in_specs=[pl.BlockSpec((t, QK_PAD), qb), pl.BlockSpec((t, QK_PAD), kb), pl.BlockSpec((t, HEAD_DIM), kb),
                  rowb, rowb, pl.BlockSpec((t, HEAD_DIM), qb)],
        out_specs=[pl.BlockSpec((s, QK_PAD), lambda h, p, it, jt: (0, h)), pl.BlockSpec((t, QK_PAD), kb),
                   pl.BlockSpec((t, HEAD_DIM), kb)],
        out_shape=[_sds((s, HEADS * QK_PAD), F32), _sds((s, HEADS * QK_PAD), F32), _sds((s, HEADS * HEAD_DIM), F32)],
        scratch=(), args=(q, k, v, lse_rows, delta_rows, do))


GDN_Q_SCALE = HEAD_DIM ** -0.5


def _shift_down(xv, prev8, sft):
    rolled = pltpu.roll(xv, sft, 0)
    top = pltpu.roll(jnp.concatenate([prev8, xv[:8]], axis=0), sft, 0)[8:]
    return jnp.concatenate([top, rolled[8:]], axis=0)


def _shift_up(xv, next8, sft):
    tm = xv.shape[0]
    rolled = pltpu.roll(xv, tm - sft, 0)
    bot = pltpu.roll(jnp.concatenate([xv[tm - 8:], next8], axis=0), 16 - sft, 0)[:8]
    return jnp.concatenate([rolled[:tm - 8], bot], axis=0)


def _conv_z(xv, prev8, w_ref):
    z = xv * w_ref[3:4, :]
    for sft in (1, 2, 3):
        z = z + _shift_down(xv, prev8, sft) * w_ref[3 - sft:4 - sft, :]
    return z


def _conv_specs(s, tm):
    nb16 = tm // 16
    cur = pl.BlockSpec((tm, 1024), lambda j, i: (i, 2 + j))
    prev = pl.BlockSpec((16, 1024), lambda j, i: (jnp.maximum(i * nb16 - 1, 0), 2 + j))
    return cur, prev


def _prev8(xp_ref, i):
    return jnp.where(i > 0, xp_ref[...].astype(F32)[8:], 0.0)


def _gdn_conv_fwd(proj, conv_w):
    s = proj.shape[0]
    tm = min(TILES["row"], s)
    cur, prev = _conv_specs(s, tm)

    def body(x_ref, xp_ref, w_ref, o_ref):
        j, i = pl.program_id(0), pl.program_id(1)
        a = _silu(_conv_z(x_ref[...].astype(F32), _prev8(xp_ref, i), w_ref))
        qk_scale = jnp.where(j == 0, GDN_Q_SCALE, 1.0)
        for h in range(HEADS):
            seg = a[:, 128 * h:128 * (h + 1)]
            r = lax.rsqrt(jnp.sum(seg * seg, axis=-1, keepdims=True) + NORM_EPS)
            o_ref[:, 128 * h:128 * (h + 1)] = jnp.where(j < 2, seg * r * qk_scale, seg)

    return _call(body, name="gdn_conv_fwd", grid=(3, s // tm),
                 in_specs=[cur, prev, pl.BlockSpec((4, 1024), lambda j, i: (0, j))],
                 out_specs=pl.BlockSpec((tm, 1024), lambda j, i: (i, j)), out_shape=_sds((s, 3072), F32),
                 args=(proj, proj, conv_w))


def _gdn_conv_bwd_a(proj, conv_w, dqkv):
    s = proj.shape[0]
    tm = min(TILES["row"], s)
    cur, prev = _conv_specs(s, tm)

    def body(x_ref, xp_ref, w_ref, d_ref, o_ref):
        j, i = pl.program_id(0), pl.program_id(1)
        z = _conv_z(x_ref[...].astype(F32), _prev8(xp_ref, i), w_ref)
        a = _silu(z)
        dsl = _dsilu(z)
        qk_scale = jnp.where(j == 0, GDN_Q_SCALE, 1.0)
        for h in range(HEADS):
            sl = slice(128 * h, 128 * (h + 1))
            seg = a[:, sl]
            dyv = d_ref[:, sl]
            r = lax.rsqrt(jnp.sum(seg * seg, axis=-1, keepdims=True) + NORM_EPS)
            yh = seg * r
            da_n = qk_scale * r * (dyv - yh * jnp.sum(yh * dyv, axis=-1, keepdims=True))
            o_ref[:, sl] = jnp.where(j < 2, da_n, dyv) * dsl[:, sl]

    return _call(body, name="gdn_conv_bwd_a", grid=(3, s // tm),
                 in_specs=[cur, prev, pl.BlockSpec((4, 1024), lambda j, i: (0, j)),
                           pl.BlockSpec((tm, 1024), lambda j, i: (i, j))],
                 out_specs=pl.BlockSpec((tm, 1024), lambda j, i: (i, j)), out_shape=_sds((s, 3072), F32),
                 args=(proj, proj, conv_w, dqkv))


def _gdn_conv_bwd_b(proj, conv_w, dz):
    s = proj.shape[0]
    tm = min(TILES["row"], s)
    nb8 = tm // 8
    last8 = s // 8 - 1
    cur, prev = _conv_specs(s, tm)

    def body(x_ref, w_ref, dz_ref, dzn_ref, dx_ref, dw_ref):
        i = pl.program_id(1)
        next8 = jnp.where(i < pl.num_programs(1) - 1, dzn_ref[...], 0.0)
        xv, dzv = x_ref[...].astype(F32), dz_ref[...]

        @pl.when(i == 0)
        def _():
            dw_ref[...] = jnp.zeros_like(dw_ref)

        dx = dzv * w_ref[3:4, :]
        dw_ref[3:4, :] += jnp.sum(dzv * xv, axis=0, keepdims=True)
        for sft in (1, 2, 3):
            up = _shift_up(dzv, next8, sft)
            dx = dx + up * w_ref[3 - sft:4 - sft, :]
            dw_ref[3 - sft:4 - sft, :] += jnp.sum(up * xv, axis=0, keepdims=True)
        dx_ref[...] = dx.astype(BF16)

    return _call(body, name="gdn_conv_bwd_b", grid=(3, s // tm),
                 in_specs=[cur, pl.BlockSpec((4, 1024), lambda j, i: (0, j)),
                           pl.BlockSpec((tm, 1024), lambda j, i: (i, j)),
                           pl.BlockSpec((8, 1024), lambda j, i: (jnp.minimum((i + 1) * nb8, last8), j))],
                 out_specs=[pl.BlockSpec((tm, 1024), lambda j, i: (i, j)), pl.BlockSpec((4, 1024), lambda j, i: (0, j))],
                 out_shape=[_sds((s, 3072), BF16), _sds((4, 3072), F32)], args=(proj, conv_w, dz, dz))


def _softplus(xv):
    return jnp.maximum(xv, 0.0) + jnp.log(1.0 + jnp.exp(-jnp.abs(xv)))


def _gdn_gates_fwd(proj, alog128, dtb128):
    s = proj.shape[0]
    tm = min(TILES["row"], s)

    def body(m_ref, a_ref, b_ref, o_ref):
        mv = m_ref[...]
        lane = lax.broadcasted_iota(jnp.int32, mv.shape, 1)
        g = -jnp.exp(a_ref[...]) * _softplus(mv + b_ref[...])
        is_g = (lane >= GA_LANE) & (lane < GA_LANE + HEADS)
        is_b = (lane >= GB_LANE) & (lane < GB_LANE + HEADS)
        o_ref[...] = jnp.where(is_g, g, jnp.where(is_b, _sigmoid(mv), 0.0))

    return _call(body, name="gdn_gates_fwd", grid=(s // tm,),
                 in_specs=[_rows(tm, 128), _full((1, 128)), _full((1, 128))],
                 out_specs=_rows(tm, 128), out_shape=_sds((s, 128), F32), args=(proj, alog128, dtb128))


def _gdn_gates_bwd(proj, alog128, dtb128, gbeta, dgbeta, dkr):
    s = proj.shape[0]
    tm = min(TILES["row"], s)

    def body(m_ref, a_ref, b_ref, gb_ref, d_ref, kr_ref, o_ref, da_ref, db_ref):
        mv, dv = m_ref[...], d_ref[...]
        lane = lax.broadcasted_iota(jnp.int32, mv.shape, 1)
        is_g = (lane >= GA_LANE) & (lane < GA_LANE + HEADS)
        is_b = (lane >= GB_LANE) & (lane < GB_LANE + HEADS)
        dga = jnp.where(is_g, dv * (-jnp.exp(a_ref[...])) * _sigmoid(mv + b_ref[...]), 0.0)
        beta = gb_ref[...]
        dgb = jnp.where(is_b, dv * beta * (1.0 - beta), 0.0)
        o_ref[...] = jnp.where(lane < 64, kr_ref[...], dga + dgb).astype(BF16)

        @pl.when(pl.program_id(0) == 0)
        def _():
            da_ref[...] = jnp.zeros_like(da_ref)
            db_ref[...] = jnp.zeros_like(db_ref)

        da_ref[...] += _acc8(jnp.where(is_g, dv * gb_ref[...], 0.0))
        db_ref[...] += _acc8(dga)

    return _call(body, name="gdn_gates_bwd", grid=(s // tm,),
                 in_specs=[_rows(tm, 128), _full((1, 128)), _full((1, 128)), _rows(tm, 128), _rows(tm, 128),
                           _rows(tm, 128)],
                 out_specs=[_rows(tm, 128), _full((8, 128)), _full((8, 128))],
                 out_shape=[_sds((s, 128), BF16), _sds((8, 128), F32), _sds((8, 128), F32)],
                 args=(proj, alog128, dtb128, gbeta, dgbeta, dkr))


def _col(mat, lane_idx, lane):
    return jnp.sum(jnp.where(lane == lane_idx, mat, 0.0), axis=-1, keepdims=True)


def _chunk_local(qh, kh, vh, gcol, bcol, ii, jj):
    lower, strict, eye = ii >= jj, ii > jj, ii == jj
    grow = jnp.sum(jnp.where(eye, gcol, 0.0), axis=0, keepdims=True)
    decay = jnp.where(lower, jnp.exp(jnp.where(lower, gcol - grow, 0.0)), 0.0)
    kb = kh * bcol
    vb = vh * bcol
    mm = _dot(kb, kh, NT)
    lmat = jnp.where(strict, mm * decay, 0.0)
    pw = -lmat
    tinv = jnp.where(eye, 1.0, 0.0) + pw
    for _ in range(5):
        pw = _dot_hi(pw, pw)
        tinv = tinv + _dot_hi(tinv, pw)
    egc = jnp.exp(gcol)
    kbg = kb * egc
    rhs = jnp.concatenate([vb, kbg], axis=-1)
    sol = _dot_hi(tinv, rhs)
    qk = _dot(qh, kh, NT)
    glast = jnp.sum(jnp.where(ii[:, :1] == CHUNK - 1, gcol, 0.0), axis=0, keepdims=True)
    ekd = jnp.exp(glast - gcol)
    return dict(decay=decay, kb=kb, vb=vb, mm=mm, lmat=lmat, tinv=tinv, egc=egc, kbg=kbg, rhs=rhs,
                u=sol[:, :HEAD_DIM], w=sol[:, HEAD_DIM:], qk=qk, amat=qk * decay, qd=qh * egc, ekd=ekd,
                kd=kh * ekd, gl=jnp.exp(glast), strict=strict, lower=lower, eye=eye)


def _tri(ii, jj):
    return jnp.where(ii >= jj, 1.0, 0.0)


def _gdn_fwd(qkv, gbeta):
    s = qkv.shape[0]
    n = s // CHUNK

    def body(qkv_ref, gb_ref, o_ref, st_ref, state):
        @pl.when(pl.program_id(0) == 0)
        def _():
            state[...] = jnp.zeros_like(state)

        ii = lax.broadcasted_iota(jnp.int32, (CHUNK, CHUNK), 0)
        jj = lax.broadcasted_iota(jnp.int32, (CHUNK, CHUNK), 1)
        lane = lax.broadcasted_iota(jnp.int32, (CHUNK, 128), 1)
        gbv = gb_ref[...]
        gc = _dot_hi(_tri(ii, jj), gbv)
        for h in range(HEADS):
            sl = slice(128 * h, 128 * (h + 1))
            qh = qkv_ref[:, 128 * h:128 * (h + 1)]
            kh = qkv_ref[:, 1024 + 128 * h:1024 + 128 * (h + 1)]
            vh = qkv_ref[:, 2048 + 128 * h:2048 + 128 * (h + 1)]
            c = _chunk_local(qh, kh, vh, _col(gc, GA_LANE + h, lane), _col(gbv, GB_LANE + h, lane), ii, jj)
            st = state[sl, :]
            st_ref[sl, :] = st
            vn = c["u"] - _dot(c["w"], st)
            o_ref[:, sl] = _dot(c["qd"], st) + _dot(c["amat"], vn)
            state[sl, :] = st * c["gl"] + _dot(c["kd"], vn, TN)

    return _call(body, name="gdn_fwd", grid=(n,),
                 in_specs=[_rows(CHUNK, 3072), _rows(CHUNK, 128)],
                 out_specs=[_rows(CHUNK, 1024), _rows(HEADS * 128, 128)],
                 out_shape=[_sds((s, 1024), F32), _sds((n * HEADS * 128, 128), F32)],
                 scratch=[pltpu.VMEM((HEADS * 128, 128), F32)], args=(qkv, gbeta))


def _gdn_bwd(qkv, gbeta, states, do):
    s = qkv.shape[0]
    n = s // CHUNK

    def body(qkv_ref, gb_ref, st_ref, do_ref, dqkv_ref, dgb_ref, dstate):
        @pl.when(pl.program_id(0) == 0)
        def _():
            dstate[...] = jnp.zeros_like(dstate)

        ii = lax.broadcasted_iota(jnp.int32, (CHUNK, CHUNK), 0)
        jj = lax.broadcasted_iota(jnp.int32, (CHUNK, CHUNK), 1)
        lane = lax.broadcasted_iota(jnp.int32, (CHUNK, 128), 1)
        row1 = ii[:, :1]
        gbv = gb_ref[...]
        gc = _dot_hi(_tri(ii, jj), gbv)
        dgc_all = jnp.zeros((CHUNK, 128), F32)
        db_all = jnp.zeros((CHUNK, 128), F32)
        for h in range(HEADS):
            sl = slice(128 * h, 128 * (h + 1))
            qh = qkv_ref[:, 128 * h:128 * (h + 1)]
            kh = qkv_ref[:, 1024 + 128 * h:1024 + 128 * (h + 1)]
            vh = qkv_ref[:, 2048 + 128 * h:2048 + 128 * (h + 1)]
            bcol = _col(gbv, GB_LANE + h, lane)
            c = _chunk_local(qh, kh, vh, _col(gc, GA_LANE + h, lane), bcol, ii, jj)
            st = st_ref[sl, :]
            dst = dstate[sl, :]
            dov = do_ref[:, sl]
            vn = c["u"] - _dot(c["w"], st)
            dvn = _dot(c["amat"], dov, TN) + _dot(c["kd"], dst)
            damat = jnp.where(c["lower"], _dot(dov, vn, NT), 0.0)
            dqd = _dot(dov, st, NT)
            dkd = _dot(vn, dst, NT)
            dw = -_dot(dvn, st, NT)
            dgl = jnp.sum(jnp.sum(st * dst, axis=-1, keepdims=True), axis=0, keepdims=True)
            dstate[sl, :] = _dot(c["qd"], dov, TN) + c["gl"] * dst - _dot(c["w"], dvn, TN)
            dsol = jnp.concatenate([dvn, dw], axis=-1)
            drhs = _dot_hi(c["tinv"], dsol, TN)
            dtinv = _dot_hi(dsol, c["rhs"], NT)
            dl = -_dot_hi(_dot_hi(c["tinv"], dtinv, TN), c["tinv"], NT)
            dl = jnp.where(c["strict"], dl, 0.0)
            dmm = dl * c["decay"]
            dqk = damat * c["decay"]
            wmat = dl * c["lmat"] + damat * c["amat"]
            dgc = jnp.sum(wmat, axis=-1, keepdims=True)
            wcol = jnp.sum(wmat, axis=0, keepdims=True)
            dgc = dgc - jnp.sum(jnp.where(c["eye"], wcol, 0.0), axis=-1, keepdims=True)
            dkb = _dot(dmm, kh) + drhs[:, HEAD_DIM:] * c["egc"]
            dk = _dot(dmm, c["kb"], TN) + _dot(dqk, qh, TN) + dkd * c["ekd"]
            dq = _dot(dqk, kh) + dqd * c["egc"]
            dgc = dgc + jnp.sum(drhs[:, HEAD_DIM:] * c["kbg"], axis=-1, keepdims=True)
            dgc = dgc + jnp.sum(dqd * c["qd"], axis=-1, keepdims=True)
            tmp = jnp.sum(dkd * c["kd"], axis=-1, keepdims=True)
            dgc = dgc - tmp
            dglast = jnp.sum(tmp, axis=0, keepdims=True) + dgl * c["gl"]
            dgc = dgc + jnp.where(row1 == CHUNK - 1, dglast, 0.0)
            dk = dk + dkb * bcol
            db = jnp.sum(dkb * kh, axis=-1, keepdims=True) + jnp.sum(drhs[:, :HEAD_DIM] * vh, axis=-1, keepdims=True)
            dqkv_ref[:, 128 * h:128 * (h + 1)] = dq
            dqkv_ref[:, 1024 + 128 * h:1024 + 128 * (h + 1)] = dk
            dqkv_ref[:, 2048 + 128 * h:2048 + 128 * (h + 1)] = drhs[:, :HEAD_DIM] * bcol
            dgc_all = dgc_all + jnp.where(lane == GA_LANE + h, dgc, 0.0)
            db_all = db_all + jnp.where(lane == GB_LANE + h, db, 0.0)
        dgb_ref[...] = _dot_hi(_tri(jj, ii), dgc_all) + db_all

    rev = lambda w: pl.BlockSpec((CHUNK, w), lambda i: (n - 1 - i, 0))
    return _call(body, name="gdn_bwd", grid=(n,),
                 in_specs=[rev(3072), rev(128), pl.BlockSpec((HEADS * 128, 128), lambda i: (n - 1 - i, 0)), rev(1024)],
                 out_specs=[rev(3072), rev(128)],
                 out_shape=[_sds((s, 3072), F32), _sds((s, 128), F32)],
                 scratch=[pltpu.VMEM((HEADS * 128, 128), F32)], args=(qkv, gbeta, states, do))


NN_B = (((2,), (1,)), ((0,), (0,)))
NT_B = (((2,), (2,)), ((0,), (0,)))
TN_B = (((1,), (1,)), ((0,), (0,)))
GDN_PAR_CHUNKS = 2
GDN_SEQ_CHUNKS = 4


def _gather_heads(qkv_ref, gc, gbv, qs, ks, vs, gs, bs, nchunks):
    lane = lax.broadcasted_iota(jnp.int32, (CHUNK, 128), 1)
    for c in range(nchunks):
        rows = slice(CHUNK * c, CHUNK * (c + 1))
        for h in range(HEADS):
            b = HEADS * c + h
            qs[b] = qkv_ref[rows, 128 * h:128 * (h + 1)]
            ks[b] = qkv_ref[rows, 1024 + 128 * h:1024 + 128 * (h + 1)]
            vs[b] = qkv_ref[rows, 2048 + 128 * h:2048 + 128 * (h + 1)]
            gs[b] = jnp.broadcast_to(_col(gc[rows], GA_LANE + h, lane), (CHUNK, 128))
            bs[b] = jnp.broadcast_to(_col(gbv[rows], GB_LANE + h, lane), (CHUNK, 128))


def _block_tri(rows, transpose=False):
    ri = lax.broadcasted_iota(jnp.int32, (rows, rows), 0)
    ci = lax.broadcasted_iota(jnp.int32, (rows, rows), 1)
    same = (ri >> 6) == (ci >> 6)
    return jnp.where(same & ((ci >= ri) if transpose else (ri >= ci)), 1.0, 0.0)


def _local_b(q, k, v, g128, b128):
    ii = lax.broadcasted_iota(jnp.int32, (1, CHUNK, CHUNK), 1)
    jj = lax.broadcasted_iota(jnp.int32, (1, CHUNK, CHUNK), 2)
    lower, strict, eye = ii >= jj, ii > jj, ii == jj
    g64 = g128[:, :, :CHUNK]
    grow = jnp.sum(jnp.where(eye, g64, 0.0), axis=1, keepdims=True)
    decay = jnp.where(lower, jnp.exp(jnp.where(lower, g64 - grow, 0.0)), 0.0)
    kb = k * b128
    vb = v * b128
    mm = lax.dot_general(kb.astype(BF16), k.astype(BF16), NT_B, preferred_element_type=F32)
    lmat = jnp.where(strict, mm * decay, 0.0)
    egc = jnp.exp(g128)
    kbg = kb * egc
    qk = lax.dot_general(q.astype(BF16), k.astype(BF16), NT_B, preferred_element_type=F32)
    row = lax.broadcasted_iota(jnp.int32, (1, CHUNK, 128), 1)
    glast = jnp.sum(jnp.where(row == CHUNK - 1, g128, 0.0), axis=1, keepdims=True)
    ekd = jnp.exp(glast - g128)
    return dict(decay=decay, kb=kb, vb=vb, lmat=lmat, egc=egc, kbg=kbg, amat=qk * decay, qd=q * egc, ekd=ekd,
                kd=k * ekd, gl=jnp.exp(glast), lower=lower, strict=strict, eye=eye)


def _bdot(a, b, dims):
    return lax.dot_general(a.astype(BF16), b.astype(BF16), dims, preferred_element_type=F32)


def _split(a):
    hi = a.astype(BF16)
    return hi, (a - hi.astype(F32)).astype(BF16)


def _bdot_hi(a, b, dims):
    ah, al = _split(a)
    bh, bl = _split(b)
    d = lambda x, y: lax.dot_general(x, y, dims, preferred_element_type=F32)
    return d(ah, bh) + d(ah, bl) + d(al, bh)


def _gdn_pre(qkv, gbeta):
    s = qkv.shape[0]
    n = s // CHUNK
    cb = min(GDN_PAR_CHUNKS, n)
    nb = cb * HEADS
    rows = cb * CHUNK

    def body(qkv_ref, gb_ref, u_ref, w_ref, qd_ref, kd_ref, a_ref, t_ref, gl_ref, qs, ks, vs, gs, bs):
        gbv = gb_ref[...]
        gc = _dot_hi(_block_tri(rows), gbv)
        _gather_heads(qkv_ref, gc, gbv, qs, ks, vs, gs, bs, cb)
        c = _local_b(qs[...], ks[...], vs[...], gs[...], bs[...])
        pw = -c["lmat"]
        tinv = jnp.where(c["eye"], 1.0, 0.0) + pw
        for _ in range(5):
            pw = _bdot_hi(pw, pw, NN_B)
            tinv = tinv + _bdot_hi(tinv, pw, NN_B)
        u_ref[...] = _bdot_hi(tinv, c["vb"], NN_B)
        w_ref[...] = _bdot_hi(tinv, c["kbg"], NN_B).astype(BF16)
        qd_ref[...] = c["qd"].astype(BF16)
        kd_ref[...] = c["kd"].astype(BF16)
        a_ref[...] = c["amat"].astype(BF16)
        t_ref[...] = tinv
        gl_ref[...] = c["gl"]

    b3 = lambda d: pl.BlockSpec((nb, CHUNK, d), lambda i: (i, 0, 0))
    nt = n * HEADS
    return _call(body, name="gdn_pre", grid=(n // cb,),
                 in_specs=[_rows(rows, 3072), _rows(rows, 128)],
                 out_specs=[b3(128), b3(128), b3(128), b3(128), b3(CHUNK), b3(CHUNK),
                            pl.BlockSpec((nb, 1, 128), lambda i: (i, 0, 0))],
                 out_shape=[_sds((nt, CHUNK, 128), F32), _sds((nt, CHUNK, 128), BF16), _sds((nt, CHUNK, 128), BF16),
                            _sds((nt, CHUNK, 128), BF16), _sds((nt, CHUNK, CHUNK), BF16), _sds((nt, CHUNK, CHUNK), F32),
                            _sds((nt, 1, 128), F32)],
                 scratch=[pltpu.VMEM((nb, CHUNK, 128), F32)] * 5, sem=("parallel",), args=(qkv, gbeta))


def _gdn_scan_fwd(u, w, qd, kd, amat, gl):
    nt = u.shape[0]
    n = nt // HEADS
    cs = min(GDN_SEQ_CHUNKS, n)

    def body(u_ref, w_ref, qd_ref, kd_ref, a_ref, gl_ref, o_ref, st_ref, state):
        @pl.when(pl.program_id(0) == 0)
        def _():
            state[...] = jnp.zeros_like(state)

        for c in range(cs):
            sl = slice(HEADS * c, HEADS * (c + 1))
            st = state[...]
            stb = st.astype(BF16)
            st_ref[sl] = stb
            vn = u_ref[sl] - lax.dot_general(w_ref[sl], stb, NN_B, preferred_element_type=F32)
            vnb = vn.astype(BF16)
            o = (lax.dot_general(qd_ref[sl], stb, NN_B, preferred_element_type=F32)
                 + lax.dot_general(a_ref[sl], vnb, NN_B, preferred_element_type=F32))
            state[...] = st * gl_ref[sl] + lax.dot_general(kd_ref[sl], vnb, TN_B, preferred_element_type=F32)
            for h in range(HEADS):
                o_ref[CHUNK * c:CHUNK * (c + 1), 128 * h:128 * (h + 1)] = o[h]

    b3 = lambda d: pl.BlockSpec((cs * HEADS, CHUNK, d), lambda i: (i, 0, 0))
    return _call(body, name="gdn_scan_fwd", grid=(n // cs,),
                 in_specs=[b3(128), b3(128), b3(128), b3(128), b3(CHUNK), pl.BlockSpec((cs * HEADS, 1, 128), lambda i: (i, 0, 0))],
                 out_specs=[_rows(cs * CHUNK, 1024), pl.BlockSpec((cs * HEADS, 128, 128), lambda i: (i, 0, 0))],
                 out_shape=[_sds((n * CHUNK, 1024), F32), _sds((nt, 128, 128), BF16)],
                 scratch=[pltpu.VMEM((HEADS, 128, 128), F32)], args=(u, w, qd, kd, amat, gl))


def _gdn_scan_bwd(w, qd, kd, amat, gl, do):
    nt = w.shape[0]
    n = nt // HEADS
    cs = min(GDN_SEQ_CHUNKS, n)
    ng = n // cs

    def body(w_ref, qd_ref, kd_ref, a_ref, gl_ref, do_ref, ds_ref, dstate, dos):
        @pl.when(pl.program_id(0) == 0)
        def _():
            dstate[...] = jnp.zeros_like(dstate)

        for c in reversed(range(cs)):
            sl = slice(HEADS * c, HEADS * (c + 1))
            for h in range(HEADS):
                dos[h] = do_ref[CHUNK * c:CHUNK * (c + 1), 128 * h:128 * (h + 1)].astype(BF16)
            dob = dos[...]
            dst = dstate[...]
            dstb = dst.astype(BF16)
            ds_ref[sl] = dstb
            dvn = (lax.dot_general(a_ref[sl], dob, TN_B, preferred_element_type=F32)
                   + lax.dot_general(kd_ref[sl], dstb, NN_B, preferred_element_type=F32))
            dstate[...] = (lax.dot_general(qd_ref[sl], dob, TN_B, preferred_element_type=F32) + gl_ref[sl] * dst
                           - lax.dot_general(w_ref[sl], dvn.astype(BF16), TN_B, preferred_element_type=F32))

    b3 = lambda d: pl.BlockSpec((cs * HEADS, CHUNK, d), lambda i: (ng - 1 - i, 0, 0))
    return _call(body, name="gdn_scan_bwd", grid=(ng,),
                 in_specs=[b3(128), b3(128), b3(128), b3(CHUNK), pl.BlockSpec((cs * HEADS, 1, 128), lambda i: (ng - 1 - i, 0, 0)),
                           pl.BlockSpec((cs * CHUNK, 1024), lambda i: (ng - 1 - i, 0))],
                 out_specs=pl.BlockSpec((cs * HEADS, 128, 128), lambda i: (ng - 1 - i, 0, 0)),
                 out_shape=_sds((nt, 128, 128), BF16),
                 scratch=[pltpu.VMEM((HEADS, 128, 128), F32), pltpu.VMEM((HEADS, CHUNK, 128), BF16)],
                 args=(w, qd, kd, amat, gl, do))


def _gdn_post_bwd(qkv, gbeta, u, w, tinv, states, dstates, do):
    s = qkv.shape[0]
    n = s // CHUNK
    cb = min(GDN_PAR_CHUNKS, n)
    nb = cb * HEADS
    rows = cb * CHUNK

    def body(qkv_ref, gb_ref, u_ref, w_ref, t_ref, st_ref, ds_ref, do_ref, dqkv_ref, dgb_ref, qs, ks, vs, gs, bs, dos):
        gbv = gb_ref[...]
        gc = _dot_hi(_block_tri(rows), gbv)
        _gather_heads(qkv_ref, gc, gbv, qs, ks, vs, gs, bs, cb)
        for c in range(cb):
            for h in range(HEADS):
                dos[HEADS * c + h] = do_ref[CHUNK * c:CHUNK * (c + 1), 128 * h:128 * (h + 1)].astype(F32)
        q, k, v, b128 = qs[...], ks[...], vs[...], bs[...]
        c = _local_b(q, k, v, gs[...], b128)
        tinv, st, dst, dov = t_ref[...], st_ref[...], ds_ref[...], dos[...]
        wv = w_ref[...]
        vn = u_ref[...] - _bdot(wv, st, NN_B)
        dvn = _bdot(c["amat"], dov, TN_B) + _bdot(c["kd"], dst, NN_B)
        damat = jnp.where(c["lower"], _bdot(dov, vn, NT_B), 0.0)
        dqd = _bdot(dov, st, NT_B)
        dkd = _bdot(vn, dst, NT_B)
        dw = -_bdot(dvn, st, NT_B)
        dgl = jnp.sum(jnp.sum(st.astype(F32) * dst.astype(F32), axis=1, keepdims=True), axis=-1, keepdims=True)
        dvb = _bdot(tinv, dvn, TN_B)
        dkbg = _bdot(tinv, dw, TN_B)
        dtinv = _bdot(dvn, c["vb"], NT_B) + _bdot(dw, c["kbg"], NT_B)
        dl = -_bdot(_bdot(tinv, dtinv, TN_B), tinv, NT_B)
        dl = jnp.where(c["strict"], dl, 0.0)
        dmm = dl * c["decay"]
        dqk = damat * c["decay"]
        wmat = dl * c["lmat"] + damat * c["amat"]
        wcol = jnp.sum(wmat, axis=1, keepdims=True)
        dgc = jnp.sum(wmat, axis=-1, keepdims=True) - jnp.sum(jnp.where(c["eye"], wcol, 0.0), axis=-1, keepdims=True)
        dkb = _bdot(dmm, k, NN_B) + dkbg * c["egc"]
        dk = _bdot(dmm, c["kb"], TN_B) + _bdot(dqk, q, TN_B) + dkd * c["ekd"] + dkb * b128
        dq = _bdot(dqk, k, NN_B) + dqd * c["egc"]
        tmp = jnp.sum(dkd * c["kd"], axis=-1, keepdims=True)
        dgc = (dgc + jnp.sum(dkbg * c["kbg"], axis=-1, keepdims=True) + jnp.sum(dqd * c["qd"], axis=-1, keepdims=True)
               - tmp)
        dglast = jnp.sum(tmp, axis=1, keepdims=True) + dgl * c["gl"][:, :, :1]
        row1 = lax.broadcasted_iota(jnp.int32, (1, CHUNK, 1), 1)
        dgc = dgc + jnp.where(row1 == CHUNK - 1, dglast, 0.0)
        db = jnp.sum(dkb * k, axis=-1, keepdims=True) + jnp.sum(dvb * v, axis=-1, keepdims=True)
        dv = dvb * b128
        lane = lax.broadcasted_iota(jnp.int32, (CHUNK, 128), 1)
        parts = []
        for cc in range(cb):
            acc = jnp.zeros((CHUNK, 128), F32)
            for h in range(HEADS):
                bi = HEADS * cc + h
                rs = slice(CHUNK * cc, CHUNK * (cc + 1))
                dqkv_ref[rs, 128 * h:128 * (h + 1)] = dq[bi]
                dqkv_ref[rs, 1024 + 128 * h:1024 + 128 * (h + 1)] = dk[bi]
                dqkv_ref[rs, 2048 + 128 * h:2048 + 128 * (h + 1)] = dv[bi]
                acc = acc + jnp.where(lane == GA_LANE + h, dgc[bi], 0.0)
            parts.append(acc)
        dgc_all = jnp.concatenate(parts, axis=0)
        dg_all = _dot_hi(_block_tri(rows, transpose=True), dgc_all)
        for cc in range(cb):
            acc = dg_all[CHUNK * cc:CHUNK * (cc + 1)]
            for h in range(HEADS):
                acc = acc + jnp.where(lane == GB_LANE + h, db[HEADS * cc + h], 0.0)
            dgb_ref[CHUNK * cc:CHUNK * (cc + 1), :] = acc

    b3 = lambda d1, d2: pl.BlockSpec((nb, d1, d2), lambda i: (i, 0, 0))
    return _call(body, name="gdn_post_bwd", grid=(n // cb,),
                 in_specs=[_rows(rows, 3072), _rows(rows, 128), b3(CHUNK, 128), b3(CHUNK, 128), b3(CHUNK, CHUNK),
                           b3(128, 128), b3(128, 128), _rows(rows, 1024)],
                 out_specs=[_rows(rows, 3072), _rows(rows, 128)],
                 out_shape=[_sds((s, 3072), F32), _sds((s, 128), F32)],
                 scratch=[pltpu.VMEM((nb, CHUNK, 128), F32)] * 6, sem=("parallel",),
                 args=(qkv, gbeta, u, w, tinv, states, dstates, do))


def _mix_fwd(o_mla, o_gdn, proj, out_gain):
    s = proj.shape[0]
    tm = min(TILES["row"], s)

    def body(om_ref, og_ref, mg_ref, gg_ref, g_ref, o_ref):
        o_ref[:, :1024] = (om_ref[...] * _silu(mg_ref[...].astype(F32))).astype(BF16)
        for h in range(HEADS):
            sl = slice(128 * h, 128 * (h + 1))
            og = og_ref[:, sl]
            on = og * _rms(og, HEAD_DIM) * g_ref[...]
            o_ref[:, 1024 + 128 * h:1024 + 128 * (h + 1)] = (on * _silu(gg_ref[:, sl].astype(F32))).astype(BF16)

    return _call(body, name="mix_fwd", grid=(s // tm,),
                 in_specs=[_rows(tm, 1024), _rows(tm, 1024), _rows(tm, 1024, 1), _rows(tm, 1024, 5), _full((1, 128))],
                 out_specs=_rows(tm, 2048), out_shape=_sds((s, 2048), BF16), args=(o_mla, o_gdn, proj, proj, out_gain))


def _mix_bwd(o_mla, o_gdn, proj, out_gain, dmixed):
    s = proj.shape[0]
    tm = min(TILES["row"], s)

    def body(om_ref, og_ref, mg_ref, gg_ref, g_ref, dm_ref, dg_ref, dom_ref, dog_ref, dmg_ref, dgg_ref, ag_ref,
             delta_ref):
        @pl.when(pl.program_id(0) == 0)
        def _():
            ag_ref[...] = jnp.zeros_like(ag_ref)

        mg = mg_ref[...].astype(F32)
        dm = dm_ref[...].astype(F32)
        om = om_ref[...]
        dom = (dm * _silu(mg)).astype(BF16)
        dom_ref[...] = dom
        dmg_ref[...] = (dm * om * _dsilu(mg)).astype(BF16)
        prod = dom.astype(F32) * om
        lane = lax.broadcasted_iota(jnp.int32, (tm, 128), 1)
        delta = jnp.zeros((tm, 128), F32)
        for h in range(HEADS):
            delta = delta + jnp.where(lane == h, jnp.sum(prod[:, 128 * h:128 * (h + 1)], axis=-1, keepdims=True), 0.0)
        delta_ref[...] = delta
        for h in range(HEADS):
            sl = slice(128 * h, 128 * (h + 1))
            og, gg, d = og_ref[:, sl], gg_ref[:, sl].astype(F32), dg_ref[:, sl].astype(F32)
            on = og * _rms(og, HEAD_DIM) * g_ref[...]
            dgg_ref[:, sl] = (d * on * _dsilu(gg)).astype(BF16)
            dx, gpart = _rms_bwd(og, g_ref[...], d * _silu(gg), HEAD_DIM)
            dog_ref[:, sl] = dx.astype(BF16)
            ag_ref[...] += _acc8(gpart)

    return _call(body, name="mix_bwd", grid=(s // tm,),
                 in_specs=[_rows(tm, 1024), _rows(tm, 1024), _rows(tm, 1024, 1), _rows(tm, 1024, 5), _full((1, 128)),
                           _rows(tm, 1024, 0), _rows(tm, 1024, 1)],
                 out_specs=[_rows(tm, 1024), _rows(tm, 1024), _rows(tm, 1024), _rows(tm, 1024), _full((8, 128)),
                            _rows(tm, 128)],
                 out_shape=[_sds((s, 1024), BF16), _sds((s, 1024), BF16), _sds((s, 1024), BF16), _sds((s, 1024), BF16),
                            _sds((8, 128), F32), _sds((s, 128), F32)],
                 args=(o_mla, o_gdn, proj, proj, out_gain, dmixed, dmixed))


def _out_fwd(mixed, w_out, x, target):
    s = x.shape[0]
    tm = min(TILES["mm"], s)
    tn = min(TILES["mm"], D_MODEL)

    def body(m_ref, w_ref, x_ref, t_ref, dy_ref, acc_ref):
        err = x_ref[...] + _dot(m_ref[...], w_ref[...]) - t_ref[...]
        dy_ref[...] = (err * (1.0 / D_MODEL)).astype(BF16)

        @pl.when(pl.program_id(1) == 0)
        def _():
            acc_ref[...] = jnp.zeros_like(acc_ref)

        acc_ref[...] += _acc8(err * err)

    return _call(body, name="out_fwd", grid=(D_MODEL // tn, s // tm),
                 in_specs=[pl.BlockSpec((tm, D_MODEL), lambda j, i: (i, 0)), pl.BlockSpec((D_MODEL, tn), lambda j, i: (0, j)),
                           pl.BlockSpec((tm, tn), lambda j, i: (i, j)), pl.BlockSpec((tm, tn), lambda j, i: (i, j))],
                 out_specs=[pl.BlockSpec((tm, tn), lambda j, i: (i, j)), pl.BlockSpec((8, tn), lambda j, i: (0, j))],
                 out_shape=[_sds((s, D_MODEL), BF16), _sds((8, D_MODEL), F32)], args=(mixed, w_out, x, target))


def _row_tile(r, c):
    if r % 8 != 0:
        return r
    t = 8
    while r % (2 * t) == 0 and 2 * t * c * 4 <= (1 << 20):
        t *= 2
    return t


def _sum_arrays(parts, name, also_bf16=False):
    r, c = parts[0].shape
    tr = _row_tile(r, c)
    n = len(parts)

    def body(*refs):
        acc = refs[0][...].astype(F32)
        for p_ref in refs[1:n]:
            acc = acc + p_ref[...].astype(F32)
        refs[n][...] = acc
        if also_bf16:
            refs[n + 1][...] = acc.astype(BF16)

    nout = 2 if also_bf16 else 1
    out = _call(body, name=name, grid=(r // tr,), in_specs=[_rows(tr, c)] * n, out_specs=[_rows(tr, c)] * nout,
                out_shape=[_sds((r, c), F32), _sds((r, c), BF16)][:nout], args=tuple(parts))
    return out if also_bf16 else out[0]


def _adamw(w, g, m, v, name):
    r, c = w.shape
    c1 = 1.0 - ADAM_B1 ** ADAM_STEP
    c2 = 1.0 - ADAM_B2 ** ADAM_STEP

    def body(w_ref, g_ref, m_ref, v_ref, d_ref, nm_ref, nv_ref):
        gv = g_ref[...]
        nm = ADAM_B1 * m_ref[...] + (1.0 - ADAM_B1) * gv
        nv = ADAM_B2 * v_ref[...] + (1.0 - ADAM_B2) * (gv * gv)
        nm_ref[...] = nm
        nv_ref[...] = nv
        d_ref[...] = -ADAM_LR * ((nm / c1) / (jnp.sqrt(nv / c2) + ADAM_EPS) + ADAM_WD * w_ref[...])

    if r % 8 == 0:
        tr = _row_tile(r, c)
        grid, spec = (r // tr,), _rows(tr, c)
    else:
        tc = c
        while tc % 256 == 0 and r * tc * 4 > (3 << 19):
            tc //= 2
        grid, spec = (c // tc,), pl.BlockSpec((r, tc), lambda i: (0, i))
    return _call(body, name=name, grid=grid, in_specs=[spec] * 4, out_specs=[spec] * 3,
                 out_shape=[_sds((r, c), F32)] * 3, args=(w, g, m, v))


ANY = pl.BlockSpec(memory_space=pl.ANY)
CHIP_FLIPS = ((1, 0), (0, 1), (1, 1))


def _comm_call(body, *, name, n_in, out_shape, scratch):
    def kfn(*refs):
        body(*refs)
    return pl.pallas_call(kfn, name=name, in_specs=[ANY] * n_in, out_specs=[ANY] * len(out_shape), out_shape=out_shape,
                          scratch_shapes=list(scratch),
                          compiler_params=pltpu.CompilerParams(has_side_effects=True))


def _all_gather_chips(shards):
    na = len(shards)

    def body(*refs):
        copies = _gather_copies(refs[:na], refs[na:2 * na], *refs[2 * na:])
        _gather_start(copies)
        _gather_finish(copies)

    out_shape = [_sds((4,) + a.shape, a.dtype) for a in shards]
    sem = pltpu.SemaphoreType.DMA((na, 3))
    got = _comm_call(body, name="all_gather_weights", n_in=na, out_shape=out_shape, scratch=[sem, sem, sem, sem])(*shards)
    return _place_own_blocks(got, shards)


def _gather_copies(ins, outs, send_sems, recv_sems, fwd_send, fwd_recv):
    x, y, c = lax.axis_index("x"), lax.axis_index("y"), lax.axis_index("c")
    my_k = 2 * x + y
    direct, forwards = [], []
    for a in range(len(ins)):
        rows = ins[a].shape[0]
        for r, (fx, fy) in enumerate(CHIP_FLIPS):
            px, py = x ^ fx, y ^ fy
            if rows % 32 == 0:
                mine = pl.ds(pl.multiple_of(c * (rows // 2), 16), rows // 2)
                other = pl.ds(pl.multiple_of((1 - c) * (rows // 2), 16), rows // 2)
                rc = pltpu.make_async_remote_copy(
                    src_ref=ins[a].at[mine], dst_ref=outs[a].at[my_k, mine], send_sem=send_sems.at[a, r],
                    recv_sem=recv_sems.at[a, r], device_id=(px, py, c), device_id_type=MESH)
                landed = outs[a].at[2 * px + py, mine]
                fw = pltpu.make_async_remote_copy(
                    src_ref=landed, dst_ref=landed, send_sem=fwd_send.at[a, r], recv_sem=fwd_recv.at[a, r],
                    device_id=(x, y, 1 - c), device_id_type=MESH)
                from_sib = outs[a].at[2 * px + py, other]
                fw_in = pltpu.make_async_remote_copy(
                    src_ref=from_sib, dst_ref=from_sib, send_sem=fwd_send.at[a, r], recv_sem=fwd_recv.at[a, r],
                    device_id=(x, y, 1 - c), device_id_type=MESH)
                forwards.append((rc, fw, fw_in))
            else:
                direct.append(pltpu.make_async_remote_copy(
                    src_ref=ins[a], dst_ref=outs[a].at[my_k], send_sem=send_sems.at[a, r],
                    recv_sem=recv_sems.at[a, r], device_id=(px, py, c), device_id_type=MESH))
    return forwards, direct


def _gather_start(copies):
    forwards, direct = copies
    for rc, _, _ in forwards:
        rc.start()
    for rc in direct:
        rc.start()


def _gather_finish(copies):
    forwards, direct = copies
    for rc, fw, _ in forwards:
        rc.wait_recv()
        fw.start()
    for rc, fw, fw_in in forwards:
        rc.wait_send()
        fw.wait_send()
        fw_in.wait_recv()
    for rc in direct:
        rc.wait()


def _place_own_blocks(got, shards):
    my_k = 2 * lax.axis_index("x") + lax.axis_index("y")
    return [[jnp.where(my_k == k, a, g[k]) for k in range(4)] for g, a in zip(got, shards)]


def _norm1_fwd_gather(x, gain, shards):
    s = x.shape[0]
    tm = min(TILES["row"], s)
    ni = s // tm
    na = len(shards)

    def kfn(x_ref, g_ref, *rest):
        o_ref = rest[na]
        sems = rest[2 * na + 1:]
        i = pl.program_id(0)

        @pl.when(i == 0)
        def _():
            _gather_start(_gather_copies(rest[:na], rest[na + 1:2 * na + 1], *sems))

        xv = x_ref[...]
        r = lax.rsqrt(jnp.mean(xv * xv, axis=-1, keepdims=True) + NORM_EPS)
        o_ref[...] = (xv * r * g_ref[...]).astype(BF16)

        @pl.when(i == ni - 1)
        def _():
            _gather_finish(_gather_copies(rest[:na], rest[na + 1:2 * na + 1], *sems))

    sem = pltpu.SemaphoreType.DMA((na, 3))
    out = pl.pallas_call(
        kfn, name="norm1_fwd_gather", grid=(ni,),
        in_specs=[_rows(tm, D_MODEL), _full((1, D_MODEL))] + [ANY] * na,
        out_specs=[_rows(tm, D_MODEL)] + [ANY] * na,
        out_shape=[_sds((s, D_MODEL), BF16)] + [_sds((4,) + s_a.shape, s_a.dtype) for s_a in shards],
        scratch_shapes=[sem, sem, sem, sem],
        compiler_params=pltpu.CompilerParams(dimension_semantics=("arbitrary",), vmem_limit_bytes=VMEM_LIMIT_V7X,
                                             has_side_effects=True))(x, gain, *shards)
    return out[0], _place_own_blocks(list(out[1:]), shards)


def _matmul_nn_gather(a, b, shards, *, name, tm, tn, out_dtype):
    m, kdim = a.shape
    n = b.shape[1]
    ni, nj = m // tm, n // tn
    na = len(shards)

    def body(a_ref, b_ref, *rest):
        o_ref = rest[na]
        sems = rest[2 * na + 1:]
        i, j = pl.program_id(0), pl.program_id(1)

        @pl.when((i == 0) & (j == 0))
        def _():
            _gather_start(_gather_copies(rest[:na], rest[na + 1:2 * na + 1], *sems))

        o_ref[...] = _dot(a_ref[...], b_ref[...]).astype(out_dtype)

        @pl.when((i == ni - 1) & (j == nj - 1))
        def _():
            _gather_finish(_gather_copies(rest[:na], rest[na + 1:2 * na + 1], *sems))

    def kfn(*refs):
        body(*refs)
    sem = pltpu.SemaphoreType.DMA((na, 3))
    out = pl.pallas_call(
        kfn, name=name, grid=(ni, nj),
        in_specs=[pl.BlockSpec((tm, kdim), lambda i, j: (i, 0)), pl.BlockSpec((kdim, tn), lambda i, j: (0, j))] + [ANY] * na,
        out_specs=[pl.BlockSpec((tm, tn), lambda i, j: (i, j))] + [ANY] * na,
        out_shape=[_sds((m, n), out_dtype)] + [_sds((4,) + s_a.shape, s_a.dtype) for s_a in shards],
        scratch_shapes=[sem, sem, sem, sem],
        compiler_params=pltpu.CompilerParams(dimension_semantics=("arbitrary", "arbitrary"),
                                             vmem_limit_bytes=VMEM_LIMIT_V7X, has_side_effects=True))(a, b, *shards)
    return out[0], _place_own_blocks(list(out[1:]), shards)


def _all_reduce_small(vec):
    r = vec.shape[0]

    def body(v_ref, o_ref, gath, send_sems, recv_sems):
        x, y, c = lax.axis_index("x"), lax.axis_index("y"), lax.axis_index("c")
        me = 4 * x + 2 * y + c
        gath[me] = v_ref[...]
        copies = []
        for rel in range(1, 8):
            fx, fy, fc = (rel >> 2) & 1, (rel >> 1) & 1, rel & 1
            rc = pltpu.make_async_remote_copy(
                src_ref=v_ref, dst_ref=gath.at[me], send_sem=send_sems.at[rel - 1], recv_sem=recv_sems.at[rel - 1],
                device_id=(x ^ fx, y ^ fy, c ^ fc), device_id_type=MESH)
            rc.start()
            copies.append(rc)
        for rc in copies:
            rc.wait()
        acc = gath[0]
        for d in range(1, 8):
            acc = acc + gath[d]
        o_ref[...] = acc

    def kfn(*refs):
        body(*refs)
    vm = pl.BlockSpec(memory_space=pltpu.VMEM)
    return pl.pallas_call(kfn, name="all_reduce_small", in_specs=[vm], out_specs=vm, out_shape=_sds((r, 128), F32),
                          scratch_shapes=[pltpu.VMEM((8, r, 128), F32), pltpu.SemaphoreType.DMA((7,)),
                                          pltpu.SemaphoreType.DMA((7,))],
                          compiler_params=pltpu.CompilerParams(has_side_effects=True))(vec)


def _exchange_halves(arrs):
    na = len(arrs)

    def body(*refs):
        ins, outs = refs[:na], refs[na:2 * na]
        send_sems, recv_sems = refs[2 * na:]
        x, y, c = lax.axis_index("x"), lax.axis_index("y"), lax.axis_index("c")
        copies = []
        for a in range(na):
            half = ins[a].shape[1] // 2
            src = ins[a].at[:, pl.ds(pl.multiple_of((1 - c) * half, 8), half), :]
            rc = pltpu.make_async_remote_copy(src_ref=src, dst_ref=outs[a], send_sem=send_sems.at[a],
                                              recv_sem=recv_sems.at[a], device_id=(x, y, 1 - c), device_id_type=MESH)
            rc.start()
            copies.append(rc)
        for rc in copies:
            rc.wait()

    out_shape = [_sds((4, a.shape[1] // 2, a.shape[2]), F32) for a in arrs]
    return _comm_call(body, name="rs_pair_exchange", n_in=na, out_shape=out_shape,
                      scratch=[pltpu.SemaphoreType.DMA((na,)), pltpu.SemaphoreType.DMA((na,))])(*arrs)


def _scatter_to_chips(arrs):
    na = len(arrs)

    def body(*refs):
        ins, outs = refs[:na], refs[na:2 * na]
        send_sems, recv_sems = refs[2 * na:]
        x, y, c = lax.axis_index("x"), lax.axis_index("y"), lax.axis_index("c")
        copies = []
        for a in range(na):
            for r, (fx, fy) in enumerate(CHIP_FLIPS):
                px, py = x ^ fx, y ^ fy
                rc = pltpu.make_async_remote_copy(
                    src_ref=ins[a].at[2 * px + py], dst_ref=outs[a].at[r], send_sem=send_sems.at[a, r],
                    recv_sem=recv_sems.at[a, r], device_id=(px, py, c), device_id_type=MESH)
                rc.start()
                copies.append(rc)
        for rc in copies:
            rc.wait()

    out_shape = [_sds((3,) + a.shape[1:], a.dtype) for a in arrs]
    return _comm_call(body, name="rs_chip_scatter", n_in=na, out_shape=out_shape,
                      scratch=[pltpu.SemaphoreType.DMA((na, 3)), pltpu.SemaphoreType.DMA((na, 3))])(*arrs)


def _sum_into_half(parts, name):
    r2, c = parts[0].shape
    tr = _row_tile(r2, c)
    nb = r2 // tr
    n = len(parts)

    def kfn(c_ref, *refs):
        acc = refs[0][...].astype(F32)
        for p_ref in refs[1:n]:
            acc = acc + p_ref[...].astype(F32)
        refs[n][...] = acc

    spec = pltpu.PrefetchScalarGridSpec(
        num_scalar_prefetch=1, grid=(nb,), in_specs=[pl.BlockSpec((tr, c), lambda i, cr: (i, 0))] * n,
        out_specs=pl.BlockSpec((tr, c), lambda i, cr: (cr[0] * nb + i, 0)))
    core = lax.axis_index("c").astype(jnp.int32).reshape(1)
    return pl.pallas_call(kfn, name=name, grid_spec=spec, out_shape=_sds((2 * r2, c), F32),
                          compiler_params=pltpu.CompilerParams(dimension_semantics=("arbitrary",),
                                                               vmem_limit_bytes=VMEM_LIMIT_V7X))(core, *parts)


def _join_in_place(arrs):
    na = len(arrs)

    def body(*refs):
        outs = refs[na:2 * na]
        send_sems, recv_sems = refs[2 * na:]
        x, y, c = lax.axis_index("x"), lax.axis_index("y"), lax.axis_index("c")
        copies = []
        for a in range(na):
            half = outs[a].shape[0] // 2
            mine = outs[a].at[pl.ds(pl.multiple_of(c * half, 8), half), :]
            rc = pltpu.make_async_remote_copy(src_ref=mine, dst_ref=mine, send_sem=send_sems.at[a],
                                              recv_sem=recv_sems.at[a], device_id=(x, y, 1 - c), device_id_type=MESH)
            rc.start()
            copies.append(rc)
        for rc in copies:
            rc.wait()

    def kfn(*refs):
        body(*refs)
    return pl.pallas_call(kfn, name="rs_pair_join", in_specs=[ANY] * na, out_specs=[ANY] * na,
                          out_shape=[_sds(a.shape, F32) for a in arrs],
                          input_output_aliases={a: a for a in range(na)},
                          scratch_shapes=[pltpu.SemaphoreType.DMA((na,)), pltpu.SemaphoreType.DMA((na,))],
                          compiler_params=pltpu.CompilerParams(has_side_effects=True))(*arrs)


def _pair_sum(g, o, name):
    _, r, c = g.shape
    half = r // 2
    tr = _row_tile(half, c)
    nb = half // tr

    def kfn(c_ref, g_ref, o_ref, s32_ref, s16_ref):
        acc = g_ref[...] + o_ref[...]
        s32_ref[...] = acc
        s16_ref[...] = acc.astype(BF16)

    blk = lambda imap: pl.BlockSpec((None, tr, c), imap)
    same = lambda k, i, cr: (k, i, 0)
    spec = pltpu.PrefetchScalarGridSpec(
        num_scalar_prefetch=1, grid=(4, nb), in_specs=[blk(lambda k, i, cr: (k, cr[0] * nb + i, 0)), blk(same)],
        out_specs=[blk(same), blk(same)])
    core = lax.axis_index("c").astype(jnp.int32).reshape(1)
    return pl.pallas_call(kfn, name=name, grid_spec=spec, out_shape=[_sds((4, half, c), F32), _sds((4, half, c), BF16)],
                          compiler_params=pltpu.CompilerParams(dimension_semantics=("arbitrary", "arbitrary"),
                                                               vmem_limit_bytes=VMEM_LIMIT_V7X))(core, g, o)


def _rs_pair_stage(grads):
    got = _exchange_halves(grads)
    sums = [_pair_sum(g, o, f"rs_pair_sum_{a}") for a, (g, o) in enumerate(zip(grads, got))]
    return [s32 for s32, _ in sums], [s16 for _, s16 in sums]


def _rs_chip_stage(pair, recv):
    k_me = 2 * lax.axis_index("x") + lax.axis_index("y")
    halves = []
    for a, (p, rv) in enumerate(zip(pair, recv)):
        own = lax.dynamic_index_in_dim(p, k_me, 0, keepdims=False)
        halves.append(_sum_into_half([own, rv[0], rv[1], rv[2]], f"rs_chip_sum_{a}"))
    return _join_in_place(halves)


def _reduce_scatter(grads):
    pair, pair_bf16 = _rs_pair_stage(grads)
    return _rs_chip_stage(pair, _scatter_to_chips(pair_bf16))


def _matmul_nt_scatter(a, b, send, *, name, tm, tn, out_dtype):
    m, kdim = a.shape
    n = b.shape[0]
    ni, nj = m // tm, n // tn
    na = len(send)

    def body(a_ref, b_ref, *rest):
        send_refs, o_ref, recv_refs = rest[:na], rest[na], rest[na + 1:2 * na + 1]
        send_sems, recv_sems = rest[2 * na + 1:]
        i, j = pl.program_id(0), pl.program_id(1)

        def copies():
            x, y, c = lax.axis_index("x"), lax.axis_index("y"), lax.axis_index("c")
            out = []
            for s_i in range(na):
                for r, (fx, fy) in enumerate(CHIP_FLIPS):
                    px, py = x ^ fx, y ^ fy
                    out.append(pltpu.make_async_remote_copy(
                        src_ref=send_refs[s_i].at[2 * px + py], dst_ref=recv_refs[s_i].at[r],
                        send_sem=send_sems.at[s_i, r], recv_sem=recv_sems.at[s_i, r], device_id=(px, py, c),
                        device_id_type=MESH))
            return out

        @pl.when((i == 0) & (j == 0))
        def _():
            for cp in copies():
                cp.start()

        o_ref[...] = _dot(a_ref[...], b_ref[...], NT).astype(out_dtype)

        @pl.when((i == ni - 1) & (j == nj - 1))
        def _():
            for cp in copies():
                cp.wait()

    def kfn(*refs):
        body(*refs)
    sem = pltpu.SemaphoreType.DMA((na, 3))
    out = pl.pallas_call(
        kfn, name=name, grid=(ni, nj),
        in_specs=[pl.BlockSpec((tm, kdim), lambda i, j: (i, 0)), pl.BlockSpec((tn, kdim), lambda i, j: (j, 0))] + [ANY] * na,
        out_specs=[pl.BlockSpec((tm, tn), lambda i, j: (i, j))] + [ANY] * na,
        out_shape=[_sds((m, n), out_dtype)] + [_sds((3,) + s_a.shape[1:], s_a.dtype) for s_a in send],
        scratch_shapes=[sem, sem],
        compiler_params=pltpu.CompilerParams(dimension_semantics=("arbitrary", "arbitrary"),
                                             vmem_limit_bytes=VMEM_LIMIT_V7X, has_side_effects=True))(a, b, *send)
    return out[0], list(out[1:])


def _pad_w_in(w):
    z = jnp.zeros((w.shape[0], 1024 - 848), w.dtype)
    return jnp.concatenate([w[:, 0:832], w[:, 4928:4944], z, w[:, 832:4928], w[:, 4944:5968]], axis=1)


def _unpad_w_in(g):
    return jnp.concatenate([g[:, 0:832], g[:, 1024:5120], g[:, 832:848], g[:, 5120:6144]], axis=1)


W_IN_SHARD = W_IN_COLS // 4
W_IN_RUNS = ((0, 832, 0), (832, 4928, 1024), (4928, 4944, 832), (4944, 5968, 5120))


def _w_in_grad_blocks(p):
    def orig_cols(lo, hi):
        parts = [p[:, pa + max(lo, a) - a:pa + min(hi, b) - a] for a, b, pa in W_IN_RUNS if max(lo, a) < min(hi, b)]
        return parts[0] if len(parts) == 1 else jnp.concatenate(parts, axis=1)
    return jnp.stack([orig_cols(W_IN_SHARD * k, W_IN_SHARD * (k + 1)) for k in range(4)])


def _pad_w_in_blocks(g):
    def orig_cols(lo, hi):
        return [g[k][:, max(lo, W_IN_SHARD * k) - W_IN_SHARD * k:min(hi, W_IN_SHARD * (k + 1)) - W_IN_SHARD * k]
                for k in range(4) if max(lo, W_IN_SHARD * k) < min(hi, W_IN_SHARD * (k + 1))]
    z = jnp.zeros((g[0].shape[0], 1024 - 848), g[0].dtype)
    return jnp.concatenate(orig_cols(0, 832) + orig_cols(4928, 4944) + [z] + orig_cols(832, 4928) + orig_cols(4944, 5968),
                           axis=1)


def _pad_heads(w):
    r = w.shape[0]
    return jnp.pad(w.reshape(r, HEADS, QK_DIM), ((0, 0), (0, 0), (0, QK_PAD - QK_DIM))).reshape(r, HEADS * QK_PAD)


def _unpad_heads(w):
    r = w.shape[0]
    return w.reshape(r, HEADS, QK_PAD)[:, :, :QK_DIM].reshape(r, HEADS * QK_DIM)


def _cols_to_blocks(w):
    r = w.shape[0]
    return w.reshape(r, 4, -1).transpose(1, 0, 2)


def _blocks_to_cols(w):
    return jnp.concatenate(list(w), axis=1)


SMALL_ROWS = {"norm_gain": (0, 2048), "mla_q_a_gain": (16, 512), "mla_kv_a_gain": (20, 256),
              "mla_q_norm_gain": (22, 192), "mla_k_norm_gain": (24, 192), "gdn_a_log": (26, 8),
              "gdn_dt_bias": (27, 8), "gdn_out_norm_gain": (28, 128)}
LOSS_ROW = 29
SMALL_PACK_ROWS = 32
CONV_ROW = 32


def _pack_small(vals, loss=None):
    rows = []
    at = 0
    for name, (row, size) in SMALL_ROWS.items():
        assert row == at
        nr = -(-size // 128)
        rows.append(jnp.pad(vals[name].reshape(-1).astype(F32), (0, nr * 128 - size)).reshape(nr, 128))
        at += nr
    assert at == LOSS_ROW
    if loss is not None:
        rows.append(jnp.pad(loss.reshape(1, 1), ((0, 0), (0, 127))))
        at += 1
    rows.append(jnp.zeros((SMALL_PACK_ROWS - at, 128), F32))
    return jnp.concatenate(rows, axis=0)


def _unpack_small(pack, name):
    row, size = SMALL_ROWS[name]
    nr = -(-size // 128)
    return pack[row:row + nr].reshape(-1)[:size].reshape(1, size)


def _local_step(x, positions, target, norm_gain, w_in_p, q_a_gain, kv_a_gain, w_uq_p, w_ukv, q_norm_gain,
                k_norm_gain, conv_w, a_log, dt_bias, out_gain, w_out, scatter_hook=None, late_weights=None,
                first_weights=None):
    half = HALF_ROPE
    inv_freq = jnp.power(10000.0, -jnp.arange(half, dtype=F32) / half)
    ang = positions.astype(F32)[:, None] * inv_freq
    cos, sin = jnp.cos(ang), jnp.sin(ang)
    zpad = jnp.zeros((x.shape[0], 64), F32)
    cs = jnp.concatenate([cos, cos, zpad], axis=1)
    sn = jnp.concatenate([-sin, sin, zpad], axis=1)
    gq = jnp.pad(q_norm_gain.reshape(1, QK_DIM), ((0, 0), (0, QK_PAD - QK_DIM)))
    gk = jnp.pad(k_norm_gain.reshape(1, QK_DIM), ((0, 0), (0, QK_PAD - QK_DIM)))
    lane_pad = ((0, 0), (GA_LANE, 128 - GA_LANE - HEADS))
    alog128 = jnp.pad(a_log.reshape(1, HEADS), lane_pad)
    dtb128 = jnp.pad(dt_bias.reshape(1, HEADS), lane_pad)
    ng, qag, kvag, og = (norm_gain.reshape(1, -1), q_a_gain.reshape(1, -1), kv_a_gain.reshape(1, -1),
                         out_gain.reshape(1, -1))

    if first_weights is None:
        xn = _norm1_fwd(x, ng)
    else:
        shards, assemble = first_weights
        xn, gathered = _norm1_fwd_gather(x, ng, shards)
        w_in_p, conv_w = assemble(gathered)
    misc = _matmul(xn, w_in_p[:, 768:896], mode="nn", out_dtype=F32, name="in_proj_misc")
    if late_weights is None:
        proj = _matmul(xn, w_in_p, mode="nn", out_dtype=BF16, name="in_proj")
    else:
        shards, assemble = late_weights
        proj, gathered = _matmul_nn_gather(xn, w_in_p, shards, name="in_proj_gather", tm=TILES["mm"], tn=TILES["mm"],
                                           out_dtype=BF16)
        w_uq_p, w_ukv, w_out = assemble(gathered)
    cqn, ckvn = _mla_a_norm(proj, qag, kvag)
    q_pre = _matmul(cqn, w_uq_p, mode="nn", out_dtype=BF16, name="q_up")
    kv_pre = _matmul(ckvn, w_ukv, mode="nn", out_dtype=BF16, name="kv_up")
    q, k, v = _mla_post_fwd(q_pre, kv_pre, misc, cs, sn, gq, gk)
    o_mla, lse = _attn_fwd(q, k, v)
    qkv = _gdn_conv_fwd(proj, conv_w)
    gbeta = _gdn_gates_fwd(misc, alog128, dtb128)
    g_u, g_w, g_qd, g_kd, g_a, g_t, g_gl = _gdn_pre(qkv, gbeta)
    o_gdn, states = _gdn_scan_fwd(g_u, g_w, g_qd, g_kd, g_a, g_gl)
    mixed = _mix_fwd(o_mla, o_gdn, proj, og)
    dy, sq = _out_fwd(mixed, w_out, x, target)

    dmixed = _matmul(dy, w_out, mode="nt", out_dtype=BF16, name="d_mixed")
    d_w_out = _matmul(mixed, dy, mode="tn", out_dtype=F32, name="d_w_out", tk=4096)
    do_mla, do_gdn, dmg, dgg, d_out_gain, delta128 = _mix_bwd(o_mla, o_gdn, proj, og, dmixed)
    s_len = x.shape[0]
    delta_rows = delta128[:, :HEADS].T.reshape(HEADS, 1, s_len)
    dq, dk, dv = _attn_bwd(q, k, v, lse.reshape(HEADS, 1, s_len), delta_rows, do_mla)
    dq_pre, dkv_pre, dkr, d_gq, d_gk = _mla_post_bwd(q_pre, kv_pre, misc, cs, sn, gq, gk, dq, dk, dv)
    d_w_uq_p = _matmul(cqn, dq_pre, mode="tn", out_dtype=F32, name="d_w_uq", tk=1024)
    d_w_ukv = _matmul(ckvn, dkv_pre, mode="tn", out_dtype=F32, name="d_w_ukv", tk=1024)
    dcqn = _matmul(dq_pre, w_uq_p, mode="nt", out_dtype=F32, name="d_cqn")
    dckvn = _matmul(dkv_pre, w_ukv, mode="nt", out_dtype=F32, name="d_ckvn")
    dcq, dckv, d_qag, d_kvag = _mla_a_norm_bwd(proj, qag, kvag, dcqn, dckvn)
    dstates = _gdn_scan_bwd(g_w, g_qd, g_kd, g_a, g_gl, do_gdn)
    dqkv, dgbeta = _gdn_post_bwd(qkv, gbeta, g_u, g_w, g_t, states, dstates, do_gdn)
    dz = _gdn_conv_bwd_a(proj, conv_w, dqkv)
    dgx, d_conv = _gdn_conv_bwd_b(proj, conv_w, dz)
    dmisc, d_alog, d_dtb = _gdn_gates_bwd(misc, alog128, dtb128, gbeta, dgbeta, dkr)
    dproj = jnp.concatenate([dcq, dckv, dmisc, jnp.zeros((x.shape[0], 128), BF16), dmg, dgx, dgg], axis=1)
    d_w_in_p = _matmul(xn, dproj, mode="tn", out_dtype=F32, name="d_w_in", tk=4096)
    big = {"w_in": d_w_in_p, "w_uq": d_w_uq_p, "w_ukv": d_w_ukv, "w_out": d_w_out, "gdn_conv_w": d_conv}
    if scatter_hook is None:
        dxn, received = _matmul(dproj, w_in_p, mode="nt", out_dtype=BF16, name="d_xn", tm=512, tn=512), None
    else:
        dxn, received = _matmul_nt_scatter(dproj, w_in_p, scatter_hook(big), name="d_xn_scatter", tm=512, tn=512,
                                           out_dtype=BF16)
    grad_x, d_ng = _norm1_bwd(x, ng, dxn, dy)

    small = {"norm_gain": d_ng.sum(0), "mla_q_a_gain": d_qag.sum(0), "mla_kv_a_gain": d_kvag.sum(0),
             "mla_q_norm_gain": d_gq.sum(0)[:QK_DIM], "mla_k_norm_gain": d_gk.sum(0)[:QK_DIM],
             "gdn_a_log": d_alog.sum(0)[GA_LANE:GA_LANE + HEADS], "gdn_dt_bias": d_dtb.sum(0)[GA_LANE:GA_LANE + HEADS],
             "gdn_out_norm_gain": d_out_gain.sum(0)}
    return sq, grad_x, small, big, received


WEIGHTS = ["norm_gain", "w_in", "mla_q_a_gain", "mla_kv_a_gain", "w_uq", "w_ukv", "mla_q_norm_gain", "mla_k_norm_gain",
           "gdn_conv_w", "gdn_a_log", "gdn_dt_bias", "gdn_out_norm_gain", "w_out"]
BIG = ["w_in", "w_uq", "w_ukv", "w_out"]


def kernel(x, positions, norm_gain, w_in, mla_q_a_gain, mla_kv_a_gain, w_uq, w_ukv, mla_q_norm_gain, mla_k_norm_gain, gdn_conv_w, gdn_a_log, gdn_dt_bias, gdn_out_norm_gain, w_out, loss_target, m_norm_gain, m_w_in, m_mla_q_a_gain, m_mla_kv_a_gain, m_w_uq, m_w_ukv, m_mla_q_norm_gain, m_mla_k_norm_gain, m_gdn_conv_w, m_gdn_a_log, m_gdn_dt_bias, m_gdn_out_norm_gain, m_w_out, v_norm_gain, v_w_in, v_mla_q_a_gain, v_mla_kv_a_gain, v_w_uq, v_w_ukv, v_mla_q_norm_gain, v_mla_k_norm_gain, v_gdn_conv_w, v_gdn_a_log, v_gdn_dt_bias, v_gdn_out_norm_gain, v_w_out):
    w = dict(norm_gain=norm_gain, w_in=w_in, mla_q_a_gain=mla_q_a_gain, mla_kv_a_gain=mla_kv_a_gain, w_uq=w_uq,
             w_ukv=w_ukv, mla_q_norm_gain=mla_q_norm_gain, mla_k_norm_gain=mla_k_norm_gain, gdn_conv_w=gdn_conv_w,
             gdn_a_log=gdn_a_log, gdn_dt_bias=gdn_dt_bias, gdn_out_norm_gain=gdn_out_norm_gain, w_out=w_out)
    m = dict(norm_gain=m_norm_gain, w_in=m_w_in, mla_q_a_gain=m_mla_q_a_gain, mla_kv_a_gain=m_mla_kv_a_gain,
             w_uq=m_w_uq, w_ukv=m_w_ukv, mla_q_norm_gain=m_mla_q_norm_gain, mla_k_norm_gain=m_mla_k_norm_gain,
             gdn_conv_w=m_gdn_conv_w, gdn_a_log=m_gdn_a_log, gdn_dt_bias=m_gdn_dt_bias,
             gdn_out_norm_gain=m_gdn_out_norm_gain, w_out=m_w_out)
    v = dict(norm_gain=v_norm_gain, w_in=v_w_in, mla_q_a_gain=v_mla_q_a_gain, mla_kv_a_gain=v_mla_kv_a_gain,
             w_uq=v_w_uq, w_ukv=v_w_ukv, mla_q_norm_gain=v_mla_q_norm_gain, mla_k_norm_gain=v_mla_k_norm_gain,
             gdn_conv_w=v_gdn_conv_w, gdn_a_log=v_gdn_a_log, gdn_dt_bias=v_gdn_dt_bias,
             gdn_out_norm_gain=v_gdn_out_norm_gain, w_out=v_w_out)
    k_me = 2 * lax.axis_index("x") + lax.axis_index("y")

    first_weights = ([w_in[0].astype(BF16), gdn_conv_w[0]], lambda g: (_pad_w_in_blocks(g[0]), _blocks_to_cols(g[1])))
    late_weights = ([w_uq[0].astype(BF16), w_ukv[0].astype(BF16), w_out[0].astype(BF16)],
                    lambda g: (_pad_heads(_blocks_to_cols(g[0])), _blocks_to_cols(g[1]), jnp.concatenate(g[2], axis=0)))

    pair_sums = []

    def scatter_hook(big):
        pair, pair_bf16 = _rs_pair_stage([
            _w_in_grad_blocks(big["w_in"]), _cols_to_blocks(_unpad_heads(big["w_uq"])),
            _cols_to_blocks(big["w_ukv"]), big["w_out"].reshape(4, 512, D_MODEL)])
        pair_sums.extend(pair)
        return pair_bf16

    sq, grad_x, small, big, received = _local_step(
        x[0], positions[0], loss_target[0], norm_gain, None, mla_q_a_gain, mla_kv_a_gain, None, None,
        mla_q_norm_gain, mla_k_norm_gain, None, gdn_a_log, gdn_dt_bias, gdn_out_norm_gain, None, scatter_hook,
        late_weights, first_weights)

    loss_local = (0.5 / D_MODEL) * jnp.sum(sq)
    pack = jnp.concatenate([_pack_small(small, loss_local), big["gdn_conv_w"].reshape(96, 128)], axis=0)
    tot = _all_reduce_small(pack)
    loss = tot[LOSS_ROW, 0]
    conv_grad = lax.dynamic_slice_in_dim(tot[CONV_ROW:].reshape(4, 3072), k_me * 768, 768, axis=1)

    shard_grads = _rs_chip_stage(pair_sums, received)

    grads = {n: _unpack_small(tot, n) for n in SMALL_ROWS}
    grads["gdn_conv_w"] = conv_grad[None]
    for n, g in zip(BIG, shard_grads):
        grads[n] = g[None]

    delta, new_m, new_v = {}, {}, {}
    sw = _pack_small({n: w[n] for n in SMALL_ROWS})
    sm = _pack_small({n: m[n] for n in SMALL_ROWS})
    sv = _pack_small({n: v[n] for n in SMALL_ROWS})
    sd, snm, snv = _adamw(sw, tot[:SMALL_PACK_ROWS], sm, sv, "adamw_small")
    for n in SMALL_ROWS:
        delta[n], new_m[n], new_v[n] = _unpack_small(sd, n), _unpack_small(snm, n), _unpack_small(snv, n)
    for n in BIG + ["gdn_conv_w"]:
        if n == "w_in":
            d, nm, nv = _adamw(w[n][0].T, grads[n][0].T, m[n][0].T, v[n][0].T, f"adamw_{n}")
            delta[n], new_m[n], new_v[n] = d.T[None], nm.T[None], nv.T[None]
        else:
            d, nm, nv = _adamw(w[n][0], grads[n][0], m[n][0], v[n][0], f"adamw_{n}")
            delta[n], new_m[n], new_v[n] = d[None], nm[None], nv[None]

    return (loss, grad_x[None], *[grads[n] for n in WEIGHTS], *[delta[n] for n in WEIGHTS],
            *[new_m[n] for n in WEIGHTS], *[new_v[n] for n in WEIGHTS])
```

```python
import functools
import math

import jax
import jax.numpy as jnp
from jax import lax
from jax.experimental import pallas as pl
from jax.experimental.pallas import tpu as pltpu

F32 = jnp.float32
BF16 = jnp.bfloat16
MESH = pl.DeviceIdType.MESH

D_MODEL = 2048
HEADS = 8
HEAD_DIM = 128
QK_DIM = 192
QK_PAD = 256
HALF_ROPE = 32
CHUNK = 64
NORM_EPS = 1e-6
W_IN_COLS = 5968
W_IN_PAD = 6144
GA_LANE = 64
GB_LANE = 72
ADAM_LR, ADAM_B1, ADAM_B2, ADAM_EPS, ADAM_WD, ADAM_STEP = 0.001, 0.9, 0.999, 1e-08, 0.01, 10
VMEM_LIMIT_V7X = 52 * 1024 * 1024
HI = lax.Precision.HIGHEST
NN = (((1,), (0,)), ((), ()))
NT = (((1,), (1,)), ((), ()))
TN = (((0,), (0,)), ((), ()))

TILES = {"row": 512, "attn": 1024, "mm": 1024}


def _call(body, *, name, grid, in_specs, out_specs, out_shape, args, scratch=(), sem=None):
    def kfn(*refs):
        body(*refs)
    if sem is None:
        sem = ("arbitrary",) * len(grid)
    return pl.pallas_call(
        kfn, name=name, grid=grid, in_specs=in_specs, out_specs=out_specs, out_shape=out_shape,
        scratch_shapes=list(scratch),
        compiler_params=pltpu.CompilerParams(dimension_semantics=sem, vmem_limit_bytes=VMEM_LIMIT_V7X),
    )(*args)


def _rows(tm, w, cb=0):
    return pl.BlockSpec((tm, w), lambda i: (i, cb))


def _full(shape):
    n = len(shape)
    return pl.BlockSpec(shape, lambda *_: (0,) * n)


def _sds(shape, dtype):
    return jax.ShapeDtypeStruct(shape, dtype)


def _acc8(x):
    tm, c = x.shape
    return jnp.sum(x.reshape(tm // 8, 8, c), axis=0)


def _sigmoid(x):
    return 1.0 / (1.0 + jnp.exp(-x))


def _silu(x):
    return x * _sigmoid(x)


def _dsilu(x):
    s = _sigmoid(x)
    return s * (1.0 + x * (1.0 - s))


def _dot(a, b, dims=NN):
    return lax.dot_general(a.astype(BF16), b.astype(BF16), dims, preferred_element_type=F32)


def _dot_hi(a, b, dims=NN):
    return lax.dot_general(a, b, dims, precision=HI, preferred_element_type=F32)


def _matmul(a, b, *, mode, out_dtype, name, tm=None, tn=None, tk=None):
    if mode == "tn":
        kdim, m = a.shape
    else:
        m, kdim = a.shape
    n = b.shape[0] if mode == "nt" else b.shape[1]
    tm = min(tm or TILES["mm"], m)
    tn = min(tn or TILES["mm"], n)
    tk = min(tk or kdim, kdim)
    nk = kdim // tk
    dims = {"nn": NN, "nt": NT, "tn": TN}[mode]
    if mode == "tn":
        a_spec = pl.BlockSpec((tk, tm), lambda i, j, k: (k, i))
    else:
        a_spec = pl.BlockSpec((tm, tk), lambda i, j, k: (i, k))
    if mode == "nt":
        b_spec = pl.BlockSpec((tn, tk), lambda i, j, k: (j, k))
    else:
        b_spec = pl.BlockSpec((tk, tn), lambda i, j, k: (k, j))

    def body(a_ref, b_ref, o_ref):
        r = _dot(a_ref[...], b_ref[...], dims)
        if nk == 1:
            o_ref[...] = r.astype(o_ref.dtype)
        else:
            k = pl.program_id(2)

            @pl.when(k == 0)
            def _():
                o_ref[...] = r

            @pl.when(k > 0)
            def _():
                o_ref[...] += r

    if nk > 1:
        assert out_dtype == F32
    return _call(body, name=name, grid=(m // tm, n // tn, nk), in_specs=[a_spec, b_spec],
                 out_specs=pl.BlockSpec((tm, tn), lambda i, j, k: (i, j)), out_shape=_sds((m, n), out_dtype),
                 args=(a, b))


def _norm1_fwd(x, gain):
    s = x.shape[0]
    tm = min(TILES["row"], s)

    def body(x_ref, g_ref, o_ref):
        xv = x_ref[...]
        r = lax.rsqrt(jnp.mean(xv * xv, axis=-1, keepdims=True) + NORM_EPS)
        o_ref[...] = (xv * r * g_ref[...]).astype(BF16)

    return _call(body, name="norm1_fwd", grid=(s // tm,), in_specs=[_rows(tm, D_MODEL), _full((1, D_MODEL))],
                 out_specs=_rows(tm, D_MODEL), out_shape=_sds((s, D_MODEL), BF16), args=(x, gain))


def _norm1_bwd(x, gain, dxn, dy):
    s = x.shape[0]
    tm = min(TILES["row"], s)

    def body(x_ref, g_ref, dxn_ref, dy_ref, gx_ref, dg_ref):
        xv = x_ref[...]
        r = lax.rsqrt(jnp.mean(xv * xv, axis=-1, keepdims=True) + NORM_EPS)
        nrm = xv * r
        d = dxn_ref[...].astype(F32)
        dn = d * g_ref[...]
        gx_ref[...] = dy_ref[...].astype(F32) + r * (dn - nrm * jnp.mean(dn * nrm, axis=-1, keepdims=True))

        @pl.when(pl.program_id(0) == 0)
        def _():
            dg_ref[...] = jnp.zeros_like(dg_ref)

        dg_ref[...] += _acc8(d * nrm)

    return _call(body, name="norm1_bwd", grid=(s // tm,),
                 in_specs=[_rows(tm, D_MODEL), _full((1, D_MODEL)), _rows(tm, D_MODEL), _rows(tm, D_MODEL)],
                 out_specs=[_rows(tm, D_MODEL), _full((8, D_MODEL))],
                 out_shape=[_sds((s, D_MODEL), F32), _sds((8, D_MODEL), F32)], args=(x, gain, dxn, dy))


def _rms(xv, width):
    return lax.rsqrt(jnp.sum(xv * xv, axis=-1, keepdims=True) * (1.0 / width) + NORM_EPS)


def _mla_a_norm(proj, gq, gkv):
    s = proj.shape[0]
    tm = min(TILES["row"], s)

    def body(cq_ref, ckv_ref, gq_ref, gkv_ref, oq_ref, okv_ref):
        a = cq_ref[...].astype(F32)
        oq_ref[...] = (a * _rms(a, 512) * gq_ref[...]).astype(BF16)
        b = ckv_ref[...].astype(F32)
        okv_ref[...] = (b * _rms(b, 256) * gkv_ref[...]).astype(BF16)

    return _call(body, name="mla_a_norm", grid=(s // tm,),
                 in_specs=[_rows(tm, 512, 0), _rows(tm, 256, 2), _full((1, 512)), _full((1, 256))],
                 out_specs=[_rows(tm, 512), _rows(tm, 256)],
                 out_shape=[_sds((s, 512), BF16), _sds((s, 256), BF16)], args=(proj, proj, gq, gkv))


def _rms_bwd(xv, gain, d, width):
    r = _rms(xv, width)
    nrm = xv * r
    dn = d * gain
    dx = r * (dn - nrm * (jnp.sum(dn * nrm, axis=-1, keepdims=True) * (1.0 / width)))
    return dx, d * nrm


def _mla_a_norm_bwd(proj, gq, gkv, dcqn, dckvn):
    s = proj.shape[0]
    tm = min(TILES["row"], s)

    def body(cq_ref, ckv_ref, gq_ref, gkv_ref, dq_ref, dkv_ref, oq_ref, okv_ref, aq_ref, akv_ref):
        dxq, gq_part = _rms_bwd(cq_ref[...].astype(F32), gq_ref[...], dq_ref[...].astype(F32), 512)
        dxk, gk_part = _rms_bwd(ckv_ref[...].astype(F32), gkv_ref[...], dkv_ref[...].astype(F32), 256)
        oq_ref[...] = dxq.astype(BF16)
        okv_ref[...] = dxk.astype(BF16)

        @pl.when(pl.program_id(0) == 0)
        def _():
            aq_ref[...] = jnp.zeros_like(aq_ref)
            akv_ref[...] = jnp.zeros_like(akv_ref)

        aq_ref[...] += _acc8(gq_part)
        akv_ref[...] += _acc8(gk_part)

    return _call(body, name="mla_a_norm_bwd", grid=(s // tm,),
                 in_specs=[_rows(tm, 512, 0), _rows(tm, 256, 2), _full((1, 512)), _full((1, 256)),
                           _rows(tm, 512), _rows(tm, 256)],
                 out_specs=[_rows(tm, 512), _rows(tm, 256), _full((8, 512)), _full((8, 256))],
                 out_shape=[_sds((s, 512), BF16), _sds((s, 256), BF16), _sds((8, 512), F32), _sds((8, 256), F32)],
                 args=(proj, proj, gq, gkv, dcqn, dckvn))


def _swap32(r):
    lane = lax.broadcasted_iota(jnp.int32, r.shape, 1)
    return jnp.where(lane < HALF_ROPE, pltpu.roll(r, 128 - HALF_ROPE, 1), pltpu.roll(r, HALF_ROPE, 1))


def _mla_post_fwd(q_pre, kv_pre, proj, cs, sn, gq, gk):
    s = q_pre.shape[0]
    tm = min(TILES["row"], s)

    def body(qp_ref, kvp_ref, misc_ref, cs_ref, sn_ref, gq_ref, gk_ref, q_ref, k_ref, v_ref):
        csv, snv = cs_ref[...], sn_ref[...]
        lane = lax.broadcasted_iota(jnp.int32, (tm, 128), 1)
        kr = jnp.where(lane < 64, misc_ref[...], 0.0)
        for h in range(HEADS):
            for src, g_ref, o_ref in ((None, gq_ref, q_ref), (kr, gk_ref, k_ref)):
                if src is None:
                    xv = qp_ref[:, QK_PAD * h:QK_PAD * (h + 1)].astype(F32)
                else:
                    xv = jnp.concatenate([kvp_ref[:, 256 * h:256 * h + 128].astype(F32), src], axis=-1)
                y = xv * _rms(xv, QK_DIM) * g_ref[...]
                if src is None:
                    y = y * Q_PRESCALE
                hi = y[:, 128:]
                hi = hi * csv + _swap32(hi) * snv
                o_ref[:, QK_PAD * h:QK_PAD * h + 128] = y[:, :128].astype(BF16)
                o_ref[:, QK_PAD * h + 128:QK_PAD * (h + 1)] = hi.astype(BF16)
            v_ref[:, 128 * h:128 * (h + 1)] = kvp_ref[:, 256 * h + 128:256 * (h + 1)].astype(BF16)

    return _call(body, name="mla_post_fwd", grid=(s // tm,),
                 in_specs=[_rows(tm, 2048), _rows(tm, 2048), _rows(tm, 128), _rows(tm, 128), _rows(tm, 128),
                           _full((1, QK_PAD)), _full((1, QK_PAD))],
                 out_specs=[_rows(tm, 2048), _rows(tm, 2048), _rows(tm, 1024)],
                 out_shape=[_sds((s, 2048), BF16), _sds((s, 2048), BF16), _sds((s, 1024), BF16)],
                 args=(q_pre, kv_pre, proj, cs, sn, gq, gk))


def _mla_post_bwd(q_pre, kv_pre, proj, cs, sn, gq, gk, dq, dk, dv):
    s = q_pre.shape[0]
    tm = min(TILES["row"], s)

    def body(qp_ref, kvp_ref, misc_ref, cs_ref, sn_ref, gq_ref, gk_ref, dq_ref, dk_ref, dv_ref,
             oq_ref, okv_ref, okr_ref, agq_ref, agk_ref):
        csv, snv = cs_ref[...], sn_ref[...]
        lane = lax.broadcasted_iota(jnp.int32, (tm, 128), 1)
        kr = jnp.where(lane < 64, misc_ref[...], 0.0)

        @pl.when(pl.program_id(0) == 0)
        def _():
            agq_ref[...] = jnp.zeros_like(agq_ref)
            agk_ref[...] = jnp.zeros_like(agk_ref)

        dkr = jnp.zeros((tm, 128), F32)
        for h in range(HEADS):
            for which in (0, 1):
                if which == 0:
                    xv = qp_ref[:, QK_PAD * h:QK_PAD * (h + 1)].astype(F32)
                    d_ref, g_ref, a_ref = dq_ref, gq_ref, agq_ref
                else:
                    xv = jnp.concatenate([kvp_ref[:, 256 * h:256 * h + 128].astype(F32), kr], axis=-1)
                    d_ref, g_ref, a_ref = dk_ref, gk_ref, agk_ref
                dhi = d_ref[:, QK_PAD * h + 128:QK_PAD * (h + 1)]
                dhi = dhi * csv - _swap32(dhi) * snv
                dyv = jnp.concatenate([d_ref[:, QK_PAD * h:QK_PAD * h + 128], dhi], axis=-1)
                if which == 0:
                    dyv = dyv * ATTN_SCALE
                dx, gpart = _rms_bwd(xv, g_ref[...], dyv, QK_DIM)
                a_ref[...] += _acc8(gpart)
                if which == 0:
                    oq_ref[:, QK_PAD * h:QK_PAD * (h + 1)] = dx.astype(BF16)
                else:
                    okv_ref[:, 256 * h:256 * h + 128] = dx[:, :128].astype(BF16)
                    dkr = dkr + dx[:, 128:]
            okv_ref[:, 256 * h + 128:256 * (h + 1)] = dv_ref[:, 128 * h:128 * (h + 1)].astype(BF16)
        okr_ref[...] = dkr

    return _call(body, name="mla_post_bwd", grid=(s // tm,),
                 in_specs=[_rows(tm, 2048), _rows(tm, 2048), _rows(tm, 128), _rows(tm, 128), _rows(tm, 128),
                           _full((1, QK_PAD)), _full((1, QK_PAD)), _rows(tm, 2048), _rows(tm, 2048), _rows(tm, 1024)],
                 out_specs=[_rows(tm, 2048), _rows(tm, 2048), _rows(tm, 128), _full((8, QK_PAD)), _full((8, QK_PAD))],
                 out_shape=[_sds((s, 2048), BF16), _sds((s, 2048), BF16), _sds((s, 128), F32),
                            _sds((8, QK_PAD), F32), _sds((8, QK_PAD), F32)],
                 args=(q_pre, kv_pre, proj, cs, sn, gq, gk, dq, dk, dv))


ATTN_SCALE = QK_DIM ** -0.5
NEG = -1e30


LOG2E = 1.4426950408889634
LN2 = 0.6931471805599453
Q_PRESCALE = ATTN_SCALE * LOG2E
ATTN_SUB_FWD = 256
ATTN_SUB_BWD = 256


def _causal_pairs(nq, kv_major):
    prs = [(i, j) for i in range(nq) for j in range(i + 1)]
    if kv_major:
        prs.sort(key=lambda ij: (ij[1], ij[0]))
    return (jnp.asarray([p[0] for p in prs], jnp.int32), jnp.asarray([p[1] for p in prs], jnp.int32))


def _pair_call(body, *, name, tables, in_specs, out_specs, out_shape, scratch, args):
    def kfn(*refs):
        body(*refs)
    spec = pltpu.PrefetchScalarGridSpec(num_scalar_prefetch=2, grid=(HEADS, tables[0].shape[0]), in_specs=in_specs,
                                        out_specs=out_specs, scratch_shapes=list(scratch))
    return pl.pallas_call(
        kfn, name=name, grid_spec=spec, out_shape=out_shape,
        compiler_params=pltpu.CompilerParams(dimension_semantics=("parallel", "arbitrary"),
                                             vmem_limit_bytes=VMEM_LIMIT_V7X))(*tables, *args)


def _diag_mask(sc, ts, qs):
    row = lax.broadcasted_iota(jnp.int32, sc.shape, 0) + qs * ts
    col = lax.broadcasted_iota(jnp.int32, sc.shape, 1)
    return jnp.where(col <= row, sc, NEG)


def _attn_fwd(q, k, v):
    s = q.shape[0]
    t = min(TILES["attn"], s)
    ts = min(ATTN_SUB_FWD, t)
    nq = s // t

    def slabs(q_ref, k_ref, v_ref, m_s, l_s, acc_s, diag):
        def scores(qs):
            kw = (qs + 1) * ts if diag else t
            sc = lax.dot_general(q_ref[qs * ts:(qs + 1) * ts, :], k_ref[0:kw, :], NT, preferred_element_type=F32)
            return _diag_mask(sc, ts, qs) if diag else sc

        nsub = t // ts
        sc_next = scores(0)
        for qs in range(nsub):
            rq = slice(qs * ts, (qs + 1) * ts)
            kw = (qs + 1) * ts if diag else t
            sc = sc_next
            if qs + 1 < nsub:
                sc_next = scores(qs + 1)
            m_prev = m_s[rq, :]
            m_new = jnp.maximum(m_prev, jnp.max(sc, axis=-1, keepdims=True))
            p = jnp.exp2(sc - m_new)
            alpha = jnp.exp2(m_prev - m_new)
            l_s[rq, :] = alpha * l_s[rq, :] + jnp.sum(p, axis=-1, keepdims=True)
            acc_s[rq, :] = acc_s[rq, :] * alpha + lax.dot_general(p.astype(BF16), v_ref[0:kw, :], NN,
                                                                  preferred_element_type=F32)
            m_s[rq, :] = m_new

    def body(it_ref, jt_ref, q_ref, k_ref, v_ref, o_ref, lse_ref, m_s, l_s, acc_s):
        p = pl.program_id(1)
        i, j = it_ref[p], jt_ref[p]

        @pl.when(j == 0)
        def _():
            m_s[...] = jnp.full_like(m_s, NEG)
            l_s[...] = jnp.zeros_like(l_s)
            acc_s[...] = jnp.zeros_like(acc_s)

        @pl.when(j < i)
        def _():
            slabs(q_ref, k_ref, v_ref, m_s, l_s, acc_s, False)

        @pl.when(j == i)
        def _():
            slabs(q_ref, k_ref, v_ref, m_s, l_s, acc_s, True)
            o_ref[...] = acc_s[...] / l_s[...]
            lse_ref[...] = m_s[...] + jnp.log2(l_s[...])

    qb = lambda h, p, it, jt: (it[p], h)
    kb = lambda h, p, it, jt: (jt[p], h)
    return _pair_call(
        body, name="attn_fwd", tables=_causal_pairs(nq, kv_major=False),
        in_specs=[pl.BlockSpec((t, QK_PAD), qb), pl.BlockSpec((t, QK_PAD), kb), pl.BlockSpec((t, HEAD_DIM), kb)],
        out_specs=[pl.BlockSpec((t, HEAD_DIM), qb),
                   pl.BlockSpec((None, t, 1), lambda h, p, it, jt: (h, it[p], 0))],
        out_shape=[_sds((s, HEADS * HEAD_DIM), F32), _sds((HEADS, s, 1), F32)],
        scratch=[pltpu.VMEM((t, 1), F32), pltpu.VMEM((t, 1), F32), pltpu.VMEM((t, HEAD_DIM), F32)],
        args=(q, k, v))


def _attn_bwd(q, k, v, lse_rows, delta_rows, do):
    s = q.shape[0]
    t = min(TILES["attn"], s)
    ts = min(ATTN_SUB_BWD, t)
    nq = s // t

    def slabs(q_ref, k_ref, v_ref, lse_ref, delta_ref, do_ref, dq_ref, dk_ref, dv_ref, i, diag):
        def products(qs):
            rq = slice(qs * ts, (qs + 1) * ts)
            kw = (qs + 1) * ts if diag else t
            qv, dob = q_ref[rq, :], do_ref[rq, :]
            sct = lax.dot_general(k_ref[0:kw, :], qv, NT, preferred_element_type=F32)
            dpt = lax.dot_general(v_ref[0:kw, :], dob, NT, preferred_element_type=F32)
            if diag:
                row = lax.broadcasted_iota(jnp.int32, sct.shape, 0)
                col = lax.broadcasted_iota(jnp.int32, sct.shape, 1) + qs * ts
                sct = jnp.where(row <= col, sct, NEG)
            return qv, dob, sct, dpt

        nsub = t // ts
        ahead = products(0)
        for qs in range(nsub):
            rq = slice(qs * ts, (qs + 1) * ts)
            kw = (qs + 1) * ts if diag else t
            qv, dob, sct, dpt = ahead
            if qs + 1 < nsub:
                ahead = products(qs + 1)
            pt = jnp.exp2(sct - lse_ref[:, rq])
            dv_ref[0:kw, :] += lax.dot_general(pt.astype(BF16), dob, NN, preferred_element_type=F32)
            dst = (pt * (dpt - delta_ref[:, rq])).astype(BF16)
            dk_ref[0:kw, :] += lax.dot_general(dst, qv, NN, preferred_element_type=F32)
            rows = pl.ds(pl.multiple_of(i * t + qs * ts, ts), ts)
            dq_ref[rows, :] += lax.dot_general(dst, k_ref[0:kw, :], TN, preferred_element_type=F32)

    def body(it_ref, jt_ref, q_ref, k_ref, v_ref, lse_ref, delta_ref, do_ref, dq_ref, dk_ref, dv_ref):
        p = pl.program_id(1)
        i, j = it_ref[p], jt_ref[p]
        refs = (q_ref, k_ref, v_ref, lse_ref, delta_ref, do_ref, dq_ref, dk_ref, dv_ref)

        @pl.when(p == 0)
        def _():
            dq_ref[...] = jnp.zeros_like(dq_ref)

        @pl.when(i == j)
        def _():
            dk_ref[...] = jnp.zeros_like(dk_ref)
            dv_ref[...] = jnp.zeros_like(dv_ref)

        @pl.when(i > j)
        def _():
            slabs(*refs, i, False)

        @pl.when(i == j)
        def _():
            slabs(*refs, i, True)

        @pl.when(i == nq - 1)
        def _():
            dk_ref[...] = dk_ref[...] * LN2

    qb = lambda h, p, it, jt: (it[p], h)
    kb = lambda h, p, it, jt: (jt[p], h)
    rowb = pl.BlockSpec((None, 1, t), lambda h, p, it, jt: (h, 0, it[p]))
    return _pair_call(
        body, name="attn_bwd", tables=_causal_pairs(nq, kv_major=True),
        in_specs=[pl.BlockSpec((t, QK_PAD), qb), pl.BlockSpec((t, QK_PAD), kb), pl.BlockSpec((t, HEAD_DIM), kb),
                  rowb, rowb, pl.BlockSpec((t, HEAD_DIM), qb)],
        out_specs=[pl.BlockSpec((s, QK_PAD), lambda h, p, it, jt: (0, h)), pl.BlockSpec((t, QK_PAD), kb),
                   pl.BlockSpec((t, HEAD_DIM), kb)],
        out_shape=[_sds((s, HEADS * QK_PAD), F32), _sds((s, HEADS * QK_PAD), F32), _sds((s, HEADS * HEAD_DIM), F32)],
        scratch=(), args=(q, k, v, lse_rows, delta_rows, do))


GDN_Q_SCALE = HEAD_DIM ** -0.5


def _shift_down(xv, prev8, sft):
    rolled = pltpu.roll(xv, sft, 0)
    top = pltpu.roll(jnp.concatenate([prev8, xv[:8]], axis=0), sft, 0)[8:]
    return jnp.concatenate([top, rolled[8:]], axis=0)


def _shift_up(xv, next8, sft):
    tm = xv.shape[0]
    rolled = pltpu.roll(xv, tm - sft, 0)
    bot = pltpu.roll(jnp.concatenate([xv[tm - 8:], next8], axis=0), 16 - sft, 0)[:8]
    return jnp.concatenate([rolled[:tm - 8], bot], axis=0)


def _conv_z(xv, prev8, w_ref):
    z = xv * w_ref[3:4, :]
    for sft in (1, 2, 3):
        z = z + _shift_down(xv, prev8, sft) * w_ref[3 - sft:4 - sft, :]
    return z


def _conv_specs(s, tm):
    nb16 = tm // 16
    cur = pl.BlockSpec((tm, 1024), lambda j, i: (i, 2 + j))
    prev = pl.BlockSpec((16, 1024), lambda j, i: (jnp.maximum(i * nb16 - 1, 0), 2 + j))
    return cur, prev


def _prev8(xp_ref, i):
    return jnp.where(i > 0, xp_ref[...].astype(F32)[8:], 0.0)


def _gdn_conv_fwd(proj, conv_w):
    s = proj.shape[0]
    tm = min(TILES["row"], s)
    cur, prev = _conv_specs(s, tm)

    def body(x_ref, xp_ref, w_ref, o_ref):
        j, i = pl.program_id(0), pl.program_id(1)
        a = _silu(_conv_z(x_ref[...].astype(F32), _prev8(xp_ref, i), w_ref))
        qk_scale = jnp.where(j == 0, GDN_Q_SCALE, 1.0)
        for h in range(HEADS):
            seg = a[:, 128 * h:128 * (h + 1)]
            r = lax.rsqrt(jnp.sum(seg * seg, axis=-1, keepdims=True) + NORM_EPS)
            o_ref[:, 128 * h:128 * (h + 1)] = jnp.where(j < 2, seg * r * qk_scale, seg)

    return _call(body, name="gdn_conv_fwd", grid=(3, s // tm),
                 in_specs=[cur, prev, pl.BlockSpec((4, 1024), lambda j, i: (0, j))],
                 out_specs=pl.BlockSpec((tm, 1024), lambda j, i: (i, j)), out_shape=_sds((s, 3072), F32),
                 args=(proj, proj, conv_w))


def _gdn_conv_bwd_a(proj, conv_w, dqkv):
    s = proj.shape[0]
    tm = min(TILES["row"], s)
    cur, prev = _conv_specs(s, tm)

    def body(x_ref, xp_ref, w_ref, d_ref, o_ref):
        j, i = pl.program_id(0), pl.program_id(1)
        z = _conv_z(x_ref[...].astype(F32), _prev8(xp_ref, i), w_ref)
        a = _silu(z)
        dsl = _dsilu(z)
        qk_scale = jnp.where(j == 0, GDN_Q_SCALE, 1.0)
        for h in range(HEADS):
            sl = slice(128 * h, 128 * (h + 1))
            seg = a[:, sl]
            dyv = d_ref[:, sl]
            r = lax.rsqrt(jnp.sum(seg * seg, axis=-1, keepdims=True) + NORM_EPS)
            yh = seg * r
            da_n = qk_scale * r * (dyv - yh * jnp.sum(yh * dyv, axis=-1, keepdims=True))
            o_ref[:, sl] = jnp.where(j < 2, da_n, dyv) * dsl[:, sl]

    return _call(body, name="gdn_conv_bwd_a", grid=(3, s // tm),
                 in_specs=[cur, prev, pl.BlockSpec((4, 1024), lambda j, i: (0, j)),
                           pl.BlockSpec((tm, 1024), lambda j, i: (i, j))],
                 out_specs=pl.BlockSpec((tm, 1024), lambda j, i: (i, j)), out_shape=_sds((s, 3072), F32),
                 args=(proj, proj, conv_w, dqkv))


def _gdn_conv_bwd_b(proj, conv_w, dz):
    s = proj.shape[0]
    tm = min(TILES["row"], s)
    nb8 = tm // 8
    last8 = s // 8 - 1
    cur, prev = _conv_specs(s, tm)

    def body(x_ref, w_ref, dz_ref, dzn_ref, dx_ref, dw_ref):
        i = pl.program_id(1)
        next8 = jnp.where(i < pl.num_programs(1) - 1, dzn_ref[...], 0.0)
        xv, dzv = x_ref[...].astype(F32), dz_ref[...]

        @pl.when(i == 0)
        def _():
            dw_ref[...] = jnp.zeros_like(dw_ref)

        dx = dzv * w_ref[3:4, :]
        dw_ref[3:4, :] += jnp.sum(dzv * xv, axis=0, keepdims=True)
        for sft in (1, 2, 3):
            up = _shift_up(dzv, next8, sft)
            dx = dx + up * w_ref[3 - sft:4 - sft, :]
            dw_ref[3 - sft:4 - sft, :] += jnp.sum(up * xv, axis=0, keepdims=True)
        dx_ref[...] = dx.astype(BF16)

    return _call(body, name="gdn_conv_bwd_b", grid=(3, s // tm),
                 in_specs=[cur, pl.BlockSpec((4, 1024), lambda j, i: (0, j)),
                           pl.BlockSpec((tm, 1024), lambda j, i: (i, j)),
                           pl.BlockSpec((8, 1024), lambda j, i: (jnp.minimum((i + 1) * nb8, last8), j))],
                 out_specs=[pl.BlockSpec((tm, 1024), lambda j, i: (i, j)), pl.BlockSpec((4, 1024), lambda j, i: (0, j))],
                 out_shape=[_sds((s, 3072), BF16), _sds((4, 3072), F32)], args=(proj, conv_w, dz, dz))


def _softplus(xv):
    return jnp.maximum(xv, 0.0) + jnp.log(1.0 + jnp.exp(-jnp.abs(xv)))


def _gdn_gates_fwd(proj, alog128, dtb128):
    s = proj.shape[0]
    tm = min(TILES["row"], s)

    def body(m_ref, a_ref, b_ref, o_ref):
        mv = m_ref[...]
        lane = lax.broadcasted_iota(jnp.int32, mv.shape, 1)
        g = -jnp.exp(a_ref[...]) * _softplus(mv + b_ref[...])
        is_g = (lane >= GA_LANE) & (lane < GA_LANE + HEADS)
        is_b = (lane >= GB_LANE) & (lane < GB_LANE + HEADS)
        o_ref[...] = jnp.where(is_g, g, jnp.where(is_b, _sigmoid(mv), 0.0))

    return _call(body, name="gdn_gates_fwd", grid=(s // tm,),
                 in_specs=[_rows(tm, 128), _full((1, 128)), _full((1, 128))],
                 out_specs=_rows(tm, 128), out_shape=_sds((s, 128), F32), args=(proj, alog128, dtb128))


def _gdn_gates_bwd(proj, alog128, dtb128, gbeta, dgbeta, dkr):
    s = proj.shape[0]
    tm = min(TILES["row"], s)

    def body(m_ref, a_ref, b_ref, gb_ref, d_ref, kr_ref, o_ref, da_ref, db_ref):
        mv, dv = m_ref[...], d_ref[...]
        lane = lax.broadcasted_iota(jnp.int32, mv.shape, 1)
        is_g = (lane >= GA_LANE) & (lane < GA_LANE + HEADS)
        is_b = (lane >= GB_LANE) & (lane < GB_LANE + HEADS)
        dga = jnp.where(is_g, dv * (-jnp.exp(a_ref[...])) * _sigmoid(mv + b_ref[...]), 0.0)
        beta = gb_ref[...]
        dgb = jnp.where(is_b, dv * beta * (1.0 - beta), 0.0)
        o_ref[...] = jnp.where(lane < 64, kr_ref[...], dga + dgb).astype(BF16)

        @pl.when(pl.program_id(0) == 0)
        def _():
            da_ref[...] = jnp.zeros_like(da_ref)
            db_ref[...] = jnp.zeros_like(db_ref)

        da_ref[...] += _acc8(jnp.where(is_g, dv * gb_ref[...], 0.0))
        db_ref[...] += _acc8(dga)

    return _call(body, name="gdn_gates_bwd", grid=(s // tm,),
                 in_specs=[_rows(tm, 128), _full((1, 128)), _full((1, 128)), _rows(tm, 128), _rows(tm, 128),
                           _rows(tm, 128)],
                 out_specs=[_rows(tm, 128), _full((8, 128)), _full((8, 128))],
                 out_shape=[_sds((s, 128), BF16), _sds((8, 128), F32), _sds((8, 128), F32)],
                 args=(proj, alog128, dtb128, gbeta, dgbeta, dkr))


def _col(mat, lane_idx, lane):
    return jnp.sum(jnp.where(lane == lane_idx, mat, 0.0), axis=-1, keepdims=True)


def _chunk_local(qh, kh, vh, gcol, bcol, ii, jj):
    lower, strict, eye = ii >= jj, ii > jj, ii == jj
    grow = jnp.sum(jnp.where(eye, gcol, 0.0), axis=0, keepdims=True)
    decay = jnp.where(lower, jnp.exp(jnp.where(lower, gcol - grow, 0.0)), 0.0)
    kb = kh * bcol
    vb = vh * bcol
    mm = _dot(kb, kh, NT)
    lmat = jnp.where(strict, mm * decay, 0.0)
    pw = -lmat
    tinv = jnp.where(eye, 1.0, 0.0) + pw
    for _ in range(5):
        pw = _dot_hi(pw, pw)
        tinv = tinv + _dot_hi(tinv, pw)
    egc = jnp.exp(gcol)
    kbg = kb * egc
    rhs = jnp.concatenate([vb, kbg], axis=-1)
    sol = _dot_hi(tinv, rhs)
    qk = _dot(qh, kh, NT)
    glast = jnp.sum(jnp.where(ii[:, :1] == CHUNK - 1, gcol, 0.0), axis=0, keepdims=True)
    ekd = jnp.exp(glast - gcol)
    return dict(decay=decay, kb=kb, vb=vb, mm=mm, lmat=lmat, tinv=tinv, egc=egc, kbg=kbg, rhs=rhs,
                u=sol[:, :HEAD_DIM], w=sol[:, HEAD_DIM:], qk=qk, amat=qk * decay, qd=qh * egc, ekd=ekd,
                kd=kh * ekd, gl=jnp.exp(glast), strict=strict, lower=lower, eye=eye)


def _tri(ii, jj):
    return jnp.where(ii >= jj, 1.0, 0.0)


def _gdn_fwd(qkv, gbeta):
    s = qkv.shape[0]
    n = s // CHUNK

    def body(qkv_ref, gb_ref, o_ref, st_ref, state):
        @pl.when(pl.program_id(0) == 0)
        def _():
            state[...] = jnp.zeros_like(state)

        ii = lax.broadcasted_iota(jnp.int32, (CHUNK, CHUNK), 0)
        jj = lax.broadcasted_iota(jnp.int32, (CHUNK, CHUNK), 1)
        lane = lax.broadcasted_iota(jnp.int32, (CHUNK, 128), 1)
        gbv = gb_ref[...]
        gc = _dot_hi(_tri(ii, jj), gbv)
        for h in range(HEADS):
            sl = slice(128 * h, 128 * (h + 1))
            qh = qkv_ref[:, 128 * h:128 * (h + 1)]
            kh = qkv_ref[:, 1024 + 128 * h:1024 + 128 * (h + 1)]
            vh = qkv_ref[:, 2048 + 128 * h:2048 + 128 * (h + 1)]
            c = _chunk_local(qh, kh, vh, _col(gc, GA_LANE + h, lane), _col(gbv, GB_LANE + h, lane), ii, jj)
            st = state[sl, :]
            st_ref[sl, :] = st
            vn = c["u"] - _dot(c["w"], st)
            o_ref[:, sl] = _dot(c["qd"], st) + _dot(c["amat"], vn)
            state[sl, :] = st * c["gl"] + _dot(c["kd"], vn, TN)

    return _call(body, name="gdn_fwd", grid=(n,),
                 in_specs=[_rows(CHUNK, 3072), _rows(CHUNK, 128)],
                 out_specs=[_rows(CHUNK, 1024), _rows(HEADS * 128, 128)],
                 out_shape=[_sds((s, 1024), F32), _sds((n * HEADS * 128, 128), F32)],
                 scratch=[pltpu.VMEM((HEADS * 128, 128), F32)], args=(qkv, gbeta))


def _gdn_bwd(qkv, gbeta, states, do):
    s = qkv.shape[0]
    n = s // CHUNK

    def body(qkv_ref, gb_ref, st_ref, do_ref, dqkv_ref, dgb_ref, dstate):
        @pl.when(pl.program_id(0) == 0)
        def _():
            dstate[...] = jnp.zeros_like(dstate)

        ii = lax.broadcasted_iota(jnp.int32, (CHUNK, CHUNK), 0)
        jj = lax.broadcasted_iota(jnp.int32, (CHUNK, CHUNK), 1)
        lane = lax.broadcasted_iota(jnp.int32, (CHUNK, 128), 1)
        row1 = ii[:, :1]
        gbv = gb_ref[...]
        gc = _dot_hi(_tri(ii, jj), gbv)
        dgc_all = jnp.zeros((CHUNK, 128), F32)
        db_all = jnp.zeros((CHUNK, 128), F32)
        for h in range(HEADS):
            sl = slice(128 * h, 128 * (h + 1))
            qh = qkv_ref[:, 128 * h:128 * (h + 1)]
            kh = qkv_ref[:, 1024 + 128 * h:1024 + 128 * (h + 1)]
            vh = qkv_ref[:, 2048 + 128 * h:2048 + 128 * (h + 1)]
            bcol = _col(gbv, GB_LANE + h, lane)
            c = _chunk_local(qh, kh, vh, _col(gc, GA_LANE + h, lane), bcol, ii, jj)
            st = st_ref[sl, :]
            dst = dstate[sl, :]
            dov = do_ref[:, sl]
            vn = c["u"] - _dot(c["w"], st)
            dvn = _dot(c["amat"], dov, TN) + _dot(c["kd"], dst)
            damat = jnp.where(c["lower"], _dot(dov, vn, NT), 0.0)
            dqd = _dot(dov, st, NT)
            dkd = _dot(vn, dst, NT)
            dw = -_dot(dvn, st, NT)
            dgl = jnp.sum(jnp.sum(st * dst, axis=-1, keepdims=True), axis=0, keepdims=True)
            dstate[sl, :] = _dot(c["qd"], dov, TN) + c["gl"] * dst - _dot(c["w"], dvn, TN)
            dsol = jnp.concatenate([dvn, dw], axis=-1)
            drhs = _dot_hi(c["tinv"], dsol, TN)
            dtinv = _dot_hi(dsol, c["rhs"], NT)
            dl = -_dot_hi(_dot_hi(c["tinv"], dtinv, TN), c["tinv"], NT)
            dl = jnp.where(c["strict"], dl, 0.0)
            dmm = dl * c["decay"]
            dqk = damat * c["decay"]
            wmat = dl * c["lmat"] + damat * c["amat"]
            dgc = jnp.sum(wmat, axis=-1, keepdims=True)
            wcol = jnp.sum(wmat, axis=0, keepdims=True)
            dgc = dgc - jnp.sum(jnp.where(c["eye"], wcol, 0.0), axis=-1, keepdims=True)
            dkb = _dot(dmm, kh) + drhs[:, HEAD_DIM:] * c["egc"]
            dk = _dot(dmm, c["kb"], TN) + _dot(dqk, qh, TN) + dkd * c["ekd"]
            dq = _dot(dqk, kh) + dqd * c["egc"]
            dgc = dgc + jnp.sum(drhs[:, HEAD_DIM:] * c["kbg"], axis=-1, keepdims=True)
            dgc = dgc + jnp.sum(dqd * c["qd"], axis=-1, keepdims=True)
            tmp = jnp.sum(dkd * c["kd"], axis=-1, keepdims=True)
            dgc = dgc - tmp
            dglast = jnp.sum(tmp, axis=0, keepdims=True) + dgl * c["gl"]
            dgc = dgc + jnp.where(row1 == CHUNK - 1, dglast, 0.0)
            dk = dk + dkb * bcol
            db = jnp.sum(dkb * kh, axis=-1, keepdims=True) + jnp.sum(drhs[:, :HEAD_DIM] * vh, axis=-1, keepdims=True)
            dqkv_ref[:, 128 * h:128 * (h + 1)] = dq
            dqkv_ref[:, 1024 + 128 * h:1024 + 128 * (h + 1)] = dk
            dqkv_ref[:, 2048 + 128 * h:2048 + 128 * (h + 1)] = drhs[:, :HEAD_DIM] * bcol
            dgc_all = dgc_all + jnp.where(lane == GA_LANE + h, dgc, 0.0)
            db_all = db_all + jnp.where(lane == GB_LANE + h, db, 0.0)
        dgb_ref[...] = _dot_hi(_tri(jj, ii), dgc_all) + db_all

    rev = lambda w: pl.BlockSpec((CHUNK, w), lambda i: (n - 1 - i, 0))
    return _call(body, name="gdn_bwd", grid=(n,),
                 in_specs=[rev(3072), rev(128), pl.BlockSpec((HEADS * 128, 128), lambda i: (n - 1 - i, 0)), rev(1024)],
                 out_specs=[rev(3072), rev(128)],
                 out_shape=[_sds((s, 3072), F32), _sds((s, 128), F32)],
                 scratch=[pltpu.VMEM((HEADS * 128, 128), F32)], args=(qkv, gbeta, states, do))


NN_B = (((2,), (1,)), ((0,), (0,)))
NT_B = (((2,), (2,)), ((0,), (0,)))
TN_B = (((1,), (1,)), ((0,), (0,)))
GDN_PAR_CHUNKS = 2
GDN_SEQ_CHUNKS = 4


def _gather_heads(qkv_ref, gc, gbv, qs, ks, vs, gs, bs, nchunks):
    lane = lax.broadcasted_iota(jnp.int32, (CHUNK, 128), 1)
    for c in range(nchunks):
        rows = slice(CHUNK * c, CHUNK * (c + 1))
        for h in range(HEADS):
            b = HEADS * c + h
            qs[b] = qkv_ref[rows, 128 * h:128 * (h + 1)]
            ks[b] = qkv_ref[rows, 1024 + 128 * h:1024 + 128 * (h + 1)]
            vs[b] = qkv_ref[rows, 2048 + 128 * h:2048 + 128 * (h + 1)]
            gs[b] = jnp.broadcast_to(_col(gc[rows], GA_LANE + h, lane), (CHUNK, 128))
            bs[b] = jnp.broadcast_to(_col(gbv[rows], GB_LANE + h, lane), (CHUNK, 128))


def _block_tri(rows, transpose=False):
    ri = lax.broadcasted_iota(jnp.int32, (rows, rows), 0)
    ci = lax.broadcasted_iota(jnp.int32, (rows, rows), 1)
    same = (ri >> 6) == (ci >> 6)
    return jnp.where(same & ((ci >= ri) if transpose else (ri >= ci)), 1.0, 0.0)


def _local_b(q, k, v, g128, b128):
    ii = lax.broadcasted_iota(jnp.int32, (1, CHUNK, CHUNK), 1)
    jj = lax.broadcasted_iota(jnp.int32, (1, CHUNK, CHUNK), 2)
    lower, strict, eye = ii >= jj, ii > jj, ii == jj
    g64 = g128[:, :, :CHUNK]
    grow = jnp.sum(jnp.where(eye, g64, 0.0), axis=1, keepdims=True)
    decay = jnp.where(lower, jnp.exp(jnp.where(lower, g64 - grow, 0.0)), 0.0)
    kb = k * b128
    vb = v * b128
    mm = lax.dot_general(kb.astype(BF16), k.astype(BF16), NT_B, preferred_element_type=F32)
    lmat = jnp.where(strict, mm * decay, 0.0)
    egc = jnp.exp(g128)
    kbg = kb * egc
    qk = lax.dot_general(q.astype(BF16), k.astype(BF16), NT_B, preferred_element_type=F32)
    row = lax.broadcasted_iota(jnp.int32, (1, CHUNK, 128), 1)
    glast = jnp.sum(jnp.where(row == CHUNK - 1, g128, 0.0), axis=1, keepdims=True)
    ekd = jnp.exp(glast - g128)
    return dict(decay=decay, kb=kb, vb=vb, lmat=lmat, egc=egc, kbg=kbg, amat=qk * decay, qd=q * egc, ekd=ekd,
                kd=k * ekd, gl=jnp.exp(glast), lower=lower, strict=strict, eye=eye)


def _bdot(a, b, dims):
    return lax.dot_general(a.astype(BF16), b.astype(BF16), dims, preferred_element_type=F32)


def _split(a):
    hi = a.astype(BF16)
    return hi, (a - hi.astype(F32)).astype(BF16)


def _bdot_hi(a, b, dims):
    ah, al = _split(a)
    bh, bl = _split(b)
    d = lambda x, y: lax.dot_general(x, y, dims, preferred_element_type=F32)
    return d(ah, bh) + d(ah, bl) + d(al, bh)


def _gdn_pre(qkv, gbeta):
    s = qkv.shape[0]
    n = s // CHUNK
    cb = min(GDN_PAR_CHUNKS, n)
    nb = cb * HEADS
    rows = cb * CHUNK

    def body(qkv_ref, gb_ref, u_ref, w_ref, qd_ref, kd_ref, a_ref, t_ref, gl_ref, qs, ks, vs, gs, bs):
        gbv = gb_ref[...]
        gc = _dot_hi(_block_tri(rows), gbv)
        _gather_heads(qkv_ref, gc, gbv, qs, ks, vs, gs, bs, cb)
        c = _local_b(qs[...], ks[...], vs[...], gs[...], bs[...])
        pw = -c["lmat"]
        tinv = jnp.where(c["eye"], 1.0, 0.0) + pw
        for _ in range(5):
            pw = _bdot_hi(pw, pw, NN_B)
            tinv = tinv + _bdot_hi(tinv, pw, NN_B)
        u_ref[...] = _bdot_hi(tinv, c["vb"], NN_B)
        w_ref[...] = _bdot_hi(tinv, c["kbg"], NN_B).astype(BF16)
        qd_ref[...] = c["qd"].astype(BF16)
        kd_ref[...] = c["kd"].astype(BF16)
        a_ref[...] = c["amat"].astype(BF16)
        t_ref[...] = tinv
        gl_ref[...] = c["gl"]

    b3 = lambda d: pl.BlockSpec((nb, CHUNK, d), lambda i: (i, 0, 0))
    nt = n * HEADS
    return _call(body, name="gdn_pre", grid=(n // cb,),
                 in_specs=[_rows(rows, 3072), _rows(rows, 128)],
                 out_specs=[b3(128), b3(128), b3(128), b3(128), b3(CHUNK), b3(CHUNK),
                            pl.BlockSpec((nb, 1, 128), lambda i: (i, 0, 0))],
                 out_shape=[_sds((nt, CHUNK, 128), F32), _sds((nt, CHUNK, 128), BF16), _sds((nt, CHUNK, 128), BF16),
                            _sds((nt, CHUNK, 128), BF16), _sds((nt, CHUNK, CHUNK), BF16), _sds((nt, CHUNK, CHUNK), F32),
                            _sds((nt, 1, 128), F32)],
                 scratch=[pltpu.VMEM((nb, CHUNK, 128), F32)] * 5, sem=("parallel",), args=(qkv, gbeta))


def _gdn_scan_fwd(u, w, qd, kd, amat, gl):
    nt = u.shape[0]
    n = nt // HEADS
    cs = min(GDN_SEQ_CHUNKS, n)

    def body(u_ref, w_ref, qd_ref, kd_ref, a_ref, gl_ref, o_ref, st_ref, state):
        @pl.when(pl.program_id(0) == 0)
        def _():
            state[...] = jnp.zeros_like(state)

        for c in range(cs):
            sl = slice(HEADS * c, HEADS * (c + 1))
            st = state[...]
            stb = st.astype(BF16)
            st_ref[sl] = stb
            vn = u_ref[sl] - lax.dot_general(w_ref[sl], stb, NN_B, preferred_element_type=F32)
            vnb = vn.astype(BF16)
            o = (lax.dot_general(qd_ref[sl], stb, NN_B, preferred_element_type=F32)
                 + lax.dot_general(a_ref[sl], vnb, NN_B, preferred_element_type=F32))
            state[...] = st * gl_ref[sl] + lax.dot_general(kd_ref[sl], vnb, TN_B, preferred_element_type=F32)
            for h in range(HEADS):
                o_ref[CHUNK * c:CHUNK * (c + 1), 128 * h:128 * (h + 1)] = o[h]

    b3 = lambda d: pl.BlockSpec((cs * HEADS, CHUNK, d), lambda i: (i, 0, 0))
    return _call(body, name="gdn_scan_fwd", grid=(n // cs,),
                 in_specs=[b3(128), b3(128), b3(128), b3(128), b3(CHUNK), pl.BlockSpec((cs * HEADS, 1, 128), lambda i: (i, 0, 0))],
                 out_specs=[_rows(cs * CHUNK, 1024), pl.BlockSpec((cs * HEADS, 128, 128), lambda i: (i, 0, 0))],
                 out_shape=[_sds((n * CHUNK, 1024), F32), _sds((nt, 128, 128), BF16)],
                 scratch=[pltpu.VMEM((HEADS, 128, 128), F32)], args=(u, w, qd, kd, amat, gl))


def _gdn_scan_bwd(w, qd, kd, amat, gl, do):
    nt = w.shape[0]
    n = nt // HEADS
    cs = min(GDN_SEQ_CHUNKS, n)
    ng = n // cs

    def body(w_ref, qd_ref, kd_ref, a_ref, gl_ref, do_ref, ds_ref, dstate, dos):
        @pl.when(pl.program_id(0) == 0)
        def _():
            dstate[...] = jnp.zeros_like(dstate)

        for c in reversed(range(cs)):
            sl = slice(HEADS * c, HEADS * (c + 1))
            for h in range(HEADS):
                dos[h] = do_ref[CHUNK * c:CHUNK * (c + 1), 128 * h:128 * (h + 1)].astype(BF16)
            dob = dos[...]
            dst = dstate[...]
            dstb = dst.astype(BF16)
            ds_ref[sl] = dstb
            dvn = (lax.dot_general(a_ref[sl], dob, TN_B, preferred_element_type=F32)
                   + lax.dot_general(kd_ref[sl], dstb, NN_B, preferred_element_type=F32))
            dstate[...] = (lax.dot_general(qd_ref[sl], dob, TN_B, preferred_element_type=F32) + gl_ref[sl] * dst
                           - lax.dot_general(w_ref[sl], dvn.astype(BF16), TN_B, preferred_element_type=F32))

    b3 = lambda d: pl.BlockSpec((cs * HEADS, CHUNK, d), lambda i: (ng - 1 - i, 0, 0))
    return _call(body, name="gdn_scan_bwd", grid=(ng,),
                 in_specs=[b3(128), b3(128), b3(128), b3(CHUNK), pl.BlockSpec((cs * HEADS, 1, 128), lambda i: (ng - 1 - i, 0, 0)),
                           pl.BlockSpec((cs * CHUNK, 1024), lambda i: (ng - 1 - i, 0))],
                 out_specs=pl.BlockSpec((cs * HEADS, 128, 128), lambda i: (ng - 1 - i, 0, 0)),
                 out_shape=_sds((nt, 128, 128), BF16),
                 scratch=[pltpu.VMEM((HEADS, 128, 128), F32), pltpu.VMEM((HEADS, CHUNK, 128), BF16)],
                 args=(w, qd, kd, amat, gl, do))


def _gdn_post_bwd(qkv, gbeta, u, w, tinv, states, dstates, do):
    s = qkv.shape[0]
    n = s // CHUNK
    cb = min(GDN_PAR_CHUNKS, n)
    nb = cb * HEADS
    rows = cb * CHUNK

    def body(qkv_ref, gb_ref, u_ref, w_ref, t_ref, st_ref, ds_ref, do_ref, dqkv_ref, dgb_ref, qs, ks, vs, gs, bs, dos):
        gbv = gb_ref[...]
        gc = _dot_hi(_block_tri(rows), gbv)
        _gather_heads(qkv_ref, gc, gbv, qs, ks, vs, gs, bs, cb)
        for c in range(cb):
            for h in range(HEADS):
                dos[HEADS * c + h] = do_ref[CHUNK * c:CHUNK * (c + 1), 128 * h:128 * (h + 1)].astype(F32)
        q, k, v, b128 = qs[...], ks[...], vs[...], bs[...]
        c = _local_b(q, k, v, gs[...], b128)
        tinv, st, dst, dov = t_ref[...], st_ref[...], ds_ref[...], dos[...]
        wv = w_ref[...]
        vn = u_ref[...] - _bdot(wv, st, NN_B)
        dvn = _bdot(c["amat"], dov, TN_B) + _bdot(c["kd"], dst, NN_B)
        damat = jnp.where(c["lower"], _bdot(dov, vn, NT_B), 0.0)
        dqd = _bdot(dov, st, NT_B)
        dkd = _bdot(vn, dst, NT_B)
        dw = -_bdot(dvn, st, NT_B)
        dgl = jnp.sum(jnp.sum(st.astype(F32) * dst.astype(F32), axis=1, keepdims=True), axis=-1, keepdims=True)
        dvb = _bdot(tinv, dvn, TN_B)
        dkbg = _bdot(tinv, dw, TN_B)
        dtinv = _bdot(dvn, c["vb"], NT_B) + _bdot(dw, c["kbg"], NT_B)
        dl = -_bdot(_bdot(tinv, dtinv, TN_B), tinv, NT_B)
        dl = jnp.where(c["strict"], dl, 0.0)
        dmm = dl * c["decay"]
        dqk = damat * c["decay"]
        wmat = dl * c["lmat"] + damat * c["amat"]
        wcol = jnp.sum(wmat, axis=1, keepdims=True)
        dgc = jnp.sum(wmat, axis=-1, keepdims=True) - jnp.sum(jnp.where(c["eye"], wcol, 0.0), axis=-1, keepdims=True)
        dkb = _bdot(dmm, k, NN_B) + dkbg * c["egc"]
        dk = _bdot(dmm, c["kb"], TN_B) + _bdot(dqk, q, TN_B) + dkd * c["ekd"] + dkb * b128
        dq = _bdot(dqk, k, NN_B) + dqd * c["egc"]
        tmp = jnp.sum(dkd * c["kd"], axis=-1, keepdims=True)
        dgc = (dgc + jnp.sum(dkbg * c["kbg"], axis=-1, keepdims=True) + jnp.sum(dqd * c["qd"], axis=-1, keepdims=True)
               - tmp)
        dglast = jnp.sum(tmp, axis=1, keepdims=True) + dgl * c["gl"][:, :, :1]
        row1 = lax.broadcasted_iota(jnp.int32, (1, CHUNK, 1), 1)
        dgc = dgc + jnp.where(row1 == CHUNK - 1, dglast, 0.0)
        db = jnp.sum(dkb * k, axis=-1, keepdims=True) + jnp.sum(dvb * v, axis=-1, keepdims=True)
        dv = dvb * b128
        lane = lax.broadcasted_iota(jnp.int32, (CHUNK, 128), 1)
        parts = []
        for cc in range(cb):
            acc = jnp.zeros((CHUNK, 128), F32)
            for h in range(HEADS):
                bi = HEADS * cc + h
                rs = slice(CHUNK * cc, CHUNK * (cc + 1))
                dqkv_ref[rs, 128 * h:128 * (h + 1)] = dq[bi]
                dqkv_ref[rs, 1024 + 128 * h:1024 + 128 * (h + 1)] = dk[bi]
                dqkv_ref[rs, 2048 + 128 * h:2048 + 128 * (h + 1)] = dv[bi]
                acc = acc + jnp.where(lane == GA_LANE + h, dgc[bi], 0.0)
            parts.append(acc)
        dgc_all = jnp.concatenate(parts, axis=0)
        dg_all = _dot_hi(_block_tri(rows, transpose=True), dgc_all)
        for cc in range(cb):
            acc = dg_all[CHUNK * cc:CHUNK * (cc + 1)]
            for h in range(HEADS):
                acc = acc + jnp.where(lane == GB_LANE + h, db[HEADS * cc + h], 0.0)
            dgb_ref[CHUNK * cc:CHUNK * (cc + 1), :] = acc

    b3 = lambda d1, d2: pl.BlockSpec((nb, d1, d2), lambda i: (i, 0, 0))
    return _call(body, name="gdn_post_bwd", grid=(n // cb,),
                 in_specs=[_rows(rows, 3072), _rows(rows, 128), b3(CHUNK, 128), b3(CHUNK, 128), b3(CHUNK, CHUNK),
                           b3(128, 128), b3(128, 128), _rows(rows, 1024)],
                 out_specs=[_rows(rows, 3072), _rows(rows, 128)],
                 out_shape=[_sds((s, 3072), F32), _sds((s, 128), F32)],
                 scratch=[pltpu.VMEM((nb, CHUNK, 128), F32)] * 6, sem=("parallel",),
                 args=(qkv, gbeta, u, w, tinv, states, dstates, do))


def _mix_fwd(o_mla, o_gdn, proj, out_gain):
    s = proj.shape[0]
    tm = min(TILES["row"], s)

    def body(om_ref, og_ref, mg_ref, gg_ref, g_ref, o_ref):
        o_ref[:, :1024] = (om_ref[...] * _silu(mg_ref[...].astype(F32))).astype(BF16)
        for h in range(HEADS):
            sl = slice(128 * h, 128 * (h + 1))
            og = og_ref[:, sl]
            on = og * _rms(og, HEAD_DIM) * g_ref[...]
            o_ref[:, 1024 + 128 * h:1024 + 128 * (h + 1)] = (on * _silu(gg_ref[:, sl].astype(F32))).astype(BF16)

    return _call(body, name="mix_fwd", grid=(s // tm,),
                 in_specs=[_rows(tm, 1024), _rows(tm, 1024), _rows(tm, 1024, 1), _rows(tm, 1024, 5), _full((1, 128))],
                 out_specs=_rows(tm, 2048), out_shape=_sds((s, 2048), BF16), args=(o_mla, o_gdn, proj, proj, out_gain))


def _mix_bwd(o_mla, o_gdn, proj, out_gain, dmixed):
    s = proj.shape[0]
    tm = min(TILES["row"], s)

    def body(om_ref, og_ref, mg_ref, gg_ref, g_ref, dm_ref, dg_ref, dom_ref, dog_ref, dmg_ref, dgg_ref, ag_ref,
             delta_ref):
        @pl.when(pl.program_id(0) == 0)
        def _():
            ag_ref[...] = jnp.zeros_like(ag_ref)

        mg = mg_ref[...].astype(F32)
        dm = dm_ref[...].astype(F32)
        om = om_ref[...]
        dom = (dm * _silu(mg)).astype(BF16)
        dom_ref[...] = dom
        dmg_ref[...] = (dm * om * _dsilu(mg)).astype(BF16)
        prod = dom.astype(F32) * om
        lane = lax.broadcasted_iota(jnp.int32, (tm, 128), 1)
        delta = jnp.zeros((tm, 128), F32)
        for h in range(HEADS):
            delta = delta + jnp.where(lane == h, jnp.sum(prod[:, 128 * h:128 * (h + 1)], axis=-1, keepdims=True), 0.0)
        delta_ref[...] = delta
        for h in range(HEADS):
            sl = slice(128 * h, 128 * (h + 1))
            og, gg, d = og_ref[:, sl], gg_ref[:, sl].astype(F32), dg_ref[:, sl].astype(F32)
            on = og * _rms(og, HEAD_DIM) * g_ref[...]
            dgg_ref[:, sl] = (d * on * _dsilu(gg)).astype(BF16)
            dx, gpart = _rms_bwd(og, g_ref[...], d * _silu(gg), HEAD_DIM)
            dog_ref[:, sl] = dx.astype(BF16)
            ag_ref[...] += _acc8(gpart)

    return _call(body, name="mix_bwd", grid=(s // tm,),
                 in_specs=[_rows(tm, 1024), _rows(tm, 1024), _rows(tm, 1024, 1), _rows(tm, 1024, 5), _full((1, 128)),
                           _rows(tm, 1024, 0), _rows(tm, 1024, 1)],
                 out_specs=[_rows(tm, 1024), _rows(tm, 1024), _rows(tm, 1024), _rows(tm, 1024), _full((8, 128)),
                            _rows(tm, 128)],
                 out_shape=[_sds((s, 1024), BF16), _sds((s, 1024), BF16), _sds((s, 1024), BF16), _sds((s, 1024), BF16),
                            _sds((8, 128), F32), _sds((s, 128), F32)],
                 args=(o_mla, o_gdn, proj, proj, out_gain, dmixed, dmixed))


def _out_fwd(mixed, w_out, x, target):
    s = x.shape[0]
    tm = min(TILES["mm"], s)
    tn = min(TILES["mm"], D_MODEL)

    def body(m_ref, w_ref, x_ref, t_ref, dy_ref, acc_ref):
        err = x_ref[...] + _dot(m_ref[...], w_ref[...]) - t_ref[...]
        dy_ref[...] = (err * (1.0 / D_MODEL)).astype(BF16)

        @pl.when(pl.program_id(1) == 0)
        def _():
            acc_ref[...] = jnp.zeros_like(acc_ref)

        acc_ref[...] += _acc8(err * err)

    return _call(body, name="out_fwd", grid=(D_MODEL // tn, s // tm),
                 in_specs=[pl.BlockSpec((tm, D_MODEL), lambda j, i: (i, 0)), pl.BlockSpec((D_MODEL, tn), lambda j, i: (0, j)),
                           pl.BlockSpec((tm, tn), lambda j, i: (i, j)), pl.BlockSpec((tm, tn), lambda j, i: (i, j))],
                 out_specs=[pl.BlockSpec((tm, tn), lambda j, i: (i, j)), pl.BlockSpec((8, tn), lambda j, i: (0, j))],
                 out_shape=[_sds((s, D_MODEL), BF16), _sds((8, D_MODEL), F32)], args=(mixed, w_out, x, target))


def _row_tile(r, c):
    if r % 8 != 0:
        return r
    t = 8
    while r % (2 * t) == 0 and 2 * t * c * 4 <= (1 << 20):
        t *= 2
    return t


def _sum_arrays(parts, name, also_bf16=False):
    r, c = parts[0].shape
    tr = _row_tile(r, c)
    n = len(parts)

    def body(*refs):
        acc = refs[0][...].astype(F32)
        for p_ref in refs[1:n]:
            acc = acc + p_ref[...].astype(F32)
        refs[n][...] = acc
        if also_bf16:
            refs[n + 1][...] = acc.astype(BF16)

    nout = 2 if also_bf16 else 1
    out = _call(body, name=name, grid=(r // tr,), in_specs=[_rows(tr, c)] * n, out_specs=[_rows(tr, c)] * nout,
                out_shape=[_sds((r, c), F32), _sds((r, c), BF16)][:nout], args=tuple(parts))
    return out if also_bf16 else out[0]


def _adamw(w, g, m, v, name):
    r, c = w.shape
    c1 = 1.0 - ADAM_B1 ** ADAM_STEP
    c2 = 1.0 - ADAM_B2 ** ADAM_STEP

    def body(w_ref, g_ref, m_ref, v_ref, d_ref, nm_ref, nv_ref):
        gv = g_ref[...]
        nm = ADAM_B1 * m_ref[...] + (1.0 - ADAM_B1) * gv
        nv = ADAM_B2 * v_ref[...] + (1.0 - ADAM_B2) * (gv * gv)
        nm_ref[...] = nm
        nv_ref[...] = nv
        d_ref[...] = -ADAM_LR * ((nm / c1) / (jnp.sqrt(nv / c2) + ADAM_EPS) + ADAM_WD * w_ref[...])

    if r % 8 == 0:
        tr = _row_tile(r, c)
        grid, spec = (r // tr,), _rows(tr, c)
    else:
        tc = c
        while tc % 256 == 0 and r * tc * 4 > (3 << 19):
            tc //= 2
        grid, spec = (c // tc,), pl.BlockSpec((r, tc), lambda i: (0, i))
    return _call(body, name=name, grid=grid, in_specs=[spec] * 4, out_specs=[spec] * 3,
                 out_shape=[_sds((r, c), F32)] * 3, args=(w, g, m, v))


ANY = pl.BlockSpec(memory_space=pl.ANY)
CHIP_FLIPS = ((1, 0), (0, 1), (1, 1))


def _comm_call(body, *, name, n_in, out_shape, scratch):
    def kfn(*refs):
        body(*refs)
    return pl.pallas_call(kfn, name=name, in_specs=[ANY] * n_in, out_specs=[ANY] * len(out_shape), out_shape=out_shape,
                          scratch_shapes=list(scratch),
                          compiler_params=pltpu.CompilerParams(has_side_effects=True))


def _all_gather_chips(shards):
    na = len(shards)

    def body(*refs):
        copies = _gather_copies(refs[:na], refs[na:2 * na], *refs[2 * na:])
        _gather_start(copies)
        _gather_finish(copies)

    out_shape = [_sds((4,) + a.shape, a.dtype) for a in shards]
    sem = pltpu.SemaphoreType.DMA((na, 3))
    got = _comm_call(body, name="all_gather_weights", n_in=na, out_shape=out_shape, scratch=[sem, sem, sem, sem])(*shards)
    return _place_own_blocks(got, shards)


def _gather_copies(ins, outs, send_sems, recv_sems, fwd_send, fwd_recv):
    x, y, c = lax.axis_index("x"), lax.axis_index("y"), lax.axis_index("c")
    my_k = 2 * x + y
    direct, forwards = [], []
    for a in range(len(ins)):
        rows = ins[a].shape[0]
        for r, (fx, fy) in enumerate(CHIP_FLIPS):
            px, py = x ^ fx, y ^ fy
            if rows % 32 == 0:
                mine = pl.ds(pl.multiple_of(c * (rows // 2), 16), rows // 2)
                other = pl.ds(pl.multiple_of((1 - c) * (rows // 2), 16), rows // 2)
                rc = pltpu.make_async_remote_copy(
                    src_ref=ins[a].at[mine], dst_ref=outs[a].at[my_k, mine], send_sem=send_sems.at[a, r],
                    recv_sem=recv_sems.at[a, r], device_id=(px, py, c), device_id_type=MESH)
                landed = outs[a].at[2 * px + py, mine]
                fw = pltpu.make_async_remote_copy(
                    src_ref=landed, dst_ref=landed, send_sem=fwd_send.at[a, r], recv_sem=fwd_recv.at[a, r],
                    device_id=(x, y, 1 - c), device_id_type=MESH)
                from_sib = outs[a].at[2 * px + py, other]
                fw_in = pltpu.make_async_remote_copy(
                    src_ref=from_sib, dst_ref=from_sib, send_sem=fwd_send.at[a, r], recv_sem=fwd_recv.at[a, r],
                    device_id=(x, y, 1 - c), device_id_type=MESH)
                forwards.append((rc, fw, fw_in))
            else:
                direct.append(pltpu.make_async_remote_copy(
                    src_ref=ins[a], dst_ref=outs[a].at[my_k], send_sem=send_sems.at[a, r],
                    recv_sem=recv_sems.at[a, r], device_id=(px, py, c), device_id_type=MESH))
    return forwards, direct


def _gather_start(copies):
    forwards, direct = copies
    for rc, _, _ in forwards:
        rc.start()
    for rc in direct:
        rc.start()


def _gather_finish(copies):
    forwards, direct = copies
    for rc, fw, _ in forwards:
        rc.wait_recv()
        fw.start()
    for rc, fw, fw_in in forwards:
        rc.wait_send()
        fw.wait_send()
        fw_in.wait_recv()
    for rc in direct:
        rc.wait()


def _place_own_blocks(got, shards):
    my_k = 2 * lax.axis_index("x") + lax.axis_index("y")
    return [lax.dynamic_update_index_in_dim(g, a, my_k, 0) for g, a in zip(got, shards)]


def _norm1_fwd_gather(x, gain, shards):
    s = x.shape[0]
    tm = min(TILES["row"], s)
    ni = s // tm
    na = len(shards)

    def kfn(x_ref, g_ref, *rest):
        o_ref = rest[na]
        sems = rest[2 * na + 1:]
        i = pl.program_id(0)

        @pl.when(i == 0)
        def _():
            _gather_start(_gather_copies(rest[:na], rest[na + 1:2 * na + 1], *sems))

        xv = x_ref[...]
        r = lax.rsqrt(jnp.mean(xv * xv, axis=-1, keepdims=True) + NORM_EPS)
        o_ref[...] = (xv * r * g_ref[...]).astype(BF16)

        @pl.when(i == ni - 1)
        def _():
            _gather_finish(_gather_copies(rest[:na], rest[na + 1:2 * na + 1], *sems))

    sem = pltpu.SemaphoreType.DMA((na, 3))
    out = pl.pallas_call(
        kfn, name="norm1_fwd_gather", grid=(ni,),
        in_specs=[_rows(tm, D_MODEL), _full((1, D_MODEL))] + [ANY] * na,
        out_specs=[_rows(tm, D_MODEL)] + [ANY] * na,
        out_shape=[_sds((s, D_MODEL), BF16)] + [_sds((4,) + s_a.shape, s_a.dtype) for s_a in shards],
        scratch_shapes=[sem, sem, sem, sem],
        compiler_params=pltpu.CompilerParams(dimension_semantics=("arbitrary",), vmem_limit_bytes=VMEM_LIMIT_V7X,
                                             has_side_effects=True))(x, gain, *shards)
    return out[0], _place_own_blocks(list(out[1:]), shards)


def _matmul_nn_gather(a, b, shards, *, name, tm, tn, out_dtype):
    m, kdim = a.shape
    n = b.shape[1]
    ni, nj = m // tm, n // tn
    na = len(shards)

    def body(a_ref, b_ref, *rest):
        o_ref = rest[na]
        sems = rest[2 * na + 1:]
        i, j = pl.program_id(0), pl.program_id(1)

        @pl.when((i == 0) & (j == 0))
        def _():
            _gather_start(_gather_copies(rest[:na], rest[na + 1:2 * na + 1], *sems))

        o_ref[...] = _dot(a_ref[...], b_ref[...]).astype(out_dtype)

        @pl.when((i == ni - 1) & (j == nj - 1))
        def _():
            _gather_finish(_gather_copies(rest[:na], rest[na + 1:2 * na + 1], *sems))

    def kfn(*refs):
        body(*refs)
    sem = pltpu.SemaphoreType.DMA((na, 3))
    out = pl.pallas_call(
        kfn, name=name, grid=(ni, nj),
        in_specs=[pl.BlockSpec((tm, kdim), lambda i, j: (i, 0)), pl.BlockSpec((kdim, tn), lambda i, j: (0, j))] + [ANY] * na,
        out_specs=[pl.BlockSpec((tm, tn), lambda i, j: (i, j))] + [ANY] * na,
        out_shape=[_sds((m, n), out_dtype)] + [_sds((4,) + s_a.shape, s_a.dtype) for s_a in shards],
        scratch_shapes=[sem, sem, sem, sem],
        compiler_params=pltpu.CompilerParams(dimension_semantics=("arbitrary", "arbitrary"),
                                             vmem_limit_bytes=VMEM_LIMIT_V7X, has_side_effects=True))(a, b, *shards)
    return out[0], _place_own_blocks(list(out[1:]), shards)


def _all_reduce_small(vec):
    r = vec.shape[0]

    def body(v_ref, o_ref, gath, send_sems, recv_sems):
        x, y, c = lax.axis_index("x"), lax.axis_index("y"), lax.axis_index("c")
        me = 4 * x + 2 * y + c
        gath[me] = v_ref[...]
        copies = []
        for rel in range(1, 8):
            fx, fy, fc = (rel >> 2) & 1, (rel >> 1) & 1, rel & 1
            rc = pltpu.make_async_remote_copy(
                src_ref=v_ref, dst_ref=gath.at[me], send_sem=send_sems.at[rel - 1], recv_sem=recv_sems.at[rel - 1],
                device_id=(x ^ fx, y ^ fy, c ^ fc), device_id_type=MESH)
            rc.start()
            copies.append(rc)
        for rc in copies:
            rc.wait()
        acc = gath[0]
        for d in range(1, 8):
            acc = acc + gath[d]
        o_ref[...] = acc

    def kfn(*refs):
        body(*refs)
    vm = pl.BlockSpec(memory_space=pltpu.VMEM)
    return pl.pallas_call(kfn, name="all_reduce_small", in_specs=[vm], out_specs=vm, out_shape=_sds((r, 128), F32),
                          scratch_shapes=[pltpu.VMEM((8, r, 128), F32), pltpu.SemaphoreType.DMA((7,)),
                                          pltpu.SemaphoreType.DMA((7,))],
                          compiler_params=pltpu.CompilerParams(has_side_effects=True))(vec)


def _exchange_halves(arrs):
    na = len(arrs)

    def body(*refs):
        ins, outs = refs[:na], refs[na:2 * na]
        send_sems, recv_sems = refs[2 * na:]
        x, y, c = lax.axis_index("x"), lax.axis_index("y"), lax.axis_index("c")
        copies = []
        for a in range(na):
            half = ins[a].shape[1] // 2
            src = ins[a].at[:, pl.ds(pl.multiple_of((1 - c) * half, 8), half), :]
            rc = pltpu.make_async_remote_copy(src_ref=src, dst_ref=outs[a], send_sem=send_sems.at[a],
                                              recv_sem=recv_sems.at[a], device_id=(x, y, 1 - c), device_id_type=MESH)
            rc.start()
            copies.append(rc)
        for rc in copies:
            rc.wait()

    out_shape = [_sds((4, a.shape[1] // 2, a.shape[2]), F32) for a in arrs]
    return _comm_call(body, name="rs_pair_exchange", n_in=na, out_shape=out_shape,
                      scratch=[pltpu.SemaphoreType.DMA((na,)), pltpu.SemaphoreType.DMA((na,))])(*arrs)


def _scatter_to_chips(arrs):
    na = len(arrs)

    def body(*refs):
        ins, outs = refs[:na], refs[na:2 * na]
        send_sems, recv_sems = refs[2 * na:]
        x, y, c = lax.axis_index("x"), lax.axis_index("y"), lax.axis_index("c")
        copies = []
        for a in range(na):
            for r, (fx, fy) in enumerate(CHIP_FLIPS):
                px, py = x ^ fx, y ^ fy
                rc = pltpu.make_async_remote_copy(
                    src_ref=ins[a].at[2 * px + py], dst_ref=outs[a].at[r], send_sem=send_sems.at[a, r],
                    recv_sem=recv_sems.at[a, r], device_id=(px, py, c), device_id_type=MESH)
                rc.start()
                copies.append(rc)
        for rc in copies:
            rc.wait()

    out_shape = [_sds((3,) + a.shape[1:], a.dtype) for a in arrs]
    return _comm_call(body, name="rs_chip_scatter", n_in=na, out_shape=out_shape,
                      scratch=[pltpu.SemaphoreType.DMA((na, 3)), pltpu.SemaphoreType.DMA((na, 3))])(*arrs)


def _sum_into_half(parts, name):
    r2, c = parts[0].shape
    tr = _row_tile(r2, c)
    nb = r2 // tr
    n = len(parts)

    def kfn(c_ref, *refs):
        acc = refs[0][...].astype(F32)
        for p_ref in refs[1:n]:
            acc = acc + p_ref[...].astype(F32)
        refs[n][...] = acc

    spec = pltpu.PrefetchScalarGridSpec(
        num_scalar_prefetch=1, grid=(nb,), in_specs=[pl.BlockSpec((tr, c), lambda i, cr: (i, 0))] * n,
        out_specs=pl.BlockSpec((tr, c), lambda i, cr: (cr[0] * nb + i, 0)))
    core = lax.axis_index("c").astype(jnp.int32).reshape(1)
    return pl.pallas_call(kfn, name=name, grid_spec=spec, out_shape=_sds((2 * r2, c), F32),
                          compiler_params=pltpu.CompilerParams(dimension_semantics=("arbitrary",),
                                                               vmem_limit_bytes=VMEM_LIMIT_V7X))(core, *parts)


def _join_in_place(arrs):
    na = len(arrs)

    def body(*refs):
        outs = refs[na:2 * na]
        send_sems, recv_sems = refs[2 * na:]
        x, y, c = lax.axis_index("x"), lax.axis_index("y"), lax.axis_index("c")
        copies = []
        for a in range(na):
            half = outs[a].shape[0] // 2
            mine = outs[a].at[pl.ds(pl.multiple_of(c * half, 8), half), :]
            rc = pltpu.make_async_remote_copy(src_ref=mine, dst_ref=mine, send_sem=send_sems.at[a],
                                              recv_sem=recv_sems.at[a], device_id=(x, y, 1 - c), device_id_type=MESH)
            rc.start()
            copies.append(rc)
        for rc in copies:
            rc.wait()

    def kfn(*refs):
        body(*refs)
    return pl.pallas_call(kfn, name="rs_pair_join", in_specs=[ANY] * na, out_specs=[ANY] * na,
                          out_shape=[_sds(a.shape, F32) for a in arrs],
                          input_output_aliases={a: a for a in range(na)},
                          scratch_shapes=[pltpu.SemaphoreType.DMA((na,)), pltpu.SemaphoreType.DMA((na,))],
                          compiler_params=pltpu.CompilerParams(has_side_effects=True))(*arrs)


def _pair_sum(g, o, name):
    _, r, c = g.shape
    half = r // 2
    tr = _row_tile(half, c)
    nb = half // tr

    def kfn(c_ref, g_ref, o_ref, s32_ref, s16_ref):
        acc = g_ref[...] + o_ref[...]
        s32_ref[...] = acc
        s16_ref[...] = acc.astype(BF16)

    blk = lambda imap: pl.BlockSpec((None, tr, c), imap)
    same = lambda k, i, cr: (k, i, 0)
    spec = pltpu.PrefetchScalarGridSpec(
        num_scalar_prefetch=1, grid=(4, nb), in_specs=[blk(lambda k, i, cr: (k, cr[0] * nb + i, 0)), blk(same)],
        out_specs=[blk(same), blk(same)])
    core = lax.axis_index("c").astype(jnp.int32).reshape(1)
    return pl.pallas_call(kfn, name=name, grid_spec=spec, out_shape=[_sds((4, half, c), F32), _sds((4, half, c), BF16)],
                          compiler_params=pltpu.CompilerParams(dimension_semantics=("arbitrary", "arbitrary"),
                                                               vmem_limit_bytes=VMEM_LIMIT_V7X))(core, g, o)


def _rs_pair_stage(grads):
    got = _exchange_halves(grads)
    sums = [_pair_sum(g, o, f"rs_pair_sum_{a}") for a, (g, o) in enumerate(zip(grads, got))]
    return [s32 for s32, _ in sums], [s16 for _, s16 in sums]


def _rs_chip_stage(pair, recv):
    k_me = 2 * lax.axis_index("x") + lax.axis_index("y")
    halves = []
    for a, (p, rv) in enumerate(zip(pair, recv)):
        own = lax.dynamic_index_in_dim(p, k_me, 0, keepdims=False)
        halves.append(_sum_into_half([own, rv[0], rv[1], rv[2]], f"rs_chip_sum_{a}"))
    return _join_in_place(halves)


def _reduce_scatter(grads):
    pair, pair_bf16 = _rs_pair_stage(grads)
    return _rs_chip_stage(pair, _scatter_to_chips(pair_bf16))


def _matmul_nt_scatter(a, b, send, *, name, tm, tn, out_dtype):
    m, kdim = a.shape
    n = b.shape[0]
    ni, nj = m // tm, n // tn
    na = len(send)

    def body(a_ref, b_ref, *rest):
        send_refs, o_ref, recv_refs = rest[:na], rest[na], rest[na + 1:2 * na + 1]
        send_sems, recv_sems = rest[2 * na + 1:]
        i, j = pl.program_id(0), pl.program_id(1)

        def copies():
            x, y, c = lax.axis_index("x"), lax.axis_index("y"), lax.axis_index("c")
            out = []
            for s_i in range(na):
                for r, (fx, fy) in enumerate(CHIP_FLIPS):
                    px, py = x ^ fx, y ^ fy
                    out.append(pltpu.make_async_remote_copy(
                        src_ref=send_refs[s_i].at[2 * px + py], dst_ref=recv_refs[s_i].at[r],
                        send_sem=send_sems.at[s_i, r], recv_sem=recv_sems.at[s_i, r], device_id=(px, py, c),
                        device_id_type=MESH))
            return out

        @pl.when((i == 0) & (j == 0))
        def _():
            for cp in copies():
                cp.start()

        o_ref[...] = _dot(a_ref[...], b_ref[...], NT).astype(out_dtype)

        @pl.when((i == ni - 1) & (j == nj - 1))
        def _():
            for cp in copies():
                cp.wait()

    def kfn(*refs):
        body(*refs)
    sem = pltpu.SemaphoreType.DMA((na, 3))
    out = pl.pallas_call(
        kfn, name=name, grid=(ni, nj),
        in_specs=[pl.BlockSpec((tm, kdim), lambda i, j: (i, 0)), pl.BlockSpec((tn, kdim), lambda i, j: (j, 0))] + [ANY] * na,
        out_specs=[pl.BlockSpec((tm, tn), lambda i, j: (i, j))] + [ANY] * na,
        out_shape=[_sds((m, n), out_dtype)] + [_sds((3,) + s_a.shape[1:], s_a.dtype) for s_a in send],
        scratch_shapes=[sem, sem],
        compiler_params=pltpu.CompilerParams(dimension_semantics=("arbitrary", "arbitrary"),
                                             vmem_limit_bytes=VMEM_LIMIT_V7X, has_side_effects=True))(a, b, *send)
    return out[0], list(out[1:])


def _pad_w_in(w):
    z = jnp.zeros((w.shape[0], 1024 - 848), w.dtype)
    return jnp.concatenate([w[:, 0:832], w[:, 4928:4944], z, w[:, 832:4928], w[:, 4944:5968]], axis=1)


def _unpad_w_in(g):
    return jnp.concatenate([g[:, 0:832], g[:, 1024:5120], g[:, 832:848], g[:, 5120:6144]], axis=1)


W_IN_SHARD = W_IN_COLS // 4
W_IN_RUNS = ((0, 832, 0), (832, 4928, 1024), (4928, 4944, 832), (4944, 5968, 5120))


def _w_in_grad_blocks(p):
    def orig_cols(lo, hi):
        parts = [p[:, pa + max(lo, a) - a:pa + min(hi, b) - a] for a, b, pa in W_IN_RUNS if max(lo, a) < min(hi, b)]
        return parts[0] if len(parts) == 1 else jnp.concatenate(parts, axis=1)
    return jnp.stack([orig_cols(W_IN_SHARD * k, W_IN_SHARD * (k + 1)) for k in range(4)])


def _pad_w_in_blocks(g):
    def orig_cols(lo, hi):
        return [g[k][:, max(lo, W_IN_SHARD * k) - W_IN_SHARD * k:min(hi, W_IN_SHARD * (k + 1)) - W_IN_SHARD * k]
                for k in range(4) if max(lo, W_IN_SHARD * k) < min(hi, W_IN_SHARD * (k + 1))]
    z = jnp.zeros((g.shape[1], 1024 - 848), g.dtype)
    return jnp.concatenate(orig_cols(0, 832) + orig_cols(4928, 4944) + [z] + orig_cols(832, 4928) + orig_cols(4944, 5968),
                           axis=1)


def _pad_heads(w):
    r = w.shape[0]
    return jnp.pad(w.reshape(r, HEADS, QK_DIM), ((0, 0), (0, 0), (0, QK_PAD - QK_DIM))).reshape(r, HEADS * QK_PAD)


def _unpad_heads(w):
    r = w.shape[0]
    return w.reshape(r, HEADS, QK_PAD)[:, :, :QK_DIM].reshape(r, HEADS * QK_DIM)


def _cols_to_blocks(w):
    r = w.shape[0]
    return w.reshape(r, 4, -1).transpose(1, 0, 2)


def _blocks_to_cols(w):
    return w.transpose(1, 0, 2).reshape(w.shape[1], -1)


SMALL_ROWS = {"norm_gain": (0, 2048), "mla_q_a_gain": (16, 512), "mla_kv_a_gain": (20, 256),
              "mla_q_norm_gain": (22, 192), "mla_k_norm_gain": (24, 192), "gdn_a_log": (26, 8),
              "gdn_dt_bias": (27, 8), "gdn_out_norm_gain": (28, 128)}
LOSS_ROW = 29
SMALL_PACK_ROWS = 32
CONV_ROW = 32


def _pack_small(vals, loss=None):
    rows = []
    at = 0
    for name, (row, size) in SMALL_ROWS.items():
        assert row == at
        nr = -(-size // 128)
        rows.append(jnp.pad(vals[name].reshape(-1).astype(F32), (0, nr * 128 - size)).reshape(nr, 128))
        at += nr
    assert at == LOSS_ROW
    if loss is not None:
        rows.append(jnp.pad(loss.reshape(1, 1), ((0, 0), (0, 127))))
        at += 1
    rows.append(jnp.zeros((SMALL_PACK_ROWS - at, 128), F32))
    return jnp.concatenate(rows, axis=0)


def _unpack_small(pack, name):
    row, size = SMALL_ROWS[name]
    nr = -(-size // 128)
    return pack[row:row + nr].reshape(-1)[:size].reshape(1, size)


def _local_step(x, positions, target, norm_gain, w_in_p, q_a_gain, kv_a_gain, w_uq_p, w_ukv, q_norm_gain,
                k_norm_gain, conv_w, a_log, dt_bias, out_gain, w_out, scatter_hook=None, late_weights=None,
                first_weights=None):
    half = HALF_ROPE
    inv_freq = jnp.power(10000.0, -jnp.arange(half, dtype=F32) / half)
    ang = positions.astype(F32)[:, None] * inv_freq
    cos, sin = jnp.cos(ang), jnp.sin(ang)
    zpad = jnp.zeros((x.shape[0], 64), F32)
    cs = jnp.concatenate([cos, cos, zpad], axis=1)
    sn = jnp.concatenate([-sin, sin, zpad], axis=1)
    gq = jnp.pad(q_norm_gain.reshape(1, QK_DIM), ((0, 0), (0, QK_PAD - QK_DIM)))
    gk = jnp.pad(k_norm_gain.reshape(1, QK_DIM), ((0, 0), (0, QK_PAD - QK_DIM)))
    lane_pad = ((0, 0), (GA_LANE, 128 - GA_LANE - HEADS))
    alog128 = jnp.pad(a_log.reshape(1, HEADS), lane_pad)
    dtb128 = jnp.pad(dt_bias.reshape(1, HEADS), lane_pad)
    ng, qag, kvag, og = (norm_gain.reshape(1, -1), q_a_gain.reshape(1, -1), kv_a_gain.reshape(1, -1),
                         out_gain.reshape(1, -1))

    if first_weights is None:
        xn = _norm1_fwd(x, ng)
    else:
        shards, assemble = first_weights
        xn, gathered = _norm1_fwd_gather(x, ng, shards)
        w_in_p, conv_w = assemble(gathered)
    misc = _matmul(xn, w_in_p[:, 768:896], mode="nn", out_dtype=F32, name="in_proj_misc")
    if late_weights is None:
        proj = _matmul(xn, w_in_p, mode="nn", out_dtype=BF16, name="in_proj")
    else:
        shards, assemble = late_weights
        proj, gathered = _matmul_nn_gather(xn, w_in_p, shards, name="in_proj_gather", tm=TILES["mm"], tn=TILES["mm"],
                                           out_dtype=BF16)
        w_uq_p, w_ukv, w_out = assemble(gathered)
    cqn, ckvn = _mla_a_norm(proj, qag, kvag)
    q_pre = _matmul(cqn, w_uq_p, mode="nn", out_dtype=BF16, name="q_up")
    kv_pre = _matmul(ckvn, w_ukv, mode="nn", out_dtype=BF16, name="kv_up")
    q, k, v = _mla_post_fwd(q_pre, kv_pre, misc, cs, sn, gq, gk)
    o_mla, lse = _attn_fwd(q, k, v)
    qkv = _gdn_conv_fwd(proj, conv_w)
    gbeta = _gdn_gates_fwd(misc, alog128, dtb128)
    g_u, g_w, g_qd, g_kd, g_a, g_t, g_gl = _gdn_pre(qkv, gbeta)
    o_gdn, states = _gdn_scan_fwd(g_u, g_w, g_qd, g_kd, g_a, g_gl)
    mixed = _mix_fwd(o_mla, o_gdn, proj, og)
    dy, sq = _out_fwd(mixed, w_out, x, target)

    dmixed = _matmul(dy, w_out, mode="nt", out_dtype=BF16, name="d_mixed")
    d_w_out = _matmul(mixed, dy, mode="tn", out_dtype=F32, name="d_w_out", tk=4096)
    do_mla, do_gdn, dmg, dgg, d_out_gain, delta128 = _mix_bwd(o_mla, o_gdn, proj, og, dmixed)
    s_len = x.shape[0]
    delta_rows = delta128[:, :HEADS].T.reshape(HEADS, 1, s_len)
    dq, dk, dv = _attn_bwd(q, k, v, lse.reshape(HEADS, 1, s_len), delta_rows, do_mla)
    dq_pre, dkv_pre, dkr, d_gq, d_gk = _mla_post_bwd(q_pre, kv_pre, misc, cs, sn, gq, gk, dq, dk, dv)
    d_w_uq_p = _matmul(cqn, dq_pre, mode="tn", out_dtype=F32, name="d_w_uq", tk=1024)
    d_w_ukv = _matmul(ckvn, dkv_pre, mode="tn", out_dtype=F32, name="d_w_ukv", tk=1024)
    dcqn = _matmul(dq_pre, w_uq_p, mode="nt", out_dtype=F32, name="d_cqn")
    dckvn = _matmul(dkv_pre, w_ukv, mode="nt", out_dtype=F32, name="d_ckvn")
    dcq, dckv, d_qag, d_kvag = _mla_a_norm_bwd(proj, qag, kvag, dcqn, dckvn)
    dstates = _gdn_scan_bwd(g_w, g_qd, g_kd, g_a, g_gl, do_gdn)
    dqkv, dgbeta = _gdn_post_bwd(qkv, gbeta, g_u, g_w, g_t, states, dstates, do_gdn)
    dz = _gdn_conv_bwd_a(proj, conv_w, dqkv)
    dgx, d_conv = _gdn_conv_bwd_b(proj, conv_w, dz)
    dmisc, d_alog, d_dtb = _gdn_gates_bwd(misc, alog128, dtb128, gbeta, dgbeta, dkr)
    dproj = jnp.concatenate([dcq, dckv, dmisc, jnp.zeros((x.shape[0], 128), BF16), dmg, dgx, dgg], axis=1)
    d_w_in_p = _matmul(xn, dproj, mode="tn", out_dtype=F32, name="d_w_in", tk=4096)
    big = {"w_in": d_w_in_p, "w_uq": d_w_uq_p, "w_ukv": d_w_ukv, "w_out": d_w_out, "gdn_conv_w": d_conv}
    if scatter_hook is None:
        dxn, received = _matmul(dproj, w_in_p, mode="nt", out_dtype=BF16, name="d_xn", tm=512, tn=512), None
    else:
        dxn, received = _matmul_nt_scatter(dproj, w_in_p, scatter_hook(big), name="d_xn_scatter", tm=512, tn=512,
                                           out_dtype=BF16)
    grad_x, d_ng = _norm1_bwd(x, ng, dxn, dy)

    small = {"norm_gain": d_ng.sum(0), "mla_q_a_gain": d_qag.sum(0), "mla_kv_a_gain": d_kvag.sum(0),
             "mla_q_norm_gain": d_gq.sum(0)[:QK_DIM], "mla_k_norm_gain": d_gk.sum(0)[:QK_DIM],
             "gdn_a_log": d_alog.sum(0)[GA_LANE:GA_LANE + HEADS], "gdn_dt_bias": d_dtb.sum(0)[GA_LANE:GA_LANE + HEADS],
             "gdn_out_norm_gain": d_out_gain.sum(0)}
    return sq, grad_x, small, big, received


WEIGHTS = ["norm_gain", "w_in", "mla_q_a_gain", "mla_kv_a_gain", "w_uq", "w_ukv", "mla_q_norm_gain", "mla_k_norm_gain",
           "gdn_conv_w", "gdn_a_log", "gdn_dt_bias", "gdn_out_norm_gain", "w_out"]
BIG = ["w_in", "w_uq", "w_ukv", "w_out"]


def kernel(x, positions, norm_gain, w_in, mla_q_a_gain, mla_kv_a_gain, w_uq, w_ukv, mla_q_norm_gain, mla_k_norm_gain, gdn_conv_w, gdn_a_log, gdn_dt_bias, gdn_out_norm_gain, w_out, loss_target, m_norm_gain, m_w_in, m_mla_q_a_gain, m_mla_kv_a_gain, m_w_uq, m_w_ukv, m_mla_q_norm_gain, m_mla_k_norm_gain, m_gdn_conv_w, m_gdn_a_log, m_gdn_dt_bias, m_gdn_out_norm_gain, m_w_out, v_norm_gain, v_w_in, v_mla_q_a_gain, v_mla_kv_a_gain, v_w_uq, v_w_ukv, v_mla_q_norm_gain, v_mla_k_norm_gain, v_gdn_conv_w, v_gdn_a_log, v_gdn_dt_bias, v_gdn_out_norm_gain, v_w_out):
    w = dict(norm_gain=norm_gain, w_in=w_in, mla_q_a_gain=mla_q_a_gain, mla_kv_a_gain=mla_kv_a_gain, w_uq=w_uq,
             w_ukv=w_ukv, mla_q_norm_gain=mla_q_norm_gain, mla_k_norm_gain=mla_k_norm_gain, gdn_conv_w=gdn_conv_w,
             gdn_a_log=gdn_a_log, gdn_dt_bias=gdn_dt_bias, gdn_out_norm_gain=gdn_out_norm_gain, w_out=w_out)
    m = dict(norm_gain=m_norm_gain, w_in=m_w_in, mla_q_a_gain=m_mla_q_a_gain, mla_kv_a_gain=m_mla_kv_a_gain,
             w_uq=m_w_uq, w_ukv=m_w_ukv, mla_q_norm_gain=m_mla_q_norm_gain, mla_k_norm_gain=m_mla_k_norm_gain,
             gdn_conv_w=m_gdn_conv_w, gdn_a_log=m_gdn_a_log, gdn_dt_bias=m_gdn_dt_bias,
             gdn_out_norm_gain=m_gdn_out_norm_gain, w_out=m_w_out)
    v = dict(norm_gain=v_norm_gain, w_in=v_w_in, mla_q_a_gain=v_mla_q_a_gain, mla_kv_a_gain=v_mla_kv_a_gain,
             w_uq=v_w_uq, w_ukv=v_w_ukv, mla_q_norm_gain=v_mla_q_norm_gain, mla_k_norm_gain=v_mla_k_norm_gain,
             gdn_conv_w=v_gdn_conv_w, gdn_a_log=v_gdn_a_log, gdn_dt_bias=v_gdn_dt_bias,
             gdn_out_norm_gain=v_gdn_out_norm_gain, w_out=v_w_out)
    k_me = 2 * lax.axis_index("x") + lax.axis_index("y")

    first_weights = ([w_in[0].astype(BF16), gdn_conv_w[0]], lambda g: (_pad_w_in_blocks(g[0]), _blocks_to_cols(g[1])))
    late_weights = ([w_uq[0].astype(BF16), w_ukv[0].astype(BF16), w_out[0].astype(BF16)],
                    lambda g: (_pad_heads(_blocks_to_cols(g[0])), _blocks_to_cols(g[1]), g[2].reshape(D_MODEL, D_MODEL)))

    pair_sums = []

    def scatter_hook(big):
        pair, pair_bf16 = _rs_pair_stage([
            _w_in_grad_blocks(big["w_in"]), _cols_to_blocks(_unpad_heads(big["w_uq"])),
            _cols_to_blocks(big["w_ukv"]), big["w_out"].reshape(4, 512, D_MODEL)])
        pair_sums.extend(pair)
        return pair_bf16

    sq, grad_x, small, big, received = _local_step(
        x[0], positions[0], loss_target[0], norm_gain, None, mla_q_a_gain, mla_kv_a_gain, None, None,
        mla_q_norm_gain, mla_k_norm_gain, None, gdn_a_log, gdn_dt_bias, gdn_out_norm_gain, None, scatter_hook,
        late_weights, first_weights)

    loss_local = (0.5 / D_MODEL) * jnp.sum(sq)
    pack = jnp.concatenate([_pack_small(small, loss_local), big["gdn_conv_w"].reshape(96, 128)], axis=0)
    tot = _all_reduce_small(pack)
    loss = tot[LOSS_ROW, 0]
    conv_grad = lax.dynamic_slice_in_dim(tot[CONV_ROW:].reshape(4, 3072), k_me * 768, 768, axis=1)

    shard_grads = _rs_chip_stage(pair_sums, received)

    grads = {n: _unpack_small(tot, n) for n in SMALL_ROWS}
    grads["gdn_conv_w"] = conv_grad[None]
    for n, g in zip(BIG, shard_grads):
        grads[n] = g[None]

    delta, new_m, new_v = {}, {}, {}
    sw = _pack_small({n: w[n] for n in SMALL_ROWS})
    sm = _pack_small({n: m[n] for n in SMALL_ROWS})
    sv = _pack_small({n: v[n] for n in SMALL_ROWS})
    sd, snm, snv = _adamw(sw, tot[:SMALL_PACK_ROWS], sm, sv, "adamw_small")
    for n in SMALL_ROWS:
        delta[n], new_m[n], new_v[n] = _unpack_small(sd, n), _unpack_small(snm, n), _unpack_small(snv, n)
    for n in BIG + ["gdn_conv_w"]:
        if n == "w_in":
            d, nm, nv = _adamw(w[n][0].T, grads[n][0].T, m[n][0].T, v[n][0].T, f"adamw_{n}")
            delta[n], new_m[n], new_v[n] = d.T[None], nm.T[None], nv.T[None]
        else:
            d, nm, nv = _adamw(w[n][0], grads[n][0], m[n][0], v[n][0], f"adamw_{n}")
            delta[n], new_m[n], new_v[n] = d[None], nm[None], nv[None]

    return (loss, grad_x[None], *[grads[n] for n in WEIGHTS], *[delta[n] for n in WEIGHTS],
            *[new_m[n] for n in WEIGHTS], *[new_v[n] for n in WEIGHTS])
```

```python
import functools
import math

import jax
import jax.numpy as jnp
from jax import lax
from jax.experimental import pallas as pl
from jax.experimental.pallas import tpu as pltpu

F32 = jnp.float32
BF16 = jnp.bfloat16
MESH = pl.DeviceIdType.MESH

D_MODEL = 2048
HEADS = 8
HEAD_DIM = 128
QK_DIM = 192
QK_PAD = 256
HALF_ROPE = 32
CHUNK = 64
NORM_EPS = 1e-6
W_IN_COLS = 5968
W_IN_PAD = 6144
GA_LANE = 64
GB_LANE = 72
ADAM_LR, ADAM_B1, ADAM_B2, ADAM_EPS, ADAM_WD, ADAM_STEP = 0.001, 0.9, 0.999, 1e-08, 0.01, 10
VMEM_LIMIT_V7X = 52 * 1024 * 1024
HI = lax.Precision.HIGHEST
NN = (((1,), (0,)), ((), ()))
NT = (((1,), (1,)), ((), ()))
TN = (((0,), (0,)), ((), ()))

TILES = {"row": 512, "attn": 2048, "mm": 1024}


def _call(body, *, name, grid, in_specs, out_specs, out_shape, args, scratch=(), sem=None):
    def kfn(*refs):
        body(*refs)
    if sem is None:
        sem = ("arbitrary",) * len(grid)
    return pl.pallas_call(
        kfn, name=name, grid=grid, in_specs=in_specs, out_specs=out_specs, out_shape=out_shape,
        scratch_shapes=list(scratch),
        compiler_params=pltpu.CompilerParams(dimension_semantics=sem, vmem_limit_bytes=VMEM_LIMIT_V7X),
    )(*args)


def _rows(tm, w, cb=0):
    return pl.BlockSpec((tm, w), lambda i: (i, cb))


def _full(shape):
    n = len(shape)
    return pl.BlockSpec(shape, lambda *_: (0,) * n)


def _sds(shape, dtype):
    return jax.ShapeDtypeStruct(shape, dtype)


def _acc8(x):
    tm, c = x.shape
    return jnp.sum(x.reshape(tm // 8, 8, c), axis=0)


def _sigmoid(x):
    return 1.0 / (1.0 + jnp.exp(-x))


def _silu(x):
    return x * _sigmoid(x)


def _dsilu(x):
    s = _sigmoid(x)
    return s * (1.0 + x * (1.0 - s))


def _dot(a, b, dims=NN):
    return lax.dot_general(a.astype(BF16), b.astype(BF16), dims, preferred_element_type=F32)


def _dot_hi(a, b, dims=NN):
    return lax.dot_general(a, b, dims, precision=HI, preferred_element_type=F32)


def _matmul(a, b, *, mode, out_dtype, name, tm=None, tn=None, tk=None):
    if mode == "tn":
        kdim, m = a.shape
    else:
        m, kdim = a.shape
    n = b.shape[0] if mode == "nt" else b.shape[1]
    tm = min(tm or TILES["mm"], m)
    tn = min(tn or TILES["mm"], n)
    tk = min(tk or kdim, kdim)
    nk = kdim // tk
    dims = {"nn": NN, "nt": NT, "tn": TN}[mode]
    if mode == "tn":
        a_spec = pl.BlockSpec((tk, tm), lambda i, j, k: (k, i))
    else:
        a_spec = pl.BlockSpec((tm, tk), lambda i, j, k: (i, k))
    if mode == "nt":
        b_spec = pl.BlockSpec((tn, tk), lambda i, j, k: (j, k))
    else:
        b_spec = pl.BlockSpec((tk, tn), lambda i, j, k: (k, j))

    def body(a_ref, b_ref, o_ref):
        r = _dot(a_ref[...], b_ref[...], dims)
        if nk == 1:
            o_ref[...] = r.astype(o_ref.dtype)
        else:
            k = pl.program_id(2)

            @pl.when(k == 0)
            def _():
                o_ref[...] = r

            @pl.when(k > 0)
            def _():
                o_ref[...] += r

    if nk > 1:
        assert out_dtype == F32
    return _call(body, name=name, grid=(m // tm, n // tn, nk), in_specs=[a_spec, b_spec],
                 out_specs=pl.BlockSpec((tm, tn), lambda i, j, k: (i, j)), out_shape=_sds((m, n), out_dtype),
                 args=(a, b))


def _norm1_fwd(x, gain):
    s = x.shape[0]
    tm = min(TILES["row"], s)

    def body(x_ref, g_ref, o_ref):
        xv = x_ref[...]
        r = lax.rsqrt(jnp.mean(xv * xv, axis=-1, keepdims=True) + NORM_EPS)
        o_ref[...] = (xv * r * g_ref[...]).astype(BF16)

    return _call(body, name="norm1_fwd", grid=(s // tm,), in_specs=[_rows(tm, D_MODEL), _full((1, D_MODEL))],
                 out_specs=_rows(tm, D_MODEL), out_shape=_sds((s, D_MODEL), BF16), args=(x, gain))


def _norm1_bwd(x, gain, dxn, dy):
    s = x.shape[0]
    tm = min(TILES["row"], s)

    def body(x_ref, g_ref, dxn_ref, dy_ref, gx_ref, dg_ref):
        xv = x_ref[...]
        r = lax.rsqrt(jnp.mean(xv * xv, axis=-1, keepdims=True) + NORM_EPS)
        nrm = xv * r
        d = dxn_ref[...].astype(F32)
        dn = d * g_ref[...]
        gx_ref[...] = dy_ref[...].astype(F32) + r * (dn - nrm * jnp.mean(dn * nrm, axis=-1, keepdims=True))

        @pl.when(pl.program_id(0) == 0)
        def _():
            dg_ref[...] = jnp.zeros_like(dg_ref)

        dg_ref[...] += _acc8(d * nrm)

    return _call(body, name="norm1_bwd", grid=(s // tm,),
                 in_specs=[_rows(tm, D_MODEL), _full((1, D_MODEL)), _rows(tm, D_MODEL), _rows(tm, D_MODEL)],
                 out_specs=[_rows(tm, D_MODEL), _full((8, D_MODEL))],
                 out_shape=[_sds((s, D_MODEL), F32), _sds((8, D_MODEL), F32)], args=(x, gain, dxn, dy))


def _rms(xv, width):
    return lax.rsqrt(jnp.sum(xv * xv, axis=-1, keepdims=True) * (1.0 / width) + NORM_EPS)


def _mla_a_norm(proj, gq, gkv):
    s = proj.shape[0]
    tm = min(TILES["row"], s)

    def body(cq_ref, ckv_ref, gq_ref, gkv_ref, oq_ref, okv_ref):
        a = cq_ref[...].astype(F32)
        oq_ref[...] = (a * _rms(a, 512) * gq_ref[...]).astype(BF16)
        b = ckv_ref[...].astype(F32)
        okv_ref[...] = (b * _rms(b, 256) * gkv_ref[...]).astype(BF16)

    return _call(body, name="mla_a_norm", grid=(s // tm,),
                 in_specs=[_rows(tm, 512, 0), _rows(tm, 256, 2), _full((1, 512)), _full((1, 256))],
                 out_specs=[_rows(tm, 512), _rows(tm, 256)],
                 out_shape=[_sds((s, 512), BF16), _sds((s, 256), BF16)], args=(proj, proj, gq, gkv))


def _rms_bwd(xv, gain, d, width):
    r = _rms(xv, width)
    nrm = xv * r
    dn = d * gain
    dx = r * (dn - nrm * (jnp.sum(dn * nrm, axis=-1, keepdims=True) * (1.0 / width)))
    return dx, d * nrm


def _mla_a_norm_bwd(proj, gq, gkv, dcqn, dckvn):
    s = proj.shape[0]
    tm = min(TILES["row"], s)

    def body(cq_ref, ckv_ref, gq_ref, gkv_ref, dq_ref, dkv_ref, oq_ref, okv_ref, aq_ref, akv_ref):
        dxq, gq_part = _rms_bwd(cq_ref[...].astype(F32), gq_ref[...], dq_ref[...].astype(F32), 512)
        dxk, gk_part = _rms_bwd(ckv_ref[...].astype(F32), gkv_ref[...], dkv_ref[...].astype(F32), 256)
        oq_ref[...] = dxq.astype(BF16)
        okv_ref[...] = dxk.astype(BF16)

        @pl.when(pl.program_id(0) == 0)
        def _():
            aq_ref[...] = jnp.zeros_like(aq_ref)
            akv_ref[...] = jnp.zeros_like(akv_ref)

        aq_ref[...] += _acc8(gq_part)
        akv_ref[...] += _acc8(gk_part)

    return _call(body, name="mla_a_norm_bwd", grid=(s // tm,),
                 in_specs=[_rows(tm, 512, 0), _rows(tm, 256, 2), _full((1, 512)), _full((1, 256)),
                           _rows(tm, 512), _rows(tm, 256)],
                 out_specs=[_rows(tm, 512), _rows(tm, 256), _full((8, 512)), _full((8, 256))],
                 out_shape=[_sds((s, 512), BF16), _sds((s, 256), BF16), _sds((8, 512), F32), _sds((8, 256), F32)],
                 args=(proj, proj, gq, gkv, dcqn, dckvn))


def _swap32(r):
    lane = lax.broadcasted_iota(jnp.int32, r.shape, 1)
    return jnp.where(lane < HALF_ROPE, pltpu.roll(r, 128 - HALF_ROPE, 1), pltpu.roll(r, HALF_ROPE, 1))


def _mla_post_fwd(q_pre, kv_pre, proj, cs, sn, gq, gk):
    s = q_pre.shape[0]
    tm = min(TILES["row"], s)

    def body(qp_ref, kvp_ref, misc_ref, cs_ref, sn_ref, gq_ref, gk_ref, q_ref, k_ref, v_ref):
        csv, snv = cs_ref[...], sn_ref[...]
        lane = lax.broadcasted_iota(jnp.int32, (tm, 128), 1)
        kr = jnp.where(lane < 64, misc_ref[...], 0.0)
        for h in range(HEADS):
            for src, g_ref, o_ref in ((None, gq_ref, q_ref), (kr, gk_ref, k_ref)):
                if src is None:
                    xv = qp_ref[:, QK_PAD * h:QK_PAD * (h + 1)].astype(F32)
                else:
                    xv = jnp.concatenate([kvp_ref[:, 256 * h:256 * h + 128].astype(F32), src], axis=-1)
                y = xv * _rms(xv, QK_DIM) * g_ref[...]
                if src is None:
                    y = y * Q_PRESCALE
                hi = y[:, 128:]
                hi = hi * csv + _swap32(hi) * snv
                o_ref[:, QK_PAD * h:QK_PAD * h + 128] = y[:, :128].astype(BF16)
                o_ref[:, QK_PAD * h + 128:QK_PAD * (h + 1)] = hi.astype(BF16)
            v_ref[:, 128 * h:128 * (h + 1)] = kvp_ref[:, 256 * h + 128:256 * (h + 1)].astype(BF16)

    return _call(body, name="mla_post_fwd", grid=(s // tm,),
                 in_specs=[_rows(tm, 2048), _rows(tm, 2048), _rows(tm, 128), _rows(tm, 128), _rows(tm, 128),
                           _full((1, QK_PAD)), _full((1, QK_PAD))],
                 out_specs=[_rows(tm, 2048), _rows(tm, 2048), _rows(tm, 1024)],
                 out_shape=[_sds((s, 2048), BF16), _sds((s, 2048), BF16), _sds((s, 1024), BF16)],
                 args=(q_pre, kv_pre, proj, cs, sn, gq, gk))


def _mla_post_bwd(q_pre, kv_pre, proj, cs, sn, gq, gk, dq, dk, dv):
    s = q_pre.shape[0]
    tm = min(TILES["row"], s)

    def body(qp_ref, kvp_ref, misc_ref, cs_ref, sn_ref, gq_ref, gk_ref, dq_ref, dk_ref, dv_ref,
             oq_ref, okv_ref, okr_ref, agq_ref, agk_ref):
        csv, snv = cs_ref[...], sn_ref[...]
        lane = lax.broadcasted_iota(jnp.int32, (tm, 128), 1)
        kr = jnp.where(lane < 64, misc_ref[...], 0.0)

        @pl.when(pl.program_id(0) == 0)
        def _():
            agq_ref[...] = jnp.zeros_like(agq_ref)
            agk_ref[...] = jnp.zeros_like(agk_ref)

        dkr = jnp.zeros((tm, 128), F32)
        for h in range(HEADS):
            for which in (0, 1):
                if which == 0:
                    xv = qp_ref[:, QK_PAD * h:QK_PAD * (h + 1)].astype(F32)
                    d_ref, g_ref, a_ref = dq_ref, gq_ref, agq_ref
                else:
                    xv = jnp.concatenate([kvp_ref[:, 256 * h:256 * h + 128].astype(F32), kr], axis=-1)
                    d_ref, g_ref, a_ref = dk_ref, gk_ref, agk_ref
                dhi = d_ref[:, QK_PAD * h + 128:QK_PAD * (h + 1)]
                dhi = dhi * csv - _swap32(dhi) * snv
                dyv = jnp.concatenate([d_ref[:, QK_PAD * h:QK_PAD * h + 128], dhi], axis=-1)
                if which == 0:
                    dyv = dyv * ATTN_SCALE
                dx, gpart = _rms_bwd(xv, g_ref[...], dyv, QK_DIM)
                a_ref[...] += _acc8(gpart)
                if which == 0:
                    oq_ref[:, QK_PAD * h:QK_PAD * (h + 1)] = dx.astype(BF16)
                else:
                    okv_ref[:, 256 * h:256 * h + 128] = dx[:, :128].astype(BF16)
                    dkr = dkr + dx[:, 128:]
            okv_ref[:, 256 * h + 128:256 * (h + 1)] = dv_ref[:, 128 * h:128 * (h + 1)].astype(BF16)
        okr_ref[...] = dkr

    return _call(body, name="mla_post_bwd", grid=(s // tm,),
                 in_specs=[_rows(tm, 2048), _rows(tm, 2048), _rows(tm, 128), _rows(tm, 128), _rows(tm, 128),
                           _full((1, QK_PAD)), _full((1, QK_PAD)), _rows(tm, 2048), _rows(tm, 2048), _rows(tm, 1024)],
                 out_specs=[_rows(tm, 2048), _rows(tm, 2048), _rows(tm, 128), _full((8, QK_PAD)), _full((8, QK_PAD))],
                 out_shape=[_sds((s, 2048), BF16), _sds((s, 2048), BF16), _sds((s, 128), F32),
                            _sds((8, QK_PAD), F32), _sds((8, QK_PAD), F32)],
                 args=(q_pre, kv_pre, proj, cs, sn, gq, gk, dq, dk, dv))


ATTN_SCALE = QK_DIM ** -0.5
NEG = -1e30


LOG2E = 1.4426950408889634
LN2 = 0.6931471805599453
Q_PRESCALE = ATTN_SCALE * LOG2E
ATTN_SUB_FWD = 512
ATTN_SUB_BWD = 256


def _causal_pairs(nq, kv_major):
    prs = [(i, j) for i in range(nq) for j in range(i + 1)]
    if kv_major:
        prs.sort(key=lambda ij: (ij[1], ij[0]))
    return (jnp.asarray([p[0] for p in prs], jnp.int32), jnp.asarray([p[1] for p in prs], jnp.int32))


def _pair_call(body, *, name, tables, in_specs, out_specs, out_shape, scratch, args):
    def kfn(*refs):
        body(*refs)
    spec = pltpu.PrefetchScalarGridSpec(num_scalar_prefetch=2, grid=(HEADS, tables[0].shape[0]), in_specs=in_specs,
                                        out_specs=out_specs, scratch_shapes=list(scratch))
    return pl.pallas_call(
        kfn, name=name, grid_spec=spec, out_shape=out_shape,
        compiler_params=pltpu.CompilerParams(dimension_semantics=("parallel", "arbitrary"),
                                             vmem_limit_bytes=VMEM_LIMIT_V7X))(*tables, *args)


def _diag_mask(sc, ts, qs):
    row = lax.broadcasted_iota(jnp.int32, sc.shape, 0) + qs * ts
    col = lax.broadcasted_iota(jnp.int32, sc.shape, 1)
    return jnp.where(col <= row, sc, NEG)


def _attn_fwd(q, k, v):
    s = q.shape[0]
    t = min(TILES["attn"], s)
    ts = min(ATTN_SUB_FWD, t)
    nq = s // t

    def slabs(q_ref, k_ref, v_ref, m_s, l_s, acc_s, diag):
        def scores(qs):
            kw = (qs + 1) * ts if diag else t
            sc = lax.dot_general(q_ref[qs * ts:(qs + 1) * ts, :], k_ref[0:kw, :], NT, preferred_element_type=F32)
            return _diag_mask(sc, ts, qs) if diag else sc

        nsub = t // ts
        sc_next = scores(0)
        for qs in range(nsub):
            rq = slice(qs * ts, (qs + 1) * ts)
            kw = (qs + 1) * ts if diag else t
            sc = sc_next
            if qs + 1 < nsub:
                sc_next = scores(qs + 1)
            m_prev = m_s[rq, :]
            m_new = jnp.maximum(m_prev, jnp.max(sc, axis=-1, keepdims=True))
            p = jnp.exp2(sc - m_new)
            alpha = jnp.exp2(m_prev - m_new)
            l_s[rq, :] = alpha * l_s[rq, :] + jnp.sum(p, axis=-1, keepdims=True)
            acc_s[rq, :] = acc_s[rq, :] * alpha + lax.dot_general(p.astype(BF16), v_ref[0:kw, :], NN,
                                                                  preferred_element_type=F32)
            m_s[rq, :] = m_new

    def body(it_ref, jt_ref, q_ref, k_ref, v_ref, o_ref, lse_ref, m_s, l_s, acc_s):
        p = pl.program_id(1)
        i, j = it_ref[p], jt_ref[p]

        @pl.when(j == 0)
        def _():
            m_s[...] = jnp.full_like(m_s, NEG)
            l_s[...] = jnp.zeros_like(l_s)
            acc_s[...] = jnp.zeros_like(acc_s)

        @pl.when(j < i)
        def _():
            slabs(q_ref, k_ref, v_ref, m_s, l_s, acc_s, False)

        @pl.when(j == i)
        def _():
            slabs(q_ref, k_ref, v_ref, m_s, l_s, acc_s, True)
            o_ref[...] = acc_s[...] / l_s[...]
            lse_ref[...] = m_s[...] + jnp.log2(l_s[...])

    qb = lambda h, p, it, jt: (it[p], h)
    kb = lambda h, p, it, jt: (jt[p], h)
    return _pair_call(
        body, name="attn_fwd", tables=_causal_pairs(nq, kv_major=False),
        in_specs=[pl.BlockSpec((t, QK_PAD), qb), pl.BlockSpec((t, QK_PAD), kb), pl.BlockSpec((t, HEAD_DIM), kb)],
        out_specs=[pl.BlockSpec((t, HEAD_DIM), qb),
                   pl.BlockSpec((None, t, 1), lambda h, p, it, jt: (h, it[p], 0))],
        out_shape=[_sds((s, HEADS * HEAD_DIM), F32), _sds((HEADS, s, 1), F32)],
        scratch=[pltpu.VMEM((t, 1), F32), pltpu.VMEM((t, 1), F32), pltpu.VMEM((t, HEAD_DIM), F32)],
        args=(q, k, v))


def _attn_bwd(q, k, v, lse_rows, delta_rows, do):
    s = q.shape[0]
    t = min(TILES["attn"], s)
    ts = min(ATTN_SUB_BWD, t)
    nq = s // t

    def slabs(q_ref, k_ref, v_ref, lse_ref, delta_ref, do_ref, dq_ref, dk_ref, dv_ref, i, diag):
        def products(qs):
            rq = slice(qs * ts, (qs + 1) * ts)
            kw = (qs + 1) * ts if diag else t
            qv, dob = q_ref[rq, :], do_ref[rq, :]
            sct = lax.dot_general(k_ref[0:kw, :], qv, NT, preferred_element_type=F32)
            dpt = lax.dot_general(v_ref[0:kw, :], dob, NT, preferred_element_type=F32)
            if diag:
                row = lax.broadcasted_iota(jnp.int32, sct.shape, 0)
                col = lax.broadcasted_iota(jnp.int32, sct.shape, 1) + qs * ts
                sct = jnp.where(row <= col, sct, NEG)
            return qv, dob, sct, dpt

        nsub = t // ts
        ahead = products(0)
        for qs in range(nsub):
            rq = slice(qs * ts, (qs + 1) * ts)
            kw = (qs + 1) * ts if diag else t
            qv, dob, sct, dpt = ahead
            if qs + 1 < nsub:
                ahead = products(qs + 1)
            pt = jnp.exp2(sct - lse_ref[:, rq])
            dv_ref[0:kw, :] += lax.dot_general(pt.astype(BF16), dob, NN, preferred_element_type=F32)
            dst = (pt * (dpt - delta_ref[:, rq])).astype(BF16)
            dk_ref[0:kw, :] += lax.dot_general(dst, qv, NN, preferred_element_type=F32)
            rows = pl.ds(pl.multiple_of(i * t + qs * ts, ts), ts)
            dq_ref[rows, :] += lax.dot_general(dst, k_ref[0:kw, :], TN, preferred_element_type=F32)

    def body(it_ref, jt_ref, q_ref, k_ref, v_ref, lse_ref, delta_ref, do_ref, dq_ref, dk_ref, dv_ref):
        p = pl.program_id(1)
        i, j = it_ref[p], jt_ref[p]
        refs = (q_ref, k_ref, v_ref, lse_ref, delta_ref, do_ref, dq_ref, dk_ref, dv_ref)

        @pl.when(p == 0)
        def _():
            dq_ref[...] = jnp.zeros_like(dq_ref)

        @pl.when(i == j)
        def _():
            dk_ref[...] = jnp.zeros_like(dk_ref)
            dv_ref[...] = jnp.zeros_like(dv_ref)

        @pl.when(i > j)
        def _():
            slabs(*refs, i, False)

        @pl.when(i == j)
        def _():
            slabs(*refs, i, True)

        @pl.when(i == nq - 1)
        def _():
            dk_ref[...] = dk_ref[...] * LN2

    qb = lambda h, p, it, jt: (it[p], h)
    kb = lambda h, p, it, jt: (jt[p], h)
    rowb = pl.BlockSpec((None, 1, t), lambda h, p, it, jt: (h, 0, it[p]))
    return _pair_call(
        body, name="attn_bwd", tables=_causal_pairs(nq, kv_major=True),
        in_specs=[pl.BlockSpec((t, QK_PAD), qb), pl.BlockSpec((t, QK_PAD), kb), pl.BlockSpec((t, HEAD_DIM), kb),
                  rowb, rowb, pl.BlockSpec((t, HEAD_DIM), qb)],
        out_specs=[pl.BlockSpec((s, QK_PAD), lambda h, p, it, jt: (0, h)), pl.BlockSpec((t, QK_PAD), kb),
                   pl.BlockSpec((t, HEAD_DIM), kb)],
        out_shape=[_sds((s, HEADS * QK_PAD), F32), _sds((s, HEADS * QK_PAD), F32), _sds((s, HEADS * HEAD_DIM), F32)],
        scratch=(), args=(q, k, v, lse_rows, delta_rows, do))


GDN_Q_SCALE = HEAD_DIM ** -0.5


def _shift_down(xv, prev8, sft):
    rolled = pltpu.roll(xv, sft, 0)
    top = pltpu.roll(jnp.concatenate([prev8, xv[:8]], axis=0), sft, 0)[8:]
    return jnp.concatenate([top, rolled[8:]], axis=0)


def _shift_up(xv, next8, sft):
    tm = xv.shape[0]
    rolled = pltpu.roll(xv, tm - sft, 0)
    bot = pltpu.roll(jnp.concatenate([xv[tm - 8:], next8], axis=0), 16 - sft, 0)[:8]
    return jnp.concatenate([rolled[:tm - 8], bot], axis=0)


def _conv_z(xv, prev8, w_ref):
    z = xv * w_ref[3:4, :]
    for sft in (1, 2, 3):
        z = z + _shift_down(xv, prev8, sft) * w_ref[3 - sft:4 - sft, :]
    return z


def _conv_specs(s, tm):
    nb16 = tm // 16
    cur = pl.BlockSpec((tm, 1024), lambda j, i: (i, 2 + j))
    prev = pl.BlockSpec((16, 1024), lambda j, i: (jnp.maximum(i * nb16 - 1, 0), 2 + j))
    return cur, prev


def _prev8(xp_ref, i):
    return jnp.where(i > 0, xp_ref[...].astype(F32)[8:], 0.0)


def _gdn_conv_fwd(proj, conv_w):
    s = proj.shape[0]
    tm = min(TILES["row"], s)
    cur, prev = _conv_specs(s, tm)

    def body(x_ref, xp_ref, w_ref, o_ref):
        j, i = pl.program_id(0), pl.program_id(1)
        a = _silu(_conv_z(x_ref[...].astype(F32), _prev8(xp_ref, i), w_ref))
        qk_scale = jnp.where(j == 0, GDN_Q_SCALE, 1.0)
        for h in range(HEADS):
            seg = a[:, 128 * h:128 * (h + 1)]
            r = lax.rsqrt(jnp.sum(seg * seg, axis=-1, keepdims=True) + NORM_EPS)
            o_ref[:, 128 * h:128 * (h + 1)] = jnp.where(j < 2, seg * r * qk_scale, seg)

    return _call(body, name="gdn_conv_fwd", grid=(3, s // tm),
                 in_specs=[cur, prev, pl.BlockSpec((4, 1024), lambda j, i: (0, j))],
                 out_specs=pl.BlockSpec((tm, 1024), lambda j, i: (i, j)), out_shape=_sds((s, 3072), F32),
                 args=(proj, proj, conv_w))


def _gdn_conv_bwd_a(proj, conv_w, dqkv):
    s = proj.shape[0]
    tm = min(TILES["row"], s)
    cur, prev = _conv_specs(s, tm)

    def body(x_ref, xp_ref, w_ref, d_ref, o_ref):
        j, i = pl.program_id(0), pl.program_id(1)
        z = _conv_z(x_ref[...].astype(F32), _prev8(xp_ref, i), w_ref)
        a = _silu(z)
        dsl = _dsilu(z)
        qk_scale = jnp.where(j == 0, GDN_Q_SCALE, 1.0)
        for h in range(HEADS):
            sl = slice(128 * h, 128 * (h + 1))
            seg = a[:, sl]
            dyv = d_ref[:, sl]
            r = lax.rsqrt(jnp.sum(seg * seg, axis=-1, keepdims=True) + NORM_EPS)
            yh = seg * r
            da_n = qk_scale * r * (dyv - yh * jnp.sum(yh * dyv, axis=-1, keepdims=True))
            o_ref[:, sl] = jnp.where(j < 2, da_n, dyv) * dsl[:, sl]

    return _call(body, name="gdn_conv_bwd_a", grid=(3, s // tm),
                 in_specs=[cur, prev, pl.BlockSpec((4, 1024), lambda j, i: (0, j)),
                           pl.BlockSpec((tm, 1024), lambda j, i: (i, j))],
                 out_specs=pl.BlockSpec((tm, 1024), lambda j, i: (i, j)), out_shape=_sds((s, 3072), F32),
                 args=(proj, proj, conv_w, dqkv))


def _gdn_conv_bwd_b(proj, conv_w, dz):
    s = proj.shape[0]
    tm = min(TILES["row"], s)
    nb8 = tm // 8
    last8 = s // 8 - 1
    cur, prev = _conv_specs(s, tm)

    def body(x_ref, w_ref, dz_ref, dzn_ref, dx_ref, dw_ref):
        i = pl.program_id(1)
        next8 = jnp.where(i < pl.num_programs(1) - 1, dzn_ref[...], 0.0)
        xv, dzv = x_ref[...].astype(F32), dz_ref[...]

        @pl.when(i == 0)
        def _():
            dw_ref[...] = jnp.zeros_like(dw_ref)

        dx = dzv * w_ref[3:4, :]
        dw_ref[3:4, :] += jnp.sum(dzv * xv, axis=0, keepdims=True)
        for sft in (1, 2, 3):
            up = _shift_up(dzv, next8, sft)
            dx = dx + up * w_ref[3 - sft:4 - sft, :]
            dw_ref[3 - sft:4 - sft, :] += jnp.sum(up * xv, axis=0, keepdims=True)
        dx_ref[...] = dx.astype(BF16)

    return _call(body, name="gdn_conv_bwd_b", grid=(3, s // tm),
                 in_specs=[cur, pl.BlockSpec((4, 1024), lambda j, i: (0, j)),
                           pl.BlockSpec((tm, 1024), lambda j, i: (i, j)),
                           pl.BlockSpec((8, 1024), lambda j, i: (jnp.minimum((i + 1) * nb8, last8), j))],
                 out_specs=[pl.BlockSpec((tm, 1024), lambda j, i: (i, j)), pl.BlockSpec((4, 1024), lambda j, i: (0, j))],
                 out_shape=[_sds((s, 3072), BF16), _sds((4, 3072), F32)], args=(proj, conv_w, dz, dz))


def _softplus(xv):
    return jnp.maximum(xv, 0.0) + jnp.log(1.0 + jnp.exp(-jnp.abs(xv)))


def _gdn_gates_fwd(proj, alog128, dtb128):
    s = proj.shape[0]
    tm = min(TILES["row"], s)

    def body(m_ref, a_ref, b_ref, o_ref):
        mv = m_ref[...]
        lane = lax.broadcasted_iota(jnp.int32, mv.shape, 1)
        g = -jnp.exp(a_ref[...]) * _softplus(mv + b_ref[...])
        is_g = (lane >= GA_LANE) & (lane < GA_LANE + HEADS)
        is_b = (lane >= GB_LANE) & (lane < GB_LANE + HEADS)
        o_ref[...] = jnp.where(is_g, g, jnp.where(is_b, _sigmoid(mv), 0.0))

    return _call(body, name="gdn_gates_fwd", grid=(s // tm,),
                 in_specs=[_rows(tm, 128), _full((1, 128)), _full((1, 128))],
                 out_specs=_rows(tm, 128), out_shape=_sds((s, 128), F32), args=(proj, alog128, dtb128))


def _gdn_gates_bwd(proj, alog128, dtb128, gbeta, dgbeta, dkr):
    s = proj.shape[0]
    tm = min(TILES["row"], s)

    def body(m_ref, a_ref, b_ref, gb_ref, d_ref, kr_ref, o_ref, da_ref, db_ref):
        mv, dv = m_ref[...], d_ref[...]
        lane = lax.broadcasted_iota(jnp.int32, mv.shape, 1)
        is_g = (lane >= GA_LANE) & (lane < GA_LANE + HEADS)
        is_b = (lane >= GB_LANE) & (lane < GB_LANE + HEADS)
        dga = jnp.where(is_g, dv * (-jnp.exp(a_ref[...])) * _sigmoid(mv + b_ref[...]), 0.0)
        beta = gb_ref[...]
        dgb = jnp.where(is_b, dv * beta * (1.0 - beta), 0.0)
        o_ref[...] = jnp.where(lane < 64, kr_ref[...], dga + dgb).astype(BF16)

        @pl.when(pl.program_id(0) == 0)
        def _():
            da_ref[...] = jnp.zeros_like(da_ref)
            db_ref[...] = jnp.zeros_like(db_ref)

        da_ref[...] += _acc8(jnp.where(is_g, dv * gb_ref[...], 0.0))
        db_ref[...] += _acc8(dga)

    return _call(body, name="gdn_gates_bwd", grid=(s // tm,),
                 in_specs=[_rows(tm, 128), _full((1, 128)), _full((1, 128)), _rows(tm, 128), _rows(tm, 128),
                           _rows(tm, 128)],
                 out_specs=[_rows(tm, 128), _full((8, 128)), _full((8, 128))],
                 out_shape=[_sds((s, 128), BF16), _sds((8, 128), F32), _sds((8, 128), F32)],
                 args=(proj, alog128, dtb128, gbeta, dgbeta, dkr))


def _col(mat, lane_idx, lane):
    return jnp.sum(jnp.where(lane == lane_idx, mat, 0.0), axis=-1, keepdims=True)


def _chunk_local(qh, kh, vh, gcol, bcol, ii, jj):
    lower, strict, eye = ii >= jj, ii > jj, ii == jj
    grow = jnp.sum(jnp.where(eye, gcol, 0.0), axis=0, keepdims=True)
    decay = jnp.where(lower, jnp.exp(jnp.where(lower, gcol - grow, 0.0)), 0.0)
    kb = kh * bcol
    vb = vh * bcol
    mm = _dot(kb, kh, NT)
    lmat = jnp.where(strict, mm * decay, 0.0)
    pw = -lmat
    tinv = jnp.where(eye, 1.0, 0.0) + pw
    for _ in range(5):
        pw = _dot_hi(pw, pw)
        tinv = tinv + _dot_hi(tinv, pw)
    egc = jnp.exp(gcol)
    kbg = kb * egc
    rhs = jnp.concatenate([vb, kbg], axis=-1)
    sol = _dot_hi(tinv, rhs)
    qk = _dot(qh, kh, NT)
    glast = jnp.sum(jnp.where(ii[:, :1] == CHUNK - 1, gcol, 0.0), axis=0, keepdims=True)
    ekd = jnp.exp(glast - gcol)
    return dict(decay=decay, kb=kb, vb=vb, mm=mm, lmat=lmat, tinv=tinv, egc=egc, kbg=kbg, rhs=rhs,
                u=sol[:, :HEAD_DIM], w=sol[:, HEAD_DIM:], qk=qk, amat=qk * decay, qd=qh * egc, ekd=ekd,
                kd=kh * ekd, gl=jnp.exp(glast), strict=strict, lower=lower, eye=eye)


def _tri(ii, jj):
    return jnp.where(ii >= jj, 1.0, 0.0)


def _gdn_fwd(qkv, gbeta):
    s = qkv.shape[0]
    n = s // CHUNK

    def body(qkv_ref, gb_ref, o_ref, st_ref, state):
        @pl.when(pl.program_id(0) == 0)
        def _():
            state[...] = jnp.zeros_like(state)

        ii = lax.broadcasted_iota(jnp.int32, (CHUNK, CHUNK), 0)
        jj = lax.broadcasted_iota(jnp.int32, (CHUNK, CHUNK), 1)
        lane = lax.broadcasted_iota(jnp.int32, (CHUNK, 128), 1)
        gbv = gb_ref[...]
        gc = _dot_hi(_tri(ii, jj), gbv)
        for h in range(HEADS):
            sl = slice(128 * h, 128 * (h + 1))
            qh = qkv_ref[:, 128 * h:128 * (h + 1)]
            kh = qkv_ref[:, 1024 + 128 * h:1024 + 128 * (h + 1)]
            vh = qkv_ref[:, 2048 + 128 * h:2048 + 128 * (h + 1)]
            c = _chunk_local(qh, kh, vh, _col(gc, GA_LANE + h, lane), _col(gbv, GB_LANE + h, lane), ii, jj)
            st = state[sl, :]
            st_ref[sl, :] = st
            vn = c["u"] - _dot(c["w"], st)
            o_ref[:, sl] = _dot(c["qd"], st) + _dot(c["amat"], vn)
            state[sl, :] = st * c["gl"] + _dot(c["kd"], vn, TN)

    return _call(body, name="gdn_fwd", grid=(n,),
                 in_specs=[_rows(CHUNK, 3072), _rows(CHUNK, 128)],
                 out_specs=[_rows(CHUNK, 1024), _rows(HEADS * 128, 128)],
                 out_shape=[_sds((s, 1024), F32), _sds((n * HEADS * 128, 128), F32)],
                 scratch=[pltpu.VMEM((HEADS * 128, 128), F32)], args=(qkv, gbeta))


def _gdn_bwd(qkv, gbeta, states, do):
    s = qkv.shape[0]
    n = s // CHUNK

    def body(qkv_ref, gb_ref, st_ref, do_ref, dqkv_ref, dgb_ref, dstate):
        @pl.when(pl.program_id(0) == 0)
        def _():
            dstate[...] = jnp.zeros_like(dstate)

        ii = lax.broadcasted_iota(jnp.int32, (CHUNK, CHUNK), 0)
        jj = lax.broadcasted_iota(jnp.int32, (CHUNK, CHUNK), 1)
        lane = lax.broadcasted_iota(jnp.int32, (CHUNK, 128), 1)
        row1 = ii[:, :1]
        gbv = gb_ref[...]
        gc = _dot_hi(_tri(ii, jj), gbv)
        dgc_all = jnp.zeros((CHUNK, 128), F32)
        db_all = jnp.zeros((CHUNK, 128), F32)
        for h in range(HEADS):
            sl = slice(128 * h, 128 * (h + 1))
            qh = qkv_ref[:, 128 * h:128 * (h + 1)]
            kh = qkv_ref[:, 1024 + 128 * h:1024 + 128 * (h + 1)]
            vh = qkv_ref[:, 2048 + 128 * h:2048 + 128 * (h + 1)]
            bcol = _col(gbv, GB_LANE + h, lane)
            c = _chunk_local(qh, kh, vh, _col(gc, GA_LANE + h, lane), bcol, ii, jj)
            st = st_ref[sl, :]
            dst = dstate[sl, :]
            dov = do_ref[:, sl]
            vn = c["u"] - _dot(c["w"], st)
            dvn = _dot(c["amat"], dov, TN) + _dot(c["kd"], dst)
            damat = jnp.where(c["lower"], _dot(dov, vn, NT), 0.0)
            dqd = _dot(dov, st, NT)
            dkd = _dot(vn, dst, NT)
            dw = -_dot(dvn, st, NT)
            dgl = jnp.sum(jnp.sum(st * dst, axis=-1, keepdims=True), axis=0, keepdims=True)
            dstate[sl, :] = _dot(c["qd"], dov, TN) + c["gl"] * dst - _dot(c["w"], dvn, TN)
            dsol = jnp.concatenate([dvn, dw], axis=-1)
            drhs = _dot_hi(c["tinv"], dsol, TN)
            dtinv = _dot_hi(dsol, c["rhs"], NT)
            dl = -_dot_hi(_dot_hi(c["tinv"], dtinv, TN), c["tinv"], NT)
            dl = jnp.where(c["strict"], dl, 0.0)
            dmm = dl * c["decay"]
            dqk = damat * c["decay"]
            wmat = dl * c["lmat"] + damat * c["amat"]
            dgc = jnp.sum(wmat, axis=-1, keepdims=True)
            wcol = jnp.sum(wmat, axis=0, keepdims=True)
            dgc = dgc - jnp.sum(jnp.where(c["eye"], wcol, 0.0), axis=-1, keepdims=True)
            dkb = _dot(dmm, kh) + drhs[:, HEAD_DIM:] * c["egc"]
            dk = _dot(dmm, c["kb"], TN) + _dot(dqk, qh, TN) + dkd * c["ekd"]
            dq = _dot(dqk, kh) + dqd * c["egc"]
            dgc = dgc + jnp.sum(drhs[:, HEAD_DIM:] * c["kbg"], axis=-1, keepdims=True)
            dgc = dgc + jnp.sum(dqd * c["qd"], axis=-1, keepdims=True)
            tmp = jnp.sum(dkd * c["kd"], axis=-1, keepdims=True)
            dgc = dgc - tmp
            dglast = jnp.sum(tmp, axis=0, keepdims=True) + dgl * c["gl"]
            dgc = dgc + jnp.where(row1 == CHUNK - 1, dglast, 0.0)
            dk = dk + dkb * bcol
            db = jnp.sum(dkb * kh, axis=-1, keepdims=True) + jnp.sum(drhs[:, :HEAD_DIM] * vh, axis=-1, keepdims=True)
            dqkv_ref[:, 128 * h:128 * (h + 1)] = dq
            dqkv_ref[:, 1024 + 128 * h:1024 + 128 * (h + 1)] = dk
            dqkv_ref[:, 2048 + 128 * h:2048 + 128 * (h + 1)] = drhs[:, :HEAD_DIM] * bcol
            dgc_all = dgc_all + jnp.where(lane == GA_LANE + h, dgc, 0.0)
            db_all = db_all + jnp.where(lane == GB_LANE + h, db, 0.0)
        dgb_ref[...] = _dot_hi(_tri(jj, ii), dgc_all) + db_all

    rev = lambda w: pl.BlockSpec((CHUNK, w), lambda i: (n - 1 - i, 0))
    return _call(body, name="gdn_bwd", grid=(n,),
                 in_specs=[rev(3072), rev(128), pl.BlockSpec((HEADS * 128, 128), lambda i: (n - 1 - i, 0)), rev(1024)],
                 out_specs=[rev(3072), rev(128)],
                 out_shape=[_sds((s, 3072), F32), _sds((s, 128), F32)],
                 scratch=[pltpu.VMEM((HEADS * 128, 128), F32)], args=(qkv, gbeta, states, do))


NN_B = (((2,), (1,)), ((0,), (0,)))
NT_B = (((2,), (2,)), ((0,), (0,)))
TN_B = (((1,), (1,)), ((0,), (0,)))
GDN_PRE_CHUNKS = 4
GDN_POST_CHUNKS = 2
GDN_SEQ_CHUNKS = 4


def _gather_heads(qkv_ref, gc, gbv, qs, ks, vs, gs, bs, nchunks):
    lane = lax.broadcasted_iota(jnp.int32, (CHUNK, 128), 1)
    for c in range(nchunks):
        rows = slice(CHUNK * c, CHUNK * (c + 1))
        for h in range(HEADS):
            b = HEADS * c + h
            qs[b] = qkv_ref[rows, 128 * h:128 * (h + 1)]
            ks[b] = qkv_ref[rows, 1024 + 128 * h:1024 + 128 * (h + 1)]
            vs[b] = qkv_ref[rows, 2048 + 128 * h:2048 + 128 * (h + 1)]
            gs[b] = jnp.broadcast_to(_col(gc[rows], GA_LANE + h, lane), (CHUNK, 128))
            bs[b] = jnp.broadcast_to(_col(gbv[rows], GB_LANE + h, lane), (CHUNK, 128))


def _block_tri(rows, transpose=False):
    ri = lax.broadcasted_iota(jnp.int32, (rows, rows), 0)
    ci = lax.broadcasted_iota(jnp.int32, (rows, rows), 1)
    same = (ri >> 6) == (ci >> 6)
    return jnp.where(same & ((ci >= ri) if transpose else (ri >= ci)), 1.0, 0.0)


def _local_b(q, k, v, g128, b128):
    ii = lax.broadcasted_iota(jnp.int32, (1, CHUNK, CHUNK), 1)
    jj = lax.broadcasted_iota(jnp.int32, (1, CHUNK, CHUNK), 2)
    lower, strict, eye = ii >= jj, ii > jj, ii == jj
    g64 = g128[:, :, :CHUNK]
    grow = jnp.sum(jnp.where(eye, g64, 0.0), axis=1, keepdims=True)
    decay = jnp.where(lower, jnp.exp(jnp.where(lower, g64 - grow, 0.0)), 0.0)
    kb = k * b128
    vb = v * b128
    mm = lax.dot_general(kb.astype(BF16), k.astype(BF16), NT_B, preferred_element_type=F32)
    lmat = jnp.where(strict, mm * decay, 0.0)
    egc = jnp.exp(g128)
    kbg = kb * egc
    qk = lax.dot_general(q.astype(BF16), k.astype(BF16), NT_B, preferred_element_type=F32)
    row = lax.broadcasted_iota(jnp.int32, (1, CHUNK, 128), 1)
    glast = jnp.sum(jnp.where(row == CHUNK - 1, g128, 0.0), axis=1, keepdims=True)
    ekd = jnp.exp(glast - g128)
    return dict(decay=decay, kb=kb, vb=vb, lmat=lmat, egc=egc, kbg=kbg, amat=qk * decay, qd=q * egc, ekd=ekd,
                kd=k * ekd, gl=jnp.exp(glast), lower=lower, strict=strict, eye=eye)


def _bdot(a, b, dims):
    return lax.dot_general(a.astype(BF16), b.astype(BF16), dims, preferred_element_type=F32)


def _split(a):
    hi = a.astype(BF16)
    return hi, (a - hi.astype(F32)).astype(BF16)


def _bdot_hi(a, b, dims):
    ah, al = _split(a)
    bh, bl = _split(b)
    d = lambda x, y: lax.dot_general(x, y, dims, preferred_element_type=F32)
    return d(ah, bh) + d(ah, bl) + d(al, bh)


def _gdn_pre(qkv, gbeta):
    s = qkv.shape[0]
    n = s // CHUNK
    cb = min(GDN_PRE_CHUNKS, n)
    nb = cb * HEADS
    rows = cb * CHUNK

    def body(qkv_ref, gb_ref, u_ref, w_ref, qd_ref, kd_ref, a_ref, t_ref, gl_ref, qs, ks, vs, gs, bs):
        gbv = gb_ref[...]
        gc = _dot_hi(_block_tri(rows), gbv)
        _gather_heads(qkv_ref, gc, gbv, qs, ks, vs, gs, bs, cb)
        c = _local_b(qs[...], ks[...], vs[...], gs[...], bs[...])
        pw = -c["lmat"]
        tinv = jnp.where(c["eye"], 1.0, 0.0) + pw
        for _ in range(5):
            pw = _bdot_hi(pw, pw, NN_B)
            tinv = tinv + _bdot_hi(tinv, pw, NN_B)
        u_ref[...] = _bdot_hi(tinv, c["vb"], NN_B)
        w_ref[...] = _bdot_hi(tinv, c["kbg"], NN_B).astype(BF16)
        qd_ref[...] = c["qd"].astype(BF16)
        kd_ref[...] = c["kd"].astype(BF16)
        a_ref[...] = c["amat"].astype(BF16)
        t_ref[...] = tinv
        gl_ref[...] = c["gl"]

    b3 = lambda d: pl.BlockSpec((nb, CHUNK, d), lambda i: (i, 0, 0))
    nt = n * HEADS
    return _call(body, name="gdn_pre", grid=(n // cb,),
                 in_specs=[_rows(rows, 3072), _rows(rows, 128)],
                 out_specs=[b3(128), b3(128), b3(128), b3(128), b3(CHUNK), b3(CHUNK),
                            pl.BlockSpec((nb, 1, 128), lambda i: (i, 0, 0))],
                 out_shape=[_sds((nt, CHUNK, 128), F32), _sds((nt, CHUNK, 128), BF16), _sds((nt, CHUNK, 128), BF16),
                            _sds((nt, CHUNK, 128), BF16), _sds((nt, CHUNK, CHUNK), BF16), _sds((nt, CHUNK, CHUNK), F32),
                            _sds((nt, 1, 128), F32)],
                 scratch=[pltpu.VMEM((nb, CHUNK, 128), F32)] * 5, sem=("parallel",), args=(qkv, gbeta))


def _gdn_scan_fwd(u, w, qd, kd, amat, gl):
    nt = u.shape[0]
    n = nt // HEADS
    cs = min(GDN_SEQ_CHUNKS, n)

    def body(u_ref, w_ref, qd_ref, kd_ref, a_ref, gl_ref, o_ref, st_ref, state):
        @pl.when(pl.program_id(0) == 0)
        def _():
            state[...] = jnp.zeros_like(state)

        for c in range(cs):
            sl = slice(HEADS * c, HEADS * (c + 1))
            st = state[...]
            stb = st.astype(BF16)
            st_ref[sl] = stb
            vn = u_ref[sl] - lax.dot_general(w_ref[sl], stb, NN_B, preferred_element_type=F32)
            vnb = vn.astype(BF16)
            o = (lax.dot_general(qd_ref[sl], stb, NN_B, preferred_element_type=F32)
                 + lax.dot_general(a_ref[sl], vnb, NN_B, preferred_element_type=F32))
            state[...] = st * gl_ref[sl] + lax.dot_general(kd_ref[sl], vnb, TN_B, preferred_element_type=F32)
            for h in range(HEADS):
                o_ref[CHUNK * c:CHUNK * (c + 1), 128 * h:128 * (h + 1)] = o[h]

    b3 = lambda d: pl.BlockSpec((cs * HEADS, CHUNK, d), lambda i: (i, 0, 0))
    return _call(body, name="gdn_scan_fwd", grid=(n // cs,),
                 in_specs=[b3(128), b3(128), b3(128), b3(128), b3(CHUNK), pl.BlockSpec((cs * HEADS, 1, 128), lambda i: (i, 0, 0))],
                 out_specs=[_rows(cs * CHUNK, 1024), pl.BlockSpec((cs * HEADS, 128, 128), lambda i: (i, 0, 0))],
                 out_shape=[_sds((n * CHUNK, 1024), F32), _sds((nt, 128, 128), BF16)],
                 scratch=[pltpu.VMEM((HEADS, 128, 128), F32)], args=(u, w, qd, kd, amat, gl))


def _gdn_scan_bwd(w, qd, kd, amat, gl, do):
    nt = w.shape[0]
    n = nt // HEADS
    cs = min(GDN_SEQ_CHUNKS, n)
    ng = n // cs

    def body(w_ref, qd_ref, kd_ref, a_ref, gl_ref, do_ref, ds_ref, dstate, dos):
        @pl.when(pl.program_id(0) == 0)
        def _():
            dstate[...] = jnp.zeros_like(dstate)

        for c in reversed(range(cs)):
            sl = slice(HEADS * c, HEADS * (c + 1))
            for h in range(HEADS):
                dos[h] = do_ref[CHUNK * c:CHUNK * (c + 1), 128 * h:128 * (h + 1)].astype(BF16)
            dob = dos[...]
            dst = dstate[...]
            dstb = dst.astype(BF16)
            ds_ref[sl] = dstb
            dvn = (lax.dot_general(a_ref[sl], dob, TN_B, preferred_element_type=F32)
                   + lax.dot_general(kd_ref[sl], dstb, NN_B, preferred_element_type=F32))
            dstate[...] = (lax.dot_general(qd_ref[sl], dob, TN_B, preferred_element_type=F32) + gl_ref[sl] * dst
                           - lax.dot_general(w_ref[sl], dvn.astype(BF16), TN_B, preferred_element_type=F32))

    b3 = lambda d: pl.BlockSpec((cs * HEADS, CHUNK, d), lambda i: (ng - 1 - i, 0, 0))
    return _call(body, name="gdn_scan_bwd", grid=(ng,),
                 in_specs=[b3(128), b3(128), b3(128), b3(CHUNK), pl.BlockSpec((cs * HEADS, 1, 128), lambda i: (ng - 1 - i, 0, 0)),
                           pl.BlockSpec((cs * CHUNK, 1024), lambda i: (ng - 1 - i, 0))],
                 out_specs=pl.BlockSpec((cs * HEADS, 128, 128), lambda i: (ng - 1 - i, 0, 0)),
                 out_shape=_sds((nt, 128, 128), BF16),
                 scratch=[pltpu.VMEM((HEADS, 128, 128), F32), pltpu.VMEM((HEADS, CHUNK, 128), BF16)],
                 args=(w, qd, kd, amat, gl, do))


def _gdn_post_bwd(qkv, gbeta, u, w, tinv, states, dstates, do):
    s = qkv.shape[0]
    n = s // CHUNK
    cb = min(GDN_POST_CHUNKS, n)
    nb = cb * HEADS
    rows = cb * CHUNK

    def body(qkv_ref, gb_ref, u_ref, w_ref, t_ref, st_ref, ds_ref, do_ref, dqkv_ref, dgb_ref, qs, ks, vs, gs, bs, dos):
        gbv = gb_ref[...]
        gc = _dot_hi(_block_tri(rows), gbv)
        _gather_heads(qkv_ref, gc, gbv, qs, ks, vs, gs, bs, cb)
        for c in range(cb):
            for h in range(HEADS):
                dos[HEADS * c + h] = do_ref[CHUNK * c:CHUNK * (c + 1), 128 * h:128 * (h + 1)].astype(F32)
        q, k, v, b128 = qs[...], ks[...], vs[...], bs[...]
        c = _local_b(q, k, v, gs[...], b128)
        tinv, st, dst, dov = t_ref[...], st_ref[...], ds_ref[...], dos[...]
        wv = w_ref[...]
        vn = u_ref[...] - _bdot(wv, st, NN_B)
        dvn = _bdot(c["amat"], dov, TN_B) + _bdot(c["kd"], dst, NN_B)
        damat = jnp.where(c["lower"], _bdot(dov, vn, NT_B), 0.0)
        dqd = _bdot(dov, st, NT_B)
        dkd = _bdot(vn, dst, NT_B)
        dw = -_bdot(dvn, st, NT_B)
        dgl = jnp.sum(jnp.sum(st.astype(F32) * dst.astype(F32), axis=1, keepdims=True), axis=-1, keepdims=True)
        dvb = _bdot(tinv, dvn, TN_B)
        dkbg = _bdot(tinv, dw, TN_B)
        dtinv = _bdot(dvn, c["vb"], NT_B) + _bdot(dw, c["kbg"], NT_B)
        dl = -_bdot(_bdot(tinv, dtinv, TN_B), tinv, NT_B)
        dl = jnp.where(c["strict"], dl, 0.0)
        dmm = dl * c["decay"]
        dqk = damat * c["decay"]
        wmat = dl * c["lmat"] + damat * c["amat"]
        wcol = jnp.sum(wmat, axis=1, keepdims=True)
        dgc = jnp.sum(wmat, axis=-1, keepdims=True) - jnp.sum(jnp.where(c["eye"], wcol, 0.0), axis=-1, keepdims=True)
        dkb = _bdot(dmm, k, NN_B) + dkbg * c["egc"]
        dk = _bdot(dmm, c["kb"], TN_B) + _bdot(dqk, q, TN_B) + dkd * c["ekd"] + dkb * b128
        dq = _bdot(dqk, k, NN_B) + dqd * c["egc"]
        tmp = jnp.sum(dkd * c["kd"], axis=-1, keepdims=True)
        dgc = (dgc + jnp.sum(dkbg * c["kbg"], axis=-1, keepdims=True) + jnp.sum(dqd * c["qd"], axis=-1, keepdims=True)
               - tmp)
        dglast = jnp.sum(tmp, axis=1, keepdims=True) + dgl * c["gl"][:, :, :1]
        row1 = lax.broadcasted_iota(jnp.int32, (1, CHUNK, 1), 1)
        dgc = dgc + jnp.where(row1 == CHUNK - 1, dglast, 0.0)
        db = jnp.sum(dkb * k, axis=-1, keepdims=True) + jnp.sum(dvb * v, axis=-1, keepdims=True)
        dv = dvb * b128
        lane = lax.broadcasted_iota(jnp.int32, (CHUNK, 128), 1)
        parts = []
        for cc in range(cb):
            acc = jnp.zeros((CHUNK, 128), F32)
            for h in range(HEADS):
                bi = HEADS * cc + h
                rs = slice(CHUNK * cc, CHUNK * (cc + 1))
                dqkv_ref[rs, 128 * h:128 * (h + 1)] = dq[bi]
                dqkv_ref[rs, 1024 + 128 * h:1024 + 128 * (h + 1)] = dk[bi]
                dqkv_ref[rs, 2048 + 128 * h:2048 + 128 * (h + 1)] = dv[bi]
                acc = acc + jnp.where(lane == GA_LANE + h, dgc[bi], 0.0)
            parts.append(acc)
        dgc_all = jnp.concatenate(parts, axis=0)
        dg_all = _dot_hi(_block_tri(rows, transpose=True), dgc_all)
        for cc in range(cb):
            acc = dg_all[CHUNK * cc:CHUNK * (cc + 1)]
            for h in range(HEADS):
                acc = acc + jnp.where(lane == GB_LANE + h, db[HEADS * cc + h], 0.0)
            dgb_ref[CHUNK * cc:CHUNK * (cc + 1), :] = acc

    b3 = lambda d1, d2: pl.BlockSpec((nb, d1, d2), lambda i: (i, 0, 0))
    return _call(body, name="gdn_post_bwd", grid=(n // cb,),
                 in_specs=[_rows(rows, 3072), _rows(rows, 128), b3(CHUNK, 128), b3(CHUNK, 128), b3(CHUNK, CHUNK),
                           b3(128, 128), b3(128, 128), _rows(rows, 1024)],
                 out_specs=[_rows(rows, 3072), _rows(rows, 128)],
                 out_shape=[_sds((s, 3072), F32), _sds((s, 128), F32)],
                 scratch=[pltpu.VMEM((nb, CHUNK, 128), F32)] * 6, sem=("parallel",),
                 args=(qkv, gbeta, u, w, tinv, states, dstates, do))


def _mix_fwd(o_mla, o_gdn, proj, out_gain):
    s = proj.shape[0]
    tm = min(TILES["row"], s)

    def body(om_ref, og_ref, mg_ref, gg_ref, g_ref, o_ref):
        o_ref[:, :1024] = (om_ref[...] * _silu(mg_ref[...].astype(F32))).astype(BF16)
        for h in range(HEADS):
            sl = slice(128 * h, 128 * (h + 1))
            og = og_ref[:, sl]
            on = og * _rms(og, HEAD_DIM) * g_ref[...]
            o_ref[:, 1024 + 128 * h:1024 + 128 * (h + 1)] = (on * _silu(gg_ref[:, sl].astype(F32))).astype(BF16)

    return _call(body, name="mix_fwd", grid=(s // tm,),
                 in_specs=[_rows(tm, 1024), _rows(tm, 1024), _rows(tm, 1024, 1), _rows(tm, 1024, 5), _full((1, 128))],
                 out_specs=_rows(tm, 2048), out_shape=_sds((s, 2048), BF16), args=(o_mla, o_gdn, proj, proj, out_gain))


def _mix_bwd(o_mla, o_gdn, proj, out_gain, dmixed):
    s = proj.shape[0]
    tm = min(TILES["row"], s)

    def body(om_ref, og_ref, mg_ref, gg_ref, g_ref, dm_ref, dg_ref, dom_ref, dog_ref, dmg_ref, dgg_ref, ag_ref,
             delta_ref):
        @pl.when(pl.program_id(0) == 0)
        def _():
            ag_ref[...] = jnp.zeros_like(ag_ref)

        mg = mg_ref[...].astype(F32)
        dm = dm_ref[...].astype(F32)
        om = om_ref[...]
        dom = (dm * _silu(mg)).astype(BF16)
        dom_ref[...] = dom
        dmg_ref[...] = (dm * om * _dsilu(mg)).astype(BF16)
        prod = dom.astype(F32) * om
        lane = lax.broadcasted_iota(jnp.int32, (tm, 128), 1)
        delta = jnp.zeros((tm, 128), F32)
        for h in range(HEADS):
            delta = delta + jnp.where(lane == h, jnp.sum(prod[:, 128 * h:128 * (h + 1)], axis=-1, keepdims=True), 0.0)
        delta_ref[...] = delta
        for h in range(HEADS):
            sl = slice(128 * h, 128 * (h + 1))
            og, gg, d = og_ref[:, sl], gg_ref[:, sl].astype(F32), dg_ref[:, sl].astype(F32)
            on = og * _rms(og, HEAD_DIM) * g_ref[...]
            dgg_ref[:, sl] = (d * on * _dsilu(gg)).astype(BF16)
            dx, gpart = _rms_bwd(og, g_ref[...], d * _silu(gg), HEAD_DIM)
            dog_ref[:, sl] = dx.astype(BF16)
            ag_ref[...] += _acc8(gpart)

    return _call(body, name="mix_bwd", grid=(s // tm,),
                 in_specs=[_rows(tm, 1024), _rows(tm, 1024), _rows(tm, 1024, 1), _rows(tm, 1024, 5), _full((1, 128)),
                           _rows(tm, 1024, 0), _rows(tm, 1024, 1)],
                 out_specs=[_rows(tm, 1024), _rows(tm, 1024), _rows(tm, 1024), _rows(tm, 1024), _full((8, 128)),
                            _rows(tm, 128)],
                 out_shape=[_sds((s, 1024), BF16), _sds((s, 1024), BF16), _sds((s, 1024), BF16), _sds((s, 1024), BF16),
                            _sds((8, 128), F32), _sds((s, 128), F32)],
                 args=(o_mla, o_gdn, proj, proj, out_gain, dmixed, dmixed))


def _out_fwd(mixed, w_out, x, target):
    s = x.shape[0]
    tm = min(TILES["mm"], s)
    tn = min(TILES["mm"], D_MODEL)

    def body(m_ref, w_ref, x_ref, t_ref, dy_ref, acc_ref):
        err = x_ref[...] + _dot(m_ref[...], w_ref[...]) - t_ref[...]
        dy_ref[...] = (err * (1.0 / D_MODEL)).astype(BF16)

        @pl.when(pl.program_id(1) == 0)
        def _():
            acc_ref[...] = jnp.zeros_like(acc_ref)

        acc_ref[...] += _acc8(err * err)

    return _call(body, name="out_fwd", grid=(D_MODEL // tn, s // tm),
                 in_specs=[pl.BlockSpec((tm, D_MODEL), lambda j, i: (i, 0)), pl.BlockSpec((D_MODEL, tn), lambda j, i: (0, j)),
                           pl.BlockSpec((tm, tn), lambda j, i: (i, j)), pl.BlockSpec((tm, tn), lambda j, i: (i, j))],
                 out_specs=[pl.BlockSpec((tm, tn), lambda j, i: (i, j)), pl.BlockSpec((8, tn), lambda j, i: (0, j))],
                 out_shape=[_sds((s, D_MODEL), BF16), _sds((8, D_MODEL), F32)], args=(mixed, w_out, x, target))


def _row_tile(r, c):
    if r % 8 != 0:
        return r
    t = 8
    while r % (2 * t) == 0 and 2 * t * c * 4 <= (1 << 20):
        t *= 2
    return t


def _sum_arrays(parts, name, also_bf16=False):
    r, c = parts[0].shape
    tr = _row_tile(r, c)
    n = len(parts)

    def body(*refs):
        acc = refs[0][...].astype(F32)
        for p_ref in refs[1:n]:
            acc = acc + p_ref[...].astype(F32)
        refs[n][...] = acc
        if also_bf16:
            refs[n + 1][...] = acc.astype(BF16)

    nout = 2 if also_bf16 else 1
    out = _call(body, name=name, grid=(r // tr,), in_specs=[_rows(tr, c)] * n, out_specs=[_rows(tr, c)] * nout,
                out_shape=[_sds((r, c), F32), _sds((r, c), BF16)][:nout], args=tuple(parts))
    return out if also_bf16 else out[0]


def _adamw(w, g, m, v, name):
    r, c = w.shape
    c1 = 1.0 - ADAM_B1 ** ADAM_STEP
    c2 = 1.0 - ADAM_B2 ** ADAM_STEP

    def body(w_ref, g_ref, m_ref, v_ref, d_ref, nm_ref, nv_ref):
        gv = g_ref[...]
        nm = ADAM_B1 * m_ref[...] + (1.0 - ADAM_B1) * gv
        nv = ADAM_B2 * v_ref[...] + (1.0 - ADAM_B2) * (gv * gv)
        nm_ref[...] = nm
        nv_ref[...] = nv
        d_ref[...] = -ADAM_LR * ((nm / c1) / (jnp.sqrt(nv / c2) + ADAM_EPS) + ADAM_WD * w_ref[...])

    if r % 8 == 0:
        tr = _row_tile(r, c)
        grid, spec = (r // tr,), _rows(tr, c)
    else:
        tc = c
        while tc % 256 == 0 and r * tc * 4 > (3 << 19):
            tc //= 2
        grid, spec = (c // tc,), pl.BlockSpec((r, tc), lambda i: (0, i))
    return _call(body, name=name, grid=grid, in_specs=[spec] * 4, out_specs=[spec] * 3,
                 out_shape=[_sds((r, c), F32)] * 3, args=(w, g, m, v))


ANY = pl.BlockSpec(memory_space=pl.ANY)
CHIP_FLIPS = ((1, 0), (0, 1), (1, 1))


def _comm_call(body, *, name, n_in, out_shape, scratch):
    def kfn(*refs):
        body(*refs)
    return pl.pallas_call(kfn, name=name, in_specs=[ANY] * n_in, out_specs=[ANY] * len(out_shape), out_shape=out_shape,
                          scratch_shapes=list(scratch),
                          compiler_params=pltpu.CompilerParams(has_side_effects=True))


def _all_gather_chips(shards):
    na = len(shards)

    def body(*refs):
        copies = _gather_copies(refs[:na], refs[na:2 * na], *refs[2 * na:])
        _gather_start(copies)
        _gather_finish(copies)

    out_shape = [_sds((4,) + a.shape, a.dtype) for a in shards]
    sem = pltpu.SemaphoreType.DMA((na, 3))
    got = _comm_call(body, name="all_gather_weights", n_in=na, out_shape=out_shape, scratch=[sem, sem, sem, sem])(*shards)
    return _place_own_blocks(got, shards)


def _gather_copies(ins, outs, send_sems, recv_sems, fwd_send, fwd_recv):
    x, y, c = lax.axis_index("x"), lax.axis_index("y"), lax.axis_index("c")
    my_k = 2 * x + y
    direct, forwards = [], []
    for a in range(len(ins)):
        rows = ins[a].shape[0]
        for r, (fx, fy) in enumerate(CHIP_FLIPS):
            px, py = x ^ fx, y ^ fy
            if rows % 32 == 0:
                mine = pl.ds(pl.multiple_of(c * (rows // 2), 16), rows // 2)
                other = pl.ds(pl.multiple_of((1 - c) * (rows // 2), 16), rows // 2)
                rc = pltpu.make_async_remote_copy(
                    src_ref=ins[a].at[mine], dst_ref=outs[a].at[my_k, mine], send_sem=send_sems.at[a, r],
                    recv_sem=recv_sems.at[a, r], device_id=(px, py, c), device_id_type=MESH)
                landed = outs[a].at[2 * px + py, mine]
                fw = pltpu.make_async_remote_copy(
                    src_ref=landed, dst_ref=landed, send_sem=fwd_send.at[a, r], recv_sem=fwd_recv.at[a, r],
                    device_id=(x, y, 1 - c), device_id_type=MESH)
                from_sib = outs[a].at[2 * px + py, other]
                fw_in = pltpu.make_async_remote_copy(
                    src_ref=from_sib, dst_ref=from_sib, send_sem=fwd_send.at[a, r], recv_sem=fwd_recv.at[a, r],
                    device_id=(x, y, 1 - c), device_id_type=MESH)
                forwards.append((rc, fw, fw_in))
            else:
                direct.append(pltpu.make_async_remote_copy(
                    src_ref=ins[a], dst_ref=outs[a].at[my_k], send_sem=send_sems.at[a, r],
                    recv_sem=recv_sems.at[a, r], device_id=(px, py, c), device_id_type=MESH))
    return forwards, direct


def _gather_start(copies):
    forwards, direct = copies
    for rc, _, _ in forwards:
        rc.start()
    for rc in direct:
        rc.start()


def _gather_finish(copies):
    forwards, direct = copies
    for rc, fw, _ in forwards:
        rc.wait_recv()
        fw.start()
    for rc, fw, fw_in in forwards:
        rc.wait_send()
        fw.wait_send()
        fw_in.wait_recv()
    for rc in direct:
        rc.wait()


def _place_own_blocks(got, shards):
    my_k = 2 * lax.axis_index("x") + lax.axis_index("y")
    return [lax.dynamic_update_index_in_dim(g, a, my_k, 0) for g, a in zip(got, shards)]


def _norm1_fwd_gather(x, gain, shards):
    s = x.shape[0]
    tm = min(TILES["row"], s)
    ni = s // tm
    na = len(shards)

    def kfn(x_ref, g_ref, *rest):
        o_ref = rest[na]
        sems = rest[2 * na + 1:]
        i = pl.program_id(0)

        @pl.when(i == 0)
        def _():
            _gather_start(_gather_copies(rest[:na], rest[na + 1:2 * na + 1], *sems))

        xv = x_ref[...]
        r = lax.rsqrt(jnp.mean(xv * xv, axis=-1, keepdims=True) + NORM_EPS)
        o_ref[...] = (xv * r * g_ref[...]).astype(BF16)

        @pl.when(i == ni - 1)
        def _():
            _gather_finish(_gather_copies(rest[:na], rest[na + 1:2 * na + 1], *sems))

    sem = pltpu.SemaphoreType.DMA((na, 3))
    out = pl.pallas_call(
        kfn, name="norm1_fwd_gather", grid=(ni,),
        in_specs=[_rows(tm, D_MODEL), _full((1, D_MODEL))] + [ANY] * na,
        out_specs=[_rows(tm, D_MODEL)] + [ANY] * na,
        out_shape=[_sds((s, D_MODEL), BF16)] + [_sds((4,) + s_a.shape, s_a.dtype) for s_a in shards],
        scratch_shapes=[sem, sem, sem, sem],
        compiler_params=pltpu.CompilerParams(dimension_semantics=("arbitrary",), vmem_limit_bytes=VMEM_LIMIT_V7X,
                                             has_side_effects=True))(x, gain, *shards)
    return out[0], _place_own_blocks(list(out[1:]), shards)


def _matmul_nn_gather(a, b, shards, *, name, tm, tn, out_dtype):
    m, kdim = a.shape
    n = b.shape[1]
    ni, nj = m // tm, n // tn
    na = len(shards)

    def body(a_ref, b_ref, *rest):
        o_ref = rest[na]
        sems = rest[2 * na + 1:]
        i, j = pl.program_id(0), pl.program_id(1)

        @pl.when((i == 0) & (j == 0))
        def _():
            _gather_start(_gather_copies(rest[:na], rest[na + 1:2 * na + 1], *sems))

        o_ref[...] = _dot(a_ref[...], b_ref[...]).astype(out_dtype)

        @pl.when((i == ni - 1) & (j == nj - 1))
        def _():
            _gather_finish(_gather_copies(rest[:na], rest[na + 1:2 * na + 1], *sems))

    def kfn(*refs):
        body(*refs)
    sem = pltpu.SemaphoreType.DMA((na, 3))
    out = pl.pallas_call(
        kfn, name=name, grid=(ni, nj),
        in_specs=[pl.BlockSpec((tm, kdim), lambda i, j: (i, 0)), pl.BlockSpec((kdim, tn), lambda i, j: (0, j))] + [ANY] * na,
        out_specs=[pl.BlockSpec((tm, tn), lambda i, j: (i, j))] + [ANY] * na,
        out_shape=[_sds((m, n), out_dtype)] + [_sds((4,) + s_a.shape, s_a.dtype) for s_a in shards],
        scratch_shapes=[sem, sem, sem, sem],
        compiler_params=pltpu.CompilerParams(dimension_semantics=("arbitrary", "arbitrary"),
                                             vmem_limit_bytes=VMEM_LIMIT_V7X, has_side_effects=True))(a, b, *shards)
    return out[0], _place_own_blocks(list(out[1:]), shards)


def _all_reduce_small(vec):
    r = vec.shape[0]

    def body(v_ref, o_ref, gath, send_sems, recv_sems):
        x, y, c = lax.axis_index("x"), lax.axis_index("y"), lax.axis_index("c")
        me = 4 * x + 2 * y + c
        gath[me] = v_ref[...]
        copies = []
        for rel in range(1, 8):
            fx, fy, fc = (rel >> 2) & 1, (rel >> 1) & 1, rel & 1
            rc = pltpu.make_async_remote_copy(
                src_ref=v_ref, dst_ref=gath.at[me], send_sem=send_sems.at[rel - 1], recv_sem=recv_sems.at[rel - 1],
                device_id=(x ^ fx, y ^ fy, c ^ fc), device_id_type=MESH)
            rc.start()
            copies.append(rc)
        for rc in copies:
            rc.wait()
        acc = gath[0]
        for d in range(1, 8):
            acc = acc + gath[d]
        o_ref[...] = acc

    def kfn(*refs):
        body(*refs)
    vm = pl.BlockSpec(memory_space=pltpu.VMEM)
    return pl.pallas_call(kfn, name="all_reduce_small", in_specs=[vm], out_specs=vm, out_shape=_sds((r, 128), F32),
                          scratch_shapes=[pltpu.VMEM((8, r, 128), F32), pltpu.SemaphoreType.DMA((7,)),
                                          pltpu.SemaphoreType.DMA((7,))],
                          compiler_params=pltpu.CompilerParams(has_side_effects=True))(vec)


def _exchange_halves(arrs):
    na = len(arrs)

    def body(*refs):
        ins, outs = refs[:na], refs[na:2 * na]
        send_sems, recv_sems = refs[2 * na:]
        x, y, c = lax.axis_index("x"), lax.axis_index("y"), lax.axis_index("c")
        copies = []
        for a in range(na):
            half = ins[a].shape[1] // 2
            src = ins[a].at[:, pl.ds(pl.multiple_of((1 - c) * half, 8), half), :]
            rc = pltpu.make_async_remote_copy(src_ref=src, dst_ref=outs[a], send_sem=send_sems.at[a],
                                              recv_sem=recv_sems.at[a], device_id=(x, y, 1 - c), device_id_type=MESH)
            rc.start()
            copies.append(rc)
        for rc in copies:
            rc.wait()

    out_shape = [_sds((4, a.shape[1] // 2, a.shape[2]), F32) for a in arrs]
    return _comm_call(body, name="rs_pair_exchange", n_in=na, out_shape=out_shape,
                      scratch=[pltpu.SemaphoreType.DMA((na,)), pltpu.SemaphoreType.DMA((na,))])(*arrs)


def _scatter_to_chips(arrs):
    na = len(arrs)

    def body(*refs):
        ins, outs = refs[:na], refs[na:2 * na]
        send_sems, recv_sems = refs[2 * na:]
        x, y, c = lax.axis_index("x"), lax.axis_index("y"), lax.axis_index("c")
        copies = []
        for a in range(na):
            for r, (fx, fy) in enumerate(CHIP_FLIPS):
                px, py = x ^ fx, y ^ fy
                rc = pltpu.make_async_remote_copy(
                    src_ref=ins[a].at[2 * px + py], dst_ref=outs[a].at[r], send_sem=send_sems.at[a, r],
                    recv_sem=recv_sems.at[a, r], device_id=(px, py, c), device_id_type=MESH)
                rc.start()
                copies.append(rc)
        for rc in copies:
            rc.wait()

    out_shape = [_sds((3,) + a.shape[1:], a.dtype) for a in arrs]
    return _comm_call(body, name="rs_chip_scatter", n_in=na, out_shape=out_shape,
                      scratch=[pltpu.SemaphoreType.DMA((na, 3)), pltpu.SemaphoreType.DMA((na, 3))])(*arrs)


def _sum_into_half(parts, name):
    r2, c = parts[0].shape
    tr = _row_tile(r2, c)
    nb = r2 // tr
    n = len(parts)

    def kfn(c_ref, *refs):
        acc = refs[0][...].astype(F32)
        for p_ref in refs[1:n]:
            acc = acc + p_ref[...].astype(F32)
        refs[n][...] = acc

    spec = pltpu.PrefetchScalarGridSpec(
        num_scalar_prefetch=1, grid=(nb,), in_specs=[pl.BlockSpec((tr, c), lambda i, cr: (i, 0))] * n,
        out_specs=pl.BlockSpec((tr, c), lambda i, cr: (cr[0] * nb + i, 0)))
    core = lax.axis_index("c").astype(jnp.int32).reshape(1)
    return pl.pallas_call(kfn, name=name, grid_spec=spec, out_shape=_sds((2 * r2, c), F32),
                          compiler_params=pltpu.CompilerParams(dimension_semantics=("arbitrary",),
                                                               vmem_limit_bytes=VMEM_LIMIT_V7X))(core, *parts)


def _join_in_place(arrs):
    na = len(arrs)

    def body(*refs):
        outs = refs[na:2 * na]
        send_sems, recv_sems = refs[2 * na:]
        x, y, c = lax.axis_index("x"), lax.axis_index("y"), lax.axis_index("c")
        copies = []
        for a in range(na):
            half = outs[a].shape[0] // 2
            mine = outs[a].at[pl.ds(pl.multiple_of(c * half, 8), half), :]
            rc = pltpu.make_async_remote_copy(src_ref=mine, dst_ref=mine, send_sem=send_sems.at[a],
                                              recv_sem=recv_sems.at[a], device_id=(x, y, 1 - c), device_id_type=MESH)
            rc.start()
            copies.append(rc)
        for rc in copies:
            rc.wait()

    def kfn(*refs):
        body(*refs)
    return pl.pallas_call(kfn, name="rs_pair_join", in_specs=[ANY] * na, out_specs=[ANY] * na,
                          out_shape=[_sds(a.shape, F32) for a in arrs],
                          input_output_aliases={a: a for a in range(na)},
                          scratch_shapes=[pltpu.SemaphoreType.DMA((na,)), pltpu.SemaphoreType.DMA((na,))],
                          compiler_params=pltpu.CompilerParams(has_side_effects=True))(*arrs)


def _pair_sum(g, o, name):
    _, r, c = g.shape
    half = r // 2
    tr = _row_tile(half, c)
    nb = half // tr

    def kfn(c_ref, g_ref, o_ref, s32_ref, s16_ref):
        acc = g_ref[...] + o_ref[...]
        s32_ref[...] = acc
        s16_ref[...] = acc.astype(BF16)

    blk = lambda imap: pl.BlockSpec((None, tr, c), imap)
    same = lambda k, i, cr: (k, i, 0)
    spec = pltpu.PrefetchScalarGridSpec(
        num_scalar_prefetch=1, grid=(4, nb), in_specs=[blk(lambda k, i, cr: (k, cr[0] * nb + i, 0)), blk(same)],
        out_specs=[blk(same), blk(same)])
    core = lax.axis_index("c").astype(jnp.int32).reshape(1)
    return pl.pallas_call(kfn, name=name, grid_spec=spec, out_shape=[_sds((4, half, c), F32), _sds((4, half, c), BF16)],
                          compiler_params=pltpu.CompilerParams(dimension_semantics=("arbitrary", "arbitrary"),
                                                               vmem_limit_bytes=VMEM_LIMIT_V7X))(core, g, o)


def _rs_pair_stage(grads):
    got = _exchange_halves(grads)
    sums = [_pair_sum(g, o, f"rs_pair_sum_{a}") for a, (g, o) in enumerate(zip(grads, got))]
    return [s32 for s32, _ in sums], [s16 for _, s16 in sums]


def _rs_chip_stage(pair, recv):
    k_me = 2 * lax.axis_index("x") + lax.axis_index("y")
    halves = []
    for a, (p, rv) in enumerate(zip(pair, recv)):
        own = lax.dynamic_index_in_dim(p, k_me, 0, keepdims=False)
        halves.append(_sum_into_half([own, rv[0], rv[1], rv[2]], f"rs_chip_sum_{a}"))
    return _join_in_place(halves)


def _reduce_scatter(grads):
    pair, pair_bf16 = _rs_pair_stage(grads)
    return _rs_chip_stage(pair, _scatter_to_chips(pair_bf16))


def _matmul_nt_scatter(a, b, send, *, name, tm, tn, out_dtype):
    m, kdim = a.shape
    n = b.shape[0]
    ni, nj = m // tm, n // tn
    na = len(send)

    def body(a_ref, b_ref, *rest):
        send_refs, o_ref, recv_refs = rest[:na], rest[na], rest[na + 1:2 * na + 1]
        send_sems, recv_sems = rest[2 * na + 1:]
        i, j = pl.program_id(0), pl.program_id(1)

        def copies():
            x, y, c = lax.axis_index("x"), lax.axis_index("y"), lax.axis_index("c")
            out = []
            for s_i in range(na):
                for r, (fx, fy) in enumerate(CHIP_FLIPS):
                    px, py = x ^ fx, y ^ fy
                    out.append(pltpu.make_async_remote_copy(
                        src_ref=send_refs[s_i].at[2 * px + py], dst_ref=recv_refs[s_i].at[r],
                        send_sem=send_sems.at[s_i, r], recv_sem=recv_sems.at[s_i, r], device_id=(px, py, c),
                        device_id_type=MESH))
            return out

        @pl.when((i == 0) & (j == 0))
        def _():
            for cp in copies():
                cp.start()

        o_ref[...] = _dot(a_ref[...], b_ref[...], NT).astype(out_dtype)

        @pl.when((i == ni - 1) & (j == nj - 1))
        def _():
            for cp in copies():
                cp.wait()

    def kfn(*refs):
        body(*refs)
    sem = pltpu.SemaphoreType.DMA((na, 3))
    out = pl.pallas_call(
        kfn, name=name, grid=(ni, nj),
        in_specs=[pl.BlockSpec((tm, kdim), lambda i, j: (i, 0)), pl.BlockSpec((tn, kdim), lambda i, j: (j, 0))] + [ANY] * na,
        out_specs=[pl.BlockSpec((tm, tn), lambda i, j: (i, j))] + [ANY] * na,
        out_shape=[_sds((m, n), out_dtype)] + [_sds((3,) + s_a.shape[1:], s_a.dtype) for s_a in send],
        scratch_shapes=[sem, sem],
        compiler_params=pltpu.CompilerParams(dimension_semantics=("arbitrary", "arbitrary"),
                                             vmem_limit_bytes=VMEM_LIMIT_V7X, has_side_effects=True))(a, b, *send)
    return out[0], list(out[1:])


def _pad_w_in(w):
    z = jnp.zeros((w.shape[0], 1024 - 848), w.dtype)
    return jnp.concatenate([w[:, 0:832], w[:, 4928:4944], z, w[:, 832:4928], w[:, 4944:5968]], axis=1)


def _unpad_w_in(g):
    return jnp.concatenate([g[:, 0:832], g[:, 1024:5120], g[:, 832:848], g[:, 5120:6144]], axis=1)


W_IN_SHARD = W_IN_COLS // 4
W_IN_RUNS = ((0, 832, 0), (832, 4928, 1024), (4928, 4944, 832), (4944, 5968, 5120))


def _w_in_grad_blocks(p):
    def orig_cols(lo, hi):
        parts = [p[:, pa + max(lo, a) - a:pa + min(hi, b) - a] for a, b, pa in W_IN_RUNS if max(lo, a) < min(hi, b)]
        return parts[0] if len(parts) == 1 else jnp.concatenate(parts, axis=1)
    return jnp.stack([orig_cols(W_IN_SHARD * k, W_IN_SHARD * (k + 1)) for k in range(4)])


def _pad_w_in_blocks(g):
    def orig_cols(lo, hi):
        return [g[k][:, max(lo, W_IN_SHARD * k) - W_IN_SHARD * k:min(hi, W_IN_SHARD * (k + 1)) - W_IN_SHARD * k]
                for k in range(4) if max(lo, W_IN_SHARD * k) < min(hi, W_IN_SHARD * (k + 1))]
    z = jnp.zeros((g.shape[1], 1024 - 848), g.dtype)
    return jnp.concatenate(orig_cols(0, 832) + orig_cols(4928, 4944) + [z] + orig_cols(832, 4928) + orig_cols(4944, 5968),
                           axis=1)


def _pad_heads(w):
    r = w.shape[0]
    return jnp.pad(w.reshape(r, HEADS, QK_DIM), ((0, 0), (0, 0), (0, QK_PAD - QK_DIM))).reshape(r, HEADS * QK_PAD)


def _unpad_heads(w):
    r = w.shape[0]
    return w.reshape(r, HEADS, QK_PAD)[:, :, :QK_DIM].reshape(r, HEADS * QK_DIM)


def _cols_to_blocks(w):
    r = w.shape[0]
    return w.reshape(r, 4, -1).transpose(1, 0, 2)


def _blocks_to_cols(w):
    return w.transpose(1, 0, 2).reshape(w.shape[1], -1)


SMALL_ROWS = {"norm_gain": (0, 2048), "mla_q_a_gain": (16, 512), "mla_kv_a_gain": (20, 256),
              "mla_q_norm_gain": (22, 192), "mla_k_norm_gain": (24, 192), "gdn_a_log": (26, 8),
              "gdn_dt_bias": (27, 8), "gdn_out_norm_gain": (28, 128)}
LOSS_ROW = 29
SMALL_PACK_ROWS = 32
CONV_ROW = 32


def _pack_small(vals, loss=None):
    rows = []
    at = 0
    for name, (row, size) in SMALL_ROWS.items():
        assert row == at
        nr = -(-size // 128)
        rows.append(jnp.pad(vals[name].reshape(-1).astype(F32), (0, nr * 128 - size)).reshape(nr, 128))
        at += nr
    assert at == LOSS_ROW
    if loss is not None:
        rows.append(jnp.pad(loss.reshape(1, 1), ((0, 0), (0, 127))))
        at += 1
    rows.append(jnp.zeros((SMALL_PACK_ROWS - at, 128), F32))
    return jnp.concatenate(rows, axis=0)


def _unpack_small(pack, name):
    row, size = SMALL_ROWS[name]
    nr = -(-size // 128)
    return pack[row:row + nr].reshape(-1)[:size].reshape(1, size)


def _local_step(x, positions, target, norm_gain, w_in_p, q_a_gain, kv_a_gain, w_uq_p, w_ukv, q_norm_gain,
                k_norm_gain, conv_w, a_log, dt_bias, out_gain, w_out, scatter_hook=None, late_weights=None,
                first_weights=None):
    half = HALF_ROPE
    inv_freq = jnp.power(10000.0, -jnp.arange(half, dtype=F32) / half)
    ang = positions.astype(F32)[:, None] * inv_freq
    cos, sin = jnp.cos(ang), jnp.sin(ang)
    zpad = jnp.zeros((x.shape[0], 64), F32)
    cs = jnp.concatenate([cos, cos, zpad], axis=1)
    sn = jnp.concatenate([-sin, sin, zpad], axis=1)
    gq = jnp.pad(q_norm_gain.reshape(1, QK_DIM), ((0, 0), (0, QK_PAD - QK_DIM)))
    gk = jnp.pad(k_norm_gain.reshape(1, QK_DIM), ((0, 0), (0, QK_PAD - QK_DIM)))
    lane_pad = ((0, 0), (GA_LANE, 128 - GA_LANE - HEADS))
    alog128 = jnp.pad(a_log.reshape(1, HEADS), lane_pad)
    dtb128 = jnp.pad(dt_bias.reshape(1, HEADS), lane_pad)
    ng, qag, kvag, og = (norm_gain.reshape(1, -1), q_a_gain.reshape(1, -1), kv_a_gain.reshape(1, -1),
                         out_gain.reshape(1, -1))

    if first_weights is None:
        xn = _norm1_fwd(x, ng)
    else:
        shards, assemble = first_weights
        xn, gathered = _norm1_fwd_gather(x, ng, shards)
        w_in_p, conv_w = assemble(gathered)
    misc = _matmul(xn, w_in_p[:, 768:896], mode="nn", out_dtype=F32, name="in_proj_misc")
    if late_weights is None:
        proj = _matmul(xn, w_in_p, mode="nn", out_dtype=BF16, name="in_proj")
    else:
        shards, assemble = late_weights
        proj, gathered = _matmul_nn_gather(xn, w_in_p, shards, name="in_proj_gather", tm=TILES["mm"], tn=TILES["mm"],
                                           out_dtype=BF16)
        w_uq_p, w_ukv, w_out = assemble(gathered)
    cqn, ckvn = _mla_a_norm(proj, qag, kvag)
    q_pre = _matmul(cqn, w_uq_p, mode="nn", out_dtype=BF16, name="q_up")
    kv_pre = _matmul(ckvn, w_ukv, mode="nn", out_dtype=BF16, name="kv_up")
    q, k, v = _mla_post_fwd(q_pre, kv_pre, misc, cs, sn, gq, gk)
    o_mla, lse = _attn_fwd(q, k, v)
    qkv = _gdn_conv_fwd(proj, conv_w)
    gbeta = _gdn_gates_fwd(misc, alog128, dtb128)
    g_u, g_w, g_qd, g_kd, g_a, g_t, g_gl = _gdn_pre(qkv, gbeta)
    o_gdn, states = _gdn_scan_fwd(g_u, g_w, g_qd, g_kd, g_a, g_gl)
    mixed = _mix_fwd(o_mla, o_gdn, proj, og)
    dy, sq = _out_fwd(mixed, w_out, x, target)

    dmixed = _matmul(dy, w_out, mode="nt", out_dtype=BF16, name="d_mixed")
    d_w_out = _matmul(mixed, dy, mode="tn", out_dtype=F32, name="d_w_out", tk=4096)
    do_mla, do_gdn, dmg, dgg, d_out_gain, delta128 = _mix_bwd(o_mla, o_gdn, proj, og, dmixed)
    s_len = x.shape[0]
    delta_rows = delta128[:, :HEADS].T.reshape(HEADS, 1, s_len)
    dq, dk, dv = _attn_bwd(q, k, v, lse.reshape(HEADS, 1, s_len), delta_rows, do_mla)
    dq_pre, dkv_pre, dkr, d_gq, d_gk = _mla_post_bwd(q_pre, kv_pre, misc, cs, sn, gq, gk, dq, dk, dv)
    d_w_uq_p = _matmul(cqn, dq_pre, mode="tn", out_dtype=F32, name="d_w_uq", tk=1024)
    d_w_ukv = _matmul(ckvn, dkv_pre, mode="tn", out_dtype=F32, name="d_w_ukv", tk=1024)
    dcqn = _matmul(dq_pre, w_uq_p, mode="nt", out_dtype=F32, name="d_cqn")
    dckvn = _matmul(dkv_pre, w_ukv, mode="nt", out_dtype=F32, name="d_ckvn")
    dcq, dckv, d_qag, d_kvag = _mla_a_norm_bwd(proj, qag, kvag, dcqn, dckvn)
    dstates = _gdn_scan_bwd(g_w, g_qd, g_kd, g_a, g_gl, do_gdn)
    dqkv, dgbeta = _gdn_post_bwd(qkv, gbeta, g_u, g_w, g_t, states, dstates, do_gdn)
    dz = _gdn_conv_bwd_a(proj, conv_w, dqkv)
    dgx, d_conv = _gdn_conv_bwd_b(proj, conv_w, dz)
    dmisc, d_alog, d_dtb = _gdn_gates_bwd(misc, alog128, dtb128, gbeta, dgbeta, dkr)
    dproj = jnp.concatenate([dcq, dckv, dmisc, jnp.zeros((x.shape[0], 128), BF16), dmg, dgx, dgg], axis=1)
    d_w_in_p = _matmul(xn, dproj, mode="tn", out_dtype=F32, name="d_w_in", tk=4096)
    big = {"w_in": d_w_in_p, "w_uq": d_w_uq_p, "w_ukv": d_w_ukv, "w_out": d_w_out, "gdn_conv_w": d_conv}
    if scatter_hook is None:
        dxn, received = _matmul(dproj, w_in_p, mode="nt", out_dtype=BF16, name="d_xn", tm=512, tn=512), None
    else:
        dxn, received = _matmul_nt_scatter(dproj, w_in_p, scatter_hook(big), name="d_xn_scatter", tm=512, tn=512,
                                           out_dtype=BF16)
    grad_x, d_ng = _norm1_bwd(x, ng, dxn, dy)

    small = {"norm_gain": d_ng.sum(0), "mla_q_a_gain": d_qag.sum(0), "mla_kv_a_gain": d_kvag.sum(0),
             "mla_q_norm_gain": d_gq.sum(0)[:QK_DIM], "mla_k_norm_gain": d_gk.sum(0)[:QK_DIM],
             "gdn_a_log": d_alog.sum(0)[GA_LANE:GA_LANE + HEADS], "gdn_dt_bias": d_dtb.sum(0)[GA_LANE:GA_LANE + HEADS],
             "gdn_out_norm_gain": d_out_gain.sum(0)}
    return sq, grad_x, small, big, received


WEIGHTS = ["norm_gain", "w_in", "mla_q_a_gain", "mla_kv_a_gain", "w_uq", "w_ukv", "mla_q_norm_gain", "mla_k_norm_gain",
           "gdn_conv_w", "gdn_a_log", "gdn_dt_bias", "gdn_out_norm_gain", "w_out"]
BIG = ["w_in", "w_uq", "w_ukv", "w_out"]


def kernel(x, positions, norm_gain, w_in, mla_q_a_gain, mla_kv_a_gain, w_uq, w_ukv, mla_q_norm_gain, mla_k_norm_gain, gdn_conv_w, gdn_a_log, gdn_dt_bias, gdn_out_norm_gain, w_out, loss_target, m_norm_gain, m_w_in, m_mla_q_a_gain, m_mla_kv_a_gain, m_w_uq, m_w_ukv, m_mla_q_norm_gain, m_mla_k_norm_gain, m_gdn_conv_w, m_gdn_a_log, m_gdn_dt_bias, m_gdn_out_norm_gain, m_w_out, v_norm_gain, v_w_in, v_mla_q_a_gain, v_mla_kv_a_gain, v_w_uq, v_w_ukv, v_mla_q_norm_gain, v_mla_k_norm_gain, v_gdn_conv_w, v_gdn_a_log, v_gdn_dt_bias, v_gdn_out_norm_gain, v_w_out):
    w = dict(norm_gain=norm_gain, w_in=w_in, mla_q_a_gain=mla_q_a_gain, mla_kv_a_gain=mla_kv_a_gain, w_uq=w_uq,
             w_ukv=w_ukv, mla_q_norm_gain=mla_q_norm_gain, mla_k_norm_gain=mla_k_norm_gain, gdn_conv_w=gdn_conv_w,
             gdn_a_log=gdn_a_log, gdn_dt_bias=gdn_dt_bias, gdn_out_norm_gain=gdn_out_norm_gain, w_out=w_out)
    m = dict(norm_gain=m_norm_gain, w_in=m_w_in, mla_q_a_gain=m_mla_q_a_gain, mla_kv_a_gain=m_mla_kv_a_gain,
             w_uq=m_w_uq, w_ukv=m_w_ukv, mla_q_norm_gain=m_mla_q_norm_gain, mla_k_norm_gain=m_mla_k_norm_gain,
             gdn_conv_w=m_gdn_conv_w, gdn_a_log=m_gdn_a_log, gdn_dt_bias=m_gdn_dt_bias,
             gdn_out_norm_gain=m_gdn_out_norm_gain, w_out=m_w_out)
    v = dict(norm_gain=v_norm_gain, w_in=v_w_in, mla_q_a_gain=v_mla_q_a_gain, mla_kv_a_gain=v_mla_kv_a_gain,
             w_uq=v_w_uq, w_ukv=v_w_ukv, mla_q_norm_gain=v_mla_q_norm_gain, mla_k_norm_gain=v_mla_k_norm_gain,
             gdn_conv_w=v_gdn_conv_w, gdn_a_log=v_gdn_a_log, gdn_dt_bias=v_gdn_dt_bias,
             gdn_out_norm_gain=v_gdn_out_norm_gain, w_out=v_w_out)
    k_me = 2 * lax.axis_index("x") + lax.axis_index("y")

    first_weights = ([w_in[0].astype(BF16), gdn_conv_w[0]], lambda g: (_pad_w_in_blocks(g[0]), _blocks_to_cols(g[1])))
    late_weights = ([w_uq[0].astype(BF16), w_ukv[0].astype(BF16), w_out[0].astype(BF16)],
                    lambda g: (_pad_heads(_blocks_to_cols(g[0])), _blocks_to_cols(g[1]), g[2].reshape(D_MODEL, D_MODEL)))

    pair_sums = []

    def scatter_hook(big):
        pair, pair_bf16 = _rs_pair_stage([
            _w_in_grad_blocks(big["w_in"]), _cols_to_blocks(_unpad_heads(big["w_uq"])),
            _cols_to_blocks(big["w_ukv"]), big["w_out"].reshape(4, 512, D_MODEL)])
        pair_sums.extend(pair)
        return pair_bf16

    sq, grad_x, small, big, received = _local_step(
        x[0], positions[0], loss_target[0], norm_gain, None, mla_q_a_gain, mla_kv_a_gain, None, None,
        mla_q_norm_gain, mla_k_norm_gain, None, gdn_a_log, gdn_dt_bias, gdn_out_norm_gain, None, scatter_hook,
        late_weights, first_weights)

    loss_local = (0.5 / D_MODEL) * jnp.sum(sq)
    pack = jnp.concatenate([_pack_small(small, loss_local), big["gdn_conv_w"].reshape(96, 128)], axis=0)
    tot = _all_reduce_small(pack)
    loss = tot[LOSS_ROW, 0]
    conv_grad = lax.dynamic_slice_in_dim(tot[CONV_ROW:].reshape(4, 3072), k_me * 768, 768, axis=1)

    shard_grads = _rs_chip_stage(pair_sums, received)

    grads = {n: _unpack_small(tot, n) for n in SMALL_ROWS}
    grads["gdn_conv_w"] = conv_grad[None]
    for n, g in zip(BIG, shard_grads):
        grads[n] = g[None]

    delta, new_m, new_v = {}, {}, {}
    sw = _pack_small({n: w[n] for n in SMALL_ROWS})
    sm = _pack_small({n: m[n] for n in SMALL_ROWS})
    sv = _pack_small({n: v[n] for n in SMALL_ROWS})
    sd, snm, snv = _adamw(sw, tot[:SMALL_PACK_ROWS], sm, sv, "adamw_small")
    for n in SMALL_ROWS:
        delta[n], new_m[n], new_v[n] = _unpack_small(sd, n), _unpack_small(snm, n), _unpack_small(snv, n)
    for n in BIG + ["gdn_conv_w"]:
        if n == "w_in":
            d, nm, nv = _adamw(w[n][0].T, grads[n][0].T, m[n][0].T, v[n][0].T, f"adamw_{n}")
            delta[n], new_m[n], new_v[n] = d.T[None], nm.T[None], nv.T[None]
        else:
            d, nm, nv = _adamw(w[n][0], grads[n][0], m[n][0], v[n][0], f"adamw_{n}")
            delta[n], new_m[n], new_v[n] = d[None], nm[None], nv[None]

    return (loss, grad_x[None], *[grads[n] for n in WEIGHTS], *[delta[n] for n in WEIGHTS],
            *[new_m[n] for n in WEIGHTS], *[new_v[n] for n in WEIGHTS])
```

```python
import functools
import math

import jax
import jax.numpy as jnp
from jax import lax
from jax.experimental import pallas as pl
from jax.experimental.pallas import tpu as pltpu

F32 = jnp.float32
BF16 = jnp.bfloat16
MESH = pl.DeviceIdType.MESH

D_MODEL = 2048
HEADS = 8
HEAD_DIM = 128
QK_DIM = 192
QK_PAD = 256
HALF_ROPE = 32
CHUNK = 64
NORM_EPS = 1e-6
W_IN_COLS = 5968
W_IN_PAD = 6144
GA_LANE = 64
GB_LANE = 72
ADAM_LR, ADAM_B1, ADAM_B2, ADAM_EPS, ADAM_WD, ADAM_STEP = 0.001, 0.9, 0.999, 1e-08, 0.01, 10
VMEM_LIMIT_V7X = 52 * 1024 * 1024
HI = lax.Precision.HIGHEST
NN = (((1,), (0,)), ((), ()))
NT = (((1,), (1,)), ((), ()))
TN = (((0,), (0,)), ((), ()))

TILES = {"row": 512, "attn": 2048, "mm": 1024}


def _call(body, *, name, grid, in_specs, out_specs, out_shape, args, scratch=(), sem=None):
    def kfn(*refs):
        body(*refs)
    if sem is None:
        sem = ("arbitrary",) * len(grid)
    return pl.pallas_call(
        kfn, name=name, grid=grid, in_specs=in_specs, out_specs=out_specs, out_shape=out_shape,
        scratch_shapes=list(scratch),
        compiler_params=pltpu.CompilerParams(dimension_semantics=sem, vmem_limit_bytes=VMEM_LIMIT_V7X),
    )(*args)


def _rows(tm, w, cb=0):
    return pl.BlockSpec((tm, w), lambda i: (i, cb))


def _full(shape):
    n = len(shape)
    return pl.BlockSpec(shape, lambda *_: (0,) * n)


def _sds(shape, dtype):
    return jax.ShapeDtypeStruct(shape, dtype)


def _acc8(x):
    tm, c = x.shape
    return jnp.sum(x.reshape(tm // 8, 8, c), axis=0)


def _sigmoid(x):
    return 1.0 / (1.0 + jnp.exp(-x))


def _silu(x):
    return x * _sigmoid(x)


def _dsilu(x):
    s = _sigmoid(x)
    return s * (1.0 + x * (1.0 - s))


def _dot(a, b, dims=NN):
    return lax.dot_general(a.astype(BF16), b.astype(BF16), dims, preferred_element_type=F32)


def _dot_hi(a, b, dims=NN):
    return lax.dot_general(a, b, dims, precision=HI, preferred_element_type=F32)


def _matmul(a, b, *, mode, out_dtype, name, tm=None, tn=None, tk=None):
    if mode == "tn":
        kdim, m = a.shape
    else:
        m, kdim = a.shape
    n = b.shape[0] if mode == "nt" else b.shape[1]
    tm = min(tm or TILES["mm"], m)
    tn = min(tn or TILES["mm"], n)
    tk = min(tk or kdim, kdim)
    nk = kdim // tk
    dims = {"nn": NN, "nt": NT, "tn": TN}[mode]
    if mode == "tn":
        a_spec = pl.BlockSpec((tk, tm), lambda i, j, k: (k, i))
    else:
        a_spec = pl.BlockSpec((tm, tk), lambda i, j, k: (i, k))
    if mode == "nt":
        b_spec = pl.BlockSpec((tn, tk), lambda i, j, k: (j, k))
    else:
        b_spec = pl.BlockSpec((tk, tn), lambda i, j, k: (k, j))

    def body(a_ref, b_ref, o_ref):
        r = _dot(a_ref[...], b_ref[...], dims)
        if nk == 1:
            o_ref[...] = r.astype(o_ref.dtype)
        else:
            k = pl.program_id(2)

            @pl.when(k == 0)
            def _():
                o_ref[...] = r

            @pl.when(k > 0)
            def _():
                o_ref[...] += r

    if nk > 1:
        assert out_dtype == F32
    return _call(body, name=name, grid=(m // tm, n // tn, nk), in_specs=[a_spec, b_spec],
                 out_specs=pl.BlockSpec((tm, tn), lambda i, j, k: (i, j)), out_shape=_sds((m, n), out_dtype),
                 args=(a, b))


def _norm1_fwd(x, gain):
    s = x.shape[0]
    tm = min(TILES["row"], s)

    def body(x_ref, g_ref, o_ref):
        xv = x_ref[...]
        r = lax.rsqrt(jnp.mean(xv * xv, axis=-1, keepdims=True) + NORM_EPS)
        o_ref[...] = (xv * r * g_ref[...]).astype(BF16)

    return _call(body, name="norm1_fwd", grid=(s // tm,), in_specs=[_rows(tm, D_MODEL), _full((1, D_MODEL))],
                 out_specs=_rows(tm, D_MODEL), out_shape=_sds((s, D_MODEL), BF16), args=(x, gain))


def _norm1_bwd(x, gain, dxn, dy):
    s = x.shape[0]
    tm = min(TILES["row"], s)

    def body(x_ref, g_ref, dxn_ref, dy_ref, gx_ref, dg_ref):
        xv = x_ref[...]
        r = lax.rsqrt(jnp.mean(xv * xv, axis=-1, keepdims=True) + NORM_EPS)
        nrm = xv * r
        d = dxn_ref[...].astype(F32)
        dn = d * g_ref[...]
        gx_ref[...] = dy_ref[...].astype(F32) + r * (dn - nrm * jnp.mean(dn * nrm, axis=-1, keepdims=True))

        @pl.when(pl.program_id(0) == 0)
        def _():
            dg_ref[...] = jnp.zeros_like(dg_ref)

        dg_ref[...] += _acc8(d * nrm)

    return _call(body, name="norm1_bwd", grid=(s // tm,),
                 in_specs=[_rows(tm, D_MODEL), _full((1, D_MODEL)), _rows(tm, D_MODEL), _rows(tm, D_MODEL)],
                 out_specs=[_rows(tm, D_MODEL), _full((8, D_MODEL))],
                 out_shape=[_sds((s, D_MODEL), F32), _sds((8, D_MODEL), F32)], args=(x, gain, dxn, dy))


def _rms(xv, width):
    return lax.rsqrt(jnp.sum(xv * xv, axis=-1, keepdims=True) * (1.0 / width) + NORM_EPS)


def _mla_a_norm(proj, gq, gkv):
    s = proj.shape[0]
    tm = min(TILES["row"], s)

    def body(cq_ref, ckv_ref, gq_ref, gkv_ref, oq_ref, okv_ref):
        a = cq_ref[...].astype(F32)
        oq_ref[...] = (a * _rms(a, 512) * gq_ref[...]).astype(BF16)
        b = ckv_ref[...].astype(F32)
        okv_ref[...] = (b * _rms(b, 256) * gkv_ref[...]).astype(BF16)

    return _call(body, name="mla_a_norm", grid=(s // tm,),
                 in_specs=[_rows(tm, 512, 0), _rows(tm, 256, 2), _full((1, 512)), _full((1, 256))],
                 out_specs=[_rows(tm, 512), _rows(tm, 256)],
                 out_shape=[_sds((s, 512), BF16), _sds((s, 256), BF16)], args=(proj, proj, gq, gkv))


def _rms_bwd(xv, gain, d, width):
    r = _rms(xv, width)
    nrm = xv * r
    dn = d * gain
    dx = r * (dn - nrm * (jnp.sum(dn * nrm, axis=-1, keepdims=True) * (1.0 / width)))
    return dx, d * nrm


def _mla_a_norm_bwd(proj, gq, gkv, dcqn, dckvn):
    s = proj.shape[0]
    tm = min(TILES["row"], s)

    def body(cq_ref, ckv_ref, gq_ref, gkv_ref, dq_ref, dkv_ref, oq_ref, okv_ref, aq_ref, akv_ref):
        dxq, gq_part = _rms_bwd(cq_ref[...].astype(F32), gq_ref[...], dq_ref[...].astype(F32), 512)
        dxk, gk_part = _rms_bwd(ckv_ref[...].astype(F32), gkv_ref[...], dkv_ref[...].astype(F32), 256)
        oq_ref[...] = dxq.astype(BF16)
        okv_ref[...] = dxk.astype(BF16)

        @pl.when(pl.program_id(0) == 0)
        def _():
            aq_ref[...] = jnp.zeros_like(aq_ref)
            akv_ref[...] = jnp.zeros_like(akv_ref)

        aq_ref[...] += _acc8(gq_part)
        akv_ref[...] += _acc8(gk_part)

    return _call(body, name="mla_a_norm_bwd", grid=(s // tm,),
                 in_specs=[_rows(tm, 512, 0), _rows(tm, 256, 2), _full((1, 512)), _full((1, 256)),
                           _rows(tm, 512), _rows(tm, 256)],
                 out_specs=[_rows(tm, 512), _rows(tm, 256), _full((8, 512)), _full((8, 256))],
                 out_shape=[_sds((s, 512), BF16), _sds((s, 256), BF16), _sds((8, 512), F32), _sds((8, 256), F32)],
                 args=(proj, proj, gq, gkv, dcqn, dckvn))


def _swap32(r):
    lane = lax.broadcasted_iota(jnp.int32, r.shape, 1)
    return jnp.where(lane < HALF_ROPE, pltpu.roll(r, 128 - HALF_ROPE, 1), pltpu.roll(r, HALF_ROPE, 1))


def _mla_post_fwd(q_pre, kv_pre, proj, cs, sn, gq, gk):
    s = q_pre.shape[0]
    tm = min(TILES["row"], s)

    def body(qp_ref, kvp_ref, misc_ref, cs_ref, sn_ref, gq_ref, gk_ref, q_ref, k_ref, v_ref):
        csv, snv = cs_ref[...], sn_ref[...]
        lane = lax.broadcasted_iota(jnp.int32, (tm, 128), 1)
        kr = jnp.where(lane < 64, misc_ref[...], 0.0)
        for h in range(HEADS):
            for src, g_ref, o_ref in ((None, gq_ref, q_ref), (kr, gk_ref, k_ref)):
                if src is None:
                    xv = qp_ref[:, QK_PAD * h:QK_PAD * (h + 1)].astype(F32)
                else:
                    xv = jnp.concatenate([kvp_ref[:, 256 * h:256 * h + 128].astype(F32), src], axis=-1)
                y = xv * _rms(xv, QK_DIM) * g_ref[...]
                if src is None:
                    y = y * Q_PRESCALE
                hi = y[:, 128:]
                hi = hi * csv + _swap32(hi) * snv
                o_ref[:, QK_PAD * h:QK_PAD * h + 128] = y[:, :128].astype(BF16)
                o_ref[:, QK_PAD * h + 128:QK_PAD * (h + 1)] = hi.astype(BF16)
            v_ref[:, 128 * h:128 * (h + 1)] = kvp_ref[:, 256 * h + 128:256 * (h + 1)].astype(BF16)

    return _call(body, name="mla_post_fwd", grid=(s // tm,),
                 in_specs=[_rows(tm, 2048), _rows(tm, 2048), _rows(tm, 128), _rows(tm, 128), _rows(tm, 128),
                           _full((1, QK_PAD)), _full((1, QK_PAD))],
                 out_specs=[_rows(tm, 2048), _rows(tm, 2048), _rows(tm, 1024)],
                 out_shape=[_sds((s, 2048), BF16), _sds((s, 2048), BF16), _sds((s, 1024), BF16)],
                 args=(q_pre, kv_pre, proj, cs, sn, gq, gk))


def _mla_post_bwd(q_pre, kv_pre, proj, cs, sn, gq, gk, dq, dk, dv):
    s = q_pre.shape[0]
    tm = min(TILES["row"], s)

    def body(qp_ref, kvp_ref, misc_ref, cs_ref, sn_ref, gq_ref, gk_ref, dq_ref, dk_ref, dv_ref,
             oq_ref, okv_ref, okr_ref, agq_ref, agk_ref):
        csv, snv = cs_ref[...], sn_ref[...]
        lane = lax.broadcasted_iota(jnp.int32, (tm, 128), 1)
        kr = jnp.where(lane < 64, misc_ref[...], 0.0)

        @pl.when(pl.program_id(0) == 0)
        def _():
            agq_ref[...] = jnp.zeros_like(agq_ref)
            agk_ref[...] = jnp.zeros_like(agk_ref)

        dkr = jnp.zeros((tm, 128), F32)
        for h in range(HEADS):
            for which in (0, 1):
                if which == 0:
                    xv = qp_ref[:, QK_PAD * h:QK_PAD * (h + 1)].astype(F32)
                    d_ref, g_ref, a_ref = dq_ref, gq_ref, agq_ref
                else:
                    xv = jnp.concatenate([kvp_ref[:, 256 * h:256 * h + 128].astype(F32), kr], axis=-1)
                    d_ref, g_ref, a_ref = dk_ref, gk_ref, agk_ref
                dhi = d_ref[:, QK_PAD * h + 128:QK_PAD * (h + 1)]
                dhi = dhi * csv - _swap32(dhi) * snv
                dyv = jnp.concatenate([d_ref[:, QK_PAD * h:QK_PAD * h + 128], dhi], axis=-1)
                if which == 0:
                    dyv = dyv * ATTN_SCALE
                dx, gpart = _rms_bwd(xv, g_ref[...], dyv, QK_DIM)
                a_ref[...] += _acc8(gpart)
                if which == 0:
                    oq_ref[:, QK_PAD * h:QK_PAD * (h + 1)] = dx.astype(BF16)
                else:
                    okv_ref[:, 256 * h:256 * h + 128] = dx[:, :128].astype(BF16)
                    dkr = dkr + dx[:, 128:]
            okv_ref[:, 256 * h + 128:256 * (h + 1)] = dv_ref[:, 128 * h:128 * (h + 1)].astype(BF16)
        okr_ref[...] = dkr

    return _call(body, name="mla_post_bwd", grid=(s // tm,),
                 in_specs=[_rows(tm, 2048), _rows(tm, 2048), _rows(tm, 128), _rows(tm, 128), _rows(tm, 128),
                           _full((1, QK_PAD)), _full((1, QK_PAD)), _rows(tm, 2048), _rows(tm, 2048), _rows(tm, 1024)],
                 out_specs=[_rows(tm, 2048), _rows(tm, 2048), _rows(tm, 128), _full((8, QK_PAD)), _full((8, QK_PAD))],
                 out_shape=[_sds((s, 2048), BF16), _sds((s, 2048), BF16), _sds((s, 128), F32),
                            _sds((8, QK_PAD), F32), _sds((8, QK_PAD), F32)],
                 args=(q_pre, kv_pre, proj, cs, sn, gq, gk, dq, dk, dv))


ATTN_SCALE = QK_DIM ** -0.5
NEG = -1e30


LOG2E = 1.4426950408889634
LN2 = 0.6931471805599453
Q_PRESCALE = ATTN_SCALE * LOG2E
ATTN_SUB_FWD = 256
ATTN_SUB_BWD = 256


def _causal_pairs(nq, kv_major):
    prs = [(i, j) for i in range(nq) for j in range(i + 1)]
    if kv_major:
        prs.sort(key=lambda ij: (ij[1], ij[0]))
    return (jnp.asarray([p[0] for p in prs], jnp.int32), jnp.asarray([p[1] for p in prs], jnp.int32))


def _pair_call(body, *, name, tables, in_specs, out_specs, out_shape, scratch, args):
    def kfn(*refs):
        body(*refs)
    spec = pltpu.PrefetchScalarGridSpec(num_scalar_prefetch=2, grid=(HEADS, tables[0].shape[0]), in_specs=in_specs,
                                        out_specs=out_specs, scratch_shapes=list(scratch))
    return pl.pallas_call(
        kfn, name=name, grid_spec=spec, out_shape=out_shape,
        compiler_params=pltpu.CompilerParams(dimension_semantics=("parallel", "arbitrary"),
                                             vmem_limit_bytes=VMEM_LIMIT_V7X))(*tables, *args)


def _diag_mask(sc, ts, qs):
    row = lax.broadcasted_iota(jnp.int32, sc.shape, 0) + qs * ts
    col = lax.broadcasted_iota(jnp.int32, sc.shape, 1)
    return jnp.where(col <= row, sc, NEG)


def _attn_fwd(q, k, v):
    s = q.shape[0]
    t = min(TILES["attn"], s)
    ts = min(ATTN_SUB_FWD, t)
    nq = s // t
    nsub = t // ts

    def q_block(i, q_ref, k_ref, v_ref, o_ref, lse_ref):
        def scores(qs):
            first = i * t + qs * ts
            sc = lax.dot_general(q_ref[qs * ts:(qs + 1) * ts, :], k_ref[0:first + ts, :], NT,
                                 preferred_element_type=F32)
            tail = sc[:, first:]
            row = lax.broadcasted_iota(jnp.int32, tail.shape, 0)
            col = lax.broadcasted_iota(jnp.int32, tail.shape, 1)
            tail = jnp.where(col <= row, tail, NEG)
            return tail if first == 0 else jnp.concatenate([sc[:, :first], tail], axis=1)

        sc_next = scores(0)
        for qs in range(nsub):
            rq = slice(qs * ts, (qs + 1) * ts)
            kw = i * t + (qs + 1) * ts
            sc = sc_next
            if qs + 1 < nsub:
                sc_next = scores(qs + 1)
            m = jnp.max(sc, axis=-1, keepdims=True)
            p = jnp.exp2(sc - m)
            l = jnp.sum(p, axis=-1, keepdims=True)
            pv = lax.dot_general(p.astype(BF16), v_ref[0:kw, :], NN, preferred_element_type=F32)
            o_ref[rq, :] = pv / l
            lse_ref[rq, :] = m + jnp.log2(l)

    def body(q_ref, k_ref, v_ref, o_ref, lse_ref):
        i = pl.program_id(1)
        for ii in range(nq):
            @pl.when(i == ii)
            def _(ii=ii):
                q_block(ii, q_ref, k_ref, v_ref, o_ref, lse_ref)

    return _call(
        body, name="attn_fwd", grid=(HEADS, nq),
        in_specs=[pl.BlockSpec((t, QK_PAD), lambda h, i: (i, h)), pl.BlockSpec((s, QK_PAD), lambda h, i: (0, h)),
                  pl.BlockSpec((s, HEAD_DIM), lambda h, i: (0, h))],
        out_specs=[pl.BlockSpec((t, HEAD_DIM), lambda h, i: (i, h)), pl.BlockSpec((None, t, 1), lambda h, i: (h, i, 0))],
        out_shape=[_sds((s, HEADS * HEAD_DIM), F32), _sds((HEADS, s, 1), F32)],
        sem=("parallel", "arbitrary"), args=(q, k, v))


def _attn_bwd(q, k, v, lse_rows, delta_rows, do):
    s = q.shape[0]
    t = min(TILES["attn"], s)
    ts = min(ATTN_SUB_BWD, t)
    nq = s // t

    def slabs(q_ref, k_ref, v_ref, lse_ref, delta_ref, do_ref, dq_ref, dk_ref, dv_ref, i, diag):
        def products(qs):
            rq = slice(qs * ts, (qs + 1) * ts)
            kw = (qs + 1) * ts if diag else t
            qv, dob = q_ref[rq, :], do_ref[rq, :]
            sct = lax.dot_general(k_ref[0:kw, :], qv, NT, preferred_element_type=F32)
            dpt = lax.dot_general(v_ref[0:kw, :], dob, NT, preferred_element_type=F32)
            if diag:
                row = lax.broadcasted_iota(jnp.int32, sct.shape, 0)
                col = lax.broadcasted_iota(jnp.int32, sct.shape, 1) + qs * ts
                sct = jnp.where(row <= col, sct, NEG)
            return qv, dob, sct, dpt

        nsub = t // ts
        ahead = products(0)
        for qs in range(nsub):
            rq = slice(qs * ts, (qs + 1) * ts)
            kw = (qs + 1) * ts if diag else t
            qv, dob, sct, dpt = ahead
            if qs + 1 < nsub:
                ahead = products(qs + 1)
            pt = jnp.exp2(sct - lse_ref[:, rq])
            dv_ref[0:kw, :] += lax.dot_general(pt.astype(BF16), dob, NN, preferred_element_type=F32)
            dst = (pt * (dpt - delta_ref[:, rq])).astype(BF16)
            dk_ref[0:kw, :] += lax.dot_general(dst, qv, NN, preferred_element_type=F32)
            rows = pl.ds(pl.multiple_of(i * t + qs * ts, ts), ts)
            dq_ref[rows, :] += lax.dot_general(dst, k_ref[0:kw, :], TN, preferred_element_type=F32)

    def body(it_ref, jt_ref, q_ref, k_ref, v_ref, lse_ref, delta_ref, do_ref, dq_ref, dk_ref, dv_ref):
        p = pl.program_id(1)
        i, j = it_ref[p], jt_ref[p]
        refs = (q_ref, k_ref, v_ref, lse_ref, delta_ref, do_ref, dq_ref, dk_ref, dv_ref)

        @pl.when(p == 0)
        def _():
            dq_ref[...] = jnp.zeros_like(dq_ref)

        @pl.when(i == j)
        def _():
            dk_ref[...] = jnp.zeros_like(dk_ref)
            dv_ref[...] = jnp.zeros_like(dv_ref)

        @pl.when(i > j)
        def _():
            slabs(*refs, i, False)

        @pl.when(i == j)
        def _():
            slabs(*refs, i, True)

        @pl.when(i == nq - 1)
        def _():
            dk_ref[...] = dk_ref[...] * LN2

    qb = lambda h, p, it, jt: (it[p], h)
    kb = lambda h, p, it, jt: (jt[p], h)
    rowb = pl.BlockSpec((None, 1, t), lambda h, p, it, jt: (h, 0, it[p]))
    return _pair_call(
        body, name="attn_bwd", tables=_causal_pairs(nq, kv_major=True),
        in_specs=[pl.BlockSpec((t, QK_PAD), qb), pl.BlockSpec((t, QK_PAD), kb), pl.BlockSpec((t, HEAD_DIM), kb),
                  rowb, rowb, pl.BlockSpec((t, HEAD_DIM), qb)],
        out_specs=[pl.BlockSpec((s, QK_PAD), lambda h, p, it, jt: (0, h)), pl.BlockSpec((t, QK_PAD), kb),
                   pl.BlockSpec((t, HEAD_DIM), kb)],
        out_shape=[_sds((s, HEADS * QK_PAD), F32), _sds((s, HEADS * QK_PAD), F32), _sds((s, HEADS * HEAD_DIM), F32)],
        scratch=(), args=(q, k, v, lse_rows, delta_rows, do))


GDN_Q_SCALE = HEAD_DIM ** -0.5


def _shift_down(xv, prev8, sft):
    rolled = pltpu.roll(xv, sft, 0)
    top = pltpu.roll(jnp.concatenate([prev8, xv[:8]], axis=0), sft, 0)[8:]
    return jnp.concatenate([top, rolled[8:]], axis=0)


def _shift_up(xv, next8, sft):
    tm = xv.shape[0]
    rolled = pltpu.roll(xv, tm - sft, 0)
    bot = pltpu.roll(jnp.concatenate([xv[tm - 8:], next8], axis=0), 16 - sft, 0)[:8]
    return jnp.concatenate([rolled[:tm - 8], bot], axis=0)


def _conv_z(xv, prev8, w_ref):
    z = xv * w_ref[3:4, :]
    for sft in (1, 2, 3):
        z = z + _shift_down(xv, prev8, sft) * w_ref[3 - sft:4 - sft, :]
    return z


def _conv_specs(s, tm):
    nb16 = tm // 16
    cur = pl.BlockSpec((tm, 1024), lambda j, i: (i, 2 + j))
    prev = pl.BlockSpec((16, 1024), lambda j, i: (jnp.maximum(i * nb16 - 1, 0), 2 + j))
    return cur, prev


def _prev8(xp_ref, i):
    return jnp.where(i > 0, xp_ref[...].astype(F32)[8:], 0.0)


def _gdn_conv_fwd(proj, conv_w):
    s = proj.shape[0]
    tm = min(TILES["row"], s)
    cur, prev = _conv_specs(s, tm)

    def body(x_ref, xp_ref, w_ref, o_ref):
        j, i = pl.program_id(0), pl.program_id(1)
        a = _silu(_conv_z(x_ref[...].astype(F32), _prev8(xp_ref, i), w_ref))
        qk_scale = jnp.where(j == 0, GDN_Q_SCALE, 1.0)
        for h in range(HEADS):
            seg = a[:, 128 * h:128 * (h + 1)]
            r = lax.rsqrt(jnp.sum(seg * seg, axis=-1, keepdims=True) + NORM_EPS)
            o_ref[:, 128 * h:128 * (h + 1)] = jnp.where(j < 2, seg * r * qk_scale, seg)

    return _call(body, name="gdn_conv_fwd", grid=(3, s // tm),
                 in_specs=[cur, prev, pl.BlockSpec((4, 1024), lambda j, i: (0, j))],
                 out_specs=pl.BlockSpec((tm, 1024), lambda j, i: (i, j)), out_shape=_sds((s, 3072), F32),
                 args=(proj, proj, conv_w))


def _gdn_conv_bwd_a(proj, conv_w, dqkv):
    s = proj.shape[0]
    tm = min(TILES["row"], s)
    cur, prev = _conv_specs(s, tm)

    def body(x_ref, xp_ref, w_ref, d_ref, o_ref):
        j, i = pl.program_id(0), pl.program_id(1)
        z = _conv_z(x_ref[...].astype(F32), _prev8(xp_ref, i), w_ref)
        a = _silu(z)
        dsl = _dsilu(z)
        qk_scale = jnp.where(j == 0, GDN_Q_SCALE, 1.0)
        for h in range(HEADS):
            sl = slice(128 * h, 128 * (h + 1))
            seg = a[:, sl]
            dyv = d_ref[:, sl]
            r = lax.rsqrt(jnp.sum(seg * seg, axis=-1, keepdims=True) + NORM_EPS)
            yh = seg * r
            da_n = qk_scale * r * (dyv - yh * jnp.sum(yh * dyv, axis=-1, keepdims=True))
            o_ref[:, sl] = jnp.where(j < 2, da_n, dyv) * dsl[:, sl]

    return _call(body, name="gdn_conv_bwd_a", grid=(3, s // tm),
                 in_specs=[cur, prev, pl.BlockSpec((4, 1024), lambda j, i: (0, j)),
                           pl.BlockSpec((tm, 1024), lambda j, i: (i, j))],
                 out_specs=pl.BlockSpec((tm, 1024), lambda j, i: (i, j)), out_shape=_sds((s, 3072), F32),
                 args=(proj, proj, conv_w, dqkv))


def _gdn_conv_bwd_b(proj, conv_w, dz):
    s = proj.shape[0]
    tm = min(TILES["row"], s)
    nb8 = tm // 8
    last8 = s // 8 - 1
    cur, prev = _conv_specs(s, tm)

    def body(x_ref, w_ref, dz_ref, dzn_ref, dx_ref, dw_ref):
        i = pl.program_id(1)
        next8 = jnp.where(i < pl.num_programs(1) - 1, dzn_ref[...], 0.0)
        xv, dzv = x_ref[...].astype(F32), dz_ref[...]

        @pl.when(i == 0)
        def _():
            dw_ref[...] = jnp.zeros_like(dw_ref)

        dx = dzv * w_ref[3:4, :]
        dw_ref[3:4, :] += jnp.sum(dzv * xv, axis=0, keepdims=True)
        for sft in (1, 2, 3):
            up = _shift_up(dzv, next8, sft)
            dx = dx + up * w_ref[3 - sft:4 - sft, :]
            dw_ref[3 - sft:4 - sft, :] += jnp.sum(up * xv, axis=0, keepdims=True)
        dx_ref[...] = dx.astype(BF16)

    return _call(body, name="gdn_conv_bwd_b", grid=(3, s // tm),
                 in_specs=[cur, pl.BlockSpec((4, 1024), lambda j, i: (0, j)),
                           pl.BlockSpec((tm, 1024), lambda j, i: (i, j)),
                           pl.BlockSpec((8, 1024), lambda j, i: (jnp.minimum((i + 1) * nb8, last8), j))],
                 out_specs=[pl.BlockSpec((tm, 1024), lambda j, i: (i, j)), pl.BlockSpec((4, 1024), lambda j, i: (0, j))],
                 out_shape=[_sds((s, 3072), BF16), _sds((4, 3072), F32)], args=(proj, conv_w, dz, dz))


def _softplus(xv):
    return jnp.maximum(xv, 0.0) + jnp.log(1.0 + jnp.exp(-jnp.abs(xv)))


def _gdn_gates_fwd(proj, alog128, dtb128):
    s = proj.shape[0]
    tm = min(TILES["row"], s)

    def body(m_ref, a_ref, b_ref, o_ref):
        mv = m_ref[...]
        lane = lax.broadcasted_iota(jnp.int32, mv.shape, 1)
        g = -jnp.exp(a_ref[...]) * _softplus(mv + b_ref[...])
        is_g = (lane >= GA_LANE) & (lane < GA_LANE + HEADS)
        is_b = (lane >= GB_LANE) & (lane < GB_LANE + HEADS)
        o_ref[...] = jnp.where(is_g, g, jnp.where(is_b, _sigmoid(mv), 0.0))

    return _call(body, name="gdn_gates_fwd", grid=(s // tm,),
                 in_specs=[_rows(tm, 128), _full((1, 128)), _full((1, 128))],
                 out_specs=_rows(tm, 128), out_shape=_sds((s, 128), F32), args=(proj, alog128, dtb128))


def _gdn_gates_bwd(proj, alog128, dtb128, gbeta, dgbeta, dkr):
    s = proj.shape[0]
    tm = min(TILES["row"], s)

    def body(m_ref, a_ref, b_ref, gb_ref, d_ref, kr_ref, o_ref, da_ref, db_ref):
        mv, dv = m_ref[...], d_ref[...]
        lane = lax.broadcasted_iota(jnp.int32, mv.shape, 1)
        is_g = (lane >= GA_LANE) & (lane < GA_LANE + HEADS)
        is_b = (lane >= GB_LANE) & (lane < GB_LANE + HEADS)
        dga = jnp.where(is_g, dv * (-jnp.exp(a_ref[...])) * _sigmoid(mv + b_ref[...]), 0.0)
        beta = gb_ref[...]
        dgb = jnp.where(is_b, dv * beta * (1.0 - beta), 0.0)
        o_ref[...] = jnp.where(lane < 64, kr_ref[...], dga + dgb).astype(BF16)

        @pl.when(pl.program_id(0) == 0)
        def _():
            da_ref[...] = jnp.zeros_like(da_ref)
            db_ref[...] = jnp.zeros_like(db_ref)

        da_ref[...] += _acc8(jnp.where(is_g, dv * gb_ref[...], 0.0))
        db_ref[...] += _acc8(dga)

    return _call(body, name="gdn_gates_bwd", grid=(s // tm,),
                 in_specs=[_rows(tm, 128), _full((1, 128)), _full((1, 128)), _rows(tm, 128), _rows(tm, 128),
                           _rows(tm, 128)],
                 out_specs=[_rows(tm, 128), _full((8, 128)), _full((8, 128))],
                 out_shape=[_sds((s, 128), BF16), _sds((8, 128), F32), _sds((8, 128), F32)],
                 args=(proj, alog128, dtb128, gbeta, dgbeta, dkr))


def _col(mat, lane_idx, lane):
    return jnp.sum(jnp.where(lane == lane_idx, mat, 0.0), axis=-1, keepdims=True)


def _chunk_local(qh, kh, vh, gcol, bcol, ii, jj):
    lower, strict, eye = ii >= jj, ii > jj, ii == jj
    grow = jnp.sum(jnp.where(eye, gcol, 0.0), axis=0, keepdims=True)
    decay = jnp.where(lower, jnp.exp(jnp.where(lower, gcol - grow, 0.0)), 0.0)
    kb = kh * bcol
    vb = vh * bcol
    mm = _dot(kb, kh, NT)
    lmat = jnp.where(strict, mm * decay, 0.0)
    pw = -lmat
    tinv = jnp.where(eye, 1.0, 0.0) + pw
    for _ in range(5):
        pw = _dot_hi(pw, pw)
        tinv = tinv + _dot_hi(tinv, pw)
    egc = jnp.exp(gcol)
    kbg = kb * egc
    rhs = jnp.concatenate([vb, kbg], axis=-1)
    sol = _dot_hi(tinv, rhs)
    qk = _dot(qh, kh, NT)
    glast = jnp.sum(jnp.where(ii[:, :1] == CHUNK - 1, gcol, 0.0), axis=0, keepdims=True)
    ekd = jnp.exp(glast - gcol)
    return dict(decay=decay, kb=kb, vb=vb, mm=mm, lmat=lmat, tinv=tinv, egc=egc, kbg=kbg, rhs=rhs,
                u=sol[:, :HEAD_DIM], w=sol[:, HEAD_DIM:], qk=qk, amat=qk * decay, qd=qh * egc, ekd=ekd,
                kd=kh * ekd, gl=jnp.exp(glast), strict=strict, lower=lower, eye=eye)


def _tri(ii, jj):
    return jnp.where(ii >= jj, 1.0, 0.0)


def _gdn_fwd(qkv, gbeta):
    s = qkv.shape[0]
    n = s // CHUNK

    def body(qkv_ref, gb_ref, o_ref, st_ref, state):
        @pl.when(pl.program_id(0) == 0)
        def _():
            state[...] = jnp.zeros_like(state)

        ii = lax.broadcasted_iota(jnp.int32, (CHUNK, CHUNK), 0)
        jj = lax.broadcasted_iota(jnp.int32, (CHUNK, CHUNK), 1)
        lane = lax.broadcasted_iota(jnp.int32, (CHUNK, 128), 1)
        gbv = gb_ref[...]
        gc = _dot_hi(_tri(ii, jj), gbv)
        for h in range(HEADS):
            sl = slice(128 * h, 128 * (h + 1))
            qh = qkv_ref[:, 128 * h:128 * (h + 1)]
            kh = qkv_ref[:, 1024 + 128 * h:1024 + 128 * (h + 1)]
            vh = qkv_ref[:, 2048 + 128 * h:2048 + 128 * (h + 1)]
            c = _chunk_local(qh, kh, vh, _col(gc, GA_LANE + h, lane), _col(gbv, GB_LANE + h, lane), ii, jj)
            st = state[sl, :]
            st_ref[sl, :] = st
            vn = c["u"] - _dot(c["w"], st)
            o_ref[:, sl] = _dot(c["qd"], st) + _dot(c["amat"], vn)
            state[sl, :] = st * c["gl"] + _dot(c["kd"], vn, TN)

    return _call(body, name="gdn_fwd", grid=(n,),
                 in_specs=[_rows(CHUNK, 3072), _rows(CHUNK, 128)],
                 out_specs=[_rows(CHUNK, 1024), _rows(HEADS * 128, 128)],
                 out_shape=[_sds((s, 1024), F32), _sds((n * HEADS * 128, 128), F32)],
                 scratch=[pltpu.VMEM((HEADS * 128, 128), F32)], args=(qkv, gbeta))


def _gdn_bwd(qkv, gbeta, states, do):
    s = qkv.shape[0]
    n = s // CHUNK

    def body(qkv_ref, gb_ref, st_ref, do_ref, dqkv_ref, dgb_ref, dstate):
        @pl.when(pl.program_id(0) == 0)
        def _():
            dstate[...] = jnp.zeros_like(dstate)

        ii = lax.broadcasted_iota(jnp.int32, (CHUNK, CHUNK), 0)
        jj = lax.broadcasted_iota(jnp.int32, (CHUNK, CHUNK), 1)
        lane = lax.broadcasted_iota(jnp.int32, (CHUNK, 128), 1)
        row1 = ii[:, :1]
        gbv = gb_ref[...]
        gc = _dot_hi(_tri(ii, jj), gbv)
        dgc_all = jnp.zeros((CHUNK, 128), F32)
        db_all = jnp.zeros((CHUNK, 128), F32)
        for h in range(HEADS):
            sl = slice(128 * h, 128 * (h + 1))
            qh = qkv_ref[:, 128 * h:128 * (h + 1)]
            kh = qkv_ref[:, 1024 + 128 * h:1024 + 128 * (h + 1)]
            vh = qkv_ref[:, 2048 + 128 * h:2048 + 128 * (h + 1)]
            bcol = _col(gbv, GB_LANE + h, lane)
            c = _chunk_local(qh, kh, vh, _col(gc, GA_LANE + h, lane), bcol, ii, jj)
            st = st_ref[sl, :]
            dst = dstate[sl, :]
            dov = do_ref[:, sl]
            vn = c["u"] - _dot(c["w"], st)
            dvn = _dot(c["amat"], dov, TN) + _dot(c["kd"], dst)
            damat = jnp.where(c["lower"], _dot(dov, vn, NT), 0.0)
            dqd = _dot(dov, st, NT)
            dkd = _dot(vn, dst, NT)
            dw = -_dot(dvn, st, NT)
            dgl = jnp.sum(jnp.sum(st * dst, axis=-1, keepdims=True), axis=0, keepdims=True)
            dstate[sl, :] = _dot(c["qd"], dov, TN) + c["gl"] * dst - _dot(c["w"], dvn, TN)
            dsol = jnp.concatenate([dvn, dw], axis=-1)
            drhs = _dot_hi(c["tinv"], dsol, TN)
            dtinv = _dot_hi(dsol, c["rhs"], NT)
            dl = -_dot_hi(_dot_hi(c["tinv"], dtinv, TN), c["tinv"], NT)
            dl = jnp.where(c["strict"], dl, 0.0)
            dmm = dl * c["decay"]
            dqk = damat * c["decay"]
            wmat = dl * c["lmat"] + damat * c["amat"]
            dgc = jnp.sum(wmat, axis=-1, keepdims=True)
            wcol = jnp.sum(wmat, axis=0, keepdims=True)
            dgc = dgc - jnp.sum(jnp.where(c["eye"], wcol, 0.0), axis=-1, keepdims=True)
            dkb = _dot(dmm, kh) + drhs[:, HEAD_DIM:] * c["egc"]
            dk = _dot(dmm, c["kb"], TN) + _dot(dqk, qh, TN) + dkd * c["ekd"]
            dq = _dot(dqk, kh) + dqd * c["egc"]
            dgc = dgc + jnp.sum(drhs[:, HEAD_DIM:] * c["kbg"], axis=-1, keepdims=True)
            dgc = dgc + jnp.sum(dqd * c["qd"], axis=-1, keepdims=True)
            tmp = jnp.sum(dkd * c["kd"], axis=-1, keepdims=True)
            dgc = dgc - tmp
            dglast = jnp.sum(tmp, axis=0, keepdims=True) + dgl * c["gl"]
            dgc = dgc + jnp.where(row1 == CHUNK - 1, dglast, 0.0)
            dk = dk + dkb * bcol
            db = jnp.sum(dkb * kh, axis=-1, keepdims=True) + jnp.sum(drhs[:, :HEAD_DIM] * vh, axis=-1, keepdims=True)
            dqkv_ref[:, 128 * h:128 * (h + 1)] = dq
            dqkv_ref[:, 1024 + 128 * h:1024 + 128 * (h + 1)] = dk
            dqkv_ref[:, 2048 + 128 * h:2048 + 128 * (h + 1)] = drhs[:, :HEAD_DIM] * bcol
            dgc_all = dgc_all + jnp.where(lane == GA_LANE + h, dgc, 0.0)
            db_all = db_all + jnp.where(lane == GB_LANE + h, db, 0.0)
        dgb_ref[...] = _dot_hi(_tri(jj, ii), dgc_all) + db_all

    rev = lambda w: pl.BlockSpec((CHUNK, w), lambda i: (n - 1 - i, 0))
    return _call(body, name="gdn_bwd", grid=(n,),
                 in_specs=[rev(3072), rev(128), pl.BlockSpec((HEADS * 128, 128), lambda i: (n - 1 - i, 0)), rev(1024)],
                 out_specs=[rev(3072), rev(128)],
                 out_shape=[_sds((s, 3072), F32), _sds((s, 128), F32)],
                 scratch=[pltpu.VMEM((HEADS * 128, 128), F32)], args=(qkv, gbeta, states, do))


NN_B = (((2,), (1,)), ((0,), (0,)))
NT_B = (((2,), (2,)), ((0,), (0,)))
TN_B = (((1,), (1,)), ((0,), (0,)))
GDN_PRE_CHUNKS = 4
GDN_POST_CHUNKS = 2
GDN_SEQ_CHUNKS = 4


def _gather_heads(qkv_ref, gc, gbv, qs, ks, vs, gs, bs, nchunks):
    lane = lax.broadcasted_iota(jnp.int32, (CHUNK, 128), 1)
    for c in range(nchunks):
        rows = slice(CHUNK * c, CHUNK * (c + 1))
        for h in range(HEADS):
            b = HEADS * c + h
            qs[b] = qkv_ref[rows, 128 * h:128 * (h + 1)]
            ks[b] = qkv_ref[rows, 1024 + 128 * h:1024 + 128 * (h + 1)]
            vs[b] = qkv_ref[rows, 2048 + 128 * h:2048 + 128 * (h + 1)]
            gs[b] = jnp.broadcast_to(_col(gc[rows], GA_LANE + h, lane), (CHUNK, 128))
            bs[b] = jnp.broadcast_to(_col(gbv[rows], GB_LANE + h, lane), (CHUNK, 128))


def _block_tri(rows, transpose=False):
    ri = lax.broadcasted_iota(jnp.int32, (rows, rows), 0)
    ci = lax.broadcasted_iota(jnp.int32, (rows, rows), 1)
    same = (ri >> 6) == (ci >> 6)
    return jnp.where(same & ((ci >= ri) if transpose else (ri >= ci)), 1.0, 0.0)


def _local_b(q, k, v, g128, b128):
    ii = lax.broadcasted_iota(jnp.int32, (1, CHUNK, CHUNK), 1)
    jj = lax.broadcasted_iota(jnp.int32, (1, CHUNK, CHUNK), 2)
    lower, strict, eye = ii >= jj, ii > jj, ii == jj
    g64 = g128[:, :, :CHUNK]
    grow = jnp.sum(jnp.where(eye, g64, 0.0), axis=1, keepdims=True)
    decay = jnp.where(lower, jnp.exp(jnp.where(lower, g64 - grow, 0.0)), 0.0)
    kb = k * b128
    vb = v * b128
    mm = lax.dot_general(kb.astype(BF16), k.astype(BF16), NT_B, preferred_element_type=F32)
    lmat = jnp.where(strict, mm * decay, 0.0)
    egc = jnp.exp(g128)
    kbg = kb * egc
    qk = lax.dot_general(q.astype(BF16), k.astype(BF16), NT_B, preferred_element_type=F32)
    row = lax.broadcasted_iota(jnp.int32, (1, CHUNK, 128), 1)
    glast = jnp.sum(jnp.where(row == CHUNK - 1, g128, 0.0), axis=1, keepdims=True)
    ekd = jnp.exp(glast - g128)
    return dict(decay=decay, kb=kb, vb=vb, lmat=lmat, egc=egc, kbg=kbg, amat=qk * decay, qd=q * egc, ekd=ekd,
                kd=k * ekd, gl=jnp.exp(glast), lower=lower, strict=strict, eye=eye)


def _bdot(a, b, dims):
    return lax.dot_general(a.astype(BF16), b.astype(BF16), dims, preferred_element_type=F32)


def _split(a):
    hi = a.astype(BF16)
    return hi, (a - hi.astype(F32)).astype(BF16)


def _bdot_hi(a, b, dims):
    ah, al = _split(a)
    bh, bl = _split(b)
    d = lambda x, y: lax.dot_general(x, y, dims, preferred_element_type=F32)
    return d(ah, bh) + d(ah, bl) + d(al, bh)


def _gdn_pre(qkv, gbeta):
    s = qkv.shape[0]
    n = s // CHUNK
    cb = min(GDN_PRE_CHUNKS, n)
    nb = cb * HEADS
    rows = cb * CHUNK

    def body(qkv_ref, gb_ref, u_ref, w_ref, qd_ref, kd_ref, a_ref, t_ref, gl_ref, qs, ks, vs, gs, bs):
        gbv = gb_ref[...]
        gc = _dot_hi(_block_tri(rows), gbv)
        _gather_heads(qkv_ref, gc, gbv, qs, ks, vs, gs, bs, cb)
        c = _local_b(qs[...], ks[...], vs[...], gs[...], bs[...])
        pw = -c["lmat"]
        tinv = jnp.where(c["eye"], 1.0, 0.0) + pw
        for _ in range(5):
            pw = _bdot_hi(pw, pw, NN_B)
            tinv = tinv + _bdot_hi(tinv, pw, NN_B)
        u_ref[...] = _bdot_hi(tinv, c["vb"], NN_B)
        w_ref[...] = _bdot_hi(tinv, c["kbg"], NN_B).astype(BF16)
        qd_ref[...] = c["qd"].astype(BF16)
        kd_ref[...] = c["kd"].astype(BF16)
        a_ref[...] = c["amat"].astype(BF16)
        t_ref[...] = tinv
        gl_ref[...] = c["gl"]

    b3 = lambda d: pl.BlockSpec((nb, CHUNK, d), lambda i: (i, 0, 0))
    nt = n * HEADS
    return _call(body, name="gdn_pre", grid=(n // cb,),
                 in_specs=[_rows(rows, 3072), _rows(rows, 128)],
                 out_specs=[b3(128), b3(128), b3(128), b3(128), b3(CHUNK), b3(CHUNK),
                            pl.BlockSpec((nb, 1, 128), lambda i: (i, 0, 0))],
                 out_shape=[_sds((nt, CHUNK, 128), F32), _sds((nt, CHUNK, 128), BF16), _sds((nt, CHUNK, 128), BF16),
                            _sds((nt, CHUNK, 128), BF16), _sds((nt, CHUNK, CHUNK), BF16), _sds((nt, CHUNK, CHUNK), F32),
                            _sds((nt, 1, 128), F32)],
                 scratch=[pltpu.VMEM((nb, CHUNK, 128), F32)] * 5, sem=("parallel",), args=(qkv, gbeta))


def _gdn_scan_fwd(u, w, qd, kd, amat, gl):
    nt = u.shape[0]
    n = nt // HEADS
    cs = min(GDN_SEQ_CHUNKS, n)

    def body(u_ref, w_ref, qd_ref, kd_ref, a_ref, gl_ref, o_ref, st_ref, state):
        @pl.when(pl.program_id(0) == 0)
        def _():
            state[...] = jnp.zeros_like(state)

        for c in range(cs):
            sl = slice(HEADS * c, HEADS * (c + 1))
            st = state[...]
            stb = st.astype(BF16)
            st_ref[sl] = stb
            vn = u_ref[sl] - lax.dot_general(w_ref[sl], stb, NN_B, preferred_element_type=F32)
            vnb = vn.astype(BF16)
            o = (lax.dot_general(qd_ref[sl], stb, NN_B, preferred_element_type=F32)
                 + lax.dot_general(a_ref[sl], vnb, NN_B, preferred_element_type=F32))
            state[...] = st * gl_ref[sl] + lax.dot_general(kd_ref[sl], vnb, TN_B, preferred_element_type=F32)
            for h in range(HEADS):
                o_ref[CHUNK * c:CHUNK * (c + 1), 128 * h:128 * (h + 1)] = o[h]

    b3 = lambda d: pl.BlockSpec((cs * HEADS, CHUNK, d), lambda i: (i, 0, 0))
    return _call(body, name="gdn_scan_fwd", grid=(n // cs,),
                 in_specs=[b3(128), b3(128), b3(128), b3(128), b3(CHUNK), pl.BlockSpec((cs * HEADS, 1, 128), lambda i: (i, 0, 0))],
                 out_specs=[_rows(cs * CHUNK, 1024), pl.BlockSpec((cs * HEADS, 128, 128), lambda i: (i, 0, 0))],
                 out_shape=[_sds((n * CHUNK, 1024), F32), _sds((nt, 128, 128), BF16)],
                 scratch=[pltpu.VMEM((HEADS, 128, 128), F32)], args=(u, w, qd, kd, amat, gl))


def _gdn_scan_bwd(w, qd, kd, amat, gl, do):
    nt = w.shape[0]
    n = nt // HEADS
    cs = min(GDN_SEQ_CHUNKS, n)
    ng = n // cs

    def body(w_ref, qd_ref, kd_ref, a_ref, gl_ref, do_ref, ds_ref, dstate, dos):
        @pl.when(pl.program_id(0) == 0)
        def _():
            dstate[...] = jnp.zeros_like(dstate)

        for c in reversed(range(cs)):
            sl = slice(HEADS * c, HEADS * (c + 1))
            for h in range(HEADS):
                dos[h] = do_ref[CHUNK * c:CHUNK * (c + 1), 128 * h:128 * (h + 1)].astype(BF16)
            dob = dos[...]
            dst = dstate[...]
            dstb = dst.astype(BF16)
            ds_ref[sl] = dstb
            dvn = (lax.dot_general(a_ref[sl], dob, TN_B, preferred_element_type=F32)
                   + lax.dot_general(kd_ref[sl], dstb, NN_B, preferred_element_type=F32))
            dstate[...] = (lax.dot_general(qd_ref[sl], dob, TN_B, preferred_element_type=F32) + gl_ref[sl] * dst
                           - lax.dot_general(w_ref[sl], dvn.astype(BF16), TN_B, preferred_element_type=F32))

    b3 = lambda d: pl.BlockSpec((cs * HEADS, CHUNK, d), lambda i: (ng - 1 - i, 0, 0))
    return _call(body, name="gdn_scan_bwd", grid=(ng,),
                 in_specs=[b3(128), b3(128), b3(128), b3(CHUNK), pl.BlockSpec((cs * HEADS, 1, 128), lambda i: (ng - 1 - i, 0, 0)),
                           pl.BlockSpec((cs * CHUNK, 1024), lambda i: (ng - 1 - i, 0))],
                 out_specs=pl.BlockSpec((cs * HEADS, 128, 128), lambda i: (ng - 1 - i, 0, 0)),
                 out_shape=_sds((nt, 128, 128), BF16),
                 scratch=[pltpu.VMEM((HEADS, 128, 128), F32), pltpu.VMEM((HEADS, CHUNK, 128), BF16)],
                 args=(w, qd, kd, amat, gl, do))


def _gdn_post_bwd(qkv, gbeta, u, w, tinv, states, dstates, do):
    s = qkv.shape[0]
    n = s // CHUNK
    cb = min(GDN_POST_CHUNKS, n)
    nb = cb * HEADS
    rows = cb * CHUNK

    def body(qkv_ref, gb_ref, u_ref, w_ref, t_ref, st_ref, ds_ref, do_ref, dqkv_ref, dgb_ref, qs, ks, vs, gs, bs, dos):
        gbv = gb_ref[...]
        gc = _dot_hi(_block_tri(rows), gbv)
        _gather_heads(qkv_ref, gc, gbv, qs, ks, vs, gs, bs, cb)
        for c in range(cb):
            for h in range(HEADS):
                dos[HEADS * c + h] = do_ref[CHUNK * c:CHUNK * (c + 1), 128 * h:128 * (h + 1)].astype(F32)
        q, k, v, b128 = qs[...], ks[...], vs[...], bs[...]
        c = _local_b(q, k, v, gs[...], b128)
        tinv, st, dst, dov = t_ref[...], st_ref[...], ds_ref[...], dos[...]
        wv = w_ref[...]
        vn = u_ref[...] - _bdot(wv, st, NN_B)
        dvn = _bdot(c["amat"], dov, TN_B) + _bdot(c["kd"], dst, NN_B)
        damat = jnp.where(c["lower"], _bdot(dov, vn, NT_B), 0.0)
        dqd = _bdot(dov, st, NT_B)
        dkd = _bdot(vn, dst, NT_B)
        dw = -_bdot(dvn, st, NT_B)
        dgl = jnp.sum(jnp.sum(st.astype(F32) * dst.astype(F32), axis=1, keepdims=True), axis=-1, keepdims=True)
        dvb = _bdot(tinv, dvn, TN_B)
        dkbg = _bdot(tinv, dw, TN_B)
        dtinv = _bdot(dvn, c["vb"], NT_B) + _bdot(dw, c["kbg"], NT_B)
        dl = -_bdot(_bdot(tinv, dtinv, TN_B), tinv, NT_B)
        dl = jnp.where(c["strict"], dl, 0.0)
        dmm = dl * c["decay"]
        dqk = damat * c["decay"]
        wmat = dl * c["lmat"] + damat * c["amat"]
        wcol = jnp.sum(wmat, axis=1, keepdims=True)
        dgc = jnp.sum(wmat, axis=-1, keepdims=True) - jnp.sum(jnp.where(c["eye"], wcol, 0.0), axis=-1, keepdims=True)
        dkb = _bdot(dmm, k, NN_B) + dkbg * c["egc"]
        dk = _bdot(dmm, c["kb"], TN_B) + _bdot(dqk, q, TN_B) + dkd * c["ekd"] + dkb * b128
        dq = _bdot(dqk, k, NN_B) + dqd * c["egc"]
        tmp = jnp.sum(dkd * c["kd"], axis=-1, keepdims=True)
        dgc = (dgc + jnp.sum(dkbg * c["kbg"], axis=-1, keepdims=True) + jnp.sum(dqd * c["qd"], axis=-1, keepdims=True)
               - tmp)
        dglast = jnp.sum(tmp, axis=1, keepdims=True) + dgl * c["gl"][:, :, :1]
        row1 = lax.broadcasted_iota(jnp.int32, (1, CHUNK, 1), 1)
        dgc = dgc + jnp.where(row1 == CHUNK - 1, dglast, 0.0)
        db = jnp.sum(dkb * k, axis=-1, keepdims=True) + jnp.sum(dvb * v, axis=-1, keepdims=True)
        dv = dvb * b128
        lane = lax.broadcasted_iota(jnp.int32, (CHUNK, 128), 1)
        parts = []
        for cc in range(cb):
            acc = jnp.zeros((CHUNK, 128), F32)
            for h in range(HEADS):
                bi = HEADS * cc + h
                rs = slice(CHUNK * cc, CHUNK * (cc + 1))
                dqkv_ref[rs, 128 * h:128 * (h + 1)] = dq[bi]
                dqkv_ref[rs, 1024 + 128 * h:1024 + 128 * (h + 1)] = dk[bi]
                dqkv_ref[rs, 2048 + 128 * h:2048 + 128 * (h + 1)] = dv[bi]
                acc = acc + jnp.where(lane == GA_LANE + h, dgc[bi], 0.0)
            parts.append(acc)
        dgc_all = jnp.concatenate(parts, axis=0)
        dg_all = _dot_hi(_block_tri(rows, transpose=True), dgc_all)
        for cc in range(cb):
            acc = dg_all[CHUNK * cc:CHUNK * (cc + 1)]
            for h in range(HEADS):
                acc = acc + jnp.where(lane == GB_LANE + h, db[HEADS * cc + h], 0.0)
            dgb_ref[CHUNK * cc:CHUNK * (cc + 1), :] = acc

    b3 = lambda d1, d2: pl.BlockSpec((nb, d1, d2), lambda i: (i, 0, 0))
    return _call(body, name="gdn_post_bwd", grid=(n // cb,),
                 in_specs=[_rows(rows, 3072), _rows(rows, 128), b3(CHUNK, 128), b3(CHUNK, 128), b3(CHUNK, CHUNK),
                           b3(128, 128), b3(128, 128), _rows(rows, 1024)],
                 out_specs=[_rows(rows, 3072), _rows(rows, 128)],
                 out_shape=[_sds((s, 3072), F32), _sds((s, 128), F32)],
                 scratch=[pltpu.VMEM((nb, CHUNK, 128), F32)] * 6, sem=("parallel",),
                 args=(qkv, gbeta, u, w, tinv, states, dstates, do))


def _mix_fwd(o_mla, o_gdn, proj, out_gain):
    s = proj.shape[0]
    tm = min(TILES["row"], s)

    def body(om_ref, og_ref, mg_ref, gg_ref, g_ref, o_ref):
        o_ref[:, :1024] = (om_ref[...] * _silu(mg_ref[...].astype(F32))).astype(BF16)
        for h in range(HEADS):
            sl = slice(128 * h, 128 * (h + 1))
            og = og_ref[:, sl]
            on = og * _rms(og, HEAD_DIM) * g_ref[...]
            o_ref[:, 1024 + 128 * h:1024 + 128 * (h + 1)] = (on * _silu(gg_ref[:, sl].astype(F32))).astype(BF16)

    return _call(body, name="mix_fwd", grid=(s // tm,),
                 in_specs=[_rows(tm, 1024), _rows(tm, 1024), _rows(tm, 1024, 1), _rows(tm, 1024, 5), _full((1, 128))],
                 out_specs=_rows(tm, 2048), out_shape=_sds((s, 2048), BF16), args=(o_mla, o_gdn, proj, proj, out_gain))


def _mix_bwd(o_mla, o_gdn, proj, out_gain, dmixed):
    s = proj.shape[0]
    tm = min(TILES["row"], s)

    def body(om_ref, og_ref, mg_ref, gg_ref, g_ref, dm_ref, dg_ref, dom_ref, dog_ref, dmg_ref, dgg_ref, ag_ref,
             delta_ref):
        @pl.when(pl.program_id(0) == 0)
        def _():
            ag_ref[...] = jnp.zeros_like(ag_ref)

        mg = mg_ref[...].astype(F32)
        dm = dm_ref[...].astype(F32)
        om = om_ref[...]
        dom = (dm * _silu(mg)).astype(BF16)
        dom_ref[...] = dom
        dmg_ref[...] = (dm * om * _dsilu(mg)).astype(BF16)
        prod = dom.astype(F32) * om
        lane = lax.broadcasted_iota(jnp.int32, (tm, 128), 1)
        delta = jnp.zeros((tm, 128), F32)
        for h in range(HEADS):
            delta = delta + jnp.where(lane == h, jnp.sum(prod[:, 128 * h:128 * (h + 1)], axis=-1, keepdims=True), 0.0)
        delta_ref[...] = delta
        for h in range(HEADS):
            sl = slice(128 * h, 128 * (h + 1))
            og, gg, d = og_ref[:, sl], gg_ref[:, sl].astype(F32), dg_ref[:, sl].astype(F32)
            on = og * _rms(og, HEAD_DIM) * g_ref[...]
            dgg_ref[:, sl] = (d * on * _dsilu(gg)).astype(BF16)
            dx, gpart = _rms_bwd(og, g_ref[...], d * _silu(gg), HEAD_DIM)
            dog_ref[:, sl] = dx.astype(BF16)
            ag_ref[...] += _acc8(gpart)

    return _call(body, name="mix_bwd", grid=(s // tm,),
                 in_specs=[_rows(tm, 1024), _rows(tm, 1024), _rows(tm, 1024, 1), _rows(tm, 1024, 5), _full((1, 128)),
                           _rows(tm, 1024, 0), _rows(tm, 1024, 1)],
                 out_specs=[_rows(tm, 1024), _rows(tm, 1024), _rows(tm, 1024), _rows(tm, 1024), _full((8, 128)),
                            _rows(tm, 128)],
                 out_shape=[_sds((s, 1024), BF16), _sds((s, 1024), BF16), _sds((s, 1024), BF16), _sds((s, 1024), BF16),
                            _sds((8, 128), F32), _sds((s, 128), F32)],
                 args=(o_mla, o_gdn, proj, proj, out_gain, dmixed, dmixed))


def _out_fwd(mixed, w_out, x, target):
    s = x.shape[0]
    tm = min(TILES["mm"], s)
    tn = min(TILES["mm"], D_MODEL)

    def body(m_ref, w_ref, x_ref, t_ref, dy_ref, acc_ref):
        err = x_ref[...] + _dot(m_ref[...], w_ref[...]) - t_ref[...]
        dy_ref[...] = (err * (1.0 / D_MODEL)).astype(BF16)

        @pl.when(pl.program_id(1) == 0)
        def _():
            acc_ref[...] = jnp.zeros_like(acc_ref)

        acc_ref[...] += _acc8(err * err)

    return _call(body, name="out_fwd", grid=(D_MODEL // tn, s // tm),
                 in_specs=[pl.BlockSpec((tm, D_MODEL), lambda j, i: (i, 0)), pl.BlockSpec((D_MODEL, tn), lambda j, i: (0, j)),
                           pl.BlockSpec((tm, tn), lambda j, i: (i, j)), pl.BlockSpec((tm, tn), lambda j, i: (i, j))],
                 out_specs=[pl.BlockSpec((tm, tn), lambda j, i: (i, j)), pl.BlockSpec((8, tn), lambda j, i: (0, j))],
                 out_shape=[_sds((s, D_MODEL), BF16), _sds((8, D_MODEL), F32)], args=(mixed, w_out, x, target))


def _row_tile(r, c):
    if r % 8 != 0:
        return r
    t = 8
    while r % (2 * t) == 0 and 2 * t * c * 4 <= (1 << 20):
        t *= 2
    return t


def _sum_arrays(parts, name, also_bf16=False):
    r, c = parts[0].shape
    tr = _row_tile(r, c)
    n = len(parts)

    def body(*refs):
        acc = refs[0][...].astype(F32)
        for p_ref in refs[1:n]:
            acc = acc + p_ref[...].astype(F32)
        refs[n][...] = acc
        if also_bf16:
            refs[n + 1][...] = acc.astype(BF16)

    nout = 2 if also_bf16 else 1
    out = _call(body, name=name, grid=(r // tr,), in_specs=[_rows(tr, c)] * n, out_specs=[_rows(tr, c)] * nout,
                out_shape=[_sds((r, c), F32), _sds((r, c), BF16)][:nout], args=tuple(parts))
    return out if also_bf16 else out[0]


def _adamw(w, g, m, v, name):
    r, c = w.shape
    c1 = 1.0 - ADAM_B1 ** ADAM_STEP
    c2 = 1.0 - ADAM_B2 ** ADAM_STEP

    def body(w_ref, g_ref, m_ref, v_ref, d_ref, nm_ref, nv_ref):
        gv = g_ref[...]
        nm = ADAM_B1 * m_ref[...] + (1.0 - ADAM_B1) * gv
        nv = ADAM_B2 * v_ref[...] + (1.0 - ADAM_B2) * (gv * gv)
        nm_ref[...] = nm
        nv_ref[...] = nv
        d_ref[...] = -ADAM_LR * ((nm / c1) / (jnp.sqrt(nv / c2) + ADAM_EPS) + ADAM_WD * w_ref[...])

    if r % 8 == 0:
        tr = _row_tile(r, c)
        grid, spec = (r // tr,), _rows(tr, c)
    else:
        tc = c
        while tc % 256 == 0 and r * tc * 4 > (3 << 19):
            tc //= 2
        grid, spec = (c // tc,), pl.BlockSpec((r, tc), lambda i: (0, i))
    return _call(body, name=name, grid=grid, in_specs=[spec] * 4, out_specs=[spec] * 3,
                 out_shape=[_sds((r, c), F32)] * 3, args=(w, g, m, v))


ANY = pl.BlockSpec(memory_space=pl.ANY)
CHIP_FLIPS = ((1, 0), (0, 1), (1, 1))


def _comm_call(body, *, name, n_in, out_shape, scratch):
    def kfn(*refs):
        body(*refs)
    return pl.pallas_call(kfn, name=name, in_specs=[ANY] * n_in, out_specs=[ANY] * len(out_shape), out_shape=out_shape,
                          scratch_shapes=list(scratch),
                          compiler_params=pltpu.CompilerParams(has_side_effects=True))


def _all_gather_chips(shards):
    na = len(shards)

    def body(*refs):
        copies = _gather_copies(refs[:na], refs[na:2 * na], *refs[2 * na:])
        _gather_start(copies)
        _gather_finish(copies)

    out_shape = [_sds((4,) + a.shape, a.dtype) for a in shards]
    sem = pltpu.SemaphoreType.DMA((na, 3))
    got = _comm_call(body, name="all_gather_weights", n_in=na, out_shape=out_shape, scratch=[sem, sem, sem, sem])(*shards)
    return _place_own_blocks(got, shards)


def _gather_copies(ins, outs, send_sems, recv_sems, fwd_send, fwd_recv):
    x, y, c = lax.axis_index("x"), lax.axis_index("y"), lax.axis_index("c")
    my_k = 2 * x + y
    direct, forwards = [], []
    for a in range(len(ins)):
        rows = ins[a].shape[0]
        for r, (fx, fy) in enumerate(CHIP_FLIPS):
            px, py = x ^ fx, y ^ fy
            if rows % 32 == 0:
                mine = pl.ds(pl.multiple_of(c * (rows // 2), 16), rows // 2)
                other = pl.ds(pl.multiple_of((1 - c) * (rows // 2), 16), rows // 2)
                rc = pltpu.make_async_remote_copy(
                    src_ref=ins[a].at[mine], dst_ref=outs[a].at[my_k, mine], send_sem=send_sems.at[a, r],
                    recv_sem=recv_sems.at[a, r], device_id=(px, py, c), device_id_type=MESH)
                landed = outs[a].at[2 * px + py, mine]
                fw = pltpu.make_async_remote_copy(
                    src_ref=landed, dst_ref=landed, send_sem=fwd_send.at[a, r], recv_sem=fwd_recv.at[a, r],
                    device_id=(x, y, 1 - c), device_id_type=MESH)
                from_sib = outs[a].at[2 * px + py, other]
                fw_in = pltpu.make_async_remote_copy(
                    src_ref=from_sib, dst_ref=from_sib, send_sem=fwd_send.at[a, r], recv_sem=fwd_recv.at[a, r],
                    device_id=(x, y, 1 - c), device_id_type=MESH)
                forwards.append((rc, fw, fw_in))
            else:
                direct.append(pltpu.make_async_remote_copy(
                    src_ref=ins[a], dst_ref=outs[a].at[my_k], send_sem=send_sems.at[a, r],
                    recv_sem=recv_sems.at[a, r], device_id=(px, py, c), device_id_type=MESH))
    return forwards, direct


def _gather_start(copies):
    forwards, direct = copies
    for rc, _, _ in forwards:
        rc.start()
    for rc in direct:
        rc.start()


def _gather_finish(copies):
    forwards, direct = copies
    for rc, fw, _ in forwards:
        rc.wait_recv()
        fw.start()
    for rc, fw, fw_in in forwards:
        rc.wait_send()
        fw.wait_send()
        fw_in.wait_recv()
    for rc in direct:
        rc.wait()


def _place_own_blocks(got, shards):
    my_k = 2 * lax.axis_index("x") + lax.axis_index("y")
    return [lax.dynamic_update_index_in_dim(g, a, my_k, 0) for g, a in zip(got, shards)]


def _norm1_fwd_gather(x, gain, shards):
    s = x.shape[0]
    tm = min(TILES["row"], s)
    ni = s // tm
    na = len(shards)

    def kfn(x_ref, g_ref, *rest):
        o_ref = rest[na]
        sems = rest[2 * na + 1:]
        i = pl.program_id(0)

        @pl.when(i == 0)
        def _():
            _gather_start(_gather_copies(rest[:na], rest[na + 1:2 * na + 1], *sems))

        xv = x_ref[...]
        r = lax.rsqrt(jnp.mean(xv * xv, axis=-1, keepdims=True) + NORM_EPS)
        o_ref[...] = (xv * r * g_ref[...]).astype(BF16)

        @pl.when(i == ni - 1)
        def _():
            _gather_finish(_gather_copies(rest[:na], rest[na + 1:2 * na + 1], *sems))

    sem = pltpu.SemaphoreType.DMA((na, 3))
    out = pl.pallas_call(
        kfn, name="norm1_fwd_gather", grid=(ni,),
        in_specs=[_rows(tm, D_MODEL), _full((1, D_MODEL))] + [ANY] * na,
        out_specs=[_rows(tm, D_MODEL)] + [ANY] * na,
        out_shape=[_sds((s, D_MODEL), BF16)] + [_sds((4,) + s_a.shape, s_a.dtype) for s_a in shards],
        scratch_shapes=[sem, sem, sem, sem],
        compiler_params=pltpu.CompilerParams(dimension_semantics=("arbitrary",), vmem_limit_bytes=VMEM_LIMIT_V7X,
                                             has_side_effects=True))(x, gain, *shards)
    return out[0], _place_own_blocks(list(out[1:]), shards)


def _matmul_nn_gather(a, b, shards, *, name, tm, tn, out_dtype):
    m, kdim = a.shape
    n = b.shape[1]
    ni, nj = m // tm, n // tn
    na = len(shards)

    def body(a_ref, b_ref, *rest):
        o_ref = rest[na]
        sems = rest[2 * na + 1:]
        i, j = pl.program_id(0), pl.program_id(1)

        @pl.when((i == 0) & (j == 0))
        def _():
            _gather_start(_gather_copies(rest[:na], rest[na + 1:2 * na + 1], *sems))

        o_ref[...] = _dot(a_ref[...], b_ref[...]).astype(out_dtype)

        @pl.when((i == ni - 1) & (j == nj - 1))
        def _():
            _gather_finish(_gather_copies(rest[:na], rest[na + 1:2 * na + 1], *sems))

    def kfn(*refs):
        body(*refs)
    sem = pltpu.SemaphoreType.DMA((na, 3))
    out = pl.pallas_call(
        kfn, name=name, grid=(ni, nj),
        in_specs=[pl.BlockSpec((tm, kdim), lambda i, j: (i, 0)), pl.BlockSpec((kdim, tn), lambda i, j: (0, j))] + [ANY] * na,
        out_specs=[pl.BlockSpec((tm, tn), lambda i, j: (i, j))] + [ANY] * na,
        out_shape=[_sds((m, n), out_dtype)] + [_sds((4,) + s_a.shape, s_a.dtype) for s_a in shards],
        scratch_shapes=[sem, sem, sem, sem],
        compiler_params=pltpu.CompilerParams(dimension_semantics=("arbitrary", "arbitrary"),
                                             vmem_limit_bytes=VMEM_LIMIT_V7X, has_side_effects=True))(a, b, *shards)
    return out[0], _place_own_blocks(list(out[1:]), shards)


def _all_reduce_small(vec):
    r = vec.shape[0]

    def body(v_ref, o_ref, gath, send_sems, recv_sems):
        x, y, c = lax.axis_index("x"), lax.axis_index("y"), lax.axis_index("c")
        me = 4 * x + 2 * y + c
        gath[me] = v_ref[...]
        copies = []
        for rel in range(1, 8):
            fx, fy, fc = (rel >> 2) & 1, (rel >> 1) & 1, rel & 1
            rc = pltpu.make_async_remote_copy(
                src_ref=v_ref, dst_ref=gath.at[me], send_sem=send_sems.at[rel - 1], recv_sem=recv_sems.at[rel - 1],
                device_id=(x ^ fx, y ^ fy, c ^ fc), device_id_type=MESH)
            rc.start()
            copies.append(rc)
        for rc in copies:
            rc.wait()
        acc = gath[0]
        for d in range(1, 8):
            acc = acc + gath[d]
        o_ref[...] = acc

    def kfn(*refs):
        body(*refs)
    vm = pl.BlockSpec(memory_space=pltpu.VMEM)
    return pl.pallas_call(kfn, name="all_reduce_small", in_specs=[vm], out_specs=vm, out_shape=_sds((r, 128), F32),
                          scratch_shapes=[pltpu.VMEM((8, r, 128), F32), pltpu.SemaphoreType.DMA((7,)),
                                          pltpu.SemaphoreType.DMA((7,))],
                          compiler_params=pltpu.CompilerParams(has_side_effects=True))(vec)


def _exchange_halves(arrs):
    na = len(arrs)

    def body(*refs):
        ins, outs = refs[:na], refs[na:2 * na]
        send_sems, recv_sems = refs[2 * na:]
        x, y, c = lax.axis_index("x"), lax.axis_index("y"), lax.axis_index("c")
        copies = []
        for a in range(na):
            half = ins[a].shape[1] // 2
            src = ins[a].at[:, pl.ds(pl.multiple_of((1 - c) * half, 8), half), :]
            rc = pltpu.make_async_remote_copy(src_ref=src, dst_ref=outs[a], send_sem=send_sems.at[a],
                                              recv_sem=recv_sems.at[a], device_id=(x, y, 1 - c), device_id_type=MESH)
            rc.start()
            copies.append(rc)
        for rc in copies:
            rc.wait()

    out_shape = [_sds((4, a.shape[1] // 2, a.shape[2]), F32) for a in arrs]
    return _comm_call(body, name="rs_pair_exchange", n_in=na, out_shape=out_shape,
                      scratch=[pltpu.SemaphoreType.DMA((na,)), pltpu.SemaphoreType.DMA((na,))])(*arrs)


def _scatter_to_chips(arrs):
    na = len(arrs)

    def body(*refs):
        ins, outs = refs[:na], refs[na:2 * na]
        send_sems, recv_sems = refs[2 * na:]
        x, y, c = lax.axis_index("x"), lax.axis_index("y"), lax.axis_index("c")
        copies = []
        for a in range(na):
            for r, (fx, fy) in enumerate(CHIP_FLIPS):
                px, py = x ^ fx, y ^ fy
                rc = pltpu.make_async_remote_copy(
                    src_ref=ins[a].at[2 * px + py], dst_ref=outs[a].at[r], send_sem=send_sems.at[a, r],
                    recv_sem=recv_sems.at[a, r], device_id=(px, py, c), device_id_type=MESH)
                rc.start()
                copies.append(rc)
        for rc in copies:
            rc.wait()

    out_shape = [_sds((3,) + a.shape[1:], a.dtype) for a in arrs]
    return _comm_call(body, name="rs_chip_scatter", n_in=na, out_shape=out_shape,
                      scratch=[pltpu.SemaphoreType.DMA((na, 3)), pltpu.SemaphoreType.DMA((na, 3))])(*arrs)


def _sum_into_half(parts, name):
    r2, c = parts[0].shape
    tr = _row_tile(r2, c)
    nb = r2 // tr
    n = len(parts)

    def kfn(c_ref, *refs):
        acc = refs[0][...].astype(F32)
        for p_ref in refs[1:n]:
            acc = acc + p_ref[...].astype(F32)
        refs[n][...] = acc

    spec = pltpu.PrefetchScalarGridSpec(
        num_scalar_prefetch=1, grid=(nb,), in_specs=[pl.BlockSpec((tr, c), lambda i, cr: (i, 0))] * n,
        out_specs=pl.BlockSpec((tr, c), lambda i, cr: (cr[0] * nb + i, 0)))
    core = lax.axis_index("c").astype(jnp.int32).reshape(1)
    return pl.pallas_call(kfn, name=name, grid_spec=spec, out_shape=_sds((2 * r2, c), F32),
                          compiler_params=pltpu.CompilerParams(dimension_semantics=("arbitrary",),
                                                               vmem_limit_bytes=VMEM_LIMIT_V7X))(core, *parts)


def _join_in_place(arrs):
    na = len(arrs)

    def body(*refs):
        outs = refs[na:2 * na]
        send_sems, recv_sems = refs[2 * na:]
        x, y, c = lax.axis_index("x"), lax.axis_index("y"), lax.axis_index("c")
        copies = []
        for a in range(na):
            half = outs[a].shape[0] // 2
            mine = outs[a].at[pl.ds(pl.multiple_of(c * half, 8), half), :]
            rc = pltpu.make_async_remote_copy(src_ref=mine, dst_ref=mine, send_sem=send_sems.at[a],
                                              recv_sem=recv_sems.at[a], device_id=(x, y, 1 - c), device_id_type=MESH)
            rc.start()
            copies.append(rc)
        for rc in copies:
            rc.wait()

    def kfn(*refs):
        body(*refs)
    return pl.pallas_call(kfn, name="rs_pair_join", in_specs=[ANY] * na, out_specs=[ANY] * na,
                          out_shape=[_sds(a.shape, F32) for a in arrs],
                          input_output_aliases={a: a for a in range(na)},
                          scratch_shapes=[pltpu.SemaphoreType.DMA((na,)), pltpu.SemaphoreType.DMA((na,))],
                          compiler_params=pltpu.CompilerParams(has_side_effects=True))(*arrs)


def _pair_sum(g, o, name):
    _, r, c = g.shape
    half = r // 2
    tr = _row_tile(half, c)
    nb = half // tr

    def kfn(c_ref, g_ref, o_ref, s32_ref, s16_ref):
        acc = g_ref[...] + o_ref[...]
        s32_ref[...] = acc
        s16_ref[...] = acc.astype(BF16)

    blk = lambda imap: pl.BlockSpec((None, tr, c), imap)
    same = lambda k, i, cr: (k, i, 0)
    spec = pltpu.PrefetchScalarGridSpec(
        num_scalar_prefetch=1, grid=(4, nb), in_specs=[blk(lambda k, i, cr: (k, cr[0] * nb + i, 0)), blk(same)],
        out_specs=[blk(same), blk(same)])
    core = lax.axis_index("c").astype(jnp.int32).reshape(1)
    return pl.pallas_call(kfn, name=name, grid_spec=spec, out_shape=[_sds((4, half, c), F32), _sds((4, half, c), BF16)],
                          compiler_params=pltpu.CompilerParams(dimension_semantics=("arbitrary", "arbitrary"),
                                                               vmem_limit_bytes=VMEM_LIMIT_V7X))(core, g, o)


def _rs_pair_stage(grads):
    got = _exchange_halves(grads)
    sums = [_pair_sum(g, o, f"rs_pair_sum_{a}") for a, (g, o) in enumerate(zip(grads, got))]
    return [s32 for s32, _ in sums], [s16 for _, s16 in sums]


def _rs_chip_stage(pair, recv):
    k_me = 2 * lax.axis_index("x") + lax.axis_index("y")
    halves = []
    for a, (p, rv) in enumerate(zip(pair, recv)):
        own = lax.dynamic_index_in_dim(p, k_me, 0, keepdims=False)
        halves.append(_sum_into_half([own, rv[0], rv[1], rv[2]], f"rs_chip_sum_{a}"))
    return _join_in_place(halves)


def _reduce_scatter(grads):
    pair, pair_bf16 = _rs_pair_stage(grads)
    return _rs_chip_stage(pair, _scatter_to_chips(pair_bf16))


def _matmul_nt_scatter(a, b, send, *, name, tm, tn, out_dtype):
    m, kdim = a.shape
    n = b.shape[0]
    ni, nj = m // tm, n // tn
    na = len(send)

    def body(a_ref, b_ref, *rest):
        send_refs, o_ref, recv_refs = rest[:na], rest[na], rest[na + 1:2 * na + 1]
        send_sems, recv_sems = rest[2 * na + 1:]
        i, j = pl.program_id(0), pl.program_id(1)

        def copies():
            x, y, c = lax.axis_index("x"), lax.axis_index("y"), lax.axis_index("c")
            out = []
            for s_i in range(na):
                for r, (fx, fy) in enumerate(CHIP_FLIPS):
                    px, py = x ^ fx, y ^ fy
                    out.append(pltpu.make_async_remote_copy(
                        src_ref=send_refs[s_i].at[2 * px + py], dst_ref=recv_refs[s_i].at[r],
                        send_sem=send_sems.at[s_i, r], recv_sem=recv_sems.at[s_i, r], device_id=(px, py, c),
                        device_id_type=MESH))
            return out

        @pl.when((i == 0) & (j == 0))
        def _():
            for cp in copies():
                cp.start()

        o_ref[...] = _dot(a_ref[...], b_ref[...], NT).astype(out_dtype)

        @pl.when((i == ni - 1) & (j == nj - 1))
        def _():
            for cp in copies():
                cp.wait()

    def kfn(*refs):
        body(*refs)
    sem = pltpu.SemaphoreType.DMA((na, 3))
    out = pl.pallas_call(
        kfn, name=name, grid=(ni, nj),
        in_specs=[pl.BlockSpec((tm, kdim), lambda i, j: (i, 0)), pl.BlockSpec((tn, kdim), lambda i, j: (j, 0))] + [ANY] * na,
        out_specs=[pl.BlockSpec((tm, tn), lambda i, j: (i, j))] + [ANY] * na,
        out_shape=[_sds((m, n), out_dtype)] + [_sds((3,) + s_a.shape[1:], s_a.dtype) for s_a in send],
        scratch_shapes=[sem, sem],
        compiler_params=pltpu.CompilerParams(dimension_semantics=("arbitrary", "arbitrary"),
                                             vmem_limit_bytes=VMEM_LIMIT_V7X, has_side_effects=True))(a, b, *send)
    return out[0], list(out[1:])


def _pad_w_in(w):
    z = jnp.zeros((w.shape[0], 1024 - 848), w.dtype)
    return jnp.concatenate([w[:, 0:832], w[:, 4928:4944], z, w[:, 832:4928], w[:, 4944:5968]], axis=1)


def _unpad_w_in(g):
    return jnp.concatenate([g[:, 0:832], g[:, 1024:5120], g[:, 832:848], g[:, 5120:6144]], axis=1)


W_IN_SHARD = W_IN_COLS // 4
W_IN_RUNS = ((0, 832, 0), (832, 4928, 1024), (4928, 4944, 832), (4944, 5968, 5120))


def _w_in_grad_blocks(p):
    def orig_cols(lo, hi):
        parts = [p[:, pa + max(lo, a) - a:pa + min(hi, b) - a] for a, b, pa in W_IN_RUNS if max(lo, a) < min(hi, b)]
        return parts[0] if len(parts) == 1 else jnp.concatenate(parts, axis=1)
    return jnp.stack([orig_cols(W_IN_SHARD * k, W_IN_SHARD * (k + 1)) for k in range(4)])


def _pad_w_in_blocks(g):
    def orig_cols(lo, hi):
        return [g[k][:, max(lo, W_IN_SHARD * k) - W_IN_SHARD * k:min(hi, W_IN_SHARD * (k + 1)) - W_IN_SHARD * k]
                for k in range(4) if max(lo, W_IN_SHARD * k) < min(hi, W_IN_SHARD * (k + 1))]
    z = jnp.zeros((g.shape[1], 1024 - 848), g.dtype)
    return jnp.concatenate(orig_cols(0, 832) + orig_cols(4928, 4944) + [z] + orig_cols(832, 4928) + orig_cols(4944, 5968),
                           axis=1)


def _pad_heads(w):
    r = w.shape[0]
    return jnp.pad(w.reshape(r, HEADS, QK_DIM), ((0, 0), (0, 0), (0, QK_PAD - QK_DIM))).reshape(r, HEADS * QK_PAD)


def _unpad_heads(w):
    r = w.shape[0]
    return w.reshape(r, HEADS, QK_PAD)[:, :, :QK_DIM].reshape(r, HEADS * QK_DIM)


def _cols_to_blocks(w):
    r = w.shape[0]
    return w.reshape(r, 4, -1).transpose(1, 0, 2)


def _blocks_to_cols(w):
    return w.transpose(1, 0, 2).reshape(w.shape[1], -1)


SMALL_ROWS = {"norm_gain": (0, 2048), "mla_q_a_gain": (16, 512), "mla_kv_a_gain": (20, 256),
              "mla_q_norm_gain": (22, 192), "mla_k_norm_gain": (24, 192), "gdn_a_log": (26, 8),
              "gdn_dt_bias": (27, 8), "gdn_out_norm_gain": (28, 128)}
LOSS_ROW = 29
SMALL_PACK_ROWS = 32
CONV_ROW = 32


def _pack_small(vals, loss=None):
    rows = []
    at = 0
    for name, (row, size) in SMALL_ROWS.items():
        assert row == at
        nr = -(-size // 128)
        rows.append(jnp.pad(vals[name].reshape(-1).astype(F32), (0, nr * 128 - size)).reshape(nr, 128))
        at += nr
    assert at == LOSS_ROW
    if loss is not None:
        rows.append(jnp.pad(loss.reshape(1, 1), ((0, 0), (0, 127))))
        at += 1
    rows.append(jnp.zeros((SMALL_PACK_ROWS - at, 128), F32))
    return jnp.concatenate(rows, axis=0)


def _unpack_small(pack, name):
    row, size = SMALL_ROWS[name]
    nr = -(-size // 128)
    return pack[row:row + nr].reshape(-1)[:size].reshape(1, size)


def _local_step(x, positions, target, norm_gain, w_in_p, q_a_gain, kv_a_gain, w_uq_p, w_ukv, q_norm_gain,
                k_norm_gain, conv_w, a_log, dt_bias, out_gain, w_out, scatter_hook=None, late_weights=None,
                first_weights=None):
    half = HALF_ROPE
    inv_freq = jnp.power(10000.0, -jnp.arange(half, dtype=F32) / half)
    ang = positions.astype(F32)[:, None] * inv_freq
    cos, sin = jnp.cos(ang), jnp.sin(ang)
    zpad = jnp.zeros((x.shape[0], 64), F32)
    cs = jnp.concatenate([cos, cos, zpad], axis=1)
    sn = jnp.concatenate([-sin, sin, zpad], axis=1)
    gq = jnp.pad(q_norm_gain.reshape(1, QK_DIM), ((0, 0), (0, QK_PAD - QK_DIM)))
    gk = jnp.pad(k_norm_gain.reshape(1, QK_DIM), ((0, 0), (0, QK_PAD - QK_DIM)))
    lane_pad = ((0, 0), (GA_LANE, 128 - GA_LANE - HEADS))
    alog128 = jnp.pad(a_log.reshape(1, HEADS), lane_pad)
    dtb128 = jnp.pad(dt_bias.reshape(1, HEADS), lane_pad)
    ng, qag, kvag, og = (norm_gain.reshape(1, -1), q_a_gain.reshape(1, -1), kv_a_gain.reshape(1, -1),
                         out_gain.reshape(1, -1))

    if first_weights is None:
        xn = _norm1_fwd(x, ng)
    else:
        shards, assemble = first_weights
        xn, gathered = _norm1_fwd_gather(x, ng, shards)
        w_in_p, conv_w = assemble(gathered)
    misc = _matmul(xn, w_in_p[:, 768:896], mode="nn", out_dtype=F32, name="in_proj_misc")
    if late_weights is None:
        proj = _matmul(xn, w_in_p, mode="nn", out_dtype=BF16, name="in_proj")
    else:
        shards, assemble = late_weights
        proj, gathered = _matmul_nn_gather(xn, w_in_p, shards, name="in_proj_gather", tm=TILES["mm"], tn=TILES["mm"],
                                           out_dtype=BF16)
        w_uq_p, w_ukv, w_out = assemble(gathered)
    cqn, ckvn = _mla_a_norm(proj, qag, kvag)
    q_pre = _matmul(cqn, w_uq_p, mode="nn", out_dtype=BF16, name="q_up")
    kv_pre = _matmul(ckvn, w_ukv, mode="nn", out_dtype=BF16, name="kv_up")
    q, k, v = _mla_post_fwd(q_pre, kv_pre, misc, cs, sn, gq, gk)
    o_mla, lse = _attn_fwd(q, k, v)
    qkv = _gdn_conv_fwd(proj, conv_w)
    gbeta = _gdn_gates_fwd(misc, alog128, dtb128)
    g_u, g_w, g_qd, g_kd, g_a, g_t, g_gl = _gdn_pre(qkv, gbeta)
    o_gdn, states = _gdn_scan_fwd(g_u, g_w, g_qd, g_kd, g_a, g_gl)
    mixed = _mix_fwd(o_mla, o_gdn, proj, og)
    dy, sq = _out_fwd(mixed, w_out, x, target)

    dmixed = _matmul(dy, w_out, mode="nt", out_dtype=BF16, name="d_mixed")
    d_w_out = _matmul(mixed, dy, mode="tn", out_dtype=F32, name="d_w_out", tk=4096)
    do_mla, do_gdn, dmg, dgg, d_out_gain, delta128 = _mix_bwd(o_mla, o_gdn, proj, og, dmixed)
    s_len = x.shape[0]
    delta_rows = delta128[:, :HEADS].T.reshape(HEADS, 1, s_len)
    dq, dk, dv = _attn_bwd(q, k, v, lse.reshape(HEADS, 1, s_len), delta_rows, do_mla)
    dq_pre, dkv_pre, dkr, d_gq, d_gk = _mla_post_bwd(q_pre, kv_pre, misc, cs, sn, gq, gk, dq, dk, dv)
    d_w_uq_p = _matmul(cqn, dq_pre, mode="tn", out_dtype=F32, name="d_w_uq", tk=1024)
    d_w_ukv = _matmul(ckvn, dkv_pre, mode="tn", out_dtype=F32, name="d_w_ukv", tk=1024)
    dcqn = _matmul(dq_pre, w_uq_p, mode="nt", out_dtype=F32, name="d_cqn")
    dckvn = _matmul(dkv_pre, w_ukv, mode="nt", out_dtype=F32, name="d_ckvn")
    dcq, dckv, d_qag, d_kvag = _mla_a_norm_bwd(proj, qag, kvag, dcqn, dckvn)
    dstates = _gdn_scan_bwd(g_w, g_qd, g_kd, g_a, g_gl, do_gdn)
    dqkv, dgbeta = _gdn_post_bwd(qkv, gbeta, g_u, g_w, g_t, states, dstates, do_gdn)
    dz = _gdn_conv_bwd_a(proj, conv_w, dqkv)
    dgx, d_conv = _gdn_conv_bwd_b(proj, conv_w, dz)
    dmisc, d_alog, d_dtb = _gdn_gates_bwd(misc, alog128, dtb128, gbeta, dgbeta, dkr)
    dproj = jnp.concatenate([dcq, dckv, dmisc, jnp.zeros((x.shape[0], 128), BF16), dmg, dgx, dgg], axis=1)
    d_w_in_p = _matmul(xn, dproj, mode="tn", out_dtype=F32, name="d_w_in", tk=4096)
    big = {"w_in": d_w_in_p, "w_uq": d_w_uq_p, "w_ukv": d_w_ukv, "w_out": d_w_out, "gdn_conv_w": d_conv}
    if scatter_hook is None:
        dxn, received = _matmul(dproj, w_in_p, mode="nt", out_dtype=BF16, name="d_xn", tm=512, tn=512), None
    else:
        dxn, received = _matmul_nt_scatter(dproj, w_in_p, scatter_hook(big), name="d_xn_scatter", tm=512, tn=512,
                                           out_dtype=BF16)
    grad_x, d_ng = _norm1_bwd(x, ng, dxn, dy)

    small = {"norm_gain": d_ng.sum(0), "mla_q_a_gain": d_qag.sum(0), "mla_kv_a_gain": d_kvag.sum(0),
             "mla_q_norm_gain": d_gq.sum(0)[:QK_DIM], "mla_k_norm_gain": d_gk.sum(0)[:QK_DIM],
             "gdn_a_log": d_alog.sum(0)[GA_LANE:GA_LANE + HEADS], "gdn_dt_bias": d_dtb.sum(0)[GA_LANE:GA_LANE + HEADS],
             "gdn_out_norm_gain": d_out_gain.sum(0)}
    return sq, grad_x, small, big, received


WEIGHTS = ["norm_gain", "w_in", "mla_q_a_gain", "mla_kv_a_gain", "w_uq", "w_ukv", "mla_q_norm_gain", "mla_k_norm_gain",
           "gdn_conv_w", "gdn_a_log", "gdn_dt_bias", "gdn_out_norm_gain", "w_out"]
BIG = ["w_in", "w_uq", "w_ukv", "w_out"]


def kernel(x, positions, norm_gain, w_in, mla_q_a_gain, mla_kv_a_gain, w_uq, w_ukv, mla_q_norm_gain, mla_k_norm_gain, gdn_conv_w, gdn_a_log, gdn_dt_bias, gdn_out_norm_gain, w_out, loss_target, m_norm_gain, m_w_in, m_mla_q_a_gain, m_mla_kv_a_gain, m_w_uq, m_w_ukv, m_mla_q_norm_gain, m_mla_k_norm_gain, m_gdn_conv_w, m_gdn_a_log, m_gdn_dt_bias, m_gdn_out_norm_gain, m_w_out, v_norm_gain, v_w_in, v_mla_q_a_gain, v_mla_kv_a_gain, v_w_uq, v_w_ukv, v_mla_q_norm_gain, v_mla_k_norm_gain, v_gdn_conv_w, v_gdn_a_log, v_gdn_dt_bias, v_gdn_out_norm_gain, v_w_out):
    w = dict(norm_gain=norm_gain, w_in=w_in, mla_q_a_gain=mla_q_a_gain, mla_kv_a_gain=mla_kv_a_gain, w_uq=w_uq,
             w_ukv=w_ukv, mla_q_norm_gain=mla_q_norm_gain, mla_k_norm_gain=mla_k_norm_gain, gdn_conv_w=gdn_conv_w,
             gdn_a_log=gdn_a_log, gdn_dt_bias=gdn_dt_bias, gdn_out_norm_gain=gdn_out_norm_gain, w_out=w_out)
    m = dict(norm_gain=m_norm_gain, w_in=m_w_in, mla_q_a_gain=m_mla_q_a_gain, mla_kv_a_gain=m_mla_kv_a_gain,
             w_uq=m_w_uq, w_ukv=m_w_ukv, mla_q_norm_gain=m_mla_q_norm_gain, mla_k_norm_gain=m_mla_k_norm_gain,
             gdn_conv_w=m_gdn_conv_w, gdn_a_log=m_gdn_a_log, gdn_dt_bias=m_gdn_dt_bias,
             gdn_out_norm_gain=m_gdn_out_norm_gain, w_out=m_w_out)
    v = dict(norm_gain=v_norm_gain, w_in=v_w_in, mla_q_a_gain=v_mla_q_a_gain, mla_kv_a_gain=v_mla_kv_a_gain,
             w_uq=v_w_uq, w_ukv=v_w_ukv, mla_q_norm_gain=v_mla_q_norm_gain, mla_k_norm_gain=v_mla_k_norm_gain,
             gdn_conv_w=v_gdn_conv_w, gdn_a_log=v_gdn_a_log, gdn_dt_bias=v_gdn_dt_bias,
             gdn_out_norm_gain=v_gdn_out_norm_gain, w_out=v_w_out)
    k_me = 2 * lax.axis_index("x") + lax.axis_index("y")

    first_weights = ([w_in[0].astype(BF16), gdn_conv_w[0]], lambda g: (_pad_w_in_blocks(g[0]), _blocks_to_cols(g[1])))
    late_weights = ([w_uq[0].astype(BF16), w_ukv[0].astype(BF16), w_out[0].astype(BF16)],
                    lambda g: (_pad_heads(_blocks_to_cols(g[0])), _blocks_to_cols(g[1]), g[2].reshape(D_MODEL, D_MODEL)))

    pair_sums = []

    def scatter_hook(big):
        pair, pair_bf16 = _rs_pair_stage([
            _w_in_grad_blocks(big["w_in"]), _cols_to_blocks(_unpad_heads(big["w_uq"])),
            _cols_to_blocks(big["w_ukv"]), big["w_out"].reshape(4, 512, D_MODEL)])
        pair_sums.extend(pair)
        return pair_bf16

    sq, grad_x, small, big, received = _local_step(
        x[0], positions[0], loss_target[0], norm_gain, None, mla_q_a_gain, mla_kv_a_gain, None, None,
        mla_q_norm_gain, mla_k_norm_gain, None, gdn_a_log, gdn_dt_bias, gdn_out_norm_gain, None, scatter_hook,
        late_weights, first_weights)

    loss_local = (0.5 / D_MODEL) * jnp.sum(sq)
    pack = jnp.concatenate([_pack_small(small, loss_local), big["gdn_conv_w"].reshape(96, 128)], axis=0)
    tot = _all_reduce_small(pack)
    loss = tot[LOSS_ROW, 0]
    conv_grad = lax.dynamic_slice_in_dim(tot[CONV_ROW:].reshape(4, 3072), k_me * 768, 768, axis=1)

    shard_grads = _rs_chip_stage(pair_sums, received)

    grads = {n: _unpack_small(tot, n) for n in SMALL_ROWS}
    grads["gdn_conv_w"] = conv_grad[None]
    for n, g in zip(BIG, shard_grads):
        grads[n] = g[None]

    delta, new_m, new_v = {}, {}, {}
    sw = _pack_small({n: w[n] for n in SMALL_ROWS})
    sm = _pack_small({n: m[n] for n in SMALL_ROWS})
    sv = _pack_small({n: v[n] for n in SMALL_ROWS})
    sd, snm, snv = _adamw(sw, tot[:SMALL_PACK_ROWS], sm, sv, "adamw_small")
    for n in SMALL_ROWS:
        delta[n], new_m[n], new_v[n] = _unpack_small(sd, n), _unpack_small(snm, n), _unpack_small(snv, n)
    for n in BIG + ["gdn_conv_w"]:
        if n == "w_in":
            d, nm, nv = _adamw(w[n][0].T, grads[n][0].T, m[n][0].T, v[n][0].T, f"adamw_{n}")
            delta[n], new_m[n], new_v[n] = d.T[None], nm.T[None], nv.T[None]
        else:
            d, nm, nv = _adamw(w[n][0], grads[n][0], m[n][0], v[n][0], f"adamw_{n}")
            delta[n], new_m[n], new_v[n] = d[None], nm[None], nv[None]

    return (loss, grad_x[None], *[grads[n] for n in WEIGHTS], *[delta[n] for n in WEIGHTS],
            *[new_m[n] for n in WEIGHTS], *[new_v[n] for n in WEIGHTS])
```

```python
import functools
import math

import jax
import jax.numpy as jnp
from jax import lax
from jax.experimental import pallas as pl
from jax.experimental.pallas import tpu as pltpu

F32 = jnp.float32
BF16 = jnp.bfloat16
MESH = pl.DeviceIdType.MESH

D_MODEL = 2048
HEADS = 8
HEAD_DIM = 128
QK_DIM = 192
QK_PAD = 256
HALF_ROPE = 32
CHUNK = 64
NORM_EPS = 1e-6
W_IN_COLS = 5968
W_IN_PAD = 6144
GA_LANE = 64
GB_LANE = 72
ADAM_LR, ADAM_B1, ADAM_B2, ADAM_EPS, ADAM_WD, ADAM_STEP = 0.001, 0.9, 0.999, 1e-08, 0.01, 10
VMEM_LIMIT_V7X = 52 * 1024 * 1024
HI = lax.Precision.HIGHEST
NN = (((1,), (0,)), ((), ()))
NT = (((1,), (1,)), ((), ()))
TN = (((0,), (0,)), ((), ()))

TILES = {"row": 512, "attn": 2048, "mm": 1024}


def _call(body, *, name, grid, in_specs, out_specs, out_shape, args, scratch=(), sem=None):
    def kfn(*refs):
        body(*refs)
    if sem is None:
        sem = ("arbitrary",) * len(grid)
    return pl.pallas_call(
        kfn, name=name, grid=grid, in_specs=in_specs, out_specs=out_specs, out_shape=out_shape,
        scratch_shapes=list(scratch),
        compiler_params=pltpu.CompilerParams(dimension_semantics=sem, vmem_limit_bytes=VMEM_LIMIT_V7X),
    )(*args)


def _rows(tm, w, cb=0):
    return pl.BlockSpec((tm, w), lambda i: (i, cb))


def _full(shape):
    n = len(shape)
    return pl.BlockSpec(shape, lambda *_: (0,) * n)


def _sds(shape, dtype):
    return jax.ShapeDtypeStruct(shape, dtype)


def _acc8(x):
    tm, c = x.shape
    return jnp.sum(x.reshape(tm // 8, 8, c), axis=0)


def _sigmoid(x):
    return 1.0 / (1.0 + jnp.exp(-x))


def _silu(x):
    return x * _sigmoid(x)


def _dsilu(x):
    s = _sigmoid(x)
    return s * (1.0 + x * (1.0 - s))


def _dot(a, b, dims=NN):
    return lax.dot_general(a.astype(BF16), b.astype(BF16), dims, preferred_element_type=F32)


def _dot_hi(a, b, dims=NN):
    return lax.dot_general(a, b, dims, precision=HI, preferred_element_type=F32)


def _matmul(a, b, *, mode, out_dtype, name, tm=None, tn=None, tk=None):
    if mode == "tn":
        kdim, m = a.shape
    else:
        m, kdim = a.shape
    n = b.shape[0] if mode == "nt" else b.shape[1]
    tm = min(tm or TILES["mm"], m)
    tn = min(tn or TILES["mm"], n)
    tk = min(tk or kdim, kdim)
    nk = kdim // tk
    dims = {"nn": NN, "nt": NT, "tn": TN}[mode]
    if mode == "tn":
        a_spec = pl.BlockSpec((tk, tm), lambda i, j, k: (k, i))
    else:
        a_spec = pl.BlockSpec((tm, tk), lambda i, j, k: (i, k))
    if mode == "nt":
        b_spec = pl.BlockSpec((tn, tk), lambda i, j, k: (j, k))
    else:
        b_spec = pl.BlockSpec((tk, tn), lambda i, j, k: (k, j))

    def body(a_ref, b_ref, o_ref):
        r = _dot(a_ref[...], b_ref[...], dims)
        if nk == 1:
            o_ref[...] = r.astype(o_ref.dtype)
        else:
            k = pl.program_id(2)

            @pl.when(k == 0)
            def _():
                o_ref[...] = r

            @pl.when(k > 0)
            def _():
                o_ref[...] += r

    if nk > 1:
        assert out_dtype == F32
    return _call(body, name=name, grid=(m // tm, n // tn, nk), in_specs=[a_spec, b_spec],
                 out_specs=pl.BlockSpec((tm, tn), lambda i, j, k: (i, j)), out_shape=_sds((m, n), out_dtype),
                 args=(a, b))


def _norm1_fwd(x, gain):
    s = x.shape[0]
    tm = min(TILES["row"], s)

    def body(x_ref, g_ref, o_ref):
        xv = x_ref[...]
        r = lax.rsqrt(jnp.mean(xv * xv, axis=-1, keepdims=True) + NORM_EPS)
        o_ref[...] = (xv * r * g_ref[...]).astype(BF16)

    return _call(body, name="norm1_fwd", grid=(s // tm,), in_specs=[_rows(tm, D_MODEL), _full((1, D_MODEL))],
                 out_specs=_rows(tm, D_MODEL), out_shape=_sds((s, D_MODEL), BF16), args=(x, gain))


def _norm1_bwd(x, gain, dxn, dy):
    s = x.shape[0]
    tm = min(TILES["row"], s)

    def body(x_ref, g_ref, dxn_ref, dy_ref, gx_ref, dg_ref):
        xv = x_ref[...]
        r = lax.rsqrt(jnp.mean(xv * xv, axis=-1, keepdims=True) + NORM_EPS)
        nrm = xv * r
        d = dxn_ref[...].astype(F32)
        dn = d * g_ref[...]
        gx_ref[...] = dy_ref[...].astype(F32) + r * (dn - nrm * jnp.mean(dn * nrm, axis=-1, keepdims=True))

        @pl.when(pl.program_id(0) == 0)
        def _():
            dg_ref[...] = jnp.zeros_like(dg_ref)

        dg_ref[...] += _acc8(d * nrm)

    return _call(body, name="norm1_bwd", grid=(s // tm,),
                 in_specs=[_rows(tm, D_MODEL), _full((1, D_MODEL)), _rows(tm, D_MODEL), _rows(tm, D_MODEL)],
                 out_specs=[_rows(tm, D_MODEL), _full((8, D_MODEL))],
                 out_shape=[_sds((s, D_MODEL), F32), _sds((8, D_MODEL), F32)], args=(x, gain, dxn, dy))


def _rms(xv, width):
    return lax.rsqrt(jnp.sum(xv * xv, axis=-1, keepdims=True) * (1.0 / width) + NORM_EPS)


def _mla_a_norm(proj, gq, gkv):
    s = proj.shape[0]
    tm = min(TILES["row"], s)

    def body(cq_ref, ckv_ref, gq_ref, gkv_ref, oq_ref, okv_ref):
        a = cq_ref[...].astype(F32)
        oq_ref[...] = (a * _rms(a, 512) * gq_ref[...]).astype(BF16)
        b = ckv_ref[...].astype(F32)
        okv_ref[...] = (b * _rms(b, 256) * gkv_ref[...]).astype(BF16)

    return _call(body, name="mla_a_norm", grid=(s // tm,),
                 in_specs=[_rows(tm, 512, 0), _rows(tm, 256, 2), _full((1, 512)), _full((1, 256))],
                 out_specs=[_rows(tm, 512), _rows(tm, 256)],
                 out_shape=[_sds((s, 512), BF16), _sds((s, 256), BF16)], args=(proj, proj, gq, gkv))


def _rms_bwd(xv, gain, d, width):
    r = _rms(xv, width)
    nrm = xv * r
    dn = d * gain
    dx = r * (dn - nrm * (jnp.sum(dn * nrm, axis=-1, keepdims=True) * (1.0 / width)))
    return dx, d * nrm


def _mla_a_norm_bwd(proj, gq, gkv, dcqn, dckvn):
    s = proj.shape[0]
    tm = min(TILES["row"], s)

    def body(cq_ref, ckv_ref, gq_ref, gkv_ref, dq_ref, dkv_ref, oq_ref, okv_ref, aq_ref, akv_ref):
        dxq, gq_part = _rms_bwd(cq_ref[...].astype(F32), gq_ref[...], dq_ref[...].astype(F32), 512)
        dxk, gk_part = _rms_bwd(ckv_ref[...].astype(F32), gkv_ref[...], dkv_ref[...].astype(F32), 256)
        oq_ref[...] = dxq.astype(BF16)
        okv_ref[...] = dxk.astype(BF16)

        @pl.when(pl.program_id(0) == 0)
        def _():
            aq_ref[...] = jnp.zeros_like(aq_ref)
            akv_ref[...] = jnp.zeros_like(akv_ref)

        aq_ref[...] += _acc8(gq_part)
        akv_ref[...] += _acc8(gk_part)

    return _call(body, name="mla_a_norm_bwd", grid=(s // tm,),
                 in_specs=[_rows(tm, 512, 0), _rows(tm, 256, 2), _full((1, 512)), _full((1, 256)),
                           _rows(tm, 512), _rows(tm, 256)],
                 out_specs=[_rows(tm, 512), _rows(tm, 256), _full((8, 512)), _full((8, 256))],
                 out_shape=[_sds((s, 512), BF16), _sds((s, 256), BF16), _sds((8, 512), F32), _sds((8, 256), F32)],
                 args=(proj, proj, gq, gkv, dcqn, dckvn))


def _swap32(r):
    lane = lax.broadcasted_iota(jnp.int32, r.shape, 1)
    return jnp.where(lane < HALF_ROPE, pltpu.roll(r, 128 - HALF_ROPE, 1), pltpu.roll(r, HALF_ROPE, 1))


def _mla_post_fwd(q_pre, kv_pre, proj, cs, sn, gq, gk):
    s = q_pre.shape[0]
    tm = min(TILES["row"], s)

    def body(qp_ref, kvp_ref, misc_ref, cs_ref, sn_ref, gq_ref, gk_ref, q_ref, k_ref, v_ref):
        csv, snv = cs_ref[...], sn_ref[...]
        lane = lax.broadcasted_iota(jnp.int32, (tm, 128), 1)
        kr = jnp.where(lane < 64, misc_ref[...], 0.0)
        for h in range(HEADS):
            for src, g_ref, o_ref in ((None, gq_ref, q_ref), (kr, gk_ref, k_ref)):
                if src is None:
                    xv = qp_ref[:, QK_PAD * h:QK_PAD * (h + 1)].astype(F32)
                else:
                    xv = jnp.concatenate([kvp_ref[:, 256 * h:256 * h + 128].astype(F32), src], axis=-1)
                y = xv * _rms(xv, QK_DIM) * g_ref[...]
                if src is None:
                    y = y * Q_PRESCALE
                hi = y[:, 128:]
                hi = hi * csv + _swap32(hi) * snv
                o_ref[:, QK_PAD * h:QK_PAD * h + 128] = y[:, :128].astype(BF16)
                o_ref[:, QK_PAD * h + 128:QK_PAD * (h + 1)] = hi.astype(BF16)
            v_ref[:, 128 * h:128 * (h + 1)] = kvp_ref[:, 256 * h + 128:256 * (h + 1)].astype(BF16)

    return _call(body, name="mla_post_fwd", grid=(s // tm,),
                 in_specs=[_rows(tm, 2048), _rows(tm, 2048), _rows(tm, 128), _rows(tm, 128), _rows(tm, 128),
                           _full((1, QK_PAD)), _full((1, QK_PAD))],
                 out_specs=[_rows(tm, 2048), _rows(tm, 2048), _rows(tm, 1024)],
                 out_shape=[_sds((s, 2048), BF16), _sds((s, 2048), BF16), _sds((s, 1024), BF16)],
                 args=(q_pre, kv_pre, proj, cs, sn, gq, gk))


def _mla_post_bwd(q_pre, kv_pre, proj, cs, sn, gq, gk, dq, dk, dv):
    s = q_pre.shape[0]
    tm = min(TILES["row"], s)

    def body(qp_ref, kvp_ref, misc_ref, cs_ref, sn_ref, gq_ref, gk_ref, dq_ref, dk_ref, dv_ref,
             oq_ref, okv_ref, okr_ref, agq_ref, agk_ref):
        csv, snv = cs_ref[...], sn_ref[...]
        lane = lax.broadcasted_iota(jnp.int32, (tm, 128), 1)
        kr = jnp.where(lane < 64, misc_ref[...], 0.0)

        @pl.when(pl.program_id(0) == 0)
        def _():
            agq_ref[...] = jnp.zeros_like(agq_ref)
            agk_ref[...] = jnp.zeros_like(agk_ref)

        dkr = jnp.zeros((tm, 128), F32)
        for h in range(HEADS):
            for which in (0, 1):
                if which == 0:
                    xv = qp_ref[:, QK_PAD * h:QK_PAD * (h + 1)].astype(F32)
                    d_ref, g_ref, a_ref = dq_ref, gq_ref, agq_ref
                else:
                    xv = jnp.concatenate([kvp_ref[:, 256 * h:256 * h + 128].astype(F32), kr], axis=-1)
                    d_ref, g_ref, a_ref = dk_ref, gk_ref, agk_ref
                dhi = d_ref[:, QK_PAD * h + 128:QK_PAD * (h + 1)].astype(F32)
                dhi = dhi * csv - _swap32(dhi) * snv
                dyv = jnp.concatenate([d_ref[:, QK_PAD * h:QK_PAD * h + 128].astype(F32), dhi], axis=-1)
                if which == 0:
                    dyv = dyv * ATTN_SCALE
                dx, gpart = _rms_bwd(xv, g_ref[...], dyv, QK_DIM)
                a_ref[...] += _acc8(gpart)
                if which == 0:
                    oq_ref[:, QK_PAD * h:QK_PAD * (h + 1)] = dx.astype(BF16)
                else:
                    okv_ref[:, 256 * h:256 * h + 128] = dx[:, :128].astype(BF16)
                    dkr = dkr + dx[:, 128:]
            okv_ref[:, 256 * h + 128:256 * (h + 1)] = dv_ref[:, 128 * h:128 * (h + 1)].astype(BF16)
        okr_ref[...] = dkr

    return _call(body, name="mla_post_bwd", grid=(s // tm,),
                 in_specs=[_rows(tm, 2048), _rows(tm, 2048), _rows(tm, 128), _rows(tm, 128), _rows(tm, 128),
                           _full((1, QK_PAD)), _full((1, QK_PAD)), _rows(tm, 2048), _rows(tm, 2048), _rows(tm, 1024)],
                 out_specs=[_rows(tm, 2048), _rows(tm, 2048), _rows(tm, 128), _full((8, QK_PAD)), _full((8, QK_PAD))],
                 out_shape=[_sds((s, 2048), BF16), _sds((s, 2048), BF16), _sds((s, 128), F32),
                            _sds((8, QK_PAD), F32), _sds((8, QK_PAD), F32)],
                 args=(q_pre, kv_pre, proj, cs, sn, gq, gk, dq, dk, dv))


ATTN_SCALE = QK_DIM ** -0.5
NEG = -1e30


LOG2E = 1.4426950408889634
LN2 = 0.6931471805599453
Q_PRESCALE = ATTN_SCALE * LOG2E
ATTN_SUB_FWD = 512
ATTN_SUB_BWD = 256


def _causal_pairs(nq, kv_major):
    prs = [(i, j) for i in range(nq) for j in range(i + 1)]
    if kv_major:
        prs.sort(key=lambda ij: (ij[1], ij[0]))
    return (jnp.asarray([p[0] for p in prs], jnp.int32), jnp.asarray([p[1] for p in prs], jnp.int32))


def _pair_call(body, *, name, tables, in_specs, out_specs, out_shape, scratch, args):
    def kfn(*refs):
        body(*refs)
    spec = pltpu.PrefetchScalarGridSpec(num_scalar_prefetch=2, grid=(HEADS, tables[0].shape[0]), in_specs=in_specs,
                                        out_specs=out_specs, scratch_shapes=list(scratch))
    return pl.pallas_call(
        kfn, name=name, grid_spec=spec, out_shape=out_shape,
        compiler_params=pltpu.CompilerParams(dimension_semantics=("parallel", "arbitrary"),
                                             vmem_limit_bytes=VMEM_LIMIT_V7X))(*tables, *args)


def _diag_mask(sc, ts, qs):
    row = lax.broadcasted_iota(jnp.int32, sc.shape, 0) + qs * ts
    col = lax.broadcasted_iota(jnp.int32, sc.shape, 1)
    return jnp.where(col <= row, sc, NEG)


def _attn_fwd(q, k, v):
    s = q.shape[0]
    t = min(TILES["attn"], s)
    ts = min(ATTN_SUB_FWD, t)
    nq = s // t

    def slabs(q_ref, k_ref, v_ref, m_s, l_s, acc_s, diag):
        def scores(qs):
            kw = (qs + 1) * ts if diag else t
            sc = lax.dot_general(q_ref[qs * ts:(qs + 1) * ts, :], k_ref[0:kw, :], NT, preferred_element_type=F32)
            return _diag_mask(sc, ts, qs) if diag else sc

        nsub = t // ts
        sc_next = scores(0)
        for qs in range(nsub):
            rq = slice(qs * ts, (qs + 1) * ts)
            kw = (qs + 1) * ts if diag else t
            sc = sc_next
            if qs + 1 < nsub:
                sc_next = scores(qs + 1)
            m_prev = m_s[rq, :]
            m_new = jnp.maximum(m_prev, jnp.max(sc, axis=-1, keepdims=True))
            p = jnp.exp2(sc - m_new)
            alpha = jnp.exp2(m_prev - m_new)
            l_s[rq, :] = alpha * l_s[rq, :] + jnp.sum(p, axis=-1, keepdims=True)
            acc_s[rq, :] = acc_s[rq, :] * alpha + lax.dot_general(p.astype(BF16), v_ref[0:kw, :], NN,
                                                                  preferred_element_type=F32)
            m_s[rq, :] = m_new

    def body(it_ref, jt_ref, q_ref, k_ref, v_ref, o_ref, lse_ref, m_s, l_s, acc_s):
        p = pl.program_id(1)
        i, j = it_ref[p], jt_ref[p]

        @pl.when(j == 0)
        def _():
            m_s[...] = jnp.full_like(m_s, NEG)
            l_s[...] = jnp.zeros_like(l_s)
            acc_s[...] = jnp.zeros_like(acc_s)

        @pl.when(j < i)
        def _():
            slabs(q_ref, k_ref, v_ref, m_s, l_s, acc_s, False)

        @pl.when(j == i)
        def _():
            slabs(q_ref, k_ref, v_ref, m_s, l_s, acc_s, True)
            o_ref[...] = acc_s[...] / l_s[...]
            lse_ref[...] = m_s[...] + jnp.log2(l_s[...])

    qb = lambda h, p, it, jt: (it[p], h)
    kb = lambda h, p, it, jt: (jt[p], h)
    return _pair_call(
        body, name="attn_fwd", tables=_causal_pairs(nq, kv_major=False),
        in_specs=[pl.BlockSpec((t, QK_PAD), qb), pl.BlockSpec((t, QK_PAD), kb), pl.BlockSpec((t, HEAD_DIM), kb)],
        out_specs=[pl.BlockSpec((t, HEAD_DIM), qb),
                   pl.BlockSpec((None, t, 1), lambda h, p, it, jt: (h, it[p], 0))],
        out_shape=[_sds((s, HEADS * HEAD_DIM), F32), _sds((HEADS, s, 1), F32)],
        scratch=[pltpu.VMEM((t, 1), F32), pltpu.VMEM((t, 1), F32), pltpu.VMEM((t, HEAD_DIM), F32)],
        args=(q, k, v))


def _attn_bwd(q, k, v, lse_rows, delta_rows, do):
    s = q.shape[0]
    t = min(TILES["attn"], s)
    ts = min(ATTN_SUB_BWD, t)
    nq = s // t

    def slabs(q_ref, k_ref, v_ref, lse_ref, delta_ref, do_ref, dq_ref, dk_ref, dv_ref, i, diag):
        def products(qs):
            rq = slice(qs * ts, (qs + 1) * ts)
            kw = (qs + 1) * ts if diag else t
            qv, dob = q_ref[rq, :], do_ref[rq, :]
            sct = lax.dot_general(k_ref[0:kw, :], qv, NT, preferred_element_type=F32)
            dpt = lax.dot_general(v_ref[0:kw, :], dob, NT, preferred_element_type=F32)
            if diag:
                row = lax.broadcasted_iota(jnp.int32, sct.shape, 0)
                col = lax.broadcasted_iota(jnp.int32, sct.shape, 1) + qs * ts
                sct = jnp.where(row <= col, sct, NEG)
            return qv, dob, sct, dpt

        nsub = t // ts
        ahead = products(0)
        for qs in range(nsub):
            rq = slice(qs * ts, (qs + 1) * ts)
            kw = (qs + 1) * ts if diag else t
            qv, dob, sct, dpt = ahead
            if qs + 1 < nsub:
                ahead = products(qs + 1)
            pt = jnp.exp2(sct - lse_ref[:, rq])
            dv_ref[0:kw, :] += lax.dot_general(pt.astype(BF16), dob, NN, preferred_element_type=F32)
            dst = (pt * (dpt - delta_ref[:, rq])).astype(BF16)
            dk_ref[0:kw, :] += lax.dot_general(dst, qv, NN, preferred_element_type=F32)
            rows = pl.ds(pl.multiple_of(i * t + qs * ts, ts), ts)
            dq_ref[rows, :] += lax.dot_general(dst, k_ref[0:kw, :], TN, preferred_element_type=F32)

    npairs = nq * (nq + 1) // 2

    def body(it_ref, jt_ref, q_ref, k_ref, v_ref, lse_ref, delta_ref, do_ref, dq_ref, dk_ref, dv_ref,
             dq_acc, dk_acc, dv_acc):
        p = pl.program_id(1)
        i, j = it_ref[p], jt_ref[p]
        refs = (q_ref, k_ref, v_ref, lse_ref, delta_ref, do_ref, dq_acc, dk_acc, dv_acc)

        @pl.when(p == 0)
        def _():
            dq_acc[...] = jnp.zeros_like(dq_acc)

        @pl.when(i == j)
        def _():
            dk_acc[...] = jnp.zeros_like(dk_acc)
            dv_acc[...] = jnp.zeros_like(dv_acc)

        @pl.when(i > j)
        def _():
            slabs(*refs, i, False)

        @pl.when(i == j)
        def _():
            slabs(*refs, i, True)

        @pl.when(i == nq - 1)
        def _():
            dk_ref[...] = (dk_acc[...] * LN2).astype(BF16)
            dv_ref[...] = dv_acc[...].astype(BF16)

        @pl.when(p == npairs - 1)
        def _():
            dq_ref[...] = dq_acc[...].astype(BF16)

    qb = lambda h, p, it, jt: (it[p], h)
    kb = lambda h, p, it, jt: (jt[p], h)
    rowb = pl.BlockSpec((None, 1, t), lambda h, p, it, jt: (h, 0, it[p]))
    return _pair_call(
        body, name="attn_bwd", tables=_causal_pairs(nq, kv_major=True),
        in_specs=[pl.BlockSpec((t, QK_PAD), qb), pl.BlockSpec((t, QK_PAD), kb), pl.BlockSpec((t, HEAD_DIM), kb),
                  rowb, rowb, pl.BlockSpec((t, HEAD_DIM), qb)],
        out_specs=[pl.BlockSpec((s, QK_PAD), lambda h, p, it, jt: (0, h)), pl.BlockSpec((t, QK_PAD), kb),
                   pl.BlockSpec((t, HEAD_DIM), kb)],
        out_shape=[_sds((s, HEADS * QK_PAD), BF16), _sds((s, HEADS * QK_PAD), BF16), _sds((s, HEADS * HEAD_DIM), BF16)],
        scratch=[pltpu.VMEM((s, QK_PAD), F32), pltpu.VMEM((t, QK_PAD), F32), pltpu.VMEM((t, HEAD_DIM), F32)],
        args=(q, k, v, lse_rows, delta_rows, do))


GDN_Q_SCALE = HEAD_DIM ** -0.5


def _shift_down(xv, prev8, sft):
    rolled = pltpu.roll(xv, sft, 0)
    top = pltpu.roll(jnp.concatenate([prev8, xv[:8]], axis=0), sft, 0)[8:]
    return jnp.concatenate([top, rolled[8:]], axis=0)


def _shift_up(xv, next8, sft):
    tm = xv.shape[0]
    rolled = pltpu.roll(xv, tm - sft, 0)
    bot = pltpu.roll(jnp.concatenate([xv[tm - 8:], next8], axis=0), 16 - sft, 0)[:8]
    return jnp.concatenate([rolled[:tm - 8], bot], axis=0)


def _conv_z(xv, prev8, w_ref):
    z = xv * w_ref[3:4, :]
    for sft in (1, 2, 3):
        z = z + _shift_down(xv, prev8, sft) * w_ref[3 - sft:4 - sft, :]
    return z


def _conv_specs(s, tm):
    nb16 = tm // 16
    cur = pl.BlockSpec((tm, 1024), lambda j, i: (i, 2 + j))
    prev = pl.BlockSpec((16, 1024), lambda j, i: (jnp.maximum(i * nb16 - 1, 0), 2 + j))
    return cur, prev


def _prev8(xp_ref, i):
    return jnp.where(i > 0, xp_ref[...].astype(F32)[8:], 0.0)


def _gdn_conv_fwd(proj, conv_w):
    s = proj.shape[0]
    tm = min(TILES["row"], s)
    cur, prev = _conv_specs(s, tm)

    def body(x_ref, xp_ref, w_ref, o_ref):
        j, i = pl.program_id(0), pl.program_id(1)
        a = _silu(_conv_z(x_ref[...].astype(F32), _prev8(xp_ref, i), w_ref))
        qk_scale = jnp.where(j == 0, GDN_Q_SCALE, 1.0)
        for h in range(HEADS):
            seg = a[:, 128 * h:128 * (h + 1)]
            r = lax.rsqrt(jnp.sum(seg * seg, axis=-1, keepdims=True) + NORM_EPS)
            o_ref[:, 128 * h:128 * (h + 1)] = jnp.where(j < 2, seg * r * qk_scale, seg)

    return _call(body, name="gdn_conv_fwd", grid=(3, s // tm),
                 in_specs=[cur, prev, pl.BlockSpec((4, 1024), lambda j, i: (0, j))],
                 out_specs=pl.BlockSpec((tm, 1024), lambda j, i: (i, j)), out_shape=_sds((s, 3072), F32),
                 args=(proj, proj, conv_w))


def _gdn_conv_bwd_a(proj, conv_w, dqkv):
    s = proj.shape[0]
    tm = min(TILES["row"], s)
    cur, prev = _conv_specs(s, tm)

    def body(x_ref, xp_ref, w_ref, d_ref, o_ref):
        j, i = pl.program_id(0), pl.program_id(1)
        z = _conv_z(x_ref[...].astype(F32), _prev8(xp_ref, i), w_ref)
        a = _silu(z)
        dsl = _dsilu(z)
        qk_scale = jnp.where(j == 0, GDN_Q_SCALE, 1.0)
        for h in range(HEADS):
            sl = slice(128 * h, 128 * (h + 1))
            seg = a[:, sl]
            dyv = d_ref[:, sl]
            r = lax.rsqrt(jnp.sum(seg * seg, axis=-1, keepdims=True) + NORM_EPS)
            yh = seg * r
            da_n = qk_scale * r * (dyv - yh * jnp.sum(yh * dyv, axis=-1, keepdims=True))
            o_ref[:, sl] = jnp.where(j < 2, da_n, dyv) * dsl[:, sl]

    return _call(body, name="gdn_conv_bwd_a", grid=(3, s // tm),
                 in_specs=[cur, prev, pl.BlockSpec((4, 1024), lambda j, i: (0, j)),
                           pl.BlockSpec((tm, 1024), lambda j, i: (i, j))],
                 out_specs=pl.BlockSpec((tm, 1024), lambda j, i: (i, j)), out_shape=_sds((s, 3072), F32),
                 args=(proj, proj, conv_w, dqkv))


def _gdn_conv_bwd_b(proj, conv_w, dz):
    s = proj.shape[0]
    tm = min(TILES["row"], s)
    nb8 = tm // 8
    last8 = s // 8 - 1
    cur, prev = _conv_specs(s, tm)

    def body(x_ref, w_ref, dz_ref, dzn_ref, dx_ref, dw_ref):
        i = pl.program_id(1)
        next8 = jnp.where(i < pl.num_programs(1) - 1, dzn_ref[...], 0.0)
        xv, dzv = x_ref[...].astype(F32), dz_ref[...]

        @pl.when(i == 0)
        def _():
            dw_ref[...] = jnp.zeros_like(dw_ref)

        dx = dzv * w_ref[3:4, :]
        dw_ref[3:4, :] += jnp.sum(dzv * xv, axis=0, keepdims=True)
        for sft in (1, 2, 3):
            up = _shift_up(dzv, next8, sft)
            dx = dx + up * w_ref[3 - sft:4 - sft, :]
            dw_ref[3 - sft:4 - sft, :] += jnp.sum(up * xv, axis=0, keepdims=True)
        dx_ref[...] = dx.astype(BF16)

    return _call(body, name="gdn_conv_bwd_b", grid=(3, s // tm),
                 in_specs=[cur, pl.BlockSpec((4, 1024), lambda j, i: (0, j)),
                           pl.BlockSpec((tm, 1024), lambda j, i: (i, j)),
                           pl.BlockSpec((8, 1024), lambda j, i: (jnp.minimum((i + 1) * nb8, last8), j))],
                 out_specs=[pl.BlockSpec((tm, 1024), lambda j, i: (i, j)), pl.BlockSpec((4, 1024), lambda j, i: (0, j))],
                 out_shape=[_sds((s, 3072), BF16), _sds((4, 3072), F32)], args=(proj, conv_w, dz, dz))


def _softplus(xv):
    return jnp.maximum(xv, 0.0) + jnp.log(1.0 + jnp.exp(-jnp.abs(xv)))


def _gdn_gates_fwd(proj, alog128, dtb128):
    s = proj.shape[0]
    tm = min(TILES["row"], s)

    def body(m_ref, a_ref, b_ref, o_ref):
        mv = m_ref[...]
        lane = lax.broadcasted_iota(jnp.int32, mv.shape, 1)
        g = -jnp.exp(a_ref[...]) * _softplus(mv + b_ref[...])
        is_g = (lane >= GA_LANE) & (lane < GA_LANE + HEADS)
        is_b = (lane >= GB_LANE) & (lane < GB_LANE + HEADS)
        o_ref[...] = jnp.where(is_g, g, jnp.where(is_b, _sigmoid(mv), 0.0))

    return _call(body, name="gdn_gates_fwd", grid=(s // tm,),
                 in_specs=[_rows(tm, 128), _full((1, 128)), _full((1, 128))],
                 out_specs=_rows(tm, 128), out_shape=_sds((s, 128), F32), args=(proj, alog128, dtb128))


def _gdn_gates_bwd(proj, alog128, dtb128, gbeta, dgbeta, dkr):
    s = proj.shape[0]
    tm = min(TILES["row"], s)

    def body(m_ref, a_ref, b_ref, gb_ref, d_ref, kr_ref, o_ref, da_ref, db_ref):
        mv, dv = m_ref[...], d_ref[...]
        lane = lax.broadcasted_iota(jnp.int32, mv.shape, 1)
        is_g = (lane >= GA_LANE) & (lane < GA_LANE + HEADS)
        is_b = (lane >= GB_LANE) & (lane < GB_LANE + HEADS)
        dga = jnp.where(is_g, dv * (-jnp.exp(a_ref[...])) * _sigmoid(mv + b_ref[...]), 0.0)
        beta = gb_ref[...]
        dgb = jnp.where(is_b, dv * beta * (1.0 - beta), 0.0)
        o_ref[...] = jnp.where(lane < 64, kr_ref[...], dga + dgb).astype(BF16)

        @pl.when(pl.program_id(0) == 0)
        def _():
            da_ref[...] = jnp.zeros_like(da_ref)
            db_ref[...] = jnp.zeros_like(db_ref)

        da_ref[...] += _acc8(jnp.where(is_g, dv * gb_ref[...], 0.0))
        db_ref[...] += _acc8(dga)

    return _call(body, name="gdn_gates_bwd", grid=(s // tm,),
                 in_specs=[_rows(tm, 128), _full((1, 128)), _full((1, 128)), _rows(tm, 128), _rows(tm, 128),
                           _rows(tm, 128)],
                 out_specs=[_rows(tm, 128), _full((8, 128)), _full((8, 128))],
                 out_shape=[_sds((s, 128), BF16), _sds((8, 128), F32), _sds((8, 128), F32)],
                 args=(proj, alog128, dtb128, gbeta, dgbeta, dkr))


def _col(mat, lane_idx, lane):
    return jnp.sum(jnp.where(lane == lane_idx, mat, 0.0), axis=-1, keepdims=True)


def _chunk_local(qh, kh, vh, gcol, bcol, ii, jj):
    lower, strict, eye = ii >= jj, ii > jj, ii == jj
    grow = jnp.sum(jnp.where(eye, gcol, 0.0), axis=0, keepdims=True)
    decay = jnp.where(lower, jnp.exp(jnp.where(lower, gcol - grow, 0.0)), 0.0)
    kb = kh * bcol
    vb = vh * bcol
    mm = _dot(kb, kh, NT)
    lmat = jnp.where(strict, mm * decay, 0.0)
    pw = -lmat
    tinv = jnp.where(eye, 1.0, 0.0) + pw
    for _ in range(5):
        pw = _dot_hi(pw, pw)
        tinv = tinv + _dot_hi(tinv, pw)
    egc = jnp.exp(gcol)
    kbg = kb * egc
    rhs = jnp.concatenate([vb, kbg], axis=-1)
    sol = _dot_hi(tinv, rhs)
    qk = _dot(qh, kh, NT)
    glast = jnp.sum(jnp.where(ii[:, :1] == CHUNK - 1, gcol, 0.0), axis=0, keepdims=True)
    ekd = jnp.exp(glast - gcol)
    return dict(decay=decay, kb=kb, vb=vb, mm=mm, lmat=lmat, tinv=tinv, egc=egc, kbg=kbg, rhs=rhs,
                u=sol[:, :HEAD_DIM], w=sol[:, HEAD_DIM:], qk=qk, amat=qk * decay, qd=qh * egc, ekd=ekd,
                kd=kh * ekd, gl=jnp.exp(glast), strict=strict, lower=lower, eye=eye)


def _tri(ii, jj):
    return jnp.where(ii >= jj, 1.0, 0.0)


def _gdn_fwd(qkv, gbeta):
    s = qkv.shape[0]
    n = s // CHUNK

    def body(qkv_ref, gb_ref, o_ref, st_ref, state):
        @pl.when(pl.program_id(0) == 0)
        def _():
            state[...] = jnp.zeros_like(state)

        ii = lax.broadcasted_iota(jnp.int32, (CHUNK, CHUNK), 0)
        jj = lax.broadcasted_iota(jnp.int32, (CHUNK, CHUNK), 1)
        lane = lax.broadcasted_iota(jnp.int32, (CHUNK, 128), 1)
        gbv = gb_ref[...]
        gc = _dot_hi(_tri(ii, jj), gbv)
        for h in range(HEADS):
            sl = slice(128 * h, 128 * (h + 1))
            qh = qkv_ref[:, 128 * h:128 * (h + 1)]
            kh = qkv_ref[:, 1024 + 128 * h:1024 + 128 * (h + 1)]
            vh = qkv_ref[:, 2048 + 128 * h:2048 + 128 * (h + 1)]
            c = _chunk_local(qh, kh, vh, _col(gc, GA_LANE + h, lane), _col(gbv, GB_LANE + h, lane), ii, jj)
            st = state[sl, :]
            st_ref[sl, :] = st
            vn = c["u"] - _dot(c["w"], st)
            o_ref[:, sl] = _dot(c["qd"], st) + _dot(c["amat"], vn)
            state[sl, :] = st * c["gl"] + _dot(c["kd"], vn, TN)

    return _call(body, name="gdn_fwd", grid=(n,),
                 in_specs=[_rows(CHUNK, 3072), _rows(CHUNK, 128)],
                 out_specs=[_rows(CHUNK, 1024), _rows(HEADS * 128, 128)],
                 out_shape=[_sds((s, 1024), F32), _sds((n * HEADS * 128, 128), F32)],
                 scratch=[pltpu.VMEM((HEADS * 128, 128), F32)], args=(qkv, gbeta))


def _gdn_bwd(qkv, gbeta, states, do):
    s = qkv.shape[0]
    n = s // CHUNK

    def body(qkv_ref, gb_ref, st_ref, do_ref, dqkv_ref, dgb_ref, dstate):
        @pl.when(pl.program_id(0) == 0)
        def _():
            dstate[...] = jnp.zeros_like(dstate)

        ii = lax.broadcasted_iota(jnp.int32, (CHUNK, CHUNK), 0)
        jj = lax.broadcasted_iota(jnp.int32, (CHUNK, CHUNK), 1)
        lane = lax.broadcasted_iota(jnp.int32, (CHUNK, 128), 1)
        row1 = ii[:, :1]
        gbv = gb_ref[...]
        gc = _dot_hi(_tri(ii, jj), gbv)
        dgc_all = jnp.zeros((CHUNK, 128), F32)
        db_all = jnp.zeros((CHUNK, 128), F32)
        for h in range(HEADS):
            sl = slice(128 * h, 128 * (h + 1))
            qh = qkv_ref[:, 128 * h:128 * (h + 1)]
            kh = qkv_ref[:, 1024 + 128 * h:1024 + 128 * (h + 1)]
            vh = qkv_ref[:, 2048 + 128 * h:2048 + 128 * (h + 1)]
            bcol = _col(gbv, GB_LANE + h, lane)
            c = _chunk_local(qh, kh, vh, _col(gc, GA_LANE + h, lane), bcol, ii, jj)
            st = st_ref[sl, :]
            dst = dstate[sl, :]
            dov = do_ref[:, sl]
            vn = c["u"] - _dot(c["w"], st)
            dvn = _dot(c["amat"], dov, TN) + _dot(c["kd"], dst)
            damat = jnp.where(c["lower"], _dot(dov, vn, NT), 0.0)
            dqd = _dot(dov, st, NT)
            dkd = _dot(vn, dst, NT)
            dw = -_dot(dvn, st, NT)
            dgl = jnp.sum(jnp.sum(st * dst, axis=-1, keepdims=True), axis=0, keepdims=True)
            dstate[sl, :] = _dot(c["qd"], dov, TN) + c["gl"] * dst - _dot(c["w"], dvn, TN)
            dsol = jnp.concatenate([dvn, dw], axis=-1)
            drhs = _dot_hi(c["tinv"], dsol, TN)
            dtinv = _dot_hi(dsol, c["rhs"], NT)
            dl = -_dot_hi(_dot_hi(c["tinv"], dtinv, TN), c["tinv"], NT)
            dl = jnp.where(c["strict"], dl, 0.0)
            dmm = dl * c["decay"]
            dqk = damat * c["decay"]
            wmat = dl * c["lmat"] + damat * c["amat"]
            dgc = jnp.sum(wmat, axis=-1, keepdims=True)
            wcol = jnp.sum(wmat, axis=0, keepdims=True)
            dgc = dgc - jnp.sum(jnp.where(c["eye"], wcol, 0.0), axis=-1, keepdims=True)
            dkb = _dot(dmm, kh) + drhs[:, HEAD_DIM:] * c["egc"]
            dk = _dot(dmm, c["kb"], TN) + _dot(dqk, qh, TN) + dkd * c["ekd"]
            dq = _dot(dqk, kh) + dqd * c["egc"]
            dgc = dgc + jnp.sum(drhs[:, HEAD_DIM:] * c["kbg"], axis=-1, keepdims=True)
            dgc = dgc + jnp.sum(dqd * c["qd"], axis=-1, keepdims=True)
            tmp = jnp.sum(dkd * c["kd"], axis=-1, keepdims=True)
            dgc = dgc - tmp
            dglast = jnp.sum(tmp, axis=0, keepdims=True) + dgl * c["gl"]
            dgc = dgc + jnp.where(row1 == CHUNK - 1, dglast, 0.0)
            dk = dk + dkb * bcol
            db = jnp.sum(dkb * kh, axis=-1, keepdims=True) + jnp.sum(drhs[:, :HEAD_DIM] * vh, axis=-1, keepdims=True)
            dqkv_ref[:, 128 * h:128 * (h + 1)] = dq
            dqkv_ref[:, 1024 + 128 * h:1024 + 128 * (h + 1)] = dk
            dqkv_ref[:, 2048 + 128 * h:2048 + 128 * (h + 1)] = drhs[:, :HEAD_DIM] * bcol
            dgc_all = dgc_all + jnp.where(lane == GA_LANE + h, dgc, 0.0)
            db_all = db_all + jnp.where(lane == GB_LANE + h, db, 0.0)
        dgb_ref[...] = _dot_hi(_tri(jj, ii), dgc_all) + db_all

    rev = lambda w: pl.BlockSpec((CHUNK, w), lambda i: (n - 1 - i, 0))
    return _call(body, name="gdn_bwd", grid=(n,),
                 in_specs=[rev(3072), rev(128), pl.BlockSpec((HEADS * 128, 128), lambda i: (n - 1 - i, 0)), rev(1024)],
                 out_specs=[rev(3072), rev(128)],
                 out_shape=[_sds((s, 3072), F32), _sds((s, 128), F32)],
                 scratch=[pltpu.VMEM((HEADS * 128, 128), F32)], args=(qkv, gbeta, states, do))


NN_B = (((2,), (1,)), ((0,), (0,)))
NT_B = (((2,), (2,)), ((0,), (0,)))
TN_B = (((1,), (1,)), ((0,), (0,)))
GDN_PRE_CHUNKS = 4
GDN_POST_CHUNKS = 2
GDN_SEQ_CHUNKS = 4


def _gather_heads(qkv_ref, gc, gbv, qs, ks, vs, gs, bs, nchunks):
    lane = lax.broadcasted_iota(jnp.int32, (CHUNK, 128), 1)
    for c in range(nchunks):
        rows = slice(CHUNK * c, CHUNK * (c + 1))
        for h in range(HEADS):
            b = HEADS * c + h
            qs[b] = qkv_ref[rows, 128 * h:128 * (h + 1)]
            ks[b] = qkv_ref[rows, 1024 + 128 * h:1024 + 128 * (h + 1)]
            vs[b] = qkv_ref[rows, 2048 + 128 * h:2048 + 128 * (h + 1)]
            gs[b] = jnp.broadcast_to(_col(gc[rows], GA_LANE + h, lane), (CHUNK, 128))
            bs[b] = jnp.broadcast_to(_col(gbv[rows], GB_LANE + h, lane), (CHUNK, 128))


def _block_tri(rows, transpose=False):
    ri = lax.broadcasted_iota(jnp.int32, (rows, rows), 0)
    ci = lax.broadcasted_iota(jnp.int32, (rows, rows), 1)
    same = (ri >> 6) == (ci >> 6)
    return jnp.where(same & ((ci >= ri) if transpose else (ri >= ci)), 1.0, 0.0)


def _local_b(q, k, v, g128, b128):
    ii = lax.broadcasted_iota(jnp.int32, (1, CHUNK, CHUNK), 1)
    jj = lax.broadcasted_iota(jnp.int32, (1, CHUNK, CHUNK), 2)
    lower, strict, eye = ii >= jj, ii > jj, ii == jj
    g64 = g128[:, :, :CHUNK]
    grow = jnp.sum(jnp.where(eye, g64, 0.0), axis=1, keepdims=True)
    decay = jnp.where(lower, jnp.exp(jnp.where(lower, g64 - grow, 0.0)), 0.0)
    kb = k * b128
    vb = v * b128
    mm = lax.dot_general(kb.astype(BF16), k.astype(BF16), NT_B, preferred_element_type=F32)
    lmat = jnp.where(strict, mm * decay, 0.0)
    egc = jnp.exp(g128)
    kbg = kb * egc
    qk = lax.dot_general(q.astype(BF16), k.astype(BF16), NT_B, preferred_element_type=F32)
    row = lax.broadcasted_iota(jnp.int32, (1, CHUNK, 128), 1)
    glast = jnp.sum(jnp.where(row == CHUNK - 1, g128, 0.0), axis=1, keepdims=True)
    ekd = jnp.exp(glast - g128)
    return dict(decay=decay, kb=kb, vb=vb, lmat=lmat, egc=egc, kbg=kbg, amat=qk * decay, qd=q * egc, ekd=ekd,
                kd=k * ekd, gl=jnp.exp(glast), lower=lower, strict=strict, eye=eye)


def _bdot(a, b, dims):
    return lax.dot_general(a.astype(BF16), b.astype(BF16), dims, preferred_element_type=F32)


def _split(a):
    hi = a.astype(BF16)
    return hi, (a - hi.astype(F32)).astype(BF16)


def _bdot_hi(a, b, dims):
    ah, al = _split(a)
    bh, bl = _split(b)
    d = lambda x, y: lax.dot_general(x, y, dims, preferred_element_type=F32)
    return d(ah, bh) + d(ah, bl) + d(al, bh)


def _gdn_pre(qkv, gbeta):
    s = qkv.shape[0]
    n = s // CHUNK
    cb = min(GDN_PRE_CHUNKS, n)
    nb = cb * HEADS
    rows = cb * CHUNK

    def body(qkv_ref, gb_ref, u_ref, w_ref, qd_ref, kd_ref, a_ref, t_ref, gl_ref, qs, ks, vs, gs, bs):
        gbv = gb_ref[...]
        gc = _dot_hi(_block_tri(rows), gbv)
        _gather_heads(qkv_ref, gc, gbv, qs, ks, vs, gs, bs, cb)
        c = _local_b(qs[...], ks[...], vs[...], gs[...], bs[...])
        pw = -c["lmat"]
        tinv = jnp.where(c["eye"], 1.0, 0.0) + pw
        for _ in range(5):
            pw = _bdot_hi(pw, pw, NN_B)
            tinv = tinv + _bdot_hi(tinv, pw, NN_B)
        u_ref[...] = _bdot_hi(tinv, c["vb"], NN_B)
        w_ref[...] = _bdot_hi(tinv, c["kbg"], NN_B).astype(BF16)
        qd_ref[...] = c["qd"].astype(BF16)
        kd_ref[...] = c["kd"].astype(BF16)
        a_ref[...] = c["amat"].astype(BF16)
        t_ref[...] = tinv
        gl_ref[...] = c["gl"]

    b3 = lambda d: pl.BlockSpec((nb, CHUNK, d), lambda i: (i, 0, 0))
    nt = n * HEADS
    return _call(body, name="gdn_pre", grid=(n // cb,),
                 in_specs=[_rows(rows, 3072), _rows(rows, 128)],
                 out_specs=[b3(128), b3(128), b3(128), b3(128), b3(CHUNK), b3(CHUNK),
                            pl.BlockSpec((nb, 1, 128), lambda i: (i, 0, 0))],
                 out_shape=[_sds((nt, CHUNK, 128), F32), _sds((nt, CHUNK, 128), BF16), _sds((nt, CHUNK, 128), BF16),
                            _sds((nt, CHUNK, 128), BF16), _sds((nt, CHUNK, CHUNK), BF16), _sds((nt, CHUNK, CHUNK), F32),
                            _sds((nt, 1, 128), F32)],
                 scratch=[pltpu.VMEM((nb, CHUNK, 128), F32)] * 5, sem=("parallel",), args=(qkv, gbeta))


def _gdn_scan_fwd(u, w, qd, kd, amat, gl):
    nt = u.shape[0]
    n = nt // HEADS
    cs = min(GDN_SEQ_CHUNKS, n)

    def body(u_ref, w_ref, qd_ref, kd_ref, a_ref, gl_ref, o_ref, st_ref, state):
        @pl.when(pl.program_id(0) == 0)
        def _():
            state[...] = jnp.zeros_like(state)

        for c in range(cs):
            sl = slice(HEADS * c, HEADS * (c + 1))
            st = state[...]
            stb = st.astype(BF16)
            st_ref[sl] = stb
            vn = u_ref[sl] - lax.dot_general(w_ref[sl], stb, NN_B, preferred_element_type=F32)
            vnb = vn.astype(BF16)
            o = (lax.dot_general(qd_ref[sl], stb, NN_B, preferred_element_type=F32)
                 + lax.dot_general(a_ref[sl], vnb, NN_B, preferred_element_type=F32))
            state[...] = st * gl_ref[sl] + lax.dot_general(kd_ref[sl], vnb, TN_B, preferred_element_type=F32)
            for h in range(HEADS):
                o_ref[CHUNK * c:CHUNK * (c + 1), 128 * h:128 * (h + 1)] = o[h]

    b3 = lambda d: pl.BlockSpec((cs * HEADS, CHUNK, d), lambda i: (i, 0, 0))
    return _call(body, name="gdn_scan_fwd", grid=(n // cs,),
                 in_specs=[b3(128), b3(128), b3(128), b3(128), b3(CHUNK), pl.BlockSpec((cs * HEADS, 1, 128), lambda i: (i, 0, 0))],
                 out_specs=[_rows(cs * CHUNK, 1024), pl.BlockSpec((cs * HEADS, 128, 128), lambda i: (i, 0, 0))],
                 out_shape=[_sds((n * CHUNK, 1024), F32), _sds((nt, 128, 128), BF16)],
                 scratch=[pltpu.VMEM((HEADS, 128, 128), F32)], args=(u, w, qd, kd, amat, gl))


def _gdn_scan_bwd(w, qd, kd, amat, gl, do):
    nt = w.shape[0]
    n = nt // HEADS
    cs = min(GDN_SEQ_CHUNKS, n)
    ng = n // cs

    def body(w_ref, qd_ref, kd_ref, a_ref, gl_ref, do_ref, ds_ref, dstate, dos):
        @pl.when(pl.program_id(0) == 0)
        def _():
            dstate[...] = jnp.zeros_like(dstate)

        for c in reversed(range(cs)):
            sl = slice(HEADS * c, HEADS * (c + 1))
            for h in range(HEADS):
                dos[h] = do_ref[CHUNK * c:CHUNK * (c + 1), 128 * h:128 * (h + 1)].astype(BF16)
            dob = dos[...]
            dst = dstate[...]
            dstb = dst.astype(BF16)
            ds_ref[sl] = dstb
            dvn = (lax.dot_general(a_ref[sl], dob, TN_B, preferred_element_type=F32)
                   + lax.dot_general(kd_ref[sl], dstb, NN_B, preferred_element_type=F32))
            dstate[...] = (lax.dot_general(qd_ref[sl], dob, TN_B, preferred_element_type=F32) + gl_ref[sl] * dst
                           - lax.dot_general(w_ref[sl], dvn.astype(BF16), TN_B, preferred_element_type=F32))

    b3 = lambda d: pl.BlockSpec((cs * HEADS, CHUNK, d), lambda i: (ng - 1 - i, 0, 0))
    return _call(body, name="gdn_scan_bwd", grid=(ng,),
                 in_specs=[b3(128), b3(128), b3(128), b3(CHUNK), pl.BlockSpec((cs * HEADS, 1, 128), lambda i: (ng - 1 - i, 0, 0)),
                           pl.BlockSpec((cs * CHUNK, 1024), lambda i: (ng - 1 - i, 0))],
                 out_specs=pl.BlockSpec((cs * HEADS, 128, 128), lambda i: (ng - 1 - i, 0, 0)),
                 out_shape=_sds((nt, 128, 128), BF16),
                 scratch=[pltpu.VMEM((HEADS, 128, 128), F32), pltpu.VMEM((HEADS, CHUNK, 128), BF16)],
                 args=(w, qd, kd, amat, gl, do))


def _gdn_post_bwd(qkv, gbeta, u, w, tinv, states, dstates, do):
    s = qkv.shape[0]
    n = s // CHUNK
    cb = min(GDN_POST_CHUNKS, n)
    nb = cb * HEADS
    rows = cb * CHUNK

    def body(qkv_ref, gb_ref, u_ref, w_ref, t_ref, st_ref, ds_ref, do_ref, dqkv_ref, dgb_ref, qs, ks, vs, gs, bs, dos):
        gbv = gb_ref[...]
        gc = _dot_hi(_block_tri(rows), gbv)
        _gather_heads(qkv_ref, gc, gbv, qs, ks, vs, gs, bs, cb)
        for c in range(cb):
            for h in range(HEADS):
                dos[HEADS * c + h] = do_ref[CHUNK * c:CHUNK * (c + 1), 128 * h:128 * (h + 1)].astype(F32)
        q, k, v, b128 = qs[...], ks[...], vs[...], bs[...]
        c = _local_b(q, k, v, gs[...], b128)
        tinv, st, dst, dov = t_ref[...], st_ref[...], ds_ref[...], dos[...]
        wv = w_ref[...]
        vn = u_ref[...] - _bdot(wv, st, NN_B)
        dvn = _bdot(c["amat"], dov, TN_B) + _bdot(c["kd"], dst, NN_B)
        damat = jnp.where(c["lower"], _bdot(dov, vn, NT_B), 0.0)
        dqd = _bdot(dov, st, NT_B)
        dkd = _bdot(vn, dst, NT_B)
        dw = -_bdot(dvn, st, NT_B)
        dgl = jnp.sum(jnp.sum(st.astype(F32) * dst.astype(F32), axis=1, keepdims=True), axis=-1, keepdims=True)
        dvb = _bdot(tinv, dvn, TN_B)
        dkbg = _bdot(tinv, dw, TN_B)
        dtinv = _bdot(dvn, c["vb"], NT_B) + _bdot(dw, c["kbg"], NT_B)
        dl = -_bdot(_bdot(tinv, dtinv, TN_B), tinv, NT_B)
        dl = jnp.where(c["strict"], dl, 0.0)
        dmm = dl * c["decay"]
        dqk = damat * c["decay"]
        wmat = dl * c["lmat"] + damat * c["amat"]
        wcol = jnp.sum(wmat, axis=1, keepdims=True)
        dgc = jnp.sum(wmat, axis=-1, keepdims=True) - jnp.sum(jnp.where(c["eye"], wcol, 0.0), axis=-1, keepdims=True)
        dkb = _bdot(dmm, k, NN_B) + dkbg * c["egc"]
        dk = _bdot(dmm, c["kb"], TN_B) + _bdot(dqk, q, TN_B) + dkd * c["ekd"] + dkb * b128
        dq = _bdot(dqk, k, NN_B) + dqd * c["egc"]
        tmp = jnp.sum(dkd * c["kd"], axis=-1, keepdims=True)
        dgc = (dgc + jnp.sum(dkbg * c["kbg"], axis=-1, keepdims=True) + jnp.sum(dqd * c["qd"], axis=-1, keepdims=True)
               - tmp)
        dglast = jnp.sum(tmp, axis=1, keepdims=True) + dgl * c["gl"][:, :, :1]
        row1 = lax.broadcasted_iota(jnp.int32, (1, CHUNK, 1), 1)
        dgc = dgc + jnp.where(row1 == CHUNK - 1, dglast, 0.0)
        db = jnp.sum(dkb * k, axis=-1, keepdims=True) + jnp.sum(dvb * v, axis=-1, keepdims=True)
        dv = dvb * b128
        lane = lax.broadcasted_iota(jnp.int32, (CHUNK, 128), 1)
        parts = []
        for cc in range(cb):
            acc = jnp.zeros((CHUNK, 128), F32)
            for h in range(HEADS):
                bi = HEADS * cc + h
                rs = slice(CHUNK * cc, CHUNK * (cc + 1))
                dqkv_ref[rs, 128 * h:128 * (h + 1)] = dq[bi]
                dqkv_ref[rs, 1024 + 128 * h:1024 + 128 * (h + 1)] = dk[bi]
                dqkv_ref[rs, 2048 + 128 * h:2048 + 128 * (h + 1)] = dv[bi]
                acc = acc + jnp.where(lane == GA_LANE + h, dgc[bi], 0.0)
            parts.append(acc)
        dgc_all = jnp.concatenate(parts, axis=0)
        dg_all = _dot_hi(_block_tri(rows, transpose=True), dgc_all)
        for cc in range(cb):
            acc = dg_all[CHUNK * cc:CHUNK * (cc + 1)]
            for h in range(HEADS):
                acc = acc + jnp.where(lane == GB_LANE + h, db[HEADS * cc + h], 0.0)
            dgb_ref[CHUNK * cc:CHUNK * (cc + 1), :] = acc

    b3 = lambda d1, d2: pl.BlockSpec((nb, d1, d2), lambda i: (i, 0, 0))
    return _call(body, name="gdn_post_bwd", grid=(n // cb,),
                 in_specs=[_rows(rows, 3072), _rows(rows, 128), b3(CHUNK, 128), b3(CHUNK, 128), b3(CHUNK, CHUNK),
                           b3(128, 128), b3(128, 128), _rows(rows, 1024)],
                 out_specs=[_rows(rows, 3072), _rows(rows, 128)],
                 out_shape=[_sds((s, 3072), F32), _sds((s, 128), F32)],
                 scratch=[pltpu.VMEM((nb, CHUNK, 128), F32)] * 6, sem=("parallel",),
                 args=(qkv, gbeta, u, w, tinv, states, dstates, do))


def _mix_fwd(o_mla, o_gdn, proj, out_gain):
    s = proj.shape[0]
    tm = min(TILES["row"], s)

    def body(om_ref, og_ref, mg_ref, gg_ref, g_ref, o_ref):
        o_ref[:, :1024] = (om_ref[...] * _silu(mg_ref[...].astype(F32))).astype(BF16)
        for h in range(HEADS):
            sl = slice(128 * h, 128 * (h + 1))
            og = og_ref[:, sl]
            on = og * _rms(og, HEAD_DIM) * g_ref[...]
            o_ref[:, 1024 + 128 * h:1024 + 128 * (h + 1)] = (on * _silu(gg_ref[:, sl].astype(F32))).astype(BF16)

    return _call(body, name="mix_fwd", grid=(s // tm,),
                 in_specs=[_rows(tm, 1024), _rows(tm, 1024), _rows(tm, 1024, 1), _rows(tm, 1024, 5), _full((1, 128))],
                 out_specs=_rows(tm, 2048), out_shape=_sds((s, 2048), BF16), args=(o_mla, o_gdn, proj, proj, out_gain))


def _mix_bwd(o_mla, o_gdn, proj, out_gain, dmixed):
    s = proj.shape[0]
    tm = min(TILES["row"], s)

    def body(om_ref, og_ref, mg_ref, gg_ref, g_ref, dm_ref, dg_ref, dom_ref, dog_ref, dmg_ref, dgg_ref, ag_ref,
             delta_ref):
        @pl.when(pl.program_id(0) == 0)
        def _():
            ag_ref[...] = jnp.zeros_like(ag_ref)

        mg = mg_ref[...].astype(F32)
        dm = dm_ref[...].astype(F32)
        om = om_ref[...]
        dom = (dm * _silu(mg)).astype(BF16)
        dom_ref[...] = dom
        dmg_ref[...] = (dm * om * _dsilu(mg)).astype(BF16)
        prod = dom.astype(F32) * om
        lane = lax.broadcasted_iota(jnp.int32, (tm, 128), 1)
        delta = jnp.zeros((tm, 128), F32)
        for h in range(HEADS):
            delta = delta + jnp.where(lane == h, jnp.sum(prod[:, 128 * h:128 * (h + 1)], axis=-1, keepdims=True), 0.0)
        delta_ref[...] = delta
        for h in range(HEADS):
            sl = slice(128 * h, 128 * (h + 1))
            og, gg, d = og_ref[:, sl], gg_ref[:, sl].astype(F32), dg_ref[:, sl].astype(F32)
            on = og * _rms(og, HEAD_DIM) * g_ref[...]
            dgg_ref[:, sl] = (d * on * _dsilu(gg)).astype(BF16)
            dx, gpart = _rms_bwd(og, g_ref[...], d * _silu(gg), HEAD_DIM)
            dog_ref[:, sl] = dx.astype(BF16)
            ag_ref[...] += _acc8(gpart)

    return _call(body, name="mix_bwd", grid=(s // tm,),
                 in_specs=[_rows(tm, 1024), _rows(tm, 1024), _rows(tm, 1024, 1), _rows(tm, 1024, 5), _full((1, 128)),
                           _rows(tm, 1024, 0), _rows(tm, 1024, 1)],
                 out_specs=[_rows(tm, 1024), _rows(tm, 1024), _rows(tm, 1024), _rows(tm, 1024), _full((8, 128)),
                            _rows(tm, 128)],
                 out_shape=[_sds((s, 1024), BF16), _sds((s, 1024), BF16), _sds((s, 1024), BF16), _sds((s, 1024), BF16),
                            _sds((8, 128), F32), _sds((s, 128), F32)],
                 args=(o_mla, o_gdn, proj, proj, out_gain, dmixed, dmixed))


def _out_fwd(mixed, w_out, x, target):
    s = x.shape[0]
    tm = min(TILES["mm"], s)
    tn = min(TILES["mm"], D_MODEL)

    def body(m_ref, w_ref, x_ref, t_ref, dy_ref, acc_ref):
        err = x_ref[...] + _dot(m_ref[...], w_ref[...]) - t_ref[...]
        dy_ref[...] = (err * (1.0 / D_MODEL)).astype(BF16)

        @pl.when(pl.program_id(1) == 0)
        def _():
            acc_ref[...] = jnp.zeros_like(acc_ref)

        acc_ref[...] += _acc8(err * err)

    return _call(body, name="out_fwd", grid=(D_MODEL // tn, s // tm),
                 in_specs=[pl.BlockSpec((tm, D_MODEL), lambda j, i: (i, 0)), pl.BlockSpec((D_MODEL, tn), lambda j, i: (0, j)),
                           pl.BlockSpec((tm, tn), lambda j, i: (i, j)), pl.BlockSpec((tm, tn), lambda j, i: (i, j))],
                 out_specs=[pl.BlockSpec((tm, tn), lambda j, i: (i, j)), pl.BlockSpec((8, tn), lambda j, i: (0, j))],
                 out_shape=[_sds((s, D_MODEL), BF16), _sds((8, D_MODEL), F32)], args=(mixed, w_out, x, target))


def _row_tile(r, c):
    if r % 8 != 0:
        return r
    t = 8
    while r % (2 * t) == 0 and 2 * t * c * 4 <= (1 << 20):
        t *= 2
    return t


def _sum_arrays(parts, name, also_bf16=False):
    r, c = parts[0].shape
    tr = _row_tile(r, c)
    n = len(parts)

    def body(*refs):
        acc = refs[0][...].astype(F32)
        for p_ref in refs[1:n]:
            acc = acc + p_ref[...].astype(F32)
        refs[n][...] = acc
        if also_bf16:
            refs[n + 1][...] = acc.astype(BF16)

    nout = 2 if also_bf16 else 1
    out = _call(body, name=name, grid=(r // tr,), in_specs=[_rows(tr, c)] * n, out_specs=[_rows(tr, c)] * nout,
                out_shape=[_sds((r, c), F32), _sds((r, c), BF16)][:nout], args=tuple(parts))
    return out if also_bf16 else out[0]


def _adamw(w, g, m, v, name):
    r, c = w.shape
    c1 = 1.0 - ADAM_B1 ** ADAM_STEP
    c2 = 1.0 - ADAM_B2 ** ADAM_STEP

    def body(w_ref, g_ref, m_ref, v_ref, d_ref, nm_ref, nv_ref):
        gv = g_ref[...]
        nm = ADAM_B1 * m_ref[...] + (1.0 - ADAM_B1) * gv
        nv = ADAM_B2 * v_ref[...] + (1.0 - ADAM_B2) * (gv * gv)
        nm_ref[...] = nm
        nv_ref[...] = nv
        d_ref[...] = -ADAM_LR * ((nm / c1) / (jnp.sqrt(nv / c2) + ADAM_EPS) + ADAM_WD * w_ref[...])

    if r % 8 == 0:
        tr = _row_tile(r, c)
        grid, spec = (r // tr,), _rows(tr, c)
    else:
        tc = c
        while tc % 256 == 0 and r * tc * 4 > (3 << 19):
            tc //= 2
        grid, spec = (c // tc,), pl.BlockSpec((r, tc), lambda i: (0, i))
    return _call(body, name=name, grid=grid, in_specs=[spec] * 4, out_specs=[spec] * 3,
                 out_shape=[_sds((r, c), F32)] * 3, args=(w, g, m, v))


ANY = pl.BlockSpec(memory_space=pl.ANY)
CHIP_FLIPS = ((1, 0), (0, 1), (1, 1))


def _comm_call(body, *, name, n_in, out_shape, scratch):
    def kfn(*refs):
        body(*refs)
    return pl.pallas_call(kfn, name=name, in_specs=[ANY] * n_in, out_specs=[ANY] * len(out_shape), out_shape=out_shape,
                          scratch_shapes=list(scratch),
                          compiler_params=pltpu.CompilerParams(has_side_effects=True))


def _all_gather_chips(shards):
    na = len(shards)

    def body(*refs):
        copies = _gather_copies(refs[:na], refs[na:2 * na], *refs[2 * na:])
        _gather_start(copies)
        _gather_finish(copies)

    out_shape = [_sds((4,) + a.shape, a.dtype) for a in shards]
    sem = pltpu.SemaphoreType.DMA((na, 3))
    got = _comm_call(body, name="all_gather_weights", n_in=na, out_shape=out_shape, scratch=[sem, sem, sem, sem])(*shards)
    return _place_own_blocks(got, shards)


def _gather_copies(ins, outs, send_sems, recv_sems, fwd_send, fwd_recv):
    x, y, c = lax.axis_index("x"), lax.axis_index("y"), lax.axis_index("c")
    my_k = 2 * x + y
    direct, forwards = [], []
    for a in range(len(ins)):
        rows = ins[a].shape[0]
        for r, (fx, fy) in enumerate(CHIP_FLIPS):
            px, py = x ^ fx, y ^ fy
            if rows % 32 == 0:
                mine = pl.ds(pl.multiple_of(c * (rows // 2), 16), rows // 2)
                other = pl.ds(pl.multiple_of((1 - c) * (rows // 2), 16), rows // 2)
                rc = pltpu.make_async_remote_copy(
                    src_ref=ins[a].at[mine], dst_ref=outs[a].at[my_k, mine], send_sem=send_sems.at[a, r],
                    recv_sem=recv_sems.at[a, r], device_id=(px, py, c), device_id_type=MESH)
                landed = outs[a].at[2 * px + py, mine]
                fw = pltpu.make_async_remote_copy(
                    src_ref=landed, dst_ref=landed, send_sem=fwd_send.at[a, r], recv_sem=fwd_recv.at[a, r],
                    device_id=(x, y, 1 - c), device_id_type=MESH)
                from_sib = outs[a].at[2 * px + py, other]
                fw_in = pltpu.make_async_remote_copy(
                    src_ref=from_sib, dst_ref=from_sib, send_sem=fwd_send.at[a, r], recv_sem=fwd_recv.at[a, r],
                    device_id=(x, y, 1 - c), device_id_type=MESH)
                forwards.append((rc, fw, fw_in))
            else:
                direct.append(pltpu.make_async_remote_copy(
                    src_ref=ins[a], dst_ref=outs[a].at[my_k], send_sem=send_sems.at[a, r],
                    recv_sem=recv_sems.at[a, r], device_id=(px, py, c), device_id_type=MESH))
    return forwards, direct


def _gather_start(copies):
    forwards, direct = copies
    for rc, _, _ in forwards:
        rc.start()
    for rc in direct:
        rc.start()


def _gather_finish(copies):
    forwards, direct = copies
    for rc, fw, _ in forwards:
        rc.wait_recv()
        fw.start()
    for rc, fw, fw_in in forwards:
        rc.wait_send()
        fw.wait_send()
        fw_in.wait_recv()
    for rc in direct:
        rc.wait()


def _place_own_blocks(got, shards):
    my_k = 2 * lax.axis_index("x") + lax.axis_index("y")
    return [lax.dynamic_update_index_in_dim(g, a, my_k, 0) for g, a in zip(got, shards)]


def _norm1_fwd_gather(x, gain, shards):
    s = x.shape[0]
    tm = min(TILES["row"], s)
    ni = s // tm
    na = len(shards)

    def kfn(x_ref, g_ref, *rest):
        o_ref = rest[na]
        sems = rest[2 * na + 1:]
        i = pl.program_id(0)

        @pl.when(i == 0)
        def _():
            _gather_start(_gather_copies(rest[:na], rest[na + 1:2 * na + 1], *sems))

        xv = x_ref[...]
        r = lax.rsqrt(jnp.mean(xv * xv, axis=-1, keepdims=True) + NORM_EPS)
        o_ref[...] = (xv * r * g_ref[...]).astype(BF16)

        @pl.when(i == ni - 1)
        def _():
            _gather_finish(_gather_copies(rest[:na], rest[na + 1:2 * na + 1], *sems))

    sem = pltpu.SemaphoreType.DMA((na, 3))
    out = pl.pallas_call(
        kfn, name="norm1_fwd_gather", grid=(ni,),
        in_specs=[_rows(tm, D_MODEL), _full((1, D_MODEL))] + [ANY] * na,
        out_specs=[_rows(tm, D_MODEL)] + [ANY] * na,
        out_shape=[_sds((s, D_MODEL), BF16)] + [_sds((4,) + s_a.shape, s_a.dtype) for s_a in shards],
        scratch_shapes=[sem, sem, sem, sem],
        compiler_params=pltpu.CompilerParams(dimension_semantics=("arbitrary",), vmem_limit_bytes=VMEM_LIMIT_V7X,
                                             has_side_effects=True))(x, gain, *shards)
    return out[0], _place_own_blocks(list(out[1:]), shards)


def _matmul_nn_gather(a, b, shards, *, name, tm, tn, out_dtype):
    m, kdim = a.shape
    n = b.shape[1]
    ni, nj = m // tm, n // tn
    na = len(shards)

    def body(a_ref, b_ref, *rest):
        o_ref = rest[na]
        sems = rest[2 * na + 1:]
        i, j = pl.program_id(0), pl.program_id(1)

        @pl.when((i == 0) & (j == 0))
        def _():
            _gather_start(_gather_copies(rest[:na], rest[na + 1:2 * na + 1], *sems))

        o_ref[...] = _dot(a_ref[...], b_ref[...]).astype(out_dtype)

        @pl.when((i == ni - 1) & (j == nj - 1))
        def _():
            _gather_finish(_gather_copies(rest[:na], rest[na + 1:2 * na + 1], *sems))

    def kfn(*refs):
        body(*refs)
    sem = pltpu.SemaphoreType.DMA((na, 3))
    out = pl.pallas_call(
        kfn, name=name, grid=(ni, nj),
        in_specs=[pl.BlockSpec((tm, kdim), lambda i, j: (i, 0)), pl.BlockSpec((kdim, tn), lambda i, j: (0, j))] + [ANY] * na,
        out_specs=[pl.BlockSpec((tm, tn), lambda i, j: (i, j))] + [ANY] * na,
        out_shape=[_sds((m, n), out_dtype)] + [_sds((4,) + s_a.shape, s_a.dtype) for s_a in shards],
        scratch_shapes=[sem, sem, sem, sem],
        compiler_params=pltpu.CompilerParams(dimension_semantics=("arbitrary", "arbitrary"),
                                             vmem_limit_bytes=VMEM_LIMIT_V7X, has_side_effects=True))(a, b, *shards)
    return out[0], _place_own_blocks(list(out[1:]), shards)


def _all_reduce_small(vec):
    r = vec.shape[0]

    def body(v_ref, o_ref, gath, send_sems, recv_sems):
        x, y, c = lax.axis_index("x"), lax.axis_index("y"), lax.axis_index("c")
        me = 4 * x + 2 * y + c
        gath[me] = v_ref[...]
        copies = []
        for rel in range(1, 8):
            fx, fy, fc = (rel >> 2) & 1, (rel >> 1) & 1, rel & 1
            rc = pltpu.make_async_remote_copy(
                src_ref=v_ref, dst_ref=gath.at[me], send_sem=send_sems.at[rel - 1], recv_sem=recv_sems.at[rel - 1],
                device_id=(x ^ fx, y ^ fy, c ^ fc), device_id_type=MESH)
            rc.start()
            copies.append(rc)
        for rc in copies:
            rc.wait()
        acc = gath[0]
        for d in range(1, 8):
            acc = acc + gath[d]
        o_ref[...] = acc

    def kfn(*refs):
        body(*refs)
    vm = pl.BlockSpec(memory_space=pltpu.VMEM)
    return pl.pallas_call(kfn, name="all_reduce_small", in_specs=[vm], out_specs=vm, out_shape=_sds((r, 128), F32),
                          scratch_shapes=[pltpu.VMEM((8, r, 128), F32), pltpu.SemaphoreType.DMA((7,)),
                                          pltpu.SemaphoreType.DMA((7,))],
                          compiler_params=pltpu.CompilerParams(has_side_effects=True))(vec)


def _exchange_halves(arrs):
    na = len(arrs)

    def body(*refs):
        ins, outs = refs[:na], refs[na:2 * na]
        send_sems, recv_sems = refs[2 * na:]
        x, y, c = lax.axis_index("x"), lax.axis_index("y"), lax.axis_index("c")
        copies = []
        for a in range(na):
            half = ins[a].shape[1] // 2
            src = ins[a].at[:, pl.ds(pl.multiple_of((1 - c) * half, 8), half), :]
            rc = pltpu.make_async_remote_copy(src_ref=src, dst_ref=outs[a], send_sem=send_sems.at[a],
                                              recv_sem=recv_sems.at[a], device_id=(x, y, 1 - c), device_id_type=MESH)
            rc.start()
            copies.append(rc)
        for rc in copies:
            rc.wait()

    out_shape = [_sds((4, a.shape[1] // 2, a.shape[2]), F32) for a in arrs]
    return _comm_call(body, name="rs_pair_exchange", n_in=na, out_shape=out_shape,
                      scratch=[pltpu.SemaphoreType.DMA((na,)), pltpu.SemaphoreType.DMA((na,))])(*arrs)


def _scatter_to_chips(arrs):
    na = len(arrs)

    def body(*refs):
        ins, outs = refs[:na], refs[na:2 * na]
        send_sems, recv_sems = refs[2 * na:]
        x, y, c = lax.axis_index("x"), lax.axis_index("y"), lax.axis_index("c")
        copies = []
        for a in range(na):
            for r, (fx, fy) in enumerate(CHIP_FLIPS):
                px, py = x ^ fx, y ^ fy
                rc = pltpu.make_async_remote_copy(
                    src_ref=ins[a].at[2 * px + py], dst_ref=outs[a].at[r], send_sem=send_sems.at[a, r],
                    recv_sem=recv_sems.at[a, r], device_id=(px, py, c), device_id_type=MESH)
                rc.start()
                copies.append(rc)
        for rc in copies:
            rc.wait()

    out_shape = [_sds((3,) + a.shape[1:], a.dtype) for a in arrs]
    return _comm_call(body, name="rs_chip_scatter", n_in=na, out_shape=out_shape,
                      scratch=[pltpu.SemaphoreType.DMA((na, 3)), pltpu.SemaphoreType.DMA((na, 3))])(*arrs)


def _sum_into_half(parts, name):
    r2, c = parts[0].shape
    tr = _row_tile(r2, c)
    nb = r2 // tr
    n = len(parts)

    def kfn(c_ref, *refs):
        acc = refs[0][...].astype(F32)
        for p_ref in refs[1:n]:
            acc = acc + p_ref[...].astype(F32)
        refs[n][...] = acc

    spec = pltpu.PrefetchScalarGridSpec(
        num_scalar_prefetch=1, grid=(nb,), in_specs=[pl.BlockSpec((tr, c), lambda i, cr: (i, 0))] * n,
        out_specs=pl.BlockSpec((tr, c), lambda i, cr: (cr[0] * nb + i, 0)))
    core = lax.axis_index("c").astype(jnp.int32).reshape(1)
    return pl.pallas_call(kfn, name=name, grid_spec=spec, out_shape=_sds((2 * r2, c), F32),
                          compiler_params=pltpu.CompilerParams(dimension_semantics=("arbitrary",),
                                                               vmem_limit_bytes=VMEM_LIMIT_V7X))(core, *parts)


def _join_in_place(arrs):
    na = len(arrs)

    def body(*refs):
        outs = refs[na:2 * na]
        send_sems, recv_sems = refs[2 * na:]
        x, y, c = lax.axis_index("x"), lax.axis_index("y"), lax.axis_index("c")
        copies = []
        for a in range(na):
            half = outs[a].shape[0] // 2
            mine = outs[a].at[pl.ds(pl.multiple_of(c * half, 8), half), :]
            rc = pltpu.make_async_remote_copy(src_ref=mine, dst_ref=mine, send_sem=send_sems.at[a],
                                              recv_sem=recv_sems.at[a], device_id=(x, y, 1 - c), device_id_type=MESH)
            rc.start()
            copies.append(rc)
        for rc in copies:
            rc.wait()

    def kfn(*refs):
        body(*refs)
    return pl.pallas_call(kfn, name="rs_pair_join", in_specs=[ANY] * na, out_specs=[ANY] * na,
                          out_shape=[_sds(a.shape, F32) for a in arrs],
                          input_output_aliases={a: a for a in range(na)},
                          scratch_shapes=[pltpu.SemaphoreType.DMA((na,)), pltpu.SemaphoreType.DMA((na,))],
                          compiler_params=pltpu.CompilerParams(has_side_effects=True))(*arrs)


def _pair_sum(g, o, name):
    _, r, c = g.shape
    half = r // 2
    tr = _row_tile(half, c)
    nb = half // tr

    def kfn(c_ref, g_ref, o_ref, s32_ref, s16_ref):
        acc = g_ref[...] + o_ref[...]
        s32_ref[...] = acc
        s16_ref[...] = acc.astype(BF16)

    blk = lambda imap: pl.BlockSpec((None, tr, c), imap)
    same = lambda k, i, cr: (k, i, 0)
    spec = pltpu.PrefetchScalarGridSpec(
        num_scalar_prefetch=1, grid=(4, nb), in_specs=[blk(lambda k, i, cr: (k, cr[0] * nb + i, 0)), blk(same)],
        out_specs=[blk(same), blk(same)])
    core = lax.axis_index("c").astype(jnp.int32).reshape(1)
    return pl.pallas_call(kfn, name=name, grid_spec=spec, out_shape=[_sds((4, half, c), F32), _sds((4, half, c), BF16)],
                          compiler_params=pltpu.CompilerParams(dimension_semantics=("arbitrary", "arbitrary"),
                                                               vmem_limit_bytes=VMEM_LIMIT_V7X))(core, g, o)


def _rs_pair_stage(grads):
    got = _exchange_halves(grads)
    sums = [_pair_sum(g, o, f"rs_pair_sum_{a}") for a, (g, o) in enumerate(zip(grads, got))]
    return [s32 for s32, _ in sums], [s16 for _, s16 in sums]


def _rs_chip_stage(pair, recv):
    k_me = 2 * lax.axis_index("x") + lax.axis_index("y")
    halves = []
    for a, (p, rv) in enumerate(zip(pair, recv)):
        own = lax.dynamic_index_in_dim(p, k_me, 0, keepdims=False)
        halves.append(_sum_into_half([own, rv[0], rv[1], rv[2]], f"rs_chip_sum_{a}"))
    return _join_in_place(halves)


def _reduce_scatter(grads):
    pair, pair_bf16 = _rs_pair_stage(grads)
    return _rs_chip_stage(pair, _scatter_to_chips(pair_bf16))


def _matmul_nt_scatter(a, b, send, *, name, tm, tn, out_dtype):
    m, kdim = a.shape
    n = b.shape[0]
    ni, nj = m // tm, n // tn
    na = len(send)

    def body(a_ref, b_ref, *rest):
        send_refs, o_ref, recv_refs = rest[:na], rest[na], rest[na + 1:2 * na + 1]
        send_sems, recv_sems = rest[2 * na + 1:]
        i, j = pl.program_id(0), pl.program_id(1)

        def copies():
            x, y, c = lax.axis_index("x"), lax.axis_index("y"), lax.axis_index("c")
            out = []
            for s_i in range(na):
                for r, (fx, fy) in enumerate(CHIP_FLIPS):
                    px, py = x ^ fx, y ^ fy
                    out.append(pltpu.make_async_remote_copy(
                        src_ref=send_refs[s_i].at[2 * px + py], dst_ref=recv_refs[s_i].at[r],
                        send_sem=send_sems.at[s_i, r], recv_sem=recv_sems.at[s_i, r], device_id=(px, py, c),
                        device_id_type=MESH))
            return out

        @pl.when((i == 0) & (j == 0))
        def _():
            for cp in copies():
                cp.start()

        o_ref[...] = _dot(a_ref[...], b_ref[...], NT).astype(out_dtype)

        @pl.when((i == ni - 1) & (j == nj - 1))
        def _():
            for cp in copies():
                cp.wait()

    def kfn(*refs):
        body(*refs)
    sem = pltpu.SemaphoreType.DMA((na, 3))
    out = pl.pallas_call(
        kfn, name=name, grid=(ni, nj),
        in_specs=[pl.BlockSpec((tm, kdim), lambda i, j: (i, 0)), pl.BlockSpec((tn, kdim), lambda i, j: (j, 0))] + [ANY] * na,
        out_specs=[pl.BlockSpec((tm, tn), lambda i, j: (i, j))] + [ANY] * na,
        out_shape=[_sds((m, n), out_dtype)] + [_sds((3,) + s_a.shape[1:], s_a.dtype) for s_a in send],
        scratch_shapes=[sem, sem],
        compiler_params=pltpu.CompilerParams(dimension_semantics=("arbitrary", "arbitrary"),
                                             vmem_limit_bytes=VMEM_LIMIT_V7X, has_side_effects=True))(a, b, *send)
    return out[0], list(out[1:])


def _pad_w_in(w):
    z = jnp.zeros((w.shape[0], 1024 - 848), w.dtype)
    return jnp.concatenate([w[:, 0:832], w[:, 4928:4944], z, w[:, 832:4928], w[:, 4944:5968]], axis=1)


def _unpad_w_in(g):
    return jnp.concatenate([g[:, 0:832], g[:, 1024:5120], g[:, 832:848], g[:, 5120:6144]], axis=1)


W_IN_SHARD = W_IN_COLS // 4
W_IN_RUNS = ((0, 832, 0), (832, 4928, 1024), (4928, 4944, 832), (4944, 5968, 5120))


def _w_in_grad_blocks(p):
    def orig_cols(lo, hi):
        parts = [p[:, pa + max(lo, a) - a:pa + min(hi, b) - a] for a, b, pa in W_IN_RUNS if max(lo, a) < min(hi, b)]
        return parts[0] if len(parts) == 1 else jnp.concatenate(parts, axis=1)
    return jnp.stack([orig_cols(W_IN_SHARD * k, W_IN_SHARD * (k + 1)) for k in range(4)])


def _pad_w_in_blocks(g):
    def orig_cols(lo, hi):
        return [g[k][:, max(lo, W_IN_SHARD * k) - W_IN_SHARD * k:min(hi, W_IN_SHARD * (k + 1)) - W_IN_SHARD * k]
                for k in range(4) if max(lo, W_IN_SHARD * k) < min(hi, W_IN_SHARD * (k + 1))]
    z = jnp.zeros((g.shape[1], 1024 - 848), g.dtype)
    return jnp.concatenate(orig_cols(0, 832) + orig_cols(4928, 4944) + [z] + orig_cols(832, 4928) + orig_cols(4944, 5968),
                           axis=1)


def _pad_heads(w):
    r = w.shape[0]
    return jnp.pad(w.reshape(r, HEADS, QK_DIM), ((0, 0), (0, 0), (0, QK_PAD - QK_DIM))).reshape(r, HEADS * QK_PAD)


def _unpad_heads(w):
    r = w.shape[0]
    return w.reshape(r, HEADS, QK_PAD)[:, :, :QK_DIM].reshape(r, HEADS * QK_DIM)


def _cols_to_blocks(w):
    r = w.shape[0]
    return w.reshape(r, 4, -1).transpose(1, 0, 2)


def _blocks_to_cols(w):
    return w.transpose(1, 0, 2).reshape(w.shape[1], -1)


SMALL_ROWS = {"norm_gain": (0, 2048), "mla_q_a_gain": (16, 512), "mla_kv_a_gain": (20, 256),
              "mla_q_norm_gain": (22, 192), "mla_k_norm_gain": (24, 192), "gdn_a_log": (26, 8),
              "gdn_dt_bias": (27, 8), "gdn_out_norm_gain": (28, 128)}
LOSS_ROW = 29
SMALL_PACK_ROWS = 32
CONV_ROW = 32


def _pack_small(vals, loss=None):
    rows = []
    at = 0
    for name, (row, size) in SMALL_ROWS.items():
        assert row == at
        nr = -(-size // 128)
        rows.append(jnp.pad(vals[name].reshape(-1).astype(F32), (0, nr * 128 - size)).reshape(nr, 128))
        at += nr
    assert at == LOSS_ROW
    if loss is not None:
        rows.append(jnp.pad(loss.reshape(1, 1), ((0, 0), (0, 127))))
        at += 1
    rows.append(jnp.zeros((SMALL_PACK_ROWS - at, 128), F32))
    return jnp.concatenate(rows, axis=0)


def _unpack_small(pack, name):
    row, size = SMALL_ROWS[name]
    nr = -(-size // 128)
    return pack[row:row + nr].reshape(-1)[:size].reshape(1, size)


def _local_step(x, positions, target, norm_gain, w_in_p, q_a_gain, kv_a_gain, w_uq_p, w_ukv, q_norm_gain,
                k_norm_gain, conv_w, a_log, dt_bias, out_gain, w_out, scatter_hook=None, late_weights=None,
                first_weights=None):
    half = HALF_ROPE
    inv_freq = jnp.power(10000.0, -jnp.arange(half, dtype=F32) / half)
    ang = positions.astype(F32)[:, None] * inv_freq
    cos, sin = jnp.cos(ang), jnp.sin(ang)
    zpad = jnp.zeros((x.shape[0], 64), F32)
    cs = jnp.concatenate([cos, cos, zpad], axis=1)
    sn = jnp.concatenate([-sin, sin, zpad], axis=1)
    gq = jnp.pad(q_norm_gain.reshape(1, QK_DIM), ((0, 0), (0, QK_PAD - QK_DIM)))
    gk = jnp.pad(k_norm_gain.reshape(1, QK_DIM), ((0, 0), (0, QK_PAD - QK_DIM)))
    lane_pad = ((0, 0), (GA_LANE, 128 - GA_LANE - HEADS))
    alog128 = jnp.pad(a_log.reshape(1, HEADS), lane_pad)
    dtb128 = jnp.pad(dt_bias.reshape(1, HEADS), lane_pad)
    ng, qag, kvag, og = (norm_gain.reshape(1, -1), q_a_gain.reshape(1, -1), kv_a_gain.reshape(1, -1),
                         out_gain.reshape(1, -1))

    if first_weights is None:
        xn = _norm1_fwd(x, ng)
    else:
        shards, assemble = first_weights
        xn, gathered = _norm1_fwd_gather(x, ng, shards)
        w_in_p, conv_w = assemble(gathered)
    misc = _matmul(xn, w_in_p[:, 768:896], mode="nn", out_dtype=F32, name="in_proj_misc")
    if late_weights is None:
        proj = _matmul(xn, w_in_p, mode="nn", out_dtype=BF16, name="in_proj")
    else:
        shards, assemble = late_weights
        proj, gathered = _matmul_nn_gather(xn, w_in_p, shards, name="in_proj_gather", tm=TILES["mm"], tn=TILES["mm"],
                                           out_dtype=BF16)
        w_uq_p, w_ukv, w_out = assemble(gathered)
    cqn, ckvn = _mla_a_norm(proj, qag, kvag)
    q_pre = _matmul(cqn, w_uq_p, mode="nn", out_dtype=BF16, name="q_up")
    kv_pre = _matmul(ckvn, w_ukv, mode="nn", out_dtype=BF16, name="kv_up")
    q, k, v = _mla_post_fwd(q_pre, kv_pre, misc, cs, sn, gq, gk)
    o_mla, lse = _attn_fwd(q, k, v)
    qkv = _gdn_conv_fwd(proj, conv_w)
    gbeta = _gdn_gates_fwd(misc, alog128, dtb128)
    g_u, g_w, g_qd, g_kd, g_a, g_t, g_gl = _gdn_pre(qkv, gbeta)
    o_gdn, states = _gdn_scan_fwd(g_u, g_w, g_qd, g_kd, g_a, g_gl)
    mixed = _mix_fwd(o_mla, o_gdn, proj, og)
    dy, sq = _out_fwd(mixed, w_out, x, target)

    dmixed = _matmul(dy, w_out, mode="nt", out_dtype=BF16, name="d_mixed")
    d_w_out = _matmul(mixed, dy, mode="tn", out_dtype=F32, name="d_w_out", tk=4096)
    do_mla, do_gdn, dmg, dgg, d_out_gain, delta128 = _mix_bwd(o_mla, o_gdn, proj, og, dmixed)
    s_len = x.shape[0]
    delta_rows = delta128[:, :HEADS].T.reshape(HEADS, 1, s_len)
    dq, dk, dv = _attn_bwd(q, k, v, lse.reshape(HEADS, 1, s_len), delta_rows, do_mla)
    dq_pre, dkv_pre, dkr, d_gq, d_gk = _mla_post_bwd(q_pre, kv_pre, misc, cs, sn, gq, gk, dq, dk, dv)
    d_w_uq_p = _matmul(cqn, dq_pre, mode="tn", out_dtype=F32, name="d_w_uq", tk=1024)
    d_w_ukv = _matmul(ckvn, dkv_pre, mode="tn", out_dtype=F32, name="d_w_ukv", tk=1024)
    dcqn = _matmul(dq_pre, w_uq_p, mode="nt", out_dtype=F32, name="d_cqn")
    dckvn = _matmul(dkv_pre, w_ukv, mode="nt", out_dtype=F32, name="d_ckvn")
    dcq, dckv, d_qag, d_kvag = _mla_a_norm_bwd(proj, qag, kvag, dcqn, dckvn)
    dstates = _gdn_scan_bwd(g_w, g_qd, g_kd, g_a, g_gl, do_gdn)
    dqkv, dgbeta = _gdn_post_bwd(qkv, gbeta, g_u, g_w, g_t, states, dstates, do_gdn)
    dz = _gdn_conv_bwd_a(proj, conv_w, dqkv)
    dgx, d_conv = _gdn_conv_bwd_b(proj, conv_w, dz)
    dmisc, d_alog, d_dtb = _gdn_gates_bwd(misc, alog128, dtb128, gbeta, dgbeta, dkr)
    dproj = jnp.concatenate([dcq, dckv, dmisc, jnp.zeros((x.shape[0], 128), BF16), dmg, dgx, dgg], axis=1)
    d_w_in_p = _matmul(xn, dproj, mode="tn", out_dtype=F32, name="d_w_in", tk=4096)
    big = {"w_in": d_w_in_p, "w_uq": d_w_uq_p, "w_ukv": d_w_ukv, "w_out": d_w_out, "gdn_conv_w": d_conv}
    if scatter_hook is None:
        dxn, received = _matmul(dproj, w_in_p, mode="nt", out_dtype=BF16, name="d_xn", tm=512, tn=512), None
    else:
        dxn, received = _matmul_nt_scatter(dproj, w_in_p, scatter_hook(big), name="d_xn_scatter", tm=1024, tn=512,
                                           out_dtype=BF16)
    grad_x, d_ng = _norm1_bwd(x, ng, dxn, dy)

    small = {"norm_gain": d_ng.sum(0), "mla_q_a_gain": d_qag.sum(0), "mla_kv_a_gain": d_kvag.sum(0),
             "mla_q_norm_gain": d_gq.sum(0)[:QK_DIM], "mla_k_norm_gain": d_gk.sum(0)[:QK_DIM],
             "gdn_a_log": d_alog.sum(0)[GA_LANE:GA_LANE + HEADS], "gdn_dt_bias": d_dtb.sum(0)[GA_LANE:GA_LANE + HEADS],
             "gdn_out_norm_gain": d_out_gain.sum(0)}
    return sq, grad_x, small, big, received


WEIGHTS = ["norm_gain", "w_in", "mla_q_a_gain", "mla_kv_a_gain", "w_uq", "w_ukv", "mla_q_norm_gain", "mla_k_norm_gain",
           "gdn_conv_w", "gdn_a_log", "gdn_dt_bias", "gdn_out_norm_gain", "w_out"]
BIG = ["w_in", "w_uq", "w_ukv", "w_out"]


def kernel(x, positions, norm_gain, w_in, mla_q_a_gain, mla_kv_a_gain, w_uq, w_ukv, mla_q_norm_gain, mla_k_norm_gain, gdn_conv_w, gdn_a_log, gdn_dt_bias, gdn_out_norm_gain, w_out, loss_target, m_norm_gain, m_w_in, m_mla_q_a_gain, m_mla_kv_a_gain, m_w_uq, m_w_ukv, m_mla_q_norm_gain, m_mla_k_norm_gain, m_gdn_conv_w, m_gdn_a_log, m_gdn_dt_bias, m_gdn_out_norm_gain, m_w_out, v_norm_gain, v_w_in, v_mla_q_a_gain, v_mla_kv_a_gain, v_w_uq, v_w_ukv, v_mla_q_norm_gain, v_mla_k_norm_gain, v_gdn_conv_w, v_gdn_a_log, v_gdn_dt_bias, v_gdn_out_norm_gain, v_w_out):
    w = dict(norm_gain=norm_gain, w_in=w_in, mla_q_a_gain=mla_q_a_gain, mla_kv_a_gain=mla_kv_a_gain, w_uq=w_uq,
             w_ukv=w_ukv, mla_q_norm_gain=mla_q_norm_gain, mla_k_norm_gain=mla_k_norm_gain, gdn_conv_w=gdn_conv_w,
             gdn_a_log=gdn_a_log, gdn_dt_bias=gdn_dt_bias, gdn_out_norm_gain=gdn_out_norm_gain, w_out=w_out)
    m = dict(norm_gain=m_norm_gain, w_in=m_w_in, mla_q_a_gain=m_mla_q_a_gain, mla_kv_a_gain=m_mla_kv_a_gain,
             w_uq=m_w_uq, w_ukv=m_w_ukv, mla_q_norm_gain=m_mla_q_norm_gain, mla_k_norm_gain=m_mla_k_norm_gain,
             gdn_conv_w=m_gdn_conv_w, gdn_a_log=m_gdn_a_log, gdn_dt_bias=m_gdn_dt_bias,
             gdn_out_norm_gain=m_gdn_out_norm_gain, w_out=m_w_out)
    v = dict(norm_gain=v_norm_gain, w_in=v_w_in, mla_q_a_gain=v_mla_q_a_gain, mla_kv_a_gain=v_mla_kv_a_gain,
             w_uq=v_w_uq, w_ukv=v_w_ukv, mla_q_norm_gain=v_mla_q_norm_gain, mla_k_norm_gain=v_mla_k_norm_gain,
             gdn_conv_w=v_gdn_conv_w, gdn_a_log=v_gdn_a_log, gdn_dt_bias=v_gdn_dt_bias,
             gdn_out_norm_gain=v_gdn_out_norm_gain, w_out=v_w_out)
    k_me = 2 * lax.axis_index("x") + lax.axis_index("y")

    first_weights = ([w_in[0].astype(BF16), gdn_conv_w[0]], lambda g: (_pad_w_in_blocks(g[0]), _blocks_to_cols(g[1])))
    late_weights = ([w_uq[0].astype(BF16), w_ukv[0].astype(BF16), w_out[0].astype(BF16)],
                    lambda g: (_pad_heads(_blocks_to_cols(g[0])), _blocks_to_cols(g[1]), g[2].reshape(D_MODEL, D_MODEL)))

    pair_sums = []

    def scatter_hook(big):
        pair, pair_bf16 = _rs_pair_stage([
            _w_in_grad_blocks(big["w_in"]), _cols_to_blocks(_unpad_heads(big["w_uq"])),
            _cols_to_blocks(big["w_ukv"]), big["w_out"].reshape(4, 512, D_MODEL)])
        pair_sums.extend(pair)
        return pair_bf16

    sq, grad_x, small, big, received = _local_step(
        x[0], positions[0], loss_target[0], norm_gain, None, mla_q_a_gain, mla_kv_a_gain, None, None,
        mla_q_norm_gain, mla_k_norm_gain, None, gdn_a_log, gdn_dt_bias, gdn_out_norm_gain, None, scatter_hook,
        late_weights, first_weights)

    loss_local = (0.5 / D_MODEL) * jnp.sum(sq)
    pack = jnp.concatenate([_pack_small(small, loss_local), big["gdn_conv_w"].reshape(96, 128)], axis=0)
    tot = _all_reduce_small(pack)
    loss = tot[LOSS_ROW, 0]
    conv_grad = lax.dynamic_slice_in_dim(tot[CONV_ROW:].reshape(4, 3072), k_me * 768, 768, axis=1)

    shard_grads = _rs_chip_stage(pair_sums, received)

    grads = {n: _unpack_small(tot, n) for n in SMALL_ROWS}
    grads["gdn_conv_w"] = conv_grad[None]
    for n, g in zip(BIG, shard_grads):
        grads[n] = g[None]

    delta, new_m, new_v = {}, {}, {}
    sw = _pack_small({n: w[n] for n in SMALL_ROWS})
    sm = _pack_small({n: m[n] for n in SMALL_ROWS})
    sv = _pack_small({n: v[n] for n in SMALL_ROWS})
    sd, snm, snv = _adamw(sw, tot[:SMALL_PACK_ROWS], sm, sv, "adamw_small")
    for n in SMALL_ROWS:
        delta[n], new_m[n], new_v[n] = _unpack_small(sd, n), _unpack_small(snm, n), _unpack_small(snv, n)
    for n in BIG + ["gdn_conv_w"]:
        if n == "w_in":
            d, nm, nv = _adamw(w[n][0].T, grads[n][0].T, m[n][0].T, v[n][0].T, f"adamw_{n}")
            delta[n], new_m[n], new_v[n] = d.T[None], nm.T[None], nv.T[None]
        else:
            d, nm, nv = _adamw(w[n][0], grads[n][0], m[n][0], v[n][0], f"adamw_{n}")
            delta[n], new_m[n], new_v[n] = d[None], nm[None], nv[None]

    return (loss, grad_x[None], *[grads[n] for n in WEIGHTS], *[delta[n] for n in WEIGHTS],
            *[new_m[n] for n in WEIGHTS], *[new_v[n] for n in WEIGHTS])
```

```python
import functools
import math

import jax
import jax.numpy as jnp
from jax import lax
from jax.experimental import pallas as pl
from jax.experimental.pallas import tpu as pltpu

F32 = jnp.float32
BF16 = jnp.bfloat16
MESH = pl.DeviceIdType.MESH

D_MODEL = 2048
HEADS = 8
HEAD_DIM = 128
QK_DIM = 192
QK_PAD = 256
HALF_ROPE = 32
CHUNK = 64
NORM_EPS = 1e-6
W_IN_COLS = 5968
W_IN_PAD = 6144
GA_LANE = 64
GB_LANE = 72
ADAM_LR, ADAM_B1, ADAM_B2, ADAM_EPS, ADAM_WD, ADAM_STEP = 0.001, 0.9, 0.999, 1e-08, 0.01, 10
VMEM_LIMIT_V7X = 52 * 1024 * 1024
HI = lax.Precision.HIGHEST
NN = (((1,), (0,)), ((), ()))
NT = (((1,), (1,)), ((), ()))
TN = (((0,), (0,)), ((), ()))

TILES = {"row": 512, "attn": 2048, "mm": 1024}


def _call(body, *, name, grid, in_specs, out_specs, out_shape, args, scratch=(), sem=None):
    def kfn(*refs):
        body(*refs)
    if sem is None:
        sem = ("arbitrary",) * len(grid)
    return pl.pallas_call(
        kfn, name=name, grid=grid, in_specs=in_specs, out_specs=out_specs, out_shape=out_shape,
        scratch_shapes=list(scratch),
        compiler_params=pltpu.CompilerParams(dimension_semantics=sem, vmem_limit_bytes=VMEM_LIMIT_V7X),
    )(*args)


def _rows(tm, w, cb=0):
    return pl.BlockSpec((tm, w), lambda i: (i, cb))


def _full(shape):
    n = len(shape)
    return pl.BlockSpec(shape, lambda *_: (0,) * n)


def _sds(shape, dtype):
    return jax.ShapeDtypeStruct(shape, dtype)


def _acc8(x):
    tm, c = x.shape
    return jnp.sum(x.reshape(tm // 8, 8, c), axis=0)


def _sigmoid(x):
    return 1.0 / (1.0 + jnp.exp(-x))


def _silu(x):
    return x * _sigmoid(x)


def _dsilu(x):
    s = _sigmoid(x)
    return s * (1.0 + x * (1.0 - s))


def _dot(a, b, dims=NN):
    return lax.dot_general(a.astype(BF16), b.astype(BF16), dims, preferred_element_type=F32)


def _dot_hi(a, b, dims=NN):
    return lax.dot_general(a, b, dims, precision=HI, preferred_element_type=F32)


def _matmul(a, b, *, mode, out_dtype, name, tm=None, tn=None, tk=None):
    if mode == "tn":
        kdim, m = a.shape
    else:
        m, kdim = a.shape
    n = b.shape[0] if mode == "nt" else b.shape[1]
    tm = min(tm or TILES["mm"], m)
    tn = min(tn or TILES["mm"], n)
    tk = min(tk or kdim, kdim)
    nk = kdim // tk
    dims = {"nn": NN, "nt": NT, "tn": TN}[mode]
    if mode == "tn":
        a_spec = pl.BlockSpec((tk, tm), lambda i, j, k: (k, i))
    else:
        a_spec = pl.BlockSpec((tm, tk), lambda i, j, k: (i, k))
    if mode == "nt":
        b_spec = pl.BlockSpec((tn, tk), lambda i, j, k: (j, k))
    else:
        b_spec = pl.BlockSpec((tk, tn), lambda i, j, k: (k, j))

    def body(a_ref, b_ref, o_ref):
        r = _dot(a_ref[...], b_ref[...], dims)
        if nk == 1:
            o_ref[...] = r.astype(o_ref.dtype)
        else:
            k = pl.program_id(2)

            @pl.when(k == 0)
            def _():
                o_ref[...] = r

            @pl.when(k > 0)
            def _():
                o_ref[...] += r

    if nk > 1:
        assert out_dtype == F32
    return _call(body, name=name, grid=(m // tm, n // tn, nk), in_specs=[a_spec, b_spec],
                 out_specs=pl.BlockSpec((tm, tn), lambda i, j, k: (i, j)), out_shape=_sds((m, n), out_dtype),
                 args=(a, b))


def _norm1_fwd(x, gain):
    s = x.shape[0]
    tm = min(TILES["row"], s)

    def body(x_ref, g_ref, o_ref):
        xv = x_ref[...]
        r = lax.rsqrt(jnp.mean(xv * xv, axis=-1, keepdims=True) + NORM_EPS)
        o_ref[...] = (xv * r * g_ref[...]).astype(BF16)

    return _call(body, name="norm1_fwd", grid=(s // tm,), in_specs=[_rows(tm, D_MODEL), _full((1, D_MODEL))],
                 out_specs=_rows(tm, D_MODEL), out_shape=_sds((s, D_MODEL), BF16), args=(x, gain))


def _norm1_bwd(x, gain, dxn, dy):
    s = x.shape[0]
    tm = min(TILES["row"], s)

    def body(x_ref, g_ref, dxn_ref, dy_ref, gx_ref, dg_ref):
        xv = x_ref[...]
        r = lax.rsqrt(jnp.mean(xv * xv, axis=-1, keepdims=True) + NORM_EPS)
        nrm = xv * r
        d = dxn_ref[...].astype(F32)
        dn = d * g_ref[...]
        gx_ref[...] = dy_ref[...].astype(F32) + r * (dn - nrm * jnp.mean(dn * nrm, axis=-1, keepdims=True))

        @pl.when(pl.program_id(0) == 0)
        def _():
            dg_ref[...] = jnp.zeros_like(dg_ref)

        dg_ref[...] += _acc8(d * nrm)

    return _call(body, name="norm1_bwd", grid=(s // tm,),
                 in_specs=[_rows(tm, D_MODEL), _full((1, D_MODEL)), _rows(tm, D_MODEL), _rows(tm, D_MODEL)],
                 out_specs=[_rows(tm, D_MODEL), _full((8, D_MODEL))],
                 out_shape=[_sds((s, D_MODEL), F32), _sds((8, D_MODEL), F32)], args=(x, gain, dxn, dy))


def _rms(xv, width):
    return lax.rsqrt(jnp.sum(xv * xv, axis=-1, keepdims=True) * (1.0 / width) + NORM_EPS)


def _mla_a_norm(proj, gq, gkv):
    s = proj.shape[0]
    tm = min(TILES["row"], s)

    def body(cq_ref, ckv_ref, gq_ref, gkv_ref, oq_ref, okv_ref):
        a = cq_ref[...].astype(F32)
        oq_ref[...] = (a * _rms(a, 512) * gq_ref[...]).astype(BF16)
        b = ckv_ref[...].astype(F32)
        okv_ref[...] = (b * _rms(b, 256) * gkv_ref[...]).astype(BF16)

    return _call(body, name="mla_a_norm", grid=(s // tm,),
                 in_specs=[_rows(tm, 512, 0), _rows(tm, 256, 2), _full((1, 512)), _full((1, 256))],
                 out_specs=[_rows(tm, 512), _rows(tm, 256)],
                 out_shape=[_sds((s, 512), BF16), _sds((s, 256), BF16)], args=(proj, proj, gq, gkv))


def _rms_bwd(xv, gain, d, width):
    r = _rms(xv, width)
    nrm = xv * r
    dn = d * gain
    dx = r * (dn - nrm * (jnp.sum(dn * nrm, axis=-1, keepdims=True) * (1.0 / width)))
    return dx, d * nrm


def _mla_a_norm_bwd(proj, gq, gkv, dcqn, dckvn):
    s = proj.shape[0]
    tm = min(TILES["row"], s)

    def body(cq_ref, ckv_ref, gq_ref, gkv_ref, dq_ref, dkv_ref, oq_ref, okv_ref, aq_ref, akv_ref):
        dxq, gq_part = _rms_bwd(cq_ref[...].astype(F32), gq_ref[...], dq_ref[...].astype(F32), 512)
        dxk, gk_part = _rms_bwd(ckv_ref[...].astype(F32), gkv_ref[...], dkv_ref[...].astype(F32), 256)
        oq_ref[...] = dxq.astype(BF16)
        okv_ref[...] = dxk.astype(BF16)

        @pl.when(pl.program_id(0) == 0)
        def _():
            aq_ref[...] = jnp.zeros_like(aq_ref)
            akv_ref[...] = jnp.zeros_like(akv_ref)

        aq_ref[...] += _acc8(gq_part)
        akv_ref[...] += _acc8(gk_part)

    return _call(body, name="mla_a_norm_bwd", grid=(s // tm,),
                 in_specs=[_rows(tm, 512, 0), _rows(tm, 256, 2), _full((1, 512)), _full((1, 256)),
                           _rows(tm, 512), _rows(tm, 256)],
                 out_specs=[_rows(tm, 512), _rows(tm, 256), _full((8, 512)), _full((8, 256))],
                 out_shape=[_sds((s, 512), BF16), _sds((s, 256), BF16), _sds((8, 512), F32), _sds((8, 256), F32)],
                 args=(proj, proj, gq, gkv, dcqn, dckvn))


def _swap32(r):
    lane = lax.broadcasted_iota(jnp.int32, r.shape, 1)
    return jnp.where(lane < HALF_ROPE, pltpu.roll(r, 128 - HALF_ROPE, 1), pltpu.roll(r, HALF_ROPE, 1))


def _mla_post_fwd(q_pre, kv_pre, proj, cs, sn, gq, gk):
    s = q_pre.shape[0]
    tm = min(TILES["row"], s)

    def body(qp_ref, kvp_ref, misc_ref, cs_ref, sn_ref, gq_ref, gk_ref, q_ref, k_ref, v_ref):
        csv, snv = cs_ref[...], sn_ref[...]
        lane = lax.broadcasted_iota(jnp.int32, (tm, 128), 1)
        kr = jnp.where(lane < 64, misc_ref[...], 0.0)
        for h in range(HEADS):
            for src, g_ref, o_ref in ((None, gq_ref, q_ref), (kr, gk_ref, k_ref)):
                if src is None:
                    xv = qp_ref[:, QK_PAD * h:QK_PAD * (h + 1)].astype(F32)
                else:
                    xv = jnp.concatenate([kvp_ref[:, 256 * h:256 * h + 128].astype(F32), src], axis=-1)
                y = xv * _rms(xv, QK_DIM) * g_ref[...]
                if src is None:
                    y = y * Q_PRESCALE
                hi = y[:, 128:]
                hi = hi * csv + _swap32(hi) * snv
                o_ref[:, QK_PAD * h:QK_PAD * h + 128] = y[:, :128].astype(BF16)
                o_ref[:, QK_PAD * h + 128:QK_PAD * (h + 1)] = hi.astype(BF16)
            v_ref[:, 128 * h:128 * (h + 1)] = kvp_ref[:, 256 * h + 128:256 * (h + 1)].astype(BF16)

    return _call(body, name="mla_post_fwd", grid=(s // tm,),
                 in_specs=[_rows(tm, 2048), _rows(tm, 2048), _rows(tm, 128), _rows(tm, 128), _rows(tm, 128),
                           _full((1, QK_PAD)), _full((1, QK_PAD))],
                 out_specs=[_rows(tm, 2048), _rows(tm, 2048), _rows(tm, 1024)],
                 out_shape=[_sds((s, 2048), BF16), _sds((s, 2048), BF16), _sds((s, 1024), BF16)],
                 args=(q_pre, kv_pre, proj, cs, sn, gq, gk))


def _mla_post_bwd(q_pre, kv_pre, proj, cs, sn, gq, gk, dq, dk, dv):
    s = q_pre.shape[0]
    tm = min(TILES["row"], s)

    def body(qp_ref, kvp_ref, misc_ref, cs_ref, sn_ref, gq_ref, gk_ref, dq_ref, dk_ref, dv_ref,
             oq_ref, okv_ref, okr_ref, agq_ref, agk_ref):
        csv, snv = cs_ref[...], sn_ref[...]
        lane = lax.broadcasted_iota(jnp.int32, (tm, 128), 1)
        kr = jnp.where(lane < 64, misc_ref[...], 0.0)

        @pl.when(pl.program_id(0) == 0)
        def _():
            agq_ref[...] = jnp.zeros_like(agq_ref)
            agk_ref[...] = jnp.zeros_like(agk_ref)

        dkr = jnp.zeros((tm, 128), F32)
        for h in range(HEADS):
            for which in (0, 1):
                if which == 0:
                    xv = qp_ref[:, QK_PAD * h:QK_PAD * (h + 1)].astype(F32)
                    d_ref, g_ref, a_ref = dq_ref, gq_ref, agq_ref
                else:
                    xv = jnp.concatenate([kvp_ref[:, 256 * h:256 * h + 128].astype(F32), kr], axis=-1)
                    d_ref, g_ref, a_ref = dk_ref, gk_ref, agk_ref
                dhi = d_ref[:, QK_PAD * h + 128:QK_PAD * (h + 1)].astype(F32)
                dhi = dhi * csv - _swap32(dhi) * snv
                dyv = jnp.concatenate([d_ref[:, QK_PAD * h:QK_PAD * h + 128].astype(F32), dhi], axis=-1)
                if which == 0:
                    dyv = dyv * ATTN_SCALE
                dx, gpart = _rms_bwd(xv, g_ref[...], dyv, QK_DIM)
                a_ref[...] += _acc8(gpart)
                if which == 0:
                    oq_ref[:, QK_PAD * h:QK_PAD * (h + 1)] = dx.astype(BF16)
                else:
                    okv_ref[:, 256 * h:256 * h + 128] = dx[:, :128].astype(BF16)
                    dkr = dkr + dx[:, 128:]
            okv_ref[:, 256 * h + 128:256 * (h + 1)] = dv_ref[:, 128 * h:128 * (h + 1)].astype(BF16)
        okr_ref[...] = dkr

    return _call(body, name="mla_post_bwd", grid=(s // tm,),
                 in_specs=[_rows(tm, 2048), _rows(tm, 2048), _rows(tm, 128), _rows(tm, 128), _rows(tm, 128),
                           _full((1, QK_PAD)), _full((1, QK_PAD)), _rows(tm, 2048), _rows(tm, 2048), _rows(tm, 1024)],
                 out_specs=[_rows(tm, 2048), _rows(tm, 2048), _rows(tm, 128), _full((8, QK_PAD)), _full((8, QK_PAD))],
                 out_shape=[_sds((s, 2048), BF16), _sds((s, 2048), BF16), _sds((s, 128), F32),
                            _sds((8, QK_PAD), F32), _sds((8, QK_PAD), F32)],
                 args=(q_pre, kv_pre, proj, cs, sn, gq, gk, dq, dk, dv))


ATTN_SCALE = QK_DIM ** -0.5
NEG = -1e30


LOG2E = 1.4426950408889634
LN2 = 0.6931471805599453
Q_PRESCALE = ATTN_SCALE * LOG2E
ATTN_SUB_FWD = 512
ATTN_SUB_BWD = 256


def _causal_pairs(nq, kv_major):
    prs = [(i, j) for i in range(nq) for j in range(i + 1)]
    if kv_major:
        prs.sort(key=lambda ij: (ij[1], ij[0]))
    return (jnp.asarray([p[0] for p in prs], jnp.int32), jnp.asarray([p[1] for p in prs], jnp.int32))


def _pair_call(body, *, name, tables, in_specs, out_specs, out_shape, scratch, args):
    def kfn(*refs):
        body(*refs)
    spec = pltpu.PrefetchScalarGridSpec(num_scalar_prefetch=2, grid=(HEADS, tables[0].shape[0]), in_specs=in_specs,
                                        out_specs=out_specs, scratch_shapes=list(scratch))
    return pl.pallas_call(
        kfn, name=name, grid_spec=spec, out_shape=out_shape,
        compiler_params=pltpu.CompilerParams(dimension_semantics=("parallel", "arbitrary"),
                                             vmem_limit_bytes=VMEM_LIMIT_V7X))(*tables, *args)


def _diag_mask(sc, ts, qs):
    row = lax.broadcasted_iota(jnp.int32, sc.shape, 0) + qs * ts
    col = lax.broadcasted_iota(jnp.int32, sc.shape, 1)
    return jnp.where(col <= row, sc, NEG)


def _attn_fwd(q, k, v):
    s = q.shape[0]
    t = min(TILES["attn"], s)
    ts = min(ATTN_SUB_FWD, t)
    nq = s // t

    def slabs(q_ref, k_ref, v_ref, m_s, l_s, acc_s, diag):
        def scores(qs):
            kw = (qs + 1) * ts if diag else t
            sc = lax.dot_general(q_ref[qs * ts:(qs + 1) * ts, :], k_ref[0:kw, :], NT, preferred_element_type=F32)
            return _diag_mask(sc, ts, qs) if diag else sc

        nsub = t // ts
        sc_next = scores(0)
        for qs in range(nsub):
            rq = slice(qs * ts, (qs + 1) * ts)
            kw = (qs + 1) * ts if diag else t
            sc = sc_next
            if qs + 1 < nsub:
                sc_next = scores(qs + 1)
            m_prev = m_s[rq, :]
            m_new = jnp.maximum(m_prev, jnp.max(sc, axis=-1, keepdims=True))
            p = jnp.exp2(sc - m_new)
            alpha = jnp.exp2(m_prev - m_new)
            l_s[rq, :] = alpha * l_s[rq, :] + jnp.sum(p, axis=-1, keepdims=True)
            acc_s[rq, :] = acc_s[rq, :] * alpha + lax.dot_general(p.astype(BF16), v_ref[0:kw, :], NN,
                                                                  preferred_element_type=F32)
            m_s[rq, :] = m_new

    def body(it_ref, jt_ref, q_ref, k_ref, v_ref, o_ref, lse_ref, m_s, l_s, acc_s):
        p = pl.program_id(1)
        i, j = it_ref[p], jt_ref[p]

        @pl.when(j == 0)
        def _():
            m_s[...] = jnp.full_like(m_s, NEG)
            l_s[...] = jnp.zeros_like(l_s)
            acc_s[...] = jnp.zeros_like(acc_s)

        @pl.when(j < i)
        def _():
            slabs(q_ref, k_ref, v_ref, m_s, l_s, acc_s, False)

        @pl.when(j == i)
        def _():
            slabs(q_ref, k_ref, v_ref, m_s, l_s, acc_s, True)
            o_ref[...] = acc_s[...] / l_s[...]
            lse_ref[...] = m_s[...] + jnp.log2(l_s[...])

    qb = lambda h, p, it, jt: (it[p], h)
    kb = lambda h, p, it, jt: (jt[p], h)
    return _pair_call(
        body, name="attn_fwd", tables=_causal_pairs(nq, kv_major=False),
        in_specs=[pl.BlockSpec((t, QK_PAD), qb), pl.BlockSpec((t, QK_PAD), kb), pl.BlockSpec((t, HEAD_DIM), kb)],
        out_specs=[pl.BlockSpec((t, HEAD_DIM), qb),
                   pl.BlockSpec((None, t, 1), lambda h, p, it, jt: (h, it[p], 0))],
        out_shape=[_sds((s, HEADS * HEAD_DIM), F32), _sds((HEADS, s, 1), F32)],
        scratch=[pltpu.VMEM((t, 1), F32), pltpu.VMEM((t, 1), F32), pltpu.VMEM((t, HEAD_DIM), F32)],
        args=(q, k, v))


def _attn_bwd(q, k, v, lse_rows, delta_rows, do):
    s = q.shape[0]
    t = min(TILES["attn"], s)
    ts = min(ATTN_SUB_BWD, t)
    nq = s // t

    def slabs(q_ref, k_ref, v_ref, lse_ref, delta_ref, do_ref, dq_ref, dk_ref, dv_ref, i, diag):
        def products(qs):
            rq = slice(qs * ts, (qs + 1) * ts)
            kw = (qs + 1) * ts if diag else t
            qv, dob = q_ref[rq, :], do_ref[rq, :]
            sct = lax.dot_general(k_ref[0:kw, :], qv, NT, preferred_element_type=F32)
            dpt = lax.dot_general(v_ref[0:kw, :], dob, NT, preferred_element_type=F32)
            if diag:
                row = lax.broadcasted_iota(jnp.int32, sct.shape, 0)
                col = lax.broadcasted_iota(jnp.int32, sct.shape, 1) + qs * ts
                sct = jnp.where(row <= col, sct, NEG)
            return qv, dob, sct, dpt

        nsub = t // ts
        ahead = products(0)
        for qs in range(nsub):
            rq = slice(qs * ts, (qs + 1) * ts)
            kw = (qs + 1) * ts if diag else t
            qv, dob, sct, dpt = ahead
            if qs + 1 < nsub:
                ahead = products(qs + 1)
            pt = jnp.exp2(sct - lse_ref[:, rq])
            dv_ref[0:kw, :] += lax.dot_general(pt.astype(BF16), dob, NN, preferred_element_type=F32)
            dst = (pt * (dpt - delta_ref[:, rq])).astype(BF16)
            dk_ref[0:kw, :] += lax.dot_general(dst, qv, NN, preferred_element_type=F32)
            rows = pl.ds(pl.multiple_of(i * t + qs * ts, ts), ts)
            dq_ref[rows, :] += lax.dot_general(dst, k_ref[0:kw, :], TN, preferred_element_type=F32)

    npairs = nq * (nq + 1) // 2

    def body(it_ref, jt_ref, q_ref, k_ref, v_ref, lse_ref, delta_ref, do_ref, dq_ref, dk_ref, dv_ref,
             dq_acc, dk_acc, dv_acc):
        p = pl.program_id(1)
        i, j = it_ref[p], jt_ref[p]
        refs = (q_ref, k_ref, v_ref, lse_ref, delta_ref, do_ref, dq_acc, dk_acc, dv_acc)

        @pl.when(p == 0)
        def _():
            dq_acc[...] = jnp.zeros_like(dq_acc)

        @pl.when(i == j)
        def _():
            dk_acc[...] = jnp.zeros_like(dk_acc)
            dv_acc[...] = jnp.zeros_like(dv_acc)

        @pl.when(i > j)
        def _():
            slabs(*refs, i, False)

        @pl.when(i == j)
        def _():
            slabs(*refs, i, True)

        @pl.when(i == nq - 1)
        def _():
            dk_ref[...] = (dk_acc[...] * LN2).astype(BF16)
            dv_ref[...] = dv_acc[...].astype(BF16)

        @pl.when(p == npairs - 1)
        def _():
            dq_ref[...] = dq_acc[...].astype(BF16)

    qb = lambda h, p, it, jt: (it[p], h)
    kb = lambda h, p, it, jt: (jt[p], h)
    rowb = pl.BlockSpec((None, 1, t), lambda h, p, it, jt: (h, 0, it[p]))
    return _pair_call(
        body, name="attn_bwd", tables=_causal_pairs(nq, kv_major=True),
        in_specs=[pl.BlockSpec((t, QK_PAD), qb), pl.BlockSpec((t, QK_PAD), kb), pl.BlockSpec((t, HEAD_DIM), kb),
                  rowb, rowb, pl.BlockSpec((t, HEAD_DIM), qb)],
        out_specs=[pl.BlockSpec((s, QK_PAD), lambda h, p, it, jt: (0, h)), pl.BlockSpec((t, QK_PAD), kb),
                   pl.BlockSpec((t, HEAD_DIM), kb)],
        out_shape=[_sds((s, HEADS * QK_PAD), BF16), _sds((s, HEADS * QK_PAD), BF16), _sds((s, HEADS * HEAD_DIM), BF16)],
        scratch=[pltpu.VMEM((s, QK_PAD), F32), pltpu.VMEM((t, QK_PAD), F32), pltpu.VMEM((t, HEAD_DIM), F32)],
        args=(q, k, v, lse_rows, delta_rows, do))


GDN_Q_SCALE = HEAD_DIM ** -0.5


def _shift_down(xv, prev8, sft):
    rolled = pltpu.roll(xv, sft, 0)
    top = pltpu.roll(jnp.concatenate([prev8, xv[:8]], axis=0), sft, 0)[8:]
    return jnp.concatenate([top, rolled[8:]], axis=0)


def _shift_up(xv, next8, sft):
    tm = xv.shape[0]
    rolled = pltpu.roll(xv, tm - sft, 0)
    bot = pltpu.roll(jnp.concatenate([xv[tm - 8:], next8], axis=0), 16 - sft, 0)[:8]
    return jnp.concatenate([rolled[:tm - 8], bot], axis=0)


def _conv_z(xv, prev8, w_ref):
    z = xv * w_ref[3:4, :]
    for sft in (1, 2, 3):
        z = z + _shift_down(xv, prev8, sft) * w_ref[3 - sft:4 - sft, :]
    return z


def _conv_specs(s, tm):
    nb16 = tm // 16
    cur = pl.BlockSpec((tm, 1024), lambda j, i: (i, 2 + j))
    prev = pl.BlockSpec((16, 1024), lambda j, i: (jnp.maximum(i * nb16 - 1, 0), 2 + j))
    return cur, prev


def _prev8(xp_ref, i):
    return jnp.where(i > 0, xp_ref[...].astype(F32)[8:], 0.0)


def _gdn_conv_fwd(proj, conv_w):
    s = proj.shape[0]
    tm = min(TILES["row"], s)
    cur, prev = _conv_specs(s, tm)

    def body(x_ref, xp_ref, w_ref, o_ref):
        j, i = pl.program_id(0), pl.program_id(1)
        a = _silu(_conv_z(x_ref[...].astype(F32), _prev8(xp_ref, i), w_ref))
        qk_scale = jnp.where(j == 0, GDN_Q_SCALE, 1.0)
        for h in range(HEADS):
            seg = a[:, 128 * h:128 * (h + 1)]
            r = lax.rsqrt(jnp.sum(seg * seg, axis=-1, keepdims=True) + NORM_EPS)
            o_ref[:, 128 * h:128 * (h + 1)] = jnp.where(j < 2, seg * r * qk_scale, seg).astype(BF16)

    return _call(body, name="gdn_conv_fwd", grid=(3, s // tm),
                 in_specs=[cur, prev, pl.BlockSpec((4, 1024), lambda j, i: (0, j))],
                 out_specs=pl.BlockSpec((tm, 1024), lambda j, i: (i, j)), out_shape=_sds((s, 3072), BF16),
                 args=(proj, proj, conv_w))


def _gdn_conv_bwd_a(proj, conv_w, dqkv):
    s = proj.shape[0]
    tm = min(TILES["row"], s)
    cur, prev = _conv_specs(s, tm)

    def body(x_ref, xp_ref, w_ref, d_ref, o_ref):
        j, i = pl.program_id(0), pl.program_id(1)
        z = _conv_z(x_ref[...].astype(F32), _prev8(xp_ref, i), w_ref)
        a = _silu(z)
        dsl = _dsilu(z)
        qk_scale = jnp.where(j == 0, GDN_Q_SCALE, 1.0)
        for h in range(HEADS):
            sl = slice(128 * h, 128 * (h + 1))
            seg = a[:, sl]
            dyv = d_ref[:, sl].astype(F32)
            r = lax.rsqrt(jnp.sum(seg * seg, axis=-1, keepdims=True) + NORM_EPS)
            yh = seg * r
            da_n = qk_scale * r * (dyv - yh * jnp.sum(yh * dyv, axis=-1, keepdims=True))
            o_ref[:, sl] = (jnp.where(j < 2, da_n, dyv) * dsl[:, sl]).astype(BF16)

    return _call(body, name="gdn_conv_bwd_a", grid=(3, s // tm),
                 in_specs=[cur, prev, pl.BlockSpec((4, 1024), lambda j, i: (0, j)),
                           pl.BlockSpec((tm, 1024), lambda j, i: (i, j))],
                 out_specs=pl.BlockSpec((tm, 1024), lambda j, i: (i, j)), out_shape=_sds((s, 3072), BF16),
                 args=(proj, proj, conv_w, dqkv))


def _gdn_conv_bwd_b(proj, conv_w, dz):
    s = proj.shape[0]
    tm = min(TILES["row"], s)
    nb16 = tm // 16
    last16 = s // 16 - 1
    cur, prev = _conv_specs(s, tm)

    def body(x_ref, w_ref, dz_ref, dzn_ref, dx_ref, dw_ref):
        i = pl.program_id(1)
        next8 = jnp.where(i < pl.num_programs(1) - 1, dzn_ref[...].astype(F32)[:8], 0.0)
        xv, dzv = x_ref[...].astype(F32), dz_ref[...].astype(F32)

        @pl.when(i == 0)
        def _():
            dw_ref[...] = jnp.zeros_like(dw_ref)

        dx = dzv * w_ref[3:4, :]
        dw_ref[3:4, :] += jnp.sum(dzv * xv, axis=0, keepdims=True)
        for sft in (1, 2, 3):
            up = _shift_up(dzv, next8, sft)
            dx = dx + up * w_ref[3 - sft:4 - sft, :]
            dw_ref[3 - sft:4 - sft, :] += jnp.sum(up * xv, axis=0, keepdims=True)
        dx_ref[...] = dx.astype(BF16)

    return _call(body, name="gdn_conv_bwd_b", grid=(3, s // tm),
                 in_specs=[cur, pl.BlockSpec((4, 1024), lambda j, i: (0, j)),
                           pl.BlockSpec((tm, 1024), lambda j, i: (i, j)),
                           pl.BlockSpec((16, 1024), lambda j, i: (jnp.minimum((i + 1) * nb16, last16), j))],
                 out_specs=[pl.BlockSpec((tm, 1024), lambda j, i: (i, j)), pl.BlockSpec((4, 1024), lambda j, i: (0, j))],
                 out_shape=[_sds((s, 3072), BF16), _sds((4, 3072), F32)], args=(proj, conv_w, dz, dz))


def _softplus(xv):
    return jnp.maximum(xv, 0.0) + jnp.log(1.0 + jnp.exp(-jnp.abs(xv)))


def _gdn_gates_fwd(proj, alog128, dtb128):
    s = proj.shape[0]
    tm = min(TILES["row"], s)

    def body(m_ref, a_ref, b_ref, o_ref):
        mv = m_ref[...]
        lane = lax.broadcasted_iota(jnp.int32, mv.shape, 1)
        g = -jnp.exp(a_ref[...]) * _softplus(mv + b_ref[...])
        is_g = (lane >= GA_LANE) & (lane < GA_LANE + HEADS)
        is_b = (lane >= GB_LANE) & (lane < GB_LANE + HEADS)
        o_ref[...] = jnp.where(is_g, g, jnp.where(is_b, _sigmoid(mv), 0.0))

    return _call(body, name="gdn_gates_fwd", grid=(s // tm,),
                 in_specs=[_rows(tm, 128), _full((1, 128)), _full((1, 128))],
                 out_specs=_rows(tm, 128), out_shape=_sds((s, 128), F32), args=(proj, alog128, dtb128))


def _gdn_gates_bwd(proj, alog128, dtb128, gbeta, dgbeta, dkr):
    s = proj.shape[0]
    tm = min(TILES["row"], s)

    def body(m_ref, a_ref, b_ref, gb_ref, d_ref, kr_ref, o_ref, da_ref, db_ref):
        mv, dv = m_ref[...], d_ref[...]
        lane = lax.broadcasted_iota(jnp.int32, mv.shape, 1)
        is_g = (lane >= GA_LANE) & (lane < GA_LANE + HEADS)
        is_b = (lane >= GB_LANE) & (lane < GB_LANE + HEADS)
        dga = jnp.where(is_g, dv * (-jnp.exp(a_ref[...])) * _sigmoid(mv + b_ref[...]), 0.0)
        beta = gb_ref[...]
        dgb = jnp.where(is_b, dv * beta * (1.0 - beta), 0.0)
        o_ref[...] = jnp.where(lane < 64, kr_ref[...], dga + dgb).astype(BF16)

        @pl.when(pl.program_id(0) == 0)
        def _():
            da_ref[...] = jnp.zeros_like(da_ref)
            db_ref[...] = jnp.zeros_like(db_ref)

        da_ref[...] += _acc8(jnp.where(is_g, dv * gb_ref[...], 0.0))
        db_ref[...] += _acc8(dga)

    return _call(body, name="gdn_gates_bwd", grid=(s // tm,),
                 in_specs=[_rows(tm, 128), _full((1, 128)), _full((1, 128)), _rows(tm, 128), _rows(tm, 128),
                           _rows(tm, 128)],
                 out_specs=[_rows(tm, 128), _full((8, 128)), _full((8, 128))],
                 out_shape=[_sds((s, 128), BF16), _sds((8, 128), F32), _sds((8, 128), F32)],
                 args=(proj, alog128, dtb128, gbeta, dgbeta, dkr))


def _col(mat, lane_idx, lane):
    return jnp.sum(jnp.where(lane == lane_idx, mat, 0.0), axis=-1, keepdims=True)


def _chunk_local(qh, kh, vh, gcol, bcol, ii, jj):
    lower, strict, eye = ii >= jj, ii > jj, ii == jj
    grow = jnp.sum(jnp.where(eye, gcol, 0.0), axis=0, keepdims=True)
    decay = jnp.where(lower, jnp.exp(jnp.where(lower, gcol - grow, 0.0)), 0.0)
    kb = kh * bcol
    vb = vh * bcol
    mm = _dot(kb, kh, NT)
    lmat = jnp.where(strict, mm * decay, 0.0)
    pw = -lmat
    tinv = jnp.where(eye, 1.0, 0.0) + pw
    for _ in range(5):
        pw = _dot_hi(pw, pw)
        tinv = tinv + _dot_hi(tinv, pw)
    egc = jnp.exp(gcol)
    kbg = kb * egc
    rhs = jnp.concatenate([vb, kbg], axis=-1)
    sol = _dot_hi(tinv, rhs)
    qk = _dot(qh, kh, NT)
    glast = jnp.sum(jnp.where(ii[:, :1] == CHUNK - 1, gcol, 0.0), axis=0, keepdims=True)
    ekd = jnp.exp(glast - gcol)
    return dict(decay=decay, kb=kb, vb=vb, mm=mm, lmat=lmat, tinv=tinv, egc=egc, kbg=kbg, rhs=rhs,
                u=sol[:, :HEAD_DIM], w=sol[:, HEAD_DIM:], qk=qk, amat=qk * decay, qd=qh * egc, ekd=ekd,
                kd=kh * ekd, gl=jnp.exp(glast), strict=strict, lower=lower, eye=eye)


def _tri(ii, jj):
    return jnp.where(ii >= jj, 1.0, 0.0)


def _gdn_fwd(qkv, gbeta):
    s = qkv.shape[0]
    n = s // CHUNK

    def body(qkv_ref, gb_ref, o_ref, st_ref, state):
        @pl.when(pl.program_id(0) == 0)
        def _():
            state[...] = jnp.zeros_like(state)

        ii = lax.broadcasted_iota(jnp.int32, (CHUNK, CHUNK), 0)
        jj = lax.broadcasted_iota(jnp.int32, (CHUNK, CHUNK), 1)
        lane = lax.broadcasted_iota(jnp.int32, (CHUNK, 128), 1)
        gbv = gb_ref[...]
        gc = _dot_hi(_tri(ii, jj), gbv)
        for h in range(HEADS):
            sl = slice(128 * h, 128 * (h + 1))
            qh = qkv_ref[:, 128 * h:128 * (h + 1)]
            kh = qkv_ref[:, 1024 + 128 * h:1024 + 128 * (h + 1)]
            vh = qkv_ref[:, 2048 + 128 * h:2048 + 128 * (h + 1)]
            c = _chunk_local(qh, kh, vh, _col(gc, GA_LANE + h, lane), _col(gbv, GB_LANE + h, lane), ii, jj)
            st = state[sl, :]
            st_ref[sl, :] = st
            vn = c["u"] - _dot(c["w"], st)
            o_ref[:, sl] = _dot(c["qd"], st) + _dot(c["amat"], vn)
            state[sl, :] = st * c["gl"] + _dot(c["kd"], vn, TN)

    return _call(body, name="gdn_fwd", grid=(n,),
                 in_specs=[_rows(CHUNK, 3072), _rows(CHUNK, 128)],
                 out_specs=[_rows(CHUNK, 1024), _rows(HEADS * 128, 128)],
                 out_shape=[_sds((s, 1024), F32), _sds((n * HEADS * 128, 128), F32)],
                 scratch=[pltpu.VMEM((HEADS * 128, 128), F32)], args=(qkv, gbeta))


def _gdn_bwd(qkv, gbeta, states, do):
    s = qkv.shape[0]
    n = s // CHUNK

    def body(qkv_ref, gb_ref, st_ref, do_ref, dqkv_ref, dgb_ref, dstate):
        @pl.when(pl.program_id(0) == 0)
        def _():
            dstate[...] = jnp.zeros_like(dstate)

        ii = lax.broadcasted_iota(jnp.int32, (CHUNK, CHUNK), 0)
        jj = lax.broadcasted_iota(jnp.int32, (CHUNK, CHUNK), 1)
        lane = lax.broadcasted_iota(jnp.int32, (CHUNK, 128), 1)
        row1 = ii[:, :1]
        gbv = gb_ref[...]
        gc = _dot_hi(_tri(ii, jj), gbv)
        dgc_all = jnp.zeros((CHUNK, 128), F32)
        db_all = jnp.zeros((CHUNK, 128), F32)
        for h in range(HEADS):
            sl = slice(128 * h, 128 * (h + 1))
            qh = qkv_ref[:, 128 * h:128 * (h + 1)]
            kh = qkv_ref[:, 1024 + 128 * h:1024 + 128 * (h + 1)]
            vh = qkv_ref[:, 2048 + 128 * h:2048 + 128 * (h + 1)]
            bcol = _col(gbv, GB_LANE + h, lane)
            c = _chunk_local(qh, kh, vh, _col(gc, GA_LANE + h, lane), bcol, ii, jj)
            st = st_ref[sl, :]
            dst = dstate[sl, :]
            dov = do_ref[:, sl]
            vn = c["u"] - _dot(c["w"], st)
            dvn = _dot(c["amat"], dov, TN) + _dot(c["kd"], dst)
            damat = jnp.where(c["lower"], _dot(dov, vn, NT), 0.0)
            dqd = _dot(dov, st, NT)
            dkd = _dot(vn, dst, NT)
            dw = -_dot(dvn, st, NT)
            dgl = jnp.sum(jnp.sum(st * dst, axis=-1, keepdims=True), axis=0, keepdims=True)
            dstate[sl, :] = _dot(c["qd"], dov, TN) + c["gl"] * dst - _dot(c["w"], dvn, TN)
            dsol = jnp.concatenate([dvn, dw], axis=-1)
            drhs = _dot_hi(c["tinv"], dsol, TN)
            dtinv = _dot_hi(dsol, c["rhs"], NT)
            dl = -_dot_hi(_dot_hi(c["tinv"], dtinv, TN), c["tinv"], NT)
            dl = jnp.where(c["strict"], dl, 0.0)
            dmm = dl * c["decay"]
            dqk = damat * c["decay"]
            wmat = dl * c["lmat"] + damat * c["amat"]
            dgc = jnp.sum(wmat, axis=-1, keepdims=True)
            wcol = jnp.sum(wmat, axis=0, keepdims=True)
            dgc = dgc - jnp.sum(jnp.where(c["eye"], wcol, 0.0), axis=-1, keepdims=True)
            dkb = _dot(dmm, kh) + drhs[:, HEAD_DIM:] * c["egc"]
            dk = _dot(dmm, c["kb"], TN) + _dot(dqk, qh, TN) + dkd * c["ekd"]
            dq = _dot(dqk, kh) + dqd * c["egc"]
            dgc = dgc + jnp.sum(drhs[:, HEAD_DIM:] * c["kbg"], axis=-1, keepdims=True)
            dgc = dgc + jnp.sum(dqd * c["qd"], axis=-1, keepdims=True)
            tmp = jnp.sum(dkd * c["kd"], axis=-1, keepdims=True)
            dgc = dgc - tmp
            dglast = jnp.sum(tmp, axis=0, keepdims=True) + dgl * c["gl"]
            dgc = dgc + jnp.where(row1 == CHUNK - 1, dglast, 0.0)
            dk = dk + dkb * bcol
            db = jnp.sum(dkb * kh, axis=-1, keepdims=True) + jnp.sum(drhs[:, :HEAD_DIM] * vh, axis=-1, keepdims=True)
            dqkv_ref[:, 128 * h:128 * (h + 1)] = dq
            dqkv_ref[:, 1024 + 128 * h:1024 + 128 * (h + 1)] = dk
            dqkv_ref[:, 2048 + 128 * h:2048 + 128 * (h + 1)] = drhs[:, :HEAD_DIM] * bcol
            dgc_all = dgc_all + jnp.where(lane == GA_LANE + h, dgc, 0.0)
            db_all = db_all + jnp.where(lane == GB_LANE + h, db, 0.0)
        dgb_ref[...] = _dot_hi(_tri(jj, ii), dgc_all) + db_all

    rev = lambda w: pl.BlockSpec((CHUNK, w), lambda i: (n - 1 - i, 0))
    return _call(body, name="gdn_bwd", grid=(n,),
                 in_specs=[rev(3072), rev(128), pl.BlockSpec((HEADS * 128, 128), lambda i: (n - 1 - i, 0)), rev(1024)],
                 out_specs=[rev(3072), rev(128)],
                 out_shape=[_sds((s, 3072), F32), _sds((s, 128), F32)],
                 scratch=[pltpu.VMEM((HEADS * 128, 128), F32)], args=(qkv, gbeta, states, do))


NN_B = (((2,), (1,)), ((0,), (0,)))
NT_B = (((2,), (2,)), ((0,), (0,)))
TN_B = (((1,), (1,)), ((0,), (0,)))
GDN_PRE_CHUNKS = 4
GDN_POST_CHUNKS = 2
GDN_SEQ_CHUNKS = 4


def _gather_heads(qkv_ref, gc, gbv, qs, ks, vs, gs, bs, nchunks):
    lane = lax.broadcasted_iota(jnp.int32, (CHUNK, 128), 1)
    for c in range(nchunks):
        rows = slice(CHUNK * c, CHUNK * (c + 1))
        for h in range(HEADS):
            b = HEADS * c + h
            qs[b] = qkv_ref[rows, 128 * h:128 * (h + 1)].astype(F32)
            ks[b] = qkv_ref[rows, 1024 + 128 * h:1024 + 128 * (h + 1)].astype(F32)
            vs[b] = qkv_ref[rows, 2048 + 128 * h:2048 + 128 * (h + 1)].astype(F32)
            gs[b] = jnp.broadcast_to(_col(gc[rows], GA_LANE + h, lane), (CHUNK, 128))
            bs[b] = jnp.broadcast_to(_col(gbv[rows], GB_LANE + h, lane), (CHUNK, 128))


def _block_tri(rows, transpose=False):
    ri = lax.broadcasted_iota(jnp.int32, (rows, rows), 0)
    ci = lax.broadcasted_iota(jnp.int32, (rows, rows), 1)
    same = (ri >> 6) == (ci >> 6)
    return jnp.where(same & ((ci >= ri) if transpose else (ri >= ci)), 1.0, 0.0)


def _local_b(q, k, v, g128, b128):
    ii = lax.broadcasted_iota(jnp.int32, (1, CHUNK, CHUNK), 1)
    jj = lax.broadcasted_iota(jnp.int32, (1, CHUNK, CHUNK), 2)
    lower, strict, eye = ii >= jj, ii > jj, ii == jj
    g64 = g128[:, :, :CHUNK]
    grow = jnp.sum(jnp.where(eye, g64, 0.0), axis=1, keepdims=True)
    decay = jnp.where(lower, jnp.exp(jnp.where(lower, g64 - grow, 0.0)), 0.0)
    kb = k * b128
    vb = v * b128
    mm = lax.dot_general(kb.astype(BF16), k.astype(BF16), NT_B, preferred_element_type=F32)
    lmat = jnp.where(strict, mm * decay, 0.0)
    egc = jnp.exp(g128)
    kbg = kb * egc
    qk = lax.dot_general(q.astype(BF16), k.astype(BF16), NT_B, preferred_element_type=F32)
    row = lax.broadcasted_iota(jnp.int32, (1, CHUNK, 128), 1)
    glast = jnp.sum(jnp.where(row == CHUNK - 1, g128, 0.0), axis=1, keepdims=True)
    ekd = jnp.exp(glast - g128)
    return dict(decay=decay, kb=kb, vb=vb, lmat=lmat, egc=egc, kbg=kbg, amat=qk * decay, qd=q * egc, ekd=ekd,
                kd=k * ekd, gl=jnp.exp(glast), lower=lower, strict=strict, eye=eye)


def _bdot(a, b, dims):
    return lax.dot_general(a.astype(BF16), b.astype(BF16), dims, preferred_element_type=F32)


def _split(a):
    hi = a.astype(BF16)
    return hi, (a - hi.astype(F32)).astype(BF16)


def _bdot_hi(a, b, dims):
    ah, al = _split(a)
    bh, bl = _split(b)
    d = lambda x, y: lax.dot_general(x, y, dims, preferred_element_type=F32)
    return d(ah, bh) + d(ah, bl) + d(al, bh)


def _gdn_pre(qkv, gbeta):
    s = qkv.shape[0]
    n = s // CHUNK
    cb = min(GDN_PRE_CHUNKS, n)
    nb = cb * HEADS
    rows = cb * CHUNK

    def body(qkv_ref, gb_ref, u_ref, w_ref, qd_ref, kd_ref, a_ref, t_ref, gl_ref, qs, ks, vs, gs, bs):
        gbv = gb_ref[...]
        gc = _dot_hi(_block_tri(rows), gbv)
        _gather_heads(qkv_ref, gc, gbv, qs, ks, vs, gs, bs, cb)
        c = _local_b(qs[...], ks[...], vs[...], gs[...], bs[...])
        pw = -c["lmat"]
        tinv = jnp.where(c["eye"], 1.0, 0.0) + pw
        for _ in range(5):
            pw = _bdot_hi(pw, pw, NN_B)
            tinv = tinv + _bdot_hi(tinv, pw, NN_B)
        u_ref[...] = _bdot_hi(tinv, c["vb"], NN_B)
        w_ref[...] = _bdot_hi(tinv, c["kbg"], NN_B).astype(BF16)
        qd_ref[...] = c["qd"].astype(BF16)
        kd_ref[...] = c["kd"].astype(BF16)
        a_ref[...] = c["amat"].astype(BF16)
        t_ref[...] = tinv
        gl_ref[...] = c["gl"]

    b3 = lambda d: pl.BlockSpec((nb, CHUNK, d), lambda i: (i, 0, 0))
    nt = n * HEADS
    return _call(body, name="gdn_pre", grid=(n // cb,),
                 in_specs=[_rows(rows, 3072), _rows(rows, 128)],
                 out_specs=[b3(128), b3(128), b3(128), b3(128), b3(CHUNK), b3(CHUNK),
                            pl.BlockSpec((nb, 1, 128), lambda i: (i, 0, 0))],
                 out_shape=[_sds((nt, CHUNK, 128), F32), _sds((nt, CHUNK, 128), BF16), _sds((nt, CHUNK, 128), BF16),
                            _sds((nt, CHUNK, 128), BF16), _sds((nt, CHUNK, CHUNK), BF16), _sds((nt, CHUNK, CHUNK), F32),
                            _sds((nt, 1, 128), F32)],
                 scratch=[pltpu.VMEM((nb, CHUNK, 128), F32)] * 5, sem=("parallel",), args=(qkv, gbeta))


def _gdn_scan_fwd(u, w, qd, kd, amat, gl):
    nt = u.shape[0]
    n = nt // HEADS
    cs = min(GDN_SEQ_CHUNKS, n)

    def body(u_ref, w_ref, qd_ref, kd_ref, a_ref, gl_ref, o_ref, st_ref, state):
        @pl.when(pl.program_id(0) == 0)
        def _():
            state[...] = jnp.zeros_like(state)

        for c in range(cs):
            sl = slice(HEADS * c, HEADS * (c + 1))
            st = state[...]
            stb = st.astype(BF16)
            st_ref[sl] = stb
            vn = u_ref[sl] - lax.dot_general(w_ref[sl], stb, NN_B, preferred_element_type=F32)
            vnb = vn.astype(BF16)
            o = (lax.dot_general(qd_ref[sl], stb, NN_B, preferred_element_type=F32)
                 + lax.dot_general(a_ref[sl], vnb, NN_B, preferred_element_type=F32))
            state[...] = st * gl_ref[sl] + lax.dot_general(kd_ref[sl], vnb, TN_B, preferred_element_type=F32)
            for h in range(HEADS):
                o_ref[CHUNK * c:CHUNK * (c + 1), 128 * h:128 * (h + 1)] = o[h]

    b3 = lambda d: pl.BlockSpec((cs * HEADS, CHUNK, d), lambda i: (i, 0, 0))
    return _call(body, name="gdn_scan_fwd", grid=(n // cs,),
                 in_specs=[b3(128), b3(128), b3(128), b3(128), b3(CHUNK), pl.BlockSpec((cs * HEADS, 1, 128), lambda i: (i, 0, 0))],
                 out_specs=[_rows(cs * CHUNK, 1024), pl.BlockSpec((cs * HEADS, 128, 128), lambda i: (i, 0, 0))],
                 out_shape=[_sds((n * CHUNK, 1024), F32), _sds((nt, 128, 128), BF16)],
                 scratch=[pltpu.VMEM((HEADS, 128, 128), F32)], args=(u, w, qd, kd, amat, gl))


def _gdn_scan_bwd(w, qd, kd, amat, gl, do):
    nt = w.shape[0]
    n = nt // HEADS
    cs = min(GDN_SEQ_CHUNKS, n)
    ng = n // cs

    def body(w_ref, qd_ref, kd_ref, a_ref, gl_ref, do_ref, ds_ref, dstate, dos):
        @pl.when(pl.program_id(0) == 0)
        def _():
            dstate[...] = jnp.zeros_like(dstate)

        for c in reversed(range(cs)):
            sl = slice(HEADS * c, HEADS * (c + 1))
            for h in range(HEADS):
                dos[h] = do_ref[CHUNK * c:CHUNK * (c + 1), 128 * h:128 * (h + 1)].astype(BF16)
            dob = dos[...]
            dst = dstate[...]
            dstb = dst.astype(BF16)
            ds_ref[sl] = dstb
            dvn = (lax.dot_general(a_ref[sl], dob, TN_B, preferred_element_type=F32)
                   + lax.dot_general(kd_ref[sl], dstb, NN_B, preferred_element_type=F32))
            dstate[...] = (lax.dot_general(qd_ref[sl], dob, TN_B, preferred_element_type=F32) + gl_ref[sl] * dst
                           - lax.dot_general(w_ref[sl], dvn.astype(BF16), TN_B, preferred_element_type=F32))

    b3 = lambda d: pl.BlockSpec((cs * HEADS, CHUNK, d), lambda i: (ng - 1 - i, 0, 0))
    return _call(body, name="gdn_scan_bwd", grid=(ng,),
                 in_specs=[b3(128), b3(128), b3(128), b3(CHUNK), pl.BlockSpec((cs * HEADS, 1, 128), lambda i: (ng - 1 - i, 0, 0)),
                           pl.BlockSpec((cs * CHUNK, 1024), lambda i: (ng - 1 - i, 0))],
                 out_specs=pl.BlockSpec((cs * HEADS, 128, 128), lambda i: (ng - 1 - i, 0, 0)),
                 out_shape=_sds((nt, 128, 128), BF16),
                 scratch=[pltpu.VMEM((HEADS, 128, 128), F32), pltpu.VMEM((HEADS, CHUNK, 128), BF16)],
                 args=(w, qd, kd, amat, gl, do))


def _gdn_post_bwd(qkv, gbeta, u, w, tinv, states, dstates, do):
    s = qkv.shape[0]
    n = s // CHUNK
    cb = min(GDN_POST_CHUNKS, n)
    nb = cb * HEADS
    rows = cb * CHUNK

    def body(qkv_ref, gb_ref, u_ref, w_ref, t_ref, st_ref, ds_ref, do_ref, dqkv_ref, dgb_ref, qs, ks, vs, gs, bs, dos):
        gbv = gb_ref[...]
        gc = _dot_hi(_block_tri(rows), gbv)
        _gather_heads(qkv_ref, gc, gbv, qs, ks, vs, gs, bs, cb)
        for c in range(cb):
            for h in range(HEADS):
                dos[HEADS * c + h] = do_ref[CHUNK * c:CHUNK * (c + 1), 128 * h:128 * (h + 1)].astype(F32)
        q, k, v, b128 = qs[...], ks[...], vs[...], bs[...]
        c = _local_b(q, k, v, gs[...], b128)
        tinv, st, dst, dov = t_ref[...], st_ref[...], ds_ref[...], dos[...]
        wv = w_ref[...]
        vn = u_ref[...] - _bdot(wv, st, NN_B)
        dvn = _bdot(c["amat"], dov, TN_B) + _bdot(c["kd"], dst, NN_B)
        damat = jnp.where(c["lower"], _bdot(dov, vn, NT_B), 0.0)
        dqd = _bdot(dov, st, NT_B)
        dkd = _bdot(vn, dst, NT_B)
        dw = -_bdot(dvn, st, NT_B)
        dgl = jnp.sum(jnp.sum(st.astype(F32) * dst.astype(F32), axis=1, keepdims=True), axis=-1, keepdims=True)
        dvb = _bdot(tinv, dvn, TN_B)
        dkbg = _bdot(tinv, dw, TN_B)
        dtinv = _bdot(dvn, c["vb"], NT_B) + _bdot(dw, c["kbg"], NT_B)
        dl = -_bdot(_bdot(tinv, dtinv, TN_B), tinv, NT_B)
        dl = jnp.where(c["strict"], dl, 0.0)
        dmm = dl * c["decay"]
        dqk = damat * c["decay"]
        wmat = dl * c["lmat"] + damat * c["amat"]
        wcol = jnp.sum(wmat, axis=1, keepdims=True)
        dgc = jnp.sum(wmat, axis=-1, keepdims=True) - jnp.sum(jnp.where(c["eye"], wcol, 0.0), axis=-1, keepdims=True)
        dkb = _bdot(dmm, k, NN_B) + dkbg * c["egc"]
        dk = _bdot(dmm, c["kb"], TN_B) + _bdot(dqk, q, TN_B) + dkd * c["ekd"] + dkb * b128
        dq = _bdot(dqk, k, NN_B) + dqd * c["egc"]
        tmp = jnp.sum(dkd * c["kd"], axis=-1, keepdims=True)
        dgc = (dgc + jnp.sum(dkbg * c["kbg"], axis=-1, keepdims=True) + jnp.sum(dqd * c["qd"], axis=-1, keepdims=True)
               - tmp)
        dglast = jnp.sum(tmp, axis=1, keepdims=True) + dgl * c["gl"][:, :, :1]
        row1 = lax.broadcasted_iota(jnp.int32, (1, CHUNK, 1), 1)
        dgc = dgc + jnp.where(row1 == CHUNK - 1, dglast, 0.0)
        db = jnp.sum(dkb * k, axis=-1, keepdims=True) + jnp.sum(dvb * v, axis=-1, keepdims=True)
        dv = dvb * b128
        lane = lax.broadcasted_iota(jnp.int32, (CHUNK, 128), 1)
        parts = []
        for cc in range(cb):
            acc = jnp.zeros((CHUNK, 128), F32)
            for h in range(HEADS):
                bi = HEADS * cc + h
                rs = slice(CHUNK * cc, CHUNK * (cc + 1))
                dqkv_ref[rs, 128 * h:128 * (h + 1)] = dq[bi].astype(BF16)
                dqkv_ref[rs, 1024 + 128 * h:1024 + 128 * (h + 1)] = dk[bi].astype(BF16)
                dqkv_ref[rs, 2048 + 128 * h:2048 + 128 * (h + 1)] = dv[bi].astype(BF16)
                acc = acc + jnp.where(lane == GA_LANE + h, dgc[bi], 0.0)
            parts.append(acc)
        dgc_all = jnp.concatenate(parts, axis=0)
        dg_all = _dot_hi(_block_tri(rows, transpose=True), dgc_all)
        for cc in range(cb):
            acc = dg_all[CHUNK * cc:CHUNK * (cc + 1)]
            for h in range(HEADS):
                acc = acc + jnp.where(lane == GB_LANE + h, db[HEADS * cc + h], 0.0)
            dgb_ref[CHUNK * cc:CHUNK * (cc + 1), :] = acc

    b3 = lambda d1, d2: pl.BlockSpec((nb, d1, d2), lambda i: (i, 0, 0))
    return _call(body, name="gdn_post_bwd", grid=(n // cb,),
                 in_specs=[_rows(rows, 3072), _rows(rows, 128), b3(CHUNK, 128), b3(CHUNK, 128), b3(CHUNK, CHUNK),
                           b3(128, 128), b3(128, 128), _rows(rows, 1024)],
                 out_specs=[_rows(rows, 3072), _rows(rows, 128)],
                 out_shape=[_sds((s, 3072), BF16), _sds((s, 128), F32)],
                 scratch=[pltpu.VMEM((nb, CHUNK, 128), F32)] * 6, sem=("parallel",),
                 args=(qkv, gbeta, u, w, tinv, states, dstates, do))


def _mix_fwd(o_mla, o_gdn, proj, out_gain):
    s = proj.shape[0]
    tm = min(TILES["row"], s)

    def body(om_ref, og_ref, mg_ref, gg_ref, g_ref, o_ref):
        o_ref[:, :1024] = (om_ref[...] * _silu(mg_ref[...].astype(F32))).astype(BF16)
        for h in range(HEADS):
            sl = slice(128 * h, 128 * (h + 1))
            og = og_ref[:, sl]
            on = og * _rms(og, HEAD_DIM) * g_ref[...]
            o_ref[:, 1024 + 128 * h:1024 + 128 * (h + 1)] = (on * _silu(gg_ref[:, sl].astype(F32))).astype(BF16)

    return _call(body, name="mix_fwd", grid=(s // tm,),
                 in_specs=[_rows(tm, 1024), _rows(tm, 1024), _rows(tm, 1024, 1), _rows(tm, 1024, 5), _full((1, 128))],
                 out_specs=_rows(tm, 2048), out_shape=_sds((s, 2048), BF16), args=(o_mla, o_gdn, proj, proj, out_gain))


def _mix_bwd(o_mla, o_gdn, proj, out_gain, dmixed):
    s = proj.shape[0]
    tm = min(TILES["row"], s)

    def body(om_ref, og_ref, mg_ref, gg_ref, g_ref, dm_ref, dg_ref, dom_ref, dog_ref, dmg_ref, dgg_ref, ag_ref,
             delta_ref):
        @pl.when(pl.program_id(0) == 0)
        def _():
            ag_ref[...] = jnp.zeros_like(ag_ref)

        mg = mg_ref[...].astype(F32)
        dm = dm_ref[...].astype(F32)
        om = om_ref[...]
        dom = (dm * _silu(mg)).astype(BF16)
        dom_ref[...] = dom
        dmg_ref[...] = (dm * om * _dsilu(mg)).astype(BF16)
        prod = dom.astype(F32) * om
        lane = lax.broadcasted_iota(jnp.int32, (tm, 128), 1)
        delta = jnp.zeros((tm, 128), F32)
        for h in range(HEADS):
            delta = delta + jnp.where(lane == h, jnp.sum(prod[:, 128 * h:128 * (h + 1)], axis=-1, keepdims=True), 0.0)
        delta_ref[...] = delta
        for h in range(HEADS):
            sl = slice(128 * h, 128 * (h + 1))
            og, gg, d = og_ref[:, sl], gg_ref[:, sl].astype(F32), dg_ref[:, sl].astype(F32)
            on = og * _rms(og, HEAD_DIM) * g_ref[...]
            dgg_ref[:, sl] = (d * on * _dsilu(gg)).astype(BF16)
            dx, gpart = _rms_bwd(og, g_ref[...], d * _silu(gg), HEAD_DIM)
            dog_ref[:, sl] = dx.astype(BF16)
            ag_ref[...] += _acc8(gpart)

    return _call(body, name="mix_bwd", grid=(s // tm,),
                 in_specs=[_rows(tm, 1024), _rows(tm, 1024), _rows(tm, 1024, 1), _rows(tm, 1024, 5), _full((1, 128)),
                           _rows(tm, 1024, 0), _rows(tm, 1024, 1)],
                 out_specs=[_rows(tm, 1024), _rows(tm, 1024), _rows(tm, 1024), _rows(tm, 1024), _full((8, 128)),
                            _rows(tm, 128)],
                 out_shape=[_sds((s, 1024), BF16), _sds((s, 1024), BF16), _sds((s, 1024), BF16), _sds((s, 1024), BF16),
                            _sds((8, 128), F32), _sds((s, 128), F32)],
                 args=(o_mla, o_gdn, proj, proj, out_gain, dmixed, dmixed))


def _out_fwd(mixed, w_out, x, target):
    s = x.shape[0]
    tm = min(TILES["mm"], s)
    tn = min(TILES["mm"], D_MODEL)

    def body(m_ref, w_ref, x_ref, t_ref, dy_ref, acc_ref):
        err = x_ref[...] + _dot(m_ref[...], w_ref[...]) - t_ref[...]
        dy_ref[...] = (err * (1.0 / D_MODEL)).astype(BF16)

        @pl.when(pl.program_id(1) == 0)
        def _():
            acc_ref[...] = jnp.zeros_like(acc_ref)

        acc_ref[...] += _acc8(err * err)

    return _call(body, name="out_fwd", grid=(D_MODEL // tn, s // tm),
                 in_specs=[pl.BlockSpec((tm, D_MODEL), lambda j, i: (i, 0)), pl.BlockSpec((D_MODEL, tn), lambda j, i: (0, j)),
                           pl.BlockSpec((tm, tn), lambda j, i: (i, j)), pl.BlockSpec((tm, tn), lambda j, i: (i, j))],
                 out_specs=[pl.BlockSpec((tm, tn), lambda j, i: (i, j)), pl.BlockSpec((8, tn), lambda j, i: (0, j))],
                 out_shape=[_sds((s, D_MODEL), BF16), _sds((8, D_MODEL), F32)], args=(mixed, w_out, x, target))


def _row_tile(r, c):
    if r % 8 != 0:
        return r
    t = 8
    while r % (2 * t) == 0 and 2 * t * c * 4 <= (1 << 20):
        t *= 2
    return t


def _sum_arrays(parts, name, also_bf16=False):
    r, c = parts[0].shape
    tr = _row_tile(r, c)
    n = len(parts)

    def body(*refs):
        acc = refs[0][...].astype(F32)
        for p_ref in refs[1:n]:
            acc = acc + p_ref[...].astype(F32)
        refs[n][...] = acc
        if also_bf16:
            refs[n + 1][...] = acc.astype(BF16)

    nout = 2 if also_bf16 else 1
    out = _call(body, name=name, grid=(r // tr,), in_specs=[_rows(tr, c)] * n, out_specs=[_rows(tr, c)] * nout,
                out_shape=[_sds((r, c), F32), _sds((r, c), BF16)][:nout], args=tuple(parts))
    return out if also_bf16 else out[0]


def _adamw(w, g, m, v, name):
    r, c = w.shape
    c1 = 1.0 - ADAM_B1 ** ADAM_STEP
    c2 = 1.0 - ADAM_B2 ** ADAM_STEP

    def body(w_ref, g_ref, m_ref, v_ref, d_ref, nm_ref, nv_ref):
        gv = g_ref[...]
        nm = ADAM_B1 * m_ref[...] + (1.0 - ADAM_B1) * gv
        nv = ADAM_B2 * v_ref[...] + (1.0 - ADAM_B2) * (gv * gv)
        nm_ref[...] = nm
        nv_ref[...] = nv
        d_ref[...] = -ADAM_LR * ((nm / c1) / (jnp.sqrt(nv / c2) + ADAM_EPS) + ADAM_WD * w_ref[...])

    if r % 8 == 0:
        tr = _row_tile(r, c)
        grid, spec = (r // tr,), _rows(tr, c)
    else:
        tc = c
        while tc % 256 == 0 and r * tc * 4 > (3 << 19):
            tc //= 2
        grid, spec = (c // tc,), pl.BlockSpec((r, tc), lambda i: (0, i))
    return _call(body, name=name, grid=grid, in_specs=[spec] * 4, out_specs=[spec] * 3,
                 out_shape=[_sds((r, c), F32)] * 3, args=(w, g, m, v))


ANY = pl.BlockSpec(memory_space=pl.ANY)
CHIP_FLIPS = ((1, 0), (0, 1), (1, 1))


def _comm_call(body, *, name, n_in, out_shape, scratch):
    def kfn(*refs):
        body(*refs)
    return pl.pallas_call(kfn, name=name, in_specs=[ANY] * n_in, out_specs=[ANY] * len(out_shape), out_shape=out_shape,
                          scratch_shapes=list(scratch),
                          compiler_params=pltpu.CompilerParams(has_side_effects=True))


def _all_gather_chips(shards):
    na = len(shards)

    def body(*refs):
        copies = _gather_copies(refs[:na], refs[na:2 * na], *refs[2 * na:])
        _gather_start(copies)
        _gather_finish(copies)

    out_shape = [_sds((4,) + a.shape, a.dtype) for a in shards]
    sem = pltpu.SemaphoreType.DMA((na, 3))
    got = _comm_call(body, name="all_gather_weights", n_in=na, out_shape=out_shape, scratch=[sem, sem, sem, sem])(*shards)
    return _place_own_blocks(got, shards)


def _gather_copies(ins, outs, send_sems, recv_sems, fwd_send, fwd_recv):
    x, y, c = lax.axis_index("x"), lax.axis_index("y"), lax.axis_index("c")
    my_k = 2 * x + y
    direct, forwards = [], []
    for a in range(len(ins)):
        rows = ins[a].shape[0]
        for r, (fx, fy) in enumerate(CHIP_FLIPS):
            px, py = x ^ fx, y ^ fy
            if rows % 32 == 0:
                mine = pl.ds(pl.multiple_of(c * (rows // 2), 16), rows // 2)
                other = pl.ds(pl.multiple_of((1 - c) * (rows // 2), 16), rows // 2)
                rc = pltpu.make_async_remote_copy(
                    src_ref=ins[a].at[mine], dst_ref=outs[a].at[my_k, mine], send_sem=send_sems.at[a, r],
                    recv_sem=recv_sems.at[a, r], device_id=(px, py, c), device_id_type=MESH)
                landed = outs[a].at[2 * px + py, mine]
                fw = pltpu.make_async_remote_copy(
                    src_ref=landed, dst_ref=landed, send_sem=fwd_send.at[a, r], recv_sem=fwd_recv.at[a, r],
                    device_id=(x, y, 1 - c), device_id_type=MESH)
                from_sib = outs[a].at[2 * px + py, other]
                fw_in = pltpu.make_async_remote_copy(
                    src_ref=from_sib, dst_ref=from_sib, send_sem=fwd_send.at[a, r], recv_sem=fwd_recv.at[a, r],
                    device_id=(x, y, 1 - c), device_id_type=MESH)
                forwards.append((rc, fw, fw_in))
            else:
                direct.append(pltpu.make_async_remote_copy(
                    src_ref=ins[a], dst_ref=outs[a].at[my_k], send_sem=send_sems.at[a, r],
                    recv_sem=recv_sems.at[a, r], device_id=(px, py, c), device_id_type=MESH))
    return forwards, direct


def _gather_start(copies):
    forwards, direct = copies
    for rc, _, _ in forwards:
        rc.start()
    for rc in direct:
        rc.start()


def _gather_finish(copies):
    forwards, direct = copies
    for rc, fw, _ in forwards:
        rc.wait_recv()
        fw.start()
    for rc, fw, fw_in in forwards:
        rc.wait_send()
        fw.wait_send()
        fw_in.wait_recv()
    for rc in direct:
        rc.wait()


def _place_own_blocks(got, shards):
    my_k = 2 * lax.axis_index("x") + lax.axis_index("y")
    return [lax.dynamic_update_index_in_dim(g, a, my_k, 0) for g, a in zip(got, shards)]


def _norm1_fwd_gather(x, gain, shards):
    s = x.shape[0]
    tm = min(TILES["row"], s)
    ni = s // tm
    na = len(shards)

    def kfn(x_ref, g_ref, *rest):
        o_ref = rest[na]
        sems = rest[2 * na + 1:]
        i = pl.program_id(0)

        @pl.when(i == 0)
        def _():
            _gather_start(_gather_copies(rest[:na], rest[na + 1:2 * na + 1], *sems))

        xv = x_ref[...]
        r = lax.rsqrt(jnp.mean(xv * xv, axis=-1, keepdims=True) + NORM_EPS)
        o_ref[...] = (xv * r * g_ref[...]).astype(BF16)

        @pl.when(i == ni - 1)
        def _():
            _gather_finish(_gather_copies(rest[:na], rest[na + 1:2 * na + 1], *sems))

    sem = pltpu.SemaphoreType.DMA((na, 3))
    out = pl.pallas_call(
        kfn, name="norm1_fwd_gather", grid=(ni,),
        in_specs=[_rows(tm, D_MODEL), _full((1, D_MODEL))] + [ANY] * na,
        out_specs=[_rows(tm, D_MODEL)] + [ANY] * na,
        out_shape=[_sds((s, D_MODEL), BF16)] + [_sds((4,) + s_a.shape, s_a.dtype) for s_a in shards],
        scratch_shapes=[sem, sem, sem, sem],
        compiler_params=pltpu.CompilerParams(dimension_semantics=("arbitrary",), vmem_limit_bytes=VMEM_LIMIT_V7X,
                                             has_side_effects=True))(x, gain, *shards)
    return out[0], _place_own_blocks(list(out[1:]), shards)


def _matmul_nn_gather(a, b, shards, *, name, tm, tn, out_dtype):
    m, kdim = a.shape
    n = b.shape[1]
    ni, nj = m // tm, n // tn
    na = len(shards)

    def body(a_ref, b_ref, *rest):
        o_ref = rest[na]
        sems = rest[2 * na + 1:]
        i, j = pl.program_id(0), pl.program_id(1)

        @pl.when((i == 0) & (j == 0))
        def _():
            _gather_start(_gather_copies(rest[:na], rest[na + 1:2 * na + 1], *sems))

        o_ref[...] = _dot(a_ref[...], b_ref[...]).astype(out_dtype)

        @pl.when((i == ni - 1) & (j == nj - 1))
        def _():
            _gather_finish(_gather_copies(rest[:na], rest[na + 1:2 * na + 1], *sems))

    def kfn(*refs):
        body(*refs)
    sem = pltpu.SemaphoreType.DMA((na, 3))
    out = pl.pallas_call(
        kfn, name=name, grid=(ni, nj),
        in_specs=[pl.BlockSpec((tm, kdim), lambda i, j: (i, 0)), pl.BlockSpec((kdim, tn), lambda i, j: (0, j))] + [ANY] * na,
        out_specs=[pl.BlockSpec((tm, tn), lambda i, j: (i, j))] + [ANY] * na,
        out_shape=[_sds((m, n), out_dtype)] + [_sds((4,) + s_a.shape, s_a.dtype) for s_a in shards],
        scratch_shapes=[sem, sem, sem, sem],
        compiler_params=pltpu.CompilerParams(dimension_semantics=("arbitrary", "arbitrary"),
                                             vmem_limit_bytes=VMEM_LIMIT_V7X, has_side_effects=True))(a, b, *shards)
    return out[0], _place_own_blocks(list(out[1:]), shards)


def _all_reduce_small(vec):
    r = vec.shape[0]

    def body(v_ref, o_ref, gath, send_sems, recv_sems):
        x, y, c = lax.axis_index("x"), lax.axis_index("y"), lax.axis_index("c")
        me = 4 * x + 2 * y + c
        gath[me] = v_ref[...]
        copies = []
        for rel in range(1, 8):
            fx, fy, fc = (rel >> 2) & 1, (rel >> 1) & 1, rel & 1
            rc = pltpu.make_async_remote_copy(
                src_ref=v_ref, dst_ref=gath.at[me], send_sem=send_sems.at[rel - 1], recv_sem=recv_sems.at[rel - 1],
                device_id=(x ^ fx, y ^ fy, c ^ fc), device_id_type=MESH)
            rc.start()
            copies.append(rc)
        for rc in copies:
            rc.wait()
        acc = gath[0]
        for d in range(1, 8):
            acc = acc + gath[d]
        o_ref[...] = acc

    def kfn(*refs):
        body(*refs)
    vm = pl.BlockSpec(memory_space=pltpu.VMEM)
    return pl.pallas_call(kfn, name="all_reduce_small", in_specs=[vm], out_specs=vm, out_shape=_sds((r, 128), F32),
                          scratch_shapes=[pltpu.VMEM((8, r, 128), F32), pltpu.SemaphoreType.DMA((7,)),
                                          pltpu.SemaphoreType.DMA((7,))],
                          compiler_params=pltpu.CompilerParams(has_side_effects=True))(vec)


def _exchange_halves(arrs):
    na = len(arrs)

    def body(*refs):
        ins, outs = refs[:na], refs[na:2 * na]
        send_sems, recv_sems = refs[2 * na:]
        x, y, c = lax.axis_index("x"), lax.axis_index("y"), lax.axis_index("c")
        copies = []
        for a in range(na):
            half = ins[a].shape[1] // 2
            src = ins[a].at[:, pl.ds(pl.multiple_of((1 - c) * half, 8), half), :]
            rc = pltpu.make_async_remote_copy(src_ref=src, dst_ref=outs[a], send_sem=send_sems.at[a],
                                              recv_sem=recv_sems.at[a], device_id=(x, y, 1 - c), device_id_type=MESH)
            rc.start()
            copies.append(rc)
        for rc in copies:
            rc.wait()

    out_shape = [_sds((4, a.shape[1] // 2, a.shape[2]), F32) for a in arrs]
    return _comm_call(body, name="rs_pair_exchange", n_in=na, out_shape=out_shape,
                      scratch=[pltpu.SemaphoreType.DMA((na,)), pltpu.SemaphoreType.DMA((na,))])(*arrs)


def _scatter_to_chips(arrs):
    na = len(arrs)

    def body(*refs):
        ins, outs = refs[:na], refs[na:2 * na]
        send_sems, recv_sems = refs[2 * na:]
        x, y, c = lax.axis_index("x"), lax.axis_index("y"), lax.axis_index("c")
        copies = []
        for a in range(na):
            for r, (fx, fy) in enumerate(CHIP_FLIPS):
                px, py = x ^ fx, y ^ fy
                rc = pltpu.make_async_remote_copy(
                    src_ref=ins[a].at[2 * px + py], dst_ref=outs[a].at[r], send_sem=send_sems.at[a, r],
                    recv_sem=recv_sems.at[a, r], device_id=(px, py, c), device_id_type=MESH)
                rc.start()
                copies.append(rc)
        for rc in copies:
            rc.wait()

    out_shape = [_sds((3,) + a.shape[1:], a.dtype) for a in arrs]
    return _comm_call(body, name="rs_chip_scatter", n_in=na, out_shape=out_shape,
                      scratch=[pltpu.SemaphoreType.DMA((na, 3)), pltpu.SemaphoreType.DMA((na, 3))])(*arrs)


def _sum_into_half(parts, name):
    r2, c = parts[0].shape
    tr = _row_tile(r2, c)
    nb = r2 // tr
    n = len(parts)

    def kfn(c_ref, *refs):
        acc = refs[0][...].astype(F32)
        for p_ref in refs[1:n]:
            acc = acc + p_ref[...].astype(F32)
        refs[n][...] = acc

    spec = pltpu.PrefetchScalarGridSpec(
        num_scalar_prefetch=1, grid=(nb,), in_specs=[pl.BlockSpec((tr, c), lambda i, cr: (i, 0))] * n,
        out_specs=pl.BlockSpec((tr, c), lambda i, cr: (cr[0] * nb + i, 0)))
    core = lax.axis_index("c").astype(jnp.int32).reshape(1)
    return pl.pallas_call(kfn, name=name, grid_spec=spec, out_shape=_sds((2 * r2, c), F32),
                          compiler_params=pltpu.CompilerParams(dimension_semantics=("arbitrary",),
                                                               vmem_limit_bytes=VMEM_LIMIT_V7X))(core, *parts)


def _join_in_place(arrs):
    na = len(arrs)

    def body(*refs):
        outs = refs[na:2 * na]
        send_sems, recv_sems = refs[2 * na:]
        x, y, c = lax.axis_index("x"), lax.axis_index("y"), lax.axis_index("c")
        copies = []
        for a in range(na):
            half = outs[a].shape[0] // 2
            mine = outs[a].at[pl.ds(pl.multiple_of(c * half, 8), half), :]
            rc = pltpu.make_async_remote_copy(src_ref=mine, dst_ref=mine, send_sem=send_sems.at[a],
                                              recv_sem=recv_sems.at[a], device_id=(x, y, 1 - c), device_id_type=MESH)
            rc.start()
            copies.append(rc)
        for rc in copies:
            rc.wait()

    def kfn(*refs):
        body(*refs)
    return pl.pallas_call(kfn, name="rs_pair_join", in_specs=[ANY] * na, out_specs=[ANY] * na,
                          out_shape=[_sds(a.shape, F32) for a in arrs],
                          input_output_aliases={a: a for a in range(na)},
                          scratch_shapes=[pltpu.SemaphoreType.DMA((na,)), pltpu.SemaphoreType.DMA((na,))],
                          compiler_params=pltpu.CompilerParams(has_side_effects=True))(*arrs)


def _pair_sum(g, o, name):
    _, r, c = g.shape
    half = r // 2
    tr = _row_tile(half, c)
    nb = half // tr

    def kfn(c_ref, g_ref, o_ref, s32_ref, s16_ref):
        acc = g_ref[...] + o_ref[...]
        s32_ref[...] = acc
        s16_ref[...] = acc.astype(BF16)

    blk = lambda imap: pl.BlockSpec((None, tr, c), imap)
    same = lambda k, i, cr: (k, i, 0)
    spec = pltpu.PrefetchScalarGridSpec(
        num_scalar_prefetch=1, grid=(4, nb), in_specs=[blk(lambda k, i, cr: (k, cr[0] * nb + i, 0)), blk(same)],
        out_specs=[blk(same), blk(same)])
    core = lax.axis_index("c").astype(jnp.int32).reshape(1)
    return pl.pallas_call(kfn, name=name, grid_spec=spec, out_shape=[_sds((4, half, c), F32), _sds((4, half, c), BF16)],
                          compiler_params=pltpu.CompilerParams(dimension_semantics=("arbitrary", "arbitrary"),
                                                               vmem_limit_bytes=VMEM_LIMIT_V7X))(core, g, o)


def _rs_pair_stage(grads):
    got = _exchange_halves(grads)
    sums = [_pair_sum(g, o, f"rs_pair_sum_{a}") for a, (g, o) in enumerate(zip(grads, got))]
    return [s32 for s32, _ in sums], [s16 for _, s16 in sums]


def _rs_chip_stage(pair, recv):
    k_me = 2 * lax.axis_index("x") + lax.axis_index("y")
    halves = []
    for a, (p, rv) in enumerate(zip(pair, recv)):
        own = lax.dynamic_index_in_dim(p, k_me, 0, keepdims=False)
        halves.append(_sum_into_half([own, rv[0], rv[1], rv[2]], f"rs_chip_sum_{a}"))
    return _join_in_place(halves)


def _reduce_scatter(grads):
    pair, pair_bf16 = _rs_pair_stage(grads)
    return _rs_chip_stage(pair, _scatter_to_chips(pair_bf16))


def _matmul_nt_scatter(a, b, send, *, name, tm, tn, out_dtype):
    m, kdim = a.shape
    n = b.shape[0]
    ni, nj = m // tm, n // tn
    na = len(send)

    def body(a_ref, b_ref, *rest):
        send_refs, o_ref, recv_refs = rest[:na], rest[na], rest[na + 1:2 * na + 1]
        send_sems, recv_sems = rest[2 * na + 1:]
        i, j = pl.program_id(0), pl.program_id(1)

        def copies():
            x, y, c = lax.axis_index("x"), lax.axis_index("y"), lax.axis_index("c")
            out = []
            for s_i in range(na):
                for r, (fx, fy) in enumerate(CHIP_FLIPS):
                    px, py = x ^ fx, y ^ fy
                    out.append(pltpu.make_async_remote_copy(
                        src_ref=send_refs[s_i].at[2 * px + py], dst_ref=recv_refs[s_i].at[r],
                        send_sem=send_sems.at[s_i, r], recv_sem=recv_sems.at[s_i, r], device_id=(px, py, c),
                        device_id_type=MESH))
            return out

        @pl.when((i == 0) & (j == 0))
        def _():
            for cp in copies():
                cp.start()

        o_ref[...] = _dot(a_ref[...], b_ref[...], NT).astype(out_dtype)

        @pl.when((i == ni - 1) & (j == nj - 1))
        def _():
            for cp in copies():
                cp.wait()

    def kfn(*refs):
        body(*refs)
    sem = pltpu.SemaphoreType.DMA((na, 3))
    out = pl.pallas_call(
        kfn, name=name, grid=(ni, nj),
        in_specs=[pl.BlockSpec((tm, kdim), lambda i, j: (i, 0)), pl.BlockSpec((tn, kdim), lambda i, j: (j, 0))] + [ANY] * na,
        out_specs=[pl.BlockSpec((tm, tn), lambda i, j: (i, j))] + [ANY] * na,
        out_shape=[_sds((m, n), out_dtype)] + [_sds((3,) + s_a.shape[1:], s_a.dtype) for s_a in send],
        scratch_shapes=[sem, sem],
        compiler_params=pltpu.CompilerParams(dimension_semantics=("arbitrary", "arbitrary"),
                                             vmem_limit_bytes=VMEM_LIMIT_V7X, has_side_effects=True))(a, b, *send)
    return out[0], list(out[1:])


def _pad_w_in(w):
    z = jnp.zeros((w.shape[0], 1024 - 848), w.dtype)
    return jnp.concatenate([w[:, 0:832], w[:, 4928:4944], z, w[:, 832:4928], w[:, 4944:5968]], axis=1)


def _unpad_w_in(g):
    return jnp.concatenate([g[:, 0:832], g[:, 1024:5120], g[:, 832:848], g[:, 5120:6144]], axis=1)


W_IN_SHARD = W_IN_COLS // 4
W_IN_RUNS = ((0, 832, 0), (832, 4928, 1024), (4928, 4944, 832), (4944, 5968, 5120))


def _w_in_grad_blocks(p):
    def orig_cols(lo, hi):
        parts = [p[:, pa + max(lo, a) - a:pa + min(hi, b) - a] for a, b, pa in W_IN_RUNS if max(lo, a) < min(hi, b)]
        return parts[0] if len(parts) == 1 else jnp.concatenate(parts, axis=1)
    return jnp.stack([orig_cols(W_IN_SHARD * k, W_IN_SHARD * (k + 1)) for k in range(4)])


def _pad_w_in_blocks(g):
    def orig_cols(lo, hi):
        return [g[k][:, max(lo, W_IN_SHARD * k) - W_IN_SHARD * k:min(hi, W_IN_SHARD * (k + 1)) - W_IN_SHARD * k]
                for k in range(4) if max(lo, W_IN_SHARD * k) < min(hi, W_IN_SHARD * (k + 1))]
    z = jnp.zeros((g.shape[1], 1024 - 848), g.dtype)
    return jnp.concatenate(orig_cols(0, 832) + orig_cols(4928, 4944) + [z] + orig_cols(832, 4928) + orig_cols(4944, 5968),
                           axis=1)


def _pad_heads(w):
    r = w.shape[0]
    return jnp.pad(w.reshape(r, HEADS, QK_DIM), ((0, 0), (0, 0), (0, QK_PAD - QK_DIM))).reshape(r, HEADS * QK_PAD)


def _unpad_heads(w):
    r = w.shape[0]
    return w.reshape(r, HEADS, QK_PAD)[:, :, :QK_DIM].reshape(r, HEADS * QK_DIM)


def _cols_to_blocks(w):
    r = w.shape[0]
    return w.reshape(r, 4, -1).transpose(1, 0, 2)


def _blocks_to_cols(w):
    return w.transpose(1, 0, 2).reshape(w.shape[1], -1)


SMALL_ROWS = {"norm_gain": (0, 2048), "mla_q_a_gain": (16, 512), "mla_kv_a_gain": (20, 256),
              "mla_q_norm_gain": (22, 192), "mla_k_norm_gain": (24, 192), "gdn_a_log": (26, 8),
              "gdn_dt_bias": (27, 8), "gdn_out_norm_gain": (28, 128)}
LOSS_ROW = 29
SMALL_PACK_ROWS = 32
CONV_ROW = 32


def _pack_small(vals, loss=None):
    rows = []
    at = 0
    for name, (row, size) in SMALL_ROWS.items():
        assert row == at
        nr = -(-size // 128)
        rows.append(jnp.pad(vals[name].reshape(-1).astype(F32), (0, nr * 128 - size)).reshape(nr, 128))
        at += nr
    assert at == LOSS_ROW
    if loss is not None:
        rows.append(jnp.pad(loss.reshape(1, 1), ((0, 0), (0, 127))))
        at += 1
    rows.append(jnp.zeros((SMALL_PACK_ROWS - at, 128), F32))
    return jnp.concatenate(rows, axis=0)


def _unpack_small(pack, name):
    row, size = SMALL_ROWS[name]
    nr = -(-size // 128)
    return pack[row:row + nr].reshape(-1)[:size].reshape(1, size)


def _local_step(x, positions, target, norm_gain, w_in_p, q_a_gain, kv_a_gain, w_uq_p, w_ukv, q_norm_gain,
                k_norm_gain, conv_w, a_log, dt_bias, out_gain, w_out, scatter_hook=None, late_weights=None,
                first_weights=None):
    half = HALF_ROPE
    inv_freq = jnp.power(10000.0, -jnp.arange(half, dtype=F32) / half)
    ang = positions.astype(F32)[:, None] * inv_freq
    cos, sin = jnp.cos(ang), jnp.sin(ang)
    zpad = jnp.zeros((x.shape[0], 64), F32)
    cs = jnp.concatenate([cos, cos, zpad], axis=1)
    sn = jnp.concatenate([-sin, sin, zpad], axis=1)
    gq = jnp.pad(q_norm_gain.reshape(1, QK_DIM), ((0, 0), (0, QK_PAD - QK_DIM)))
    gk = jnp.pad(k_norm_gain.reshape(1, QK_DIM), ((0, 0), (0, QK_PAD - QK_DIM)))
    lane_pad = ((0, 0), (GA_LANE, 128 - GA_LANE - HEADS))
    alog128 = jnp.pad(a_log.reshape(1, HEADS), lane_pad)
    dtb128 = jnp.pad(dt_bias.reshape(1, HEADS), lane_pad)
    ng, qag, kvag, og = (norm_gain.reshape(1, -1), q_a_gain.reshape(1, -1), kv_a_gain.reshape(1, -1),
                         out_gain.reshape(1, -1))

    if first_weights is None:
        xn = _norm1_fwd(x, ng)
    else:
        shards, assemble = first_weights
        xn, gathered = _norm1_fwd_gather(x, ng, shards)
        w_in_p, conv_w = assemble(gathered)
    misc = _matmul(xn, w_in_p[:, 768:896], mode="nn", out_dtype=F32, name="in_proj_misc")
    if late_weights is None:
        proj = _matmul(xn, w_in_p, mode="nn", out_dtype=BF16, name="in_proj")
    else:
        shards, assemble = late_weights
        proj, gathered = _matmul_nn_gather(xn, w_in_p, shards, name="in_proj_gather", tm=TILES["mm"], tn=TILES["mm"],
                                           out_dtype=BF16)
        w_uq_p, w_ukv, w_out = assemble(gathered)
    cqn, ckvn = _mla_a_norm(proj, qag, kvag)
    q_pre = _matmul(cqn, w_uq_p, mode="nn", out_dtype=BF16, name="q_up")
    kv_pre = _matmul(ckvn, w_ukv, mode="nn", out_dtype=BF16, name="kv_up")
    q, k, v = _mla_post_fwd(q_pre, kv_pre, misc, cs, sn, gq, gk)
    o_mla, lse = _attn_fwd(q, k, v)
    qkv = _gdn_conv_fwd(proj, conv_w)
    gbeta = _gdn_gates_fwd(misc, alog128, dtb128)
    g_u, g_w, g_qd, g_kd, g_a, g_t, g_gl = _gdn_pre(qkv, gbeta)
    o_gdn, states = _gdn_scan_fwd(g_u, g_w, g_qd, g_kd, g_a, g_gl)
    mixed = _mix_fwd(o_mla, o_gdn, proj, og)
    dy, sq = _out_fwd(mixed, w_out, x, target)

    dmixed = _matmul(dy, w_out, mode="nt", out_dtype=BF16, name="d_mixed")
    d_w_out = _matmul(mixed, dy, mode="tn", out_dtype=F32, name="d_w_out", tk=4096)
    do_mla, do_gdn, dmg, dgg, d_out_gain, delta128 = _mix_bwd(o_mla, o_gdn, proj, og, dmixed)
    s_len = x.shape[0]
    delta_rows = delta128[:, :HEADS].T.reshape(HEADS, 1, s_len)
    dq, dk, dv = _attn_bwd(q, k, v, lse.reshape(HEADS, 1, s_len), delta_rows, do_mla)
    dq_pre, dkv_pre, dkr, d_gq, d_gk = _mla_post_bwd(q_pre, kv_pre, misc, cs, sn, gq, gk, dq, dk, dv)
    d_w_uq_p = _matmul(cqn, dq_pre, mode="tn", out_dtype=F32, name="d_w_uq", tk=1024)
    d_w_ukv = _matmul(ckvn, dkv_pre, mode="tn", out_dtype=F32, name="d_w_ukv", tk=1024)
    dcqn = _matmul(dq_pre, w_uq_p, mode="nt", out_dtype=F32, name="d_cqn")
    dckvn = _matmul(dkv_pre, w_ukv, mode="nt", out_dtype=F32, name="d_ckvn")
    dcq, dckv, d_qag, d_kvag = _mla_a_norm_bwd(proj, qag, kvag, dcqn, dckvn)
    dstates = _gdn_scan_bwd(g_w, g_qd, g_kd, g_a, g_gl, do_gdn)
    dqkv, dgbeta = _gdn_post_bwd(qkv, gbeta, g_u, g_w, g_t, states, dstates, do_gdn)
    dz = _gdn_conv_bwd_a(proj, conv_w, dqkv)
    dgx, d_conv = _gdn_conv_bwd_b(proj, conv_w, dz)
    dmisc, d_alog, d_dtb = _gdn_gates_bwd(misc, alog128, dtb128, gbeta, dgbeta, dkr)
    dproj = jnp.concatenate([dcq, dckv, dmisc, jnp.zeros((x.shape[0], 128), BF16), dmg, dgx, dgg], axis=1)
    d_w_in_p = _matmul(xn, dproj, mode="tn", out_dtype=F32, name="d_w_in", tk=4096)
    big = {"w_in": d_w_in_p, "w_uq": d_w_uq_p, "w_ukv": d_w_ukv, "w_out": d_w_out, "gdn_conv_w": d_conv}
    if scatter_hook is None:
        dxn, received = _matmul(dproj, w_in_p, mode="nt", out_dtype=BF16, name="d_xn", tm=512, tn=512), None
    else:
        dxn, received = _matmul_nt_scatter(dproj, w_in_p, scatter_hook(big), name="d_xn_scatter", tm=1024, tn=512,
                                           out_dtype=BF16)
    grad_x, d_ng = _norm1_bwd(x, ng, dxn, dy)

    small = {"norm_gain": d_ng.sum(0), "mla_q_a_gain": d_qag.sum(0), "mla_kv_a_gain": d_kvag.sum(0),
             "mla_q_norm_gain": d_gq.sum(0)[:QK_DIM], "mla_k_norm_gain": d_gk.sum(0)[:QK_DIM],
             "gdn_a_log": d_alog.sum(0)[GA_LANE:GA_LANE + HEADS], "gdn_dt_bias": d_dtb.sum(0)[GA_LANE:GA_LANE + HEADS],
             "gdn_out_norm_gain": d_out_gain.sum(0)}
    return sq, grad_x, small, big, received


WEIGHTS = ["norm_gain", "w_in", "mla_q_a_gain", "mla_kv_a_gain", "w_uq", "w_ukv", "mla_q_norm_gain", "mla_k_norm_gain",
           "gdn_conv_w", "gdn_a_log", "gdn_dt_bias", "gdn_out_norm_gain", "w_out"]
BIG = ["w_in", "w_uq", "w_ukv", "w_out"]


def kernel(x, positions, norm_gain, w_in, mla_q_a_gain, mla_kv_a_gain, w_uq, w_ukv, mla_q_norm_gain, mla_k_norm_gain, gdn_conv_w, gdn_a_log, gdn_dt_bias, gdn_out_norm_gain, w_out, loss_target, m_norm_gain, m_w_in, m_mla_q_a_gain, m_mla_kv_a_gain, m_w_uq, m_w_ukv, m_mla_q_norm_gain, m_mla_k_norm_gain, m_gdn_conv_w, m_gdn_a_log, m_gdn_dt_bias, m_gdn_out_norm_gain, m_w_out, v_norm_gain, v_w_in, v_mla_q_a_gain, v_mla_kv_a_gain, v_w_uq, v_w_ukv, v_mla_q_norm_gain, v_mla_k_norm_gain, v_gdn_conv_w, v_gdn_a_log, v_gdn_dt_bias, v_gdn_out_norm_gain, v_w_out):
    w = dict(norm_gain=norm_gain, w_in=w_in, mla_q_a_gain=mla_q_a_gain, mla_kv_a_gain=mla_kv_a_gain, w_uq=w_uq,
             w_ukv=w_ukv, mla_q_norm_gain=mla_q_norm_gain, mla_k_norm_gain=mla_k_norm_gain, gdn_conv_w=gdn_conv_w,
             gdn_a_log=gdn_a_log, gdn_dt_bias=gdn_dt_bias, gdn_out_norm_gain=gdn_out_norm_gain, w_out=w_out)
    m = dict(norm_gain=m_norm_gain, w_in=m_w_in, mla_q_a_gain=m_mla_q_a_gain, mla_kv_a_gain=m_mla_kv_a_gain,
             w_uq=m_w_uq, w_ukv=m_w_ukv, mla_q_norm_gain=m_mla_q_norm_gain, mla_k_norm_gain=m_mla_k_norm_gain,
             gdn_conv_w=m_gdn_conv_w, gdn_a_log=m_gdn_a_log, gdn_dt_bias=m_gdn_dt_bias,
             gdn_out_norm_gain=m_gdn_out_norm_gain, w_out=m_w_out)
    v = dict(norm_gain=v_norm_gain, w_in=v_w_in, mla_q_a_gain=v_mla_q_a_gain, mla_kv_a_gain=v_mla_kv_a_gain,
             w_uq=v_w_uq, w_ukv=v_w_ukv, mla_q_norm_gain=v_mla_q_norm_gain, mla_k_norm_gain=v_mla_k_norm_gain,
             gdn_conv_w=v_gdn_conv_w, gdn_a_log=v_gdn_a_log, gdn_dt_bias=v_gdn_dt_bias,
             gdn_out_norm_gain=v_gdn_out_norm_gain, w_out=v_w_out)
    k_me = 2 * lax.axis_index("x") + lax.axis_index("y")

    first_weights = ([w_in[0].astype(BF16), gdn_conv_w[0]], lambda g: (_pad_w_in_blocks(g[0]), _blocks_to_cols(g[1])))
    late_weights = ([w_uq[0].astype(BF16), w_ukv[0].astype(BF16), w_out[0].astype(BF16)],
                    lambda g: (_pad_heads(_blocks_to_cols(g[0])), _blocks_to_cols(g[1]), g[2].reshape(D_MODEL, D_MODEL)))

    pair_sums = []

    def scatter_hook(big):
        pair, pair_bf16 = _rs_pair_stage([
            _w_in_grad_blocks(big["w_in"]), _cols_to_blocks(_unpad_heads(big["w_uq"])),
            _cols_to_blocks(big["w_ukv"]), big["w_out"].reshape(4, 512, D_MODEL)])
        pair_sums.extend(pair)
        return pair_bf16

    sq, grad_x, small, big, received = _local_step(
        x[0], positions[0], loss_target[0], norm_gain, None, mla_q_a_gain, mla_kv_a_gain, None, None,
        mla_q_norm_gain, mla_k_norm_gain, None, gdn_a_log, gdn_dt_bias, gdn_out_norm_gain, None, scatter_hook,
        late_weights, first_weights)

    loss_local = (0.5 / D_MODEL) * jnp.sum(sq)
    pack = jnp.concatenate([_pack_small(small, loss_local), big["gdn_conv_w"].reshape(96, 128)], axis=0)
    tot = _all_reduce_small(pack)
    loss = tot[LOSS_ROW, 0]
    conv_grad = lax.dynamic_slice_in_dim(tot[CONV_ROW:].reshape(4, 3072), k_me * 768, 768, axis=1)

    shard_grads = _rs_chip_stage(pair_sums, received)

    grads = {n: _unpack_small(tot, n) for n in SMALL_ROWS}
    grads["gdn_conv_w"] = conv_grad[None]
    for n, g in zip(BIG, shard_grads):
        grads[n] = g[None]

    delta, new_m, new_v = {}, {}, {}
    sw = _pack_small({n: w[n] for n in SMALL_ROWS})
    sm = _pack_small({n: m[n] for n in SMALL_ROWS})
    sv = _pack_small({n: v[n] for n in SMALL_ROWS})
    sd, snm, snv = _adamw(sw, tot[:SMALL_PACK_ROWS], sm, sv, "adamw_small")
    for n in SMALL_ROWS:
        delta[n], new_m[n], new_v[n] = _unpack_small(sd, n), _unpack_small(snm, n), _unpack_small(snv, n)
    for n in BIG + ["gdn_conv_w"]:
        if n == "w_in":
            d, nm, nv = _adamw(w[n][0].T, grads[n][0].T, m[n][0].T, v[n][0].T, f"adamw_{n}")
            delta[n], new_m[n], new_v[n] = d.T[None], nm.T[None], nv.T[None]
        else:
            d, nm, nv = _adamw(w[n][0], grads[n][0], m[n][0], v[n][0], f"adamw_{n}")
            delta[n], new_m[n], new_v[n] = d[None], nm[None], nv[None]

    return (loss, grad_x[None], *[grads[n] for n in WEIGHTS], *[delta[n] for n in WEIGHTS],
            *[new_m[n] for n in WEIGHTS], *[new_v[n] for n in WEIGHTS])
```

```python
import functools
import math

import jax
import jax.numpy as jnp
from jax import lax
from jax.experimental import pallas as pl
from jax.experimental.pallas import tpu as pltpu

F32 = jnp.float32
BF16 = jnp.bfloat16
MESH = pl.DeviceIdType.MESH

D_MODEL = 2048
HEADS = 8
HEAD_DIM = 128
QK_DIM = 192
QK_PAD = 256
HALF_ROPE = 32
CHUNK = 64
NORM_EPS = 1e-6
W_IN_COLS = 5968
W_IN_PAD = 6144
GA_LANE = 64
GB_LANE = 72
ADAM_LR, ADAM_B1, ADAM_B2, ADAM_EPS, ADAM_WD, ADAM_STEP = 0.001, 0.9, 0.999, 1e-08, 0.01, 10
VMEM_LIMIT_V7X = 52 * 1024 * 1024
HI = lax.Precision.HIGHEST
NN = (((1,), (0,)), ((), ()))
NT = (((1,), (1,)), ((), ()))
TN = (((0,), (0,)), ((), ()))

TILES = {"row": 512, "attn": 2048, "mm": 1024}


def _call(body, *, name, grid, in_specs, out_specs, out_shape, args, scratch=(), sem=None):
    def kfn(*refs):
        body(*refs)
    if sem is None:
        sem = ("arbitrary",) * len(grid)
    return pl.pallas_call(
        kfn, name=name, grid=grid, in_specs=in_specs, out_specs=out_specs, out_shape=out_shape,
        scratch_shapes=list(scratch),
        compiler_params=pltpu.CompilerParams(dimension_semantics=sem, vmem_limit_bytes=VMEM_LIMIT_V7X),
    )(*args)


def _rows(tm, w, cb=0):
    return pl.BlockSpec((tm, w), lambda i: (i, cb))


def _full(shape):
    n = len(shape)
    return pl.BlockSpec(shape, lambda *_: (0,) * n)


def _sds(shape, dtype):
    return jax.ShapeDtypeStruct(shape, dtype)


def _acc8(x):
    tm, c = x.shape
    return jnp.sum(x.reshape(tm // 8, 8, c), axis=0)


def _sigmoid(x):
    return 1.0 / (1.0 + jnp.exp(-x))


def _silu(x):
    return x * _sigmoid(x)


def _dsilu(x):
    s = _sigmoid(x)
    return s * (1.0 + x * (1.0 - s))


def _dot(a, b, dims=NN):
    return lax.dot_general(a.astype(BF16), b.astype(BF16), dims, preferred_element_type=F32)


def _dot_hi(a, b, dims=NN):
    return lax.dot_general(a, b, dims, precision=HI, preferred_element_type=F32)


def _matmul(a, b, *, mode, out_dtype, name, tm=None, tn=None, tk=None):
    if mode == "tn":
        kdim, m = a.shape
    else:
        m, kdim = a.shape
    n = b.shape[0] if mode == "nt" else b.shape[1]
    tm = min(tm or TILES["mm"], m)
    tn = min(tn or TILES["mm"], n)
    tk = min(tk or kdim, kdim)
    nk = kdim // tk
    dims = {"nn": NN, "nt": NT, "tn": TN}[mode]
    if mode == "tn":
        a_spec = pl.BlockSpec((tk, tm), lambda i, j, k: (k, i))
    else:
        a_spec = pl.BlockSpec((tm, tk), lambda i, j, k: (i, k))
    if mode == "nt":
        b_spec = pl.BlockSpec((tn, tk), lambda i, j, k: (j, k))
    else:
        b_spec = pl.BlockSpec((tk, tn), lambda i, j, k: (k, j))

    def body(a_ref, b_ref, o_ref):
        r = _dot(a_ref[...], b_ref[...], dims)
        if nk == 1:
            o_ref[...] = r.astype(o_ref.dtype)
        else:
            k = pl.program_id(2)

            @pl.when(k == 0)
            def _():
                o_ref[...] = r

            @pl.when(k > 0)
            def _():
                o_ref[...] += r

    if nk > 1:
        assert out_dtype == F32
    return _call(body, name=name, grid=(m // tm, n // tn, nk), in_specs=[a_spec, b_spec],
                 out_specs=pl.BlockSpec((tm, tn), lambda i, j, k: (i, j)), out_shape=_sds((m, n), out_dtype),
                 args=(a, b))


def _norm1_fwd(x, gain):
    s = x.shape[0]
    tm = min(TILES["row"], s)

    def body(x_ref, g_ref, o_ref):
        xv = x_ref[...]
        r = lax.rsqrt(jnp.mean(xv * xv, axis=-1, keepdims=True) + NORM_EPS)
        o_ref[...] = (xv * r * g_ref[...]).astype(BF16)

    return _call(body, name="norm1_fwd", grid=(s // tm,), in_specs=[_rows(tm, D_MODEL), _full((1, D_MODEL))],
                 out_specs=_rows(tm, D_MODEL), out_shape=_sds((s, D_MODEL), BF16), args=(x, gain))


def _norm1_bwd(x, gain, dxn, dy):
    s = x.shape[0]
    tm = min(TILES["row"], s)

    def body(x_ref, g_ref, dxn_ref, dy_ref, gx_ref, dg_ref):
        xv = x_ref[...]
        r = lax.rsqrt(jnp.mean(xv * xv, axis=-1, keepdims=True) + NORM_EPS)
        nrm = xv * r
        d = dxn_ref[...].astype(F32)
        dn = d * g_ref[...]
        gx_ref[...] = dy_ref[...].astype(F32) + r * (dn - nrm * jnp.mean(dn * nrm, axis=-1, keepdims=True))

        @pl.when(pl.program_id(0) == 0)
        def _():
            dg_ref[...] = jnp.zeros_like(dg_ref)

        dg_ref[...] += _acc8(d * nrm)

    return _call(body, name="norm1_bwd", grid=(s // tm,),
                 in_specs=[_rows(tm, D_MODEL), _full((1, D_MODEL)), _rows(tm, D_MODEL), _rows(tm, D_MODEL)],
                 out_specs=[_rows(tm, D_MODEL), _full((8, D_MODEL))],
                 out_shape=[_sds((s, D_MODEL), F32), _sds((8, D_MODEL), F32)], args=(x, gain, dxn, dy))


def _rms(xv, width):
    return lax.rsqrt(jnp.sum(xv * xv, axis=-1, keepdims=True) * (1.0 / width) + NORM_EPS)


def _mla_a_norm(proj, gq, gkv):
    s = proj.shape[0]
    tm = min(TILES["row"], s)

    def body(cq_ref, ckv_ref, gq_ref, gkv_ref, oq_ref, okv_ref):
        a = cq_ref[...].astype(F32)
        oq_ref[...] = (a * _rms(a, 512) * gq_ref[...]).astype(BF16)
        b = ckv_ref[...].astype(F32)
        okv_ref[...] = (b * _rms(b, 256) * gkv_ref[...]).astype(BF16)

    return _call(body, name="mla_a_norm", grid=(s // tm,),
                 in_specs=[_rows(tm, 512, 0), _rows(tm, 256, 2), _full((1, 512)), _full((1, 256))],
                 out_specs=[_rows(tm, 512), _rows(tm, 256)],
                 out_shape=[_sds((s, 512), BF16), _sds((s, 256), BF16)], args=(proj, proj, gq, gkv))


def _rms_bwd(xv, gain, d, width):
    r = _rms(xv, width)
    nrm = xv * r
    dn = d * gain
    dx = r * (dn - nrm * (jnp.sum(dn * nrm, axis=-1, keepdims=True) * (1.0 / width)))
    return dx, d * nrm


def _mla_a_norm_bwd(proj, gq, gkv, dcqn, dckvn):
    s = proj.shape[0]
    tm = min(TILES["row"], s)

    def body(cq_ref, ckv_ref, gq_ref, gkv_ref, dq_ref, dkv_ref, oq_ref, okv_ref, aq_ref, akv_ref):
        dxq, gq_part = _rms_bwd(cq_ref[...].astype(F32), gq_ref[...], dq_ref[...].astype(F32), 512)
        dxk, gk_part = _rms_bwd(ckv_ref[...].astype(F32), gkv_ref[...], dkv_ref[...].astype(F32), 256)
        oq_ref[...] = dxq.astype(BF16)
        okv_ref[...] = dxk.astype(BF16)

        @pl.when(pl.program_id(0) == 0)
        def _():
            aq_ref[...] = jnp.zeros_like(aq_ref)
            akv_ref[...] = jnp.zeros_like(akv_ref)

        aq_ref[...] += _acc8(gq_part)
        akv_ref[...] += _acc8(gk_part)

    return _call(body, name="mla_a_norm_bwd", grid=(s // tm,),
                 in_specs=[_rows(tm, 512, 0), _rows(tm, 256, 2), _full((1, 512)), _full((1, 256)),
                           _rows(tm, 512), _rows(tm, 256)],
                 out_specs=[_rows(tm, 512), _rows(tm, 256), _full((8, 512)), _full((8, 256))],
                 out_shape=[_sds((s, 512), BF16), _sds((s, 256), BF16), _sds((8, 512), F32), _sds((8, 256), F32)],
                 args=(proj, proj, gq, gkv, dcqn, dckvn))


def _swap32(r):
    lane = lax.broadcasted_iota(jnp.int32, r.shape, 1)
    return jnp.where(lane < HALF_ROPE, pltpu.roll(r, 128 - HALF_ROPE, 1), pltpu.roll(r, HALF_ROPE, 1))


def _mla_post_fwd(q_pre, kv_pre, proj, cs, sn, gq, gk):
    s = q_pre.shape[0]
    tm = min(TILES["row"], s)

    def body(qp_ref, kvp_ref, misc_ref, cs_ref, sn_ref, gq_ref, gk_ref, q_ref, k_ref, v_ref):
        csv, snv = cs_ref[...], sn_ref[...]
        lane = lax.broadcasted_iota(jnp.int32, (tm, 128), 1)
        kr = jnp.where(lane < 64, misc_ref[...], 0.0)
        for h in range(HEADS):
            for src, g_ref, o_ref in ((None, gq_ref, q_ref), (kr, gk_ref, k_ref)):
                if src is None:
                    xv = qp_ref[:, QK_PAD * h:QK_PAD * (h + 1)].astype(F32)
                else:
                    xv = jnp.concatenate([kvp_ref[:, 256 * h:256 * h + 128].astype(F32), src], axis=-1)
                y = xv * _rms(xv, QK_DIM) * g_ref[...]
                if src is None:
                    y = y * Q_PRESCALE
                hi = y[:, 128:]
                hi = hi * csv + _swap32(hi) * snv
                o_ref[:, QK_PAD * h:QK_PAD * h + 128] = y[:, :128].astype(BF16)
                o_ref[:, QK_PAD * h + 128:QK_PAD * (h + 1)] = hi.astype(BF16)
            v_ref[:, 128 * h:128 * (h + 1)] = kvp_ref[:, 256 * h + 128:256 * (h + 1)].astype(BF16)

    return _call(body, name="mla_post_fwd", grid=(s // tm,),
                 in_specs=[_rows(tm, 2048), _rows(tm, 2048), _rows(tm, 128), _rows(tm, 128), _rows(tm, 128),
                           _full((1, QK_PAD)), _full((1, QK_PAD))],
                 out_specs=[_rows(tm, 2048), _rows(tm, 2048), _rows(tm, 1024)],
                 out_shape=[_sds((s, 2048), BF16), _sds((s, 2048), BF16), _sds((s, 1024), BF16)],
                 args=(q_pre, kv_pre, proj, cs, sn, gq, gk))


def _mla_post_bwd(q_pre, kv_pre, proj, cs, sn, gq, gk, dq, dk, dv):
    s = q_pre.shape[0]
    tm = min(TILES["row"], s)

    def body(qp_ref, kvp_ref, misc_ref, cs_ref, sn_ref, gq_ref, gk_ref, dq_ref, dk_ref, dv_ref,
             oq_ref, okv_ref, okr_ref, agq_ref, agk_ref):
        csv, snv = cs_ref[...], sn_ref[...]
        lane = lax.broadcasted_iota(jnp.int32, (tm, 128), 1)
        kr = jnp.where(lane < 64, misc_ref[...], 0.0)

        @pl.when(pl.program_id(0) == 0)
        def _():
            agq_ref[...] = jnp.zeros_like(agq_ref)
            agk_ref[...] = jnp.zeros_like(agk_ref)

        dkr = jnp.zeros((tm, 128), F32)
        for h in range(HEADS):
            for which in (0, 1):
                if which == 0:
                    xv = qp_ref[:, QK_PAD * h:QK_PAD * (h + 1)].astype(F32)
                    d_ref, g_ref, a_ref = dq_ref, gq_ref, agq_ref
                else:
                    xv = jnp.concatenate([kvp_ref[:, 256 * h:256 * h + 128].astype(F32), kr], axis=-1)
                    d_ref, g_ref, a_ref = dk_ref, gk_ref, agk_ref
                dhi = d_ref[:, QK_PAD * h + 128:QK_PAD * (h + 1)].astype(F32)
                dhi = dhi * csv - _swap32(dhi) * snv
                dyv = jnp.concatenate([d_ref[:, QK_PAD * h:QK_PAD * h + 128].astype(F32), dhi], axis=-1)
                if which == 0:
                    dyv = dyv * ATTN_SCALE
                dx, gpart = _rms_bwd(xv, g_ref[...], dyv, QK_DIM)
                a_ref[...] += _acc8(gpart)
                if which == 0:
                    oq_ref[:, QK_PAD * h:QK_PAD * (h + 1)] = dx.astype(BF16)
                else:
                    okv_ref[:, 256 * h:256 * h + 128] = dx[:, :128].astype(BF16)
                    dkr = dkr + dx[:, 128:]
            okv_ref[:, 256 * h + 128:256 * (h + 1)] = dv_ref[:, 128 * h:128 * (h + 1)].astype(BF16)
        okr_ref[...] = dkr

    return _call(body, name="mla_post_bwd", grid=(s // tm,),
                 in_specs=[_rows(tm, 2048), _rows(tm, 2048), _rows(tm, 128), _rows(tm, 128), _rows(tm, 128),
                           _full((1, QK_PAD)), _full((1, QK_PAD)), _rows(tm, 2048), _rows(tm, 2048), _rows(tm, 1024)],
                 out_specs=[_rows(tm, 2048), _rows(tm, 2048), _rows(tm, 128), _full((8, QK_PAD)), _full((8, QK_PAD))],
                 out_shape=[_sds((s, 2048), BF16), _sds((s, 2048), BF16), _sds((s, 128), F32),
                            _sds((8, QK_PAD), F32), _sds((8, QK_PAD), F32)],
                 args=(q_pre, kv_pre, proj, cs, sn, gq, gk, dq, dk, dv))


ATTN_SCALE = QK_DIM ** -0.5
NEG = -1e30


LOG2E = 1.4426950408889634
LN2 = 0.6931471805599453
Q_PRESCALE = ATTN_SCALE * LOG2E
ATTN_SUB_FWD = 512
ATTN_SUB_BWD = 256


def _causal_pairs(nq, kv_major):
    prs = [(i, j) for i in range(nq) for j in range(i + 1)]
    if kv_major:
        prs.sort(key=lambda ij: (ij[1], ij[0]))
    return (jnp.asarray([p[0] for p in prs], jnp.int32), jnp.asarray([p[1] for p in prs], jnp.int32))


def _pair_call(body, *, name, tables, in_specs, out_specs, out_shape, scratch, args):
    def kfn(*refs):
        body(*refs)
    spec = pltpu.PrefetchScalarGridSpec(num_scalar_prefetch=2, grid=(HEADS, tables[0].shape[0]), in_specs=in_specs,
                                        out_specs=out_specs, scratch_shapes=list(scratch))
    return pl.pallas_call(
        kfn, name=name, grid_spec=spec, out_shape=out_shape,
        compiler_params=pltpu.CompilerParams(dimension_semantics=("parallel", "arbitrary"),
                                             vmem_limit_bytes=VMEM_LIMIT_V7X))(*tables, *args)


def _diag_mask(sc, ts, qs):
    row = lax.broadcasted_iota(jnp.int32, sc.shape, 0) + qs * ts
    col = lax.broadcasted_iota(jnp.int32, sc.shape, 1)
    return jnp.where(col <= row, sc, NEG)


def _attn_fwd(q, k, v):
    s = q.shape[0]
    t = min(TILES["attn"], s)
    ts = min(ATTN_SUB_FWD, t)
    nq = s // t

    def slabs(q_ref, k_ref, v_ref, m_s, l_s, acc_s, diag):
        def scores(qs):
            kw = (qs + 1) * ts if diag else t
            sc = lax.dot_general(q_ref[qs * ts:(qs + 1) * ts, :], k_ref[0:kw, :], NT, preferred_element_type=F32)
            return _diag_mask(sc, ts, qs) if diag else sc

        nsub = t // ts
        sc_next = scores(0)
        for qs in range(nsub):
            rq = slice(qs * ts, (qs + 1) * ts)
            kw = (qs + 1) * ts if diag else t
            sc = sc_next
            if qs + 1 < nsub:
                sc_next = scores(qs + 1)
            m_prev = m_s[rq, :]
            m_new = jnp.maximum(m_prev, jnp.max(sc, axis=-1, keepdims=True))
            p = jnp.exp2(sc - m_new)
            alpha = jnp.exp2(m_prev - m_new)
            l_s[rq, :] = alpha * l_s[rq, :] + jnp.sum(p, axis=-1, keepdims=True)
            acc_s[rq, :] = acc_s[rq, :] * alpha + lax.dot_general(p.astype(BF16), v_ref[0:kw, :], NN,
                                                                  preferred_element_type=F32)
            m_s[rq, :] = m_new

    def body(it_ref, jt_ref, q_ref, k_ref, v_ref, o_ref, lse_ref, m_s, l_s, acc_s):
        p = pl.program_id(1)
        i, j = it_ref[p], jt_ref[p]

        @pl.when(j == 0)
        def _():
            m_s[...] = jnp.full_like(m_s, NEG)
            l_s[...] = jnp.zeros_like(l_s)
            acc_s[...] = jnp.zeros_like(acc_s)

        @pl.when(j < i)
        def _():
            slabs(q_ref, k_ref, v_ref, m_s, l_s, acc_s, False)

        @pl.when(j == i)
        def _():
            slabs(q_ref, k_ref, v_ref, m_s, l_s, acc_s, True)
            o_ref[...] = acc_s[...] / l_s[...]
            lse_ref[...] = m_s[...] + jnp.log2(l_s[...])

    qb = lambda h, p, it, jt: (it[p], h)
    kb = lambda h, p, it, jt: (jt[p], h)
    return _pair_call(
        body, name="attn_fwd", tables=_causal_pairs(nq, kv_major=False),
        in_specs=[pl.BlockSpec((t, QK_PAD), qb), pl.BlockSpec((t, QK_PAD), kb), pl.BlockSpec((t, HEAD_DIM), kb)],
        out_specs=[pl.BlockSpec((t, HEAD_DIM), qb),
                   pl.BlockSpec((None, t, 1), lambda h, p, it, jt: (h, it[p], 0))],
        out_shape=[_sds((s, HEADS * HEAD_DIM), F32), _sds((HEADS, s, 1), F32)],
        scratch=[pltpu.VMEM((t, 1), F32), pltpu.VMEM((t, 1), F32), pltpu.VMEM((t, HEAD_DIM), F32)],
        args=(q, k, v))


def _attn_bwd(q, k, v, lse_rows, delta_rows, do):
    s = q.shape[0]
    t = min(TILES["attn"], s)
    ts = min(ATTN_SUB_BWD, t)
    nq = s // t

    def slabs(q_ref, k_ref, v_ref, lse_ref, delta_ref, do_ref, dq_ref, dk_ref, dv_ref, i, diag):
        def products(qs):
            rq = slice(qs * ts, (qs + 1) * ts)
            kw = (qs + 1) * ts if diag else t
            qv, dob = q_ref[rq, :], do_ref[rq, :]
            sct = lax.dot_general(k_ref[0:kw, :], qv, NT, preferred_element_type=F32)
            dpt = lax.dot_general(v_ref[0:kw, :], dob, NT, preferred_element_type=F32)
            if diag:
                row = lax.broadcasted_iota(jnp.int32, sct.shape, 0)
                col = lax.broadcasted_iota(jnp.int32, sct.shape, 1) + qs * ts
                sct = jnp.where(row <= col, sct, NEG)
            return qv, dob, sct, dpt

        nsub = t // ts
        ahead = products(0)
        for qs in range(nsub):
            rq = slice(qs * ts, (qs + 1) * ts)
            kw = (qs + 1) * ts if diag else t
            qv, dob, sct, dpt = ahead
            if qs + 1 < nsub:
                ahead = products(qs + 1)
            pt = jnp.exp2(sct - lse_ref[:, rq])
            dv_ref[0:kw, :] += lax.dot_general(pt.astype(BF16), dob, NN, preferred_element_type=F32)
            dst = (pt * (dpt - delta_ref[:, rq])).astype(BF16)
            dk_ref[0:kw, :] += lax.dot_general(dst, qv, NN, preferred_element_type=F32)
            rows = pl.ds(pl.multiple_of(i * t + qs * ts, ts), ts)
            dq_ref[rows, :] += lax.dot_general(dst, k_ref[0:kw, :], TN, preferred_element_type=F32)

    npairs = nq * (nq + 1) // 2

    def body(it_ref, jt_ref, q_ref, k_ref, v_ref, lse_ref, delta_ref, do_ref, dq_ref, dk_ref, dv_ref,
             dq_acc, dk_acc, dv_acc):
        p = pl.program_id(1)
        i, j = it_ref[p], jt_ref[p]
        refs = (q_ref, k_ref, v_ref, lse_ref, delta_ref, do_ref, dq_acc, dk_acc, dv_acc)

        @pl.when(p == 0)
        def _():
            dq_acc[...] = jnp.zeros_like(dq_acc)

        @pl.when(i == j)
        def _():
            dk_acc[...] = jnp.zeros_like(dk_acc)
            dv_acc[...] = jnp.zeros_like(dv_acc)

        @pl.when(i > j)
        def _():
            slabs(*refs, i, False)

        @pl.when(i == j)
        def _():
            slabs(*refs, i, True)

        @pl.when(i == nq - 1)
        def _():
            dk_ref[...] = (dk_acc[...] * LN2).astype(BF16)
            dv_ref[...] = dv_acc[...].astype(BF16)

        @pl.when(p == npairs - 1)
        def _():
            dq_ref[...] = dq_acc[...].astype(BF16)

    qb = lambda h, p, it, jt: (it[p], h)
    kb = lambda h, p, it, jt: (jt[p], h)
    rowb = pl.BlockSpec((None, 1, t), lambda h, p, it, jt: (h, 0, it[p]))
    return _pair_call(
        body, name="attn_bwd", tables=_causal_pairs(nq, kv_major=True),
        in_specs=[pl.BlockSpec((t, QK_PAD), qb), pl.BlockSpec((t, QK_PAD), kb), pl.BlockSpec((t, HEAD_DIM), kb),
                  rowb, rowb, pl.BlockSpec((t, HEAD_DIM), qb)],
        out_specs=[pl.BlockSpec((s, QK_PAD), lambda h, p, it, jt: (0, h)), pl.BlockSpec((t, QK_PAD), kb),
                   pl.BlockSpec((t, HEAD_DIM), kb)],
        out_shape=[_sds((s, HEADS * QK_PAD), BF16), _sds((s, HEADS * QK_PAD), BF16), _sds((s, HEADS * HEAD_DIM), BF16)],
        scratch=[pltpu.VMEM((s, QK_PAD), F32), pltpu.VMEM((t, QK_PAD), F32), pltpu.VMEM((t, HEAD_DIM), F32)],
        args=(q, k, v, lse_rows, delta_rows, do))


GDN_Q_SCALE = HEAD_DIM ** -0.5


def _shift_down(xv, prev8, sft):
    rolled = pltpu.roll(xv, sft, 0)
    top = pltpu.roll(jnp.concatenate([prev8, xv[:8]], axis=0), sft, 0)[8:]
    return jnp.concatenate([top, rolled[8:]], axis=0)


def _shift_up(xv, next8, sft):
    tm = xv.shape[0]
    rolled = pltpu.roll(xv, tm - sft, 0)
    bot = pltpu.roll(jnp.concatenate([xv[tm - 8:], next8], axis=0), 16 - sft, 0)[:8]
    return jnp.concatenate([rolled[:tm - 8], bot], axis=0)


def _conv_z(xv, prev8, w_ref):
    z = xv * w_ref[3:4, :]
    for sft in (1, 2, 3):
        z = z + _shift_down(xv, prev8, sft) * w_ref[3 - sft:4 - sft, :]
    return z


def _conv_specs(s, tm):
    nb16 = tm // 16
    cur = pl.BlockSpec((tm, 1024), lambda j, i: (i, 2 + j))
    prev = pl.BlockSpec((16, 1024), lambda j, i: (jnp.maximum(i * nb16 - 1, 0), 2 + j))
    return cur, prev


def _prev8(xp_ref, i):
    return jnp.where(i > 0, xp_ref[...].astype(F32)[8:], 0.0)


def _gdn_conv_fwd(proj, conv_w):
    s = proj.shape[0]
    tm = min(TILES["row"], s)
    cur, prev = _conv_specs(s, tm)

    def body(x_ref, xp_ref, w_ref, o_ref):
        j, i = pl.program_id(0), pl.program_id(1)
        a = _silu(_conv_z(x_ref[...].astype(F32), _prev8(xp_ref, i), w_ref))
        qk_scale = jnp.where(j == 0, GDN_Q_SCALE, 1.0)
        for h in range(HEADS):
            seg = a[:, 128 * h:128 * (h + 1)]
            r = lax.rsqrt(jnp.sum(seg * seg, axis=-1, keepdims=True) + NORM_EPS)
            o_ref[:, 128 * h:128 * (h + 1)] = jnp.where(j < 2, seg * r * qk_scale, seg).astype(BF16)

    return _call(body, name="gdn_conv_fwd", grid=(3, s // tm),
                 in_specs=[cur, prev, pl.BlockSpec((4, 1024), lambda j, i: (0, j))],
                 out_specs=pl.BlockSpec((tm, 1024), lambda j, i: (i, j)), out_shape=_sds((s, 3072), BF16),
                 args=(proj, proj, conv_w))


def _gdn_conv_bwd_a(proj, conv_w, dqkv):
    s = proj.shape[0]
    tm = min(TILES["row"], s)
    cur, prev = _conv_specs(s, tm)

    def body(x_ref, xp_ref, w_ref, d_ref, o_ref):
        j, i = pl.program_id(0), pl.program_id(1)
        z = _conv_z(x_ref[...].astype(F32), _prev8(xp_ref, i), w_ref)
        a = _silu(z)
        dsl = _dsilu(z)
        qk_scale = jnp.where(j == 0, GDN_Q_SCALE, 1.0)
        for h in range(HEADS):
            sl = slice(128 * h, 128 * (h + 1))
            seg = a[:, sl]
            dyv = d_ref[:, sl].astype(F32)
            r = lax.rsqrt(jnp.sum(seg * seg, axis=-1, keepdims=True) + NORM_EPS)
            yh = seg * r
            da_n = qk_scale * r * (dyv - yh * jnp.sum(yh * dyv, axis=-1, keepdims=True))
            o_ref[:, sl] = (jnp.where(j < 2, da_n, dyv) * dsl[:, sl]).astype(BF16)

    return _call(body, name="gdn_conv_bwd_a", grid=(3, s // tm),
                 in_specs=[cur, prev, pl.BlockSpec((4, 1024), lambda j, i: (0, j)),
                           pl.BlockSpec((tm, 1024), lambda j, i: (i, j))],
                 out_specs=pl.BlockSpec((tm, 1024), lambda j, i: (i, j)), out_shape=_sds((s, 3072), BF16),
                 args=(proj, proj, conv_w, dqkv))


def _gdn_conv_bwd_b(proj, conv_w, dz):
    s = proj.shape[0]
    tm = min(TILES["row"], s)
    nb16 = tm // 16
    last16 = s // 16 - 1
    cur, prev = _conv_specs(s, tm)

    def body(x_ref, w_ref, dz_ref, dzn_ref, dx_ref, dw_ref):
        i = pl.program_id(1)
        next8 = jnp.where(i < pl.num_programs(1) - 1, dzn_ref[...].astype(F32)[:8], 0.0)
        xv, dzv = x_ref[...].astype(F32), dz_ref[...].astype(F32)

        @pl.when(i == 0)
        def _():
            dw_ref[...] = jnp.zeros_like(dw_ref)

        dx = dzv * w_ref[3:4, :]
        dw_ref[3:4, :] += jnp.sum(dzv * xv, axis=0, keepdims=True)
        for sft in (1, 2, 3):
            up = _shift_up(dzv, next8, sft)
            dx = dx + up * w_ref[3 - sft:4 - sft, :]
            dw_ref[3 - sft:4 - sft, :] += jnp.sum(up * xv, axis=0, keepdims=True)
        dx_ref[...] = dx.astype(BF16)

    return _call(body, name="gdn_conv_bwd_b", grid=(3, s // tm),
                 in_specs=[cur, pl.BlockSpec((4, 1024), lambda j, i: (0, j)),
                           pl.BlockSpec((tm, 1024), lambda j, i: (i, j)),
                           pl.BlockSpec((16, 1024), lambda j, i: (jnp.minimum((i + 1) * nb16, last16), j))],
                 out_specs=[pl.BlockSpec((tm, 1024), lambda j, i: (i, j)), pl.BlockSpec((4, 1024), lambda j, i: (0, j))],
                 out_shape=[_sds((s, 3072), BF16), _sds((4, 3072), F32)], args=(proj, conv_w, dz, dz))


def _softplus(xv):
    return jnp.maximum(xv, 0.0) + jnp.log(1.0 + jnp.exp(-jnp.abs(xv)))


def _gdn_gates_fwd(proj, alog128, dtb128):
    s = proj.shape[0]
    tm = min(TILES["row"], s)

    def body(m_ref, a_ref, b_ref, o_ref):
        mv = m_ref[...]
        lane = lax.broadcasted_iota(jnp.int32, mv.shape, 1)
        g = -jnp.exp(a_ref[...]) * _softplus(mv + b_ref[...])
        is_g = (lane >= GA_LANE) & (lane < GA_LANE + HEADS)
        is_b = (lane >= GB_LANE) & (lane < GB_LANE + HEADS)
        o_ref[...] = jnp.where(is_g, g, jnp.where(is_b, _sigmoid(mv), 0.0))

    return _call(body, name="gdn_gates_fwd", grid=(s // tm,),
                 in_specs=[_rows(tm, 128), _full((1, 128)), _full((1, 128))],
                 out_specs=_rows(tm, 128), out_shape=_sds((s, 128), F32), args=(proj, alog128, dtb128))


def _gdn_gates_bwd(proj, alog128, dtb128, gbeta, dgbeta, dkr):
    s = proj.shape[0]
    tm = min(TILES["row"], s)

    def body(m_ref, a_ref, b_ref, gb_ref, d_ref, kr_ref, o_ref, da_ref, db_ref):
        mv, dv = m_ref[...], d_ref[...]
        lane = lax.broadcasted_iota(jnp.int32, mv.shape, 1)
        is_g = (lane >= GA_LANE) & (lane < GA_LANE + HEADS)
        is_b = (lane >= GB_LANE) & (lane < GB_LANE + HEADS)
        dga = jnp.where(is_g, dv * (-jnp.exp(a_ref[...])) * _sigmoid(mv + b_ref[...]), 0.0)
        beta = gb_ref[...]
        dgb = jnp.where(is_b, dv * beta * (1.0 - beta), 0.0)
        o_ref[...] = jnp.where(lane < 64, kr_ref[...], dga + dgb).astype(BF16)

        @pl.when(pl.program_id(0) == 0)
        def _():
            da_ref[...] = jnp.zeros_like(da_ref)
            db_ref[...] = jnp.zeros_like(db_ref)

        da_ref[...] += _acc8(jnp.where(is_g, dv * gb_ref[...], 0.0))
        db_ref[...] += _acc8(dga)

    return _call(body, name="gdn_gates_bwd", grid=(s // tm,),
                 in_specs=[_rows(tm, 128), _full((1, 128)), _full((1, 128)), _rows(tm, 128), _rows(tm, 128),
                           _rows(tm, 128)],
                 out_specs=[_rows(tm, 128), _full((8, 128)), _full((8, 128))],
                 out_shape=[_sds((s, 128), BF16), _sds((8, 128), F32), _sds((8, 128), F32)],
                 args=(proj, alog128, dtb128, gbeta, dgbeta, dkr))


def _col(mat, lane_idx, lane):
    return jnp.sum(jnp.where(lane == lane_idx, mat, 0.0), axis=-1, keepdims=True)


def _chunk_local(qh, kh, vh, gcol, bcol, ii, jj):
    lower, strict, eye = ii >= jj, ii > jj, ii == jj
    grow = jnp.sum(jnp.where(eye, gcol, 0.0), axis=0, keepdims=True)
    decay = jnp.where(lower, jnp.exp(jnp.where(lower, gcol - grow, 0.0)), 0.0)
    kb = kh * bcol
    vb = vh * bcol
    mm = _dot(kb, kh, NT)
    lmat = jnp.where(strict, mm * decay, 0.0)
    pw = -lmat
    tinv = jnp.where(eye, 1.0, 0.0) + pw
    for _ in range(5):
        pw = _dot_hi(pw, pw)
        tinv = tinv + _dot_hi(tinv, pw)
    egc = jnp.exp(gcol)
    kbg = kb * egc
    rhs = jnp.concatenate([vb, kbg], axis=-1)
    sol = _dot_hi(tinv, rhs)
    qk = _dot(qh, kh, NT)
    glast = jnp.sum(jnp.where(ii[:, :1] == CHUNK - 1, gcol, 0.0), axis=0, keepdims=True)
    ekd = jnp.exp(glast - gcol)
    return dict(decay=decay, kb=kb, vb=vb, mm=mm, lmat=lmat, tinv=tinv, egc=egc, kbg=kbg, rhs=rhs,
                u=sol[:, :HEAD_DIM], w=sol[:, HEAD_DIM:], qk=qk, amat=qk * decay, qd=qh * egc, ekd=ekd,
                kd=kh * ekd, gl=jnp.exp(glast), strict=strict, lower=lower, eye=eye)


def _tri(ii, jj):
    return jnp.where(ii >= jj, 1.0, 0.0)


def _gdn_fwd(qkv, gbeta):
    s = qkv.shape[0]
    n = s // CHUNK

    def body(qkv_ref, gb_ref, o_ref, st_ref, state):
        @pl.when(pl.program_id(0) == 0)
        def _():
            state[...] = jnp.zeros_like(state)

        ii = lax.broadcasted_iota(jnp.int32, (CHUNK, CHUNK), 0)
        jj = lax.broadcasted_iota(jnp.int32, (CHUNK, CHUNK), 1)
        lane = lax.broadcasted_iota(jnp.int32, (CHUNK, 128), 1)
        gbv = gb_ref[...]
        gc = _dot_hi(_tri(ii, jj), gbv)
        for h in range(HEADS):
            sl = slice(128 * h, 128 * (h + 1))
            qh = qkv_ref[:, 128 * h:128 * (h + 1)]
            kh = qkv_ref[:, 1024 + 128 * h:1024 + 128 * (h + 1)]
            vh = qkv_ref[:, 2048 + 128 * h:2048 + 128 * (h + 1)]
            c = _chunk_local(qh, kh, vh, _col(gc, GA_LANE + h, lane), _col(gbv, GB_LANE + h, lane), ii, jj)
            st = state[sl, :]
            st_ref[sl, :] = st
            vn = c["u"] - _dot(c["w"], st)
            o_ref[:, sl] = _dot(c["qd"], st) + _dot(c["amat"], vn)
            state[sl, :] = st * c["gl"] + _dot(c["kd"], vn, TN)

    return _call(body, name="gdn_fwd", grid=(n,),
                 in_specs=[_rows(CHUNK, 3072), _rows(CHUNK, 128)],
                 out_specs=[_rows(CHUNK, 1024), _rows(HEADS * 128, 128)],
                 out_shape=[_sds((s, 1024), F32), _sds((n * HEADS * 128, 128), F32)],
                 scratch=[pltpu.VMEM((HEADS * 128, 128), F32)], args=(qkv, gbeta))


def _gdn_bwd(qkv, gbeta, states, do):
    s = qkv.shape[0]
    n = s // CHUNK

    def body(qkv_ref, gb_ref, st_ref, do_ref, dqkv_ref, dgb_ref, dstate):
        @pl.when(pl.program_id(0) == 0)
        def _():
            dstate[...] = jnp.zeros_like(dstate)

        ii = lax.broadcasted_iota(jnp.int32, (CHUNK, CHUNK), 0)
        jj = lax.broadcasted_iota(jnp.int32, (CHUNK, CHUNK), 1)
        lane = lax.broadcasted_iota(jnp.int32, (CHUNK, 128), 1)
        row1 = ii[:, :1]
        gbv = gb_ref[...]
        gc = _dot_hi(_tri(ii, jj), gbv)
        dgc_all = jnp.zeros((CHUNK, 128), F32)
        db_all = jnp.zeros((CHUNK, 128), F32)
        for h in range(HEADS):
            sl = slice(128 * h, 128 * (h + 1))
            qh = qkv_ref[:, 128 * h:128 * (h + 1)]
            kh = qkv_ref[:, 1024 + 128 * h:1024 + 128 * (h + 1)]
            vh = qkv_ref[:, 2048 + 128 * h:2048 + 128 * (h + 1)]
            bcol = _col(gbv, GB_LANE + h, lane)
            c = _chunk_local(qh, kh, vh, _col(gc, GA_LANE + h, lane), bcol, ii, jj)
            st = st_ref[sl, :]
            dst = dstate[sl, :]
            dov = do_ref[:, sl]
            vn = c["u"] - _dot(c["w"], st)
            dvn = _dot(c["amat"], dov, TN) + _dot(c["kd"], dst)
            damat = jnp.where(c["lower"], _dot(dov, vn, NT), 0.0)
            dqd = _dot(dov, st, NT)
            dkd = _dot(vn, dst, NT)
            dw = -_dot(dvn, st, NT)
            dgl = jnp.sum(jnp.sum(st * dst, axis=-1, keepdims=True), axis=0, keepdims=True)
            dstate[sl, :] = _dot(c["qd"], dov, TN) + c["gl"] * dst - _dot(c["w"], dvn, TN)
            dsol = jnp.concatenate([dvn, dw], axis=-1)
            drhs = _dot_hi(c["tinv"], dsol, TN)
            dtinv = _dot_hi(dsol, c["rhs"], NT)
            dl = -_dot_hi(_dot_hi(c["tinv"], dtinv, TN), c["tinv"], NT)
            dl = jnp.where(c["strict"], dl, 0.0)
            dmm = dl * c["decay"]
            dqk = damat * c["decay"]
            wmat = dl * c["lmat"] + damat * c["amat"]
            dgc = jnp.sum(wmat, axis=-1, keepdims=True)
            wcol = jnp.sum(wmat, axis=0, keepdims=True)
            dgc = dgc - jnp.sum(jnp.where(c["eye"], wcol, 0.0), axis=-1, keepdims=True)
            dkb = _dot(dmm, kh) + drhs[:, HEAD_DIM:] * c["egc"]
            dk = _dot(dmm, c["kb"], TN) + _dot(dqk, qh, TN) + dkd * c["ekd"]
            dq = _dot(dqk, kh) + dqd * c["egc"]
            dgc = dgc + jnp.sum(drhs[:, HEAD_DIM:] * c["kbg"], axis=-1, keepdims=True)
            dgc = dgc + jnp.sum(dqd * c["qd"], axis=-1, keepdims=True)
            tmp = jnp.sum(dkd * c["kd"], axis=-1, keepdims=True)
            dgc = dgc - tmp
            dglast = jnp.sum(tmp, axis=0, keepdims=True) + dgl * c["gl"]
            dgc = dgc + jnp.where(row1 == CHUNK - 1, dglast, 0.0)
            dk = dk + dkb * bcol
            db = jnp.sum(dkb * kh, axis=-1, keepdims=True) + jnp.sum(drhs[:, :HEAD_DIM] * vh, axis=-1, keepdims=True)
            dqkv_ref[:, 128 * h:128 * (h + 1)] = dq
            dqkv_ref[:, 1024 + 128 * h:1024 + 128 * (h + 1)] = dk
            dqkv_ref[:, 2048 + 128 * h:2048 + 128 * (h + 1)] = drhs[:, :HEAD_DIM] * bcol
            dgc_all = dgc_all + jnp.where(lane == GA_LANE + h, dgc, 0.0)
            db_all = db_all + jnp.where(lane == GB_LANE + h, db, 0.0)
        dgb_ref[...] = _dot_hi(_tri(jj, ii), dgc_all) + db_all

    rev = lambda w: pl.BlockSpec((CHUNK, w), lambda i: (n - 1 - i, 0))
    return _call(body, name="gdn_bwd", grid=(n,),
                 in_specs=[rev(3072), rev(128), pl.BlockSpec((HEADS * 128, 128), lambda i: (n - 1 - i, 0)), rev(1024)],
                 out_specs=[rev(3072), rev(128)],
                 out_shape=[_sds((s, 3072), F32), _sds((s, 128), F32)],
                 scratch=[pltpu.VMEM((HEADS * 128, 128), F32)], args=(qkv, gbeta, states, do))


NN_B = (((2,), (1,)), ((0,), (0,)))
NT_B = (((2,), (2,)), ((0,), (0,)))
TN_B = (((1,), (1,)), ((0,), (0,)))
GDN_PRE_CHUNKS = 4
GDN_POST_CHUNKS = 2
GDN_SEQ_CHUNKS = 8


def _gather_heads(qkv_ref, gc, gbv, qs, ks, vs, gs, bs, nchunks):
    lane = lax.broadcasted_iota(jnp.int32, (CHUNK, 128), 1)
    for c in range(nchunks):
        rows = slice(CHUNK * c, CHUNK * (c + 1))
        for h in range(HEADS):
            b = HEADS * c + h
            qs[b] = qkv_ref[rows, 128 * h:128 * (h + 1)].astype(F32)
            ks[b] = qkv_ref[rows, 1024 + 128 * h:1024 + 128 * (h + 1)].astype(F32)
            vs[b] = qkv_ref[rows, 2048 + 128 * h:2048 + 128 * (h + 1)].astype(F32)
            gs[b] = jnp.broadcast_to(_col(gc[rows], GA_LANE + h, lane), (CHUNK, 128))
            bs[b] = jnp.broadcast_to(_col(gbv[rows], GB_LANE + h, lane), (CHUNK, 128))


def _block_tri(rows, transpose=False):
    ri = lax.broadcasted_iota(jnp.int32, (rows, rows), 0)
    ci = lax.broadcasted_iota(jnp.int32, (rows, rows), 1)
    same = (ri >> 6) == (ci >> 6)
    return jnp.where(same & ((ci >= ri) if transpose else (ri >= ci)), 1.0, 0.0)


def _local_b(q, k, v, g128, b128):
    ii = lax.broadcasted_iota(jnp.int32, (1, CHUNK, CHUNK), 1)
    jj = lax.broadcasted_iota(jnp.int32, (1, CHUNK, CHUNK), 2)
    lower, strict, eye = ii >= jj, ii > jj, ii == jj
    g64 = g128[:, :, :CHUNK]
    grow = jnp.sum(jnp.where(eye, g64, 0.0), axis=1, keepdims=True)
    decay = jnp.where(lower, jnp.exp(jnp.where(lower, g64 - grow, 0.0)), 0.0)
    kb = k * b128
    vb = v * b128
    mm = lax.dot_general(kb.astype(BF16), k.astype(BF16), NT_B, preferred_element_type=F32)
    lmat = jnp.where(strict, mm * decay, 0.0)
    egc = jnp.exp(g128)
    kbg = kb * egc
    qk = lax.dot_general(q.astype(BF16), k.astype(BF16), NT_B, preferred_element_type=F32)
    row = lax.broadcasted_iota(jnp.int32, (1, CHUNK, 128), 1)
    glast = jnp.sum(jnp.where(row == CHUNK - 1, g128, 0.0), axis=1, keepdims=True)
    ekd = jnp.exp(glast - g128)
    return dict(decay=decay, kb=kb, vb=vb, lmat=lmat, egc=egc, kbg=kbg, amat=qk * decay, qd=q * egc, ekd=ekd,
                kd=k * ekd, gl=jnp.exp(glast), lower=lower, strict=strict, eye=eye)


def _bdot(a, b, dims):
    return lax.dot_general(a.astype(BF16), b.astype(BF16), dims, preferred_element_type=F32)


def _split(a):
    hi = a.astype(BF16)
    return hi, (a - hi.astype(F32)).astype(BF16)


def _bdot_hi(a, b, dims):
    ah, al = _split(a)
    bh, bl = _split(b)
    d = lambda x, y: lax.dot_general(x, y, dims, preferred_element_type=F32)
    return d(ah, bh) + d(ah, bl) + d(al, bh)


def _gdn_pre(qkv, gbeta):
    s = qkv.shape[0]
    n = s // CHUNK
    cb = min(GDN_PRE_CHUNKS, n)
    nb = cb * HEADS
    rows = cb * CHUNK

    def body(qkv_ref, gb_ref, u_ref, w_ref, qd_ref, kd_ref, a_ref, t_ref, gl_ref, qs, ks, vs, gs, bs):
        gbv = gb_ref[...]
        gc = _dot_hi(_block_tri(rows), gbv)
        _gather_heads(qkv_ref, gc, gbv, qs, ks, vs, gs, bs, cb)
        c = _local_b(qs[...], ks[...], vs[...], gs[...], bs[...])
        pw = -c["lmat"]
        tinv = jnp.where(c["eye"], 1.0, 0.0) + pw
        for _ in range(5):
            pw = _bdot_hi(pw, pw, NN_B)
            tinv = tinv + _bdot_hi(tinv, pw, NN_B)
        u_ref[...] = _bdot_hi(tinv, c["vb"], NN_B)
        w_ref[...] = _bdot_hi(tinv, c["kbg"], NN_B).astype(BF16)
        qd_ref[...] = c["qd"].astype(BF16)
        kd_ref[...] = c["kd"].astype(BF16)
        a_ref[...] = c["amat"].astype(BF16)
        t_ref[...] = tinv
        gl_ref[...] = c["gl"]

    b3 = lambda d: pl.BlockSpec((nb, CHUNK, d), lambda i: (i, 0, 0))
    nt = n * HEADS
    return _call(body, name="gdn_pre", grid=(n // cb,),
                 in_specs=[_rows(rows, 3072), _rows(rows, 128)],
                 out_specs=[b3(128), b3(128), b3(128), b3(128), b3(CHUNK), b3(CHUNK),
                            pl.BlockSpec((nb, 1, 128), lambda i: (i, 0, 0))],
                 out_shape=[_sds((nt, CHUNK, 128), F32), _sds((nt, CHUNK, 128), BF16), _sds((nt, CHUNK, 128), BF16),
                            _sds((nt, CHUNK, 128), BF16), _sds((nt, CHUNK, CHUNK), BF16), _sds((nt, CHUNK, CHUNK), F32),
                            _sds((nt, 1, 128), F32)],
                 scratch=[pltpu.VMEM((nb, CHUNK, 128), F32)] * 5, sem=("parallel",), args=(qkv, gbeta))


def _gdn_scan_fwd(u, w, qd, kd, amat, gl):
    nt = u.shape[0]
    n = nt // HEADS
    cs = min(GDN_SEQ_CHUNKS, n)

    def body(u_ref, w_ref, qd_ref, kd_ref, a_ref, gl_ref, o_ref, st_ref, state):
        @pl.when(pl.program_id(0) == 0)
        def _():
            state[...] = jnp.zeros_like(state)

        for c in range(cs):
            sl = slice(HEADS * c, HEADS * (c + 1))
            st = state[...]
            stb = st.astype(BF16)
            st_ref[sl] = stb
            vn = u_ref[sl] - lax.dot_general(w_ref[sl], stb, NN_B, preferred_element_type=F32)
            vnb = vn.astype(BF16)
            o = (lax.dot_general(qd_ref[sl], stb, NN_B, preferred_element_type=F32)
                 + lax.dot_general(a_ref[sl], vnb, NN_B, preferred_element_type=F32))
            state[...] = st * gl_ref[sl] + lax.dot_general(kd_ref[sl], vnb, TN_B, preferred_element_type=F32)
            for h in range(HEADS):
                o_ref[CHUNK * c:CHUNK * (c + 1), 128 * h:128 * (h + 1)] = o[h]

    b3 = lambda d: pl.BlockSpec((cs * HEADS, CHUNK, d), lambda i: (i, 0, 0))
    return _call(body, name="gdn_scan_fwd", grid=(n // cs,),
                 in_specs=[b3(128), b3(128), b3(128), b3(128), b3(CHUNK), pl.BlockSpec((cs * HEADS, 1, 128), lambda i: (i, 0, 0))],
                 out_specs=[_rows(cs * CHUNK, 1024), pl.BlockSpec((cs * HEADS, 128, 128), lambda i: (i, 0, 0))],
                 out_shape=[_sds((n * CHUNK, 1024), F32), _sds((nt, 128, 128), BF16)],
                 scratch=[pltpu.VMEM((HEADS, 128, 128), F32)], args=(u, w, qd, kd, amat, gl))


def _gdn_scan_bwd(w, qd, kd, amat, gl, do):
    nt = w.shape[0]
    n = nt // HEADS
    cs = min(GDN_SEQ_CHUNKS, n)
    ng = n // cs

    def body(w_ref, qd_ref, kd_ref, a_ref, gl_ref, do_ref, ds_ref, dstate, dos):
        @pl.when(pl.program_id(0) == 0)
        def _():
            dstate[...] = jnp.zeros_like(dstate)

        for c in reversed(range(cs)):
            sl = slice(HEADS * c, HEADS * (c + 1))
            for h in range(HEADS):
                dos[h] = do_ref[CHUNK * c:CHUNK * (c + 1), 128 * h:128 * (h + 1)].astype(BF16)
            dob = dos[...]
            dst = dstate[...]
            dstb = dst.astype(BF16)
            ds_ref[sl] = dstb
            dvn = (lax.dot_general(a_ref[sl], dob, TN_B, preferred_element_type=F32)
                   + lax.dot_general(kd_ref[sl], dstb, NN_B, preferred_element_type=F32))
            dstate[...] = (lax.dot_general(qd_ref[sl], dob, TN_B, preferred_element_type=F32) + gl_ref[sl] * dst
                           - lax.dot_general(w_ref[sl], dvn.astype(BF16), TN_B, preferred_element_type=F32))

    b3 = lambda d: pl.BlockSpec((cs * HEADS, CHUNK, d), lambda i: (ng - 1 - i, 0, 0))
    return _call(body, name="gdn_scan_bwd", grid=(ng,),
                 in_specs=[b3(128), b3(128), b3(128), b3(CHUNK), pl.BlockSpec((cs * HEADS, 1, 128), lambda i: (ng - 1 - i, 0, 0)),
                           pl.BlockSpec((cs * CHUNK, 1024), lambda i: (ng - 1 - i, 0))],
                 out_specs=pl.BlockSpec((cs * HEADS, 128, 128), lambda i: (ng - 1 - i, 0, 0)),
                 out_shape=_sds((nt, 128, 128), BF16),
                 scratch=[pltpu.VMEM((HEADS, 128, 128), F32), pltpu.VMEM((HEADS, CHUNK, 128), BF16)],
                 args=(w, qd, kd, amat, gl, do))


def _gdn_post_bwd(qkv, gbeta, u, w, tinv, states, dstates, do):
    s = qkv.shape[0]
    n = s // CHUNK
    cb = min(GDN_POST_CHUNKS, n)
    nb = cb * HEADS
    rows = cb * CHUNK

    def body(qkv_ref, gb_ref, u_ref, w_ref, t_ref, st_ref, ds_ref, do_ref, dqkv_ref, dgb_ref, qs, ks, vs, gs, bs, dos):
        gbv = gb_ref[...]
        gc = _dot_hi(_block_tri(rows), gbv)
        _gather_heads(qkv_ref, gc, gbv, qs, ks, vs, gs, bs, cb)
        for c in range(cb):
            for h in range(HEADS):
                dos[HEADS * c + h] = do_ref[CHUNK * c:CHUNK * (c + 1), 128 * h:128 * (h + 1)].astype(F32)
        q, k, v, b128 = qs[...], ks[...], vs[...], bs[...]
        c = _local_b(q, k, v, gs[...], b128)
        tinv, st, dst, dov = t_ref[...], st_ref[...], ds_ref[...], dos[...]
        wv = w_ref[...]
        vn = u_ref[...] - _bdot(wv, st, NN_B)
        dvn = _bdot(c["amat"], dov, TN_B) + _bdot(c["kd"], dst, NN_B)
        damat = jnp.where(c["lower"], _bdot(dov, vn, NT_B), 0.0)
        dqd = _bdot(dov, st, NT_B)
        dkd = _bdot(vn, dst, NT_B)
        dw = -_bdot(dvn, st, NT_B)
        dgl = jnp.sum(jnp.sum(st.astype(F32) * dst.astype(F32), axis=1, keepdims=True), axis=-1, keepdims=True)
        dvb = _bdot(tinv, dvn, TN_B)
        dkbg = _bdot(tinv, dw, TN_B)
        dtinv = _bdot(dvn, c["vb"], NT_B) + _bdot(dw, c["kbg"], NT_B)
        dl = -_bdot(_bdot(tinv, dtinv, TN_B), tinv, NT_B)
        dl = jnp.where(c["strict"], dl, 0.0)
        dmm = dl * c["decay"]
        dqk = damat * c["decay"]
        wmat = dl * c["lmat"] + damat * c["amat"]
        wcol = jnp.sum(wmat, axis=1, keepdims=True)
        dgc = jnp.sum(wmat, axis=-1, keepdims=True) - jnp.sum(jnp.where(c["eye"], wcol, 0.0), axis=-1, keepdims=True)
        dkb = _bdot(dmm, k, NN_B) + dkbg * c["egc"]
        dk = _bdot(dmm, c["kb"], TN_B) + _bdot(dqk, q, TN_B) + dkd * c["ekd"] + dkb * b128
        dq = _bdot(dqk, k, NN_B) + dqd * c["egc"]
        tmp = jnp.sum(dkd * c["kd"], axis=-1, keepdims=True)
        dgc = (dgc + jnp.sum(dkbg * c["kbg"], axis=-1, keepdims=True) + jnp.sum(dqd * c["qd"], axis=-1, keepdims=True)
               - tmp)
        dglast = jnp.sum(tmp, axis=1, keepdims=True) + dgl * c["gl"][:, :, :1]
        row1 = lax.broadcasted_iota(jnp.int32, (1, CHUNK, 1), 1)
        dgc = dgc + jnp.where(row1 == CHUNK - 1, dglast, 0.0)
        db = jnp.sum(dkb * k, axis=-1, keepdims=True) + jnp.sum(dvb * v, axis=-1, keepdims=True)
        dv = dvb * b128
        lane = lax.broadcasted_iota(jnp.int32, (CHUNK, 128), 1)
        parts = []
        for cc in range(cb):
            acc = jnp.zeros((CHUNK, 128), F32)
            for h in range(HEADS):
                bi = HEADS * cc + h
                rs = slice(CHUNK * cc, CHUNK * (cc + 1))
                dqkv_ref[rs, 128 * h:128 * (h + 1)] = dq[bi].astype(BF16)
                dqkv_ref[rs, 1024 + 128 * h:1024 + 128 * (h + 1)] = dk[bi].astype(BF16)
                dqkv_ref[rs, 2048 + 128 * h:2048 + 128 * (h + 1)] = dv[bi].astype(BF16)
                acc = acc + jnp.where(lane == GA_LANE + h, dgc[bi], 0.0)
            parts.append(acc)
        dgc_all = jnp.concatenate(parts, axis=0)
        dg_all = _dot_hi(_block_tri(rows, transpose=True), dgc_all)
        for cc in range(cb):
            acc = dg_all[CHUNK * cc:CHUNK * (cc + 1)]
            for h in range(HEADS):
                acc = acc + jnp.where(lane == GB_LANE + h, db[HEADS * cc + h], 0.0)
            dgb_ref[CHUNK * cc:CHUNK * (cc + 1), :] = acc

    b3 = lambda d1, d2: pl.BlockSpec((nb, d1, d2), lambda i: (i, 0, 0))
    return _call(body, name="gdn_post_bwd", grid=(n // cb,),
                 in_specs=[_rows(rows, 3072), _rows(rows, 128), b3(CHUNK, 128), b3(CHUNK, 128), b3(CHUNK, CHUNK),
                           b3(128, 128), b3(128, 128), _rows(rows, 1024)],
                 out_specs=[_rows(rows, 3072), _rows(rows, 128)],
                 out_shape=[_sds((s, 3072), BF16), _sds((s, 128), F32)],
                 scratch=[pltpu.VMEM((nb, CHUNK, 128), F32)] * 6, sem=("parallel",),
                 args=(qkv, gbeta, u, w, tinv, states, dstates, do))


def _mix_fwd(o_mla, o_gdn, proj, out_gain):
    s = proj.shape[0]
    tm = min(TILES["row"], s)

    def body(om_ref, og_ref, mg_ref, gg_ref, g_ref, o_ref):
        o_ref[:, :1024] = (om_ref[...] * _silu(mg_ref[...].astype(F32))).astype(BF16)
        for h in range(HEADS):
            sl = slice(128 * h, 128 * (h + 1))
            og = og_ref[:, sl]
            on = og * _rms(og, HEAD_DIM) * g_ref[...]
            o_ref[:, 1024 + 128 * h:1024 + 128 * (h + 1)] = (on * _silu(gg_ref[:, sl].astype(F32))).astype(BF16)

    return _call(body, name="mix_fwd", grid=(s // tm,),
                 in_specs=[_rows(tm, 1024), _rows(tm, 1024), _rows(tm, 1024, 1), _rows(tm, 1024, 5), _full((1, 128))],
                 out_specs=_rows(tm, 2048), out_shape=_sds((s, 2048), BF16), args=(o_mla, o_gdn, proj, proj, out_gain))


def _mix_bwd(o_mla, o_gdn, proj, out_gain, dmixed):
    s = proj.shape[0]
    tm = min(TILES["row"], s)

    def body(om_ref, og_ref, mg_ref, gg_ref, g_ref, dm_ref, dg_ref, dom_ref, dog_ref, dmg_ref, dgg_ref, ag_ref,
             delta_ref):
        @pl.when(pl.program_id(0) == 0)
        def _():
            ag_ref[...] = jnp.zeros_like(ag_ref)

        mg = mg_ref[...].astype(F32)
        dm = dm_ref[...].astype(F32)
        om = om_ref[...]
        dom = (dm * _silu(mg)).astype(BF16)
        dom_ref[...] = dom
        dmg_ref[...] = (dm * om * _dsilu(mg)).astype(BF16)
        prod = dom.astype(F32) * om
        lane = lax.broadcasted_iota(jnp.int32, (tm, 128), 1)
        delta = jnp.zeros((tm, 128), F32)
        for h in range(HEADS):
            delta = delta + jnp.where(lane == h, jnp.sum(prod[:, 128 * h:128 * (h + 1)], axis=-1, keepdims=True), 0.0)
        delta_ref[...] = delta
        for h in range(HEADS):
            sl = slice(128 * h, 128 * (h + 1))
            og, gg, d = og_ref[:, sl], gg_ref[:, sl].astype(F32), dg_ref[:, sl].astype(F32)
            on = og * _rms(og, HEAD_DIM) * g_ref[...]
            dgg_ref[:, sl] = (d * on * _dsilu(gg)).astype(BF16)
            dx, gpart = _rms_bwd(og, g_ref[...], d * _silu(gg), HEAD_DIM)
            dog_ref[:, sl] = dx.astype(BF16)
            ag_ref[...] += _acc8(gpart)

    return _call(body, name="mix_bwd", grid=(s // tm,),
                 in_specs=[_rows(tm, 1024), _rows(tm, 1024), _rows(tm, 1024, 1), _rows(tm, 1024, 5), _full((1, 128)),
                           _rows(tm, 1024, 0), _rows(tm, 1024, 1)],
                 out_specs=[_rows(tm, 1024), _rows(tm, 1024), _rows(tm, 1024), _rows(tm, 1024), _full((8, 128)),
                            _rows(tm, 128)],
                 out_shape=[_sds((s, 1024), BF16), _sds((s, 1024), BF16), _sds((s, 1024), BF16), _sds((s, 1024), BF16),
                            _sds((8, 128), F32), _sds((s, 128), F32)],
                 args=(o_mla, o_gdn, proj, proj, out_gain, dmixed, dmixed))


def _out_fwd(mixed, w_out, x, target):
    s = x.shape[0]
    tm = min(TILES["mm"], s)
    tn = min(TILES["mm"], D_MODEL)

    def body(m_ref, w_ref, x_ref, t_ref, dy_ref, acc_ref):
        err = x_ref[...] + _dot(m_ref[...], w_ref[...]) - t_ref[...]
        dy_ref[...] = (err * (1.0 / D_MODEL)).astype(BF16)

        @pl.when(pl.program_id(1) == 0)
        def _():
            acc_ref[...] = jnp.zeros_like(acc_ref)

        acc_ref[...] += _acc8(err * err)

    return _call(body, name="out_fwd", grid=(D_MODEL // tn, s // tm),
                 in_specs=[pl.BlockSpec((tm, D_MODEL), lambda j, i: (i, 0)), pl.BlockSpec((D_MODEL, tn), lambda j, i: (0, j)),
                           pl.BlockSpec((tm, tn), lambda j, i: (i, j)), pl.BlockSpec((tm, tn), lambda j, i: (i, j))],
                 out_specs=[pl.BlockSpec((tm, tn), lambda j, i: (i, j)), pl.BlockSpec((8, tn), lambda j, i: (0, j))],
                 out_shape=[_sds((s, D_MODEL), BF16), _sds((8, D_MODEL), F32)], args=(mixed, w_out, x, target))


def _row_tile(r, c):
    if r % 8 != 0:
        return r
    t = 8
    while r % (2 * t) == 0 and 2 * t * c * 4 <= (1 << 20):
        t *= 2
    return t


def _sum_arrays(parts, name, also_bf16=False):
    r, c = parts[0].shape
    tr = _row_tile(r, c)
    n = len(parts)

    def body(*refs):
        acc = refs[0][...].astype(F32)
        for p_ref in refs[1:n]:
            acc = acc + p_ref[...].astype(F32)
        refs[n][...] = acc
        if also_bf16:
            refs[n + 1][...] = acc.astype(BF16)

    nout = 2 if also_bf16 else 1
    out = _call(body, name=name, grid=(r // tr,), in_specs=[_rows(tr, c)] * n, out_specs=[_rows(tr, c)] * nout,
                out_shape=[_sds((r, c), F32), _sds((r, c), BF16)][:nout], args=tuple(parts))
    return out if also_bf16 else out[0]


def _adamw(w, g, m, v, name):
    r, c = w.shape
    c1 = 1.0 - ADAM_B1 ** ADAM_STEP
    c2 = 1.0 - ADAM_B2 ** ADAM_STEP

    def body(w_ref, g_ref, m_ref, v_ref, d_ref, nm_ref, nv_ref):
        gv = g_ref[...]
        nm = ADAM_B1 * m_ref[...] + (1.0 - ADAM_B1) * gv
        nv = ADAM_B2 * v_ref[...] + (1.0 - ADAM_B2) * (gv * gv)
        nm_ref[...] = nm
        nv_ref[...] = nv
        d_ref[...] = -ADAM_LR * ((nm / c1) / (jnp.sqrt(nv / c2) + ADAM_EPS) + ADAM_WD * w_ref[...])

    if r % 8 == 0:
        tr = _row_tile(r, c)
        grid, spec = (r // tr,), _rows(tr, c)
    else:
        tc = c
        while tc % 256 == 0 and r * tc * 4 > (3 << 19):
            tc //= 2
        grid, spec = (c // tc,), pl.BlockSpec((r, tc), lambda i: (0, i))
    return _call(body, name=name, grid=grid, in_specs=[spec] * 4, out_specs=[spec] * 3,
                 out_shape=[_sds((r, c), F32)] * 3, args=(w, g, m, v))


ANY = pl.BlockSpec(memory_space=pl.ANY)
CHIP_FLIPS = ((1, 0), (0, 1), (1, 1))


def _comm_call(body, *, name, n_in, out_shape, scratch):
    def kfn(*refs):
        body(*refs)
    return pl.pallas_call(kfn, name=name, in_specs=[ANY] * n_in, out_specs=[ANY] * len(out_shape), out_shape=out_shape,
                          scratch_shapes=list(scratch),
                          compiler_params=pltpu.CompilerParams(has_side_effects=True))


def _all_gather_chips(shards):
    na = len(shards)

    def body(*refs):
        copies = _gather_copies(refs[:na], refs[na:2 * na], *refs[2 * na:])
        _gather_start(copies)
        _gather_finish(copies)

    out_shape = [_sds((4,) + a.shape, a.dtype) for a in shards]
    sem = pltpu.SemaphoreType.DMA((na, 3))
    got = _comm_call(body, name="all_gather_weights", n_in=na, out_shape=out_shape, scratch=[sem, sem, sem, sem])(*shards)
    return _place_own_blocks(got, shards)


def _gather_copies(ins, outs, send_sems, recv_sems, fwd_send, fwd_recv):
    x, y, c = lax.axis_index("x"), lax.axis_index("y"), lax.axis_index("c")
    my_k = 2 * x + y
    direct, forwards = [], []
    for a in range(len(ins)):
        rows = ins[a].shape[0]
        for r, (fx, fy) in enumerate(CHIP_FLIPS):
            px, py = x ^ fx, y ^ fy
            if rows % 32 == 0:
                mine = pl.ds(pl.multiple_of(c * (rows // 2), 16), rows // 2)
                other = pl.ds(pl.multiple_of((1 - c) * (rows // 2), 16), rows // 2)
                rc = pltpu.make_async_remote_copy(
                    src_ref=ins[a].at[mine], dst_ref=outs[a].at[my_k, mine], send_sem=send_sems.at[a, r],
                    recv_sem=recv_sems.at[a, r], device_id=(px, py, c), device_id_type=MESH)
                landed = outs[a].at[2 * px + py, mine]
                fw = pltpu.make_async_remote_copy(
                    src_ref=landed, dst_ref=landed, send_sem=fwd_send.at[a, r], recv_sem=fwd_recv.at[a, r],
                    device_id=(x, y, 1 - c), device_id_type=MESH)
                from_sib = outs[a].at[2 * px + py, other]
                fw_in = pltpu.make_async_remote_copy(
                    src_ref=from_sib, dst_ref=from_sib, send_sem=fwd_send.at[a, r], recv_sem=fwd_recv.at[a, r],
                    device_id=(x, y, 1 - c), device_id_type=MESH)
                forwards.append((rc, fw, fw_in))
            else:
                direct.append(pltpu.make_async_remote_copy(
                    src_ref=ins[a], dst_ref=outs[a].at[my_k], send_sem=send_sems.at[a, r],
                    recv_sem=recv_sems.at[a, r], device_id=(px, py, c), device_id_type=MESH))
    return forwards, direct


def _gather_start(copies):
    forwards, direct = copies
    for rc, _, _ in forwards:
        rc.start()
    for rc in direct:
        rc.start()


def _gather_finish(copies):
    forwards, direct = copies
    for rc, fw, _ in forwards:
        rc.wait_recv()
        fw.start()
    for rc, fw, fw_in in forwards:
        rc.wait_send()
        fw.wait_send()
        fw_in.wait_recv()
    for rc in direct:
        rc.wait()


def _place_own_blocks(got, shards):
    my_k = 2 * lax.axis_index("x") + lax.axis_index("y")
    return [lax.dynamic_update_index_in_dim(g, a, my_k, 0) for g, a in zip(got, shards)]


def _norm1_fwd_gather(x, gain, shards):
    s = x.shape[0]
    tm = min(TILES["row"], s)
    ni = s // tm
    na = len(shards)

    def kfn(x_ref, g_ref, *rest):
        o_ref = rest[na]
        sems = rest[2 * na + 1:]
        i = pl.program_id(0)

        @pl.when(i == 0)
        def _():
            _gather_start(_gather_copies(rest[:na], rest[na + 1:2 * na + 1], *sems))

        xv = x_ref[...]
        r = lax.rsqrt(jnp.mean(xv * xv, axis=-1, keepdims=True) + NORM_EPS)
        o_ref[...] = (xv * r * g_ref[...]).astype(BF16)

        @pl.when(i == ni - 1)
        def _():
            _gather_finish(_gather_copies(rest[:na], rest[na + 1:2 * na + 1], *sems))

    sem = pltpu.SemaphoreType.DMA((na, 3))
    out = pl.pallas_call(
        kfn, name="norm1_fwd_gather", grid=(ni,),
        in_specs=[_rows(tm, D_MODEL), _full((1, D_MODEL))] + [ANY] * na,
        out_specs=[_rows(tm, D_MODEL)] + [ANY] * na,
        out_shape=[_sds((s, D_MODEL), BF16)] + [_sds((4,) + s_a.shape, s_a.dtype) for s_a in shards],
        scratch_shapes=[sem, sem, sem, sem],
        compiler_params=pltpu.CompilerParams(dimension_semantics=("arbitrary",), vmem_limit_bytes=VMEM_LIMIT_V7X,
                                             has_side_effects=True))(x, gain, *shards)
    return out[0], _place_own_blocks(list(out[1:]), shards)


def _matmul_nn_gather(a, b, shards, *, name, tm, tn, out_dtype):
    m, kdim = a.shape
    n = b.shape[1]
    ni, nj = m // tm, n // tn
    na = len(shards)

    def body(a_ref, b_ref, *rest):
        o_ref = rest[na]
        sems = rest[2 * na + 1:]
        i, j = pl.program_id(0), pl.program_id(1)

        @pl.when((i == 0) & (j == 0))
        def _():
            _gather_start(_gather_copies(rest[:na], rest[na + 1:2 * na + 1], *sems))

        o_ref[...] = _dot(a_ref[...], b_ref[...]).astype(out_dtype)

        @pl.when((i == ni - 1) & (j == nj - 1))
        def _():
            _gather_finish(_gather_copies(rest[:na], rest[na + 1:2 * na + 1], *sems))

    def kfn(*refs):
        body(*refs)
    sem = pltpu.SemaphoreType.DMA((na, 3))
    out = pl.pallas_call(
        kfn, name=name, grid=(ni, nj),
        in_specs=[pl.BlockSpec((tm, kdim), lambda i, j: (i, 0)), pl.BlockSpec((kdim, tn), lambda i, j: (0, j))] + [ANY] * na,
        out_specs=[pl.BlockSpec((tm, tn), lambda i, j: (i, j))] + [ANY] * na,
        out_shape=[_sds((m, n), out_dtype)] + [_sds((4,) + s_a.shape, s_a.dtype) for s_a in shards],
        scratch_shapes=[sem, sem, sem, sem],
        compiler_params=pltpu.CompilerParams(dimension_semantics=("arbitrary", "arbitrary"),
                                             vmem_limit_bytes=VMEM_LIMIT_V7X, has_side_effects=True))(a, b, *shards)
    return out[0], _place_own_blocks(list(out[1:]), shards)


def _all_reduce_small(vec):
    r = vec.shape[0]

    def body(v_ref, o_ref, gath, send_sems, recv_sems):
        x, y, c = lax.axis_index("x"), lax.axis_index("y"), lax.axis_index("c")
        me = 4 * x + 2 * y + c
        gath[me] = v_ref[...]
        copies = []
        for rel in range(1, 8):
            fx, fy, fc = (rel >> 2) & 1, (rel >> 1) & 1, rel & 1
            rc = pltpu.make_async_remote_copy(
                src_ref=v_ref, dst_ref=gath.at[me], send_sem=send_sems.at[rel - 1], recv_sem=recv_sems.at[rel - 1],
                device_id=(x ^ fx, y ^ fy, c ^ fc), device_id_type=MESH)
            rc.start()
            copies.append(rc)
        for rc in copies:
            rc.wait()
        acc = gath[0]
        for d in range(1, 8):
            acc = acc + gath[d]
        o_ref[...] = acc

    def kfn(*refs):
        body(*refs)
    vm = pl.BlockSpec(memory_space=pltpu.VMEM)
    return pl.pallas_call(kfn, name="all_reduce_small", in_specs=[vm], out_specs=vm, out_shape=_sds((r, 128), F32),
                          scratch_shapes=[pltpu.VMEM((8, r, 128), F32), pltpu.SemaphoreType.DMA((7,)),
                                          pltpu.SemaphoreType.DMA((7,))],
                          compiler_params=pltpu.CompilerParams(has_side_effects=True))(vec)


def _exchange_halves(arrs):
    na = len(arrs)

    def body(*refs):
        ins, outs = refs[:na], refs[na:2 * na]
        send_sems, recv_sems = refs[2 * na:]
        x, y, c = lax.axis_index("x"), lax.axis_index("y"), lax.axis_index("c")
        copies = []
        for a in range(na):
            half = ins[a].shape[1] // 2
            src = ins[a].at[:, pl.ds(pl.multiple_of((1 - c) * half, 8), half), :]
            rc = pltpu.make_async_remote_copy(src_ref=src, dst_ref=outs[a], send_sem=send_sems.at[a],
                                              recv_sem=recv_sems.at[a], device_id=(x, y, 1 - c), device_id_type=MESH)
            rc.start()
            copies.append(rc)
        for rc in copies:
            rc.wait()

    out_shape = [_sds((4, a.shape[1] // 2, a.shape[2]), F32) for a in arrs]
    return _comm_call(body, name="rs_pair_exchange", n_in=na, out_shape=out_shape,
                      scratch=[pltpu.SemaphoreType.DMA((na,)), pltpu.SemaphoreType.DMA((na,))])(*arrs)


def _scatter_to_chips(arrs):
    na = len(arrs)

    def body(*refs):
        ins, outs = refs[:na], refs[na:2 * na]
        send_sems, recv_sems = refs[2 * na:]
        x, y, c = lax.axis_index("x"), lax.axis_index("y"), lax.axis_index("c")
        copies = []
        for a in range(na):
            for r, (fx, fy) in enumerate(CHIP_FLIPS):
                px, py = x ^ fx, y ^ fy
                rc = pltpu.make_async_remote_copy(
                    src_ref=ins[a].at[2 * px + py], dst_ref=outs[a].at[r], send_sem=send_sems.at[a, r],
                    recv_sem=recv_sems.at[a, r], device_id=(px, py, c), device_id_type=MESH)
                rc.start()
                copies.append(rc)
        for rc in copies:
            rc.wait()

    out_shape = [_sds((3,) + a.shape[1:], a.dtype) for a in arrs]
    return _comm_call(body, name="rs_chip_scatter", n_in=na, out_shape=out_shape,
                      scratch=[pltpu.SemaphoreType.DMA((na, 3)), pltpu.SemaphoreType.DMA((na, 3))])(*arrs)


def _sum_into_half(parts, name):
    r2, c = parts[0].shape
    tr = _row_tile(r2, c)
    nb = r2 // tr
    n = len(parts)

    def kfn(c_ref, *refs):
        acc = refs[0][...].astype(F32)
        for p_ref in refs[1:n]:
            acc = acc + p_ref[...].astype(F32)
        refs[n][...] = acc

    spec = pltpu.PrefetchScalarGridSpec(
        num_scalar_prefetch=1, grid=(nb,), in_specs=[pl.BlockSpec((tr, c), lambda i, cr: (i, 0))] * n,
        out_specs=pl.BlockSpec((tr, c), lambda i, cr: (cr[0] * nb + i, 0)))
    core = lax.axis_index("c").astype(jnp.int32).reshape(1)
    return pl.pallas_call(kfn, name=name, grid_spec=spec, out_shape=_sds((2 * r2, c), F32),
                          compiler_params=pltpu.CompilerParams(dimension_semantics=("arbitrary",),
                                                               vmem_limit_bytes=VMEM_LIMIT_V7X))(core, *parts)


def _join_in_place(arrs):
    na = len(arrs)

    def body(*refs):
        outs = refs[na:2 * na]
        send_sems, recv_sems = refs[2 * na:]
        x, y, c = lax.axis_index("x"), lax.axis_index("y"), lax.axis_index("c")
        copies = []
        for a in range(na):
            half = outs[a].shape[0] // 2
            mine = outs[a].at[pl.ds(pl.multiple_of(c * half, 8), half), :]
            rc = pltpu.make_async_remote_copy(src_ref=mine, dst_ref=mine, send_sem=send_sems.at[a],
                                              recv_sem=recv_sems.at[a], device_id=(x, y, 1 - c), device_id_type=MESH)
            rc.start()
            copies.append(rc)
        for rc in copies:
            rc.wait()

    def kfn(*refs):
        body(*refs)
    return pl.pallas_call(kfn, name="rs_pair_join", in_specs=[ANY] * na, out_specs=[ANY] * na,
                          out_shape=[_sds(a.shape, F32) for a in arrs],
                          input_output_aliases={a: a for a in range(na)},
                          scratch_shapes=[pltpu.SemaphoreType.DMA((na,)), pltpu.SemaphoreType.DMA((na,))],
                          compiler_params=pltpu.CompilerParams(has_side_effects=True))(*arrs)


def _pair_sum(g, o, name):
    _, r, c = g.shape
    half = r // 2
    tr = _row_tile(half, c)
    nb = half // tr

    def kfn(c_ref, g_ref, o_ref, s32_ref, s16_ref):
        acc = g_ref[...] + o_ref[...]
        s32_ref[...] = acc
        s16_ref[...] = acc.astype(BF16)

    blk = lambda imap: pl.BlockSpec((None, tr, c), imap)
    same = lambda k, i, cr: (k, i, 0)
    spec = pltpu.PrefetchScalarGridSpec(
        num_scalar_prefetch=1, grid=(4, nb), in_specs=[blk(lambda k, i, cr: (k, cr[0] * nb + i, 0)), blk(same)],
        out_specs=[blk(same), blk(same)])
    core = lax.axis_index("c").astype(jnp.int32).reshape(1)
    return pl.pallas_call(kfn, name=name, grid_spec=spec, out_shape=[_sds((4, half, c), F32), _sds((4, half, c), BF16)],
                          compiler_params=pltpu.CompilerParams(dimension_semantics=("arbitrary", "arbitrary"),
                                                               vmem_limit_bytes=VMEM_LIMIT_V7X))(core, g, o)


def _rs_pair_stage(grads):
    got = _exchange_halves(grads)
    sums = [_pair_sum(g, o, f"rs_pair_sum_{a}") for a, (g, o) in enumerate(zip(grads, got))]
    return [s32 for s32, _ in sums], [s16 for _, s16 in sums]


def _rs_chip_stage(pair, recv):
    k_me = 2 * lax.axis_index("x") + lax.axis_index("y")
    halves = []
    for a, (p, rv) in enumerate(zip(pair, recv)):
        own = lax.dynamic_index_in_dim(p, k_me, 0, keepdims=False)
        halves.append(_sum_into_half([own, rv[0], rv[1], rv[2]], f"rs_chip_sum_{a}"))
    return _join_in_place(halves)


def _reduce_scatter(grads):
    pair, pair_bf16 = _rs_pair_stage(grads)
    return _rs_chip_stage(pair, _scatter_to_chips(pair_bf16))


def _matmul_nt_scatter(a, b, send, *, name, tm, tn, out_dtype):
    m, kdim = a.shape
    n = b.shape[0]
    ni, nj = m // tm, n // tn
    na = len(send)

    def body(a_ref, b_ref, *rest):
        send_refs, o_ref, recv_refs = rest[:na], rest[na], rest[na + 1:2 * na + 1]
        send_sems, recv_sems = rest[2 * na + 1:]
        i, j = pl.program_id(0), pl.program_id(1)

        def copies():
            x, y, c = lax.axis_index("x"), lax.axis_index("y"), lax.axis_index("c")
            out = []
            for s_i in range(na):
                for r, (fx, fy) in enumerate(CHIP_FLIPS):
                    px, py = x ^ fx, y ^ fy
                    out.append(pltpu.make_async_remote_copy(
                        src_ref=send_refs[s_i].at[2 * px + py], dst_ref=recv_refs[s_i].at[r],
                        send_sem=send_sems.at[s_i, r], recv_sem=recv_sems.at[s_i, r], device_id=(px, py, c),
                        device_id_type=MESH))
            return out

        @pl.when((i == 0) & (j == 0))
        def _():
            for cp in copies():
                cp.start()

        o_ref[...] = _dot(a_ref[...], b_ref[...], NT).astype(out_dtype)

        @pl.when((i == ni - 1) & (j == nj - 1))
        def _():
            for cp in copies():
                cp.wait()

    def kfn(*refs):
        body(*refs)
    sem = pltpu.SemaphoreType.DMA((na, 3))
    out = pl.pallas_call(
        kfn, name=name, grid=(ni, nj),
        in_specs=[pl.BlockSpec((tm, kdim), lambda i, j: (i, 0)), pl.BlockSpec((tn, kdim), lambda i, j: (j, 0))] + [ANY] * na,
        out_specs=[pl.BlockSpec((tm, tn), lambda i, j: (i, j))] + [ANY] * na,
        out_shape=[_sds((m, n), out_dtype)] + [_sds((3,) + s_a.shape[1:], s_a.dtype) for s_a in send],
        scratch_shapes=[sem, sem],
        compiler_params=pltpu.CompilerParams(dimension_semantics=("arbitrary", "arbitrary"),
                                             vmem_limit_bytes=VMEM_LIMIT_V7X, has_side_effects=True))(a, b, *send)
    return out[0], list(out[1:])


def _pad_w_in(w):
    z = jnp.zeros((w.shape[0], 1024 - 848), w.dtype)
    return jnp.concatenate([w[:, 0:832], w[:, 4928:4944], z, w[:, 832:4928], w[:, 4944:5968]], axis=1)


def _unpad_w_in(g):
    return jnp.concatenate([g[:, 0:832], g[:, 1024:5120], g[:, 832:848], g[:, 5120:6144]], axis=1)


W_IN_SHARD = W_IN_COLS // 4
W_IN_RUNS = ((0, 832, 0), (832, 4928, 1024), (4928, 4944, 832), (4944, 5968, 5120))


def _w_in_grad_blocks(p):
    def orig_cols(lo, hi):
        parts = [p[:, pa + max(lo, a) - a:pa + min(hi, b) - a] for a, b, pa in W_IN_RUNS if max(lo, a) < min(hi, b)]
        return parts[0] if len(parts) == 1 else jnp.concatenate(parts, axis=1)
    return jnp.stack([orig_cols(W_IN_SHARD * k, W_IN_SHARD * (k + 1)) for k in range(4)])


def _pad_w_in_blocks(g):
    def orig_cols(lo, hi):
        return [g[k][:, max(lo, W_IN_SHARD * k) - W_IN_SHARD * k:min(hi, W_IN_SHARD * (k + 1)) - W_IN_SHARD * k]
                for k in range(4) if max(lo, W_IN_SHARD * k) < min(hi, W_IN_SHARD * (k + 1))]
    z = jnp.zeros((g.shape[1], 1024 - 848), g.dtype)
    return jnp.concatenate(orig_cols(0, 832) + orig_cols(4928, 4944) + [z] + orig_cols(832, 4928) + orig_cols(4944, 5968),
                           axis=1)


def _pad_heads(w):
    r = w.shape[0]
    return jnp.pad(w.reshape(r, HEADS, QK_DIM), ((0, 0), (0, 0), (0, QK_PAD - QK_DIM))).reshape(r, HEADS * QK_PAD)


def _unpad_heads(w):
    r = w.shape[0]
    return w.reshape(r, HEADS, QK_PAD)[:, :, :QK_DIM].reshape(r, HEADS * QK_DIM)


def _cols_to_blocks(w):
    r = w.shape[0]
    return w.reshape(r, 4, -1).transpose(1, 0, 2)


def _blocks_to_cols(w):
    return w.transpose(1, 0, 2).reshape(w.shape[1], -1)


SMALL_ROWS = {"norm_gain": (0, 2048), "mla_q_a_gain": (16, 512), "mla_kv_a_gain": (20, 256),
              "mla_q_norm_gain": (22, 192), "mla_k_norm_gain": (24, 192), "gdn_a_log": (26, 8),
              "gdn_dt_bias": (27, 8), "gdn_out_norm_gain": (28, 128)}
LOSS_ROW = 29
SMALL_PACK_ROWS = 32
CONV_ROW = 32


def _pack_small(vals, loss=None):
    rows = []
    at = 0
    for name, (row, size) in SMALL_ROWS.items():
        assert row == at
        nr = -(-size // 128)
        rows.append(jnp.pad(vals[name].reshape(-1).astype(F32), (0, nr * 128 - size)).reshape(nr, 128))
        at += nr
    assert at == LOSS_ROW
    if loss is not None:
        rows.append(jnp.pad(loss.reshape(1, 1), ((0, 0), (0, 127))))
        at += 1
    rows.append(jnp.zeros((SMALL_PACK_ROWS - at, 128), F32))
    return jnp.concatenate(rows, axis=0)


def _unpack_small(pack, name):
    row, size = SMALL_ROWS[name]
    nr = -(-size // 128)
    return pack[row:row + nr].reshape(-1)[:size].reshape(1, size)


def _local_step(x, positions, target, norm_gain, w_in_p, q_a_gain, kv_a_gain, w_uq_p, w_ukv, q_norm_gain,
                k_norm_gain, conv_w, a_log, dt_bias, out_gain, w_out, scatter_hook=None, late_weights=None,
                first_weights=None):
    half = HALF_ROPE
    inv_freq = jnp.power(10000.0, -jnp.arange(half, dtype=F32) / half)
    ang = positions.astype(F32)[:, None] * inv_freq
    cos, sin = jnp.cos(ang), jnp.sin(ang)
    zpad = jnp.zeros((x.shape[0], 64), F32)
    cs = jnp.concatenate([cos, cos, zpad], axis=1)
    sn = jnp.concatenate([-sin, sin, zpad], axis=1)
    gq = jnp.pad(q_norm_gain.reshape(1, QK_DIM), ((0, 0), (0, QK_PAD - QK_DIM)))
    gk = jnp.pad(k_norm_gain.reshape(1, QK_DIM), ((0, 0), (0, QK_PAD - QK_DIM)))
    lane_pad = ((0, 0), (GA_LANE, 128 - GA_LANE - HEADS))
    alog128 = jnp.pad(a_log.reshape(1, HEADS), lane_pad)
    dtb128 = jnp.pad(dt_bias.reshape(1, HEADS), lane_pad)
    ng, qag, kvag, og = (norm_gain.reshape(1, -1), q_a_gain.reshape(1, -1), kv_a_gain.reshape(1, -1),
                         out_gain.reshape(1, -1))

    if first_weights is None:
        xn = _norm1_fwd(x, ng)
    else:
        shards, assemble = first_weights
        xn, gathered = _norm1_fwd_gather(x, ng, shards)
        w_in_p, conv_w = assemble(gathered)
    misc = _matmul(xn, w_in_p[:, 768:896], mode="nn", out_dtype=F32, name="in_proj_misc")
    if late_weights is None:
        proj = _matmul(xn, w_in_p, mode="nn", out_dtype=BF16, name="in_proj")
    else:
        shards, assemble = late_weights
        proj, gathered = _matmul_nn_gather(xn, w_in_p, shards, name="in_proj_gather", tm=TILES["mm"], tn=2 * TILES["mm"],
                                           out_dtype=BF16)
        w_uq_p, w_ukv, w_out = assemble(gathered)
    cqn, ckvn = _mla_a_norm(proj, qag, kvag)
    q_pre = _matmul(cqn, w_uq_p, mode="nn", out_dtype=BF16, name="q_up")
    kv_pre = _matmul(ckvn, w_ukv, mode="nn", out_dtype=BF16, name="kv_up")
    q, k, v = _mla_post_fwd(q_pre, kv_pre, misc, cs, sn, gq, gk)
    o_mla, lse = _attn_fwd(q, k, v)
    qkv = _gdn_conv_fwd(proj, conv_w)
    gbeta = _gdn_gates_fwd(misc, alog128, dtb128)
    g_u, g_w, g_qd, g_kd, g_a, g_t, g_gl = _gdn_pre(qkv, gbeta)
    o_gdn, states = _gdn_scan_fwd(g_u, g_w, g_qd, g_kd, g_a, g_gl)
    mixed = _mix_fwd(o_mla, o_gdn, proj, og)
    dy, sq = _out_fwd(mixed, w_out, x, target)

    dmixed = _matmul(dy, w_out, mode="nt", out_dtype=BF16, name="d_mixed")
    d_w_out = _matmul(mixed, dy, mode="tn", out_dtype=F32, name="d_w_out", tk=4096)
    do_mla, do_gdn, dmg, dgg, d_out_gain, delta128 = _mix_bwd(o_mla, o_gdn, proj, og, dmixed)
    s_len = x.shape[0]
    delta_rows = delta128[:, :HEADS].T.reshape(HEADS, 1, s_len)
    dq, dk, dv = _attn_bwd(q, k, v, lse.reshape(HEADS, 1, s_len), delta_rows, do_mla)
    dq_pre, dkv_pre, dkr, d_gq, d_gk = _mla_post_bwd(q_pre, kv_pre, misc, cs, sn, gq, gk, dq, dk, dv)
    d_w_uq_p = _matmul(cqn, dq_pre, mode="tn", out_dtype=F32, name="d_w_uq", tk=1024)
    d_w_ukv = _matmul(ckvn, dkv_pre, mode="tn", out_dtype=F32, name="d_w_ukv", tk=1024)
    dcqn = _matmul(dq_pre, w_uq_p, mode="nt", out_dtype=F32, name="d_cqn")
    dckvn = _matmul(dkv_pre, w_ukv, mode="nt", out_dtype=F32, name="d_ckvn")
    dcq, dckv, d_qag, d_kvag = _mla_a_norm_bwd(proj, qag, kvag, dcqn, dckvn)
    dstates = _gdn_scan_bwd(g_w, g_qd, g_kd, g_a, g_gl, do_gdn)
    dqkv, dgbeta = _gdn_post_bwd(qkv, gbeta, g_u, g_w, g_t, states, dstates, do_gdn)
    dz = _gdn_conv_bwd_a(proj, conv_w, dqkv)
    dgx, d_conv = _gdn_conv_bwd_b(proj, conv_w, dz)
    dmisc, d_alog, d_dtb = _gdn_gates_bwd(misc, alog128, dtb128, gbeta, dgbeta, dkr)
    dproj = jnp.concatenate([dcq, dckv, dmisc, jnp.zeros((x.shape[0], 128), BF16), dmg, dgx, dgg], axis=1)
    d_w_in_p = _matmul(xn, dproj, mode="tn", out_dtype=F32, name="d_w_in", tk=4096)
    big = {"w_in": d_w_in_p, "w_uq": d_w_uq_p, "w_ukv": d_w_ukv, "w_out": d_w_out, "gdn_conv_w": d_conv}
    if scatter_hook is None:
        dxn, received = _matmul(dproj, w_in_p, mode="nt", out_dtype=BF16, name="d_xn", tm=512, tn=512), None
    else:
        dxn, received = _matmul_nt_scatter(dproj, w_in_p, scatter_hook(big), name="d_xn_scatter", tm=1024, tn=512,
                                           out_dtype=BF16)
    grad_x, d_ng = _norm1_bwd(x, ng, dxn, dy)

    small = {"norm_gain": d_ng.sum(0), "mla_q_a_gain": d_qag.sum(0), "mla_kv_a_gain": d_kvag.sum(0),
             "mla_q_norm_gain": d_gq.sum(0)[:QK_DIM], "mla_k_norm_gain": d_gk.sum(0)[:QK_DIM],
             "gdn_a_log": d_alog.sum(0)[GA_LANE:GA_LANE + HEADS], "gdn_dt_bias": d_dtb.sum(0)[GA_LANE:GA_LANE + HEADS],
             "gdn_out_norm_gain": d_out_gain.sum(0)}
    return sq, grad_x, small, big, received


WEIGHTS = ["norm_gain", "w_in", "mla_q_a_gain", "mla_kv_a_gain", "w_uq", "w_ukv", "mla_q_norm_gain", "mla_k_norm_gain",
           "gdn_conv_w", "gdn_a_log", "gdn_dt_bias", "gdn_out_norm_gain", "w_out"]
BIG = ["w_in", "w_uq", "w_ukv", "w_out"]


def kernel(x, positions, norm_gain, w_in, mla_q_a_gain, mla_kv_a_gain, w_uq, w_ukv, mla_q_norm_gain, mla_k_norm_gain, gdn_conv_w, gdn_a_log, gdn_dt_bias, gdn_out_norm_gain, w_out, loss_target, m_norm_gain, m_w_in, m_mla_q_a_gain, m_mla_kv_a_gain, m_w_uq, m_w_ukv, m_mla_q_norm_gain, m_mla_k_norm_gain, m_gdn_conv_w, m_gdn_a_log, m_gdn_dt_bias, m_gdn_out_norm_gain, m_w_out, v_norm_gain, v_w_in, v_mla_q_a_gain, v_mla_kv_a_gain, v_w_uq, v_w_ukv, v_mla_q_norm_gain, v_mla_k_norm_gain, v_gdn_conv_w, v_gdn_a_log, v_gdn_dt_bias, v_gdn_out_norm_gain, v_w_out):
    w = dict(norm_gain=norm_gain, w_in=w_in, mla_q_a_gain=mla_q_a_gain, mla_kv_a_gain=mla_kv_a_gain, w_uq=w_uq,
             w_ukv=w_ukv, mla_q_norm_gain=mla_q_norm_gain, mla_k_norm_gain=mla_k_norm_gain, gdn_conv_w=gdn_conv_w,
             gdn_a_log=gdn_a_log, gdn_dt_bias=gdn_dt_bias, gdn_out_norm_gain=gdn_out_norm_gain, w_out=w_out)
    m = dict(norm_gain=m_norm_gain, w_in=m_w_in, mla_q_a_gain=m_mla_q_a_gain, mla_kv_a_gain=m_mla_kv_a_gain,
             w_uq=m_w_uq, w_ukv=m_w_ukv, mla_q_norm_gain=m_mla_q_norm_gain, mla_k_norm_gain=m_mla_k_norm_gain,
             gdn_conv_w=m_gdn_conv_w, gdn_a_log=m_gdn_a_log, gdn_dt_bias=m_gdn_dt_bias,
             gdn_out_norm_gain=m_gdn_out_norm_gain, w_out=m_w_out)
    v = dict(norm_gain=v_norm_gain, w_in=v_w_in, mla_q_a_gain=v_mla_q_a_gain, mla_kv_a_gain=v_mla_kv_a_gain,
             w_uq=v_w_uq, w_ukv=v_w_ukv, mla_q_norm_gain=v_mla_q_norm_gain, mla_k_norm_gain=v_mla_k_norm_gain,
             gdn_conv_w=v_gdn_conv_w, gdn_a_log=v_gdn_a_log, gdn_dt_bias=v_gdn_dt_bias,
             gdn_out_norm_gain=v_gdn_out_norm_gain, w_out=v_w_out)
    k_me = 2 * lax.axis_index("x") + lax.axis_index("y")

    first_weights = ([w_in[0].astype(BF16), gdn_conv_w[0]], lambda g: (_pad_w_in_blocks(g[0]), _blocks_to_cols(g[1])))
    late_weights = ([w_uq[0].astype(BF16), w_ukv[0].astype(BF16), w_out[0].astype(BF16)],
                    lambda g: (_pad_heads(_blocks_to_cols(g[0])), _blocks_to_cols(g[1]), g[2].reshape(D_MODEL, D_MODEL)))

    pair_sums = []

    def scatter_hook(big):
        pair, pair_bf16 = _rs_pair_stage([
            _w_in_grad_blocks(big["w_in"]), _cols_to_blocks(_unpad_heads(big["w_uq"])),
            _cols_to_blocks(big["w_ukv"]), big["w_out"].reshape(4, 512, D_MODEL)])
        pair_sums.extend(pair)
        return pair_bf16

    sq, grad_x, small, big, received = _local_step(
        x[0], positions[0], loss_target[0], norm_gain, None, mla_q_a_gain, mla_kv_a_gain, None, None,
        mla_q_norm_gain, mla_k_norm_gain, None, gdn_a_log, gdn_dt_bias, gdn_out_norm_gain, None, scatter_hook,
        late_weights, first_weights)

    loss_local = (0.5 / D_MODEL) * jnp.sum(sq)
    pack = jnp.concatenate([_pack_small(small, loss_local), big["gdn_conv_w"].reshape(96, 128)], axis=0)
    tot = _all_reduce_small(pack)
    loss = tot[LOSS_ROW, 0]
    conv_grad = lax.dynamic_slice_in_dim(tot[CONV_ROW:].reshape(4, 3072), k_me * 768, 768, axis=1)

    shard_grads = _rs_chip_stage(pair_sums, received)

    grads = {n: _unpack_small(tot, n) for n in SMALL_ROWS}
    grads["gdn_conv_w"] = conv_grad[None]
    for n, g in zip(BIG, shard_grads):
        grads[n] = g[None]

    delta, new_m, new_v = {}, {}, {}
    sw = _pack_small({n: w[n] for n in SMALL_ROWS})
    sm = _pack_small({n: m[n] for n in SMALL_ROWS})
    sv = _pack_small({n: v[n] for n in SMALL_ROWS})
    sd, snm, snv = _adamw(sw, tot[:SMALL_PACK_ROWS], sm, sv, "adamw_small")
    for n in SMALL_ROWS:
        delta[n], new_m[n], new_v[n] = _unpack_small(sd, n), _unpack_small(snm, n), _unpack_small(snv, n)
    for n in BIG + ["gdn_conv_w"]:
        if n == "w_in":
            d, nm, nv = _adamw(w[n][0].T, grads[n][0].T, m[n][0].T, v[n][0].T, f"adamw_{n}")
            delta[n], new_m[n], new_v[n] = d.T[None], nm.T[None], nv.T[None]
        else:
            d, nm, nv = _adamw(w[n][0], grads[n][0], m[n][0], v[n][0], f"adamw_{n}")
            delta[n], new_m[n], new_v[n] = d[None], nm[None], nv[None]

    return (loss, grad_x[None], *[grads[n] for n in WEIGHTS], *[delta[n] for n in WEIGHTS],
            *[new_m[n] for n in WEIGHTS], *[new_v[n] for n in WEIGHTS])
```

```python
import functools
import math

import jax
import jax.numpy as jnp
from jax import lax
from jax.experimental import pallas as pl
from jax.experimental.pallas import tpu as pltpu

F32 = jnp.float32
BF16 = jnp.bfloat16
MESH = pl.DeviceIdType.MESH

D_MODEL = 2048
HEADS = 8
HEAD_DIM = 128
QK_DIM = 192
QK_PAD = 256
HALF_ROPE = 32
CHUNK = 64
NORM_EPS = 1e-6
W_IN_COLS = 5968
W_IN_PAD = 6144
GA_LANE = 64
GB_LANE = 72
ADAM_LR, ADAM_B1, ADAM_B2, ADAM_EPS, ADAM_WD, ADAM_STEP = 0.001, 0.9, 0.999, 1e-08, 0.01, 10
VMEM_LIMIT_V7X = 52 * 1024 * 1024
HI = lax.Precision.HIGHEST
NN = (((1,), (0,)), ((), ()))
NT = (((1,), (1,)), ((), ()))
TN = (((0,), (0,)), ((), ()))

TILES = {"row": 512, "attn": 2048, "mm": 1024}


def _call(body, *, name, grid, in_specs, out_specs, out_shape, args, scratch=(), sem=None):
    def kfn(*refs):
        body(*refs)
    if sem is None:
        sem = ("arbitrary",) * len(grid)
    return pl.pallas_call(
        kfn, name=name, grid=grid, in_specs=in_specs, out_specs=out_specs, out_shape=out_shape,
        scratch_shapes=list(scratch),
        compiler_params=pltpu.CompilerParams(dimension_semantics=sem, vmem_limit_bytes=VMEM_LIMIT_V7X),
    )(*args)


def _rows(tm, w, cb=0):
    return pl.BlockSpec((tm, w), lambda i: (i, cb))


def _full(shape):
    n = len(shape)
    return pl.BlockSpec(shape, lambda *_: (0,) * n)


def _sds(shape, dtype):
    return jax.ShapeDtypeStruct(shape, dtype)


def _acc8(x):
    tm, c = x.shape
    return jnp.sum(x.reshape(tm // 8, 8, c), axis=0)


def _sigmoid(x):
    return 1.0 / (1.0 + jnp.exp(-x))


def _silu(x):
    return x * _sigmoid(x)


def _dsilu(x):
    s = _sigmoid(x)
    return s * (1.0 + x * (1.0 - s))


def _dot(a, b, dims=NN):
    return lax.dot_general(a.astype(BF16), b.astype(BF16), dims, preferred_element_type=F32)


def _dot_hi(a, b, dims=NN):
    return lax.dot_general(a, b, dims, precision=HI, preferred_element_type=F32)


def _matmul(a, b, *, mode, out_dtype, name, tm=None, tn=None, tk=None):
    if mode == "tn":
        kdim, m = a.shape
    else:
        m, kdim = a.shape
    n = b.shape[0] if mode == "nt" else b.shape[1]
    tm = min(tm or TILES["mm"], m)
    tn = min(tn or TILES["mm"], n)
    tk = min(tk or kdim, kdim)
    nk = kdim // tk
    dims = {"nn": NN, "nt": NT, "tn": TN}[mode]
    if mode == "tn":
        a_spec = pl.BlockSpec((tk, tm), lambda i, j, k: (k, i))
    else:
        a_spec = pl.BlockSpec((tm, tk), lambda i, j, k: (i, k))
    if mode == "nt":
        b_spec = pl.BlockSpec((tn, tk), lambda i, j, k: (j, k))
    else:
        b_spec = pl.BlockSpec((tk, tn), lambda i, j, k: (k, j))

    def body(a_ref, b_ref, o_ref):
        r = _dot(a_ref[...], b_ref[...], dims)
        if nk == 1:
            o_ref[...] = r.astype(o_ref.dtype)
        else:
            k = pl.program_id(2)

            @pl.when(k == 0)
            def _():
                o_ref[...] = r

            @pl.when(k > 0)
            def _():
                o_ref[...] += r

    if nk > 1:
        assert out_dtype == F32
    return _call(body, name=name, grid=(m // tm, n // tn, nk), in_specs=[a_spec, b_spec],
                 out_specs=pl.BlockSpec((tm, tn), lambda i, j, k: (i, j)), out_shape=_sds((m, n), out_dtype),
                 args=(a, b))


def _norm1_fwd(x, gain):
    s = x.shape[0]
    tm = min(TILES["row"], s)

    def body(x_ref, g_ref, o_ref):
        xv = x_ref[...]
        r = lax.rsqrt(jnp.mean(xv * xv, axis=-1, keepdims=True) + NORM_EPS)
        o_ref[...] = (xv * r * g_ref[...]).astype(BF16)

    return _call(body, name="norm1_fwd", grid=(s // tm,), in_specs=[_rows(tm, D_MODEL), _full((1, D_MODEL))],
                 out_specs=_rows(tm, D_MODEL), out_shape=_sds((s, D_MODEL), BF16), args=(x, gain))


def _norm1_bwd(x, gain, dxn, dy):
    s = x.shape[0]
    tm = min(TILES["row"], s)

    def body(x_ref, g_ref, dxn_ref, dy_ref, gx_ref, dg_ref):
        xv = x_ref[...]
        r = lax.rsqrt(jnp.mean(xv * xv, axis=-1, keepdims=True) + NORM_EPS)
        nrm = xv * r
        d = dxn_ref[...].astype(F32)
        dn = d * g_ref[...]
        gx_ref[...] = dy_ref[...].astype(F32) + r * (dn - nrm * jnp.mean(dn * nrm, axis=-1, keepdims=True))

        @pl.when(pl.program_id(0) == 0)
        def _():
            dg_ref[...] = jnp.zeros_like(dg_ref)

        dg_ref[...] += _acc8(d * nrm)

    return _call(body, name="norm1_bwd", grid=(s // tm,),
                 in_specs=[_rows(tm, D_MODEL), _full((1, D_MODEL)), _rows(tm, D_MODEL), _rows(tm, D_MODEL)],
                 out_specs=[_rows(tm, D_MODEL), _full((8, D_MODEL))],
                 out_shape=[_sds((s, D_MODEL), F32), _sds((8, D_MODEL), F32)], args=(x, gain, dxn, dy))


def _rms(xv, width):
    return lax.rsqrt(jnp.sum(xv * xv, axis=-1, keepdims=True) * (1.0 / width) + NORM_EPS)


def _mla_a_norm(proj, gq, gkv):
    s = proj.shape[0]
    tm = min(TILES["row"], s)

    def body(cq_ref, ckv_ref, gq_ref, gkv_ref, oq_ref, okv_ref):
        a = cq_ref[...].astype(F32)
        oq_ref[...] = (a * _rms(a, 512) * gq_ref[...]).astype(BF16)
        b = ckv_ref[...].astype(F32)
        okv_ref[...] = (b * _rms(b, 256) * gkv_ref[...]).astype(BF16)

    return _call(body, name="mla_a_norm", grid=(s // tm,),
                 in_specs=[_rows(tm, 512, 0), _rows(tm, 256, 2), _full((1, 512)), _full((1, 256))],
                 out_specs=[_rows(tm, 512), _rows(tm, 256)],
                 out_shape=[_sds((s, 512), BF16), _sds((s, 256), BF16)], args=(proj, proj, gq, gkv))


def _rms_bwd(xv, gain, d, width):
    r = _rms(xv, width)
    nrm = xv * r
    dn = d * gain
    dx = r * (dn - nrm * (jnp.sum(dn * nrm, axis=-1, keepdims=True) * (1.0 / width)))
    return dx, d * nrm


def _mla_a_norm_bwd(proj, gq, gkv, dcqn, dckvn):
    s = proj.shape[0]
    tm = min(TILES["row"], s)

    def body(cq_ref, ckv_ref, gq_ref, gkv_ref, dq_ref, dkv_ref, oq_ref, okv_ref, aq_ref, akv_ref):
        dxq, gq_part = _rms_bwd(cq_ref[...].astype(F32), gq_ref[...], dq_ref[...].astype(F32), 512)
        dxk, gk_part = _rms_bwd(ckv_ref[...].astype(F32), gkv_ref[...], dkv_ref[...].astype(F32), 256)
        oq_ref[...] = dxq.astype(BF16)
        okv_ref[...] = dxk.astype(BF16)

        @pl.when(pl.program_id(0) == 0)
        def _():
            aq_ref[...] = jnp.zeros_like(aq_ref)
            akv_ref[...] = jnp.zeros_like(akv_ref)

        aq_ref[...] += _acc8(gq_part)
        akv_ref[...] += _acc8(gk_part)

    return _call(body, name="mla_a_norm_bwd", grid=(s // tm,),
                 in_specs=[_rows(tm, 512, 0), _rows(tm, 256, 2), _full((1, 512)), _full((1, 256)),
                           _rows(tm, 512), _rows(tm, 256)],
                 out_specs=[_rows(tm, 512), _rows(tm, 256), _full((8, 512)), _full((8, 256))],
                 out_shape=[_sds((s, 512), BF16), _sds((s, 256), BF16), _sds((8, 512), F32), _sds((8, 256), F32)],
                 args=(proj, proj, gq, gkv, dcqn, dckvn))


def _swap32(r):
    lane = lax.broadcasted_iota(jnp.int32, r.shape, 1)
    return jnp.where(lane < HALF_ROPE, pltpu.roll(r, 128 - HALF_ROPE, 1), pltpu.roll(r, HALF_ROPE, 1))


def _mla_post_fwd(q_pre, kv_pre, proj, cs, sn, gq, gk):
    s = q_pre.shape[0]
    tm = min(TILES["row"], s)

    def body(qp_ref, kvp_ref, misc_ref, cs_ref, sn_ref, gq_ref, gk_ref, q_ref, k_ref, v_ref):
        csv, snv = cs_ref[...], sn_ref[...]
        lane = lax.broadcasted_iota(jnp.int32, (tm, 128), 1)
        kr = jnp.where(lane < 64, misc_ref[...], 0.0)
        for h in range(HEADS):
            for src, g_ref, o_ref in ((None, gq_ref, q_ref), (kr, gk_ref, k_ref)):
                if src is None:
                    xv = qp_ref[:, QK_PAD * h:QK_PAD * (h + 1)].astype(F32)
                else:
                    xv = jnp.concatenate([kvp_ref[:, 256 * h:256 * h + 128].astype(F32), src], axis=-1)
                y = xv * _rms(xv, QK_DIM) * g_ref[...]
                if src is None:
                    y = y * Q_PRESCALE
                hi = y[:, 128:]
                hi = hi * csv + _swap32(hi) * snv
                o_ref[:, QK_PAD * h:QK_PAD * h + 128] = y[:, :128].astype(BF16)
                o_ref[:, QK_PAD * h + 128:QK_PAD * (h + 1)] = hi.astype(BF16)
            v_ref[:, 128 * h:128 * (h + 1)] = kvp_ref[:, 256 * h + 128:256 * (h + 1)].astype(BF16)

    return _call(body, name="mla_post_fwd", grid=(s // tm,),
                 in_specs=[_rows(tm, 2048), _rows(tm, 2048), _rows(tm, 128), _rows(tm, 128), _rows(tm, 128),
                           _full((1, QK_PAD)), _full((1, QK_PAD))],
                 out_specs=[_rows(tm, 2048), _rows(tm, 2048), _rows(tm, 1024)],
                 out_shape=[_sds((s, 2048), BF16), _sds((s, 2048), BF16), _sds((s, 1024), BF16)],
                 args=(q_pre, kv_pre, proj, cs, sn, gq, gk))


def _mla_post_bwd(q_pre, kv_pre, proj, cs, sn, gq, gk, dq, dk, dv):
    s = q_pre.shape[0]
    tm = min(TILES["row"], s)

    def body(qp_ref, kvp_ref, misc_ref, cs_ref, sn_ref, gq_ref, gk_ref, dq_ref, dk_ref, dv_ref,
             oq_ref, okv_ref, okr_ref, agq_ref, agk_ref):
        csv, snv = cs_ref[...], sn_ref[...]
        lane = lax.broadcasted_iota(jnp.int32, (tm, 128), 1)
        kr = jnp.where(lane < 64, misc_ref[...], 0.0)

        @pl.when(pl.program_id(0) == 0)
        def _():
            agq_ref[...] = jnp.zeros_like(agq_ref)
            agk_ref[...] = jnp.zeros_like(agk_ref)

        dkr = jnp.zeros((tm, 128), F32)
        for h in range(HEADS):
            for which in (0, 1):
                if which == 0:
                    xv = qp_ref[:, QK_PAD * h:QK_PAD * (h + 1)].astype(F32)
                    d_ref, g_ref, a_ref = dq_ref, gq_ref, agq_ref
                else:
                    xv = jnp.concatenate([kvp_ref[:, 256 * h:256 * h + 128].astype(F32), kr], axis=-1)
                    d_ref, g_ref, a_ref = dk_ref, gk_ref, agk_ref
                dhi = d_ref[:, QK_PAD * h + 128:QK_PAD * (h + 1)].astype(F32)
                dhi = dhi * csv - _swap32(dhi) * snv
                dyv = jnp.concatenate([d_ref[:, QK_PAD * h:QK_PAD * h + 128].astype(F32), dhi], axis=-1)
                if which == 0:
                    dyv = dyv * ATTN_SCALE
                dx, gpart = _rms_bwd(xv, g_ref[...], dyv, QK_DIM)
                a_ref[...] += _acc8(gpart)
                if which == 0:
                    oq_ref[:, QK_PAD * h:QK_PAD * (h + 1)] = dx.astype(BF16)
                else:
                    okv_ref[:, 256 * h:256 * h + 128] = dx[:, :128].astype(BF16)
                    dkr = dkr + dx[:, 128:]
            okv_ref[:, 256 * h + 128:256 * (h + 1)] = dv_ref[:, 128 * h:128 * (h + 1)].astype(BF16)
        okr_ref[...] = dkr

    return _call(body, name="mla_post_bwd", grid=(s // tm,),
                 in_specs=[_rows(tm, 2048), _rows(tm, 2048), _rows(tm, 128), _rows(tm, 128), _rows(tm, 128),
                           _full((1, QK_PAD)), _full((1, QK_PAD)), _rows(tm, 2048), _rows(tm, 2048), _rows(tm, 1024)],
                 out_specs=[_rows(tm, 2048), _rows(tm, 2048), _rows(tm, 128), _full((8, QK_PAD)), _full((8, QK_PAD))],
                 out_shape=[_sds((s, 2048), BF16), _sds((s, 2048), BF16), _sds((s, 128), F32),
                            _sds((8, QK_PAD), F32), _sds((8, QK_PAD), F32)],
                 args=(q_pre, kv_pre, proj, cs, sn, gq, gk, dq, dk, dv))


ATTN_SCALE = QK_DIM ** -0.5
NEG = -1e30


LOG2E = 1.4426950408889634
LN2 = 0.6931471805599453
Q_PRESCALE = ATTN_SCALE * LOG2E
ATTN_SUB_FWD = 512
ATTN_SUB_BWD = 256


def _causal_pairs(nq, kv_major):
    prs = [(i, j) for i in range(nq) for j in range(i + 1)]
    if kv_major:
        prs.sort(key=lambda ij: (ij[1], ij[0]))
    return (jnp.asarray([p[0] for p in prs], jnp.int32), jnp.asarray([p[1] for p in prs], jnp.int32))


def _pair_call(body, *, name, tables, in_specs, out_specs, out_shape, scratch, args):
    def kfn(*refs):
        body(*refs)
    spec = pltpu.PrefetchScalarGridSpec(num_scalar_prefetch=2, grid=(HEADS, tables[0].shape[0]), in_specs=in_specs,
                                        out_specs=out_specs, scratch_shapes=list(scratch))
    return pl.pallas_call(
        kfn, name=name, grid_spec=spec, out_shape=out_shape,
        compiler_params=pltpu.CompilerParams(dimension_semantics=("parallel", "arbitrary"),
                                             vmem_limit_bytes=VMEM_LIMIT_V7X))(*tables, *args)


def _diag_mask(sc, ts, qs):
    row = lax.broadcasted_iota(jnp.int32, sc.shape, 0) + qs * ts
    col = lax.broadcasted_iota(jnp.int32, sc.shape, 1)
    return jnp.where(col <= row, sc, NEG)


def _attn_fwd(q, k, v):
    s = q.shape[0]
    t = min(TILES["attn"], s)
    ts = min(ATTN_SUB_FWD, t)
    nq = s // t

    def slabs(q_ref, k_ref, v_ref, m_s, l_s, acc_s, diag):
        def scores(qs):
            kw = (qs + 1) * ts if diag else t
            sc = lax.dot_general(q_ref[qs * ts:(qs + 1) * ts, :], k_ref[0:kw, :], NT, preferred_element_type=F32)
            return _diag_mask(sc, ts, qs) if diag else sc

        nsub = t // ts
        sc_next = scores(0)
        for qs in range(nsub):
            rq = slice(qs * ts, (qs + 1) * ts)
            kw = (qs + 1) * ts if diag else t
            sc = sc_next
            if qs + 1 < nsub:
                sc_next = scores(qs + 1)
            m_prev = m_s[rq, :]
            m_new = jnp.maximum(m_prev, jnp.max(sc, axis=-1, keepdims=True))
            p = jnp.exp2(sc - m_new)
            alpha = jnp.exp2(m_prev - m_new)
            l_s[rq, :] = alpha * l_s[rq, :] + jnp.sum(p, axis=-1, keepdims=True)
            acc_s[rq, :] = acc_s[rq, :] * alpha + lax.dot_general(p.astype(BF16), v_ref[0:kw, :], NN,
                                                                  preferred_element_type=F32)
            m_s[rq, :] = m_new

    def body(it_ref, jt_ref, q_ref, k_ref, v_ref, o_ref, lse_ref, m_s, l_s, acc_s):
        p = pl.program_id(1)
        i, j = it_ref[p], jt_ref[p]

        @pl.when(j == 0)
        def _():
            m_s[...] = jnp.full_like(m_s, NEG)
            l_s[...] = jnp.zeros_like(l_s)
            acc_s[...] = jnp.zeros_like(acc_s)

        @pl.when(j < i)
        def _():
            slabs(q_ref, k_ref, v_ref, m_s, l_s, acc_s, False)

        @pl.when(j == i)
        def _():
            slabs(q_ref, k_ref, v_ref, m_s, l_s, acc_s, True)
            o_ref[...] = acc_s[...] / l_s[...]
            lse_ref[...] = m_s[...] + jnp.log2(l_s[...])

    qb = lambda h, p, it, jt: (it[p], h)
    kb = lambda h, p, it, jt: (jt[p], h)
    return _pair_call(
        body, name="attn_fwd", tables=_causal_pairs(nq, kv_major=False),
        in_specs=[pl.BlockSpec((t, QK_PAD), qb), pl.BlockSpec((t, QK_PAD), kb), pl.BlockSpec((t, HEAD_DIM), kb)],
        out_specs=[pl.BlockSpec((t, HEAD_DIM), qb),
                   pl.BlockSpec((None, t, 1), lambda h, p, it, jt: (h, it[p], 0))],
        out_shape=[_sds((s, HEADS * HEAD_DIM), F32), _sds((HEADS, s, 1), F32)],
        scratch=[pltpu.VMEM((t, 1), F32), pltpu.VMEM((t, 1), F32), pltpu.VMEM((t, HEAD_DIM), F32)],
        args=(q, k, v))


def _attn_bwd(q, k, v, lse_rows, delta_rows, do):
    s = q.shape[0]
    t = min(TILES["attn"], s)
    ts = min(ATTN_SUB_BWD, t)
    nq = s // t

    def slabs(q_ref, k_ref, v_ref, lse_ref, delta_ref, do_ref, dq_ref, dk_ref, dv_ref, i, diag):
        def products(qs):
            rq = slice(qs * ts, (qs + 1) * ts)
            kw = (qs + 1) * ts if diag else t
            qv, dob = q_ref[rq, :], do_ref[rq, :]
            sct = lax.dot_general(k_ref[0:kw, :], qv, NT, preferred_element_type=F32)
            dpt = lax.dot_general(v_ref[0:kw, :], dob, NT, preferred_element_type=F32)
            if diag:
                row = lax.broadcasted_iota(jnp.int32, sct.shape, 0)
                col = lax.broadcasted_iota(jnp.int32, sct.shape, 1) + qs * ts
                sct = jnp.where(row <= col, sct, NEG)
            return qv, dob, sct, dpt

        nsub = t // ts
        ahead = products(0)
        for qs in range(nsub):
            rq = slice(qs * ts, (qs + 1) * ts)
            kw = (qs + 1) * ts if diag else t
            qv, dob, sct, dpt = ahead
            if qs + 1 < nsub:
                ahead = products(qs + 1)
            pt = jnp.exp2(sct - lse_ref[:, rq])
            dv_ref[0:kw, :] += lax.dot_general(pt.astype(BF16), dob, NN, preferred_element_type=F32)
            dst = (pt * (dpt - delta_ref[:, rq])).astype(BF16)
            dk_ref[0:kw, :] += lax.dot_general(dst, qv, NN, preferred_element_type=F32)
            rows = pl.ds(pl.multiple_of(i * t + qs * ts, ts), ts)
            dq_ref[rows, :] += lax.dot_general(dst, k_ref[0:kw, :], TN, preferred_element_type=F32)

    npairs = nq * (nq + 1) // 2

    def body(it_ref, jt_ref, q_ref, k_ref, v_ref, lse_ref, delta_ref, do_ref, dq_ref, dk_ref, dv_ref,
             dq_acc, dk_acc, dv_acc):
        p = pl.program_id(1)
        i, j = it_ref[p], jt_ref[p]
        refs = (q_ref, k_ref, v_ref, lse_ref, delta_ref, do_ref, dq_acc, dk_acc, dv_acc)

        @pl.when(p == 0)
        def _():
            dq_acc[...] = jnp.zeros_like(dq_acc)

        @pl.when(i == j)
        def _():
            dk_acc[...] = jnp.zeros_like(dk_acc)
            dv_acc[...] = jnp.zeros_like(dv_acc)

        @pl.when(i > j)
        def _():
            slabs(*refs, i, False)

        @pl.when(i == j)
        def _():
            slabs(*refs, i, True)

        @pl.when(i == nq - 1)
        def _():
            dk_ref[...] = (dk_acc[...] * LN2).astype(BF16)
            dv_ref[...] = dv_acc[...].astype(BF16)

        @pl.when(p == npairs - 1)
        def _():
            dq_ref[...] = dq_acc[...].astype(BF16)

    qb = lambda h, p, it, jt: (it[p], h)
    kb = lambda h, p, it, jt: (jt[p], h)
    rowb = pl.BlockSpec((None, 1, t), lambda h, p, it, jt: (h, 0, it[p]))
    return _pair_call(
        body, name="attn_bwd", tables=_causal_pairs(nq, kv_major=True),
        in_specs=[pl.BlockSpec((t, QK_PAD), qb), pl.BlockSpec((t, QK_PAD), kb), pl.BlockSpec((t, HEAD_DIM), kb),
                  rowb, rowb, pl.BlockSpec((t, HEAD_DIM), qb)],
        out_specs=[pl.BlockSpec((s, QK_PAD), lambda h, p, it, jt: (0, h)), pl.BlockSpec((t, QK_PAD), kb),
                   pl.BlockSpec((t, HEAD_DIM), kb)],
        out_shape=[_sds((s, HEADS * QK_PAD), BF16), _sds((s, HEADS * QK_PAD), BF16), _sds((s, HEADS * HEAD_DIM), BF16)],
        scratch=[pltpu.VMEM((s, QK_PAD), F32), pltpu.VMEM((t, QK_PAD), F32), pltpu.VMEM((t, HEAD_DIM), F32)],
        args=(q, k, v, lse_rows, delta_rows, do))


GDN_Q_SCALE = HEAD_DIM ** -0.5


def _shift_down(xv, prev8, sft):
    rolled = pltpu.roll(xv, sft, 0)
    top = pltpu.roll(jnp.concatenate([prev8, xv[:8]], axis=0), sft, 0)[8:]
    return jnp.concatenate([top, rolled[8:]], axis=0)


def _shift_up(xv, next8, sft):
    tm = xv.shape[0]
    rolled = pltpu.roll(xv, tm - sft, 0)
    bot = pltpu.roll(jnp.concatenate([xv[tm - 8:], next8], axis=0), 16 - sft, 0)[:8]
    return jnp.concatenate([rolled[:tm - 8], bot], axis=0)


def _conv_z(xv, prev8, w_ref):
    z = xv * w_ref[3:4, :]
    for sft in (1, 2, 3):
        z = z + _shift_down(xv, prev8, sft) * w_ref[3 - sft:4 - sft, :]
    return z


def _conv_specs(s, tm):
    nb16 = tm // 16
    cur = pl.BlockSpec((tm, 1024), lambda j, i: (i, 2 + j))
    prev = pl.BlockSpec((16, 1024), lambda j, i: (jnp.maximum(i * nb16 - 1, 0), 2 + j))
    return cur, prev


def _prev8(xp_ref, i):
    return jnp.where(i > 0, xp_ref[...].astype(F32)[8:], 0.0)


def _gdn_conv_fwd(proj, conv_w):
    s = proj.shape[0]
    tm = min(TILES["row"], s)
    cur, prev = _conv_specs(s, tm)

    def body(x_ref, xp_ref, w_ref, o_ref):
        j, i = pl.program_id(0), pl.program_id(1)
        a = _silu(_conv_z(x_ref[...].astype(F32), _prev8(xp_ref, i), w_ref))
        qk_scale = jnp.where(j == 0, GDN_Q_SCALE, 1.0)
        for h in range(HEADS):
            seg = a[:, 128 * h:128 * (h + 1)]
            r = lax.rsqrt(jnp.sum(seg * seg, axis=-1, keepdims=True) + NORM_EPS)
            o_ref[:, 128 * h:128 * (h + 1)] = jnp.where(j < 2, seg * r * qk_scale, seg).astype(BF16)

    return _call(body, name="gdn_conv_fwd", grid=(3, s // tm),
                 in_specs=[cur, prev, pl.BlockSpec((4, 1024), lambda j, i: (0, j))],
                 out_specs=pl.BlockSpec((tm, 1024), lambda j, i: (i, j)), out_shape=_sds((s, 3072), BF16),
                 args=(proj, proj, conv_w))


def _gdn_conv_bwd_a(proj, conv_w, dqkv):
    s = proj.shape[0]
    tm = min(TILES["row"], s)
    cur, prev = _conv_specs(s, tm)

    def body(x_ref, xp_ref, w_ref, d_ref, o_ref):
        j, i = pl.program_id(0), pl.program_id(1)
        z = _conv_z(x_ref[...].astype(F32), _prev8(xp_ref, i), w_ref)
        a = _silu(z)
        dsl = _dsilu(z)
        qk_scale = jnp.where(j == 0, GDN_Q_SCALE, 1.0)
        for h in range(HEADS):
            sl = slice(128 * h, 128 * (h + 1))
            seg = a[:, sl]
            dyv = d_ref[:, sl].astype(F32)
            r = lax.rsqrt(jnp.sum(seg * seg, axis=-1, keepdims=True) + NORM_EPS)
            yh = seg * r
            da_n = qk_scale * r * (dyv - yh * jnp.sum(yh * dyv, axis=-1, keepdims=True))
            o_ref[:, sl] = (jnp.where(j < 2, da_n, dyv) * dsl[:, sl]).astype(BF16)

    return _call(body, name="gdn_conv_bwd_a", grid=(3, s // tm),
                 in_specs=[cur, prev, pl.BlockSpec((4, 1024), lambda j, i: (0, j)),
                           pl.BlockSpec((tm, 1024), lambda j, i: (i, j))],
                 out_specs=pl.BlockSpec((tm, 1024), lambda j, i: (i, j)), out_shape=_sds((s, 3072), BF16),
                 args=(proj, proj, conv_w, dqkv))


def _gdn_conv_bwd_b(proj, conv_w, dz):
    s = proj.shape[0]
    tm = min(TILES["row"], s)
    nb16 = tm // 16
    last16 = s // 16 - 1
    cur, prev = _conv_specs(s, tm)

    def body(x_ref, w_ref, dz_ref, dzn_ref, dx_ref, dw_ref):
        i = pl.program_id(1)
        next8 = jnp.where(i < pl.num_programs(1) - 1, dzn_ref[...].astype(F32)[:8], 0.0)
        xv, dzv = x_ref[...].astype(F32), dz_ref[...].astype(F32)

        @pl.when(i == 0)
        def _():
            dw_ref[...] = jnp.zeros_like(dw_ref)

        dx = dzv * w_ref[3:4, :]
        dw_ref[3:4, :] += jnp.sum(dzv * xv, axis=0, keepdims=True)
        for sft in (1, 2, 3):
            up = _shift_up(dzv, next8, sft)
            dx = dx + up * w_ref[3 - sft:4 - sft, :]
            dw_ref[3 - sft:4 - sft, :] += jnp.sum(up * xv, axis=0, keepdims=True)
        dx_ref[...] = dx.astype(BF16)

    return _call(body, name="gdn_conv_bwd_b", grid=(3, s // tm),
                 in_specs=[cur, pl.BlockSpec((4, 1024), lambda j, i: (0, j)),
                           pl.BlockSpec((tm, 1024), lambda j, i: (i, j)),
                           pl.BlockSpec((16, 1024), lambda j, i: (jnp.minimum((i + 1) * nb16, last16), j))],
                 out_specs=[pl.BlockSpec((tm, 1024), lambda j, i: (i, j)), pl.BlockSpec((4, 1024), lambda j, i: (0, j))],
                 out_shape=[_sds((s, 3072), BF16), _sds((4, 3072), F32)], args=(proj, conv_w, dz, dz))


def _softplus(xv):
    return jnp.maximum(xv, 0.0) + jnp.log(1.0 + jnp.exp(-jnp.abs(xv)))


def _gdn_gates_fwd(proj, alog128, dtb128):
    s = proj.shape[0]
    tm = min(TILES["row"], s)

    def body(m_ref, a_ref, b_ref, o_ref):
        mv = m_ref[...]
        lane = lax.broadcasted_iota(jnp.int32, mv.shape, 1)
        g = -jnp.exp(a_ref[...]) * _softplus(mv + b_ref[...])
        is_g = (lane >= GA_LANE) & (lane < GA_LANE + HEADS)
        is_b = (lane >= GB_LANE) & (lane < GB_LANE + HEADS)
        o_ref[...] = jnp.where(is_g, g, jnp.where(is_b, _sigmoid(mv), 0.0))

    return _call(body, name="gdn_gates_fwd", grid=(s // tm,),
                 in_specs=[_rows(tm, 128), _full((1, 128)), _full((1, 128))],
                 out_specs=_rows(tm, 128), out_shape=_sds((s, 128), F32), args=(proj, alog128, dtb128))


def _gdn_gates_bwd(proj, alog128, dtb128, gbeta, dgbeta, dkr):
    s = proj.shape[0]
    tm = min(TILES["row"], s)

    def body(m_ref, a_ref, b_ref, gb_ref, d_ref, kr_ref, o_ref, da_ref, db_ref):
        mv, dv = m_ref[...], d_ref[...]
        lane = lax.broadcasted_iota(jnp.int32, mv.shape, 1)
        is_g = (lane >= GA_LANE) & (lane < GA_LANE + HEADS)
        is_b = (lane >= GB_LANE) & (lane < GB_LANE + HEADS)
        dga = jnp.where(is_g, dv * (-jnp.exp(a_ref[...])) * _sigmoid(mv + b_ref[...]), 0.0)
        beta = gb_ref[...]
        dgb = jnp.where(is_b, dv * beta * (1.0 - beta), 0.0)
        o_ref[...] = jnp.where(lane < 64, kr_ref[...], dga + dgb).astype(BF16)

        @pl.when(pl.program_id(0) == 0)
        def _():
            da_ref[...] = jnp.zeros_like(da_ref)
            db_ref[...] = jnp.zeros_like(db_ref)

        da_ref[...] += _acc8(jnp.where(is_g, dv * gb_ref[...], 0.0))
        db_ref[...] += _acc8(dga)

    return _call(body, name="gdn_gates_bwd", grid=(s // tm,),
                 in_specs=[_rows(tm, 128), _full((1, 128)), _full((1, 128)), _rows(tm, 128), _rows(tm, 128),
                           _rows(tm, 128)],
                 out_specs=[_rows(tm, 128), _full((8, 128)), _full((8, 128))],
                 out_shape=[_sds((s, 128), BF16), _sds((8, 128), F32), _sds((8, 128), F32)],
                 args=(proj, alog128, dtb128, gbeta, dgbeta, dkr))


def _col(mat, lane_idx, lane):
    return jnp.sum(jnp.where(lane == lane_idx, mat, 0.0), axis=-1, keepdims=True)


def _chunk_local(qh, kh, vh, gcol, bcol, ii, jj):
    lower, strict, eye = ii >= jj, ii > jj, ii == jj
    grow = jnp.sum(jnp.where(eye, gcol, 0.0), axis=0, keepdims=True)
    decay = jnp.where(lower, jnp.exp(jnp.where(lower, gcol - grow, 0.0)), 0.0)
    kb = kh * bcol
    vb = vh * bcol
    mm = _dot(kb, kh, NT)
    lmat = jnp.where(strict, mm * decay, 0.0)
    pw = -lmat
    tinv = jnp.where(eye, 1.0, 0.0) + pw
    for _ in range(5):
        pw = _dot_hi(pw, pw)
        tinv = tinv + _dot_hi(tinv, pw)
    egc = jnp.exp(gcol)
    kbg = kb * egc
    rhs = jnp.concatenate([vb, kbg], axis=-1)
    sol = _dot_hi(tinv, rhs)
    qk = _dot(qh, kh, NT)
    glast = jnp.sum(jnp.where(ii[:, :1] == CHUNK - 1, gcol, 0.0), axis=0, keepdims=True)
    ekd = jnp.exp(glast - gcol)
    return dict(decay=decay, kb=kb, vb=vb, mm=mm, lmat=lmat, tinv=tinv, egc=egc, kbg=kbg, rhs=rhs,
                u=sol[:, :HEAD_DIM], w=sol[:, HEAD_DIM:], qk=qk, amat=qk * decay, qd=qh * egc, ekd=ekd,
                kd=kh * ekd, gl=jnp.exp(glast), strict=strict, lower=lower, eye=eye)


def _tri(ii, jj):
    return jnp.where(ii >= jj, 1.0, 0.0)


def _gdn_fwd(qkv, gbeta):
    s = qkv.shape[0]
    n = s // CHUNK

    def body(qkv_ref, gb_ref, o_ref, st_ref, state):
        @pl.when(pl.program_id(0) == 0)
        def _():
            state[...] = jnp.zeros_like(state)

        ii = lax.broadcasted_iota(jnp.int32, (CHUNK, CHUNK), 0)
        jj = lax.broadcasted_iota(jnp.int32, (CHUNK, CHUNK), 1)
        lane = lax.broadcasted_iota(jnp.int32, (CHUNK, 128), 1)
        gbv = gb_ref[...]
        gc = _dot_hi(_tri(ii, jj), gbv)
        for h in range(HEADS):
            sl = slice(128 * h, 128 * (h + 1))
            qh = qkv_ref[:, 128 * h:128 * (h + 1)]
            kh = qkv_ref[:, 1024 + 128 * h:1024 + 128 * (h + 1)]
            vh = qkv_ref[:, 2048 + 128 * h:2048 + 128 * (h + 1)]
            c = _chunk_local(qh, kh, vh, _col(gc, GA_LANE + h, lane), _col(gbv, GB_LANE + h, lane), ii, jj)
            st = state[sl, :]
            st_ref[sl, :] = st
            vn = c["u"] - _dot(c["w"], st)
            o_ref[:, sl] = _dot(c["qd"], st) + _dot(c["amat"], vn)
            state[sl, :] = st * c["gl"] + _dot(c["kd"], vn, TN)

    return _call(body, name="gdn_fwd", grid=(n,),
                 in_specs=[_rows(CHUNK, 3072), _rows(CHUNK, 128)],
                 out_specs=[_rows(CHUNK, 1024), _rows(HEADS * 128, 128)],
                 out_shape=[_sds((s, 1024), F32), _sds((n * HEADS * 128, 128), F32)],
                 scratch=[pltpu.VMEM((HEADS * 128, 128), F32)], args=(qkv, gbeta))


def _gdn_bwd(qkv, gbeta, states, do):
    s = qkv.shape[0]
    n = s // CHUNK

    def body(qkv_ref, gb_ref, st_ref, do_ref, dqkv_ref, dgb_ref, dstate):
        @pl.when(pl.program_id(0) == 0)
        def _():
            dstate[...] = jnp.zeros_like(dstate)

        ii = lax.broadcasted_iota(jnp.int32, (CHUNK, CHUNK), 0)
        jj = lax.broadcasted_iota(jnp.int32, (CHUNK, CHUNK), 1)
        lane = lax.broadcasted_iota(jnp.int32, (CHUNK, 128), 1)
        row1 = ii[:, :1]
        gbv = gb_ref[...]
        gc = _dot_hi(_tri(ii, jj), gbv)
        dgc_all = jnp.zeros((CHUNK, 128), F32)
        db_all = jnp.zeros((CHUNK, 128), F32)
        for h in range(HEADS):
            sl = slice(128 * h, 128 * (h + 1))
            qh = qkv_ref[:, 128 * h:128 * (h + 1)]
            kh = qkv_ref[:, 1024 + 128 * h:1024 + 128 * (h + 1)]
            vh = qkv_ref[:, 2048 + 128 * h:2048 + 128 * (h + 1)]
            bcol = _col(gbv, GB_LANE + h, lane)
            c = _chunk_local(qh, kh, vh, _col(gc, GA_LANE + h, lane), bcol, ii, jj)
            st = st_ref[sl, :]
            dst = dstate[sl, :]
            dov = do_ref[:, sl]
            vn = c["u"] - _dot(c["w"], st)
            dvn = _dot(c["amat"], dov, TN) + _dot(c["kd"], dst)
            damat = jnp.where(c["lower"], _dot(dov, vn, NT), 0.0)
            dqd = _dot(dov, st, NT)
            dkd = _dot(vn, dst, NT)
            dw = -_dot(dvn, st, NT)
            dgl = jnp.sum(jnp.sum(st * dst, axis=-1, keepdims=True), axis=0, keepdims=True)
            dstate[sl, :] = _dot(c["qd"], dov, TN) + c["gl"] * dst - _dot(c["w"], dvn, TN)
            dsol = jnp.concatenate([dvn, dw], axis=-1)
            drhs = _dot_hi(c["tinv"], dsol, TN)
            dtinv = _dot_hi(dsol, c["rhs"], NT)
            dl = -_dot_hi(_dot_hi(c["tinv"], dtinv, TN), c["tinv"], NT)
            dl = jnp.where(c["strict"], dl, 0.0)
            dmm = dl * c["decay"]
            dqk = damat * c["decay"]
            wmat = dl * c["lmat"] + damat * c["amat"]
            dgc = jnp.sum(wmat, axis=-1, keepdims=True)
            wcol = jnp.sum(wmat, axis=0, keepdims=True)
            dgc = dgc - jnp.sum(jnp.where(c["eye"], wcol, 0.0), axis=-1, keepdims=True)
            dkb = _dot(dmm, kh) + drhs[:, HEAD_DIM:] * c["egc"]
            dk = _dot(dmm, c["kb"], TN) + _dot(dqk, qh, TN) + dkd * c["ekd"]
            dq = _dot(dqk, kh) + dqd * c["egc"]
            dgc = dgc + jnp.sum(drhs[:, HEAD_DIM:] * c["kbg"], axis=-1, keepdims=True)
            dgc = dgc + jnp.sum(dqd * c["qd"], axis=-1, keepdims=True)
            tmp = jnp.sum(dkd * c["kd"], axis=-1, keepdims=True)
            dgc = dgc - tmp
            dglast = jnp.sum(tmp, axis=0, keepdims=True) + dgl * c["gl"]
            dgc = dgc + jnp.where(row1 == CHUNK - 1, dglast, 0.0)
            dk = dk + dkb * bcol
            db = jnp.sum(dkb * kh, axis=-1, keepdims=True) + jnp.sum(drhs[:, :HEAD_DIM] * vh, axis=-1, keepdims=True)
            dqkv_ref[:, 128 * h:128 * (h + 1)] = dq
            dqkv_ref[:, 1024 + 128 * h:1024 + 128 * (h + 1)] = dk
            dqkv_ref[:, 2048 + 128 * h:2048 + 128 * (h + 1)] = drhs[:, :HEAD_DIM] * bcol
            dgc_all = dgc_all + jnp.where(lane == GA_LANE + h, dgc, 0.0)
            db_all = db_all + jnp.where(lane == GB_LANE + h, db, 0.0)
        dgb_ref[...] = _dot_hi(_tri(jj, ii), dgc_all) + db_all

    rev = lambda w: pl.BlockSpec((CHUNK, w), lambda i: (n - 1 - i, 0))
    return _call(body, name="gdn_bwd", grid=(n,),
                 in_specs=[rev(3072), rev(128), pl.BlockSpec((HEADS * 128, 128), lambda i: (n - 1 - i, 0)), rev(1024)],
                 out_specs=[rev(3072), rev(128)],
                 out_shape=[_sds((s, 3072), F32), _sds((s, 128), F32)],
                 scratch=[pltpu.VMEM((HEADS * 128, 128), F32)], args=(qkv, gbeta, states, do))


NN_B = (((2,), (1,)), ((0,), (0,)))
NT_B = (((2,), (2,)), ((0,), (0,)))
TN_B = (((1,), (1,)), ((0,), (0,)))
GDN_PRE_CHUNKS = 4
GDN_POST_CHUNKS = 2
GDN_SEQ_CHUNKS = 8


def _gather_heads(qkv_ref, gc, gbv, qs, ks, vs, gs, bs, nchunks):
    lane = lax.broadcasted_iota(jnp.int32, (CHUNK, 128), 1)
    for c in range(nchunks):
        rows = slice(CHUNK * c, CHUNK * (c + 1))
        for h in range(HEADS):
            b = HEADS * c + h
            qs[b] = qkv_ref[rows, 128 * h:128 * (h + 1)].astype(F32)
            ks[b] = qkv_ref[rows, 1024 + 128 * h:1024 + 128 * (h + 1)].astype(F32)
            vs[b] = qkv_ref[rows, 2048 + 128 * h:2048 + 128 * (h + 1)].astype(F32)
            gs[b] = jnp.broadcast_to(_col(gc[rows], GA_LANE + h, lane), (CHUNK, 128))
            bs[b] = jnp.broadcast_to(_col(gbv[rows], GB_LANE + h, lane), (CHUNK, 128))


def _block_tri(rows, transpose=False):
    ri = lax.broadcasted_iota(jnp.int32, (rows, rows), 0)
    ci = lax.broadcasted_iota(jnp.int32, (rows, rows), 1)
    same = (ri >> 6) == (ci >> 6)
    return jnp.where(same & ((ci >= ri) if transpose else (ri >= ci)), 1.0, 0.0)


def _local_b(q, k, v, g128, b128):
    ii = lax.broadcasted_iota(jnp.int32, (1, CHUNK, CHUNK), 1)
    jj = lax.broadcasted_iota(jnp.int32, (1, CHUNK, CHUNK), 2)
    lower, strict, eye = ii >= jj, ii > jj, ii == jj
    g64 = g128[:, :, :CHUNK]
    grow = jnp.sum(jnp.where(eye, g64, 0.0), axis=1, keepdims=True)
    decay = jnp.where(lower, jnp.exp(jnp.where(lower, g64 - grow, 0.0)), 0.0)
    kb = k * b128
    vb = v * b128
    mm = lax.dot_general(kb.astype(BF16), k.astype(BF16), NT_B, preferred_element_type=F32)
    lmat = jnp.where(strict, mm * decay, 0.0)
    egc = jnp.exp(g128)
    kbg = kb * egc
    qk = lax.dot_general(q.astype(BF16), k.astype(BF16), NT_B, preferred_element_type=F32)
    row = lax.broadcasted_iota(jnp.int32, (1, CHUNK, 128), 1)
    glast = jnp.sum(jnp.where(row == CHUNK - 1, g128, 0.0), axis=1, keepdims=True)
    ekd = jnp.exp(glast - g128)
    return dict(decay=decay, kb=kb, vb=vb, lmat=lmat, egc=egc, kbg=kbg, amat=qk * decay, qd=q * egc, ekd=ekd,
                kd=k * ekd, gl=jnp.exp(glast), lower=lower, strict=strict, eye=eye)


def _bdot(a, b, dims):
    return lax.dot_general(a.astype(BF16), b.astype(BF16), dims, preferred_element_type=F32)


def _split(a):
    hi = a.astype(BF16)
    return hi, (a - hi.astype(F32)).astype(BF16)


def _bdot_hi(a, b, dims):
    ah, al = _split(a)
    bh, bl = _split(b)
    d = lambda x, y: lax.dot_general(x, y, dims, preferred_element_type=F32)
    return d(ah, bh) + d(ah, bl) + d(al, bh)


def _gdn_pre(qkv, gbeta):
    s = qkv.shape[0]
    n = s // CHUNK
    cb = min(GDN_PRE_CHUNKS, n)
    nb = cb * HEADS
    rows = cb * CHUNK

    def body(qkv_ref, gb_ref, u_ref, w_ref, qd_ref, kd_ref, a_ref, t_ref, gl_ref, qs, ks, vs, gs, bs):
        gbv = gb_ref[...]
        gc = _dot_hi(_block_tri(rows), gbv)
        _gather_heads(qkv_ref, gc, gbv, qs, ks, vs, gs, bs, cb)
        c = _local_b(qs[...], ks[...], vs[...], gs[...], bs[...])
        pw = -c["lmat"]
        tinv = jnp.where(c["eye"], 1.0, 0.0) + pw
        for _ in range(5):
            pw = _bdot_hi(pw, pw, NN_B)
            tinv = tinv + _bdot_hi(tinv, pw, NN_B)
        u_ref[...] = _bdot_hi(tinv, c["vb"], NN_B)
        w_ref[...] = _bdot_hi(tinv, c["kbg"], NN_B).astype(BF16)
        qd_ref[...] = c["qd"].astype(BF16)
        kd_ref[...] = c["kd"].astype(BF16)
        a_ref[...] = c["amat"].astype(BF16)
        t_ref[...] = tinv
        gl_ref[...] = c["gl"]

    b3 = lambda d: pl.BlockSpec((nb, CHUNK, d), lambda i: (i, 0, 0))
    nt = n * HEADS
    return _call(body, name="gdn_pre", grid=(n // cb,),
                 in_specs=[_rows(rows, 3072), _rows(rows, 128)],
                 out_specs=[b3(128), b3(128), b3(128), b3(128), b3(CHUNK), b3(CHUNK),
                            pl.BlockSpec((nb, 1, 128), lambda i: (i, 0, 0))],
                 out_shape=[_sds((nt, CHUNK, 128), F32), _sds((nt, CHUNK, 128), BF16), _sds((nt, CHUNK, 128), BF16),
                            _sds((nt, CHUNK, 128), BF16), _sds((nt, CHUNK, CHUNK), BF16), _sds((nt, CHUNK, CHUNK), F32),
                            _sds((nt, 1, 128), F32)],
                 scratch=[pltpu.VMEM((nb, CHUNK, 128), F32)] * 5, sem=("parallel",), args=(qkv, gbeta))


def _gdn_scan_fwd(u, w, qd, kd, amat, gl):
    nt = u.shape[0]
    n = nt // HEADS
    cs = min(GDN_SEQ_CHUNKS, n)

    def body(u_ref, w_ref, qd_ref, kd_ref, a_ref, gl_ref, o_ref, st_ref, state):
        @pl.when(pl.program_id(0) == 0)
        def _():
            state[...] = jnp.zeros_like(state)

        for c in range(cs):
            sl = slice(HEADS * c, HEADS * (c + 1))
            st = state[...]
            stb = st.astype(BF16)
            st_ref[sl] = stb
            vn = u_ref[sl] - lax.dot_general(w_ref[sl], stb, NN_B, preferred_element_type=F32)
            vnb = vn.astype(BF16)
            o = (lax.dot_general(qd_ref[sl], stb, NN_B, preferred_element_type=F32)
                 + lax.dot_general(a_ref[sl], vnb, NN_B, preferred_element_type=F32))
            state[...] = st * gl_ref[sl] + lax.dot_general(kd_ref[sl], vnb, TN_B, preferred_element_type=F32)
            for h in range(HEADS):
                o_ref[CHUNK * c:CHUNK * (c + 1), 128 * h:128 * (h + 1)] = o[h]

    b3 = lambda d: pl.BlockSpec((cs * HEADS, CHUNK, d), lambda i: (i, 0, 0))
    return _call(body, name="gdn_scan_fwd", grid=(n // cs,),
                 in_specs=[b3(128), b3(128), b3(128), b3(128), b3(CHUNK), pl.BlockSpec((cs * HEADS, 1, 128), lambda i: (i, 0, 0))],
                 out_specs=[_rows(cs * CHUNK, 1024), pl.BlockSpec((cs * HEADS, 128, 128), lambda i: (i, 0, 0))],
                 out_shape=[_sds((n * CHUNK, 1024), F32), _sds((nt, 128, 128), BF16)],
                 scratch=[pltpu.VMEM((HEADS, 128, 128), F32)], args=(u, w, qd, kd, amat, gl))


def _gdn_scan_bwd(w, qd, kd, amat, gl, do):
    nt = w.shape[0]
    n = nt // HEADS
    cs = min(GDN_SEQ_CHUNKS, n)
    ng = n // cs

    def body(w_ref, qd_ref, kd_ref, a_ref, gl_ref, do_ref, ds_ref, dstate, dos):
        @pl.when(pl.program_id(0) == 0)
        def _():
            dstate[...] = jnp.zeros_like(dstate)

        for c in reversed(range(cs)):
            sl = slice(HEADS * c, HEADS * (c + 1))
            for h in range(HEADS):
                dos[h] = do_ref[CHUNK * c:CHUNK * (c + 1), 128 * h:128 * (h + 1)].astype(BF16)
            dob = dos[...]
            dst = dstate[...]
            dstb = dst.astype(BF16)
            ds_ref[sl] = dstb
            dvn = (lax.dot_general(a_ref[sl], dob, TN_B, preferred_element_type=F32)
                   + lax.dot_general(kd_ref[sl], dstb, NN_B, preferred_element_type=F32))
            dstate[...] = (lax.dot_general(qd_ref[sl], dob, TN_B, preferred_element_type=F32) + gl_ref[sl] * dst
                           - lax.dot_general(w_ref[sl], dvn.astype(BF16), TN_B, preferred_element_type=F32))

    b3 = lambda d: pl.BlockSpec((cs * HEADS, CHUNK, d), lambda i: (ng - 1 - i, 0, 0))
    return _call(body, name="gdn_scan_bwd", grid=(ng,),
                 in_specs=[b3(128), b3(128), b3(128), b3(CHUNK), pl.BlockSpec((cs * HEADS, 1, 128), lambda i: (ng - 1 - i, 0, 0)),
                           pl.BlockSpec((cs * CHUNK, 1024), lambda i: (ng - 1 - i, 0))],
                 out_specs=pl.BlockSpec((cs * HEADS, 128, 128), lambda i: (ng - 1 - i, 0, 0)),
                 out_shape=_sds((nt, 128, 128), BF16),
                 scratch=[pltpu.VMEM((HEADS, 128, 128), F32), pltpu.VMEM((HEADS, CHUNK, 128), BF16)],
                 args=(w, qd, kd, amat, gl, do))


def _gdn_post_bwd(qkv, gbeta, u, w, tinv, states, dstates, do):
    s = qkv.shape[0]
    n = s // CHUNK
    cb = min(GDN_POST_CHUNKS, n)
    nb = cb * HEADS
    rows = cb * CHUNK

    def body(qkv_ref, gb_ref, u_ref, w_ref, t_ref, st_ref, ds_ref, do_ref, dqkv_ref, dgb_ref, qs, ks, vs, gs, bs, dos):
        gbv = gb_ref[...]
        gc = _dot_hi(_block_tri(rows), gbv)
        _gather_heads(qkv_ref, gc, gbv, qs, ks, vs, gs, bs, cb)
        for c in range(cb):
            for h in range(HEADS):
                dos[HEADS * c + h] = do_ref[CHUNK * c:CHUNK * (c + 1), 128 * h:128 * (h + 1)].astype(F32)
        q, k, v, b128 = qs[...], ks[...], vs[...], bs[...]
        c = _local_b(q, k, v, gs[...], b128)
        tinv, st, dst, dov = t_ref[...], st_ref[...], ds_ref[...], dos[...]
        wv = w_ref[...]
        uv = u_ref[...]
        vn = uv - _bdot(wv, st, NN_B)
        dvn = _bdot(c["amat"], dov, TN_B) + _bdot(c["kd"], dst, NN_B)
        damat = jnp.where(c["lower"], _bdot(dov, vn, NT_B), 0.0)
        dqd = _bdot(dov, st, NT_B)
        dkd = _bdot(vn, dst, NT_B)
        dw = -_bdot(dvn, st, NT_B)
        dgl = jnp.sum(jnp.sum(st.astype(F32) * dst.astype(F32), axis=1, keepdims=True), axis=-1, keepdims=True)
        dvb = _bdot(tinv, dvn, TN_B)
        dkbg = _bdot(tinv, dw, TN_B)
        dl = -(_bdot(dvb, uv, NT_B) + _bdot(dkbg, wv, NT_B))
        dl = jnp.where(c["strict"], dl, 0.0)
        dmm = dl * c["decay"]
        dqk = damat * c["decay"]
        wmat = dl * c["lmat"] + damat * c["amat"]
        wcol = jnp.sum(wmat, axis=1, keepdims=True)
        dgc = jnp.sum(wmat, axis=-1, keepdims=True) - jnp.sum(jnp.where(c["eye"], wcol, 0.0), axis=-1, keepdims=True)
        dkb = _bdot(dmm, k, NN_B) + dkbg * c["egc"]
        dk = _bdot(dmm, c["kb"], TN_B) + _bdot(dqk, q, TN_B) + dkd * c["ekd"] + dkb * b128
        dq = _bdot(dqk, k, NN_B) + dqd * c["egc"]
        tmp = jnp.sum(dkd * c["kd"], axis=-1, keepdims=True)
        dgc = (dgc + jnp.sum(dkbg * c["kbg"], axis=-1, keepdims=True) + jnp.sum(dqd * c["qd"], axis=-1, keepdims=True)
               - tmp)
        dglast = jnp.sum(tmp, axis=1, keepdims=True) + dgl * c["gl"][:, :, :1]
        row1 = lax.broadcasted_iota(jnp.int32, (1, CHUNK, 1), 1)
        dgc = dgc + jnp.where(row1 == CHUNK - 1, dglast, 0.0)
        db = jnp.sum(dkb * k, axis=-1, keepdims=True) + jnp.sum(dvb * v, axis=-1, keepdims=True)
        dv = dvb * b128
        lane = lax.broadcasted_iota(jnp.int32, (CHUNK, 128), 1)
        parts = []
        for cc in range(cb):
            acc = jnp.zeros((CHUNK, 128), F32)
            for h in range(HEADS):
                bi = HEADS * cc + h
                rs = slice(CHUNK * cc, CHUNK * (cc + 1))
                dqkv_ref[rs, 128 * h:128 * (h + 1)] = dq[bi].astype(BF16)
                dqkv_ref[rs, 1024 + 128 * h:1024 + 128 * (h + 1)] = dk[bi].astype(BF16)
                dqkv_ref[rs, 2048 + 128 * h:2048 + 128 * (h + 1)] = dv[bi].astype(BF16)
                acc = acc + jnp.where(lane == GA_LANE + h, dgc[bi], 0.0)
            parts.append(acc)
        dgc_all = jnp.concatenate(parts, axis=0)
        dg_all = _dot_hi(_block_tri(rows, transpose=True), dgc_all)
        for cc in range(cb):
            acc = dg_all[CHUNK * cc:CHUNK * (cc + 1)]
            for h in range(HEADS):
                acc = acc + jnp.where(lane == GB_LANE + h, db[HEADS * cc + h], 0.0)
            dgb_ref[CHUNK * cc:CHUNK * (cc + 1), :] = acc

    b3 = lambda d1, d2: pl.BlockSpec((nb, d1, d2), lambda i: (i, 0, 0))
    return _call(body, name="gdn_post_bwd", grid=(n // cb,),
                 in_specs=[_rows(rows, 3072), _rows(rows, 128), b3(CHUNK, 128), b3(CHUNK, 128), b3(CHUNK, CHUNK),
                           b3(128, 128), b3(128, 128), _rows(rows, 1024)],
                 out_specs=[_rows(rows, 3072), _rows(rows, 128)],
                 out_shape=[_sds((s, 3072), BF16), _sds((s, 128), F32)],
                 scratch=[pltpu.VMEM((nb, CHUNK, 128), F32)] * 6, sem=("parallel",),
                 args=(qkv, gbeta, u, w, tinv, states, dstates, do))


def _mix_fwd(o_mla, o_gdn, proj, out_gain):
    s = proj.shape[0]
    tm = min(TILES["row"], s)

    def body(om_ref, og_ref, mg_ref, gg_ref, g_ref, o_ref):
        o_ref[:, :1024] = (om_ref[...] * _silu(mg_ref[...].astype(F32))).astype(BF16)
        for h in range(HEADS):
            sl = slice(128 * h, 128 * (h + 1))
            og = og_ref[:, sl]
            on = og * _rms(og, HEAD_DIM) * g_ref[...]
            o_ref[:, 1024 + 128 * h:1024 + 128 * (h + 1)] = (on * _silu(gg_ref[:, sl].astype(F32))).astype(BF16)

    return _call(body, name="mix_fwd", grid=(s // tm,),
                 in_specs=[_rows(tm, 1024), _rows(tm, 1024), _rows(tm, 1024, 1), _rows(tm, 1024, 5), _full((1, 128))],
                 out_specs=_rows(tm, 2048), out_shape=_sds((s, 2048), BF16), args=(o_mla, o_gdn, proj, proj, out_gain))


def _mix_bwd(o_mla, o_gdn, proj, out_gain, dmixed):
    s = proj.shape[0]
    tm = min(TILES["row"], s)

    def body(om_ref, og_ref, mg_ref, gg_ref, g_ref, dm_ref, dg_ref, dom_ref, dog_ref, dmg_ref, dgg_ref, ag_ref,
             delta_ref):
        @pl.when(pl.program_id(0) == 0)
        def _():
            ag_ref[...] = jnp.zeros_like(ag_ref)

        mg = mg_ref[...].astype(F32)
        dm = dm_ref[...].astype(F32)
        om = om_ref[...]
        dom = (dm * _silu(mg)).astype(BF16)
        dom_ref[...] = dom
        dmg_ref[...] = (dm * om * _dsilu(mg)).astype(BF16)
        prod = dom.astype(F32) * om
        lane = lax.broadcasted_iota(jnp.int32, (tm, 128), 1)
        delta = jnp.zeros((tm, 128), F32)
        for h in range(HEADS):
            delta = delta + jnp.where(lane == h, jnp.sum(prod[:, 128 * h:128 * (h + 1)], axis=-1, keepdims=True), 0.0)
        delta_ref[...] = delta
        for h in range(HEADS):
            sl = slice(128 * h, 128 * (h + 1))
            og, gg, d = og_ref[:, sl], gg_ref[:, sl].astype(F32), dg_ref[:, sl].astype(F32)
            on = og * _rms(og, HEAD_DIM) * g_ref[...]
            dgg_ref[:, sl] = (d * on * _dsilu(gg)).astype(BF16)
            dx, gpart = _rms_bwd(og, g_ref[...], d * _silu(gg), HEAD_DIM)
            dog_ref[:, sl] = dx.astype(BF16)
            ag_ref[...] += _acc8(gpart)

    return _call(body, name="mix_bwd", grid=(s // tm,),
                 in_specs=[_rows(tm, 1024), _rows(tm, 1024), _rows(tm, 1024, 1), _rows(tm, 1024, 5), _full((1, 128)),
                           _rows(tm, 1024, 0), _rows(tm, 1024, 1)],
                 out_specs=[_rows(tm, 1024), _rows(tm, 1024), _rows(tm, 1024), _rows(tm, 1024), _full((8, 128)),
                            _rows(tm, 128)],
                 out_shape=[_sds((s, 1024), BF16), _sds((s, 1024), BF16), _sds((s, 1024), BF16), _sds((s, 1024), BF16),
                            _sds((8, 128), F32), _sds((s, 128), F32)],
                 args=(o_mla, o_gdn, proj, proj, out_gain, dmixed, dmixed))


def _out_fwd(mixed, w_out, x, target):
    s = x.shape[0]
    tm = min(TILES["mm"], s)
    tn = min(TILES["mm"], D_MODEL)

    def body(m_ref, w_ref, x_ref, t_ref, dy_ref, acc_ref):
        err = x_ref[...] + _dot(m_ref[...], w_ref[...]) - t_ref[...]
        dy_ref[...] = (err * (1.0 / D_MODEL)).astype(BF16)

        @pl.when(pl.program_id(1) == 0)
        def _():
            acc_ref[...] = jnp.zeros_like(acc_ref)

        acc_ref[...] += _acc8(err * err)

    return _call(body, name="out_fwd", grid=(D_MODEL // tn, s // tm),
                 in_specs=[pl.BlockSpec((tm, D_MODEL), lambda j, i: (i, 0)), pl.BlockSpec((D_MODEL, tn), lambda j, i: (0, j)),
                           pl.BlockSpec((tm, tn), lambda j, i: (i, j)), pl.BlockSpec((tm, tn), lambda j, i: (i, j))],
                 out_specs=[pl.BlockSpec((tm, tn), lambda j, i: (i, j)), pl.BlockSpec((8, tn), lambda j, i: (0, j))],
                 out_shape=[_sds((s, D_MODEL), BF16), _sds((8, D_MODEL), F32)], args=(mixed, w_out, x, target))


def _row_tile(r, c):
    if r % 8 != 0:
        return r
    t = 8
    while r % (2 * t) == 0 and 2 * t * c * 4 <= (1 << 20):
        t *= 2
    return t


def _sum_arrays(parts, name, also_bf16=False):
    r, c = parts[0].shape
    tr = _row_tile(r, c)
    n = len(parts)

    def body(*refs):
        acc = refs[0][...].astype(F32)
        for p_ref in refs[1:n]:
            acc = acc + p_ref[...].astype(F32)
        refs[n][...] = acc
        if also_bf16:
            refs[n + 1][...] = acc.astype(BF16)

    nout = 2 if also_bf16 else 1
    out = _call(body, name=name, grid=(r // tr,), in_specs=[_rows(tr, c)] * n, out_specs=[_rows(tr, c)] * nout,
                out_shape=[_sds((r, c), F32), _sds((r, c), BF16)][:nout], args=tuple(parts))
    return out if also_bf16 else out[0]


def _adamw(w, g, m, v, name):
    r, c = w.shape
    c1 = 1.0 - ADAM_B1 ** ADAM_STEP
    c2 = 1.0 - ADAM_B2 ** ADAM_STEP

    def body(w_ref, g_ref, m_ref, v_ref, d_ref, nm_ref, nv_ref):
        gv = g_ref[...]
        nm = ADAM_B1 * m_ref[...] + (1.0 - ADAM_B1) * gv
        nv = ADAM_B2 * v_ref[...] + (1.0 - ADAM_B2) * (gv * gv)
        nm_ref[...] = nm
        nv_ref[...] = nv
        d_ref[...] = -ADAM_LR * ((nm / c1) / (jnp.sqrt(nv / c2) + ADAM_EPS) + ADAM_WD * w_ref[...])

    if r % 8 == 0:
        tr = _row_tile(r, c)
        grid, spec = (r // tr,), _rows(tr, c)
    else:
        tc = c
        while tc % 256 == 0 and r * tc * 4 > (3 << 19):
            tc //= 2
        grid, spec = (c // tc,), pl.BlockSpec((r, tc), lambda i: (0, i))
    return _call(body, name=name, grid=grid, in_specs=[spec] * 4, out_specs=[spec] * 3,
                 out_shape=[_sds((r, c), F32)] * 3, args=(w, g, m, v))


ANY = pl.BlockSpec(memory_space=pl.ANY)
CHIP_FLIPS = ((1, 0), (0, 1), (1, 1))


def _comm_call(body, *, name, n_in, out_shape, scratch):
    def kfn(*refs):
        body(*refs)
    return pl.pallas_call(kfn, name=name, in_specs=[ANY] * n_in, out_specs=[ANY] * len(out_shape), out_shape=out_shape,
                          scratch_shapes=list(scratch),
                          compiler_params=pltpu.CompilerParams(has_side_effects=True))


def _all_gather_chips(shards):
    na = len(shards)

    def body(*refs):
        copies = _gather_copies(refs[:na], refs[na:2 * na], *refs[2 * na:])
        _gather_start(copies)
        _gather_finish(copies)

    out_shape = [_sds((4,) + a.shape, a.dtype) for a in shards]
    sem = pltpu.SemaphoreType.DMA((na, 3))
    got = _comm_call(body, name="all_gather_weights", n_in=na, out_shape=out_shape, scratch=[sem, sem, sem, sem])(*shards)
    return _place_own_blocks(got, shards)


def _gather_copies(ins, outs, send_sems, recv_sems, fwd_send, fwd_recv):
    x, y, c = lax.axis_index("x"), lax.axis_index("y"), lax.axis_index("c")
    my_k = 2 * x + y
    direct, forwards = [], []
    for a in range(len(ins)):
        rows = ins[a].shape[0]
        for r, (fx, fy) in enumerate(CHIP_FLIPS):
            px, py = x ^ fx, y ^ fy
            if rows % 32 == 0:
                mine = pl.ds(pl.multiple_of(c * (rows // 2), 16), rows // 2)
                other = pl.ds(pl.multiple_of((1 - c) * (rows // 2), 16), rows // 2)
                rc = pltpu.make_async_remote_copy(
                    src_ref=ins[a].at[mine], dst_ref=outs[a].at[my_k, mine], send_sem=send_sems.at[a, r],
                    recv_sem=recv_sems.at[a, r], device_id=(px, py, c), device_id_type=MESH)
                landed = outs[a].at[2 * px + py, mine]
                fw = pltpu.make_async_remote_copy(
                    src_ref=landed, dst_ref=landed, send_sem=fwd_send.at[a, r], recv_sem=fwd_recv.at[a, r],
                    device_id=(x, y, 1 - c), device_id_type=MESH)
                from_sib = outs[a].at[2 * px + py, other]
                fw_in = pltpu.make_async_remote_copy(
                    src_ref=from_sib, dst_ref=from_sib, send_sem=fwd_send.at[a, r], recv_sem=fwd_recv.at[a, r],
                    device_id=(x, y, 1 - c), device_id_type=MESH)
                forwards.append((rc, fw, fw_in))
            else:
                direct.append(pltpu.make_async_remote_copy(
                    src_ref=ins[a], dst_ref=outs[a].at[my_k], send_sem=send_sems.at[a, r],
                    recv_sem=recv_sems.at[a, r], device_id=(px, py, c), device_id_type=MESH))
    return forwards, direct


def _gather_start(copies):
    forwards, direct = copies
    for rc, _, _ in forwards:
        rc.start()
    for rc in direct:
        rc.start()


def _gather_finish(copies):
    forwards, direct = copies
    for rc, fw, _ in forwards:
        rc.wait_recv()
        fw.start()
    for rc, fw, fw_in in forwards:
        rc.wait_send()
        fw.wait_send()
        fw_in.wait_recv()
    for rc in direct:
        rc.wait()


def _place_own_blocks(got, shards):
    my_k = 2 * lax.axis_index("x") + lax.axis_index("y")
    return [lax.dynamic_update_index_in_dim(g, a, my_k, 0) for g, a in zip(got, shards)]


def _norm1_fwd_gather(x, gain, shards):
    s = x.shape[0]
    tm = min(TILES["row"], s)
    ni = s // tm
    na = len(shards)

    def kfn(x_ref, g_ref, *rest):
        o_ref = rest[na]
        sems = rest[2 * na + 1:]
        i = pl.program_id(0)

        @pl.when(i == 0)
        def _():
            _gather_start(_gather_copies(rest[:na], rest[na + 1:2 * na + 1], *sems))

        xv = x_ref[...]
        r = lax.rsqrt(jnp.mean(xv * xv, axis=-1, keepdims=True) + NORM_EPS)
        o_ref[...] = (xv * r * g_ref[...]).astype(BF16)

        @pl.when(i == ni - 1)
        def _():
            _gather_finish(_gather_copies(rest[:na], rest[na + 1:2 * na + 1], *sems))

    sem = pltpu.SemaphoreType.DMA((na, 3))
    out = pl.pallas_call(
        kfn, name="norm1_fwd_gather", grid=(ni,),
        in_specs=[_rows(tm, D_MODEL), _full((1, D_MODEL))] + [ANY] * na,
        out_specs=[_rows(tm, D_MODEL)] + [ANY] * na,
        out_shape=[_sds((s, D_MODEL), BF16)] + [_sds((4,) + s_a.shape, s_a.dtype) for s_a in shards],
        scratch_shapes=[sem, sem, sem, sem],
        compiler_params=pltpu.CompilerParams(dimension_semantics=("arbitrary",), vmem_limit_bytes=VMEM_LIMIT_V7X,
                                             has_side_effects=True))(x, gain, *shards)
    return out[0], _place_own_blocks(list(out[1:]), shards)


def _matmul_nn_gather(a, b, shards, *, name, tm, tn, out_dtype):
    m, kdim = a.shape
    n = b.shape[1]
    ni, nj = m // tm, n // tn
    na = len(shards)

    def body(a_ref, b_ref, *rest):
        o_ref = rest[na]
        sems = rest[2 * na + 1:]
        i, j = pl.program_id(0), pl.program_id(1)

        @pl.when((i == 0) & (j == 0))
        def _():
            _gather_start(_gather_copies(rest[:na], rest[na + 1:2 * na + 1], *sems))

        o_ref[...] = _dot(a_ref[...], b_ref[...]).astype(out_dtype)

        @pl.when((i == ni - 1) & (j == nj - 1))
        def _():
            _gather_finish(_gather_copies(rest[:na], rest[na + 1:2 * na + 1], *sems))

    def kfn(*refs):
        body(*refs)
    sem = pltpu.SemaphoreType.DMA((na, 3))
    out = pl.pallas_call(
        kfn, name=name, grid=(ni, nj),
        in_specs=[pl.BlockSpec((tm, kdim), lambda i, j: (i, 0)), pl.BlockSpec((kdim, tn), lambda i, j: (0, j))] + [ANY] * na,
        out_specs=[pl.BlockSpec((tm, tn), lambda i, j: (i, j))] + [ANY] * na,
        out_shape=[_sds((m, n), out_dtype)] + [_sds((4,) + s_a.shape, s_a.dtype) for s_a in shards],
        scratch_shapes=[sem, sem, sem, sem],
        compiler_params=pltpu.CompilerParams(dimension_semantics=("arbitrary", "arbitrary"),
                                             vmem_limit_bytes=VMEM_LIMIT_V7X, has_side_effects=True))(a, b, *shards)
    return out[0], _place_own_blocks(list(out[1:]), shards)


def _all_reduce_small(vec):
    r = vec.shape[0]

    def body(v_ref, o_ref, gath, send_sems, recv_sems):
        x, y, c = lax.axis_index("x"), lax.axis_index("y"), lax.axis_index("c")
        me = 4 * x + 2 * y + c
        gath[me] = v_ref[...]
        copies = []
        for rel in range(1, 8):
            fx, fy, fc = (rel >> 2) & 1, (rel >> 1) & 1, rel & 1
            rc = pltpu.make_async_remote_copy(
                src_ref=v_ref, dst_ref=gath.at[me], send_sem=send_sems.at[rel - 1], recv_sem=recv_sems.at[rel - 1],
                device_id=(x ^ fx, y ^ fy, c ^ fc), device_id_type=MESH)
            rc.start()
            copies.append(rc)
        for rc in copies:
            rc.wait()
        acc = gath[0]
        for d in range(1, 8):
            acc = acc + gath[d]
        o_ref[...] = acc

    def kfn(*refs):
        body(*refs)
    vm = pl.BlockSpec(memory_space=pltpu.VMEM)
    return pl.pallas_call(kfn, name="all_reduce_small", in_specs=[vm], out_specs=vm, out_shape=_sds((r, 128), F32),
                          scratch_shapes=[pltpu.VMEM((8, r, 128), F32), pltpu.SemaphoreType.DMA((7,)),
                                          pltpu.SemaphoreType.DMA((7,))],
                          compiler_params=pltpu.CompilerParams(has_side_effects=True))(vec)


def _exchange_halves(arrs):
    na = len(arrs)

    def body(*refs):
        ins, outs = refs[:na], refs[na:2 * na]
        send_sems, recv_sems = refs[2 * na:]
        x, y, c = lax.axis_index("x"), lax.axis_index("y"), lax.axis_index("c")
        copies = []
        for a in range(na):
            half = ins[a].shape[1] // 2
            src = ins[a].at[:, pl.ds(pl.multiple_of((1 - c) * half, 8), half), :]
            rc = pltpu.make_async_remote_copy(src_ref=src, dst_ref=outs[a], send_sem=send_sems.at[a],
                                              recv_sem=recv_sems.at[a], device_id=(x, y, 1 - c), device_id_type=MESH)
            rc.start()
            copies.append(rc)
        for rc in copies:
            rc.wait()

    out_shape = [_sds((4, a.shape[1] // 2, a.shape[2]), F32) for a in arrs]
    return _comm_call(body, name="rs_pair_exchange", n_in=na, out_shape=out_shape,
                      scratch=[pltpu.SemaphoreType.DMA((na,)), pltpu.SemaphoreType.DMA((na,))])(*arrs)


def _scatter_to_chips(arrs):
    na = len(arrs)

    def body(*refs):
        ins, outs = refs[:na], refs[na:2 * na]
        send_sems, recv_sems = refs[2 * na:]
        x, y, c = lax.axis_index("x"), lax.axis_index("y"), lax.axis_index("c")
        copies = []
        for a in range(na):
            for r, (fx, fy) in enumerate(CHIP_FLIPS):
                px, py = x ^ fx, y ^ fy
                rc = pltpu.make_async_remote_copy(
                    src_ref=ins[a].at[2 * px + py], dst_ref=outs[a].at[r], send_sem=send_sems.at[a, r],
                    recv_sem=recv_sems.at[a, r], device_id=(px, py, c), device_id_type=MESH)
                rc.start()
                copies.append(rc)
        for rc in copies:
            rc.wait()

    out_shape = [_sds((3,) + a.shape[1:], a.dtype) for a in arrs]
    return _comm_call(body, name="rs_chip_scatter", n_in=na, out_shape=out_shape,
                      scratch=[pltpu.SemaphoreType.DMA((na, 3)), pltpu.SemaphoreType.DMA((na, 3))])(*arrs)


def _sum_into_half(parts, name):
    r2, c = parts[0].shape
    tr = _row_tile(r2, c)
    nb = r2 // tr
    n = len(parts)

    def kfn(c_ref, *refs):
        acc = refs[0][...].astype(F32)
        for p_ref in refs[1:n]:
            acc = acc + p_ref[...].astype(F32)
        refs[n][...] = acc

    spec = pltpu.PrefetchScalarGridSpec(
        num_scalar_prefetch=1, grid=(nb,), in_specs=[pl.BlockSpec((tr, c), lambda i, cr: (i, 0))] * n,
        out_specs=pl.BlockSpec((tr, c), lambda i, cr: (cr[0] * nb + i, 0)))
    core = lax.axis_index("c").astype(jnp.int32).reshape(1)
    return pl.pallas_call(kfn, name=name, grid_spec=spec, out_shape=_sds((2 * r2, c), F32),
                          compiler_params=pltpu.CompilerParams(dimension_semantics=("arbitrary",),
                                                               vmem_limit_bytes=VMEM_LIMIT_V7X))(core, *parts)


def _join_in_place(arrs):
    na = len(arrs)

    def body(*refs):
        outs = refs[na:2 * na]
        send_sems, recv_sems = refs[2 * na:]
        x, y, c = lax.axis_index("x"), lax.axis_index("y"), lax.axis_index("c")
        copies = []
        for a in range(na):
            half = outs[a].shape[0] // 2
            mine = outs[a].at[pl.ds(pl.multiple_of(c * half, 8), half), :]
            rc = pltpu.make_async_remote_copy(src_ref=mine, dst_ref=mine, send_sem=send_sems.at[a],
                                              recv_sem=recv_sems.at[a], device_id=(x, y, 1 - c), device_id_type=MESH)
            rc.start()
            copies.append(rc)
        for rc in copies:
            rc.wait()

    def kfn(*refs):
        body(*refs)
    return pl.pallas_call(kfn, name="rs_pair_join", in_specs=[ANY] * na, out_specs=[ANY] * na,
                          out_shape=[_sds(a.shape, F32) for a in arrs],
                          input_output_aliases={a: a for a in range(na)},
                          scratch_shapes=[pltpu.SemaphoreType.DMA((na,)), pltpu.SemaphoreType.DMA((na,))],
                          compiler_params=pltpu.CompilerParams(has_side_effects=True))(*arrs)


def _pair_sum(g, o, name):
    _, r, c = g.shape
    half = r // 2
    tr = _row_tile(half, c)
    nb = half // tr

    def kfn(c_ref, g_ref, o_ref, s32_ref, s16_ref):
        acc = g_ref[...] + o_ref[...]
        s32_ref[...] = acc
        s16_ref[...] = acc.astype(BF16)

    blk = lambda imap: pl.BlockSpec((None, tr, c), imap)
    same = lambda k, i, cr: (k, i, 0)
    spec = pltpu.PrefetchScalarGridSpec(
        num_scalar_prefetch=1, grid=(4, nb), in_specs=[blk(lambda k, i, cr: (k, cr[0] * nb + i, 0)), blk(same)],
        out_specs=[blk(same), blk(same)])
    core = lax.axis_index("c").astype(jnp.int32).reshape(1)
    return pl.pallas_call(kfn, name=name, grid_spec=spec, out_shape=[_sds((4, half, c), F32), _sds((4, half, c), BF16)],
                          compiler_params=pltpu.CompilerParams(dimension_semantics=("arbitrary", "arbitrary"),
                                                               vmem_limit_bytes=VMEM_LIMIT_V7X))(core, g, o)


def _rs_pair_stage(grads):
    got = _exchange_halves(grads)
    sums = [_pair_sum(g, o, f"rs_pair_sum_{a}") for a, (g, o) in enumerate(zip(grads, got))]
    return [s32 for s32, _ in sums], [s16 for _, s16 in sums]


def _rs_chip_stage(pair, recv):
    k_me = 2 * lax.axis_index("x") + lax.axis_index("y")
    halves = []
    for a, (p, rv) in enumerate(zip(pair, recv)):
        own = lax.dynamic_index_in_dim(p, k_me, 0, keepdims=False)
        halves.append(_sum_into_half([own, rv[0], rv[1], rv[2]], f"rs_chip_sum_{a}"))
    return _join_in_place(halves)


def _reduce_scatter(grads):
    pair, pair_bf16 = _rs_pair_stage(grads)
    return _rs_chip_stage(pair, _scatter_to_chips(pair_bf16))


def _matmul_nt_scatter(a, b, send, *, name, tm, tn, out_dtype):
    m, kdim = a.shape
    n = b.shape[0]
    ni, nj = m // tm, n // tn
    na = len(send)

    def body(a_ref, b_ref, *rest):
        send_refs, o_ref, recv_refs = rest[:na], rest[na], rest[na + 1:2 * na + 1]
        send_sems, recv_sems = rest[2 * na + 1:]
        i, j = pl.program_id(0), pl.program_id(1)

        def copies():
            x, y, c = lax.axis_index("x"), lax.axis_index("y"), lax.axis_index("c")
            out = []
            for s_i in range(na):
                for r, (fx, fy) in enumerate(CHIP_FLIPS):
                    px, py = x ^ fx, y ^ fy
                    out.append(pltpu.make_async_remote_copy(
                        src_ref=send_refs[s_i].at[2 * px + py], dst_ref=recv_refs[s_i].at[r],
                        send_sem=send_sems.at[s_i, r], recv_sem=recv_sems.at[s_i, r], device_id=(px, py, c),
                        device_id_type=MESH))
            return out

        @pl.when((i == 0) & (j == 0))
        def _():
            for cp in copies():
                cp.start()

        o_ref[...] = _dot(a_ref[...], b_ref[...], NT).astype(out_dtype)

        @pl.when((i == ni - 1) & (j == nj - 1))
        def _():
            for cp in copies():
                cp.wait()

    def kfn(*refs):
        body(*refs)
    sem = pltpu.SemaphoreType.DMA((na, 3))
    out = pl.pallas_call(
        kfn, name=name, grid=(ni, nj),
        in_specs=[pl.BlockSpec((tm, kdim), lambda i, j: (i, 0)), pl.BlockSpec((tn, kdim), lambda i, j: (j, 0))] + [ANY] * na,
        out_specs=[pl.BlockSpec((tm, tn), lambda i, j: (i, j))] + [ANY] * na,
        out_shape=[_sds((m, n), out_dtype)] + [_sds((3,) + s_a.shape[1:], s_a.dtype) for s_a in send],
        scratch_shapes=[sem, sem],
        compiler_params=pltpu.CompilerParams(dimension_semantics=("arbitrary", "arbitrary"),
                                             vmem_limit_bytes=VMEM_LIMIT_V7X, has_side_effects=True))(a, b, *send)
    return out[0], list(out[1:])


def _pad_w_in(w):
    z = jnp.zeros((w.shape[0], 1024 - 848), w.dtype)
    return jnp.concatenate([w[:, 0:832], w[:, 4928:4944], z, w[:, 832:4928], w[:, 4944:5968]], axis=1)


def _unpad_w_in(g):
    return jnp.concatenate([g[:, 0:832], g[:, 1024:5120], g[:, 832:848], g[:, 5120:6144]], axis=1)


W_IN_SHARD = W_IN_COLS // 4
W_IN_RUNS = ((0, 832, 0), (832, 4928, 1024), (4928, 4944, 832), (4944, 5968, 5120))


def _w_in_grad_blocks(p):
    def orig_cols(lo, hi):
        parts = [p[:, pa + max(lo, a) - a:pa + min(hi, b) - a] for a, b, pa in W_IN_RUNS if max(lo, a) < min(hi, b)]
        return parts[0] if len(parts) == 1 else jnp.concatenate(parts, axis=1)
    return jnp.stack([orig_cols(W_IN_SHARD * k, W_IN_SHARD * (k + 1)) for k in range(4)])


def _pad_w_in_blocks(g):
    def orig_cols(lo, hi):
        return [g[k][:, max(lo, W_IN_SHARD * k) - W_IN_SHARD * k:min(hi, W_IN_SHARD * (k + 1)) - W_IN_SHARD * k]
                for k in range(4) if max(lo, W_IN_SHARD * k) < min(hi, W_IN_SHARD * (k + 1))]
    z = jnp.zeros((g.shape[1], 1024 - 848), g.dtype)
    return jnp.concatenate(orig_cols(0, 832) + orig_cols(4928, 4944) + [z] + orig_cols(832, 4928) + orig_cols(4944, 5968),
                           axis=1)


def _pad_heads(w):
    r = w.shape[0]
    return jnp.pad(w.reshape(r, HEADS, QK_DIM), ((0, 0), (0, 0), (0, QK_PAD - QK_DIM))).reshape(r, HEADS * QK_PAD)


def _unpad_heads(w):
    r = w.shape[0]
    return w.reshape(r, HEADS, QK_PAD)[:, :, :QK_DIM].reshape(r, HEADS * QK_DIM)


def _cols_to_blocks(w):
    r = w.shape[0]
    return w.reshape(r, 4, -1).transpose(1, 0, 2)


def _blocks_to_cols(w):
    return w.transpose(1, 0, 2).reshape(w.shape[1], -1)


SMALL_ROWS = {"norm_gain": (0, 2048), "mla_q_a_gain": (16, 512), "mla_kv_a_gain": (20, 256),
              "mla_q_norm_gain": (22, 192), "mla_k_norm_gain": (24, 192), "gdn_a_log": (26, 8),
              "gdn_dt_bias": (27, 8), "gdn_out_norm_gain": (28, 128)}
LOSS_ROW = 29
SMALL_PACK_ROWS = 32
CONV_ROW = 32


def _pack_small(vals, loss=None):
    rows = []
    at = 0
    for name, (row, size) in SMALL_ROWS.items():
        assert row == at
        nr = -(-size // 128)
        rows.append(jnp.pad(vals[name].reshape(-1).astype(F32), (0, nr * 128 - size)).reshape(nr, 128))
        at += nr
    assert at == LOSS_ROW
    if loss is not None:
        rows.append(jnp.pad(loss.reshape(1, 1), ((0, 0), (0, 127))))
        at += 1
    rows.append(jnp.zeros((SMALL_PACK_ROWS - at, 128), F32))
    return jnp.concatenate(rows, axis=0)


def _unpack_small(pack, name):
    row, size = SMALL_ROWS[name]
    nr = -(-size // 128)
    return pack[row:row + nr].reshape(-1)[:size].reshape(1, size)


def _local_step(x, positions, target, norm_gain, w_in_p, q_a_gain, kv_a_gain, w_uq_p, w_ukv, q_norm_gain,
                k_norm_gain, conv_w, a_log, dt_bias, out_gain, w_out, scatter_hook=None, late_weights=None,
                first_weights=None):
    half = HALF_ROPE
    inv_freq = jnp.power(10000.0, -jnp.arange(half, dtype=F32) / half)
    ang = positions.astype(F32)[:, None] * inv_freq
    cos, sin = jnp.cos(ang), jnp.sin(ang)
    zpad = jnp.zeros((x.shape[0], 64), F32)
    cs = jnp.concatenate([cos, cos, zpad], axis=1)
    sn = jnp.concatenate([-sin, sin, zpad], axis=1)
    gq = jnp.pad(q_norm_gain.reshape(1, QK_DIM), ((0, 0), (0, QK_PAD - QK_DIM)))
    gk = jnp.pad(k_norm_gain.reshape(1, QK_DIM), ((0, 0), (0, QK_PAD - QK_DIM)))
    lane_pad = ((0, 0), (GA_LANE, 128 - GA_LANE - HEADS))
    alog128 = jnp.pad(a_log.reshape(1, HEADS), lane_pad)
    dtb128 = jnp.pad(dt_bias.reshape(1, HEADS), lane_pad)
    ng, qag, kvag, og = (norm_gain.reshape(1, -1), q_a_gain.reshape(1, -1), kv_a_gain.reshape(1, -1),
                         out_gain.reshape(1, -1))

    if first_weights is None:
        xn = _norm1_fwd(x, ng)
    else:
        shards, assemble = first_weights
        xn, gathered = _norm1_fwd_gather(x, ng, shards)
        w_in_p, conv_w = assemble(gathered)
    misc = _matmul(xn, w_in_p[:, 768:896], mode="nn", out_dtype=F32, name="in_proj_misc")
    if late_weights is None:
        proj = _matmul(xn, w_in_p, mode="nn", out_dtype=BF16, name="in_proj")
    else:
        shards, assemble = late_weights
        proj, gathered = _matmul_nn_gather(xn, w_in_p, shards, name="in_proj_gather", tm=TILES["mm"], tn=2 * TILES["mm"],
                                           out_dtype=BF16)
        w_uq_p, w_ukv, w_out = assemble(gathered)
    cqn, ckvn = _mla_a_norm(proj, qag, kvag)
    q_pre = _matmul(cqn, w_uq_p, mode="nn", out_dtype=BF16, name="q_up")
    kv_pre = _matmul(ckvn, w_ukv, mode="nn", out_dtype=BF16, name="kv_up")
    q, k, v = _mla_post_fwd(q_pre, kv_pre, misc, cs, sn, gq, gk)
    o_mla, lse = _attn_fwd(q, k, v)
    qkv = _gdn_conv_fwd(proj, conv_w)
    gbeta = _gdn_gates_fwd(misc, alog128, dtb128)
    g_u, g_w, g_qd, g_kd, g_a, g_t, g_gl = _gdn_pre(qkv, gbeta)
    o_gdn, states = _gdn_scan_fwd(g_u, g_w, g_qd, g_kd, g_a, g_gl)
    mixed = _mix_fwd(o_mla, o_gdn, proj, og)
    dy, sq = _out_fwd(mixed, w_out, x, target)

    dmixed = _matmul(dy, w_out, mode="nt", out_dtype=BF16, name="d_mixed")
    d_w_out = _matmul(mixed, dy, mode="tn", out_dtype=F32, name="d_w_out", tk=4096)
    do_mla, do_gdn, dmg, dgg, d_out_gain, delta128 = _mix_bwd(o_mla, o_gdn, proj, og, dmixed)
    s_len = x.shape[0]
    delta_rows = delta128[:, :HEADS].T.reshape(HEADS, 1, s_len)
    dq, dk, dv = _attn_bwd(q, k, v, lse.reshape(HEADS, 1, s_len), delta_rows, do_mla)
    dq_pre, dkv_pre, dkr, d_gq, d_gk = _mla_post_bwd(q_pre, kv_pre, misc, cs, sn, gq, gk, dq, dk, dv)
    d_w_uq_p = _matmul(cqn, dq_pre, mode="tn", out_dtype=F32, name="d_w_uq", tk=1024)
    d_w_ukv = _matmul(ckvn, dkv_pre, mode="tn", out_dtype=F32, name="d_w_ukv", tk=1024)
    dcqn = _matmul(dq_pre, w_uq_p, mode="nt", out_dtype=F32, name="d_cqn")
    dckvn = _matmul(dkv_pre, w_ukv, mode="nt", out_dtype=F32, name="d_ckvn")
    dcq, dckv, d_qag, d_kvag = _mla_a_norm_bwd(proj, qag, kvag, dcqn, dckvn)
    dstates = _gdn_scan_bwd(g_w, g_qd, g_kd, g_a, g_gl, do_gdn)
    dqkv, dgbeta = _gdn_post_bwd(qkv, gbeta, g_u, g_w, g_t, states, dstates, do_gdn)
    dz = _gdn_conv_bwd_a(proj, conv_w, dqkv)
    dgx, d_conv = _gdn_conv_bwd_b(proj, conv_w, dz)
    dmisc, d_alog, d_dtb = _gdn_gates_bwd(misc, alog128, dtb128, gbeta, dgbeta, dkr)
    dproj = jnp.concatenate([dcq, dckv, dmisc, jnp.zeros((x.shape[0], 128), BF16), dmg, dgx, dgg], axis=1)
    d_w_in_p = _matmul(xn, dproj, mode="tn", out_dtype=F32, name="d_w_in", tk=4096)
    big = {"w_in": d_w_in_p, "w_uq": d_w_uq_p, "w_ukv": d_w_ukv, "w_out": d_w_out, "gdn_conv_w": d_conv}
    if scatter_hook is None:
        dxn, received = _matmul(dproj, w_in_p, mode="nt", out_dtype=BF16, name="d_xn", tm=512, tn=512), None
    else:
        dxn, received = _matmul_nt_scatter(dproj, w_in_p, scatter_hook(big), name="d_xn_scatter", tm=1024, tn=512,
                                           out_dtype=BF16)
    grad_x, d_ng = _norm1_bwd(x, ng, dxn, dy)

    small = {"norm_gain": d_ng.sum(0), "mla_q_a_gain": d_qag.sum(0), "mla_kv_a_gain": d_kvag.sum(0),
             "mla_q_norm_gain": d_gq.sum(0)[:QK_DIM], "mla_k_norm_gain": d_gk.sum(0)[:QK_DIM],
             "gdn_a_log": d_alog.sum(0)[GA_LANE:GA_LANE + HEADS], "gdn_dt_bias": d_dtb.sum(0)[GA_LANE:GA_LANE + HEADS],
             "gdn_out_norm_gain": d_out_gain.sum(0)}
    return sq, grad_x, small, big, received


WEIGHTS = ["norm_gain", "w_in", "mla_q_a_gain", "mla_kv_a_gain", "w_uq", "w_ukv", "mla_q_norm_gain", "mla_k_norm_gain",
           "gdn_conv_w", "gdn_a_log", "gdn_dt_bias", "gdn_out_norm_gain", "w_out"]
BIG = ["w_in", "w_uq", "w_ukv", "w_out"]


def kernel(x, positions, norm_gain, w_in, mla_q_a_gain, mla_kv_a_gain, w_uq, w_ukv, mla_q_norm_gain, mla_k_norm_gain, gdn_conv_w, gdn_a_log, gdn_dt_bias, gdn_out_norm_gain, w_out, loss_target, m_norm_gain, m_w_in, m_mla_q_a_gain, m_mla_kv_a_gain, m_w_uq, m_w_ukv, m_mla_q_norm_gain, m_mla_k_norm_gain, m_gdn_conv_w, m_gdn_a_log, m_gdn_dt_bias, m_gdn_out_norm_gain, m_w_out, v_norm_gain, v_w_in, v_mla_q_a_gain, v_mla_kv_a_gain, v_w_uq, v_w_ukv, v_mla_q_norm_gain, v_mla_k_norm_gain, v_gdn_conv_w, v_gdn_a_log, v_gdn_dt_bias, v_gdn_out_norm_gain, v_w_out):
    w = dict(norm_gain=norm_gain, w_in=w_in, mla_q_a_gain=mla_q_a_gain, mla_kv_a_gain=mla_kv_a_gain, w_uq=w_uq,
             w_ukv=w_ukv, mla_q_norm_gain=mla_q_norm_gain, mla_k_norm_gain=mla_k_norm_gain, gdn_conv_w=gdn_conv_w,
             gdn_a_log=gdn_a_log, gdn_dt_bias=gdn_dt_bias, gdn_out_norm_gain=gdn_out_norm_gain, w_out=w_out)
    m = dict(norm_gain=m_norm_gain, w_in=m_w_in, mla_q_a_gain=m_mla_q_a_gain, mla_kv_a_gain=m_mla_kv_a_gain,
             w_uq=m_w_uq, w_ukv=m_w_ukv, mla_q_norm_gain=m_mla_q_norm_gain, mla_k_norm_gain=m_mla_k_norm_gain,
             gdn_conv_w=m_gdn_conv_w, gdn_a_log=m_gdn_a_log, gdn_dt_bias=m_gdn_dt_bias,
             gdn_out_norm_gain=m_gdn_out_norm_gain, w_out=m_w_out)
    v = dict(norm_gain=v_norm_gain, w_in=v_w_in, mla_q_a_gain=v_mla_q_a_gain, mla_kv_a_gain=v_mla_kv_a_gain,
             w_uq=v_w_uq, w_ukv=v_w_ukv, mla_q_norm_gain=v_mla_q_norm_gain, mla_k_norm_gain=v_mla_k_norm_gain,
             gdn_conv_w=v_gdn_conv_w, gdn_a_log=v_gdn_a_log, gdn_dt_bias=v_gdn_dt_bias,
             gdn_out_norm_gain=v_gdn_out_norm_gain, w_out=v_w_out)
    k_me = 2 * lax.axis_index("x") + lax.axis_index("y")

    first_weights = ([w_in[0].astype(BF16), gdn_conv_w[0]], lambda g: (_pad_w_in_blocks(g[0]), _blocks_to_cols(g[1])))
    late_weights = ([w_uq[0].astype(BF16), w_ukv[0].astype(BF16), w_out[0].astype(BF16)],
                    lambda g: (_pad_heads(_blocks_to_cols(g[0])), _blocks_to_cols(g[1]), g[2].reshape(D_MODEL, D_MODEL)))

    pair_sums = []

    def scatter_hook(big):
        pair, pair_bf16 = _rs_pair_stage([
            _w_in_grad_blocks(big["w_in"]), _cols_to_blocks(_unpad_heads(big["w_uq"])),
            _cols_to_blocks(big["w_ukv"]), big["w_out"].reshape(4, 512, D_MODEL)])
        pair_sums.extend(pair)
        return pair_bf16

    sq, grad_x, small, big, received = _local_step(
        x[0], positions[0], loss_target[0], norm_gain, None, mla_q_a_gain, mla_kv_a_gain, None, None,
        mla_q_norm_gain, mla_k_norm_gain, None, gdn_a_log, gdn_dt_bias, gdn_out_norm_gain, None, scatter_hook,
        late_weights, first_weights)

    loss_local = (0.5 / D_MODEL) * jnp.sum(sq)
    pack = jnp.concatenate([_pack_small(small, loss_local), big["gdn_conv_w"].reshape(96, 128)], axis=0)
    tot = _all_reduce_small(pack)
    loss = tot[LOSS_ROW, 0]
    conv_grad = lax.dynamic_slice_in_dim(tot[CONV_ROW:].reshape(4, 3072), k_me * 768, 768, axis=1)

    shard_grads = _rs_chip_stage(pair_sums, received)

    grads = {n: _unpack_small(tot, n) for n in SMALL_ROWS}
    grads["gdn_conv_w"] = conv_grad[None]
    for n, g in zip(BIG, shard_grads):
        grads[n] = g[None]

    delta, new_m, new_v = {}, {}, {}
    sw = _pack_small({n: w[n] for n in SMALL_ROWS})
    sm = _pack_small({n: m[n] for n in SMALL_ROWS})
    sv = _pack_small({n: v[n] for n in SMALL_ROWS})
    sd, snm, snv = _adamw(sw, tot[:SMALL_PACK_ROWS], sm, sv, "adamw_small")
    for n in SMALL_ROWS:
        delta[n], new_m[n], new_v[n] = _unpack_small(sd, n), _unpack_small(snm, n), _unpack_small(snv, n)
    for n in BIG + ["gdn_conv_w"]:
        if n == "w_in":
            d, nm, nv = _adamw(w[n][0].T, grads[n][0].T, m[n][0].T, v[n][0].T, f"adamw_{n}")
            delta[n], new_m[n], new_v[n] = d.T[None], nm.T[None], nv.T[None]
        else:
            d, nm, nv = _adamw(w[n][0], grads[n][0], m[n][0], v[n][0], f"adamw_{n}")
            delta[n], new_m[n], new_v[n] = d[None], nm[None], nv[None]

    return (loss, grad_x[None], *[grads[n] for n in WEIGHTS], *[delta[n] for n in WEIGHTS],
            *[new_m[n] for n in WEIGHTS], *[new_v[n] for n in WEIGHTS])
```

```python
import functools
import math

import jax
import jax.numpy as jnp
from jax import lax
from jax.experimental import pallas as pl
from jax.experimental.pallas import tpu as pltpu

F32 = jnp.float32
BF16 = jnp.bfloat16
MESH = pl.DeviceIdType.MESH

D_MODEL = 2048
HEADS = 8
HEAD_DIM = 128
QK_DIM = 192
QK_PAD = 256
HALF_ROPE = 32
CHUNK = 64
NORM_EPS = 1e-6
W_IN_COLS = 5968
W_IN_PAD = 6144
GA_LANE = 64
GB_LANE = 72
ADAM_LR, ADAM_B1, ADAM_B2, ADAM_EPS, ADAM_WD, ADAM_STEP = 0.001, 0.9, 0.999, 1e-08, 0.01, 10
VMEM_LIMIT_V7X = 52 * 1024 * 1024
HI = lax.Precision.HIGHEST
NN = (((1,), (0,)), ((), ()))
NT = (((1,), (1,)), ((), ()))
TN = (((0,), (0,)), ((), ()))

TILES = {"row": 512, "attn": 2048, "mm": 1024}


def _call(body, *, name, grid, in_specs, out_specs, out_shape, args, scratch=(), sem=None):
    def kfn(*refs):
        body(*refs)
    if sem is None:
        sem = ("arbitrary",) * len(grid)
    return pl.pallas_call(
        kfn, name=name, grid=grid, in_specs=in_specs, out_specs=out_specs, out_shape=out_shape,
        scratch_shapes=list(scratch),
        compiler_params=pltpu.CompilerParams(dimension_semantics=sem, vmem_limit_bytes=VMEM_LIMIT_V7X),
    )(*args)


def _rows(tm, w, cb=0):
    return pl.BlockSpec((tm, w), lambda i: (i, cb))


def _full(shape):
    n = len(shape)
    return pl.BlockSpec(shape, lambda *_: (0,) * n)


def _sds(shape, dtype):
    return jax.ShapeDtypeStruct(shape, dtype)


def _acc8(x):
    tm, c = x.shape
    return jnp.sum(x.reshape(tm // 8, 8, c), axis=0)


def _sigmoid(x):
    return 1.0 / (1.0 + jnp.exp(-x))


def _silu(x):
    return x * _sigmoid(x)


def _dsilu(x):
    s = _sigmoid(x)
    return s * (1.0 + x * (1.0 - s))


def _dot(a, b, dims=NN):
    return lax.dot_general(a.astype(BF16), b.astype(BF16), dims, preferred_element_type=F32)


def _dot_hi(a, b, dims=NN):
    return lax.dot_general(a, b, dims, precision=HI, preferred_element_type=F32)


def _matmul(a, b, *, mode, out_dtype, name, tm=None, tn=None, tk=None):
    if mode == "tn":
        kdim, m = a.shape
    else:
        m, kdim = a.shape
    n = b.shape[0] if mode == "nt" else b.shape[1]
    tm = min(tm or TILES["mm"], m)
    tn = min(tn or TILES["mm"], n)
    tk = min(tk or kdim, kdim)
    nk = kdim // tk
    dims = {"nn": NN, "nt": NT, "tn": TN}[mode]
    if mode == "tn":
        a_spec = pl.BlockSpec((tk, tm), lambda i, j, k: (k, i))
    else:
        a_spec = pl.BlockSpec((tm, tk), lambda i, j, k: (i, k))
    if mode == "nt":
        b_spec = pl.BlockSpec((tn, tk), lambda i, j, k: (j, k))
    else:
        b_spec = pl.BlockSpec((tk, tn), lambda i, j, k: (k, j))

    def body(a_ref, b_ref, o_ref):
        r = _dot(a_ref[...], b_ref[...], dims)
        if nk == 1:
            o_ref[...] = r.astype(o_ref.dtype)
        else:
            k = pl.program_id(2)

            @pl.when(k == 0)
            def _():
                o_ref[...] = r

            @pl.when(k > 0)
            def _():
                o_ref[...] += r

    if nk > 1:
        assert out_dtype == F32
    return _call(body, name=name, grid=(m // tm, n // tn, nk), in_specs=[a_spec, b_spec],
                 out_specs=pl.BlockSpec((tm, tn), lambda i, j, k: (i, j)), out_shape=_sds((m, n), out_dtype),
                 args=(a, b))


def _norm1_fwd(x, gain):
    s = x.shape[0]
    tm = min(TILES["row"], s)

    def body(x_ref, g_ref, o_ref):
        xv = x_ref[...]
        r = lax.rsqrt(jnp.mean(xv * xv, axis=-1, keepdims=True) + NORM_EPS)
        o_ref[...] = (xv * r * g_ref[...]).astype(BF16)

    return _call(body, name="norm1_fwd", grid=(s // tm,), in_specs=[_rows(tm, D_MODEL), _full((1, D_MODEL))],
                 out_specs=_rows(tm, D_MODEL), out_shape=_sds((s, D_MODEL), BF16), args=(x, gain))


def _norm1_bwd(x, gain, dxn, dy):
    s = x.shape[0]
    tm = min(TILES["row"], s)

    def body(x_ref, g_ref, dxn_ref, dy_ref, gx_ref, dg_ref):
        xv = x_ref[...]
        r = lax.rsqrt(jnp.mean(xv * xv, axis=-1, keepdims=True) + NORM_EPS)
        nrm = xv * r
        d = dxn_ref[...].astype(F32)
        dn = d * g_ref[...]
        gx_ref[...] = dy_ref[...].astype(F32) + r * (dn - nrm * jnp.mean(dn * nrm, axis=-1, keepdims=True))

        @pl.when(pl.program_id(0) == 0)
        def _():
            dg_ref[...] = jnp.zeros_like(dg_ref)

        dg_ref[...] += _acc8(d * nrm)

    return _call(body, name="norm1_bwd", grid=(s // tm,),
                 in_specs=[_rows(tm, D_MODEL), _full((1, D_MODEL)), _rows(tm, D_MODEL), _rows(tm, D_MODEL)],
                 out_specs=[_rows(tm, D_MODEL), _full((8, D_MODEL))],
                 out_shape=[_sds((s, D_MODEL), F32), _sds((8, D_MODEL), F32)], args=(x, gain, dxn, dy))


def _rms(xv, width):
    return lax.rsqrt(jnp.sum(xv * xv, axis=-1, keepdims=True) * (1.0 / width) + NORM_EPS)


def _mla_a_norm(proj, gq, gkv):
    s = proj.shape[0]
    tm = min(TILES["row"], s)

    def body(cq_ref, ckv_ref, gq_ref, gkv_ref, oq_ref, okv_ref):
        a = cq_ref[...].astype(F32)
        oq_ref[...] = (a * _rms(a, 512) * gq_ref[...]).astype(BF16)
        b = ckv_ref[...].astype(F32)
        okv_ref[...] = (b * _rms(b, 256) * gkv_ref[...]).astype(BF16)

    return _call(body, name="mla_a_norm", grid=(s // tm,),
                 in_specs=[_rows(tm, 512, 0), _rows(tm, 256, 2), _full((1, 512)), _full((1, 256))],
                 out_specs=[_rows(tm, 512), _rows(tm, 256)],
                 out_shape=[_sds((s, 512), BF16), _sds((s, 256), BF16)], args=(proj, proj, gq, gkv))


def _rms_bwd(xv, gain, d, width):
    r = _rms(xv, width)
    nrm = xv * r
    dn = d * gain
    dx = r * (dn - nrm * (jnp.sum(dn * nrm, axis=-1, keepdims=True) * (1.0 / width)))
    return dx, d * nrm


def _mla_a_norm_bwd(proj, gq, gkv, dcqn, dckvn):
    s = proj.shape[0]
    tm = min(TILES["row"], s)

    def body(cq_ref, ckv_ref, gq_ref, gkv_ref, dq_ref, dkv_ref, oq_ref, okv_ref, aq_ref, akv_ref):
        dxq, gq_part = _rms_bwd(cq_ref[...].astype(F32), gq_ref[...], dq_ref[...].astype(F32), 512)
        dxk, gk_part = _rms_bwd(ckv_ref[...].astype(F32), gkv_ref[...], dkv_ref[...].astype(F32), 256)
        oq_ref[...] = dxq.astype(BF16)
        okv_ref[...] = dxk.astype(BF16)

        @pl.when(pl.program_id(0) == 0)
        def _():
            aq_ref[...] = jnp.zeros_like(aq_ref)
            akv_ref[...] = jnp.zeros_like(akv_ref)

        aq_ref[...] += _acc8(gq_part)
        akv_ref[...] += _acc8(gk_part)

    return _call(body, name="mla_a_norm_bwd", grid=(s // tm,),
                 in_specs=[_rows(tm, 512, 0), _rows(tm, 256, 2), _full((1, 512)), _full((1, 256)),
                           _rows(tm, 512), _rows(tm, 256)],
                 out_specs=[_rows(tm, 512), _rows(tm, 256), _full((8, 512)), _full((8, 256))],
                 out_shape=[_sds((s, 512), BF16), _sds((s, 256), BF16), _sds((8, 512), F32), _sds((8, 256), F32)],
                 args=(proj, proj, gq, gkv, dcqn, dckvn))


def _swap32(r):
    lane = lax.broadcasted_iota(jnp.int32, r.shape, 1)
    return jnp.where(lane < HALF_ROPE, pltpu.roll(r, 128 - HALF_ROPE, 1), pltpu.roll(r, HALF_ROPE, 1))


def _mla_post_fwd(q_pre, kv_pre, proj, cs, sn, gq, gk):
    s = q_pre.shape[0]
    tm = min(TILES["row"], s)

    def body(qp_ref, kvp_ref, misc_ref, cs_ref, sn_ref, gq_ref, gk_ref, q_ref, k_ref, v_ref):
        csv, snv = cs_ref[...], sn_ref[...]
        lane = lax.broadcasted_iota(jnp.int32, (tm, 128), 1)
        kr = jnp.where(lane < 64, misc_ref[...], 0.0)
        for h in range(HEADS):
            for src, g_ref, o_ref in ((None, gq_ref, q_ref), (kr, gk_ref, k_ref)):
                if src is None:
                    xv = qp_ref[:, QK_PAD * h:QK_PAD * (h + 1)].astype(F32)
                else:
                    xv = jnp.concatenate([kvp_ref[:, 256 * h:256 * h + 128].astype(F32), src], axis=-1)
                y = xv * _rms(xv, QK_DIM) * g_ref[...]
                if src is None:
                    y = y * Q_PRESCALE
                hi = y[:, 128:]
                hi = hi * csv + _swap32(hi) * snv
                o_ref[:, QK_PAD * h:QK_PAD * h + 128] = y[:, :128].astype(BF16)
                o_ref[:, QK_PAD * h + 128:QK_PAD * (h + 1)] = hi.astype(BF16)
            v_ref[:, 128 * h:128 * (h + 1)] = kvp_ref[:, 256 * h + 128:256 * (h + 1)].astype(BF16)

    return _call(body, name="mla_post_fwd", grid=(s // tm,),
                 in_specs=[_rows(tm, 2048), _rows(tm, 2048), _rows(tm, 128), _rows(tm, 128), _rows(tm, 128),
                           _full((1, QK_PAD)), _full((1, QK_PAD))],
                 out_specs=[_rows(tm, 2048), _rows(tm, 2048), _rows(tm, 1024)],
                 out_shape=[_sds((s, 2048), BF16), _sds((s, 2048), BF16), _sds((s, 1024), BF16)],
                 args=(q_pre, kv_pre, proj, cs, sn, gq, gk))


def _mla_post_bwd(q_pre, kv_pre, proj, cs, sn, gq, gk, dq, dk, dv):
    s = q_pre.shape[0]
    tm = min(TILES["row"], s)

    def body(qp_ref, kvp_ref, misc_ref, cs_ref, sn_ref, gq_ref, gk_ref, dq_ref, dk_ref, dv_ref,
             oq_ref, okv_ref, okr_ref, agq_ref, agk_ref):
        csv, snv = cs_ref[...], sn_ref[...]
        lane = lax.broadcasted_iota(jnp.int32, (tm, 128), 1)
        kr = jnp.where(lane < 64, misc_ref[...], 0.0)

        @pl.when(pl.program_id(0) == 0)
        def _():
            agq_ref[...] = jnp.zeros_like(agq_ref)
            agk_ref[...] = jnp.zeros_like(agk_ref)

        dkr = jnp.zeros((tm, 128), F32)
        for h in range(HEADS):
            for which in (0, 1):
                if which == 0:
                    xv = qp_ref[:, QK_PAD * h:QK_PAD * (h + 1)].astype(F32)
                    d_ref, g_ref, a_ref = dq_ref, gq_ref, agq_ref
                else:
                    xv = jnp.concatenate([kvp_ref[:, 256 * h:256 * h + 128].astype(F32), kr], axis=-1)
                    d_ref, g_ref, a_ref = dk_ref, gk_ref, agk_ref
                dhi = d_ref[:, QK_PAD * h + 128:QK_PAD * (h + 1)].astype(F32)
                dhi = dhi * csv - _swap32(dhi) * snv
                dyv = jnp.concatenate([d_ref[:, QK_PAD * h:QK_PAD * h + 128].astype(F32), dhi], axis=-1)
                if which == 0:
                    dyv = dyv * ATTN_SCALE
                dx, gpart = _rms_bwd(xv, g_ref[...], dyv, QK_DIM)
                a_ref[...] += _acc8(gpart)
                if which == 0:
                    oq_ref[:, QK_PAD * h:QK_PAD * (h + 1)] = dx.astype(BF16)
                else:
                    okv_ref[:, 256 * h:256 * h + 128] = dx[:, :128].astype(BF16)
                    dkr = dkr + dx[:, 128:]
            okv_ref[:, 256 * h + 128:256 * (h + 1)] = dv_ref[:, 128 * h:128 * (h + 1)].astype(BF16)
        okr_ref[...] = dkr

    return _call(body, name="mla_post_bwd", grid=(s // tm,),
                 in_specs=[_rows(tm, 2048), _rows(tm, 2048), _rows(tm, 128), _rows(tm, 128), _rows(tm, 128),
                           _full((1, QK_PAD)), _full((1, QK_PAD)), _rows(tm, 2048), _rows(tm, 2048), _rows(tm, 1024)],
                 out_specs=[_rows(tm, 2048), _rows(tm, 2048), _rows(tm, 128), _full((8, QK_PAD)), _full((8, QK_PAD))],
                 out_shape=[_sds((s, 2048), BF16), _sds((s, 2048), BF16), _sds((s, 128), F32),
                            _sds((8, QK_PAD), F32), _sds((8, QK_PAD), F32)],
                 args=(q_pre, kv_pre, proj, cs, sn, gq, gk, dq, dk, dv))


ATTN_SCALE = QK_DIM ** -0.5
NEG = -1e30


LOG2E = 1.4426950408889634
LN2 = 0.6931471805599453
Q_PRESCALE = ATTN_SCALE * LOG2E
ATTN_SUB_FWD = 512
ATTN_SUB_BWD = 256


def _causal_pairs(nq, kv_major):
    prs = [(i, j) for i in range(nq) for j in range(i + 1)]
    if kv_major:
        prs.sort(key=lambda ij: (ij[1], ij[0]))
    return (jnp.asarray([p[0] for p in prs], jnp.int32), jnp.asarray([p[1] for p in prs], jnp.int32))


def _pair_call(body, *, name, tables, in_specs, out_specs, out_shape, scratch, args):
    def kfn(*refs):
        body(*refs)
    spec = pltpu.PrefetchScalarGridSpec(num_scalar_prefetch=2, grid=(HEADS, tables[0].shape[0]), in_specs=in_specs,
                                        out_specs=out_specs, scratch_shapes=list(scratch))
    return pl.pallas_call(
        kfn, name=name, grid_spec=spec, out_shape=out_shape,
        compiler_params=pltpu.CompilerParams(dimension_semantics=("parallel", "arbitrary"),
                                             vmem_limit_bytes=VMEM_LIMIT_V7X))(*tables, *args)


def _diag_mask(sc, ts, qs):
    row = lax.broadcasted_iota(jnp.int32, sc.shape, 0) + qs * ts
    col = lax.broadcasted_iota(jnp.int32, sc.shape, 1)
    return jnp.where(col <= row, sc, NEG)


def _attn_fwd(q, k, v):
    s = q.shape[0]
    t = min(TILES["attn"], s)
    ts = min(ATTN_SUB_FWD, t)
    nq = s // t

    def slabs(q_ref, k_ref, v_ref, m_s, l_s, acc_s, diag):
        def scores(qs):
            kw = (qs + 1) * ts if diag else t
            sc = lax.dot_general(q_ref[qs * ts:(qs + 1) * ts, :], k_ref[0:kw, :], NT, preferred_element_type=F32)
            return _diag_mask(sc, ts, qs) if diag else sc

        nsub = t // ts
        sc_next = scores(0)
        for qs in range(nsub):
            rq = slice(qs * ts, (qs + 1) * ts)
            kw = (qs + 1) * ts if diag else t
            sc = sc_next
            if qs + 1 < nsub:
                sc_next = scores(qs + 1)
            m_prev = m_s[rq, :]
            m_new = jnp.maximum(m_prev, jnp.max(sc, axis=-1, keepdims=True))
            p = jnp.exp2(sc - m_new)
            alpha = jnp.exp2(m_prev - m_new)
            l_s[rq, :] = alpha * l_s[rq, :] + jnp.sum(p, axis=-1, keepdims=True)
            acc_s[rq, :] = acc_s[rq, :] * alpha + lax.dot_general(p.astype(BF16), v_ref[0:kw, :], NN,
                                                                  preferred_element_type=F32)
            m_s[rq, :] = m_new

    def body(it_ref, jt_ref, q_ref, k_ref, v_ref, o_ref, lse_ref, m_s, l_s, acc_s):
        p = pl.program_id(1)
        i, j = it_ref[p], jt_ref[p]

        @pl.when(j == 0)
        def _():
            m_s[...] = jnp.full_like(m_s, NEG)
            l_s[...] = jnp.zeros_like(l_s)
            acc_s[...] = jnp.zeros_like(acc_s)

        @pl.when(j < i)
        def _():
            slabs(q_ref, k_ref, v_ref, m_s, l_s, acc_s, False)

        @pl.when(j == i)
        def _():
            slabs(q_ref, k_ref, v_ref, m_s, l_s, acc_s, True)
            o_ref[...] = acc_s[...] / l_s[...]
            lse_ref[...] = m_s[...] + jnp.log2(l_s[...])

    qb = lambda h, p, it, jt: (it[p], h)
    kb = lambda h, p, it, jt: (jt[p], h)
    return _pair_call(
        body, name="attn_fwd", tables=_causal_pairs(nq, kv_major=False),
        in_specs=[pl.BlockSpec((t, QK_PAD), qb), pl.BlockSpec((t, QK_PAD), kb), pl.BlockSpec((t, HEAD_DIM), kb)],
        out_specs=[pl.BlockSpec((t, HEAD_DIM), qb),
                   pl.BlockSpec((None, t, 1), lambda h, p, it, jt: (h, it[p], 0))],
        out_shape=[_sds((s, HEADS * HEAD_DIM), F32), _sds((HEADS, s, 1), F32)],
        scratch=[pltpu.VMEM((t, 1), F32), pltpu.VMEM((t, 1), F32), pltpu.VMEM((t, HEAD_DIM), F32)],
        args=(q, k, v))


def _attn_bwd(q, k, v, lse_rows, delta_rows, do):
    s = q.shape[0]
    t = min(TILES["attn"], s)
    ts = min(ATTN_SUB_BWD, t)
    nq = s // t

    def slabs(q_ref, k_ref, v_ref, lse_ref, delta_ref, do_ref, dq_ref, dk_ref, dv_ref, i, diag):
        def products(qs):
            rq = slice(qs * ts, (qs + 1) * ts)
            kw = (qs + 1) * ts if diag else t
            qv, dob = q_ref[rq, :], do_ref[rq, :]
            sct = lax.dot_general(k_ref[0:kw, :], qv, NT, preferred_element_type=F32)
            dpt = lax.dot_general(v_ref[0:kw, :], dob, NT, preferred_element_type=F32)
            if diag:
                row = lax.broadcasted_iota(jnp.int32, sct.shape, 0)
                col = lax.broadcasted_iota(jnp.int32, sct.shape, 1) + qs * ts
                sct = jnp.where(row <= col, sct, NEG)
            return qv, dob, sct, dpt

        nsub = t // ts
        ahead = products(0)
        for qs in range(nsub):
            rq = slice(qs * ts, (qs + 1) * ts)
            kw = (qs + 1) * ts if diag else t
            qv, dob, sct, dpt = ahead
            if qs + 1 < nsub:
                ahead = products(qs + 1)
            pt = jnp.exp2(sct - lse_ref[:, rq])
            dv_ref[0:kw, :] += lax.dot_general(pt.astype(BF16), dob, NN, preferred_element_type=F32)
            dst = (pt * (dpt - delta_ref[:, rq])).astype(BF16)
            dk_ref[0:kw, :] += lax.dot_general(dst, qv, NN, preferred_element_type=F32)
            rows = pl.ds(pl.multiple_of(i * t + qs * ts, ts), ts)
            dq_ref[rows, :] += lax.dot_general(dst, k_ref[0:kw, :], TN, preferred_element_type=F32)

    npairs = nq * (nq + 1) // 2

    def body(it_ref, jt_ref, q_ref, k_ref, v_ref, lse_ref, delta_ref, do_ref, dq_ref, dk_ref, dv_ref,
             dq_acc, dk_acc, dv_acc):
        p = pl.program_id(1)
        i, j = it_ref[p], jt_ref[p]
        refs = (q_ref, k_ref, v_ref, lse_ref, delta_ref, do_ref, dq_acc, dk_acc, dv_acc)

        @pl.when(p == 0)
        def _():
            dq_acc[...] = jnp.zeros_like(dq_acc)

        @pl.when(i == j)
        def _():
            dk_acc[...] = jnp.zeros_like(dk_acc)
            dv_acc[...] = jnp.zeros_like(dv_acc)

        @pl.when(i > j)
        def _():
            slabs(*refs, i, False)

        @pl.when(i == j)
        def _():
            slabs(*refs, i, True)

        @pl.when(i == nq - 1)
        def _():
            dk_ref[...] = (dk_acc[...] * LN2).astype(BF16)
            dv_ref[...] = dv_acc[...].astype(BF16)

        @pl.when(p == npairs - 1)
        def _():
            dq_ref[...] = dq_acc[...].astype(BF16)

    qb = lambda h, p, it, jt: (it[p], h)
    kb = lambda h, p, it, jt: (jt[p], h)
    rowb = pl.BlockSpec((None, 1, t), lambda h, p, it, jt: (h, 0, it[p]))
    return _pair_call(
        body, name="attn_bwd", tables=_causal_pairs(nq, kv_major=True),
        in_specs=[pl.BlockSpec((t, QK_PAD), qb), pl.BlockSpec((t, QK_PAD), kb), pl.BlockSpec((t, HEAD_DIM), kb),
                  rowb, rowb, pl.BlockSpec((t, HEAD_DIM), qb)],
        out_specs=[pl.BlockSpec((s, QK_PAD), lambda h, p, it, jt: (0, h)), pl.BlockSpec((t, QK_PAD), kb),
                   pl.BlockSpec((t, HEAD_DIM), kb)],
        out_shape=[_sds((s, HEADS * QK_PAD), BF16), _sds((s, HEADS * QK_PAD), BF16), _sds((s, HEADS * HEAD_DIM), BF16)],
        scratch=[pltpu.VMEM((s, QK_PAD), F32), pltpu.VMEM((t, QK_PAD), F32), pltpu.VMEM((t, HEAD_DIM), F32)],
        args=(q, k, v, lse_rows, delta_rows, do))


GDN_Q_SCALE = HEAD_DIM ** -0.5


def _shift_down(xv, prev8, sft):
    rolled = pltpu.roll(xv, sft, 0)
    top = pltpu.roll(jnp.concatenate([prev8, xv[:8]], axis=0), sft, 0)[8:]
    return jnp.concatenate([top, rolled[8:]], axis=0)


def _shift_up(xv, next8, sft):
    tm = xv.shape[0]
    rolled = pltpu.roll(xv, tm - sft, 0)
    bot = pltpu.roll(jnp.concatenate([xv[tm - 8:], next8], axis=0), 16 - sft, 0)[:8]
    return jnp.concatenate([rolled[:tm - 8], bot], axis=0)


def _conv_z(xv, prev8, w_ref):
    z = xv * w_ref[3:4, :]
    for sft in (1, 2, 3):
        z = z + _shift_down(xv, prev8, sft) * w_ref[3 - sft:4 - sft, :]
    return z


def _conv_specs(s, tm):
    nb16 = tm // 16
    cur = pl.BlockSpec((tm, 1024), lambda j, i: (i, 2 + j))
    prev = pl.BlockSpec((16, 1024), lambda j, i: (jnp.maximum(i * nb16 - 1, 0), 2 + j))
    return cur, prev


def _prev8(xp_ref, i):
    return jnp.where(i > 0, xp_ref[...].astype(F32)[8:], 0.0)


def _gdn_conv_fwd(proj, conv_w):
    s = proj.shape[0]
    tm = min(TILES["row"], s)
    cur, prev = _conv_specs(s, tm)

    def body(x_ref, xp_ref, w_ref, o_ref):
        j, i = pl.program_id(0), pl.program_id(1)
        a = _silu(_conv_z(x_ref[...].astype(F32), _prev8(xp_ref, i), w_ref))
        qk_scale = jnp.where(j == 0, GDN_Q_SCALE, 1.0)
        for h in range(HEADS):
            seg = a[:, 128 * h:128 * (h + 1)]
            r = lax.rsqrt(jnp.sum(seg * seg, axis=-1, keepdims=True) + NORM_EPS)
            o_ref[:, 128 * h:128 * (h + 1)] = jnp.where(j < 2, seg * r * qk_scale, seg).astype(BF16)

    return _call(body, name="gdn_conv_fwd", grid=(3, s // tm),
                 in_specs=[cur, prev, pl.BlockSpec((4, 1024), lambda j, i: (0, j))],
                 out_specs=pl.BlockSpec((tm, 1024), lambda j, i: (i, j)), out_shape=_sds((s, 3072), BF16),
                 args=(proj, proj, conv_w))


def _gdn_conv_bwd_a(proj, conv_w, dqkv):
    s = proj.shape[0]
    tm = min(TILES["row"], s)
    cur, prev = _conv_specs(s, tm)

    def body(x_ref, xp_ref, w_ref, d_ref, o_ref):
        j, i = pl.program_id(0), pl.program_id(1)
        z = _conv_z(x_ref[...].astype(F32), _prev8(xp_ref, i), w_ref)
        a = _silu(z)
        dsl = _dsilu(z)
        qk_scale = jnp.where(j == 0, GDN_Q_SCALE, 1.0)
        for h in range(HEADS):
            sl = slice(128 * h, 128 * (h + 1))
            seg = a[:, sl]
            dyv = d_ref[:, sl].astype(F32)
            r = lax.rsqrt(jnp.sum(seg * seg, axis=-1, keepdims=True) + NORM_EPS)
            yh = seg * r
            da_n = qk_scale * r * (dyv - yh * jnp.sum(yh * dyv, axis=-1, keepdims=True))
            o_ref[:, sl] = (jnp.where(j < 2, da_n, dyv) * dsl[:, sl]).astype(BF16)

    return _call(body, name="gdn_conv_bwd_a", grid=(3, s // tm),
                 in_specs=[cur, prev, pl.BlockSpec((4, 1024), lambda j, i: (0, j)),
                           pl.BlockSpec((tm, 1024), lambda j, i: (i, j))],
                 out_specs=pl.BlockSpec((tm, 1024), lambda j, i: (i, j)), out_shape=_sds((s, 3072), BF16),
                 args=(proj, proj, conv_w, dqkv))


def _gdn_conv_bwd_b(proj, conv_w, dz):
    s = proj.shape[0]
    tm = min(TILES["row"], s)
    nb16 = tm // 16
    last16 = s // 16 - 1
    cur, prev = _conv_specs(s, tm)

    def body(x_ref, w_ref, dz_ref, dzn_ref, dx_ref, dw_ref):
        i = pl.program_id(1)
        next8 = jnp.where(i < pl.num_programs(1) - 1, dzn_ref[...].astype(F32)[:8], 0.0)
        xv, dzv = x_ref[...].astype(F32), dz_ref[...].astype(F32)

        @pl.when(i == 0)
        def _():
            dw_ref[...] = jnp.zeros_like(dw_ref)

        dx = dzv * w_ref[3:4, :]
        dw_ref[3:4, :] += jnp.sum(dzv * xv, axis=0, keepdims=True)
        for sft in (1, 2, 3):
            up = _shift_up(dzv, next8, sft)
            dx = dx + up * w_ref[3 - sft:4 - sft, :]
            dw_ref[3 - sft:4 - sft, :] += jnp.sum(up * xv, axis=0, keepdims=True)
        dx_ref[...] = dx.astype(BF16)

    return _call(body, name="gdn_conv_bwd_b", grid=(3, s // tm),
                 in_specs=[cur, pl.BlockSpec((4, 1024), lambda j, i: (0, j)),
                           pl.BlockSpec((tm, 1024), lambda j, i: (i, j)),
                           pl.BlockSpec((16, 1024), lambda j, i: (jnp.minimum((i + 1) * nb16, last16), j))],
                 out_specs=[pl.BlockSpec((tm, 1024), lambda j, i: (i, j)), pl.BlockSpec((4, 1024), lambda j, i: (0, j))],
                 out_shape=[_sds((s, 3072), BF16), _sds((4, 3072), F32)], args=(proj, conv_w, dz, dz))


def _softplus(xv):
    return jnp.maximum(xv, 0.0) + jnp.log(1.0 + jnp.exp(-jnp.abs(xv)))


def _gdn_gates_fwd(proj, alog128, dtb128):
    s = proj.shape[0]
    tm = min(TILES["row"], s)

    def body(m_ref, a_ref, b_ref, o_ref):
        mv = m_ref[...]
        lane = lax.broadcasted_iota(jnp.int32, mv.shape, 1)
        g = -jnp.exp(a_ref[...]) * _softplus(mv + b_ref[...])
        is_g = (lane >= GA_LANE) & (lane < GA_LANE + HEADS)
        is_b = (lane >= GB_LANE) & (lane < GB_LANE + HEADS)
        o_ref[...] = jnp.where(is_g, g, jnp.where(is_b, _sigmoid(mv), 0.0))

    return _call(body, name="gdn_gates_fwd", grid=(s // tm,),
                 in_specs=[_rows(tm, 128), _full((1, 128)), _full((1, 128))],
                 out_specs=_rows(tm, 128), out_shape=_sds((s, 128), F32), args=(proj, alog128, dtb128))


def _gdn_gates_bwd(proj, alog128, dtb128, gbeta, dgbeta, dkr):
    s = proj.shape[0]
    tm = min(TILES["row"], s)

    def body(m_ref, a_ref, b_ref, gb_ref, d_ref, kr_ref, o_ref, da_ref, db_ref):
        mv, dv = m_ref[...], d_ref[...]
        lane = lax.broadcasted_iota(jnp.int32, mv.shape, 1)
        is_g = (lane >= GA_LANE) & (lane < GA_LANE + HEADS)
        is_b = (lane >= GB_LANE) & (lane < GB_LANE + HEADS)
        dga = jnp.where(is_g, dv * (-jnp.exp(a_ref[...])) * _sigmoid(mv + b_ref[...]), 0.0)
        beta = gb_ref[...]
        dgb = jnp.where(is_b, dv * beta * (1.0 - beta), 0.0)
        o_ref[...] = jnp.where(lane < 64, kr_ref[...], dga + dgb).astype(BF16)

        @pl.when(pl.program_id(0) == 0)
        def _():
            da_ref[...] = jnp.zeros_like(da_ref)
            db_ref[...] = jnp.zeros_like(db_ref)

        da_ref[...] += _acc8(jnp.where(is_g, dv * gb_ref[...], 0.0))
        db_ref[...] += _acc8(dga)

    return _call(body, name="gdn_gates_bwd", grid=(s // tm,),
                 in_specs=[_rows(tm, 128), _full((1, 128)), _full((1, 128)), _rows(tm, 128), _rows(tm, 128),
                           _rows(tm, 128)],
                 out_specs=[_rows(tm, 128), _full((8, 128)), _full((8, 128))],
                 out_shape=[_sds((s, 128), BF16), _sds((8, 128), F32), _sds((8, 128), F32)],
                 args=(proj, alog128, dtb128, gbeta, dgbeta, dkr))


def _col(mat, lane_idx, lane):
    return jnp.sum(jnp.where(lane == lane_idx, mat, 0.0), axis=-1, keepdims=True)


def _chunk_local(qh, kh, vh, gcol, bcol, ii, jj):
    lower, strict, eye = ii >= jj, ii > jj, ii == jj
    grow = jnp.sum(jnp.where(eye, gcol, 0.0), axis=0, keepdims=True)
    decay = jnp.where(lower, jnp.exp(jnp.where(lower, gcol - grow, 0.0)), 0.0)
    kb = kh * bcol
    vb = vh * bcol
    mm = _dot(kb, kh, NT)
    lmat = jnp.where(strict, mm * decay, 0.0)
    pw = -lmat
    tinv = jnp.where(eye, 1.0, 0.0) + pw
    for _ in range(5):
        pw = _dot_hi(pw, pw)
        tinv = tinv + _dot_hi(tinv, pw)
    egc = jnp.exp(gcol)
    kbg = kb * egc
    rhs = jnp.concatenate([vb, kbg], axis=-1)
    sol = _dot_hi(tinv, rhs)
    qk = _dot(qh, kh, NT)
    glast = jnp.sum(jnp.where(ii[:, :1] == CHUNK - 1, gcol, 0.0), axis=0, keepdims=True)
    ekd = jnp.exp(glast - gcol)
    return dict(decay=decay, kb=kb, vb=vb, mm=mm, lmat=lmat, tinv=tinv, egc=egc, kbg=kbg, rhs=rhs,
                u=sol[:, :HEAD_DIM], w=sol[:, HEAD_DIM:], qk=qk, amat=qk * decay, qd=qh * egc, ekd=ekd,
                kd=kh * ekd, gl=jnp.exp(glast), strict=strict, lower=lower, eye=eye)


def _tri(ii, jj):
    return jnp.where(ii >= jj, 1.0, 0.0)


def _gdn_fwd(qkv, gbeta):
    s = qkv.shape[0]
    n = s // CHUNK

    def body(qkv_ref, gb_ref, o_ref, st_ref, state):
        @pl.when(pl.program_id(0) == 0)
        def _():
            state[...] = jnp.zeros_like(state)

        ii = lax.broadcasted_iota(jnp.int32, (CHUNK, CHUNK), 0)
        jj = lax.broadcasted_iota(jnp.int32, (CHUNK, CHUNK), 1)
        lane = lax.broadcasted_iota(jnp.int32, (CHUNK, 128), 1)
        gbv = gb_ref[...]
        gc = _dot_hi(_tri(ii, jj), gbv)
        for h in range(HEADS):
            sl = slice(128 * h, 128 * (h + 1))
            qh = qkv_ref[:, 128 * h:128 * (h + 1)]
            kh = qkv_ref[:, 1024 + 128 * h:1024 + 128 * (h + 1)]
            vh = qkv_ref[:, 2048 + 128 * h:2048 + 128 * (h + 1)]
            c = _chunk_local(qh, kh, vh, _col(gc, GA_LANE + h, lane), _col(gbv, GB_LANE + h, lane), ii, jj)
            st = state[sl, :]
            st_ref[sl, :] = st
            vn = c["u"] - _dot(c["w"], st)
            o_ref[:, sl] = _dot(c["qd"], st) + _dot(c["amat"], vn)
            state[sl, :] = st * c["gl"] + _dot(c["kd"], vn, TN)

    return _call(body, name="gdn_fwd", grid=(n,),
                 in_specs=[_rows(CHUNK, 3072), _rows(CHUNK, 128)],
                 out_specs=[_rows(CHUNK, 1024), _rows(HEADS * 128, 128)],
                 out_shape=[_sds((s, 1024), F32), _sds((n * HEADS * 128, 128), F32)],
                 scratch=[pltpu.VMEM((HEADS * 128, 128), F32)], args=(qkv, gbeta))


def _gdn_bwd(qkv, gbeta, states, do):
    s = qkv.shape[0]
    n = s // CHUNK

    def body(qkv_ref, gb_ref, st_ref, do_ref, dqkv_ref, dgb_ref, dstate):
        @pl.when(pl.program_id(0) == 0)
        def _():
            dstate[...] = jnp.zeros_like(dstate)

        ii = lax.broadcasted_iota(jnp.int32, (CHUNK, CHUNK), 0)
        jj = lax.broadcasted_iota(jnp.int32, (CHUNK, CHUNK), 1)
        lane = lax.broadcasted_iota(jnp.int32, (CHUNK, 128), 1)
        row1 = ii[:, :1]
        gbv = gb_ref[...]
        gc = _dot_hi(_tri(ii, jj), gbv)
        dgc_all = jnp.zeros((CHUNK, 128), F32)
        db_all = jnp.zeros((CHUNK, 128), F32)
        for h in range(HEADS):
            sl = slice(128 * h, 128 * (h + 1))
            qh = qkv_ref[:, 128 * h:128 * (h + 1)]
            kh = qkv_ref[:, 1024 + 128 * h:1024 + 128 * (h + 1)]
            vh = qkv_ref[:, 2048 + 128 * h:2048 + 128 * (h + 1)]
            bcol = _col(gbv, GB_LANE + h, lane)
            c = _chunk_local(qh, kh, vh, _col(gc, GA_LANE + h, lane), bcol, ii, jj)
            st = st_ref[sl, :]
            dst = dstate[sl, :]
            dov = do_ref[:, sl]
            vn = c["u"] - _dot(c["w"], st)
            dvn = _dot(c["amat"], dov, TN) + _dot(c["kd"], dst)
            damat = jnp.where(c["lower"], _dot(dov, vn, NT), 0.0)
            dqd = _dot(dov, st, NT)
            dkd = _dot(vn, dst, NT)
            dw = -_dot(dvn, st, NT)
            dgl = jnp.sum(jnp.sum(st * dst, axis=-1, keepdims=True), axis=0, keepdims=True)
            dstate[sl, :] = _dot(c["qd"], dov, TN) + c["gl"] * dst - _dot(c["w"], dvn, TN)
            dsol = jnp.concatenate([dvn, dw], axis=-1)
            drhs = _dot_hi(c["tinv"], dsol, TN)
            dtinv = _dot_hi(dsol, c["rhs"], NT)
            dl = -_dot_hi(_dot_hi(c["tinv"], dtinv, TN), c["tinv"], NT)
            dl = jnp.where(c["strict"], dl, 0.0)
            dmm = dl * c["decay"]
            dqk = damat * c["decay"]
            wmat = dl * c["lmat"] + damat * c["amat"]
            dgc = jnp.sum(wmat, axis=-1, keepdims=True)
            wcol = jnp.sum(wmat, axis=0, keepdims=True)
            dgc = dgc - jnp.sum(jnp.where(c["eye"], wcol, 0.0), axis=-1, keepdims=True)
            dkb = _dot(dmm, kh) + drhs[:, HEAD_DIM:] * c["egc"]
            dk = _dot(dmm, c["kb"], TN) + _dot(dqk, qh, TN) + dkd * c["ekd"]
            dq = _dot(dqk, kh) + dqd * c["egc"]
            dgc = dgc + jnp.sum(drhs[:, HEAD_DIM:] * c["kbg"], axis=-1, keepdims=True)
            dgc = dgc + jnp.sum(dqd * c["qd"], axis=-1, keepdims=True)
            tmp = jnp.sum(dkd * c["kd"], axis=-1, keepdims=True)
            dgc = dgc - tmp
            dglast = jnp.sum(tmp, axis=0, keepdims=True) + dgl * c["gl"]
            dgc = dgc + jnp.where(row1 == CHUNK - 1, dglast, 0.0)
            dk = dk + dkb * bcol
            db = jnp.sum(dkb * kh, axis=-1, keepdims=True) + jnp.sum(drhs[:, :HEAD_DIM] * vh, axis=-1, keepdims=True)
            dqkv_ref[:, 128 * h:128 * (h + 1)] = dq
            dqkv_ref[:, 1024 + 128 * h:1024 + 128 * (h + 1)] = dk
            dqkv_ref[:, 2048 + 128 * h:2048 + 128 * (h + 1)] = drhs[:, :HEAD_DIM] * bcol
            dgc_all = dgc_all + jnp.where(lane == GA_LANE + h, dgc, 0.0)
            db_all = db_all + jnp.where(lane == GB_LANE + h, db, 0.0)
        dgb_ref[...] = _dot_hi(_tri(jj, ii), dgc_all) + db_all

    rev = lambda w: pl.BlockSpec((CHUNK, w), lambda i: (n - 1 - i, 0))
    return _call(body, name="gdn_bwd", grid=(n,),
                 in_specs=[rev(3072), rev(128), pl.BlockSpec((HEADS * 128, 128), lambda i: (n - 1 - i, 0)), rev(1024)],
                 out_specs=[rev(3072), rev(128)],
                 out_shape=[_sds((s, 3072), F32), _sds((s, 128), F32)],
                 scratch=[pltpu.VMEM((HEADS * 128, 128), F32)], args=(qkv, gbeta, states, do))


NN_B = (((2,), (1,)), ((0,), (0,)))
NT_B = (((2,), (2,)), ((0,), (0,)))
TN_B = (((1,), (1,)), ((0,), (0,)))
GDN_PRE_CHUNKS = 4
GDN_POST_CHUNKS = 2
GDN_SEQ_CHUNKS = 8


def _gather_heads(qkv_ref, gc, gbv, qs, ks, vs, gs, bs, nchunks):
    lane = lax.broadcasted_iota(jnp.int32, (CHUNK, 128), 1)
    for c in range(nchunks):
        rows = slice(CHUNK * c, CHUNK * (c + 1))
        for h in range(HEADS):
            b = HEADS * c + h
            qs[b] = qkv_ref[rows, 128 * h:128 * (h + 1)].astype(F32)
            ks[b] = qkv_ref[rows, 1024 + 128 * h:1024 + 128 * (h + 1)].astype(F32)
            vs[b] = qkv_ref[rows, 2048 + 128 * h:2048 + 128 * (h + 1)].astype(F32)
            gs[b] = jnp.broadcast_to(_col(gc[rows], GA_LANE + h, lane), (CHUNK, 128))
            bs[b] = jnp.broadcast_to(_col(gbv[rows], GB_LANE + h, lane), (CHUNK, 128))


def _block_tri(rows, transpose=False):
    ri = lax.broadcasted_iota(jnp.int32, (rows, rows), 0)
    ci = lax.broadcasted_iota(jnp.int32, (rows, rows), 1)
    same = (ri >> 6) == (ci >> 6)
    return jnp.where(same & ((ci >= ri) if transpose else (ri >= ci)), 1.0, 0.0)


def _local_b(q, k, v, g128, b128):
    ii = lax.broadcasted_iota(jnp.int32, (1, CHUNK, CHUNK), 1)
    jj = lax.broadcasted_iota(jnp.int32, (1, CHUNK, CHUNK), 2)
    lower, strict, eye = ii >= jj, ii > jj, ii == jj
    g64 = g128[:, :, :CHUNK]
    grow = jnp.sum(jnp.where(eye, g64, 0.0), axis=1, keepdims=True)
    decay = jnp.where(lower, jnp.exp(jnp.where(lower, g64 - grow, 0.0)), 0.0)
    kb = k * b128
    vb = v * b128
    mm = lax.dot_general(kb.astype(BF16), k.astype(BF16), NT_B, preferred_element_type=F32)
    lmat = jnp.where(strict, mm * decay, 0.0)
    egc = jnp.exp(g128)
    kbg = kb * egc
    qk = lax.dot_general(q.astype(BF16), k.astype(BF16), NT_B, preferred_element_type=F32)
    row = lax.broadcasted_iota(jnp.int32, (1, CHUNK, 128), 1)
    glast = jnp.sum(jnp.where(row == CHUNK - 1, g128, 0.0), axis=1, keepdims=True)
    ekd = jnp.exp(glast - g128)
    return dict(decay=decay, kb=kb, vb=vb, lmat=lmat, egc=egc, kbg=kbg, amat=qk * decay, qd=q * egc, ekd=ekd,
                kd=k * ekd, gl=jnp.exp(glast), lower=lower, strict=strict, eye=eye)


def _bdot(a, b, dims):
    return lax.dot_general(a.astype(BF16), b.astype(BF16), dims, preferred_element_type=F32)


def _split(a):
    hi = a.astype(BF16)
    return hi, (a - hi.astype(F32)).astype(BF16)


def _bdot_hi(a, b, dims):
    ah, al = _split(a)
    bh, bl = _split(b)
    d = lambda x, y: lax.dot_general(x, y, dims, preferred_element_type=F32)
    return d(ah, bh) + d(ah, bl) + d(al, bh)


def _gdn_pre(qkv, gbeta):
    s = qkv.shape[0]
    n = s // CHUNK
    cb = min(GDN_PRE_CHUNKS, n)
    nb = cb * HEADS
    rows = cb * CHUNK

    def body(qkv_ref, gb_ref, u_ref, w_ref, qd_ref, kd_ref, a_ref, t_ref, gl_ref, qs, ks, vs, gs, bs):
        gbv = gb_ref[...]
        gc = _dot_hi(_block_tri(rows), gbv)
        _gather_heads(qkv_ref, gc, gbv, qs, ks, vs, gs, bs, cb)
        c = _local_b(qs[...], ks[...], vs[...], gs[...], bs[...])
        pw = -c["lmat"]
        tinv = jnp.where(c["eye"], 1.0, 0.0) + pw
        for _ in range(5):
            pw = _bdot_hi(pw, pw, NN_B)
            tinv = tinv + _bdot_hi(tinv, pw, NN_B)
        u_ref[...] = _bdot_hi(tinv, c["vb"], NN_B)
        w_ref[...] = _bdot_hi(tinv, c["kbg"], NN_B).astype(BF16)
        qd_ref[...] = c["qd"].astype(BF16)
        kd_ref[...] = c["kd"].astype(BF16)
        a_ref[...] = c["amat"].astype(BF16)
        t_ref[...] = tinv
        gl_ref[...] = c["gl"]

    b3 = lambda d: pl.BlockSpec((nb, CHUNK, d), lambda i: (i, 0, 0))
    nt = n * HEADS
    return _call(body, name="gdn_pre", grid=(n // cb,),
                 in_specs=[_rows(rows, 3072), _rows(rows, 128)],
                 out_specs=[b3(128), b3(128), b3(128), b3(128), b3(CHUNK), b3(CHUNK),
                            pl.BlockSpec((nb, 1, 128), lambda i: (i, 0, 0))],
                 out_shape=[_sds((nt, CHUNK, 128), F32), _sds((nt, CHUNK, 128), BF16), _sds((nt, CHUNK, 128), BF16),
                            _sds((nt, CHUNK, 128), BF16), _sds((nt, CHUNK, CHUNK), BF16), _sds((nt, CHUNK, CHUNK), F32),
                            _sds((nt, 1, 128), F32)],
                 scratch=[pltpu.VMEM((nb, CHUNK, 128), F32)] * 5, sem=("parallel",), args=(qkv, gbeta))


def _gdn_scan_fwd(u, w, qd, kd, amat, gl):
    nt = u.shape[0]
    n = nt // HEADS
    cs = min(GDN_SEQ_CHUNKS, n)

    def body(u_ref, w_ref, qd_ref, kd_ref, a_ref, gl_ref, o_ref, st_ref, state):
        @pl.when(pl.program_id(0) == 0)
        def _():
            state[...] = jnp.zeros_like(state)

        for c in range(cs):
            sl = slice(HEADS * c, HEADS * (c + 1))
            st = state[...]
            stb = st.astype(BF16)
            st_ref[sl] = stb
            vn = u_ref[sl] - lax.dot_general(w_ref[sl], stb, NN_B, preferred_element_type=F32)
            vnb = vn.astype(BF16)
            o = (lax.dot_general(qd_ref[sl], stb, NN_B, preferred_element_type=F32)
                 + lax.dot_general(a_ref[sl], vnb, NN_B, preferred_element_type=F32))
            state[...] = st * gl_ref[sl] + lax.dot_general(kd_ref[sl], vnb, TN_B, preferred_element_type=F32)
            for h in range(HEADS):
                o_ref[CHUNK * c:CHUNK * (c + 1), 128 * h:128 * (h + 1)] = o[h]

    b3 = lambda d: pl.BlockSpec((cs * HEADS, CHUNK, d), lambda i: (i, 0, 0))
    return _call(body, name="gdn_scan_fwd", grid=(n // cs,),
                 in_specs=[b3(128), b3(128), b3(128), b3(128), b3(CHUNK), pl.BlockSpec((cs * HEADS, 1, 128), lambda i: (i, 0, 0))],
                 out_specs=[_rows(cs * CHUNK, 1024), pl.BlockSpec((cs * HEADS, 128, 128), lambda i: (i, 0, 0))],
                 out_shape=[_sds((n * CHUNK, 1024), F32), _sds((nt, 128, 128), BF16)],
                 scratch=[pltpu.VMEM((HEADS, 128, 128), F32)], args=(u, w, qd, kd, amat, gl))


def _gdn_scan_bwd(w, qd, kd, amat, gl, do):
    nt = w.shape[0]
    n = nt // HEADS
    cs = min(GDN_SEQ_CHUNKS, n)
    ng = n // cs

    def body(w_ref, qd_ref, kd_ref, a_ref, gl_ref, do_ref, ds_ref, dstate, dos):
        @pl.when(pl.program_id(0) == 0)
        def _():
            dstate[...] = jnp.zeros_like(dstate)

        for c in reversed(range(cs)):
            sl = slice(HEADS * c, HEADS * (c + 1))
            for h in range(HEADS):
                dos[h] = do_ref[CHUNK * c:CHUNK * (c + 1), 128 * h:128 * (h + 1)].astype(BF16)
            dob = dos[...]
            dst = dstate[...]
            dstb = dst.astype(BF16)
            ds_ref[sl] = dstb
            dvn = (lax.dot_general(a_ref[sl], dob, TN_B, preferred_element_type=F32)
                   + lax.dot_general(kd_ref[sl], dstb, NN_B, preferred_element_type=F32))
            dstate[...] = (lax.dot_general(qd_ref[sl], dob, TN_B, preferred_element_type=F32) + gl_ref[sl] * dst
                           - lax.dot_general(w_ref[sl], dvn.astype(BF16), TN_B, preferred_element_type=F32))

    b3 = lambda d: pl.BlockSpec((cs * HEADS, CHUNK, d), lambda i: (ng - 1 - i, 0, 0))
    return _call(body, name="gdn_scan_bwd", grid=(ng,),
                 in_specs=[b3(128), b3(128), b3(128), b3(CHUNK), pl.BlockSpec((cs * HEADS, 1, 128), lambda i: (ng - 1 - i, 0, 0)),
                           pl.BlockSpec((cs * CHUNK, 1024), lambda i: (ng - 1 - i, 0))],
                 out_specs=pl.BlockSpec((cs * HEADS, 128, 128), lambda i: (ng - 1 - i, 0, 0)),
                 out_shape=_sds((nt, 128, 128), BF16),
                 scratch=[pltpu.VMEM((HEADS, 128, 128), F32), pltpu.VMEM((HEADS, CHUNK, 128), BF16)],
                 args=(w, qd, kd, amat, gl, do))


def _gdn_post_bwd(qkv, gbeta, u, w, tinv, states, dstates, do):
    s = qkv.shape[0]
    n = s // CHUNK
    cb = min(GDN_POST_CHUNKS, n)
    nb = cb * HEADS
    rows = cb * CHUNK

    def body(qkv_ref, gb_ref, u_ref, w_ref, t_ref, st_ref, ds_ref, do_ref, dqkv_ref, dgb_ref, qs, ks, vs, gs, bs, dos):
        gbv = gb_ref[...]
        gc = _dot_hi(_block_tri(rows), gbv)
        _gather_heads(qkv_ref, gc, gbv, qs, ks, vs, gs, bs, cb)
        for c in range(cb):
            for h in range(HEADS):
                dos[HEADS * c + h] = do_ref[CHUNK * c:CHUNK * (c + 1), 128 * h:128 * (h + 1)].astype(F32)
        q, k, v, b128 = qs[...], ks[...], vs[...], bs[...]
        c = _local_b(q, k, v, gs[...], b128)
        tinv, st, dst, dov = t_ref[...], st_ref[...], ds_ref[...], dos[...]
        wv = w_ref[...]
        uv = u_ref[...]
        vn = uv - _bdot(wv, st, NN_B)
        dvn = _bdot(c["amat"], dov, TN_B) + _bdot(c["kd"], dst, NN_B)
        damat = jnp.where(c["lower"], _bdot(dov, vn, NT_B), 0.0)
        dqd = _bdot(dov, st, NT_B)
        dkd = _bdot(vn, dst, NT_B)
        dw = -_bdot(dvn, st, NT_B)
        dgl = jnp.sum(jnp.sum(st.astype(F32) * dst.astype(F32), axis=1, keepdims=True), axis=-1, keepdims=True)
        dvb = _bdot(tinv, dvn, TN_B)
        dkbg = _bdot(tinv, dw, TN_B)
        dl = -(_bdot(dvb, uv, NT_B) + _bdot(dkbg, wv, NT_B))
        dl = jnp.where(c["strict"], dl, 0.0)
        dmm = dl * c["decay"]
        dqk = damat * c["decay"]
        wmat = dl * c["lmat"] + damat * c["amat"]
        wcol = jnp.sum(wmat, axis=1, keepdims=True)
        dgc = jnp.sum(wmat, axis=-1, keepdims=True) - jnp.sum(jnp.where(c["eye"], wcol, 0.0), axis=-1, keepdims=True)
        dkb = _bdot(dmm, k, NN_B) + dkbg * c["egc"]
        dk = _bdot(dmm, c["kb"], TN_B) + _bdot(dqk, q, TN_B) + dkd * c["ekd"] + dkb * b128
        dq = _bdot(dqk, k, NN_B) + dqd * c["egc"]
        tmp = jnp.sum(dkd * c["kd"], axis=-1, keepdims=True)
        dgc = (dgc + jnp.sum(dkbg * c["kbg"], axis=-1, keepdims=True) + jnp.sum(dqd * c["qd"], axis=-1, keepdims=True)
               - tmp)
        dglast = jnp.sum(tmp, axis=1, keepdims=True) + dgl * c["gl"][:, :, :1]
        row1 = lax.broadcasted_iota(jnp.int32, (1, CHUNK, 1), 1)
        dgc = dgc + jnp.where(row1 == CHUNK - 1, dglast, 0.0)
        db = jnp.sum(dkb * k, axis=-1, keepdims=True) + jnp.sum(dvb * v, axis=-1, keepdims=True)
        dv = dvb * b128
        lane = lax.broadcasted_iota(jnp.int32, (CHUNK, 128), 1)
        parts = []
        for cc in range(cb):
            acc = jnp.zeros((CHUNK, 128), F32)
            for h in range(HEADS):
                bi = HEADS * cc + h
                rs = slice(CHUNK * cc, CHUNK * (cc + 1))
                dqkv_ref[rs, 128 * h:128 * (h + 1)] = dq[bi].astype(BF16)
                dqkv_ref[rs, 1024 + 128 * h:1024 + 128 * (h + 1)] = dk[bi].astype(BF16)
                dqkv_ref[rs, 2048 + 128 * h:2048 + 128 * (h + 1)] = dv[bi].astype(BF16)
                acc = acc + jnp.where(lane == GA_LANE + h, dgc[bi], 0.0)
            parts.append(acc)
        dgc_all = jnp.concatenate(parts, axis=0)
        dg_all = _dot_hi(_block_tri(rows, transpose=True), dgc_all)
        for cc in range(cb):
            acc = dg_all[CHUNK * cc:CHUNK * (cc + 1)]
            for h in range(HEADS):
                acc = acc + jnp.where(lane == GB_LANE + h, db[HEADS * cc + h], 0.0)
            dgb_ref[CHUNK * cc:CHUNK * (cc + 1), :] = acc

    b3 = lambda d1, d2: pl.BlockSpec((nb, d1, d2), lambda i: (i, 0, 0))
    return _call(body, name="gdn_post_bwd", grid=(n // cb,),
                 in_specs=[_rows(rows, 3072), _rows(rows, 128), b3(CHUNK, 128), b3(CHUNK, 128), b3(CHUNK, CHUNK),
                           b3(128, 128), b3(128, 128), _rows(rows, 1024)],
                 out_specs=[_rows(rows, 3072), _rows(rows, 128)],
                 out_shape=[_sds((s, 3072), BF16), _sds((s, 128), F32)],
                 scratch=[pltpu.VMEM((nb, CHUNK, 128), F32)] * 6, sem=("parallel",),
                 args=(qkv, gbeta, u, w, tinv, states, dstates, do))


def _mix_fwd(o_mla, o_gdn, proj, out_gain):
    s = proj.shape[0]
    tm = min(TILES["row"], s)

    def body(om_ref, og_ref, mg_ref, gg_ref, g_ref, o_ref):
        o_ref[:, :1024] = (om_ref[...] * _silu(mg_ref[...].astype(F32))).astype(BF16)
        for h in range(HEADS):
            sl = slice(128 * h, 128 * (h + 1))
            og = og_ref[:, sl]
            on = og * _rms(og, HEAD_DIM) * g_ref[...]
            o_ref[:, 1024 + 128 * h:1024 + 128 * (h + 1)] = (on * _silu(gg_ref[:, sl].astype(F32))).astype(BF16)

    return _call(body, name="mix_fwd", grid=(s // tm,),
                 in_specs=[_rows(tm, 1024), _rows(tm, 1024), _rows(tm, 1024, 1), _rows(tm, 1024, 5), _full((1, 128))],
                 out_specs=_rows(tm, 2048), out_shape=_sds((s, 2048), BF16), args=(o_mla, o_gdn, proj, proj, out_gain))


def _mix_bwd(o_mla, o_gdn, proj, out_gain, dmixed):
    s = proj.shape[0]
    tm = min(TILES["row"], s)

    def body(om_ref, og_ref, mg_ref, gg_ref, g_ref, dm_ref, dg_ref, dom_ref, dog_ref, dmg_ref, dgg_ref, ag_ref,
             delta_ref):
        @pl.when(pl.program_id(0) == 0)
        def _():
            ag_ref[...] = jnp.zeros_like(ag_ref)

        mg = mg_ref[...].astype(F32)
        dm = dm_ref[...].astype(F32)
        om = om_ref[...]
        dom = (dm * _silu(mg)).astype(BF16)
        dom_ref[...] = dom
        dmg_ref[...] = (dm * om * _dsilu(mg)).astype(BF16)
        prod = dom.astype(F32) * om
        lane = lax.broadcasted_iota(jnp.int32, (tm, 128), 1)
        delta = jnp.zeros((tm, 128), F32)
        for h in range(HEADS):
            delta = delta + jnp.where(lane == h, jnp.sum(prod[:, 128 * h:128 * (h + 1)], axis=-1, keepdims=True), 0.0)
        delta_ref[...] = delta
        for h in range(HEADS):
            sl = slice(128 * h, 128 * (h + 1))
            og, gg, d = og_ref[:, sl], gg_ref[:, sl].astype(F32), dg_ref[:, sl].astype(F32)
            on = og * _rms(og, HEAD_DIM) * g_ref[...]
            dgg_ref[:, sl] = (d * on * _dsilu(gg)).astype(BF16)
            dx, gpart = _rms_bwd(og, g_ref[...], d * _silu(gg), HEAD_DIM)
            dog_ref[:, sl] = dx.astype(BF16)
            ag_ref[...] += _acc8(gpart)

    return _call(body, name="mix_bwd", grid=(s // tm,),
                 in_specs=[_rows(tm, 1024), _rows(tm, 1024), _rows(tm, 1024, 1), _rows(tm, 1024, 5), _full((1, 128)),
                           _rows(tm, 1024, 0), _rows(tm, 1024, 1)],
                 out_specs=[_rows(tm, 1024), _rows(tm, 1024), _rows(tm, 1024), _rows(tm, 1024), _full((8, 128)),
                            _rows(tm, 128)],
                 out_shape=[_sds((s, 1024), BF16), _sds((s, 1024), BF16), _sds((s, 1024), BF16), _sds((s, 1024), BF16),
                            _sds((8, 128), F32), _sds((s, 128), F32)],
                 args=(o_mla, o_gdn, proj, proj, out_gain, dmixed, dmixed))


def _out_fwd(mixed, w_out, x, target):
    s = x.shape[0]
    tm = min(TILES["mm"], s)
    tn = min(TILES["mm"], D_MODEL)

    def body(m_ref, w_ref, x_ref, t_ref, dy_ref, acc_ref):
        err = x_ref[...] + _dot(m_ref[...], w_ref[...]) - t_ref[...]
        dy_ref[...] = (err * (1.0 / D_MODEL)).astype(BF16)

        @pl.when(pl.program_id(1) == 0)
        def _():
            acc_ref[...] = jnp.zeros_like(acc_ref)

        acc_ref[...] += _acc8(err * err)

    return _call(body, name="out_fwd", grid=(D_MODEL // tn, s // tm),
                 in_specs=[pl.BlockSpec((tm, D_MODEL), lambda j, i: (i, 0)), pl.BlockSpec((D_MODEL, tn), lambda j, i: (0, j)),
                           pl.BlockSpec((tm, tn), lambda j, i: (i, j)), pl.BlockSpec((tm, tn), lambda j, i: (i, j))],
                 out_specs=[pl.BlockSpec((tm, tn), lambda j, i: (i, j)), pl.BlockSpec((8, tn), lambda j, i: (0, j))],
                 out_shape=[_sds((s, D_MODEL), BF16), _sds((8, D_MODEL), F32)], args=(mixed, w_out, x, target))


def _row_tile(r, c):
    if r % 8 != 0:
        return r
    t = 8
    while r % (2 * t) == 0 and 2 * t * c * 4 <= (1 << 20):
        t *= 2
    return t


def _sum_arrays(parts, name, also_bf16=False):
    r, c = parts[0].shape
    tr = _row_tile(r, c)
    n = len(parts)

    def body(*refs):
        acc = refs[0][...].astype(F32)
        for p_ref in refs[1:n]:
            acc = acc + p_ref[...].astype(F32)
        refs[n][...] = acc
        if also_bf16:
            refs[n + 1][...] = acc.astype(BF16)

    nout = 2 if also_bf16 else 1
    out = _call(body, name=name, grid=(r // tr,), in_specs=[_rows(tr, c)] * n, out_specs=[_rows(tr, c)] * nout,
                out_shape=[_sds((r, c), F32), _sds((r, c), BF16)][:nout], args=tuple(parts))
    return out if also_bf16 else out[0]


def _adamw(w, g, m, v, name):
    r, c = w.shape
    c1 = 1.0 - ADAM_B1 ** ADAM_STEP
    c2 = 1.0 - ADAM_B2 ** ADAM_STEP

    def body(w_ref, g_ref, m_ref, v_ref, d_ref, nm_ref, nv_ref):
        gv = g_ref[...]
        nm = ADAM_B1 * m_ref[...] + (1.0 - ADAM_B1) * gv
        nv = ADAM_B2 * v_ref[...] + (1.0 - ADAM_B2) * (gv * gv)
        nm_ref[...] = nm
        nv_ref[...] = nv
        d_ref[...] = -ADAM_LR * ((nm / c1) / (jnp.sqrt(nv / c2) + ADAM_EPS) + ADAM_WD * w_ref[...])

    if r % 8 == 0:
        tr = _row_tile(r, c)
        grid, spec = (r // tr,), _rows(tr, c)
    else:
        tc = c
        while tc % 256 == 0 and r * tc * 4 > (3 << 19):
            tc //= 2
        grid, spec = (c // tc,), pl.BlockSpec((r, tc), lambda i: (0, i))
    return _call(body, name=name, grid=grid, in_specs=[spec] * 4, out_specs=[spec] * 3,
                 out_shape=[_sds((r, c), F32)] * 3, args=(w, g, m, v))


ANY = pl.BlockSpec(memory_space=pl.ANY)
CHIP_FLIPS = ((1, 0), (0, 1), (1, 1))


def _comm_call(body, *, name, n_in, out_shape, scratch):
    def kfn(*refs):
        body(*refs)
    return pl.pallas_call(kfn, name=name, in_specs=[ANY] * n_in, out_specs=[ANY] * len(out_shape), out_shape=out_shape,
                          scratch_shapes=list(scratch),
                          compiler_params=pltpu.CompilerParams(has_side_effects=True))


def _all_gather_chips(shards):
    na = len(shards)

    def body(*refs):
        copies = _gather_copies(refs[:na], refs[na:2 * na], *refs[2 * na:])
        _gather_start(copies)
        _gather_finish(copies)

    out_shape = [_sds((4,) + a.shape, a.dtype) for a in shards]
    sem = pltpu.SemaphoreType.DMA((na, 3))
    got = _comm_call(body, name="all_gather_weights", n_in=na, out_shape=out_shape, scratch=[sem, sem, sem, sem])(*shards)
    return _place_own_blocks(got, shards)


def _gather_copies(ins, outs, send_sems, recv_sems, fwd_send, fwd_recv):
    x, y, c = lax.axis_index("x"), lax.axis_index("y"), lax.axis_index("c")
    my_k = 2 * x + y
    direct, forwards = [], []
    for a in range(len(ins)):
        rows = ins[a].shape[0]
        for r, (fx, fy) in enumerate(CHIP_FLIPS):
            px, py = x ^ fx, y ^ fy
            if rows % 32 == 0:
                mine = pl.ds(pl.multiple_of(c * (rows // 2), 16), rows // 2)
                other = pl.ds(pl.multiple_of((1 - c) * (rows // 2), 16), rows // 2)
                rc = pltpu.make_async_remote_copy(
                    src_ref=ins[a].at[mine], dst_ref=outs[a].at[my_k, mine], send_sem=send_sems.at[a, r],
                    recv_sem=recv_sems.at[a, r], device_id=(px, py, c), device_id_type=MESH)
                landed = outs[a].at[2 * px + py, mine]
                fw = pltpu.make_async_remote_copy(
                    src_ref=landed, dst_ref=landed, send_sem=fwd_send.at[a, r], recv_sem=fwd_recv.at[a, r],
                    device_id=(x, y, 1 - c), device_id_type=MESH)
                from_sib = outs[a].at[2 * px + py, other]
                fw_in = pltpu.make_async_remote_copy(
                    src_ref=from_sib, dst_ref=from_sib, send_sem=fwd_send.at[a, r], recv_sem=fwd_recv.at[a, r],
                    device_id=(x, y, 1 - c), device_id_type=MESH)
                forwards.append((rc, fw, fw_in))
            else:
                direct.append(pltpu.make_async_remote_copy(
                    src_ref=ins[a], dst_ref=outs[a].at[my_k], send_sem=send_sems.at[a, r],
                    recv_sem=recv_sems.at[a, r], device_id=(px, py, c), device_id_type=MESH))
    return forwards, direct


def _gather_start(copies):
    forwards, direct = copies
    for rc, _, _ in forwards:
        rc.start()
    for rc in direct:
        rc.start()


def _gather_finish(copies):
    forwards, direct = copies
    for rc, fw, _ in forwards:
        rc.wait_recv()
        fw.start()
    for rc, fw, fw_in in forwards:
        rc.wait_send()
        fw.wait_send()
        fw_in.wait_recv()
    for rc in direct:
        rc.wait()


def _place_own_blocks(got, shards):
    my_k = 2 * lax.axis_index("x") + lax.axis_index("y")
    return [lax.dynamic_update_index_in_dim(g, a, my_k, 0) for g, a in zip(got, shards)]


def _norm1_fwd_gather(x, gain, shards):
    s = x.shape[0]
    tm = min(TILES["row"], s)
    ni = s // tm
    na = len(shards)

    def kfn(x_ref, g_ref, *rest):
        o_ref = rest[na]
        sems = rest[2 * na + 1:]
        i = pl.program_id(0)

        @pl.when(i == 0)
        def _():
            _gather_start(_gather_copies(rest[:na], rest[na + 1:2 * na + 1], *sems))

        xv = x_ref[...]
        r = lax.rsqrt(jnp.mean(xv * xv, axis=-1, keepdims=True) + NORM_EPS)
        o_ref[...] = (xv * r * g_ref[...]).astype(BF16)

        @pl.when(i == ni - 1)
        def _():
            _gather_finish(_gather_copies(rest[:na], rest[na + 1:2 * na + 1], *sems))

    sem = pltpu.SemaphoreType.DMA((na, 3))
    out = pl.pallas_call(
        kfn, name="norm1_fwd_gather", grid=(ni,),
        in_specs=[_rows(tm, D_MODEL), _full((1, D_MODEL))] + [ANY] * na,
        out_specs=[_rows(tm, D_MODEL)] + [ANY] * na,
        out_shape=[_sds((s, D_MODEL), BF16)] + [_sds((4,) + s_a.shape, s_a.dtype) for s_a in shards],
        scratch_shapes=[sem, sem, sem, sem],
        compiler_params=pltpu.CompilerParams(dimension_semantics=("arbitrary",), vmem_limit_bytes=VMEM_LIMIT_V7X,
                                             has_side_effects=True))(x, gain, *shards)
    return out[0], _place_own_blocks(list(out[1:]), shards)


def _matmul_nn_gather(a, b, shards, *, name, tm, tn, out_dtype):
    m, kdim = a.shape
    n = b.shape[1]
    ni, nj = m // tm, n // tn
    na = len(shards)

    def body(a_ref, b_ref, *rest):
        o_ref = rest[na]
        sems = rest[2 * na + 1:]
        i, j = pl.program_id(0), pl.program_id(1)

        @pl.when((i == 0) & (j == 0))
        def _():
            _gather_start(_gather_copies(rest[:na], rest[na + 1:2 * na + 1], *sems))

        o_ref[...] = _dot(a_ref[...], b_ref[...]).astype(out_dtype)

        @pl.when((i == ni - 1) & (j == nj - 1))
        def _():
            _gather_finish(_gather_copies(rest[:na], rest[na + 1:2 * na + 1], *sems))

    def kfn(*refs):
        body(*refs)
    sem = pltpu.SemaphoreType.DMA((na, 3))
    out = pl.pallas_call(
        kfn, name=name, grid=(ni, nj),
        in_specs=[pl.BlockSpec((tm, kdim), lambda i, j: (i, 0)), pl.BlockSpec((kdim, tn), lambda i, j: (0, j))] + [ANY] * na,
        out_specs=[pl.BlockSpec((tm, tn), lambda i, j: (i, j))] + [ANY] * na,
        out_shape=[_sds((m, n), out_dtype)] + [_sds((4,) + s_a.shape, s_a.dtype) for s_a in shards],
        scratch_shapes=[sem, sem, sem, sem],
        compiler_params=pltpu.CompilerParams(dimension_semantics=("arbitrary", "arbitrary"),
                                             vmem_limit_bytes=VMEM_LIMIT_V7X, has_side_effects=True))(a, b, *shards)
    return out[0], _place_own_blocks(list(out[1:]), shards)


def _all_reduce_small(vec):
    r = vec.shape[0]

    def body(v_ref, o_ref, gath, send_sems, recv_sems):
        x, y, c = lax.axis_index("x"), lax.axis_index("y"), lax.axis_index("c")
        me = 4 * x + 2 * y + c
        gath[me] = v_ref[...]
        copies = []
        for rel in range(1, 8):
            fx, fy, fc = (rel >> 2) & 1, (rel >> 1) & 1, rel & 1
            rc = pltpu.make_async_remote_copy(
                src_ref=v_ref, dst_ref=gath.at[me], send_sem=send_sems.at[rel - 1], recv_sem=recv_sems.at[rel - 1],
                device_id=(x ^ fx, y ^ fy, c ^ fc), device_id_type=MESH)
            rc.start()
            copies.append(rc)
        for rc in copies:
            rc.wait()
        acc = gath[0]
        for d in range(1, 8):
            acc = acc + gath[d]
        o_ref[...] = acc

    def kfn(*refs):
        body(*refs)
    vm = pl.BlockSpec(memory_space=pltpu.VMEM)
    return pl.pallas_call(kfn, name="all_reduce_small", in_specs=[vm], out_specs=vm, out_shape=_sds((r, 128), F32),
                          scratch_shapes=[pltpu.VMEM((8, r, 128), F32), pltpu.SemaphoreType.DMA((7,)),
                                          pltpu.SemaphoreType.DMA((7,))],
                          compiler_params=pltpu.CompilerParams(has_side_effects=True))(vec)


def _exchange_halves(arrs):
    na = len(arrs)

    def body(*refs):
        ins, outs = refs[:na], refs[na:2 * na]
        send_sems, recv_sems = refs[2 * na:]
        x, y, c = lax.axis_index("x"), lax.axis_index("y"), lax.axis_index("c")
        copies = []
        for a in range(na):
            half = ins[a].shape[1] // 2
            src = ins[a].at[:, pl.ds(pl.multiple_of((1 - c) * half, 16), half), :]
            rc = pltpu.make_async_remote_copy(src_ref=src, dst_ref=outs[a], send_sem=send_sems.at[a],
                                              recv_sem=recv_sems.at[a], device_id=(x, y, 1 - c), device_id_type=MESH)
            rc.start()
            copies.append(rc)
        for rc in copies:
            rc.wait()

    out_shape = [_sds((4, a.shape[1] // 2, a.shape[2]), a.dtype) for a in arrs]
    return _comm_call(body, name="rs_pair_exchange", n_in=na, out_shape=out_shape,
                      scratch=[pltpu.SemaphoreType.DMA((na,)), pltpu.SemaphoreType.DMA((na,))])(*arrs)


def _scatter_to_chips(arrs):
    na = len(arrs)

    def body(*refs):
        ins, outs = refs[:na], refs[na:2 * na]
        send_sems, recv_sems = refs[2 * na:]
        x, y, c = lax.axis_index("x"), lax.axis_index("y"), lax.axis_index("c")
        copies = []
        for a in range(na):
            for r, (fx, fy) in enumerate(CHIP_FLIPS):
                px, py = x ^ fx, y ^ fy
                rc = pltpu.make_async_remote_copy(
                    src_ref=ins[a].at[2 * px + py], dst_ref=outs[a].at[r], send_sem=send_sems.at[a, r],
                    recv_sem=recv_sems.at[a, r], device_id=(px, py, c), device_id_type=MESH)
                rc.start()
                copies.append(rc)
        for rc in copies:
            rc.wait()

    out_shape = [_sds((3,) + a.shape[1:], a.dtype) for a in arrs]
    return _comm_call(body, name="rs_chip_scatter", n_in=na, out_shape=out_shape,
                      scratch=[pltpu.SemaphoreType.DMA((na, 3)), pltpu.SemaphoreType.DMA((na, 3))])(*arrs)


def _sum_into_half(parts, name):
    r2, c = parts[0].shape
    tr = _row_tile(r2, c)
    nb = r2 // tr
    n = len(parts)

    def kfn(c_ref, *refs):
        acc = refs[0][...].astype(F32)
        for p_ref in refs[1:n]:
            acc = acc + p_ref[...].astype(F32)
        refs[n][...] = acc

    spec = pltpu.PrefetchScalarGridSpec(
        num_scalar_prefetch=1, grid=(nb,), in_specs=[pl.BlockSpec((tr, c), lambda i, cr: (i, 0))] * n,
        out_specs=pl.BlockSpec((tr, c), lambda i, cr: (cr[0] * nb + i, 0)))
    core = lax.axis_index("c").astype(jnp.int32).reshape(1)
    return pl.pallas_call(kfn, name=name, grid_spec=spec, out_shape=_sds((2 * r2, c), F32),
                          compiler_params=pltpu.CompilerParams(dimension_semantics=("arbitrary",),
                                                               vmem_limit_bytes=VMEM_LIMIT_V7X))(core, *parts)


def _join_in_place(arrs):
    na = len(arrs)

    def body(*refs):
        outs = refs[na:2 * na]
        send_sems, recv_sems = refs[2 * na:]
        x, y, c = lax.axis_index("x"), lax.axis_index("y"), lax.axis_index("c")
        copies = []
        for a in range(na):
            half = outs[a].shape[0] // 2
            mine = outs[a].at[pl.ds(pl.multiple_of(c * half, 8), half), :]
            rc = pltpu.make_async_remote_copy(src_ref=mine, dst_ref=mine, send_sem=send_sems.at[a],
                                              recv_sem=recv_sems.at[a], device_id=(x, y, 1 - c), device_id_type=MESH)
            rc.start()
            copies.append(rc)
        for rc in copies:
            rc.wait()

    def kfn(*refs):
        body(*refs)
    return pl.pallas_call(kfn, name="rs_pair_join", in_specs=[ANY] * na, out_specs=[ANY] * na,
                          out_shape=[_sds(a.shape, F32) for a in arrs],
                          input_output_aliases={a: a for a in range(na)},
                          scratch_shapes=[pltpu.SemaphoreType.DMA((na,)), pltpu.SemaphoreType.DMA((na,))],
                          compiler_params=pltpu.CompilerParams(has_side_effects=True))(*arrs)


def _pair_sum(g, o, name):
    _, r, c = g.shape
    half = r // 2
    tr = _row_tile(half, c)
    nb = half // tr

    def kfn(c_ref, g_ref, o_ref, s32_ref, s16_ref):
        acc = g_ref[...] + o_ref[...].astype(F32)
        s32_ref[...] = acc
        s16_ref[...] = acc.astype(BF16)

    blk = lambda imap: pl.BlockSpec((None, tr, c), imap)
    same = lambda k, i, cr: (k, i, 0)
    spec = pltpu.PrefetchScalarGridSpec(
        num_scalar_prefetch=1, grid=(4, nb), in_specs=[blk(lambda k, i, cr: (k, cr[0] * nb + i, 0)), blk(same)],
        out_specs=[blk(same), blk(same)])
    core = lax.axis_index("c").astype(jnp.int32).reshape(1)
    return pl.pallas_call(kfn, name=name, grid_spec=spec, out_shape=[_sds((4, half, c), F32), _sds((4, half, c), BF16)],
                          compiler_params=pltpu.CompilerParams(dimension_semantics=("arbitrary", "arbitrary"),
                                                               vmem_limit_bytes=VMEM_LIMIT_V7X))(core, g, o)


def _rs_pair_stage(grads):
    got = _exchange_halves([g.astype(BF16) for g in grads])
    sums = [_pair_sum(g, o, f"rs_pair_sum_{a}") for a, (g, o) in enumerate(zip(grads, got))]
    return [s32 for s32, _ in sums], [s16 for _, s16 in sums]


def _rs_chip_stage(pair, recv):
    k_me = 2 * lax.axis_index("x") + lax.axis_index("y")
    halves = []
    for a, (p, rv) in enumerate(zip(pair, recv)):
        own = lax.dynamic_index_in_dim(p, k_me, 0, keepdims=False)
        halves.append(_sum_into_half([own, rv[0], rv[1], rv[2]], f"rs_chip_sum_{a}"))
    return _join_in_place(halves)


def _reduce_scatter(grads):
    pair, pair_bf16 = _rs_pair_stage(grads)
    return _rs_chip_stage(pair, _scatter_to_chips(pair_bf16))


def _matmul_nt_scatter(a, b, send, *, name, tm, tn, out_dtype):
    m, kdim = a.shape
    n = b.shape[0]
    ni, nj = m // tm, n // tn
    na = len(send)

    def body(a_ref, b_ref, *rest):
        send_refs, o_ref, recv_refs = rest[:na], rest[na], rest[na + 1:2 * na + 1]
        send_sems, recv_sems = rest[2 * na + 1:]
        i, j = pl.program_id(0), pl.program_id(1)

        def copies():
            x, y, c = lax.axis_index("x"), lax.axis_index("y"), lax.axis_index("c")
            out = []
            for s_i in range(na):
                for r, (fx, fy) in enumerate(CHIP_FLIPS):
                    px, py = x ^ fx, y ^ fy
                    out.append(pltpu.make_async_remote_copy(
                        src_ref=send_refs[s_i].at[2 * px + py], dst_ref=recv_refs[s_i].at[r],
                        send_sem=send_sems.at[s_i, r], recv_sem=recv_sems.at[s_i, r], device_id=(px, py, c),
                        device_id_type=MESH))
            return out

        @pl.when((i == 0) & (j == 0))
        def _():
            for cp in copies():
                cp.start()

        o_ref[...] = _dot(a_ref[...], b_ref[...], NT).astype(out_dtype)

        @pl.when((i == ni - 1) & (j == nj - 1))
        def _():
            for cp in copies():
                cp.wait()

    def kfn(*refs):
        body(*refs)
    sem = pltpu.SemaphoreType.DMA((na, 3))
    out = pl.pallas_call(
        kfn, name=name, grid=(ni, nj),
        in_specs=[pl.BlockSpec((tm, kdim), lambda i, j: (i, 0)), pl.BlockSpec((tn, kdim), lambda i, j: (j, 0))] + [ANY] * na,
        out_specs=[pl.BlockSpec((tm, tn), lambda i, j: (i, j))] + [ANY] * na,
        out_shape=[_sds((m, n), out_dtype)] + [_sds((3,) + s_a.shape[1:], s_a.dtype) for s_a in send],
        scratch_shapes=[sem, sem],
        compiler_params=pltpu.CompilerParams(dimension_semantics=("arbitrary", "arbitrary"),
                                             vmem_limit_bytes=VMEM_LIMIT_V7X, has_side_effects=True))(a, b, *send)
    return out[0], list(out[1:])


def _pad_w_in(w):
    z = jnp.zeros((w.shape[0], 1024 - 848), w.dtype)
    return jnp.concatenate([w[:, 0:832], w[:, 4928:4944], z, w[:, 832:4928], w[:, 4944:5968]], axis=1)


def _unpad_w_in(g):
    return jnp.concatenate([g[:, 0:832], g[:, 1024:5120], g[:, 832:848], g[:, 5120:6144]], axis=1)


W_IN_SHARD = W_IN_COLS // 4
W_IN_RUNS = ((0, 832, 0), (832, 4928, 1024), (4928, 4944, 832), (4944, 5968, 5120))


def _w_in_grad_blocks(p):
    def orig_cols(lo, hi):
        parts = [p[:, pa + max(lo, a) - a:pa + min(hi, b) - a] for a, b, pa in W_IN_RUNS if max(lo, a) < min(hi, b)]
        return parts[0] if len(parts) == 1 else jnp.concatenate(parts, axis=1)
    return jnp.stack([orig_cols(W_IN_SHARD * k, W_IN_SHARD * (k + 1)) for k in range(4)])


def _pad_w_in_blocks(g):
    def orig_cols(lo, hi):
        return [g[k][:, max(lo, W_IN_SHARD * k) - W_IN_SHARD * k:min(hi, W_IN_SHARD * (k + 1)) - W_IN_SHARD * k]
                for k in range(4) if max(lo, W_IN_SHARD * k) < min(hi, W_IN_SHARD * (k + 1))]
    z = jnp.zeros((g.shape[1], 1024 - 848), g.dtype)
    return jnp.concatenate(orig_cols(0, 832) + orig_cols(4928, 4944) + [z] + orig_cols(832, 4928) + orig_cols(4944, 5968),
                           axis=1)


def _pad_heads(w):
    r = w.shape[0]
    return jnp.pad(w.reshape(r, HEADS, QK_DIM), ((0, 0), (0, 0), (0, QK_PAD - QK_DIM))).reshape(r, HEADS * QK_PAD)


def _unpad_heads(w):
    r = w.shape[0]
    return w.reshape(r, HEADS, QK_PAD)[:, :, :QK_DIM].reshape(r, HEADS * QK_DIM)


def _cols_to_blocks(w):
    r = w.shape[0]
    return w.reshape(r, 4, -1).transpose(1, 0, 2)


def _blocks_to_cols(w):
    return w.transpose(1, 0, 2).reshape(w.shape[1], -1)


SMALL_ROWS = {"norm_gain": (0, 2048), "mla_q_a_gain": (16, 512), "mla_kv_a_gain": (20, 256),
              "mla_q_norm_gain": (22, 192), "mla_k_norm_gain": (24, 192), "gdn_a_log": (26, 8),
              "gdn_dt_bias": (27, 8), "gdn_out_norm_gain": (28, 128)}
LOSS_ROW = 29
SMALL_PACK_ROWS = 32
CONV_ROW = 32


def _pack_small(vals, loss=None):
    rows = []
    at = 0
    for name, (row, size) in SMALL_ROWS.items():
        assert row == at
        nr = -(-size // 128)
        rows.append(jnp.pad(vals[name].reshape(-1).astype(F32), (0, nr * 128 - size)).reshape(nr, 128))
        at += nr
    assert at == LOSS_ROW
    if loss is not None:
        rows.append(jnp.pad(loss.reshape(1, 1), ((0, 0), (0, 127))))
        at += 1
    rows.append(jnp.zeros((SMALL_PACK_ROWS - at, 128), F32))
    return jnp.concatenate(rows, axis=0)


def _unpack_small(pack, name):
    row, size = SMALL_ROWS[name]
    nr = -(-size // 128)
    return pack[row:row + nr].reshape(-1)[:size].reshape(1, size)


def _local_step(x, positions, target, norm_gain, w_in_p, q_a_gain, kv_a_gain, w_uq_p, w_ukv, q_norm_gain,
                k_norm_gain, conv_w, a_log, dt_bias, out_gain, w_out, scatter_hook=None, late_weights=None,
                first_weights=None):
    half = HALF_ROPE
    inv_freq = jnp.power(10000.0, -jnp.arange(half, dtype=F32) / half)
    ang = positions.astype(F32)[:, None] * inv_freq
    cos, sin = jnp.cos(ang), jnp.sin(ang)
    zpad = jnp.zeros((x.shape[0], 64), F32)
    cs = jnp.concatenate([cos, cos, zpad], axis=1)
    sn = jnp.concatenate([-sin, sin, zpad], axis=1)
    gq = jnp.pad(q_norm_gain.reshape(1, QK_DIM), ((0, 0), (0, QK_PAD - QK_DIM)))
    gk = jnp.pad(k_norm_gain.reshape(1, QK_DIM), ((0, 0), (0, QK_PAD - QK_DIM)))
    lane_pad = ((0, 0), (GA_LANE, 128 - GA_LANE - HEADS))
    alog128 = jnp.pad(a_log.reshape(1, HEADS), lane_pad)
    dtb128 = jnp.pad(dt_bias.reshape(1, HEADS), lane_pad)
    ng, qag, kvag, og = (norm_gain.reshape(1, -1), q_a_gain.reshape(1, -1), kv_a_gain.reshape(1, -1),
                         out_gain.reshape(1, -1))

    if first_weights is None:
        xn = _norm1_fwd(x, ng)
    else:
        shards, assemble = first_weights
        xn, gathered = _norm1_fwd_gather(x, ng, shards)
        w_in_p, conv_w = assemble(gathered)
    misc = _matmul(xn, w_in_p[:, 768:896], mode="nn", out_dtype=F32, name="in_proj_misc")
    if late_weights is None:
        proj = _matmul(xn, w_in_p, mode="nn", out_dtype=BF16, name="in_proj")
    else:
        shards, assemble = late_weights
        proj, gathered = _matmul_nn_gather(xn, w_in_p, shards, name="in_proj_gather", tm=TILES["mm"], tn=2 * TILES["mm"],
                                           out_dtype=BF16)
        w_uq_p, w_ukv, w_out = assemble(gathered)
    cqn, ckvn = _mla_a_norm(proj, qag, kvag)
    q_pre = _matmul(cqn, w_uq_p, mode="nn", out_dtype=BF16, name="q_up")
    kv_pre = _matmul(ckvn, w_ukv, mode="nn", out_dtype=BF16, name="kv_up")
    q, k, v = _mla_post_fwd(q_pre, kv_pre, misc, cs, sn, gq, gk)
    o_mla, lse = _attn_fwd(q, k, v)
    qkv = _gdn_conv_fwd(proj, conv_w)
    gbeta = _gdn_gates_fwd(misc, alog128, dtb128)
    g_u, g_w, g_qd, g_kd, g_a, g_t, g_gl = _gdn_pre(qkv, gbeta)
    o_gdn, states = _gdn_scan_fwd(g_u, g_w, g_qd, g_kd, g_a, g_gl)
    mixed = _mix_fwd(o_mla, o_gdn, proj, og)
    dy, sq = _out_fwd(mixed, w_out, x, target)

    dmixed = _matmul(dy, w_out, mode="nt", out_dtype=BF16, name="d_mixed")
    d_w_out = _matmul(mixed, dy, mode="tn", out_dtype=F32, name="d_w_out", tk=4096)
    do_mla, do_gdn, dmg, dgg, d_out_gain, delta128 = _mix_bwd(o_mla, o_gdn, proj, og, dmixed)
    s_len = x.shape[0]
    delta_rows = delta128[:, :HEADS].T.reshape(HEADS, 1, s_len)
    dq, dk, dv = _attn_bwd(q, k, v, lse.reshape(HEADS, 1, s_len), delta_rows, do_mla)
    dq_pre, dkv_pre, dkr, d_gq, d_gk = _mla_post_bwd(q_pre, kv_pre, misc, cs, sn, gq, gk, dq, dk, dv)
    d_w_uq_p = _matmul(cqn, dq_pre, mode="tn", out_dtype=F32, name="d_w_uq", tk=1024)
    d_w_ukv = _matmul(ckvn, dkv_pre, mode="tn", out_dtype=F32, name="d_w_ukv", tk=1024)
    dcqn = _matmul(dq_pre, w_uq_p, mode="nt", out_dtype=F32, name="d_cqn")
    dckvn = _matmul(dkv_pre, w_ukv, mode="nt", out_dtype=F32, name="d_ckvn")
    dcq, dckv, d_qag, d_kvag = _mla_a_norm_bwd(proj, qag, kvag, dcqn, dckvn)
    dstates = _gdn_scan_bwd(g_w, g_qd, g_kd, g_a, g_gl, do_gdn)
    dqkv, dgbeta = _gdn_post_bwd(qkv, gbeta, g_u, g_w, g_t, states, dstates, do_gdn)
    dz = _gdn_conv_bwd_a(proj, conv_w, dqkv)
    dgx, d_conv = _gdn_conv_bwd_b(proj, conv_w, dz)
    dmisc, d_alog, d_dtb = _gdn_gates_bwd(misc, alog128, dtb128, gbeta, dgbeta, dkr)
    dproj = jnp.concatenate([dcq, dckv, dmisc, jnp.zeros((x.shape[0], 128), BF16), dmg, dgx, dgg], axis=1)
    d_w_in_p = _matmul(xn, dproj, mode="tn", out_dtype=F32, name="d_w_in", tk=4096)
    big = {"w_in": d_w_in_p, "w_uq": d_w_uq_p, "w_ukv": d_w_ukv, "w_out": d_w_out, "gdn_conv_w": d_conv}
    if scatter_hook is None:
        dxn, received = _matmul(dproj, w_in_p, mode="nt", out_dtype=BF16, name="d_xn", tm=512, tn=512), None
    else:
        dxn, received = _matmul_nt_scatter(dproj, w_in_p, scatter_hook(big), name="d_xn_scatter", tm=1024, tn=512,
                                           out_dtype=BF16)
    grad_x, d_ng = _norm1_bwd(x, ng, dxn, dy)

    small = {"norm_gain": d_ng.sum(0), "mla_q_a_gain": d_qag.sum(0), "mla_kv_a_gain": d_kvag.sum(0),
             "mla_q_norm_gain": d_gq.sum(0)[:QK_DIM], "mla_k_norm_gain": d_gk.sum(0)[:QK_DIM],
             "gdn_a_log": d_alog.sum(0)[GA_LANE:GA_LANE + HEADS], "gdn_dt_bias": d_dtb.sum(0)[GA_LANE:GA_LANE + HEADS],
             "gdn_out_norm_gain": d_out_gain.sum(0)}
    return sq, grad_x, small, big, received


WEIGHTS = ["norm_gain", "w_in", "mla_q_a_gain", "mla_kv_a_gain", "w_uq", "w_ukv", "mla_q_norm_gain", "mla_k_norm_gain",
           "gdn_conv_w", "gdn_a_log", "gdn_dt_bias", "gdn_out_norm_gain", "w_out"]
BIG = ["w_in", "w_uq", "w_ukv", "w_out"]


def kernel(x, positions, norm_gain, w_in, mla_q_a_gain, mla_kv_a_gain, w_uq, w_ukv, mla_q_norm_gain, mla_k_norm_gain, gdn_conv_w, gdn_a_log, gdn_dt_bias, gdn_out_norm_gain, w_out, loss_target, m_norm_gain, m_w_in, m_mla_q_a_gain, m_mla_kv_a_gain, m_w_uq, m_w_ukv, m_mla_q_norm_gain, m_mla_k_norm_gain, m_gdn_conv_w, m_gdn_a_log, m_gdn_dt_bias, m_gdn_out_norm_gain, m_w_out, v_norm_gain, v_w_in, v_mla_q_a_gain, v_mla_kv_a_gain, v_w_uq, v_w_ukv, v_mla_q_norm_gain, v_mla_k_norm_gain, v_gdn_conv_w, v_gdn_a_log, v_gdn_dt_bias, v_gdn_out_norm_gain, v_w_out):
    w = dict(norm_gain=norm_gain, w_in=w_in, mla_q_a_gain=mla_q_a_gain, mla_kv_a_gain=mla_kv_a_gain, w_uq=w_uq,
             w_ukv=w_ukv, mla_q_norm_gain=mla_q_norm_gain, mla_k_norm_gain=mla_k_norm_gain, gdn_conv_w=gdn_conv_w,
             gdn_a_log=gdn_a_log, gdn_dt_bias=gdn_dt_bias, gdn_out_norm_gain=gdn_out_norm_gain, w_out=w_out)
    m = dict(norm_gain=m_norm_gain, w_in=m_w_in, mla_q_a_gain=m_mla_q_a_gain, mla_kv_a_gain=m_mla_kv_a_gain,
             w_uq=m_w_uq, w_ukv=m_w_ukv, mla_q_norm_gain=m_mla_q_norm_gain, mla_k_norm_gain=m_mla_k_norm_gain,
             gdn_conv_w=m_gdn_conv_w, gdn_a_log=m_gdn_a_log, gdn_dt_bias=m_gdn_dt_bias,
             gdn_out_norm_gain=m_gdn_out_norm_gain, w_out=m_w_out)
    v = dict(norm_gain=v_norm_gain, w_in=v_w_in, mla_q_a_gain=v_mla_q_a_gain, mla_kv_a_gain=v_mla_kv_a_gain,
             w_uq=v_w_uq, w_ukv=v_w_ukv, mla_q_norm_gain=v_mla_q_norm_gain, mla_k_norm_gain=v_mla_k_norm_gain,
             gdn_conv_w=v_gdn_conv_w, gdn_a_log=v_gdn_a_log, gdn_dt_bias=v_gdn_dt_bias,
             gdn_out_norm_gain=v_gdn_out_norm_gain, w_out=v_w_out)
    k_me = 2 * lax.axis_index("x") + lax.axis_index("y")

    first_weights = ([w_in[0].astype(BF16), gdn_conv_w[0]], lambda g: (_pad_w_in_blocks(g[0]), _blocks_to_cols(g[1])))
    late_weights = ([w_uq[0].astype(BF16), w_ukv[0].astype(BF16), w_out[0].astype(BF16)],
                    lambda g: (_pad_heads(_blocks_to_cols(g[0])), _blocks_to_cols(g[1]), g[2].reshape(D_MODEL, D_MODEL)))

    pair_sums = []

    def scatter_hook(big):
        pair, pair_bf16 = _rs_pair_stage([
            _w_in_grad_blocks(big["w_in"]), _cols_to_blocks(_unpad_heads(big["w_uq"])),
            _cols_to_blocks(big["w_ukv"]), big["w_out"].reshape(4, 512, D_MODEL)])
        pair_sums.extend(pair)
        return pair_bf16

    sq, grad_x, small, big, received = _local_step(
        x[0], positions[0], loss_target[0], norm_gain, None, mla_q_a_gain, mla_kv_a_gain, None, None,
        mla_q_norm_gain, mla_k_norm_gain, None, gdn_a_log, gdn_dt_bias, gdn_out_norm_gain, None, scatter_hook,
        late_weights, first_weights)

    loss_local = (0.5 / D_MODEL) * jnp.sum(sq)
    pack = jnp.concatenate([_pack_small(small, loss_local), big["gdn_conv_w"].reshape(96, 128)], axis=0)
    tot = _all_reduce_small(pack)
    loss = tot[LOSS_ROW, 0]
    conv_grad = lax.dynamic_slice_in_dim(tot[CONV_ROW:].reshape(4, 3072), k_me * 768, 768, axis=1)

    shard_grads = _rs_chip_stage(pair_sums, received)

    grads = {n: _unpack_small(tot, n) for n in SMALL_ROWS}
    grads["gdn_conv_w"] = conv_grad[None]
    for n, g in zip(BIG, shard_grads):
        grads[n] = g[None]

    delta, new_m, new_v = {}, {}, {}
    sw = _pack_small({n: w[n] for n in SMALL_ROWS})
    sm = _pack_small({n: m[n] for n in SMALL_ROWS})
    sv = _pack_small({n: v[n] for n in SMALL_ROWS})
    sd, snm, snv = _adamw(sw, tot[:SMALL_PACK_ROWS], sm, sv, "adamw_small")
    for n in SMALL_ROWS:
        delta[n], new_m[n], new_v[n] = _unpack_small(sd, n), _unpack_small(snm, n), _unpack_small(snv, n)
    for n in BIG + ["gdn_conv_w"]:
        if n == "w_in":
            d, nm, nv = _adamw(w[n][0].T, grads[n][0].T, m[n][0].T, v[n][0].T, f"adamw_{n}")
            delta[n], new_m[n], new_v[n] = d.T[None], nm.T[None], nv.T[None]
        else:
            d, nm, nv = _adamw(w[n][0], grads[n][0], m[n][0], v[n][0], f"adamw_{n}")
            delta[n], new_m[n], new_v[n] = d[None], nm[None], nv[None]

    return (loss, grad_x[None], *[grads[n] for n in WEIGHTS], *[delta[n] for n in WEIGHTS],
            *[new_m[n] for n in WEIGHTS], *[new_v[n] for n in WEIGHTS])
```

```python
import functools
import math

import jax
import jax.numpy as jnp
from jax import lax
from jax.experimental import pallas as pl
from jax.experimental.pallas import tpu as pltpu

F32 = jnp.float32
BF16 = jnp.bfloat16
MESH = pl.DeviceIdType.MESH

D_MODEL = 2048
HEADS = 8
HEAD_DIM = 128
QK_DIM = 192
QK_PAD = 256
HALF_ROPE = 32
CHUNK = 64
NORM_EPS = 1e-6
W_IN_COLS = 5968
W_IN_PAD = 6144
GA_LANE = 64
GB_LANE = 72
ADAM_LR, ADAM_B1, ADAM_B2, ADAM_EPS, ADAM_WD, ADAM_STEP = 0.001, 0.9, 0.999, 1e-08, 0.01, 10
VMEM_LIMIT_V7X = 52 * 1024 * 1024
HI = lax.Precision.HIGHEST
NN = (((1,), (0,)), ((), ()))
NT = (((1,), (1,)), ((), ()))
TN = (((0,), (0,)), ((), ()))

TILES = {"row": 512, "attn": 2048, "mm": 1024}


def _call(body, *, name, grid, in_specs, out_specs, out_shape, args, scratch=(), sem=None):
    def kfn(*refs):
        body(*refs)
    if sem is None:
        sem = ("arbitrary",) * len(grid)
    return pl.pallas_call(
        kfn, name=name, grid=grid, in_specs=in_specs, out_specs=out_specs, out_shape=out_shape,
        scratch_shapes=list(scratch),
        compiler_params=pltpu.CompilerParams(dimension_semantics=sem, vmem_limit_bytes=VMEM_LIMIT_V7X),
    )(*args)


def _rows(tm, w, cb=0):
    return pl.BlockSpec((tm, w), lambda i: (i, cb))


def _full(shape):
    n = len(shape)
    return pl.BlockSpec(shape, lambda *_: (0,) * n)


def _sds(shape, dtype):
    return jax.ShapeDtypeStruct(shape, dtype)


def _acc8(x):
    tm, c = x.shape
    return jnp.sum(x.reshape(tm // 8, 8, c), axis=0)


def _sigmoid(x):
    return 1.0 / (1.0 + jnp.exp(-x))


def _silu(x):
    return x * _sigmoid(x)


def _dsilu(x):
    s = _sigmoid(x)
    return s * (1.0 + x * (1.0 - s))


def _dot(a, b, dims=NN):
    return lax.dot_general(a.astype(BF16), b.astype(BF16), dims, preferred_element_type=F32)


def _dot_hi(a, b, dims=NN):
    return lax.dot_general(a, b, dims, precision=HI, preferred_element_type=F32)


def _matmul(a, b, *, mode, out_dtype, name, tm=None, tn=None, tk=None):
    if mode == "tn":
        kdim, m = a.shape
    else:
        m, kdim = a.shape
    n = b.shape[0] if mode == "nt" else b.shape[1]
    tm = min(tm or TILES["mm"], m)
    tn = min(tn or TILES["mm"], n)
    tk = min(tk or kdim, kdim)
    nk = kdim // tk
    dims = {"nn": NN, "nt": NT, "tn": TN}[mode]
    if mode == "tn":
        a_spec = pl.BlockSpec((tk, tm), lambda i, j, k: (k, i))
    else:
        a_spec = pl.BlockSpec((tm, tk), lambda i, j, k: (i, k))
    if mode == "nt":
        b_spec = pl.BlockSpec((tn, tk), lambda i, j, k: (j, k))
    else:
        b_spec = pl.BlockSpec((tk, tn), lambda i, j, k: (k, j))

    def body(a_ref, b_ref, o_ref):
        r = _dot(a_ref[...], b_ref[...], dims)
        if nk == 1:
            o_ref[...] = r.astype(o_ref.dtype)
        else:
            k = pl.program_id(2)

            @pl.when(k == 0)
            def _():
                o_ref[...] = r

            @pl.when(k > 0)
            def _():
                o_ref[...] += r

    if nk > 1:
        assert out_dtype == F32
    return _call(body, name=name, grid=(m // tm, n // tn, nk), in_specs=[a_spec, b_spec],
                 out_specs=pl.BlockSpec((tm, tn), lambda i, j, k: (i, j)), out_shape=_sds((m, n), out_dtype),
                 args=(a, b))


def _norm1_fwd(x, gain):
    s = x.shape[0]
    tm = min(TILES["row"], s)

    def body(x_ref, g_ref, o_ref):
        xv = x_ref[...]
        r = lax.rsqrt(jnp.mean(xv * xv, axis=-1, keepdims=True) + NORM_EPS)
        o_ref[...] = (xv * r * g_ref[...]).astype(BF16)

    return _call(body, name="norm1_fwd", grid=(s // tm,), in_specs=[_rows(tm, D_MODEL), _full((1, D_MODEL))],
                 out_specs=_rows(tm, D_MODEL), out_shape=_sds((s, D_MODEL), BF16), args=(x, gain))


def _norm1_bwd(x, gain, dxn, dy):
    s = x.shape[0]
    tm = min(TILES["row"], s)

    def body(x_ref, g_ref, dxn_ref, dy_ref, gx_ref, dg_ref):
        xv = x_ref[...]
        r = lax.rsqrt(jnp.mean(xv * xv, axis=-1, keepdims=True) + NORM_EPS)
        nrm = xv * r
        d = dxn_ref[...].astype(F32)
        dn = d * g_ref[...]
        gx_ref[...] = dy_ref[...].astype(F32) + r * (dn - nrm * jnp.mean(dn * nrm, axis=-1, keepdims=True))

        @pl.when(pl.program_id(0) == 0)
        def _():
            dg_ref[...] = jnp.zeros_like(dg_ref)

        dg_ref[...] += _acc8(d * nrm)

    return _call(body, name="norm1_bwd", grid=(s // tm,),
                 in_specs=[_rows(tm, D_MODEL), _full((1, D_MODEL)), _rows(tm, D_MODEL), _rows(tm, D_MODEL)],
                 out_specs=[_rows(tm, D_MODEL), _full((8, D_MODEL))],
                 out_shape=[_sds((s, D_MODEL), F32), _sds((8, D_MODEL), F32)], args=(x, gain, dxn, dy))


def _rms(xv, width):
    return lax.rsqrt(jnp.sum(xv * xv, axis=-1, keepdims=True) * (1.0 / width) + NORM_EPS)


def _mla_a_norm(proj, gq, gkv):
    s = proj.shape[0]
    tm = min(TILES["row"], s)

    def body(cq_ref, ckv_ref, gq_ref, gkv_ref, oq_ref, okv_ref):
        a = cq_ref[...].astype(F32)
        oq_ref[...] = (a * _rms(a, 512) * gq_ref[...]).astype(BF16)
        b = ckv_ref[...].astype(F32)
        okv_ref[...] = (b * _rms(b, 256) * gkv_ref[...]).astype(BF16)

    return _call(body, name="mla_a_norm", grid=(s // tm,),
                 in_specs=[_rows(tm, 512, 0), _rows(tm, 256, 2), _full((1, 512)), _full((1, 256))],
                 out_specs=[_rows(tm, 512), _rows(tm, 256)],
                 out_shape=[_sds((s, 512), BF16), _sds((s, 256), BF16)], args=(proj, proj, gq, gkv))


def _rms_bwd(xv, gain, d, width):
    r = _rms(xv, width)
    nrm = xv * r
    dn = d * gain
    dx = r * (dn - nrm * (jnp.sum(dn * nrm, axis=-1, keepdims=True) * (1.0 / width)))
    return dx, d * nrm


def _mla_a_norm_bwd(proj, gq, gkv, dcqn, dckvn):
    s = proj.shape[0]
    tm = min(TILES["row"], s)

    def body(cq_ref, ckv_ref, gq_ref, gkv_ref, dq_ref, dkv_ref, oq_ref, okv_ref, aq_ref, akv_ref):
        dxq, gq_part = _rms_bwd(cq_ref[...].astype(F32), gq_ref[...], dq_ref[...].astype(F32), 512)
        dxk, gk_part = _rms_bwd(ckv_ref[...].astype(F32), gkv_ref[...], dkv_ref[...].astype(F32), 256)
        oq_ref[...] = dxq.astype(BF16)
        okv_ref[...] = dxk.astype(BF16)

        @pl.when(pl.program_id(0) == 0)
        def _():
            aq_ref[...] = jnp.zeros_like(aq_ref)
            akv_ref[...] = jnp.zeros_like(akv_ref)

        aq_ref[...] += _acc8(gq_part)
        akv_ref[...] += _acc8(gk_part)

    return _call(body, name="mla_a_norm_bwd", grid=(s // tm,),
                 in_specs=[_rows(tm, 512, 0), _rows(tm, 256, 2), _full((1, 512)), _full((1, 256)),
                           _rows(tm, 512), _rows(tm, 256)],
                 out_specs=[_rows(tm, 512), _rows(tm, 256), _full((8, 512)), _full((8, 256))],
                 out_shape=[_sds((s, 512), BF16), _sds((s, 256), BF16), _sds((8, 512), F32), _sds((8, 256), F32)],
                 args=(proj, proj, gq, gkv, dcqn, dckvn))


def _swap32(r):
    lane = lax.broadcasted_iota(jnp.int32, r.shape, 1)
    return jnp.where(lane < HALF_ROPE, pltpu.roll(r, 128 - HALF_ROPE, 1), pltpu.roll(r, HALF_ROPE, 1))


def _mla_post_fwd(q_pre, kv_pre, proj, cs, sn, gq, gk):
    s = q_pre.shape[0]
    tm = min(TILES["row"], s)

    def body(qp_ref, kvp_ref, misc_ref, cs_ref, sn_ref, gq_ref, gk_ref, q_ref, k_ref, v_ref):
        csv, snv = cs_ref[...], sn_ref[...]
        lane = lax.broadcasted_iota(jnp.int32, (tm, 128), 1)
        kr = jnp.where(lane < 64, misc_ref[...], 0.0)
        for h in range(HEADS):
            for src, g_ref, o_ref in ((None, gq_ref, q_ref), (kr, gk_ref, k_ref)):
                if src is None:
                    xv = qp_ref[:, QK_PAD * h:QK_PAD * (h + 1)].astype(F32)
                else:
                    xv = jnp.concatenate([kvp_ref[:, 256 * h:256 * h + 128].astype(F32), src], axis=-1)
                y = xv * _rms(xv, QK_DIM) * g_ref[...]
                if src is None:
                    y = y * Q_PRESCALE
                hi = y[:, 128:]
                hi = hi * csv + _swap32(hi) * snv
                o_ref[:, QK_PAD * h:QK_PAD * h + 128] = y[:, :128].astype(BF16)
                o_ref[:, QK_PAD * h + 128:QK_PAD * (h + 1)] = hi.astype(BF16)
            v_ref[:, 128 * h:128 * (h + 1)] = kvp_ref[:, 256 * h + 128:256 * (h + 1)].astype(BF16)

    return _call(body, name="mla_post_fwd", grid=(s // tm,),
                 in_specs=[_rows(tm, 2048), _rows(tm, 2048), _rows(tm, 128), _rows(tm, 128), _rows(tm, 128),
                           _full((1, QK_PAD)), _full((1, QK_PAD))],
                 out_specs=[_rows(tm, 2048), _rows(tm, 2048), _rows(tm, 1024)],
                 out_shape=[_sds((s, 2048), BF16), _sds((s, 2048), BF16), _sds((s, 1024), BF16)],
                 args=(q_pre, kv_pre, proj, cs, sn, gq, gk))


def _mla_post_bwd(q_pre, kv_pre, proj, cs, sn, gq, gk, dq, dk, dv):
    s = q_pre.shape[0]
    tm = min(TILES["row"], s)

    def body(qp_ref, kvp_ref, misc_ref, cs_ref, sn_ref, gq_ref, gk_ref, dq_ref, dk_ref, dv_ref,
             oq_ref, okv_ref, okr_ref, agq_ref, agk_ref):
        csv, snv = cs_ref[...], sn_ref[...]
        lane = lax.broadcasted_iota(jnp.int32, (tm, 128), 1)
        kr = jnp.where(lane < 64, misc_ref[...], 0.0)

        @pl.when(pl.program_id(0) == 0)
        def _():
            agq_ref[...] = jnp.zeros_like(agq_ref)
            agk_ref[...] = jnp.zeros_like(agk_ref)

        dkr = jnp.zeros((tm, 128), F32)
        for h in range(HEADS):
            for which in (0, 1):
                if which == 0:
                    xv = qp_ref[:, QK_PAD * h:QK_PAD * (h + 1)].astype(F32)
                    d_ref, g_ref, a_ref = dq_ref, gq_ref, agq_ref
                else:
                    xv = jnp.concatenate([kvp_ref[:, 256 * h:256 * h + 128].astype(F32), kr], axis=-1)
                    d_ref, g_ref, a_ref = dk_ref, gk_ref, agk_ref
                dhi = d_ref[:, QK_PAD * h + 128:QK_PAD * (h + 1)].astype(F32)
                dhi = dhi * csv - _swap32(dhi) * snv
                dyv = jnp.concatenate([d_ref[:, QK_PAD * h:QK_PAD * h + 128].astype(F32), dhi], axis=-1)
                if which == 0:
                    dyv = dyv * ATTN_SCALE
                dx, gpart = _rms_bwd(xv, g_ref[...], dyv, QK_DIM)
                a_ref[...] += _acc8(gpart)
                if which == 0:
                    oq_ref[:, QK_PAD * h:QK_PAD * (h + 1)] = dx.astype(BF16)
                else:
                    okv_ref[:, 256 * h:256 * h + 128] = dx[:, :128].astype(BF16)
                    dkr = dkr + dx[:, 128:]
            okv_ref[:, 256 * h + 128:256 * (h + 1)] = dv_ref[:, 128 * h:128 * (h + 1)].astype(BF16)
        okr_ref[...] = dkr

    return _call(body, name="mla_post_bwd", grid=(s // tm,),
                 in_specs=[_rows(tm, 2048), _rows(tm, 2048), _rows(tm, 128), _rows(tm, 128), _rows(tm, 128),
                           _full((1, QK_PAD)), _full((1, QK_PAD)), _rows(tm, 2048), _rows(tm, 2048), _rows(tm, 1024)],
                 out_specs=[_rows(tm, 2048), _rows(tm, 2048), _rows(tm, 128), _full((8, QK_PAD)), _full((8, QK_PAD))],
                 out_shape=[_sds((s, 2048), BF16), _sds((s, 2048), BF16), _sds((s, 128), F32),
                            _sds((8, QK_PAD), F32), _sds((8, QK_PAD), F32)],
                 args=(q_pre, kv_pre, proj, cs, sn, gq, gk, dq, dk, dv))


ATTN_SCALE = QK_DIM ** -0.5
NEG = -1e30


LOG2E = 1.4426950408889634
LN2 = 0.6931471805599453
Q_PRESCALE = ATTN_SCALE * LOG2E
ATTN_SUB_FWD = 512
ATTN_SUB_BWD = 256


def _causal_pairs(nq, kv_major):
    prs = [(i, j) for i in range(nq) for j in range(i + 1)]
    if kv_major:
        prs.sort(key=lambda ij: (ij[1], ij[0]))
    return (jnp.asarray([p[0] for p in prs], jnp.int32), jnp.asarray([p[1] for p in prs], jnp.int32))


def _pair_call(body, *, name, tables, in_specs, out_specs, out_shape, scratch, args):
    def kfn(*refs):
        body(*refs)
    spec = pltpu.PrefetchScalarGridSpec(num_scalar_prefetch=2, grid=(HEADS, tables[0].shape[0]), in_specs=in_specs,
                                        out_specs=out_specs, scratch_shapes=list(scratch))
    return pl.pallas_call(
        kfn, name=name, grid_spec=spec, out_shape=out_shape,
        compiler_params=pltpu.CompilerParams(dimension_semantics=("parallel", "arbitrary"),
                                             vmem_limit_bytes=VMEM_LIMIT_V7X))(*tables, *args)


def _diag_mask(sc, ts, qs):
    row = lax.broadcasted_iota(jnp.int32, sc.shape, 0) + qs * ts
    col = lax.broadcasted_iota(jnp.int32, sc.shape, 1)
    return jnp.where(col <= row, sc, NEG)


def _attn_fwd(q, k, v):
    s = q.shape[0]
    t = min(TILES["attn"], s)
    ts = min(ATTN_SUB_FWD, t)
    nq = s // t

    def slabs(q_ref, k_ref, v_ref, m_s, l_s, acc_s, diag):
        def scores(qs):
            kw = (qs + 1) * ts if diag else t
            sc = lax.dot_general(q_ref[qs * ts:(qs + 1) * ts, :], k_ref[0:kw, :], NT, preferred_element_type=F32)
            return _diag_mask(sc, ts, qs) if diag else sc

        nsub = t // ts
        sc_next = scores(0)
        for qs in range(nsub):
            rq = slice(qs * ts, (qs + 1) * ts)
            kw = (qs + 1) * ts if diag else t
            sc = sc_next
            if qs + 1 < nsub:
                sc_next = scores(qs + 1)
            m_prev = m_s[rq, :]
            m_new = jnp.maximum(m_prev, jnp.max(sc, axis=-1, keepdims=True))
            p = jnp.exp2(sc - m_new)
            alpha = jnp.exp2(m_prev - m_new)
            l_s[rq, :] = alpha * l_s[rq, :] + jnp.sum(p, axis=-1, keepdims=True)
            acc_s[rq, :] = acc_s[rq, :] * alpha + lax.dot_general(p.astype(BF16), v_ref[0:kw, :], NN,
                                                                  preferred_element_type=F32)
            m_s[rq, :] = m_new

    def body(it_ref, jt_ref, q_ref, k_ref, v_ref, o_ref, lse_ref, m_s, l_s, acc_s):
        p = pl.program_id(1)
        i, j = it_ref[p], jt_ref[p]

        @pl.when(j == 0)
        def _():
            m_s[...] = jnp.full_like(m_s, NEG)
            l_s[...] = jnp.zeros_like(l_s)
            acc_s[...] = jnp.zeros_like(acc_s)

        @pl.when(j < i)
        def _():
            slabs(q_ref, k_ref, v_ref, m_s, l_s, acc_s, False)

        @pl.when(j == i)
        def _():
            slabs(q_ref, k_ref, v_ref, m_s, l_s, acc_s, True)
            o_ref[...] = acc_s[...] / l_s[...]
            lse_ref[...] = m_s[...] + jnp.log2(l_s[...])

    qb = lambda h, p, it, jt: (it[p], h)
    kb = lambda h, p, it, jt: (jt[p], h)
    return _pair_call(
        body, name="attn_fwd", tables=_causal_pairs(nq, kv_major=False),
        in_specs=[pl.BlockSpec((t, QK_PAD), qb), pl.BlockSpec((t, QK_PAD), kb), pl.BlockSpec((t, HEAD_DIM), kb)],
        out_specs=[pl.BlockSpec((t, HEAD_DIM), qb),
                   pl.BlockSpec((None, t, 1), lambda h, p, it, jt: (h, it[p], 0))],
        out_shape=[_sds((s, HEADS * HEAD_DIM), F32), _sds((HEADS, s, 1), F32)],
        scratch=[pltpu.VMEM((t, 1), F32), pltpu.VMEM((t, 1), F32), pltpu.VMEM((t, HEAD_DIM), F32)],
        args=(q, k, v))


def _attn_bwd(q, k, v, lse_rows, delta_rows, do):
    s = q.shape[0]
    t = min(TILES["attn"], s)
    ts = min(ATTN_SUB_BWD, t)
    nq = s // t

    def slabs(q_ref, k_ref, v_ref, lse_ref, delta_ref, do_ref, dq_ref, dk_ref, dv_ref, i, diag):
        def products(qs):
            rq = slice(qs * ts, (qs + 1) * ts)
            kw = (qs + 1) * ts if diag else t
            qv, dob = q_ref[rq, :], do_ref[rq, :]
            sct = lax.dot_general(k_ref[0:kw, :], qv, NT, preferred_element_type=F32)
            dpt = lax.dot_general(v_ref[0:kw, :], dob, NT, preferred_element_type=F32)
            if diag:
                row = lax.broadcasted_iota(jnp.int32, sct.shape, 0)
                col = lax.broadcasted_iota(jnp.int32, sct.shape, 1) + qs * ts
                sct = jnp.where(row <= col, sct, NEG)
            return qv, dob, sct, dpt

        nsub = t // ts
        ahead = products(0)
        for qs in range(nsub):
            rq = slice(qs * ts, (qs + 1) * ts)
            kw = (qs + 1) * ts if diag else t
            qv, dob, sct, dpt = ahead
            if qs + 1 < nsub:
                ahead = products(qs + 1)
            pt = jnp.exp2(sct - lse_ref[:, rq])
            dv_ref[0:kw, :] += lax.dot_general(pt.astype(BF16), dob, NN, preferred_element_type=F32)
            dst = (pt * (dpt - delta_ref[:, rq])).astype(BF16)
            dk_ref[0:kw, :] += lax.dot_general(dst, qv, NN, preferred_element_type=F32)
            rows = pl.ds(pl.multiple_of(i * t + qs * ts, ts), ts)
            dq_ref[rows, :] += lax.dot_general(dst, k_ref[0:kw, :], TN, preferred_element_type=F32)

    npairs = nq * (nq + 1) // 2

    def body(it_ref, jt_ref, q_ref, k_ref, v_ref, lse_ref, delta_ref, do_ref, dq_ref, dk_ref, dv_ref,
             dq_acc, dk_acc, dv_acc):
        p = pl.program_id(1)
        i, j = it_ref[p], jt_ref[p]
        refs = (q_ref, k_ref, v_ref, lse_ref, delta_ref, do_ref, dq_acc, dk_acc, dv_acc)

        @pl.when(p == 0)
        def _():
            dq_acc[...] = jnp.zeros_like(dq_acc)

        @pl.when(i == j)
        def _():
            dk_acc[...] = jnp.zeros_like(dk_acc)
            dv_acc[...] = jnp.zeros_like(dv_acc)

        @pl.when(i > j)
        def _():
            slabs(*refs, i, False)

        @pl.when(i == j)
        def _():
            slabs(*refs, i, True)

        @pl.when(i == nq - 1)
        def _():
            dk_ref[...] = (dk_acc[...] * LN2).astype(BF16)
            dv_ref[...] = dv_acc[...].astype(BF16)

        @pl.when(p == npairs - 1)
        def _():
            dq_ref[...] = dq_acc[...].astype(BF16)

    qb = lambda h, p, it, jt: (it[p], h)
    kb = lambda h, p, it, jt: (jt[p], h)
    rowb = pl.BlockSpec((None, 1, t), lambda h, p, it, jt: (h, 0, it[p]))
    return _pair_call(
        body, name="attn_bwd", tables=_causal_pairs(nq, kv_major=True),
        in_specs=[pl.BlockSpec((t, QK_PAD), qb), pl.BlockSpec((t, QK_PAD), kb), pl.BlockSpec((t, HEAD_DIM), kb),
                  rowb, rowb, pl.BlockSpec((t, HEAD_DIM), qb)],
        out_specs=[pl.BlockSpec((s, QK_PAD), lambda h, p, it, jt: (0, h)), pl.BlockSpec((t, QK_PAD), kb),
                   pl.BlockSpec((t, HEAD_DIM), kb)],
        out_shape=[_sds((s, HEADS * QK_PAD), BF16), _sds((s, HEADS * QK_PAD), BF16), _sds((s, HEADS * HEAD_DIM), BF16)],
        scratch=[pltpu.VMEM((s, QK_PAD), F32), pltpu.VMEM((t, QK_PAD), F32), pltpu.VMEM((t, HEAD_DIM), F32)],
        args=(q, k, v, lse_rows, delta_rows, do))


GDN_Q_SCALE = HEAD_DIM ** -0.5


def _shift_down(xv, prev8, sft):
    rolled = pltpu.roll(xv, sft, 0)
    top = pltpu.roll(jnp.concatenate([prev8, xv[:8]], axis=0), sft, 0)[8:]
    return jnp.concatenate([top, rolled[8:]], axis=0)


def _shift_up(xv, next8, sft):
    tm = xv.shape[0]
    rolled = pltpu.roll(xv, tm - sft, 0)
    bot = pltpu.roll(jnp.concatenate([xv[tm - 8:], next8], axis=0), 16 - sft, 0)[:8]
    return jnp.concatenate([rolled[:tm - 8], bot], axis=0)


def _conv_z(xv, prev8, w_ref):
    z = xv * w_ref[3:4, :]
    for sft in (1, 2, 3):
        z = z + _shift_down(xv, prev8, sft) * w_ref[3 - sft:4 - sft, :]
    return z


def _conv_specs(s, tm):
    nb16 = tm // 16
    cur = pl.BlockSpec((tm, 1024), lambda j, i: (i, 2 + j))
    prev = pl.BlockSpec((16, 1024), lambda j, i: (jnp.maximum(i * nb16 - 1, 0), 2 + j))
    return cur, prev


def _prev8(xp_ref, i):
    return jnp.where(i > 0, xp_ref[...].astype(F32)[8:], 0.0)


def _gdn_conv_fwd(proj, conv_w):
    s = proj.shape[0]
    tm = min(TILES["row"], s)
    cur, prev = _conv_specs(s, tm)

    def body(x_ref, xp_ref, w_ref, o_ref):
        j, i = pl.program_id(0), pl.program_id(1)
        a = _silu(_conv_z(x_ref[...].astype(F32), _prev8(xp_ref, i), w_ref))
        qk_scale = jnp.where(j == 0, GDN_Q_SCALE, 1.0)
        for h in range(HEADS):
            seg = a[:, 128 * h:128 * (h + 1)]
            r = lax.rsqrt(jnp.sum(seg * seg, axis=-1, keepdims=True) + NORM_EPS)
            o_ref[:, 128 * h:128 * (h + 1)] = jnp.where(j < 2, seg * r * qk_scale, seg).astype(BF16)

    return _call(body, name="gdn_conv_fwd", grid=(3, s // tm),
                 in_specs=[cur, prev, pl.BlockSpec((4, 1024), lambda j, i: (0, j))],
                 out_specs=pl.BlockSpec((tm, 1024), lambda j, i: (i, j)), out_shape=_sds((s, 3072), BF16),
                 args=(proj, proj, conv_w))


def _gdn_conv_bwd_a(proj, conv_w, dqkv):
    s = proj.shape[0]
    tm = min(TILES["row"], s)
    cur, prev = _conv_specs(s, tm)

    def body(x_ref, xp_ref, w_ref, d_ref, o_ref):
        j, i = pl.program_id(0), pl.program_id(1)
        z = _conv_z(x_ref[...].astype(F32), _prev8(xp_ref, i), w_ref)
        a = _silu(z)
        dsl = _dsilu(z)
        qk_scale = jnp.where(j == 0, GDN_Q_SCALE, 1.0)
        for h in range(HEADS):
            sl = slice(128 * h, 128 * (h + 1))
            seg = a[:, sl]
            dyv = d_ref[:, sl].astype(F32)
            r = lax.rsqrt(jnp.sum(seg * seg, axis=-1, keepdims=True) + NORM_EPS)
            yh = seg * r
            da_n = qk_scale * r * (dyv - yh * jnp.sum(yh * dyv, axis=-1, keepdims=True))
            o_ref[:, sl] = (jnp.where(j < 2, da_n, dyv) * dsl[:, sl]).astype(BF16)

    return _call(body, name="gdn_conv_bwd_a", grid=(3, s // tm),
                 in_specs=[cur, prev, pl.BlockSpec((4, 1024), lambda j, i: (0, j)),
                           pl.BlockSpec((tm, 1024), lambda j, i: (i, j))],
                 out_specs=pl.BlockSpec((tm, 1024), lambda j, i: (i, j)), out_shape=_sds((s, 3072), BF16),
                 args=(proj, proj, conv_w, dqkv))


def _gdn_conv_bwd_b(proj, conv_w, dz):
    s = proj.shape[0]
    tm = min(TILES["row"], s)
    nb16 = tm // 16
    last16 = s // 16 - 1
    cur, prev = _conv_specs(s, tm)

    def body(x_ref, w_ref, dz_ref, dzn_ref, dx_ref, dw_ref):
        i = pl.program_id(1)
        next8 = jnp.where(i < pl.num_programs(1) - 1, dzn_ref[...].astype(F32)[:8], 0.0)
        xv, dzv = x_ref[...].astype(F32), dz_ref[...].astype(F32)

        @pl.when(i == 0)
        def _():
            dw_ref[...] = jnp.zeros_like(dw_ref)

        dx = dzv * w_ref[3:4, :]
        dw_ref[3:4, :] += jnp.sum(dzv * xv, axis=0, keepdims=True)
        for sft in (1, 2, 3):
            up = _shift_up(dzv, next8, sft)
            dx = dx + up * w_ref[3 - sft:4 - sft, :]
            dw_ref[3 - sft:4 - sft, :] += jnp.sum(up * xv, axis=0, keepdims=True)
        dx_ref[...] = dx.astype(BF16)

    return _call(body, name="gdn_conv_bwd_b", grid=(3, s // tm),
                 in_specs=[cur, pl.BlockSpec((4, 1024), lambda j, i: (0, j)),
                           pl.BlockSpec((tm, 1024), lambda j, i: (i, j)),
                           pl.BlockSpec((16, 1024), lambda j, i: (jnp.minimum((i + 1) * nb16, last16), j))],
                 out_specs=[pl.BlockSpec((tm, 1024), lambda j, i: (i, j)), pl.BlockSpec((4, 1024), lambda j, i: (0, j))],
                 out_shape=[_sds((s, 3072), BF16), _sds((4, 3072), F32)], args=(proj, conv_w, dz, dz))


def _softplus(xv):
    return jnp.maximum(xv, 0.0) + jnp.log(1.0 + jnp.exp(-jnp.abs(xv)))


def _gdn_gates_fwd(proj, alog128, dtb128):
    s = proj.shape[0]
    tm = min(TILES["row"], s)

    def body(m_ref, a_ref, b_ref, o_ref):
        mv = m_ref[...]
        lane = lax.broadcasted_iota(jnp.int32, mv.shape, 1)
        g = -jnp.exp(a_ref[...]) * _softplus(mv + b_ref[...])
        is_g = (lane >= GA_LANE) & (lane < GA_LANE + HEADS)
        is_b = (lane >= GB_LANE) & (lane < GB_LANE + HEADS)
        o_ref[...] = jnp.where(is_g, g, jnp.where(is_b, _sigmoid(mv), 0.0))

    return _call(body, name="gdn_gates_fwd", grid=(s // tm,),
                 in_specs=[_rows(tm, 128), _full((1, 128)), _full((1, 128))],
                 out_specs=_rows(tm, 128), out_shape=_sds((s, 128), F32), args=(proj, alog128, dtb128))


def _gdn_gates_bwd(proj, alog128, dtb128, gbeta, dgbeta, dkr):
    s = proj.shape[0]
    tm = min(TILES["row"], s)

    def body(m_ref, a_ref, b_ref, gb_ref, d_ref, kr_ref, o_ref, da_ref, db_ref):
        mv, dv = m_ref[...], d_ref[...]
        lane = lax.broadcasted_iota(jnp.int32, mv.shape, 1)
        is_g = (lane >= GA_LANE) & (lane < GA_LANE + HEADS)
        is_b = (lane >= GB_LANE) & (lane < GB_LANE + HEADS)
        dga = jnp.where(is_g, dv * (-jnp.exp(a_ref[...])) * _sigmoid(mv + b_ref[...]), 0.0)
        beta = gb_ref[...]
        dgb = jnp.where(is_b, dv * beta * (1.0 - beta), 0.0)
        o_ref[...] = jnp.where(lane < 64, kr_ref[...], dga + dgb).astype(BF16)

        @pl.when(pl.program_id(0) == 0)
        def _():
            da_ref[...] = jnp.zeros_like(da_ref)
            db_ref[...] = jnp.zeros_like(db_ref)

        da_ref[...] += _acc8(jnp.where(is_g, dv * gb_ref[...], 0.0))
        db_ref[...] += _acc8(dga)

    return _call(body, name="gdn_gates_bwd", grid=(s // tm,),
                 in_specs=[_rows(tm, 128), _full((1, 128)), _full((1, 128)), _rows(tm, 128), _rows(tm, 128),
                           _rows(tm, 128)],
                 out_specs=[_rows(tm, 128), _full((8, 128)), _full((8, 128))],
                 out_shape=[_sds((s, 128), BF16), _sds((8, 128), F32), _sds((8, 128), F32)],
                 args=(proj, alog128, dtb128, gbeta, dgbeta, dkr))


def _col(mat, lane_idx, lane):
    return jnp.sum(jnp.where(lane == lane_idx, mat, 0.0), axis=-1, keepdims=True)


def _chunk_local(qh, kh, vh, gcol, bcol, ii, jj):
    lower, strict, eye = ii >= jj, ii > jj, ii == jj
    grow = jnp.sum(jnp.where(eye, gcol, 0.0), axis=0, keepdims=True)
    decay = jnp.where(lower, jnp.exp(jnp.where(lower, gcol - grow, 0.0)), 0.0)
    kb = kh * bcol
    vb = vh * bcol
    mm = _dot(kb, kh, NT)
    lmat = jnp.where(strict, mm * decay, 0.0)
    pw = -lmat
    tinv = jnp.where(eye, 1.0, 0.0) + pw
    for _ in range(5):
        pw = _dot_hi(pw, pw)
        tinv = tinv + _dot_hi(tinv, pw)
    egc = jnp.exp(gcol)
    kbg = kb * egc
    rhs = jnp.concatenate([vb, kbg], axis=-1)
    sol = _dot_hi(tinv, rhs)
    qk = _dot(qh, kh, NT)
    glast = jnp.sum(jnp.where(ii[:, :1] == CHUNK - 1, gcol, 0.0), axis=0, keepdims=True)
    ekd = jnp.exp(glast - gcol)
    return dict(decay=decay, kb=kb, vb=vb, mm=mm, lmat=lmat, tinv=tinv, egc=egc, kbg=kbg, rhs=rhs,
                u=sol[:, :HEAD_DIM], w=sol[:, HEAD_DIM:], qk=qk, amat=qk * decay, qd=qh * egc, ekd=ekd,
                kd=kh * ekd, gl=jnp.exp(glast), strict=strict, lower=lower, eye=eye)


def _tri(ii, jj):
    return jnp.where(ii >= jj, 1.0, 0.0)


def _gdn_fwd(qkv, gbeta):
    s = qkv.shape[0]
    n = s // CHUNK

    def body(qkv_ref, gb_ref, o_ref, st_ref, state):
        @pl.when(pl.program_id(0) == 0)
        def _():
            state[...] = jnp.zeros_like(state)

        ii = lax.broadcasted_iota(jnp.int32, (CHUNK, CHUNK), 0)
        jj = lax.broadcasted_iota(jnp.int32, (CHUNK, CHUNK), 1)
        lane = lax.broadcasted_iota(jnp.int32, (CHUNK, 128), 1)
        gbv = gb_ref[...]
        gc = _dot_hi(_tri(ii, jj), gbv)
        for h in range(HEADS):
            sl = slice(128 * h, 128 * (h + 1))
            qh = qkv_ref[:, 128 * h:128 * (h + 1)]
            kh = qkv_ref[:, 1024 + 128 * h:1024 + 128 * (h + 1)]
            vh = qkv_ref[:, 2048 + 128 * h:2048 + 128 * (h + 1)]
            c = _chunk_local(qh, kh, vh, _col(gc, GA_LANE + h, lane), _col(gbv, GB_LANE + h, lane), ii, jj)
            st = state[sl, :]
            st_ref[sl, :] = st
            vn = c["u"] - _dot(c["w"], st)
            o_ref[:, sl] = _dot(c["qd"], st) + _dot(c["amat"], vn)
            state[sl, :] = st * c["gl"] + _dot(c["kd"], vn, TN)

    return _call(body, name="gdn_fwd", grid=(n,),
                 in_specs=[_rows(CHUNK, 3072), _rows(CHUNK, 128)],
                 out_specs=[_rows(CHUNK, 1024), _rows(HEADS * 128, 128)],
                 out_shape=[_sds((s, 1024), F32), _sds((n * HEADS * 128, 128), F32)],
                 scratch=[pltpu.VMEM((HEADS * 128, 128), F32)], args=(qkv, gbeta))


def _gdn_bwd(qkv, gbeta, states, do):
    s = qkv.shape[0]
    n = s // CHUNK

    def body(qkv_ref, gb_ref, st_ref, do_ref, dqkv_ref, dgb_ref, dstate):
        @pl.when(pl.program_id(0) == 0)
        def _():
            dstate[...] = jnp.zeros_like(dstate)

        ii = lax.broadcasted_iota(jnp.int32, (CHUNK, CHUNK), 0)
        jj = lax.broadcasted_iota(jnp.int32, (CHUNK, CHUNK), 1)
        lane = lax.broadcasted_iota(jnp.int32, (CHUNK, 128), 1)
        row1 = ii[:, :1]
        gbv = gb_ref[...]
        gc = _dot_hi(_tri(ii, jj), gbv)
        dgc_all = jnp.zeros((CHUNK, 128), F32)
        db_all = jnp.zeros((CHUNK, 128), F32)
        for h in range(HEADS):
            sl = slice(128 * h, 128 * (h + 1))
            qh = qkv_ref[:, 128 * h:128 * (h + 1)]
            kh = qkv_ref[:, 1024 + 128 * h:1024 + 128 * (h + 1)]
            vh = qkv_ref[:, 2048 + 128 * h:2048 + 128 * (h + 1)]
            bcol = _col(gbv, GB_LANE + h, lane)
            c = _chunk_local(qh, kh, vh, _col(gc, GA_LANE + h, lane), bcol, ii, jj)
            st = st_ref[sl, :]
            dst = dstate[sl, :]
            dov = do_ref[:, sl]
            vn = c["u"] - _dot(c["w"], st)
            dvn = _dot(c["amat"], dov, TN) + _dot(c["kd"], dst)
            damat = jnp.where(c["lower"], _dot(dov, vn, NT), 0.0)
            dqd = _dot(dov, st, NT)
            dkd = _dot(vn, dst, NT)
            dw = -_dot(dvn, st, NT)
            dgl = jnp.sum(jnp.sum(st * dst, axis=-1, keepdims=True), axis=0, keepdims=True)
            dstate[sl, :] = _dot(c["qd"], dov, TN) + c["gl"] * dst - _dot(c["w"], dvn, TN)
            dsol = jnp.concatenate([dvn, dw], axis=-1)
            drhs = _dot_hi(c["tinv"], dsol, TN)
            dtinv = _dot_hi(dsol, c["rhs"], NT)
            dl = -_dot_hi(_dot_hi(c["tinv"], dtinv, TN), c["tinv"], NT)
            dl = jnp.where(c["strict"], dl, 0.0)
            dmm = dl * c["decay"]
            dqk = damat * c["decay"]
            wmat = dl * c["lmat"] + damat * c["amat"]
            dgc = jnp.sum(wmat, axis=-1, keepdims=True)
            wcol = jnp.sum(wmat, axis=0, keepdims=True)
            dgc = dgc - jnp.sum(jnp.where(c["eye"], wcol, 0.0), axis=-1, keepdims=True)
            dkb = _dot(dmm, kh) + drhs[:, HEAD_DIM:] * c["egc"]
            dk = _dot(dmm, c["kb"], TN) + _dot(dqk, qh, TN) + dkd * c["ekd"]
            dq = _dot(dqk, kh) + dqd * c["egc"]
            dgc = dgc + jnp.sum(drhs[:, HEAD_DIM:] * c["kbg"], axis=-1, keepdims=True)
            dgc = dgc + jnp.sum(dqd * c["qd"], axis=-1, keepdims=True)
            tmp = jnp.sum(dkd * c["kd"], axis=-1, keepdims=True)
            dgc = dgc - tmp
            dglast = jnp.sum(tmp, axis=0, keepdims=True) + dgl * c["gl"]
            dgc = dgc + jnp.where(row1 == CHUNK - 1, dglast, 0.0)
            dk = dk + dkb * bcol
            db = jnp.sum(dkb * kh, axis=-1, keepdims=True) + jnp.sum(drhs[:, :HEAD_DIM] * vh, axis=-1, keepdims=True)
            dqkv_ref[:, 128 * h:128 * (h + 1)] = dq
            dqkv_ref[:, 1024 + 128 * h:1024 + 128 * (h + 1)] = dk
            dqkv_ref[:, 2048 + 128 * h:2048 + 128 * (h + 1)] = drhs[:, :HEAD_DIM] * bcol
            dgc_all = dgc_all + jnp.where(lane == GA_LANE + h, dgc, 0.0)
            db_all = db_all + jnp.where(lane == GB_LANE + h, db, 0.0)
        dgb_ref[...] = _dot_hi(_tri(jj, ii), dgc_all) + db_all

    rev = lambda w: pl.BlockSpec((CHUNK, w), lambda i: (n - 1 - i, 0))
    return _call(body, name="gdn_bwd", grid=(n,),
                 in_specs=[rev(3072), rev(128), pl.BlockSpec((HEADS * 128, 128), lambda i: (n - 1 - i, 0)), rev(1024)],
                 out_specs=[rev(3072), rev(128)],
                 out_shape=[_sds((s, 3072), F32), _sds((s, 128), F32)],
                 scratch=[pltpu.VMEM((HEADS * 128, 128), F32)], args=(qkv, gbeta, states, do))


NN_B = (((2,), (1,)), ((0,), (0,)))
NT_B = (((2,), (2,)), ((0,), (0,)))
TN_B = (((1,), (1,)), ((0,), (0,)))
GDN_PRE_CHUNKS = 4
GDN_POST_CHUNKS = 2
GDN_SEQ_CHUNKS = 8


def _gather_heads(qkv_ref, gc, gbv, qs, ks, vs, gs, bs, nchunks):
    lane = lax.broadcasted_iota(jnp.int32, (CHUNK, 128), 1)
    for c in range(nchunks):
        rows = slice(CHUNK * c, CHUNK * (c + 1))
        for h in range(HEADS):
            b = HEADS * c + h
            qs[b] = qkv_ref[rows, 128 * h:128 * (h + 1)].astype(F32)
            ks[b] = qkv_ref[rows, 1024 + 128 * h:1024 + 128 * (h + 1)].astype(F32)
            vs[b] = qkv_ref[rows, 2048 + 128 * h:2048 + 128 * (h + 1)].astype(F32)
            gs[b] = jnp.broadcast_to(_col(gc[rows], GA_LANE + h, lane), (CHUNK, 128))
            bs[b] = jnp.broadcast_to(_col(gbv[rows], GB_LANE + h, lane), (CHUNK, 128))


def _block_tri(rows, transpose=False):
    ri = lax.broadcasted_iota(jnp.int32, (rows, rows), 0)
    ci = lax.broadcasted_iota(jnp.int32, (rows, rows), 1)
    same = (ri >> 6) == (ci >> 6)
    return jnp.where(same & ((ci >= ri) if transpose else (ri >= ci)), 1.0, 0.0)


def _local_b(q, k, v, g128, b128):
    ii = lax.broadcasted_iota(jnp.int32, (1, CHUNK, CHUNK), 1)
    jj = lax.broadcasted_iota(jnp.int32, (1, CHUNK, CHUNK), 2)
    lower, strict, eye = ii >= jj, ii > jj, ii == jj
    g64 = g128[:, :, :CHUNK]
    grow = jnp.sum(jnp.where(eye, g64, 0.0), axis=1, keepdims=True)
    decay = jnp.where(lower, jnp.exp(jnp.where(lower, g64 - grow, 0.0)), 0.0)
    kb = k * b128
    vb = v * b128
    mm = lax.dot_general(kb.astype(BF16), k.astype(BF16), NT_B, preferred_element_type=F32)
    lmat = jnp.where(strict, mm * decay, 0.0)
    egc = jnp.exp(g128)
    kbg = kb * egc
    qk = lax.dot_general(q.astype(BF16), k.astype(BF16), NT_B, preferred_element_type=F32)
    row = lax.broadcasted_iota(jnp.int32, (1, CHUNK, 128), 1)
    glast = jnp.sum(jnp.where(row == CHUNK - 1, g128, 0.0), axis=1, keepdims=True)
    ekd = jnp.exp(glast - g128)
    return dict(decay=decay, kb=kb, vb=vb, lmat=lmat, egc=egc, kbg=kbg, amat=qk * decay, qd=q * egc, ekd=ekd,
                kd=k * ekd, gl=jnp.exp(glast), lower=lower, strict=strict, eye=eye)


def _bdot(a, b, dims):
    return lax.dot_general(a.astype(BF16), b.astype(BF16), dims, preferred_element_type=F32)


def _split(a):
    hi = a.astype(BF16)
    return hi, (a - hi.astype(F32)).astype(BF16)


def _bdot_hi(a, b, dims):
    ah, al = _split(a)
    bh, bl = _split(b)
    d = lambda x, y: lax.dot_general(x, y, dims, preferred_element_type=F32)
    return d(ah, bh) + d(ah, bl) + d(al, bh)


def _gdn_pre(qkv, gbeta):
    s = qkv.shape[0]
    n = s // CHUNK
    cb = min(GDN_PRE_CHUNKS, n)
    nb = cb * HEADS
    rows = cb * CHUNK

    def body(qkv_ref, gb_ref, u_ref, w_ref, qd_ref, kd_ref, a_ref, t_ref, gl_ref, qs, ks, vs, gs, bs):
        gbv = gb_ref[...]
        gc = _dot_hi(_block_tri(rows), gbv)
        _gather_heads(qkv_ref, gc, gbv, qs, ks, vs, gs, bs, cb)
        c = _local_b(qs[...], ks[...], vs[...], gs[...], bs[...])
        pw = -c["lmat"]
        tinv = jnp.where(c["eye"], 1.0, 0.0) + pw
        for _ in range(5):
            pw = _bdot_hi(pw, pw, NN_B)
            tinv = tinv + _bdot_hi(tinv, pw, NN_B)
        u_ref[...] = _bdot_hi(tinv, c["vb"], NN_B)
        w_ref[...] = _bdot_hi(tinv, c["kbg"], NN_B).astype(BF16)
        qd_ref[...] = c["qd"].astype(BF16)
        kd_ref[...] = c["kd"].astype(BF16)
        a_ref[...] = c["amat"].astype(BF16)
        t_ref[...] = tinv
        gl_ref[...] = c["gl"]

    b3 = lambda d: pl.BlockSpec((nb, CHUNK, d), lambda i: (i, 0, 0))
    nt = n * HEADS
    return _call(body, name="gdn_pre", grid=(n // cb,),
                 in_specs=[_rows(rows, 3072), _rows(rows, 128)],
                 out_specs=[b3(128), b3(128), b3(128), b3(128), b3(CHUNK), b3(CHUNK),
                            pl.BlockSpec((nb, 1, 128), lambda i: (i, 0, 0))],
                 out_shape=[_sds((nt, CHUNK, 128), F32), _sds((nt, CHUNK, 128), BF16), _sds((nt, CHUNK, 128), BF16),
                            _sds((nt, CHUNK, 128), BF16), _sds((nt, CHUNK, CHUNK), BF16), _sds((nt, CHUNK, CHUNK), F32),
                            _sds((nt, 1, 128), F32)],
                 scratch=[pltpu.VMEM((nb, CHUNK, 128), F32)] * 5, sem=("parallel",), args=(qkv, gbeta))


def _gdn_scan_fwd(u, w, qd, kd, amat, gl):
    nt = u.shape[0]
    n = nt // HEADS
    cs = min(GDN_SEQ_CHUNKS, n)

    def body(u_ref, w_ref, qd_ref, kd_ref, a_ref, gl_ref, o_ref, st_ref, state):
        @pl.when(pl.program_id(0) == 0)
        def _():
            state[...] = jnp.zeros_like(state)

        for c in range(cs):
            sl = slice(HEADS * c, HEADS * (c + 1))
            st = state[...]
            stb = st.astype(BF16)
            st_ref[sl] = stb
            vn = u_ref[sl] - lax.dot_general(w_ref[sl], stb, NN_B, preferred_element_type=F32)
            vnb = vn.astype(BF16)
            o = (lax.dot_general(qd_ref[sl], stb, NN_B, preferred_element_type=F32)
                 + lax.dot_general(a_ref[sl], vnb, NN_B, preferred_element_type=F32))
            state[...] = st * gl_ref[sl] + lax.dot_general(kd_ref[sl], vnb, TN_B, preferred_element_type=F32)
            for h in range(HEADS):
                o_ref[CHUNK * c:CHUNK * (c + 1), 128 * h:128 * (h + 1)] = o[h]

    b3 = lambda d: pl.BlockSpec((cs * HEADS, CHUNK, d), lambda i: (i, 0, 0))
    return _call(body, name="gdn_scan_fwd", grid=(n // cs,),
                 in_specs=[b3(128), b3(128), b3(128), b3(128), b3(CHUNK), pl.BlockSpec((cs * HEADS, 1, 128), lambda i: (i, 0, 0))],
                 out_specs=[_rows(cs * CHUNK, 1024), pl.BlockSpec((cs * HEADS, 128, 128), lambda i: (i, 0, 0))],
                 out_shape=[_sds((n * CHUNK, 1024), F32), _sds((nt, 128, 128), BF16)],
                 scratch=[pltpu.VMEM((HEADS, 128, 128), F32)], args=(u, w, qd, kd, amat, gl))


def _gdn_scan_bwd(w, qd, kd, amat, gl, do):
    nt = w.shape[0]
    n = nt // HEADS
    cs = min(GDN_SEQ_CHUNKS, n)
    ng = n // cs

    def body(w_ref, qd_ref, kd_ref, a_ref, gl_ref, do_ref, ds_ref, dstate, dos):
        @pl.when(pl.program_id(0) == 0)
        def _():
            dstate[...] = jnp.zeros_like(dstate)

        for c in reversed(range(cs)):
            sl = slice(HEADS * c, HEADS * (c + 1))
            for h in range(HEADS):
                dos[h] = do_ref[CHUNK * c:CHUNK * (c + 1), 128 * h:128 * (h + 1)].astype(BF16)
            dob = dos[...]
            dst = dstate[...]
            dstb = dst.astype(BF16)
            ds_ref[sl] = dstb
            dvn = (lax.dot_general(a_ref[sl], dob, TN_B, preferred_element_type=F32)
                   + lax.dot_general(kd_ref[sl], dstb, NN_B, preferred_element_type=F32))
            dstate[...] = (lax.dot_general(qd_ref[sl], dob, TN_B, preferred_element_type=F32) + gl_ref[sl] * dst
                           - lax.dot_general(w_ref[sl], dvn.astype(BF16), TN_B, preferred_element_type=F32))

    b3 = lambda d: pl.BlockSpec((cs * HEADS, CHUNK, d), lambda i: (ng - 1 - i, 0, 0))
    return _call(body, name="gdn_scan_bwd", grid=(ng,),
                 in_specs=[b3(128), b3(128), b3(128), b3(CHUNK), pl.BlockSpec((cs * HEADS, 1, 128), lambda i: (ng - 1 - i, 0, 0)),
                           pl.BlockSpec((cs * CHUNK, 1024), lambda i: (ng - 1 - i, 0))],
                 out_specs=pl.BlockSpec((cs * HEADS, 128, 128), lambda i: (ng - 1 - i, 0, 0)),
                 out_shape=_sds((nt, 128, 128), BF16),
                 scratch=[pltpu.VMEM((HEADS, 128, 128), F32), pltpu.VMEM((HEADS, CHUNK, 128), BF16)],
                 args=(w, qd, kd, amat, gl, do))


def _gdn_post_bwd(qkv, gbeta, u, w, tinv, states, dstates, do):
    s = qkv.shape[0]
    n = s // CHUNK
    cb = min(GDN_POST_CHUNKS, n)
    nb = cb * HEADS
    rows = cb * CHUNK

    def body(qkv_ref, gb_ref, u_ref, w_ref, t_ref, st_ref, ds_ref, do_ref, dqkv_ref, dgb_ref, qs, ks, vs, gs, bs, dos):
        gbv = gb_ref[...]
        gc = _dot_hi(_block_tri(rows), gbv)
        _gather_heads(qkv_ref, gc, gbv, qs, ks, vs, gs, bs, cb)
        for c in range(cb):
            for h in range(HEADS):
                dos[HEADS * c + h] = do_ref[CHUNK * c:CHUNK * (c + 1), 128 * h:128 * (h + 1)].astype(F32)
        q, k, v, b128 = qs[...], ks[...], vs[...], bs[...]
        c = _local_b(q, k, v, gs[...], b128)
        tinv, st, dst, dov = t_ref[...], st_ref[...], ds_ref[...], dos[...]
        wv = w_ref[...]
        uv = u_ref[...]
        vn = uv - _bdot(wv, st, NN_B)
        dvn = _bdot(c["amat"], dov, TN_B) + _bdot(c["kd"], dst, NN_B)
        damat = jnp.where(c["lower"], _bdot(dov, vn, NT_B), 0.0)
        dqd = _bdot(dov, st, NT_B)
        dkd = _bdot(vn, dst, NT_B)
        dw = -_bdot(dvn, st, NT_B)
        dgl = jnp.sum(jnp.sum(st.astype(F32) * dst.astype(F32), axis=1, keepdims=True), axis=-1, keepdims=True)
        dvb = _bdot(tinv, dvn, TN_B)
        dkbg = _bdot(tinv, dw, TN_B)
        dl = -(_bdot(dvb, uv, NT_B) + _bdot(dkbg, wv, NT_B))
        dl = jnp.where(c["strict"], dl, 0.0)
        dmm = dl * c["decay"]
        dqk = damat * c["decay"]
        wmat = dl * c["lmat"] + damat * c["amat"]
        wcol = jnp.sum(wmat, axis=1, keepdims=True)
        dgc = jnp.sum(wmat, axis=-1, keepdims=True) - jnp.sum(jnp.where(c["eye"], wcol, 0.0), axis=-1, keepdims=True)
        dkb = _bdot(dmm, k, NN_B) + dkbg * c["egc"]
        dk = _bdot(dmm, c["kb"], TN_B) + _bdot(dqk, q, TN_B) + dkd * c["ekd"] + dkb * b128
        dq = _bdot(dqk, k, NN_B) + dqd * c["egc"]
        tmp = jnp.sum(dkd * c["kd"], axis=-1, keepdims=True)
        dgc = (dgc + jnp.sum(dkbg * c["kbg"], axis=-1, keepdims=True) + jnp.sum(dqd * c["qd"], axis=-1, keepdims=True)
               - tmp)
        dglast = jnp.sum(tmp, axis=1, keepdims=True) + dgl * c["gl"][:, :, :1]
        row1 = lax.broadcasted_iota(jnp.int32, (1, CHUNK, 1), 1)
        dgc = dgc + jnp.where(row1 == CHUNK - 1, dglast, 0.0)
        db = jnp.sum(dkb * k, axis=-1, keepdims=True) + jnp.sum(dvb * v, axis=-1, keepdims=True)
        dv = dvb * b128
        lane = lax.broadcasted_iota(jnp.int32, (CHUNK, 128), 1)
        parts = []
        for cc in range(cb):
            acc = jnp.zeros((CHUNK, 128), F32)
            for h in range(HEADS):
                bi = HEADS * cc + h
                rs = slice(CHUNK * cc, CHUNK * (cc + 1))
                dqkv_ref[rs, 128 * h:128 * (h + 1)] = dq[bi].astype(BF16)
                dqkv_ref[rs, 1024 + 128 * h:1024 + 128 * (h + 1)] = dk[bi].astype(BF16)
                dqkv_ref[rs, 2048 + 128 * h:2048 + 128 * (h + 1)] = dv[bi].astype(BF16)
                acc = acc + jnp.where(lane == GA_LANE + h, dgc[bi], 0.0)
            parts.append(acc)
        dgc_all = jnp.concatenate(parts, axis=0)
        dg_all = _dot_hi(_block_tri(rows, transpose=True), dgc_all)
        for cc in range(cb):
            acc = dg_all[CHUNK * cc:CHUNK * (cc + 1)]
            for h in range(HEADS):
                acc = acc + jnp.where(lane == GB_LANE + h, db[HEADS * cc + h], 0.0)
            dgb_ref[CHUNK * cc:CHUNK * (cc + 1), :] = acc

    b3 = lambda d1, d2: pl.BlockSpec((nb, d1, d2), lambda i: (i, 0, 0))
    return _call(body, name="gdn_post_bwd", grid=(n // cb,),
                 in_specs=[_rows(rows, 3072), _rows(rows, 128), b3(CHUNK, 128), b3(CHUNK, 128), b3(CHUNK, CHUNK),
                           b3(128, 128), b3(128, 128), _rows(rows, 1024)],
                 out_specs=[_rows(rows, 3072), _rows(rows, 128)],
                 out_shape=[_sds((s, 3072), BF16), _sds((s, 128), F32)],
                 scratch=[pltpu.VMEM((nb, CHUNK, 128), F32)] * 6, sem=("parallel",),
                 args=(qkv, gbeta, u, w, tinv, states, dstates, do))


def _mix_fwd(o_mla, o_gdn, proj, out_gain):
    s = proj.shape[0]
    tm = min(TILES["row"], s)

    def body(om_ref, og_ref, mg_ref, gg_ref, g_ref, o_ref):
        o_ref[:, :1024] = (om_ref[...] * _silu(mg_ref[...].astype(F32))).astype(BF16)
        for h in range(HEADS):
            sl = slice(128 * h, 128 * (h + 1))
            og = og_ref[:, sl]
            on = og * _rms(og, HEAD_DIM) * g_ref[...]
            o_ref[:, 1024 + 128 * h:1024 + 128 * (h + 1)] = (on * _silu(gg_ref[:, sl].astype(F32))).astype(BF16)

    return _call(body, name="mix_fwd", grid=(s // tm,),
                 in_specs=[_rows(tm, 1024), _rows(tm, 1024), _rows(tm, 1024, 1), _rows(tm, 1024, 5), _full((1, 128))],
                 out_specs=_rows(tm, 2048), out_shape=_sds((s, 2048), BF16), args=(o_mla, o_gdn, proj, proj, out_gain))


def _mix_bwd(o_mla, o_gdn, proj, out_gain, dmixed):
    s = proj.shape[0]
    tm = min(TILES["row"], s)

    def body(om_ref, og_ref, mg_ref, gg_ref, g_ref, dm_ref, dg_ref, dom_ref, dog_ref, dmg_ref, dgg_ref, ag_ref,
             delta_ref):
        @pl.when(pl.program_id(0) == 0)
        def _():
            ag_ref[...] = jnp.zeros_like(ag_ref)

        mg = mg_ref[...].astype(F32)
        dm = dm_ref[...].astype(F32)
        om = om_ref[...]
        dom = (dm * _silu(mg)).astype(BF16)
        dom_ref[...] = dom
        dmg_ref[...] = (dm * om * _dsilu(mg)).astype(BF16)
        prod = dom.astype(F32) * om
        lane = lax.broadcasted_iota(jnp.int32, (tm, 128), 1)
        delta = jnp.zeros((tm, 128), F32)
        for h in range(HEADS):
            delta = delta + jnp.where(lane == h, jnp.sum(prod[:, 128 * h:128 * (h + 1)], axis=-1, keepdims=True), 0.0)
        delta_ref[...] = delta
        for h in range(HEADS):
            sl = slice(128 * h, 128 * (h + 1))
            og, gg, d = og_ref[:, sl], gg_ref[:, sl].astype(F32), dg_ref[:, sl].astype(F32)
            on = og * _rms(og, HEAD_DIM) * g_ref[...]
            dgg_ref[:, sl] = (d * on * _dsilu(gg)).astype(BF16)
            dx, gpart = _rms_bwd(og, g_ref[...], d * _silu(gg), HEAD_DIM)
            dog_ref[:, sl] = dx.astype(BF16)
            ag_ref[...] += _acc8(gpart)

    return _call(body, name="mix_bwd", grid=(s // tm,),
                 in_specs=[_rows(tm, 1024), _rows(tm, 1024), _rows(tm, 1024, 1), _rows(tm, 1024, 5), _full((1, 128)),
                           _rows(tm, 1024, 0), _rows(tm, 1024, 1)],
                 out_specs=[_rows(tm, 1024), _rows(tm, 1024), _rows(tm, 1024), _rows(tm, 1024), _full((8, 128)),
                            _rows(tm, 128)],
                 out_shape=[_sds((s, 1024), BF16), _sds((s, 1024), BF16), _sds((s, 1024), BF16), _sds((s, 1024), BF16),
                            _sds((8, 128), F32), _sds((s, 128), F32)],
                 args=(o_mla, o_gdn, proj, proj, out_gain, dmixed, dmixed))


def _out_fwd(mixed, w_out, x, target):
    s = x.shape[0]
    tm = min(TILES["mm"], s)
    tn = min(TILES["mm"], D_MODEL)

    def body(m_ref, w_ref, x_ref, t_ref, dy_ref, acc_ref):
        err = x_ref[...] + _dot(m_ref[...], w_ref[...]) - t_ref[...]
        dy_ref[...] = (err * (1.0 / D_MODEL)).astype(BF16)

        @pl.when(pl.program_id(1) == 0)
        def _():
            acc_ref[...] = jnp.zeros_like(acc_ref)

        acc_ref[...] += _acc8(err * err)

    return _call(body, name="out_fwd", grid=(D_MODEL // tn, s // tm),
                 in_specs=[pl.BlockSpec((tm, D_MODEL), lambda j, i: (i, 0)), pl.BlockSpec((D_MODEL, tn), lambda j, i: (0, j)),
                           pl.BlockSpec((tm, tn), lambda j, i: (i, j)), pl.BlockSpec((tm, tn), lambda j, i: (i, j))],
                 out_specs=[pl.BlockSpec((tm, tn), lambda j, i: (i, j)), pl.BlockSpec((8, tn), lambda j, i: (0, j))],
                 out_shape=[_sds((s, D_MODEL), BF16), _sds((8, D_MODEL), F32)], args=(mixed, w_out, x, target))


def _row_tile(r, c):
    if r % 8 != 0:
        return r
    t = 8
    while r % (2 * t) == 0 and 2 * t * c * 4 <= (1 << 20):
        t *= 2
    return t


def _sum_arrays(parts, name, also_bf16=False):
    r, c = parts[0].shape
    tr = _row_tile(r, c)
    n = len(parts)

    def body(*refs):
        acc = refs[0][...].astype(F32)
        for p_ref in refs[1:n]:
            acc = acc + p_ref[...].astype(F32)
        refs[n][...] = acc
        if also_bf16:
            refs[n + 1][...] = acc.astype(BF16)

    nout = 2 if also_bf16 else 1
    out = _call(body, name=name, grid=(r // tr,), in_specs=[_rows(tr, c)] * n, out_specs=[_rows(tr, c)] * nout,
                out_shape=[_sds((r, c), F32), _sds((r, c), BF16)][:nout], args=tuple(parts))
    return out if also_bf16 else out[0]


def _adamw(w, g, m, v, name):
    r, c = w.shape
    c1 = 1.0 - ADAM_B1 ** ADAM_STEP
    c2 = 1.0 - ADAM_B2 ** ADAM_STEP

    def body(w_ref, g_ref, m_ref, v_ref, d_ref, nm_ref, nv_ref):
        gv = g_ref[...]
        nm = ADAM_B1 * m_ref[...] + (1.0 - ADAM_B1) * gv
        nv = ADAM_B2 * v_ref[...] + (1.0 - ADAM_B2) * (gv * gv)
        nm_ref[...] = nm
        nv_ref[...] = nv
        d_ref[...] = -ADAM_LR * ((nm / c1) / (jnp.sqrt(nv / c2) + ADAM_EPS) + ADAM_WD * w_ref[...])

    if r % 8 == 0:
        tr = _row_tile(r, c)
        grid, spec = (r // tr,), _rows(tr, c)
    else:
        tc = c
        while tc % 256 == 0 and r * tc * 4 > (3 << 19):
            tc //= 2
        grid, spec = (c // tc,), pl.BlockSpec((r, tc), lambda i: (0, i))
    return _call(body, name=name, grid=grid, in_specs=[spec] * 4, out_specs=[spec] * 3,
                 out_shape=[_sds((r, c), F32)] * 3, args=(w, g, m, v))


ANY = pl.BlockSpec(memory_space=pl.ANY)
CHIP_FLIPS = ((1, 0), (0, 1), (1, 1))


def _comm_call(body, *, name, n_in, out_shape, scratch):
    def kfn(*refs):
        body(*refs)
    return pl.pallas_call(kfn, name=name, in_specs=[ANY] * n_in, out_specs=[ANY] * len(out_shape), out_shape=out_shape,
                          scratch_shapes=list(scratch),
                          compiler_params=pltpu.CompilerParams(has_side_effects=True))


def _all_gather_chips(shards):
    na = len(shards)

    def body(*refs):
        copies = _gather_copies(refs[:na], refs[na:2 * na], *refs[2 * na:])
        _gather_start(copies)
        _gather_finish(copies)

    out_shape = [_sds((4,) + a.shape, a.dtype) for a in shards]
    sem = pltpu.SemaphoreType.DMA((na, 3))
    got = _comm_call(body, name="all_gather_weights", n_in=na, out_shape=out_shape, scratch=[sem, sem, sem, sem])(*shards)
    return _place_own_blocks(got, shards)


def _gather_copies(ins, outs, send_sems, recv_sems, fwd_send, fwd_recv):
    x, y, c = lax.axis_index("x"), lax.axis_index("y"), lax.axis_index("c")
    my_k = 2 * x + y
    direct, forwards = [], []
    for a in range(len(ins)):
        rows = ins[a].shape[0]
        for r, (fx, fy) in enumerate(CHIP_FLIPS):
            px, py = x ^ fx, y ^ fy
            if rows % 32 == 0:
                mine = pl.ds(pl.multiple_of(c * (rows // 2), 16), rows // 2)
                other = pl.ds(pl.multiple_of((1 - c) * (rows // 2), 16), rows // 2)
                rc = pltpu.make_async_remote_copy(
                    src_ref=ins[a].at[mine], dst_ref=outs[a].at[my_k, mine], send_sem=send_sems.at[a, r],
                    recv_sem=recv_sems.at[a, r], device_id=(px, py, c), device_id_type=MESH)
                landed = outs[a].at[2 * px + py, mine]
                fw = pltpu.make_async_remote_copy(
                    src_ref=landed, dst_ref=landed, send_sem=fwd_send.at[a, r], recv_sem=fwd_recv.at[a, r],
                    device_id=(x, y, 1 - c), device_id_type=MESH)
                from_sib = outs[a].at[2 * px + py, other]
                fw_in = pltpu.make_async_remote_copy(
                    src_ref=from_sib, dst_ref=from_sib, send_sem=fwd_send.at[a, r], recv_sem=fwd_recv.at[a, r],
                    device_id=(x, y, 1 - c), device_id_type=MESH)
                forwards.append((rc, fw, fw_in))
            else:
                direct.append(pltpu.make_async_remote_copy(
                    src_ref=ins[a], dst_ref=outs[a].at[my_k], send_sem=send_sems.at[a, r],
                    recv_sem=recv_sems.at[a, r], device_id=(px, py, c), device_id_type=MESH))
    return forwards, direct


def _gather_start(copies):
    forwards, direct = copies
    for rc, _, _ in forwards:
        rc.start()
    for rc in direct:
        rc.start()


def _gather_finish(copies):
    forwards, direct = copies
    for rc, fw, _ in forwards:
        rc.wait_recv()
        fw.start()
    for rc, fw, fw_in in forwards:
        rc.wait_send()
        fw.wait_send()
        fw_in.wait_recv()
    for rc in direct:
        rc.wait()


def _place_own_blocks(got, shards):
    my_k = 2 * lax.axis_index("x") + lax.axis_index("y")
    return [lax.dynamic_update_index_in_dim(g, a, my_k, 0) for g, a in zip(got, shards)]


def _norm1_fwd_gather(x, gain, shards):
    s = x.shape[0]
    tm = min(TILES["row"], s)
    ni = s // tm
    na = len(shards)

    def kfn(x_ref, g_ref, *rest):
        o_ref = rest[na]
        sems = rest[2 * na + 1:]
        i = pl.program_id(0)

        @pl.when(i == 0)
        def _():
            _gather_start(_gather_copies(rest[:na], rest[na + 1:2 * na + 1], *sems))

        xv = x_ref[...]
        r = lax.rsqrt(jnp.mean(xv * xv, axis=-1, keepdims=True) + NORM_EPS)
        o_ref[...] = (xv * r * g_ref[...]).astype(BF16)

        @pl.when(i == ni - 1)
        def _():
            _gather_finish(_gather_copies(rest[:na], rest[na + 1:2 * na + 1], *sems))

    sem = pltpu.SemaphoreType.DMA((na, 3))
    out = pl.pallas_call(
        kfn, name="norm1_fwd_gather", grid=(ni,),
        in_specs=[_rows(tm, D_MODEL), _full((1, D_MODEL))] + [ANY] * na,
        out_specs=[_rows(tm, D_MODEL)] + [ANY] * na,
        out_shape=[_sds((s, D_MODEL), BF16)] + [_sds((4,) + s_a.shape, s_a.dtype) for s_a in shards],
        scratch_shapes=[sem, sem, sem, sem],
        compiler_params=pltpu.CompilerParams(dimension_semantics=("arbitrary",), vmem_limit_bytes=VMEM_LIMIT_V7X,
                                             has_side_effects=True))(x, gain, *shards)
    return out[0], _place_own_blocks(list(out[1:]), shards)


def _matmul_nn_gather(a, b, shards, *, name, tm, tn, out_dtype, f32_cols=None):
    m, kdim = a.shape
    n = b.shape[1]
    ni, nj = m // tm, n // tn
    na = len(shards)
    extra = 0 if f32_cols is None else 1

    def body(a_ref, b_ref, *rest):
        o_ref = rest[na]
        sems = rest[2 * na + 1 + extra:]
        i, j = pl.program_id(0), pl.program_id(1)

        @pl.when((i == 0) & (j == 0))
        def _():
            _gather_start(_gather_copies(rest[:na], rest[na + 1:2 * na + 1], *sems))

        r = _dot(a_ref[...], b_ref[...])
        o_ref[...] = r.astype(out_dtype)
        if extra:
            @pl.when(j == 0)
            def _():
                rest[2 * na + 1][...] = r[:, f32_cols[0]:f32_cols[1]]

        @pl.when((i == ni - 1) & (j == nj - 1))
        def _():
            _gather_finish(_gather_copies(rest[:na], rest[na + 1:2 * na + 1], *sems))

    def kfn(*refs):
        body(*refs)
    sem = pltpu.SemaphoreType.DMA((na, 3))
    out = pl.pallas_call(
        kfn, name=name, grid=(ni, nj),
        in_specs=[pl.BlockSpec((tm, kdim), lambda i, j: (i, 0)), pl.BlockSpec((kdim, tn), lambda i, j: (0, j))] + [ANY] * na,
        out_specs=([pl.BlockSpec((tm, tn), lambda i, j: (i, j))] + [ANY] * na
                   + ([pl.BlockSpec((tm, f32_cols[1] - f32_cols[0]), lambda i, j: (i, 0))] if extra else [])),
        out_shape=([_sds((m, n), out_dtype)] + [_sds((4,) + s_a.shape, s_a.dtype) for s_a in shards]
                   + ([_sds((m, f32_cols[1] - f32_cols[0]), F32)] if extra else [])),
        scratch_shapes=[sem, sem, sem, sem],
        compiler_params=pltpu.CompilerParams(dimension_semantics=("arbitrary", "arbitrary"),
                                             vmem_limit_bytes=VMEM_LIMIT_V7X, has_side_effects=True))(a, b, *shards)
    gathered = _place_own_blocks(list(out[1:1 + na]), shards)
    return (out[0], gathered, out[1 + na]) if extra else (out[0], gathered)


def _all_reduce_small(vec):
    r = vec.shape[0]

    def body(v_ref, o_ref, gath, send_sems, recv_sems):
        x, y, c = lax.axis_index("x"), lax.axis_index("y"), lax.axis_index("c")
        me = 4 * x + 2 * y + c
        gath[me] = v_ref[...]
        copies = []
        for rel in range(1, 8):
            fx, fy, fc = (rel >> 2) & 1, (rel >> 1) & 1, rel & 1
            rc = pltpu.make_async_remote_copy(
                src_ref=v_ref, dst_ref=gath.at[me], send_sem=send_sems.at[rel - 1], recv_sem=recv_sems.at[rel - 1],
                device_id=(x ^ fx, y ^ fy, c ^ fc), device_id_type=MESH)
            rc.start()
            copies.append(rc)
        for rc in copies:
            rc.wait()
        acc = gath[0]
        for d in range(1, 8):
            acc = acc + gath[d]
        o_ref[...] = acc

    def kfn(*refs):
        body(*refs)
    vm = pl.BlockSpec(memory_space=pltpu.VMEM)
    return pl.pallas_call(kfn, name="all_reduce_small", in_specs=[vm], out_specs=vm, out_shape=_sds((r, 128), F32),
                          scratch_shapes=[pltpu.VMEM((8, r, 128), F32), pltpu.SemaphoreType.DMA((7,)),
                                          pltpu.SemaphoreType.DMA((7,))],
                          compiler_params=pltpu.CompilerParams(has_side_effects=True))(vec)


def _exchange_halves(arrs):
    na = len(arrs)

    def body(*refs):
        ins, outs = refs[:na], refs[na:2 * na]
        send_sems, recv_sems = refs[2 * na:]
        x, y, c = lax.axis_index("x"), lax.axis_index("y"), lax.axis_index("c")
        copies = []
        for a in range(na):
            half = ins[a].shape[1] // 2
            src = ins[a].at[:, pl.ds(pl.multiple_of((1 - c) * half, 16), half), :]
            rc = pltpu.make_async_remote_copy(src_ref=src, dst_ref=outs[a], send_sem=send_sems.at[a],
                                              recv_sem=recv_sems.at[a], device_id=(x, y, 1 - c), device_id_type=MESH)
            rc.start()
            copies.append(rc)
        for rc in copies:
            rc.wait()

    out_shape = [_sds((4, a.shape[1] // 2, a.shape[2]), a.dtype) for a in arrs]
    return _comm_call(body, name="rs_pair_exchange", n_in=na, out_shape=out_shape,
                      scratch=[pltpu.SemaphoreType.DMA((na,)), pltpu.SemaphoreType.DMA((na,))])(*arrs)


def _scatter_to_chips(arrs):
    na = len(arrs)

    def body(*refs):
        ins, outs = refs[:na], refs[na:2 * na]
        send_sems, recv_sems = refs[2 * na:]
        x, y, c = lax.axis_index("x"), lax.axis_index("y"), lax.axis_index("c")
        copies = []
        for a in range(na):
            for r, (fx, fy) in enumerate(CHIP_FLIPS):
                px, py = x ^ fx, y ^ fy
                rc = pltpu.make_async_remote_copy(
                    src_ref=ins[a].at[2 * px + py], dst_ref=outs[a].at[r], send_sem=send_sems.at[a, r],
                    recv_sem=recv_sems.at[a, r], device_id=(px, py, c), device_id_type=MESH)
                rc.start()
                copies.append(rc)
        for rc in copies:
            rc.wait()

    out_shape = [_sds((3,) + a.shape[1:], a.dtype) for a in arrs]
    return _comm_call(body, name="rs_chip_scatter", n_in=na, out_shape=out_shape,
                      scratch=[pltpu.SemaphoreType.DMA((na, 3)), pltpu.SemaphoreType.DMA((na, 3))])(*arrs)


def _sum_into_half(parts, name):
    r2, c = parts[0].shape
    tr = _row_tile(r2, c)
    nb = r2 // tr
    n = len(parts)

    def kfn(c_ref, *refs):
        acc = refs[0][...].astype(F32)
        for p_ref in refs[1:n]:
            acc = acc + p_ref[...].astype(F32)
        refs[n][...] = acc

    spec = pltpu.PrefetchScalarGridSpec(
        num_scalar_prefetch=1, grid=(nb,), in_specs=[pl.BlockSpec((tr, c), lambda i, cr: (i, 0))] * n,
        out_specs=pl.BlockSpec((tr, c), lambda i, cr: (cr[0] * nb + i, 0)))
    core = lax.axis_index("c").astype(jnp.int32).reshape(1)
    return pl.pallas_call(kfn, name=name, grid_spec=spec, out_shape=_sds((2 * r2, c), F32),
                          compiler_params=pltpu.CompilerParams(dimension_semantics=("arbitrary",),
                                                               vmem_limit_bytes=VMEM_LIMIT_V7X))(core, *parts)


def _join_in_place(arrs):
    na = len(arrs)

    def body(*refs):
        outs = refs[na:2 * na]
        send_sems, recv_sems = refs[2 * na:]
        x, y, c = lax.axis_index("x"), lax.axis_index("y"), lax.axis_index("c")
        copies = []
        for a in range(na):
            half = outs[a].shape[0] // 2
            mine = outs[a].at[pl.ds(pl.multiple_of(c * half, 8), half), :]
            rc = pltpu.make_async_remote_copy(src_ref=mine, dst_ref=mine, send_sem=send_sems.at[a],
                                              recv_sem=recv_sems.at[a], device_id=(x, y, 1 - c), device_id_type=MESH)
            rc.start()
            copies.append(rc)
        for rc in copies:
            rc.wait()

    def kfn(*refs):
        body(*refs)
    return pl.pallas_call(kfn, name="rs_pair_join", in_specs=[ANY] * na, out_specs=[ANY] * na,
                          out_shape=[_sds(a.shape, F32) for a in arrs],
                          input_output_aliases={a: a for a in range(na)},
                          scratch_shapes=[pltpu.SemaphoreType.DMA((na,)), pltpu.SemaphoreType.DMA((na,))],
                          compiler_params=pltpu.CompilerParams(has_side_effects=True))(*arrs)


def _pair_sum(g, o, name):
    _, r, c = g.shape
    half = r // 2
    tr = _row_tile(half, c)
    nb = half // tr

    def kfn(c_ref, g_ref, o_ref, s32_ref, s16_ref):
        acc = g_ref[...] + o_ref[...].astype(F32)
        s32_ref[...] = acc
        s16_ref[...] = acc.astype(BF16)

    blk = lambda imap: pl.BlockSpec((None, tr, c), imap)
    same = lambda k, i, cr: (k, i, 0)
    spec = pltpu.PrefetchScalarGridSpec(
        num_scalar_prefetch=1, grid=(4, nb), in_specs=[blk(lambda k, i, cr: (k, cr[0] * nb + i, 0)), blk(same)],
        out_specs=[blk(same), blk(same)])
    core = lax.axis_index("c").astype(jnp.int32).reshape(1)
    return pl.pallas_call(kfn, name=name, grid_spec=spec, out_shape=[_sds((4, half, c), F32), _sds((4, half, c), BF16)],
                          compiler_params=pltpu.CompilerParams(dimension_semantics=("arbitrary", "arbitrary"),
                                                               vmem_limit_bytes=VMEM_LIMIT_V7X))(core, g, o)


def _rs_pair_stage(grads):
    got = _exchange_halves([g.astype(BF16) for g in grads])
    sums = [_pair_sum(g, o, f"rs_pair_sum_{a}") for a, (g, o) in enumerate(zip(grads, got))]
    return [s32 for s32, _ in sums], [s16 for _, s16 in sums]


def _rs_chip_stage(pair, recv):
    k_me = 2 * lax.axis_index("x") + lax.axis_index("y")
    halves = []
    for a, (p, rv) in enumerate(zip(pair, recv)):
        own = lax.dynamic_index_in_dim(p, k_me, 0, keepdims=False)
        halves.append(_sum_into_half([own, rv[0], rv[1], rv[2]], f"rs_chip_sum_{a}"))
    return _join_in_place(halves)


def _reduce_scatter(grads):
    pair, pair_bf16 = _rs_pair_stage(grads)
    return _rs_chip_stage(pair, _scatter_to_chips(pair_bf16))


def _matmul_nt_scatter(a, b, send, *, name, tm, tn, out_dtype):
    m, kdim = a.shape
    n = b.shape[0]
    ni, nj = m // tm, n // tn
    na = len(send)

    def body(a_ref, b_ref, *rest):
        send_refs, o_ref, recv_refs = rest[:na], rest[na], rest[na + 1:2 * na + 1]
        send_sems, recv_sems = rest[2 * na + 1:]
        i, j = pl.program_id(0), pl.program_id(1)

        def copies():
            x, y, c = lax.axis_index("x"), lax.axis_index("y"), lax.axis_index("c")
            out = []
            for s_i in range(na):
                for r, (fx, fy) in enumerate(CHIP_FLIPS):
                    px, py = x ^ fx, y ^ fy
                    out.append(pltpu.make_async_remote_copy(
                        src_ref=send_refs[s_i].at[2 * px + py], dst_ref=recv_refs[s_i].at[r],
                        send_sem=send_sems.at[s_i, r], recv_sem=recv_sems.at[s_i, r], device_id=(px, py, c),
                        device_id_type=MESH))
            return out

        @pl.when((i == 0) & (j == 0))
        def _():
            for cp in copies():
                cp.start()

        o_ref[...] = _dot(a_ref[...], b_ref[...], NT).astype(out_dtype)

        @pl.when((i == ni - 1) & (j == nj - 1))
        def _():
            for cp in copies():
                cp.wait()

    def kfn(*refs):
        body(*refs)
    sem = pltpu.SemaphoreType.DMA((na, 3))
    out = pl.pallas_call(
        kfn, name=name, grid=(ni, nj),
        in_specs=[pl.BlockSpec((tm, kdim), lambda i, j: (i, 0)), pl.BlockSpec((tn, kdim), lambda i, j: (j, 0))] + [ANY] * na,
        out_specs=[pl.BlockSpec((tm, tn), lambda i, j: (i, j))] + [ANY] * na,
        out_shape=[_sds((m, n), out_dtype)] + [_sds((3,) + s_a.shape[1:], s_a.dtype) for s_a in send],
        scratch_shapes=[sem, sem],
        compiler_params=pltpu.CompilerParams(dimension_semantics=("arbitrary", "arbitrary"),
                                             vmem_limit_bytes=VMEM_LIMIT_V7X, has_side_effects=True))(a, b, *send)
    return out[0], list(out[1:])


def _pad_w_in(w):
    z = jnp.zeros((w.shape[0], 1024 - 848), w.dtype)
    return jnp.concatenate([w[:, 0:832], w[:, 4928:4944], z, w[:, 832:4928], w[:, 4944:5968]], axis=1)


def _unpad_w_in(g):
    return jnp.concatenate([g[:, 0:832], g[:, 1024:5120], g[:, 832:848], g[:, 5120:6144]], axis=1)


W_IN_SHARD = W_IN_COLS // 4
W_IN_RUNS = ((0, 832, 0), (832, 4928, 1024), (4928, 4944, 832), (4944, 5968, 5120))


def _w_in_grad_blocks(p):
    def orig_cols(lo, hi):
        parts = [p[:, pa + max(lo, a) - a:pa + min(hi, b) - a] for a, b, pa in W_IN_RUNS if max(lo, a) < min(hi, b)]
        return parts[0] if len(parts) == 1 else jnp.concatenate(parts, axis=1)
    return jnp.stack([orig_cols(W_IN_SHARD * k, W_IN_SHARD * (k + 1)) for k in range(4)])


def _pad_w_in_blocks(g):
    def orig_cols(lo, hi):
        return [g[k][:, max(lo, W_IN_SHARD * k) - W_IN_SHARD * k:min(hi, W_IN_SHARD * (k + 1)) - W_IN_SHARD * k]
                for k in range(4) if max(lo, W_IN_SHARD * k) < min(hi, W_IN_SHARD * (k + 1))]
    z = jnp.zeros((g.shape[1], 1024 - 848), g.dtype)
    return jnp.concatenate(orig_cols(0, 832) + orig_cols(4928, 4944) + [z] + orig_cols(832, 4928) + orig_cols(4944, 5968),
                           axis=1)


def _pad_heads(w):
    r = w.shape[0]
    return jnp.pad(w.reshape(r, HEADS, QK_DIM), ((0, 0), (0, 0), (0, QK_PAD - QK_DIM))).reshape(r, HEADS * QK_PAD)


def _unpad_heads(w):
    r = w.shape[0]
    return w.reshape(r, HEADS, QK_PAD)[:, :, :QK_DIM].reshape(r, HEADS * QK_DIM)


def _cols_to_blocks(w):
    r = w.shape[0]
    return w.reshape(r, 4, -1).transpose(1, 0, 2)


def _blocks_to_cols(w):
    return w.transpose(1, 0, 2).reshape(w.shape[1], -1)


SMALL_ROWS = {"norm_gain": (0, 2048), "mla_q_a_gain": (16, 512), "mla_kv_a_gain": (20, 256),
              "mla_q_norm_gain": (22, 192), "mla_k_norm_gain": (24, 192), "gdn_a_log": (26, 8),
              "gdn_dt_bias": (27, 8), "gdn_out_norm_gain": (28, 128)}
LOSS_ROW = 29
SMALL_PACK_ROWS = 32
CONV_ROW = 32


def _pack_small(vals, loss=None):
    rows = []
    at = 0
    for name, (row, size) in SMALL_ROWS.items():
        assert row == at
        nr = -(-size // 128)
        rows.append(jnp.pad(vals[name].reshape(-1).astype(F32), (0, nr * 128 - size)).reshape(nr, 128))
        at += nr
    assert at == LOSS_ROW
    if loss is not None:
        rows.append(jnp.pad(loss.reshape(1, 1), ((0, 0), (0, 127))))
        at += 1
    rows.append(jnp.zeros((SMALL_PACK_ROWS - at, 128), F32))
    return jnp.concatenate(rows, axis=0)


def _unpack_small(pack, name):
    row, size = SMALL_ROWS[name]
    nr = -(-size // 128)
    return pack[row:row + nr].reshape(-1)[:size].reshape(1, size)


def _local_step(x, positions, target, norm_gain, w_in_p, q_a_gain, kv_a_gain, w_uq_p, w_ukv, q_norm_gain,
                k_norm_gain, conv_w, a_log, dt_bias, out_gain, w_out, scatter_hook=None, late_weights=None,
                first_weights=None):
    half = HALF_ROPE
    inv_freq = jnp.power(10000.0, -jnp.arange(half, dtype=F32) / half)
    ang = positions.astype(F32)[:, None] * inv_freq
    cos, sin = jnp.cos(ang), jnp.sin(ang)
    zpad = jnp.zeros((x.shape[0], 64), F32)
    cs = jnp.concatenate([cos, cos, zpad], axis=1)
    sn = jnp.concatenate([-sin, sin, zpad], axis=1)
    gq = jnp.pad(q_norm_gain.reshape(1, QK_DIM), ((0, 0), (0, QK_PAD - QK_DIM)))
    gk = jnp.pad(k_norm_gain.reshape(1, QK_DIM), ((0, 0), (0, QK_PAD - QK_DIM)))
    lane_pad = ((0, 0), (GA_LANE, 128 - GA_LANE - HEADS))
    alog128 = jnp.pad(a_log.reshape(1, HEADS), lane_pad)
    dtb128 = jnp.pad(dt_bias.reshape(1, HEADS), lane_pad)
    ng, qag, kvag, og = (norm_gain.reshape(1, -1), q_a_gain.reshape(1, -1), kv_a_gain.reshape(1, -1),
                         out_gain.reshape(1, -1))

    if first_weights is None:
        xn = _norm1_fwd(x, ng)
    else:
        shards, assemble = first_weights
        xn, gathered = _norm1_fwd_gather(x, ng, shards)
        w_in_p, conv_w = assemble(gathered)
    if late_weights is None:
        misc = _matmul(xn, w_in_p[:, 768:896], mode="nn", out_dtype=F32, name="in_proj_misc")
        proj = _matmul(xn, w_in_p, mode="nn", out_dtype=BF16, name="in_proj")
    else:
        shards, assemble = late_weights
        proj, gathered, misc = _matmul_nn_gather(xn, w_in_p, shards, name="in_proj_gather", tm=TILES["mm"],
                                                 tn=2 * TILES["mm"], out_dtype=BF16, f32_cols=(768, 896))
        w_uq_p, w_ukv, w_out = assemble(gathered)
    cqn, ckvn = _mla_a_norm(proj, qag, kvag)
    q_pre = _matmul(cqn, w_uq_p, mode="nn", out_dtype=BF16, name="q_up")
    kv_pre = _matmul(ckvn, w_ukv, mode="nn", out_dtype=BF16, name="kv_up")
    q, k, v = _mla_post_fwd(q_pre, kv_pre, misc, cs, sn, gq, gk)
    o_mla, lse = _attn_fwd(q, k, v)
    qkv = _gdn_conv_fwd(proj, conv_w)
    gbeta = _gdn_gates_fwd(misc, alog128, dtb128)
    g_u, g_w, g_qd, g_kd, g_a, g_t, g_gl = _gdn_pre(qkv, gbeta)
    o_gdn, states = _gdn_scan_fwd(g_u, g_w, g_qd, g_kd, g_a, g_gl)
    mixed = _mix_fwd(o_mla, o_gdn, proj, og)
    dy, sq = _out_fwd(mixed, w_out, x, target)

    dmixed = _matmul(dy, w_out, mode="nt", out_dtype=BF16, name="d_mixed")
    d_w_out = _matmul(mixed, dy, mode="tn", out_dtype=F32, name="d_w_out", tk=4096)
    do_mla, do_gdn, dmg, dgg, d_out_gain, delta128 = _mix_bwd(o_mla, o_gdn, proj, og, dmixed)
    s_len = x.shape[0]
    delta_rows = delta128[:, :HEADS].T.reshape(HEADS, 1, s_len)
    dq, dk, dv = _attn_bwd(q, k, v, lse.reshape(HEADS, 1, s_len), delta_rows, do_mla)
    dq_pre, dkv_pre, dkr, d_gq, d_gk = _mla_post_bwd(q_pre, kv_pre, misc, cs, sn, gq, gk, dq, dk, dv)
    d_w_uq_p = _matmul(cqn, dq_pre, mode="tn", out_dtype=F32, name="d_w_uq", tk=1024)
    d_w_ukv = _matmul(ckvn, dkv_pre, mode="tn", out_dtype=F32, name="d_w_ukv", tk=1024)
    dcqn = _matmul(dq_pre, w_uq_p, mode="nt", out_dtype=F32, name="d_cqn")
    dckvn = _matmul(dkv_pre, w_ukv, mode="nt", out_dtype=F32, name="d_ckvn")
    dcq, dckv, d_qag, d_kvag = _mla_a_norm_bwd(proj, qag, kvag, dcqn, dckvn)
    dstates = _gdn_scan_bwd(g_w, g_qd, g_kd, g_a, g_gl, do_gdn)
    dqkv, dgbeta = _gdn_post_bwd(qkv, gbeta, g_u, g_w, g_t, states, dstates, do_gdn)
    dz = _gdn_conv_bwd_a(proj, conv_w, dqkv)
    dgx, d_conv = _gdn_conv_bwd_b(proj, conv_w, dz)
    dmisc, d_alog, d_dtb = _gdn_gates_bwd(misc, alog128, dtb128, gbeta, dgbeta, dkr)
    dproj = jnp.concatenate([dcq, dckv, dmisc, jnp.zeros((x.shape[0], 128), BF16), dmg, dgx, dgg], axis=1)
    d_w_in_p = _matmul(xn, dproj, mode="tn", out_dtype=F32, name="d_w_in", tk=4096)
    big = {"w_in": d_w_in_p, "w_uq": d_w_uq_p, "w_ukv": d_w_ukv, "w_out": d_w_out, "gdn_conv_w": d_conv}
    if scatter_hook is None:
        dxn, received = _matmul(dproj, w_in_p, mode="nt", out_dtype=BF16, name="d_xn", tm=512, tn=512), None
    else:
        dxn, received = _matmul_nt_scatter(dproj, w_in_p, scatter_hook(big), name="d_xn_scatter", tm=1024, tn=512,
                                           out_dtype=BF16)
    grad_x, d_ng = _norm1_bwd(x, ng, dxn, dy)

    small = {"norm_gain": d_ng.sum(0), "mla_q_a_gain": d_qag.sum(0), "mla_kv_a_gain": d_kvag.sum(0),
             "mla_q_norm_gain": d_gq.sum(0)[:QK_DIM], "mla_k_norm_gain": d_gk.sum(0)[:QK_DIM],
             "gdn_a_log": d_alog.sum(0)[GA_LANE:GA_LANE + HEADS], "gdn_dt_bias": d_dtb.sum(0)[GA_LANE:GA_LANE + HEADS],
             "gdn_out_norm_gain": d_out_gain.sum(0)}
    return sq, grad_x, small, big, received


WEIGHTS = ["norm_gain", "w_in", "mla_q_a_gain", "mla_kv_a_gain", "w_uq", "w_ukv", "mla_q_norm_gain", "mla_k_norm_gain",
           "gdn_conv_w", "gdn_a_log", "gdn_dt_bias", "gdn_out_norm_gain", "w_out"]
BIG = ["w_in", "w_uq", "w_ukv", "w_out"]


def kernel(x, positions, norm_gain, w_in, mla_q_a_gain, mla_kv_a_gain, w_uq, w_ukv, mla_q_norm_gain, mla_k_norm_gain, gdn_conv_w, gdn_a_log, gdn_dt_bias, gdn_out_norm_gain, w_out, loss_target, m_norm_gain, m_w_in, m_mla_q_a_gain, m_mla_kv_a_gain, m_w_uq, m_w_ukv, m_mla_q_norm_gain, m_mla_k_norm_gain, m_gdn_conv_w, m_gdn_a_log, m_gdn_dt_bias, m_gdn_out_norm_gain, m_w_out, v_norm_gain, v_w_in, v_mla_q_a_gain, v_mla_kv_a_gain, v_w_uq, v_w_ukv, v_mla_q_norm_gain, v_mla_k_norm_gain, v_gdn_conv_w, v_gdn_a_log, v_gdn_dt_bias, v_gdn_out_norm_gain, v_w_out):
    w = dict(norm_gain=norm_gain, w_in=w_in, mla_q_a_gain=mla_q_a_gain, mla_kv_a_gain=mla_kv_a_gain, w_uq=w_uq,
             w_ukv=w_ukv, mla_q_norm_gain=mla_q_norm_gain, mla_k_norm_gain=mla_k_norm_gain, gdn_conv_w=gdn_conv_w,
             gdn_a_log=gdn_a_log, gdn_dt_bias=gdn_dt_bias, gdn_out_norm_gain=gdn_out_norm_gain, w_out=w_out)
    m = dict(norm_gain=m_norm_gain, w_in=m_w_in, mla_q_a_gain=m_mla_q_a_gain, mla_kv_a_gain=m_mla_kv_a_gain,
             w_uq=m_w_uq, w_ukv=m_w_ukv, mla_q_norm_gain=m_mla_q_norm_gain, mla_k_norm_gain=m_mla_k_norm_gain,
             gdn_conv_w=m_gdn_conv_w, gdn_a_log=m_gdn_a_log, gdn_dt_bias=m_gdn_dt_bias,
             gdn_out_norm_gain=m_gdn_out_norm_gain, w_out=m_w_out)
    v = dict(norm_gain=v_norm_gain, w_in=v_w_in, mla_q_a_gain=v_mla_q_a_gain, mla_kv_a_gain=v_mla_kv_a_gain,
             w_uq=v_w_uq, w_ukv=v_w_ukv, mla_q_norm_gain=v_mla_q_norm_gain, mla_k_norm_gain=v_mla_k_norm_gain,
             gdn_conv_w=v_gdn_conv_w, gdn_a_log=v_gdn_a_log, gdn_dt_bias=v_gdn_dt_bias,
             gdn_out_norm_gain=v_gdn_out_norm_gain, w_out=v_w_out)
    k_me = 2 * lax.axis_index("x") + lax.axis_index("y")

    first_weights = ([w_in[0].astype(BF16), gdn_conv_w[0]], lambda g: (_pad_w_in_blocks(g[0]), _blocks_to_cols(g[1])))
    late_weights = ([w_uq[0].astype(BF16), w_ukv[0].astype(BF16), w_out[0].astype(BF16)],
                    lambda g: (_pad_heads(_blocks_to_cols(g[0])), _blocks_to_cols(g[1]), g[2].reshape(D_MODEL, D_MODEL)))

    pair_sums = []

    def scatter_hook(big):
        pair, pair_bf16 = _rs_pair_stage([
            _w_in_grad_blocks(big["w_in"]), _cols_to_blocks(_unpad_heads(big["w_uq"])),
            _cols_to_blocks(big["w_ukv"]), big["w_out"].reshape(4, 512, D_MODEL)])
        pair_sums.extend(pair)
        return pair_bf16

    sq, grad_x, small, big, received = _local_step(
        x[0], positions[0], loss_target[0], norm_gain, None, mla_q_a_gain, mla_kv_a_gain, None, None,
        mla_q_norm_gain, mla_k_norm_gain, None, gdn_a_log, gdn_dt_bias, gdn_out_norm_gain, None, scatter_hook,
        late_weights, first_weights)

    loss_local = (0.5 / D_MODEL) * jnp.sum(sq)
    pack = jnp.concatenate([_pack_small(small, loss_local), big["gdn_conv_w"].reshape(96, 128)], axis=0)
    tot = _all_reduce_small(pack)
    loss = tot[LOSS_ROW, 0]
    conv_grad = lax.dynamic_slice_in_dim(tot[CONV_ROW:].reshape(4, 3072), k_me * 768, 768, axis=1)

    shard_grads = _rs_chip_stage(pair_sums, received)

    grads = {n: _unpack_small(tot, n) for n in SMALL_ROWS}
    grads["gdn_conv_w"] = conv_grad[None]
    for n, g in zip(BIG, shard_grads):
        grads[n] = g[None]

    delta, new_m, new_v = {}, {}, {}
    sw = _pack_small({n: w[n] for n in SMALL_ROWS})
    sm = _pack_small({n: m[n] for n in SMALL_ROWS})
    sv = _pack_small({n: v[n] for n in SMALL_ROWS})
    sd, snm, snv = _adamw(sw, tot[:SMALL_PACK_ROWS], sm, sv, "adamw_small")
    for n in SMALL_ROWS:
        delta[n], new_m[n], new_v[n] = _unpack_small(sd, n), _unpack_small(snm, n), _unpack_small(snv, n)
    for n in BIG + ["gdn_conv_w"]:
        if n == "w_in":
            d, nm, nv = _adamw(w[n][0].T, grads[n][0].T, m[n][0].T, v[n][0].T, f"adamw_{n}")
            delta[n], new_m[n], new_v[n] = d.T[None], nm.T[None], nv.T[None]
        else:
            d, nm, nv = _adamw(w[n][0], grads[n][0], m[n][0], v[n][0], f"adamw_{n}")
            delta[n], new_m[n], new_v[n] = d[None], nm[None], nv[None]

    return (loss, grad_x[None], *[grads[n] for n in WEIGHTS], *[delta[n] for n in WEIGHTS],
            *[new_m[n] for n in WEIGHTS], *[new_v[n] for n in WEIGHTS])
```
